```python
import jax, jax.numpy as jnp
from jax import lax
import numpy as np

D_MODEL = 1024
BATCH = 8
SEQ = 4096
DEPTH = 1

PLE_DIM = 256
ATTN_HEADS = 8
ATTN_KV_HEADS = 2
ATTN_HEAD_DIM = 64
ATTN_GROUPS = ATTN_HEADS // ATTN_KV_HEADS
WINDOW = 128
ATTN_BLOCK = 128
ROPE_THETA = 10000.0
DN_HEADS = 4
DN_HEAD_DIM = 128
DN_CONV = 4
DN_CHUNK = 64
D_FF = 4 * D_MODEL
EPS = 1e-6

ATTN_Q = ATTN_HEADS * ATTN_HEAD_DIM
ATTN_KV = ATTN_KV_HEADS * ATTN_HEAD_DIM
DN_W = DN_HEADS * DN_HEAD_DIM
MIX_WIDTH = ATTN_Q + DN_W
SPLIT_SIZES = (ATTN_Q, ATTN_KV, ATTN_KV, DN_W, DN_W, DN_W, DN_W, DN_HEADS, DN_HEADS)
D_IN = sum(SPLIT_SIZES)
CONV_CH = 3 * DN_W

kernel_name = "hybrid_swa_sink_gated_deltanet_block"


def rmsnorm(x, g):
    xf = x.astype(jnp.float32)
    y = xf * lax.rsqrt(jnp.mean(xf * xf, axis=-1, keepdims=True) + EPS) * g.astype(jnp.float32)
    return y.astype(x.dtype)


def l2norm(x):
    return x * lax.rsqrt(jnp.sum(x * x, axis=-1, keepdims=True) + EPS)


def rope(t, positions):
    dh = t.shape[-1]
    half = dh // 2
    inv = 1.0 / (ROPE_THETA ** (jnp.arange(half, dtype=jnp.float32) * (2.0 / dh)))
    ang = positions.astype(jnp.float32)[:, None] * inv[None, :]
    cos = jnp.cos(ang)[None, :, None, :]
    sin = jnp.sin(ang)[None, :, None, :]
    tf = t.astype(jnp.float32)
    t1, t2 = tf[..., :half], tf[..., half:]
    out = jnp.concatenate([t1 * cos - t2 * sin, t2 * cos + t1 * sin], axis=-1)
    return out.astype(t.dtype)


def sliding_window_sink_attention(q, k, v, sinks):
    B, S = q.shape[0], q.shape[1]
    nb = S // ATTN_BLOCK
    qb = q.reshape(B, nb, ATTN_BLOCK, ATTN_KV_HEADS, ATTN_GROUPS, ATTN_HEAD_DIM)
    kb = k.reshape(B, nb, ATTN_BLOCK, ATTN_KV_HEADS, ATTN_HEAD_DIM)
    vb = v.reshape(B, nb, ATTN_BLOCK, ATTN_KV_HEADS, ATTN_HEAD_DIM)

    def with_prev(t):
        prev = jnp.concatenate([jnp.zeros_like(t[:, :1]), t[:, :-1]], axis=1)
        return jnp.concatenate([prev, t], axis=2)

    kw, vw = with_prev(kb), with_prev(vb)
    scale = ATTN_HEAD_DIM ** -0.5
    s = jnp.einsum('bnqhgd,bnkhd->bnhgqk', qb, kw).astype(jnp.float32) * scale
    blk = jnp.arange(nb)[:, None, None]
    qpos = blk * ATTN_BLOCK + jnp.arange(ATTN_BLOCK)[None, :, None]
    kpos = blk * ATTN_BLOCK - ATTN_BLOCK + jnp.arange(2 * ATTN_BLOCK)[None, None, :]
    valid = (kpos <= qpos) & (qpos - kpos < WINDOW) & (kpos >= 0)
    s = jnp.where(valid[:, None, None, :, :], s, -jnp.inf)
    sink = jnp.broadcast_to(
        sinks.astype(jnp.float32).reshape(1, 1, ATTN_KV_HEADS, ATTN_GROUPS, 1, 1),
        s.shape[:-1] + (1,))
    probs = jax.nn.softmax(jnp.concatenate([s, sink], axis=-1), axis=-1)[..., :-1]
    o = jnp.einsum('bnhgqk,bnkhd->bnqhgd', probs.astype(v.dtype), vw)
    return o.reshape(B, S, ATTN_Q)


def causal_conv(x, w):
    c = x.shape[-1]
    return lax.conv_general_dilated(
        x, w[:, None, :].astype(x.dtype), window_strides=(1,),
        padding=((DN_CONV - 1, 0),), dimension_numbers=('NWC', 'WIO', 'NWC'),
        feature_group_count=c)


def chunk_gated_delta_rule(q, k, v, g, beta):
    B, S, H, DK = q.shape
    DV = v.shape[-1]
    C = DN_CHUNK
    nc = S // C

    def to_chunks(t):
        t = jnp.moveaxis(t, 2, 1)
        return t.reshape(t.shape[:2] + (nc, C) + t.shape[3:])

    q, k, v, g, beta = (to_chunks(t) for t in (q, k, v, g, beta))
    g = jnp.cumsum(g, axis=-1)
    tril = jnp.tril(jnp.ones((C, C), dtype=bool))
    strict = jnp.tril(jnp.ones((C, C), dtype=bool), -1)
    decay = jnp.exp(jnp.where(tril, g[..., :, None] - g[..., None, :], -jnp.inf))
    k_beta = k * beta[..., None]
    v_beta = v * beta[..., None]
    L = jnp.where(strict, jnp.einsum('bhncd,bhnsd->bhncs', k_beta, k) * decay, 0.0)
    eye = jnp.eye(C, dtype=q.dtype)
    T = lax.linalg.triangular_solve(eye + L, jnp.broadcast_to(eye, L.shape),
                                    left_side=True, lower=True)
    u = jnp.einsum('bhncs,bhnsd->bhncd', T, v_beta)
    w = jnp.einsum('bhncs,bhnsd->bhncd', T, k_beta * jnp.exp(g)[..., None])
    a_qk = jnp.einsum('bhncd,bhnsd->bhncs', q, k) * decay
    q_g = q * jnp.exp(g)[..., None]
    g_last = g[..., -1]
    k_d = k * jnp.exp(g_last[..., None] - g)[..., None]
    d_last = jnp.exp(g_last)

    xs = tuple(jnp.moveaxis(t, 2, 0) for t in (q_g, k_d, u, w, a_qk, d_last))

    def step(state, inp):
        q_i, k_i, u_i, w_i, a_i, d_i = inp
        v_new = u_i - jnp.einsum('bhck,bhkv->bhcv', w_i, state)
        o = jnp.einsum('bhck,bhkv->bhcv', q_i, state) + jnp.einsum('bhcs,bhsv->bhcv', a_i, v_new)
        state = state * d_i[..., None, None] + jnp.einsum('bhck,bhcv->bhkv', k_i, v_new)
        return state, o

    s0 = jnp.zeros((B, H, DK, DV), dtype=q.dtype)
    _, o = lax.scan(step, s0, xs)
    o = jnp.moveaxis(o, 0, 2).reshape(B, H, S, DV)
    return jnp.moveaxis(o, 1, 2)


def gated_deltanet(q, k, v, z, b, a, conv_w, a_log, dt_bias, norm_w):
    B, S = q.shape[0], q.shape[1]
    qkv = jax.nn.silu(causal_conv(jnp.concatenate([q, k, v], axis=-1), conv_w))
    qc, kc, vc = jnp.split(qkv.astype(jnp.float32), 3, axis=-1)
    shp = (B, S, DN_HEADS, DN_HEAD_DIM)
    qc = l2norm(qc.reshape(shp)) * (DN_HEAD_DIM ** -0.5)
    kc = l2norm(kc.reshape(shp))
    vc = vc.reshape(shp)
    beta = jax.nn.sigmoid(b.astype(jnp.float32))
    g = -jnp.exp(a_log.astype(jnp.float32)) * jax.nn.softplus(
        a.astype(jnp.float32) + dt_bias.astype(jnp.float32))
    o = chunk_gated_delta_rule(qc, kc, vc, g, beta)
    o = o * lax.rsqrt(jnp.mean(o * o, axis=-1, keepdims=True) + EPS) * norm_w.astype(jnp.float32)
    o = o * jax.nn.silu(z.astype(jnp.float32).reshape(shp))
    return o.reshape(B, S, DN_W).astype(q.dtype)


def _fwd_setup_inputs(seed: int = 0) -> dict:
    key = jax.random.key(seed)
    ks = jax.random.split(key, 20)
    f32 = jnp.float32
    nrm = lambda k, shape, s: jax.random.normal(k, shape, f32) * s
    gain = lambda k, shape: 1.0 + 0.02 * jax.random.normal(k, shape, f32)
    dt = jnp.exp(jax.random.uniform(ks[6], (DEPTH, DN_HEADS), f32, np.log(1e-3), np.log(1e-1)))
    return {
        "x": nrm(ks[0], (BATCH, SEQ, D_MODEL), 1.0),
        "p": nrm(ks[1], (DEPTH, BATCH, SEQ, PLE_DIM), 1.0),
        "norm_mix": gain(ks[2], (DEPTH, D_MODEL)),
        "w_in": nrm(ks[3], (DEPTH, D_MODEL, D_IN), D_MODEL ** -0.5),
        "conv_w": nrm(ks[4], (DEPTH, DN_CONV, CONV_CH), DN_CONV ** -0.5),
        "a_log": jnp.log(jax.random.uniform(ks[5], (DEPTH, DN_HEADS), f32, 1.0, 16.0)),
        "dt_bias": dt + jnp.log(-jnp.expm1(-dt)),
        "dn_norm": gain(ks[7], (DEPTH, DN_HEAD_DIM)),
        "sinks": nrm(ks[8], (DEPTH, ATTN_HEADS), 0.5),
        "w_o": nrm(ks[9], (DEPTH, MIX_WIDTH, D_MODEL), MIX_WIDTH ** -0.5),
        "norm_mlp": gain(ks[10], (DEPTH, D_MODEL)),
        "w_up": nrm(ks[11], (DEPTH, D_MODEL, D_FF), D_MODEL ** -0.5),
        "w_down": nrm(ks[12], (DEPTH, D_FF, D_MODEL), D_FF ** -0.5),
        "norm_ple": gain(ks[13], (DEPTH, D_MODEL)),
        "w_ple_gate": nrm(ks[14], (DEPTH, D_MODEL, D_MODEL), D_MODEL ** -0.5),
        "w_ple_proj": nrm(ks[15], (DEPTH, PLE_DIM, D_MODEL), PLE_DIM ** -0.5),
        "norm_final": gain(ks[16], (D_MODEL,)),
    }


def _fwd_reference(x, p, norm_mix, w_in, conv_w, a_log, dt_bias, dn_norm, sinks, w_o,
              norm_mlp, w_up, w_down, norm_ple, w_ple_gate, w_ple_proj, norm_final):
    B, S, _ = x.shape
    positions = jnp.arange(S)
    split_idx = np.cumsum(SPLIT_SIZES)[:-1].tolist()
    h = x
    for i in range(DEPTH):
        u = rmsnorm(h, norm_mix[i])
        proj = u @ w_in[i]
        aq, ak, av, dq, dk, dv, dz, db, da = jnp.split(proj, split_idx, axis=-1)
        aq = rope(aq.reshape(B, S, ATTN_HEADS, ATTN_HEAD_DIM), positions)
        ak = rope(ak.reshape(B, S, ATTN_KV_HEADS, ATTN_HEAD_DIM), positions)
        av = av.reshape(B, S, ATTN_KV_HEADS, ATTN_HEAD_DIM)
        attn_out = sliding_window_sink_attention(aq, ak, av, sinks[i])
        dn_out = gated_deltanet(dq, dk, dv, dz, db, da, conv_w[i], a_log[i],
                                dt_bias[i], dn_norm[i])
        h = h + jnp.concatenate([attn_out, dn_out], axis=-1) @ w_o[i]
        m = rmsnorm(h, norm_mlp[i])
        h = h + jnp.square(jax.nn.relu(m @ w_up[i])) @ w_down[i]
        gate = jax.nn.sigmoid(rmsnorm(h, norm_ple[i]) @ w_ple_gate[i])
        h = h + gate * (p[i] @ w_ple_proj[i])
    return rmsnorm(h, norm_final)


import jax as _jax
import jax.numpy as _jnp

TWIN_FORMAT = 'train_step'
FWD_PARAMS = ['x', 'p', 'norm_mix', 'w_in', 'conv_w', 'a_log', 'dt_bias', 'dn_norm', 'sinks', 'w_o', 'norm_mlp', 'w_up', 'w_down', 'norm_ple', 'w_ple_gate', 'w_ple_proj', 'norm_final']
TWIN_WEIGHTS = ['norm_mix', 'w_in', 'conv_w', 'a_log', 'dt_bias', 'dn_norm', 'sinks', 'w_o', 'norm_mlp', 'w_up', 'w_down', 'norm_ple', 'w_ple_gate', 'w_ple_proj', 'norm_final']
TWIN_DIFF_INPUT = 'x'
TWIN_INPUTS = ['x', 'p', 'norm_mix', 'w_in', 'conv_w', 'a_log', 'dt_bias', 'dn_norm', 'sinks', 'w_o', 'norm_mlp', 'w_up', 'w_down', 'norm_ple', 'w_ple_gate', 'w_ple_proj', 'norm_final', 'loss_target', 'm_norm_mix', 'm_w_in', 'm_conv_w', 'm_a_log', 'm_dt_bias', 'm_dn_norm', 'm_sinks', 'm_w_o', 'm_norm_mlp', 'm_w_up', 'm_w_down', 'm_norm_ple', 'm_w_ple_gate', 'm_w_ple_proj', 'm_norm_final', 'v_norm_mix', 'v_w_in', 'v_conv_w', 'v_a_log', 'v_dt_bias', 'v_dn_norm', 'v_sinks', 'v_w_o', 'v_norm_mlp', 'v_w_up', 'v_w_down', 'v_norm_ple', 'v_w_ple_gate', 'v_w_ple_proj', 'v_norm_final']
TWIN_OUTPUTS = ['loss', 'grad_x', 'grad_norm_mix', 'grad_w_in', 'grad_conv_w', 'grad_a_log', 'grad_dt_bias', 'grad_dn_norm', 'grad_sinks', 'grad_w_o', 'grad_norm_mlp', 'grad_w_up', 'grad_w_down', 'grad_norm_ple', 'grad_w_ple_gate', 'grad_w_ple_proj', 'grad_norm_final', 'delta_norm_mix', 'delta_w_in', 'delta_conv_w', 'delta_a_log', 'delta_dt_bias', 'delta_dn_norm', 'delta_sinks', 'delta_w_o', 'delta_norm_mlp', 'delta_w_up', 'delta_w_down', 'delta_norm_ple', 'delta_w_ple_gate', 'delta_w_ple_proj', 'delta_norm_final', 'new_m_norm_mix', 'new_m_w_in', 'new_m_conv_w', 'new_m_a_log', 'new_m_dt_bias', 'new_m_dn_norm', 'new_m_sinks', 'new_m_w_o', 'new_m_norm_mlp', 'new_m_w_up', 'new_m_w_down', 'new_m_norm_ple', 'new_m_w_ple_gate', 'new_m_w_ple_proj', 'new_m_norm_final', 'new_v_norm_mix', 'new_v_w_in', 'new_v_conv_w', 'new_v_a_log', 'new_v_dt_bias', 'new_v_dn_norm', 'new_v_sinks', 'new_v_w_o', 'new_v_norm_mlp', 'new_v_w_up', 'new_v_w_down', 'new_v_norm_ple', 'new_v_w_ple_gate', 'new_v_w_ple_proj', 'new_v_norm_final']
TWIN_LEAF_KINDS = {'loss': 'loss', 'grad_x': 'grad_x', 'grad_norm_mix': 'grad_w', 'grad_w_in': 'grad_w', 'grad_conv_w': 'grad_w', 'grad_a_log': 'grad_w', 'grad_dt_bias': 'grad_w', 'grad_dn_norm': 'grad_w', 'grad_sinks': 'grad_w', 'grad_w_o': 'grad_w', 'grad_norm_mlp': 'grad_w', 'grad_w_up': 'grad_w', 'grad_w_down': 'grad_w', 'grad_norm_ple': 'grad_w', 'grad_w_ple_gate': 'grad_w', 'grad_w_ple_proj': 'grad_w', 'grad_norm_final': 'grad_w', 'delta_norm_mix': 'delta_w', 'delta_w_in': 'delta_w', 'delta_conv_w': 'delta_w', 'delta_a_log': 'delta_w', 'delta_dt_bias': 'delta_w', 'delta_dn_norm': 'delta_w', 'delta_sinks': 'delta_w', 'delta_w_o': 'delta_w', 'delta_norm_mlp': 'delta_w', 'delta_w_up': 'delta_w', 'delta_w_down': 'delta_w', 'delta_norm_ple': 'delta_w', 'delta_w_ple_gate': 'delta_w', 'delta_w_ple_proj': 'delta_w', 'delta_norm_final': 'delta_w', 'new_m_norm_mix': 'new_m', 'new_m_w_in': 'new_m', 'new_m_conv_w': 'new_m', 'new_m_a_log': 'new_m', 'new_m_dt_bias': 'new_m', 'new_m_dn_norm': 'new_m', 'new_m_sinks': 'new_m', 'new_m_w_o': 'new_m', 'new_m_norm_mlp': 'new_m', 'new_m_w_up': 'new_m', 'new_m_w_down': 'new_m', 'new_m_norm_ple': 'new_m', 'new_m_w_ple_gate': 'new_m', 'new_m_w_ple_proj': 'new_m', 'new_m_norm_final': 'new_m', 'new_v_norm_mix': 'new_v', 'new_v_w_in': 'new_v', 'new_v_conv_w': 'new_v', 'new_v_a_log': 'new_v', 'new_v_dt_bias': 'new_v', 'new_v_dn_norm': 'new_v', 'new_v_sinks': 'new_v', 'new_v_w_o': 'new_v', 'new_v_norm_mlp': 'new_v', 'new_v_w_up': 'new_v', 'new_v_w_down': 'new_v', 'new_v_norm_ple': 'new_v', 'new_v_w_ple_gate': 'new_v', 'new_v_w_ple_proj': 'new_v', 'new_v_norm_final': 'new_v'}


def _forward(args):
    return _fwd_reference(*[args[k] for k in FWD_PARAMS])


def _output_shape():
    out = _jax.eval_shape(lambda: _forward(_fwd_setup_inputs(0)))
    return out.shape, out.dtype

N_MICROBATCH = 1
ADAM_LR = 0.001
ADAM_B1 = 0.9
ADAM_B2 = 0.999
ADAM_EPS = 1e-08
ADAM_WD = 0.01
ADAM_STEP = 10
PER_EXAMPLE_BATCH_AXIS = {'x': 0, 'p': 1, 'loss_target': 0}
SHARED_INPUTS = []
_WEIGHT_DTYPES = {'norm_mix': _jnp.float32, 'w_in': _jnp.float32, 'conv_w': _jnp.float32, 'a_log': _jnp.float32, 'dt_bias': _jnp.float32, 'dn_norm': _jnp.float32, 'sinks': _jnp.float32, 'w_o': _jnp.float32, 'norm_mlp': _jnp.float32, 'w_up': _jnp.float32, 'w_down': _jnp.float32, 'norm_ple': _jnp.float32, 'w_ple_gate': _jnp.float32, 'w_ple_proj': _jnp.float32, 'norm_final': _jnp.float32}
MOMENT_SCALE = {'norm_mix': 1.274604e-01, 'w_in': 7.467751e-02, 'conv_w': 7.671476e-02, 'a_log': 5.135488e-01, 'dt_bias': 4.963502e-01, 'dn_norm': 2.173677e-01, 'sinks': 3.876157e-02, 'w_o': 7.259662e-02, 'norm_mlp': 1.609107e-01, 'w_up': 7.333354e-02, 'w_down': 1.399821e-01, 'norm_ple': 2.479264e-02, 'w_ple_gate': 2.286742e-02, 'w_ple_proj': 5.590296e-02, 'norm_final': 3.218461e+01}


def _to_microbatches(a, axis):
    t = _jnp.moveaxis(a, axis, 0)
    t = t.reshape((N_MICROBATCH, t.shape[0] // N_MICROBATCH) + t.shape[1:])
    return _jnp.moveaxis(t, 1, axis + 1)


def setup_inputs(seed: int = 0) -> dict:
    inp = _fwd_setup_inputs(seed)
    key = _jax.random.fold_in(_jax.random.key(seed), 7919)
    shape, _ = _output_shape()
    out = dict(inp)
    out["loss_target"] = _jax.random.normal(_jax.random.fold_in(key, 0), shape, _jnp.float32)
    for i, name in enumerate(TWIN_WEIGHTS):
        w = inp[name].astype(_jnp.float32)
        if MOMENT_SCALE is None:
            s = _jnp.sqrt(_jnp.mean(_jnp.square(w)) + 1e-30)
        else:
            s = MOMENT_SCALE[name]
        km, kv = _jax.random.split(_jax.random.fold_in(key, i + 1))
        out[name] = w
        out["m_" + name] = s * _jax.random.normal(km, w.shape, _jnp.float32)
        out["v_" + name] = (s * s) * _jax.random.uniform(kv, w.shape, _jnp.float32, 0.5, 1.5)
    if N_MICROBATCH > 1:
        for name, axis in PER_EXAMPLE_BATCH_AXIS.items():
            out[name] = _to_microbatches(out[name], axis)
    return {'x': out['x'], 'p': out['p'], 'norm_mix': out['norm_mix'], 'w_in': out['w_in'], 'conv_w': out['conv_w'], 'a_log': out['a_log'], 'dt_bias': out['dt_bias'], 'dn_norm': out['dn_norm'], 'sinks': out['sinks'], 'w_o': out['w_o'], 'norm_mlp': out['norm_mlp'], 'w_up': out['w_up'], 'w_down': out['w_down'], 'norm_ple': out['norm_ple'], 'w_ple_gate': out['w_ple_gate'], 'w_ple_proj': out['w_ple_proj'], 'norm_final': out['norm_final'], 'loss_target': out['loss_target'], 'm_norm_mix': out['m_norm_mix'], 'm_w_in': out['m_w_in'], 'm_conv_w': out['m_conv_w'], 'm_a_log': out['m_a_log'], 'm_dt_bias': out['m_dt_bias'], 'm_dn_norm': out['m_dn_norm'], 'm_sinks': out['m_sinks'], 'm_w_o': out['m_w_o'], 'm_norm_mlp': out['m_norm_mlp'], 'm_w_up': out['m_w_up'], 'm_w_down': out['m_w_down'], 'm_norm_ple': out['m_norm_ple'], 'm_w_ple_gate': out['m_w_ple_gate'], 'm_w_ple_proj': out['m_w_ple_proj'], 'm_norm_final': out['m_norm_final'], 'v_norm_mix': out['v_norm_mix'], 'v_w_in': out['v_w_in'], 'v_conv_w': out['v_conv_w'], 'v_a_log': out['v_a_log'], 'v_dt_bias': out['v_dt_bias'], 'v_dn_norm': out['v_dn_norm'], 'v_sinks': out['v_sinks'], 'v_w_o': out['v_w_o'], 'v_norm_mlp': out['v_norm_mlp'], 'v_w_up': out['v_w_up'], 'v_w_down': out['v_w_down'], 'v_norm_ple': out['v_norm_ple'], 'v_w_ple_gate': out['v_w_ple_gate'], 'v_w_ple_proj': out['v_w_ple_proj'], 'v_norm_final': out['v_norm_final']}


def _loss(weights, diff, rest, loss_target):
    with _jax.named_scope("forward"):
        args = {**rest, TWIN_DIFF_INPUT: diff, **{k: w.astype(_WEIGHT_DTYPES[k]) for k, w in weights.items()}}
        y = _forward(args)
    with _jax.named_scope("loss_head"):
        err = _jnp.square(y.astype(_jnp.float32) - loss_target)
        return 0.5 * _jnp.sum(_jnp.mean(err, axis=-1)) if err.ndim else 0.5 * err


def _adamw(w, g, m, v):
    m = ADAM_B1 * m + (1.0 - ADAM_B1) * g
    v = ADAM_B2 * v + (1.0 - ADAM_B2) * _jnp.square(g)
    m_hat = m / (1.0 - ADAM_B1 ** ADAM_STEP)
    v_hat = v / (1.0 - ADAM_B2 ** ADAM_STEP)
    delta = -ADAM_LR * (m_hat / (_jnp.sqrt(v_hat) + ADAM_EPS) + ADAM_WD * w)
    return delta, m, v


def reference(x, p, norm_mix, w_in, conv_w, a_log, dt_bias, dn_norm, sinks, w_o, norm_mlp, w_up, w_down, norm_ple, w_ple_gate, w_ple_proj, norm_final, loss_target, m_norm_mix, m_w_in, m_conv_w, m_a_log, m_dt_bias, m_dn_norm, m_sinks, m_w_o, m_norm_mlp, m_w_up, m_w_down, m_norm_ple, m_w_ple_gate, m_w_ple_proj, m_norm_final, v_norm_mix, v_w_in, v_conv_w, v_a_log, v_dt_bias, v_dn_norm, v_sinks, v_w_o, v_norm_mlp, v_w_up, v_w_down, v_norm_ple, v_w_ple_gate, v_w_ple_proj, v_norm_final):
    given = dict(x=x, p=p, norm_mix=norm_mix, w_in=w_in, conv_w=conv_w, a_log=a_log, dt_bias=dt_bias, dn_norm=dn_norm, sinks=sinks, w_o=w_o, norm_mlp=norm_mlp, w_up=w_up, w_down=w_down, norm_ple=norm_ple, w_ple_gate=w_ple_gate, w_ple_proj=w_ple_proj, norm_final=norm_final, loss_target=loss_target, m_norm_mix=m_norm_mix, m_w_in=m_w_in, m_conv_w=m_conv_w, m_a_log=m_a_log, m_dt_bias=m_dt_bias, m_dn_norm=m_dn_norm, m_sinks=m_sinks, m_w_o=m_w_o, m_norm_mlp=m_norm_mlp, m_w_up=m_w_up, m_w_down=m_w_down, m_norm_ple=m_norm_ple, m_w_ple_gate=m_w_ple_gate, m_w_ple_proj=m_w_ple_proj, m_norm_final=m_norm_final, v_norm_mix=v_norm_mix, v_w_in=v_w_in, v_conv_w=v_conv_w, v_a_log=v_a_log, v_dt_bias=v_dt_bias, v_dn_norm=v_dn_norm, v_sinks=v_sinks, v_w_o=v_w_o, v_norm_mlp=v_norm_mlp, v_w_up=v_w_up, v_w_down=v_w_down, v_norm_ple=v_norm_ple, v_w_ple_gate=v_w_ple_gate, v_w_ple_proj=v_w_ple_proj, v_norm_final=v_norm_final)
    weights = {n: given[n] for n in TWIN_WEIGHTS}
    shared = {n: given[n] for n in SHARED_INPUTS}
    per_example = {n: given[n] for n in ['x', 'p']}
    grad_fn = _jax.value_and_grad(_loss, argnums=(0, 1))

    def one_microbatch(ex, loss_target):
        ex = dict(ex)
        diff = ex.pop(TWIN_DIFF_INPUT)
        return grad_fn(weights, diff, {**shared, **ex}, loss_target)

    if N_MICROBATCH == 1:
        loss, (grad_w, grad_x) = one_microbatch(per_example, given["loss_target"])
    else:
        def body(carry, xs):
            loss_sum, grad_sum = carry
            l_k, (gw_k, gx_k) = one_microbatch(xs[0], xs[1])
            with _jax.named_scope("update"):
                return (loss_sum + l_k, _jax.tree.map(_jnp.add, grad_sum, gw_k)), gx_k

        init = (_jnp.zeros((), _jnp.float32), _jax.tree.map(_jnp.zeros_like, weights))
        (loss, grad_w), grad_x = _jax.lax.scan(body, init, (per_example, given["loss_target"]))
    with _jax.named_scope("update"):
        delta_w, new_m, new_v = {}, {}, {}
        for n in TWIN_WEIGHTS:
            delta_w[n], new_m[n], new_v[n] = _adamw(weights[n], grad_w[n], given["m_" + n], given["v_" + n])
    return (loss, grad_x, *[grad_w[n] for n in TWIN_WEIGHTS], *[delta_w[n] for n in TWIN_WEIGHTS],
            *[new_m[n] for n in TWIN_WEIGHTS], *[new_v[n] for n in TWIN_WEIGHTS])
```

```python
import jax
import jax.numpy as jnp
from jax import lax
from jax.experimental import pallas as pl
from jax.experimental.pallas import tpu as pltpu

F32, BF16 = jnp.float32, jnp.bfloat16
EPS = 1e-6
D_MODEL = 1024
N_DEV = 8
ATTN_BLOCK = 128
HEAD_PAIR = 128
DN_HEADS = 4
DN_DIM = 128
DN_CHUNK = 64
DN_CONV = 4
ROPE_THETA = 10000.0
D_IN = 2824
D_IN_PAD = 3072
W_IN_SHARD = 353
W_IN_SHARD_PAD = 384
VMEM_LIMIT = 48 * 1024 * 1024
NEG = -1e30
ADAM_LR, ADAM_B1, ADAM_B2, ADAM_EPS, ADAM_WD, ADAM_STEP = 0.001, 0.9, 0.999, 1e-08, 0.01, 10
MESH = pl.DeviceIdType.MESH


def _bf(x):
    return x.astype(BF16)


def _dot(a, b):
    return jnp.dot(a, b, preferred_element_type=F32)


def _dot_nt(a, b):
    return lax.dot_general(a, b, (((1,), (1,)), ((), ())), preferred_element_type=F32)


def _dot_tn(a, b):
    return lax.dot_general(a, b, (((0,), (0,)), ((), ())), preferred_element_type=F32)


def _dot_hi(a, b):
    return jnp.dot(a, b, preferred_element_type=F32, precision=lax.Precision.HIGHEST)


def _sigmoid(x):
    return 1.0 / (1.0 + jnp.exp(-x))


def _params(sem):
    return pltpu.CompilerParams(dimension_semantics=sem, vmem_limit_bytes=VMEM_LIMIT)


def _mm_nn(x, w, *, name, out_dtypes, tn, epi=None, extra=(), tm=512):
    S, K = x.shape
    N = w.shape[1]
    tm = min(tm, S)
    n_extra = len(extra)

    def body(x_ref, w_ref, *rest):
        acc = _dot(_bf(x_ref[...]), w_ref[...])
        res = epi(acc, *[r[...] for r in rest[:n_extra]]) if epi else (acc,)
        for o, r in zip(rest[n_extra:], res):
            o[...] = r.astype(o.dtype)

    tile = pl.BlockSpec((tm, tn), lambda i, j: (i, j))
    return pl.pallas_call(
        body, grid=(S // tm, N // tn), name=name,
        in_specs=[pl.BlockSpec((tm, K), lambda i, j: (i, 0)), pl.BlockSpec((K, tn), lambda i, j: (0, j))] + [tile] * n_extra,
        out_specs=[tile] * len(out_dtypes),
        out_shape=[jax.ShapeDtypeStruct((S, N), dt) for dt in out_dtypes],
        compiler_params=_params(("parallel", "parallel")),
    )(x, w, *extra)


def _mm_nt(dy, w, *, name, out_dtype, tn, epi=None, extra=(), tm=512):
    S, N = dy.shape
    K = w.shape[0]
    tm = min(tm, S)
    n_extra = len(extra)

    def body(dy_ref, w_ref, *rest):
        acc = _dot_nt(_bf(dy_ref[...]), w_ref[...])
        if epi:
            acc = epi(acc, *[r[...] for r in rest[:n_extra]])
        rest[n_extra][...] = acc.astype(out_dtype)

    tile = pl.BlockSpec((tm, tn), lambda i, j: (i, j))
    return pl.pallas_call(
        body, grid=(S // tm, K // tn), name=name,
        in_specs=[pl.BlockSpec((tm, N), lambda i, j: (i, 0)), pl.BlockSpec((tn, N), lambda i, j: (j, 0))] + [tile] * n_extra,
        out_specs=tile,
        out_shape=jax.ShapeDtypeStruct((S, K), out_dtype),
        compiler_params=_params(("parallel", "parallel")),
    )(dy, w, *extra)


def _mm_tn(x, dy, *, name, tm, tn):
    S, K = x.shape
    N = dy.shape[1]

    def body(x_ref, dy_ref, o_ref):
        o_ref[...] = _dot_tn(_bf(x_ref[...]), _bf(dy_ref[...]))

    return pl.pallas_call(
        body, grid=(K // tm, N // tn), name=name,
        in_specs=[pl.BlockSpec((S, tm), lambda i, j: (0, i)), pl.BlockSpec((S, tn), lambda i, j: (0, j))],
        out_specs=pl.BlockSpec((tm, tn), lambda i, j: (i, j)),
        out_shape=jax.ShapeDtypeStruct((K, N), F32),
        compiler_params=_params(("parallel", "parallel")),
    )(x, dy)


def _rowwise(body, *, tiled, full, out_tiled, out_acc, name, tm=512, smem=()):
    S = tiled[0].shape[0]
    tm = min(tm, S)
    n_in = len(smem) + len(tiled) + len(full)

    def kern(*refs):
        @pl.when(pl.program_id(0) == 0)
        def _():
            for r in refs[n_in + len(out_tiled):]:
                r[...] = jnp.zeros_like(r)
        body(*refs)

    in_specs = [pl.BlockSpec(memory_space=pltpu.SMEM) for _ in smem]
    in_specs += [pl.BlockSpec((tm, a.shape[1]), lambda i: (i, 0)) for a in tiled]
    in_specs += [pl.BlockSpec(a.shape, lambda i, nd=a.ndim: (0,) * nd) for a in full]
    out_specs = [pl.BlockSpec((tm, w), lambda i: (i, 0)) for w, _ in out_tiled]
    out_specs += [pl.BlockSpec(shp, lambda i, nd=len(shp): (0,) * nd) for shp, _ in out_acc]
    out_shape = [jax.ShapeDtypeStruct((S, w), dt) for w, dt in out_tiled]
    out_shape += [jax.ShapeDtypeStruct(shp, dt) for shp, dt in out_acc]
    return pl.pallas_call(
        kern, grid=(S // tm,), name=name, in_specs=in_specs, out_specs=out_specs, out_shape=out_shape,
        compiler_params=_params(("arbitrary",)),
    )(*smem, *tiled, *full)


def _rms_stats(x):
    r = lax.rsqrt(jnp.mean(x * x, axis=-1, keepdims=True) + EPS)
    return r, x * r


def _rmsnorm_fwd(x, g, name):
    def body(x_ref, g_ref, o_ref):
        _, xh = _rms_stats(x_ref[...])
        o_ref[...] = _bf(xh * g_ref[...])

    return _rowwise(body, tiled=[x], full=[g], out_tiled=[(x.shape[1], BF16)], out_acc=[], name=name)[0]


def _rms_bwd_tile(x, g, dxn):
    r, xh = _rms_stats(x)
    dg = jnp.sum(dxn * xh, axis=0, keepdims=True)
    dn = dxn * g
    dx = r * (dn - xh * jnp.mean(dn * xh, axis=-1, keepdims=True))
    return dx, dg


def _rmsnorm_bwd(x, g, dxn, dres, name):
    def body(x_ref, dxn_ref, dres_ref, g_ref, dx_ref, dg_ref):
        dx, dg = _rms_bwd_tile(x_ref[...], g_ref[...], dxn_ref[...])
        dx_ref[...] = dres_ref[...] + dx
        dg_ref[...] += dg

    n = x.shape[1]
    return _rowwise(body, tiled=[x, dxn, dres], full=[g], out_tiled=[(n, F32)], out_acc=[((1, n), F32)], name=name)


def _final_loss(h3, g, target):
    n = h3.shape[1]

    def body(h_ref, t_ref, g_ref, dh_ref, loss_ref, dg_ref):
        x = h_ref[...]
        _, xh = _rms_stats(x)
        e = xh * g_ref[...] - t_ref[...]
        per_tok = jnp.mean(e * e, axis=-1, keepdims=True)
        loss_ref[...] += 0.5 * jnp.sum(per_tok, axis=0, keepdims=True)
        dx, dg = _rms_bwd_tile(x, g_ref[...], e * (1.0 / n))
        dh_ref[...] = dx
        dg_ref[...] += dg

    return _rowwise(body, tiled=[h3, target], full=[g], out_tiled=[(n, F32)],
                    out_acc=[((1, 128), F32), ((1, n), F32)], name="final_loss")


def _ple_bwd(dh3, pp, gate):
    def body(dh_ref, pp_ref, gate_ref, dgl_ref, dpp_ref):
        dh, gt = dh_ref[...], gate_ref[...]
        dgl_ref[...] = _bf(dh * pp_ref[...] * gt * (1.0 - gt))
        dpp_ref[...] = _bf(dh * gt)

    n = dh3.shape[1]
    return _rowwise(body, tiled=[dh3, pp, gate], full=[], out_tiled=[(n, BF16), (n, BF16)], out_acc=[], name="ple_bwd")


def _rope_tables(S):
    half = 32
    inv = 1.0 / (ROPE_THETA ** (jnp.arange(half, dtype=F32) * (2.0 / 64)))
    ang = jnp.arange(S).astype(F32)[:, None] * inv[None, :]
    cos, sin = jnp.cos(ang), jnp.sin(ang)
    return jnp.tile(cos, (1, 4)), jnp.concatenate([-sin, sin, -sin, sin], axis=1)


def _attn_common(i, kc, kp, vc, vp, cc, sc, cp, sp):
    lane = lax.broadcasted_iota(jnp.int32, (1, HEAD_PAIR), 1)
    lane_lo = jnp.bitwise_and(lane, 63) < 32
    slot = [lane < 64, lane >= 64]

    def swap_halves(t):
        return jnp.where(lane_lo, pltpu.roll(t, 96, 1), pltpu.roll(t, 32, 1))

    def rope(t, cos, sin):
        return t * cos + swap_halves(t) * sin

    def unrope(d, cos, sin):
        return d * cos + swap_halves(d * sin)

    k2 = jnp.concatenate([rope(kp, cp, sp), rope(kc, cc, sc)], axis=0)
    v2 = jnp.concatenate([vp, vc], axis=0)
    r = lax.broadcasted_iota(jnp.int32, (ATTN_BLOCK, 2 * ATTN_BLOCK), 0)
    c = lax.broadcasted_iota(jnp.int32, (ATTN_BLOCK, 2 * ATTN_BLOCK), 1)
    valid = (c > r) & (c <= r + ATTN_BLOCK) & jnp.logical_or(c >= ATTN_BLOCK, i > 0)
    ks, vs = {}, {}
    for j in range(2):
        kn = jnp.where(slot[j], k2, 0.0)
        vn = jnp.where(slot[j], v2, 0.0)
        for s in range(2):
            ks[j, s] = _bf(kn if s == j else pltpu.roll(kn, 64, 1))
            vs[j, s] = _bf(vn if s == j else pltpu.roll(vn, 64, 1))
    return slot, rope, unrope, valid, ks, vs


def _attn_probs(qp, k, valid, sink):
    s = jnp.where(valid, _dot_nt(qp, k) * 0.125, NEG)
    m = jnp.maximum(jnp.max(s, axis=1, keepdims=True), sink)
    e = jnp.exp(s - m)
    inv_z = 1.0 / (jnp.sum(e, axis=1, keepdims=True) + jnp.exp(sink - m))
    return e * inv_z, jnp.exp(sink - m) * inv_z


def _attn_specs(S):
    nb = S // ATTN_BLOCK
    prev = lambda i: jnp.maximum(i - 1, 0)
    blk = lambda w, col, row=(lambda i: i): pl.BlockSpec((ATTN_BLOCK, w), lambda i: (row(i), col))
    in_specs = [pl.BlockSpec(memory_space=pltpu.SMEM),
                blk(512, 0), blk(128, 4), blk(128, 4, prev), blk(128, 5), blk(128, 5, prev),
                blk(128, 0), blk(128, 0), blk(128, 0, prev), blk(128, 0, prev)]
    return nb, in_specs


def _attn_fwd(pa, cos, sin, sinks):
    S = pa.shape[0]
    nb, in_specs = _attn_specs(S)

    def body(sinks_ref, q_ref, kc_ref, kp_ref, vc_ref, vp_ref, cc_ref, sc_ref, cp_ref, sp_ref, o_ref):
        i = pl.program_id(0)
        cc, sc = cc_ref[...], sc_ref[...]
        _, rope, _, valid, ks, vs = _attn_common(i, kc_ref[...], kp_ref[...], vc_ref[...], vp_ref[...],
                                                 cc, sc, cp_ref[...], sp_ref[...])
        for pair in range(4):
            cols = slice(HEAD_PAIR * pair, HEAD_PAIR * (pair + 1))
            qp = _bf(rope(q_ref[:, cols], cc, sc))
            acc = jnp.zeros((ATTN_BLOCK, HEAD_PAIR), F32)
            for s in range(2):
                h = 2 * pair + s
                j = h // 4
                p, _ = _attn_probs(qp, ks[j, s], valid, sinks_ref[h])
                acc = acc + _dot(_bf(p), vs[j, s])
            o_ref[:, cols] = acc

    return pl.pallas_call(
        body, grid=(nb,), name="attn_fwd", in_specs=in_specs,
        out_specs=pl.BlockSpec((ATTN_BLOCK, 512), lambda i: (i, 0)),
        out_shape=jax.ShapeDtypeStruct((S, 512), F32),
        compiler_params=_params(("parallel",)),
    )(sinks, pa, pa, pa, pa, pa, cos, sin, cos, sin)


def _attn_bwd(pa, cos, sin, sinks, dcat):
    S = pa.shape[0]
    nb, in_specs = _attn_specs(S)
    in_specs = in_specs + [pl.BlockSpec((ATTN_BLOCK, 512), lambda i: (i, 0))]

    def body(sinks_ref, q_ref, kc_ref, kp_ref, vc_ref, vp_ref, cc_ref, sc_ref, cp_ref, sp_ref, do_ref,
             dq_ref, dk_ref, dv_ref, dsink_ref):
        i = pl.program_id(0)

        @pl.when(i == 0)
        def _():
            dk_ref[...] = jnp.zeros_like(dk_ref)
            dv_ref[...] = jnp.zeros_like(dv_ref)
            dsink_ref[...] = jnp.zeros_like(dsink_ref)

        cc, sc, cp, sp = cc_ref[...], sc_ref[...], cp_ref[...], sp_ref[...]
        slot, rope, unrope, valid, ks, vs = _attn_common(i, kc_ref[...], kp_ref[...], vc_ref[...], vp_ref[...], cc, sc, cp, sp)
        dk2 = jnp.zeros((2 * ATTN_BLOCK, HEAD_PAIR), F32)
        dv2 = jnp.zeros((2 * ATTN_BLOCK, HEAD_PAIR), F32)
        for pair in range(4):
            cols = slice(HEAD_PAIR * pair, HEAD_PAIR * (pair + 1))
            qp = _bf(rope(q_ref[:, cols], cc, sc))
            dob = _bf(do_ref[:, cols])
            dq = jnp.zeros((ATTN_BLOCK, HEAD_PAIR), F32)
            for s in range(2):
                h = 2 * pair + s
                j = h // 4
                p, p_sink = _attn_probs(qp, ks[j, s], valid, sinks_ref[h])
                dp = _dot_nt(dob, vs[j, s])
                dr = jnp.sum(p * dp, axis=1, keepdims=True)
                ds = _bf(p * (dp - dr) * 0.125)
                dsink_ref[h:h + 1, :] += -jnp.sum(p_sink * dr, axis=0, keepdims=True)
                dq = dq + _dot(ds, ks[j, s])
                dk_h = jnp.where(slot[s], _dot_tn(ds, qp), 0.0)
                dv_h = jnp.where(slot[s], _dot_tn(_bf(p), dob), 0.0)
                if s != j:
                    dk_h, dv_h = pltpu.roll(dk_h, 64, 1), pltpu.roll(dv_h, 64, 1)
                dk2, dv2 = dk2 + dk_h, dv2 + dv_h
            dq_ref[:, cols] = unrope(dq, cc, sc)
        cur = pl.ds(pl.multiple_of(i * ATTN_BLOCK, ATTN_BLOCK), ATTN_BLOCK)
        dk_ref[cur, :] += unrope(dk2[ATTN_BLOCK:], cc, sc)
        dv_ref[cur, :] += dv2[ATTN_BLOCK:]

        @pl.when(i > 0)
        def _():
            prv = pl.ds(pl.multiple_of((i - 1) * ATTN_BLOCK, ATTN_BLOCK), ATTN_BLOCK)
            dk_ref[prv, :] += unrope(dk2[:ATTN_BLOCK], cp, sp)
            dv_ref[prv, :] += dv2[:ATTN_BLOCK]

    whole = lambda w: pl.BlockSpec((S, w), lambda i: (0, 0))
    return pl.pallas_call(
        body, grid=(nb,), name="attn_bwd", in_specs=in_specs,
        out_specs=[pl.BlockSpec((ATTN_BLOCK, 512), lambda i: (i, 0)), whole(128), whole(128),
                   pl.BlockSpec((8, 128), lambda i: (0, 0))],
        out_shape=[jax.ShapeDtypeStruct((S, 512), F32), jax.ShapeDtypeStruct((S, 128), F32),
                   jax.ShapeDtypeStruct((S, 128), F32), jax.ShapeDtypeStruct((8, 128), F32)],
        compiler_params=_params(("arbitrary",)),
    )(sinks, pa, pa, pa, pa, pa, cos, sin, cos, sin, dcat)


CONV_ROWS = 512
CONV_PAD = 8


def _conv_silu(scr, w, r0):
    y = w[3:4, :] * scr[pl.ds(CONV_PAD + r0, CONV_ROWS), :]
    for j in range(DN_CONV - 1):
        y = y + w[j:j + 1, :] * scr[pl.ds(CONV_PAD + r0 - 3 + j, CONV_ROWS), :]
    return y


def _dn_prep_fwd(pd, conv_w):
    S = pd.shape[0]
    rows = min(CONV_ROWS, S)
    assert rows == CONV_ROWS

    def body(x_ref, w_ref, o_ref, scr):
        b = pl.program_id(0)
        scr[0:CONV_PAD, :] = jnp.zeros((CONV_PAD, DN_DIM), F32)
        scr[pl.ds(CONV_PAD, S), :] = x_ref[...]
        w = w_ref[...]
        q_scale = jnp.where(b < DN_HEADS, DN_DIM ** -0.5, 1.0)
        for r0 in range(0, S, CONV_ROWS):
            y = _conv_silu(scr, w, r0)
            a = y * _sigmoid(y)
            rs = lax.rsqrt(jnp.sum(a * a, axis=1, keepdims=True) + EPS)
            o_ref[pl.ds(r0, CONV_ROWS), :] = a * jnp.where(b < 2 * DN_HEADS, rs * q_scale, 1.0)

    col = pl.BlockSpec((S, DN_DIM), lambda b: (0, b))
    return pl.pallas_call(
        body, grid=(3 * DN_HEADS,), name="dn_prep_fwd",
        in_specs=[col, pl.BlockSpec((DN_CONV, DN_DIM), lambda b: (0, b))], out_specs=col,
        out_shape=jax.ShapeDtypeStruct((S, 3 * DN_HEADS * DN_DIM), F32),
        scratch_shapes=[pltpu.VMEM((S + CONV_PAD, DN_DIM), F32)],
        compiler_params=_params(("parallel",)),
    )(pd, conv_w)


def _dn_prep_bwd(pd, conv_w, dqkv):
    S = pd.shape[0]

    def body(x_ref, w_ref, d_ref, dx_ref, dw_ref, scr, dscr):
        b = pl.program_id(0)
        scr[0:CONV_PAD, :] = jnp.zeros((CONV_PAD, DN_DIM), F32)
        scr[pl.ds(CONV_PAD, S), :] = x_ref[...]
        dscr[pl.ds(S, CONV_PAD), :] = jnp.zeros((CONV_PAD, DN_DIM), F32)
        w = w_ref[...]
        q_scale = jnp.where(b < DN_HEADS, DN_DIM ** -0.5, 1.0)
        is_qk = b < 2 * DN_HEADS
        dw = [jnp.zeros((1, DN_DIM), F32) for _ in range(DN_CONV)]
        for r0 in range(0, S, CONV_ROWS):
            y = _conv_silu(scr, w, r0)
            sg = _sigmoid(y)
            a = y * sg
            dout = d_ref[pl.ds(r0, CONV_ROWS), :]
            rs = lax.rsqrt(jnp.sum(a * a, axis=1, keepdims=True) + EPS)
            da_qk = q_scale * rs * (dout - a * (rs * rs) * jnp.sum(dout * a, axis=1, keepdims=True))
            dy = jnp.where(is_qk, da_qk, dout) * (sg * (1.0 + y * (1.0 - sg)))
            dscr[pl.ds(r0, CONV_ROWS), :] = dy
            for j in range(DN_CONV):
                dw[j] = dw[j] + jnp.sum(dy * scr[pl.ds(CONV_PAD + r0 - 3 + j, CONV_ROWS), :], axis=0, keepdims=True)
        for j in range(DN_CONV):
            dw_ref[j:j + 1, :] = dw[j]
        for r0 in range(0, S, CONV_ROWS):
            dx = w[3:4, :] * dscr[pl.ds(r0, CONV_ROWS), :]
            for j in range(DN_CONV - 1):
                dx = dx + w[j:j + 1, :] * dscr[pl.ds(r0 + 3 - j, CONV_ROWS), :]
            dx_ref[pl.ds(r0, CONV_ROWS), :] = dx

    col = pl.BlockSpec((S, DN_DIM), lambda b: (0, b))
    wcol = pl.BlockSpec((DN_CONV, DN_DIM), lambda b: (0, b))
    return pl.pallas_call(
        body, grid=(3 * DN_HEADS,), name="dn_prep_bwd",
        in_specs=[col, wcol, col], out_specs=[col, wcol],
        out_shape=[jax.ShapeDtypeStruct((S, 3 * DN_HEADS * DN_DIM), F32), jax.ShapeDtypeStruct((DN_CONV, 3 * DN_HEADS * DN_DIM), F32)],
        scratch_shapes=[pltpu.VMEM((S + CONV_PAD, DN_DIM), F32), pltpu.VMEM((S + CONV_PAD, DN_DIM), F32)],
        compiler_params=_params(("parallel",)),
    )(pd, conv_w, dqkv)


def _chunk_masks():
    C = DN_CHUNK
    ii = lax.broadcasted_iota(jnp.int32, (C, C), 0)
    jj = lax.broadcasted_iota(jnp.int32, (C, C), 1)
    return ii, jj


def _col_to_row(col, ii, jj):
    return jnp.sum(jnp.where(ii == jj, col, 0.0), axis=0, keepdims=True)


def _row_to_col(row, ii, jj):
    return jnp.sum(jnp.where(ii == jj, row, 0.0), axis=1, keepdims=True)


def _decay(gc_col, ii, jj):
    diff = gc_col - _col_to_row(gc_col, ii, jj)
    return jnp.where(jj <= ii, jnp.exp(jnp.where(jj <= ii, diff, 0.0)), 0.0)


def _softplus(x):
    return jnp.maximum(x, 0.0) + jnp.log(1.0 + jnp.exp(-jnp.abs(x)))


def _head(h):
    return slice(DN_DIM * h, DN_DIM * (h + 1))


def _dn_chunk_fwd(qkv, pg, a_log, dt_bias):
    S = qkv.shape[0]
    C = DN_CHUNK
    nc = S // C

    def body(alog_ref, dtb_ref, qkv_ref, pg_ref, w_ref, u_ref, qg_ref, kd_ref, a_ref, t_ref, gcs_ref):
        ii, jj = _chunk_masks()
        lane = lax.broadcasted_iota(jnp.int32, (1, 128), 1)
        eye = (ii == jj).astype(F32)
        gcs = jnp.zeros((C, 128), F32)
        for h in range(DN_HEADS):
            q, k, v = qkv_ref[:, _head(h)], qkv_ref[:, _head(DN_HEADS + h)], qkv_ref[:, _head(2 * DN_HEADS + h)]
            beta = _sigmoid(pg_ref[:, h:h + 1])
            g_col = -jnp.exp(alog_ref[h]) * _softplus(pg_ref[:, DN_HEADS + h:DN_HEADS + h + 1] + dtb_ref[h])
            g_row = _col_to_row(g_col, ii, jj)
            gc_col = jnp.sum(jnp.where(jj <= ii, g_row, 0.0), axis=1, keepdims=True)
            dec = _decay(gc_col, ii, jj)
            eg = jnp.exp(gc_col)
            kb, vb = k * beta, v * beta
            kbb, kbf = _bf(kb), _bf(k)
            lmat = jnp.where(jj < ii, _dot_nt(kbb, kbf) * dec, 0.0)
            pw = -lmat
            t = eye + pw
            for _ in range(5):
                pw = _dot_hi(pw, pw)
                t = t + _dot_hi(t, pw)
            tb = _bf(t)
            u_ref[:, _head(h)] = _dot(tb, _bf(vb))
            w_ref[:, _head(h)] = _dot(tb, _bf(kb * eg))
            a_ref[h] = _dot_nt(_bf(q), kbf) * dec
            t_ref[h] = t
            qg_ref[:, _head(h)] = q * eg
            kd_ref[:, _head(h)] = k * jnp.exp(gc_col[C - 1:C, :] - gc_col)
            gcs = gcs + jnp.where(lane == h, gc_col, 0.0) + jnp.where(lane == DN_HEADS + h, beta, 0.0) \
                + jnp.where(lane == 2 * DN_HEADS + h, g_col, 0.0)
        gcs_ref[...] = gcs

    smem = pl.BlockSpec(memory_space=pltpu.SMEM)
    wide = pl.BlockSpec((C, 512), lambda n: (n, 0))
    sq = pl.BlockSpec((DN_HEADS, C, C), lambda n: (0, n, 0))
    narrow = pl.BlockSpec((C, 128), lambda n: (n, 0))
    f = lambda *shp: jax.ShapeDtypeStruct(shp, F32)
    return pl.pallas_call(
        body, grid=(nc,), name="dn_chunk_fwd",
        in_specs=[smem, smem, pl.BlockSpec((C, 1536), lambda n: (n, 0)), narrow],
        out_specs=[wide, wide, wide, wide, sq, sq, narrow],
        out_shape=[f(S, 512), f(S, 512), f(S, 512), f(S, 512), f(DN_HEADS, S, C), f(DN_HEADS, S, C), f(S, 128)],
        compiler_params=_params(("parallel",)),
    )(a_log, dt_bias, qkv, pg)


def _gated_norm(o, z, gn):
    r, oh = _rms_stats(o)
    return oh * gn * (z * _sigmoid(z))


def _dn_scan_fwd(w, u, qg, kd, a, gcs, pz, gn):
    S = w.shape[0]
    C = DN_CHUNK
    nc = S // C

    def body(w_ref, u_ref, qg_ref, kd_ref, a_ref, gcs_ref, z_ref, gn_ref, o_ref, vn_ref, sst_ref, out_ref, state):
        @pl.when(pl.program_id(0) == 0)
        def _():
            state[...] = jnp.zeros_like(state)

        for h in range(DN_HEADS):
            hs = _head(h)
            s_in = state[h]
            sst_ref[0, h] = s_in
            sb = _bf(s_in)
            vn = u_ref[:, hs] - _dot(_bf(w_ref[:, hs]), sb)
            vnb = _bf(vn)
            o = _dot(_bf(qg_ref[:, hs]), sb) + _dot(_bf(a_ref[h]), vnb)
            d_last = jnp.exp(gcs_ref[C - 1:C, h:h + 1])
            state[h] = s_in * d_last + _dot_tn(_bf(kd_ref[:, hs]), vnb)
            o_ref[:, hs] = o
            vn_ref[:, hs] = vn
            out_ref[:, hs] = _gated_norm(o, z_ref[:, hs], gn_ref[...])

    wide = pl.BlockSpec((C, 512), lambda n: (n, 0))
    f = lambda *shp: jax.ShapeDtypeStruct(shp, F32)
    return pl.pallas_call(
        body, grid=(nc,), name="dn_scan_fwd",
        in_specs=[wide, wide, wide, wide, pl.BlockSpec((DN_HEADS, C, C), lambda n: (0, n, 0)),
                  pl.BlockSpec((C, 128), lambda n: (n, 0)), wide, pl.BlockSpec((1, DN_DIM), lambda n: (0, 0))],
        out_specs=[wide, wide, pl.BlockSpec((1, DN_HEADS, DN_DIM, DN_DIM), lambda n: (n, 0, 0, 0)), wide],
        out_shape=[f(S, 512), f(S, 512), f(nc, DN_HEADS, DN_DIM, DN_DIM), f(S, 512)],
        scratch_shapes=[pltpu.VMEM((DN_HEADS, DN_DIM, DN_DIM), F32)],
        compiler_params=_params(("arbitrary",)),
    )(w, u, qg, kd, a, gcs, pz, gn)


def _dn_scan_bwd(dcat, o, pz, gn, sst, vnew, w, qg, kd, a, gcs):
    S = o.shape[0]
    C = DN_CHUNK
    nc = S // C

    def body(dy_ref, o_ref, z_ref, gn_ref, sst_ref, vn_ref, w_ref, qg_ref, kd_ref, a_ref, gcs_ref,
             du_ref, dw_ref, dqg_ref, dkd_ref, da_ref, dz_ref, dsc_ref, dgn_ref, dstate):
        @pl.when(pl.program_id(0) == 0)
        def _():
            dstate[...] = jnp.zeros_like(dstate)
            dgn_ref[...] = jnp.zeros_like(dgn_ref)

        gn_ = gn_ref[...]
        lane = lax.broadcasted_iota(jnp.int32, (C, 128), 1)
        row = lax.broadcasted_iota(jnp.int32, (C, 128), 0)
        dsc = jnp.zeros((C, 128), F32)
        for h in range(DN_HEADS):
            hs = _head(h)
            ov, z, dout = o_ref[:, hs], z_ref[:, hs], dy_ref[:, hs]
            r, oh = _rms_stats(ov)
            sg = _sigmoid(z)
            don = dout * (z * sg)
            dz_ref[:, hs] = dout * (oh * gn_) * (sg * (1.0 + z * (1.0 - sg)))
            dgn_ref[...] += jnp.sum(don * oh, axis=0, keepdims=True)
            dn = don * gn_
            do = _bf(r * (dn - oh * jnp.mean(dn * oh, axis=-1, keepdims=True)))
            s_in = sst_ref[0, h]
            sb = _bf(s_in)
            ds_out = dstate[h]
            dsb = _bf(ds_out)
            vnb = _bf(vn_ref[:, hs])
            wb, qgb, kdb, ab = _bf(w_ref[:, hs]), _bf(qg_ref[:, hs]), _bf(kd_ref[:, hs]), _bf(a_ref[h])
            dvn = _dot_tn(ab, do) + _dot(kdb, dsb)
            dvnb = _bf(dvn)
            da_ref[h] = _dot_nt(do, vnb)
            dqg_ref[:, hs] = _dot_nt(do, sb)
            dkd_ref[:, hs] = _dot_nt(vnb, dsb)
            dw_ref[:, hs] = -_dot_nt(dvnb, sb)
            du_ref[:, hs] = dvn
            d_last = jnp.exp(gcs_ref[C - 1:C, h:h + 1])
            dd = jnp.sum(jnp.sum(ds_out * s_in, axis=1, keepdims=True), axis=0, keepdims=True)
            dsc = dsc + jnp.where((lane == h) & (row == C - 1), dd * d_last, 0.0)
            dstate[h] = ds_out * d_last + _dot_tn(qgb, do) - _dot_tn(wb, dvnb)
        dsc_ref[...] = dsc

    rev = lambda n: nc - 1 - n
    wide = pl.BlockSpec((C, 512), lambda n: (rev(n), 0))
    sq = pl.BlockSpec((DN_HEADS, C, C), lambda n: (0, rev(n), 0))
    narrow = pl.BlockSpec((C, 128), lambda n: (rev(n), 0))
    gn_spec = pl.BlockSpec((1, DN_DIM), lambda n: (0, 0))
    f = lambda *shp: jax.ShapeDtypeStruct(shp, F32)
    return pl.pallas_call(
        body, grid=(nc,), name="dn_scan_bwd",
        in_specs=[pl.BlockSpec((C, 512), lambda n: (rev(n), 1)), wide, wide, gn_spec,
                  pl.BlockSpec((1, DN_HEADS, DN_DIM, DN_DIM), lambda n: (rev(n), 0, 0, 0)),
                  wide, wide, wide, wide, sq, narrow],
        out_specs=[wide, wide, wide, wide, sq, wide, narrow, gn_spec],
        out_shape=[f(S, 512), f(S, 512), f(S, 512), f(S, 512), f(DN_HEADS, S, C), f(S, 512), f(S, 128), f(1, DN_DIM)],
        scratch_shapes=[pltpu.VMEM((DN_HEADS, DN_DIM, DN_DIM), F32)],
        compiler_params=_params(("arbitrary",)),
    )(dcat, o, pz, gn, sst, vnew, w, qg, kd, a, gcs)


def _dn_chunk_bwd(qkv, pg, t_inv, gcs, du, dw, dqg, dkd, da, dsc, a_log, dt_bias):
    S = qkv.shape[0]
    C = DN_CHUNK
    nc = S // C

    def body(alog_ref, dtb_ref, qkv_ref, pg_ref, t_ref, gcs_ref, du_ref, dw_ref, dqg_ref, dkd_ref, da_ref, dsc_ref,
             dqkv_ref, dpg_ref, acc_ref):
        @pl.when(pl.program_id(0) == 0)
        def _():
            acc_ref[...] = jnp.zeros_like(acc_ref)

        ii, jj = _chunk_masks()
        lane = lax.broadcasted_iota(jnp.int32, (1, 128), 1)
        row8 = lax.broadcasted_iota(jnp.int32, (8, 128), 0)
        lane8 = lax.broadcasted_iota(jnp.int32, (8, 128), 1)
        rowc = lax.broadcasted_iota(jnp.int32, (C, 1), 0)
        tril, strict = jj <= ii, jj < ii
        dpg = jnp.zeros((C, 128), F32)
        acc = jnp.zeros((8, 128), F32)
        for h in range(DN_HEADS):
            q, k, v = qkv_ref[:, _head(h)], qkv_ref[:, _head(DN_HEADS + h)], qkv_ref[:, _head(2 * DN_HEADS + h)]
            gc_col, beta, g_col = gcs_ref[:, h:h + 1], gcs_ref[:, DN_HEADS + h:DN_HEADS + h + 1], \
                gcs_ref[:, 2 * DN_HEADS + h:2 * DN_HEADS + h + 1]
            dec = _decay(gc_col, ii, jj)
            eg = jnp.exp(gc_col)
            g_last = gc_col[C - 1:C, :]
            ek = jnp.exp(g_last - gc_col)
            kb, vb = k * beta, v * beta
            kbg = kb * eg
            qb, kbf, kbb = _bf(q), _bf(k), _bf(kb)
            t = t_ref[h]
            tb = _bf(t)
            dub, dwb = _bf(du_ref[:, _head(h)]), _bf(dw_ref[:, _head(h)])
            dqg_, dkd_ = dqg_ref[:, _head(h)], dkd_ref[:, _head(h)]
            dt = _dot_nt(dub, _bf(vb)) + _dot_nt(dwb, _bf(kbg))
            dvb = _dot_tn(tb, dub)
            dkbg = _dot_tn(tb, dwb)
            dl = -lax.dot_general(lax.dot_general(t, dt, (((0,), (0,)), ((), ())), preferred_element_type=F32,
                                                  precision=lax.Precision.HIGHEST),
                                  t, (((1,), (1,)), ((), ())), preferred_element_type=F32, precision=lax.Precision.HIGHEST)
            kk = _dot_nt(kbb, kbf)
            qk = _dot_nt(qb, kbf)
            dm = jnp.where(strict, dl * dec, 0.0)
            dqk = jnp.where(tril, da_ref[h] * dec, 0.0)
            gmat = dm * kk + dqk * qk
            dgc = jnp.sum(gmat, axis=1, keepdims=True) - _row_to_col(jnp.sum(gmat, axis=0, keepdims=True), ii, jj)
            dmb, dqkb = _bf(dm), _bf(dqk)
            dkb = _dot(dmb, kbf) + dkbg * eg
            dk = _dot_tn(dmb, kbb) + _dot_tn(dqkb, qb) + dkd_ * ek
            dq = _dot(dqkb, kbf) + dqg_ * eg
            tk = jnp.sum(dkd_ * k * ek, axis=1, keepdims=True)
            dgc = dgc + jnp.sum(dqg_ * q * eg, axis=1, keepdims=True) - tk + jnp.sum(dkbg * kbg, axis=1, keepdims=True)
            dgl = jnp.sum(tk, axis=0, keepdims=True) + dsc_ref[C - 1:C, h:h + 1]
            dgc = dgc + jnp.where(rowc == C - 1, dgl, 0.0)
            dk = dk + dkb * beta
            dbeta = jnp.sum(dkb * k, axis=1, keepdims=True) + jnp.sum(dvb * v, axis=1, keepdims=True)
            dqkv_ref[:, _head(h)] = dq
            dqkv_ref[:, _head(DN_HEADS + h)] = dk
            dqkv_ref[:, _head(2 * DN_HEADS + h)] = dvb * beta
            dg_col = jnp.sum(jnp.where(jj >= ii, _col_to_row(dgc, ii, jj), 0.0), axis=1, keepdims=True)
            db = dbeta * beta * (1.0 - beta)
            da_in = dg_col * (-jnp.exp(alog_ref[h])) * _sigmoid(pg_ref[:, DN_HEADS + h:DN_HEADS + h + 1] + dtb_ref[h])
            dpg = dpg + jnp.where(lane == h, db, 0.0) + jnp.where(lane == DN_HEADS + h, da_in, 0.0)
            acc = acc + jnp.where((row8 == 0) & (lane8 == h), jnp.sum(dg_col * g_col, axis=0, keepdims=True), 0.0) \
                + jnp.where((row8 == 1) & (lane8 == h), jnp.sum(da_in, axis=0, keepdims=True), 0.0)
        dpg_ref[...] = dpg
        acc_ref[...] += acc

    smem = pl.BlockSpec(memory_space=pltpu.SMEM)
    wide = pl.BlockSpec((C, 512), lambda n: (n, 0))
    sq = pl.BlockSpec((DN_HEADS, C, C), lambda n: (0, n, 0))
    narrow = pl.BlockSpec((C, 128), lambda n: (n, 0))
    qkv_spec = pl.BlockSpec((C, 1536), lambda n: (n, 0))
    f = lambda *shp: jax.ShapeDtypeStruct(shp, F32)
    return pl.pallas_call(
        body, grid=(nc,), name="dn_chunk_bwd",
        in_specs=[smem, smem, qkv_spec, narrow, sq, narrow, wide, wide, wide, wide, sq, narrow],
        out_specs=[qkv_spec, narrow, pl.BlockSpec((8, 128), lambda n: (0, 0))],
        out_shape=[f(S, 1536), f(S, 128), f(8, 128)],
        compiler_params=_params(("arbitrary",)),
    )(a_log, dt_bias, qkv, pg, t_inv, gcs, du, dw, dqg, dkd, da, dsc)


def _local_step(x, p, target, wts):
    S = x.shape[0]
    w_in, w_o, w_up, w_down, w_pg, w_pp = (wts[k] for k in ("w_in", "w_o", "w_up", "w_down", "w_ple_gate", "w_ple_proj"))
    cos, sin = _rope_tables(S)
    sinks, a_log, dt_bias = wts["sinks"].reshape(8), wts["a_log"].reshape(4), wts["dt_bias"].reshape(4)
    gn = wts["dn_norm"].reshape(1, DN_DIM)
    add = lambda acc, res: (acc + res,)

    u = _rmsnorm_fwd(x, wts["norm_mix"], "norm_mix_fwd")
    pa, = _mm_nn(u, w_in[:, 0:768], name="in_proj_attn", out_dtypes=[F32], tn=256)
    pd, = _mm_nn(u, w_in[:, 768:2304], name="in_proj_dn", out_dtypes=[F32], tn=512)
    pz, = _mm_nn(u, w_in[:, 2304:2816], name="in_proj_z", out_dtypes=[F32], tn=512)
    pg, = _mm_nn(u, w_in[:, 2816:2944], name="in_proj_gates", out_dtypes=[F32], tn=128)
    attn = _attn_fwd(pa, cos, sin, sinks)
    qkv = _dn_prep_fwd(pd, wts["conv_w"])
    cw, cu, cqg, ckd, ca, ct, gcs = _dn_chunk_fwd(qkv, pg, a_log, dt_bias)
    o, vnew, sst, dn_out = _dn_scan_fwd(cw, cu, cqg, ckd, ca, gcs, pz, gn)
    h1a, = _mm_nn(attn, w_o[:512], name="out_proj_attn", out_dtypes=[F32], tn=512, epi=add, extra=[x])
    h1, = _mm_nn(dn_out, w_o[512:], name="out_proj_dn", out_dtypes=[F32], tn=512, epi=add, extra=[h1a])
    m = _rmsnorm_fwd(h1, wts["norm_mlp"], "norm_mlp_fwd")

    def relu2(acc):
        r = jnp.maximum(acc, 0.0)
        return r * r, r

    hid, relu = _mm_nn(m, w_up, name="mlp_up", out_dtypes=[BF16, BF16], tn=512, epi=relu2)
    h2, = _mm_nn(hid, w_down, name="mlp_down", out_dtypes=[F32], tn=512, epi=add, extra=[h1])
    n3 = _rmsnorm_fwd(h2, wts["norm_ple"], "norm_ple_fwd")
    pp, = _mm_nn(p, w_pp, name="ple_proj", out_dtypes=[F32], tn=512)

    def ple(acc, h2_t, pp_t):
        gate = _sigmoid(acc)
        return h2_t + gate * pp_t, gate

    h3, gate = _mm_nn(n3, w_pg, name="ple_gate", out_dtypes=[F32, F32], tn=512, epi=ple, extra=[h2, pp])
    dh3, loss, d_norm_final = _final_loss(h3, wts["norm_final"].reshape(1, D_MODEL), target)

    g = {"norm_final": d_norm_final}
    dgl, dpp = _ple_bwd(dh3, pp, gate)
    g["w_ple_gate"] = _mm_tn(n3, dgl, name="d_w_ple_gate", tm=512, tn=512)
    g["w_ple_proj"] = _mm_tn(p, dpp, name="d_w_ple_proj", tm=256, tn=512)
    dn3 = _mm_nt(dgl, w_pg, name="d_n3", out_dtype=F32, tn=512)
    dh2, g["norm_ple"] = _rmsnorm_bwd(h2, wts["norm_ple"], dn3, dh3, "norm_ple_bwd")
    d_act = _mm_nt(dh2, w_down, name="d_hidden", out_dtype=BF16, tn=512, epi=lambda acc, r: acc * (2.0 * r.astype(F32)), extra=[relu])
    g["w_down"] = _mm_tn(hid, dh2, name="d_w_down", tm=512, tn=512)
    g["w_up"] = _mm_tn(m, d_act, name="d_w_up", tm=512, tn=512)
    dm = _mm_nt(d_act, w_up, name="d_m", out_dtype=F32, tn=512)
    dh1, g["norm_mlp"] = _rmsnorm_bwd(h1, wts["norm_mlp"], dm, dh2, "norm_mlp_bwd")
    dcat = _mm_nt(dh1, w_o, name="d_cat", out_dtype=F32, tn=512)
    g["w_o"] = jnp.concatenate([_mm_tn(attn, dh1, name="d_w_o_attn", tm=512, tn=512),
                                _mm_tn(dn_out, dh1, name="d_w_o_dn", tm=512, tn=512)], axis=0)
    dq, dk, dv, dsinks = _attn_bwd(pa, cos, sin, sinks, dcat)
    g["sinks"] = dsinks[:, 0].reshape(1, 8)
    du_, dw_, dqg, dkd, da, dz, dsc, g_dn_norm = _dn_scan_bwd(dcat, o, pz, gn, sst, vnew, cw, cqg, ckd, ca, gcs)
    g["dn_norm"] = g_dn_norm
    dqkv, dpg, gate_acc = _dn_chunk_bwd(qkv, pg, ct, gcs, du_, dw_, dqg, dkd, da, dsc, a_log, dt_bias)
    g["a_log"], g["dt_bias"] = gate_acc[0:1, 0:4], gate_acc[1:2, 0:4]
    dpd, g["conv_w"] = _dn_prep_bwd(pd, wts["conv_w"], dqkv)
    dproj = jnp.concatenate([_bf(dq), _bf(dk), _bf(dv), _bf(dpd), _bf(dz), _bf(dpg),
                             jnp.zeros((S, D_IN_PAD - 2944), BF16)], axis=1)
    g["w_in"] = _mm_tn(u, dproj, name="d_w_in", tm=512, tn=512)
    du_in = _mm_nt(dproj, w_in, name="d_u", out_dtype=F32, tn=512)
    grad_x, g["norm_mix"] = _rmsnorm_bwd(x, wts["norm_mix"], du_in, dh1, "norm_mix_bwd")
    return loss, grad_x, g


def _peer(k):
    x, y, c = lax.axis_index("x"), lax.axis_index("y"), lax.axis_index("c")
    px = 1 - x if k & 4 else x
    py = 1 - y if k & 2 else y
    pc = 1 - c if k & 1 else c
    return (px, py, pc), 4 * px + 2 * py + pc


def _exchange(src, name, gather):
    shape = src.shape if not gather else (N_DEV,) + src.shape

    def body(src_ref, out_ref, send_sems, recv_sems, local_sem):
        _, me = _peer(0)
        piece = (lambda d: src_ref) if gather else (lambda d: src_ref.at[d])
        mine = pltpu.make_async_copy(piece(me), out_ref.at[me], local_sem)
        mine.start()
        copies = []
        for k in range(1, N_DEV):
            dev, idx = _peer(k)
            cp = pltpu.make_async_remote_copy(src_ref=piece(idx), dst_ref=out_ref.at[me], send_sem=send_sems.at[k - 1],
                                              recv_sem=recv_sems.at[k - 1], device_id=dev, device_id_type=MESH)
            cp.start()
            copies.append(cp)
        for cp in copies:
            cp.wait_recv()
        for cp in copies:
            cp.wait_send()
        mine.wait()

    anywhere = pl.BlockSpec(memory_space=pl.ANY)
    return pl.pallas_call(
        body, name=name, in_specs=[anywhere], out_specs=anywhere,
        out_shape=jax.ShapeDtypeStruct(shape, src.dtype),
        scratch_shapes=[pltpu.SemaphoreType.DMA((N_DEV - 1,)), pltpu.SemaphoreType.DMA((N_DEV - 1,)), pltpu.SemaphoreType.DMA],
    )(src)


def _adamw(parts, w, m, v, name):
    n, R, W = parts.shape
    tm = 8 if R <= 64 else next(t for t in (128, 64, 32, 16, 8) if R % t == 0)

    def body(p_ref, w_ref, m_ref, v_ref, g_ref, d_ref, nm_ref, nv_ref):
        g = p_ref[0].astype(F32)
        for s in range(1, n):
            g = g + p_ref[s].astype(F32)
        nm = ADAM_B1 * m_ref[...] + (1.0 - ADAM_B1) * g
        nv = ADAM_B2 * v_ref[...] + (1.0 - ADAM_B2) * (g * g)
        m_hat = nm / (1.0 - ADAM_B1 ** ADAM_STEP)
        v_hat = nv / (1.0 - ADAM_B2 ** ADAM_STEP)
        g_ref[...] = g
        d_ref[...] = -ADAM_LR * (m_hat / (jnp.sqrt(v_hat) + ADAM_EPS) + ADAM_WD * w_ref[...])
        nm_ref[...] = nm
        nv_ref[...] = nv

    tile = pl.BlockSpec((tm, W), lambda i: (i, 0))
    return pl.pallas_call(
        body, grid=(R // tm,), name=name,
        in_specs=[pl.BlockSpec((n, tm, W), lambda i: (0, i, 0)), tile, tile, tile],
        out_specs=[tile] * 4, out_shape=[jax.ShapeDtypeStruct((R, W), F32)] * 4,
        compiler_params=_params(("parallel",)),
    )(parts, w, m, v)


_PACK = (("w_in", W_IN_SHARD_PAD), ("w_o", 128), ("w_up", 512), ("w_down", 512), ("w_ple_gate", 128), ("w_ple_proj", 32))
_PACK_ROWS = sum(r for _, r in _PACK)


def _pack_shard(s):
    return jnp.concatenate([
        jnp.pad(s["w_in"], ((0, 0), (0, W_IN_SHARD_PAD - W_IN_SHARD))).reshape(W_IN_SHARD_PAD, 1024),
        s["w_o"], s["w_up"].reshape(512, 1024), s["w_down"], s["w_ple_gate"], s["w_ple_proj"].reshape(32, 1024)], axis=0)


def _unpack_shard(buf):
    out, r0 = {}, 0
    for name, rows in _PACK:
        out[name] = buf[r0:r0 + rows]
        r0 += rows
    out["w_in"] = out["w_in"].reshape(1024, W_IN_SHARD_PAD)[:, :W_IN_SHARD]
    out["w_up"] = out["w_up"].reshape(1024, 512)
    out["w_ple_proj"] = out["w_ple_proj"].reshape(256, 128)
    return out


def _unpack_gathered(buf):
    out, r0 = {}, 0
    for name, rows in _PACK:
        out[name] = buf[:, r0:r0 + rows]
        r0 += rows
    w_in = out["w_in"].reshape(N_DEV, 1024, W_IN_SHARD_PAD)[:, :, :W_IN_SHARD]
    w_in = jnp.transpose(w_in, (1, 0, 2)).reshape(1024, D_IN)
    out["w_in"] = jnp.pad(w_in, ((0, 0), (0, D_IN_PAD - D_IN)))
    out["w_o"] = out["w_o"].reshape(1024, 1024)
    out["w_up"] = jnp.transpose(out["w_up"].reshape(N_DEV, 1024, 512), (1, 0, 2)).reshape(1024, 4096)
    out["w_down"] = out["w_down"].reshape(4096, 1024)
    out["w_ple_gate"] = out["w_ple_gate"].reshape(1024, 1024)
    out["w_ple_proj"] = jnp.transpose(out["w_ple_proj"].reshape(N_DEV, 256, 128), (1, 0, 2)).reshape(256, 1024)
    return out


def _pack_full_grads(g):
    w_in = jnp.transpose(g["w_in"][:, :D_IN].reshape(1024, N_DEV, W_IN_SHARD), (1, 0, 2))
    w_in = jnp.pad(w_in, ((0, 0), (0, 0), (0, W_IN_SHARD_PAD - W_IN_SHARD))).reshape(N_DEV, W_IN_SHARD_PAD, 1024)
    parts = [w_in, g["w_o"].reshape(N_DEV, 128, 1024),
             jnp.transpose(g["w_up"].reshape(1024, N_DEV, 512), (1, 0, 2)).reshape(N_DEV, 512, 1024),
             g["w_down"].reshape(N_DEV, 512, 1024), g["w_ple_gate"].reshape(N_DEV, 128, 1024),
             jnp.transpose(g["w_ple_proj"].reshape(256, N_DEV, 128), (1, 0, 2)).reshape(N_DEV, 32, 1024)]
    return jnp.concatenate([_bf(t) for t in parts], axis=1)


_SMALL_ROWS = 16
_VEC_ROW = {"norm_mix": 0, "norm_mlp": 1, "norm_ple": 2, "norm_final": 3}
_VEC_LANES = {"a_log": (0, 4), "dt_bias": (4, 8), "sinks": (8, 16), "dn_norm": (128, 256)}
_LOSS_ROW, _CONV_ROW = 5, 8


def _pack_small(vals, extra_rows):
    row4 = jnp.zeros((1024,), F32)
    for n, (a, b) in _VEC_LANES.items():
        row4 = row4.at[a:b].set(vals[n].reshape(b - a))
    rows = [vals[n].reshape(1, 1024) for n in ("norm_mix", "norm_mlp", "norm_ple", "norm_final")] + [row4.reshape(1, 1024)]
    return jnp.concatenate(rows + extra_rows, axis=0)


def _unpack_small(buf, like):
    out = {n: buf[r].reshape(like[n].shape) for n, r in _VEC_ROW.items()}
    for n, (a, b) in _VEC_LANES.items():
        out[n] = buf[4, a:b].reshape(like[n].shape)
    return out


_MATRICES = tuple(n for n, _ in _PACK)
_ORDER = ("norm_mix", "w_in", "conv_w", "a_log", "dt_bias", "dn_norm", "sinks", "w_o", "norm_mlp", "w_up", "w_down",
          "norm_ple", "w_ple_gate", "w_ple_proj", "norm_final")


def kernel(x, p, norm_mix, w_in, conv_w, a_log, dt_bias, dn_norm, sinks, w_o, norm_mlp, w_up, w_down, norm_ple, w_ple_gate, w_ple_proj, norm_final, loss_target, m_norm_mix, m_w_in, m_conv_w, m_a_log, m_dt_bias, m_dn_norm, m_sinks, m_w_o, m_norm_mlp, m_w_up, m_w_down, m_norm_ple, m_w_ple_gate, m_w_ple_proj, m_norm_final, v_norm_mix, v_w_in, v_conv_w, v_a_log, v_dt_bias, v_dn_norm, v_sinks, v_w_o, v_norm_mlp, v_w_up, v_w_down, v_norm_ple, v_w_ple_gate, v_w_ple_proj, v_norm_final):
    w = dict(norm_mix=norm_mix, w_in=w_in[0], conv_w=conv_w[0], a_log=a_log, dt_bias=dt_bias, dn_norm=dn_norm, sinks=sinks,
             w_o=w_o[0], norm_mlp=norm_mlp, w_up=w_up[0], w_down=w_down[0], norm_ple=norm_ple, w_ple_gate=w_ple_gate[0],
             w_ple_proj=w_ple_proj[0], norm_final=norm_final)
    m = dict(norm_mix=m_norm_mix, w_in=m_w_in[0], conv_w=m_conv_w[0], a_log=m_a_log, dt_bias=m_dt_bias, dn_norm=m_dn_norm,
             sinks=m_sinks, w_o=m_w_o[0], norm_mlp=m_norm_mlp, w_up=m_w_up[0], w_down=m_w_down[0], norm_ple=m_norm_ple,
             w_ple_gate=m_w_ple_gate[0], w_ple_proj=m_w_ple_proj[0], norm_final=m_norm_final)
    v = dict(norm_mix=v_norm_mix, w_in=v_w_in[0], conv_w=v_conv_w[0], a_log=v_a_log, dt_bias=v_dt_bias, dn_norm=v_dn_norm,
             sinks=v_sinks, w_o=v_w_o[0], norm_mlp=v_norm_mlp, w_up=v_w_up[0], w_down=v_w_down[0], norm_ple=v_norm_ple,
             w_ple_gate=v_w_ple_gate[0], w_ple_proj=v_w_ple_proj[0], norm_final=v_norm_final)
    me = 4 * lax.axis_index("x") + 2 * lax.axis_index("y") + lax.axis_index("c")
    conv_shard = conv_w.shape[2]

    packed_w = _pack_shard(w)
    gathered = _exchange(_bf(packed_w), "gather_weights", gather=True)
    conv_rows = jnp.pad(w["conv_w"].reshape(1, DN_CONV * conv_shard), ((0, 7), (0, 1024 - DN_CONV * conv_shard)))
    conv_all = _exchange(conv_rows, "gather_conv", gather=True)[:, 0, :DN_CONV * conv_shard]
    full = _unpack_gathered(gathered)
    full["conv_w"] = jnp.transpose(conv_all.reshape(N_DEV, DN_CONV, conv_shard), (1, 0, 2)).reshape(DN_CONV, N_DEV * conv_shard)
    for n in _ORDER:
        full.setdefault(n, w[n])

    loss, grad_x, g = _local_step(x[0], p[0, 0], loss_target[0], full)

    pieces = _exchange(_pack_full_grads(g), "scatter_grads", gather=False)
    big = _adamw(pieces, packed_w, _pack_shard(m), _pack_shard(v), "adamw_matrices")
    small = _pack_small(g, [loss[:, :1] * jnp.ones((1, 1024), F32), jnp.zeros((2, 1024), F32),
                            g["conv_w"].reshape(6, 1024), jnp.zeros((2, 1024), F32)])
    small_all = _exchange(small, "gather_small", gather=True)
    zeros16 = jnp.zeros((_SMALL_ROWS, 1024), F32)
    summed = _adamw(small_all, zeros16, zeros16, zeros16, "sum_small")[0]
    conv_g = lax.dynamic_slice(summed[_CONV_ROW:_CONV_ROW + 6].reshape(DN_CONV, N_DEV * conv_shard), (0, me * conv_shard),
                               (DN_CONV, conv_shard))
    pad_conv = lambda t: jnp.pad(t.reshape(1, DN_CONV * conv_shard), ((0, 2), (0, 1024 - DN_CONV * conv_shard)))
    small_g = jnp.concatenate([summed[0:5], pad_conv(conv_g)], axis=0)[None]
    pack8 = lambda d: _pack_small(d, [pad_conv(d["conv_w"])])
    sm = _adamw(small_g, pack8(w), pack8(m), pack8(v), "adamw_vectors")

    outs = []
    for big_buf, small_buf in zip(big, sm):
        d = _unpack_shard(big_buf)
        d.update(_unpack_small(small_buf, w))
        d["conv_w"] = small_buf[5, :DN_CONV * conv_shard].reshape(DN_CONV, conv_shard)
        outs.append(d)
    result = [summed[_LOSS_ROW, 0], grad_x[None]]
    for d in outs:
        for n in _ORDER:
            result.append(d[n].reshape(w[n].shape)[None] if n in _MATRICES or n == "conv_w" else d[n].reshape(w[n].shape))
    return tuple(result)
```

```python
import jax
import jax.numpy as jnp
from jax import lax
from jax.experimental import pallas as pl
from jax.experimental.pallas import tpu as pltpu

F32, BF16 = jnp.float32, jnp.bfloat16
EPS = 1e-6
D_MODEL = 1024
N_DEV = 8
ATTN_BLOCK = 128
HEAD_PAIR = 128
DN_HEADS = 4
DN_DIM = 128
DN_CHUNK = 64
DN_CONV = 4
ROPE_THETA = 10000.0
D_IN = 2824
D_IN_PAD = 3072
BLK_Q, BLK_Z = 0, 1
BLK_DN, BLK_K, BLK_V, BLK_G = 8, 20, 21, 22
BLK_KV, BLK_G_PAD = 10, 11
VMEM_LIMIT = 48 * 1024 * 1024
NEG = -1e30
ADAM_LR, ADAM_B1, ADAM_B2, ADAM_EPS, ADAM_WD, ADAM_STEP = 0.001, 0.9, 0.999, 1e-08, 0.01, 10
MESH = pl.DeviceIdType.MESH


def _bf(x):
    return x.astype(BF16)


def _dot(a, b):
    return jnp.dot(a, b, preferred_element_type=F32)


def _dot_nt(a, b):
    return lax.dot_general(a, b, (((1,), (1,)), ((), ())), preferred_element_type=F32)


def _dot_tn(a, b):
    return lax.dot_general(a, b, (((0,), (0,)), ((), ())), preferred_element_type=F32)


def _sigmoid(x):
    return 1.0 / (1.0 + jnp.exp(-x))


def _params(sem):
    return pltpu.CompilerParams(dimension_semantics=sem, vmem_limit_bytes=VMEM_LIMIT)


def _mm_nn(x, w, *, name, out_dtypes, tn, epi=None, extra=(), tm=512, w_row_block=0):
    S, K = x.shape
    N = w.shape[1]
    rb = w_row_block
    tm = min(tm, S)
    n_extra = len(extra)

    def body(x_ref, w_ref, *rest):
        acc = _dot(_bf(x_ref[...]), w_ref[...])
        res = epi(acc, *[r[...] for r in rest[:n_extra]]) if epi else (acc,)
        for o, r in zip(rest[n_extra:], res):
            o[...] = r.astype(o.dtype)

    tile = pl.BlockSpec((tm, tn), lambda i, j: (i, j))
    return pl.pallas_call(
        body, grid=(S // tm, N // tn), name=name,
        in_specs=[pl.BlockSpec((tm, K), lambda i, j: (i, 0)), pl.BlockSpec((K, tn), lambda i, j: (rb, j))] + [tile] * n_extra,
        out_specs=[tile] * len(out_dtypes),
        out_shape=[jax.ShapeDtypeStruct((S, N), dt) for dt in out_dtypes],
        compiler_params=_params(("parallel", "parallel")),
    )(x, w, *extra)


def _mm_nt(dy, w, *, name, out_dtype, tn, epi=None, extra=(), tm=512):
    S, N = dy.shape
    K = w.shape[0]
    tm = min(tm, S)
    n_extra = len(extra)

    def body(dy_ref, w_ref, *rest):
        acc = _dot_nt(_bf(dy_ref[...]), w_ref[...])
        if epi:
            acc = epi(acc, *[r[...] for r in rest[:n_extra]])
        rest[n_extra][...] = acc.astype(out_dtype)

    tile = pl.BlockSpec((tm, tn), lambda i, j: (i, j))
    return pl.pallas_call(
        body, grid=(S // tm, K // tn), name=name,
        in_specs=[pl.BlockSpec((tm, N), lambda i, j: (i, 0)), pl.BlockSpec((tn, N), lambda i, j: (j, 0))] + [tile] * n_extra,
        out_specs=tile,
        out_shape=jax.ShapeDtypeStruct((S, K), out_dtype),
        compiler_params=_params(("parallel", "parallel")),
    )(dy, w, *extra)


def _mm_tn(x, dy, *, name, tm, tn, out_dtype=F32, column_shards=False):
    S, K = x.shape
    N = dy.shape[1]

    def body(x_ref, dy_ref, o_ref):
        o_ref[...] = _dot_tn(_bf(x_ref[...]), _bf(dy_ref[...])).astype(out_dtype)

    if column_shards:
        out_spec = pl.BlockSpec((None, tm, tn), lambda i, j: (j, i, 0))
        out_shape = jax.ShapeDtypeStruct((N // tn, K, tn), out_dtype)
    else:
        out_spec = pl.BlockSpec((tm, tn), lambda i, j: (i, j))
        out_shape = jax.ShapeDtypeStruct((K, N), out_dtype)
    return pl.pallas_call(
        body, grid=(K // tm, N // tn), name=name,
        in_specs=[pl.BlockSpec((S, tm), lambda i, j: (0, i)), pl.BlockSpec((S, tn), lambda i, j: (0, j))],
        out_specs=out_spec, out_shape=out_shape,
        compiler_params=_params(("parallel", "parallel")),
    )(x, dy)


def _rowwise(body, *, tiled, full, out_tiled, out_acc, name, tm=512, smem=()):
    S = tiled[0].shape[0]
    tm = min(tm, S)
    n_in = len(smem) + len(tiled) + len(full)

    def kern(*refs):
        @pl.when(pl.program_id(0) == 0)
        def _():
            for r in refs[n_in + len(out_tiled):]:
                r[...] = jnp.zeros_like(r)
        body(*refs)

    in_specs = [pl.BlockSpec(memory_space=pltpu.SMEM) for _ in smem]
    in_specs += [pl.BlockSpec((tm, a.shape[1]), lambda i: (i, 0)) for a in tiled]
    in_specs += [pl.BlockSpec(a.shape, lambda i, nd=a.ndim: (0,) * nd) for a in full]
    out_specs = [pl.BlockSpec((tm, w), lambda i: (i, 0)) for w, _ in out_tiled]
    out_specs += [pl.BlockSpec(shp, lambda i, nd=len(shp): (0,) * nd) for shp, _ in out_acc]
    out_shape = [jax.ShapeDtypeStruct((S, w), dt) for w, dt in out_tiled]
    out_shape += [jax.ShapeDtypeStruct(shp, dt) for shp, dt in out_acc]
    return pl.pallas_call(
        kern, grid=(S // tm,), name=name, in_specs=in_specs, out_specs=out_specs, out_shape=out_shape,
        compiler_params=_params(("arbitrary",)),
    )(*smem, *tiled, *full)


def _rms_stats(x):
    r = lax.rsqrt(jnp.mean(x * x, axis=-1, keepdims=True) + EPS)
    return r, x * r


def _rmsnorm_fwd(x, g, name):
    def body(x_ref, g_ref, o_ref):
        _, xh = _rms_stats(x_ref[...])
        o_ref[...] = _bf(xh * g_ref[...])

    return _rowwise(body, tiled=[x], full=[g], out_tiled=[(x.shape[1], BF16)], out_acc=[], name=name)[0]


def _rms_bwd_tile(x, g, dxn):
    r, xh = _rms_stats(x)
    dg = jnp.sum(dxn * xh, axis=0, keepdims=True)
    dn = dxn * g
    dx = r * (dn - xh * jnp.mean(dn * xh, axis=-1, keepdims=True))
    return dx, dg


def _rmsnorm_bwd(x, g, dxn, dres, name):
    def body(x_ref, dxn_ref, dres_ref, g_ref, dx_ref, dg_ref):
        dx, dg = _rms_bwd_tile(x_ref[...], g_ref[...], dxn_ref[...])
        dx_ref[...] = dres_ref[...] + dx
        dg_ref[...] += dg

    n = x.shape[1]
    return _rowwise(body, tiled=[x, dxn, dres], full=[g], out_tiled=[(n, F32)], out_acc=[((1, n), F32)], name=name)


def _final_loss(h3, g, target):
    n = h3.shape[1]

    def body(h_ref, t_ref, g_ref, dh_ref, loss_ref, dg_ref):
        x = h_ref[...]
        _, xh = _rms_stats(x)
        e = xh * g_ref[...] - t_ref[...]
        per_tok = jnp.mean(e * e, axis=-1, keepdims=True)
        loss_ref[...] += 0.5 * jnp.sum(per_tok, axis=0, keepdims=True)
        dx, dg = _rms_bwd_tile(x, g_ref[...], e * (1.0 / n))
        dh_ref[...] = dx
        dg_ref[...] += dg

    return _rowwise(body, tiled=[h3, target], full=[g], out_tiled=[(n, F32)],
                    out_acc=[((1, 128), F32), ((1, n), F32)], name="final_loss")


def _ple_bwd(dh3, pp, gate):
    def body(dh_ref, pp_ref, gate_ref, dgl_ref, dpp_ref):
        dh, gt = dh_ref[...], gate_ref[...]
        dgl_ref[...] = _bf(dh * pp_ref[...] * gt * (1.0 - gt))
        dpp_ref[...] = _bf(dh * gt)

    n = dh3.shape[1]
    return _rowwise(body, tiled=[dh3, pp, gate], full=[], out_tiled=[(n, BF16), (n, BF16)], out_acc=[], name="ple_bwd")


def _rope_tables(S):
    half = 32
    inv = 1.0 / (ROPE_THETA ** (jnp.arange(half, dtype=F32) * (2.0 / 64)))
    ang = jnp.arange(S).astype(F32)[:, None] * inv[None, :]
    cos, sin = jnp.cos(ang), jnp.sin(ang)
    return jnp.tile(cos, (1, 4)), jnp.concatenate([-sin, sin, -sin, sin], axis=1)


def _attn_common(i, kc, kp, vc, vp, cc, sc, cp, sp):
    lane = lax.broadcasted_iota(jnp.int32, (1, HEAD_PAIR), 1)
    lane_lo = jnp.bitwise_and(lane, 63) < 32
    slot = [lane < 64, lane >= 64]

    def swap_halves(t):
        return jnp.where(lane_lo, pltpu.roll(t, 96, 1), pltpu.roll(t, 32, 1))

    def rope(t, cos, sin):
        return t * cos + swap_halves(t) * sin

    def unrope(d, cos, sin):
        return d * cos + swap_halves(d * sin)

    k2 = jnp.concatenate([rope(kp, cp, sp), rope(kc, cc, sc)], axis=0)
    v2 = jnp.concatenate([vp, vc], axis=0)
    r = lax.broadcasted_iota(jnp.int32, (ATTN_BLOCK, 2 * ATTN_BLOCK), 0)
    c = lax.broadcasted_iota(jnp.int32, (ATTN_BLOCK, 2 * ATTN_BLOCK), 1)
    valid = (c > r) & (c <= r + ATTN_BLOCK) & jnp.logical_or(c >= ATTN_BLOCK, i > 0)
    ks, vs = {}, {}
    for j in range(2):
        kn = jnp.where(slot[j], k2, 0.0)
        vn = jnp.where(slot[j], v2, 0.0)
        for s in range(2):
            ks[j, s] = _bf(kn if s == j else pltpu.roll(kn, 64, 1))
            vs[j, s] = _bf(vn if s == j else pltpu.roll(vn, 64, 1))
    return slot, rope, unrope, valid, ks, vs


def _attn_probs(qp, k, valid, sink):
    s = jnp.where(valid, _dot_nt(qp, k) * 0.125, NEG)
    m = jnp.maximum(jnp.max(s, axis=1, keepdims=True), sink)
    e = jnp.exp(s - m)
    inv_z = 1.0 / (jnp.sum(e, axis=1, keepdims=True) + jnp.exp(sink - m))
    return e * inv_z, jnp.exp(sink - m) * inv_z


def _attn_specs(S):
    nb = S // ATTN_BLOCK
    prev = lambda i: jnp.maximum(i - 1, 0)
    blk = lambda w, col, row=(lambda i: i): pl.BlockSpec((ATTN_BLOCK, w), lambda i: (row(i), col))
    in_specs = [pl.BlockSpec(memory_space=pltpu.SMEM),
                blk(512, BLK_Q), blk(128, BLK_K), blk(128, BLK_K, prev), blk(128, BLK_V), blk(128, BLK_V, prev),
                blk(128, 0), blk(128, 0), blk(128, 0, prev), blk(128, 0, prev)]
    return nb, in_specs


def _attn_fwd(pa, cos, sin, sinks):
    S = pa.shape[0]
    nb, in_specs = _attn_specs(S)

    def body(sinks_ref, q_ref, kc_ref, kp_ref, vc_ref, vp_ref, cc_ref, sc_ref, cp_ref, sp_ref, o_ref):
        i = pl.program_id(0)
        cc, sc = cc_ref[...], sc_ref[...]
        _, rope, _, valid, ks, vs = _attn_common(i, kc_ref[...], kp_ref[...], vc_ref[...], vp_ref[...],
                                                 cc, sc, cp_ref[...], sp_ref[...])
        for pair in range(4):
            cols = slice(HEAD_PAIR * pair, HEAD_PAIR * (pair + 1))
            qp = _bf(rope(q_ref[:, cols], cc, sc))
            acc = jnp.zeros((ATTN_BLOCK, HEAD_PAIR), F32)
            for s in range(2):
                h = 2 * pair + s
                j = h // 4
                p, _ = _attn_probs(qp, ks[j, s], valid, sinks_ref[h])
                acc = acc + _dot(_bf(p), vs[j, s])
            o_ref[:, cols] = acc

    return pl.pallas_call(
        body, grid=(nb,), name="attn_fwd", in_specs=in_specs,
        out_specs=pl.BlockSpec((ATTN_BLOCK, 512), lambda i: (i, 0)),
        out_shape=jax.ShapeDtypeStruct((S, 512), F32),
        compiler_params=_params(("parallel",)),
    )(sinks, pa, pa, pa, pa, pa, cos, sin, cos, sin)


def _attn_bwd(pa, cos, sin, sinks, dcat):
    S = pa.shape[0]
    nb, in_specs = _attn_specs(S)
    in_specs = in_specs + [pl.BlockSpec((ATTN_BLOCK, 512), lambda i: (i, 0))]

    def body(sinks_ref, q_ref, kc_ref, kp_ref, vc_ref, vp_ref, cc_ref, sc_ref, cp_ref, sp_ref, do_ref,
             dq_ref, dk_ref, dv_ref, dsink_ref):
        i = pl.program_id(0)

        @pl.when(i == 0)
        def _():
            dk_ref[...] = jnp.zeros_like(dk_ref)
            dv_ref[...] = jnp.zeros_like(dv_ref)
            dsink_ref[...] = jnp.zeros_like(dsink_ref)

        cc, sc, cp, sp = cc_ref[...], sc_ref[...], cp_ref[...], sp_ref[...]
        slot, rope, unrope, valid, ks, vs = _attn_common(i, kc_ref[...], kp_ref[...], vc_ref[...], vp_ref[...], cc, sc, cp, sp)
        dk2 = jnp.zeros((2 * ATTN_BLOCK, HEAD_PAIR), F32)
        dv2 = jnp.zeros((2 * ATTN_BLOCK, HEAD_PAIR), F32)
        for pair in range(4):
            cols = slice(HEAD_PAIR * pair, HEAD_PAIR * (pair + 1))
            qp = _bf(rope(q_ref[:, cols], cc, sc))
            dob = _bf(do_ref[:, cols])
            dq = jnp.zeros((ATTN_BLOCK, HEAD_PAIR), F32)
            for s in range(2):
                h = 2 * pair + s
                j = h // 4
                p, p_sink = _attn_probs(qp, ks[j, s], valid, sinks_ref[h])
                dp = _dot_nt(dob, vs[j, s])
                dr = jnp.sum(p * dp, axis=1, keepdims=True)
                ds = _bf(p * (dp - dr) * 0.125)
                dsink_ref[h:h + 1, :] += -jnp.sum(p_sink * dr, axis=0, keepdims=True)
                dq = dq + _dot(ds, ks[j, s])
                dk_h = jnp.where(slot[s], _dot_tn(ds, qp), 0.0)
                dv_h = jnp.where(slot[s], _dot_tn(_bf(p), dob), 0.0)
                if s != j:
                    dk_h, dv_h = pltpu.roll(dk_h, 64, 1), pltpu.roll(dv_h, 64, 1)
                dk2, dv2 = dk2 + dk_h, dv2 + dv_h
            dq_ref[:, cols] = _bf(unrope(dq, cc, sc))
        cur = pl.ds(pl.multiple_of(i * ATTN_BLOCK, ATTN_BLOCK), ATTN_BLOCK)
        dk_ref[cur, :] += unrope(dk2[ATTN_BLOCK:], cc, sc)
        dv_ref[cur, :] += dv2[ATTN_BLOCK:]

        @pl.when(i > 0)
        def _():
            prv = pl.ds(pl.multiple_of((i - 1) * ATTN_BLOCK, ATTN_BLOCK), ATTN_BLOCK)
            dk_ref[prv, :] += unrope(dk2[:ATTN_BLOCK], cp, sp)
            dv_ref[prv, :] += dv2[:ATTN_BLOCK]

    whole = lambda w: pl.BlockSpec((S, w), lambda i: (0, 0))
    return pl.pallas_call(
        body, grid=(nb,), name="attn_bwd", in_specs=in_specs,
        out_specs=[pl.BlockSpec((ATTN_BLOCK, 512), lambda i: (i, BLK_Q)), whole(128), whole(128),
                   pl.BlockSpec((8, 128), lambda i: (0, 0))],
        out_shape=[jax.ShapeDtypeStruct((S, D_IN_PAD), BF16), jax.ShapeDtypeStruct((S, 128), F32),
                   jax.ShapeDtypeStruct((S, 128), F32), jax.ShapeDtypeStruct((8, 128), F32)],
        compiler_params=_params(("arbitrary",)),
    )(sinks, pa, pa, pa, pa, pa, cos, sin, cos, sin, dcat)


CONV_ROWS = 512
CONV_PAD = 8


def _conv_silu(scr, w, r0):
    y = w[3:4, :] * scr[pl.ds(CONV_PAD + r0, CONV_ROWS), :]
    for j in range(DN_CONV - 1):
        y = y + w[j:j + 1, :] * scr[pl.ds(CONV_PAD + r0 - 3 + j, CONV_ROWS), :]
    return y


def _dn_prep_fwd(pd, conv_w):
    S = pd.shape[0]
    assert S % CONV_ROWS == 0

    def body(x_ref, w_ref, o_ref, scr):
        b = pl.program_id(0)
        scr[0:CONV_PAD, :] = jnp.zeros((CONV_PAD, DN_DIM), F32)
        scr[pl.ds(CONV_PAD, S), :] = x_ref[...]
        w = w_ref[...]
        q_scale = jnp.where(b < DN_HEADS, DN_DIM ** -0.5, 1.0)
        for r0 in range(0, S, CONV_ROWS):
            y = _conv_silu(scr, w, r0)
            a = y * _sigmoid(y)
            rs = lax.rsqrt(jnp.sum(a * a, axis=1, keepdims=True) + EPS)
            o_ref[pl.ds(r0, CONV_ROWS), :] = a * jnp.where(b < 2 * DN_HEADS, rs * q_scale, 1.0)

    col = pl.BlockSpec((S, DN_DIM), lambda b: (0, b))
    return pl.pallas_call(
        body, grid=(3 * DN_HEADS,), name="dn_prep_fwd",
        in_specs=[pl.BlockSpec((S, DN_DIM), lambda b: (0, BLK_DN + b)), pl.BlockSpec((DN_CONV, DN_DIM), lambda b: (0, b))],
        out_specs=col,
        out_shape=jax.ShapeDtypeStruct((S, 3 * DN_HEADS * DN_DIM), F32),
        scratch_shapes=[pltpu.VMEM((S + CONV_PAD, DN_DIM), F32)],
        compiler_params=_params(("parallel",)),
    )(pd, conv_w)


def _dn_prep_bwd(pd, conv_w, dqkv, dproj):
    S = pd.shape[0]

    def body(x_ref, w_ref, d_ref, _, dx_ref, dw_ref, scr, dscr):
        b = pl.program_id(0)
        scr[0:CONV_PAD, :] = jnp.zeros((CONV_PAD, DN_DIM), F32)
        scr[pl.ds(CONV_PAD, S), :] = x_ref[...]
        dscr[pl.ds(S, CONV_PAD), :] = jnp.zeros((CONV_PAD, DN_DIM), F32)
        w = w_ref[...]
        q_scale = jnp.where(b < DN_HEADS, DN_DIM ** -0.5, 1.0)
        is_qk = b < 2 * DN_HEADS
        dw = [jnp.zeros((1, DN_DIM), F32) for _ in range(DN_CONV)]
        for r0 in range(0, S, CONV_ROWS):
            y = _conv_silu(scr, w, r0)
            sg = _sigmoid(y)
            a = y * sg
            dout = d_ref[pl.ds(r0, CONV_ROWS), :]
            rs = lax.rsqrt(jnp.sum(a * a, axis=1, keepdims=True) + EPS)
            da_qk = q_scale * rs * (dout - a * (rs * rs) * jnp.sum(dout * a, axis=1, keepdims=True))
            dy = jnp.where(is_qk, da_qk, dout) * (sg * (1.0 + y * (1.0 - sg)))
            dscr[pl.ds(r0, CONV_ROWS), :] = dy
            for j in range(DN_CONV):
                dw[j] = dw[j] + jnp.sum(dy * scr[pl.ds(CONV_PAD + r0 - 3 + j, CONV_ROWS), :], axis=0, keepdims=True)
        for j in range(DN_CONV):
            dw_ref[j:j + 1, :] = dw[j]
        for r0 in range(0, S, CONV_ROWS):
            dx = w[3:4, :] * dscr[pl.ds(r0, CONV_ROWS), :]
            for j in range(DN_CONV - 1):
                dx = dx + w[j:j + 1, :] * dscr[pl.ds(r0 + 3 - j, CONV_ROWS), :]
            dx_ref[pl.ds(r0, CONV_ROWS), :] = _bf(dx)

    col = pl.BlockSpec((S, DN_DIM), lambda b: (0, b))
    proj_col = pl.BlockSpec((S, DN_DIM), lambda b: (0, BLK_DN + b))
    wcol = pl.BlockSpec((DN_CONV, DN_DIM), lambda b: (0, b))
    return pl.pallas_call(
        body, grid=(3 * DN_HEADS,), name="dn_prep_bwd",
        in_specs=[proj_col, wcol, col, pl.BlockSpec(memory_space=pl.ANY)], out_specs=[proj_col, wcol],
        out_shape=[jax.ShapeDtypeStruct(dproj.shape, dproj.dtype), jax.ShapeDtypeStruct((DN_CONV, 3 * DN_HEADS * DN_DIM), F32)],
        scratch_shapes=[pltpu.VMEM((S + CONV_PAD, DN_DIM), F32), pltpu.VMEM((S + CONV_PAD, DN_DIM), F32)],
        input_output_aliases={3: 0},
        compiler_params=_params(("parallel",)),
    )(pd, conv_w, dqkv, dproj)


CPAD = 128


def _chunk_masks():
    ii = lax.broadcasted_iota(jnp.int32, (DN_CHUNK, CPAD), 0)
    jj = lax.broadcasted_iota(jnp.int32, (DN_CHUNK, CPAD), 1)
    return ii, jj


def _rows_pad(a):
    return jnp.concatenate([a, jnp.zeros_like(a)], axis=0)


def _split(a):
    hi = _bf(a).astype(F32)
    return hi, a - hi


def _lanes_hi_lo(a):
    hi, lo = _split(a)
    return _bf(hi + pltpu.roll(lo, DN_CHUNK, 1))


def _rows_hi_lo(a):
    hi, lo = _split(a)
    return _bf(jnp.concatenate([hi, lo], axis=0))


def _three_terms(r, keep):
    top = r[:DN_CHUNK]
    return jnp.where(keep, top + pltpu.roll(top, DN_CHUNK, 1) + r[DN_CHUNK:], 0.0)


def _dot3_nt(a, b, keep):
    return _three_terms(_dot_nt(_rows_hi_lo(a), _rows_hi_lo(b)), keep)


def _dot3_tn(a, b, keep):
    return _three_terms(_dot_tn(_lanes_hi_lo(a), _lanes_hi_lo(b)), keep)


def _col_to_row(col, ii, jj):
    return jnp.sum(jnp.where(ii == jj, col, 0.0), axis=0, keepdims=True)


def _row_to_col(row, ii, jj):
    return jnp.sum(jnp.where(ii == jj, row, 0.0), axis=1, keepdims=True)


def _decay(gc_col, ii, jj):
    diff = gc_col - _col_to_row(gc_col, ii, jj)
    return jnp.where(jj <= ii, jnp.exp(jnp.where(jj <= ii, diff, 0.0)), 0.0)


def _softplus(x):
    return jnp.maximum(x, 0.0) + jnp.log(1.0 + jnp.exp(-jnp.abs(x)))


def _head(h):
    return slice(DN_DIM * h, DN_DIM * (h + 1))


def _dn_chunk_fwd(qkv, pg, a_log, dt_bias):
    S = qkv.shape[0]
    C = DN_CHUNK
    nc = S // C

    def body(alog_ref, dtb_ref, qkv_ref, pg_ref, w_ref, u_ref, qg_ref, kd_ref, a_ref, t_ref, gcs_ref):
        ii, jj = _chunk_masks()
        lane = lax.broadcasted_iota(jnp.int32, (1, 128), 1)
        keep = lane < C
        eye = (ii == jj).astype(F32)
        gcs = jnp.zeros((C, 128), F32)
        for h in range(DN_HEADS):
            q, k, v = qkv_ref[:, _head(h)], qkv_ref[:, _head(DN_HEADS + h)], qkv_ref[:, _head(2 * DN_HEADS + h)]
            beta = _sigmoid(pg_ref[:, h:h + 1])
            g_col = -jnp.exp(alog_ref[h]) * _softplus(pg_ref[:, DN_HEADS + h:DN_HEADS + h + 1] + dtb_ref[h])
            g_row = _col_to_row(g_col, ii, jj)
            gc_col = jnp.sum(jnp.where(jj <= ii, g_row, 0.0), axis=1, keepdims=True)
            dec = _decay(gc_col, ii, jj)
            eg = jnp.exp(gc_col)
            kb, vb = k * beta, v * beta
            k_rows = _rows_pad(_bf(k))
            lmat = jnp.where(jj < ii, _dot_nt(_bf(kb), k_rows) * dec, 0.0)
            pw = -lmat
            t = eye + pw
            for step in range(5):
                pw = _three_terms(_dot(_rows_hi_lo(pw), _rows_pad(_lanes_hi_lo(pw))), keep) if step == 0 else nxt
                r = _dot(jnp.concatenate([_rows_hi_lo(t), _rows_hi_lo(pw)], axis=0), _rows_pad(_lanes_hi_lo(pw)))
                t = t + _three_terms(r[:2 * C], keep)
                nxt = _three_terms(r[2 * C:], keep)
            tb = _bf(t)
            u_ref[:, _head(h)] = _dot(tb, _rows_pad(_bf(vb)))
            w_ref[:, _head(h)] = _dot(tb, _rows_pad(_bf(kb * eg)))
            a_ref[h] = _dot_nt(_bf(q), k_rows) * dec
            t_ref[h] = t
            qg_ref[:, _head(h)] = q * eg
            kd_ref[:, _head(h)] = k * jnp.exp(gc_col[C - 1:C, :] - gc_col)
            gcs = gcs + jnp.where(lane == h, gc_col, 0.0) + jnp.where(lane == DN_HEADS + h, beta, 0.0) \
                + jnp.where(lane == 2 * DN_HEADS + h, g_col, 0.0)
        gcs_ref[...] = gcs

    smem = pl.BlockSpec(memory_space=pltpu.SMEM)
    wide = pl.BlockSpec((C, 512), lambda n: (n, 0))
    sq = pl.BlockSpec((DN_HEADS, C, CPAD), lambda n: (0, n, 0))
    narrow = pl.BlockSpec((C, 128), lambda n: (n, 0))
    f = lambda *shp: jax.ShapeDtypeStruct(shp, F32)
    return pl.pallas_call(
        body, grid=(nc,), name="dn_chunk_fwd",
        in_specs=[smem, smem, pl.BlockSpec((C, 1536), lambda n: (n, 0)), pl.BlockSpec((C, 128), lambda n: (n, BLK_G))],
        out_specs=[wide, wide, wide, wide, sq, sq, narrow],
        out_shape=[f(S, 512), f(S, 512), f(S, 512), f(S, 512), f(DN_HEADS, S, CPAD), f(DN_HEADS, S, CPAD), f(S, 128)],
        compiler_params=_params(("parallel",)),
    )(a_log, dt_bias, qkv, pg)


def _gated_norm(o, z, gn):
    r, oh = _rms_stats(o)
    return oh * gn * (z * _sigmoid(z))


def _dn_scan_fwd(w, u, qg, kd, a, gcs, pz, gn):
    S = w.shape[0]
    C = DN_CHUNK
    nc = S // C

    def body(w_ref, u_ref, qg_ref, kd_ref, a_ref, gcs_ref, z_ref, gn_ref, o_ref, vn_ref, sst_ref, out_ref, state):
        @pl.when(pl.program_id(0) == 0)
        def _():
            state[...] = jnp.zeros_like(state)

        for h in range(DN_HEADS):
            hs = _head(h)
            s_in = state[h]
            sst_ref[0, h] = s_in
            sb = _bf(s_in)
            vn = u_ref[:, hs] - _dot(_bf(w_ref[:, hs]), sb)
            vnb = _bf(vn)
            o = _dot(_bf(qg_ref[:, hs]), sb) + _dot(_bf(a_ref[h]), _rows_pad(vnb))
            d_last = jnp.exp(gcs_ref[C - 1:C, h:h + 1])
            state[h] = s_in * d_last + _dot_tn(_bf(kd_ref[:, hs]), vnb)
            o_ref[:, hs] = o
            vn_ref[:, hs] = vn
            out_ref[:, hs] = _gated_norm(o, z_ref[:, hs], gn_ref[...])

    wide = pl.BlockSpec((C, 512), lambda n: (n, 0))
    f = lambda *shp: jax.ShapeDtypeStruct(shp, F32)
    return pl.pallas_call(
        body, grid=(nc,), name="dn_scan_fwd",
        in_specs=[wide, wide, wide, wide, pl.BlockSpec((DN_HEADS, C, CPAD), lambda n: (0, n, 0)),
                  pl.BlockSpec((C, 128), lambda n: (n, 0)), pl.BlockSpec((C, 512), lambda n: (n, BLK_Z)),
                  pl.BlockSpec((1, DN_DIM), lambda n: (0, 0))],
        out_specs=[wide, wide, pl.BlockSpec((1, DN_HEADS, DN_DIM, DN_DIM), lambda n: (n, 0, 0, 0)), wide],
        out_shape=[f(S, 512), f(S, 512), f(nc, DN_HEADS, DN_DIM, DN_DIM), f(S, 512)],
        scratch_shapes=[pltpu.VMEM((DN_HEADS, DN_DIM, DN_DIM), F32)],
        compiler_params=_params(("arbitrary",)),
    )(w, u, qg, kd, a, gcs, pz, gn)


def _dn_scan_bwd(dcat, o, pz, gn, sst, vnew, w, qg, kd, a, gcs, dproj):
    S = o.shape[0]
    C = DN_CHUNK
    nc = S // C

    def body(dy_ref, o_ref, z_ref, gn_ref, sst_ref, vn_ref, w_ref, qg_ref, kd_ref, a_ref, gcs_ref, _,
             du_ref, dw_ref, dqg_ref, dkd_ref, da_ref, dz_ref, dsc_ref, dgn_ref, dstate):
        @pl.when(pl.program_id(0) == 0)
        def _():
            dstate[...] = jnp.zeros_like(dstate)
            dgn_ref[...] = jnp.zeros_like(dgn_ref)

        gn_ = gn_ref[...]
        lane = lax.broadcasted_iota(jnp.int32, (C, 128), 1)
        row = lax.broadcasted_iota(jnp.int32, (C, 128), 0)
        dsc = jnp.zeros((C, 128), F32)
        for h in range(DN_HEADS):
            hs = _head(h)
            ov, z, dout = o_ref[:, hs], z_ref[:, hs], dy_ref[:, hs]
            r, oh = _rms_stats(ov)
            sg = _sigmoid(z)
            don = dout * (z * sg)
            dz_ref[:, hs] = _bf(dout * (oh * gn_) * (sg * (1.0 + z * (1.0 - sg))))
            dgn_ref[...] += jnp.sum(don * oh, axis=0, keepdims=True)
            dn = don * gn_
            do = _bf(r * (dn - oh * jnp.mean(dn * oh, axis=-1, keepdims=True)))
            s_in = sst_ref[0, h]
            sb = _bf(s_in)
            ds_out = dstate[h]
            dsb = _bf(ds_out)
            vnb = _bf(vn_ref[:, hs])
            wb, qgb, kdb, ab = _bf(w_ref[:, hs]), _bf(qg_ref[:, hs]), _bf(kd_ref[:, hs]), _bf(a_ref[h])
            dvn = _dot_tn(ab, do)[:C] + _dot(kdb, dsb)
            dvnb = _bf(dvn)
            da_ref[h] = _dot_nt(do, _rows_pad(vnb))
            dqg_ref[:, hs] = _dot_nt(do, sb)
            dkd_ref[:, hs] = _dot_nt(vnb, dsb)
            dw_ref[:, hs] = -_dot_nt(dvnb, sb)
            du_ref[:, hs] = dvn
            d_last = jnp.exp(gcs_ref[C - 1:C, h:h + 1])
            dd = jnp.sum(jnp.sum(ds_out * s_in, axis=1, keepdims=True), axis=0, keepdims=True)
            dsc = dsc + jnp.where((lane == h) & (row == C - 1), dd * d_last, 0.0)
            dstate[h] = ds_out * d_last + _dot_tn(qgb, do) - _dot_tn(wb, dvnb)
        dsc_ref[...] = dsc

    rev = lambda n: nc - 1 - n
    wide = pl.BlockSpec((C, 512), lambda n: (rev(n), 0))
    z_spec = pl.BlockSpec((C, 512), lambda n: (rev(n), BLK_Z))
    sq = pl.BlockSpec((DN_HEADS, C, CPAD), lambda n: (0, rev(n), 0))
    narrow = pl.BlockSpec((C, 128), lambda n: (rev(n), 0))
    gn_spec = pl.BlockSpec((1, DN_DIM), lambda n: (0, 0))
    f = lambda *shp: jax.ShapeDtypeStruct(shp, F32)
    return pl.pallas_call(
        body, grid=(nc,), name="dn_scan_bwd",
        in_specs=[pl.BlockSpec((C, 512), lambda n: (rev(n), 1)), wide, z_spec, gn_spec,
                  pl.BlockSpec((1, DN_HEADS, DN_DIM, DN_DIM), lambda n: (rev(n), 0, 0, 0)),
                  wide, wide, wide, wide, sq, narrow, pl.BlockSpec(memory_space=pl.ANY)],
        out_specs=[wide, wide, wide, wide, sq, z_spec, narrow, gn_spec],
        out_shape=[f(S, 512), f(S, 512), f(S, 512), f(S, 512), f(DN_HEADS, S, CPAD),
                   jax.ShapeDtypeStruct(dproj.shape, dproj.dtype), f(S, 128), f(1, DN_DIM)],
        scratch_shapes=[pltpu.VMEM((DN_HEADS, DN_DIM, DN_DIM), F32)],
        input_output_aliases={11: 5},
        compiler_params=_params(("arbitrary",)),
    )(dcat, o, pz, gn, sst, vnew, w, qg, kd, a, gcs, dproj)


def _dn_chunk_bwd(qkv, pg, t_inv, gcs, du, dw, dqg, dkd, da, dsc, a_log, dt_bias, dproj):
    S = qkv.shape[0]
    C = DN_CHUNK
    nc = S // C

    def body(alog_ref, dtb_ref, qkv_ref, pg_ref, t_ref, gcs_ref, du_ref, dw_ref, dqg_ref, dkd_ref, da_ref, dsc_ref, _,
             dqkv_ref, dpg_ref, acc_ref):
        @pl.when(pl.program_id(0) == 0)
        def _():
            acc_ref[...] = jnp.zeros_like(acc_ref)

        ii, jj = _chunk_masks()
        lane = lax.broadcasted_iota(jnp.int32, (1, 128), 1)
        row8 = lax.broadcasted_iota(jnp.int32, (8, 128), 0)
        lane8 = lax.broadcasted_iota(jnp.int32, (8, 128), 1)
        rowc = lax.broadcasted_iota(jnp.int32, (C, 1), 0)
        tril, strict = jj <= ii, jj < ii
        keep = lane < C
        dpg = jnp.zeros((C, 128), F32)
        acc = jnp.zeros((8, 128), F32)
        for h in range(DN_HEADS):
            q, k, v = qkv_ref[:, _head(h)], qkv_ref[:, _head(DN_HEADS + h)], qkv_ref[:, _head(2 * DN_HEADS + h)]
            gc_col, beta, g_col = gcs_ref[:, h:h + 1], gcs_ref[:, DN_HEADS + h:DN_HEADS + h + 1], \
                gcs_ref[:, 2 * DN_HEADS + h:2 * DN_HEADS + h + 1]
            dec = _decay(gc_col, ii, jj)
            eg = jnp.exp(gc_col)
            g_last = gc_col[C - 1:C, :]
            ek = jnp.exp(g_last - gc_col)
            kb, vb = k * beta, v * beta
            kbg = kb * eg
            qb, kbb = _bf(q), _bf(kb)
            k_rows = _rows_pad(_bf(k))
            t = t_ref[h]
            tb = _bf(t)
            dub, dwb = _bf(du_ref[:, _head(h)]), _bf(dw_ref[:, _head(h)])
            dqg_, dkd_ = dqg_ref[:, _head(h)], dkd_ref[:, _head(h)]
            dt = _dot_nt(dub, _rows_pad(_bf(vb))) + _dot_nt(dwb, _rows_pad(_bf(kbg)))
            dvb = _dot_tn(tb, dub)[:C]
            dkbg = _dot_tn(tb, dwb)[:C]
            dl = -_dot3_tn(t, _dot3_nt(dt, t, keep), keep)
            kk = _dot_nt(kbb, k_rows)
            qk = _dot_nt(qb, k_rows)
            dm = jnp.where(strict, dl * dec, 0.0)
            dqk = jnp.where(tril, da_ref[h] * dec, 0.0)
            gmat = dm * kk + dqk * qk
            dgc = jnp.sum(gmat, axis=1, keepdims=True) - _row_to_col(jnp.sum(gmat, axis=0, keepdims=True), ii, jj)
            dmb, dqkb = _bf(dm), _bf(dqk)
            dkb = _dot(dmb, k_rows) + dkbg * eg
            dk = _dot_tn(dmb, kbb)[:C] + _dot_tn(dqkb, qb)[:C] + dkd_ * ek
            dq = _dot(dqkb, k_rows) + dqg_ * eg
            tk = jnp.sum(dkd_ * k * ek, axis=1, keepdims=True)
            dgc = dgc + jnp.sum(dqg_ * q * eg, axis=1, keepdims=True) - tk + jnp.sum(dkbg * kbg, axis=1, keepdims=True)
            dgl = jnp.sum(tk, axis=0, keepdims=True) + dsc_ref[C - 1:C, h:h + 1]
            dgc = dgc + jnp.where(rowc == C - 1, dgl, 0.0)
            dk = dk + dkb * beta
            dbeta = jnp.sum(dkb * k, axis=1, keepdims=True) + jnp.sum(dvb * v, axis=1, keepdims=True)
            dqkv_ref[:, _head(h)] = dq
            dqkv_ref[:, _head(DN_HEADS + h)] = dk
            dqkv_ref[:, _head(2 * DN_HEADS + h)] = dvb * beta
            dg_col = jnp.sum(jnp.where(jj >= ii, _col_to_row(dgc, ii, jj), 0.0), axis=1, keepdims=True)
            db = dbeta * beta * (1.0 - beta)
            da_in = dg_col * (-jnp.exp(alog_ref[h])) * _sigmoid(pg_ref[:, DN_HEADS + h:DN_HEADS + h + 1] + dtb_ref[h])
            dpg = dpg + jnp.where(lane == h, db, 0.0) + jnp.where(lane == DN_HEADS + h, da_in, 0.0)
            acc = acc + jnp.where((row8 == 0) & (lane8 == h), jnp.sum(dg_col * g_col, axis=0, keepdims=True), 0.0) \
                + jnp.where((row8 == 1) & (lane8 == h), jnp.sum(da_in, axis=0, keepdims=True), 0.0)
        dpg_ref[...] = _bf(jnp.concatenate([dpg, jnp.zeros_like(dpg)], axis=1))
        acc_ref[...] += acc

    smem = pl.BlockSpec(memory_space=pltpu.SMEM)
    wide = pl.BlockSpec((C, 512), lambda n: (n, 0))
    sq = pl.BlockSpec((DN_HEADS, C, CPAD), lambda n: (0, n, 0))
    narrow = pl.BlockSpec((C, 128), lambda n: (n, 0))
    qkv_spec = pl.BlockSpec((C, 1536), lambda n: (n, 0))
    f = lambda *shp: jax.ShapeDtypeStruct(shp, F32)
    return pl.pallas_call(
        body, grid=(nc,), name="dn_chunk_bwd",
        in_specs=[smem, smem, qkv_spec, pl.BlockSpec((C, 128), lambda n: (n, BLK_G)), sq, narrow, wide, wide, wide, wide, sq,
                  narrow, pl.BlockSpec(memory_space=pl.ANY)],
        out_specs=[qkv_spec, pl.BlockSpec((C, 256), lambda n: (n, BLK_G_PAD)), pl.BlockSpec((8, 128), lambda n: (0, 0))],
        out_shape=[f(S, 1536), jax.ShapeDtypeStruct(dproj.shape, dproj.dtype), f(8, 128)],
        input_output_aliases={12: 1},
        compiler_params=_params(("arbitrary",)),
    )(a_log, dt_bias, qkv, pg, t_inv, gcs, du, dw, dqg, dkd, da, dsc, dproj)


def _fill_kv(dk, dv, dproj):
    S = dk.shape[0]
    tm = min(512, S)

    def body(dk_ref, dv_ref, _, o_ref):
        o_ref[...] = _bf(jnp.concatenate([dk_ref[...], dv_ref[...]], axis=1))

    tile = pl.BlockSpec((tm, 128), lambda i: (i, 0))
    return pl.pallas_call(
        body, grid=(S // tm,), name="fill_kv",
        in_specs=[tile, tile, pl.BlockSpec(memory_space=pl.ANY)],
        out_specs=pl.BlockSpec((tm, 256), lambda i: (i, BLK_KV)),
        out_shape=jax.ShapeDtypeStruct(dproj.shape, dproj.dtype),
        input_output_aliases={2: 0},
        compiler_params=_params(("parallel",)),
    )(dk, dv, dproj)


def _w_in_to_internal(w):
    return jnp.concatenate([w[:, 0:512], w[:, 2304:2816], w[:, 768:2304], w[:, 512:768], w[:, 2816:2824],
                            jnp.zeros((w.shape[0], D_IN_PAD - D_IN), w.dtype)], axis=1)


def _w_in_from_internal(g):
    return jnp.concatenate([g[:, 0:512], g[:, 2560:2816], g[:, 1024:2560], g[:, 512:1024], g[:, 2816:2824]], axis=1)


def _local_step(x, p, target, wts):
    S = x.shape[0]
    w_in, w_o, w_up, w_down, w_pg, w_pp = (wts[k] for k in ("w_in", "w_o", "w_up", "w_down", "w_ple_gate", "w_ple_proj"))
    cos, sin = _rope_tables(S)
    sinks, a_log, dt_bias = wts["sinks"].reshape(8), wts["a_log"].reshape(4), wts["dt_bias"].reshape(4)
    gn = wts["dn_norm"].reshape(1, DN_DIM)
    add = lambda acc, res: (acc + res,)

    u = _rmsnorm_fwd(x, wts["norm_mix"], "norm_mix_fwd")
    proj, = _mm_nn(u, w_in, name="in_proj", out_dtypes=[F32], tn=512)
    attn = _attn_fwd(proj, cos, sin, sinks)
    qkv = _dn_prep_fwd(proj, wts["conv_w"])
    cw, cu, cqg, ckd, ca, ct, gcs = _dn_chunk_fwd(qkv, proj, a_log, dt_bias)
    o, vnew, sst, dn_out = _dn_scan_fwd(cw, cu, cqg, ckd, ca, gcs, proj, gn)
    h1a, = _mm_nn(attn, w_o, name="out_proj_attn", out_dtypes=[F32], tn=512, epi=add, extra=[x], w_row_block=0)
    h1, = _mm_nn(dn_out, w_o, name="out_proj_dn", out_dtypes=[F32], tn=512, epi=add, extra=[h1a], w_row_block=1)
    m = _rmsnorm_fwd(h1, wts["norm_mlp"], "norm_mlp_fwd")

    def relu2(acc):
        r = jnp.maximum(acc, 0.0)
        return r * r, r

    hid, relu = _mm_nn(m, w_up, name="mlp_up", out_dtypes=[BF16, BF16], tn=512, epi=relu2)
    h2, = _mm_nn(hid, w_down, name="mlp_down", out_dtypes=[F32], tn=512, epi=add, extra=[h1])
    n3 = _rmsnorm_fwd(h2, wts["norm_ple"], "norm_ple_fwd")
    pp, = _mm_nn(p, w_pp, name="ple_proj", out_dtypes=[F32], tn=512)

    def ple(acc, h2_t, pp_t):
        gate = _sigmoid(acc)
        return h2_t + gate * pp_t, gate

    h3, gate = _mm_nn(n3, w_pg, name="ple_gate", out_dtypes=[F32, F32], tn=512, epi=ple, extra=[h2, pp])
    dh3, loss, d_norm_final = _final_loss(h3, wts["norm_final"].reshape(1, D_MODEL), target)

    g = {"norm_final": d_norm_final}
    dgl, dpp = _ple_bwd(dh3, pp, gate)
    g["w_ple_gate"] = _mm_tn(n3, dgl, name="d_w_ple_gate", tm=512, tn=512, out_dtype=BF16)
    g["w_ple_proj"] = _mm_tn(p, dpp, name="d_w_ple_proj", tm=256, tn=128, out_dtype=BF16, column_shards=True)
    dn3 = _mm_nt(dgl, w_pg, name="d_n3", out_dtype=F32, tn=512)
    dh2, g["norm_ple"] = _rmsnorm_bwd(h2, wts["norm_ple"], dn3, dh3, "norm_ple_bwd")
    d_act = _mm_nt(dh2, w_down, name="d_hidden", out_dtype=BF16, tn=512, epi=lambda acc, r: acc * (2.0 * r.astype(F32)), extra=[relu])
    g["w_down"] = _mm_tn(hid, dh2, name="d_w_down", tm=512, tn=512, out_dtype=BF16)
    g["w_up"] = _mm_tn(m, d_act, name="d_w_up", tm=512, tn=512, out_dtype=BF16, column_shards=True)
    dm = _mm_nt(d_act, w_up, name="d_m", out_dtype=F32, tn=512)
    dh1, g["norm_mlp"] = _rmsnorm_bwd(h1, wts["norm_mlp"], dm, dh2, "norm_mlp_bwd")
    dcat = _mm_nt(dh1, w_o, name="d_cat", out_dtype=F32, tn=512)
    g["w_o"] = jnp.concatenate([_mm_tn(attn, dh1, name="d_w_o_attn", tm=512, tn=512, out_dtype=BF16),
                                _mm_tn(dn_out, dh1, name="d_w_o_dn", tm=512, tn=512, out_dtype=BF16)], axis=0)
    dproj, dk, dv, dsinks = _attn_bwd(proj, cos, sin, sinks, dcat)
    g["sinks"] = dsinks[:, 0].reshape(1, 8)
    du_, dw_, dqg, dkd, da, dproj, dsc, g["dn_norm"] = _dn_scan_bwd(dcat, o, proj, gn, sst, vnew, cw, cqg, ckd, ca, gcs, dproj)
    dqkv, dproj, gate_acc = _dn_chunk_bwd(qkv, proj, ct, gcs, du_, dw_, dqg, dkd, da, dsc, a_log, dt_bias, dproj)
    g["a_log"], g["dt_bias"] = gate_acc[0:1, 0:4], gate_acc[1:2, 0:4]
    dproj, g["conv_w"] = _dn_prep_bwd(proj, wts["conv_w"], dqkv, dproj)
    dproj = _fill_kv(dk, dv, dproj)
    g["w_in"] = _mm_tn(u, dproj, name="d_w_in", tm=512, tn=512)
    du_in = _mm_nt(dproj, w_in, name="d_u", out_dtype=F32, tn=512)
    grad_x, g["norm_mix"] = _rmsnorm_bwd(x, wts["norm_mix"], du_in, dh1, "norm_mix_bwd")
    return loss, grad_x, g


def _peer(k):
    x, y, c = lax.axis_index("x"), lax.axis_index("y"), lax.axis_index("c")
    px = 1 - x if k & 4 else x
    py = 1 - y if k & 2 else y
    pc = 1 - c if k & 1 else c
    return (px, py, pc), 4 * px + 2 * py + pc


def _exchange(srcs, name, gather):
    n = len(srcs)
    shapes = [(N_DEV,) + s.shape if gather else s.shape for s in srcs]

    def body(*refs):
        src_refs, out_refs = refs[:n], refs[n:2 * n]
        send_sems, recv_sems, local_sems = refs[2 * n:]
        _, me = _peer(0)
        piece = (lambda a, d: src_refs[a]) if gather else (lambda a, d: src_refs[a].at[d])
        local = [pltpu.make_async_copy(piece(a, me), out_refs[a].at[me], local_sems.at[a]) for a in range(n)]
        for cp in local:
            cp.start()
        copies = []
        for a in range(n):
            for k in range(1, N_DEV):
                dev, idx = _peer(k)
                cp = pltpu.make_async_remote_copy(src_ref=piece(a, idx), dst_ref=out_refs[a].at[me],
                                                  send_sem=send_sems.at[a, k - 1], recv_sem=recv_sems.at[a, k - 1],
                                                  device_id=dev, device_id_type=MESH)
                cp.start()
                copies.append(cp)
        for cp in copies:
            cp.wait_recv()
        for cp in copies:
            cp.wait_send()
        for cp in local:
            cp.wait()

    anywhere = pl.BlockSpec(memory_space=pl.ANY)
    return pl.pallas_call(
        body, name=name, in_specs=[anywhere] * n, out_specs=[anywhere] * n,
        out_shape=[jax.ShapeDtypeStruct(shp, s.dtype) for shp, s in zip(shapes, srcs)],
        scratch_shapes=[pltpu.SemaphoreType.DMA((n, N_DEV - 1)), pltpu.SemaphoreType.DMA((n, N_DEV - 1)),
                        pltpu.SemaphoreType.DMA((n,))],
    )(*srcs)


def _adamw(parts, w, m, v, name):
    n, R, W = parts.shape
    tm = 128 if R % 128 == 0 else R

    def body(p_ref, w_ref, m_ref, v_ref, g_ref, d_ref, nm_ref, nv_ref):
        g = p_ref[0].astype(F32)
        for s in range(1, n):
            g = g + p_ref[s].astype(F32)
        nm = ADAM_B1 * m_ref[...] + (1.0 - ADAM_B1) * g
        nv = ADAM_B2 * v_ref[...] + (1.0 - ADAM_B2) * (g * g)
        m_hat = nm / (1.0 - ADAM_B1 ** ADAM_STEP)
        v_hat = nv / (1.0 - ADAM_B2 ** ADAM_STEP)
        g_ref[...] = g
        d_ref[...] = -ADAM_LR * (m_hat / (jnp.sqrt(v_hat) + ADAM_EPS) + ADAM_WD * w_ref[...])
        nm_ref[...] = nm
        nv_ref[...] = nv

    tile = pl.BlockSpec((tm, W), lambda i: (i, 0))
    return pl.pallas_call(
        body, grid=(R // tm,), name=name,
        in_specs=[pl.BlockSpec((n, tm, W), lambda i: (0, i, 0)), tile, tile, tile],
        out_specs=[tile] * 4, out_shape=[jax.ShapeDtypeStruct((R, W), F32)] * 4,
        compiler_params=_params(("parallel",)),
    )(parts, w, m, v)


_MATRICES = ("w_in", "w_o", "w_up", "w_down", "w_ple_gate", "w_ple_proj")


def _full_weights(gathered):
    w_in = jnp.transpose(gathered["w_in"], (1, 0, 2)).reshape(D_MODEL, D_IN)
    return dict(
        w_in=_w_in_to_internal(w_in),
        w_o=gathered["w_o"].reshape(1024, 1024),
        w_up=jnp.transpose(gathered["w_up"], (1, 0, 2)).reshape(1024, 4096),
        w_down=gathered["w_down"].reshape(4096, 1024),
        w_ple_gate=gathered["w_ple_gate"].reshape(1024, 1024),
        w_ple_proj=jnp.transpose(gathered["w_ple_proj"], (1, 0, 2)).reshape(256, 1024))


def _grad_pieces(g):
    w_in = _w_in_from_internal(g["w_in"]).reshape(D_MODEL, N_DEV, D_IN // N_DEV)
    return dict(
        w_in=_bf(jnp.transpose(w_in, (1, 0, 2))), w_o=g["w_o"].reshape(N_DEV, 128, 1024), w_up=g["w_up"],
        w_down=g["w_down"].reshape(N_DEV, 512, 1024), w_ple_gate=g["w_ple_gate"].reshape(N_DEV, 128, 1024),
        w_ple_proj=g["w_ple_proj"])


_SMALL_ROWS = 16
_VEC_ROW = {"norm_mix": 0, "norm_mlp": 1, "norm_ple": 2, "norm_final": 3}
_VEC_LANES = {"a_log": (0, 4), "dt_bias": (4, 8), "sinks": (8, 16), "dn_norm": (128, 256)}
_LOSS_ROW, _CONV_ROW = 5, 8


def _pack_small(vals, extra_rows):
    row4 = jnp.zeros((1024,), F32)
    for n, (a, b) in _VEC_LANES.items():
        row4 = row4.at[a:b].set(vals[n].reshape(b - a))
    rows = [vals[n].reshape(1, 1024) for n in ("norm_mix", "norm_mlp", "norm_ple", "norm_final")] + [row4.reshape(1, 1024)]
    return jnp.concatenate(rows + extra_rows, axis=0)


def _unpack_small(buf, like):
    out = {n: buf[r].reshape(like[n].shape) for n, r in _VEC_ROW.items()}
    for n, (a, b) in _VEC_LANES.items():
        out[n] = buf[4, a:b].reshape(like[n].shape)
    return out


_ORDER = ("norm_mix", "w_in", "conv_w", "a_log", "dt_bias", "dn_norm", "sinks", "w_o", "norm_mlp", "w_up", "w_down",
          "norm_ple", "w_ple_gate", "w_ple_proj", "norm_final")


def kernel(x, p, norm_mix, w_in, conv_w, a_log, dt_bias, dn_norm, sinks, w_o, norm_mlp, w_up, w_down, norm_ple, w_ple_gate, w_ple_proj, norm_final, loss_target, m_norm_mix, m_w_in, m_conv_w, m_a_log, m_dt_bias, m_dn_norm, m_sinks, m_w_o, m_norm_mlp, m_w_up, m_w_down, m_norm_ple, m_w_ple_gate, m_w_ple_proj, m_norm_final, v_norm_mix, v_w_in, v_conv_w, v_a_log, v_dt_bias, v_dn_norm, v_sinks, v_w_o, v_norm_mlp, v_w_up, v_w_down, v_norm_ple, v_w_ple_gate, v_w_ple_proj, v_norm_final):
    w = dict(norm_mix=norm_mix, w_in=w_in[0], conv_w=conv_w[0], a_log=a_log, dt_bias=dt_bias, dn_norm=dn_norm, sinks=sinks,
             w_o=w_o[0], norm_mlp=norm_mlp, w_up=w_up[0], w_down=w_down[0], norm_ple=norm_ple, w_ple_gate=w_ple_gate[0],
             w_ple_proj=w_ple_proj[0], norm_final=norm_final)
    m = dict(norm_mix=m_norm_mix, w_in=m_w_in[0], conv_w=m_conv_w[0], a_log=m_a_log, dt_bias=m_dt_bias, dn_norm=m_dn_norm,
             sinks=m_sinks, w_o=m_w_o[0], norm_mlp=m_norm_mlp, w_up=m_w_up[0], w_down=m_w_down[0], norm_ple=m_norm_ple,
             w_ple_gate=m_w_ple_gate[0], w_ple_proj=m_w_ple_proj[0], norm_final=m_norm_final)
    v = dict(norm_mix=v_norm_mix, w_in=v_w_in[0], conv_w=v_conv_w[0], a_log=v_a_log, dt_bias=v_dt_bias, dn_norm=v_dn_norm,
             sinks=v_sinks, w_o=v_w_o[0], norm_mlp=v_norm_mlp, w_up=v_w_up[0], w_down=v_w_down[0], norm_ple=v_norm_ple,
             w_ple_gate=v_w_ple_gate[0], w_ple_proj=v_w_ple_proj[0], norm_final=v_norm_final)
    me = 4 * lax.axis_index("x") + 2 * lax.axis_index("y") + lax.axis_index("c")
    conv_shard = conv_w.shape[2]

    conv_pad = jnp.pad(w["conv_w"], ((0, 8 - DN_CONV), (0, 256 - conv_shard)))
    *gathered, conv_all = _exchange([_bf(w[n]) for n in _MATRICES] + [conv_pad], "gather_weights", gather=True)
    full = _full_weights(dict(zip(_MATRICES, gathered)))
    conv_all = conv_all[:, :DN_CONV, :conv_shard]
    full["conv_w"] = jnp.transpose(conv_all, (1, 0, 2)).reshape(DN_CONV, N_DEV * conv_shard)
    for n in _ORDER:
        full.setdefault(n, w[n])

    loss, grad_x, g = _local_step(x[0], p[0, 0], loss_target[0], full)

    pieces = _grad_pieces(g)
    received = _exchange([pieces[n] for n in _MATRICES], "scatter_grads", gather=False)
    big = {n: _adamw(r, w[n], m[n], v[n], "adamw_" + n) for n, r in zip(_MATRICES, received)}
    small = _pack_small(g, [loss[:, :1] * jnp.ones((1, 1024), F32), jnp.zeros((2, 1024), F32),
                            g["conv_w"].reshape(6, 1024), jnp.zeros((2, 1024), F32)])
    small_all, = _exchange([small], "gather_small", gather=True)
    zeros16 = jnp.zeros((_SMALL_ROWS, 1024), F32)
    summed = _adamw(small_all, zeros16, zeros16, zeros16, "sum_small")[0]
    conv_g = lax.dynamic_slice(summed[_CONV_ROW:_CONV_ROW + 6].reshape(DN_CONV, N_DEV * conv_shard), (0, me * conv_shard),
                               (DN_CONV, conv_shard))
    pad_conv = lambda t: jnp.pad(t.reshape(1, DN_CONV * conv_shard), ((0, 2), (0, 1024 - DN_CONV * conv_shard)))
    small_g = jnp.concatenate([summed[0:5], pad_conv(conv_g)], axis=0)[None]
    pack8 = lambda d: _pack_small(d, [pad_conv(d["conv_w"])])
    sm = _adamw(small_g, pack8(w), pack8(m), pack8(v), "adamw_vectors")

    outs = []
    for i, small_buf in enumerate(sm):
        d = {n: big[n][i] for n in _MATRICES}
        d.update(_unpack_small(small_buf, w))
        d["conv_w"] = small_buf[5, :DN_CONV * conv_shard].reshape(DN_CONV, conv_shard)
        outs.append(d)
    result = [summed[_LOSS_ROW, 0], grad_x[None]]
    for d in outs:
        for n in _ORDER:
            result.append(d[n].reshape(w[n].shape)[None] if n in _MATRICES or n == "conv_w" else d[n].reshape(w[n].shape))
    return tuple(result)
```

```python
import jax
import jax.numpy as jnp
from jax import lax
from jax.experimental import pallas as pl
from jax.experimental.pallas import tpu as pltpu

F32, BF16 = jnp.float32, jnp.bfloat16
EPS = 1e-6
D_MODEL = 1024
N_DEV = 8
ATTN_BLOCK = 128
HEAD_PAIR = 128
DN_HEADS = 4
DN_DIM = 128
DN_CHUNK = 64
DN_CONV = 4
ROPE_THETA = 10000.0
D_IN = 2824
D_IN_PAD = 3072
BLK_Q, BLK_Z = 0, 1
BLK_DN, BLK_K, BLK_V, BLK_G = 8, 20, 21, 22
BLK_KV, BLK_G_PAD = 10, 11
VMEM_LIMIT = 48 * 1024 * 1024
NEG = -1e30
ADAM_LR, ADAM_B1, ADAM_B2, ADAM_EPS, ADAM_WD, ADAM_STEP = 0.001, 0.9, 0.999, 1e-08, 0.01, 10
MESH = pl.DeviceIdType.MESH


def _bf(x):
    return x.astype(BF16)


def _dot(a, b):
    return jnp.dot(a, b, preferred_element_type=F32)


def _dot_nt(a, b):
    return lax.dot_general(a, b, (((1,), (1,)), ((), ())), preferred_element_type=F32)


def _dot_tn(a, b):
    return lax.dot_general(a, b, (((0,), (0,)), ((), ())), preferred_element_type=F32)


def _sigmoid(x):
    return 1.0 / (1.0 + jnp.exp(-x))


def _params(sem):
    return pltpu.CompilerParams(dimension_semantics=sem, vmem_limit_bytes=VMEM_LIMIT)


def _mm_nn(x, w, *, name, out_dtypes, tn, epi=None, extra=(), tm=512, w_row_block=0):
    S, K = x.shape
    N = w.shape[1]
    rb = w_row_block
    tm = min(tm, S)
    n_extra = len(extra)

    def body(x_ref, w_ref, *rest):
        acc = _dot(_bf(x_ref[...]), w_ref[...])
        res = epi(acc, *[r[...] for r in rest[:n_extra]]) if epi else (acc,)
        for o, r in zip(rest[n_extra:], res):
            o[...] = r.astype(o.dtype)

    tile = pl.BlockSpec((tm, tn), lambda i, j: (i, j))
    return pl.pallas_call(
        body, grid=(S // tm, N // tn), name=name,
        in_specs=[pl.BlockSpec((tm, K), lambda i, j: (i, 0)), pl.BlockSpec((K, tn), lambda i, j: (rb, j))] + [tile] * n_extra,
        out_specs=[tile] * len(out_dtypes),
        out_shape=[jax.ShapeDtypeStruct((S, N), dt) for dt in out_dtypes],
        compiler_params=_params(("parallel", "parallel")),
    )(x, w, *extra)


def _mm_nt(dy, w, *, name, out_dtype, tn, epi=None, extra=(), tm=512):
    S, N = dy.shape
    K = w.shape[0]
    tm = min(tm, S)
    n_extra = len(extra)

    def body(dy_ref, w_ref, *rest):
        acc = _dot_nt(_bf(dy_ref[...]), w_ref[...])
        if epi:
            acc = epi(acc, *[r[...] for r in rest[:n_extra]])
        rest[n_extra][...] = acc.astype(out_dtype)

    tile = pl.BlockSpec((tm, tn), lambda i, j: (i, j))
    return pl.pallas_call(
        body, grid=(S // tm, K // tn), name=name,
        in_specs=[pl.BlockSpec((tm, N), lambda i, j: (i, 0)), pl.BlockSpec((tn, N), lambda i, j: (j, 0))] + [tile] * n_extra,
        out_specs=tile,
        out_shape=jax.ShapeDtypeStruct((S, K), out_dtype),
        compiler_params=_params(("parallel", "parallel")),
    )(dy, w, *extra)


def _mm_tn(x, dy, *, name, tm, tn, out_dtype=F32, column_shards=False):
    S, K = x.shape
    N = dy.shape[1]

    def body(x_ref, dy_ref, o_ref):
        o_ref[...] = _dot_tn(_bf(x_ref[...]), _bf(dy_ref[...])).astype(out_dtype)

    if column_shards:
        out_spec = pl.BlockSpec((None, tm, tn), lambda i, j: (j, i, 0))
        out_shape = jax.ShapeDtypeStruct((N // tn, K, tn), out_dtype)
    else:
        out_spec = pl.BlockSpec((tm, tn), lambda i, j: (i, j))
        out_shape = jax.ShapeDtypeStruct((K, N), out_dtype)
    return pl.pallas_call(
        body, grid=(K // tm, N // tn), name=name,
        in_specs=[pl.BlockSpec((S, tm), lambda i, j: (0, i)), pl.BlockSpec((S, tn), lambda i, j: (0, j))],
        out_specs=out_spec, out_shape=out_shape,
        compiler_params=_params(("parallel", "parallel")),
    )(x, dy)


def _rowwise(body, *, tiled, full, out_tiled, out_acc, name, tm=512, smem=()):
    S = tiled[0].shape[0]
    tm = min(tm, S)
    n_in = len(smem) + len(tiled) + len(full)

    def kern(*refs):
        @pl.when(pl.program_id(0) == 0)
        def _():
            for r in refs[n_in + len(out_tiled):]:
                r[...] = jnp.zeros_like(r)
        body(*refs)

    in_specs = [pl.BlockSpec(memory_space=pltpu.SMEM) for _ in smem]
    in_specs += [pl.BlockSpec((tm, a.shape[1]), lambda i: (i, 0)) for a in tiled]
    in_specs += [pl.BlockSpec(a.shape, lambda i, nd=a.ndim: (0,) * nd) for a in full]
    out_specs = [pl.BlockSpec((tm, w), lambda i: (i, 0)) for w, _ in out_tiled]
    out_specs += [pl.BlockSpec(shp, lambda i, nd=len(shp): (0,) * nd) for shp, _ in out_acc]
    out_shape = [jax.ShapeDtypeStruct((S, w), dt) for w, dt in out_tiled]
    out_shape += [jax.ShapeDtypeStruct(shp, dt) for shp, dt in out_acc]
    return pl.pallas_call(
        kern, grid=(S // tm,), name=name, in_specs=in_specs, out_specs=out_specs, out_shape=out_shape,
        compiler_params=_params(("arbitrary",)),
    )(*smem, *tiled, *full)


def _rms_stats(x):
    r = lax.rsqrt(jnp.mean(x * x, axis=-1, keepdims=True) + EPS)
    return r, x * r


def _rmsnorm_fwd(x, g, name):
    def body(x_ref, g_ref, o_ref):
        _, xh = _rms_stats(x_ref[...])
        o_ref[...] = _bf(xh * g_ref[...])

    return _rowwise(body, tiled=[x], full=[g], out_tiled=[(x.shape[1], BF16)], out_acc=[], name=name)[0]


def _rms_bwd_tile(x, g, dxn):
    r, xh = _rms_stats(x)
    dg = jnp.sum(dxn * xh, axis=0, keepdims=True)
    dn = dxn * g
    dx = r * (dn - xh * jnp.mean(dn * xh, axis=-1, keepdims=True))
    return dx, dg


def _rmsnorm_bwd(x, g, dxn, dres, name):
    def body(x_ref, dxn_ref, dres_ref, g_ref, dx_ref, dg_ref):
        dx, dg = _rms_bwd_tile(x_ref[...], g_ref[...], dxn_ref[...])
        dx_ref[...] = dres_ref[...] + dx
        dg_ref[...] += dg

    n = x.shape[1]
    return _rowwise(body, tiled=[x, dxn, dres], full=[g], out_tiled=[(n, F32)], out_acc=[((1, n), F32)], name=name)


def _final_loss(h3, g, target):
    n = h3.shape[1]

    def body(h_ref, t_ref, g_ref, dh_ref, loss_ref, dg_ref):
        x = h_ref[...]
        _, xh = _rms_stats(x)
        e = xh * g_ref[...] - t_ref[...]
        per_tok = jnp.mean(e * e, axis=-1, keepdims=True)
        loss_ref[...] += 0.5 * jnp.sum(per_tok, axis=0, keepdims=True)
        dx, dg = _rms_bwd_tile(x, g_ref[...], e * (1.0 / n))
        dh_ref[...] = dx
        dg_ref[...] += dg

    return _rowwise(body, tiled=[h3, target], full=[g], out_tiled=[(n, F32)],
                    out_acc=[((1, 128), F32), ((1, n), F32)], name="final_loss")


def _ple_bwd(dh3, pp, gate):
    def body(dh_ref, pp_ref, gate_ref, dgl_ref, dpp_ref):
        dh, gt = dh_ref[...], gate_ref[...]
        dgl_ref[...] = _bf(dh * pp_ref[...] * gt * (1.0 - gt))
        dpp_ref[...] = _bf(dh * gt)

    n = dh3.shape[1]
    return _rowwise(body, tiled=[dh3, pp, gate], full=[], out_tiled=[(n, BF16), (n, BF16)], out_acc=[], name="ple_bwd")


def _rope_tables(S):
    half = 32
    inv = 1.0 / (ROPE_THETA ** (jnp.arange(half, dtype=F32) * (2.0 / 64)))
    ang = jnp.arange(S).astype(F32)[:, None] * inv[None, :]
    cos, sin = jnp.cos(ang), jnp.sin(ang)
    return jnp.tile(cos, (1, 4)), jnp.concatenate([-sin, sin, -sin, sin], axis=1)


def _attn_common(i, kc, kp, vc, vp, cc, sc, cp, sp):
    lane = lax.broadcasted_iota(jnp.int32, (1, HEAD_PAIR), 1)
    lane_lo = jnp.bitwise_and(lane, 63) < 32
    slot = [lane < 64, lane >= 64]

    def swap_halves(t):
        return jnp.where(lane_lo, pltpu.roll(t, 96, 1), pltpu.roll(t, 32, 1))

    def rope(t, cos, sin):
        return t * cos + swap_halves(t) * sin

    def unrope(d, cos, sin):
        return d * cos + swap_halves(d * sin)

    k2 = jnp.concatenate([rope(kp, cp, sp), rope(kc, cc, sc)], axis=0)
    v2 = jnp.concatenate([vp, vc], axis=0)
    r = lax.broadcasted_iota(jnp.int32, (ATTN_BLOCK, 2 * ATTN_BLOCK), 0)
    c = lax.broadcasted_iota(jnp.int32, (ATTN_BLOCK, 2 * ATTN_BLOCK), 1)
    valid = (c > r) & (c <= r + ATTN_BLOCK) & jnp.logical_or(c >= ATTN_BLOCK, i > 0)
    ks, vs = {}, {}
    for j in range(2):
        kn = jnp.where(slot[j], k2, 0.0)
        vn = jnp.where(slot[j], v2, 0.0)
        for s in range(2):
            ks[j, s] = _bf(kn if s == j else pltpu.roll(kn, 64, 1))
            vs[j, s] = _bf(vn if s == j else pltpu.roll(vn, 64, 1))
    return slot, rope, unrope, valid, ks, vs


def _attn_probs(scores, valid, sink):
    s = jnp.where(valid, scores * 0.125, NEG)
    m = jnp.maximum(jnp.max(s, axis=1, keepdims=True), sink)
    e = jnp.exp(s - m)
    inv_z = 1.0 / (jnp.sum(e, axis=1, keepdims=True) + jnp.exp(sink - m))
    return e * inv_z, jnp.exp(sink - m) * inv_z


def _attn_specs(S):
    nb = S // ATTN_BLOCK
    prev = lambda i: jnp.maximum(i - 1, 0)
    blk = lambda w, col, row=(lambda i: i): pl.BlockSpec((ATTN_BLOCK, w), lambda i: (row(i), col))
    in_specs = [pl.BlockSpec(memory_space=pltpu.SMEM),
                blk(512, BLK_Q), blk(128, BLK_K), blk(128, BLK_K, prev), blk(128, BLK_V), blk(128, BLK_V, prev),
                blk(128, 0), blk(128, 0), blk(128, 0, prev), blk(128, 0, prev)]
    return nb, in_specs


def _attn_fwd(pa, cos, sin, sinks):
    S = pa.shape[0]
    nb, in_specs = _attn_specs(S)

    def body(sinks_ref, q_ref, kc_ref, kp_ref, vc_ref, vp_ref, cc_ref, sc_ref, cp_ref, sp_ref, o_ref):
        i = pl.program_id(0)
        cc, sc = cc_ref[...], sc_ref[...]
        _, rope, _, valid, ks, vs = _attn_common(i, kc_ref[...], kp_ref[...], vc_ref[...], vp_ref[...],
                                                 cc, sc, cp_ref[...], sp_ref[...])
        pair_cols = [slice(HEAD_PAIR * pair, HEAD_PAIR * (pair + 1)) for pair in range(4)]
        qps = [_bf(rope(q_ref[:, cols], cc, sc)) for cols in pair_cols]
        outs = {}

        def head_program(h):
            pair, s = divmod(h, 2)
            j = h // 4
            scores = _dot_nt(qps[pair], ks[j, s])
            yield
            p, _ = _attn_probs(scores, valid, sinks_ref[h])
            outs[h] = _dot(_bf(p), vs[j, s])

        _interleave(head_program(h) for h in range(8))
        for pair, cols in enumerate(pair_cols):
            o_ref[:, cols] = outs[2 * pair] + outs[2 * pair + 1]

    return pl.pallas_call(
        body, grid=(nb,), name="attn_fwd", in_specs=in_specs,
        out_specs=pl.BlockSpec((ATTN_BLOCK, 512), lambda i: (i, 0)),
        out_shape=jax.ShapeDtypeStruct((S, 512), F32),
        compiler_params=_params(("parallel",)),
    )(sinks, pa, pa, pa, pa, pa, cos, sin, cos, sin)


def _attn_bwd(pa, cos, sin, sinks, dcat):
    S = pa.shape[0]
    nb, in_specs = _attn_specs(S)
    in_specs = in_specs + [pl.BlockSpec((ATTN_BLOCK, 512), lambda i: (i, 0))]

    def body(sinks_ref, q_ref, kc_ref, kp_ref, vc_ref, vp_ref, cc_ref, sc_ref, cp_ref, sp_ref, do_ref,
             dq_ref, dk_ref, dv_ref, dsink_ref):
        i = pl.program_id(0)

        @pl.when(i == 0)
        def _():
            dk_ref[...] = jnp.zeros_like(dk_ref)
            dv_ref[...] = jnp.zeros_like(dv_ref)
            dsink_ref[...] = jnp.zeros_like(dsink_ref)

        cc, sc, cp, sp = cc_ref[...], sc_ref[...], cp_ref[...], sp_ref[...]
        slot, rope, unrope, valid, ks, vs = _attn_common(i, kc_ref[...], kp_ref[...], vc_ref[...], vp_ref[...], cc, sc, cp, sp)
        pair_cols = [slice(HEAD_PAIR * pair, HEAD_PAIR * (pair + 1)) for pair in range(4)]
        qps = [_bf(rope(q_ref[:, cols], cc, sc)) for cols in pair_cols]
        dobs = [_bf(do_ref[:, cols]) for cols in pair_cols]
        dqs, dks, dvs = {}, {}, {}

        def head_program(h):
            pair, s = divmod(h, 2)
            j = h // 4
            qp, dob = qps[pair], dobs[pair]
            scores = _dot_nt(qp, ks[j, s])
            dp = _dot_nt(dob, vs[j, s])
            yield
            p, p_sink = _attn_probs(scores, valid, sinks_ref[h])
            dr = jnp.sum(p * dp, axis=1, keepdims=True)
            ds = _bf(p * (dp - dr) * 0.125)
            dsink_ref[h:h + 1, :] += -jnp.sum(p_sink * dr, axis=0, keepdims=True)
            dqs[h] = _dot(ds, ks[j, s])
            dk_h = _dot_tn(ds, qp)
            dv_h = _dot_tn(_bf(p), dob)
            yield
            dk_h, dv_h = jnp.where(slot[s], dk_h, 0.0), jnp.where(slot[s], dv_h, 0.0)
            if s != j:
                dk_h, dv_h = pltpu.roll(dk_h, 64, 1), pltpu.roll(dv_h, 64, 1)
            dks[h], dvs[h] = dk_h, dv_h

        _interleave(head_program(h) for h in range(8))
        dk2 = sum((dks[h] for h in range(1, 8)), dks[0])
        dv2 = sum((dvs[h] for h in range(1, 8)), dvs[0])
        for pair, cols in enumerate(pair_cols):
            dq_ref[:, cols] = _bf(unrope(dqs[2 * pair] + dqs[2 * pair + 1], cc, sc))
        cur = pl.ds(pl.multiple_of(i * ATTN_BLOCK, ATTN_BLOCK), ATTN_BLOCK)
        dk_ref[cur, :] += unrope(dk2[ATTN_BLOCK:], cc, sc)
        dv_ref[cur, :] += dv2[ATTN_BLOCK:]

        @pl.when(i > 0)
        def _():
            prv = pl.ds(pl.multiple_of((i - 1) * ATTN_BLOCK, ATTN_BLOCK), ATTN_BLOCK)
            dk_ref[prv, :] += unrope(dk2[:ATTN_BLOCK], cp, sp)
            dv_ref[prv, :] += dv2[:ATTN_BLOCK]

    whole = lambda w: pl.BlockSpec((S, w), lambda i: (0, 0))
    return pl.pallas_call(
        body, grid=(nb,), name="attn_bwd", in_specs=in_specs,
        out_specs=[pl.BlockSpec((ATTN_BLOCK, 512), lambda i: (i, BLK_Q)), whole(128), whole(128),
                   pl.BlockSpec((8, 128), lambda i: (0, 0))],
        out_shape=[jax.ShapeDtypeStruct((S, D_IN_PAD), BF16), jax.ShapeDtypeStruct((S, 128), F32),
                   jax.ShapeDtypeStruct((S, 128), F32), jax.ShapeDtypeStruct((8, 128), F32)],
        compiler_params=_params(("arbitrary",)),
    )(sinks, pa, pa, pa, pa, pa, cos, sin, cos, sin, dcat)


CONV_ROWS = 512
CONV_PAD = 8


def _conv_silu(scr, w, r0):
    y = w[3:4, :] * scr[pl.ds(CONV_PAD + r0, CONV_ROWS), :]
    for j in range(DN_CONV - 1):
        y = y + w[j:j + 1, :] * scr[pl.ds(CONV_PAD + r0 - 3 + j, CONV_ROWS), :]
    return y


def _dn_prep_fwd(pd, conv_w):
    S = pd.shape[0]
    assert S % CONV_ROWS == 0

    def body(x_ref, w_ref, o_ref, scr):
        b = pl.program_id(0)
        scr[0:CONV_PAD, :] = jnp.zeros((CONV_PAD, DN_DIM), F32)
        scr[pl.ds(CONV_PAD, S), :] = x_ref[...]
        w = w_ref[...]
        q_scale = jnp.where(b < DN_HEADS, DN_DIM ** -0.5, 1.0)
        for r0 in range(0, S, CONV_ROWS):
            y = _conv_silu(scr, w, r0)
            a = y * _sigmoid(y)
            rs = lax.rsqrt(jnp.sum(a * a, axis=1, keepdims=True) + EPS)
            o_ref[pl.ds(r0, CONV_ROWS), :] = a * jnp.where(b < 2 * DN_HEADS, rs * q_scale, 1.0)

    col = pl.BlockSpec((S, DN_DIM), lambda b: (0, b))
    return pl.pallas_call(
        body, grid=(3 * DN_HEADS,), name="dn_prep_fwd",
        in_specs=[pl.BlockSpec((S, DN_DIM), lambda b: (0, BLK_DN + b)), pl.BlockSpec((DN_CONV, DN_DIM), lambda b: (0, b))],
        out_specs=col,
        out_shape=jax.ShapeDtypeStruct((S, 3 * DN_HEADS * DN_DIM), F32),
        scratch_shapes=[pltpu.VMEM((S + CONV_PAD, DN_DIM), F32)],
        compiler_params=_params(("parallel",)),
    )(pd, conv_w)


def _dn_prep_bwd(pd, conv_w, dqkv, dproj):
    S = pd.shape[0]

    def body(x_ref, w_ref, d_ref, _, dx_ref, dw_ref, scr, dscr):
        b = pl.program_id(0)
        scr[0:CONV_PAD, :] = jnp.zeros((CONV_PAD, DN_DIM), F32)
        scr[pl.ds(CONV_PAD, S), :] = x_ref[...]
        dscr[pl.ds(S, CONV_PAD), :] = jnp.zeros((CONV_PAD, DN_DIM), F32)
        w = w_ref[...]
        q_scale = jnp.where(b < DN_HEADS, DN_DIM ** -0.5, 1.0)
        is_qk = b < 2 * DN_HEADS
        dw = [jnp.zeros((1, DN_DIM), F32) for _ in range(DN_CONV)]
        for r0 in range(0, S, CONV_ROWS):
            y = _conv_silu(scr, w, r0)
            sg = _sigmoid(y)
            a = y * sg
            dout = d_ref[pl.ds(r0, CONV_ROWS), :]
            rs = lax.rsqrt(jnp.sum(a * a, axis=1, keepdims=True) + EPS)
            da_qk = q_scale * rs * (dout - a * (rs * rs) * jnp.sum(dout * a, axis=1, keepdims=True))
            dy = jnp.where(is_qk, da_qk, dout) * (sg * (1.0 + y * (1.0 - sg)))
            dscr[pl.ds(r0, CONV_ROWS), :] = dy
            for j in range(DN_CONV):
                dw[j] = dw[j] + jnp.sum(dy * scr[pl.ds(CONV_PAD + r0 - 3 + j, CONV_ROWS), :], axis=0, keepdims=True)
        for j in range(DN_CONV):
            dw_ref[j:j + 1, :] = dw[j]
        for r0 in range(0, S, CONV_ROWS):
            dx = w[3:4, :] * dscr[pl.ds(r0, CONV_ROWS), :]
            for j in range(DN_CONV - 1):
                dx = dx + w[j:j + 1, :] * dscr[pl.ds(r0 + 3 - j, CONV_ROWS), :]
            dx_ref[pl.ds(r0, CONV_ROWS), :] = _bf(dx)

    col = pl.BlockSpec((S, DN_DIM), lambda b: (0, b))
    proj_col = pl.BlockSpec((S, DN_DIM), lambda b: (0, BLK_DN + b))
    wcol = pl.BlockSpec((DN_CONV, DN_DIM), lambda b: (0, b))
    return pl.pallas_call(
        body, grid=(3 * DN_HEADS,), name="dn_prep_bwd",
        in_specs=[proj_col, wcol, col, pl.BlockSpec(memory_space=pl.ANY)], out_specs=[proj_col, wcol],
        out_shape=[jax.ShapeDtypeStruct(dproj.shape, dproj.dtype), jax.ShapeDtypeStruct((DN_CONV, 3 * DN_HEADS * DN_DIM), F32)],
        scratch_shapes=[pltpu.VMEM((S + CONV_PAD, DN_DIM), F32), pltpu.VMEM((S + CONV_PAD, DN_DIM), F32)],
        input_output_aliases={3: 0},
        compiler_params=_params(("parallel",)),
    )(pd, conv_w, dqkv, dproj)


CPAD = 128


def _chunk_masks():
    ii = lax.broadcasted_iota(jnp.int32, (DN_CHUNK, CPAD), 0)
    jj = lax.broadcasted_iota(jnp.int32, (DN_CHUNK, CPAD), 1)
    return ii, jj


def _rows_pad(a):
    return jnp.concatenate([a, jnp.zeros_like(a)], axis=0)


def _hi_lo(a):
    hi = _bf(a)
    return hi, _bf(a - hi.astype(F32))


def _double_step(t, p):
    C = DN_CHUNK
    th, tl = _hi_lo(t)
    ph, pl_ = _hi_lo(p)
    r1 = _dot(jnp.concatenate([th, tl, ph, pl_], axis=0), _rows_pad(ph))
    r2 = _dot(jnp.concatenate([th, ph], axis=0), _rows_pad(pl_))
    return t + (r1[:C] + r1[C:2 * C] + r2[:C]), r1[2 * C:3 * C] + r1[3 * C:] + r2[C:]


def _dot3_nt(a, b):
    C = DN_CHUNK
    ah, al = _hi_lo(a)
    bh, bl = _hi_lo(b)
    r1 = _dot_nt(jnp.concatenate([ah, al], axis=0), _rows_pad(bh))
    return r1[:C] + r1[C:] + _dot_nt(ah, _rows_pad(bl))


def _dot3_tn(a, b):
    C = DN_CHUNK
    ah, al = _hi_lo(a)
    bh, bl = _hi_lo(b)
    return _dot_tn(ah, bh)[:C] + _dot_tn(al, bh)[:C] + _dot_tn(ah, bl)[:C]


def _interleave(programs):
    programs = list(programs)
    while programs:
        alive = []
        for prog in programs:
            try:
                next(prog)
                alive.append(prog)
            except StopIteration:
                pass
        programs = alive


def _col_to_row(col, ii, jj):
    return jnp.sum(jnp.where(ii == jj, col, 0.0), axis=0, keepdims=True)


def _row_to_col(row, ii, jj):
    return jnp.sum(jnp.where(ii == jj, row, 0.0), axis=1, keepdims=True)


def _decay(gc_col, ii, jj):
    diff = gc_col - _col_to_row(gc_col, ii, jj)
    return jnp.where(jj <= ii, jnp.exp(jnp.where(jj <= ii, diff, 0.0)), 0.0)


def _softplus(x):
    return jnp.maximum(x, 0.0) + jnp.log(1.0 + jnp.exp(-jnp.abs(x)))


def _head(h):
    return slice(DN_DIM * h, DN_DIM * (h + 1))


def _dn_chunk_fwd(qkv, pg, a_log, dt_bias):
    S = qkv.shape[0]
    C = DN_CHUNK
    nc = S // C

    def body(alog_ref, dtb_ref, qkv_ref, pg_ref, w_ref, u_ref, qg_ref, kd_ref, a_ref, t_ref, gcs_ref):
        ii, jj = _chunk_masks()
        lane = lax.broadcasted_iota(jnp.int32, (1, 128), 1)
        eye = (ii == jj).astype(F32)
        gcs_parts = []

        def head_program(h):
            q, k, v = qkv_ref[:, _head(h)], qkv_ref[:, _head(DN_HEADS + h)], qkv_ref[:, _head(2 * DN_HEADS + h)]
            beta = _sigmoid(pg_ref[:, h:h + 1])
            g_col = -jnp.exp(alog_ref[h]) * _softplus(pg_ref[:, DN_HEADS + h:DN_HEADS + h + 1] + dtb_ref[h])
            g_row = _col_to_row(g_col, ii, jj)
            gc_col = jnp.sum(jnp.where(jj <= ii, g_row, 0.0), axis=1, keepdims=True)
            dec = _decay(gc_col, ii, jj)
            eg = jnp.exp(gc_col)
            kb, vb = k * beta, v * beta
            k_rows = _rows_pad(_bf(k))
            kk = _dot_nt(_bf(kb), k_rows)
            qk = _dot_nt(_bf(q), k_rows)
            yield
            t, pw = eye, -jnp.where(jj < ii, kk * dec, 0.0)
            for _ in range(6):
                t, pw = _double_step(t, pw)
                yield
            tb = _bf(t)
            u_ref[:, _head(h)] = _dot(tb, _rows_pad(_bf(vb)))
            w_ref[:, _head(h)] = _dot(tb, _rows_pad(_bf(kb * eg)))
            a_ref[h] = qk * dec
            t_ref[h] = t
            qg_ref[:, _head(h)] = q * eg
            kd_ref[:, _head(h)] = k * jnp.exp(gc_col[C - 1:C, :] - gc_col)
            gcs_parts.append(jnp.where(lane == h, gc_col, 0.0) + jnp.where(lane == DN_HEADS + h, beta, 0.0)
                             + jnp.where(lane == 2 * DN_HEADS + h, g_col, 0.0))

        _interleave(head_program(h) for h in range(DN_HEADS))
        gcs_ref[...] = sum(gcs_parts[1:], gcs_parts[0])

    smem = pl.BlockSpec(memory_space=pltpu.SMEM)
    wide = pl.BlockSpec((C, 512), lambda n: (n, 0))
    sq = pl.BlockSpec((DN_HEADS, C, CPAD), lambda n: (0, n, 0))
    narrow = pl.BlockSpec((C, 128), lambda n: (n, 0))
    f = lambda *shp: jax.ShapeDtypeStruct(shp, F32)
    return pl.pallas_call(
        body, grid=(nc,), name="dn_chunk_fwd",
        in_specs=[smem, smem, pl.BlockSpec((C, 1536), lambda n: (n, 0)), pl.BlockSpec((C, 128), lambda n: (n, BLK_G))],
        out_specs=[wide, wide, wide, wide, sq, sq, narrow],
        out_shape=[f(S, 512), f(S, 512), f(S, 512), f(S, 512), f(DN_HEADS, S, CPAD), f(DN_HEADS, S, CPAD), f(S, 128)],
        compiler_params=_params(("parallel",)),
    )(a_log, dt_bias, qkv, pg)


def _gated_norm(o, z, gn):
    r, oh = _rms_stats(o)
    return oh * gn * (z * _sigmoid(z))


def _dn_scan_fwd(w, u, qg, kd, a, gcs, pz, gn):
    S = w.shape[0]
    C = DN_CHUNK
    nc = S // C

    def body(w_ref, u_ref, qg_ref, kd_ref, a_ref, gcs_ref, z_ref, gn_ref, o_ref, vn_ref, sst_ref, out_ref, state):
        @pl.when(pl.program_id(0) == 0)
        def _():
            state[...] = jnp.zeros_like(state)

        def head_program(h):
            hs = _head(h)
            s_in = state[h]
            sst_ref[0, h] = s_in
            sb = _bf(s_in)
            w_s = _dot(_bf(w_ref[:, hs]), sb)
            q_s = _dot(_bf(qg_ref[:, hs]), sb)
            yield
            vn = u_ref[:, hs] - w_s
            vnb = _bf(vn)
            o = q_s + _dot(_bf(a_ref[h]), _rows_pad(vnb))
            k_v = _dot_tn(_bf(kd_ref[:, hs]), vnb)
            yield
            state[h] = s_in * jnp.exp(gcs_ref[C - 1:C, h:h + 1]) + k_v
            o_ref[:, hs] = o
            vn_ref[:, hs] = vn
            out_ref[:, hs] = _gated_norm(o, z_ref[:, hs], gn_ref[...])

        _interleave(head_program(h) for h in range(DN_HEADS))

    wide = pl.BlockSpec((C, 512), lambda n: (n, 0))
    f = lambda *shp: jax.ShapeDtypeStruct(shp, F32)
    return pl.pallas_call(
        body, grid=(nc,), name="dn_scan_fwd",
        in_specs=[wide, wide, wide, wide, pl.BlockSpec((DN_HEADS, C, CPAD), lambda n: (0, n, 0)),
                  pl.BlockSpec((C, 128), lambda n: (n, 0)), pl.BlockSpec((C, 512), lambda n: (n, BLK_Z)),
                  pl.BlockSpec((1, DN_DIM), lambda n: (0, 0))],
        out_specs=[wide, wide, pl.BlockSpec((1, DN_HEADS, DN_DIM, DN_DIM), lambda n: (n, 0, 0, 0)), wide],
        out_shape=[f(S, 512), f(S, 512), f(nc, DN_HEADS, DN_DIM, DN_DIM), f(S, 512)],
        scratch_shapes=[pltpu.VMEM((DN_HEADS, DN_DIM, DN_DIM), F32)],
        compiler_params=_params(("arbitrary",)),
    )(w, u, qg, kd, a, gcs, pz, gn)


def _dn_scan_bwd(dcat, o, pz, gn, sst, vnew, w, qg, kd, a, gcs, dproj):
    S = o.shape[0]
    C = DN_CHUNK
    nc = S // C

    def body(dy_ref, o_ref, z_ref, gn_ref, sst_ref, vn_ref, w_ref, qg_ref, kd_ref, a_ref, gcs_ref, _,
             du_ref, dw_ref, dqg_ref, dkd_ref, da_ref, dz_ref, dsc_ref, dgn_ref, dstate):
        @pl.when(pl.program_id(0) == 0)
        def _():
            dstate[...] = jnp.zeros_like(dstate)
            dgn_ref[...] = jnp.zeros_like(dgn_ref)

        gn_ = gn_ref[...]
        lane = lax.broadcasted_iota(jnp.int32, (C, 128), 1)
        row = lax.broadcasted_iota(jnp.int32, (C, 128), 0)
        dsc_parts, dgn_parts = [], []

        def head_program(h):
            hs = _head(h)
            ov, z, dout = o_ref[:, hs], z_ref[:, hs], dy_ref[:, hs]
            r, oh = _rms_stats(ov)
            sg = _sigmoid(z)
            don = dout * (z * sg)
            dz_ref[:, hs] = _bf(dout * (oh * gn_) * (sg * (1.0 + z * (1.0 - sg))))
            dgn_parts.append(jnp.sum(don * oh, axis=0, keepdims=True))
            dn = don * gn_
            do = _bf(r * (dn - oh * jnp.mean(dn * oh, axis=-1, keepdims=True)))
            s_in = sst_ref[0, h]
            sb = _bf(s_in)
            ds_out = dstate[h]
            dsb = _bf(ds_out)
            vnb = _bf(vn_ref[:, hs])
            wb, qgb, kdb, ab = _bf(w_ref[:, hs]), _bf(qg_ref[:, hs]), _bf(kd_ref[:, hs]), _bf(a_ref[h])
            dvn = _dot_tn(ab, do)[:C] + _dot(kdb, dsb)
            da_ref[h] = _dot_nt(do, _rows_pad(vnb))
            dqg_ref[:, hs] = _dot_nt(do, sb)
            dkd_ref[:, hs] = _dot_nt(vnb, dsb)
            q_do = _dot_tn(qgb, do)
            yield
            dvnb = _bf(dvn)
            dw_ref[:, hs] = -_dot_nt(dvnb, sb)
            w_dvn = _dot_tn(wb, dvnb)
            du_ref[:, hs] = dvn
            yield
            d_last = jnp.exp(gcs_ref[C - 1:C, h:h + 1])
            dd = jnp.sum(jnp.sum(ds_out * s_in, axis=1, keepdims=True), axis=0, keepdims=True)
            dsc_parts.append(jnp.where((lane == h) & (row == C - 1), dd * d_last, 0.0))
            dstate[h] = ds_out * d_last + q_do - w_dvn

        _interleave(head_program(h) for h in range(DN_HEADS))
        dsc_ref[...] = sum(dsc_parts[1:], dsc_parts[0])
        dgn_ref[...] += sum(dgn_parts[1:], dgn_parts[0])

    rev = lambda n: nc - 1 - n
    wide = pl.BlockSpec((C, 512), lambda n: (rev(n), 0))
    z_spec = pl.BlockSpec((C, 512), lambda n: (rev(n), BLK_Z))
    sq = pl.BlockSpec((DN_HEADS, C, CPAD), lambda n: (0, rev(n), 0))
    narrow = pl.BlockSpec((C, 128), lambda n: (rev(n), 0))
    gn_spec = pl.BlockSpec((1, DN_DIM), lambda n: (0, 0))
    f = lambda *shp: jax.ShapeDtypeStruct(shp, F32)
    return pl.pallas_call(
        body, grid=(nc,), name="dn_scan_bwd",
        in_specs=[pl.BlockSpec((C, 512), lambda n: (rev(n), 1)), wide, z_spec, gn_spec,
                  pl.BlockSpec((1, DN_HEADS, DN_DIM, DN_DIM), lambda n: (rev(n), 0, 0, 0)),
                  wide, wide, wide, wide, sq, narrow, pl.BlockSpec(memory_space=pl.ANY)],
        out_specs=[wide, wide, wide, wide, sq, z_spec, narrow, gn_spec],
        out_shape=[f(S, 512), f(S, 512), f(S, 512), f(S, 512), f(DN_HEADS, S, CPAD),
                   jax.ShapeDtypeStruct(dproj.shape, dproj.dtype), f(S, 128), f(1, DN_DIM)],
        scratch_shapes=[pltpu.VMEM((DN_HEADS, DN_DIM, DN_DIM), F32)],
        input_output_aliases={11: 5},
        compiler_params=_params(("arbitrary",)),
    )(dcat, o, pz, gn, sst, vnew, w, qg, kd, a, gcs, dproj)


def _dn_chunk_bwd(qkv, pg, t_inv, gcs, du, dw, dqg, dkd, da, dsc, a_log, dt_bias, dproj):
    S = qkv.shape[0]
    C = DN_CHUNK
    nc = S // C

    def body(alog_ref, dtb_ref, qkv_ref, pg_ref, t_ref, gcs_ref, du_ref, dw_ref, dqg_ref, dkd_ref, da_ref, dsc_ref, _,
             dqkv_ref, dpg_ref, acc_ref):
        @pl.when(pl.program_id(0) == 0)
        def _():
            acc_ref[...] = jnp.zeros_like(acc_ref)

        ii, jj = _chunk_masks()
        lane = lax.broadcasted_iota(jnp.int32, (1, 128), 1)
        row8 = lax.broadcasted_iota(jnp.int32, (8, 128), 0)
        lane8 = lax.broadcasted_iota(jnp.int32, (8, 128), 1)
        rowc = lax.broadcasted_iota(jnp.int32, (C, 1), 0)
        tril, strict = jj <= ii, jj < ii
        dpg_parts, acc_parts = [], []

        def head_program(h):
            q, k, v = qkv_ref[:, _head(h)], qkv_ref[:, _head(DN_HEADS + h)], qkv_ref[:, _head(2 * DN_HEADS + h)]
            gc_col, beta, g_col = gcs_ref[:, h:h + 1], gcs_ref[:, DN_HEADS + h:DN_HEADS + h + 1], \
                gcs_ref[:, 2 * DN_HEADS + h:2 * DN_HEADS + h + 1]
            dec = _decay(gc_col, ii, jj)
            eg = jnp.exp(gc_col)
            g_last = gc_col[C - 1:C, :]
            ek = jnp.exp(g_last - gc_col)
            kb, vb = k * beta, v * beta
            kbg = kb * eg
            qb, kbb = _bf(q), _bf(kb)
            k_rows = _rows_pad(_bf(k))
            t = t_ref[h]
            tb = _bf(t)
            dub, dwb = _bf(du_ref[:, _head(h)]), _bf(dw_ref[:, _head(h)])
            dqg_, dkd_ = dqg_ref[:, _head(h)], dkd_ref[:, _head(h)]
            dt = _dot_nt(dub, _rows_pad(_bf(vb))) + _dot_nt(dwb, _rows_pad(_bf(kbg)))
            dvb = _dot_tn(tb, dub)[:C]
            dkbg = _dot_tn(tb, dwb)[:C]
            kk = _dot_nt(kbb, k_rows)
            qk = _dot_nt(qb, k_rows)
            yield
            dt_t = _dot3_nt(dt, t)
            yield
            dl = -_dot3_tn(t, dt_t)
            yield
            dm = jnp.where(strict, dl * dec, 0.0)
            dqk = jnp.where(tril, da_ref[h] * dec, 0.0)
            gmat = dm * kk + dqk * qk
            dgc = jnp.sum(gmat, axis=1, keepdims=True) - _row_to_col(jnp.sum(gmat, axis=0, keepdims=True), ii, jj)
            dmb, dqkb = _bf(dm), _bf(dqk)
            dkb = _dot(dmb, k_rows) + dkbg * eg
            dk = _dot_tn(dmb, kbb)[:C] + _dot_tn(dqkb, qb)[:C] + dkd_ * ek
            dq = _dot(dqkb, k_rows) + dqg_ * eg
            yield
            tk = jnp.sum(dkd_ * k * ek, axis=1, keepdims=True)
            dgc = dgc + jnp.sum(dqg_ * q * eg, axis=1, keepdims=True) - tk + jnp.sum(dkbg * kbg, axis=1, keepdims=True)
            dgl = jnp.sum(tk, axis=0, keepdims=True) + dsc_ref[C - 1:C, h:h + 1]
            dgc = dgc + jnp.where(rowc == C - 1, dgl, 0.0)
            dk = dk + dkb * beta
            dbeta = jnp.sum(dkb * k, axis=1, keepdims=True) + jnp.sum(dvb * v, axis=1, keepdims=True)
            dqkv_ref[:, _head(h)] = dq
            dqkv_ref[:, _head(DN_HEADS + h)] = dk
            dqkv_ref[:, _head(2 * DN_HEADS + h)] = dvb * beta
            dg_col = jnp.sum(jnp.where(jj >= ii, _col_to_row(dgc, ii, jj), 0.0), axis=1, keepdims=True)
            db = dbeta * beta * (1.0 - beta)
            da_in = dg_col * (-jnp.exp(alog_ref[h])) * _sigmoid(pg_ref[:, DN_HEADS + h:DN_HEADS + h + 1] + dtb_ref[h])
            dpg_parts.append(jnp.where(lane == h, db, 0.0) + jnp.where(lane == DN_HEADS + h, da_in, 0.0))
            acc_parts.append(jnp.where((row8 == 0) & (lane8 == h), jnp.sum(dg_col * g_col, axis=0, keepdims=True), 0.0)
                             + jnp.where((row8 == 1) & (lane8 == h), jnp.sum(da_in, axis=0, keepdims=True), 0.0))

        _interleave(head_program(h) for h in range(DN_HEADS))
        dpg = sum(dpg_parts[1:], dpg_parts[0])
        dpg_ref[...] = _bf(jnp.concatenate([dpg, jnp.zeros_like(dpg)], axis=1))
        acc_ref[...] += sum(acc_parts[1:], acc_parts[0])

    smem = pl.BlockSpec(memory_space=pltpu.SMEM)
    wide = pl.BlockSpec((C, 512), lambda n: (n, 0))
    sq = pl.BlockSpec((DN_HEADS, C, CPAD), lambda n: (0, n, 0))
    narrow = pl.BlockSpec((C, 128), lambda n: (n, 0))
    qkv_spec = pl.BlockSpec((C, 1536), lambda n: (n, 0))
    f = lambda *shp: jax.ShapeDtypeStruct(shp, F32)
    return pl.pallas_call(
        body, grid=(nc,), name="dn_chunk_bwd",
        in_specs=[smem, smem, qkv_spec, pl.BlockSpec((C, 128), lambda n: (n, BLK_G)), sq, narrow, wide, wide, wide, wide, sq,
                  narrow, pl.BlockSpec(memory_space=pl.ANY)],
        out_specs=[qkv_spec, pl.BlockSpec((C, 256), lambda n: (n, BLK_G_PAD)), pl.BlockSpec((8, 128), lambda n: (0, 0))],
        out_shape=[f(S, 1536), jax.ShapeDtypeStruct(dproj.shape, dproj.dtype), f(8, 128)],
        input_output_aliases={12: 1},
        compiler_params=_params(("arbitrary",)),
    )(a_log, dt_bias, qkv, pg, t_inv, gcs, du, dw, dqg, dkd, da, dsc, dproj)


def _fill_kv(dk, dv, dproj):
    S = dk.shape[0]
    tm = min(512, S)

    def body(dk_ref, dv_ref, _, o_ref):
        o_ref[...] = _bf(jnp.concatenate([dk_ref[...], dv_ref[...]], axis=1))

    tile = pl.BlockSpec((tm, 128), lambda i: (i, 0))
    return pl.pallas_call(
        body, grid=(S // tm,), name="fill_kv",
        in_specs=[tile, tile, pl.BlockSpec(memory_space=pl.ANY)],
        out_specs=pl.BlockSpec((tm, 256), lambda i: (i, BLK_KV)),
        out_shape=jax.ShapeDtypeStruct(dproj.shape, dproj.dtype),
        input_output_aliases={2: 0},
        compiler_params=_params(("parallel",)),
    )(dk, dv, dproj)


def _w_in_to_internal(w):
    return jnp.concatenate([w[:, 0:512], w[:, 2304:2816], w[:, 768:2304], w[:, 512:768], w[:, 2816:2824],
                            jnp.zeros((w.shape[0], D_IN_PAD - D_IN), w.dtype)], axis=1)


def _w_in_from_internal(g):
    return jnp.concatenate([g[:, 0:512], g[:, 2560:2816], g[:, 1024:2560], g[:, 512:1024], g[:, 2816:2824]], axis=1)


def _local_step(x, p, target, wts):
    S = x.shape[0]
    w_in, w_o, w_up, w_down, w_pg, w_pp = (wts[k] for k in ("w_in", "w_o", "w_up", "w_down", "w_ple_gate", "w_ple_proj"))
    cos, sin = _rope_tables(S)
    sinks, a_log, dt_bias = wts["sinks"].reshape(8), wts["a_log"].reshape(4), wts["dt_bias"].reshape(4)
    gn = wts["dn_norm"].reshape(1, DN_DIM)
    add = lambda acc, res: (acc + res,)

    u = _rmsnorm_fwd(x, wts["norm_mix"], "norm_mix_fwd")
    proj, = _mm_nn(u, w_in, name="in_proj", out_dtypes=[F32], tn=512)
    attn = _attn_fwd(proj, cos, sin, sinks)
    qkv = _dn_prep_fwd(proj, wts["conv_w"])
    cw, cu, cqg, ckd, ca, ct, gcs = _dn_chunk_fwd(qkv, proj, a_log, dt_bias)
    o, vnew, sst, dn_out = _dn_scan_fwd(cw, cu, cqg, ckd, ca, gcs, proj, gn)
    h1a, = _mm_nn(attn, w_o, name="out_proj_attn", out_dtypes=[F32], tn=512, epi=add, extra=[x], w_row_block=0)
    h1, = _mm_nn(dn_out, w_o, name="out_proj_dn", out_dtypes=[F32], tn=512, epi=add, extra=[h1a], w_row_block=1)
    m = _rmsnorm_fwd(h1, wts["norm_mlp"], "norm_mlp_fwd")

    def relu2(acc):
        r = jnp.maximum(acc, 0.0)
        return r * r, r

    hid, relu = _mm_nn(m, w_up, name="mlp_up", out_dtypes=[BF16, BF16], tn=512, epi=relu2)
    h2, = _mm_nn(hid, w_down, name="mlp_down", out_dtypes=[F32], tn=512, epi=add, extra=[h1])
    n3 = _rmsnorm_fwd(h2, wts["norm_ple"], "norm_ple_fwd")
    pp, = _mm_nn(p, w_pp, name="ple_proj", out_dtypes=[F32], tn=512)

    def ple(acc, h2_t, pp_t):
        gate = _sigmoid(acc)
        return h2_t + gate * pp_t, gate

    h3, gate = _mm_nn(n3, w_pg, name="ple_gate", out_dtypes=[F32, F32], tn=512, epi=ple, extra=[h2, pp])
    dh3, loss, d_norm_final = _final_loss(h3, wts["norm_final"].reshape(1, D_MODEL), target)

    g = {"norm_final": d_norm_final}
    dgl, dpp = _ple_bwd(dh3, pp, gate)
    g["w_ple_gate"] = _mm_tn(n3, dgl, name="d_w_ple_gate", tm=512, tn=512, out_dtype=BF16)
    g["w_ple_proj"] = _mm_tn(p, dpp, name="d_w_ple_proj", tm=256, tn=128, out_dtype=BF16, column_shards=True)
    dn3 = _mm_nt(dgl, w_pg, name="d_n3", out_dtype=F32, tn=512)
    dh2, g["norm_ple"] = _rmsnorm_bwd(h2, wts["norm_ple"], dn3, dh3, "norm_ple_bwd")
    d_act = _mm_nt(dh2, w_down, name="d_hidden", out_dtype=BF16, tn=512, epi=lambda acc, r: acc * (2.0 * r.astype(F32)), extra=[relu])
    g["w_down"] = _mm_tn(hid, dh2, name="d_w_down", tm=512, tn=512, out_dtype=BF16)
    g["w_up"] = _mm_tn(m, d_act, name="d_w_up", tm=512, tn=512, out_dtype=BF16, column_shards=True)
    dm = _mm_nt(d_act, w_up, name="d_m", out_dtype=F32, tn=512)
    dh1, g["norm_mlp"] = _rmsnorm_bwd(h1, wts["norm_mlp"], dm, dh2, "norm_mlp_bwd")
    dcat = _mm_nt(dh1, w_o, name="d_cat", out_dtype=F32, tn=512)
    g["w_o"] = jnp.concatenate([_mm_tn(attn, dh1, name="d_w_o_attn", tm=512, tn=512, out_dtype=BF16),
                                _mm_tn(dn_out, dh1, name="d_w_o_dn", tm=512, tn=512, out_dtype=BF16)], axis=0)
    dproj, dk, dv, dsinks = _attn_bwd(proj, cos, sin, sinks, dcat)
    g["sinks"] = dsinks[:, 0].reshape(1, 8)
    du_, dw_, dqg, dkd, da, dproj, dsc, g["dn_norm"] = _dn_scan_bwd(dcat, o, proj, gn, sst, vnew, cw, cqg, ckd, ca, gcs, dproj)
    dqkv, dproj, gate_acc = _dn_chunk_bwd(qkv, proj, ct, gcs, du_, dw_, dqg, dkd, da, dsc, a_log, dt_bias, dproj)
    g["a_log"], g["dt_bias"] = gate_acc[0:1, 0:4], gate_acc[1:2, 0:4]
    dproj, g["conv_w"] = _dn_prep_bwd(proj, wts["conv_w"], dqkv, dproj)
    dproj = _fill_kv(dk, dv, dproj)
    g["w_in"] = _mm_tn(u, dproj, name="d_w_in", tm=512, tn=512)
    du_in = _mm_nt(dproj, w_in, name="d_u", out_dtype=F32, tn=512)
    grad_x, g["norm_mix"] = _rmsnorm_bwd(x, wts["norm_mix"], du_in, dh1, "norm_mix_bwd")
    return loss, grad_x, g


def _peer(k):
    x, y, c = lax.axis_index("x"), lax.axis_index("y"), lax.axis_index("c")
    px = 1 - x if k & 4 else x
    py = 1 - y if k & 2 else y
    pc = 1 - c if k & 1 else c
    return (px, py, pc), 4 * px + 2 * py + pc


def _exchange(srcs, name, gather):
    n = len(srcs)
    shapes = [(N_DEV,) + s.shape if gather else s.shape for s in srcs]

    def body(*refs):
        src_refs, out_refs = refs[:n], refs[n:2 * n]
        send_sems, recv_sems, local_sems = refs[2 * n:]
        _, me = _peer(0)
        piece = (lambda a, d: src_refs[a]) if gather else (lambda a, d: src_refs[a].at[d])
        local = [pltpu.make_async_copy(piece(a, me), out_refs[a].at[me], local_sems.at[a]) for a in range(n)]
        for cp in local:
            cp.start()
        copies = []
        for a in range(n):
            for k in range(1, N_DEV):
                dev, idx = _peer(k)
                cp = pltpu.make_async_remote_copy(src_ref=piece(a, idx), dst_ref=out_refs[a].at[me],
                                                  send_sem=send_sems.at[a, k - 1], recv_sem=recv_sems.at[a, k - 1],
                                                  device_id=dev, device_id_type=MESH)
                cp.start()
                copies.append(cp)
        for cp in copies:
            cp.wait_recv()
        for cp in copies:
            cp.wait_send()
        for cp in local:
            cp.wait()

    anywhere = pl.BlockSpec(memory_space=pl.ANY)
    return pl.pallas_call(
        body, name=name, in_specs=[anywhere] * n, out_specs=[anywhere] * n,
        out_shape=[jax.ShapeDtypeStruct(shp, s.dtype) for shp, s in zip(shapes, srcs)],
        scratch_shapes=[pltpu.SemaphoreType.DMA((n, N_DEV - 1)), pltpu.SemaphoreType.DMA((n, N_DEV - 1)),
                        pltpu.SemaphoreType.DMA((n,))],
    )(*srcs)


def _adamw(parts, w, m, v, name):
    n, R, W = parts.shape
    tm = 128 if R % 128 == 0 else R

    def body(p_ref, w_ref, m_ref, v_ref, g_ref, d_ref, nm_ref, nv_ref):
        g = p_ref[0].astype(F32)
        for s in range(1, n):
            g = g + p_ref[s].astype(F32)
        nm = ADAM_B1 * m_ref[...] + (1.0 - ADAM_B1) * g
        nv = ADAM_B2 * v_ref[...] + (1.0 - ADAM_B2) * (g * g)
        m_hat = nm / (1.0 - ADAM_B1 ** ADAM_STEP)
        v_hat = nv / (1.0 - ADAM_B2 ** ADAM_STEP)
        g_ref[...] = g
        d_ref[...] = -ADAM_LR * (m_hat / (jnp.sqrt(v_hat) + ADAM_EPS) + ADAM_WD * w_ref[...])
        nm_ref[...] = nm
        nv_ref[...] = nv

    tile = pl.BlockSpec((tm, W), lambda i: (i, 0))
    return pl.pallas_call(
        body, grid=(R // tm,), name=name,
        in_specs=[pl.BlockSpec((n, tm, W), lambda i: (0, i, 0)), tile, tile, tile],
        out_specs=[tile] * 4, out_shape=[jax.ShapeDtypeStruct((R, W), F32)] * 4,
        compiler_params=_params(("parallel",)),
    )(parts, w, m, v)


_MATRICES = ("w_in", "w_o", "w_up", "w_down", "w_ple_gate", "w_ple_proj")


def _full_weights(gathered):
    w_in = jnp.transpose(gathered["w_in"], (1, 0, 2)).reshape(D_MODEL, D_IN)
    return dict(
        w_in=_w_in_to_internal(w_in),
        w_o=gathered["w_o"].reshape(1024, 1024),
        w_up=jnp.transpose(gathered["w_up"], (1, 0, 2)).reshape(1024, 4096),
        w_down=gathered["w_down"].reshape(4096, 1024),
        w_ple_gate=gathered["w_ple_gate"].reshape(1024, 1024),
        w_ple_proj=jnp.transpose(gathered["w_ple_proj"], (1, 0, 2)).reshape(256, 1024))


def _grad_pieces(g):
    w_in = _w_in_from_internal(g["w_in"]).reshape(D_MODEL, N_DEV, D_IN // N_DEV)
    return dict(
        w_in=_bf(jnp.transpose(w_in, (1, 0, 2))), w_o=g["w_o"].reshape(N_DEV, 128, 1024), w_up=g["w_up"],
        w_down=g["w_down"].reshape(N_DEV, 512, 1024), w_ple_gate=g["w_ple_gate"].reshape(N_DEV, 128, 1024),
        w_ple_proj=g["w_ple_proj"])


_SMALL_ROWS = 16
_VEC_ROW = {"norm_mix": 0, "norm_mlp": 1, "norm_ple": 2, "norm_final": 3}
_VEC_LANES = {"a_log": (0, 4), "dt_bias": (4, 8), "sinks": (8, 16), "dn_norm": (128, 256)}
_LOSS_ROW, _CONV_ROW = 5, 8


def _pack_small(vals, extra_rows):
    row4 = jnp.zeros((1024,), F32)
    for n, (a, b) in _VEC_LANES.items():
        row4 = row4.at[a:b].set(vals[n].reshape(b - a))
    rows = [vals[n].reshape(1, 1024) for n in ("norm_mix", "norm_mlp", "norm_ple", "norm_final")] + [row4.reshape(1, 1024)]
    return jnp.concatenate(rows + extra_rows, axis=0)


def _unpack_small(buf, like):
    out = {n: buf[r].reshape(like[n].shape) for n, r in _VEC_ROW.items()}
    for n, (a, b) in _VEC_LANES.items():
        out[n] = buf[4, a:b].reshape(like[n].shape)
    return out


_ORDER = ("norm_mix", "w_in", "conv_w", "a_log", "dt_bias", "dn_norm", "sinks", "w_o", "norm_mlp", "w_up", "w_down",
          "norm_ple", "w_ple_gate", "w_ple_proj", "norm_final")


def kernel(x, p, norm_mix, w_in, conv_w, a_log, dt_bias, dn_norm, sinks, w_o, norm_mlp, w_up, w_down, norm_ple, w_ple_gate, w_ple_proj, norm_final, loss_target, m_norm_mix, m_w_in, m_conv_w, m_a_log, m_dt_bias, m_dn_norm, m_sinks, m_w_o, m_norm_mlp, m_w_up, m_w_down, m_norm_ple, m_w_ple_gate, m_w_ple_proj, m_norm_final, v_norm_mix, v_w_in, v_conv_w, v_a_log, v_dt_bias, v_dn_norm, v_sinks, v_w_o, v_norm_mlp, v_w_up, v_w_down, v_norm_ple, v_w_ple_gate, v_w_ple_proj, v_norm_final):
    w = dict(norm_mix=norm_mix, w_in=w_in[0], conv_w=conv_w[0], a_log=a_log, dt_bias=dt_bias, dn_norm=dn_norm, sinks=sinks,
             w_o=w_o[0], norm_mlp=norm_mlp, w_up=w_up[0], w_down=w_down[0], norm_ple=norm_ple, w_ple_gate=w_ple_gate[0],
             w_ple_proj=w_ple_proj[0], norm_final=norm_final)
    m = dict(norm_mix=m_norm_mix, w_in=m_w_in[0], conv_w=m_conv_w[0], a_log=m_a_log, dt_bias=m_dt_bias, dn_norm=m_dn_norm,
             sinks=m_sinks, w_o=m_w_o[0], norm_mlp=m_norm_mlp, w_up=m_w_up[0], w_down=m_w_down[0], norm_ple=m_norm_ple,
             w_ple_gate=m_w_ple_gate[0], w_ple_proj=m_w_ple_proj[0], norm_final=m_norm_final)
    v = dict(norm_mix=v_norm_mix, w_in=v_w_in[0], conv_w=v_conv_w[0], a_log=v_a_log, dt_bias=v_dt_bias, dn_norm=v_dn_norm,
             sinks=v_sinks, w_o=v_w_o[0], norm_mlp=v_norm_mlp, w_up=v_w_up[0], w_down=v_w_down[0], norm_ple=v_norm_ple,
             w_ple_gate=v_w_ple_gate[0], w_ple_proj=v_w_ple_proj[0], norm_final=v_norm_final)
    me = 4 * lax.axis_index("x") + 2 * lax.axis_index("y") + lax.axis_index("c")
    conv_shard = conv_w.shape[2]

    conv_pad = jnp.pad(w["conv_w"], ((0, 8 - DN_CONV), (0, 256 - conv_shard)))
    *gathered, conv_all = _exchange([_bf(w[n]) for n in _MATRICES] + [conv_pad], "gather_weights", gather=True)
    full = _full_weights(dict(zip(_MATRICES, gathered)))
    conv_all = conv_all[:, :DN_CONV, :conv_shard]
    full["conv_w"] = jnp.transpose(conv_all, (1, 0, 2)).reshape(DN_CONV, N_DEV * conv_shard)
    for n in _ORDER:
        full.setdefault(n, w[n])

    loss, grad_x, g = _local_step(x[0], p[0, 0], loss_target[0], full)

    pieces = _grad_pieces(g)
    received = _exchange([pieces[n] for n in _MATRICES], "scatter_grads", gather=False)
    big = {n: _adamw(r, w[n], m[n], v[n], "adamw_" + n) for n, r in zip(_MATRICES, received)}
    small = _pack_small(g, [loss[:, :1] * jnp.ones((1, 1024), F32), jnp.zeros((2, 1024), F32),
                            g["conv_w"].reshape(6, 1024), jnp.zeros((2, 1024), F32)])
    small_all, = _exchange([small], "gather_small", gather=True)
    zeros16 = jnp.zeros((_SMALL_ROWS, 1024), F32)
    summed = _adamw(small_all, zeros16, zeros16, zeros16, "sum_small")[0]
    conv_g = lax.dynamic_slice(summed[_CONV_ROW:_CONV_ROW + 6].reshape(DN_CONV, N_DEV * conv_shard), (0, me * conv_shard),
                               (DN_CONV, conv_shard))
    pad_conv = lambda t: jnp.pad(t.reshape(1, DN_CONV * conv_shard), ((0, 2), (0, 1024 - DN_CONV * conv_shard)))
    small_g = jnp.concatenate([summed[0:5], pad_conv(conv_g)], axis=0)[None]
    pack8 = lambda d: _pack_small(d, [pad_conv(d["conv_w"])])
    sm = _adamw(small_g, pack8(w), pack8(m), pack8(v), "adamw_vectors")

    outs = []
    for i, small_buf in enumerate(sm):
        d = {n: big[n][i] for n in _MATRICES}
        d.update(_unpack_small(small_buf, w))
        d["conv_w"] = small_buf[5, :DN_CONV * conv_shard].reshape(DN_CONV, conv_shard)
        outs.append(d)
    result = [summed[_LOSS_ROW, 0], grad_x[None]]
    for d in outs:
        for n in _ORDER:
            result.append(d[n].reshape(w[n].shape)[None] if n in _MATRICES or n == "conv_w" else d[n].reshape(w[n].shape))
    return tuple(result)
```

```python
import jax
import jax.numpy as jnp
from jax import lax
from jax.experimental import pallas as pl
from jax.experimental.pallas import tpu as pltpu

F32, BF16 = jnp.float32, jnp.bfloat16
EPS = 1e-6
D_MODEL = 1024
N_DEV = 8
ATTN_BLOCK = 128
HEAD_PAIR = 128
DN_HEADS = 4
DN_DIM = 128
DN_CHUNK = 64
DN_CONV = 4
ROPE_THETA = 10000.0
D_IN = 2824
D_IN_PAD = 3072
BLK_Q, BLK_Z = 0, 1
BLK_DN, BLK_K, BLK_V, BLK_G = 8, 20, 21, 22
BLK_KV, BLK_G_PAD = 10, 11
VMEM_LIMIT = 48 * 1024 * 1024
NEG = -1e30
ADAM_LR, ADAM_B1, ADAM_B2, ADAM_EPS, ADAM_WD, ADAM_STEP = 0.001, 0.9, 0.999, 1e-08, 0.01, 10
MESH = pl.DeviceIdType.MESH


def _bf(x):
    return x.astype(BF16)


def _dot(a, b):
    return jnp.dot(a, b, preferred_element_type=F32)


def _dot_nt(a, b):
    return lax.dot_general(a, b, (((1,), (1,)), ((), ())), preferred_element_type=F32)


def _dot_tn(a, b):
    return lax.dot_general(a, b, (((0,), (0,)), ((), ())), preferred_element_type=F32)


def _sigmoid(x):
    return 1.0 / (1.0 + jnp.exp(-x))


def _params(sem):
    return pltpu.CompilerParams(dimension_semantics=sem, vmem_limit_bytes=VMEM_LIMIT)


def _mm_nn(x, w, *, name, out_dtypes, tn, epi=None, extra=(), tm=512, w_row_block=0):
    S, K = x.shape
    N = w.shape[1]
    rb = w_row_block
    tm = min(tm, S)
    n_extra = len(extra)

    def body(x_ref, w_ref, *rest):
        acc = _dot(_bf(x_ref[...]), w_ref[...])
        res = epi(acc, *[r[...] for r in rest[:n_extra]]) if epi else (acc,)
        for o, r in zip(rest[n_extra:], res):
            o[...] = r.astype(o.dtype)

    tile = pl.BlockSpec((tm, tn), lambda i, j: (i, j))
    return pl.pallas_call(
        body, grid=(S // tm, N // tn), name=name,
        in_specs=[pl.BlockSpec((tm, K), lambda i, j: (i, 0)), pl.BlockSpec((K, tn), lambda i, j: (rb, j))] + [tile] * n_extra,
        out_specs=[tile] * len(out_dtypes),
        out_shape=[jax.ShapeDtypeStruct((S, N), dt) for dt in out_dtypes],
        compiler_params=_params(("parallel", "parallel")),
    )(x, w, *extra)


def _mm_nt(dy, w, *, name, out_dtype, tn, epi=None, extra=(), tm=512):
    S, N = dy.shape
    K = w.shape[0]
    tm = min(tm, S)
    n_extra = len(extra)

    def body(dy_ref, w_ref, *rest):
        acc = _dot_nt(_bf(dy_ref[...]), w_ref[...])
        if epi:
            acc = epi(acc, *[r[...] for r in rest[:n_extra]])
        rest[n_extra][...] = acc.astype(out_dtype)

    tile = pl.BlockSpec((tm, tn), lambda i, j: (i, j))
    return pl.pallas_call(
        body, grid=(S // tm, K // tn), name=name,
        in_specs=[pl.BlockSpec((tm, N), lambda i, j: (i, 0)), pl.BlockSpec((tn, N), lambda i, j: (j, 0))] + [tile] * n_extra,
        out_specs=tile,
        out_shape=jax.ShapeDtypeStruct((S, K), out_dtype),
        compiler_params=_params(("parallel", "parallel")),
    )(dy, w, *extra)


def _mm_tn(x, dy, *, name, tm, tn, out_dtype=F32, column_shards=False):
    S, K = x.shape
    N = dy.shape[1]

    def body(x_ref, dy_ref, o_ref):
        o_ref[...] = _dot_tn(_bf(x_ref[...]), _bf(dy_ref[...])).astype(out_dtype)

    if column_shards:
        out_spec = pl.BlockSpec((None, tm, tn), lambda i, j: (j, i, 0))
        out_shape = jax.ShapeDtypeStruct((N // tn, K, tn), out_dtype)
    else:
        out_spec = pl.BlockSpec((tm, tn), lambda i, j: (i, j))
        out_shape = jax.ShapeDtypeStruct((K, N), out_dtype)
    return pl.pallas_call(
        body, grid=(K // tm, N // tn), name=name,
        in_specs=[pl.BlockSpec((S, tm), lambda i, j: (0, i)), pl.BlockSpec((S, tn), lambda i, j: (0, j))],
        out_specs=out_spec, out_shape=out_shape,
        compiler_params=_params(("parallel", "parallel")),
    )(x, dy)


def _rowwise(body, *, tiled, full, out_tiled, out_acc, name, tm=512, smem=()):
    S = tiled[0].shape[0]
    tm = min(tm, S)
    n_in = len(smem) + len(tiled) + len(full)

    def kern(*refs):
        @pl.when(pl.program_id(0) == 0)
        def _():
            for r in refs[n_in + len(out_tiled):]:
                r[...] = jnp.zeros_like(r)
        body(*refs)

    in_specs = [pl.BlockSpec(memory_space=pltpu.SMEM) for _ in smem]
    in_specs += [pl.BlockSpec((tm, a.shape[1]), lambda i: (i, 0)) for a in tiled]
    in_specs += [pl.BlockSpec(a.shape, lambda i, nd=a.ndim: (0,) * nd) for a in full]
    out_specs = [pl.BlockSpec((tm, w), lambda i: (i, 0)) for w, _ in out_tiled]
    out_specs += [pl.BlockSpec(shp, lambda i, nd=len(shp): (0,) * nd) for shp, _ in out_acc]
    out_shape = [jax.ShapeDtypeStruct((S, w), dt) for w, dt in out_tiled]
    out_shape += [jax.ShapeDtypeStruct(shp, dt) for shp, dt in out_acc]
    return pl.pallas_call(
        kern, grid=(S // tm,), name=name, in_specs=in_specs, out_specs=out_specs, out_shape=out_shape,
        compiler_params=_params(("arbitrary",)),
    )(*smem, *tiled, *full)


def _rms_stats(x):
    r = lax.rsqrt(jnp.mean(x * x, axis=-1, keepdims=True) + EPS)
    return r, x * r


def _rmsnorm_fwd(x, g, name):
    def body(x_ref, g_ref, o_ref):
        _, xh = _rms_stats(x_ref[...])
        o_ref[...] = _bf(xh * g_ref[...])

    return _rowwise(body, tiled=[x], full=[g], out_tiled=[(x.shape[1], BF16)], out_acc=[], name=name)[0]


def _rms_bwd_tile(x, g, dxn):
    r, xh = _rms_stats(x)
    dg = jnp.sum(dxn * xh, axis=0, keepdims=True)
    dn = dxn * g
    dx = r * (dn - xh * jnp.mean(dn * xh, axis=-1, keepdims=True))
    return dx, dg


def _rmsnorm_bwd(x, g, dxn, dres, name):
    def body(x_ref, dxn_ref, dres_ref, g_ref, dx_ref, dg_ref):
        dx, dg = _rms_bwd_tile(x_ref[...], g_ref[...], dxn_ref[...])
        dx_ref[...] = dres_ref[...] + dx
        dg_ref[...] += dg

    n = x.shape[1]
    return _rowwise(body, tiled=[x, dxn, dres], full=[g], out_tiled=[(n, F32)], out_acc=[((1, n), F32)], name=name)


def _final_loss(h3, g, target):
    n = h3.shape[1]

    def body(h_ref, t_ref, g_ref, dh_ref, loss_ref, dg_ref):
        x = h_ref[...]
        _, xh = _rms_stats(x)
        e = xh * g_ref[...] - t_ref[...]
        per_tok = jnp.mean(e * e, axis=-1, keepdims=True)
        loss_ref[...] += 0.5 * jnp.sum(per_tok, axis=0, keepdims=True)
        dx, dg = _rms_bwd_tile(x, g_ref[...], e * (1.0 / n))
        dh_ref[...] = dx
        dg_ref[...] += dg

    return _rowwise(body, tiled=[h3, target], full=[g], out_tiled=[(n, F32)],
                    out_acc=[((1, 128), F32), ((1, n), F32)], name="final_loss")


def _ple_bwd(dh3, pp, gate):
    def body(dh_ref, pp_ref, gate_ref, dgl_ref, dpp_ref):
        dh, gt = dh_ref[...], gate_ref[...]
        dgl_ref[...] = _bf(dh * pp_ref[...] * gt * (1.0 - gt))
        dpp_ref[...] = _bf(dh * gt)

    n = dh3.shape[1]
    return _rowwise(body, tiled=[dh3, pp, gate], full=[], out_tiled=[(n, BF16), (n, BF16)], out_acc=[], name="ple_bwd")


def _rope_tables(S):
    half = 32
    inv = 1.0 / (ROPE_THETA ** (jnp.arange(half, dtype=F32) * (2.0 / 64)))
    ang = jnp.arange(S).astype(F32)[:, None] * inv[None, :]
    cos, sin = jnp.cos(ang), jnp.sin(ang)
    return jnp.tile(cos, (1, 4)), jnp.concatenate([-sin, sin, -sin, sin], axis=1)


def _attn_common(i, kc, kp, vc, vp, cc, sc, cp, sp):
    lane = lax.broadcasted_iota(jnp.int32, (1, HEAD_PAIR), 1)
    lane_lo = jnp.bitwise_and(lane, 63) < 32
    slot = [lane < 64, lane >= 64]

    def swap_halves(t):
        return jnp.where(lane_lo, pltpu.roll(t, 96, 1), pltpu.roll(t, 32, 1))

    def rope(t, cos, sin):
        return t * cos + swap_halves(t) * sin

    def unrope(d, cos, sin):
        return d * cos + swap_halves(d * sin)

    k2 = jnp.concatenate([rope(kp, cp, sp), rope(kc, cc, sc)], axis=0)
    v2 = jnp.concatenate([vp, vc], axis=0)
    r = lax.broadcasted_iota(jnp.int32, (ATTN_BLOCK, 2 * ATTN_BLOCK), 0)
    c = lax.broadcasted_iota(jnp.int32, (ATTN_BLOCK, 2 * ATTN_BLOCK), 1)
    valid = (c > r) & (c <= r + ATTN_BLOCK) & jnp.logical_or(c >= ATTN_BLOCK, i > 0)
    ks, vs = {}, {}
    for j in range(2):
        kn = jnp.where(slot[j], k2, 0.0)
        vn = jnp.where(slot[j], v2, 0.0)
        for s in range(2):
            ks[j, s] = _bf(kn if s == j else pltpu.roll(kn, 64, 1))
            vs[j, s] = _bf(vn if s == j else pltpu.roll(vn, 64, 1))
    return slot, rope, unrope, valid, ks, vs


def _attn_probs(scores, valid, sink):
    s = jnp.where(valid, scores * 0.125, NEG)
    m = jnp.maximum(jnp.max(s, axis=1, keepdims=True), sink)
    e = jnp.exp(s - m)
    inv_z = 1.0 / (jnp.sum(e, axis=1, keepdims=True) + jnp.exp(sink - m))
    return e * inv_z, jnp.exp(sink - m) * inv_z


def _attn_specs(S):
    nb = S // ATTN_BLOCK
    prev = lambda i: jnp.maximum(i - 1, 0)
    blk = lambda w, col, row=(lambda i: i): pl.BlockSpec((ATTN_BLOCK, w), lambda i: (row(i), col))
    in_specs = [pl.BlockSpec(memory_space=pltpu.SMEM),
                blk(512, BLK_Q), blk(128, BLK_K), blk(128, BLK_K, prev), blk(128, BLK_V), blk(128, BLK_V, prev),
                blk(128, 0), blk(128, 0), blk(128, 0, prev), blk(128, 0, prev)]
    return nb, in_specs


def _attn_fwd(pa, cos, sin, sinks):
    S = pa.shape[0]
    nb, in_specs = _attn_specs(S)

    def body(sinks_ref, q_ref, kc_ref, kp_ref, vc_ref, vp_ref, cc_ref, sc_ref, cp_ref, sp_ref, o_ref):
        i = pl.program_id(0)
        cc, sc = cc_ref[...], sc_ref[...]
        _, rope, _, valid, ks, vs = _attn_common(i, kc_ref[...], kp_ref[...], vc_ref[...], vp_ref[...],
                                                 cc, sc, cp_ref[...], sp_ref[...])
        pair_cols = [slice(HEAD_PAIR * pair, HEAD_PAIR * (pair + 1)) for pair in range(4)]
        qps = [_bf(rope(q_ref[:, cols], cc, sc)) for cols in pair_cols]
        outs = {}

        def head_program(h):
            pair, s = divmod(h, 2)
            j = h // 4
            scores = _dot_nt(qps[pair], ks[j, s])
            yield
            p, _ = _attn_probs(scores, valid, sinks_ref[h])
            outs[h] = _dot(_bf(p), vs[j, s])

        _interleave(head_program(h) for h in range(8))
        for pair, cols in enumerate(pair_cols):
            o_ref[:, cols] = outs[2 * pair] + outs[2 * pair + 1]

    return pl.pallas_call(
        body, grid=(nb,), name="attn_fwd", in_specs=in_specs,
        out_specs=pl.BlockSpec((ATTN_BLOCK, 512), lambda i: (i, 0)),
        out_shape=jax.ShapeDtypeStruct((S, 512), F32),
        compiler_params=_params(("parallel",)),
    )(sinks, pa, pa, pa, pa, pa, cos, sin, cos, sin)


def _attn_bwd(pa, cos, sin, sinks, dcat):
    S = pa.shape[0]
    nb, in_specs = _attn_specs(S)
    in_specs = in_specs + [pl.BlockSpec((ATTN_BLOCK, 512), lambda i: (i, 0))]

    def body(sinks_ref, q_ref, kc_ref, kp_ref, vc_ref, vp_ref, cc_ref, sc_ref, cp_ref, sp_ref, do_ref,
             dq_ref, dk_ref, dv_ref, dsink_ref):
        i = pl.program_id(0)

        @pl.when(i == 0)
        def _():
            dk_ref[...] = jnp.zeros_like(dk_ref)
            dv_ref[...] = jnp.zeros_like(dv_ref)
            dsink_ref[...] = jnp.zeros_like(dsink_ref)

        cc, sc, cp, sp = cc_ref[...], sc_ref[...], cp_ref[...], sp_ref[...]
        slot, rope, unrope, valid, ks, vs = _attn_common(i, kc_ref[...], kp_ref[...], vc_ref[...], vp_ref[...], cc, sc, cp, sp)
        pair_cols = [slice(HEAD_PAIR * pair, HEAD_PAIR * (pair + 1)) for pair in range(4)]
        qps = [_bf(rope(q_ref[:, cols], cc, sc)) for cols in pair_cols]
        dobs = [_bf(do_ref[:, cols]) for cols in pair_cols]
        dqs, dks, dvs = {}, {}, {}

        def head_program(h):
            pair, s = divmod(h, 2)
            j = h // 4
            qp, dob = qps[pair], dobs[pair]
            scores = _dot_nt(qp, ks[j, s])
            dp = _dot_nt(dob, vs[j, s])
            yield
            p, p_sink = _attn_probs(scores, valid, sinks_ref[h])
            dr = jnp.sum(p * dp, axis=1, keepdims=True)
            ds = _bf(p * (dp - dr) * 0.125)
            dsink_ref[h:h + 1, :] += -jnp.sum(p_sink * dr, axis=0, keepdims=True)
            dqs[h] = _dot(ds, ks[j, s])
            dk_h = _dot_tn(ds, qp)
            dv_h = _dot_tn(_bf(p), dob)
            yield
            dk_h, dv_h = jnp.where(slot[s], dk_h, 0.0), jnp.where(slot[s], dv_h, 0.0)
            if s != j:
                dk_h, dv_h = pltpu.roll(dk_h, 64, 1), pltpu.roll(dv_h, 64, 1)
            dks[h], dvs[h] = dk_h, dv_h

        _interleave(head_program(h) for h in range(8))
        dk2 = sum((dks[h] for h in range(1, 8)), dks[0])
        dv2 = sum((dvs[h] for h in range(1, 8)), dvs[0])
        for pair, cols in enumerate(pair_cols):
            dq_ref[:, cols] = _bf(unrope(dqs[2 * pair] + dqs[2 * pair + 1], cc, sc))
        cur = pl.ds(pl.multiple_of(i * ATTN_BLOCK, ATTN_BLOCK), ATTN_BLOCK)
        dk_ref[cur, :] += unrope(dk2[ATTN_BLOCK:], cc, sc)
        dv_ref[cur, :] += dv2[ATTN_BLOCK:]

        @pl.when(i > 0)
        def _():
            prv = pl.ds(pl.multiple_of((i - 1) * ATTN_BLOCK, ATTN_BLOCK), ATTN_BLOCK)
            dk_ref[prv, :] += unrope(dk2[:ATTN_BLOCK], cp, sp)
            dv_ref[prv, :] += dv2[:ATTN_BLOCK]

    whole = lambda w: pl.BlockSpec((S, w), lambda i: (0, 0))
    return pl.pallas_call(
        body, grid=(nb,), name="attn_bwd", in_specs=in_specs,
        out_specs=[pl.BlockSpec((ATTN_BLOCK, 512), lambda i: (i, BLK_Q)), whole(128), whole(128),
                   pl.BlockSpec((8, 128), lambda i: (0, 0))],
        out_shape=[jax.ShapeDtypeStruct((S, D_IN_PAD), BF16), jax.ShapeDtypeStruct((S, 128), F32),
                   jax.ShapeDtypeStruct((S, 128), F32), jax.ShapeDtypeStruct((8, 128), F32)],
        compiler_params=_params(("arbitrary",)),
    )(sinks, pa, pa, pa, pa, pa, cos, sin, cos, sin, dcat)


CONV_ROWS = 512
CONV_PAD = 8


def _conv_silu(scr, w, r0):
    y = w[3:4, :] * scr[pl.ds(CONV_PAD + r0, CONV_ROWS), :]
    for j in range(DN_CONV - 1):
        y = y + w[j:j + 1, :] * scr[pl.ds(CONV_PAD + r0 - 3 + j, CONV_ROWS), :]
    return y


def _dn_prep_fwd(pd, conv_w):
    S = pd.shape[0]
    assert S % CONV_ROWS == 0

    def body(x_ref, w_ref, o_ref, scr):
        b = pl.program_id(0)
        scr[0:CONV_PAD, :] = jnp.zeros((CONV_PAD, DN_DIM), F32)
        scr[pl.ds(CONV_PAD, S), :] = x_ref[...]
        w = w_ref[...]
        q_scale = jnp.where(b < DN_HEADS, DN_DIM ** -0.5, 1.0)
        for r0 in range(0, S, CONV_ROWS):
            y = _conv_silu(scr, w, r0)
            a = y * _sigmoid(y)
            rs = lax.rsqrt(jnp.sum(a * a, axis=1, keepdims=True) + EPS)
            o_ref[pl.ds(r0, CONV_ROWS), :] = a * jnp.where(b < 2 * DN_HEADS, rs * q_scale, 1.0)

    col = pl.BlockSpec((S, DN_DIM), lambda b: (0, b))
    return pl.pallas_call(
        body, grid=(3 * DN_HEADS,), name="dn_prep_fwd",
        in_specs=[pl.BlockSpec((S, DN_DIM), lambda b: (0, BLK_DN + b)), pl.BlockSpec((DN_CONV, DN_DIM), lambda b: (0, b))],
        out_specs=col,
        out_shape=jax.ShapeDtypeStruct((S, 3 * DN_HEADS * DN_DIM), F32),
        scratch_shapes=[pltpu.VMEM((S + CONV_PAD, DN_DIM), F32)],
        compiler_params=_params(("parallel",)),
    )(pd, conv_w)


def _dn_prep_bwd(pd, conv_w, dqkv, dproj):
    S = pd.shape[0]

    def body(x_ref, w_ref, d_ref, _, dx_ref, dw_ref, scr, dscr):
        b = pl.program_id(0)
        scr[0:CONV_PAD, :] = jnp.zeros((CONV_PAD, DN_DIM), F32)
        scr[pl.ds(CONV_PAD, S), :] = x_ref[...]
        dscr[pl.ds(S, CONV_PAD), :] = jnp.zeros((CONV_PAD, DN_DIM), F32)
        w = w_ref[...]
        q_scale = jnp.where(b < DN_HEADS, DN_DIM ** -0.5, 1.0)
        is_qk = b < 2 * DN_HEADS
        dw = [jnp.zeros((1, DN_DIM), F32) for _ in range(DN_CONV)]
        for r0 in range(0, S, CONV_ROWS):
            y = _conv_silu(scr, w, r0)
            sg = _sigmoid(y)
            a = y * sg
            dout = d_ref[pl.ds(r0, CONV_ROWS), :]
            rs = lax.rsqrt(jnp.sum(a * a, axis=1, keepdims=True) + EPS)
            da_qk = q_scale * rs * (dout - a * (rs * rs) * jnp.sum(dout * a, axis=1, keepdims=True))
            dy = jnp.where(is_qk, da_qk, dout) * (sg * (1.0 + y * (1.0 - sg)))
            dscr[pl.ds(r0, CONV_ROWS), :] = dy
            for j in range(DN_CONV):
                dw[j] = dw[j] + jnp.sum(dy * scr[pl.ds(CONV_PAD + r0 - 3 + j, CONV_ROWS), :], axis=0, keepdims=True)
        for j in range(DN_CONV):
            dw_ref[j:j + 1, :] = dw[j]
        for r0 in range(0, S, CONV_ROWS):
            dx = w[3:4, :] * dscr[pl.ds(r0, CONV_ROWS), :]
            for j in range(DN_CONV - 1):
                dx = dx + w[j:j + 1, :] * dscr[pl.ds(r0 + 3 - j, CONV_ROWS), :]
            dx_ref[pl.ds(r0, CONV_ROWS), :] = _bf(dx)

    col = pl.BlockSpec((S, DN_DIM), lambda b: (0, b))
    proj_col = pl.BlockSpec((S, DN_DIM), lambda b: (0, BLK_DN + b))
    wcol = pl.BlockSpec((DN_CONV, DN_DIM), lambda b: (0, b))
    return pl.pallas_call(
        body, grid=(3 * DN_HEADS,), name="dn_prep_bwd",
        in_specs=[proj_col, wcol, col, pl.BlockSpec(memory_space=pl.ANY)], out_specs=[proj_col, wcol],
        out_shape=[jax.ShapeDtypeStruct(dproj.shape, dproj.dtype), jax.ShapeDtypeStruct((DN_CONV, 3 * DN_HEADS * DN_DIM), F32)],
        scratch_shapes=[pltpu.VMEM((S + CONV_PAD, DN_DIM), F32), pltpu.VMEM((S + CONV_PAD, DN_DIM), F32)],
        input_output_aliases={3: 0},
        compiler_params=_params(("parallel",)),
    )(pd, conv_w, dqkv, dproj)


CPAD = 128


def _chunk_masks():
    ii = lax.broadcasted_iota(jnp.int32, (DN_CHUNK, CPAD), 0)
    jj = lax.broadcasted_iota(jnp.int32, (DN_CHUNK, CPAD), 1)
    return ii, jj


def _rows_pad(a):
    return jnp.concatenate([a, jnp.zeros_like(a)], axis=0)


def _hi_lo(a):
    hi = _bf(a)
    return hi, _bf(a - hi.astype(F32))


def _double_step(t, p):
    C = DN_CHUNK
    th, tl = _hi_lo(t)
    ph, pl_ = _hi_lo(p)
    r1 = _dot(jnp.concatenate([th, tl, ph, pl_], axis=0), _rows_pad(ph))
    r2 = _dot(jnp.concatenate([th, ph], axis=0), _rows_pad(pl_))
    return t + (r1[:C] + r1[C:2 * C] + r2[:C]), r1[2 * C:3 * C] + r1[3 * C:] + r2[C:]


def _dot3_nt(a, b):
    C = DN_CHUNK
    ah, al = _hi_lo(a)
    bh, bl = _hi_lo(b)
    r1 = _dot_nt(jnp.concatenate([ah, al], axis=0), _rows_pad(bh))
    return r1[:C] + r1[C:] + _dot_nt(ah, _rows_pad(bl))


def _dot3_tn(a, b):
    C = DN_CHUNK
    ah, al = _hi_lo(a)
    bh, bl = _hi_lo(b)
    return _dot_tn(ah, bh)[:C] + _dot_tn(al, bh)[:C] + _dot_tn(ah, bl)[:C]


def _interleave(programs):
    programs = list(programs)
    while programs:
        alive = []
        for prog in programs:
            try:
                next(prog)
                alive.append(prog)
            except StopIteration:
                pass
        programs = alive


def _col_to_row(col, ii, jj):
    return jnp.sum(jnp.where(ii == jj, col, 0.0), axis=0, keepdims=True)


def _row_to_col(row, ii, jj):
    return jnp.sum(jnp.where(ii == jj, row, 0.0), axis=1, keepdims=True)


def _decay(gc_col, ii, jj):
    diff = gc_col - _col_to_row(gc_col, ii, jj)
    return jnp.where(jj <= ii, jnp.exp(jnp.where(jj <= ii, diff, 0.0)), 0.0)


def _softplus(x):
    return jnp.maximum(x, 0.0) + jnp.log(1.0 + jnp.exp(-jnp.abs(x)))


def _head(h):
    return slice(DN_DIM * h, DN_DIM * (h + 1))


def _dn_chunk_fwd(qkv, pg, a_log, dt_bias):
    S = qkv.shape[0]
    C = DN_CHUNK
    nc = S // C

    def body(alog_ref, dtb_ref, qkv_ref, pg_ref, w_ref, u_ref, qg_ref, kd_ref, a_ref, t_ref, gcs_ref):
        ii, jj = _chunk_masks()
        lane = lax.broadcasted_iota(jnp.int32, (1, 128), 1)
        eye = (ii == jj).astype(F32)
        gcs_parts = []

        def head_program(h):
            q, k, v = qkv_ref[:, _head(h)], qkv_ref[:, _head(DN_HEADS + h)], qkv_ref[:, _head(2 * DN_HEADS + h)]
            beta = _sigmoid(pg_ref[:, h:h + 1])
            g_col = -jnp.exp(alog_ref[h]) * _softplus(pg_ref[:, DN_HEADS + h:DN_HEADS + h + 1] + dtb_ref[h])
            g_row = _col_to_row(g_col, ii, jj)
            gc_col = jnp.sum(jnp.where(jj <= ii, g_row, 0.0), axis=1, keepdims=True)
            dec = _decay(gc_col, ii, jj)
            eg = jnp.exp(gc_col)
            kb, vb = k * beta, v * beta
            k_rows = _rows_pad(_bf(k))
            kk = _dot_nt(_bf(kb), k_rows)
            qk = _dot_nt(_bf(q), k_rows)
            yield
            t, pw = eye, -jnp.where(jj < ii, kk * dec, 0.0)
            for _ in range(6):
                t, pw = _double_step(t, pw)
                yield
            tb = _bf(t)
            u_ref[:, _head(h)] = _dot(tb, _rows_pad(_bf(vb)))
            w_ref[:, _head(h)] = _dot(tb, _rows_pad(_bf(kb * eg)))
            a_ref[h] = qk * dec
            t_ref[h] = t
            qg_ref[:, _head(h)] = q * eg
            kd_ref[:, _head(h)] = k * jnp.exp(gc_col[C - 1:C, :] - gc_col)
            gcs_parts.append(jnp.where(lane == h, gc_col, 0.0) + jnp.where(lane == DN_HEADS + h, beta, 0.0)
                             + jnp.where(lane == 2 * DN_HEADS + h, g_col, 0.0))

        _interleave(head_program(h) for h in range(DN_HEADS))
        gcs_ref[...] = sum(gcs_parts[1:], gcs_parts[0])

    smem = pl.BlockSpec(memory_space=pltpu.SMEM)
    wide = pl.BlockSpec((C, 512), lambda n: (n, 0))
    sq = pl.BlockSpec((DN_HEADS, C, CPAD), lambda n: (0, n, 0))
    narrow = pl.BlockSpec((C, 128), lambda n: (n, 0))
    f = lambda *shp: jax.ShapeDtypeStruct(shp, F32)
    return pl.pallas_call(
        body, grid=(nc,), name="dn_chunk_fwd",
        in_specs=[smem, smem, pl.BlockSpec((C, 1536), lambda n: (n, 0)), pl.BlockSpec((C, 128), lambda n: (n, BLK_G))],
        out_specs=[wide, wide, wide, wide, sq, sq, narrow],
        out_shape=[f(S, 512), f(S, 512), f(S, 512), f(S, 512), f(DN_HEADS, S, CPAD), f(DN_HEADS, S, CPAD), f(S, 128)],
        compiler_params=_params(("parallel",)),
    )(a_log, dt_bias, qkv, pg)


def _gated_norm(o, z, gn):
    r, oh = _rms_stats(o)
    return oh * gn * (z * _sigmoid(z))


def _dn_scan_fwd(w, u, qg, kd, a, gcs, pz, gn):
    S = w.shape[0]
    C = DN_CHUNK
    nc = S // C

    def body(w_ref, u_ref, qg_ref, kd_ref, a_ref, gcs_ref, z_ref, gn_ref, o_ref, vn_ref, sst_ref, out_ref, state):
        @pl.when(pl.program_id(0) == 0)
        def _():
            state[...] = jnp.zeros_like(state)

        def head_program(h):
            hs = _head(h)
            s_in = state[h]
            sst_ref[0, h] = s_in
            sb = _bf(s_in)
            w_s = _dot(_bf(w_ref[:, hs]), sb)
            q_s = _dot(_bf(qg_ref[:, hs]), sb)
            yield
            vn = u_ref[:, hs] - w_s
            vnb = _bf(vn)
            o = q_s + _dot(_bf(a_ref[h]), _rows_pad(vnb))
            k_v = _dot_tn(_bf(kd_ref[:, hs]), vnb)
            yield
            state[h] = s_in * jnp.exp(gcs_ref[C - 1:C, h:h + 1]) + k_v
            o_ref[:, hs] = o
            vn_ref[:, hs] = vn
            out_ref[:, hs] = _gated_norm(o, z_ref[:, hs], gn_ref[...])

        _interleave(head_program(h) for h in range(DN_HEADS))

    wide = pl.BlockSpec((C, 512), lambda n: (n, 0))
    f = lambda *shp: jax.ShapeDtypeStruct(shp, F32)
    return pl.pallas_call(
        body, grid=(nc,), name="dn_scan_fwd",
        in_specs=[wide, wide, wide, wide, pl.BlockSpec((DN_HEADS, C, CPAD), lambda n: (0, n, 0)),
                  pl.BlockSpec((C, 128), lambda n: (n, 0)), pl.BlockSpec((C, 512), lambda n: (n, BLK_Z)),
                  pl.BlockSpec((1, DN_DIM), lambda n: (0, 0))],
        out_specs=[wide, wide, pl.BlockSpec((1, DN_HEADS, DN_DIM, DN_DIM), lambda n: (n, 0, 0, 0)), wide],
        out_shape=[f(S, 512), f(S, 512), f(nc, DN_HEADS, DN_DIM, DN_DIM), f(S, 512)],
        scratch_shapes=[pltpu.VMEM((DN_HEADS, DN_DIM, DN_DIM), F32)],
        compiler_params=_params(("arbitrary",)),
    )(w, u, qg, kd, a, gcs, pz, gn)


def _dn_scan_bwd(dcat, o, pz, gn, sst, vnew, w, qg, kd, a, gcs, dproj):
    S = o.shape[0]
    C = DN_CHUNK
    nc = S // C

    def body(dy_ref, o_ref, z_ref, gn_ref, sst_ref, vn_ref, w_ref, qg_ref, kd_ref, a_ref, gcs_ref, _,
             du_ref, dw_ref, dqg_ref, dkd_ref, da_ref, dz_ref, dsc_ref, dgn_ref, dstate):
        @pl.when(pl.program_id(0) == 0)
        def _():
            dstate[...] = jnp.zeros_like(dstate)
            dgn_ref[...] = jnp.zeros_like(dgn_ref)

        gn_ = gn_ref[...]
        lane = lax.broadcasted_iota(jnp.int32, (C, 128), 1)
        row = lax.broadcasted_iota(jnp.int32, (C, 128), 0)
        dsc_parts, dgn_parts = [], []

        def head_program(h):
            hs = _head(h)
            ov, z, dout = o_ref[:, hs], z_ref[:, hs], dy_ref[:, hs]
            r, oh = _rms_stats(ov)
            sg = _sigmoid(z)
            don = dout * (z * sg)
            dz_ref[:, hs] = _bf(dout * (oh * gn_) * (sg * (1.0 + z * (1.0 - sg))))
            dgn_parts.append(jnp.sum(don * oh, axis=0, keepdims=True))
            dn = don * gn_
            do = _bf(r * (dn - oh * jnp.mean(dn * oh, axis=-1, keepdims=True)))
            s_in = sst_ref[0, h]
            sb = _bf(s_in)
            ds_out = dstate[h]
            dsb = _bf(ds_out)
            vnb = _bf(vn_ref[:, hs])
            wb, qgb, kdb, ab = _bf(w_ref[:, hs]), _bf(qg_ref[:, hs]), _bf(kd_ref[:, hs]), _bf(a_ref[h])
            dvn = _dot_tn(ab, do)[:C] + _dot(kdb, dsb)
            da_ref[h] = _dot_nt(do, _rows_pad(vnb))
            dqg_ref[:, hs] = _dot_nt(do, sb)
            dkd_ref[:, hs] = _dot_nt(vnb, dsb)
            q_do = _dot_tn(qgb, do)
            yield
            dvnb = _bf(dvn)
            dw_ref[:, hs] = -_dot_nt(dvnb, sb)
            w_dvn = _dot_tn(wb, dvnb)
            du_ref[:, hs] = dvn
            yield
            d_last = jnp.exp(gcs_ref[C - 1:C, h:h + 1])
            dd = jnp.sum(jnp.sum(ds_out * s_in, axis=1, keepdims=True), axis=0, keepdims=True)
            dsc_parts.append(jnp.where((lane == h) & (row == C - 1), dd * d_last, 0.0))
            dstate[h] = ds_out * d_last + q_do - w_dvn

        _interleave(head_program(h) for h in range(DN_HEADS))
        dsc_ref[...] = sum(dsc_parts[1:], dsc_parts[0])
        dgn_ref[...] += sum(dgn_parts[1:], dgn_parts[0])

    rev = lambda n: nc - 1 - n
    wide = pl.BlockSpec((C, 512), lambda n: (rev(n), 0))
    z_spec = pl.BlockSpec((C, 512), lambda n: (rev(n), BLK_Z))
    sq = pl.BlockSpec((DN_HEADS, C, CPAD), lambda n: (0, rev(n), 0))
    narrow = pl.BlockSpec((C, 128), lambda n: (rev(n), 0))
    gn_spec = pl.BlockSpec((1, DN_DIM), lambda n: (0, 0))
    f = lambda *shp: jax.ShapeDtypeStruct(shp, F32)
    return pl.pallas_call(
        body, grid=(nc,), name="dn_scan_bwd",
        in_specs=[pl.BlockSpec((C, 512), lambda n: (rev(n), 1)), wide, z_spec, gn_spec,
                  pl.BlockSpec((1, DN_HEADS, DN_DIM, DN_DIM), lambda n: (rev(n), 0, 0, 0)),
                  wide, wide, wide, wide, sq, narrow, pl.BlockSpec(memory_space=pl.ANY)],
        out_specs=[wide, wide, wide, wide, sq, z_spec, narrow, gn_spec],
        out_shape=[f(S, 512), f(S, 512), f(S, 512), f(S, 512), f(DN_HEADS, S, CPAD),
                   jax.ShapeDtypeStruct(dproj.shape, dproj.dtype), f(S, 128), f(1, DN_DIM)],
        scratch_shapes=[pltpu.VMEM((DN_HEADS, DN_DIM, DN_DIM), F32)],
        input_output_aliases={11: 5},
        compiler_params=_params(("arbitrary",)),
    )(dcat, o, pz, gn, sst, vnew, w, qg, kd, a, gcs, dproj)


def _dn_chunk_bwd(qkv, pg, t_inv, gcs, du, dw, dqg, dkd, da, dsc, a_log, dt_bias, dproj):
    S = qkv.shape[0]
    C = DN_CHUNK
    nc = S // C

    def body(alog_ref, dtb_ref, qkv_ref, pg_ref, t_ref, gcs_ref, du_ref, dw_ref, dqg_ref, dkd_ref, da_ref, dsc_ref, _,
             dqkv_ref, dpg_ref, acc_ref):
        @pl.when(pl.program_id(0) == 0)
        def _():
            acc_ref[...] = jnp.zeros_like(acc_ref)

        ii, jj = _chunk_masks()
        lane = lax.broadcasted_iota(jnp.int32, (1, 128), 1)
        row8 = lax.broadcasted_iota(jnp.int32, (8, 128), 0)
        lane8 = lax.broadcasted_iota(jnp.int32, (8, 128), 1)
        rowc = lax.broadcasted_iota(jnp.int32, (C, 1), 0)
        tril, strict = jj <= ii, jj < ii
        dpg_parts, acc_parts = [], []

        def head_program(h):
            q, k, v = qkv_ref[:, _head(h)], qkv_ref[:, _head(DN_HEADS + h)], qkv_ref[:, _head(2 * DN_HEADS + h)]
            gc_col, beta, g_col = gcs_ref[:, h:h + 1], gcs_ref[:, DN_HEADS + h:DN_HEADS + h + 1], \
                gcs_ref[:, 2 * DN_HEADS + h:2 * DN_HEADS + h + 1]
            dec = _decay(gc_col, ii, jj)
            eg = jnp.exp(gc_col)
            g_last = gc_col[C - 1:C, :]
            ek = jnp.exp(g_last - gc_col)
            kb, vb = k * beta, v * beta
            kbg = kb * eg
            qb, kbb = _bf(q), _bf(kb)
            k_rows = _rows_pad(_bf(k))
            t = t_ref[h]
            tb = _bf(t)
            dub, dwb = _bf(du_ref[:, _head(h)]), _bf(dw_ref[:, _head(h)])
            dqg_, dkd_ = dqg_ref[:, _head(h)], dkd_ref[:, _head(h)]
            dt = _dot_nt(dub, _rows_pad(_bf(vb))) + _dot_nt(dwb, _rows_pad(_bf(kbg)))
            dvb = _dot_tn(tb, dub)[:C]
            dkbg = _dot_tn(tb, dwb)[:C]
            kk = _dot_nt(kbb, k_rows)
            qk = _dot_nt(qb, k_rows)
            yield
            dt_t = _dot3_nt(dt, t)
            yield
            dl = -_dot3_tn(t, dt_t)
            yield
            dm = jnp.where(strict, dl * dec, 0.0)
            dqk = jnp.where(tril, da_ref[h] * dec, 0.0)
            gmat = dm * kk + dqk * qk
            dgc = jnp.sum(gmat, axis=1, keepdims=True) - _row_to_col(jnp.sum(gmat, axis=0, keepdims=True), ii, jj)
            dmb, dqkb = _bf(dm), _bf(dqk)
            dkb = _dot(dmb, k_rows) + dkbg * eg
            dk = _dot_tn(dmb, kbb)[:C] + _dot_tn(dqkb, qb)[:C] + dkd_ * ek
            dq = _dot(dqkb, k_rows) + dqg_ * eg
            yield
            tk = jnp.sum(dkd_ * k * ek, axis=1, keepdims=True)
            dgc = dgc + jnp.sum(dqg_ * q * eg, axis=1, keepdims=True) - tk + jnp.sum(dkbg * kbg, axis=1, keepdims=True)
            dgl = jnp.sum(tk, axis=0, keepdims=True) + dsc_ref[C - 1:C, h:h + 1]
            dgc = dgc + jnp.where(rowc == C - 1, dgl, 0.0)
            dk = dk + dkb * beta
            dbeta = jnp.sum(dkb * k, axis=1, keepdims=True) + jnp.sum(dvb * v, axis=1, keepdims=True)
            dqkv_ref[:, _head(h)] = dq
            dqkv_ref[:, _head(DN_HEADS + h)] = dk
            dqkv_ref[:, _head(2 * DN_HEADS + h)] = dvb * beta
            dg_col = jnp.sum(jnp.where(jj >= ii, _col_to_row(dgc, ii, jj), 0.0), axis=1, keepdims=True)
            db = dbeta * beta * (1.0 - beta)
            da_in = dg_col * (-jnp.exp(alog_ref[h])) * _sigmoid(pg_ref[:, DN_HEADS + h:DN_HEADS + h + 1] + dtb_ref[h])
            dpg_parts.append(jnp.where(lane == h, db, 0.0) + jnp.where(lane == DN_HEADS + h, da_in, 0.0))
            acc_parts.append(jnp.where((row8 == 0) & (lane8 == h), jnp.sum(dg_col * g_col, axis=0, keepdims=True), 0.0)
                             + jnp.where((row8 == 1) & (lane8 == h), jnp.sum(da_in, axis=0, keepdims=True), 0.0))

        _interleave(head_program(h) for h in range(DN_HEADS))
        dpg = sum(dpg_parts[1:], dpg_parts[0])
        dpg_ref[...] = _bf(jnp.concatenate([dpg, jnp.zeros_like(dpg)], axis=1))
        acc_ref[...] += sum(acc_parts[1:], acc_parts[0])

    smem = pl.BlockSpec(memory_space=pltpu.SMEM)
    wide = pl.BlockSpec((C, 512), lambda n: (n, 0))
    sq = pl.BlockSpec((DN_HEADS, C, CPAD), lambda n: (0, n, 0))
    narrow = pl.BlockSpec((C, 128), lambda n: (n, 0))
    qkv_spec = pl.BlockSpec((C, 1536), lambda n: (n, 0))
    f = lambda *shp: jax.ShapeDtypeStruct(shp, F32)
    return pl.pallas_call(
        body, grid=(nc,), name="dn_chunk_bwd",
        in_specs=[smem, smem, qkv_spec, pl.BlockSpec((C, 128), lambda n: (n, BLK_G)), sq, narrow, wide, wide, wide, wide, sq,
                  narrow, pl.BlockSpec(memory_space=pl.ANY)],
        out_specs=[qkv_spec, pl.BlockSpec((C, 256), lambda n: (n, BLK_G_PAD)), pl.BlockSpec((8, 128), lambda n: (0, 0))],
        out_shape=[f(S, 1536), jax.ShapeDtypeStruct(dproj.shape, dproj.dtype), f(8, 128)],
        input_output_aliases={12: 1},
        compiler_params=_params(("arbitrary",)),
    )(a_log, dt_bias, qkv, pg, t_inv, gcs, du, dw, dqg, dkd, da, dsc, dproj)


def _fill_kv(dk, dv, dproj):
    S = dk.shape[0]
    tm = min(512, S)

    def body(dk_ref, dv_ref, _, o_ref):
        o_ref[...] = _bf(jnp.concatenate([dk_ref[...], dv_ref[...]], axis=1))

    tile = pl.BlockSpec((tm, 128), lambda i: (i, 0))
    return pl.pallas_call(
        body, grid=(S // tm,), name="fill_kv",
        in_specs=[tile, tile, pl.BlockSpec(memory_space=pl.ANY)],
        out_specs=pl.BlockSpec((tm, 256), lambda i: (i, BLK_KV)),
        out_shape=jax.ShapeDtypeStruct(dproj.shape, dproj.dtype),
        input_output_aliases={2: 0},
        compiler_params=_params(("parallel",)),
    )(dk, dv, dproj)


def _w_in_to_internal(w):
    return jnp.concatenate([w[:, 0:512], w[:, 2304:2816], w[:, 768:2304], w[:, 512:768], w[:, 2816:2824],
                            jnp.zeros((w.shape[0], D_IN_PAD - D_IN), w.dtype)], axis=1)


def _w_in_from_internal(g):
    return jnp.concatenate([g[:, 0:512], g[:, 2560:2816], g[:, 1024:2560], g[:, 512:1024], g[:, 2816:2824]], axis=1)


def _local_step(x, p, target, wts, first_weights, other_weights, ship_early):
    S = x.shape[0]
    cos, sin = _rope_tables(S)
    sinks, a_log, dt_bias = wts["sinks"].reshape(8), wts["a_log"].reshape(4), wts["dt_bias"].reshape(4)
    gn = wts["dn_norm"].reshape(1, DN_DIM)
    add = lambda acc, res: (acc + res,)

    u = _rmsnorm_fwd(x, wts["norm_mix"], "norm_mix_fwd")
    w_in, conv_w = first_weights(u)
    proj, = _mm_nn(u, w_in, name="in_proj", out_dtypes=[F32], tn=512)
    attn = _attn_fwd(proj, cos, sin, sinks)
    qkv = _dn_prep_fwd(proj, conv_w)
    cw, cu, cqg, ckd, ca, ct, gcs = _dn_chunk_fwd(qkv, proj, a_log, dt_bias)
    o, vnew, sst, dn_out = _dn_scan_fwd(cw, cu, cqg, ckd, ca, gcs, proj, gn)
    w_o, w_up, w_down, w_pg, w_pp = other_weights(dn_out)
    h1a, = _mm_nn(attn, w_o, name="out_proj_attn", out_dtypes=[F32], tn=512, epi=add, extra=[x], w_row_block=0)
    h1, = _mm_nn(dn_out, w_o, name="out_proj_dn", out_dtypes=[F32], tn=512, epi=add, extra=[h1a], w_row_block=1)
    m = _rmsnorm_fwd(h1, wts["norm_mlp"], "norm_mlp_fwd")

    def relu2(acc):
        r = jnp.maximum(acc, 0.0)
        return r * r, r

    hid, relu = _mm_nn(m, w_up, name="mlp_up", out_dtypes=[BF16, BF16], tn=512, epi=relu2)
    h2, = _mm_nn(hid, w_down, name="mlp_down", out_dtypes=[F32], tn=512, epi=add, extra=[h1])
    n3 = _rmsnorm_fwd(h2, wts["norm_ple"], "norm_ple_fwd")
    pp, = _mm_nn(p, w_pp, name="ple_proj", out_dtypes=[F32], tn=512)

    def ple(acc, h2_t, pp_t):
        gate = _sigmoid(acc)
        return h2_t + gate * pp_t, gate

    h3, gate = _mm_nn(n3, w_pg, name="ple_gate", out_dtypes=[F32, F32], tn=512, epi=ple, extra=[h2, pp])
    dh3, loss, d_norm_final = _final_loss(h3, wts["norm_final"].reshape(1, D_MODEL), target)

    g = {"norm_final": d_norm_final}
    dgl, dpp = _ple_bwd(dh3, pp, gate)
    early = {"w_ple_gate": _mm_tn(n3, dgl, name="d_w_ple_gate", tm=512, tn=512, out_dtype=BF16).reshape(N_DEV, 128, 1024),
             "w_ple_proj": _mm_tn(p, dpp, name="d_w_ple_proj", tm=256, tn=128, out_dtype=BF16, column_shards=True)}
    dn3 = _mm_nt(dgl, w_pg, name="d_n3", out_dtype=F32, tn=512)
    dh2, g["norm_ple"] = _rmsnorm_bwd(h2, wts["norm_ple"], dn3, dh3, "norm_ple_bwd")
    d_act = _mm_nt(dh2, w_down, name="d_hidden", out_dtype=BF16, tn=512, epi=lambda acc, r: acc * (2.0 * r.astype(F32)), extra=[relu])
    early["w_down"] = _mm_tn(hid, dh2, name="d_w_down", tm=512, tn=512, out_dtype=BF16).reshape(N_DEV, 512, 1024)
    early["w_up"] = _mm_tn(m, d_act, name="d_w_up", tm=512, tn=512, out_dtype=BF16, column_shards=True)
    token = ship_early(early)
    dm = _mm_nt(d_act, w_up, name="d_m", out_dtype=F32, tn=512)
    dh1, g["norm_mlp"] = _rmsnorm_bwd(h1, wts["norm_mlp"] + token[0:1, 0:1], dm, dh2, "norm_mlp_bwd")
    dcat = _mm_nt(dh1, w_o, name="d_cat", out_dtype=F32, tn=512)
    g["w_o"] = jnp.concatenate([_mm_tn(attn, dh1, name="d_w_o_attn", tm=512, tn=512, out_dtype=BF16),
                                _mm_tn(dn_out, dh1, name="d_w_o_dn", tm=512, tn=512, out_dtype=BF16)], axis=0)
    dproj, dk, dv, dsinks = _attn_bwd(proj, cos, sin, sinks, dcat)
    g["sinks"] = dsinks[:, 0].reshape(1, 8)
    du_, dw_, dqg, dkd, da, dproj, dsc, g["dn_norm"] = _dn_scan_bwd(dcat, o, proj, gn, sst, vnew, cw, cqg, ckd, ca, gcs, dproj)
    dqkv, dproj, gate_acc = _dn_chunk_bwd(qkv, proj, ct, gcs, du_, dw_, dqg, dkd, da, dsc, a_log, dt_bias, dproj)
    g["a_log"], g["dt_bias"] = gate_acc[0:1, 0:4], gate_acc[1:2, 0:4]
    dproj, g["conv_w"] = _dn_prep_bwd(proj, conv_w, dqkv, dproj)
    dproj = _fill_kv(dk, dv, dproj)
    g["w_in"] = _mm_tn(u, dproj, name="d_w_in", tm=512, tn=512)
    du_in = _mm_nt(dproj, w_in, name="d_u", out_dtype=F32, tn=512)
    grad_x, g["norm_mix"] = _rmsnorm_bwd(x, wts["norm_mix"], du_in, dh1, "norm_mix_bwd")
    return loss, grad_x, g


def _peer(k):
    x, y, c = lax.axis_index("x"), lax.axis_index("y"), lax.axis_index("c")
    px = 1 - x if k & 4 else x
    py = 1 - y if k & 2 else y
    pc = 1 - c if k & 1 else c
    return (px, py, pc), 4 * px + 2 * py + pc


def _exchange(srcs, name, gather):
    n = len(srcs)
    shapes = [(N_DEV,) + s.shape if gather else s.shape for s in srcs]

    def body(*refs):
        src_refs, out_refs = refs[:n], refs[n:2 * n]
        send_sems, recv_sems, local_sems = refs[2 * n:]
        _, me = _peer(0)
        piece = (lambda a, d: src_refs[a]) if gather else (lambda a, d: src_refs[a].at[d])
        local = [pltpu.make_async_copy(piece(a, me), out_refs[a].at[me], local_sems.at[a]) for a in range(n)]
        for cp in local:
            cp.start()
        copies = []
        for a in range(n):
            for k in range(1, N_DEV):
                dev, idx = _peer(k)
                cp = pltpu.make_async_remote_copy(src_ref=piece(a, idx), dst_ref=out_refs[a].at[me],
                                                  send_sem=send_sems.at[a, k - 1], recv_sem=recv_sems.at[a, k - 1],
                                                  device_id=dev, device_id_type=MESH)
                cp.start()
                copies.append(cp)
        for cp in copies:
            cp.wait_recv()
        for cp in copies:
            cp.wait_send()
        for cp in local:
            cp.wait()

    anywhere = pl.BlockSpec(memory_space=pl.ANY)
    return pl.pallas_call(
        body, name=name, in_specs=[anywhere] * n, out_specs=[anywhere] * n,
        out_shape=[jax.ShapeDtypeStruct(shp, s.dtype) for shp, s in zip(shapes, srcs)],
        scratch_shapes=[pltpu.SemaphoreType.DMA((n, N_DEV - 1)), pltpu.SemaphoreType.DMA((n, N_DEV - 1)),
                        pltpu.SemaphoreType.DMA((n,))],
    )(*srcs)


_HBM = pl.BlockSpec(memory_space=pltpu.HBM)
_SEM = pl.BlockSpec(memory_space=pltpu.SEMAPHORE)
_EFFECT = pltpu.SideEffectType.DATAFLOW_SIDE_EFFECTING


def _split_copies(src_refs, land_refs, send_sems, recv_sems, gather):
    _, me = _peer(0)
    copies = []
    for a, (src, land) in enumerate(zip(src_refs, land_refs)):
        for k in range(1, N_DEV):
            dev, idx = _peer(k)
            sem = a * (N_DEV - 1) + k - 1
            copies.append(pltpu.make_async_remote_copy(
                src_ref=src if gather else src.at[idx], dst_ref=land.at[me], send_sem=send_sems.at[sem],
                recv_sem=recv_sems.at[sem], device_id=dev, device_id_type=MESH))
    return copies


def _exchange_start(srcs, name, gather):
    n = len(srcs)
    me = 4 * lax.axis_index("x") + 2 * lax.axis_index("y") + lax.axis_index("c")
    lands = []
    for s in srcs:
        own = s if gather else lax.dynamic_index_in_dim(s, me, 0, keepdims=False)
        shape = (N_DEV,) + s.shape if gather else s.shape
        lands.append(lax.dynamic_update_index_in_dim(lax.empty(shape, s.dtype), own, me, 0))

    def body(*refs):
        src_refs, land_refs = refs[:n], refs[n:2 * n]
        send_sems, recv_sems = refs[2 * n], refs[2 * n + 1]
        for cp in _split_copies(src_refs, land_refs, send_sems, recv_sems, gather):
            cp.start()
        refs[-1][...] = jnp.zeros_like(refs[-1])

    both = list(srcs) + lands
    sems = pltpu.SemaphoreType.DMA((n * (N_DEV - 1),))
    out = pl.pallas_call(
        body, name=name,
        out_shape=(sems, sems, *[pltpu.HBM(t.shape, t.dtype) for t in both], jax.ShapeDtypeStruct((8, 128), F32)),
        in_specs=[_HBM] * (2 * n), out_specs=(_SEM, _SEM, *[_HBM] * (2 * n), pl.BlockSpec(memory_space=pltpu.VMEM)),
        input_output_aliases={i: 2 + i for i in range(2 * n)},
        compiler_params=pltpu.CompilerParams(has_side_effects=_EFFECT),
    )(*[pltpu.with_memory_space_constraint(t, pltpu.HBM) for t in both])
    return (n, gather, out[:-1]), out[-1]


def _exchange_wait(handle, after, name):
    n, gather, (send_sems, recv_sems, *both) = handle

    def body(*refs):
        src_refs, land_refs = refs[:n], refs[n:2 * n]
        for cp in _split_copies(src_refs, land_refs, refs[2 * n], refs[2 * n + 1], gather):
            cp.wait_send()
            cp.wait_recv()

    out = pl.pallas_call(
        body, name=name, out_shape=tuple(pltpu.HBM(t.shape, t.dtype) for t in both),
        in_specs=[_HBM] * (2 * n) + [_SEM, _SEM, pl.BlockSpec(memory_space=pl.ANY)], out_specs=tuple([_HBM] * (2 * n)),
        input_output_aliases={i: i for i in range(2 * n)},
        compiler_params=pltpu.CompilerParams(has_side_effects=_EFFECT),
    )(*both, send_sems, recv_sems, after)
    return list(out[n:])


def _adamw(parts, w, m, v, name):
    n, R, W = parts.shape
    tm = 128 if R % 128 == 0 else R

    def body(p_ref, w_ref, m_ref, v_ref, g_ref, d_ref, nm_ref, nv_ref):
        g = p_ref[0].astype(F32)
        for s in range(1, n):
            g = g + p_ref[s].astype(F32)
        nm = ADAM_B1 * m_ref[...] + (1.0 - ADAM_B1) * g
        nv = ADAM_B2 * v_ref[...] + (1.0 - ADAM_B2) * (g * g)
        m_hat = nm / (1.0 - ADAM_B1 ** ADAM_STEP)
        v_hat = nv / (1.0 - ADAM_B2 ** ADAM_STEP)
        g_ref[...] = g
        d_ref[...] = -ADAM_LR * (m_hat / (jnp.sqrt(v_hat) + ADAM_EPS) + ADAM_WD * w_ref[...])
        nm_ref[...] = nm
        nv_ref[...] = nv

    tile = pl.BlockSpec((tm, W), lambda i: (i, 0))
    return pl.pallas_call(
        body, grid=(R // tm,), name=name,
        in_specs=[pl.BlockSpec((n, tm, W), lambda i: (0, i, 0)), tile, tile, tile],
        out_specs=[tile] * 4, out_shape=[jax.ShapeDtypeStruct((R, W), F32)] * 4,
        compiler_params=_params(("parallel",)),
    )(parts, w, m, v)


_MATRICES = ("w_in", "w_o", "w_up", "w_down", "w_ple_gate", "w_ple_proj")


_FIRST = ("w_in",)
_OTHERS = ("w_o", "w_up", "w_down", "w_ple_gate", "w_ple_proj")
_EARLY = ("w_ple_gate", "w_ple_proj", "w_down", "w_up")
_LATE = ("w_in", "w_o")


def _cols_from_shards(t):
    return jnp.transpose(t, (1, 0, 2)).reshape(t.shape[1], N_DEV * t.shape[2])


def _late_pieces(g):
    w_in = _w_in_from_internal(g["w_in"]).reshape(D_MODEL, N_DEV, D_IN // N_DEV)
    return [_bf(jnp.transpose(w_in, (1, 0, 2))), g["w_o"].reshape(N_DEV, 128, 1024)]


_SMALL_ROWS = 16
_VEC_ROW = {"norm_mix": 0, "norm_mlp": 1, "norm_ple": 2, "norm_final": 3}
_VEC_LANES = {"a_log": (0, 4), "dt_bias": (4, 8), "sinks": (8, 16), "dn_norm": (128, 256)}
_LOSS_ROW, _CONV_ROW = 5, 8


def _pack_small(vals, extra_rows):
    row4 = jnp.zeros((1024,), F32)
    for n, (a, b) in _VEC_LANES.items():
        row4 = row4.at[a:b].set(vals[n].reshape(b - a))
    rows = [vals[n].reshape(1, 1024) for n in ("norm_mix", "norm_mlp", "norm_ple", "norm_final")] + [row4.reshape(1, 1024)]
    return jnp.concatenate(rows + extra_rows, axis=0)


def _unpack_small(buf, like):
    out = {n: buf[r].reshape(like[n].shape) for n, r in _VEC_ROW.items()}
    for n, (a, b) in _VEC_LANES.items():
        out[n] = buf[4, a:b].reshape(like[n].shape)
    return out


_ORDER = ("norm_mix", "w_in", "conv_w", "a_log", "dt_bias", "dn_norm", "sinks", "w_o", "norm_mlp", "w_up", "w_down",
          "norm_ple", "w_ple_gate", "w_ple_proj", "norm_final")


def kernel(x, p, norm_mix, w_in, conv_w, a_log, dt_bias, dn_norm, sinks, w_o, norm_mlp, w_up, w_down, norm_ple, w_ple_gate, w_ple_proj, norm_final, loss_target, m_norm_mix, m_w_in, m_conv_w, m_a_log, m_dt_bias, m_dn_norm, m_sinks, m_w_o, m_norm_mlp, m_w_up, m_w_down, m_norm_ple, m_w_ple_gate, m_w_ple_proj, m_norm_final, v_norm_mix, v_w_in, v_conv_w, v_a_log, v_dt_bias, v_dn_norm, v_sinks, v_w_o, v_norm_mlp, v_w_up, v_w_down, v_norm_ple, v_w_ple_gate, v_w_ple_proj, v_norm_final):
    w = dict(norm_mix=norm_mix, w_in=w_in[0], conv_w=conv_w[0], a_log=a_log, dt_bias=dt_bias, dn_norm=dn_norm, sinks=sinks,
             w_o=w_o[0], norm_mlp=norm_mlp, w_up=w_up[0], w_down=w_down[0], norm_ple=norm_ple, w_ple_gate=w_ple_gate[0],
             w_ple_proj=w_ple_proj[0], norm_final=norm_final)
    m = dict(norm_mix=m_norm_mix, w_in=m_w_in[0], conv_w=m_conv_w[0], a_log=m_a_log, dt_bias=m_dt_bias, dn_norm=m_dn_norm,
             sinks=m_sinks, w_o=m_w_o[0], norm_mlp=m_norm_mlp, w_up=m_w_up[0], w_down=m_w_down[0], norm_ple=m_norm_ple,
             w_ple_gate=m_w_ple_gate[0], w_ple_proj=m_w_ple_proj[0], norm_final=m_norm_final)
    v = dict(norm_mix=v_norm_mix, w_in=v_w_in[0], conv_w=v_conv_w[0], a_log=v_a_log, dt_bias=v_dt_bias, dn_norm=v_dn_norm,
             sinks=v_sinks, w_o=v_w_o[0], norm_mlp=v_norm_mlp, w_up=v_w_up[0], w_down=v_w_down[0], norm_ple=v_norm_ple,
             w_ple_gate=v_w_ple_gate[0], w_ple_proj=v_w_ple_proj[0], norm_final=v_norm_final)
    me = 4 * lax.axis_index("x") + 2 * lax.axis_index("y") + lax.axis_index("c")
    conv_shard = conv_w.shape[2]

    conv_pad = jnp.pad(w["conv_w"], ((0, 8 - DN_CONV), (0, 256 - conv_shard)))
    first, token_first = _exchange_start([_bf(w["w_in"]), conv_pad], "gather_first_start", gather=True)
    later = [_bf(w[n]) for n in _OTHERS]
    later[-1] = _bf(w["w_ple_proj"] + token_first[0:1, 0:1])
    others, token_others = _exchange_start(later, "gather_others_start", gather=True)
    vectors = dict(w)
    vectors["norm_mix"] = w["norm_mix"] + token_others[0:1, 0:1]

    def first_weights(after):
        w_in_all, conv_all = _exchange_wait(first, after, "gather_first_wait")
        conv_all = jnp.transpose(conv_all[:, :DN_CONV, :conv_shard], (1, 0, 2)).reshape(DN_CONV, N_DEV * conv_shard)
        return _w_in_to_internal(_cols_from_shards(w_in_all)), conv_all

    def other_weights(after):
        w_o_all, w_up_all, w_down_all, w_pg_all, w_pp_all = _exchange_wait(others, after, "gather_others_wait")
        return (w_o_all.reshape(1024, 1024), _cols_from_shards(w_up_all), w_down_all.reshape(4096, 1024),
                w_pg_all.reshape(1024, 1024), _cols_from_shards(w_pp_all))

    shipped = []

    def ship_early(pieces):
        handle, token = _exchange_start([pieces[n] for n in _EARLY], "scatter_early_start", gather=False)
        shipped.append(handle)
        return token

    loss, grad_x, g = _local_step(x[0], p[0, 0], loss_target[0], vectors, first_weights, other_weights, ship_early)

    received = dict(zip(_LATE, _exchange(_late_pieces(g), "scatter_late", gather=False)))
    received.update(zip(_EARLY, _exchange_wait(shipped[0], grad_x, "scatter_early_wait")))
    big = {n: _adamw(received[n], w[n], m[n], v[n], "adamw_" + n) for n in _MATRICES}
    small = _pack_small(g, [loss[:, :1] * jnp.ones((1, 1024), F32), jnp.zeros((2, 1024), F32),
                            g["conv_w"].reshape(6, 1024), jnp.zeros((2, 1024), F32)])
    small_all, = _exchange([small], "gather_small", gather=True)
    zeros16 = jnp.zeros((_SMALL_ROWS, 1024), F32)
    summed = _adamw(small_all, zeros16, zeros16, zeros16, "sum_small")[0]
    conv_g = lax.dynamic_slice(summed[_CONV_ROW:_CONV_ROW + 6].reshape(DN_CONV, N_DEV * conv_shard), (0, me * conv_shard),
                               (DN_CONV, conv_shard))
    pad_conv = lambda t: jnp.pad(t.reshape(1, DN_CONV * conv_shard), ((0, 2), (0, 1024 - DN_CONV * conv_shard)))
    small_g = jnp.concatenate([summed[0:5], pad_conv(conv_g)], axis=0)[None]
    pack8 = lambda d: _pack_small(d, [pad_conv(d["conv_w"])])
    sm = _adamw(small_g, pack8(w), pack8(m), pack8(v), "adamw_vectors")

    outs = []
    for i, small_buf in enumerate(sm):
        d = {n: big[n][i] for n in _MATRICES}
        d.update(_unpack_small(small_buf, w))
        d["conv_w"] = small_buf[5, :DN_CONV * conv_shard].reshape(DN_CONV, conv_shard)
        outs.append(d)
    result = [summed[_LOSS_ROW, 0], grad_x[None]]
    for d in outs:
        for n in _ORDER:
            result.append(d[n].reshape(w[n].shape)[None] if n in _MATRICES or n == "conv_w" else d[n].reshape(w[n].shape))
    return tuple(result)
```

```python
import jax
import jax.numpy as jnp
from jax import lax
from jax.experimental import pallas as pl
from jax.experimental.pallas import tpu as pltpu

F32, BF16 = jnp.float32, jnp.bfloat16
EPS = 1e-6
D_MODEL = 1024
N_DEV = 8
ATTN_BLOCK = 128
HEAD_PAIR = 128
DN_HEADS = 4
DN_DIM = 128
DN_CHUNK = 64
DN_CONV = 4
ROPE_THETA = 10000.0
D_IN = 2824
D_IN_PAD = 3072
BLK_Q, BLK_Z = 0, 1
BLK_DN, BLK_K, BLK_V, BLK_G = 8, 20, 21, 22
BLK_KV, BLK_G_PAD = 10, 11
VMEM_LIMIT = 48 * 1024 * 1024
NEG = -1e30
ADAM_LR, ADAM_B1, ADAM_B2, ADAM_EPS, ADAM_WD, ADAM_STEP = 0.001, 0.9, 0.999, 1e-08, 0.01, 10
MESH = pl.DeviceIdType.MESH


def _bf(x):
    return x.astype(BF16)


def _dot(a, b):
    return jnp.dot(a, b, preferred_element_type=F32)


def _dot_nt(a, b):
    return lax.dot_general(a, b, (((1,), (1,)), ((), ())), preferred_element_type=F32)


def _dot_tn(a, b):
    return lax.dot_general(a, b, (((0,), (0,)), ((), ())), preferred_element_type=F32)


def _sigmoid(x):
    return 1.0 / (1.0 + jnp.exp(-x))


def _params(sem):
    return pltpu.CompilerParams(dimension_semantics=sem, vmem_limit_bytes=VMEM_LIMIT)


def _mm_nn(x, w, *, name, out_dtypes, tn, epi=None, extra=(), tm=512, w_row_block=0):
    S, K = x.shape
    N = w.shape[1]
    rb = w_row_block
    tm = min(tm, S)
    n_extra = len(extra)

    def body(x_ref, w_ref, *rest):
        acc = _dot(_bf(x_ref[...]), w_ref[...])
        res = epi(acc, *[r[...] for r in rest[:n_extra]]) if epi else (acc,)
        for o, r in zip(rest[n_extra:], res):
            o[...] = r.astype(o.dtype)

    tile = pl.BlockSpec((tm, tn), lambda i, j: (i, j))
    return pl.pallas_call(
        body, grid=(S // tm, N // tn), name=name,
        in_specs=[pl.BlockSpec((tm, K), lambda i, j: (i, 0)), pl.BlockSpec((K, tn), lambda i, j: (rb, j))] + [tile] * n_extra,
        out_specs=[tile] * len(out_dtypes),
        out_shape=[jax.ShapeDtypeStruct((S, N), dt) for dt in out_dtypes],
        compiler_params=_params(("parallel", "parallel")),
    )(x, w, *extra)


def _mm_nt(dy, w, *, name, out_dtype, tn, epi=None, extra=(), tm=512):
    S, N = dy.shape
    K = w.shape[0]
    tm = min(tm, S)
    n_extra = len(extra)

    def body(dy_ref, w_ref, *rest):
        acc = _dot_nt(_bf(dy_ref[...]), w_ref[...])
        if epi:
            acc = epi(acc, *[r[...] for r in rest[:n_extra]])
        rest[n_extra][...] = acc.astype(out_dtype)

    tile = pl.BlockSpec((tm, tn), lambda i, j: (i, j))
    return pl.pallas_call(
        body, grid=(S // tm, K // tn), name=name,
        in_specs=[pl.BlockSpec((tm, N), lambda i, j: (i, 0)), pl.BlockSpec((tn, N), lambda i, j: (j, 0))] + [tile] * n_extra,
        out_specs=tile,
        out_shape=jax.ShapeDtypeStruct((S, K), out_dtype),
        compiler_params=_params(("parallel", "parallel")),
    )(dy, w, *extra)


def _mm_tn(x, dy, *, name, tm, tn, out_dtype=F32, column_shards=False):
    S, K = x.shape
    N = dy.shape[1]

    def body(x_ref, dy_ref, o_ref):
        o_ref[...] = _dot_tn(_bf(x_ref[...]), _bf(dy_ref[...])).astype(out_dtype)

    if column_shards:
        out_spec = pl.BlockSpec((None, tm, tn), lambda i, j: (j, i, 0))
        out_shape = jax.ShapeDtypeStruct((N // tn, K, tn), out_dtype)
    else:
        out_spec = pl.BlockSpec((tm, tn), lambda i, j: (i, j))
        out_shape = jax.ShapeDtypeStruct((K, N), out_dtype)
    return pl.pallas_call(
        body, grid=(K // tm, N // tn), name=name,
        in_specs=[pl.BlockSpec((S, tm), lambda i, j: (0, i)), pl.BlockSpec((S, tn), lambda i, j: (0, j))],
        out_specs=out_spec, out_shape=out_shape,
        compiler_params=_params(("parallel", "parallel")),
    )(x, dy)


def _rowwise(body, *, tiled, full, out_tiled, out_acc, name, tm=512, smem=()):
    S = tiled[0].shape[0]
    tm = min(tm, S)
    n_in = len(smem) + len(tiled) + len(full)

    def kern(*refs):
        @pl.when(pl.program_id(0) == 0)
        def _():
            for r in refs[n_in + len(out_tiled):]:
                r[...] = jnp.zeros_like(r)
        body(*refs)

    in_specs = [pl.BlockSpec(memory_space=pltpu.SMEM) for _ in smem]
    in_specs += [pl.BlockSpec((tm, a.shape[1]), lambda i: (i, 0)) for a in tiled]
    in_specs += [pl.BlockSpec(a.shape, lambda i, nd=a.ndim: (0,) * nd) for a in full]
    out_specs = [pl.BlockSpec((tm, w), lambda i: (i, 0)) for w, _ in out_tiled]
    out_specs += [pl.BlockSpec(shp, lambda i, nd=len(shp): (0,) * nd) for shp, _ in out_acc]
    out_shape = [jax.ShapeDtypeStruct((S, w), dt) for w, dt in out_tiled]
    out_shape += [jax.ShapeDtypeStruct(shp, dt) for shp, dt in out_acc]
    return pl.pallas_call(
        kern, grid=(S // tm,), name=name, in_specs=in_specs, out_specs=out_specs, out_shape=out_shape,
        compiler_params=_params(("arbitrary",)),
    )(*smem, *tiled, *full)


def _rms_stats(x):
    r = lax.rsqrt(jnp.mean(x * x, axis=-1, keepdims=True) + EPS)
    return r, x * r


def _rmsnorm_fwd(x, g, name):
    def body(x_ref, g_ref, o_ref):
        _, xh = _rms_stats(x_ref[...])
        o_ref[...] = _bf(xh * g_ref[...])

    return _rowwise(body, tiled=[x], full=[g], out_tiled=[(x.shape[1], BF16)], out_acc=[], name=name)[0]


def _rms_bwd_tile(x, g, dxn):
    r, xh = _rms_stats(x)
    dg = jnp.sum(dxn * xh, axis=0, keepdims=True)
    dn = dxn * g
    dx = r * (dn - xh * jnp.mean(dn * xh, axis=-1, keepdims=True))
    return dx, dg


def _rmsnorm_bwd(x, g, dxn, dres, name):
    def body(x_ref, dxn_ref, dres_ref, g_ref, dx_ref, dg_ref):
        dx, dg = _rms_bwd_tile(x_ref[...], g_ref[...], dxn_ref[...])
        dx_ref[...] = dres_ref[...] + dx
        dg_ref[...] += dg

    n = x.shape[1]
    return _rowwise(body, tiled=[x, dxn, dres], full=[g], out_tiled=[(n, F32)], out_acc=[((1, n), F32)], name=name)


def _final_loss(h3, g, target):
    n = h3.shape[1]

    def body(h_ref, t_ref, g_ref, dh_ref, loss_ref, dg_ref):
        x = h_ref[...]
        _, xh = _rms_stats(x)
        e = xh * g_ref[...] - t_ref[...]
        per_tok = jnp.mean(e * e, axis=-1, keepdims=True)
        loss_ref[...] += 0.5 * jnp.sum(per_tok, axis=0, keepdims=True)
        dx, dg = _rms_bwd_tile(x, g_ref[...], e * (1.0 / n))
        dh_ref[...] = dx
        dg_ref[...] += dg

    return _rowwise(body, tiled=[h3, target], full=[g], out_tiled=[(n, F32)],
                    out_acc=[((1, 128), F32), ((1, n), F32)], name="final_loss")


def _ple_bwd(dh3, pp, gate):
    def body(dh_ref, pp_ref, gate_ref, dgl_ref, dpp_ref):
        dh, gt = dh_ref[...], gate_ref[...]
        dgl_ref[...] = _bf(dh * pp_ref[...] * gt * (1.0 - gt))
        dpp_ref[...] = _bf(dh * gt)

    n = dh3.shape[1]
    return _rowwise(body, tiled=[dh3, pp, gate], full=[], out_tiled=[(n, BF16), (n, BF16)], out_acc=[], name="ple_bwd")


def _rope_tables(S):
    half = 32
    inv = 1.0 / (ROPE_THETA ** (jnp.arange(half, dtype=F32) * (2.0 / 64)))
    ang = jnp.arange(S).astype(F32)[:, None] * inv[None, :]
    cos, sin = jnp.cos(ang), jnp.sin(ang)
    return jnp.tile(cos, (1, 4)), jnp.concatenate([-sin, sin, -sin, sin], axis=1)


def _attn_common(i, kc, kp, vc, vp, cc, sc, cp, sp):
    lane = lax.broadcasted_iota(jnp.int32, (1, HEAD_PAIR), 1)
    lane_lo = jnp.bitwise_and(lane, 63) < 32
    slot = [lane < 64, lane >= 64]

    def swap_halves(t):
        return jnp.where(lane_lo, pltpu.roll(t, 96, 1), pltpu.roll(t, 32, 1))

    def rope(t, cos, sin):
        return t * cos + swap_halves(t) * sin

    def unrope(d, cos, sin):
        return d * cos + swap_halves(d * sin)

    k2 = jnp.concatenate([rope(kp, cp, sp), rope(kc, cc, sc)], axis=0)
    v2 = jnp.concatenate([vp, vc], axis=0)
    r = lax.broadcasted_iota(jnp.int32, (ATTN_BLOCK, 2 * ATTN_BLOCK), 0)
    c = lax.broadcasted_iota(jnp.int32, (ATTN_BLOCK, 2 * ATTN_BLOCK), 1)
    valid = (c > r) & (c <= r + ATTN_BLOCK) & jnp.logical_or(c >= ATTN_BLOCK, i > 0)
    ks, vs = {}, {}
    for j in range(2):
        kn = jnp.where(slot[j], k2, 0.0)
        vn = jnp.where(slot[j], v2, 0.0)
        for s in range(2):
            ks[j, s] = _bf(kn if s == j else pltpu.roll(kn, 64, 1))
            vs[j, s] = _bf(vn if s == j else pltpu.roll(vn, 64, 1))
    return slot, rope, unrope, valid, ks, vs


def _attn_probs(scores, valid, sink):
    s = jnp.where(valid, scores * 0.125, NEG)
    m = jnp.maximum(jnp.max(s, axis=1, keepdims=True), sink)
    e = jnp.exp(s - m)
    inv_z = 1.0 / (jnp.sum(e, axis=1, keepdims=True) + jnp.exp(sink - m))
    return e * inv_z, jnp.exp(sink - m) * inv_z


def _attn_specs(S):
    nb = S // ATTN_BLOCK
    prev = lambda i: jnp.maximum(i - 1, 0)
    blk = lambda w, col, row=(lambda i: i): pl.BlockSpec((ATTN_BLOCK, w), lambda i: (row(i), col))
    in_specs = [pl.BlockSpec(memory_space=pltpu.SMEM),
                blk(512, BLK_Q), blk(128, BLK_K), blk(128, BLK_K, prev), blk(128, BLK_V), blk(128, BLK_V, prev),
                blk(128, 0), blk(128, 0), blk(128, 0, prev), blk(128, 0, prev)]
    return nb, in_specs


def _attn_fwd(pa, cos, sin, sinks):
    S = pa.shape[0]
    nb, in_specs = _attn_specs(S)

    def body(sinks_ref, q_ref, kc_ref, kp_ref, vc_ref, vp_ref, cc_ref, sc_ref, cp_ref, sp_ref, o_ref):
        i = pl.program_id(0)
        cc, sc = cc_ref[...], sc_ref[...]
        _, rope, _, valid, ks, vs = _attn_common(i, kc_ref[...], kp_ref[...], vc_ref[...], vp_ref[...],
                                                 cc, sc, cp_ref[...], sp_ref[...])
        pair_cols = [slice(HEAD_PAIR * pair, HEAD_PAIR * (pair + 1)) for pair in range(4)]
        qps = [_bf(rope(q_ref[:, cols], cc, sc)) for cols in pair_cols]
        outs = {}

        def head_program(h):
            pair, s = divmod(h, 2)
            j = h // 4
            scores = _dot_nt(qps[pair], ks[j, s])
            yield
            p, _ = _attn_probs(scores, valid, sinks_ref[h])
            outs[h] = _dot(_bf(p), vs[j, s])

        _interleave(head_program(h) for h in range(8))
        for pair, cols in enumerate(pair_cols):
            o_ref[:, cols] = outs[2 * pair] + outs[2 * pair + 1]

    return pl.pallas_call(
        body, grid=(nb,), name="attn_fwd", in_specs=in_specs,
        out_specs=pl.BlockSpec((ATTN_BLOCK, 512), lambda i: (i, 0)),
        out_shape=jax.ShapeDtypeStruct((S, 512), F32),
        compiler_params=_params(("parallel",)),
    )(sinks, pa, pa, pa, pa, pa, cos, sin, cos, sin)


def _attn_bwd(pa, cos, sin, sinks, dcat):
    S = pa.shape[0]
    nb, in_specs = _attn_specs(S)
    in_specs = in_specs + [pl.BlockSpec((ATTN_BLOCK, 512), lambda i: (i, 0))]

    def body(sinks_ref, q_ref, kc_ref, kp_ref, vc_ref, vp_ref, cc_ref, sc_ref, cp_ref, sp_ref, do_ref,
             dq_ref, dk_ref, dv_ref, dsink_ref):
        i = pl.program_id(0)

        @pl.when(i == 0)
        def _():
            dk_ref[...] = jnp.zeros_like(dk_ref)
            dv_ref[...] = jnp.zeros_like(dv_ref)
            dsink_ref[...] = jnp.zeros_like(dsink_ref)

        cc, sc, cp, sp = cc_ref[...], sc_ref[...], cp_ref[...], sp_ref[...]
        slot, rope, unrope, valid, ks, vs = _attn_common(i, kc_ref[...], kp_ref[...], vc_ref[...], vp_ref[...], cc, sc, cp, sp)
        pair_cols = [slice(HEAD_PAIR * pair, HEAD_PAIR * (pair + 1)) for pair in range(4)]
        qps = [_bf(rope(q_ref[:, cols], cc, sc)) for cols in pair_cols]
        dobs = [_bf(do_ref[:, cols]) for cols in pair_cols]
        dqs, dks, dvs = {}, {}, {}

        def head_program(h):
            pair, s = divmod(h, 2)
            j = h // 4
            qp, dob = qps[pair], dobs[pair]
            scores = _dot_nt(qp, ks[j, s])
            dp = _dot_nt(dob, vs[j, s])
            yield
            p, p_sink = _attn_probs(scores, valid, sinks_ref[h])
            dr = jnp.sum(p * dp, axis=1, keepdims=True)
            ds = _bf(p * (dp - dr) * 0.125)
            dsink_ref[h:h + 1, :] += -jnp.sum(p_sink * dr, axis=0, keepdims=True)
            dqs[h] = _dot(ds, ks[j, s])
            dk_h = _dot_tn(ds, qp)
            dv_h = _dot_tn(_bf(p), dob)
            yield
            dk_h, dv_h = jnp.where(slot[s], dk_h, 0.0), jnp.where(slot[s], dv_h, 0.0)
            if s != j:
                dk_h, dv_h = pltpu.roll(dk_h, 64, 1), pltpu.roll(dv_h, 64, 1)
            dks[h], dvs[h] = dk_h, dv_h

        _interleave(head_program(h) for h in range(8))
        dk2 = sum((dks[h] for h in range(1, 8)), dks[0])
        dv2 = sum((dvs[h] for h in range(1, 8)), dvs[0])
        for pair, cols in enumerate(pair_cols):
            dq_ref[:, cols] = _bf(unrope(dqs[2 * pair] + dqs[2 * pair + 1], cc, sc))
        cur = pl.ds(pl.multiple_of(i * ATTN_BLOCK, ATTN_BLOCK), ATTN_BLOCK)
        dk_ref[cur, :] += unrope(dk2[ATTN_BLOCK:], cc, sc)
        dv_ref[cur, :] += dv2[ATTN_BLOCK:]

        @pl.when(i > 0)
        def _():
            prv = pl.ds(pl.multiple_of((i - 1) * ATTN_BLOCK, ATTN_BLOCK), ATTN_BLOCK)
            dk_ref[prv, :] += unrope(dk2[:ATTN_BLOCK], cp, sp)
            dv_ref[prv, :] += dv2[:ATTN_BLOCK]

    whole = lambda w: pl.BlockSpec((S, w), lambda i: (0, 0))
    return pl.pallas_call(
        body, grid=(nb,), name="attn_bwd", in_specs=in_specs,
        out_specs=[pl.BlockSpec((ATTN_BLOCK, 512), lambda i: (i, BLK_Q)), whole(128), whole(128),
                   pl.BlockSpec((8, 128), lambda i: (0, 0))],
        out_shape=[jax.ShapeDtypeStruct((S, D_IN_PAD), BF16), jax.ShapeDtypeStruct((S, 128), F32),
                   jax.ShapeDtypeStruct((S, 128), F32), jax.ShapeDtypeStruct((8, 128), F32)],
        compiler_params=_params(("arbitrary",)),
    )(sinks, pa, pa, pa, pa, pa, cos, sin, cos, sin, dcat)


CONV_ROWS = 512
CONV_PAD = 8


def _conv_silu(scr, w, r0):
    y = w[3:4, :] * scr[pl.ds(CONV_PAD + r0, CONV_ROWS), :]
    for j in range(DN_CONV - 1):
        y = y + w[j:j + 1, :] * scr[pl.ds(CONV_PAD + r0 - 3 + j, CONV_ROWS), :]
    return y


def _dn_prep_fwd(pd, conv_w):
    S = pd.shape[0]
    assert S % CONV_ROWS == 0

    def body(x_ref, w_ref, o_ref, scr):
        b = pl.program_id(0)
        scr[0:CONV_PAD, :] = jnp.zeros((CONV_PAD, DN_DIM), F32)
        scr[pl.ds(CONV_PAD, S), :] = x_ref[...]
        w = w_ref[...]
        q_scale = jnp.where(b < DN_HEADS, DN_DIM ** -0.5, 1.0)
        for r0 in range(0, S, CONV_ROWS):
            y = _conv_silu(scr, w, r0)
            a = y * _sigmoid(y)
            rs = lax.rsqrt(jnp.sum(a * a, axis=1, keepdims=True) + EPS)
            o_ref[pl.ds(r0, CONV_ROWS), :] = a * jnp.where(b < 2 * DN_HEADS, rs * q_scale, 1.0)

    col = pl.BlockSpec((S, DN_DIM), lambda b: (0, b))
    return pl.pallas_call(
        body, grid=(3 * DN_HEADS,), name="dn_prep_fwd",
        in_specs=[pl.BlockSpec((S, DN_DIM), lambda b: (0, BLK_DN + b)), pl.BlockSpec((DN_CONV, DN_DIM), lambda b: (0, b))],
        out_specs=col,
        out_shape=jax.ShapeDtypeStruct((S, 3 * DN_HEADS * DN_DIM), F32),
        scratch_shapes=[pltpu.VMEM((S + CONV_PAD, DN_DIM), F32)],
        compiler_params=_params(("parallel",)),
    )(pd, conv_w)


def _dn_prep_bwd(pd, conv_w, dqkv, dproj):
    S = pd.shape[0]

    def body(x_ref, w_ref, d_ref, _, dx_ref, dw_ref, scr, dscr):
        b = pl.program_id(0)
        scr[0:CONV_PAD, :] = jnp.zeros((CONV_PAD, DN_DIM), F32)
        scr[pl.ds(CONV_PAD, S), :] = x_ref[...]
        dscr[pl.ds(S, CONV_PAD), :] = jnp.zeros((CONV_PAD, DN_DIM), F32)
        w = w_ref[...]
        q_scale = jnp.where(b < DN_HEADS, DN_DIM ** -0.5, 1.0)
        is_qk = b < 2 * DN_HEADS
        dw = [jnp.zeros((1, DN_DIM), F32) for _ in range(DN_CONV)]
        for r0 in range(0, S, CONV_ROWS):
            y = _conv_silu(scr, w, r0)
            sg = _sigmoid(y)
            a = y * sg
            dout = d_ref[pl.ds(r0, CONV_ROWS), :]
            rs = lax.rsqrt(jnp.sum(a * a, axis=1, keepdims=True) + EPS)
            da_qk = q_scale * rs * (dout - a * (rs * rs) * jnp.sum(dout * a, axis=1, keepdims=True))
            dy = jnp.where(is_qk, da_qk, dout) * (sg * (1.0 + y * (1.0 - sg)))
            dscr[pl.ds(r0, CONV_ROWS), :] = dy
            for j in range(DN_CONV):
                dw[j] = dw[j] + jnp.sum(dy * scr[pl.ds(CONV_PAD + r0 - 3 + j, CONV_ROWS), :], axis=0, keepdims=True)
        for j in range(DN_CONV):
            dw_ref[j:j + 1, :] = dw[j]
        for r0 in range(0, S, CONV_ROWS):
            dx = w[3:4, :] * dscr[pl.ds(r0, CONV_ROWS), :]
            for j in range(DN_CONV - 1):
                dx = dx + w[j:j + 1, :] * dscr[pl.ds(r0 + 3 - j, CONV_ROWS), :]
            dx_ref[pl.ds(r0, CONV_ROWS), :] = _bf(dx)

    col = pl.BlockSpec((S, DN_DIM), lambda b: (0, b))
    proj_col = pl.BlockSpec((S, DN_DIM), lambda b: (0, BLK_DN + b))
    wcol = pl.BlockSpec((DN_CONV, DN_DIM), lambda b: (0, b))
    return pl.pallas_call(
        body, grid=(3 * DN_HEADS,), name="dn_prep_bwd",
        in_specs=[proj_col, wcol, col, pl.BlockSpec(memory_space=pl.ANY)], out_specs=[proj_col, wcol],
        out_shape=[jax.ShapeDtypeStruct(dproj.shape, dproj.dtype), jax.ShapeDtypeStruct((DN_CONV, 3 * DN_HEADS * DN_DIM), F32)],
        scratch_shapes=[pltpu.VMEM((S + CONV_PAD, DN_DIM), F32), pltpu.VMEM((S + CONV_PAD, DN_DIM), F32)],
        input_output_aliases={3: 0},
        compiler_params=_params(("parallel",)),
    )(pd, conv_w, dqkv, dproj)


CPAD = 128


def _chunk_masks():
    ii = lax.broadcasted_iota(jnp.int32, (DN_CHUNK, CPAD), 0)
    jj = lax.broadcasted_iota(jnp.int32, (DN_CHUNK, CPAD), 1)
    return ii, jj


def _rows_pad(a):
    return jnp.concatenate([a, jnp.zeros_like(a)], axis=0)


def _hi_lo(a):
    hi = _bf(a)
    return hi, _bf(a - hi.astype(F32))


def _double_step(t, p):
    C = DN_CHUNK
    th, tl = _hi_lo(t)
    ph, pl_ = _hi_lo(p)
    r1 = _dot(jnp.concatenate([th, tl, ph, pl_], axis=0), _rows_pad(ph))
    r2 = _dot(jnp.concatenate([th, ph], axis=0), _rows_pad(pl_))
    return t + (r1[:C] + r1[C:2 * C] + r2[:C]), r1[2 * C:3 * C] + r1[3 * C:] + r2[C:]


def _dot3_nt(a, b):
    C = DN_CHUNK
    ah, al = _hi_lo(a)
    bh, bl = _hi_lo(b)
    r1 = _dot_nt(jnp.concatenate([ah, al], axis=0), _rows_pad(bh))
    return r1[:C] + r1[C:] + _dot_nt(ah, _rows_pad(bl))


def _dot3_tn(a, b):
    C = DN_CHUNK
    ah, al = _hi_lo(a)
    bh, bl = _hi_lo(b)
    return _dot_tn(ah, bh)[:C] + _dot_tn(al, bh)[:C] + _dot_tn(ah, bl)[:C]


def _interleave(programs):
    programs = list(programs)
    while programs:
        alive = []
        for prog in programs:
            try:
                next(prog)
                alive.append(prog)
            except StopIteration:
                pass
        programs = alive


def _col_to_row(col, ii, jj):
    return jnp.sum(jnp.where(ii == jj, col, 0.0), axis=0, keepdims=True)


def _row_to_col(row, ii, jj):
    return jnp.sum(jnp.where(ii == jj, row, 0.0), axis=1, keepdims=True)


def _decay(gc_col, ii, jj):
    diff = gc_col - _col_to_row(gc_col, ii, jj)
    return jnp.where(jj <= ii, jnp.exp(jnp.where(jj <= ii, diff, 0.0)), 0.0)


def _softplus(x):
    return jnp.maximum(x, 0.0) + jnp.log(1.0 + jnp.exp(-jnp.abs(x)))


def _head(h):
    return slice(DN_DIM * h, DN_DIM * (h + 1))


def _dn_chunk_fwd(qkv, pg, a_log, dt_bias):
    S = qkv.shape[0]
    C = DN_CHUNK
    nc = S // C

    def body(alog_ref, dtb_ref, qkv_ref, pg_ref, w_ref, u_ref, qg_ref, kd_ref, a_ref, t_ref, gcs_ref):
        ii, jj = _chunk_masks()
        lane = lax.broadcasted_iota(jnp.int32, (1, 128), 1)
        eye = (ii == jj).astype(F32)
        gcs_parts = []

        def head_program(h):
            q, k, v = qkv_ref[:, _head(h)], qkv_ref[:, _head(DN_HEADS + h)], qkv_ref[:, _head(2 * DN_HEADS + h)]
            beta = _sigmoid(pg_ref[:, h:h + 1])
            g_col = -jnp.exp(alog_ref[h]) * _softplus(pg_ref[:, DN_HEADS + h:DN_HEADS + h + 1] + dtb_ref[h])
            g_row = _col_to_row(g_col, ii, jj)
            gc_col = jnp.sum(jnp.where(jj <= ii, g_row, 0.0), axis=1, keepdims=True)
            dec = _decay(gc_col, ii, jj)
            eg = jnp.exp(gc_col)
            kb, vb = k * beta, v * beta
            k_rows = _rows_pad(_bf(k))
            kk = _dot_nt(_bf(kb), k_rows)
            qk = _dot_nt(_bf(q), k_rows)
            yield
            t, pw = eye, -jnp.where(jj < ii, kk * dec, 0.0)
            for _ in range(6):
                t, pw = _double_step(t, pw)
                yield
            tb = _bf(t)
            u_ref[:, _head(h)] = _dot(tb, _rows_pad(_bf(vb)))
            w_ref[:, _head(h)] = _dot(tb, _rows_pad(_bf(kb * eg)))
            a_ref[h] = qk * dec
            t_ref[h] = t
            qg_ref[:, _head(h)] = q * eg
            kd_ref[:, _head(h)] = k * jnp.exp(gc_col[C - 1:C, :] - gc_col)
            gcs_parts.append(jnp.where(lane == h, gc_col, 0.0) + jnp.where(lane == DN_HEADS + h, beta, 0.0)
                             + jnp.where(lane == 2 * DN_HEADS + h, g_col, 0.0))

        _interleave(head_program(h) for h in range(DN_HEADS))
        gcs_ref[...] = sum(gcs_parts[1:], gcs_parts[0])

    smem = pl.BlockSpec(memory_space=pltpu.SMEM)
    wide = pl.BlockSpec((C, 512), lambda n: (n, 0))
    sq = pl.BlockSpec((DN_HEADS, C, CPAD), lambda n: (0, n, 0))
    narrow = pl.BlockSpec((C, 128), lambda n: (n, 0))
    f = lambda *shp: jax.ShapeDtypeStruct(shp, F32)
    return pl.pallas_call(
        body, grid=(nc,), name="dn_chunk_fwd",
        in_specs=[smem, smem, pl.BlockSpec((C, 1536), lambda n: (n, 0)), pl.BlockSpec((C, 128), lambda n: (n, BLK_G))],
        out_specs=[wide, wide, wide, wide, sq, sq, narrow],
        out_shape=[f(S, 512), f(S, 512), f(S, 512), f(S, 512), f(DN_HEADS, S, CPAD), f(DN_HEADS, S, CPAD), f(S, 128)],
        compiler_params=_params(("parallel",)),
    )(a_log, dt_bias, qkv, pg)


def _gated_norm(o, z, gn):
    r, oh = _rms_stats(o)
    return oh * gn * (z * _sigmoid(z))


def _dn_scan_fwd(w, u, qg, kd, a, gcs, pz, gn):
    S = w.shape[0]
    C = DN_CHUNK
    nc = S // C

    def body(w_ref, u_ref, qg_ref, kd_ref, a_ref, gcs_ref, z_ref, gn_ref, o_ref, vn_ref, sst_ref, out_ref, state):
        @pl.when(pl.program_id(0) == 0)
        def _():
            state[...] = jnp.zeros_like(state)

        def head_program(h):
            hs = _head(h)
            s_in = state[h]
            sst_ref[0, h] = s_in
            sb = _bf(s_in)
            w_s = _dot(_bf(w_ref[:, hs]), sb)
            q_s = _dot(_bf(qg_ref[:, hs]), sb)
            yield
            vn = u_ref[:, hs] - w_s
            vnb = _bf(vn)
            o = q_s + _dot(_bf(a_ref[h]), _rows_pad(vnb))
            k_v = _dot_tn(_bf(kd_ref[:, hs]), vnb)
            yield
            state[h] = s_in * jnp.exp(gcs_ref[C - 1:C, h:h + 1]) + k_v
            o_ref[:, hs] = o
            vn_ref[:, hs] = vn
            out_ref[:, hs] = _gated_norm(o, z_ref[:, hs], gn_ref[...])

        _interleave(head_program(h) for h in range(DN_HEADS))

    wide = pl.BlockSpec((C, 512), lambda n: (n, 0))
    f = lambda *shp: jax.ShapeDtypeStruct(shp, F32)
    return pl.pallas_call(
        body, grid=(nc,), name="dn_scan_fwd",
        in_specs=[wide, wide, wide, wide, pl.BlockSpec((DN_HEADS, C, CPAD), lambda n: (0, n, 0)),
                  pl.BlockSpec((C, 128), lambda n: (n, 0)), pl.BlockSpec((C, 512), lambda n: (n, BLK_Z)),
                  pl.BlockSpec((1, DN_DIM), lambda n: (0, 0))],
        out_specs=[wide, wide, pl.BlockSpec((1, DN_HEADS, DN_DIM, DN_DIM), lambda n: (n, 0, 0, 0)), wide],
        out_shape=[f(S, 512), f(S, 512), f(nc, DN_HEADS, DN_DIM, DN_DIM), f(S, 512)],
        scratch_shapes=[pltpu.VMEM((DN_HEADS, DN_DIM, DN_DIM), F32)],
        compiler_params=_params(("arbitrary",)),
    )(w, u, qg, kd, a, gcs, pz, gn)


def _dn_scan_bwd(dcat, o, pz, gn, sst, vnew, w, qg, kd, a, gcs, dproj):
    S = o.shape[0]
    C = DN_CHUNK
    nc = S // C

    def body(dy_ref, o_ref, z_ref, gn_ref, sst_ref, vn_ref, w_ref, qg_ref, kd_ref, a_ref, gcs_ref, _,
             du_ref, dw_ref, dqg_ref, dkd_ref, da_ref, dz_ref, dsc_ref, dgn_ref, dstate):
        @pl.when(pl.program_id(0) == 0)
        def _():
            dstate[...] = jnp.zeros_like(dstate)
            dgn_ref[...] = jnp.zeros_like(dgn_ref)

        gn_ = gn_ref[...]
        lane = lax.broadcasted_iota(jnp.int32, (C, 128), 1)
        row = lax.broadcasted_iota(jnp.int32, (C, 128), 0)
        dsc_parts, dgn_parts = [], []

        def head_program(h):
            hs = _head(h)
            ov, z, dout = o_ref[:, hs], z_ref[:, hs], dy_ref[:, hs]
            r, oh = _rms_stats(ov)
            sg = _sigmoid(z)
            don = dout * (z * sg)
            dz_ref[:, hs] = _bf(dout * (oh * gn_) * (sg * (1.0 + z * (1.0 - sg))))
            dgn_parts.append(jnp.sum(don * oh, axis=0, keepdims=True))
            dn = don * gn_
            do = _bf(r * (dn - oh * jnp.mean(dn * oh, axis=-1, keepdims=True)))
            s_in = sst_ref[0, h]
            sb = _bf(s_in)
            ds_out = dstate[h]
            dsb = _bf(ds_out)
            vnb = _bf(vn_ref[:, hs])
            wb, qgb, kdb, ab = _bf(w_ref[:, hs]), _bf(qg_ref[:, hs]), _bf(kd_ref[:, hs]), _bf(a_ref[h])
            dvn = _dot_tn(ab, do)[:C] + _dot(kdb, dsb)
            da_ref[h] = _dot_nt(do, _rows_pad(vnb))
            dqg_ref[:, hs] = _dot_nt(do, sb)
            dkd_ref[:, hs] = _dot_nt(vnb, dsb)
            q_do = _dot_tn(qgb, do)
            yield
            dvnb = _bf(dvn)
            dw_ref[:, hs] = -_dot_nt(dvnb, sb)
            w_dvn = _dot_tn(wb, dvnb)
            du_ref[:, hs] = dvn
            yield
            d_last = jnp.exp(gcs_ref[C - 1:C, h:h + 1])
            dd = jnp.sum(jnp.sum(ds_out * s_in, axis=1, keepdims=True), axis=0, keepdims=True)
            dsc_parts.append(jnp.where((lane == h) & (row == C - 1), dd * d_last, 0.0))
            dstate[h] = ds_out * d_last + q_do - w_dvn

        _interleave(head_program(h) for h in range(DN_HEADS))
        dsc_ref[...] = sum(dsc_parts[1:], dsc_parts[0])
        dgn_ref[...] += sum(dgn_parts[1:], dgn_parts[0])

    rev = lambda n: nc - 1 - n
    wide = pl.BlockSpec((C, 512), lambda n: (rev(n), 0))
    z_spec = pl.BlockSpec((C, 512), lambda n: (rev(n), BLK_Z))
    sq = pl.BlockSpec((DN_HEADS, C, CPAD), lambda n: (0, rev(n), 0))
    narrow = pl.BlockSpec((C, 128), lambda n: (rev(n), 0))
    gn_spec = pl.BlockSpec((1, DN_DIM), lambda n: (0, 0))
    f = lambda *shp: jax.ShapeDtypeStruct(shp, F32)
    return pl.pallas_call(
        body, grid=(nc,), name="dn_scan_bwd",
        in_specs=[pl.BlockSpec((C, 512), lambda n: (rev(n), 1)), wide, z_spec, gn_spec,
                  pl.BlockSpec((1, DN_HEADS, DN_DIM, DN_DIM), lambda n: (rev(n), 0, 0, 0)),
                  wide, wide, wide, wide, sq, narrow, pl.BlockSpec(memory_space=pl.ANY)],
        out_specs=[wide, wide, wide, wide, sq, z_spec, narrow, gn_spec],
        out_shape=[f(S, 512), f(S, 512), f(S, 512), f(S, 512), f(DN_HEADS, S, CPAD),
                   jax.ShapeDtypeStruct(dproj.shape, dproj.dtype), f(S, 128), f(1, DN_DIM)],
        scratch_shapes=[pltpu.VMEM((DN_HEADS, DN_DIM, DN_DIM), F32)],
        input_output_aliases={11: 5},
        compiler_params=_params(("arbitrary",)),
    )(dcat, o, pz, gn, sst, vnew, w, qg, kd, a, gcs, dproj)


def _dn_chunk_bwd(qkv, pg, t_inv, gcs, du, dw, dqg, dkd, da, dsc, a_log, dt_bias, dproj):
    S = qkv.shape[0]
    C = DN_CHUNK
    nc = S // C

    def body(alog_ref, dtb_ref, qkv_ref, pg_ref, t_ref, gcs_ref, du_ref, dw_ref, dqg_ref, dkd_ref, da_ref, dsc_ref, _,
             dqkv_ref, dpg_ref, acc_ref):
        @pl.when(pl.program_id(0) == 0)
        def _():
            acc_ref[...] = jnp.zeros_like(acc_ref)

        ii, jj = _chunk_masks()
        lane = lax.broadcasted_iota(jnp.int32, (1, 128), 1)
        row8 = lax.broadcasted_iota(jnp.int32, (8, 128), 0)
        lane8 = lax.broadcasted_iota(jnp.int32, (8, 128), 1)
        rowc = lax.broadcasted_iota(jnp.int32, (C, 1), 0)
        tril, strict = jj <= ii, jj < ii
        dpg_parts, acc_parts = [], []

        def head_program(h):
            q, k, v = qkv_ref[:, _head(h)], qkv_ref[:, _head(DN_HEADS + h)], qkv_ref[:, _head(2 * DN_HEADS + h)]
            gc_col, beta, g_col = gcs_ref[:, h:h + 1], gcs_ref[:, DN_HEADS + h:DN_HEADS + h + 1], \
                gcs_ref[:, 2 * DN_HEADS + h:2 * DN_HEADS + h + 1]
            dec = _decay(gc_col, ii, jj)
            eg = jnp.exp(gc_col)
            g_last = gc_col[C - 1:C, :]
            ek = jnp.exp(g_last - gc_col)
            kb, vb = k * beta, v * beta
            kbg = kb * eg
            qb, kbb = _bf(q), _bf(kb)
            k_rows = _rows_pad(_bf(k))
            t = t_ref[h]
            tb = _bf(t)
            dub, dwb = _bf(du_ref[:, _head(h)]), _bf(dw_ref[:, _head(h)])
            dqg_, dkd_ = dqg_ref[:, _head(h)], dkd_ref[:, _head(h)]
            dt = _dot_nt(dub, _rows_pad(_bf(vb))) + _dot_nt(dwb, _rows_pad(_bf(kbg)))
            dvb = _dot_tn(tb, dub)[:C]
            dkbg = _dot_tn(tb, dwb)[:C]
            kk = _dot_nt(kbb, k_rows)
            qk = _dot_nt(qb, k_rows)
            yield
            dt_t = _dot3_nt(dt, t)
            yield
            dl = -_dot3_tn(t, dt_t)
            yield
            dm = jnp.where(strict, dl * dec, 0.0)
            dqk = jnp.where(tril, da_ref[h] * dec, 0.0)
            gmat = dm * kk + dqk * qk
            dgc = jnp.sum(gmat, axis=1, keepdims=True) - _row_to_col(jnp.sum(gmat, axis=0, keepdims=True), ii, jj)
            dmb, dqkb = _bf(dm), _bf(dqk)
            dkb = _dot(dmb, k_rows) + dkbg * eg
            dk = _dot_tn(dmb, kbb)[:C] + _dot_tn(dqkb, qb)[:C] + dkd_ * ek
            dq = _dot(dqkb, k_rows) + dqg_ * eg
            yield
            tk = jnp.sum(dkd_ * k * ek, axis=1, keepdims=True)
            dgc = dgc + jnp.sum(dqg_ * q * eg, axis=1, keepdims=True) - tk + jnp.sum(dkbg * kbg, axis=1, keepdims=True)
            dgl = jnp.sum(tk, axis=0, keepdims=True) + dsc_ref[C - 1:C, h:h + 1]
            dgc = dgc + jnp.where(rowc == C - 1, dgl, 0.0)
            dk = dk + dkb * beta
            dbeta = jnp.sum(dkb * k, axis=1, keepdims=True) + jnp.sum(dvb * v, axis=1, keepdims=True)
            dqkv_ref[:, _head(h)] = dq
            dqkv_ref[:, _head(DN_HEADS + h)] = dk
            dqkv_ref[:, _head(2 * DN_HEADS + h)] = dvb * beta
            dg_col = jnp.sum(jnp.where(jj >= ii, _col_to_row(dgc, ii, jj), 0.0), axis=1, keepdims=True)
            db = dbeta * beta * (1.0 - beta)
            da_in = dg_col * (-jnp.exp(alog_ref[h])) * _sigmoid(pg_ref[:, DN_HEADS + h:DN_HEADS + h + 1] + dtb_ref[h])
            dpg_parts.append(jnp.where(lane == h, db, 0.0) + jnp.where(lane == DN_HEADS + h, da_in, 0.0))
            acc_parts.append(jnp.where((row8 == 0) & (lane8 == h), jnp.sum(dg_col * g_col, axis=0, keepdims=True), 0.0)
                             + jnp.where((row8 == 1) & (lane8 == h), jnp.sum(da_in, axis=0, keepdims=True), 0.0))

        _interleave(head_program(h) for h in range(DN_HEADS))
        dpg = sum(dpg_parts[1:], dpg_parts[0])
        dpg_ref[...] = _bf(jnp.concatenate([dpg, jnp.zeros_like(dpg)], axis=1))
        acc_ref[...] += sum(acc_parts[1:], acc_parts[0])

    smem = pl.BlockSpec(memory_space=pltpu.SMEM)
    wide = pl.BlockSpec((C, 512), lambda n: (n, 0))
    sq = pl.BlockSpec((DN_HEADS, C, CPAD), lambda n: (0, n, 0))
    narrow = pl.BlockSpec((C, 128), lambda n: (n, 0))
    qkv_spec = pl.BlockSpec((C, 1536), lambda n: (n, 0))
    f = lambda *shp: jax.ShapeDtypeStruct(shp, F32)
    return pl.pallas_call(
        body, grid=(nc,), name="dn_chunk_bwd",
        in_specs=[smem, smem, qkv_spec, pl.BlockSpec((C, 128), lambda n: (n, BLK_G)), sq, narrow, wide, wide, wide, wide, sq,
                  narrow, pl.BlockSpec(memory_space=pl.ANY)],
        out_specs=[qkv_spec, pl.BlockSpec((C, 256), lambda n: (n, BLK_G_PAD)), pl.BlockSpec((8, 128), lambda n: (0, 0))],
        out_shape=[f(S, 1536), jax.ShapeDtypeStruct(dproj.shape, dproj.dtype), f(8, 128)],
        input_output_aliases={12: 1},
        compiler_params=_params(("arbitrary",)),
    )(a_log, dt_bias, qkv, pg, t_inv, gcs, du, dw, dqg, dkd, da, dsc, dproj)


def _fill_kv(dk, dv, dproj):
    S = dk.shape[0]
    tm = min(512, S)

    def body(dk_ref, dv_ref, _, o_ref):
        o_ref[...] = _bf(jnp.concatenate([dk_ref[...], dv_ref[...]], axis=1))

    tile = pl.BlockSpec((tm, 128), lambda i: (i, 0))
    return pl.pallas_call(
        body, grid=(S // tm,), name="fill_kv",
        in_specs=[tile, tile, pl.BlockSpec(memory_space=pl.ANY)],
        out_specs=pl.BlockSpec((tm, 256), lambda i: (i, BLK_KV)),
        out_shape=jax.ShapeDtypeStruct(dproj.shape, dproj.dtype),
        input_output_aliases={2: 0},
        compiler_params=_params(("parallel",)),
    )(dk, dv, dproj)


def _w_in_to_internal(wt):
    return jnp.concatenate([wt[0:512], wt[2304:2816], wt[768:2304], wt[512:768], wt[2816:2824],
                            jnp.zeros((D_IN_PAD - D_IN, wt.shape[1]), wt.dtype)], axis=0)


def _w_in_from_internal(gt):
    return jnp.concatenate([gt[0:512], gt[2560:2816], gt[1024:2560], gt[512:1024], gt[2816:2824]], axis=0)


def _local_step(x, p, target, wts, first_weights, other_weights, ship_early):
    S = x.shape[0]
    cos, sin = _rope_tables(S)
    sinks, a_log, dt_bias = wts["sinks"].reshape(8), wts["a_log"].reshape(4), wts["dt_bias"].reshape(4)
    gn = wts["dn_norm"].reshape(1, DN_DIM)
    add = lambda acc, res: (acc + res,)

    u = _rmsnorm_fwd(x, wts["norm_mix"], "norm_mix_fwd")
    w_in_t, conv_w = first_weights(u)
    proj = _mm_nt(u, w_in_t, name="in_proj", out_dtype=F32, tn=512)
    attn = _attn_fwd(proj, cos, sin, sinks)
    qkv = _dn_prep_fwd(proj, conv_w)
    cw, cu, cqg, ckd, ca, ct, gcs = _dn_chunk_fwd(qkv, proj, a_log, dt_bias)
    o, vnew, sst, dn_out = _dn_scan_fwd(cw, cu, cqg, ckd, ca, gcs, proj, gn)
    w_o, w_up, w_down, w_pg, w_pp = other_weights(dn_out)
    h1a, = _mm_nn(attn, w_o, name="out_proj_attn", out_dtypes=[F32], tn=512, epi=add, extra=[x], w_row_block=0)
    h1, = _mm_nn(dn_out, w_o, name="out_proj_dn", out_dtypes=[F32], tn=512, epi=add, extra=[h1a], w_row_block=1)
    m = _rmsnorm_fwd(h1, wts["norm_mlp"], "norm_mlp_fwd")

    def relu2(acc):
        r = jnp.maximum(acc, 0.0)
        return r * r, r

    hid, relu = _mm_nn(m, w_up, name="mlp_up", out_dtypes=[BF16, BF16], tn=512, epi=relu2)
    h2, = _mm_nn(hid, w_down, name="mlp_down", out_dtypes=[F32], tn=512, epi=add, extra=[h1])
    n3 = _rmsnorm_fwd(h2, wts["norm_ple"], "norm_ple_fwd")
    pp, = _mm_nn(p, w_pp, name="ple_proj", out_dtypes=[F32], tn=512)

    def ple(acc, h2_t, pp_t):
        gate = _sigmoid(acc)
        return h2_t + gate * pp_t, gate

    h3, gate = _mm_nn(n3, w_pg, name="ple_gate", out_dtypes=[F32, F32], tn=512, epi=ple, extra=[h2, pp])
    dh3, loss, d_norm_final = _final_loss(h3, wts["norm_final"].reshape(1, D_MODEL), target)

    g = {"norm_final": d_norm_final}
    dgl, dpp = _ple_bwd(dh3, pp, gate)
    early = {"w_ple_gate": _mm_tn(n3, dgl, name="d_w_ple_gate", tm=512, tn=512, out_dtype=BF16).reshape(N_DEV, 128, 1024),
             "w_ple_proj": _mm_tn(p, dpp, name="d_w_ple_proj", tm=256, tn=128, out_dtype=BF16, column_shards=True)}
    dn3 = _mm_nt(dgl, w_pg, name="d_n3", out_dtype=F32, tn=512)
    dh2, g["norm_ple"] = _rmsnorm_bwd(h2, wts["norm_ple"], dn3, dh3, "norm_ple_bwd")
    d_act = _mm_nt(dh2, w_down, name="d_hidden", out_dtype=BF16, tn=512, epi=lambda acc, r: acc * (2.0 * r.astype(F32)), extra=[relu])
    early["w_down"] = _mm_tn(hid, dh2, name="d_w_down", tm=512, tn=512, out_dtype=BF16).reshape(N_DEV, 512, 1024)
    early["w_up"] = _mm_tn(m, d_act, name="d_w_up", tm=512, tn=512, out_dtype=BF16, column_shards=True)
    token = ship_early(early)
    dm = _mm_nt(d_act, w_up, name="d_m", out_dtype=F32, tn=512)
    dh1, g["norm_mlp"] = _rmsnorm_bwd(h1, wts["norm_mlp"] + token[0:1, 0:1], dm, dh2, "norm_mlp_bwd")
    dcat = _mm_nt(dh1, w_o, name="d_cat", out_dtype=F32, tn=512)
    d_w_o = jnp.concatenate([_mm_tn(attn, dh1, name="d_w_o_attn", tm=512, tn=512, out_dtype=BF16),
                             _mm_tn(dn_out, dh1, name="d_w_o_dn", tm=512, tn=512, out_dtype=BF16)], axis=0)
    token = ship_early({"w_o": d_w_o.reshape(N_DEV, 128, 1024)})
    dproj, dk, dv, dsinks = _attn_bwd(proj, cos, sin, sinks + token[0, 0], dcat)
    g["sinks"] = dsinks[:, 0].reshape(1, 8)
    du_, dw_, dqg, dkd, da, dproj, dsc, g["dn_norm"] = _dn_scan_bwd(dcat, o, proj, gn, sst, vnew, cw, cqg, ckd, ca, gcs, dproj)
    dqkv, dproj, gate_acc = _dn_chunk_bwd(qkv, proj, ct, gcs, du_, dw_, dqg, dkd, da, dsc, a_log, dt_bias, dproj)
    g["a_log"], g["dt_bias"] = gate_acc[0:1, 0:4], gate_acc[1:2, 0:4]
    dproj, g["conv_w"] = _dn_prep_bwd(proj, conv_w, dqkv, dproj)
    dproj = _fill_kv(dk, dv, dproj)
    g["w_in"] = _mm_tn(dproj, u, name="d_w_in", tm=512, tn=512)
    du_in, = _mm_nn(dproj, w_in_t, name="d_u", out_dtypes=[F32], tn=512)
    grad_x, g["norm_mix"] = _rmsnorm_bwd(x, wts["norm_mix"], du_in, dh1, "norm_mix_bwd")
    return loss, grad_x, g


def _peer(k):
    x, y, c = lax.axis_index("x"), lax.axis_index("y"), lax.axis_index("c")
    px = 1 - x if k & 4 else x
    py = 1 - y if k & 2 else y
    pc = 1 - c if k & 1 else c
    return (px, py, pc), 4 * px + 2 * py + pc


def _exchange(srcs, name, gather):
    n = len(srcs)
    gathers = list(gather) if isinstance(gather, (list, tuple)) else [gather] * n
    shapes = [(N_DEV,) + s.shape if gt else s.shape for s, gt in zip(srcs, gathers)]

    def body(*refs):
        src_refs, out_refs = refs[:n], refs[n:2 * n]
        send_sems, recv_sems, local_sems = refs[2 * n:]
        _, me = _peer(0)
        piece = lambda a, d: src_refs[a] if gathers[a] else src_refs[a].at[d]
        local = [pltpu.make_async_copy(piece(a, me), out_refs[a].at[me], local_sems.at[a]) for a in range(n)]
        for cp in local:
            cp.start()
        copies = []
        for a in range(n):
            for k in range(1, N_DEV):
                dev, idx = _peer(k)
                cp = pltpu.make_async_remote_copy(src_ref=piece(a, idx), dst_ref=out_refs[a].at[me],
                                                  send_sem=send_sems.at[a, k - 1], recv_sem=recv_sems.at[a, k - 1],
                                                  device_id=dev, device_id_type=MESH)
                cp.start()
                copies.append(cp)
        for cp in copies:
            cp.wait_recv()
        for cp in copies:
            cp.wait_send()
        for cp in local:
            cp.wait()

    anywhere = pl.BlockSpec(memory_space=pl.ANY)
    return pl.pallas_call(
        body, name=name, in_specs=[anywhere] * n, out_specs=[anywhere] * n,
        out_shape=[jax.ShapeDtypeStruct(shp, s.dtype) for shp, s in zip(shapes, srcs)],
        scratch_shapes=[pltpu.SemaphoreType.DMA((n, N_DEV - 1)), pltpu.SemaphoreType.DMA((n, N_DEV - 1)),
                        pltpu.SemaphoreType.DMA((n,))],
    )(*srcs)


_HBM = pl.BlockSpec(memory_space=pltpu.HBM)
_SEM = pl.BlockSpec(memory_space=pltpu.SEMAPHORE)
_EFFECT = pltpu.SideEffectType.DATAFLOW_SIDE_EFFECTING


def _split_copies(src_refs, land_refs, send_sems, recv_sems, gather):
    _, me = _peer(0)
    copies = []
    for a, (src, land) in enumerate(zip(src_refs, land_refs)):
        for k in range(1, N_DEV):
            dev, idx = _peer(k)
            sem = a * (N_DEV - 1) + k - 1
            copies.append(pltpu.make_async_remote_copy(
                src_ref=src if gather else src.at[idx], dst_ref=land.at[me], send_sem=send_sems.at[sem],
                recv_sem=recv_sems.at[sem], device_id=dev, device_id_type=MESH))
    return copies


def _exchange_start(srcs, name, gather):
    n = len(srcs)
    me = 4 * lax.axis_index("x") + 2 * lax.axis_index("y") + lax.axis_index("c")
    lands = []
    for s in srcs:
        own = s if gather else lax.dynamic_index_in_dim(s, me, 0, keepdims=False)
        shape = (N_DEV,) + s.shape if gather else s.shape
        lands.append(lax.dynamic_update_index_in_dim(lax.empty(shape, s.dtype), own, me, 0))

    def body(*refs):
        src_refs, land_refs = refs[:n], refs[n:2 * n]
        send_sems, recv_sems = refs[2 * n], refs[2 * n + 1]
        for cp in _split_copies(src_refs, land_refs, send_sems, recv_sems, gather):
            cp.start()
        refs[-1][...] = jnp.zeros_like(refs[-1])

    both = list(srcs) + lands
    sems = pltpu.SemaphoreType.DMA((n * (N_DEV - 1),))
    out = pl.pallas_call(
        body, name=name,
        out_shape=(sems, sems, *[pltpu.HBM(t.shape, t.dtype) for t in both], jax.ShapeDtypeStruct((8, 128), F32)),
        in_specs=[_HBM] * (2 * n), out_specs=(_SEM, _SEM, *[_HBM] * (2 * n), pl.BlockSpec(memory_space=pltpu.VMEM)),
        input_output_aliases={i: 2 + i for i in range(2 * n)},
        compiler_params=pltpu.CompilerParams(has_side_effects=_EFFECT),
    )(*[pltpu.with_memory_space_constraint(t, pltpu.HBM) for t in both])
    return (n, gather, out[:-1]), out[-1]


def _exchange_wait(handle, after, name):
    n, gather, (send_sems, recv_sems, *both) = handle

    def body(*refs):
        src_refs, land_refs = refs[:n], refs[n:2 * n]
        for cp in _split_copies(src_refs, land_refs, refs[2 * n], refs[2 * n + 1], gather):
            cp.wait_send()
            cp.wait_recv()

    out = pl.pallas_call(
        body, name=name, out_shape=tuple(pltpu.HBM(t.shape, t.dtype) for t in both),
        in_specs=[_HBM] * (2 * n) + [_SEM, _SEM, pl.BlockSpec(memory_space=pl.ANY)], out_specs=tuple([_HBM] * (2 * n)),
        input_output_aliases={i: i for i in range(2 * n)},
        compiler_params=pltpu.CompilerParams(has_side_effects=_EFFECT),
    )(*both, send_sems, recv_sems, after)
    return list(out[n:])


def _adamw(parts, w, m, v, name):
    n, R, W = parts.shape
    tm = 128 if R % 128 == 0 else R

    def body(p_ref, w_ref, m_ref, v_ref, g_ref, d_ref, nm_ref, nv_ref):
        g = p_ref[0].astype(F32)
        for s in range(1, n):
            g = g + p_ref[s].astype(F32)
        nm = ADAM_B1 * m_ref[...] + (1.0 - ADAM_B1) * g
        nv = ADAM_B2 * v_ref[...] + (1.0 - ADAM_B2) * (g * g)
        m_hat = nm / (1.0 - ADAM_B1 ** ADAM_STEP)
        v_hat = nv / (1.0 - ADAM_B2 ** ADAM_STEP)
        g_ref[...] = g
        d_ref[...] = -ADAM_LR * (m_hat / (jnp.sqrt(v_hat) + ADAM_EPS) + ADAM_WD * w_ref[...])
        nm_ref[...] = nm
        nv_ref[...] = nv

    tile = pl.BlockSpec((tm, W), lambda i: (i, 0))
    return pl.pallas_call(
        body, grid=(R // tm,), name=name,
        in_specs=[pl.BlockSpec((n, tm, W), lambda i: (0, i, 0)), tile, tile, tile],
        out_specs=[tile] * 4, out_shape=[jax.ShapeDtypeStruct((R, W), F32)] * 4,
        compiler_params=_params(("parallel",)),
    )(parts, w, m, v)


_MATRICES = ("w_in", "w_o", "w_up", "w_down", "w_ple_gate", "w_ple_proj")


_OTHERS = ("w_o", "w_up", "w_down", "w_ple_gate", "w_ple_proj")


def _cols_from_shards(t):
    return jnp.transpose(t, (1, 0, 2)).reshape(t.shape[1], N_DEV * t.shape[2])


_SMALL_ROWS = 16
_VEC_ROW = {"norm_mix": 0, "norm_mlp": 1, "norm_ple": 2, "norm_final": 3}
_VEC_LANES = {"a_log": (0, 4), "dt_bias": (4, 8), "sinks": (8, 16), "dn_norm": (128, 256)}
_LOSS_ROW, _CONV_ROW = 5, 8


def _pack_small(vals, extra_rows):
    row4 = jnp.zeros((1024,), F32)
    for n, (a, b) in _VEC_LANES.items():
        row4 = row4.at[a:b].set(vals[n].reshape(b - a))
    rows = [vals[n].reshape(1, 1024) for n in ("norm_mix", "norm_mlp", "norm_ple", "norm_final")] + [row4.reshape(1, 1024)]
    return jnp.concatenate(rows + extra_rows, axis=0)


def _unpack_small(buf, like):
    out = {n: buf[r].reshape(like[n].shape) for n, r in _VEC_ROW.items()}
    for n, (a, b) in _VEC_LANES.items():
        out[n] = buf[4, a:b].reshape(like[n].shape)
    return out


_ORDER = ("norm_mix", "w_in", "conv_w", "a_log", "dt_bias", "dn_norm", "sinks", "w_o", "norm_mlp", "w_up", "w_down",
          "norm_ple", "w_ple_gate", "w_ple_proj", "norm_final")


def kernel(x, p, norm_mix, w_in, conv_w, a_log, dt_bias, dn_norm, sinks, w_o, norm_mlp, w_up, w_down, norm_ple, w_ple_gate, w_ple_proj, norm_final, loss_target, m_norm_mix, m_w_in, m_conv_w, m_a_log, m_dt_bias, m_dn_norm, m_sinks, m_w_o, m_norm_mlp, m_w_up, m_w_down, m_norm_ple, m_w_ple_gate, m_w_ple_proj, m_norm_final, v_norm_mix, v_w_in, v_conv_w, v_a_log, v_dt_bias, v_dn_norm, v_sinks, v_w_o, v_norm_mlp, v_w_up, v_w_down, v_norm_ple, v_w_ple_gate, v_w_ple_proj, v_norm_final):
    w = dict(norm_mix=norm_mix, w_in=w_in[0], conv_w=conv_w[0], a_log=a_log, dt_bias=dt_bias, dn_norm=dn_norm, sinks=sinks,
             w_o=w_o[0], norm_mlp=norm_mlp, w_up=w_up[0], w_down=w_down[0], norm_ple=norm_ple, w_ple_gate=w_ple_gate[0],
             w_ple_proj=w_ple_proj[0], norm_final=norm_final)
    m = dict(norm_mix=m_norm_mix, w_in=m_w_in[0], conv_w=m_conv_w[0], a_log=m_a_log, dt_bias=m_dt_bias, dn_norm=m_dn_norm,
             sinks=m_sinks, w_o=m_w_o[0], norm_mlp=m_norm_mlp, w_up=m_w_up[0], w_down=m_w_down[0], norm_ple=m_norm_ple,
             w_ple_gate=m_w_ple_gate[0], w_ple_proj=m_w_ple_proj[0], norm_final=m_norm_final)
    v = dict(norm_mix=v_norm_mix, w_in=v_w_in[0], conv_w=v_conv_w[0], a_log=v_a_log, dt_bias=v_dt_bias, dn_norm=v_dn_norm,
             sinks=v_sinks, w_o=v_w_o[0], norm_mlp=v_norm_mlp, w_up=v_w_up[0], w_down=v_w_down[0], norm_ple=v_norm_ple,
             w_ple_gate=v_w_ple_gate[0], w_ple_proj=v_w_ple_proj[0], norm_final=v_norm_final)
    me = 4 * lax.axis_index("x") + 2 * lax.axis_index("y") + lax.axis_index("c")
    conv_shard = conv_w.shape[2]

    for d in (w, m, v):
        d["w_in"] = d["w_in"].T
    conv_pad = jnp.pad(w["conv_w"], ((0, 8 - DN_CONV), (0, 256 - conv_shard)))
    first, token_first = _exchange_start([_bf(w["w_in"]), conv_pad], "gather_first_start", gather=True)
    later = [_bf(w[n]) for n in _OTHERS]
    later[-1] = _bf(w["w_ple_proj"] + token_first[0:1, 0:1])
    others, token_others = _exchange_start(later, "gather_others_start", gather=True)
    vectors = dict(w)
    vectors["norm_mix"] = w["norm_mix"] + token_others[0:1, 0:1]

    def first_weights(after):
        w_in_all, conv_all = _exchange_wait(first, after, "gather_first_wait")
        conv_all = jnp.transpose(conv_all[:, :DN_CONV, :conv_shard], (1, 0, 2)).reshape(DN_CONV, N_DEV * conv_shard)
        return _w_in_to_internal(w_in_all.reshape(D_IN, D_MODEL)), conv_all

    def other_weights(after):
        w_o_all, w_up_all, w_down_all, w_pg_all, w_pp_all = _exchange_wait(others, after, "gather_others_wait")
        return (w_o_all.reshape(1024, 1024), _cols_from_shards(w_up_all), w_down_all.reshape(4096, 1024),
                w_pg_all.reshape(1024, 1024), _cols_from_shards(w_pp_all))

    shipped = []

    def ship_early(pieces):
        names = tuple(pieces)
        handle, token = _exchange_start([pieces[n] for n in names], "scatter_start_" + names[0], gather=False)
        shipped.append((names, handle))
        return token

    loss, grad_x, g = _local_step(x[0], p[0, 0], loss_target[0], vectors, first_weights, other_weights, ship_early)

    w_in_pieces = _bf(_w_in_from_internal(g["w_in"])).reshape(N_DEV, D_IN // N_DEV, D_MODEL)
    small = _pack_small(g, [loss[:, :1] * jnp.ones((1, 1024), F32), jnp.zeros((2, 1024), F32),
                            g["conv_w"].reshape(6, 1024), jnp.zeros((2, 1024), F32)])
    w_in_received, small_all = _exchange([w_in_pieces, small], "scatter_late", gather=[False, True])
    received = {"w_in": w_in_received}
    for names, handle in shipped:
        received.update(zip(names, _exchange_wait(handle, grad_x, "scatter_wait_" + names[0])))
    big = {n: _adamw(received[n], w[n], m[n], v[n], "adamw_" + n) for n in _MATRICES}
    zeros16 = jnp.zeros((_SMALL_ROWS, 1024), F32)
    summed = _adamw(small_all, zeros16, zeros16, zeros16, "sum_small")[0]
    conv_g = lax.dynamic_slice(summed[_CONV_ROW:_CONV_ROW + 6].reshape(DN_CONV, N_DEV * conv_shard), (0, me * conv_shard),
                               (DN_CONV, conv_shard))
    pad_conv = lambda t: jnp.pad(t.reshape(1, DN_CONV * conv_shard), ((0, 2), (0, 1024 - DN_CONV * conv_shard)))
    small_g = jnp.concatenate([summed[0:5], pad_conv(conv_g)], axis=0)[None]
    pack8 = lambda d: _pack_small(d, [pad_conv(d["conv_w"])])
    sm = _adamw(small_g, pack8(w), pack8(m), pack8(v), "adamw_vectors")

    outs = []
    for i, small_buf in enumerate(sm):
        d = {n: big[n][i] for n in _MATRICES}
        d.update(_unpack_small(small_buf, w))
        d["conv_w"] = small_buf[5, :DN_CONV * conv_shard].reshape(DN_CONV, conv_shard)
        outs.append(d)
    result = [summed[_LOSS_ROW, 0], grad_x[None]]
    for d in outs:
        d["w_in"] = d["w_in"].T
        for n in _ORDER:
            result.append(d[n][None] if n in _MATRICES or n == "conv_w" else d[n].reshape(w[n].shape))
    return tuple(result)
```

```python
import jax
import jax.numpy as jnp
from jax import lax
from jax.experimental import pallas as pl
from jax.experimental.pallas import tpu as pltpu

F32, BF16 = jnp.float32, jnp.bfloat16
EPS = 1e-6
D_MODEL = 1024
N_DEV = 8
ATTN_BLOCK = 128
HEAD_PAIR = 128
DN_HEADS = 4
DN_DIM = 128
DN_CHUNK = 64
DN_CONV = 4
ROPE_THETA = 10000.0
D_IN = 2824
D_IN_PAD = 3072
BLK_Q, BLK_Z = 0, 1
BLK_DN, BLK_K, BLK_V, BLK_G = 8, 20, 21, 22
BLK_KV, BLK_G_PAD = 10, 11
VMEM_LIMIT = 56 * 1024 * 1024
NEG = -1e30
ADAM_LR, ADAM_B1, ADAM_B2, ADAM_EPS, ADAM_WD, ADAM_STEP = 0.001, 0.9, 0.999, 1e-08, 0.01, 10
MESH = pl.DeviceIdType.MESH


def _bf(x):
    return x.astype(BF16)


def _dot(a, b):
    return jnp.dot(a, b, preferred_element_type=F32)


def _dot_nt(a, b):
    return lax.dot_general(a, b, (((1,), (1,)), ((), ())), preferred_element_type=F32)


def _dot_tn(a, b):
    return lax.dot_general(a, b, (((0,), (0,)), ((), ())), preferred_element_type=F32)


def _sigmoid(x):
    return 1.0 / (1.0 + jnp.exp(-x))


def _params(sem):
    return pltpu.CompilerParams(dimension_semantics=sem, vmem_limit_bytes=VMEM_LIMIT)


def _mm_nn(x, w, *, name, out_dtypes, tn, epi=None, extra=(), tm=512, w_row_block=0):
    S, K = x.shape
    N = w.shape[1]
    r0 = w_row_block * K
    tm = min(tm, S)
    n_extra = len(extra)

    def body(x_ref, w_ref, *rest):
        xb = _bf(x_ref[...])
        for c in range(N // tn):
            cols = slice(c * tn, (c + 1) * tn)
            acc = _dot(xb, w_ref[r0:r0 + K, cols])
            res = epi(acc, *[r[:, cols] for r in rest[:n_extra]]) if epi else (acc,)
            for o, r in zip(rest[n_extra:], res):
                o[:, cols] = r.astype(o.dtype)

    tile = pl.BlockSpec((tm, N), lambda i: (i, 0))
    return pl.pallas_call(
        body, grid=(S // tm,), name=name,
        in_specs=[pl.BlockSpec((tm, K), lambda i: (i, 0)), pl.BlockSpec(w.shape, lambda i: (0, 0))] + [tile] * n_extra,
        out_specs=[tile] * len(out_dtypes),
        out_shape=[jax.ShapeDtypeStruct((S, N), dt) for dt in out_dtypes],
        compiler_params=_params(("parallel",)),
    )(x, w, *extra)


def _mm_nt(dy, w, *, name, out_dtype, tn, epi=None, extra=(), tm=512):
    S, N = dy.shape
    K = w.shape[0]
    tm = min(tm, S)
    n_extra = len(extra)

    def body(dy_ref, w_ref, *rest):
        dyb = _bf(dy_ref[...])
        for c in range(K // tn):
            cols = slice(c * tn, (c + 1) * tn)
            acc = _dot_nt(dyb, w_ref[cols, :])
            if epi:
                acc = epi(acc, *[r[:, cols] for r in rest[:n_extra]])
            rest[n_extra][:, cols] = acc.astype(out_dtype)

    tile = pl.BlockSpec((tm, K), lambda i: (i, 0))
    return pl.pallas_call(
        body, grid=(S // tm,), name=name,
        in_specs=[pl.BlockSpec((tm, N), lambda i: (i, 0)), pl.BlockSpec(w.shape, lambda i: (0, 0))] + [tile] * n_extra,
        out_specs=tile,
        out_shape=jax.ShapeDtypeStruct((S, K), out_dtype),
        compiler_params=_params(("parallel",)),
    )(dy, w, *extra)


def _mm_tn(x, dy, *, name, tm, tn, out_dtype=F32, column_shards=False):
    S, K = x.shape
    N = dy.shape[1]

    def body(x_ref, dy_ref, o_ref):
        o_ref[...] = _dot_tn(_bf(x_ref[...]), _bf(dy_ref[...])).astype(out_dtype)

    if column_shards:
        out_spec = pl.BlockSpec((None, tm, tn), lambda i, j: (j, i, 0))
        out_shape = jax.ShapeDtypeStruct((N // tn, K, tn), out_dtype)
    else:
        out_spec = pl.BlockSpec((tm, tn), lambda i, j: (i, j))
        out_shape = jax.ShapeDtypeStruct((K, N), out_dtype)
    return pl.pallas_call(
        body, grid=(K // tm, N // tn), name=name,
        in_specs=[pl.BlockSpec((S, tm), lambda i, j: (0, i)), pl.BlockSpec((S, tn), lambda i, j: (0, j))],
        out_specs=out_spec, out_shape=out_shape,
        compiler_params=_params(("parallel", "parallel")),
    )(x, dy)


def _rowwise(body, *, tiled, full, out_tiled, out_acc, name, tm=512, smem=()):
    S = tiled[0].shape[0]
    tm = min(tm, S)
    n_in = len(smem) + len(tiled) + len(full)

    def kern(*refs):
        @pl.when(pl.program_id(0) == 0)
        def _():
            for r in refs[n_in + len(out_tiled):]:
                r[...] = jnp.zeros_like(r)
        body(*refs)

    in_specs = [pl.BlockSpec(memory_space=pltpu.SMEM) for _ in smem]
    in_specs += [pl.BlockSpec((tm, a.shape[1]), lambda i: (i, 0)) for a in tiled]
    in_specs += [pl.BlockSpec(a.shape, lambda i, nd=a.ndim: (0,) * nd) for a in full]
    out_specs = [pl.BlockSpec((tm, w), lambda i: (i, 0)) for w, _ in out_tiled]
    out_specs += [pl.BlockSpec(shp, lambda i, nd=len(shp): (0,) * nd) for shp, _ in out_acc]
    out_shape = [jax.ShapeDtypeStruct((S, w), dt) for w, dt in out_tiled]
    out_shape += [jax.ShapeDtypeStruct(shp, dt) for shp, dt in out_acc]
    return pl.pallas_call(
        kern, grid=(S // tm,), name=name, in_specs=in_specs, out_specs=out_specs, out_shape=out_shape,
        compiler_params=_params(("arbitrary",)),
    )(*smem, *tiled, *full)


def _rms_stats(x):
    r = lax.rsqrt(jnp.mean(x * x, axis=-1, keepdims=True) + EPS)
    return r, x * r


def _rmsnorm_fwd(x, g, name):
    def body(x_ref, g_ref, o_ref):
        _, xh = _rms_stats(x_ref[...])
        o_ref[...] = _bf(xh * g_ref[...])

    return _rowwise(body, tiled=[x], full=[g], out_tiled=[(x.shape[1], BF16)], out_acc=[], name=name)[0]


def _rms_bwd_tile(x, g, dxn):
    r, xh = _rms_stats(x)
    dg = jnp.sum(dxn * xh, axis=0, keepdims=True)
    dn = dxn * g
    dx = r * (dn - xh * jnp.mean(dn * xh, axis=-1, keepdims=True))
    return dx, dg


def _rmsnorm_bwd(x, g, dxn, dres, name):
    def body(x_ref, dxn_ref, dres_ref, g_ref, dx_ref, dg_ref):
        dx, dg = _rms_bwd_tile(x_ref[...], g_ref[...], dxn_ref[...])
        dx_ref[...] = dres_ref[...] + dx
        dg_ref[...] += dg

    n = x.shape[1]
    return _rowwise(body, tiled=[x, dxn, dres], full=[g], out_tiled=[(n, F32)], out_acc=[((1, n), F32)], name=name)


def _final_loss(h3, g, target):
    n = h3.shape[1]

    def body(h_ref, t_ref, g_ref, dh_ref, loss_ref, dg_ref):
        x = h_ref[...]
        _, xh = _rms_stats(x)
        e = xh * g_ref[...] - t_ref[...]
        per_tok = jnp.mean(e * e, axis=-1, keepdims=True)
        loss_ref[...] += 0.5 * jnp.sum(per_tok, axis=0, keepdims=True)
        dx, dg = _rms_bwd_tile(x, g_ref[...], e * (1.0 / n))
        dh_ref[...] = dx
        dg_ref[...] += dg

    return _rowwise(body, tiled=[h3, target], full=[g], out_tiled=[(n, F32)],
                    out_acc=[((1, 128), F32), ((1, n), F32)], name="final_loss")


def _ple_bwd(dh3, pp, gate):
    def body(dh_ref, pp_ref, gate_ref, dgl_ref, dpp_ref):
        dh, gt = dh_ref[...], gate_ref[...]
        dgl_ref[...] = _bf(dh * pp_ref[...] * gt * (1.0 - gt))
        dpp_ref[...] = _bf(dh * gt)

    n = dh3.shape[1]
    return _rowwise(body, tiled=[dh3, pp, gate], full=[], out_tiled=[(n, BF16), (n, BF16)], out_acc=[], name="ple_bwd")


def _rope_tables(S):
    half = 32
    inv = 1.0 / (ROPE_THETA ** (jnp.arange(half, dtype=F32) * (2.0 / 64)))
    ang = jnp.arange(S).astype(F32)[:, None] * inv[None, :]
    cos, sin = jnp.cos(ang), jnp.sin(ang)
    return jnp.tile(cos, (1, 4)), jnp.concatenate([-sin, sin, -sin, sin], axis=1)


def _attn_common(i, kc, kp, vc, vp, cc, sc, cp, sp):
    lane = lax.broadcasted_iota(jnp.int32, (1, HEAD_PAIR), 1)
    lane_lo = jnp.bitwise_and(lane, 63) < 32
    slot = [lane < 64, lane >= 64]

    def swap_halves(t):
        return jnp.where(lane_lo, pltpu.roll(t, 96, 1), pltpu.roll(t, 32, 1))

    def rope(t, cos, sin):
        return t * cos + swap_halves(t) * sin

    def unrope(d, cos, sin):
        return d * cos + swap_halves(d * sin)

    k2 = jnp.concatenate([rope(kp, cp, sp), rope(kc, cc, sc)], axis=0)
    v2 = jnp.concatenate([vp, vc], axis=0)
    r = lax.broadcasted_iota(jnp.int32, (ATTN_BLOCK, 2 * ATTN_BLOCK), 0)
    c = lax.broadcasted_iota(jnp.int32, (ATTN_BLOCK, 2 * ATTN_BLOCK), 1)
    valid = (c > r) & (c <= r + ATTN_BLOCK) & jnp.logical_or(c >= ATTN_BLOCK, i > 0)
    ks, vs = {}, {}
    for j in range(2):
        kn = jnp.where(slot[j], k2, 0.0)
        vn = jnp.where(slot[j], v2, 0.0)
        for s in range(2):
            ks[j, s] = _bf(kn if s == j else pltpu.roll(kn, 64, 1))
            vs[j, s] = _bf(vn if s == j else pltpu.roll(vn, 64, 1))
    return slot, rope, unrope, valid, ks, vs


def _attn_probs(scores, valid, sink):
    s = jnp.where(valid, scores * 0.125, NEG)
    m = jnp.maximum(jnp.max(s, axis=1, keepdims=True), sink)
    e = jnp.exp(s - m)
    inv_z = 1.0 / (jnp.sum(e, axis=1, keepdims=True) + jnp.exp(sink - m))
    return e * inv_z, jnp.exp(sink - m) * inv_z


def _attn_specs(S):
    nb = S // ATTN_BLOCK
    prev = lambda i: jnp.maximum(i - 1, 0)
    blk = lambda w, col, row=(lambda i: i): pl.BlockSpec((ATTN_BLOCK, w), lambda i: (row(i), col))
    in_specs = [pl.BlockSpec(memory_space=pltpu.SMEM),
                blk(512, BLK_Q), blk(128, BLK_K), blk(128, BLK_K, prev), blk(128, BLK_V), blk(128, BLK_V, prev),
                blk(128, 0), blk(128, 0), blk(128, 0, prev), blk(128, 0, prev)]
    return nb, in_specs


def _attn_fwd(pa, cos, sin, sinks):
    S = pa.shape[0]
    nb, in_specs = _attn_specs(S)

    def body(sinks_ref, q_ref, kc_ref, kp_ref, vc_ref, vp_ref, cc_ref, sc_ref, cp_ref, sp_ref, o_ref):
        i = pl.program_id(0)
        cc, sc = cc_ref[...], sc_ref[...]
        _, rope, _, valid, ks, vs = _attn_common(i, kc_ref[...], kp_ref[...], vc_ref[...], vp_ref[...],
                                                 cc, sc, cp_ref[...], sp_ref[...])
        pair_cols = [slice(HEAD_PAIR * pair, HEAD_PAIR * (pair + 1)) for pair in range(4)]
        qps = [_bf(rope(q_ref[:, cols], cc, sc)) for cols in pair_cols]
        outs = {}

        def head_program(h):
            pair, s = divmod(h, 2)
            j = h // 4
            scores = _dot_nt(qps[pair], ks[j, s])
            yield
            p, _ = _attn_probs(scores, valid, sinks_ref[h])
            outs[h] = _dot(_bf(p), vs[j, s])

        _interleave(head_program(h) for h in range(8))
        for pair, cols in enumerate(pair_cols):
            o_ref[:, cols] = outs[2 * pair] + outs[2 * pair + 1]

    return pl.pallas_call(
        body, grid=(nb,), name="attn_fwd", in_specs=in_specs,
        out_specs=pl.BlockSpec((ATTN_BLOCK, 512), lambda i: (i, 0)),
        out_shape=jax.ShapeDtypeStruct((S, 512), F32),
        compiler_params=_params(("parallel",)),
    )(sinks, pa, pa, pa, pa, pa, cos, sin, cos, sin)


def _attn_bwd(pa, cos, sin, sinks, dcat):
    S = pa.shape[0]
    nb, in_specs = _attn_specs(S)
    in_specs = in_specs + [pl.BlockSpec((ATTN_BLOCK, 512), lambda i: (i, 0))]

    def body(sinks_ref, q_ref, kc_ref, kp_ref, vc_ref, vp_ref, cc_ref, sc_ref, cp_ref, sp_ref, do_ref,
             dq_ref, dk_ref, dv_ref, dsink_ref):
        i = pl.program_id(0)

        @pl.when(i == 0)
        def _():
            dk_ref[...] = jnp.zeros_like(dk_ref)
            dv_ref[...] = jnp.zeros_like(dv_ref)
            dsink_ref[...] = jnp.zeros_like(dsink_ref)

        cc, sc, cp, sp = cc_ref[...], sc_ref[...], cp_ref[...], sp_ref[...]
        slot, rope, unrope, valid, ks, vs = _attn_common(i, kc_ref[...], kp_ref[...], vc_ref[...], vp_ref[...], cc, sc, cp, sp)
        pair_cols = [slice(HEAD_PAIR * pair, HEAD_PAIR * (pair + 1)) for pair in range(4)]
        qps = [_bf(rope(q_ref[:, cols], cc, sc)) for cols in pair_cols]
        dobs = [_bf(do_ref[:, cols]) for cols in pair_cols]
        dqs, dks, dvs = {}, {}, {}

        def head_program(h):
            pair, s = divmod(h, 2)
            j = h // 4
            qp, dob = qps[pair], dobs[pair]
            scores = _dot_nt(qp, ks[j, s])
            dp = _dot_nt(dob, vs[j, s])
            yield
            p, p_sink = _attn_probs(scores, valid, sinks_ref[h])
            dr = jnp.sum(p * dp, axis=1, keepdims=True)
            ds = _bf(p * (dp - dr) * 0.125)
            dsink_ref[h:h + 1, :] += -jnp.sum(p_sink * dr, axis=0, keepdims=True)
            dqs[h] = _dot(ds, ks[j, s])
            dk_h = _dot_tn(ds, qp)
            dv_h = _dot_tn(_bf(p), dob)
            yield
            dk_h, dv_h = jnp.where(slot[s], dk_h, 0.0), jnp.where(slot[s], dv_h, 0.0)
            if s != j:
                dk_h, dv_h = pltpu.roll(dk_h, 64, 1), pltpu.roll(dv_h, 64, 1)
            dks[h], dvs[h] = dk_h, dv_h

        _interleave(head_program(h) for h in range(8))
        dk2 = sum((dks[h] for h in range(1, 8)), dks[0])
        dv2 = sum((dvs[h] for h in range(1, 8)), dvs[0])
        for pair, cols in enumerate(pair_cols):
            dq_ref[:, cols] = _bf(unrope(dqs[2 * pair] + dqs[2 * pair + 1], cc, sc))
        cur = pl.ds(pl.multiple_of(i * ATTN_BLOCK, ATTN_BLOCK), ATTN_BLOCK)
        dk_ref[cur, :] += unrope(dk2[ATTN_BLOCK:], cc, sc)
        dv_ref[cur, :] += dv2[ATTN_BLOCK:]

        @pl.when(i > 0)
        def _():
            prv = pl.ds(pl.multiple_of((i - 1) * ATTN_BLOCK, ATTN_BLOCK), ATTN_BLOCK)
            dk_ref[prv, :] += unrope(dk2[:ATTN_BLOCK], cp, sp)
            dv_ref[prv, :] += dv2[:ATTN_BLOCK]

    whole = lambda w: pl.BlockSpec((S, w), lambda i: (0, 0))
    return pl.pallas_call(
        body, grid=(nb,), name="attn_bwd", in_specs=in_specs,
        out_specs=[pl.BlockSpec((ATTN_BLOCK, 512), lambda i: (i, BLK_Q)), whole(128), whole(128),
                   pl.BlockSpec((8, 128), lambda i: (0, 0))],
        out_shape=[jax.ShapeDtypeStruct((S, D_IN_PAD), BF16), jax.ShapeDtypeStruct((S, 128), F32),
                   jax.ShapeDtypeStruct((S, 128), F32), jax.ShapeDtypeStruct((8, 128), F32)],
        compiler_params=_params(("arbitrary",)),
    )(sinks, pa, pa, pa, pa, pa, cos, sin, cos, sin, dcat)


CONV_ROWS = 512
CONV_PAD = 8


def _conv_silu(scr, w, r0):
    y = w[3:4, :] * scr[pl.ds(CONV_PAD + r0, CONV_ROWS), :]
    for j in range(DN_CONV - 1):
        y = y + w[j:j + 1, :] * scr[pl.ds(CONV_PAD + r0 - 3 + j, CONV_ROWS), :]
    return y


def _dn_prep_fwd(pd, conv_w):
    S = pd.shape[0]
    assert S % CONV_ROWS == 0

    def body(x_ref, w_ref, o_ref, scr):
        b = pl.program_id(0)
        scr[0:CONV_PAD, :] = jnp.zeros((CONV_PAD, DN_DIM), F32)
        scr[pl.ds(CONV_PAD, S), :] = x_ref[...]
        w = w_ref[...]
        q_scale = jnp.where(b < DN_HEADS, DN_DIM ** -0.5, 1.0)
        for r0 in range(0, S, CONV_ROWS):
            y = _conv_silu(scr, w, r0)
            a = y * _sigmoid(y)
            rs = lax.rsqrt(jnp.sum(a * a, axis=1, keepdims=True) + EPS)
            o_ref[pl.ds(r0, CONV_ROWS), :] = a * jnp.where(b < 2 * DN_HEADS, rs * q_scale, 1.0)

    col = pl.BlockSpec((S, DN_DIM), lambda b: (0, b))
    return pl.pallas_call(
        body, grid=(3 * DN_HEADS,), name="dn_prep_fwd",
        in_specs=[pl.BlockSpec((S, DN_DIM), lambda b: (0, BLK_DN + b)), pl.BlockSpec((DN_CONV, DN_DIM), lambda b: (0, b))],
        out_specs=col,
        out_shape=jax.ShapeDtypeStruct((S, 3 * DN_HEADS * DN_DIM), F32),
        scratch_shapes=[pltpu.VMEM((S + CONV_PAD, DN_DIM), F32)],
        compiler_params=_params(("parallel",)),
    )(pd, conv_w)


def _dn_prep_bwd(pd, conv_w, dqkv, dproj):
    S = pd.shape[0]

    def body(x_ref, w_ref, d_ref, _, dx_ref, dw_ref, scr, dscr):
        b = pl.program_id(0)
        scr[0:CONV_PAD, :] = jnp.zeros((CONV_PAD, DN_DIM), F32)
        scr[pl.ds(CONV_PAD, S), :] = x_ref[...]
        dscr[pl.ds(S, CONV_PAD), :] = jnp.zeros((CONV_PAD, DN_DIM), F32)
        w = w_ref[...]
        q_scale = jnp.where(b < DN_HEADS, DN_DIM ** -0.5, 1.0)
        is_qk = b < 2 * DN_HEADS
        dw = [jnp.zeros((1, DN_DIM), F32) for _ in range(DN_CONV)]
        for r0 in range(0, S, CONV_ROWS):
            y = _conv_silu(scr, w, r0)
            sg = _sigmoid(y)
            a = y * sg
            dout = d_ref[pl.ds(r0, CONV_ROWS), :]
            rs = lax.rsqrt(jnp.sum(a * a, axis=1, keepdims=True) + EPS)
            da_qk = q_scale * rs * (dout - a * (rs * rs) * jnp.sum(dout * a, axis=1, keepdims=True))
            dy = jnp.where(is_qk, da_qk, dout) * (sg * (1.0 + y * (1.0 - sg)))
            dscr[pl.ds(r0, CONV_ROWS), :] = dy
            for j in range(DN_CONV):
                dw[j] = dw[j] + jnp.sum(dy * scr[pl.ds(CONV_PAD + r0 - 3 + j, CONV_ROWS), :], axis=0, keepdims=True)
        for j in range(DN_CONV):
            dw_ref[j:j + 1, :] = dw[j]
        for r0 in range(0, S, CONV_ROWS):
            dx = w[3:4, :] * dscr[pl.ds(r0, CONV_ROWS), :]
            for j in range(DN_CONV - 1):
                dx = dx + w[j:j + 1, :] * dscr[pl.ds(r0 + 3 - j, CONV_ROWS), :]
            dx_ref[pl.ds(r0, CONV_ROWS), :] = _bf(dx)

    col = pl.BlockSpec((S, DN_DIM), lambda b: (0, b))
    proj_col = pl.BlockSpec((S, DN_DIM), lambda b: (0, BLK_DN + b))
    wcol = pl.BlockSpec((DN_CONV, DN_DIM), lambda b: (0, b))
    return pl.pallas_call(
        body, grid=(3 * DN_HEADS,), name="dn_prep_bwd",
        in_specs=[proj_col, wcol, col, pl.BlockSpec(memory_space=pl.ANY)], out_specs=[proj_col, wcol],
        out_shape=[jax.ShapeDtypeStruct(dproj.shape, dproj.dtype), jax.ShapeDtypeStruct((DN_CONV, 3 * DN_HEADS * DN_DIM), F32)],
        scratch_shapes=[pltpu.VMEM((S + CONV_PAD, DN_DIM), F32), pltpu.VMEM((S + CONV_PAD, DN_DIM), F32)],
        input_output_aliases={3: 0},
        compiler_params=_params(("parallel",)),
    )(pd, conv_w, dqkv, dproj)


CPAD = 128


def _chunk_masks():
    ii = lax.broadcasted_iota(jnp.int32, (DN_CHUNK, CPAD), 0)
    jj = lax.broadcasted_iota(jnp.int32, (DN_CHUNK, CPAD), 1)
    return ii, jj


def _rows_pad(a):
    return jnp.concatenate([a, jnp.zeros_like(a)], axis=0)


def _hi_lo(a):
    hi = _bf(a)
    return hi, _bf(a - hi.astype(F32))


def _double_step(t, p):
    C = DN_CHUNK
    th, tl = _hi_lo(t)
    ph, pl_ = _hi_lo(p)
    r1 = _dot(jnp.concatenate([th, tl, ph, pl_], axis=0), _rows_pad(ph))
    r2 = _dot(jnp.concatenate([th, ph], axis=0), _rows_pad(pl_))
    return t + (r1[:C] + r1[C:2 * C] + r2[:C]), r1[2 * C:3 * C] + r1[3 * C:] + r2[C:]


def _dot3_nt(a, b):
    C = DN_CHUNK
    ah, al = _hi_lo(a)
    bh, bl = _hi_lo(b)
    r1 = _dot_nt(jnp.concatenate([ah, al], axis=0), _rows_pad(bh))
    return r1[:C] + r1[C:] + _dot_nt(ah, _rows_pad(bl))


def _dot3_tn(a, b):
    C = DN_CHUNK
    ah, al = _hi_lo(a)
    bh, bl = _hi_lo(b)
    return _dot_tn(ah, bh)[:C] + _dot_tn(al, bh)[:C] + _dot_tn(ah, bl)[:C]


def _interleave(programs):
    programs = list(programs)
    while programs:
        alive = []
        for prog in programs:
            try:
                next(prog)
                alive.append(prog)
            except StopIteration:
                pass
        programs = alive


def _col_to_row(col, ii, jj):
    return jnp.sum(jnp.where(ii == jj, col, 0.0), axis=0, keepdims=True)


def _row_to_col(row, ii, jj):
    return jnp.sum(jnp.where(ii == jj, row, 0.0), axis=1, keepdims=True)


def _decay(gc_col, ii, jj):
    diff = gc_col - _col_to_row(gc_col, ii, jj)
    return jnp.where(jj <= ii, jnp.exp(jnp.where(jj <= ii, diff, 0.0)), 0.0)


def _softplus(x):
    return jnp.maximum(x, 0.0) + jnp.log(1.0 + jnp.exp(-jnp.abs(x)))


def _head(h):
    return slice(DN_DIM * h, DN_DIM * (h + 1))


def _dn_chunk_fwd(qkv, pg, a_log, dt_bias):
    S = qkv.shape[0]
    C = DN_CHUNK
    nc = S // C

    def body(alog_ref, dtb_ref, qkv_ref, pg_ref, w_ref, u_ref, qg_ref, kd_ref, a_ref, t_ref, gcs_ref):
        ii, jj = _chunk_masks()
        lane = lax.broadcasted_iota(jnp.int32, (1, 128), 1)
        eye = (ii == jj).astype(F32)
        gcs_parts = []

        def head_program(h):
            q, k, v = qkv_ref[:, _head(h)], qkv_ref[:, _head(DN_HEADS + h)], qkv_ref[:, _head(2 * DN_HEADS + h)]
            beta = _sigmoid(pg_ref[:, h:h + 1])
            g_col = -jnp.exp(alog_ref[h]) * _softplus(pg_ref[:, DN_HEADS + h:DN_HEADS + h + 1] + dtb_ref[h])
            g_row = _col_to_row(g_col, ii, jj)
            gc_col = jnp.sum(jnp.where(jj <= ii, g_row, 0.0), axis=1, keepdims=True)
            dec = _decay(gc_col, ii, jj)
            eg = jnp.exp(gc_col)
            kb, vb = k * beta, v * beta
            k_rows = _rows_pad(_bf(k))
            kk = _dot_nt(_bf(kb), k_rows)
            qk = _dot_nt(_bf(q), k_rows)
            yield
            t, pw = eye, -jnp.where(jj < ii, kk * dec, 0.0)
            for _ in range(6):
                t, pw = _double_step(t, pw)
                yield
            tb = _bf(t)
            u_ref[:, _head(h)] = _dot(tb, _rows_pad(_bf(vb)))
            w_ref[:, _head(h)] = _dot(tb, _rows_pad(_bf(kb * eg)))
            a_ref[h] = qk * dec
            t_ref[h] = t
            qg_ref[:, _head(h)] = q * eg
            kd_ref[:, _head(h)] = k * jnp.exp(gc_col[C - 1:C, :] - gc_col)
            gcs_parts.append(jnp.where(lane == h, gc_col, 0.0) + jnp.where(lane == DN_HEADS + h, beta, 0.0)
                             + jnp.where(lane == 2 * DN_HEADS + h, g_col, 0.0))

        _interleave(head_program(h) for h in range(DN_HEADS))
        gcs_ref[...] = sum(gcs_parts[1:], gcs_parts[0])

    smem = pl.BlockSpec(memory_space=pltpu.SMEM)
    wide = pl.BlockSpec((C, 512), lambda n: (n, 0))
    sq = pl.BlockSpec((DN_HEADS, C, CPAD), lambda n: (0, n, 0))
    narrow = pl.BlockSpec((C, 128), lambda n: (n, 0))
    f = lambda *shp: jax.ShapeDtypeStruct(shp, F32)
    return pl.pallas_call(
        body, grid=(nc,), name="dn_chunk_fwd",
        in_specs=[smem, smem, pl.BlockSpec((C, 1536), lambda n: (n, 0)), pl.BlockSpec((C, 128), lambda n: (n, BLK_G))],
        out_specs=[wide, wide, wide, wide, sq, sq, narrow],
        out_shape=[f(S, 512), f(S, 512), f(S, 512), f(S, 512), f(DN_HEADS, S, CPAD), f(DN_HEADS, S, CPAD), f(S, 128)],
        compiler_params=_params(("parallel",)),
    )(a_log, dt_bias, qkv, pg)


def _gated_norm(o, z, gn):
    r, oh = _rms_stats(o)
    return oh * gn * (z * _sigmoid(z))


def _dn_scan_fwd(w, u, qg, kd, a, gcs, pz, gn):
    S = w.shape[0]
    C = DN_CHUNK
    nc = S // C

    def body(w_ref, u_ref, qg_ref, kd_ref, a_ref, gcs_ref, z_ref, gn_ref, o_ref, vn_ref, sst_ref, out_ref, state):
        @pl.when(pl.program_id(0) == 0)
        def _():
            state[...] = jnp.zeros_like(state)

        def head_program(h):
            hs = _head(h)
            s_in = state[h]
            sst_ref[0, h] = s_in
            sb = _bf(s_in)
            w_s = _dot(_bf(w_ref[:, hs]), sb)
            q_s = _dot(_bf(qg_ref[:, hs]), sb)
            yield
            vn = u_ref[:, hs] - w_s
            vnb = _bf(vn)
            o = q_s + _dot(_bf(a_ref[h]), _rows_pad(vnb))
            k_v = _dot_tn(_bf(kd_ref[:, hs]), vnb)
            yield
            state[h] = s_in * jnp.exp(gcs_ref[C - 1:C, h:h + 1]) + k_v
            o_ref[:, hs] = o
            vn_ref[:, hs] = vn
            out_ref[:, hs] = _gated_norm(o, z_ref[:, hs], gn_ref[...])

        _interleave(head_program(h) for h in range(DN_HEADS))

    wide = pl.BlockSpec((C, 512), lambda n: (n, 0))
    f = lambda *shp: jax.ShapeDtypeStruct(shp, F32)
    return pl.pallas_call(
        body, grid=(nc,), name="dn_scan_fwd",
        in_specs=[wide, wide, wide, wide, pl.BlockSpec((DN_HEADS, C, CPAD), lambda n: (0, n, 0)),
                  pl.BlockSpec((C, 128), lambda n: (n, 0)), pl.BlockSpec((C, 512), lambda n: (n, BLK_Z)),
                  pl.BlockSpec((1, DN_DIM), lambda n: (0, 0))],
        out_specs=[wide, wide, pl.BlockSpec((1, DN_HEADS, DN_DIM, DN_DIM), lambda n: (n, 0, 0, 0)), wide],
        out_shape=[f(S, 512), f(S, 512), f(nc, DN_HEADS, DN_DIM, DN_DIM), f(S, 512)],
        scratch_shapes=[pltpu.VMEM((DN_HEADS, DN_DIM, DN_DIM), F32)],
        compiler_params=_params(("arbitrary",)),
    )(w, u, qg, kd, a, gcs, pz, gn)


def _dn_scan_bwd(dcat, o, pz, gn, sst, vnew, w, qg, kd, a, gcs, dproj):
    S = o.shape[0]
    C = DN_CHUNK
    nc = S // C

    def body(dy_ref, o_ref, z_ref, gn_ref, sst_ref, vn_ref, w_ref, qg_ref, kd_ref, a_ref, gcs_ref, _,
             du_ref, dw_ref, dqg_ref, dkd_ref, da_ref, dz_ref, dsc_ref, dgn_ref, dstate):
        @pl.when(pl.program_id(0) == 0)
        def _():
            dstate[...] = jnp.zeros_like(dstate)
            dgn_ref[...] = jnp.zeros_like(dgn_ref)

        gn_ = gn_ref[...]
        lane = lax.broadcasted_iota(jnp.int32, (C, 128), 1)
        row = lax.broadcasted_iota(jnp.int32, (C, 128), 0)
        dsc_parts, dgn_parts = [], []

        def head_program(h):
            hs = _head(h)
            ov, z, dout = o_ref[:, hs], z_ref[:, hs], dy_ref[:, hs]
            r, oh = _rms_stats(ov)
            sg = _sigmoid(z)
            don = dout * (z * sg)
            dz_ref[:, hs] = _bf(dout * (oh * gn_) * (sg * (1.0 + z * (1.0 - sg))))
            dgn_parts.append(jnp.sum(don * oh, axis=0, keepdims=True))
            dn = don * gn_
            do = _bf(r * (dn - oh * jnp.mean(dn * oh, axis=-1, keepdims=True)))
            s_in = sst_ref[0, h]
            sb = _bf(s_in)
            ds_out = dstate[h]
            dsb = _bf(ds_out)
            vnb = _bf(vn_ref[:, hs])
            wb, qgb, kdb, ab = _bf(w_ref[:, hs]), _bf(qg_ref[:, hs]), _bf(kd_ref[:, hs]), _bf(a_ref[h])
            dvn = _dot_tn(ab, do)[:C] + _dot(kdb, dsb)
            da_ref[h] = _dot_nt(do, _rows_pad(vnb))
            dqg_ref[:, hs] = _dot_nt(do, sb)
            dkd_ref[:, hs] = _dot_nt(vnb, dsb)
            q_do = _dot_tn(qgb, do)
            yield
            dvnb = _bf(dvn)
            dw_ref[:, hs] = -_dot_nt(dvnb, sb)
            w_dvn = _dot_tn(wb, dvnb)
            du_ref[:, hs] = dvn
            yield
            d_last = jnp.exp(gcs_ref[C - 1:C, h:h + 1])
            dd = jnp.sum(jnp.sum(ds_out * s_in, axis=1, keepdims=True), axis=0, keepdims=True)
            dsc_parts.append(jnp.where((lane == h) & (row == C - 1), dd * d_last, 0.0))
            dstate[h] = ds_out * d_last + q_do - w_dvn

        _interleave(head_program(h) for h in range(DN_HEADS))
        dsc_ref[...] = sum(dsc_parts[1:], dsc_parts[0])
        dgn_ref[...] += sum(dgn_parts[1:], dgn_parts[0])

    rev = lambda n: nc - 1 - n
    wide = pl.BlockSpec((C, 512), lambda n: (rev(n), 0))
    z_spec = pl.BlockSpec((C, 512), lambda n: (rev(n), BLK_Z))
    sq = pl.BlockSpec((DN_HEADS, C, CPAD), lambda n: (0, rev(n), 0))
    narrow = pl.BlockSpec((C, 128), lambda n: (rev(n), 0))
    gn_spec = pl.BlockSpec((1, DN_DIM), lambda n: (0, 0))
    f = lambda *shp: jax.ShapeDtypeStruct(shp, F32)
    return pl.pallas_call(
        body, grid=(nc,), name="dn_scan_bwd",
        in_specs=[pl.BlockSpec((C, 512), lambda n: (rev(n), 1)), wide, z_spec, gn_spec,
                  pl.BlockSpec((1, DN_HEADS, DN_DIM, DN_DIM), lambda n: (rev(n), 0, 0, 0)),
                  wide, wide, wide, wide, sq, narrow, pl.BlockSpec(memory_space=pl.ANY)],
        out_specs=[wide, wide, wide, wide, sq, z_spec, narrow, gn_spec],
        out_shape=[f(S, 512), f(S, 512), f(S, 512), f(S, 512), f(DN_HEADS, S, CPAD),
                   jax.ShapeDtypeStruct(dproj.shape, dproj.dtype), f(S, 128), f(1, DN_DIM)],
        scratch_shapes=[pltpu.VMEM((DN_HEADS, DN_DIM, DN_DIM), F32)],
        input_output_aliases={11: 5},
        compiler_params=_params(("arbitrary",)),
    )(dcat, o, pz, gn, sst, vnew, w, qg, kd, a, gcs, dproj)


def _dn_chunk_bwd(qkv, pg, t_inv, gcs, du, dw, dqg, dkd, da, dsc, a_log, dt_bias, dproj):
    S = qkv.shape[0]
    C = DN_CHUNK
    nc = S // C

    def body(alog_ref, dtb_ref, qkv_ref, pg_ref, t_ref, gcs_ref, du_ref, dw_ref, dqg_ref, dkd_ref, da_ref, dsc_ref, _,
             dqkv_ref, dpg_ref, acc_ref):
        @pl.when(pl.program_id(0) == 0)
        def _():
            acc_ref[...] = jnp.zeros_like(acc_ref)

        ii, jj = _chunk_masks()
        lane = lax.broadcasted_iota(jnp.int32, (1, 128), 1)
        row8 = lax.broadcasted_iota(jnp.int32, (8, 128), 0)
        lane8 = lax.broadcasted_iota(jnp.int32, (8, 128), 1)
        rowc = lax.broadcasted_iota(jnp.int32, (C, 1), 0)
        tril, strict = jj <= ii, jj < ii
        dpg_parts, acc_parts = [], []

        def head_program(h):
            q, k, v = qkv_ref[:, _head(h)], qkv_ref[:, _head(DN_HEADS + h)], qkv_ref[:, _head(2 * DN_HEADS + h)]
            gc_col, beta, g_col = gcs_ref[:, h:h + 1], gcs_ref[:, DN_HEADS + h:DN_HEADS + h + 1], \
                gcs_ref[:, 2 * DN_HEADS + h:2 * DN_HEADS + h + 1]
            dec = _decay(gc_col, ii, jj)
            eg = jnp.exp(gc_col)
            g_last = gc_col[C - 1:C, :]
            ek = jnp.exp(g_last - gc_col)
            kb, vb = k * beta, v * beta
            kbg = kb * eg
            qb, kbb = _bf(q), _bf(kb)
            k_rows = _rows_pad(_bf(k))
            t = t_ref[h]
            tb = _bf(t)
            dub, dwb = _bf(du_ref[:, _head(h)]), _bf(dw_ref[:, _head(h)])
            dqg_, dkd_ = dqg_ref[:, _head(h)], dkd_ref[:, _head(h)]
            dt = _dot_nt(dub, _rows_pad(_bf(vb))) + _dot_nt(dwb, _rows_pad(_bf(kbg)))
            dvb = _dot_tn(tb, dub)[:C]
            dkbg = _dot_tn(tb, dwb)[:C]
            kk = _dot_nt(kbb, k_rows)
            qk = _dot_nt(qb, k_rows)
            yield
            dt_t = _dot3_nt(dt, t)
            yield
            dl = -_dot3_tn(t, dt_t)
            yield
            dm = jnp.where(strict, dl * dec, 0.0)
            dqk = jnp.where(tril, da_ref[h] * dec, 0.0)
            gmat = dm * kk + dqk * qk
            dgc = jnp.sum(gmat, axis=1, keepdims=True) - _row_to_col(jnp.sum(gmat, axis=0, keepdims=True), ii, jj)
            dmb, dqkb = _bf(dm), _bf(dqk)
            dkb = _dot(dmb, k_rows) + dkbg * eg
            dk = _dot_tn(dmb, kbb)[:C] + _dot_tn(dqkb, qb)[:C] + dkd_ * ek
            dq = _dot(dqkb, k_rows) + dqg_ * eg
            yield
            tk = jnp.sum(dkd_ * k * ek, axis=1, keepdims=True)
            dgc = dgc + jnp.sum(dqg_ * q * eg, axis=1, keepdims=True) - tk + jnp.sum(dkbg * kbg, axis=1, keepdims=True)
            dgl = jnp.sum(tk, axis=0, keepdims=True) + dsc_ref[C - 1:C, h:h + 1]
            dgc = dgc + jnp.where(rowc == C - 1, dgl, 0.0)
            dk = dk + dkb * beta
            dbeta = jnp.sum(dkb * k, axis=1, keepdims=True) + jnp.sum(dvb * v, axis=1, keepdims=True)
            dqkv_ref[:, _head(h)] = dq
            dqkv_ref[:, _head(DN_HEADS + h)] = dk
            dqkv_ref[:, _head(2 * DN_HEADS + h)] = dvb * beta
            dg_col = jnp.sum(jnp.where(jj >= ii, _col_to_row(dgc, ii, jj), 0.0), axis=1, keepdims=True)
            db = dbeta * beta * (1.0 - beta)
            da_in = dg_col * (-jnp.exp(alog_ref[h])) * _sigmoid(pg_ref[:, DN_HEADS + h:DN_HEADS + h + 1] + dtb_ref[h])
            dpg_parts.append(jnp.where(lane == h, db, 0.0) + jnp.where(lane == DN_HEADS + h, da_in, 0.0))
            acc_parts.append(jnp.where((row8 == 0) & (lane8 == h), jnp.sum(dg_col * g_col, axis=0, keepdims=True), 0.0)
                             + jnp.where((row8 == 1) & (lane8 == h), jnp.sum(da_in, axis=0, keepdims=True), 0.0))

        _interleave(head_program(h) for h in range(DN_HEADS))
        dpg = sum(dpg_parts[1:], dpg_parts[0])
        dpg_ref[...] = _bf(jnp.concatenate([dpg, jnp.zeros_like(dpg)], axis=1))
        acc_ref[...] += sum(acc_parts[1:], acc_parts[0])

    smem = pl.BlockSpec(memory_space=pltpu.SMEM)
    wide = pl.BlockSpec((C, 512), lambda n: (n, 0))
    sq = pl.BlockSpec((DN_HEADS, C, CPAD), lambda n: (0, n, 0))
    narrow = pl.BlockSpec((C, 128), lambda n: (n, 0))
    qkv_spec = pl.BlockSpec((C, 1536), lambda n: (n, 0))
    f = lambda *shp: jax.ShapeDtypeStruct(shp, F32)
    return pl.pallas_call(
        body, grid=(nc,), name="dn_chunk_bwd",
        in_specs=[smem, smem, qkv_spec, pl.BlockSpec((C, 128), lambda n: (n, BLK_G)), sq, narrow, wide, wide, wide, wide, sq,
                  narrow, pl.BlockSpec(memory_space=pl.ANY)],
        out_specs=[qkv_spec, pl.BlockSpec((C, 256), lambda n: (n, BLK_G_PAD)), pl.BlockSpec((8, 128), lambda n: (0, 0))],
        out_shape=[f(S, 1536), jax.ShapeDtypeStruct(dproj.shape, dproj.dtype), f(8, 128)],
        input_output_aliases={12: 1},
        compiler_params=_params(("arbitrary",)),
    )(a_log, dt_bias, qkv, pg, t_inv, gcs, du, dw, dqg, dkd, da, dsc, dproj)


def _fill_kv(dk, dv, dproj):
    S = dk.shape[0]
    tm = min(512, S)

    def body(dk_ref, dv_ref, _, o_ref):
        o_ref[...] = _bf(jnp.concatenate([dk_ref[...], dv_ref[...]], axis=1))

    tile = pl.BlockSpec((tm, 128), lambda i: (i, 0))
    return pl.pallas_call(
        body, grid=(S // tm,), name="fill_kv",
        in_specs=[tile, tile, pl.BlockSpec(memory_space=pl.ANY)],
        out_specs=pl.BlockSpec((tm, 256), lambda i: (i, BLK_KV)),
        out_shape=jax.ShapeDtypeStruct(dproj.shape, dproj.dtype),
        input_output_aliases={2: 0},
        compiler_params=_params(("parallel",)),
    )(dk, dv, dproj)


def _w_in_to_internal(wt):
    return jnp.concatenate([wt[0:512], wt[2304:2816], wt[768:2304], wt[512:768], wt[2816:2824],
                            jnp.zeros((D_IN_PAD - D_IN, wt.shape[1]), wt.dtype)], axis=0)


def _w_in_from_internal(gt):
    return jnp.concatenate([gt[0:512], gt[2560:2816], gt[1024:2560], gt[512:1024], gt[2816:2824]], axis=0)


def _local_step(x, p, target, wts, first_weights, other_weights, ship_early):
    S = x.shape[0]
    cos, sin = _rope_tables(S)
    sinks, a_log, dt_bias = wts["sinks"].reshape(8), wts["a_log"].reshape(4), wts["dt_bias"].reshape(4)
    gn = wts["dn_norm"].reshape(1, DN_DIM)
    add = lambda acc, res: (acc + res,)

    u = _rmsnorm_fwd(x, wts["norm_mix"], "norm_mix_fwd")
    w_in_t, conv_w = first_weights(u)
    proj = _mm_nt(u, w_in_t, name="in_proj", out_dtype=F32, tn=512)
    attn = _attn_fwd(proj, cos, sin, sinks)
    qkv = _dn_prep_fwd(proj, conv_w)
    cw, cu, cqg, ckd, ca, ct, gcs = _dn_chunk_fwd(qkv, proj, a_log, dt_bias)
    o, vnew, sst, dn_out = _dn_scan_fwd(cw, cu, cqg, ckd, ca, gcs, proj, gn)
    w_o, w_up, w_down, w_pg, w_pp = other_weights(dn_out)
    h1a, = _mm_nn(attn, w_o, name="out_proj_attn", out_dtypes=[F32], tn=512, epi=add, extra=[x], w_row_block=0)
    h1, = _mm_nn(dn_out, w_o, name="out_proj_dn", out_dtypes=[F32], tn=512, epi=add, extra=[h1a], w_row_block=1)
    m = _rmsnorm_fwd(h1, wts["norm_mlp"], "norm_mlp_fwd")

    def relu2(acc):
        r = jnp.maximum(acc, 0.0)
        return r * r, r

    hid, relu = _mm_nn(m, w_up, name="mlp_up", out_dtypes=[BF16, BF16], tn=512, epi=relu2)
    h2, = _mm_nn(hid, w_down, name="mlp_down", out_dtypes=[F32], tn=512, epi=add, extra=[h1])
    n3 = _rmsnorm_fwd(h2, wts["norm_ple"], "norm_ple_fwd")
    pp, = _mm_nn(p, w_pp, name="ple_proj", out_dtypes=[F32], tn=512)

    def ple(acc, h2_t, pp_t):
        gate = _sigmoid(acc)
        return h2_t + gate * pp_t, gate

    h3, gate = _mm_nn(n3, w_pg, name="ple_gate", out_dtypes=[F32, F32], tn=512, epi=ple, extra=[h2, pp])
    dh3, loss, d_norm_final = _final_loss(h3, wts["norm_final"].reshape(1, D_MODEL), target)

    g = {"norm_final": d_norm_final}
    dgl, dpp = _ple_bwd(dh3, pp, gate)
    early = {"w_ple_gate": _mm_tn(n3, dgl, name="d_w_ple_gate", tm=512, tn=1024, out_dtype=BF16).reshape(N_DEV, 128, 1024),
             "w_ple_proj": _mm_tn(p, dpp, name="d_w_ple_proj", tm=256, tn=128, out_dtype=BF16, column_shards=True)}
    dn3 = _mm_nt(dgl, w_pg, name="d_n3", out_dtype=F32, tn=512)
    dh2, g["norm_ple"] = _rmsnorm_bwd(h2, wts["norm_ple"], dn3, dh3, "norm_ple_bwd")
    d_act = _mm_nt(dh2, w_down, name="d_hidden", out_dtype=BF16, tn=512, epi=lambda acc, r: acc * (2.0 * r.astype(F32)), extra=[relu])
    early["w_down"] = _mm_tn(hid, dh2, name="d_w_down", tm=512, tn=1024, out_dtype=BF16).reshape(N_DEV, 512, 1024)
    early["w_up"] = _mm_tn(m, d_act, name="d_w_up", tm=1024, tn=512, out_dtype=BF16, column_shards=True)
    token = ship_early(early)
    dm = _mm_nt(d_act, w_up, name="d_m", out_dtype=F32, tn=512)
    dh1, g["norm_mlp"] = _rmsnorm_bwd(h1, wts["norm_mlp"] + token[0:1, 0:1], dm, dh2, "norm_mlp_bwd")
    dcat = _mm_nt(dh1, w_o, name="d_cat", out_dtype=F32, tn=512)
    d_w_o = jnp.concatenate([_mm_tn(attn, dh1, name="d_w_o_attn", tm=512, tn=512, out_dtype=BF16),
                             _mm_tn(dn_out, dh1, name="d_w_o_dn", tm=512, tn=512, out_dtype=BF16)], axis=0)
    token = ship_early({"w_o": d_w_o.reshape(N_DEV, 128, 1024)})
    dproj, dk, dv, dsinks = _attn_bwd(proj, cos, sin, sinks + token[0, 0], dcat)
    g["sinks"] = dsinks[:, 0].reshape(1, 8)
    du_, dw_, dqg, dkd, da, dproj, dsc, g["dn_norm"] = _dn_scan_bwd(dcat, o, proj, gn, sst, vnew, cw, cqg, ckd, ca, gcs, dproj)
    dqkv, dproj, gate_acc = _dn_chunk_bwd(qkv, proj, ct, gcs, du_, dw_, dqg, dkd, da, dsc, a_log, dt_bias, dproj)
    g["a_log"], g["dt_bias"] = gate_acc[0:1, 0:4], gate_acc[1:2, 0:4]
    dproj, g["conv_w"] = _dn_prep_bwd(proj, conv_w, dqkv, dproj)
    dproj = _fill_kv(dk, dv, dproj)
    g["w_in"] = _mm_tn(dproj, u, name="d_w_in", tm=512, tn=1024)
    du_in, = _mm_nn(dproj, w_in_t, name="d_u", out_dtypes=[F32], tn=512)
    grad_x, g["norm_mix"] = _rmsnorm_bwd(x, wts["norm_mix"], du_in, dh1, "norm_mix_bwd")
    return loss, grad_x, g


def _peer(k):
    x, y, c = lax.axis_index("x"), lax.axis_index("y"), lax.axis_index("c")
    px = 1 - x if k & 4 else x
    py = 1 - y if k & 2 else y
    pc = 1 - c if k & 1 else c
    return (px, py, pc), 4 * px + 2 * py + pc


def _exchange(srcs, name, gather):
    n = len(srcs)
    gathers = list(gather) if isinstance(gather, (list, tuple)) else [gather] * n
    shapes = [(N_DEV,) + s.shape if gt else s.shape for s, gt in zip(srcs, gathers)]

    def body(*refs):
        src_refs, out_refs = refs[:n], refs[n:2 * n]
        send_sems, recv_sems, local_sems = refs[2 * n:]
        _, me = _peer(0)
        piece = lambda a, d: src_refs[a] if gathers[a] else src_refs[a].at[d]
        local = [pltpu.make_async_copy(piece(a, me), out_refs[a].at[me], local_sems.at[a]) for a in range(n)]
        for cp in local:
            cp.start()
        copies = []
        for a in range(n):
            for k in range(1, N_DEV):
                dev, idx = _peer(k)
                cp = pltpu.make_async_remote_copy(src_ref=piece(a, idx), dst_ref=out_refs[a].at[me],
                                                  send_sem=send_sems.at[a, k - 1], recv_sem=recv_sems.at[a, k - 1],
                                                  device_id=dev, device_id_type=MESH)
                cp.start()
                copies.append(cp)
        for cp in copies:
            cp.wait_recv()
        for cp in copies:
            cp.wait_send()
        for cp in local:
            cp.wait()

    anywhere = pl.BlockSpec(memory_space=pl.ANY)
    return pl.pallas_call(
        body, name=name, in_specs=[anywhere] * n, out_specs=[anywhere] * n,
        out_shape=[jax.ShapeDtypeStruct(shp, s.dtype) for shp, s in zip(shapes, srcs)],
        scratch_shapes=[pltpu.SemaphoreType.DMA((n, N_DEV - 1)), pltpu.SemaphoreType.DMA((n, N_DEV - 1)),
                        pltpu.SemaphoreType.DMA((n,))],
    )(*srcs)


_HBM = pl.BlockSpec(memory_space=pltpu.HBM)
_SEM = pl.BlockSpec(memory_space=pltpu.SEMAPHORE)
_EFFECT = pltpu.SideEffectType.DATAFLOW_SIDE_EFFECTING


def _split_copies(src_refs, land_refs, send_sems, recv_sems, gather):
    _, me = _peer(0)
    copies = []
    for a, (src, land) in enumerate(zip(src_refs, land_refs)):
        for k in range(1, N_DEV):
            dev, idx = _peer(k)
            sem = a * (N_DEV - 1) + k - 1
            copies.append(pltpu.make_async_remote_copy(
                src_ref=src if gather else src.at[idx], dst_ref=land.at[me], send_sem=send_sems.at[sem],
                recv_sem=recv_sems.at[sem], device_id=dev, device_id_type=MESH))
    return copies


def _exchange_start(srcs, name, gather):
    n = len(srcs)
    me = 4 * lax.axis_index("x") + 2 * lax.axis_index("y") + lax.axis_index("c")
    lands = []
    for s in srcs:
        own = s if gather else lax.dynamic_index_in_dim(s, me, 0, keepdims=False)
        shape = (N_DEV,) + s.shape if gather else s.shape
        lands.append(lax.dynamic_update_index_in_dim(lax.empty(shape, s.dtype), own, me, 0))

    def body(*refs):
        src_refs, land_refs = refs[:n], refs[n:2 * n]
        send_sems, recv_sems = refs[2 * n], refs[2 * n + 1]
        for cp in _split_copies(src_refs, land_refs, send_sems, recv_sems, gather):
            cp.start()
        refs[-1][...] = jnp.zeros_like(refs[-1])

    both = list(srcs) + lands
    sems = pltpu.SemaphoreType.DMA((n * (N_DEV - 1),))
    out = pl.pallas_call(
        body, name=name,
        out_shape=(sems, sems, *[pltpu.HBM(t.shape, t.dtype) for t in both], jax.ShapeDtypeStruct((8, 128), F32)),
        in_specs=[_HBM] * (2 * n), out_specs=(_SEM, _SEM, *[_HBM] * (2 * n), pl.BlockSpec(memory_space=pltpu.VMEM)),
        input_output_aliases={i: 2 + i for i in range(2 * n)},
        compiler_params=pltpu.CompilerParams(has_side_effects=_EFFECT),
    )(*[pltpu.with_memory_space_constraint(t, pltpu.HBM) for t in both])
    return (n, gather, out[:-1]), out[-1]


def _exchange_wait(handle, after, name):
    n, gather, (send_sems, recv_sems, *both) = handle

    def body(*refs):
        src_refs, land_refs = refs[:n], refs[n:2 * n]
        for cp in _split_copies(src_refs, land_refs, refs[2 * n], refs[2 * n + 1], gather):
            cp.wait_send()
            cp.wait_recv()

    out = pl.pallas_call(
        body, name=name, out_shape=tuple(pltpu.HBM(t.shape, t.dtype) for t in both),
        in_specs=[_HBM] * (2 * n) + [_SEM, _SEM, pl.BlockSpec(memory_space=pl.ANY)], out_specs=tuple([_HBM] * (2 * n)),
        input_output_aliases={i: i for i in range(2 * n)},
        compiler_params=pltpu.CompilerParams(has_side_effects=_EFFECT),
    )(*both, send_sems, recv_sems, after)
    return list(out[n:])


def _adamw(parts, w, m, v, name):
    n, R, W = parts.shape
    tm = 128 if R % 128 == 0 else R

    def body(p_ref, w_ref, m_ref, v_ref, g_ref, d_ref, nm_ref, nv_ref):
        g = p_ref[0].astype(F32)
        for s in range(1, n):
            g = g + p_ref[s].astype(F32)
        nm = ADAM_B1 * m_ref[...] + (1.0 - ADAM_B1) * g
        nv = ADAM_B2 * v_ref[...] + (1.0 - ADAM_B2) * (g * g)
        m_hat = nm / (1.0 - ADAM_B1 ** ADAM_STEP)
        v_hat = nv / (1.0 - ADAM_B2 ** ADAM_STEP)
        g_ref[...] = g
        d_ref[...] = -ADAM_LR * (m_hat / (jnp.sqrt(v_hat) + ADAM_EPS) + ADAM_WD * w_ref[...])
        nm_ref[...] = nm
        nv_ref[...] = nv

    tile = pl.BlockSpec((tm, W), lambda i: (i, 0))
    return pl.pallas_call(
        body, grid=(R // tm,), name=name,
        in_specs=[pl.BlockSpec((n, tm, W), lambda i: (0, i, 0)), tile, tile, tile],
        out_specs=[tile] * 4, out_shape=[jax.ShapeDtypeStruct((R, W), F32)] * 4,
        compiler_params=_params(("parallel",)),
    )(parts, w, m, v)


_MATRICES = ("w_in", "w_o", "w_up", "w_down", "w_ple_gate", "w_ple_proj")


_OTHERS = ("w_o", "w_up", "w_down", "w_ple_gate", "w_ple_proj")


def _cols_from_shards(t):
    return jnp.transpose(t, (1, 0, 2)).reshape(t.shape[1], N_DEV * t.shape[2])


_SMALL_ROWS = 16
_VEC_ROW = {"norm_mix": 0, "norm_mlp": 1, "norm_ple": 2, "norm_final": 3}
_VEC_LANES = {"a_log": (0, 4), "dt_bias": (4, 8), "sinks": (8, 16), "dn_norm": (128, 256)}
_LOSS_ROW, _CONV_ROW = 5, 8


def _pack_small(vals, extra_rows):
    row4 = jnp.zeros((1024,), F32)
    for n, (a, b) in _VEC_LANES.items():
        row4 = row4.at[a:b].set(vals[n].reshape(b - a))
    rows = [vals[n].reshape(1, 1024) for n in ("norm_mix", "norm_mlp", "norm_ple", "norm_final")] + [row4.reshape(1, 1024)]
    return jnp.concatenate(rows + extra_rows, axis=0)


def _unpack_small(buf, like):
    out = {n: buf[r].reshape(like[n].shape) for n, r in _VEC_ROW.items()}
    for n, (a, b) in _VEC_LANES.items():
        out[n] = buf[4, a:b].reshape(like[n].shape)
    return out


_ORDER = ("norm_mix", "w_in", "conv_w", "a_log", "dt_bias", "dn_norm", "sinks", "w_o", "norm_mlp", "w_up", "w_down",
          "norm_ple", "w_ple_gate", "w_ple_proj", "norm_final")


def kernel(x, p, norm_mix, w_in, conv_w, a_log, dt_bias, dn_norm, sinks, w_o, norm_mlp, w_up, w_down, norm_ple, w_ple_gate, w_ple_proj, norm_final, loss_target, m_norm_mix, m_w_in, m_conv_w, m_a_log, m_dt_bias, m_dn_norm, m_sinks, m_w_o, m_norm_mlp, m_w_up, m_w_down, m_norm_ple, m_w_ple_gate, m_w_ple_proj, m_norm_final, v_norm_mix, v_w_in, v_conv_w, v_a_log, v_dt_bias, v_dn_norm, v_sinks, v_w_o, v_norm_mlp, v_w_up, v_w_down, v_norm_ple, v_w_ple_gate, v_w_ple_proj, v_norm_final):
    w = dict(norm_mix=norm_mix, w_in=w_in[0], conv_w=conv_w[0], a_log=a_log, dt_bias=dt_bias, dn_norm=dn_norm, sinks=sinks,
             w_o=w_o[0], norm_mlp=norm_mlp, w_up=w_up[0], w_down=w_down[0], norm_ple=norm_ple, w_ple_gate=w_ple_gate[0],
             w_ple_proj=w_ple_proj[0], norm_final=norm_final)
    m = dict(norm_mix=m_norm_mix, w_in=m_w_in[0], conv_w=m_conv_w[0], a_log=m_a_log, dt_bias=m_dt_bias, dn_norm=m_dn_norm,
             sinks=m_sinks, w_o=m_w_o[0], norm_mlp=m_norm_mlp, w_up=m_w_up[0], w_down=m_w_down[0], norm_ple=m_norm_ple,
             w_ple_gate=m_w_ple_gate[0], w_ple_proj=m_w_ple_proj[0], norm_final=m_norm_final)
    v = dict(norm_mix=v_norm_mix, w_in=v_w_in[0], conv_w=v_conv_w[0], a_log=v_a_log, dt_bias=v_dt_bias, dn_norm=v_dn_norm,
             sinks=v_sinks, w_o=v_w_o[0], norm_mlp=v_norm_mlp, w_up=v_w_up[0], w_down=v_w_down[0], norm_ple=v_norm_ple,
             w_ple_gate=v_w_ple_gate[0], w_ple_proj=v_w_ple_proj[0], norm_final=v_norm_final)
    me = 4 * lax.axis_index("x") + 2 * lax.axis_index("y") + lax.axis_index("c")
    conv_shard = conv_w.shape[2]

    for d in (w, m, v):
        d["w_in"] = d["w_in"].T
    conv_pad = jnp.pad(w["conv_w"], ((0, 8 - DN_CONV), (0, 256 - conv_shard)))
    first, token_first = _exchange_start([_bf(w["w_in"]), conv_pad], "gather_first_start", gather=True)
    later = [_bf(w[n]) for n in _OTHERS]
    later[-1] = _bf(w["w_ple_proj"] + token_first[0:1, 0:1])
    others, token_others = _exchange_start(later, "gather_others_start", gather=True)
    vectors = dict(w)
    vectors["norm_mix"] = w["norm_mix"] + token_others[0:1, 0:1]

    def first_weights(after):
        w_in_all, conv_all = _exchange_wait(first, after, "gather_first_wait")
        conv_all = jnp.transpose(conv_all[:, :DN_CONV, :conv_shard], (1, 0, 2)).reshape(DN_CONV, N_DEV * conv_shard)
        return _w_in_to_internal(w_in_all.reshape(D_IN, D_MODEL)), conv_all

    def other_weights(after):
        w_o_all, w_up_all, w_down_all, w_pg_all, w_pp_all = _exchange_wait(others, after, "gather_others_wait")
        return (w_o_all.reshape(1024, 1024), _cols_from_shards(w_up_all), w_down_all.reshape(4096, 1024),
                w_pg_all.reshape(1024, 1024), _cols_from_shards(w_pp_all))

    shipped = []

    def ship_early(pieces):
        names = tuple(pieces)
        handle, token = _exchange_start([pieces[n] for n in names], "scatter_start_" + names[0], gather=False)
        shipped.append((names, handle))
        return token

    loss, grad_x, g = _local_step(x[0], p[0, 0], loss_target[0], vectors, first_weights, other_weights, ship_early)

    w_in_pieces = _bf(_w_in_from_internal(g["w_in"])).reshape(N_DEV, D_IN // N_DEV, D_MODEL)
    small = _pack_small(g, [loss[:, :1] * jnp.ones((1, 1024), F32), jnp.zeros((2, 1024), F32),
                            g["conv_w"].reshape(6, 1024), jnp.zeros((2, 1024), F32)])
    w_in_received, small_all = _exchange([w_in_pieces, small], "scatter_late", gather=[False, True])
    received = {"w_in": w_in_received}
    for names, handle in shipped:
        received.update(zip(names, _exchange_wait(handle, grad_x, "scatter_wait_" + names[0])))
    big = {n: _adamw(received[n], w[n], m[n], v[n], "adamw_" + n) for n in _MATRICES}
    zeros16 = jnp.zeros((_SMALL_ROWS, 1024), F32)
    summed = _adamw(small_all, zeros16, zeros16, zeros16, "sum_small")[0]
    conv_g = lax.dynamic_slice(summed[_CONV_ROW:_CONV_ROW + 6].reshape(DN_CONV, N_DEV * conv_shard), (0, me * conv_shard),
                               (DN_CONV, conv_shard))
    pad_conv = lambda t: jnp.pad(t.reshape(1, DN_CONV * conv_shard), ((0, 2), (0, 1024 - DN_CONV * conv_shard)))
    small_g = jnp.concatenate([summed[0:5], pad_conv(conv_g)], axis=0)[None]
    pack8 = lambda d: _pack_small(d, [pad_conv(d["conv_w"])])
    sm = _adamw(small_g, pack8(w), pack8(m), pack8(v), "adamw_vectors")

    outs = []
    for i, small_buf in enumerate(sm):
        d = {n: big[n][i] for n in _MATRICES}
        d.update(_unpack_small(small_buf, w))
        d["conv_w"] = small_buf[5, :DN_CONV * conv_shard].reshape(DN_CONV, conv_shard)
        outs.append(d)
    result = [summed[_LOSS_ROW, 0], grad_x[None]]
    for d in outs:
        d["w_in"] = d["w_in"].T
        for n in _ORDER:
            result.append(d[n][None] if n in _MATRICES or n == "conv_w" else d[n].reshape(w[n].shape))
    return tuple(result)
```

```python
import jax
import jax.numpy as jnp
from jax import lax
from jax.experimental import pallas as pl
from jax.experimental.pallas import tpu as pltpu

F32, BF16 = jnp.float32, jnp.bfloat16
EPS = 1e-6
D_MODEL = 1024
N_DEV = 8
ATTN_BLOCK = 128
HEAD_PAIR = 128
DN_HEADS = 4
DN_DIM = 128
DN_CHUNK = 64
DN_CONV = 4
ROPE_THETA = 10000.0
D_IN = 2824
D_IN_PAD = 3072
BLK_Q, BLK_Z = 0, 1
BLK_DN, BLK_K, BLK_V, BLK_G = 8, 20, 21, 22
BLK_KV, BLK_G_PAD = 10, 11
VMEM_LIMIT = 56 * 1024 * 1024
NEG = -1e30
ADAM_LR, ADAM_B1, ADAM_B2, ADAM_EPS, ADAM_WD, ADAM_STEP = 0.001, 0.9, 0.999, 1e-08, 0.01, 10
MESH = pl.DeviceIdType.MESH


def _bf(x):
    return x.astype(BF16)


def _dot(a, b):
    return jnp.dot(a, b, preferred_element_type=F32)


def _dot_nt(a, b):
    return lax.dot_general(a, b, (((1,), (1,)), ((), ())), preferred_element_type=F32)


def _dot_tn(a, b):
    return lax.dot_general(a, b, (((0,), (0,)), ((), ())), preferred_element_type=F32)


def _sigmoid(x):
    return 1.0 / (1.0 + jnp.exp(-x))


def _params(sem):
    return pltpu.CompilerParams(dimension_semantics=sem, vmem_limit_bytes=VMEM_LIMIT)


def _mm_nn(x, w, *, name, out_dtypes, tn, epi=None, extra=(), tm=512, w_row_block=0):
    S, K = x.shape
    N = w.shape[1]
    r0 = w_row_block * K
    tm = min(tm, S)
    n_extra = len(extra)

    def body(x_ref, w_ref, *rest):
        xb = _bf(x_ref[...])
        for c in range(N // tn):
            cols = slice(c * tn, (c + 1) * tn)
            acc = _dot(xb, w_ref[r0:r0 + K, cols])
            res = epi(acc, *[r[:, cols] for r in rest[:n_extra]]) if epi else (acc,)
            for o, r in zip(rest[n_extra:], res):
                o[:, cols] = r.astype(o.dtype)

    tile = pl.BlockSpec((tm, N), lambda i: (i, 0))
    return pl.pallas_call(
        body, grid=(S // tm,), name=name,
        in_specs=[pl.BlockSpec((tm, K), lambda i: (i, 0)), pl.BlockSpec(w.shape, lambda i: (0, 0))] + [tile] * n_extra,
        out_specs=[tile] * len(out_dtypes),
        out_shape=[jax.ShapeDtypeStruct((S, N), dt) for dt in out_dtypes],
        compiler_params=_params(("parallel",)),
    )(x, w, *extra)


def _mm_nt(dy, w, *, name, out_dtype, tn, epi=None, extra=(), tm=512):
    S, N = dy.shape
    K = w.shape[0]
    tm = min(tm, S)
    n_extra = len(extra)

    def body(dy_ref, w_ref, *rest):
        dyb = _bf(dy_ref[...])
        for c in range(K // tn):
            cols = slice(c * tn, (c + 1) * tn)
            acc = _dot_nt(dyb, w_ref[cols, :])
            if epi:
                acc = epi(acc, *[r[:, cols] for r in rest[:n_extra]])
            rest[n_extra][:, cols] = acc.astype(out_dtype)

    tile = pl.BlockSpec((tm, K), lambda i: (i, 0))
    return pl.pallas_call(
        body, grid=(S // tm,), name=name,
        in_specs=[pl.BlockSpec((tm, N), lambda i: (i, 0)), pl.BlockSpec(w.shape, lambda i: (0, 0))] + [tile] * n_extra,
        out_specs=tile,
        out_shape=jax.ShapeDtypeStruct((S, K), out_dtype),
        compiler_params=_params(("parallel",)),
    )(dy, w, *extra)


def _mm_tn(x, dy, *, name, tm, tn, out_dtype=F32, column_shards=False):
    S, K = x.shape
    N = dy.shape[1]

    def body(x_ref, dy_ref, o_ref):
        o_ref[...] = _dot_tn(_bf(x_ref[...]), _bf(dy_ref[...])).astype(out_dtype)

    if column_shards:
        out_spec = pl.BlockSpec((None, tm, tn), lambda i, j: (j, i, 0))
        out_shape = jax.ShapeDtypeStruct((N // tn, K, tn), out_dtype)
    else:
        out_spec = pl.BlockSpec((tm, tn), lambda i, j: (i, j))
        out_shape = jax.ShapeDtypeStruct((K, N), out_dtype)
    return pl.pallas_call(
        body, grid=(K // tm, N // tn), name=name,
        in_specs=[pl.BlockSpec((S, tm), lambda i, j: (0, i)), pl.BlockSpec((S, tn), lambda i, j: (0, j))],
        out_specs=out_spec, out_shape=out_shape,
        compiler_params=_params(("parallel", "parallel")),
    )(x, dy)


def _rowwise(body, *, tiled, full, out_tiled, out_acc, name, tm=512, smem=()):
    S = tiled[0].shape[0]
    tm = min(tm, S)
    n_in = len(smem) + len(tiled) + len(full)

    def kern(*refs):
        @pl.when(pl.program_id(0) == 0)
        def _():
            for r in refs[n_in + len(out_tiled):]:
                r[...] = jnp.zeros_like(r)
        body(*refs)

    in_specs = [pl.BlockSpec(memory_space=pltpu.SMEM) for _ in smem]
    in_specs += [pl.BlockSpec((tm, a.shape[1]), lambda i: (i, 0)) for a in tiled]
    in_specs += [pl.BlockSpec(a.shape, lambda i, nd=a.ndim: (0,) * nd) for a in full]
    out_specs = [pl.BlockSpec((tm, w), lambda i: (i, 0)) for w, _ in out_tiled]
    out_specs += [pl.BlockSpec(shp, lambda i, nd=len(shp): (0,) * nd) for shp, _ in out_acc]
    out_shape = [jax.ShapeDtypeStruct((S, w), dt) for w, dt in out_tiled]
    out_shape += [jax.ShapeDtypeStruct(shp, dt) for shp, dt in out_acc]
    return pl.pallas_call(
        kern, grid=(S // tm,), name=name, in_specs=in_specs, out_specs=out_specs, out_shape=out_shape,
        compiler_params=_params(("arbitrary",)),
    )(*smem, *tiled, *full)


def _rms_stats(x):
    r = lax.rsqrt(jnp.mean(x * x, axis=-1, keepdims=True) + EPS)
    return r, x * r


def _rmsnorm_fwd(x, g, name):
    def body(x_ref, g_ref, o_ref):
        _, xh = _rms_stats(x_ref[...])
        o_ref[...] = _bf(xh * g_ref[...])

    return _rowwise(body, tiled=[x], full=[g], out_tiled=[(x.shape[1], BF16)], out_acc=[], name=name)[0]


def _rms_bwd_tile(x, g, dxn):
    r, xh = _rms_stats(x)
    dg = jnp.sum(dxn * xh, axis=0, keepdims=True)
    dn = dxn * g
    dx = r * (dn - xh * jnp.mean(dn * xh, axis=-1, keepdims=True))
    return dx, dg


def _rmsnorm_bwd(x, g, dxn, dres, name):
    def body(x_ref, dxn_ref, dres_ref, g_ref, dx_ref, dg_ref):
        dx, dg = _rms_bwd_tile(x_ref[...], g_ref[...], dxn_ref[...])
        dx_ref[...] = dres_ref[...] + dx
        dg_ref[...] += dg

    n = x.shape[1]
    return _rowwise(body, tiled=[x, dxn, dres], full=[g], out_tiled=[(n, F32)], out_acc=[((1, n), F32)], name=name)


def _final_loss(h3, g, target):
    n = h3.shape[1]

    def body(h_ref, t_ref, g_ref, dh_ref, loss_ref, dg_ref):
        x = h_ref[...]
        _, xh = _rms_stats(x)
        e = xh * g_ref[...] - t_ref[...]
        per_tok = jnp.mean(e * e, axis=-1, keepdims=True)
        loss_ref[...] += 0.5 * jnp.sum(per_tok, axis=0, keepdims=True)
        dx, dg = _rms_bwd_tile(x, g_ref[...], e * (1.0 / n))
        dh_ref[...] = dx
        dg_ref[...] += dg

    return _rowwise(body, tiled=[h3, target], full=[g], out_tiled=[(n, F32)],
                    out_acc=[((1, 128), F32), ((1, n), F32)], name="final_loss")


def _ple_bwd(dh3, pp, gate):
    def body(dh_ref, pp_ref, gate_ref, dgl_ref, dpp_ref):
        dh, gt = dh_ref[...], gate_ref[...]
        dgl_ref[...] = _bf(dh * pp_ref[...] * gt * (1.0 - gt))
        dpp_ref[...] = _bf(dh * gt)

    n = dh3.shape[1]
    return _rowwise(body, tiled=[dh3, pp, gate], full=[], out_tiled=[(n, BF16), (n, BF16)], out_acc=[], name="ple_bwd")


def _rope_tables(S):
    half = 32
    inv = 1.0 / (ROPE_THETA ** (jnp.arange(half, dtype=F32) * (2.0 / 64)))
    ang = jnp.arange(S).astype(F32)[:, None] * inv[None, :]
    cos, sin = jnp.cos(ang), jnp.sin(ang)
    return jnp.tile(cos, (1, 4)), jnp.concatenate([-sin, sin, -sin, sin], axis=1)


def _attn_common(i, kc, kp, vc, vp, cc, sc, cp, sp):
    lane = lax.broadcasted_iota(jnp.int32, (1, HEAD_PAIR), 1)
    lane_lo = jnp.bitwise_and(lane, 63) < 32
    slot = [lane < 64, lane >= 64]

    def swap_halves(t):
        return jnp.where(lane_lo, pltpu.roll(t, 96, 1), pltpu.roll(t, 32, 1))

    def rope(t, cos, sin):
        return t * cos + swap_halves(t) * sin

    def unrope(d, cos, sin):
        return d * cos + swap_halves(d * sin)

    k2 = jnp.concatenate([rope(kp, cp, sp), rope(kc, cc, sc)], axis=0)
    v2 = jnp.concatenate([vp, vc], axis=0)
    r = lax.broadcasted_iota(jnp.int32, (ATTN_BLOCK, 2 * ATTN_BLOCK), 0)
    c = lax.broadcasted_iota(jnp.int32, (ATTN_BLOCK, 2 * ATTN_BLOCK), 1)
    valid = (c > r) & (c <= r + ATTN_BLOCK) & jnp.logical_or(c >= ATTN_BLOCK, i > 0)
    ks, vs = {}, {}
    for j in range(2):
        kn = jnp.where(slot[j], k2, 0.0)
        vn = jnp.where(slot[j], v2, 0.0)
        for s in range(2):
            ks[j, s] = _bf(kn if s == j else pltpu.roll(kn, 64, 1))
            vs[j, s] = _bf(vn if s == j else pltpu.roll(vn, 64, 1))
    return slot, rope, unrope, valid, ks, vs


def _attn_probs(scores, valid, sink):
    s = jnp.where(valid, scores * 0.125, NEG)
    m = jnp.maximum(jnp.max(s, axis=1, keepdims=True), sink)
    e = jnp.exp(s - m)
    inv_z = 1.0 / (jnp.sum(e, axis=1, keepdims=True) + jnp.exp(sink - m))
    return e * inv_z, jnp.exp(sink - m) * inv_z


def _attn_specs(S):
    nb = S // ATTN_BLOCK
    prev = lambda i: jnp.maximum(i - 1, 0)
    blk = lambda w, col, row=(lambda i: i): pl.BlockSpec((ATTN_BLOCK, w), lambda i: (row(i), col))
    in_specs = [pl.BlockSpec(memory_space=pltpu.SMEM),
                blk(512, BLK_Q), blk(128, BLK_K), blk(128, BLK_K, prev), blk(128, BLK_V), blk(128, BLK_V, prev),
                blk(128, 0), blk(128, 0), blk(128, 0, prev), blk(128, 0, prev)]
    return nb, in_specs


def _attn_fwd(pa, cos, sin, sinks):
    S = pa.shape[0]
    nb, in_specs = _attn_specs(S)

    def body(sinks_ref, q_ref, kc_ref, kp_ref, vc_ref, vp_ref, cc_ref, sc_ref, cp_ref, sp_ref, o_ref):
        i = pl.program_id(0)
        cc, sc = cc_ref[...], sc_ref[...]
        _, rope, _, valid, ks, vs = _attn_common(i, kc_ref[...], kp_ref[...], vc_ref[...], vp_ref[...],
                                                 cc, sc, cp_ref[...], sp_ref[...])
        pair_cols = [slice(HEAD_PAIR * pair, HEAD_PAIR * (pair + 1)) for pair in range(4)]
        qps = [_bf(rope(q_ref[:, cols], cc, sc)) for cols in pair_cols]
        outs = {}

        def head_program(h):
            pair, s = divmod(h, 2)
            j = h // 4
            scores = _dot_nt(qps[pair], ks[j, s])
            yield
            p, _ = _attn_probs(scores, valid, sinks_ref[h])
            outs[h] = _dot(_bf(p), vs[j, s])

        _interleave(head_program(h) for h in range(8))
        for pair, cols in enumerate(pair_cols):
            o_ref[:, cols] = outs[2 * pair] + outs[2 * pair + 1]

    return pl.pallas_call(
        body, grid=(nb,), name="attn_fwd", in_specs=in_specs,
        out_specs=pl.BlockSpec((ATTN_BLOCK, 512), lambda i: (i, 0)),
        out_shape=jax.ShapeDtypeStruct((S, 512), F32),
        compiler_params=_params(("parallel",)),
    )(sinks, pa, pa, pa, pa, pa, cos, sin, cos, sin)


def _attn_bwd(pa, cos, sin, sinks, dcat):
    S = pa.shape[0]
    nb, in_specs = _attn_specs(S)
    in_specs = in_specs + [pl.BlockSpec((ATTN_BLOCK, 512), lambda i: (i, 0))]

    def body(sinks_ref, q_ref, kc_ref, kp_ref, vc_ref, vp_ref, cc_ref, sc_ref, cp_ref, sp_ref, do_ref,
             dq_ref, dk_ref, dv_ref, dsink_ref):
        i = pl.program_id(0)

        @pl.when(i == 0)
        def _():
            dk_ref[...] = jnp.zeros_like(dk_ref)
            dv_ref[...] = jnp.zeros_like(dv_ref)
            dsink_ref[...] = jnp.zeros_like(dsink_ref)

        cc, sc, cp, sp = cc_ref[...], sc_ref[...], cp_ref[...], sp_ref[...]
        slot, rope, unrope, valid, ks, vs = _attn_common(i, kc_ref[...], kp_ref[...], vc_ref[...], vp_ref[...], cc, sc, cp, sp)
        pair_cols = [slice(HEAD_PAIR * pair, HEAD_PAIR * (pair + 1)) for pair in range(4)]
        qps = [_bf(rope(q_ref[:, cols], cc, sc)) for cols in pair_cols]
        dobs = [_bf(do_ref[:, cols]) for cols in pair_cols]
        dqs, dks, dvs = {}, {}, {}

        def head_program(h):
            pair, s = divmod(h, 2)
            j = h // 4
            qp, dob = qps[pair], dobs[pair]
            scores = _dot_nt(qp, ks[j, s])
            dp = _dot_nt(dob, vs[j, s])
            yield
            p, p_sink = _attn_probs(scores, valid, sinks_ref[h])
            dr = jnp.sum(p * dp, axis=1, keepdims=True)
            ds = _bf(p * (dp - dr) * 0.125)
            dsink_ref[h:h + 1, :] += -jnp.sum(p_sink * dr, axis=0, keepdims=True)
            dqs[h] = _dot(ds, ks[j, s])
            dk_h = _dot_tn(ds, qp)
            dv_h = _dot_tn(_bf(p), dob)
            yield
            dk_h, dv_h = jnp.where(slot[s], dk_h, 0.0), jnp.where(slot[s], dv_h, 0.0)
            if s != j:
                dk_h, dv_h = pltpu.roll(dk_h, 64, 1), pltpu.roll(dv_h, 64, 1)
            dks[h], dvs[h] = dk_h, dv_h

        _interleave(head_program(h) for h in range(8))
        dk2 = sum((dks[h] for h in range(1, 8)), dks[0])
        dv2 = sum((dvs[h] for h in range(1, 8)), dvs[0])
        for pair, cols in enumerate(pair_cols):
            dq_ref[:, cols] = _bf(unrope(dqs[2 * pair] + dqs[2 * pair + 1], cc, sc))
        cur = pl.ds(pl.multiple_of(i * ATTN_BLOCK, ATTN_BLOCK), ATTN_BLOCK)
        dk_ref[cur, :] += unrope(dk2[ATTN_BLOCK:], cc, sc)
        dv_ref[cur, :] += dv2[ATTN_BLOCK:]

        @pl.when(i > 0)
        def _():
            prv = pl.ds(pl.multiple_of((i - 1) * ATTN_BLOCK, ATTN_BLOCK), ATTN_BLOCK)
            dk_ref[prv, :] += unrope(dk2[:ATTN_BLOCK], cp, sp)
            dv_ref[prv, :] += dv2[:ATTN_BLOCK]

    whole = lambda w: pl.BlockSpec((S, w), lambda i: (0, 0))
    return pl.pallas_call(
        body, grid=(nb,), name="attn_bwd", in_specs=in_specs,
        out_specs=[pl.BlockSpec((ATTN_BLOCK, 512), lambda i: (i, BLK_Q)), whole(128), whole(128),
                   pl.BlockSpec((8, 128), lambda i: (0, 0))],
        out_shape=[jax.ShapeDtypeStruct((S, D_IN_PAD), BF16), jax.ShapeDtypeStruct((S, 128), F32),
                   jax.ShapeDtypeStruct((S, 128), F32), jax.ShapeDtypeStruct((8, 128), F32)],
        compiler_params=_params(("arbitrary",)),
    )(sinks, pa, pa, pa, pa, pa, cos, sin, cos, sin, dcat)


CONV_ROWS = 512
CONV_PAD = 8


def _conv_silu(scr, w, r0):
    y = w[3:4, :] * scr[pl.ds(CONV_PAD + r0, CONV_ROWS), :]
    for j in range(DN_CONV - 1):
        y = y + w[j:j + 1, :] * scr[pl.ds(CONV_PAD + r0 - 3 + j, CONV_ROWS), :]
    return y


def _dn_prep_fwd(pd, conv_w):
    S = pd.shape[0]
    assert S % CONV_ROWS == 0

    def body(x_ref, w_ref, o_ref, scr):
        b = pl.program_id(0)
        scr[0:CONV_PAD, :] = jnp.zeros((CONV_PAD, DN_DIM), F32)
        scr[pl.ds(CONV_PAD, S), :] = x_ref[...]
        w = w_ref[...]
        q_scale = jnp.where(b < DN_HEADS, DN_DIM ** -0.5, 1.0)
        for r0 in range(0, S, CONV_ROWS):
            y = _conv_silu(scr, w, r0)
            a = y * _sigmoid(y)
            rs = lax.rsqrt(jnp.sum(a * a, axis=1, keepdims=True) + EPS)
            o_ref[pl.ds(r0, CONV_ROWS), :] = a * jnp.where(b < 2 * DN_HEADS, rs * q_scale, 1.0)

    col = pl.BlockSpec((S, DN_DIM), lambda b: (0, b))
    return pl.pallas_call(
        body, grid=(3 * DN_HEADS,), name="dn_prep_fwd",
        in_specs=[pl.BlockSpec((S, DN_DIM), lambda b: (0, BLK_DN + b)), pl.BlockSpec((DN_CONV, DN_DIM), lambda b: (0, b))],
        out_specs=col,
        out_shape=jax.ShapeDtypeStruct((S, 3 * DN_HEADS * DN_DIM), F32),
        scratch_shapes=[pltpu.VMEM((S + CONV_PAD, DN_DIM), F32)],
        compiler_params=_params(("parallel",)),
    )(pd, conv_w)


def _dn_prep_bwd(pd, conv_w, dqkv, dproj):
    S = pd.shape[0]

    def body(x_ref, w_ref, d_ref, _, dx_ref, dw_ref, scr, dscr):
        b = pl.program_id(0)
        scr[0:CONV_PAD, :] = jnp.zeros((CONV_PAD, DN_DIM), F32)
        scr[pl.ds(CONV_PAD, S), :] = x_ref[...]
        dscr[pl.ds(S, CONV_PAD), :] = jnp.zeros((CONV_PAD, DN_DIM), F32)
        w = w_ref[...]
        q_scale = jnp.where(b < DN_HEADS, DN_DIM ** -0.5, 1.0)
        is_qk = b < 2 * DN_HEADS
        dw = [jnp.zeros((1, DN_DIM), F32) for _ in range(DN_CONV)]
        for r0 in range(0, S, CONV_ROWS):
            y = _conv_silu(scr, w, r0)
            sg = _sigmoid(y)
            a = y * sg
            dout = d_ref[pl.ds(r0, CONV_ROWS), :]
            rs = lax.rsqrt(jnp.sum(a * a, axis=1, keepdims=True) + EPS)
            da_qk = q_scale * rs * (dout - a * (rs * rs) * jnp.sum(dout * a, axis=1, keepdims=True))
            dy = jnp.where(is_qk, da_qk, dout) * (sg * (1.0 + y * (1.0 - sg)))
            dscr[pl.ds(r0, CONV_ROWS), :] = dy
            for j in range(DN_CONV):
                dw[j] = dw[j] + jnp.sum(dy * scr[pl.ds(CONV_PAD + r0 - 3 + j, CONV_ROWS), :], axis=0, keepdims=True)
        for j in range(DN_CONV):
            dw_ref[j:j + 1, :] = dw[j]
        for r0 in range(0, S, CONV_ROWS):
            dx = w[3:4, :] * dscr[pl.ds(r0, CONV_ROWS), :]
            for j in range(DN_CONV - 1):
                dx = dx + w[j:j + 1, :] * dscr[pl.ds(r0 + 3 - j, CONV_ROWS), :]
            dx_ref[pl.ds(r0, CONV_ROWS), :] = _bf(dx)

    col = pl.BlockSpec((S, DN_DIM), lambda b: (0, b))
    proj_col = pl.BlockSpec((S, DN_DIM), lambda b: (0, BLK_DN + b))
    wcol = pl.BlockSpec((DN_CONV, DN_DIM), lambda b: (0, b))
    return pl.pallas_call(
        body, grid=(3 * DN_HEADS,), name="dn_prep_bwd",
        in_specs=[proj_col, wcol, col, pl.BlockSpec(memory_space=pl.ANY)], out_specs=[proj_col, wcol],
        out_shape=[jax.ShapeDtypeStruct(dproj.shape, dproj.dtype), jax.ShapeDtypeStruct((DN_CONV, 3 * DN_HEADS * DN_DIM), F32)],
        scratch_shapes=[pltpu.VMEM((S + CONV_PAD, DN_DIM), F32), pltpu.VMEM((S + CONV_PAD, DN_DIM), F32)],
        input_output_aliases={3: 0},
        compiler_params=_params(("parallel",)),
    )(pd, conv_w, dqkv, dproj)


CPAD = 128
CHUNKS_LOCAL = 2
CHUNKS_SCAN = 4


def _chunk_masks():
    ii = lax.broadcasted_iota(jnp.int32, (DN_CHUNK, CPAD), 0)
    jj = lax.broadcasted_iota(jnp.int32, (DN_CHUNK, CPAD), 1)
    return ii, jj


def _rows_pad(a):
    return jnp.concatenate([a, jnp.zeros_like(a)], axis=0)


def _hi_lo(a):
    hi = _bf(a)
    return hi, _bf(a - hi.astype(F32))


def _double_step(t, p):
    C = DN_CHUNK
    th, tl = _hi_lo(t)
    ph, pl_ = _hi_lo(p)
    r1 = _dot(jnp.concatenate([th, tl, ph, pl_], axis=0), _rows_pad(ph))
    r2 = _dot(jnp.concatenate([th, ph], axis=0), _rows_pad(pl_))
    return t + (r1[:C] + r1[C:2 * C] + r2[:C]), r1[2 * C:3 * C] + r1[3 * C:] + r2[C:]


def _dot3_nt(a, b):
    C = DN_CHUNK
    ah, al = _hi_lo(a)
    bh, bl = _hi_lo(b)
    r1 = _dot_nt(jnp.concatenate([ah, al], axis=0), _rows_pad(bh))
    return r1[:C] + r1[C:] + _dot_nt(ah, _rows_pad(bl))


def _dot3_tn(a, b):
    C = DN_CHUNK
    ah, al = _hi_lo(a)
    bh, bl = _hi_lo(b)
    return _dot_tn(ah, bh)[:C] + _dot_tn(al, bh)[:C] + _dot_tn(ah, bl)[:C]


def _interleave(programs):
    programs = list(programs)
    while programs:
        alive = []
        for prog in programs:
            try:
                next(prog)
                alive.append(prog)
            except StopIteration:
                pass
        programs = alive


def _col_to_row(col, ii, jj):
    return jnp.sum(jnp.where(ii == jj, col, 0.0), axis=0, keepdims=True)


def _row_to_col(row, ii, jj):
    return jnp.sum(jnp.where(ii == jj, row, 0.0), axis=1, keepdims=True)


def _decay(gc_col, ii, jj):
    diff = gc_col - _col_to_row(gc_col, ii, jj)
    return jnp.where(jj <= ii, jnp.exp(jnp.where(jj <= ii, diff, 0.0)), 0.0)


def _softplus(x):
    return jnp.maximum(x, 0.0) + jnp.log(1.0 + jnp.exp(-jnp.abs(x)))


def _head(h):
    return slice(DN_DIM * h, DN_DIM * (h + 1))


def _dn_chunk_fwd(qkv, pg, a_log, dt_bias):
    S = qkv.shape[0]
    C = DN_CHUNK
    G = CHUNKS_LOCAL
    R = G * C
    steps = S // R

    def body(alog_ref, dtb_ref, qkv_ref, pg_ref, w_ref, u_ref, qg_ref, kd_ref, a_ref, t_ref, gcs_ref):
        ii, jj = _chunk_masks()
        lane = lax.broadcasted_iota(jnp.int32, (1, 128), 1)
        eye = (ii == jj).astype(F32)
        gcs_parts = [[] for _ in range(G)]

        def head_program(chunk, h):
            rows = slice(chunk * C, (chunk + 1) * C)
            q, k, v = qkv_ref[rows, _head(h)], qkv_ref[rows, _head(DN_HEADS + h)], qkv_ref[rows, _head(2 * DN_HEADS + h)]
            beta = _sigmoid(pg_ref[rows, h:h + 1])
            g_col = -jnp.exp(alog_ref[h]) * _softplus(pg_ref[rows, DN_HEADS + h:DN_HEADS + h + 1] + dtb_ref[h])
            g_row = _col_to_row(g_col, ii, jj)
            gc_col = jnp.sum(jnp.where(jj <= ii, g_row, 0.0), axis=1, keepdims=True)
            dec = _decay(gc_col, ii, jj)
            eg = jnp.exp(gc_col)
            kb, vb = k * beta, v * beta
            k_rows = _rows_pad(_bf(k))
            kk = _dot_nt(_bf(kb), k_rows)
            qk = _dot_nt(_bf(q), k_rows)
            yield
            t, pw = eye, -jnp.where(jj < ii, kk * dec, 0.0)
            for _ in range(6):
                t, pw = _double_step(t, pw)
                yield
            tb = _bf(t)
            u_ref[rows, _head(h)] = _dot(tb, _rows_pad(_bf(vb)))
            w_ref[rows, _head(h)] = _bf(_dot(tb, _rows_pad(_bf(kb * eg))))
            a_ref[h, rows] = _bf(qk * dec)
            t_ref[h, rows] = t
            qg_ref[rows, _head(h)] = _bf(q * eg)
            kd_ref[rows, _head(h)] = _bf(k * jnp.exp(gc_col[C - 1:C, :] - gc_col))
            gcs_parts[chunk].append(jnp.where(lane == h, gc_col, 0.0) + jnp.where(lane == DN_HEADS + h, beta, 0.0)
                                    + jnp.where(lane == 2 * DN_HEADS + h, g_col, 0.0))

        _interleave(head_program(chunk, h) for chunk in range(G) for h in range(DN_HEADS))
        for chunk in range(G):
            gcs_ref[chunk * C:(chunk + 1) * C, :] = sum(gcs_parts[chunk][1:], gcs_parts[chunk][0])

    smem = pl.BlockSpec(memory_space=pltpu.SMEM)
    wide = pl.BlockSpec((R, 512), lambda n: (n, 0))
    sq = pl.BlockSpec((DN_HEADS, R, CPAD), lambda n: (0, n, 0))
    narrow = pl.BlockSpec((R, 128), lambda n: (n, 0))
    f = lambda *shp: jax.ShapeDtypeStruct(shp, F32)
    b = lambda *shp: jax.ShapeDtypeStruct(shp, BF16)
    return pl.pallas_call(
        body, grid=(steps,), name="dn_chunk_fwd",
        in_specs=[smem, smem, pl.BlockSpec((R, 1536), lambda n: (n, 0)), pl.BlockSpec((R, 128), lambda n: (n, BLK_G))],
        out_specs=[wide, wide, wide, wide, sq, sq, narrow],
        out_shape=[b(S, 512), f(S, 512), b(S, 512), b(S, 512), b(DN_HEADS, S, CPAD), f(DN_HEADS, S, CPAD), f(S, 128)],
        compiler_params=_params(("parallel",)),
    )(a_log, dt_bias, qkv, pg)


def _gated_norm(o, z, gn):
    r, oh = _rms_stats(o)
    return oh * gn * (z * _sigmoid(z))


def _dn_scan_fwd(w, u, qg, kd, a, gcs, pz, gn):
    S = w.shape[0]
    C = DN_CHUNK
    nc = S // C
    G = CHUNKS_SCAN
    R = G * C

    def body(w_ref, u_ref, qg_ref, kd_ref, a_ref, gcs_ref, z_ref, gn_ref, o_ref, vn_ref, sst_ref, out_ref, state):
        @pl.when(pl.program_id(0) == 0)
        def _():
            state[...] = jnp.zeros_like(state)

        def head_program(chunk, h):
            hs = _head(h)
            rows = slice(chunk * C, (chunk + 1) * C)
            s_in = state[h]
            sst_ref[chunk, h] = s_in
            sb = _bf(s_in)
            w_s = _dot(w_ref[rows, hs], sb)
            q_s = _dot(qg_ref[rows, hs], sb)
            yield
            vn = u_ref[rows, hs] - w_s
            vnb = _bf(vn)
            o = q_s + _dot(a_ref[h, rows], _rows_pad(vnb))
            k_v = _dot_tn(kd_ref[rows, hs], vnb)
            yield
            state[h] = s_in * jnp.exp(gcs_ref[(chunk + 1) * C - 1:(chunk + 1) * C, h:h + 1]) + k_v
            o_ref[rows, hs] = o
            vn_ref[rows, hs] = vn
            out_ref[rows, hs] = _gated_norm(o, z_ref[rows, hs], gn_ref[...])

        for chunk in range(G):
            _interleave(head_program(chunk, h) for h in range(DN_HEADS))

    wide = pl.BlockSpec((R, 512), lambda n: (n, 0))
    f = lambda *shp: jax.ShapeDtypeStruct(shp, F32)
    return pl.pallas_call(
        body, grid=(nc // G,), name="dn_scan_fwd",
        in_specs=[wide, wide, wide, wide, pl.BlockSpec((DN_HEADS, R, CPAD), lambda n: (0, n, 0)),
                  pl.BlockSpec((R, 128), lambda n: (n, 0)), pl.BlockSpec((R, 512), lambda n: (n, BLK_Z)),
                  pl.BlockSpec((1, DN_DIM), lambda n: (0, 0))],
        out_specs=[wide, wide, pl.BlockSpec((G, DN_HEADS, DN_DIM, DN_DIM), lambda n: (n, 0, 0, 0)), wide],
        out_shape=[f(S, 512), f(S, 512), f(nc, DN_HEADS, DN_DIM, DN_DIM), f(S, 512)],
        scratch_shapes=[pltpu.VMEM((DN_HEADS, DN_DIM, DN_DIM), F32)],
        compiler_params=_params(("arbitrary",)),
    )(w, u, qg, kd, a, gcs, pz, gn)


def _dn_scan_bwd(dcat, o, pz, gn, sst, vnew, w, qg, kd, a, gcs, dproj):
    S = o.shape[0]
    C = DN_CHUNK
    G = CHUNKS_SCAN
    R = G * C
    steps = S // R

    def body(dy_ref, o_ref, z_ref, gn_ref, sst_ref, vn_ref, w_ref, qg_ref, kd_ref, a_ref, gcs_ref, _,
             du_ref, dw_ref, dqg_ref, dkd_ref, da_ref, dz_ref, dsc_ref, dgn_ref, dstate):
        @pl.when(pl.program_id(0) == 0)
        def _():
            dstate[...] = jnp.zeros_like(dstate)
            dgn_ref[...] = jnp.zeros_like(dgn_ref)

        gn_ = gn_ref[...]
        lane = lax.broadcasted_iota(jnp.int32, (C, 128), 1)
        row = lax.broadcasted_iota(jnp.int32, (C, 128), 0)
        dgn_parts = []

        def head_program(chunk, h, dsc_parts):
            hs = _head(h)
            rows = slice(chunk * C, (chunk + 1) * C)
            ov, z, dout = o_ref[rows, hs], z_ref[rows, hs], dy_ref[rows, hs]
            r, oh = _rms_stats(ov)
            sg = _sigmoid(z)
            don = dout * (z * sg)
            dz_ref[rows, hs] = _bf(dout * (oh * gn_) * (sg * (1.0 + z * (1.0 - sg))))
            dgn_parts.append(jnp.sum(don * oh, axis=0, keepdims=True))
            dn = don * gn_
            do = _bf(r * (dn - oh * jnp.mean(dn * oh, axis=-1, keepdims=True)))
            s_in = sst_ref[chunk, h]
            sb = _bf(s_in)
            ds_out = dstate[h]
            dsb = _bf(ds_out)
            vnb = _bf(vn_ref[rows, hs])
            wb, qgb, kdb, ab = w_ref[rows, hs], qg_ref[rows, hs], kd_ref[rows, hs], a_ref[h, rows]
            dvn = _dot_tn(ab, do)[:C] + _dot(kdb, dsb)
            da_ref[h, rows] = _dot_nt(do, _rows_pad(vnb))
            dqg_ref[rows, hs] = _dot_nt(do, sb)
            dkd_ref[rows, hs] = _dot_nt(vnb, dsb)
            q_do = _dot_tn(qgb, do)
            yield
            dvnb = _bf(dvn)
            dw_ref[rows, hs] = _bf(-_dot_nt(dvnb, sb))
            w_dvn = _dot_tn(wb, dvnb)
            du_ref[rows, hs] = dvnb
            yield
            d_last = jnp.exp(gcs_ref[(chunk + 1) * C - 1:(chunk + 1) * C, h:h + 1])
            dd = jnp.sum(jnp.sum(ds_out * s_in, axis=1, keepdims=True), axis=0, keepdims=True)
            dsc_parts.append(jnp.where((lane == h) & (row == C - 1), dd * d_last, 0.0))
            dstate[h] = ds_out * d_last + q_do - w_dvn

        for chunk in reversed(range(G)):
            dsc_parts = []
            _interleave(head_program(chunk, h, dsc_parts) for h in range(DN_HEADS))
            dsc_ref[chunk * C:(chunk + 1) * C, :] = sum(dsc_parts[1:], dsc_parts[0])
        dgn_ref[...] += sum(dgn_parts[1:], dgn_parts[0])

    rev = lambda n: steps - 1 - n
    wide = pl.BlockSpec((R, 512), lambda n: (rev(n), 0))
    z_spec = pl.BlockSpec((R, 512), lambda n: (rev(n), BLK_Z))
    sq = pl.BlockSpec((DN_HEADS, R, CPAD), lambda n: (0, rev(n), 0))
    narrow = pl.BlockSpec((R, 128), lambda n: (rev(n), 0))
    gn_spec = pl.BlockSpec((1, DN_DIM), lambda n: (0, 0))
    f = lambda *shp: jax.ShapeDtypeStruct(shp, F32)
    b = lambda *shp: jax.ShapeDtypeStruct(shp, BF16)
    return pl.pallas_call(
        body, grid=(steps,), name="dn_scan_bwd",
        in_specs=[pl.BlockSpec((R, 512), lambda n: (rev(n), 1)), wide, z_spec, gn_spec,
                  pl.BlockSpec((G, DN_HEADS, DN_DIM, DN_DIM), lambda n: (rev(n), 0, 0, 0)),
                  wide, wide, wide, wide, sq, narrow, pl.BlockSpec(memory_space=pl.ANY)],
        out_specs=[wide, wide, wide, wide, sq, z_spec, narrow, gn_spec],
        out_shape=[b(S, 512), b(S, 512), f(S, 512), f(S, 512), f(DN_HEADS, S, CPAD),
                   jax.ShapeDtypeStruct(dproj.shape, dproj.dtype), f(S, 128), f(1, DN_DIM)],
        scratch_shapes=[pltpu.VMEM((DN_HEADS, DN_DIM, DN_DIM), F32)],
        input_output_aliases={11: 5},
        compiler_params=_params(("arbitrary",)),
    )(dcat, o, pz, gn, sst, vnew, w, qg, kd, a, gcs, dproj)


def _dn_chunk_bwd(qkv, pg, t_inv, gcs, du, dw, dqg, dkd, da, dsc, a_log, dt_bias, dproj):
    S = qkv.shape[0]
    C = DN_CHUNK
    G = CHUNKS_LOCAL
    R = G * C

    def body(alog_ref, dtb_ref, qkv_ref, pg_ref, t_ref, gcs_ref, du_ref, dw_ref, dqg_ref, dkd_ref, da_ref, dsc_ref, _,
             dqkv_ref, dpg_ref, acc_ref):
        @pl.when(pl.program_id(0) == 0)
        def _():
            acc_ref[...] = jnp.zeros_like(acc_ref)

        ii, jj = _chunk_masks()
        lane = lax.broadcasted_iota(jnp.int32, (1, 128), 1)
        row8 = lax.broadcasted_iota(jnp.int32, (8, 128), 0)
        lane8 = lax.broadcasted_iota(jnp.int32, (8, 128), 1)
        rowc = lax.broadcasted_iota(jnp.int32, (C, 1), 0)
        tril, strict = jj <= ii, jj < ii
        dpg_parts, acc_parts = [[] for _ in range(G)], []

        def head_program(chunk, h):
            rows = slice(chunk * C, (chunk + 1) * C)
            q, k, v = qkv_ref[rows, _head(h)], qkv_ref[rows, _head(DN_HEADS + h)], qkv_ref[rows, _head(2 * DN_HEADS + h)]
            gc_col, beta, g_col = gcs_ref[rows, h:h + 1], gcs_ref[rows, DN_HEADS + h:DN_HEADS + h + 1], \
                gcs_ref[rows, 2 * DN_HEADS + h:2 * DN_HEADS + h + 1]
            dec = _decay(gc_col, ii, jj)
            eg = jnp.exp(gc_col)
            g_last = gc_col[C - 1:C, :]
            ek = jnp.exp(g_last - gc_col)
            kb, vb = k * beta, v * beta
            kbg = kb * eg
            qb, kbb = _bf(q), _bf(kb)
            k_rows = _rows_pad(_bf(k))
            t = t_ref[h, rows]
            tb = _bf(t)
            dub, dwb = du_ref[rows, _head(h)], dw_ref[rows, _head(h)]
            dqg_, dkd_ = dqg_ref[rows, _head(h)], dkd_ref[rows, _head(h)]
            dt = _dot_nt(dub, _rows_pad(_bf(vb))) + _dot_nt(dwb, _rows_pad(_bf(kbg)))
            dvb = _dot_tn(tb, dub)[:C]
            dkbg = _dot_tn(tb, dwb)[:C]
            kk = _dot_nt(kbb, k_rows)
            qk = _dot_nt(qb, k_rows)
            yield
            dt_t = _dot3_nt(dt, t)
            yield
            dl = -_dot3_tn(t, dt_t)
            yield
            dm = jnp.where(strict, dl * dec, 0.0)
            dqk = jnp.where(tril, da_ref[h, rows] * dec, 0.0)
            gmat = dm * kk + dqk * qk
            dgc = jnp.sum(gmat, axis=1, keepdims=True) - _row_to_col(jnp.sum(gmat, axis=0, keepdims=True), ii, jj)
            dmb, dqkb = _bf(dm), _bf(dqk)
            dkb = _dot(dmb, k_rows) + dkbg * eg
            dk = _dot_tn(dmb, kbb)[:C] + _dot_tn(dqkb, qb)[:C] + dkd_ * ek
            dq = _dot(dqkb, k_rows) + dqg_ * eg
            yield
            tk = jnp.sum(dkd_ * k * ek, axis=1, keepdims=True)
            dgc = dgc + jnp.sum(dqg_ * q * eg, axis=1, keepdims=True) - tk + jnp.sum(dkbg * kbg, axis=1, keepdims=True)
            dgl = jnp.sum(tk, axis=0, keepdims=True) + dsc_ref[(chunk + 1) * C - 1:(chunk + 1) * C, h:h + 1]
            dgc = dgc + jnp.where(rowc == C - 1, dgl, 0.0)
            dk = dk + dkb * beta
            dbeta = jnp.sum(dkb * k, axis=1, keepdims=True) + jnp.sum(dvb * v, axis=1, keepdims=True)
            dqkv_ref[rows, _head(h)] = dq
            dqkv_ref[rows, _head(DN_HEADS + h)] = dk
            dqkv_ref[rows, _head(2 * DN_HEADS + h)] = dvb * beta
            dg_col = jnp.sum(jnp.where(jj >= ii, _col_to_row(dgc, ii, jj), 0.0), axis=1, keepdims=True)
            db = dbeta * beta * (1.0 - beta)
            da_in = dg_col * (-jnp.exp(alog_ref[h])) * _sigmoid(pg_ref[rows, DN_HEADS + h:DN_HEADS + h + 1] + dtb_ref[h])
            dpg_parts[chunk].append(jnp.where(lane == h, db, 0.0) + jnp.where(lane == DN_HEADS + h, da_in, 0.0))
            acc_parts.append(jnp.where((row8 == 0) & (lane8 == h), jnp.sum(dg_col * g_col, axis=0, keepdims=True), 0.0)
                             + jnp.where((row8 == 1) & (lane8 == h), jnp.sum(da_in, axis=0, keepdims=True), 0.0))

        _interleave(head_program(chunk, h) for chunk in range(G) for h in range(DN_HEADS))
        for chunk in range(G):
            dpg = sum(dpg_parts[chunk][1:], dpg_parts[chunk][0])
            dpg_ref[chunk * C:(chunk + 1) * C, :] = _bf(jnp.concatenate([dpg, jnp.zeros_like(dpg)], axis=1))
        acc_ref[...] += sum(acc_parts[1:], acc_parts[0])

    smem = pl.BlockSpec(memory_space=pltpu.SMEM)
    wide = pl.BlockSpec((R, 512), lambda n: (n, 0))
    sq = pl.BlockSpec((DN_HEADS, R, CPAD), lambda n: (0, n, 0))
    narrow = pl.BlockSpec((R, 128), lambda n: (n, 0))
    qkv_spec = pl.BlockSpec((R, 1536), lambda n: (n, 0))
    f = lambda *shp: jax.ShapeDtypeStruct(shp, F32)
    return pl.pallas_call(
        body, grid=(S // R,), name="dn_chunk_bwd",
        in_specs=[smem, smem, qkv_spec, pl.BlockSpec((R, 128), lambda n: (n, BLK_G)), sq, narrow, wide, wide, wide, wide, sq,
                  narrow, pl.BlockSpec(memory_space=pl.ANY)],
        out_specs=[qkv_spec, pl.BlockSpec((R, 256), lambda n: (n, BLK_G_PAD)), pl.BlockSpec((8, 128), lambda n: (0, 0))],
        out_shape=[f(S, 1536), jax.ShapeDtypeStruct(dproj.shape, dproj.dtype), f(8, 128)],
        input_output_aliases={12: 1},
        compiler_params=_params(("arbitrary",)),
    )(a_log, dt_bias, qkv, pg, t_inv, gcs, du, dw, dqg, dkd, da, dsc, dproj)


def _fill_kv(dk, dv, dproj):
    S = dk.shape[0]
    tm = min(512, S)

    def body(dk_ref, dv_ref, _, o_ref):
        o_ref[...] = _bf(jnp.concatenate([dk_ref[...], dv_ref[...]], axis=1))

    tile = pl.BlockSpec((tm, 128), lambda i: (i, 0))
    return pl.pallas_call(
        body, grid=(S // tm,), name="fill_kv",
        in_specs=[tile, tile, pl.BlockSpec(memory_space=pl.ANY)],
        out_specs=pl.BlockSpec((tm, 256), lambda i: (i, BLK_KV)),
        out_shape=jax.ShapeDtypeStruct(dproj.shape, dproj.dtype),
        input_output_aliases={2: 0},
        compiler_params=_params(("parallel",)),
    )(dk, dv, dproj)


def _w_in_to_internal(wt):
    return jnp.concatenate([wt[0:512], wt[2304:2816], wt[768:2304], wt[512:768], wt[2816:2824],
                            jnp.zeros((D_IN_PAD - D_IN, wt.shape[1]), wt.dtype)], axis=0)


def _w_in_from_internal(gt):
    return jnp.concatenate([gt[0:512], gt[2560:2816], gt[1024:2560], gt[512:1024], gt[2816:2824]], axis=0)


def _local_step(x, p, target, wts, first_weights, other_weights, ship_early):
    S = x.shape[0]
    cos, sin = _rope_tables(S)
    sinks, a_log, dt_bias = wts["sinks"].reshape(8), wts["a_log"].reshape(4), wts["dt_bias"].reshape(4)
    gn = wts["dn_norm"].reshape(1, DN_DIM)
    add = lambda acc, res: (acc + res,)

    u = _rmsnorm_fwd(x, wts["norm_mix"], "norm_mix_fwd")
    w_in_t, conv_w = first_weights(u)
    proj = _mm_nt(u, w_in_t, name="in_proj", out_dtype=F32, tn=512)
    attn = _attn_fwd(proj, cos, sin, sinks)
    qkv = _dn_prep_fwd(proj, conv_w)
    cw, cu, cqg, ckd, ca, ct, gcs = _dn_chunk_fwd(qkv, proj, a_log, dt_bias)
    o, vnew, sst, dn_out = _dn_scan_fwd(cw, cu, cqg, ckd, ca, gcs, proj, gn)
    w_o, w_up, w_down, w_pg, w_pp = other_weights(dn_out)
    h1a, = _mm_nn(attn, w_o, name="out_proj_attn", out_dtypes=[F32], tn=512, epi=add, extra=[x], w_row_block=0)
    h1, = _mm_nn(dn_out, w_o, name="out_proj_dn", out_dtypes=[F32], tn=512, epi=add, extra=[h1a], w_row_block=1)
    m = _rmsnorm_fwd(h1, wts["norm_mlp"], "norm_mlp_fwd")

    def relu2(acc):
        r = jnp.maximum(acc, 0.0)
        return r * r, r

    hid, relu = _mm_nn(m, w_up, name="mlp_up", out_dtypes=[BF16, BF16], tn=512, epi=relu2)
    h2, = _mm_nn(hid, w_down, name="mlp_down", out_dtypes=[F32], tn=512, epi=add, extra=[h1])
    n3 = _rmsnorm_fwd(h2, wts["norm_ple"], "norm_ple_fwd")
    pp, = _mm_nn(p, w_pp, name="ple_proj", out_dtypes=[F32], tn=512)

    def ple(acc, h2_t, pp_t):
        gate = _sigmoid(acc)
        return h2_t + gate * pp_t, gate

    h3, gate = _mm_nn(n3, w_pg, name="ple_gate", out_dtypes=[F32, F32], tn=512, epi=ple, extra=[h2, pp])
    dh3, loss, d_norm_final = _final_loss(h3, wts["norm_final"].reshape(1, D_MODEL), target)

    g = {"norm_final": d_norm_final}
    dgl, dpp = _ple_bwd(dh3, pp, gate)
    early = {"w_ple_gate": _mm_tn(n3, dgl, name="d_w_ple_gate", tm=512, tn=1024, out_dtype=BF16).reshape(N_DEV, 128, 1024),
             "w_ple_proj": _mm_tn(p, dpp, name="d_w_ple_proj", tm=256, tn=128, out_dtype=BF16, column_shards=True)}
    dn3 = _mm_nt(dgl, w_pg, name="d_n3", out_dtype=F32, tn=512)
    dh2, g["norm_ple"] = _rmsnorm_bwd(h2, wts["norm_ple"], dn3, dh3, "norm_ple_bwd")
    d_act = _mm_nt(dh2, w_down, name="d_hidden", out_dtype=BF16, tn=512, epi=lambda acc, r: acc * (2.0 * r.astype(F32)), extra=[relu])
    early["w_down"] = _mm_tn(hid, dh2, name="d_w_down", tm=512, tn=1024, out_dtype=BF16).reshape(N_DEV, 512, 1024)
    early["w_up"] = _mm_tn(m, d_act, name="d_w_up", tm=1024, tn=512, out_dtype=BF16, column_shards=True)
    token = ship_early(early)
    dm = _mm_nt(d_act, w_up, name="d_m", out_dtype=F32, tn=512)
    dh1, g["norm_mlp"] = _rmsnorm_bwd(h1, wts["norm_mlp"] + token[0:1, 0:1], dm, dh2, "norm_mlp_bwd")
    dcat = _mm_nt(dh1, w_o, name="d_cat", out_dtype=F32, tn=512)
    d_w_o = jnp.concatenate([_mm_tn(attn, dh1, name="d_w_o_attn", tm=512, tn=512, out_dtype=BF16),
                             _mm_tn(dn_out, dh1, name="d_w_o_dn", tm=512, tn=512, out_dtype=BF16)], axis=0)
    token = ship_early({"w_o": d_w_o.reshape(N_DEV, 128, 1024)})
    dproj, dk, dv, dsinks = _attn_bwd(proj, cos, sin, sinks + token[0, 0], dcat)
    g["sinks"] = dsinks[:, 0].reshape(1, 8)
    du_, dw_, dqg, dkd, da, dproj, dsc, g["dn_norm"] = _dn_scan_bwd(dcat, o, proj, gn, sst, vnew, cw, cqg, ckd, ca, gcs, dproj)
    dqkv, dproj, gate_acc = _dn_chunk_bwd(qkv, proj, ct, gcs, du_, dw_, dqg, dkd, da, dsc, a_log, dt_bias, dproj)
    g["a_log"], g["dt_bias"] = gate_acc[0:1, 0:4], gate_acc[1:2, 0:4]
    dproj, g["conv_w"] = _dn_prep_bwd(proj, conv_w, dqkv, dproj)
    dproj = _fill_kv(dk, dv, dproj)
    g["w_in"] = _mm_tn(dproj, u, name="d_w_in", tm=512, tn=1024)
    du_in, = _mm_nn(dproj, w_in_t, name="d_u", out_dtypes=[F32], tn=512)
    grad_x, g["norm_mix"] = _rmsnorm_bwd(x, wts["norm_mix"], du_in, dh1, "norm_mix_bwd")
    return loss, grad_x, g


def _peer(k):
    x, y, c = lax.axis_index("x"), lax.axis_index("y"), lax.axis_index("c")
    px = 1 - x if k & 4 else x
    py = 1 - y if k & 2 else y
    pc = 1 - c if k & 1 else c
    return (px, py, pc), 4 * px + 2 * py + pc


def _exchange(srcs, name, gather):
    n = len(srcs)
    gathers = list(gather) if isinstance(gather, (list, tuple)) else [gather] * n
    shapes = [(N_DEV,) + s.shape if gt else s.shape for s, gt in zip(srcs, gathers)]

    def body(*refs):
        src_refs, out_refs = refs[:n], refs[n:2 * n]
        send_sems, recv_sems, local_sems = refs[2 * n:]
        _, me = _peer(0)
        piece = lambda a, d: src_refs[a] if gathers[a] else src_refs[a].at[d]
        local = [pltpu.make_async_copy(piece(a, me), out_refs[a].at[me], local_sems.at[a]) for a in range(n)]
        for cp in local:
            cp.start()
        copies = []
        for a in range(n):
            for k in range(1, N_DEV):
                dev, idx = _peer(k)
                cp = pltpu.make_async_remote_copy(src_ref=piece(a, idx), dst_ref=out_refs[a].at[me],
                                                  send_sem=send_sems.at[a, k - 1], recv_sem=recv_sems.at[a, k - 1],
                                                  device_id=dev, device_id_type=MESH)
                cp.start()
                copies.append(cp)
        for cp in copies:
            cp.wait_recv()
        for cp in copies:
            cp.wait_send()
        for cp in local:
            cp.wait()

    anywhere = pl.BlockSpec(memory_space=pl.ANY)
    return pl.pallas_call(
        body, name=name, in_specs=[anywhere] * n, out_specs=[anywhere] * n,
        out_shape=[jax.ShapeDtypeStruct(shp, s.dtype) for shp, s in zip(shapes, srcs)],
        scratch_shapes=[pltpu.SemaphoreType.DMA((n, N_DEV - 1)), pltpu.SemaphoreType.DMA((n, N_DEV - 1)),
                        pltpu.SemaphoreType.DMA((n,))],
    )(*srcs)


_HBM = pl.BlockSpec(memory_space=pltpu.HBM)
_SEM = pl.BlockSpec(memory_space=pltpu.SEMAPHORE)
_EFFECT = pltpu.SideEffectType.DATAFLOW_SIDE_EFFECTING


def _split_copies(src_refs, land_refs, send_sems, recv_sems, gather):
    _, me = _peer(0)
    copies = []
    for a, (src, land) in enumerate(zip(src_refs, land_refs)):
        for k in range(1, N_DEV):
            dev, idx = _peer(k)
            sem = a * (N_DEV - 1) + k - 1
            copies.append(pltpu.make_async_remote_copy(
                src_ref=src if gather else src.at[idx], dst_ref=land.at[me], send_sem=send_sems.at[sem],
                recv_sem=recv_sems.at[sem], device_id=dev, device_id_type=MESH))
    return copies


def _exchange_start(srcs, name, gather):
    n = len(srcs)
    me = 4 * lax.axis_index("x") + 2 * lax.axis_index("y") + lax.axis_index("c")
    lands = []
    for s in srcs:
        own = s if gather else lax.dynamic_index_in_dim(s, me, 0, keepdims=False)
        shape = (N_DEV,) + s.shape if gather else s.shape
        lands.append(lax.dynamic_update_index_in_dim(lax.empty(shape, s.dtype), own, me, 0))

    def body(*refs):
        src_refs, land_refs = refs[:n], refs[n:2 * n]
        send_sems, recv_sems = refs[2 * n], refs[2 * n + 1]
        for cp in _split_copies(src_refs, land_refs, send_sems, recv_sems, gather):
            cp.start()
        refs[-1][...] = jnp.zeros_like(refs[-1])

    both = list(srcs) + lands
    sems = pltpu.SemaphoreType.DMA((n * (N_DEV - 1),))
    out = pl.pallas_call(
        body, name=name,
        out_shape=(sems, sems, *[pltpu.HBM(t.shape, t.dtype) for t in both], jax.ShapeDtypeStruct((8, 128), F32)),
        in_specs=[_HBM] * (2 * n), out_specs=(_SEM, _SEM, *[_HBM] * (2 * n), pl.BlockSpec(memory_space=pltpu.VMEM)),
        input_output_aliases={i: 2 + i for i in range(2 * n)},
        compiler_params=pltpu.CompilerParams(has_side_effects=_EFFECT),
    )(*[pltpu.with_memory_space_constraint(t, pltpu.HBM) for t in both])
    return (n, gather, out[:-1]), out[-1]


def _exchange_wait(handle, after, name):
    n, gather, (send_sems, recv_sems, *both) = handle

    def body(*refs):
        src_refs, land_refs = refs[:n], refs[n:2 * n]
        for cp in _split_copies(src_refs, land_refs, refs[2 * n], refs[2 * n + 1], gather):
            cp.wait_send()
            cp.wait_recv()

    out = pl.pallas_call(
        body, name=name, out_shape=tuple(pltpu.HBM(t.shape, t.dtype) for t in both),
        in_specs=[_HBM] * (2 * n) + [_SEM, _SEM, pl.BlockSpec(memory_space=pl.ANY)], out_specs=tuple([_HBM] * (2 * n)),
        input_output_aliases={i: i for i in range(2 * n)},
        compiler_params=pltpu.CompilerParams(has_side_effects=_EFFECT),
    )(*both, send_sems, recv_sems, after)
    return list(out[n:])


def _adamw(parts, w, m, v, name):
    n, R, W = parts.shape
    tm = 128 if R % 128 == 0 else R

    def body(p_ref, w_ref, m_ref, v_ref, g_ref, d_ref, nm_ref, nv_ref):
        g = p_ref[0].astype(F32)
        for s in range(1, n):
            g = g + p_ref[s].astype(F32)
        nm = ADAM_B1 * m_ref[...] + (1.0 - ADAM_B1) * g
        nv = ADAM_B2 * v_ref[...] + (1.0 - ADAM_B2) * (g * g)
        m_hat = nm / (1.0 - ADAM_B1 ** ADAM_STEP)
        v_hat = nv / (1.0 - ADAM_B2 ** ADAM_STEP)
        g_ref[...] = g
        d_ref[...] = -ADAM_LR * (m_hat / (jnp.sqrt(v_hat) + ADAM_EPS) + ADAM_WD * w_ref[...])
        nm_ref[...] = nm
        nv_ref[...] = nv

    tile = pl.BlockSpec((tm, W), lambda i: (i, 0))
    return pl.pallas_call(
        body, grid=(R // tm,), name=name,
        in_specs=[pl.BlockSpec((n, tm, W), lambda i: (0, i, 0)), tile, tile, tile],
        out_specs=[tile] * 4, out_shape=[jax.ShapeDtypeStruct((R, W), F32)] * 4,
        compiler_params=_params(("parallel",)),
    )(parts, w, m, v)


_MATRICES = ("w_in", "w_o", "w_up", "w_down", "w_ple_gate", "w_ple_proj")


_OTHERS = ("w_o", "w_up", "w_down", "w_ple_gate", "w_ple_proj")


def _cols_from_shards(t):
    return jnp.transpose(t, (1, 0, 2)).reshape(t.shape[1], N_DEV * t.shape[2])


_SMALL_ROWS = 16
_VEC_ROW = {"norm_mix": 0, "norm_mlp": 1, "norm_ple": 2, "norm_final": 3}
_VEC_LANES = {"a_log": (0, 4), "dt_bias": (4, 8), "sinks": (8, 16), "dn_norm": (128, 256)}
_LOSS_ROW, _CONV_ROW = 5, 8


def _pack_small(vals, extra_rows):
    row4 = jnp.zeros((1024,), F32)
    for n, (a, b) in _VEC_LANES.items():
        row4 = row4.at[a:b].set(vals[n].reshape(b - a))
    rows = [vals[n].reshape(1, 1024) for n in ("norm_mix", "norm_mlp", "norm_ple", "norm_final")] + [row4.reshape(1, 1024)]
    return jnp.concatenate(rows + extra_rows, axis=0)


def _unpack_small(buf, like):
    out = {n: buf[r].reshape(like[n].shape) for n, r in _VEC_ROW.items()}
    for n, (a, b) in _VEC_LANES.items():
        out[n] = buf[4, a:b].reshape(like[n].shape)
    return out


_ORDER = ("norm_mix", "w_in", "conv_w", "a_log", "dt_bias", "dn_norm", "sinks", "w_o", "norm_mlp", "w_up", "w_down",
          "norm_ple", "w_ple_gate", "w_ple_proj", "norm_final")


def kernel(x, p, norm_mix, w_in, conv_w, a_log, dt_bias, dn_norm, sinks, w_o, norm_mlp, w_up, w_down, norm_ple, w_ple_gate, w_ple_proj, norm_final, loss_target, m_norm_mix, m_w_in, m_conv_w, m_a_log, m_dt_bias, m_dn_norm, m_sinks, m_w_o, m_norm_mlp, m_w_up, m_w_down, m_norm_ple, m_w_ple_gate, m_w_ple_proj, m_norm_final, v_norm_mix, v_w_in, v_conv_w, v_a_log, v_dt_bias, v_dn_norm, v_sinks, v_w_o, v_norm_mlp, v_w_up, v_w_down, v_norm_ple, v_w_ple_gate, v_w_ple_proj, v_norm_final):
    w = dict(norm_mix=norm_mix, w_in=w_in[0], conv_w=conv_w[0], a_log=a_log, dt_bias=dt_bias, dn_norm=dn_norm, sinks=sinks,
             w_o=w_o[0], norm_mlp=norm_mlp, w_up=w_up[0], w_down=w_down[0], norm_ple=norm_ple, w_ple_gate=w_ple_gate[0],
             w_ple_proj=w_ple_proj[0], norm_final=norm_final)
    m = dict(norm_mix=m_norm_mix, w_in=m_w_in[0], conv_w=m_conv_w[0], a_log=m_a_log, dt_bias=m_dt_bias, dn_norm=m_dn_norm,
             sinks=m_sinks, w_o=m_w_o[0], norm_mlp=m_norm_mlp, w_up=m_w_up[0], w_down=m_w_down[0], norm_ple=m_norm_ple,
             w_ple_gate=m_w_ple_gate[0], w_ple_proj=m_w_ple_proj[0], norm_final=m_norm_final)
    v = dict(norm_mix=v_norm_mix, w_in=v_w_in[0], conv_w=v_conv_w[0], a_log=v_a_log, dt_bias=v_dt_bias, dn_norm=v_dn_norm,
             sinks=v_sinks, w_o=v_w_o[0], norm_mlp=v_norm_mlp, w_up=v_w_up[0], w_down=v_w_down[0], norm_ple=v_norm_ple,
             w_ple_gate=v_w_ple_gate[0], w_ple_proj=v_w_ple_proj[0], norm_final=v_norm_final)
    me = 4 * lax.axis_index("x") + 2 * lax.axis_index("y") + lax.axis_index("c")
    conv_shard = conv_w.shape[2]

    for d in (w, m, v):
        d["w_in"] = d["w_in"].T
    conv_pad = jnp.pad(w["conv_w"], ((0, 8 - DN_CONV), (0, 256 - conv_shard)))
    first, token_first = _exchange_start([_bf(w["w_in"]), conv_pad], "gather_first_start", gather=True)
    later = [_bf(w[n]) for n in _OTHERS]
    later[-1] = _bf(w["w_ple_proj"] + token_first[0:1, 0:1])
    others, token_others = _exchange_start(later, "gather_others_start", gather=True)
    vectors = dict(w)
    vectors["norm_mix"] = w["norm_mix"] + token_others[0:1, 0:1]

    def first_weights(after):
        w_in_all, conv_all = _exchange_wait(first, after, "gather_first_wait")
        conv_all = jnp.transpose(conv_all[:, :DN_CONV, :conv_shard], (1, 0, 2)).reshape(DN_CONV, N_DEV * conv_shard)
        return _w_in_to_internal(w_in_all.reshape(D_IN, D_MODEL)), conv_all

    def other_weights(after):
        w_o_all, w_up_all, w_down_all, w_pg_all, w_pp_all = _exchange_wait(others, after, "gather_others_wait")
        return (w_o_all.reshape(1024, 1024), _cols_from_shards(w_up_all), w_down_all.reshape(4096, 1024),
                w_pg_all.reshape(1024, 1024), _cols_from_shards(w_pp_all))

    shipped = []

    def ship_early(pieces):
        names = tuple(pieces)
        handle, token = _exchange_start([pieces[n] for n in names], "scatter_start_" + names[0], gather=False)
        shipped.append((names, handle))
        return token

    loss, grad_x, g = _local_step(x[0], p[0, 0], loss_target[0], vectors, first_weights, other_weights, ship_early)

    w_in_pieces = _bf(_w_in_from_internal(g["w_in"])).reshape(N_DEV, D_IN // N_DEV, D_MODEL)
    small = _pack_small(g, [loss[:, :1] * jnp.ones((1, 1024), F32), jnp.zeros((2, 1024), F32),
                            g["conv_w"].reshape(6, 1024), jnp.zeros((2, 1024), F32)])
    w_in_received, small_all = _exchange([w_in_pieces, small], "scatter_late", gather=[False, True])
    received = {"w_in": w_in_received}
    for names, handle in shipped:
        received.update(zip(names, _exchange_wait(handle, grad_x, "scatter_wait_" + names[0])))
    big = {n: _adamw(received[n], w[n], m[n], v[n], "adamw_" + n) for n in _MATRICES}
    zeros16 = jnp.zeros((_SMALL_ROWS, 1024), F32)
    summed = _adamw(small_all, zeros16, zeros16, zeros16, "sum_small")[0]
    conv_g = lax.dynamic_slice(summed[_CONV_ROW:_CONV_ROW + 6].reshape(DN_CONV, N_DEV * conv_shard), (0, me * conv_shard),
                               (DN_CONV, conv_shard))
    pad_conv = lambda t: jnp.pad(t.reshape(1, DN_CONV * conv_shard), ((0, 2), (0, 1024 - DN_CONV * conv_shard)))
    small_g = jnp.concatenate([summed[0:5], pad_conv(conv_g)], axis=0)[None]
    pack8 = lambda d: _pack_small(d, [pad_conv(d["conv_w"])])
    sm = _adamw(small_g, pack8(w), pack8(m), pack8(v), "adamw_vectors")

    outs = []
    for i, small_buf in enumerate(sm):
        d = {n: big[n][i] for n in _MATRICES}
        d.update(_unpack_small(small_buf, w))
        d["conv_w"] = small_buf[5, :DN_CONV * conv_shard].reshape(DN_CONV, conv_shard)
        outs.append(d)
    result = [summed[_LOSS_ROW, 0], grad_x[None]]
    for d in outs:
        d["w_in"] = d["w_in"].T
        for n in _ORDER:
            result.append(d[n][None] if n in _MATRICES or n == "conv_w" else d[n].reshape(w[n].shape))
    return tuple(result)
```

```python
import jax
import jax.numpy as jnp
from jax import lax
from jax.experimental import pallas as pl
from jax.experimental.pallas import tpu as pltpu

F32, BF16 = jnp.float32, jnp.bfloat16
EPS = 1e-6
D_MODEL = 1024
N_DEV = 8
ATTN_BLOCK = 128
HEAD_PAIR = 128
DN_HEADS = 4
DN_DIM = 128
DN_CHUNK = 64
DN_CONV = 4
ROPE_THETA = 10000.0
D_IN = 2824
D_IN_PAD = 3072
BLK_Q, BLK_Z = 0, 1
BLK_DN, BLK_K, BLK_V, BLK_G = 8, 20, 21, 22
BLK_KV, BLK_G_PAD = 10, 11
VMEM_LIMIT = 56 * 1024 * 1024
NEG = -1e30
ADAM_LR, ADAM_B1, ADAM_B2, ADAM_EPS, ADAM_WD, ADAM_STEP = 0.001, 0.9, 0.999, 1e-08, 0.01, 10
MESH = pl.DeviceIdType.MESH


def _bf(x):
    return x.astype(BF16)


def _dot(a, b):
    return jnp.dot(a, b, preferred_element_type=F32)


def _dot_nt(a, b):
    return lax.dot_general(a, b, (((1,), (1,)), ((), ())), preferred_element_type=F32)


def _dot_tn(a, b):
    return lax.dot_general(a, b, (((0,), (0,)), ((), ())), preferred_element_type=F32)


def _sigmoid(x):
    return 1.0 / (1.0 + jnp.exp(-x))


def _params(sem):
    return pltpu.CompilerParams(dimension_semantics=sem, vmem_limit_bytes=VMEM_LIMIT)


def _mm_nn(x, w, *, name, out_dtypes, tn, epi=None, extra=(), tm=512, w_row_block=0, after=None):
    S, K = x.shape
    N = w.shape[1]
    r0 = w_row_block * K
    tm = min(tm, S)
    n_extra = len(extra)
    waits = [] if after is None else [after]

    def body(x_ref, w_ref, *rest):
        rest = rest[:len(rest) - len(out_dtypes) - len(waits)] + rest[len(rest) - len(out_dtypes):]
        xb = _bf(x_ref[...])
        for c in range(N // tn):
            cols = slice(c * tn, (c + 1) * tn)
            acc = _dot(xb, w_ref[r0:r0 + K, cols])
            res = epi(acc, *[r[:, cols] for r in rest[:n_extra]]) if epi else (acc,)
            for o, r in zip(rest[n_extra:], res):
                o[:, cols] = r.astype(o.dtype)

    tile = pl.BlockSpec((tm, N), lambda i: (i, 0))
    return pl.pallas_call(
        body, grid=(S // tm,), name=name,
        in_specs=[pl.BlockSpec((tm, K), lambda i: (i, 0)), pl.BlockSpec(w.shape, lambda i: (0, 0))] + [tile] * n_extra
        + [pl.BlockSpec((8, 128), lambda i: (0, 0))] * len(waits),
        out_specs=[tile] * len(out_dtypes),
        out_shape=[jax.ShapeDtypeStruct((S, N), dt) for dt in out_dtypes],
        compiler_params=_params(("parallel",)),
    )(x, w, *extra, *waits)


def _mm_nt(dy, w, *, name, out_dtype, tn, epi=None, extra=(), tm=512):
    S, N = dy.shape
    K = w.shape[0]
    tm = min(tm, S)
    n_extra = len(extra)

    def body(dy_ref, w_ref, *rest):
        dyb = _bf(dy_ref[...])
        for c in range(K // tn):
            cols = slice(c * tn, (c + 1) * tn)
            acc = _dot_nt(dyb, w_ref[cols, :])
            if epi:
                acc = epi(acc, *[r[:, cols] for r in rest[:n_extra]])
            rest[n_extra][:, cols] = acc.astype(out_dtype)

    tile = pl.BlockSpec((tm, K), lambda i: (i, 0))
    return pl.pallas_call(
        body, grid=(S // tm,), name=name,
        in_specs=[pl.BlockSpec((tm, N), lambda i: (i, 0)), pl.BlockSpec(w.shape, lambda i: (0, 0))] + [tile] * n_extra,
        out_specs=tile,
        out_shape=jax.ShapeDtypeStruct((S, K), out_dtype),
        compiler_params=_params(("parallel",)),
    )(dy, w, *extra)


def _mm_tn(x, dy, *, name, tm, tn, out_dtype=F32, column_shards=False):
    S, K = x.shape
    N = dy.shape[1]

    def body(x_ref, dy_ref, o_ref):
        o_ref[...] = _dot_tn(_bf(x_ref[...]), _bf(dy_ref[...])).astype(out_dtype)

    if column_shards:
        out_spec = pl.BlockSpec((None, tm, tn), lambda i, j: (j, i, 0))
        out_shape = jax.ShapeDtypeStruct((N // tn, K, tn), out_dtype)
    else:
        out_spec = pl.BlockSpec((tm, tn), lambda i, j: (i, j))
        out_shape = jax.ShapeDtypeStruct((K, N), out_dtype)
    return pl.pallas_call(
        body, grid=(K // tm, N // tn), name=name,
        in_specs=[pl.BlockSpec((S, tm), lambda i, j: (0, i)), pl.BlockSpec((S, tn), lambda i, j: (0, j))],
        out_specs=out_spec, out_shape=out_shape,
        compiler_params=_params(("parallel", "parallel")),
    )(x, dy)


def _rowwise(body, *, tiled, full, out_tiled, out_acc, name, tm=512, smem=()):
    S = tiled[0].shape[0]
    tm = min(tm, S)
    n_in = len(smem) + len(tiled) + len(full)

    def kern(*refs):
        @pl.when(pl.program_id(0) == 0)
        def _():
            for r in refs[n_in + len(out_tiled):]:
                r[...] = jnp.zeros_like(r)
        body(*refs)

    in_specs = [pl.BlockSpec(memory_space=pltpu.SMEM) for _ in smem]
    in_specs += [pl.BlockSpec((tm, a.shape[1]), lambda i: (i, 0)) for a in tiled]
    in_specs += [pl.BlockSpec(a.shape, lambda i, nd=a.ndim: (0,) * nd) for a in full]
    out_specs = [pl.BlockSpec((tm, w), lambda i: (i, 0)) for w, _ in out_tiled]
    out_specs += [pl.BlockSpec(shp, lambda i, nd=len(shp): (0,) * nd) for shp, _ in out_acc]
    out_shape = [jax.ShapeDtypeStruct((S, w), dt) for w, dt in out_tiled]
    out_shape += [jax.ShapeDtypeStruct(shp, dt) for shp, dt in out_acc]
    return pl.pallas_call(
        kern, grid=(S // tm,), name=name, in_specs=in_specs, out_specs=out_specs, out_shape=out_shape,
        compiler_params=_params(("arbitrary",)),
    )(*smem, *tiled, *full)


def _rms_stats(x):
    r = lax.rsqrt(jnp.mean(x * x, axis=-1, keepdims=True) + EPS)
    return r, x * r


def _rmsnorm_fwd(x, g, name):
    def body(x_ref, g_ref, o_ref):
        _, xh = _rms_stats(x_ref[...])
        o_ref[...] = _bf(xh * g_ref[...])

    return _rowwise(body, tiled=[x], full=[g], out_tiled=[(x.shape[1], BF16)], out_acc=[], name=name)[0]


def _rms_bwd_tile(x, g, dxn):
    r, xh = _rms_stats(x)
    dg = jnp.sum(dxn * xh, axis=0, keepdims=True)
    dn = dxn * g
    dx = r * (dn - xh * jnp.mean(dn * xh, axis=-1, keepdims=True))
    return dx, dg


def _rmsnorm_bwd(x, g, dxn, dres, name):
    def body(x_ref, dxn_ref, dres_ref, g_ref, dx_ref, dg_ref):
        dx, dg = _rms_bwd_tile(x_ref[...], g_ref[...], dxn_ref[...])
        dx_ref[...] = dres_ref[...] + dx
        dg_ref[...] += dg

    n = x.shape[1]
    return _rowwise(body, tiled=[x, dxn, dres], full=[g], out_tiled=[(n, F32)], out_acc=[((1, n), F32)], name=name)


def _final_loss(h3, g, target):
    n = h3.shape[1]

    def body(h_ref, t_ref, g_ref, dh_ref, loss_ref, dg_ref):
        x = h_ref[...]
        _, xh = _rms_stats(x)
        e = xh * g_ref[...] - t_ref[...]
        per_tok = jnp.mean(e * e, axis=-1, keepdims=True)
        loss_ref[...] += 0.5 * jnp.sum(per_tok, axis=0, keepdims=True)
        dx, dg = _rms_bwd_tile(x, g_ref[...], e * (1.0 / n))
        dh_ref[...] = dx
        dg_ref[...] += dg

    return _rowwise(body, tiled=[h3, target], full=[g], out_tiled=[(n, F32)],
                    out_acc=[((1, 128), F32), ((1, n), F32)], name="final_loss")


def _ple_bwd(dh3, pp, gate):
    def body(dh_ref, pp_ref, gate_ref, dgl_ref, dpp_ref):
        dh, gt = dh_ref[...], gate_ref[...]
        dgl_ref[...] = _bf(dh * pp_ref[...] * gt * (1.0 - gt))
        dpp_ref[...] = _bf(dh * gt)

    n = dh3.shape[1]
    return _rowwise(body, tiled=[dh3, pp, gate], full=[], out_tiled=[(n, BF16), (n, BF16)], out_acc=[], name="ple_bwd")


def _rope_tables(S):
    half = 32
    inv = 1.0 / (ROPE_THETA ** (jnp.arange(half, dtype=F32) * (2.0 / 64)))
    ang = jnp.arange(S).astype(F32)[:, None] * inv[None, :]
    cos, sin = jnp.cos(ang), jnp.sin(ang)
    return jnp.tile(cos, (1, 4)), jnp.concatenate([-sin, sin, -sin, sin], axis=1)


def _attn_common(i, kc, kp, vc, vp, cc, sc, cp, sp):
    lane = lax.broadcasted_iota(jnp.int32, (1, HEAD_PAIR), 1)
    lane_lo = jnp.bitwise_and(lane, 63) < 32
    slot = [lane < 64, lane >= 64]

    def swap_halves(t):
        return jnp.where(lane_lo, pltpu.roll(t, 96, 1), pltpu.roll(t, 32, 1))

    def rope(t, cos, sin):
        return t * cos + swap_halves(t) * sin

    def unrope(d, cos, sin):
        return d * cos + swap_halves(d * sin)

    k2 = jnp.concatenate([rope(kp, cp, sp), rope(kc, cc, sc)], axis=0)
    v2 = jnp.concatenate([vp, vc], axis=0)
    r = lax.broadcasted_iota(jnp.int32, (ATTN_BLOCK, 2 * ATTN_BLOCK), 0)
    c = lax.broadcasted_iota(jnp.int32, (ATTN_BLOCK, 2 * ATTN_BLOCK), 1)
    valid = (c > r) & (c <= r + ATTN_BLOCK) & jnp.logical_or(c >= ATTN_BLOCK, i > 0)
    ks, vs = {}, {}
    for j in range(2):
        kn = jnp.where(slot[j], k2, 0.0)
        vn = jnp.where(slot[j], v2, 0.0)
        for s in range(2):
            ks[j, s] = _bf(kn if s == j else pltpu.roll(kn, 64, 1))
            vs[j, s] = _bf(vn if s == j else pltpu.roll(vn, 64, 1))
    return slot, rope, unrope, valid, ks, vs


def _attn_probs(scores, valid, sink):
    s = jnp.where(valid, scores * 0.125, NEG)
    m = jnp.maximum(jnp.max(s, axis=1, keepdims=True), sink)
    e = jnp.exp(s - m)
    inv_z = 1.0 / (jnp.sum(e, axis=1, keepdims=True) + jnp.exp(sink - m))
    return e * inv_z, jnp.exp(sink - m) * inv_z


def _attn_specs(S):
    nb = S // ATTN_BLOCK
    prev = lambda i: jnp.maximum(i - 1, 0)
    blk = lambda w, col, row=(lambda i: i): pl.BlockSpec((ATTN_BLOCK, w), lambda i: (row(i), col))
    in_specs = [pl.BlockSpec(memory_space=pltpu.SMEM),
                blk(512, BLK_Q), blk(128, BLK_K), blk(128, BLK_K, prev), blk(128, BLK_V), blk(128, BLK_V, prev),
                blk(128, 0), blk(128, 0), blk(128, 0, prev), blk(128, 0, prev)]
    return nb, in_specs


def _attn_fwd(pa, cos, sin, sinks):
    S = pa.shape[0]
    nb, in_specs = _attn_specs(S)

    def body(sinks_ref, q_ref, kc_ref, kp_ref, vc_ref, vp_ref, cc_ref, sc_ref, cp_ref, sp_ref, o_ref):
        i = pl.program_id(0)
        cc, sc = cc_ref[...], sc_ref[...]
        _, rope, _, valid, ks, vs = _attn_common(i, kc_ref[...], kp_ref[...], vc_ref[...], vp_ref[...],
                                                 cc, sc, cp_ref[...], sp_ref[...])
        pair_cols = [slice(HEAD_PAIR * pair, HEAD_PAIR * (pair + 1)) for pair in range(4)]
        qps = [_bf(rope(q_ref[:, cols], cc, sc)) for cols in pair_cols]
        outs = {}

        def head_program(h):
            pair, s = divmod(h, 2)
            j = h // 4
            scores = _dot_nt(qps[pair], ks[j, s])
            yield
            p, _ = _attn_probs(scores, valid, sinks_ref[h])
            outs[h] = _dot(_bf(p), vs[j, s])

        _interleave(head_program(h) for h in range(8))
        for pair, cols in enumerate(pair_cols):
            o_ref[:, cols] = outs[2 * pair] + outs[2 * pair + 1]

    return pl.pallas_call(
        body, grid=(nb,), name="attn_fwd", in_specs=in_specs,
        out_specs=pl.BlockSpec((ATTN_BLOCK, 512), lambda i: (i, 0)),
        out_shape=jax.ShapeDtypeStruct((S, 512), F32),
        compiler_params=_params(("parallel",)),
    )(sinks, pa, pa, pa, pa, pa, cos, sin, cos, sin)


def _attn_bwd(pa, cos, sin, sinks, dcat):
    S = pa.shape[0]
    nb, in_specs = _attn_specs(S)
    in_specs = in_specs + [pl.BlockSpec((ATTN_BLOCK, 512), lambda i: (i, 0))]

    def body(sinks_ref, q_ref, kc_ref, kp_ref, vc_ref, vp_ref, cc_ref, sc_ref, cp_ref, sp_ref, do_ref,
             dq_ref, dk_ref, dv_ref, dsink_ref):
        i = pl.program_id(0)

        @pl.when(i == 0)
        def _():
            dk_ref[...] = jnp.zeros_like(dk_ref)
            dv_ref[...] = jnp.zeros_like(dv_ref)
            dsink_ref[...] = jnp.zeros_like(dsink_ref)

        cc, sc, cp, sp = cc_ref[...], sc_ref[...], cp_ref[...], sp_ref[...]
        slot, rope, unrope, valid, ks, vs = _attn_common(i, kc_ref[...], kp_ref[...], vc_ref[...], vp_ref[...], cc, sc, cp, sp)
        pair_cols = [slice(HEAD_PAIR * pair, HEAD_PAIR * (pair + 1)) for pair in range(4)]
        qps = [_bf(rope(q_ref[:, cols], cc, sc)) for cols in pair_cols]
        dobs = [_bf(do_ref[:, cols]) for cols in pair_cols]
        dqs, dks, dvs = {}, {}, {}

        def head_program(h):
            pair, s = divmod(h, 2)
            j = h // 4
            qp, dob = qps[pair], dobs[pair]
            scores = _dot_nt(qp, ks[j, s])
            dp = _dot_nt(dob, vs[j, s])
            yield
            p, p_sink = _attn_probs(scores, valid, sinks_ref[h])
            dr = jnp.sum(p * dp, axis=1, keepdims=True)
            ds = _bf(p * (dp - dr) * 0.125)
            dsink_ref[h:h + 1, :] += -jnp.sum(p_sink * dr, axis=0, keepdims=True)
            dqs[h] = _dot(ds, ks[j, s])
            dk_h = _dot_tn(ds, qp)
            dv_h = _dot_tn(_bf(p), dob)
            yield
            dk_h, dv_h = jnp.where(slot[s], dk_h, 0.0), jnp.where(slot[s], dv_h, 0.0)
            if s != j:
                dk_h, dv_h = pltpu.roll(dk_h, 64, 1), pltpu.roll(dv_h, 64, 1)
            dks[h], dvs[h] = dk_h, dv_h

        _interleave(head_program(h) for h in range(8))
        dk2 = sum((dks[h] for h in range(1, 8)), dks[0])
        dv2 = sum((dvs[h] for h in range(1, 8)), dvs[0])
        for pair, cols in enumerate(pair_cols):
            dq_ref[:, cols] = _bf(unrope(dqs[2 * pair] + dqs[2 * pair + 1], cc, sc))
        cur = pl.ds(pl.multiple_of(i * ATTN_BLOCK, ATTN_BLOCK), ATTN_BLOCK)
        dk_ref[cur, :] += unrope(dk2[ATTN_BLOCK:], cc, sc)
        dv_ref[cur, :] += dv2[ATTN_BLOCK:]

        @pl.when(i > 0)
        def _():
            prv = pl.ds(pl.multiple_of((i - 1) * ATTN_BLOCK, ATTN_BLOCK), ATTN_BLOCK)
            dk_ref[prv, :] += unrope(dk2[:ATTN_BLOCK], cp, sp)
            dv_ref[prv, :] += dv2[:ATTN_BLOCK]

    whole = lambda w: pl.BlockSpec((S, w), lambda i: (0, 0))
    return pl.pallas_call(
        body, grid=(nb,), name="attn_bwd", in_specs=in_specs,
        out_specs=[pl.BlockSpec((ATTN_BLOCK, 512), lambda i: (i, BLK_Q)), whole(128), whole(128),
                   pl.BlockSpec((8, 128), lambda i: (0, 0))],
        out_shape=[jax.ShapeDtypeStruct((S, D_IN_PAD), BF16), jax.ShapeDtypeStruct((S, 128), F32),
                   jax.ShapeDtypeStruct((S, 128), F32), jax.ShapeDtypeStruct((8, 128), F32)],
        compiler_params=_params(("arbitrary",)),
    )(sinks, pa, pa, pa, pa, pa, cos, sin, cos, sin, dcat)


CONV_ROWS = 512
CONV_PAD = 8


def _conv_silu(scr, w, r0):
    y = w[3:4, :] * scr[pl.ds(CONV_PAD + r0, CONV_ROWS), :]
    for j in range(DN_CONV - 1):
        y = y + w[j:j + 1, :] * scr[pl.ds(CONV_PAD + r0 - 3 + j, CONV_ROWS), :]
    return y


def _dn_prep_fwd(pd, conv_w):
    S = pd.shape[0]
    assert S % CONV_ROWS == 0

    def body(x_ref, w_ref, o_ref, scr):
        b = pl.program_id(0)
        scr[0:CONV_PAD, :] = jnp.zeros((CONV_PAD, DN_DIM), F32)
        scr[pl.ds(CONV_PAD, S), :] = x_ref[...]
        w = w_ref[...]
        q_scale = jnp.where(b < DN_HEADS, DN_DIM ** -0.5, 1.0)
        for r0 in range(0, S, CONV_ROWS):
            y = _conv_silu(scr, w, r0)
            a = y * _sigmoid(y)
            rs = lax.rsqrt(jnp.sum(a * a, axis=1, keepdims=True) + EPS)
            o_ref[pl.ds(r0, CONV_ROWS), :] = a * jnp.where(b < 2 * DN_HEADS, rs * q_scale, 1.0)

    col = pl.BlockSpec((S, DN_DIM), lambda b: (0, b))
    return pl.pallas_call(
        body, grid=(3 * DN_HEADS,), name="dn_prep_fwd",
        in_specs=[pl.BlockSpec((S, DN_DIM), lambda b: (0, BLK_DN + b)), pl.BlockSpec((DN_CONV, DN_DIM), lambda b: (0, b))],
        out_specs=col,
        out_shape=jax.ShapeDtypeStruct((S, 3 * DN_HEADS * DN_DIM), F32),
        scratch_shapes=[pltpu.VMEM((S + CONV_PAD, DN_DIM), F32)],
        compiler_params=_params(("parallel",)),
    )(pd, conv_w)


def _dn_prep_bwd(pd, conv_w, dqkv, dproj):
    S = pd.shape[0]

    def body(x_ref, w_ref, d_ref, _, dx_ref, dw_ref, scr, dscr):
        b = pl.program_id(0)
        scr[0:CONV_PAD, :] = jnp.zeros((CONV_PAD, DN_DIM), F32)
        scr[pl.ds(CONV_PAD, S), :] = x_ref[...]
        dscr[pl.ds(S, CONV_PAD), :] = jnp.zeros((CONV_PAD, DN_DIM), F32)
        w = w_ref[...]
        q_scale = jnp.where(b < DN_HEADS, DN_DIM ** -0.5, 1.0)
        is_qk = b < 2 * DN_HEADS
        dw = [jnp.zeros((1, DN_DIM), F32) for _ in range(DN_CONV)]
        for r0 in range(0, S, CONV_ROWS):
            y = _conv_silu(scr, w, r0)
            sg = _sigmoid(y)
            a = y * sg
            dout = d_ref[pl.ds(r0, CONV_ROWS), :]
            rs = lax.rsqrt(jnp.sum(a * a, axis=1, keepdims=True) + EPS)
            da_qk = q_scale * rs * (dout - a * (rs * rs) * jnp.sum(dout * a, axis=1, keepdims=True))
            dy = jnp.where(is_qk, da_qk, dout) * (sg * (1.0 + y * (1.0 - sg)))
            dscr[pl.ds(r0, CONV_ROWS), :] = dy
            for j in range(DN_CONV):
                dw[j] = dw[j] + jnp.sum(dy * scr[pl.ds(CONV_PAD + r0 - 3 + j, CONV_ROWS), :], axis=0, keepdims=True)
        for j in range(DN_CONV):
            dw_ref[j:j + 1, :] = dw[j]
        for r0 in range(0, S, CONV_ROWS):
            dx = w[3:4, :] * dscr[pl.ds(r0, CONV_ROWS), :]
            for j in range(DN_CONV - 1):
                dx = dx + w[j:j + 1, :] * dscr[pl.ds(r0 + 3 - j, CONV_ROWS), :]
            dx_ref[pl.ds(r0, CONV_ROWS), :] = _bf(dx)

    col = pl.BlockSpec((S, DN_DIM), lambda b: (0, b))
    proj_col = pl.BlockSpec((S, DN_DIM), lambda b: (0, BLK_DN + b))
    wcol = pl.BlockSpec((DN_CONV, DN_DIM), lambda b: (0, b))
    return pl.pallas_call(
        body, grid=(3 * DN_HEADS,), name="dn_prep_bwd",
        in_specs=[proj_col, wcol, col, pl.BlockSpec(memory_space=pl.ANY)], out_specs=[proj_col, wcol],
        out_shape=[jax.ShapeDtypeStruct(dproj.shape, dproj.dtype), jax.ShapeDtypeStruct((DN_CONV, 3 * DN_HEADS * DN_DIM), F32)],
        scratch_shapes=[pltpu.VMEM((S + CONV_PAD, DN_DIM), F32), pltpu.VMEM((S + CONV_PAD, DN_DIM), F32)],
        input_output_aliases={3: 0},
        compiler_params=_params(("parallel",)),
    )(pd, conv_w, dqkv, dproj)


CPAD = 128
CHUNKS_LOCAL = 2
CHUNKS_SCAN = 4


def _chunk_masks():
    ii = lax.broadcasted_iota(jnp.int32, (DN_CHUNK, CPAD), 0)
    jj = lax.broadcasted_iota(jnp.int32, (DN_CHUNK, CPAD), 1)
    return ii, jj


def _rows_pad(a):
    return jnp.concatenate([a, jnp.zeros_like(a)], axis=0)


def _hi_lo(a):
    hi = _bf(a)
    return hi, _bf(a - hi.astype(F32))


def _double_step(t, p):
    C = DN_CHUNK
    th, tl = _hi_lo(t)
    ph, pl_ = _hi_lo(p)
    r1 = _dot(jnp.concatenate([th, tl, ph, pl_], axis=0), _rows_pad(ph))
    r2 = _dot(jnp.concatenate([th, ph], axis=0), _rows_pad(pl_))
    return t + (r1[:C] + r1[C:2 * C] + r2[:C]), r1[2 * C:3 * C] + r1[3 * C:] + r2[C:]


def _dot3_nt(a, b):
    C = DN_CHUNK
    ah, al = _hi_lo(a)
    bh, bl = _hi_lo(b)
    r1 = _dot_nt(jnp.concatenate([ah, al], axis=0), _rows_pad(bh))
    return r1[:C] + r1[C:] + _dot_nt(ah, _rows_pad(bl))


def _dot3_tn(a, b):
    C = DN_CHUNK
    ah, al = _hi_lo(a)
    bh, bl = _hi_lo(b)
    return _dot_tn(ah, bh)[:C] + _dot_tn(al, bh)[:C] + _dot_tn(ah, bl)[:C]


def _interleave(programs):
    programs = list(programs)
    while programs:
        alive = []
        for prog in programs:
            try:
                next(prog)
                alive.append(prog)
            except StopIteration:
                pass
        programs = alive


def _col_to_row(col, ii, jj):
    return jnp.sum(jnp.where(ii == jj, col, 0.0), axis=0, keepdims=True)


def _row_to_col(row, ii, jj):
    return jnp.sum(jnp.where(ii == jj, row, 0.0), axis=1, keepdims=True)


def _decay(gc_col, ii, jj):
    diff = gc_col - _col_to_row(gc_col, ii, jj)
    return jnp.where(jj <= ii, jnp.exp(jnp.where(jj <= ii, diff, 0.0)), 0.0)


def _softplus(x):
    return jnp.maximum(x, 0.0) + jnp.log(1.0 + jnp.exp(-jnp.abs(x)))


def _head(h):
    return slice(DN_DIM * h, DN_DIM * (h + 1))


def _dn_chunk_fwd(qkv, pg, a_log, dt_bias):
    S = qkv.shape[0]
    C = DN_CHUNK
    G = CHUNKS_LOCAL
    R = G * C
    steps = S // R

    def body(alog_ref, dtb_ref, qkv_ref, pg_ref, w_ref, u_ref, qg_ref, kd_ref, a_ref, t_ref, gcs_ref):
        ii, jj = _chunk_masks()
        lane = lax.broadcasted_iota(jnp.int32, (1, 128), 1)
        eye = (ii == jj).astype(F32)
        gcs_parts = [[] for _ in range(G)]

        def head_program(chunk, h):
            rows = slice(chunk * C, (chunk + 1) * C)
            q, k, v = qkv_ref[rows, _head(h)], qkv_ref[rows, _head(DN_HEADS + h)], qkv_ref[rows, _head(2 * DN_HEADS + h)]
            beta = _sigmoid(pg_ref[rows, h:h + 1])
            g_col = -jnp.exp(alog_ref[h]) * _softplus(pg_ref[rows, DN_HEADS + h:DN_HEADS + h + 1] + dtb_ref[h])
            g_row = _col_to_row(g_col, ii, jj)
            gc_col = jnp.sum(jnp.where(jj <= ii, g_row, 0.0), axis=1, keepdims=True)
            dec = _decay(gc_col, ii, jj)
            eg = jnp.exp(gc_col)
            kb, vb = k * beta, v * beta
            k_rows = _rows_pad(_bf(k))
            kk = _dot_nt(_bf(kb), k_rows)
            qk = _dot_nt(_bf(q), k_rows)
            yield
            t, pw = eye, -jnp.where(jj < ii, kk * dec, 0.0)
            for _ in range(6):
                t, pw = _double_step(t, pw)
                yield
            tb = _bf(t)
            u_ref[rows, _head(h)] = _dot(tb, _rows_pad(_bf(vb)))
            w_ref[rows, _head(h)] = _bf(_dot(tb, _rows_pad(_bf(kb * eg))))
            a_ref[h, rows] = _bf(qk * dec)
            t_ref[h, rows] = t
            qg_ref[rows, _head(h)] = _bf(q * eg)
            kd_ref[rows, _head(h)] = _bf(k * jnp.exp(gc_col[C - 1:C, :] - gc_col))
            gcs_parts[chunk].append(jnp.where(lane == h, gc_col, 0.0) + jnp.where(lane == DN_HEADS + h, beta, 0.0)
                                    + jnp.where(lane == 2 * DN_HEADS + h, g_col, 0.0))

        _interleave(head_program(chunk, h) for chunk in range(G) for h in range(DN_HEADS))
        for chunk in range(G):
            gcs_ref[chunk * C:(chunk + 1) * C, :] = sum(gcs_parts[chunk][1:], gcs_parts[chunk][0])

    smem = pl.BlockSpec(memory_space=pltpu.SMEM)
    wide = pl.BlockSpec((R, 512), lambda n: (n, 0))
    sq = pl.BlockSpec((DN_HEADS, R, CPAD), lambda n: (0, n, 0))
    narrow = pl.BlockSpec((R, 128), lambda n: (n, 0))
    f = lambda *shp: jax.ShapeDtypeStruct(shp, F32)
    b = lambda *shp: jax.ShapeDtypeStruct(shp, BF16)
    return pl.pallas_call(
        body, grid=(steps,), name="dn_chunk_fwd",
        in_specs=[smem, smem, pl.BlockSpec((R, 1536), lambda n: (n, 0)), pl.BlockSpec((R, 128), lambda n: (n, BLK_G))],
        out_specs=[wide, wide, wide, wide, sq, sq, narrow],
        out_shape=[b(S, 512), f(S, 512), b(S, 512), b(S, 512), b(DN_HEADS, S, CPAD), f(DN_HEADS, S, CPAD), f(S, 128)],
        compiler_params=_params(("parallel",)),
    )(a_log, dt_bias, qkv, pg)


def _gated_norm(o, z, gn):
    r, oh = _rms_stats(o)
    return oh * gn * (z * _sigmoid(z))


def _dn_scan_fwd(w, u, qg, kd, a, gcs, pz, gn):
    S = w.shape[0]
    C = DN_CHUNK
    nc = S // C
    G = CHUNKS_SCAN
    R = G * C

    def body(w_ref, u_ref, qg_ref, kd_ref, a_ref, gcs_ref, z_ref, gn_ref, o_ref, vn_ref, sst_ref, out_ref, state):
        @pl.when(pl.program_id(0) == 0)
        def _():
            state[...] = jnp.zeros_like(state)

        def head_program(chunk, h):
            hs = _head(h)
            rows = slice(chunk * C, (chunk + 1) * C)
            s_in = state[h]
            sst_ref[chunk, h] = s_in
            sb = _bf(s_in)
            w_s = _dot(w_ref[rows, hs], sb)
            q_s = _dot(qg_ref[rows, hs], sb)
            yield
            vn = u_ref[rows, hs] - w_s
            vnb = _bf(vn)
            o = q_s + _dot(a_ref[h, rows], _rows_pad(vnb))
            k_v = _dot_tn(kd_ref[rows, hs], vnb)
            yield
            state[h] = s_in * jnp.exp(gcs_ref[(chunk + 1) * C - 1:(chunk + 1) * C, h:h + 1]) + k_v
            o_ref[rows, hs] = o
            vn_ref[rows, hs] = vn
            out_ref[rows, hs] = _gated_norm(o, z_ref[rows, hs], gn_ref[...])

        for chunk in range(G):
            _interleave(head_program(chunk, h) for h in range(DN_HEADS))

    wide = pl.BlockSpec((R, 512), lambda n: (n, 0))
    f = lambda *shp: jax.ShapeDtypeStruct(shp, F32)
    return pl.pallas_call(
        body, grid=(nc // G,), name="dn_scan_fwd",
        in_specs=[wide, wide, wide, wide, pl.BlockSpec((DN_HEADS, R, CPAD), lambda n: (0, n, 0)),
                  pl.BlockSpec((R, 128), lambda n: (n, 0)), pl.BlockSpec((R, 512), lambda n: (n, BLK_Z)),
                  pl.BlockSpec((1, DN_DIM), lambda n: (0, 0))],
        out_specs=[wide, wide, pl.BlockSpec((G, DN_HEADS, DN_DIM, DN_DIM), lambda n: (n, 0, 0, 0)), wide],
        out_shape=[f(S, 512), f(S, 512), f(nc, DN_HEADS, DN_DIM, DN_DIM), f(S, 512)],
        scratch_shapes=[pltpu.VMEM((DN_HEADS, DN_DIM, DN_DIM), F32)],
        compiler_params=_params(("arbitrary",)),
    )(w, u, qg, kd, a, gcs, pz, gn)


def _dn_scan_bwd(dcat, o, pz, gn, sst, vnew, w, qg, kd, a, gcs, dproj):
    S = o.shape[0]
    C = DN_CHUNK
    G = CHUNKS_SCAN
    R = G * C
    steps = S // R

    def body(dy_ref, o_ref, z_ref, gn_ref, sst_ref, vn_ref, w_ref, qg_ref, kd_ref, a_ref, gcs_ref, _,
             du_ref, dw_ref, dqg_ref, dkd_ref, da_ref, dz_ref, dsc_ref, dgn_ref, dstate):
        @pl.when(pl.program_id(0) == 0)
        def _():
            dstate[...] = jnp.zeros_like(dstate)
            dgn_ref[...] = jnp.zeros_like(dgn_ref)

        gn_ = gn_ref[...]
        lane = lax.broadcasted_iota(jnp.int32, (C, 128), 1)
        row = lax.broadcasted_iota(jnp.int32, (C, 128), 0)
        dgn_parts = []

        def head_program(chunk, h, dsc_parts):
            hs = _head(h)
            rows = slice(chunk * C, (chunk + 1) * C)
            ov, z, dout = o_ref[rows, hs], z_ref[rows, hs], dy_ref[rows, hs]
            r, oh = _rms_stats(ov)
            sg = _sigmoid(z)
            don = dout * (z * sg)
            dz_ref[rows, hs] = _bf(dout * (oh * gn_) * (sg * (1.0 + z * (1.0 - sg))))
            dgn_parts.append(jnp.sum(don * oh, axis=0, keepdims=True))
            dn = don * gn_
            do = _bf(r * (dn - oh * jnp.mean(dn * oh, axis=-1, keepdims=True)))
            s_in = sst_ref[chunk, h]
            sb = _bf(s_in)
            ds_out = dstate[h]
            dsb = _bf(ds_out)
            vnb = _bf(vn_ref[rows, hs])
            wb, qgb, kdb, ab = w_ref[rows, hs], qg_ref[rows, hs], kd_ref[rows, hs], a_ref[h, rows]
            dvn = _dot_tn(ab, do)[:C] + _dot(kdb, dsb)
            da_ref[h, rows] = _dot_nt(do, _rows_pad(vnb))
            dqg_ref[rows, hs] = _dot_nt(do, sb)
            dkd_ref[rows, hs] = _dot_nt(vnb, dsb)
            q_do = _dot_tn(qgb, do)
            yield
            dvnb = _bf(dvn)
            dw_ref[rows, hs] = _bf(-_dot_nt(dvnb, sb))
            w_dvn = _dot_tn(wb, dvnb)
            du_ref[rows, hs] = dvnb
            yield
            d_last = jnp.exp(gcs_ref[(chunk + 1) * C - 1:(chunk + 1) * C, h:h + 1])
            dd = jnp.sum(jnp.sum(ds_out * s_in, axis=1, keepdims=True), axis=0, keepdims=True)
            dsc_parts.append(jnp.where((lane == h) & (row == C - 1), dd * d_last, 0.0))
            dstate[h] = ds_out * d_last + q_do - w_dvn

        for chunk in reversed(range(G)):
            dsc_parts = []
            _interleave(head_program(chunk, h, dsc_parts) for h in range(DN_HEADS))
            dsc_ref[chunk * C:(chunk + 1) * C, :] = sum(dsc_parts[1:], dsc_parts[0])
        dgn_ref[...] += sum(dgn_parts[1:], dgn_parts[0])

    rev = lambda n: steps - 1 - n
    wide = pl.BlockSpec((R, 512), lambda n: (rev(n), 0))
    z_spec = pl.BlockSpec((R, 512), lambda n: (rev(n), BLK_Z))
    sq = pl.BlockSpec((DN_HEADS, R, CPAD), lambda n: (0, rev(n), 0))
    narrow = pl.BlockSpec((R, 128), lambda n: (rev(n), 0))
    gn_spec = pl.BlockSpec((1, DN_DIM), lambda n: (0, 0))
    f = lambda *shp: jax.ShapeDtypeStruct(shp, F32)
    b = lambda *shp: jax.ShapeDtypeStruct(shp, BF16)
    return pl.pallas_call(
        body, grid=(steps,), name="dn_scan_bwd",
        in_specs=[pl.BlockSpec((R, 512), lambda n: (rev(n), 1)), wide, z_spec, gn_spec,
                  pl.BlockSpec((G, DN_HEADS, DN_DIM, DN_DIM), lambda n: (rev(n), 0, 0, 0)),
                  wide, wide, wide, wide, sq, narrow, pl.BlockSpec(memory_space=pl.ANY)],
        out_specs=[wide, wide, wide, wide, sq, z_spec, narrow, gn_spec],
        out_shape=[b(S, 512), b(S, 512), f(S, 512), f(S, 512), f(DN_HEADS, S, CPAD),
                   jax.ShapeDtypeStruct(dproj.shape, dproj.dtype), f(S, 128), f(1, DN_DIM)],
        scratch_shapes=[pltpu.VMEM((DN_HEADS, DN_DIM, DN_DIM), F32)],
        input_output_aliases={11: 5},
        compiler_params=_params(("arbitrary",)),
    )(dcat, o, pz, gn, sst, vnew, w, qg, kd, a, gcs, dproj)


def _dn_chunk_bwd(qkv, pg, t_inv, gcs, du, dw, dqg, dkd, da, dsc, a_log, dt_bias, dproj):
    S = qkv.shape[0]
    C = DN_CHUNK
    G = CHUNKS_LOCAL
    R = G * C

    def body(alog_ref, dtb_ref, qkv_ref, pg_ref, t_ref, gcs_ref, du_ref, dw_ref, dqg_ref, dkd_ref, da_ref, dsc_ref, _,
             dqkv_ref, dpg_ref, acc_ref):
        @pl.when(pl.program_id(0) == 0)
        def _():
            acc_ref[...] = jnp.zeros_like(acc_ref)

        ii, jj = _chunk_masks()
        lane = lax.broadcasted_iota(jnp.int32, (1, 128), 1)
        row8 = lax.broadcasted_iota(jnp.int32, (8, 128), 0)
        lane8 = lax.broadcasted_iota(jnp.int32, (8, 128), 1)
        rowc = lax.broadcasted_iota(jnp.int32, (C, 1), 0)
        tril, strict = jj <= ii, jj < ii
        dpg_parts, acc_parts = [[] for _ in range(G)], []

        def head_program(chunk, h):
            rows = slice(chunk * C, (chunk + 1) * C)
            q, k, v = qkv_ref[rows, _head(h)], qkv_ref[rows, _head(DN_HEADS + h)], qkv_ref[rows, _head(2 * DN_HEADS + h)]
            gc_col, beta, g_col = gcs_ref[rows, h:h + 1], gcs_ref[rows, DN_HEADS + h:DN_HEADS + h + 1], \
                gcs_ref[rows, 2 * DN_HEADS + h:2 * DN_HEADS + h + 1]
            dec = _decay(gc_col, ii, jj)
            eg = jnp.exp(gc_col)
            g_last = gc_col[C - 1:C, :]
            ek = jnp.exp(g_last - gc_col)
            kb, vb = k * beta, v * beta
            kbg = kb * eg
            qb, kbb = _bf(q), _bf(kb)
            k_rows = _rows_pad(_bf(k))
            t = t_ref[h, rows]
            tb = _bf(t)
            dub, dwb = du_ref[rows, _head(h)], dw_ref[rows, _head(h)]
            dqg_, dkd_ = dqg_ref[rows, _head(h)], dkd_ref[rows, _head(h)]
            dt = _dot_nt(dub, _rows_pad(_bf(vb))) + _dot_nt(dwb, _rows_pad(_bf(kbg)))
            dvb = _dot_tn(tb, dub)[:C]
            dkbg = _dot_tn(tb, dwb)[:C]
            kk = _dot_nt(kbb, k_rows)
            qk = _dot_nt(qb, k_rows)
            yield
            dt_t = _dot3_nt(dt, t)
            yield
            dl = -_dot3_tn(t, dt_t)
            yield
            dm = jnp.where(strict, dl * dec, 0.0)
            dqk = jnp.where(tril, da_ref[h, rows] * dec, 0.0)
            gmat = dm * kk + dqk * qk
            dgc = jnp.sum(gmat, axis=1, keepdims=True) - _row_to_col(jnp.sum(gmat, axis=0, keepdims=True), ii, jj)
            dmb, dqkb = _bf(dm), _bf(dqk)
            dkb = _dot(dmb, k_rows) + dkbg * eg
            dk = _dot_tn(dmb, kbb)[:C] + _dot_tn(dqkb, qb)[:C] + dkd_ * ek
            dq = _dot(dqkb, k_rows) + dqg_ * eg
            yield
            tk = jnp.sum(dkd_ * k * ek, axis=1, keepdims=True)
            dgc = dgc + jnp.sum(dqg_ * q * eg, axis=1, keepdims=True) - tk + jnp.sum(dkbg * kbg, axis=1, keepdims=True)
            dgl = jnp.sum(tk, axis=0, keepdims=True) + dsc_ref[(chunk + 1) * C - 1:(chunk + 1) * C, h:h + 1]
            dgc = dgc + jnp.where(rowc == C - 1, dgl, 0.0)
            dk = dk + dkb * beta
            dbeta = jnp.sum(dkb * k, axis=1, keepdims=True) + jnp.sum(dvb * v, axis=1, keepdims=True)
            dqkv_ref[rows, _head(h)] = dq
            dqkv_ref[rows, _head(DN_HEADS + h)] = dk
            dqkv_ref[rows, _head(2 * DN_HEADS + h)] = dvb * beta
            dg_col = jnp.sum(jnp.where(jj >= ii, _col_to_row(dgc, ii, jj), 0.0), axis=1, keepdims=True)
            db = dbeta * beta * (1.0 - beta)
            da_in = dg_col * (-jnp.exp(alog_ref[h])) * _sigmoid(pg_ref[rows, DN_HEADS + h:DN_HEADS + h + 1] + dtb_ref[h])
            dpg_parts[chunk].append(jnp.where(lane == h, db, 0.0) + jnp.where(lane == DN_HEADS + h, da_in, 0.0))
            acc_parts.append(jnp.where((row8 == 0) & (lane8 == h), jnp.sum(dg_col * g_col, axis=0, keepdims=True), 0.0)
                             + jnp.where((row8 == 1) & (lane8 == h), jnp.sum(da_in, axis=0, keepdims=True), 0.0))

        _interleave(head_program(chunk, h) for chunk in range(G) for h in range(DN_HEADS))
        for chunk in range(G):
            dpg = sum(dpg_parts[chunk][1:], dpg_parts[chunk][0])
            dpg_ref[chunk * C:(chunk + 1) * C, :] = _bf(jnp.concatenate([dpg, jnp.zeros_like(dpg)], axis=1))
        acc_ref[...] += sum(acc_parts[1:], acc_parts[0])

    smem = pl.BlockSpec(memory_space=pltpu.SMEM)
    wide = pl.BlockSpec((R, 512), lambda n: (n, 0))
    sq = pl.BlockSpec((DN_HEADS, R, CPAD), lambda n: (0, n, 0))
    narrow = pl.BlockSpec((R, 128), lambda n: (n, 0))
    qkv_spec = pl.BlockSpec((R, 1536), lambda n: (n, 0))
    f = lambda *shp: jax.ShapeDtypeStruct(shp, F32)
    return pl.pallas_call(
        body, grid=(S // R,), name="dn_chunk_bwd",
        in_specs=[smem, smem, qkv_spec, pl.BlockSpec((R, 128), lambda n: (n, BLK_G)), sq, narrow, wide, wide, wide, wide, sq,
                  narrow, pl.BlockSpec(memory_space=pl.ANY)],
        out_specs=[qkv_spec, pl.BlockSpec((R, 256), lambda n: (n, BLK_G_PAD)), pl.BlockSpec((8, 128), lambda n: (0, 0))],
        out_shape=[f(S, 1536), jax.ShapeDtypeStruct(dproj.shape, dproj.dtype), f(8, 128)],
        input_output_aliases={12: 1},
        compiler_params=_params(("arbitrary",)),
    )(a_log, dt_bias, qkv, pg, t_inv, gcs, du, dw, dqg, dkd, da, dsc, dproj)


def _fill_kv(dk, dv, dproj):
    S = dk.shape[0]
    tm = min(512, S)

    def body(dk_ref, dv_ref, _, o_ref):
        o_ref[...] = _bf(jnp.concatenate([dk_ref[...], dv_ref[...]], axis=1))

    tile = pl.BlockSpec((tm, 128), lambda i: (i, 0))
    return pl.pallas_call(
        body, grid=(S // tm,), name="fill_kv",
        in_specs=[tile, tile, pl.BlockSpec(memory_space=pl.ANY)],
        out_specs=pl.BlockSpec((tm, 256), lambda i: (i, BLK_KV)),
        out_shape=jax.ShapeDtypeStruct(dproj.shape, dproj.dtype),
        input_output_aliases={2: 0},
        compiler_params=_params(("parallel",)),
    )(dk, dv, dproj)


def _w_in_to_internal(wt):
    return jnp.concatenate([wt[0:512], wt[2304:2816], wt[768:2304], wt[512:768], wt[2816:2824],
                            jnp.zeros((D_IN_PAD - D_IN, wt.shape[1]), wt.dtype)], axis=0)


def _w_in_from_internal(gt):
    return jnp.concatenate([gt[0:512], gt[2560:2816], gt[1024:2560], gt[512:1024], gt[2816:2824]], axis=0)


def _local_step(x, p, target, wts, first_weights, other_weights, ship_early):
    S = x.shape[0]
    cos, sin = _rope_tables(S)
    sinks, a_log, dt_bias = wts["sinks"].reshape(8), wts["a_log"].reshape(4), wts["dt_bias"].reshape(4)
    gn = wts["dn_norm"].reshape(1, DN_DIM)
    add = lambda acc, res: (acc + res,)

    u = _rmsnorm_fwd(x, wts["norm_mix"], "norm_mix_fwd")
    w_in_t, conv_w = first_weights(u)
    proj = _mm_nt(u, w_in_t, name="in_proj", out_dtype=F32, tn=512)
    attn = _attn_fwd(proj, cos, sin, sinks)
    qkv = _dn_prep_fwd(proj, conv_w)
    cw, cu, cqg, ckd, ca, ct, gcs = _dn_chunk_fwd(qkv, proj, a_log, dt_bias)
    o, vnew, sst, dn_out = _dn_scan_fwd(cw, cu, cqg, ckd, ca, gcs, proj, gn)
    w_o, = other_weights(("w_o",), dn_out)
    h1a, = _mm_nn(attn, w_o, name="out_proj_attn", out_dtypes=[F32], tn=512, epi=add, extra=[x], w_row_block=0)
    h1, = _mm_nn(dn_out, w_o, name="out_proj_dn", out_dtypes=[F32], tn=512, epi=add, extra=[h1a], w_row_block=1)
    m = _rmsnorm_fwd(h1, wts["norm_mlp"], "norm_mlp_fwd")

    def relu2(acc):
        r = jnp.maximum(acc, 0.0)
        return r * r, r

    w_up, = other_weights(("w_up",), m)
    hid, relu = _mm_nn(m, w_up, name="mlp_up", out_dtypes=[BF16, BF16], tn=512, epi=relu2)
    w_down, = other_weights(("w_down",), hid)
    h2, = _mm_nn(hid, w_down, name="mlp_down", out_dtypes=[F32], tn=512, epi=add, extra=[h1])
    n3 = _rmsnorm_fwd(h2, wts["norm_ple"], "norm_ple_fwd")
    w_pg, w_pp = other_weights(("w_ple_gate", "w_ple_proj"), n3)
    pp, = _mm_nn(p, w_pp, name="ple_proj", out_dtypes=[F32], tn=512)

    def ple(acc, h2_t, pp_t):
        gate = _sigmoid(acc)
        return h2_t + gate * pp_t, gate

    h3, gate = _mm_nn(n3, w_pg, name="ple_gate", out_dtypes=[F32, F32], tn=512, epi=ple, extra=[h2, pp])
    dh3, loss, d_norm_final = _final_loss(h3, wts["norm_final"].reshape(1, D_MODEL), target)

    g = {"norm_final": d_norm_final}
    dgl, dpp = _ple_bwd(dh3, pp, gate)
    early = {"w_ple_gate": _mm_tn(n3, dgl, name="d_w_ple_gate", tm=512, tn=1024, out_dtype=BF16).reshape(N_DEV, 128, 1024),
             "w_ple_proj": _mm_tn(p, dpp, name="d_w_ple_proj", tm=256, tn=128, out_dtype=BF16, column_shards=True)}
    dn3 = _mm_nt(dgl, w_pg, name="d_n3", out_dtype=F32, tn=512)
    dh2, g["norm_ple"] = _rmsnorm_bwd(h2, wts["norm_ple"], dn3, dh3, "norm_ple_bwd")
    d_act = _mm_nt(dh2, w_down, name="d_hidden", out_dtype=BF16, tn=512, epi=lambda acc, r: acc * (2.0 * r.astype(F32)), extra=[relu])
    early["w_down"] = _mm_tn(hid, dh2, name="d_w_down", tm=512, tn=1024, out_dtype=BF16).reshape(N_DEV, 512, 1024)
    early["w_up"] = _mm_tn(m, d_act, name="d_w_up", tm=1024, tn=512, out_dtype=BF16, column_shards=True)
    token = ship_early(early)
    dm = _mm_nt(d_act, w_up, name="d_m", out_dtype=F32, tn=512)
    dh1, g["norm_mlp"] = _rmsnorm_bwd(h1, wts["norm_mlp"] + token[0:1, 0:1], dm, dh2, "norm_mlp_bwd")
    dcat = _mm_nt(dh1, w_o, name="d_cat", out_dtype=F32, tn=512)
    d_w_o = jnp.concatenate([_mm_tn(attn, dh1, name="d_w_o_attn", tm=512, tn=512, out_dtype=BF16),
                             _mm_tn(dn_out, dh1, name="d_w_o_dn", tm=512, tn=512, out_dtype=BF16)], axis=0)
    token = ship_early({"w_o": d_w_o.reshape(N_DEV, 128, 1024)})
    dproj, dk, dv, dsinks = _attn_bwd(proj, cos, sin, sinks + token[0, 0], dcat)
    g["sinks"] = dsinks[:, 0].reshape(1, 8)
    du_, dw_, dqg, dkd, da, dproj, dsc, g["dn_norm"] = _dn_scan_bwd(dcat, o, proj, gn, sst, vnew, cw, cqg, ckd, ca, gcs, dproj)
    dqkv, dproj, gate_acc = _dn_chunk_bwd(qkv, proj, ct, gcs, du_, dw_, dqg, dkd, da, dsc, a_log, dt_bias, dproj)
    g["a_log"], g["dt_bias"] = gate_acc[0:1, 0:4], gate_acc[1:2, 0:4]
    dproj, g["conv_w"] = _dn_prep_bwd(proj, conv_w, dqkv, dproj)
    dproj = _fill_kv(dk, dv, dproj)
    token = ship_early({"w_in": _mm_tn(dproj, u, name="d_w_in", tm=512, tn=1024)})
    du_in, = _mm_nn(dproj, w_in_t, name="d_u", out_dtypes=[F32], tn=512, after=token)
    grad_x, g["norm_mix"] = _rmsnorm_bwd(x, wts["norm_mix"], du_in, dh1, "norm_mix_bwd")
    return loss, grad_x, g


def _peer(k):
    x, y, c = lax.axis_index("x"), lax.axis_index("y"), lax.axis_index("c")
    px = 1 - x if k & 4 else x
    py = 1 - y if k & 2 else y
    pc = 1 - c if k & 1 else c
    return (px, py, pc), 4 * px + 2 * py + pc


def _exchange(srcs, name, gather):
    n = len(srcs)
    gathers = list(gather) if isinstance(gather, (list, tuple)) else [gather] * n
    shapes = [(N_DEV,) + s.shape if gt else s.shape for s, gt in zip(srcs, gathers)]

    def body(*refs):
        src_refs, out_refs = refs[:n], refs[n:2 * n]
        send_sems, recv_sems, local_sems = refs[2 * n:]
        _, me = _peer(0)
        piece = lambda a, d: src_refs[a] if gathers[a] else src_refs[a].at[d]
        local = [pltpu.make_async_copy(piece(a, me), out_refs[a].at[me], local_sems.at[a]) for a in range(n)]
        for cp in local:
            cp.start()
        copies = []
        for a in range(n):
            for k in range(1, N_DEV):
                dev, idx = _peer(k)
                cp = pltpu.make_async_remote_copy(src_ref=piece(a, idx), dst_ref=out_refs[a].at[me],
                                                  send_sem=send_sems.at[a, k - 1], recv_sem=recv_sems.at[a, k - 1],
                                                  device_id=dev, device_id_type=MESH)
                cp.start()
                copies.append(cp)
        for cp in copies:
            cp.wait_recv()
        for cp in copies:
            cp.wait_send()
        for cp in local:
            cp.wait()

    anywhere = pl.BlockSpec(memory_space=pl.ANY)
    return pl.pallas_call(
        body, name=name, in_specs=[anywhere] * n, out_specs=[anywhere] * n,
        out_shape=[jax.ShapeDtypeStruct(shp, s.dtype) for shp, s in zip(shapes, srcs)],
        scratch_shapes=[pltpu.SemaphoreType.DMA((n, N_DEV - 1)), pltpu.SemaphoreType.DMA((n, N_DEV - 1)),
                        pltpu.SemaphoreType.DMA((n,))],
    )(*srcs)


_HBM = pl.BlockSpec(memory_space=pltpu.HBM)
_SEM = pl.BlockSpec(memory_space=pltpu.SEMAPHORE)
_EFFECT = pltpu.SideEffectType.DATAFLOW_SIDE_EFFECTING


def _split_copies(src_refs, land_refs, send_sems, recv_sems, gather, which=None):
    _, me = _peer(0)
    copies = []
    which = range(len(src_refs)) if which is None else which
    for a, src, land in zip(which, src_refs, land_refs):
        for k in range(1, N_DEV):
            dev, idx = _peer(k)
            sem = a * (N_DEV - 1) + k - 1
            copies.append(pltpu.make_async_remote_copy(
                src_ref=src if gather else src.at[idx], dst_ref=land.at[me], send_sem=send_sems.at[sem],
                recv_sem=recv_sems.at[sem], device_id=dev, device_id_type=MESH))
    return copies


def _exchange_start(srcs, name, gather):
    n = len(srcs)
    me = 4 * lax.axis_index("x") + 2 * lax.axis_index("y") + lax.axis_index("c")
    lands = []
    for s in srcs:
        own = s if gather else lax.dynamic_index_in_dim(s, me, 0, keepdims=False)
        shape = (N_DEV,) + s.shape if gather else s.shape
        lands.append(lax.dynamic_update_index_in_dim(lax.empty(shape, s.dtype), own, me, 0))

    def body(*refs):
        src_refs, land_refs = refs[:n], refs[n:2 * n]
        send_sems, recv_sems = refs[2 * n], refs[2 * n + 1]
        for cp in _split_copies(src_refs, land_refs, send_sems, recv_sems, gather):
            cp.start()
        refs[-1][...] = jnp.zeros_like(refs[-1])

    both = list(srcs) + lands
    sems = pltpu.SemaphoreType.DMA((n * (N_DEV - 1),))
    out = pl.pallas_call(
        body, name=name,
        out_shape=(sems, sems, *[pltpu.HBM(t.shape, t.dtype) for t in both], jax.ShapeDtypeStruct((8, 128), F32)),
        in_specs=[_HBM] * (2 * n), out_specs=(_SEM, _SEM, *[_HBM] * (2 * n), pl.BlockSpec(memory_space=pltpu.VMEM)),
        input_output_aliases={i: 2 + i for i in range(2 * n)},
        compiler_params=pltpu.CompilerParams(has_side_effects=_EFFECT),
    )(*[pltpu.with_memory_space_constraint(t, pltpu.HBM) for t in both])
    return (n, gather, out[:-1]), out[-1]


def _exchange_wait(handle, after, name, which=None):
    n_all, gather, (send_sems, recv_sems, *both_all) = handle
    which = list(range(n_all)) if which is None else list(which)
    n = len(which)
    both = [both_all[a] for a in which] + [both_all[n_all + a] for a in which]

    def body(*refs):
        src_refs, land_refs = refs[:n], refs[n:2 * n]
        for cp in _split_copies(src_refs, land_refs, refs[2 * n], refs[2 * n + 1], gather, which):
            cp.wait_send()
            cp.wait_recv()

    out = pl.pallas_call(
        body, name=name, out_shape=tuple(pltpu.HBM(t.shape, t.dtype) for t in both),
        in_specs=[_HBM] * (2 * n) + [_SEM, _SEM, pl.BlockSpec(memory_space=pl.ANY)], out_specs=tuple([_HBM] * (2 * n)),
        input_output_aliases={i: i for i in range(2 * n)},
        compiler_params=pltpu.CompilerParams(has_side_effects=_EFFECT),
    )(*both, send_sems, recv_sems, after)
    return list(out[n:])


def _adamw(parts, w, m, v, name):
    n, R, W = parts.shape
    tm = 128 if R % 128 == 0 else R

    def body(p_ref, w_ref, m_ref, v_ref, g_ref, d_ref, nm_ref, nv_ref):
        g = p_ref[0].astype(F32)
        for s in range(1, n):
            g = g + p_ref[s].astype(F32)
        nm = ADAM_B1 * m_ref[...] + (1.0 - ADAM_B1) * g
        nv = ADAM_B2 * v_ref[...] + (1.0 - ADAM_B2) * (g * g)
        m_hat = nm / (1.0 - ADAM_B1 ** ADAM_STEP)
        v_hat = nv / (1.0 - ADAM_B2 ** ADAM_STEP)
        g_ref[...] = g
        d_ref[...] = -ADAM_LR * (m_hat / (jnp.sqrt(v_hat) + ADAM_EPS) + ADAM_WD * w_ref[...])
        nm_ref[...] = nm
        nv_ref[...] = nv

    tile = pl.BlockSpec((tm, W), lambda i: (i, 0))
    return pl.pallas_call(
        body, grid=(R // tm,), name=name,
        in_specs=[pl.BlockSpec((n, tm, W), lambda i: (0, i, 0)), tile, tile, tile],
        out_specs=[tile] * 4, out_shape=[jax.ShapeDtypeStruct((R, W), F32)] * 4,
        compiler_params=_params(("parallel",)),
    )(parts, w, m, v)


_MATRICES = ("w_in", "w_o", "w_up", "w_down", "w_ple_gate", "w_ple_proj")


_OTHERS = ("w_o", "w_up", "w_down", "w_ple_gate", "w_ple_proj")


def _cols_from_shards(t):
    return jnp.transpose(t, (1, 0, 2)).reshape(t.shape[1], N_DEV * t.shape[2])


_SMALL_ROWS = 16
_VEC_ROW = {"norm_mix": 0, "norm_mlp": 1, "norm_ple": 2, "norm_final": 3}
_VEC_LANES = {"a_log": (0, 4), "dt_bias": (4, 8), "sinks": (8, 16), "dn_norm": (128, 256)}
_LOSS_ROW, _CONV_ROW = 5, 8


def _pack_small(vals, extra_rows):
    row4 = jnp.zeros((1024,), F32)
    for n, (a, b) in _VEC_LANES.items():
        row4 = row4.at[a:b].set(vals[n].reshape(b - a))
    rows = [vals[n].reshape(1, 1024) for n in ("norm_mix", "norm_mlp", "norm_ple", "norm_final")] + [row4.reshape(1, 1024)]
    return jnp.concatenate(rows + extra_rows, axis=0)


def _unpack_small(buf, like):
    out = {n: buf[r].reshape(like[n].shape) for n, r in _VEC_ROW.items()}
    for n, (a, b) in _VEC_LANES.items():
        out[n] = buf[4, a:b].reshape(like[n].shape)
    return out


_ORDER = ("norm_mix", "w_in", "conv_w", "a_log", "dt_bias", "dn_norm", "sinks", "w_o", "norm_mlp", "w_up", "w_down",
          "norm_ple", "w_ple_gate", "w_ple_proj", "norm_final")


def kernel(x, p, norm_mix, w_in, conv_w, a_log, dt_bias, dn_norm, sinks, w_o, norm_mlp, w_up, w_down, norm_ple, w_ple_gate, w_ple_proj, norm_final, loss_target, m_norm_mix, m_w_in, m_conv_w, m_a_log, m_dt_bias, m_dn_norm, m_sinks, m_w_o, m_norm_mlp, m_w_up, m_w_down, m_norm_ple, m_w_ple_gate, m_w_ple_proj, m_norm_final, v_norm_mix, v_w_in, v_conv_w, v_a_log, v_dt_bias, v_dn_norm, v_sinks, v_w_o, v_norm_mlp, v_w_up, v_w_down, v_norm_ple, v_w_ple_gate, v_w_ple_proj, v_norm_final):
    w = dict(norm_mix=norm_mix, w_in=w_in[0], conv_w=conv_w[0], a_log=a_log, dt_bias=dt_bias, dn_norm=dn_norm, sinks=sinks,
             w_o=w_o[0], norm_mlp=norm_mlp, w_up=w_up[0], w_down=w_down[0], norm_ple=norm_ple, w_ple_gate=w_ple_gate[0],
             w_ple_proj=w_ple_proj[0], norm_final=norm_final)
    m = dict(norm_mix=m_norm_mix, w_in=m_w_in[0], conv_w=m_conv_w[0], a_log=m_a_log, dt_bias=m_dt_bias, dn_norm=m_dn_norm,
             sinks=m_sinks, w_o=m_w_o[0], norm_mlp=m_norm_mlp, w_up=m_w_up[0], w_down=m_w_down[0], norm_ple=m_norm_ple,
             w_ple_gate=m_w_ple_gate[0], w_ple_proj=m_w_ple_proj[0], norm_final=m_norm_final)
    v = dict(norm_mix=v_norm_mix, w_in=v_w_in[0], conv_w=v_conv_w[0], a_log=v_a_log, dt_bias=v_dt_bias, dn_norm=v_dn_norm,
             sinks=v_sinks, w_o=v_w_o[0], norm_mlp=v_norm_mlp, w_up=v_w_up[0], w_down=v_w_down[0], norm_ple=v_norm_ple,
             w_ple_gate=v_w_ple_gate[0], w_ple_proj=v_w_ple_proj[0], norm_final=v_norm_final)
    me = 4 * lax.axis_index("x") + 2 * lax.axis_index("y") + lax.axis_index("c")
    conv_shard = conv_w.shape[2]

    for d in (w, m, v):
        d["w_in"] = d["w_in"].T
    conv_pad = jnp.pad(w["conv_w"], ((0, 8 - DN_CONV), (0, 256 - conv_shard)))
    first, token_first = _exchange_start([_bf(w["w_in"]), conv_pad], "gather_first_start", gather=True)
    later = [_bf(w[n]) for n in _OTHERS]
    later[-1] = _bf(w["w_ple_proj"] + token_first[0:1, 0:1])
    others, token_others = _exchange_start(later, "gather_others_start", gather=True)
    vectors = dict(w)
    vectors["norm_mix"] = w["norm_mix"] + token_others[0:1, 0:1]

    def first_weights(after):
        w_in_all, conv_all = _exchange_wait(first, after, "gather_first_wait")
        conv_all = jnp.transpose(conv_all[:, :DN_CONV, :conv_shard], (1, 0, 2)).reshape(DN_CONV, N_DEV * conv_shard)
        return _w_in_to_internal(w_in_all.reshape(D_IN, D_MODEL)), conv_all

    as_taken = {"w_o": lambda t: t.reshape(1024, 1024), "w_up": _cols_from_shards, "w_down": lambda t: t.reshape(4096, 1024),
                "w_ple_gate": lambda t: t.reshape(1024, 1024), "w_ple_proj": _cols_from_shards}

    def other_weights(names, after):
        which = [_OTHERS.index(n) for n in names]
        got = _exchange_wait(others, after, "gather_wait_" + names[0], which)
        return [as_taken[n](t) for n, t in zip(names, got)]

    shipped = []

    def ship_early(pieces):
        names = tuple(pieces)
        if names == ("w_in",):
            pieces = {"w_in": _bf(_w_in_from_internal(pieces["w_in"])).reshape(N_DEV, D_IN // N_DEV, D_MODEL)}
        handle, token = _exchange_start([pieces[n] for n in names], "scatter_start_" + names[0], gather=False)
        shipped.append((names, handle))
        return token

    loss, grad_x, g = _local_step(x[0], p[0, 0], loss_target[0], vectors, first_weights, other_weights, ship_early)

    small = _pack_small(g, [loss[:, :1] * jnp.ones((1, 1024), F32), jnp.zeros((2, 1024), F32),
                            g["conv_w"].reshape(6, 1024), jnp.zeros((2, 1024), F32)])
    small_all, = _exchange([small], "gather_small", gather=True)
    received = {}
    for names, handle in shipped:
        received.update(zip(names, _exchange_wait(handle, grad_x, "scatter_wait_" + names[0])))
    big = {n: _adamw(received[n], w[n], m[n], v[n], "adamw_" + n) for n in _MATRICES}
    zeros16 = jnp.zeros((_SMALL_ROWS, 1024), F32)
    summed = _adamw(small_all, zeros16, zeros16, zeros16, "sum_small")[0]
    conv_g = lax.dynamic_slice(summed[_CONV_ROW:_CONV_ROW + 6].reshape(DN_CONV, N_DEV * conv_shard), (0, me * conv_shard),
                               (DN_CONV, conv_shard))
    pad_conv = lambda t: jnp.pad(t.reshape(1, DN_CONV * conv_shard), ((0, 2), (0, 1024 - DN_CONV * conv_shard)))
    small_g = jnp.concatenate([summed[0:5], pad_conv(conv_g)], axis=0)[None]
    pack8 = lambda d: _pack_small(d, [pad_conv(d["conv_w"])])
    sm = _adamw(small_g, pack8(w), pack8(m), pack8(v), "adamw_vectors")

    outs = []
    for i, small_buf in enumerate(sm):
        d = {n: big[n][i] for n in _MATRICES}
        d.update(_unpack_small(small_buf, w))
        d["conv_w"] = small_buf[5, :DN_CONV * conv_shard].reshape(DN_CONV, conv_shard)
        outs.append(d)
    result = [summed[_LOSS_ROW, 0], grad_x[None]]
    for d in outs:
        d["w_in"] = d["w_in"].T
        for n in _ORDER:
            result.append(d[n][None] if n in _MATRICES or n == "conv_w" else d[n].reshape(w[n].shape))
    return tuple(result)
```

```python
import jax
import jax.numpy as jnp
from jax import lax
from jax.experimental import pallas as pl
from jax.experimental.pallas import tpu as pltpu

F32, BF16 = jnp.float32, jnp.bfloat16
EPS = 1e-6
D_MODEL = 1024
N_DEV = 8
ATTN_BLOCK = 128
HEAD_PAIR = 128
DN_HEADS = 4
DN_DIM = 128
DN_CHUNK = 64
DN_CONV = 4
ROPE_THETA = 10000.0
D_IN = 2824
D_IN_PAD = 3072
BLK_Q, BLK_Z = 0, 1
BLK_DN, BLK_K, BLK_V, BLK_G = 8, 20, 21, 22
BLK_KV, BLK_G_PAD = 10, 11
VMEM_LIMIT = 56 * 1024 * 1024
NEG = -1e30
ADAM_LR, ADAM_B1, ADAM_B2, ADAM_EPS, ADAM_WD, ADAM_STEP = 0.001, 0.9, 0.999, 1e-08, 0.01, 10
MESH = pl.DeviceIdType.MESH


def _bf(x):
    return x.astype(BF16)


def _dot(a, b):
    return jnp.dot(a, b, preferred_element_type=F32)


def _dot_nt(a, b):
    return lax.dot_general(a, b, (((1,), (1,)), ((), ())), preferred_element_type=F32)


def _dot_tn(a, b):
    return lax.dot_general(a, b, (((0,), (0,)), ((), ())), preferred_element_type=F32)


def _sigmoid(x):
    return 1.0 / (1.0 + jnp.exp(-x))


def _params(sem):
    return pltpu.CompilerParams(dimension_semantics=sem, vmem_limit_bytes=VMEM_LIMIT)


def _mm(x, w, *, form, name, out_dtypes, tn, epi=None, extra=(), tm=512, w_row_block=0, after=None, norm=None,
        norm_bwd=None):
    S, K = x.shape
    shards = w.ndim == 3
    N = (w.shape[2] * N_DEV if shards else w.shape[1]) if form == "nn" else w.shape[-2]
    assert not (shards and form == "nn" and tn != w.shape[2])
    r0 = w_row_block * K
    tm = min(tm, S)
    n_extra, n_out = len(extra), len(out_dtypes)
    tile = lambda width: pl.BlockSpec((tm, width), lambda i: (i, 0))
    whole = lambda a: pl.BlockSpec(a.shape, lambda i, nd=a.ndim: (0,) * nd)
    ins, in_specs = [x, w, *extra], [tile(K), whole(w)] + [tile(N)] * n_extra
    if norm is not None:
        ins, in_specs = ins + [norm], in_specs + [whole(norm)]
    if norm_bwd is not None:
        ins, in_specs = ins + list(norm_bwd), in_specs + [tile(N), whole(norm_bwd[1]), tile(N)]
    if after is not None:
        ins, in_specs = ins + [after], in_specs + [whole(after)]
    out_shape = [jax.ShapeDtypeStruct((S, N), dt) for dt in out_dtypes]
    out_specs = [tile(N)] * n_out
    if norm is not None:
        out_shape, out_specs = out_shape + [jax.ShapeDtypeStruct((S, K), BF16)], out_specs + [tile(K)]
    if norm_bwd is not None:
        out_shape, out_specs = out_shape + [jax.ShapeDtypeStruct((1, N), F32)], out_specs + [pl.BlockSpec((1, N), lambda i: (0, 0))]

    def product(xb, w_ref, cols, c):
        if form == "nn":
            return _dot(xb, w_ref[c] if shards else w_ref[r0:r0 + K, cols])
        if not shards:
            return _dot_nt(xb, w_ref[cols, :])
        ks = w.shape[2]
        acc = _dot_nt(xb[:, 0:ks], w_ref[0, cols, :])
        for s in range(1, N_DEV):
            acc = acc + _dot_nt(xb[:, s * ks:(s + 1) * ks], w_ref[s, cols, :])
        return acc

    def body(*refs):
        x_ref, w_ref = refs[0], refs[1]
        extra_refs = refs[2:2 + n_extra]
        at = 2 + n_extra
        if norm is not None:
            gain_ref, at = refs[at], at + 1
        if norm_bwd is not None:
            (y_ref, ygain_ref, dres_ref), at = refs[at:at + 3], at + 3
        outs = refs[len(ins):]
        if norm is not None:
            _, xh = _rms_stats(x_ref[...])
            xb = _bf(xh * gain_ref[...])
            outs[n_out][...] = xb
        else:
            xb = _bf(x_ref[...])
        for c in range(N // tn):
            cols = slice(c * tn, (c + 1) * tn)
            acc = product(xb, w_ref, cols, c)
            res = epi(acc, *[r[:, cols] for r in extra_refs]) if epi else (acc,)
            for o, r in zip(outs[:n_out], res):
                o[:, cols] = r.astype(o.dtype)
        if norm_bwd is not None:
            dx, dg = _rms_bwd_tile(y_ref[...], ygain_ref[...], outs[0][...])
            outs[0][...] = dres_ref[...] + dx
            dg_ref = outs[-1]

            @pl.when(pl.program_id(0) == 0)
            def _():
                dg_ref[...] = jnp.zeros_like(dg_ref)

            dg_ref[...] += dg

    return pl.pallas_call(
        body, grid=(S // tm,), name=name, in_specs=in_specs, out_specs=out_specs, out_shape=out_shape,
        compiler_params=_params(("arbitrary",) if norm_bwd is not None else ("parallel",)),
    )(*ins)


def _mm_tn(x, dy, *, name, tm, tn, out_dtype=F32, column_shards=False):
    S, K = x.shape
    N = dy.shape[1]

    def body(x_ref, dy_ref, o_ref):
        o_ref[...] = _dot_tn(_bf(x_ref[...]), _bf(dy_ref[...])).astype(out_dtype)

    if column_shards:
        out_spec = pl.BlockSpec((None, tm, tn), lambda i, j: (j, i, 0))
        out_shape = jax.ShapeDtypeStruct((N // tn, K, tn), out_dtype)
    else:
        out_spec = pl.BlockSpec((tm, tn), lambda i, j: (i, j))
        out_shape = jax.ShapeDtypeStruct((K, N), out_dtype)
    return pl.pallas_call(
        body, grid=(K // tm, N // tn), name=name,
        in_specs=[pl.BlockSpec((S, tm), lambda i, j: (0, i)), pl.BlockSpec((S, tn), lambda i, j: (0, j))],
        out_specs=out_spec, out_shape=out_shape,
        compiler_params=_params(("parallel", "parallel")),
    )(x, dy)


def _rowwise(body, *, tiled, full, out_tiled, out_acc, name, tm=512, smem=()):
    S = tiled[0].shape[0]
    tm = min(tm, S)
    n_in = len(smem) + len(tiled) + len(full)

    def kern(*refs):
        @pl.when(pl.program_id(0) == 0)
        def _():
            for r in refs[n_in + len(out_tiled):]:
                r[...] = jnp.zeros_like(r)
        body(*refs)

    in_specs = [pl.BlockSpec(memory_space=pltpu.SMEM) for _ in smem]
    in_specs += [pl.BlockSpec((tm, a.shape[1]), lambda i: (i, 0)) for a in tiled]
    in_specs += [pl.BlockSpec(a.shape, lambda i, nd=a.ndim: (0,) * nd) for a in full]
    out_specs = [pl.BlockSpec((tm, w), lambda i: (i, 0)) for w, _ in out_tiled]
    out_specs += [pl.BlockSpec(shp, lambda i, nd=len(shp): (0,) * nd) for shp, _ in out_acc]
    out_shape = [jax.ShapeDtypeStruct((S, w), dt) for w, dt in out_tiled]
    out_shape += [jax.ShapeDtypeStruct(shp, dt) for shp, dt in out_acc]
    return pl.pallas_call(
        kern, grid=(S // tm,), name=name, in_specs=in_specs, out_specs=out_specs, out_shape=out_shape,
        compiler_params=_params(("arbitrary",)),
    )(*smem, *tiled, *full)


def _rms_stats(x):
    r = lax.rsqrt(jnp.mean(x * x, axis=-1, keepdims=True) + EPS)
    return r, x * r


def _rmsnorm_fwd(x, g, name):
    def body(x_ref, g_ref, o_ref):
        _, xh = _rms_stats(x_ref[...])
        o_ref[...] = _bf(xh * g_ref[...])

    return _rowwise(body, tiled=[x], full=[g], out_tiled=[(x.shape[1], BF16)], out_acc=[], name=name)[0]


def _rms_bwd_tile(x, g, dxn):
    r, xh = _rms_stats(x)
    dg = jnp.sum(dxn * xh, axis=0, keepdims=True)
    dn = dxn * g
    dx = r * (dn - xh * jnp.mean(dn * xh, axis=-1, keepdims=True))
    return dx, dg


def _final_loss(h3, g, target):
    n = h3.shape[1]

    def body(h_ref, t_ref, g_ref, dh_ref, loss_ref, dg_ref):
        x = h_ref[...]
        _, xh = _rms_stats(x)
        e = xh * g_ref[...] - t_ref[...]
        per_tok = jnp.mean(e * e, axis=-1, keepdims=True)
        loss_ref[...] += 0.5 * jnp.sum(per_tok, axis=0, keepdims=True)
        dx, dg = _rms_bwd_tile(x, g_ref[...], e * (1.0 / n))
        dh_ref[...] = dx
        dg_ref[...] += dg

    return _rowwise(body, tiled=[h3, target], full=[g], out_tiled=[(n, F32)],
                    out_acc=[((1, 128), F32), ((1, n), F32)], name="final_loss")


def _ple_bwd(dh3, pp, gate):
    def body(dh_ref, pp_ref, gate_ref, dgl_ref, dpp_ref):
        dh, gt = dh_ref[...], gate_ref[...]
        dgl_ref[...] = _bf(dh * pp_ref[...] * gt * (1.0 - gt))
        dpp_ref[...] = _bf(dh * gt)

    n = dh3.shape[1]
    return _rowwise(body, tiled=[dh3, pp, gate], full=[], out_tiled=[(n, BF16), (n, BF16)], out_acc=[], name="ple_bwd")


def _rope_tables(S):
    half = 32
    inv = 1.0 / (ROPE_THETA ** (jnp.arange(half, dtype=F32) * (2.0 / 64)))
    ang = jnp.arange(S).astype(F32)[:, None] * inv[None, :]
    cos, sin = jnp.cos(ang), jnp.sin(ang)
    return jnp.tile(cos, (1, 4)), jnp.concatenate([-sin, sin, -sin, sin], axis=1)


def _attn_common(i, kc, kp, vc, vp, cc, sc, cp, sp):
    lane = lax.broadcasted_iota(jnp.int32, (1, HEAD_PAIR), 1)
    lane_lo = jnp.bitwise_and(lane, 63) < 32
    slot = [lane < 64, lane >= 64]

    def swap_halves(t):
        return jnp.where(lane_lo, pltpu.roll(t, 96, 1), pltpu.roll(t, 32, 1))

    def rope(t, cos, sin):
        return t * cos + swap_halves(t) * sin

    def unrope(d, cos, sin):
        return d * cos + swap_halves(d * sin)

    k2 = jnp.concatenate([rope(kp, cp, sp), rope(kc, cc, sc)], axis=0)
    v2 = jnp.concatenate([vp, vc], axis=0)
    r = lax.broadcasted_iota(jnp.int32, (ATTN_BLOCK, 2 * ATTN_BLOCK), 0)
    c = lax.broadcasted_iota(jnp.int32, (ATTN_BLOCK, 2 * ATTN_BLOCK), 1)
    valid = (c > r) & (c <= r + ATTN_BLOCK) & jnp.logical_or(c >= ATTN_BLOCK, i > 0)
    ks, vs = {}, {}
    for j in range(2):
        kn = jnp.where(slot[j], k2, 0.0)
        vn = jnp.where(slot[j], v2, 0.0)
        for s in range(2):
            ks[j, s] = _bf(kn if s == j else pltpu.roll(kn, 64, 1))
            vs[j, s] = _bf(vn if s == j else pltpu.roll(vn, 64, 1))
    return slot, rope, unrope, valid, ks, vs


def _attn_probs(scores, valid, sink):
    s = jnp.where(valid, scores * 0.125, NEG)
    m = jnp.maximum(jnp.max(s, axis=1, keepdims=True), sink)
    e = jnp.exp(s - m)
    inv_z = 1.0 / (jnp.sum(e, axis=1, keepdims=True) + jnp.exp(sink - m))
    return e * inv_z, jnp.exp(sink - m) * inv_z


def _attn_specs(S):
    nb = S // ATTN_BLOCK
    prev = lambda i: jnp.maximum(i - 1, 0)
    blk = lambda w, col, row=(lambda i: i): pl.BlockSpec((ATTN_BLOCK, w), lambda i: (row(i), col))
    in_specs = [pl.BlockSpec(memory_space=pltpu.SMEM),
                blk(512, BLK_Q), blk(128, BLK_K), blk(128, BLK_K, prev), blk(128, BLK_V), blk(128, BLK_V, prev),
                blk(128, 0), blk(128, 0), blk(128, 0, prev), blk(128, 0, prev)]
    return nb, in_specs


def _attn_fwd(pa, cos, sin, sinks):
    S = pa.shape[0]
    nb, in_specs = _attn_specs(S)

    def body(sinks_ref, q_ref, kc_ref, kp_ref, vc_ref, vp_ref, cc_ref, sc_ref, cp_ref, sp_ref, o_ref):
        i = pl.program_id(0)
        cc, sc = cc_ref[...], sc_ref[...]
        _, rope, _, valid, ks, vs = _attn_common(i, kc_ref[...], kp_ref[...], vc_ref[...], vp_ref[...],
                                                 cc, sc, cp_ref[...], sp_ref[...])
        pair_cols = [slice(HEAD_PAIR * pair, HEAD_PAIR * (pair + 1)) for pair in range(4)]
        qps = [_bf(rope(q_ref[:, cols], cc, sc)) for cols in pair_cols]
        outs = {}

        def head_program(h):
            pair, s = divmod(h, 2)
            j = h // 4
            scores = _dot_nt(qps[pair], ks[j, s])
            yield
            p, _ = _attn_probs(scores, valid, sinks_ref[h])
            outs[h] = _dot(_bf(p), vs[j, s])

        _interleave(head_program(h) for h in range(8))
        for pair, cols in enumerate(pair_cols):
            o_ref[:, cols] = outs[2 * pair] + outs[2 * pair + 1]

    return pl.pallas_call(
        body, grid=(nb,), name="attn_fwd", in_specs=in_specs,
        out_specs=pl.BlockSpec((ATTN_BLOCK, 512), lambda i: (i, 0)),
        out_shape=jax.ShapeDtypeStruct((S, 512), F32),
        compiler_params=_params(("parallel",)),
    )(sinks, pa, pa, pa, pa, pa, cos, sin, cos, sin)


def _attn_bwd(pa, cos, sin, sinks, dcat):
    S = pa.shape[0]
    nb, in_specs = _attn_specs(S)
    in_specs = in_specs + [pl.BlockSpec((ATTN_BLOCK, 512), lambda i: (i, 0))]

    def body(sinks_ref, q_ref, kc_ref, kp_ref, vc_ref, vp_ref, cc_ref, sc_ref, cp_ref, sp_ref, do_ref,
             dq_ref, dk_ref, dv_ref, dsink_ref):
        i = pl.program_id(0)

        @pl.when(i == 0)
        def _():
            dk_ref[...] = jnp.zeros_like(dk_ref)
            dv_ref[...] = jnp.zeros_like(dv_ref)
            dsink_ref[...] = jnp.zeros_like(dsink_ref)

        cc, sc, cp, sp = cc_ref[...], sc_ref[...], cp_ref[...], sp_ref[...]
        slot, rope, unrope, valid, ks, vs = _attn_common(i, kc_ref[...], kp_ref[...], vc_ref[...], vp_ref[...], cc, sc, cp, sp)
        pair_cols = [slice(HEAD_PAIR * pair, HEAD_PAIR * (pair + 1)) for pair in range(4)]
        qps = [_bf(rope(q_ref[:, cols], cc, sc)) for cols in pair_cols]
        dobs = [_bf(do_ref[:, cols]) for cols in pair_cols]
        dqs, dks, dvs = {}, {}, {}

        def head_program(h):
            pair, s = divmod(h, 2)
            j = h // 4
            qp, dob = qps[pair], dobs[pair]
            scores = _dot_nt(qp, ks[j, s])
            dp = _dot_nt(dob, vs[j, s])
            yield
            p, p_sink = _attn_probs(scores, valid, sinks_ref[h])
            dr = jnp.sum(p * dp, axis=1, keepdims=True)
            ds = _bf(p * (dp - dr) * 0.125)
            dsink_ref[h:h + 1, :] += -jnp.sum(p_sink * dr, axis=0, keepdims=True)
            dqs[h] = _dot(ds, ks[j, s])
            dk_h = _dot_tn(ds, qp)
            dv_h = _dot_tn(_bf(p), dob)
            yield
            dk_h, dv_h = jnp.where(slot[s], dk_h, 0.0), jnp.where(slot[s], dv_h, 0.0)
            if s != j:
                dk_h, dv_h = pltpu.roll(dk_h, 64, 1), pltpu.roll(dv_h, 64, 1)
            dks[h], dvs[h] = dk_h, dv_h

        _interleave(head_program(h) for h in range(8))
        dk2 = sum((dks[h] for h in range(1, 8)), dks[0])
        dv2 = sum((dvs[h] for h in range(1, 8)), dvs[0])
        for pair, cols in enumerate(pair_cols):
            dq_ref[:, cols] = _bf(unrope(dqs[2 * pair] + dqs[2 * pair + 1], cc, sc))
        cur = pl.ds(pl.multiple_of(i * ATTN_BLOCK, ATTN_BLOCK), ATTN_BLOCK)
        dk_ref[cur, :] += unrope(dk2[ATTN_BLOCK:], cc, sc)
        dv_ref[cur, :] += dv2[ATTN_BLOCK:]

        @pl.when(i > 0)
        def _():
            prv = pl.ds(pl.multiple_of((i - 1) * ATTN_BLOCK, ATTN_BLOCK), ATTN_BLOCK)
            dk_ref[prv, :] += unrope(dk2[:ATTN_BLOCK], cp, sp)
            dv_ref[prv, :] += dv2[:ATTN_BLOCK]

    whole = lambda w: pl.BlockSpec((S, w), lambda i: (0, 0))
    return pl.pallas_call(
        body, grid=(nb,), name="attn_bwd", in_specs=in_specs,
        out_specs=[pl.BlockSpec((ATTN_BLOCK, 512), lambda i: (i, BLK_Q)), whole(128), whole(128),
                   pl.BlockSpec((8, 128), lambda i: (0, 0))],
        out_shape=[jax.ShapeDtypeStruct((S, D_IN_PAD), BF16), jax.ShapeDtypeStruct((S, 128), F32),
                   jax.ShapeDtypeStruct((S, 128), F32), jax.ShapeDtypeStruct((8, 128), F32)],
        compiler_params=_params(("arbitrary",)),
    )(sinks, pa, pa, pa, pa, pa, cos, sin, cos, sin, dcat)


CONV_ROWS = 512
CONV_PAD = 8


def _conv_silu(scr, w, r0):
    y = w[3:4, :] * scr[pl.ds(CONV_PAD + r0, CONV_ROWS), :]
    for j in range(DN_CONV - 1):
        y = y + w[j:j + 1, :] * scr[pl.ds(CONV_PAD + r0 - 3 + j, CONV_ROWS), :]
    return y


def _dn_prep_fwd(pd, conv_w):
    S = pd.shape[0]
    assert S % CONV_ROWS == 0

    def body(x_ref, w_ref, o_ref, scr):
        b = pl.program_id(0)
        scr[0:CONV_PAD, :] = jnp.zeros((CONV_PAD, DN_DIM), F32)
        scr[pl.ds(CONV_PAD, S), :] = x_ref[...]
        w = w_ref[...]
        q_scale = jnp.where(b < DN_HEADS, DN_DIM ** -0.5, 1.0)
        for r0 in range(0, S, CONV_ROWS):
            y = _conv_silu(scr, w, r0)
            a = y * _sigmoid(y)
            rs = lax.rsqrt(jnp.sum(a * a, axis=1, keepdims=True) + EPS)
            o_ref[pl.ds(r0, CONV_ROWS), :] = a * jnp.where(b < 2 * DN_HEADS, rs * q_scale, 1.0)

    col = pl.BlockSpec((S, DN_DIM), lambda b: (0, b))
    return pl.pallas_call(
        body, grid=(3 * DN_HEADS,), name="dn_prep_fwd",
        in_specs=[pl.BlockSpec((S, DN_DIM), lambda b: (0, BLK_DN + b)), pl.BlockSpec((DN_CONV, DN_DIM), lambda b: (0, b))],
        out_specs=col,
        out_shape=jax.ShapeDtypeStruct((S, 3 * DN_HEADS * DN_DIM), F32),
        scratch_shapes=[pltpu.VMEM((S + CONV_PAD, DN_DIM), F32)],
        compiler_params=_params(("parallel",)),
    )(pd, conv_w)


def _dn_prep_bwd(pd, conv_w, dqkv, dproj):
    S = pd.shape[0]

    def body(x_ref, w_ref, d_ref, _, dx_ref, dw_ref, scr, dscr):
        b = pl.program_id(0)
        scr[0:CONV_PAD, :] = jnp.zeros((CONV_PAD, DN_DIM), F32)
        scr[pl.ds(CONV_PAD, S), :] = x_ref[...]
        dscr[pl.ds(S, CONV_PAD), :] = jnp.zeros((CONV_PAD, DN_DIM), F32)
        w = w_ref[...]
        q_scale = jnp.where(b < DN_HEADS, DN_DIM ** -0.5, 1.0)
        is_qk = b < 2 * DN_HEADS
        dw = [jnp.zeros((1, DN_DIM), F32) for _ in range(DN_CONV)]
        for r0 in range(0, S, CONV_ROWS):
            y = _conv_silu(scr, w, r0)
            sg = _sigmoid(y)
            a = y * sg
            dout = d_ref[pl.ds(r0, CONV_ROWS), :]
            rs = lax.rsqrt(jnp.sum(a * a, axis=1, keepdims=True) + EPS)
            da_qk = q_scale * rs * (dout - a * (rs * rs) * jnp.sum(dout * a, axis=1, keepdims=True))
            dy = jnp.where(is_qk, da_qk, dout) * (sg * (1.0 + y * (1.0 - sg)))
            dscr[pl.ds(r0, CONV_ROWS), :] = dy
            for j in range(DN_CONV):
                dw[j] = dw[j] + jnp.sum(dy * scr[pl.ds(CONV_PAD + r0 - 3 + j, CONV_ROWS), :], axis=0, keepdims=True)
        for j in range(DN_CONV):
            dw_ref[j:j + 1, :] = dw[j]
        for r0 in range(0, S, CONV_ROWS):
            dx = w[3:4, :] * dscr[pl.ds(r0, CONV_ROWS), :]
            for j in range(DN_CONV - 1):
                dx = dx + w[j:j + 1, :] * dscr[pl.ds(r0 + 3 - j, CONV_ROWS), :]
            dx_ref[pl.ds(r0, CONV_ROWS), :] = _bf(dx)

    col = pl.BlockSpec((S, DN_DIM), lambda b: (0, b))
    proj_col = pl.BlockSpec((S, DN_DIM), lambda b: (0, BLK_DN + b))
    wcol = pl.BlockSpec((DN_CONV, DN_DIM), lambda b: (0, b))
    return pl.pallas_call(
        body, grid=(3 * DN_HEADS,), name="dn_prep_bwd",
        in_specs=[proj_col, wcol, col, pl.BlockSpec(memory_space=pl.ANY)], out_specs=[proj_col, wcol],
        out_shape=[jax.ShapeDtypeStruct(dproj.shape, dproj.dtype), jax.ShapeDtypeStruct((DN_CONV, 3 * DN_HEADS * DN_DIM), F32)],
        scratch_shapes=[pltpu.VMEM((S + CONV_PAD, DN_DIM), F32), pltpu.VMEM((S + CONV_PAD, DN_DIM), F32)],
        input_output_aliases={3: 0},
        compiler_params=_params(("parallel",)),
    )(pd, conv_w, dqkv, dproj)


CPAD = 128
CHUNKS_LOCAL = 2
CHUNKS_SCAN = 4


def _chunk_masks():
    ii = lax.broadcasted_iota(jnp.int32, (DN_CHUNK, CPAD), 0)
    jj = lax.broadcasted_iota(jnp.int32, (DN_CHUNK, CPAD), 1)
    return ii, jj


def _rows_pad(a):
    return jnp.concatenate([a, jnp.zeros_like(a)], axis=0)


def _hi_lo(a):
    hi = _bf(a)
    return hi, _bf(a - hi.astype(F32))


def _double_step(t, p):
    C = DN_CHUNK
    th, tl = _hi_lo(t)
    ph, pl_ = _hi_lo(p)
    r1 = _dot(jnp.concatenate([th, tl, ph, pl_], axis=0), _rows_pad(ph))
    r2 = _dot(jnp.concatenate([th, ph], axis=0), _rows_pad(pl_))
    return t + (r1[:C] + r1[C:2 * C] + r2[:C]), r1[2 * C:3 * C] + r1[3 * C:] + r2[C:]


def _dot3_nt(a, b):
    C = DN_CHUNK
    ah, al = _hi_lo(a)
    bh, bl = _hi_lo(b)
    r1 = _dot_nt(jnp.concatenate([ah, al], axis=0), _rows_pad(bh))
    return r1[:C] + r1[C:] + _dot_nt(ah, _rows_pad(bl))


def _dot3_tn(a, b):
    C = DN_CHUNK
    ah, al = _hi_lo(a)
    bh, bl = _hi_lo(b)
    return _dot_tn(ah, bh)[:C] + _dot_tn(al, bh)[:C] + _dot_tn(ah, bl)[:C]


def _interleave(programs):
    programs = list(programs)
    while programs:
        alive = []
        for prog in programs:
            try:
                next(prog)
                alive.append(prog)
            except StopIteration:
                pass
        programs = alive


def _col_to_row(col, ii, jj):
    return jnp.sum(jnp.where(ii == jj, col, 0.0), axis=0, keepdims=True)


def _row_to_col(row, ii, jj):
    return jnp.sum(jnp.where(ii == jj, row, 0.0), axis=1, keepdims=True)


def _decay(gc_col, ii, jj):
    diff = gc_col - _col_to_row(gc_col, ii, jj)
    return jnp.where(jj <= ii, jnp.exp(jnp.where(jj <= ii, diff, 0.0)), 0.0)


def _softplus(x):
    return jnp.maximum(x, 0.0) + jnp.log(1.0 + jnp.exp(-jnp.abs(x)))


def _head(h):
    return slice(DN_DIM * h, DN_DIM * (h + 1))


def _dn_chunk_fwd(qkv, pg, a_log, dt_bias):
    S = qkv.shape[0]
    C = DN_CHUNK
    G = CHUNKS_LOCAL
    R = G * C
    steps = S // R

    def body(alog_ref, dtb_ref, qkv_ref, pg_ref, w_ref, u_ref, qg_ref, kd_ref, a_ref, t_ref, gcs_ref):
        ii, jj = _chunk_masks()
        lane = lax.broadcasted_iota(jnp.int32, (1, 128), 1)
        eye = (ii == jj).astype(F32)
        gcs_parts = [[] for _ in range(G)]

        def head_program(chunk, h):
            rows = slice(chunk * C, (chunk + 1) * C)
            q, k, v = qkv_ref[rows, _head(h)], qkv_ref[rows, _head(DN_HEADS + h)], qkv_ref[rows, _head(2 * DN_HEADS + h)]
            beta = _sigmoid(pg_ref[rows, h:h + 1])
            g_col = -jnp.exp(alog_ref[h]) * _softplus(pg_ref[rows, DN_HEADS + h:DN_HEADS + h + 1] + dtb_ref[h])
            g_row = _col_to_row(g_col, ii, jj)
            gc_col = jnp.sum(jnp.where(jj <= ii, g_row, 0.0), axis=1, keepdims=True)
            dec = _decay(gc_col, ii, jj)
            eg = jnp.exp(gc_col)
            kb, vb = k * beta, v * beta
            k_rows = _rows_pad(_bf(k))
            kk = _dot_nt(_bf(kb), k_rows)
            qk = _dot_nt(_bf(q), k_rows)
            yield
            t, pw = eye, -jnp.where(jj < ii, kk * dec, 0.0)
            for _ in range(6):
                t, pw = _double_step(t, pw)
                yield
            tb = _bf(t)
            u_ref[rows, _head(h)] = _dot(tb, _rows_pad(_bf(vb)))
            w_ref[rows, _head(h)] = _bf(_dot(tb, _rows_pad(_bf(kb * eg))))
            a_ref[h, rows] = _bf(qk * dec)
            t_ref[h, rows] = t
            qg_ref[rows, _head(h)] = _bf(q * eg)
            kd_ref[rows, _head(h)] = _bf(k * jnp.exp(gc_col[C - 1:C, :] - gc_col))
            gcs_parts[chunk].append(jnp.where(lane == h, gc_col, 0.0) + jnp.where(lane == DN_HEADS + h, beta, 0.0)
                                    + jnp.where(lane == 2 * DN_HEADS + h, g_col, 0.0))

        _interleave(head_program(chunk, h) for chunk in range(G) for h in range(DN_HEADS))
        for chunk in range(G):
            gcs_ref[chunk * C:(chunk + 1) * C, :] = sum(gcs_parts[chunk][1:], gcs_parts[chunk][0])

    smem = pl.BlockSpec(memory_space=pltpu.SMEM)
    wide = pl.BlockSpec((R, 512), lambda n: (n, 0))
    sq = pl.BlockSpec((DN_HEADS, R, CPAD), lambda n: (0, n, 0))
    narrow = pl.BlockSpec((R, 128), lambda n: (n, 0))
    f = lambda *shp: jax.ShapeDtypeStruct(shp, F32)
    b = lambda *shp: jax.ShapeDtypeStruct(shp, BF16)
    return pl.pallas_call(
        body, grid=(steps,), name="dn_chunk_fwd",
        in_specs=[smem, smem, pl.BlockSpec((R, 1536), lambda n: (n, 0)), pl.BlockSpec((R, 128), lambda n: (n, BLK_G))],
        out_specs=[wide, wide, wide, wide, sq, sq, narrow],
        out_shape=[b(S, 512), f(S, 512), b(S, 512), b(S, 512), b(DN_HEADS, S, CPAD), f(DN_HEADS, S, CPAD), f(S, 128)],
        compiler_params=_params(("parallel",)),
    )(a_log, dt_bias, qkv, pg)


def _gated_norm(o, z, gn):
    r, oh = _rms_stats(o)
    return oh * gn * (z * _sigmoid(z))


def _dn_scan_fwd(w, u, qg, kd, a, gcs, pz, gn):
    S = w.shape[0]
    C = DN_CHUNK
    nc = S // C
    G = CHUNKS_SCAN
    R = G * C

    def body(w_ref, u_ref, qg_ref, kd_ref, a_ref, gcs_ref, z_ref, gn_ref, o_ref, vn_ref, sst_ref, out_ref, state):
        @pl.when(pl.program_id(0) == 0)
        def _():
            state[...] = jnp.zeros_like(state)

        def head_program(chunk, h):
            hs = _head(h)
            rows = slice(chunk * C, (chunk + 1) * C)
            s_in = state[h]
            sst_ref[chunk, h] = s_in
            sb = _bf(s_in)
            w_s = _dot(w_ref[rows, hs], sb)
            q_s = _dot(qg_ref[rows, hs], sb)
            yield
            vn = u_ref[rows, hs] - w_s
            vnb = _bf(vn)
            o = q_s + _dot(a_ref[h, rows], _rows_pad(vnb))
            k_v = _dot_tn(kd_ref[rows, hs], vnb)
            yield
            state[h] = s_in * jnp.exp(gcs_ref[(chunk + 1) * C - 1:(chunk + 1) * C, h:h + 1]) + k_v
            o_ref[rows, hs] = o
            vn_ref[rows, hs] = vn
            out_ref[rows, hs] = _gated_norm(o, z_ref[rows, hs], gn_ref[...])

        for chunk in range(G):
            _interleave(head_program(chunk, h) for h in range(DN_HEADS))

    wide = pl.BlockSpec((R, 512), lambda n: (n, 0))
    f = lambda *shp: jax.ShapeDtypeStruct(shp, F32)
    return pl.pallas_call(
        body, grid=(nc // G,), name="dn_scan_fwd",
        in_specs=[wide, wide, wide, wide, pl.BlockSpec((DN_HEADS, R, CPAD), lambda n: (0, n, 0)),
                  pl.BlockSpec((R, 128), lambda n: (n, 0)), pl.BlockSpec((R, 512), lambda n: (n, BLK_Z)),
                  pl.BlockSpec((1, DN_DIM), lambda n: (0, 0))],
        out_specs=[wide, wide, pl.BlockSpec((G, DN_HEADS, DN_DIM, DN_DIM), lambda n: (n, 0, 0, 0)), wide],
        out_shape=[f(S, 512), f(S, 512), f(nc, DN_HEADS, DN_DIM, DN_DIM), f(S, 512)],
        scratch_shapes=[pltpu.VMEM((DN_HEADS, DN_DIM, DN_DIM), F32)],
        compiler_params=_params(("arbitrary",)),
    )(w, u, qg, kd, a, gcs, pz, gn)


def _dn_scan_bwd(dcat, o, pz, gn, sst, vnew, w, qg, kd, a, gcs, dproj):
    S = o.shape[0]
    C = DN_CHUNK
    G = CHUNKS_SCAN
    R = G * C
    steps = S // R

    def body(dy_ref, o_ref, z_ref, gn_ref, sst_ref, vn_ref, w_ref, qg_ref, kd_ref, a_ref, gcs_ref, _,
             du_ref, dw_ref, dqg_ref, dkd_ref, da_ref, dz_ref, dsc_ref, dgn_ref, dstate):
        @pl.when(pl.program_id(0) == 0)
        def _():
            dstate[...] = jnp.zeros_like(dstate)
            dgn_ref[...] = jnp.zeros_like(dgn_ref)

        gn_ = gn_ref[...]
        lane = lax.broadcasted_iota(jnp.int32, (C, 128), 1)
        row = lax.broadcasted_iota(jnp.int32, (C, 128), 0)
        dgn_parts = []

        def head_program(chunk, h, dsc_parts):
            hs = _head(h)
            rows = slice(chunk * C, (chunk + 1) * C)
            ov, z, dout = o_ref[rows, hs], z_ref[rows, hs], dy_ref[rows, hs]
            r, oh = _rms_stats(ov)
            sg = _sigmoid(z)
            don = dout * (z * sg)
            dz_ref[rows, hs] = _bf(dout * (oh * gn_) * (sg * (1.0 + z * (1.0 - sg))))
            dgn_parts.append(jnp.sum(don * oh, axis=0, keepdims=True))
            dn = don * gn_
            do = _bf(r * (dn - oh * jnp.mean(dn * oh, axis=-1, keepdims=True)))
            s_in = sst_ref[chunk, h]
            sb = _bf(s_in)
            ds_out = dstate[h]
            dsb = _bf(ds_out)
            vnb = _bf(vn_ref[rows, hs])
            wb, qgb, kdb, ab = w_ref[rows, hs], qg_ref[rows, hs], kd_ref[rows, hs], a_ref[h, rows]
            dvn = _dot_tn(ab, do)[:C] + _dot(kdb, dsb)
            da_ref[h, rows] = _dot_nt(do, _rows_pad(vnb))
            dqg_ref[rows, hs] = _dot_nt(do, sb)
            dkd_ref[rows, hs] = _dot_nt(vnb, dsb)
            q_do = _dot_tn(qgb, do)
            yield
            dvnb = _bf(dvn)
            dw_ref[rows, hs] = _bf(-_dot_nt(dvnb, sb))
            w_dvn = _dot_tn(wb, dvnb)
            du_ref[rows, hs] = dvnb
            yield
            d_last = jnp.exp(gcs_ref[(chunk + 1) * C - 1:(chunk + 1) * C, h:h + 1])
            dd = jnp.sum(jnp.sum(ds_out * s_in, axis=1, keepdims=True), axis=0, keepdims=True)
            dsc_parts.append(jnp.where((lane == h) & (row == C - 1), dd * d_last, 0.0))
            dstate[h] = ds_out * d_last + q_do - w_dvn

        for chunk in reversed(range(G)):
            dsc_parts = []
            _interleave(head_program(chunk, h, dsc_parts) for h in range(DN_HEADS))
            dsc_ref[chunk * C:(chunk + 1) * C, :] = sum(dsc_parts[1:], dsc_parts[0])
        dgn_ref[...] += sum(dgn_parts[1:], dgn_parts[0])

    rev = lambda n: steps - 1 - n
    wide = pl.BlockSpec((R, 512), lambda n: (rev(n), 0))
    z_spec = pl.BlockSpec((R, 512), lambda n: (rev(n), BLK_Z))
    sq = pl.BlockSpec((DN_HEADS, R, CPAD), lambda n: (0, rev(n), 0))
    narrow = pl.BlockSpec((R, 128), lambda n: (rev(n), 0))
    gn_spec = pl.BlockSpec((1, DN_DIM), lambda n: (0, 0))
    f = lambda *shp: jax.ShapeDtypeStruct(shp, F32)
    b = lambda *shp: jax.ShapeDtypeStruct(shp, BF16)
    return pl.pallas_call(
        body, grid=(steps,), name="dn_scan_bwd",
        in_specs=[pl.BlockSpec((R, 512), lambda n: (rev(n), 1)), wide, z_spec, gn_spec,
                  pl.BlockSpec((G, DN_HEADS, DN_DIM, DN_DIM), lambda n: (rev(n), 0, 0, 0)),
                  wide, wide, wide, wide, sq, narrow, pl.BlockSpec(memory_space=pl.ANY)],
        out_specs=[wide, wide, wide, wide, sq, z_spec, narrow, gn_spec],
        out_shape=[b(S, 512), b(S, 512), f(S, 512), f(S, 512), f(DN_HEADS, S, CPAD),
                   jax.ShapeDtypeStruct(dproj.shape, dproj.dtype), f(S, 128), f(1, DN_DIM)],
        scratch_shapes=[pltpu.VMEM((DN_HEADS, DN_DIM, DN_DIM), F32)],
        input_output_aliases={11: 5},
        compiler_params=_params(("arbitrary",)),
    )(dcat, o, pz, gn, sst, vnew, w, qg, kd, a, gcs, dproj)


def _dn_chunk_bwd(qkv, pg, t_inv, gcs, du, dw, dqg, dkd, da, dsc, a_log, dt_bias, dproj):
    S = qkv.shape[0]
    C = DN_CHUNK
    G = CHUNKS_LOCAL
    R = G * C

    def body(alog_ref, dtb_ref, qkv_ref, pg_ref, t_ref, gcs_ref, du_ref, dw_ref, dqg_ref, dkd_ref, da_ref, dsc_ref, _,
             dqkv_ref, dpg_ref, acc_ref):
        @pl.when(pl.program_id(0) == 0)
        def _():
            acc_ref[...] = jnp.zeros_like(acc_ref)

        ii, jj = _chunk_masks()
        lane = lax.broadcasted_iota(jnp.int32, (1, 128), 1)
        row8 = lax.broadcasted_iota(jnp.int32, (8, 128), 0)
        lane8 = lax.broadcasted_iota(jnp.int32, (8, 128), 1)
        rowc = lax.broadcasted_iota(jnp.int32, (C, 1), 0)
        tril, strict = jj <= ii, jj < ii
        dpg_parts, acc_parts = [[] for _ in range(G)], []

        def head_program(chunk, h):
            rows = slice(chunk * C, (chunk + 1) * C)
            q, k, v = qkv_ref[rows, _head(h)], qkv_ref[rows, _head(DN_HEADS + h)], qkv_ref[rows, _head(2 * DN_HEADS + h)]
            gc_col, beta, g_col = gcs_ref[rows, h:h + 1], gcs_ref[rows, DN_HEADS + h:DN_HEADS + h + 1], \
                gcs_ref[rows, 2 * DN_HEADS + h:2 * DN_HEADS + h + 1]
            dec = _decay(gc_col, ii, jj)
            eg = jnp.exp(gc_col)
            g_last = gc_col[C - 1:C, :]
            ek = jnp.exp(g_last - gc_col)
            kb, vb = k * beta, v * beta
            kbg = kb * eg
            qb, kbb = _bf(q), _bf(kb)
            k_rows = _rows_pad(_bf(k))
            t = t_ref[h, rows]
            tb = _bf(t)
            dub, dwb = du_ref[rows, _head(h)], dw_ref[rows, _head(h)]
            dqg_, dkd_ = dqg_ref[rows, _head(h)], dkd_ref[rows, _head(h)]
            dt = _dot_nt(dub, _rows_pad(_bf(vb))) + _dot_nt(dwb, _rows_pad(_bf(kbg)))
            dvb = _dot_tn(tb, dub)[:C]
            dkbg = _dot_tn(tb, dwb)[:C]
            kk = _dot_nt(kbb, k_rows)
            qk = _dot_nt(qb, k_rows)
            yield
            dt_t = _dot3_nt(dt, t)
            yield
            dl = -_dot3_tn(t, dt_t)
            yield
            dm = jnp.where(strict, dl * dec, 0.0)
            dqk = jnp.where(tril, da_ref[h, rows] * dec, 0.0)
            gmat = dm * kk + dqk * qk
            dgc = jnp.sum(gmat, axis=1, keepdims=True) - _row_to_col(jnp.sum(gmat, axis=0, keepdims=True), ii, jj)
            dmb, dqkb = _bf(dm), _bf(dqk)
            dkb = _dot(dmb, k_rows) + dkbg * eg
            dk = _dot_tn(dmb, kbb)[:C] + _dot_tn(dqkb, qb)[:C] + dkd_ * ek
            dq = _dot(dqkb, k_rows) + dqg_ * eg
            yield
            tk = jnp.sum(dkd_ * k * ek, axis=1, keepdims=True)
            dgc = dgc + jnp.sum(dqg_ * q * eg, axis=1, keepdims=True) - tk + jnp.sum(dkbg * kbg, axis=1, keepdims=True)
            dgl = jnp.sum(tk, axis=0, keepdims=True) + dsc_ref[(chunk + 1) * C - 1:(chunk + 1) * C, h:h + 1]
            dgc = dgc + jnp.where(rowc == C - 1, dgl, 0.0)
            dk = dk + dkb * beta
            dbeta = jnp.sum(dkb * k, axis=1, keepdims=True) + jnp.sum(dvb * v, axis=1, keepdims=True)
            dqkv_ref[rows, _head(h)] = dq
            dqkv_ref[rows, _head(DN_HEADS + h)] = dk
            dqkv_ref[rows, _head(2 * DN_HEADS + h)] = dvb * beta
            dg_col = jnp.sum(jnp.where(jj >= ii, _col_to_row(dgc, ii, jj), 0.0), axis=1, keepdims=True)
            db = dbeta * beta * (1.0 - beta)
            da_in = dg_col * (-jnp.exp(alog_ref[h])) * _sigmoid(pg_ref[rows, DN_HEADS + h:DN_HEADS + h + 1] + dtb_ref[h])
            dpg_parts[chunk].append(jnp.where(lane == h, db, 0.0) + jnp.where(lane == DN_HEADS + h, da_in, 0.0))
            acc_parts.append(jnp.where((row8 == 0) & (lane8 == h), jnp.sum(dg_col * g_col, axis=0, keepdims=True), 0.0)
                             + jnp.where((row8 == 1) & (lane8 == h), jnp.sum(da_in, axis=0, keepdims=True), 0.0))

        _interleave(head_program(chunk, h) for chunk in range(G) for h in range(DN_HEADS))
        for chunk in range(G):
            dpg = sum(dpg_parts[chunk][1:], dpg_parts[chunk][0])
            dpg_ref[chunk * C:(chunk + 1) * C, :] = _bf(jnp.concatenate([dpg, jnp.zeros_like(dpg)], axis=1))
        acc_ref[...] += sum(acc_parts[1:], acc_parts[0])

    smem = pl.BlockSpec(memory_space=pltpu.SMEM)
    wide = pl.BlockSpec((R, 512), lambda n: (n, 0))
    sq = pl.BlockSpec((DN_HEADS, R, CPAD), lambda n: (0, n, 0))
    narrow = pl.BlockSpec((R, 128), lambda n: (n, 0))
    qkv_spec = pl.BlockSpec((R, 1536), lambda n: (n, 0))
    f = lambda *shp: jax.ShapeDtypeStruct(shp, F32)
    return pl.pallas_call(
        body, grid=(S // R,), name="dn_chunk_bwd",
        in_specs=[smem, smem, qkv_spec, pl.BlockSpec((R, 128), lambda n: (n, BLK_G)), sq, narrow, wide, wide, wide, wide, sq,
                  narrow, pl.BlockSpec(memory_space=pl.ANY)],
        out_specs=[qkv_spec, pl.BlockSpec((R, 256), lambda n: (n, BLK_G_PAD)), pl.BlockSpec((8, 128), lambda n: (0, 0))],
        out_shape=[f(S, 1536), jax.ShapeDtypeStruct(dproj.shape, dproj.dtype), f(8, 128)],
        input_output_aliases={12: 1},
        compiler_params=_params(("arbitrary",)),
    )(a_log, dt_bias, qkv, pg, t_inv, gcs, du, dw, dqg, dkd, da, dsc, dproj)


def _fill_kv(dk, dv, dproj):
    S = dk.shape[0]
    tm = min(512, S)

    def body(dk_ref, dv_ref, _, o_ref):
        o_ref[...] = _bf(jnp.concatenate([dk_ref[...], dv_ref[...]], axis=1))

    tile = pl.BlockSpec((tm, 128), lambda i: (i, 0))
    return pl.pallas_call(
        body, grid=(S // tm,), name="fill_kv",
        in_specs=[tile, tile, pl.BlockSpec(memory_space=pl.ANY)],
        out_specs=pl.BlockSpec((tm, 256), lambda i: (i, BLK_KV)),
        out_shape=jax.ShapeDtypeStruct(dproj.shape, dproj.dtype),
        input_output_aliases={2: 0},
        compiler_params=_params(("parallel",)),
    )(dk, dv, dproj)


def _w_in_to_internal(wt):
    return jnp.concatenate([wt[0:512], wt[2304:2816], wt[768:2304], wt[512:768], wt[2816:2824],
                            jnp.zeros((D_IN_PAD - D_IN, wt.shape[1]), wt.dtype)], axis=0)


def _w_in_from_internal(gt):
    return jnp.concatenate([gt[0:512], gt[2560:2816], gt[1024:2560], gt[512:1024], gt[2816:2824]], axis=0)


def _local_step(x, p, target, wts, first_weights, other_weights, ship_early):
    S = x.shape[0]
    cos, sin = _rope_tables(S)
    sinks, a_log, dt_bias = wts["sinks"].reshape(8), wts["a_log"].reshape(4), wts["dt_bias"].reshape(4)
    gn = wts["dn_norm"].reshape(1, DN_DIM)
    add = lambda acc, res: (acc + res,)

    u = _rmsnorm_fwd(x, wts["norm_mix"], "norm_mix_fwd")
    w_in_t, conv_w = first_weights(u)
    proj, = _mm(u, w_in_t, form="nt", name="in_proj", out_dtypes=[F32], tn=512)
    attn = _attn_fwd(proj, cos, sin, sinks)
    qkv = _dn_prep_fwd(proj, conv_w)
    cw, cu, cqg, ckd, ca, ct, gcs = _dn_chunk_fwd(qkv, proj, a_log, dt_bias)
    o, vnew, sst, dn_out = _dn_scan_fwd(cw, cu, cqg, ckd, ca, gcs, proj, gn)
    w_o, = other_weights(("w_o",), dn_out)
    h1a, = _mm(attn, w_o, form="nn", name="out_proj_attn", out_dtypes=[F32], tn=512, epi=add, extra=[x], w_row_block=0)
    h1, = _mm(dn_out, w_o, form="nn", name="out_proj_dn", out_dtypes=[F32], tn=512, epi=add, extra=[h1a], w_row_block=1)

    def relu2(acc):
        r = jnp.maximum(acc, 0.0)
        return r * r, r

    w_up, = other_weights(("w_up",), h1)
    hid, relu, m = _mm(h1, w_up, form="nn", name="mlp_up", out_dtypes=[BF16, BF16], tn=512, epi=relu2, norm=wts["norm_mlp"])
    w_down, = other_weights(("w_down",), hid)
    h2, = _mm(hid, w_down, form="nn", name="mlp_down", out_dtypes=[F32], tn=512, epi=add, extra=[h1])
    w_pg, w_pp = other_weights(("w_ple_gate", "w_ple_proj"), h2)
    pp, = _mm(p, w_pp, form="nn", name="ple_proj", out_dtypes=[F32], tn=128)

    def ple(acc, h2_t, pp_t):
        gate = _sigmoid(acc)
        return h2_t + gate * pp_t, gate

    h3, gate, n3 = _mm(h2, w_pg, form="nn", name="ple_gate", out_dtypes=[F32, F32], tn=512, epi=ple, extra=[h2, pp],
                       norm=wts["norm_ple"])
    dh3, loss, d_norm_final = _final_loss(h3, wts["norm_final"].reshape(1, D_MODEL), target)

    g = {"norm_final": d_norm_final}
    dgl, dpp = _ple_bwd(dh3, pp, gate)
    early = {"w_ple_gate": _mm_tn(n3, dgl, name="d_w_ple_gate", tm=512, tn=1024, out_dtype=BF16).reshape(N_DEV, 128, 1024),
             "w_ple_proj": _mm_tn(p, dpp, name="d_w_ple_proj", tm=256, tn=128, out_dtype=BF16, column_shards=True)}
    dh2, g["norm_ple"] = _mm(dgl, w_pg, form="nt", name="d_n3", out_dtypes=[F32], tn=512,
                             norm_bwd=(h2, wts["norm_ple"], dh3))
    d_act, = _mm(dh2, w_down, form="nt", name="d_hidden", out_dtypes=[BF16], tn=512,
                 epi=lambda acc, r: (acc * (2.0 * r.astype(F32)),), extra=[relu])
    early["w_down"] = _mm_tn(hid, dh2, name="d_w_down", tm=512, tn=1024, out_dtype=BF16).reshape(N_DEV, 512, 1024)
    early["w_up"] = _mm_tn(m, d_act, name="d_w_up", tm=1024, tn=512, out_dtype=BF16, column_shards=True)
    token = ship_early(early)
    dh1, g["norm_mlp"] = _mm(d_act, w_up, form="nt", name="d_m", out_dtypes=[F32], tn=512, after=token,
                             norm_bwd=(h1, wts["norm_mlp"], dh2))
    dcat, = _mm(dh1, w_o, form="nt", name="d_cat", out_dtypes=[F32], tn=512)
    d_w_o = jnp.concatenate([_mm_tn(attn, dh1, name="d_w_o_attn", tm=512, tn=512, out_dtype=BF16),
                             _mm_tn(dn_out, dh1, name="d_w_o_dn", tm=512, tn=512, out_dtype=BF16)], axis=0)
    token = ship_early({"w_o": d_w_o.reshape(N_DEV, 128, 1024)})
    dproj, dk, dv, dsinks = _attn_bwd(proj, cos, sin, sinks + token[0, 0], dcat)
    g["sinks"] = dsinks[:, 0].reshape(1, 8)
    du_, dw_, dqg, dkd, da, dproj, dsc, g["dn_norm"] = _dn_scan_bwd(dcat, o, proj, gn, sst, vnew, cw, cqg, ckd, ca, gcs, dproj)
    dqkv, dproj, gate_acc = _dn_chunk_bwd(qkv, proj, ct, gcs, du_, dw_, dqg, dkd, da, dsc, a_log, dt_bias, dproj)
    g["a_log"], g["dt_bias"] = gate_acc[0:1, 0:4], gate_acc[1:2, 0:4]
    dproj, g["conv_w"] = _dn_prep_bwd(proj, conv_w, dqkv, dproj)
    dproj = _fill_kv(dk, dv, dproj)
    token = ship_early({"w_in": _mm_tn(dproj, u, name="d_w_in", tm=512, tn=1024)})
    grad_x, g["norm_mix"] = _mm(dproj, w_in_t, form="nn", name="d_u", out_dtypes=[F32], tn=512, after=token,
                                norm_bwd=(x, wts["norm_mix"], dh1))
    return loss, grad_x, g


def _peer(k):
    x, y, c = lax.axis_index("x"), lax.axis_index("y"), lax.axis_index("c")
    px = 1 - x if k & 4 else x
    py = 1 - y if k & 2 else y
    pc = 1 - c if k & 1 else c
    return (px, py, pc), 4 * px + 2 * py + pc


def _exchange(srcs, name, gather):
    n = len(srcs)
    gathers = list(gather) if isinstance(gather, (list, tuple)) else [gather] * n
    shapes = [(N_DEV,) + s.shape if gt else s.shape for s, gt in zip(srcs, gathers)]

    def body(*refs):
        src_refs, out_refs = refs[:n], refs[n:2 * n]
        send_sems, recv_sems, local_sems = refs[2 * n:]
        _, me = _peer(0)
        piece = lambda a, d: src_refs[a] if gathers[a] else src_refs[a].at[d]
        local = [pltpu.make_async_copy(piece(a, me), out_refs[a].at[me], local_sems.at[a]) for a in range(n)]
        for cp in local:
            cp.start()
        copies = []
        for a in range(n):
            for k in range(1, N_DEV):
                dev, idx = _peer(k)
                cp = pltpu.make_async_remote_copy(src_ref=piece(a, idx), dst_ref=out_refs[a].at[me],
                                                  send_sem=send_sems.at[a, k - 1], recv_sem=recv_sems.at[a, k - 1],
                                                  device_id=dev, device_id_type=MESH)
                cp.start()
                copies.append(cp)
        for cp in copies:
            cp.wait_recv()
        for cp in copies:
            cp.wait_send()
        for cp in local:
            cp.wait()

    anywhere = pl.BlockSpec(memory_space=pl.ANY)
    return pl.pallas_call(
        body, name=name, in_specs=[anywhere] * n, out_specs=[anywhere] * n,
        out_shape=[jax.ShapeDtypeStruct(shp, s.dtype) for shp, s in zip(shapes, srcs)],
        scratch_shapes=[pltpu.SemaphoreType.DMA((n, N_DEV - 1)), pltpu.SemaphoreType.DMA((n, N_DEV - 1)),
                        pltpu.SemaphoreType.DMA((n,))],
    )(*srcs)


_HBM = pl.BlockSpec(memory_space=pltpu.HBM)
_SEM = pl.BlockSpec(memory_space=pltpu.SEMAPHORE)
_EFFECT = pltpu.SideEffectType.DATAFLOW_SIDE_EFFECTING


def _split_copies(src_refs, land_refs, send_sems, recv_sems, gather, which=None):
    _, me = _peer(0)
    copies = []
    which = range(len(src_refs)) if which is None else which
    for a, src, land in zip(which, src_refs, land_refs):
        for k in range(1, N_DEV):
            dev, idx = _peer(k)
            sem = a * (N_DEV - 1) + k - 1
            copies.append(pltpu.make_async_remote_copy(
                src_ref=src if gather else src.at[idx], dst_ref=land.at[me], send_sem=send_sems.at[sem],
                recv_sem=recv_sems.at[sem], device_id=dev, device_id_type=MESH))
    return copies


def _exchange_start(srcs, name, gather):
    n = len(srcs)
    me = 4 * lax.axis_index("x") + 2 * lax.axis_index("y") + lax.axis_index("c")
    lands = []
    for s in srcs:
        own = s if gather else lax.dynamic_index_in_dim(s, me, 0, keepdims=False)
        shape = (N_DEV,) + s.shape if gather else s.shape
        lands.append(lax.dynamic_update_index_in_dim(lax.empty(shape, s.dtype), own, me, 0))

    def body(*refs):
        src_refs, land_refs = refs[:n], refs[n:2 * n]
        send_sems, recv_sems = refs[2 * n], refs[2 * n + 1]
        for cp in _split_copies(src_refs, land_refs, send_sems, recv_sems, gather):
            cp.start()
        refs[-1][...] = jnp.zeros_like(refs[-1])

    both = list(srcs) + lands
    sems = pltpu.SemaphoreType.DMA((n * (N_DEV - 1),))
    out = pl.pallas_call(
        body, name=name,
        out_shape=(sems, sems, *[pltpu.HBM(t.shape, t.dtype) for t in both], jax.ShapeDtypeStruct((8, 128), F32)),
        in_specs=[_HBM] * (2 * n), out_specs=(_SEM, _SEM, *[_HBM] * (2 * n), pl.BlockSpec(memory_space=pltpu.VMEM)),
        input_output_aliases={i: 2 + i for i in range(2 * n)},
        compiler_params=pltpu.CompilerParams(has_side_effects=_EFFECT),
    )(*[pltpu.with_memory_space_constraint(t, pltpu.HBM) for t in both])
    return (n, gather, out[:-1]), out[-1]


def _exchange_wait(handle, after, name, which=None):
    n_all, gather, (send_sems, recv_sems, *both_all) = handle
    which = list(range(n_all)) if which is None else list(which)
    n = len(which)
    both = [both_all[a] for a in which] + [both_all[n_all + a] for a in which]

    def body(*refs):
        src_refs, land_refs = refs[:n], refs[n:2 * n]
        for cp in _split_copies(src_refs, land_refs, refs[2 * n], refs[2 * n + 1], gather, which):
            cp.wait_send()
            cp.wait_recv()

    out = pl.pallas_call(
        body, name=name, out_shape=tuple(pltpu.HBM(t.shape, t.dtype) for t in both),
        in_specs=[_HBM] * (2 * n) + [_SEM, _SEM, pl.BlockSpec(memory_space=pl.ANY)], out_specs=tuple([_HBM] * (2 * n)),
        input_output_aliases={i: i for i in range(2 * n)},
        compiler_params=pltpu.CompilerParams(has_side_effects=_EFFECT),
    )(*both, send_sems, recv_sems, after)
    return list(out[n:])


def _adamw(parts, w, m, v, name):
    n, R, W = parts.shape
    tm = 128 if R % 128 == 0 else R

    def body(p_ref, w_ref, m_ref, v_ref, g_ref, d_ref, nm_ref, nv_ref):
        g = p_ref[0].astype(F32)
        for s in range(1, n):
            g = g + p_ref[s].astype(F32)
        nm = ADAM_B1 * m_ref[...] + (1.0 - ADAM_B1) * g
        nv = ADAM_B2 * v_ref[...] + (1.0 - ADAM_B2) * (g * g)
        m_hat = nm / (1.0 - ADAM_B1 ** ADAM_STEP)
        v_hat = nv / (1.0 - ADAM_B2 ** ADAM_STEP)
        g_ref[...] = g
        d_ref[...] = -ADAM_LR * (m_hat / (jnp.sqrt(v_hat) + ADAM_EPS) + ADAM_WD * w_ref[...])
        nm_ref[...] = nm
        nv_ref[...] = nv

    tile = pl.BlockSpec((tm, W), lambda i: (i, 0))
    return pl.pallas_call(
        body, grid=(R // tm,), name=name,
        in_specs=[pl.BlockSpec((n, tm, W), lambda i: (0, i, 0)), tile, tile, tile],
        out_specs=[tile] * 4, out_shape=[jax.ShapeDtypeStruct((R, W), F32)] * 4,
        compiler_params=_params(("parallel",)),
    )(parts, w, m, v)


_MATRICES = ("w_in", "w_o", "w_up", "w_down", "w_ple_gate", "w_ple_proj")


_OTHERS = ("w_o", "w_up", "w_down", "w_ple_gate", "w_ple_proj")


_SMALL_ROWS = 16
_VEC_ROW = {"norm_mix": 0, "norm_mlp": 1, "norm_ple": 2, "norm_final": 3}
_VEC_LANES = {"a_log": (0, 4), "dt_bias": (4, 8), "sinks": (8, 16), "dn_norm": (128, 256)}
_LOSS_ROW, _CONV_ROW = 5, 8


def _pack_small(vals, extra_rows):
    row4 = jnp.zeros((1024,), F32)
    for n, (a, b) in _VEC_LANES.items():
        row4 = row4.at[a:b].set(vals[n].reshape(b - a))
    rows = [vals[n].reshape(1, 1024) for n in ("norm_mix", "norm_mlp", "norm_ple", "norm_final")] + [row4.reshape(1, 1024)]
    return jnp.concatenate(rows + extra_rows, axis=0)


def _unpack_small(buf, like):
    out = {n: buf[r].reshape(like[n].shape) for n, r in _VEC_ROW.items()}
    for n, (a, b) in _VEC_LANES.items():
        out[n] = buf[4, a:b].reshape(like[n].shape)
    return out


_ORDER = ("norm_mix", "w_in", "conv_w", "a_log", "dt_bias", "dn_norm", "sinks", "w_o", "norm_mlp", "w_up", "w_down",
          "norm_ple", "w_ple_gate", "w_ple_proj", "norm_final")


def kernel(x, p, norm_mix, w_in, conv_w, a_log, dt_bias, dn_norm, sinks, w_o, norm_mlp, w_up, w_down, norm_ple, w_ple_gate, w_ple_proj, norm_final, loss_target, m_norm_mix, m_w_in, m_conv_w, m_a_log, m_dt_bias, m_dn_norm, m_sinks, m_w_o, m_norm_mlp, m_w_up, m_w_down, m_norm_ple, m_w_ple_gate, m_w_ple_proj, m_norm_final, v_norm_mix, v_w_in, v_conv_w, v_a_log, v_dt_bias, v_dn_norm, v_sinks, v_w_o, v_norm_mlp, v_w_up, v_w_down, v_norm_ple, v_w_ple_gate, v_w_ple_proj, v_norm_final):
    w = dict(norm_mix=norm_mix, w_in=w_in[0], conv_w=conv_w[0], a_log=a_log, dt_bias=dt_bias, dn_norm=dn_norm, sinks=sinks,
             w_o=w_o[0], norm_mlp=norm_mlp, w_up=w_up[0], w_down=w_down[0], norm_ple=norm_ple, w_ple_gate=w_ple_gate[0],
             w_ple_proj=w_ple_proj[0], norm_final=norm_final)
    m = dict(norm_mix=m_norm_mix, w_in=m_w_in[0], conv_w=m_conv_w[0], a_log=m_a_log, dt_bias=m_dt_bias, dn_norm=m_dn_norm,
             sinks=m_sinks, w_o=m_w_o[0], norm_mlp=m_norm_mlp, w_up=m_w_up[0], w_down=m_w_down[0], norm_ple=m_norm_ple,
             w_ple_gate=m_w_ple_gate[0], w_ple_proj=m_w_ple_proj[0], norm_final=m_norm_final)
    v = dict(norm_mix=v_norm_mix, w_in=v_w_in[0], conv_w=v_conv_w[0], a_log=v_a_log, dt_bias=v_dt_bias, dn_norm=v_dn_norm,
             sinks=v_sinks, w_o=v_w_o[0], norm_mlp=v_norm_mlp, w_up=v_w_up[0], w_down=v_w_down[0], norm_ple=v_norm_ple,
             w_ple_gate=v_w_ple_gate[0], w_ple_proj=v_w_ple_proj[0], norm_final=v_norm_final)
    me = 4 * lax.axis_index("x") + 2 * lax.axis_index("y") + lax.axis_index("c")
    conv_shard = conv_w.shape[2]

    for d in (w, m, v):
        d["w_in"] = d["w_in"].T
    conv_pad = jnp.pad(w["conv_w"], ((0, 8 - DN_CONV), (0, 256 - conv_shard)))
    first, token_first = _exchange_start([_bf(w["w_in"]), conv_pad], "gather_first_start", gather=True)
    later = [_bf(w[n]) for n in _OTHERS]
    later[-1] = _bf(w["w_ple_proj"] + token_first[0:1, 0:1])
    others, token_others = _exchange_start(later, "gather_others_start", gather=True)
    vectors = dict(w)
    vectors["norm_mix"] = w["norm_mix"] + token_others[0:1, 0:1]

    def first_weights(after):
        w_in_all, conv_all = _exchange_wait(first, after, "gather_first_wait")
        conv_all = jnp.transpose(conv_all[:, :DN_CONV, :conv_shard], (1, 0, 2)).reshape(DN_CONV, N_DEV * conv_shard)
        return _w_in_to_internal(w_in_all.reshape(D_IN, D_MODEL)), conv_all

    as_taken = {"w_o": lambda t: t.reshape(1024, 1024), "w_up": lambda t: t, "w_down": lambda t: t.reshape(4096, 1024),
                "w_ple_gate": lambda t: t.reshape(1024, 1024), "w_ple_proj": lambda t: t}

    def other_weights(names, after):
        which = [_OTHERS.index(n) for n in names]
        got = _exchange_wait(others, after, "gather_wait_" + names[0], which)
        return [as_taken[n](t) for n, t in zip(names, got)]

    shipped = []

    def ship_early(pieces):
        names = tuple(pieces)
        if names == ("w_in",):
            pieces = {"w_in": _bf(_w_in_from_internal(pieces["w_in"])).reshape(N_DEV, D_IN // N_DEV, D_MODEL)}
        handle, token = _exchange_start([pieces[n] for n in names], "scatter_start_" + names[0], gather=False)
        shipped.append((names, handle))
        return token

    loss, grad_x, g = _local_step(x[0], p[0, 0], loss_target[0], vectors, first_weights, other_weights, ship_early)

    small = _pack_small(g, [loss[:, :1] * jnp.ones((1, 1024), F32), jnp.zeros((2, 1024), F32),
                            g["conv_w"].reshape(6, 1024), jnp.zeros((2, 1024), F32)])
    small_all, = _exchange([small], "gather_small", gather=True)
    received = {}
    for names, handle in shipped:
        received.update(zip(names, _exchange_wait(handle, grad_x, "scatter_wait_" + names[0])))
    big = {n: _adamw(received[n], w[n], m[n], v[n], "adamw_" + n) for n in _MATRICES}
    zeros16 = jnp.zeros((_SMALL_ROWS, 1024), F32)
    summed = _adamw(small_all, zeros16, zeros16, zeros16, "sum_small")[0]
    conv_g = lax.dynamic_slice(summed[_CONV_ROW:_CONV_ROW + 6].reshape(DN_CONV, N_DEV * conv_shard), (0, me * conv_shard),
                               (DN_CONV, conv_shard))
    pad_conv = lambda t: jnp.pad(t.reshape(1, DN_CONV * conv_shard), ((0, 2), (0, 1024 - DN_CONV * conv_shard)))
    small_g = jnp.concatenate([summed[0:5], pad_conv(conv_g)], axis=0)[None]
    pack8 = lambda d: _pack_small(d, [pad_conv(d["conv_w"])])
    sm = _adamw(small_g, pack8(w), pack8(m), pack8(v), "adamw_vectors")

    outs = []
    for i, small_buf in enumerate(sm):
        d = {n: big[n][i] for n in _MATRICES}
        d.update(_unpack_small(small_buf, w))
        d["conv_w"] = small_buf[5, :DN_CONV * conv_shard].reshape(DN_CONV, conv_shard)
        outs.append(d)
    result = [summed[_LOSS_ROW, 0], grad_x[None]]
    for d in outs:
        d["w_in"] = d["w_in"].T
        for n in _ORDER:
            result.append(d[n][None] if n in _MATRICES or n == "conv_w" else d[n].reshape(w[n].shape))
    return tuple(result)
```

```python
import jax
import jax.numpy as jnp
import numpy as np
from jax import lax
from jax.experimental import pallas as pl
from jax.experimental.pallas import tpu as pltpu

F32, BF16 = jnp.float32, jnp.bfloat16
EPS = 1e-6
D_MODEL = 1024
N_DEV = 8
ATTN_BLOCK = 128
HEAD_PAIR = 128
DN_HEADS = 4
DN_DIM = 128
DN_CHUNK = 64
DN_CONV = 4
ROPE_THETA = 10000.0
D_IN = 2824
D_IN_PAD = 3072
BLK_Q, BLK_Z = 0, 1
BLK_DN, BLK_K, BLK_V, BLK_G = 8, 20, 21, 22
BLK_KV, BLK_G_PAD = 10, 11
VMEM_LIMIT = 56 * 1024 * 1024
NEG = -1e30
ADAM_LR, ADAM_B1, ADAM_B2, ADAM_EPS, ADAM_WD, ADAM_STEP = 0.001, 0.9, 0.999, 1e-08, 0.01, 10
MESH = pl.DeviceIdType.MESH


def _bf(x):
    return x.astype(BF16)


def _dot(a, b):
    return jnp.dot(a, b, preferred_element_type=F32)


def _dot_nt(a, b):
    return lax.dot_general(a, b, (((1,), (1,)), ((), ())), preferred_element_type=F32)


def _dot_tn(a, b):
    return lax.dot_general(a, b, (((0,), (0,)), ((), ())), preferred_element_type=F32)


def _sigmoid(x):
    return 1.0 / (1.0 + jnp.exp(-x))


def _params(sem):
    return pltpu.CompilerParams(dimension_semantics=sem, vmem_limit_bytes=VMEM_LIMIT)


def _mm(x, w, *, form, name, out_dtypes, tn, epi=None, extra=(), tm=512, w_row_block=0, after=None, norm=None,
        norm_bwd=None):
    xs = list(x) if isinstance(x, (list, tuple)) else [x]
    nx = len(xs)
    S, K = xs[0].shape
    shards = w.ndim == 3
    N = (w.shape[2] * N_DEV if shards else w.shape[1]) if form == "nn" else w.shape[-2]
    assert not (shards and form == "nn" and tn != w.shape[2]) and (nx == 1 or (form == "nn" and not shards and norm is None))
    r0 = w_row_block * K
    tm = min(tm, S)
    n_extra, n_out = len(extra), len(out_dtypes)
    tile = lambda width: pl.BlockSpec((tm, width), lambda i: (i, 0))
    whole = lambda a: pl.BlockSpec(a.shape, lambda i, nd=a.ndim: (0,) * nd)
    ins, in_specs = [*xs, w, *extra], [tile(K)] * nx + [whole(w)] + [tile(N)] * n_extra
    if norm is not None:
        ins, in_specs = ins + [norm], in_specs + [whole(norm)]
    if norm_bwd is not None:
        ins, in_specs = ins + list(norm_bwd), in_specs + [tile(N), whole(norm_bwd[1]), tile(N)]
    if after is not None:
        ins, in_specs = ins + [after], in_specs + [whole(after)]
    out_shape = [jax.ShapeDtypeStruct((S, N), dt) for dt in out_dtypes]
    out_specs = [tile(N)] * n_out
    if norm is not None:
        out_shape, out_specs = out_shape + [jax.ShapeDtypeStruct((S, K), BF16)], out_specs + [tile(K)]
    if norm_bwd is not None:
        out_shape, out_specs = out_shape + [jax.ShapeDtypeStruct((1, N), F32)], out_specs + [pl.BlockSpec((1, N), lambda i: (0, 0))]

    def product(xb, w_ref, cols, c):
        if form == "nn" and nx > 1:
            return sum(_dot(part, w_ref[r0 + p * K:r0 + (p + 1) * K, cols]) for p, part in enumerate(xb))
        if form == "nn":
            return _dot(xb, w_ref[c] if shards else w_ref[r0:r0 + K, cols])
        if not shards:
            return _dot_nt(xb, w_ref[cols, :])
        ks = w.shape[2]
        acc = _dot_nt(xb[:, 0:ks], w_ref[0, cols, :])
        for s in range(1, N_DEV):
            acc = acc + _dot_nt(xb[:, s * ks:(s + 1) * ks], w_ref[s, cols, :])
        return acc

    def body(*refs):
        x_ref, w_ref = refs[0], refs[nx]
        extra_refs = refs[nx + 1:nx + 1 + n_extra]
        at = nx + 1 + n_extra
        if norm is not None:
            gain_ref, at = refs[at], at + 1
        if norm_bwd is not None:
            (y_ref, ygain_ref, dres_ref), at = refs[at:at + 3], at + 3
        outs = refs[len(ins):]
        if norm is not None:
            _, xh = _rms_stats(x_ref[...])
            xb = _bf(xh * gain_ref[...])
            outs[n_out][...] = xb
        else:
            xb = _bf(x_ref[...]) if nx == 1 else [_bf(r[...]) for r in refs[:nx]]
        for c in range(N // tn):
            cols = slice(c * tn, (c + 1) * tn)
            acc = product(xb, w_ref, cols, c)
            res = epi(acc, *[r[:, cols] for r in extra_refs]) if epi else (acc,)
            for o, r in zip(outs[:n_out], res):
                o[:, cols] = r.astype(o.dtype)
        if norm_bwd is not None:
            dx, dg = _rms_bwd_tile(y_ref[...], ygain_ref[...], outs[0][...])
            outs[0][...] = dres_ref[...] + dx
            dg_ref = outs[-1]

            @pl.when(pl.program_id(0) == 0)
            def _():
                dg_ref[...] = jnp.zeros_like(dg_ref)

            dg_ref[...] += dg

    return pl.pallas_call(
        body, grid=(S // tm,), name=name, in_specs=in_specs, out_specs=out_specs, out_shape=out_shape,
        compiler_params=_params(("arbitrary",) if norm_bwd is not None else ("parallel",)),
    )(*ins)


def _mm_tn(x, dy, *, name, tm, tn, out_dtype=F32, column_shards=False):
    S, K = x.shape
    N = dy.shape[1]

    def body(x_ref, dy_ref, o_ref):
        o_ref[...] = _dot_tn(_bf(x_ref[...]), _bf(dy_ref[...])).astype(out_dtype)

    if column_shards:
        out_spec = pl.BlockSpec((None, tm, tn), lambda i, j: (j, i, 0))
        out_shape = jax.ShapeDtypeStruct((N // tn, K, tn), out_dtype)
    else:
        out_spec = pl.BlockSpec((tm, tn), lambda i, j: (i, j))
        out_shape = jax.ShapeDtypeStruct((K, N), out_dtype)
    return pl.pallas_call(
        body, grid=(K // tm, N // tn), name=name,
        in_specs=[pl.BlockSpec((S, tm), lambda i, j: (0, i)), pl.BlockSpec((S, tn), lambda i, j: (0, j))],
        out_specs=out_spec, out_shape=out_shape,
        compiler_params=_params(("parallel", "parallel")),
    )(x, dy)


def _rowwise(body, *, tiled, full, out_tiled, out_acc, name, tm=512, smem=()):
    S = tiled[0].shape[0]
    tm = min(tm, S)
    n_in = len(smem) + len(tiled) + len(full)

    def kern(*refs):
        @pl.when(pl.program_id(0) == 0)
        def _():
            for r in refs[n_in + len(out_tiled):]:
                r[...] = jnp.zeros_like(r)
        body(*refs)

    in_specs = [pl.BlockSpec(memory_space=pltpu.SMEM) for _ in smem]
    in_specs += [pl.BlockSpec((tm, a.shape[1]), lambda i: (i, 0)) for a in tiled]
    in_specs += [pl.BlockSpec(a.shape, lambda i, nd=a.ndim: (0,) * nd) for a in full]
    out_specs = [pl.BlockSpec((tm, w), lambda i: (i, 0)) for w, _ in out_tiled]
    out_specs += [pl.BlockSpec(shp, lambda i, nd=len(shp): (0,) * nd) for shp, _ in out_acc]
    out_shape = [jax.ShapeDtypeStruct((S, w), dt) for w, dt in out_tiled]
    out_shape += [jax.ShapeDtypeStruct(shp, dt) for shp, dt in out_acc]
    return pl.pallas_call(
        kern, grid=(S // tm,), name=name, in_specs=in_specs, out_specs=out_specs, out_shape=out_shape,
        compiler_params=_params(("arbitrary",)),
    )(*smem, *tiled, *full)


def _rms_stats(x):
    r = lax.rsqrt(jnp.mean(x * x, axis=-1, keepdims=True) + EPS)
    return r, x * r


def _rmsnorm_fwd(x, g, name):
    def body(x_ref, g_ref, o_ref):
        _, xh = _rms_stats(x_ref[...])
        o_ref[...] = _bf(xh * g_ref[...])

    return _rowwise(body, tiled=[x], full=[g], out_tiled=[(x.shape[1], BF16)], out_acc=[], name=name)[0]


def _rms_bwd_tile(x, g, dxn):
    r, xh = _rms_stats(x)
    dg = jnp.sum(dxn * xh, axis=0, keepdims=True)
    dn = dxn * g
    dx = r * (dn - xh * jnp.mean(dn * xh, axis=-1, keepdims=True))
    return dx, dg


def _final_loss(h3, g, target, pp, gate):
    n = h3.shape[1]

    def body(h_ref, t_ref, pp_ref, gate_ref, g_ref, dh_ref, dgl_ref, dpp_ref, loss_ref, dg_ref):
        x = h_ref[...]
        _, xh = _rms_stats(x)
        e = xh * g_ref[...] - t_ref[...]
        per_tok = jnp.mean(e * e, axis=-1, keepdims=True)
        loss_ref[...] += 0.5 * jnp.sum(per_tok, axis=0, keepdims=True)
        dh, dg = _rms_bwd_tile(x, g_ref[...], e * (1.0 / n))
        dh_ref[...] = dh
        dg_ref[...] += dg
        gt = gate_ref[...]
        dgl_ref[...] = _bf(dh * pp_ref[...] * gt * (1.0 - gt))
        dpp_ref[...] = _bf(dh * gt)

    return _rowwise(body, tiled=[h3, target, pp, gate], full=[g], out_tiled=[(n, F32), (n, BF16), (n, BF16)],
                    out_acc=[((1, 128), F32), ((1, n), F32)], name="final_loss")


def _rope_tables(S):
    half = 32
    inv = (1.0 / (np.float32(ROPE_THETA) ** (np.arange(half, dtype=np.float32) * np.float32(2.0 / 64)))).astype(np.float32)
    ang = np.arange(S).astype(np.float32)[:, None] * inv[None, :]
    cos, sin = np.cos(ang), np.sin(ang)
    return jnp.asarray(np.tile(cos, (1, 4))), jnp.asarray(np.concatenate([-sin, sin, -sin, sin], axis=1))


def _attn_common(i, kc, kp, vc, vp, cc, sc, cp, sp):
    lane = lax.broadcasted_iota(jnp.int32, (1, HEAD_PAIR), 1)
    lane_lo = jnp.bitwise_and(lane, 63) < 32
    slot = [lane < 64, lane >= 64]

    def swap_halves(t):
        return jnp.where(lane_lo, pltpu.roll(t, 96, 1), pltpu.roll(t, 32, 1))

    def rope(t, cos, sin):
        return t * cos + swap_halves(t) * sin

    def unrope(d, cos, sin):
        return d * cos + swap_halves(d * sin)

    k2 = jnp.concatenate([rope(kp, cp, sp), rope(kc, cc, sc)], axis=0)
    v2 = jnp.concatenate([vp, vc], axis=0)
    r = lax.broadcasted_iota(jnp.int32, (ATTN_BLOCK, 2 * ATTN_BLOCK), 0)
    c = lax.broadcasted_iota(jnp.int32, (ATTN_BLOCK, 2 * ATTN_BLOCK), 1)
    valid = (c > r) & (c <= r + ATTN_BLOCK) & jnp.logical_or(c >= ATTN_BLOCK, i > 0)
    ks, vs = {}, {}
    for j in range(2):
        kn = jnp.where(slot[j], k2, 0.0)
        vn = jnp.where(slot[j], v2, 0.0)
        for s in range(2):
            ks[j, s] = _bf(kn if s == j else pltpu.roll(kn, 64, 1))
            vs[j, s] = _bf(vn if s == j else pltpu.roll(vn, 64, 1))
    return slot, rope, unrope, valid, ks, vs


def _attn_probs(scores, valid, sink):
    s = jnp.where(valid, scores * 0.125, NEG)
    m = jnp.maximum(jnp.max(s, axis=1, keepdims=True), sink)
    e = jnp.exp(s - m)
    inv_z = 1.0 / (jnp.sum(e, axis=1, keepdims=True) + jnp.exp(sink - m))
    return e * inv_z, jnp.exp(sink - m) * inv_z


def _attn_specs(S):
    nb = S // ATTN_BLOCK
    prev = lambda i: jnp.maximum(i - 1, 0)
    blk = lambda w, col, row=(lambda i: i): pl.BlockSpec((ATTN_BLOCK, w), lambda i: (row(i), col))
    in_specs = [pl.BlockSpec(memory_space=pltpu.SMEM),
                blk(512, BLK_Q), blk(128, BLK_K), blk(128, BLK_K, prev), blk(128, BLK_V), blk(128, BLK_V, prev),
                blk(128, 0), blk(128, 0), blk(128, 0, prev), blk(128, 0, prev)]
    return nb, in_specs


def _attn_fwd(pa, cos, sin, sinks):
    S = pa.shape[0]
    nb, in_specs = _attn_specs(S)

    def body(sinks_ref, q_ref, kc_ref, kp_ref, vc_ref, vp_ref, cc_ref, sc_ref, cp_ref, sp_ref, o_ref):
        i = pl.program_id(0)
        cc, sc = cc_ref[...], sc_ref[...]
        _, rope, _, valid, ks, vs = _attn_common(i, kc_ref[...], kp_ref[...], vc_ref[...], vp_ref[...],
                                                 cc, sc, cp_ref[...], sp_ref[...])
        pair_cols = [slice(HEAD_PAIR * pair, HEAD_PAIR * (pair + 1)) for pair in range(4)]
        qps = [_bf(rope(q_ref[:, cols], cc, sc)) for cols in pair_cols]
        outs = {}

        def head_program(h):
            pair, s = divmod(h, 2)
            j = h // 4
            scores = _dot_nt(qps[pair], ks[j, s])
            yield
            p, _ = _attn_probs(scores, valid, sinks_ref[h])
            outs[h] = _dot(_bf(p), vs[j, s])

        _interleave(head_program(h) for h in range(8))
        for pair, cols in enumerate(pair_cols):
            o_ref[:, cols] = outs[2 * pair] + outs[2 * pair + 1]

    return pl.pallas_call(
        body, grid=(nb,), name="attn_fwd", in_specs=in_specs,
        out_specs=pl.BlockSpec((ATTN_BLOCK, 512), lambda i: (i, 0)),
        out_shape=jax.ShapeDtypeStruct((S, 512), F32),
        compiler_params=_params(("parallel",)),
    )(sinks, pa, pa, pa, pa, pa, cos, sin, cos, sin)


def _attn_bwd(pa, cos, sin, sinks, dcat):
    S = pa.shape[0]
    nb, in_specs = _attn_specs(S)
    in_specs = in_specs + [pl.BlockSpec((ATTN_BLOCK, 512), lambda i: (i, 0))]

    def body(sinks_ref, q_ref, kc_ref, kp_ref, vc_ref, vp_ref, cc_ref, sc_ref, cp_ref, sp_ref, do_ref,
             dq_ref, dk_ref, dv_ref, dsink_ref):
        i = pl.program_id(0)

        @pl.when(i == 0)
        def _():
            dk_ref[...] = jnp.zeros_like(dk_ref)
            dv_ref[...] = jnp.zeros_like(dv_ref)
            dsink_ref[...] = jnp.zeros_like(dsink_ref)

        cc, sc, cp, sp = cc_ref[...], sc_ref[...], cp_ref[...], sp_ref[...]
        slot, rope, unrope, valid, ks, vs = _attn_common(i, kc_ref[...], kp_ref[...], vc_ref[...], vp_ref[...], cc, sc, cp, sp)
        pair_cols = [slice(HEAD_PAIR * pair, HEAD_PAIR * (pair + 1)) for pair in range(4)]
        qps = [_bf(rope(q_ref[:, cols], cc, sc)) for cols in pair_cols]
        dobs = [_bf(do_ref[:, cols]) for cols in pair_cols]
        dqs, dks, dvs = {}, {}, {}

        def head_program(h):
            pair, s = divmod(h, 2)
            j = h // 4
            qp, dob = qps[pair], dobs[pair]
            scores = _dot_nt(qp, ks[j, s])
            dp = _dot_nt(dob, vs[j, s])
            yield
            p, p_sink = _attn_probs(scores, valid, sinks_ref[h])
            dr = jnp.sum(p * dp, axis=1, keepdims=True)
            ds = _bf(p * (dp - dr) * 0.125)
            dsink_ref[h:h + 1, :] += -jnp.sum(p_sink * dr, axis=0, keepdims=True)
            dqs[h] = _dot(ds, ks[j, s])
            dk_h = _dot_tn(ds, qp)
            dv_h = _dot_tn(_bf(p), dob)
            yield
            dk_h, dv_h = jnp.where(slot[s], dk_h, 0.0), jnp.where(slot[s], dv_h, 0.0)
            if s != j:
                dk_h, dv_h = pltpu.roll(dk_h, 64, 1), pltpu.roll(dv_h, 64, 1)
            dks[h], dvs[h] = dk_h, dv_h

        _interleave(head_program(h) for h in range(8))
        dk2 = sum((dks[h] for h in range(1, 8)), dks[0])
        dv2 = sum((dvs[h] for h in range(1, 8)), dvs[0])
        for pair, cols in enumerate(pair_cols):
            dq_ref[:, cols] = _bf(unrope(dqs[2 * pair] + dqs[2 * pair + 1], cc, sc))
        cur = pl.ds(pl.multiple_of(i * ATTN_BLOCK, ATTN_BLOCK), ATTN_BLOCK)
        dk_ref[cur, :] += unrope(dk2[ATTN_BLOCK:], cc, sc)
        dv_ref[cur, :] += dv2[ATTN_BLOCK:]

        @pl.when(i > 0)
        def _():
            prv = pl.ds(pl.multiple_of((i - 1) * ATTN_BLOCK, ATTN_BLOCK), ATTN_BLOCK)
            dk_ref[prv, :] += unrope(dk2[:ATTN_BLOCK], cp, sp)
            dv_ref[prv, :] += dv2[:ATTN_BLOCK]

    whole = lambda w: pl.BlockSpec((S, w), lambda i: (0, 0))
    return pl.pallas_call(
        body, grid=(nb,), name="attn_bwd", in_specs=in_specs,
        out_specs=[pl.BlockSpec((ATTN_BLOCK, 512), lambda i: (i, BLK_Q)), whole(128), whole(128),
                   pl.BlockSpec((8, 128), lambda i: (0, 0))],
        out_shape=[jax.ShapeDtypeStruct((S, D_IN_PAD), BF16), jax.ShapeDtypeStruct((S, 128), F32),
                   jax.ShapeDtypeStruct((S, 128), F32), jax.ShapeDtypeStruct((8, 128), F32)],
        compiler_params=_params(("arbitrary",)),
    )(sinks, pa, pa, pa, pa, pa, cos, sin, cos, sin, dcat)


CONV_ROWS = 512
CONV_PAD = 8


def _conv_silu(scr, w, r0):
    y = w[3:4, :] * scr[pl.ds(CONV_PAD + r0, CONV_ROWS), :]
    for j in range(DN_CONV - 1):
        y = y + w[j:j + 1, :] * scr[pl.ds(CONV_PAD + r0 - 3 + j, CONV_ROWS), :]
    return y


def _dn_prep_fwd(pd, conv_w):
    S = pd.shape[0]
    assert S % CONV_ROWS == 0

    def body(x_ref, w_ref, o_ref, scr):
        b = pl.program_id(0)
        scr[0:CONV_PAD, :] = jnp.zeros((CONV_PAD, DN_DIM), F32)
        scr[pl.ds(CONV_PAD, S), :] = x_ref[...]
        w = w_ref[...]
        q_scale = jnp.where(b < DN_HEADS, DN_DIM ** -0.5, 1.0)
        for r0 in range(0, S, CONV_ROWS):
            y = _conv_silu(scr, w, r0)
            a = y * _sigmoid(y)
            rs = lax.rsqrt(jnp.sum(a * a, axis=1, keepdims=True) + EPS)
            o_ref[pl.ds(r0, CONV_ROWS), :] = a * jnp.where(b < 2 * DN_HEADS, rs * q_scale, 1.0)

    col = pl.BlockSpec((S, DN_DIM), lambda b: (0, b))
    return pl.pallas_call(
        body, grid=(3 * DN_HEADS,), name="dn_prep_fwd",
        in_specs=[pl.BlockSpec((S, DN_DIM), lambda b: (0, BLK_DN + b)), pl.BlockSpec((DN_CONV, DN_DIM), lambda b: (0, b))],
        out_specs=col,
        out_shape=jax.ShapeDtypeStruct((S, 3 * DN_HEADS * DN_DIM), F32),
        scratch_shapes=[pltpu.VMEM((S + CONV_PAD, DN_DIM), F32)],
        compiler_params=_params(("parallel",)),
    )(pd, conv_w)


def _dn_prep_bwd(pd, conv_w, dqkv, dproj):
    S = pd.shape[0]

    def body(x_ref, w_ref, d_ref, _, dx_ref, dw_ref, scr, dscr):
        b = pl.program_id(0)
        scr[0:CONV_PAD, :] = jnp.zeros((CONV_PAD, DN_DIM), F32)
        scr[pl.ds(CONV_PAD, S), :] = x_ref[...]
        dscr[pl.ds(S, CONV_PAD), :] = jnp.zeros((CONV_PAD, DN_DIM), F32)
        w = w_ref[...]
        q_scale = jnp.where(b < DN_HEADS, DN_DIM ** -0.5, 1.0)
        is_qk = b < 2 * DN_HEADS
        dw = [jnp.zeros((1, DN_DIM), F32) for _ in range(DN_CONV)]
        for r0 in range(0, S, CONV_ROWS):
            y = _conv_silu(scr, w, r0)
            sg = _sigmoid(y)
            a = y * sg
            dout = d_ref[pl.ds(r0, CONV_ROWS), :]
            rs = lax.rsqrt(jnp.sum(a * a, axis=1, keepdims=True) + EPS)
            da_qk = q_scale * rs * (dout - a * (rs * rs) * jnp.sum(dout * a, axis=1, keepdims=True))
            dy = jnp.where(is_qk, da_qk, dout) * (sg * (1.0 + y * (1.0 - sg)))
            dscr[pl.ds(r0, CONV_ROWS), :] = dy
            for j in range(DN_CONV):
                dw[j] = dw[j] + jnp.sum(dy * scr[pl.ds(CONV_PAD + r0 - 3 + j, CONV_ROWS), :], axis=0, keepdims=True)
        for j in range(DN_CONV):
            dw_ref[j:j + 1, :] = dw[j]
        for r0 in range(0, S, CONV_ROWS):
            dx = w[3:4, :] * dscr[pl.ds(r0, CONV_ROWS), :]
            for j in range(DN_CONV - 1):
                dx = dx + w[j:j + 1, :] * dscr[pl.ds(r0 + 3 - j, CONV_ROWS), :]
            dx_ref[pl.ds(r0, CONV_ROWS), :] = _bf(dx)

    col = pl.BlockSpec((S, DN_DIM), lambda b: (0, b))
    proj_col = pl.BlockSpec((S, DN_DIM), lambda b: (0, BLK_DN + b))
    wcol = pl.BlockSpec((DN_CONV, DN_DIM), lambda b: (0, b))
    return pl.pallas_call(
        body, grid=(3 * DN_HEADS,), name="dn_prep_bwd",
        in_specs=[proj_col, wcol, col, pl.BlockSpec(memory_space=pl.ANY)], out_specs=[proj_col, wcol],
        out_shape=[jax.ShapeDtypeStruct(dproj.shape, dproj.dtype), jax.ShapeDtypeStruct((DN_CONV, 3 * DN_HEADS * DN_DIM), F32)],
        scratch_shapes=[pltpu.VMEM((S + CONV_PAD, DN_DIM), F32), pltpu.VMEM((S + CONV_PAD, DN_DIM), F32)],
        input_output_aliases={3: 0},
        compiler_params=_params(("parallel",)),
    )(pd, conv_w, dqkv, dproj)


CPAD = 128
CHUNKS_LOCAL = 2
CHUNKS_SCAN = 4


def _chunk_masks():
    ii = lax.broadcasted_iota(jnp.int32, (DN_CHUNK, CPAD), 0)
    jj = lax.broadcasted_iota(jnp.int32, (DN_CHUNK, CPAD), 1)
    return ii, jj


def _rows_pad(a):
    return jnp.concatenate([a, jnp.zeros_like(a)], axis=0)


def _hi_lo(a):
    hi = _bf(a)
    return hi, _bf(a - hi.astype(F32))


def _double_step(t, p):
    C = DN_CHUNK
    th, tl = _hi_lo(t)
    ph, pl_ = _hi_lo(p)
    r1 = _dot(jnp.concatenate([th, tl, ph, pl_], axis=0), _rows_pad(ph))
    r2 = _dot(jnp.concatenate([th, ph], axis=0), _rows_pad(pl_))
    return t + (r1[:C] + r1[C:2 * C] + r2[:C]), r1[2 * C:3 * C] + r1[3 * C:] + r2[C:]


def _dot3_nt(a, b):
    C = DN_CHUNK
    ah, al = _hi_lo(a)
    bh, bl = _hi_lo(b)
    r1 = _dot_nt(jnp.concatenate([ah, al], axis=0), _rows_pad(bh))
    return r1[:C] + r1[C:] + _dot_nt(ah, _rows_pad(bl))


def _dot3_tn(a, b):
    C = DN_CHUNK
    ah, al = _hi_lo(a)
    bh, bl = _hi_lo(b)
    return _dot_tn(jnp.concatenate([ah, al, ah], axis=0), jnp.concatenate([bh, bh, bl], axis=0))[:C]


def _interleave(programs):
    programs = list(programs)
    while programs:
        alive = []
        for prog in programs:
            try:
                next(prog)
                alive.append(prog)
            except StopIteration:
                pass
        programs = alive


def _col_to_row(col, ii, jj):
    return jnp.sum(jnp.where(ii == jj, col, 0.0), axis=0, keepdims=True)


def _row_to_col(row, ii, jj):
    return jnp.sum(jnp.where(ii == jj, row, 0.0), axis=1, keepdims=True)


def _decay(gc_col, ii, jj):
    diff = gc_col - _col_to_row(gc_col, ii, jj)
    return jnp.where(jj <= ii, jnp.exp(jnp.where(jj <= ii, diff, 0.0)), 0.0)


def _softplus(x):
    return jnp.maximum(x, 0.0) + jnp.log(1.0 + jnp.exp(-jnp.abs(x)))


def _head(h):
    return slice(DN_DIM * h, DN_DIM * (h + 1))


def _dn_chunk_fwd(qkv, pg, a_log, dt_bias):
    S = qkv.shape[0]
    C = DN_CHUNK
    G = CHUNKS_LOCAL
    R = G * C
    steps = S // R

    def body(alog_ref, dtb_ref, qkv_ref, pg_ref, w_ref, u_ref, qg_ref, kd_ref, a_ref, t_ref, gcs_ref):
        ii, jj = _chunk_masks()
        lane = lax.broadcasted_iota(jnp.int32, (1, 128), 1)
        eye = (ii == jj).astype(F32)
        gcs_parts = [[] for _ in range(G)]

        def head_program(chunk, h):
            rows = slice(chunk * C, (chunk + 1) * C)
            q, k, v = qkv_ref[rows, _head(h)], qkv_ref[rows, _head(DN_HEADS + h)], qkv_ref[rows, _head(2 * DN_HEADS + h)]
            beta = _sigmoid(pg_ref[rows, h:h + 1])
            g_col = -jnp.exp(alog_ref[h]) * _softplus(pg_ref[rows, DN_HEADS + h:DN_HEADS + h + 1] + dtb_ref[h])
            g_row = _col_to_row(g_col, ii, jj)
            gc_col = jnp.sum(jnp.where(jj <= ii, g_row, 0.0), axis=1, keepdims=True)
            dec = _decay(gc_col, ii, jj)
            eg = jnp.exp(gc_col)
            kb, vb = k * beta, v * beta
            k_rows = _rows_pad(_bf(k))
            kk = _dot_nt(_bf(kb), k_rows)
            qk = _dot_nt(_bf(q), k_rows)
            yield
            t, pw = eye, -jnp.where(jj < ii, kk * dec, 0.0)
            for _ in range(6):
                t, pw = _double_step(t, pw)
                yield
            tb = _bf(t)
            u_ref[rows, _head(h)] = _dot(tb, _rows_pad(_bf(vb)))
            w_ref[rows, _head(h)] = _bf(_dot(tb, _rows_pad(_bf(kb * eg))))
            a_ref[h, rows] = _bf(qk * dec)
            t_ref[h, rows] = t
            qg_ref[rows, _head(h)] = _bf(q * eg)
            kd_ref[rows, _head(h)] = _bf(k * jnp.exp(gc_col[C - 1:C, :] - gc_col))
            gcs_parts[chunk].append(jnp.where(lane == h, gc_col, 0.0) + jnp.where(lane == DN_HEADS + h, beta, 0.0)
                                    + jnp.where(lane == 2 * DN_HEADS + h, g_col, 0.0))

        _interleave(head_program(chunk, h) for chunk in range(G) for h in range(DN_HEADS))
        for chunk in range(G):
            gcs_ref[chunk * C:(chunk + 1) * C, :] = sum(gcs_parts[chunk][1:], gcs_parts[chunk][0])

    smem = pl.BlockSpec(memory_space=pltpu.SMEM)
    wide = pl.BlockSpec((R, 512), lambda n: (n, 0))
    sq = pl.BlockSpec((DN_HEADS, R, CPAD), lambda n: (0, n, 0))
    narrow = pl.BlockSpec((R, 128), lambda n: (n, 0))
    f = lambda *shp: jax.ShapeDtypeStruct(shp, F32)
    b = lambda *shp: jax.ShapeDtypeStruct(shp, BF16)
    return pl.pallas_call(
        body, grid=(steps,), name="dn_chunk_fwd",
        in_specs=[smem, smem, pl.BlockSpec((R, 1536), lambda n: (n, 0)), pl.BlockSpec((R, 128), lambda n: (n, BLK_G))],
        out_specs=[wide, wide, wide, wide, sq, sq, narrow],
        out_shape=[b(S, 512), f(S, 512), b(S, 512), b(S, 512), b(DN_HEADS, S, CPAD), f(DN_HEADS, S, CPAD), f(S, 128)],
        compiler_params=_params(("parallel",)),
    )(a_log, dt_bias, qkv, pg)


def _gated_norm(o, z, gn):
    r, oh = _rms_stats(o)
    return oh * gn * (z * _sigmoid(z))


def _dn_scan_fwd(w, u, qg, kd, a, gcs, pz, gn):
    S = w.shape[0]
    C = DN_CHUNK
    nc = S // C
    G = CHUNKS_SCAN
    R = G * C

    def body(w_ref, u_ref, qg_ref, kd_ref, a_ref, gcs_ref, z_ref, gn_ref, o_ref, vn_ref, sst_ref, out_ref, state):
        @pl.when(pl.program_id(0) == 0)
        def _():
            state[...] = jnp.zeros_like(state)

        def head_program(chunk, h):
            hs = _head(h)
            rows = slice(chunk * C, (chunk + 1) * C)
            s_in = state[h]
            sst_ref[chunk, h] = s_in
            sb = _bf(s_in)
            w_s = _dot(w_ref[rows, hs], sb)
            q_s = _dot(qg_ref[rows, hs], sb)
            yield
            vn = u_ref[rows, hs] - w_s
            vnb = _bf(vn)
            o = q_s + _dot(a_ref[h, rows], _rows_pad(vnb))
            k_v = _dot_tn(kd_ref[rows, hs], vnb)
            yield
            state[h] = s_in * jnp.exp(gcs_ref[(chunk + 1) * C - 1:(chunk + 1) * C, h:h + 1]) + k_v
            o_ref[rows, hs] = o
            vn_ref[rows, hs] = vn
            out_ref[rows, hs] = _gated_norm(o, z_ref[rows, hs], gn_ref[...])

        for chunk in range(G):
            _interleave(head_program(chunk, h) for h in range(DN_HEADS))

    wide = pl.BlockSpec((R, 512), lambda n: (n, 0))
    f = lambda *shp: jax.ShapeDtypeStruct(shp, F32)
    return pl.pallas_call(
        body, grid=(nc // G,), name="dn_scan_fwd",
        in_specs=[wide, wide, wide, wide, pl.BlockSpec((DN_HEADS, R, CPAD), lambda n: (0, n, 0)),
                  pl.BlockSpec((R, 128), lambda n: (n, 0)), pl.BlockSpec((R, 512), lambda n: (n, BLK_Z)),
                  pl.BlockSpec((1, DN_DIM), lambda n: (0, 0))],
        out_specs=[wide, wide, pl.BlockSpec((G, DN_HEADS, DN_DIM, DN_DIM), lambda n: (n, 0, 0, 0)), wide],
        out_shape=[f(S, 512), f(S, 512), f(nc, DN_HEADS, DN_DIM, DN_DIM), f(S, 512)],
        scratch_shapes=[pltpu.VMEM((DN_HEADS, DN_DIM, DN_DIM), F32)],
        compiler_params=_params(("arbitrary",)),
    )(w, u, qg, kd, a, gcs, pz, gn)


def _dn_scan_bwd(dcat, o, pz, gn, sst, vnew, w, qg, kd, a, gcs, dproj):
    S = o.shape[0]
    C = DN_CHUNK
    G = CHUNKS_SCAN
    R = G * C
    steps = S // R

    def body(dy_ref, o_ref, z_ref, gn_ref, sst_ref, vn_ref, w_ref, qg_ref, kd_ref, a_ref, gcs_ref, _,
             du_ref, dw_ref, dqg_ref, dkd_ref, da_ref, dz_ref, dsc_ref, dgn_ref, dstate):
        @pl.when(pl.program_id(0) == 0)
        def _():
            dstate[...] = jnp.zeros_like(dstate)
            dgn_ref[...] = jnp.zeros_like(dgn_ref)

        gn_ = gn_ref[...]
        lane = lax.broadcasted_iota(jnp.int32, (C, 128), 1)
        row = lax.broadcasted_iota(jnp.int32, (C, 128), 0)
        dgn_parts = []

        def head_program(chunk, h, dsc_parts):
            hs = _head(h)
            rows = slice(chunk * C, (chunk + 1) * C)
            ov, z, dout = o_ref[rows, hs], z_ref[rows, hs], dy_ref[rows, hs]
            r, oh = _rms_stats(ov)
            sg = _sigmoid(z)
            don = dout * (z * sg)
            dz_ref[rows, hs] = _bf(dout * (oh * gn_) * (sg * (1.0 + z * (1.0 - sg))))
            dgn_parts.append(jnp.sum(don * oh, axis=0, keepdims=True))
            dn = don * gn_
            do = _bf(r * (dn - oh * jnp.mean(dn * oh, axis=-1, keepdims=True)))
            s_in = sst_ref[chunk, h]
            sb = _bf(s_in)
            ds_out = dstate[h]
            dsb = _bf(ds_out)
            vnb = _bf(vn_ref[rows, hs])
            wb, qgb, kdb, ab = w_ref[rows, hs], qg_ref[rows, hs], kd_ref[rows, hs], a_ref[h, rows]
            dvn = _dot_tn(ab, do)[:C] + _dot(kdb, dsb)
            da_ref[h, rows] = _dot_nt(do, _rows_pad(vnb))
            dqg_ref[rows, hs] = _dot_nt(do, sb)
            dkd_ref[rows, hs] = _dot_nt(vnb, dsb)
            q_do = _dot_tn(qgb, do)
            yield
            dvnb = _bf(dvn)
            dw_ref[rows, hs] = _bf(-_dot_nt(dvnb, sb))
            w_dvn = _dot_tn(wb, dvnb)
            du_ref[rows, hs] = dvnb
            yield
            d_last = jnp.exp(gcs_ref[(chunk + 1) * C - 1:(chunk + 1) * C, h:h + 1])
            dd = jnp.sum(jnp.sum(ds_out * s_in, axis=1, keepdims=True), axis=0, keepdims=True)
            dsc_parts.append(jnp.where((lane == h) & (row == C - 1), dd * d_last, 0.0))
            dstate[h] = ds_out * d_last + q_do - w_dvn

        for chunk in reversed(range(G)):
            dsc_parts = []
            _interleave(head_program(chunk, h, dsc_parts) for h in range(DN_HEADS))
            dsc_ref[chunk * C:(chunk + 1) * C, :] = sum(dsc_parts[1:], dsc_parts[0])
        dgn_ref[...] += sum(dgn_parts[1:], dgn_parts[0])

    rev = lambda n: steps - 1 - n
    wide = pl.BlockSpec((R, 512), lambda n: (rev(n), 0))
    z_spec = pl.BlockSpec((R, 512), lambda n: (rev(n), BLK_Z))
    sq = pl.BlockSpec((DN_HEADS, R, CPAD), lambda n: (0, rev(n), 0))
    narrow = pl.BlockSpec((R, 128), lambda n: (rev(n), 0))
    gn_spec = pl.BlockSpec((1, DN_DIM), lambda n: (0, 0))
    f = lambda *shp: jax.ShapeDtypeStruct(shp, F32)
    b = lambda *shp: jax.ShapeDtypeStruct(shp, BF16)
    return pl.pallas_call(
        body, grid=(steps,), name="dn_scan_bwd",
        in_specs=[pl.BlockSpec((R, 512), lambda n: (rev(n), 1)), wide, z_spec, gn_spec,
                  pl.BlockSpec((G, DN_HEADS, DN_DIM, DN_DIM), lambda n: (rev(n), 0, 0, 0)),
                  wide, wide, wide, wide, sq, narrow, pl.BlockSpec(memory_space=pl.ANY)],
        out_specs=[wide, wide, wide, wide, sq, z_spec, narrow, gn_spec],
        out_shape=[b(S, 512), b(S, 512), f(S, 512), f(S, 512), f(DN_HEADS, S, CPAD),
                   jax.ShapeDtypeStruct(dproj.shape, dproj.dtype), f(S, 128), f(1, DN_DIM)],
        scratch_shapes=[pltpu.VMEM((DN_HEADS, DN_DIM, DN_DIM), F32)],
        input_output_aliases={11: 5},
        compiler_params=_params(("arbitrary",)),
    )(dcat, o, pz, gn, sst, vnew, w, qg, kd, a, gcs, dproj)


def _dn_chunk_bwd(qkv, pg, t_inv, gcs, du, dw, dqg, dkd, da, dsc, a_log, dt_bias, dproj):
    S = qkv.shape[0]
    C = DN_CHUNK
    G = CHUNKS_LOCAL
    R = G * C

    def body(alog_ref, dtb_ref, qkv_ref, pg_ref, t_ref, gcs_ref, du_ref, dw_ref, dqg_ref, dkd_ref, da_ref, dsc_ref, _,
             dqkv_ref, dpg_ref, acc_ref):
        @pl.when(pl.program_id(0) == 0)
        def _():
            acc_ref[...] = jnp.zeros_like(acc_ref)

        ii, jj = _chunk_masks()
        lane = lax.broadcasted_iota(jnp.int32, (1, 128), 1)
        row8 = lax.broadcasted_iota(jnp.int32, (8, 128), 0)
        lane8 = lax.broadcasted_iota(jnp.int32, (8, 128), 1)
        rowc = lax.broadcasted_iota(jnp.int32, (C, 1), 0)
        tril, strict = jj <= ii, jj < ii
        dpg_parts, acc_parts = [[] for _ in range(G)], []

        def head_program(chunk, h):
            rows = slice(chunk * C, (chunk + 1) * C)
            q, k, v = qkv_ref[rows, _head(h)], qkv_ref[rows, _head(DN_HEADS + h)], qkv_ref[rows, _head(2 * DN_HEADS + h)]
            gc_col, beta, g_col = gcs_ref[rows, h:h + 1], gcs_ref[rows, DN_HEADS + h:DN_HEADS + h + 1], \
                gcs_ref[rows, 2 * DN_HEADS + h:2 * DN_HEADS + h + 1]
            dec = _decay(gc_col, ii, jj)
            eg = jnp.exp(gc_col)
            g_last = gc_col[C - 1:C, :]
            ek = jnp.exp(g_last - gc_col)
            kb, vb = k * beta, v * beta
            kbg = kb * eg
            qb, kbb = _bf(q), _bf(kb)
            k_rows = _rows_pad(_bf(k))
            t = t_ref[h, rows]
            tb = _bf(t)
            dub, dwb = du_ref[rows, _head(h)], dw_ref[rows, _head(h)]
            dqg_, dkd_ = dqg_ref[rows, _head(h)], dkd_ref[rows, _head(h)]
            dt = _dot_nt(dub, _rows_pad(_bf(vb))) + _dot_nt(dwb, _rows_pad(_bf(kbg)))
            t_du_dw = _dot_tn(tb, jnp.concatenate([dub, dwb], axis=1))
            dvb, dkbg = t_du_dw[:C, :DN_DIM], t_du_dw[:C, DN_DIM:]
            kk = _dot_nt(kbb, k_rows)
            qk = _dot_nt(qb, k_rows)
            yield
            dt_t = _dot3_nt(dt, t)
            yield
            dl = -_dot3_tn(t, dt_t)
            yield
            dm = jnp.where(strict, dl * dec, 0.0)
            dqk = jnp.where(tril, da_ref[h, rows] * dec, 0.0)
            gmat = dm * kk + dqk * qk
            dgc = jnp.sum(gmat, axis=1, keepdims=True) - _row_to_col(jnp.sum(gmat, axis=0, keepdims=True), ii, jj)
            dmb, dqkb = _bf(dm), _bf(dqk)
            dkb = _dot(dmb, k_rows) + dkbg * eg
            dk = _dot_tn(jnp.concatenate([dmb, dqkb], axis=0), jnp.concatenate([kbb, qb], axis=0))[:C] + dkd_ * ek
            dq = _dot(dqkb, k_rows) + dqg_ * eg
            yield
            tk = jnp.sum(dkd_ * k * ek, axis=1, keepdims=True)
            dgc = dgc + jnp.sum(dqg_ * q * eg, axis=1, keepdims=True) - tk + jnp.sum(dkbg * kbg, axis=1, keepdims=True)
            dgl = jnp.sum(tk, axis=0, keepdims=True) + dsc_ref[(chunk + 1) * C - 1:(chunk + 1) * C, h:h + 1]
            dgc = dgc + jnp.where(rowc == C - 1, dgl, 0.0)
            dk = dk + dkb * beta
            dbeta = jnp.sum(dkb * k, axis=1, keepdims=True) + jnp.sum(dvb * v, axis=1, keepdims=True)
            dqkv_ref[rows, _head(h)] = dq
            dqkv_ref[rows, _head(DN_HEADS + h)] = dk
            dqkv_ref[rows, _head(2 * DN_HEADS + h)] = dvb * beta
            dg_col = jnp.sum(jnp.where(jj >= ii, _col_to_row(dgc, ii, jj), 0.0), axis=1, keepdims=True)
            db = dbeta * beta * (1.0 - beta)
            da_in = dg_col * (-jnp.exp(alog_ref[h])) * _sigmoid(pg_ref[rows, DN_HEADS + h:DN_HEADS + h + 1] + dtb_ref[h])
            dpg_parts[chunk].append(jnp.where(lane == h, db, 0.0) + jnp.where(lane == DN_HEADS + h, da_in, 0.0))
            acc_parts.append(jnp.where((row8 == 0) & (lane8 == h), jnp.sum(dg_col * g_col, axis=0, keepdims=True), 0.0)
                             + jnp.where((row8 == 1) & (lane8 == h), jnp.sum(da_in, axis=0, keepdims=True), 0.0))

        _interleave(head_program(chunk, h) for chunk in range(G) for h in range(DN_HEADS))
        for chunk in range(G):
            dpg = sum(dpg_parts[chunk][1:], dpg_parts[chunk][0])
            dpg_ref[chunk * C:(chunk + 1) * C, :] = _bf(jnp.concatenate([dpg, jnp.zeros_like(dpg)], axis=1))
        acc_ref[...] += sum(acc_parts[1:], acc_parts[0])

    smem = pl.BlockSpec(memory_space=pltpu.SMEM)
    wide = pl.BlockSpec((R, 512), lambda n: (n, 0))
    sq = pl.BlockSpec((DN_HEADS, R, CPAD), lambda n: (0, n, 0))
    narrow = pl.BlockSpec((R, 128), lambda n: (n, 0))
    qkv_spec = pl.BlockSpec((R, 1536), lambda n: (n, 0))
    f = lambda *shp: jax.ShapeDtypeStruct(shp, F32)
    return pl.pallas_call(
        body, grid=(S // R,), name="dn_chunk_bwd",
        in_specs=[smem, smem, qkv_spec, pl.BlockSpec((R, 128), lambda n: (n, BLK_G)), sq, narrow, wide, wide, wide, wide, sq,
                  narrow, pl.BlockSpec(memory_space=pl.ANY)],
        out_specs=[qkv_spec, pl.BlockSpec((R, 256), lambda n: (n, BLK_G_PAD)), pl.BlockSpec((8, 128), lambda n: (0, 0))],
        out_shape=[f(S, 1536), jax.ShapeDtypeStruct(dproj.shape, dproj.dtype), f(8, 128)],
        input_output_aliases={12: 1},
        compiler_params=_params(("arbitrary",)),
    )(a_log, dt_bias, qkv, pg, t_inv, gcs, du, dw, dqg, dkd, da, dsc, dproj)


def _fill_kv(dk, dv, dproj):
    S = dk.shape[0]
    tm = min(512, S)

    def body(dk_ref, dv_ref, _, o_ref):
        o_ref[...] = _bf(jnp.concatenate([dk_ref[...], dv_ref[...]], axis=1))

    tile = pl.BlockSpec((tm, 128), lambda i: (i, 0))
    return pl.pallas_call(
        body, grid=(S // tm,), name="fill_kv",
        in_specs=[tile, tile, pl.BlockSpec(memory_space=pl.ANY)],
        out_specs=pl.BlockSpec((tm, 256), lambda i: (i, BLK_KV)),
        out_shape=jax.ShapeDtypeStruct(dproj.shape, dproj.dtype),
        input_output_aliases={2: 0},
        compiler_params=_params(("parallel",)),
    )(dk, dv, dproj)


def _w_in_to_internal(wt):
    return jnp.concatenate([wt[0:512], wt[2304:2816], wt[768:2304], wt[512:768], wt[2816:2824],
                            jnp.zeros((D_IN_PAD - D_IN, wt.shape[1]), wt.dtype)], axis=0)


def _w_in_from_internal(gt):
    return jnp.concatenate([gt[0:512], gt[2560:2816], gt[1024:2560], gt[512:1024], gt[2816:2824]], axis=0)


def _local_step(x, p, target, wts, first_weights, other_weights, ship_early):
    S = x.shape[0]
    cos, sin = _rope_tables(S)
    sinks, a_log, dt_bias = wts["sinks"].reshape(8), wts["a_log"].reshape(4), wts["dt_bias"].reshape(4)
    gn = wts["dn_norm"].reshape(1, DN_DIM)
    add = lambda acc, res: (acc + res,)

    u = _rmsnorm_fwd(x, wts["norm_mix"], "norm_mix_fwd")
    w_in_t, conv_w = first_weights(u)
    proj, = _mm(u, w_in_t, form="nt", name="in_proj", out_dtypes=[F32], tn=512)
    attn = _attn_fwd(proj, cos, sin, sinks)
    qkv = _dn_prep_fwd(proj, conv_w)
    cw, cu, cqg, ckd, ca, ct, gcs = _dn_chunk_fwd(qkv, proj, a_log, dt_bias)
    o, vnew, sst, dn_out = _dn_scan_fwd(cw, cu, cqg, ckd, ca, gcs, proj, gn)
    w_o, = other_weights(("w_o",), dn_out)
    h1, = _mm([attn, dn_out], w_o, form="nn", name="out_proj", out_dtypes=[F32], tn=512, epi=add, extra=[x])

    def relu2(acc):
        r = jnp.maximum(acc, 0.0)
        return r * r, r

    w_up, = other_weights(("w_up",), h1)
    hid, relu, m = _mm(h1, w_up, form="nn", name="mlp_up", out_dtypes=[BF16, BF16], tn=512, epi=relu2, norm=wts["norm_mlp"])
    w_down, = other_weights(("w_down",), hid)
    h2, = _mm(hid, w_down, form="nn", name="mlp_down", out_dtypes=[F32], tn=512, epi=add, extra=[h1])
    w_pg, w_pp = other_weights(("w_ple_gate", "w_ple_proj"), h2)
    pp, = _mm(p, w_pp, form="nn", name="ple_proj", out_dtypes=[F32], tn=128)

    def ple(acc, h2_t, pp_t):
        gate = _sigmoid(acc)
        return h2_t + gate * pp_t, gate

    h3, gate, n3 = _mm(h2, w_pg, form="nn", name="ple_gate", out_dtypes=[F32, F32], tn=512, epi=ple, extra=[h2, pp],
                       norm=wts["norm_ple"])
    dh3, dgl, dpp, loss, d_norm_final = _final_loss(h3, wts["norm_final"].reshape(1, D_MODEL), target, pp, gate)
    g = {"norm_final": d_norm_final}
    early = {"w_ple_gate": _mm_tn(n3, dgl, name="d_w_ple_gate", tm=512, tn=1024, out_dtype=BF16).reshape(N_DEV, 128, 1024),
             "w_ple_proj": _mm_tn(p, dpp, name="d_w_ple_proj", tm=256, tn=128, out_dtype=BF16, column_shards=True)}
    dh2, g["norm_ple"] = _mm(dgl, w_pg, form="nt", name="d_n3", out_dtypes=[F32], tn=512,
                             norm_bwd=(h2, wts["norm_ple"], dh3))
    d_act, = _mm(dh2, w_down, form="nt", name="d_hidden", out_dtypes=[BF16], tn=512,
                 epi=lambda acc, r: (acc * (2.0 * r.astype(F32)),), extra=[relu])
    early["w_down"] = _mm_tn(hid, dh2, name="d_w_down", tm=512, tn=1024, out_dtype=BF16).reshape(N_DEV, 512, 1024)
    early["w_up"] = _mm_tn(m, d_act, name="d_w_up", tm=1024, tn=512, out_dtype=BF16, column_shards=True)
    token = ship_early(early)
    dh1, g["norm_mlp"] = _mm(d_act, w_up, form="nt", name="d_m", out_dtypes=[F32], tn=512, after=token,
                             norm_bwd=(h1, wts["norm_mlp"], dh2))
    dcat, = _mm(dh1, w_o, form="nt", name="d_cat", out_dtypes=[F32], tn=512)
    d_w_o = jnp.concatenate([_mm_tn(attn, dh1, name="d_w_o_attn", tm=512, tn=512, out_dtype=BF16),
                             _mm_tn(dn_out, dh1, name="d_w_o_dn", tm=512, tn=512, out_dtype=BF16)], axis=0)
    token = ship_early({"w_o": d_w_o.reshape(N_DEV, 128, 1024)})
    dproj, dk, dv, dsinks = _attn_bwd(proj, cos, sin, sinks + token[0, 0], dcat)
    g["sinks"] = dsinks[:, 0].reshape(1, 8)
    du_, dw_, dqg, dkd, da, dproj, dsc, g["dn_norm"] = _dn_scan_bwd(dcat, o, proj, gn, sst, vnew, cw, cqg, ckd, ca, gcs, dproj)
    dqkv, dproj, gate_acc = _dn_chunk_bwd(qkv, proj, ct, gcs, du_, dw_, dqg, dkd, da, dsc, a_log, dt_bias, dproj)
    g["a_log"], g["dt_bias"] = gate_acc[0:1, 0:4], gate_acc[1:2, 0:4]
    dproj, g["conv_w"] = _dn_prep_bwd(proj, conv_w, dqkv, dproj)
    dproj = _fill_kv(dk, dv, dproj)
    token = ship_early({"w_in": _mm_tn(dproj, u, name="d_w_in", tm=512, tn=1024)})
    grad_x, g["norm_mix"] = _mm(dproj, w_in_t, form="nn", name="d_u", out_dtypes=[F32], tn=512, after=token,
                                norm_bwd=(x, wts["norm_mix"], dh1))
    return loss, grad_x, g


def _peer(k):
    x, y, c = lax.axis_index("x"), lax.axis_index("y"), lax.axis_index("c")
    px = 1 - x if k & 4 else x
    py = 1 - y if k & 2 else y
    pc = 1 - c if k & 1 else c
    return (px, py, pc), 4 * px + 2 * py + pc


def _exchange(srcs, name, gather):
    n = len(srcs)
    gathers = list(gather) if isinstance(gather, (list, tuple)) else [gather] * n
    shapes = [(N_DEV,) + s.shape if gt else s.shape for s, gt in zip(srcs, gathers)]

    def body(*refs):
        src_refs, out_refs = refs[:n], refs[n:2 * n]
        send_sems, recv_sems, local_sems = refs[2 * n:]
        _, me = _peer(0)
        piece = lambda a, d: src_refs[a] if gathers[a] else src_refs[a].at[d]
        local = [pltpu.make_async_copy(piece(a, me), out_refs[a].at[me], local_sems.at[a]) for a in range(n)]
        for cp in local:
            cp.start()
        copies = []
        for a in range(n):
            for k in range(1, N_DEV):
                dev, idx = _peer(k)
                cp = pltpu.make_async_remote_copy(src_ref=piece(a, idx), dst_ref=out_refs[a].at[me],
                                                  send_sem=send_sems.at[a, k - 1], recv_sem=recv_sems.at[a, k - 1],
                                                  device_id=dev, device_id_type=MESH)
                cp.start()
                copies.append(cp)
        for cp in copies:
            cp.wait_recv()
        for cp in copies:
            cp.wait_send()
        for cp in local:
            cp.wait()

    anywhere = pl.BlockSpec(memory_space=pl.ANY)
    return pl.pallas_call(
        body, name=name, in_specs=[anywhere] * n, out_specs=[anywhere] * n,
        out_shape=[jax.ShapeDtypeStruct(shp, s.dtype) for shp, s in zip(shapes, srcs)],
        scratch_shapes=[pltpu.SemaphoreType.DMA((n, N_DEV - 1)), pltpu.SemaphoreType.DMA((n, N_DEV - 1)),
                        pltpu.SemaphoreType.DMA((n,))],
    )(*srcs)


_HBM = pl.BlockSpec(memory_space=pltpu.HBM)
_SEM = pl.BlockSpec(memory_space=pltpu.SEMAPHORE)
_EFFECT = pltpu.SideEffectType.DATAFLOW_SIDE_EFFECTING


def _split_copies(src_refs, land_refs, send_sems, recv_sems, gather, which=None):
    _, me = _peer(0)
    copies = []
    which = range(len(src_refs)) if which is None else which
    for a, src, land in zip(which, src_refs, land_refs):
        for k in range(1, N_DEV):
            dev, idx = _peer(k)
            sem = a * (N_DEV - 1) + k - 1
            copies.append(pltpu.make_async_remote_copy(
                src_ref=src if gather else src.at[idx], dst_ref=land.at[me], send_sem=send_sems.at[sem],
                recv_sem=recv_sems.at[sem], device_id=dev, device_id_type=MESH))
    return copies


def _exchange_start(srcs, name, gather):
    n = len(srcs)
    me = 4 * lax.axis_index("x") + 2 * lax.axis_index("y") + lax.axis_index("c")
    lands = []
    for s in srcs:
        own = s if gather else lax.dynamic_index_in_dim(s, me, 0, keepdims=False)
        shape = (N_DEV,) + s.shape if gather else s.shape
        lands.append(lax.dynamic_update_index_in_dim(lax.empty(shape, s.dtype), own, me, 0))

    def body(*refs):
        src_refs, land_refs = refs[:n], refs[n:2 * n]
        send_sems, recv_sems = refs[2 * n], refs[2 * n + 1]
        for cp in _split_copies(src_refs, land_refs, send_sems, recv_sems, gather):
            cp.start()
        refs[-1][...] = jnp.zeros_like(refs[-1])

    both = list(srcs) + lands
    sems = pltpu.SemaphoreType.DMA((n * (N_DEV - 1),))
    out = pl.pallas_call(
        body, name=name,
        out_shape=(sems, sems, *[pltpu.HBM(t.shape, t.dtype) for t in both], jax.ShapeDtypeStruct((8, 128), F32)),
        in_specs=[_HBM] * (2 * n), out_specs=(_SEM, _SEM, *[_HBM] * (2 * n), pl.BlockSpec(memory_space=pltpu.VMEM)),
        input_output_aliases={i: 2 + i for i in range(2 * n)},
        compiler_params=pltpu.CompilerParams(has_side_effects=_EFFECT),
    )(*[pltpu.with_memory_space_constraint(t, pltpu.HBM) for t in both])
    return (n, gather, out[:-1]), out[-1]


def _exchange_wait(handle, after, name, which=None):
    n_all, gather, (send_sems, recv_sems, *both_all) = handle
    which = list(range(n_all)) if which is None else list(which)
    n = len(which)
    both = [both_all[a] for a in which] + [both_all[n_all + a] for a in which]

    def body(*refs):
        src_refs, land_refs = refs[:n], refs[n:2 * n]
        for cp in _split_copies(src_refs, land_refs, refs[2 * n], refs[2 * n + 1], gather, which):
            cp.wait_send()
            cp.wait_recv()

    out = pl.pallas_call(
        body, name=name, out_shape=tuple(pltpu.HBM(t.shape, t.dtype) for t in both),
        in_specs=[_HBM] * (2 * n) + [_SEM, _SEM, pl.BlockSpec(memory_space=pl.ANY)], out_specs=tuple([_HBM] * (2 * n)),
        input_output_aliases={i: i for i in range(2 * n)},
        compiler_params=pltpu.CompilerParams(has_side_effects=_EFFECT),
    )(*both, send_sems, recv_sems, after)
    return list(out[n:])


def _adamw(parts, w, m, v, name):
    n, R, W = parts.shape
    tm = 128 if R % 128 == 0 else R

    def body(p_ref, w_ref, m_ref, v_ref, g_ref, d_ref, nm_ref, nv_ref):
        g = p_ref[0].astype(F32)
        for s in range(1, n):
            g = g + p_ref[s].astype(F32)
        nm = ADAM_B1 * m_ref[...] + (1.0 - ADAM_B1) * g
        nv = ADAM_B2 * v_ref[...] + (1.0 - ADAM_B2) * (g * g)
        m_hat = nm / (1.0 - ADAM_B1 ** ADAM_STEP)
        v_hat = nv / (1.0 - ADAM_B2 ** ADAM_STEP)
        g_ref[...] = g
        d_ref[...] = -ADAM_LR * (m_hat / (jnp.sqrt(v_hat) + ADAM_EPS) + ADAM_WD * w_ref[...])
        nm_ref[...] = nm
        nv_ref[...] = nv

    tile = pl.BlockSpec((tm, W), lambda i: (i, 0))
    return pl.pallas_call(
        body, grid=(R // tm,), name=name,
        in_specs=[pl.BlockSpec((n, tm, W), lambda i: (0, i, 0)), tile, tile, tile],
        out_specs=[tile] * 4, out_shape=[jax.ShapeDtypeStruct((R, W), F32)] * 4,
        compiler_params=_params(("parallel",)),
    )(parts, w, m, v)


_MATRICES = ("w_in", "w_o", "w_up", "w_down", "w_ple_gate", "w_ple_proj")


_OTHERS = ("w_o", "w_up", "w_down", "w_ple_gate", "w_ple_proj")


_SMALL_ROWS = 16
_VEC_ROW = {"norm_mix": 0, "norm_mlp": 1, "norm_ple": 2, "norm_final": 3}
_VEC_LANES = {"a_log": (0, 4), "dt_bias": (4, 8), "sinks": (8, 16), "dn_norm": (128, 256)}
_LOSS_ROW, _CONV_ROW = 5, 8


def _pack_small(vals, extra_rows):
    row4 = jnp.zeros((1024,), F32)
    for n, (a, b) in _VEC_LANES.items():
        row4 = row4.at[a:b].set(vals[n].reshape(b - a))
    rows = [vals[n].reshape(1, 1024) for n in ("norm_mix", "norm_mlp", "norm_ple", "norm_final")] + [row4.reshape(1, 1024)]
    return jnp.concatenate(rows + extra_rows, axis=0)


def _unpack_small(buf, like):
    out = {n: buf[r].reshape(like[n].shape) for n, r in _VEC_ROW.items()}
    for n, (a, b) in _VEC_LANES.items():
        out[n] = buf[4, a:b].reshape(like[n].shape)
    return out


_ORDER = ("norm_mix", "w_in", "conv_w", "a_log", "dt_bias", "dn_norm", "sinks", "w_o", "norm_mlp", "w_up", "w_down",
          "norm_ple", "w_ple_gate", "w_ple_proj", "norm_final")


def kernel(x, p, norm_mix, w_in, conv_w, a_log, dt_bias, dn_norm, sinks, w_o, norm_mlp, w_up, w_down, norm_ple, w_ple_gate, w_ple_proj, norm_final, loss_target, m_norm_mix, m_w_in, m_conv_w, m_a_log, m_dt_bias, m_dn_norm, m_sinks, m_w_o, m_norm_mlp, m_w_up, m_w_down, m_norm_ple, m_w_ple_gate, m_w_ple_proj, m_norm_final, v_norm_mix, v_w_in, v_conv_w, v_a_log, v_dt_bias, v_dn_norm, v_sinks, v_w_o, v_norm_mlp, v_w_up, v_w_down, v_norm_ple, v_w_ple_gate, v_w_ple_proj, v_norm_final):
    w = dict(norm_mix=norm_mix, w_in=w_in[0], conv_w=conv_w[0], a_log=a_log, dt_bias=dt_bias, dn_norm=dn_norm, sinks=sinks,
             w_o=w_o[0], norm_mlp=norm_mlp, w_up=w_up[0], w_down=w_down[0], norm_ple=norm_ple, w_ple_gate=w_ple_gate[0],
             w_ple_proj=w_ple_proj[0], norm_final=norm_final)
    m = dict(norm_mix=m_norm_mix, w_in=m_w_in[0], conv_w=m_conv_w[0], a_log=m_a_log, dt_bias=m_dt_bias, dn_norm=m_dn_norm,
             sinks=m_sinks, w_o=m_w_o[0], norm_mlp=m_norm_mlp, w_up=m_w_up[0], w_down=m_w_down[0], norm_ple=m_norm_ple,
             w_ple_gate=m_w_ple_gate[0], w_ple_proj=m_w_ple_proj[0], norm_final=m_norm_final)
    v = dict(norm_mix=v_norm_mix, w_in=v_w_in[0], conv_w=v_conv_w[0], a_log=v_a_log, dt_bias=v_dt_bias, dn_norm=v_dn_norm,
             sinks=v_sinks, w_o=v_w_o[0], norm_mlp=v_norm_mlp, w_up=v_w_up[0], w_down=v_w_down[0], norm_ple=v_norm_ple,
             w_ple_gate=v_w_ple_gate[0], w_ple_proj=v_w_ple_proj[0], norm_final=v_norm_final)
    me = 4 * lax.axis_index("x") + 2 * lax.axis_index("y") + lax.axis_index("c")
    conv_shard = conv_w.shape[2]

    for d in (w, m, v):
        d["w_in"] = d["w_in"].T
    conv_pad = jnp.pad(w["conv_w"], ((0, 8 - DN_CONV), (0, 256 - conv_shard)))
    first, token_first = _exchange_start([_bf(w["w_in"]), conv_pad], "gather_first_start", gather=True)
    later = [_bf(w[n]) for n in _OTHERS]
    later[-1] = _bf(w["w_ple_proj"] + token_first[0:1, 0:1])
    others, token_others = _exchange_start(later, "gather_others_start", gather=True)
    vectors = dict(w)
    vectors["norm_mix"] = w["norm_mix"] + token_others[0:1, 0:1]

    def first_weights(after):
        w_in_all, conv_all = _exchange_wait(first, after, "gather_first_wait")
        conv_all = jnp.transpose(conv_all[:, :DN_CONV, :conv_shard], (1, 0, 2)).reshape(DN_CONV, N_DEV * conv_shard)
        return _w_in_to_internal(w_in_all.reshape(D_IN, D_MODEL)), conv_all

    as_taken = {"w_o": lambda t: t.reshape(1024, 1024), "w_up": lambda t: t, "w_down": lambda t: t.reshape(4096, 1024),
                "w_ple_gate": lambda t: t.reshape(1024, 1024), "w_ple_proj": lambda t: t}

    def other_weights(names, after):
        which = [_OTHERS.index(n) for n in names]
        got = _exchange_wait(others, after, "gather_wait_" + names[0], which)
        return [as_taken[n](t) for n, t in zip(names, got)]

    shipped = []

    def ship_early(pieces):
        names = tuple(pieces)
        if names == ("w_in",):
            pieces = {"w_in": _bf(_w_in_from_internal(pieces["w_in"])).reshape(N_DEV, D_IN // N_DEV, D_MODEL)}
        handle, token = _exchange_start([pieces[n] for n in names], "scatter_start_" + names[0], gather=False)
        shipped.append((names, handle))
        return token

    loss, grad_x, g = _local_step(x[0], p[0, 0], loss_target[0], vectors, first_weights, other_weights, ship_early)

    small = _pack_small(g, [loss[:, :1] * jnp.ones((1, 1024), F32), jnp.zeros((2, 1024), F32),
                            g["conv_w"].reshape(6, 1024), jnp.zeros((2, 1024), F32)])
    small_all, = _exchange([small], "gather_small", gather=True)
    received = {}
    for names, handle in shipped:
        received.update(zip(names, _exchange_wait(handle, grad_x, "scatter_wait_" + names[0])))
    big = {n: _adamw(received[n], w[n], m[n], v[n], "adamw_" + n) for n in _MATRICES}
    zeros16 = jnp.zeros((_SMALL_ROWS, 1024), F32)
    summed = _adamw(small_all, zeros16, zeros16, zeros16, "sum_small")[0]
    conv_g = lax.dynamic_slice(summed[_CONV_ROW:_CONV_ROW + 6].reshape(DN_CONV, N_DEV * conv_shard), (0, me * conv_shard),
                               (DN_CONV, conv_shard))
    pad_conv = lambda t: jnp.pad(t.reshape(1, DN_CONV * conv_shard), ((0, 2), (0, 1024 - DN_CONV * conv_shard)))
    small_g = jnp.concatenate([summed[0:5], pad_conv(conv_g)], axis=0)[None]
    pack8 = lambda d: _pack_small(d, [pad_conv(d["conv_w"])])
    sm = _adamw(small_g, pack8(w), pack8(m), pack8(v), "adamw_vectors")

    outs = []
    for i, small_buf in enumerate(sm):
        d = {n: big[n][i] for n in _MATRICES}
        d.update(_unpack_small(small_buf, w))
        d["conv_w"] = small_buf[5, :DN_CONV * conv_shard].reshape(DN_CONV, conv_shard)
        outs.append(d)
    result = [summed[_LOSS_ROW, 0], grad_x[None]]
    for d in outs:
        d["w_in"] = d["w_in"].T
        for n in _ORDER:
            result.append(d[n][None] if n in _MATRICES or n == "conv_w" else d[n].reshape(w[n].shape))
    return tuple(result)
```

```python
import jax
import jax.numpy as jnp
import numpy as np
from jax import lax
from jax.experimental import pallas as pl
from jax.experimental.pallas import tpu as pltpu

F32, BF16 = jnp.float32, jnp.bfloat16
EPS = 1e-6
D_MODEL = 1024
N_DEV = 8
ATTN_BLOCK = 128
HEAD_PAIR = 128
DN_HEADS = 4
DN_DIM = 128
DN_CHUNK = 64
DN_CONV = 4
ROPE_THETA = 10000.0
D_IN = 2824
D_IN_PAD = 3072
BLK_Q, BLK_Z = 0, 1
BLK_DN, BLK_K, BLK_V, BLK_G = 8, 20, 21, 22
BLK_KV, BLK_G_PAD = 10, 11
VMEM_LIMIT = 56 * 1024 * 1024
NEG = -1e30
ADAM_LR, ADAM_B1, ADAM_B2, ADAM_EPS, ADAM_WD, ADAM_STEP = 0.001, 0.9, 0.999, 1e-08, 0.01, 10
MESH = pl.DeviceIdType.MESH


def _bf(x):
    return x.astype(BF16)


def _dot(a, b):
    return jnp.dot(a, b, preferred_element_type=F32)


def _dot_nt(a, b):
    return lax.dot_general(a, b, (((1,), (1,)), ((), ())), preferred_element_type=F32)


def _dot_tn(a, b):
    return lax.dot_general(a, b, (((0,), (0,)), ((), ())), preferred_element_type=F32)


def _sigmoid(x):
    return 1.0 / (1.0 + jnp.exp(-x))


def _params(sem):
    return pltpu.CompilerParams(dimension_semantics=sem, vmem_limit_bytes=VMEM_LIMIT)


def _mm(x, w, *, form, name, out_dtypes, tn, epi=None, extra=(), tm=512, w_row_block=0, after=None, norm=None,
        norm_bwd=None):
    xs = list(x) if isinstance(x, (list, tuple)) else [x]
    nx = len(xs)
    S, K = xs[0].shape
    shards = w.ndim == 3
    N = (w.shape[2] * N_DEV if shards else w.shape[1]) if form == "nn" else w.shape[-2]
    assert not (shards and form == "nn" and tn != w.shape[2]) and (nx == 1 or (form == "nn" and not shards and norm is None))
    r0 = w_row_block * K
    tm = min(tm, S)
    n_extra, n_out = len(extra), len(out_dtypes)
    tile = lambda width: pl.BlockSpec((tm, width), lambda i: (i, 0))
    whole = lambda a: pl.BlockSpec(a.shape, lambda i, nd=a.ndim: (0,) * nd)
    ins, in_specs = [*xs, w, *extra], [tile(K)] * nx + [whole(w)] + [tile(N)] * n_extra
    if norm is not None:
        ins, in_specs = ins + [norm], in_specs + [whole(norm)]
    if norm_bwd is not None:
        ins, in_specs = ins + list(norm_bwd), in_specs + [tile(N), whole(norm_bwd[1]), tile(N)]
    if after is not None:
        ins, in_specs = ins + [after], in_specs + [whole(after)]
    out_shape = [jax.ShapeDtypeStruct((S, N), dt) for dt in out_dtypes]
    out_specs = [tile(N)] * n_out
    if norm is not None:
        out_shape, out_specs = out_shape + [jax.ShapeDtypeStruct((S, K), BF16)], out_specs + [tile(K)]
    if norm_bwd is not None:
        out_shape, out_specs = out_shape + [jax.ShapeDtypeStruct((1, N), F32)], out_specs + [pl.BlockSpec((1, N), lambda i: (0, 0))]

    def product(xb, w_ref, cols, c):
        if form == "nn" and nx > 1:
            return sum(_dot(part, w_ref[r0 + p * K:r0 + (p + 1) * K, cols]) for p, part in enumerate(xb))
        if form == "nn":
            return _dot(xb, w_ref[c] if shards else w_ref[r0:r0 + K, cols])
        if not shards:
            return _dot_nt(xb, w_ref[cols, :])
        ks = w.shape[2]
        acc = _dot_nt(xb[:, 0:ks], w_ref[0, cols, :])
        for s in range(1, N_DEV):
            acc = acc + _dot_nt(xb[:, s * ks:(s + 1) * ks], w_ref[s, cols, :])
        return acc

    def body(*refs):
        x_ref, w_ref = refs[0], refs[nx]
        extra_refs = refs[nx + 1:nx + 1 + n_extra]
        at = nx + 1 + n_extra
        if norm is not None:
            gain_ref, at = refs[at], at + 1
        if norm_bwd is not None:
            (y_ref, ygain_ref, dres_ref), at = refs[at:at + 3], at + 3
        outs = refs[len(ins):]
        if norm is not None:
            _, xh = _rms_stats(x_ref[...])
            xb = _bf(xh * gain_ref[...])
            outs[n_out][...] = xb
        else:
            xb = _bf(x_ref[...]) if nx == 1 else [_bf(r[...]) for r in refs[:nx]]
        for c in range(N // tn):
            cols = slice(c * tn, (c + 1) * tn)
            acc = product(xb, w_ref, cols, c)
            res = epi(acc, *[r[:, cols] for r in extra_refs]) if epi else (acc,)
            for o, r in zip(outs[:n_out], res):
                o[:, cols] = r.astype(o.dtype)
        if norm_bwd is not None:
            dx, dg = _rms_bwd_tile(y_ref[...], ygain_ref[...], outs[0][...])
            outs[0][...] = dres_ref[...] + dx
            dg_ref = outs[-1]

            @pl.when(pl.program_id(0) == 0)
            def _():
                dg_ref[...] = jnp.zeros_like(dg_ref)

            dg_ref[...] += dg

    return pl.pallas_call(
        body, grid=(S // tm,), name=name, in_specs=in_specs, out_specs=out_specs, out_shape=out_shape,
        compiler_params=_params(("arbitrary",) if norm_bwd is not None else ("parallel",)),
    )(*ins)


def _mm_tn(x, dy, *, name, tm, tn, out_dtype=F32, column_shards=False):
    S, K = x.shape
    N = dy.shape[1]

    def body(x_ref, dy_ref, o_ref):
        o_ref[...] = _dot_tn(_bf(x_ref[...]), _bf(dy_ref[...])).astype(out_dtype)

    if column_shards:
        out_spec = pl.BlockSpec((None, tm, tn), lambda i, j: (j, i, 0))
        out_shape = jax.ShapeDtypeStruct((N // tn, K, tn), out_dtype)
    else:
        out_spec = pl.BlockSpec((tm, tn), lambda i, j: (i, j))
        out_shape = jax.ShapeDtypeStruct((K, N), out_dtype)
    return pl.pallas_call(
        body, grid=(K // tm, N // tn), name=name,
        in_specs=[pl.BlockSpec((S, tm), lambda i, j: (0, i)), pl.BlockSpec((S, tn), lambda i, j: (0, j))],
        out_specs=out_spec, out_shape=out_shape,
        compiler_params=_params(("parallel", "parallel")),
    )(x, dy)


def _rowwise(body, *, tiled, full, out_tiled, out_acc, name, tm=512, smem=()):
    S = tiled[0].shape[0]
    tm = min(tm, S)
    n_in = len(smem) + len(tiled) + len(full)

    def kern(*refs):
        @pl.when(pl.program_id(0) == 0)
        def _():
            for r in refs[n_in + len(out_tiled):]:
                r[...] = jnp.zeros_like(r)
        body(*refs)

    in_specs = [pl.BlockSpec(memory_space=pltpu.SMEM) for _ in smem]
    in_specs += [pl.BlockSpec((tm, a.shape[1]), lambda i: (i, 0)) for a in tiled]
    in_specs += [pl.BlockSpec(a.shape, lambda i, nd=a.ndim: (0,) * nd) for a in full]
    out_specs = [pl.BlockSpec((tm, w), lambda i: (i, 0)) for w, _ in out_tiled]
    out_specs += [pl.BlockSpec(shp, lambda i, nd=len(shp): (0,) * nd) for shp, _ in out_acc]
    out_shape = [jax.ShapeDtypeStruct((S, w), dt) for w, dt in out_tiled]
    out_shape += [jax.ShapeDtypeStruct(shp, dt) for shp, dt in out_acc]
    return pl.pallas_call(
        kern, grid=(S // tm,), name=name, in_specs=in_specs, out_specs=out_specs, out_shape=out_shape,
        compiler_params=_params(("arbitrary",)),
    )(*smem, *tiled, *full)


def _rms_stats(x):
    r = lax.rsqrt(jnp.mean(x * x, axis=-1, keepdims=True) + EPS)
    return r, x * r


def _rmsnorm_fwd(x, g, name):
    def body(x_ref, g_ref, o_ref):
        _, xh = _rms_stats(x_ref[...])
        o_ref[...] = _bf(xh * g_ref[...])

    return _rowwise(body, tiled=[x], full=[g], out_tiled=[(x.shape[1], BF16)], out_acc=[], name=name)[0]


def _rms_bwd_tile(x, g, dxn):
    r, xh = _rms_stats(x)
    dg = jnp.sum(dxn * xh, axis=0, keepdims=True)
    dn = dxn * g
    dx = r * (dn - xh * jnp.mean(dn * xh, axis=-1, keepdims=True))
    return dx, dg


def _ple_and_loss(h2, p, target, w_pg, w_pp, g_ple, g_final):
    S, n = h2.shape
    tm = min(512, S)
    tn = 512

    def body(h2_ref, p_ref, t_ref, wpg_ref, wpp_ref, gple_ref, gfin_ref,
             n3_ref, dh_ref, dgl_ref, dpp_ref, loss_ref, dg_ref, pp, gate, h3):
        @pl.when(pl.program_id(0) == 0)
        def _():
            loss_ref[...] = jnp.zeros_like(loss_ref)
            dg_ref[...] = jnp.zeros_like(dg_ref)

        x = h2_ref[...]
        _, xh = _rms_stats(x)
        n3 = _bf(xh * gple_ref[...])
        n3_ref[...] = n3
        pb = _bf(p_ref[...])
        ps = wpp_ref.shape[2]
        for c in range(N_DEV):
            pp[:, c * ps:(c + 1) * ps] = _dot(pb, wpp_ref[c])
        for c in range(n // tn):
            cols = slice(c * tn, (c + 1) * tn)
            gt = _sigmoid(_dot(n3, wpg_ref[:, cols]))
            gate[:, cols] = gt
            h3[:, cols] = x[:, cols] + gt * pp[:, cols]
        y = h3[...]
        _, yh = _rms_stats(y)
        e = yh * gfin_ref[...] - t_ref[...]
        per_tok = jnp.mean(e * e, axis=-1, keepdims=True)
        loss_ref[...] += 0.5 * jnp.sum(per_tok, axis=0, keepdims=True)
        dh, dg = _rms_bwd_tile(y, gfin_ref[...], e * (1.0 / n))
        dh_ref[...] = dh
        dg_ref[...] += dg
        gt = gate[...]
        dgl_ref[...] = _bf(dh * pp[...] * gt * (1.0 - gt))
        dpp_ref[...] = _bf(dh * gt)

    tile = lambda width: pl.BlockSpec((tm, width), lambda i: (i, 0))
    whole = lambda a: pl.BlockSpec(a.shape, lambda i, nd=a.ndim: (0,) * nd)
    return pl.pallas_call(
        body, grid=(S // tm,), name="ple_and_loss",
        in_specs=[tile(n), tile(p.shape[1]), tile(n), whole(w_pg), whole(w_pp), whole(g_ple), whole(g_final)],
        out_specs=[tile(n), tile(n), tile(n), tile(n), pl.BlockSpec((1, 128), lambda i: (0, 0)), pl.BlockSpec((1, n), lambda i: (0, 0))],
        out_shape=[jax.ShapeDtypeStruct((S, n), BF16), jax.ShapeDtypeStruct((S, n), F32), jax.ShapeDtypeStruct((S, n), BF16),
                   jax.ShapeDtypeStruct((S, n), BF16), jax.ShapeDtypeStruct((1, 128), F32), jax.ShapeDtypeStruct((1, n), F32)],
        scratch_shapes=[pltpu.VMEM((tm, n), F32)] * 3,
        compiler_params=_params(("arbitrary",)),
    )(h2, p, target, w_pg, w_pp, g_ple, g_final)


def _rope_tables(S):
    half = 32
    inv = (1.0 / (np.float32(ROPE_THETA) ** (np.arange(half, dtype=np.float32) * np.float32(2.0 / 64)))).astype(np.float32)
    ang = np.arange(S).astype(np.float32)[:, None] * inv[None, :]
    cos, sin = np.cos(ang), np.sin(ang)
    return jnp.asarray(np.tile(cos, (1, 4))), jnp.asarray(np.concatenate([-sin, sin, -sin, sin], axis=1))


def _attn_common(i, kc, kp, vc, vp, cc, sc, cp, sp):
    lane = lax.broadcasted_iota(jnp.int32, (1, HEAD_PAIR), 1)
    lane_lo = jnp.bitwise_and(lane, 63) < 32
    slot = [lane < 64, lane >= 64]

    def swap_halves(t):
        return jnp.where(lane_lo, pltpu.roll(t, 96, 1), pltpu.roll(t, 32, 1))

    def rope(t, cos, sin):
        return t * cos + swap_halves(t) * sin

    def unrope(d, cos, sin):
        return d * cos + swap_halves(d * sin)

    k2 = jnp.concatenate([rope(kp, cp, sp), rope(kc, cc, sc)], axis=0)
    v2 = jnp.concatenate([vp, vc], axis=0)
    r = lax.broadcasted_iota(jnp.int32, (ATTN_BLOCK, 2 * ATTN_BLOCK), 0)
    c = lax.broadcasted_iota(jnp.int32, (ATTN_BLOCK, 2 * ATTN_BLOCK), 1)
    valid = (c > r) & (c <= r + ATTN_BLOCK) & jnp.logical_or(c >= ATTN_BLOCK, i > 0)
    ks, vs = {}, {}
    for j in range(2):
        kn = jnp.where(slot[j], k2, 0.0)
        vn = jnp.where(slot[j], v2, 0.0)
        for s in range(2):
            ks[j, s] = _bf(kn if s == j else pltpu.roll(kn, 64, 1))
            vs[j, s] = _bf(vn if s == j else pltpu.roll(vn, 64, 1))
    return slot, rope, unrope, valid, ks, vs


def _attn_probs(scores, valid, sink):
    s = jnp.where(valid, scores * 0.125, NEG)
    m = jnp.maximum(jnp.max(s, axis=1, keepdims=True), sink)
    e = jnp.exp(s - m)
    inv_z = 1.0 / (jnp.sum(e, axis=1, keepdims=True) + jnp.exp(sink - m))
    return e * inv_z, jnp.exp(sink - m) * inv_z


def _attn_specs(S):
    nb = S // ATTN_BLOCK
    prev = lambda i: jnp.maximum(i - 1, 0)
    blk = lambda w, col, row=(lambda i: i): pl.BlockSpec((ATTN_BLOCK, w), lambda i: (row(i), col))
    in_specs = [pl.BlockSpec(memory_space=pltpu.SMEM),
                blk(512, BLK_Q), blk(128, BLK_K), blk(128, BLK_K, prev), blk(128, BLK_V), blk(128, BLK_V, prev),
                blk(128, 0), blk(128, 0), blk(128, 0, prev), blk(128, 0, prev)]
    return nb, in_specs


def _attn_fwd(pa, cos, sin, sinks):
    S = pa.shape[0]
    nb, in_specs = _attn_specs(S)

    def body(sinks_ref, q_ref, kc_ref, kp_ref, vc_ref, vp_ref, cc_ref, sc_ref, cp_ref, sp_ref, o_ref):
        i = pl.program_id(0)
        cc, sc = cc_ref[...], sc_ref[...]
        _, rope, _, valid, ks, vs = _attn_common(i, kc_ref[...], kp_ref[...], vc_ref[...], vp_ref[...],
                                                 cc, sc, cp_ref[...], sp_ref[...])
        pair_cols = [slice(HEAD_PAIR * pair, HEAD_PAIR * (pair + 1)) for pair in range(4)]
        qps = [_bf(rope(q_ref[:, cols], cc, sc)) for cols in pair_cols]
        outs = {}

        def head_program(h):
            pair, s = divmod(h, 2)
            j = h // 4
            scores = _dot_nt(qps[pair], ks[j, s])
            yield
            p, _ = _attn_probs(scores, valid, sinks_ref[h])
            outs[h] = _dot(_bf(p), vs[j, s])

        _interleave(head_program(h) for h in range(8))
        for pair, cols in enumerate(pair_cols):
            o_ref[:, cols] = outs[2 * pair] + outs[2 * pair + 1]

    return pl.pallas_call(
        body, grid=(nb,), name="attn_fwd", in_specs=in_specs,
        out_specs=pl.BlockSpec((ATTN_BLOCK, 512), lambda i: (i, 0)),
        out_shape=jax.ShapeDtypeStruct((S, 512), F32),
        compiler_params=_params(("parallel",)),
    )(sinks, pa, pa, pa, pa, pa, cos, sin, cos, sin)


def _attn_bwd(pa, cos, sin, sinks, dcat):
    S = pa.shape[0]
    nb, in_specs = _attn_specs(S)
    in_specs = in_specs + [pl.BlockSpec((ATTN_BLOCK, 512), lambda i: (i, 0))]

    def body(sinks_ref, q_ref, kc_ref, kp_ref, vc_ref, vp_ref, cc_ref, sc_ref, cp_ref, sp_ref, do_ref,
             dq_ref, dk_ref, dv_ref, dsink_ref):
        i = pl.program_id(0)

        @pl.when(i == 0)
        def _():
            dk_ref[...] = jnp.zeros_like(dk_ref)
            dv_ref[...] = jnp.zeros_like(dv_ref)
            dsink_ref[...] = jnp.zeros_like(dsink_ref)

        cc, sc, cp, sp = cc_ref[...], sc_ref[...], cp_ref[...], sp_ref[...]
        slot, rope, unrope, valid, ks, vs = _attn_common(i, kc_ref[...], kp_ref[...], vc_ref[...], vp_ref[...], cc, sc, cp, sp)
        pair_cols = [slice(HEAD_PAIR * pair, HEAD_PAIR * (pair + 1)) for pair in range(4)]
        qps = [_bf(rope(q_ref[:, cols], cc, sc)) for cols in pair_cols]
        dobs = [_bf(do_ref[:, cols]) for cols in pair_cols]
        dqs, dks, dvs = {}, {}, {}

        def head_program(h):
            pair, s = divmod(h, 2)
            j = h // 4
            qp, dob = qps[pair], dobs[pair]
            scores = _dot_nt(qp, ks[j, s])
            dp = _dot_nt(dob, vs[j, s])
            yield
            p, p_sink = _attn_probs(scores, valid, sinks_ref[h])
            dr = jnp.sum(p * dp, axis=1, keepdims=True)
            ds = _bf(p * (dp - dr) * 0.125)
            dsink_ref[h:h + 1, :] += -jnp.sum(p_sink * dr, axis=0, keepdims=True)
            dqs[h] = _dot(ds, ks[j, s])
            dk_h = _dot_tn(ds, qp)
            dv_h = _dot_tn(_bf(p), dob)
            yield
            dk_h, dv_h = jnp.where(slot[s], dk_h, 0.0), jnp.where(slot[s], dv_h, 0.0)
            if s != j:
                dk_h, dv_h = pltpu.roll(dk_h, 64, 1), pltpu.roll(dv_h, 64, 1)
            dks[h], dvs[h] = dk_h, dv_h

        _interleave(head_program(h) for h in range(8))
        dk2 = sum((dks[h] for h in range(1, 8)), dks[0])
        dv2 = sum((dvs[h] for h in range(1, 8)), dvs[0])
        for pair, cols in enumerate(pair_cols):
            dq_ref[:, cols] = _bf(unrope(dqs[2 * pair] + dqs[2 * pair + 1], cc, sc))
        cur = pl.ds(pl.multiple_of(i * ATTN_BLOCK, ATTN_BLOCK), ATTN_BLOCK)
        dk_ref[cur, :] += unrope(dk2[ATTN_BLOCK:], cc, sc)
        dv_ref[cur, :] += dv2[ATTN_BLOCK:]

        @pl.when(i > 0)
        def _():
            prv = pl.ds(pl.multiple_of((i - 1) * ATTN_BLOCK, ATTN_BLOCK), ATTN_BLOCK)
            dk_ref[prv, :] += unrope(dk2[:ATTN_BLOCK], cp, sp)
            dv_ref[prv, :] += dv2[:ATTN_BLOCK]

    whole = lambda w: pl.BlockSpec((S, w), lambda i: (0, 0))
    return pl.pallas_call(
        body, grid=(nb,), name="attn_bwd", in_specs=in_specs,
        out_specs=[pl.BlockSpec((ATTN_BLOCK, 512), lambda i: (i, BLK_Q)), whole(128), whole(128),
                   pl.BlockSpec((8, 128), lambda i: (0, 0))],
        out_shape=[jax.ShapeDtypeStruct((S, D_IN_PAD), BF16), jax.ShapeDtypeStruct((S, 128), F32),
                   jax.ShapeDtypeStruct((S, 128), F32), jax.ShapeDtypeStruct((8, 128), F32)],
        compiler_params=_params(("arbitrary",)),
    )(sinks, pa, pa, pa, pa, pa, cos, sin, cos, sin, dcat)


CONV_ROWS = 512
CONV_PAD = 8


def _conv_silu(scr, w, r0):
    y = w[3:4, :] * scr[pl.ds(CONV_PAD + r0, CONV_ROWS), :]
    for j in range(DN_CONV - 1):
        y = y + w[j:j + 1, :] * scr[pl.ds(CONV_PAD + r0 - 3 + j, CONV_ROWS), :]
    return y


def _dn_prep_fwd(pd, conv_w):
    S = pd.shape[0]
    assert S % CONV_ROWS == 0

    def body(x_ref, w_ref, o_ref, scr):
        b = pl.program_id(0)
        scr[0:CONV_PAD, :] = jnp.zeros((CONV_PAD, DN_DIM), F32)
        scr[pl.ds(CONV_PAD, S), :] = x_ref[...]
        w = w_ref[...]
        q_scale = jnp.where(b < DN_HEADS, DN_DIM ** -0.5, 1.0)
        for r0 in range(0, S, CONV_ROWS):
            y = _conv_silu(scr, w, r0)
            a = y * _sigmoid(y)
            rs = lax.rsqrt(jnp.sum(a * a, axis=1, keepdims=True) + EPS)
            o_ref[pl.ds(r0, CONV_ROWS), :] = a * jnp.where(b < 2 * DN_HEADS, rs * q_scale, 1.0)

    col = pl.BlockSpec((S, DN_DIM), lambda b: (0, b))
    return pl.pallas_call(
        body, grid=(3 * DN_HEADS,), name="dn_prep_fwd",
        in_specs=[pl.BlockSpec((S, DN_DIM), lambda b: (0, BLK_DN + b)), pl.BlockSpec((DN_CONV, DN_DIM), lambda b: (0, b))],
        out_specs=col,
        out_shape=jax.ShapeDtypeStruct((S, 3 * DN_HEADS * DN_DIM), F32),
        scratch_shapes=[pltpu.VMEM((S + CONV_PAD, DN_DIM), F32)],
        compiler_params=_params(("parallel",)),
    )(pd, conv_w)


def _dn_prep_bwd(pd, conv_w, dqkv, dproj):
    S = pd.shape[0]

    def body(x_ref, w_ref, d_ref, _, dx_ref, dw_ref, scr, dscr):
        b = pl.program_id(0)
        scr[0:CONV_PAD, :] = jnp.zeros((CONV_PAD, DN_DIM), F32)
        scr[pl.ds(CONV_PAD, S), :] = x_ref[...]
        dscr[pl.ds(S, CONV_PAD), :] = jnp.zeros((CONV_PAD, DN_DIM), F32)
        w = w_ref[...]
        q_scale = jnp.where(b < DN_HEADS, DN_DIM ** -0.5, 1.0)
        is_qk = b < 2 * DN_HEADS
        dw = [jnp.zeros((1, DN_DIM), F32) for _ in range(DN_CONV)]
        for r0 in range(0, S, CONV_ROWS):
            y = _conv_silu(scr, w, r0)
            sg = _sigmoid(y)
            a = y * sg
            dout = d_ref[pl.ds(r0, CONV_ROWS), :]
            rs = lax.rsqrt(jnp.sum(a * a, axis=1, keepdims=True) + EPS)
            da_qk = q_scale * rs * (dout - a * (rs * rs) * jnp.sum(dout * a, axis=1, keepdims=True))
            dy = jnp.where(is_qk, da_qk, dout) * (sg * (1.0 + y * (1.0 - sg)))
            dscr[pl.ds(r0, CONV_ROWS), :] = dy
            for j in range(DN_CONV):
                dw[j] = dw[j] + jnp.sum(dy * scr[pl.ds(CONV_PAD + r0 - 3 + j, CONV_ROWS), :], axis=0, keepdims=True)
        for j in range(DN_CONV):
            dw_ref[j:j + 1, :] = dw[j]
        for r0 in range(0, S, CONV_ROWS):
            dx = w[3:4, :] * dscr[pl.ds(r0, CONV_ROWS), :]
            for j in range(DN_CONV - 1):
                dx = dx + w[j:j + 1, :] * dscr[pl.ds(r0 + 3 - j, CONV_ROWS), :]
            dx_ref[pl.ds(r0, CONV_ROWS), :] = _bf(dx)

    col = pl.BlockSpec((S, DN_DIM), lambda b: (0, b))
    proj_col = pl.BlockSpec((S, DN_DIM), lambda b: (0, BLK_DN + b))
    wcol = pl.BlockSpec((DN_CONV, DN_DIM), lambda b: (0, b))
    return pl.pallas_call(
        body, grid=(3 * DN_HEADS,), name="dn_prep_bwd",
        in_specs=[proj_col, wcol, col, pl.BlockSpec(memory_space=pl.ANY)], out_specs=[proj_col, wcol],
        out_shape=[jax.ShapeDtypeStruct(dproj.shape, dproj.dtype), jax.ShapeDtypeStruct((DN_CONV, 3 * DN_HEADS * DN_DIM), F32)],
        scratch_shapes=[pltpu.VMEM((S + CONV_PAD, DN_DIM), F32), pltpu.VMEM((S + CONV_PAD, DN_DIM), F32)],
        input_output_aliases={3: 0},
        compiler_params=_params(("parallel",)),
    )(pd, conv_w, dqkv, dproj)


CPAD = 128
CHUNKS_LOCAL = 4
CHUNKS_SCAN = 4


def _chunk_masks():
    ii = lax.broadcasted_iota(jnp.int32, (DN_CHUNK, CPAD), 0)
    jj = lax.broadcasted_iota(jnp.int32, (DN_CHUNK, CPAD), 1)
    return ii, jj


def _rows_pad(a):
    return jnp.concatenate([a, jnp.zeros_like(a)], axis=0)


def _hi_lo(a):
    hi = _bf(a)
    return hi, _bf(a - hi.astype(F32))


def _double_step(t, p):
    C = DN_CHUNK
    th, tl = _hi_lo(t)
    ph, pl_ = _hi_lo(p)
    r1 = _dot(jnp.concatenate([th, tl, ph, pl_], axis=0), _rows_pad(ph))
    r2 = _dot(jnp.concatenate([th, ph], axis=0), _rows_pad(pl_))
    return t + (r1[:C] + r1[C:2 * C] + r2[:C]), r1[2 * C:3 * C] + r1[3 * C:] + r2[C:]


def _dot3_nt(a, b):
    C = DN_CHUNK
    ah, al = _hi_lo(a)
    bh, bl = _hi_lo(b)
    r1 = _dot_nt(jnp.concatenate([ah, al], axis=0), _rows_pad(bh))
    return r1[:C] + r1[C:] + _dot_nt(ah, _rows_pad(bl))


def _dot3_tn(a, b):
    C = DN_CHUNK
    ah, al = _hi_lo(a)
    bh, bl = _hi_lo(b)
    return _dot_tn(jnp.concatenate([ah, al, ah], axis=0), jnp.concatenate([bh, bh, bl], axis=0))[:C]


def _interleave(programs):
    programs = list(programs)
    while programs:
        alive = []
        for prog in programs:
            try:
                next(prog)
                alive.append(prog)
            except StopIteration:
                pass
        programs = alive


def _col_to_row(col, ii, jj):
    return jnp.sum(jnp.where(ii == jj, col, 0.0), axis=0, keepdims=True)


def _row_to_col(row, ii, jj):
    return jnp.sum(jnp.where(ii == jj, row, 0.0), axis=1, keepdims=True)


def _decay(gc_col, ii, jj):
    diff = gc_col - _col_to_row(gc_col, ii, jj)
    return jnp.where(jj <= ii, jnp.exp(jnp.where(jj <= ii, diff, 0.0)), 0.0)


def _softplus(x):
    return jnp.maximum(x, 0.0) + jnp.log(1.0 + jnp.exp(-jnp.abs(x)))


def _head(h):
    return slice(DN_DIM * h, DN_DIM * (h + 1))


def _dn_chunk_fwd(qkv, pg, a_log, dt_bias):
    S = qkv.shape[0]
    C = DN_CHUNK
    G = CHUNKS_LOCAL
    R = G * C
    steps = S // R

    def body(alog_ref, dtb_ref, qkv_ref, pg_ref, w_ref, u_ref, qg_ref, kd_ref, a_ref, t_ref, gcs_ref):
        ii, jj = _chunk_masks()
        lane = lax.broadcasted_iota(jnp.int32, (1, 128), 1)
        eye = (ii == jj).astype(F32)
        gcs_parts = [[] for _ in range(G)]

        def head_program(chunk, h):
            rows = slice(chunk * C, (chunk + 1) * C)
            q, k, v = qkv_ref[rows, _head(h)], qkv_ref[rows, _head(DN_HEADS + h)], qkv_ref[rows, _head(2 * DN_HEADS + h)]
            beta = _sigmoid(pg_ref[rows, h:h + 1])
            g_col = -jnp.exp(alog_ref[h]) * _softplus(pg_ref[rows, DN_HEADS + h:DN_HEADS + h + 1] + dtb_ref[h])
            g_row = _col_to_row(g_col, ii, jj)
            gc_col = jnp.sum(jnp.where(jj <= ii, g_row, 0.0), axis=1, keepdims=True)
            dec = _decay(gc_col, ii, jj)
            eg = jnp.exp(gc_col)
            kb, vb = k * beta, v * beta
            k_rows = _rows_pad(_bf(k))
            kk = _dot_nt(_bf(kb), k_rows)
            qk = _dot_nt(_bf(q), k_rows)
            yield
            t, pw = eye, -jnp.where(jj < ii, kk * dec, 0.0)
            for _ in range(6):
                t, pw = _double_step(t, pw)
                yield
            tb = _bf(t)
            u_ref[rows, _head(h)] = _dot(tb, _rows_pad(_bf(vb)))
            w_ref[rows, _head(h)] = _bf(_dot(tb, _rows_pad(_bf(kb * eg))))
            a_ref[h, rows] = _bf(qk * dec)
            t_ref[h, rows] = t
            qg_ref[rows, _head(h)] = _bf(q * eg)
            kd_ref[rows, _head(h)] = _bf(k * jnp.exp(gc_col[C - 1:C, :] - gc_col))
            gcs_parts[chunk].append(jnp.where(lane == h, gc_col, 0.0) + jnp.where(lane == DN_HEADS + h, beta, 0.0)
                                    + jnp.where(lane == 2 * DN_HEADS + h, g_col, 0.0))

        _interleave(head_program(chunk, h) for chunk in range(G) for h in range(DN_HEADS))
        for chunk in range(G):
            gcs_ref[chunk * C:(chunk + 1) * C, :] = sum(gcs_parts[chunk][1:], gcs_parts[chunk][0])

    smem = pl.BlockSpec(memory_space=pltpu.SMEM)
    wide = pl.BlockSpec((R, 512), lambda n: (n, 0))
    sq = pl.BlockSpec((DN_HEADS, R, CPAD), lambda n: (0, n, 0))
    narrow = pl.BlockSpec((R, 128), lambda n: (n, 0))
    f = lambda *shp: jax.ShapeDtypeStruct(shp, F32)
    b = lambda *shp: jax.ShapeDtypeStruct(shp, BF16)
    return pl.pallas_call(
        body, grid=(steps,), name="dn_chunk_fwd",
        in_specs=[smem, smem, pl.BlockSpec((R, 1536), lambda n: (n, 0)), pl.BlockSpec((R, 128), lambda n: (n, BLK_G))],
        out_specs=[wide, wide, wide, wide, sq, sq, narrow],
        out_shape=[b(S, 512), f(S, 512), b(S, 512), b(S, 512), b(DN_HEADS, S, CPAD), f(DN_HEADS, S, CPAD), f(S, 128)],
        compiler_params=_params(("parallel",)),
    )(a_log, dt_bias, qkv, pg)


def _gated_norm(o, z, gn):
    r, oh = _rms_stats(o)
    return oh * gn * (z * _sigmoid(z))


def _dn_scan_fwd(w, u, qg, kd, a, gcs, pz, gn):
    S = w.shape[0]
    C = DN_CHUNK
    nc = S // C
    G = CHUNKS_SCAN
    R = G * C

    def body(w_ref, u_ref, qg_ref, kd_ref, a_ref, gcs_ref, z_ref, gn_ref, o_ref, vn_ref, sst_ref, out_ref, state):
        @pl.when(pl.program_id(0) == 0)
        def _():
            state[...] = jnp.zeros_like(state)

        def head_program(chunk, h):
            hs = _head(h)
            rows = slice(chunk * C, (chunk + 1) * C)
            s_in = state[h]
            sst_ref[chunk, h] = s_in
            sb = _bf(s_in)
            w_s = _dot(w_ref[rows, hs], sb)
            q_s = _dot(qg_ref[rows, hs], sb)
            yield
            vn = u_ref[rows, hs] - w_s
            vnb = _bf(vn)
            o = q_s + _dot(a_ref[h, rows], _rows_pad(vnb))
            k_v = _dot_tn(kd_ref[rows, hs], vnb)
            yield
            state[h] = s_in * jnp.exp(gcs_ref[(chunk + 1) * C - 1:(chunk + 1) * C, h:h + 1]) + k_v
            o_ref[rows, hs] = o
            vn_ref[rows, hs] = vn
            out_ref[rows, hs] = _gated_norm(o, z_ref[rows, hs], gn_ref[...])

        for chunk in range(G):
            _interleave(head_program(chunk, h) for h in range(DN_HEADS))

    wide = pl.BlockSpec((R, 512), lambda n: (n, 0))
    f = lambda *shp: jax.ShapeDtypeStruct(shp, F32)
    return pl.pallas_call(
        body, grid=(nc // G,), name="dn_scan_fwd",
        in_specs=[wide, wide, wide, wide, pl.BlockSpec((DN_HEADS, R, CPAD), lambda n: (0, n, 0)),
                  pl.BlockSpec((R, 128), lambda n: (n, 0)), pl.BlockSpec((R, 512), lambda n: (n, BLK_Z)),
                  pl.BlockSpec((1, DN_DIM), lambda n: (0, 0))],
        out_specs=[wide, wide, pl.BlockSpec((G, DN_HEADS, DN_DIM, DN_DIM), lambda n: (n, 0, 0, 0)), wide],
        out_shape=[f(S, 512), f(S, 512), f(nc, DN_HEADS, DN_DIM, DN_DIM), f(S, 512)],
        scratch_shapes=[pltpu.VMEM((DN_HEADS, DN_DIM, DN_DIM), F32)],
        compiler_params=_params(("arbitrary",)),
    )(w, u, qg, kd, a, gcs, pz, gn)


def _dn_scan_bwd(dcat, o, pz, gn, sst, vnew, w, qg, kd, a, gcs, dproj):
    S = o.shape[0]
    C = DN_CHUNK
    G = CHUNKS_SCAN
    R = G * C
    steps = S // R

    def body(dy_ref, o_ref, z_ref, gn_ref, sst_ref, vn_ref, w_ref, qg_ref, kd_ref, a_ref, gcs_ref, _,
             du_ref, dw_ref, dqg_ref, dkd_ref, da_ref, dz_ref, dsc_ref, dgn_ref, dstate):
        @pl.when(pl.program_id(0) == 0)
        def _():
            dstate[...] = jnp.zeros_like(dstate)
            dgn_ref[...] = jnp.zeros_like(dgn_ref)

        gn_ = gn_ref[...]
        lane = lax.broadcasted_iota(jnp.int32, (C, 128), 1)
        row = lax.broadcasted_iota(jnp.int32, (C, 128), 0)
        dgn_parts = []

        def head_program(chunk, h, dsc_parts):
            hs = _head(h)
            rows = slice(chunk * C, (chunk + 1) * C)
            ov, z, dout = o_ref[rows, hs], z_ref[rows, hs], dy_ref[rows, hs]
            r, oh = _rms_stats(ov)
            sg = _sigmoid(z)
            don = dout * (z * sg)
            dz_ref[rows, hs] = _bf(dout * (oh * gn_) * (sg * (1.0 + z * (1.0 - sg))))
            dgn_parts.append(jnp.sum(don * oh, axis=0, keepdims=True))
            dn = don * gn_
            do = _bf(r * (dn - oh * jnp.mean(dn * oh, axis=-1, keepdims=True)))
            s_in = sst_ref[chunk, h]
            sb = _bf(s_in)
            ds_out = dstate[h]
            dsb = _bf(ds_out)
            vnb = _bf(vn_ref[rows, hs])
            wb, qgb, kdb, ab = w_ref[rows, hs], qg_ref[rows, hs], kd_ref[rows, hs], a_ref[h, rows]
            dvn = _dot_tn(ab, do)[:C] + _dot(kdb, dsb)
            da_ref[h, rows] = _dot_nt(do, _rows_pad(vnb))
            dqg_ref[rows, hs] = _dot_nt(do, sb)
            dkd_ref[rows, hs] = _dot_nt(vnb, dsb)
            q_do = _dot_tn(qgb, do)
            yield
            dvnb = _bf(dvn)
            dw_ref[rows, hs] = _bf(-_dot_nt(dvnb, sb))
            w_dvn = _dot_tn(wb, dvnb)
            du_ref[rows, hs] = dvnb
            yield
            d_last = jnp.exp(gcs_ref[(chunk + 1) * C - 1:(chunk + 1) * C, h:h + 1])
            dd = jnp.sum(jnp.sum(ds_out * s_in, axis=1, keepdims=True), axis=0, keepdims=True)
            dsc_parts.append(jnp.where((lane == h) & (row == C - 1), dd * d_last, 0.0))
            dstate[h] = ds_out * d_last + q_do - w_dvn

        for chunk in reversed(range(G)):
            dsc_parts = []
            _interleave(head_program(chunk, h, dsc_parts) for h in range(DN_HEADS))
            dsc_ref[chunk * C:(chunk + 1) * C, :] = sum(dsc_parts[1:], dsc_parts[0])
        dgn_ref[...] += sum(dgn_parts[1:], dgn_parts[0])

    rev = lambda n: steps - 1 - n
    wide = pl.BlockSpec((R, 512), lambda n: (rev(n), 0))
    z_spec = pl.BlockSpec((R, 512), lambda n: (rev(n), BLK_Z))
    sq = pl.BlockSpec((DN_HEADS, R, CPAD), lambda n: (0, rev(n), 0))
    narrow = pl.BlockSpec((R, 128), lambda n: (rev(n), 0))
    gn_spec = pl.BlockSpec((1, DN_DIM), lambda n: (0, 0))
    f = lambda *shp: jax.ShapeDtypeStruct(shp, F32)
    b = lambda *shp: jax.ShapeDtypeStruct(shp, BF16)
    return pl.pallas_call(
        body, grid=(steps,), name="dn_scan_bwd",
        in_specs=[pl.BlockSpec((R, 512), lambda n: (rev(n), 1)), wide, z_spec, gn_spec,
                  pl.BlockSpec((G, DN_HEADS, DN_DIM, DN_DIM), lambda n: (rev(n), 0, 0, 0)),
                  wide, wide, wide, wide, sq, narrow, pl.BlockSpec(memory_space=pl.ANY)],
        out_specs=[wide, wide, wide, wide, sq, z_spec, narrow, gn_spec],
        out_shape=[b(S, 512), b(S, 512), f(S, 512), f(S, 512), f(DN_HEADS, S, CPAD),
                   jax.ShapeDtypeStruct(dproj.shape, dproj.dtype), f(S, 128), f(1, DN_DIM)],
        scratch_shapes=[pltpu.VMEM((DN_HEADS, DN_DIM, DN_DIM), F32)],
        input_output_aliases={11: 5},
        compiler_params=_params(("arbitrary",)),
    )(dcat, o, pz, gn, sst, vnew, w, qg, kd, a, gcs, dproj)


def _dn_chunk_bwd(qkv, pg, t_inv, gcs, du, dw, dqg, dkd, da, dsc, a_log, dt_bias, dproj):
    S = qkv.shape[0]
    C = DN_CHUNK
    G = CHUNKS_LOCAL
    R = G * C

    def body(alog_ref, dtb_ref, qkv_ref, pg_ref, t_ref, gcs_ref, du_ref, dw_ref, dqg_ref, dkd_ref, da_ref, dsc_ref, _,
             dqkv_ref, dpg_ref, acc_ref):
        @pl.when(pl.program_id(0) == 0)
        def _():
            acc_ref[...] = jnp.zeros_like(acc_ref)

        ii, jj = _chunk_masks()
        lane = lax.broadcasted_iota(jnp.int32, (1, 128), 1)
        row8 = lax.broadcasted_iota(jnp.int32, (8, 128), 0)
        lane8 = lax.broadcasted_iota(jnp.int32, (8, 128), 1)
        rowc = lax.broadcasted_iota(jnp.int32, (C, 1), 0)
        tril, strict = jj <= ii, jj < ii
        dpg_parts, acc_parts = [[] for _ in range(G)], []

        def head_program(chunk, h):
            rows = slice(chunk * C, (chunk + 1) * C)
            q, k, v = qkv_ref[rows, _head(h)], qkv_ref[rows, _head(DN_HEADS + h)], qkv_ref[rows, _head(2 * DN_HEADS + h)]
            gc_col, beta, g_col = gcs_ref[rows, h:h + 1], gcs_ref[rows, DN_HEADS + h:DN_HEADS + h + 1], \
                gcs_ref[rows, 2 * DN_HEADS + h:2 * DN_HEADS + h + 1]
            dec = _decay(gc_col, ii, jj)
            eg = jnp.exp(gc_col)
            g_last = gc_col[C - 1:C, :]
            ek = jnp.exp(g_last - gc_col)
            kb, vb = k * beta, v * beta
            kbg = kb * eg
            qb, kbb = _bf(q), _bf(kb)
            k_rows = _rows_pad(_bf(k))
            t = t_ref[h, rows]
            tb = _bf(t)
            dub, dwb = du_ref[rows, _head(h)], dw_ref[rows, _head(h)]
            dqg_, dkd_ = dqg_ref[rows, _head(h)], dkd_ref[rows, _head(h)]
            dt = _dot_nt(dub, _rows_pad(_bf(vb))) + _dot_nt(dwb, _rows_pad(_bf(kbg)))
            t_du_dw = _dot_tn(tb, jnp.concatenate([dub, dwb], axis=1))
            dvb, dkbg = t_du_dw[:C, :DN_DIM], t_du_dw[:C, DN_DIM:]
            kk = _dot_nt(kbb, k_rows)
            qk = _dot_nt(qb, k_rows)
            yield
            dt_t = _dot3_nt(dt, t)
            yield
            dl = -_dot3_tn(t, dt_t)
            yield
            dm = jnp.where(strict, dl * dec, 0.0)
            dqk = jnp.where(tril, da_ref[h, rows] * dec, 0.0)
            gmat = dm * kk + dqk * qk
            dgc = jnp.sum(gmat, axis=1, keepdims=True) - _row_to_col(jnp.sum(gmat, axis=0, keepdims=True), ii, jj)
            dmb, dqkb = _bf(dm), _bf(dqk)
            dkb = _dot(dmb, k_rows) + dkbg * eg
            dk = _dot_tn(jnp.concatenate([dmb, dqkb], axis=0), jnp.concatenate([kbb, qb], axis=0))[:C] + dkd_ * ek
            dq = _dot(dqkb, k_rows) + dqg_ * eg
            yield
            tk = jnp.sum(dkd_ * k * ek, axis=1, keepdims=True)
            dgc = dgc + jnp.sum(dqg_ * q * eg, axis=1, keepdims=True) - tk + jnp.sum(dkbg * kbg, axis=1, keepdims=True)
            dgl = jnp.sum(tk, axis=0, keepdims=True) + dsc_ref[(chunk + 1) * C - 1:(chunk + 1) * C, h:h + 1]
            dgc = dgc + jnp.where(rowc == C - 1, dgl, 0.0)
            dk = dk + dkb * beta
            dbeta = jnp.sum(dkb * k, axis=1, keepdims=True) + jnp.sum(dvb * v, axis=1, keepdims=True)
            dqkv_ref[rows, _head(h)] = dq
            dqkv_ref[rows, _head(DN_HEADS + h)] = dk
            dqkv_ref[rows, _head(2 * DN_HEADS + h)] = dvb * beta
            dg_col = jnp.sum(jnp.where(jj >= ii, _col_to_row(dgc, ii, jj), 0.0), axis=1, keepdims=True)
            db = dbeta * beta * (1.0 - beta)
            da_in = dg_col * (-jnp.exp(alog_ref[h])) * _sigmoid(pg_ref[rows, DN_HEADS + h:DN_HEADS + h + 1] + dtb_ref[h])
            dpg_parts[chunk].append(jnp.where(lane == h, db, 0.0) + jnp.where(lane == DN_HEADS + h, da_in, 0.0))
            acc_parts.append(jnp.where((row8 == 0) & (lane8 == h), jnp.sum(dg_col * g_col, axis=0, keepdims=True), 0.0)
                             + jnp.where((row8 == 1) & (lane8 == h), jnp.sum(da_in, axis=0, keepdims=True), 0.0))

        _interleave(head_program(chunk, h) for chunk in range(G) for h in range(DN_HEADS))
        for chunk in range(G):
            dpg = sum(dpg_parts[chunk][1:], dpg_parts[chunk][0])
            dpg_ref[chunk * C:(chunk + 1) * C, :] = _bf(jnp.concatenate([dpg, jnp.zeros_like(dpg)], axis=1))
        acc_ref[...] += sum(acc_parts[1:], acc_parts[0])

    smem = pl.BlockSpec(memory_space=pltpu.SMEM)
    wide = pl.BlockSpec((R, 512), lambda n: (n, 0))
    sq = pl.BlockSpec((DN_HEADS, R, CPAD), lambda n: (0, n, 0))
    narrow = pl.BlockSpec((R, 128), lambda n: (n, 0))
    qkv_spec = pl.BlockSpec((R, 1536), lambda n: (n, 0))
    f = lambda *shp: jax.ShapeDtypeStruct(shp, F32)
    return pl.pallas_call(
        body, grid=(S // R,), name="dn_chunk_bwd",
        in_specs=[smem, smem, qkv_spec, pl.BlockSpec((R, 128), lambda n: (n, BLK_G)), sq, narrow, wide, wide, wide, wide, sq,
                  narrow, pl.BlockSpec(memory_space=pl.ANY)],
        out_specs=[qkv_spec, pl.BlockSpec((R, 256), lambda n: (n, BLK_G_PAD)), pl.BlockSpec((8, 128), lambda n: (0, 0))],
        out_shape=[f(S, 1536), jax.ShapeDtypeStruct(dproj.shape, dproj.dtype), f(8, 128)],
        input_output_aliases={12: 1},
        compiler_params=_params(("arbitrary",)),
    )(a_log, dt_bias, qkv, pg, t_inv, gcs, du, dw, dqg, dkd, da, dsc, dproj)


def _fill_kv(dk, dv, dproj):
    S = dk.shape[0]
    tm = min(512, S)

    def body(dk_ref, dv_ref, _, o_ref):
        o_ref[...] = _bf(jnp.concatenate([dk_ref[...], dv_ref[...]], axis=1))

    tile = pl.BlockSpec((tm, 128), lambda i: (i, 0))
    return pl.pallas_call(
        body, grid=(S // tm,), name="fill_kv",
        in_specs=[tile, tile, pl.BlockSpec(memory_space=pl.ANY)],
        out_specs=pl.BlockSpec((tm, 256), lambda i: (i, BLK_KV)),
        out_shape=jax.ShapeDtypeStruct(dproj.shape, dproj.dtype),
        input_output_aliases={2: 0},
        compiler_params=_params(("parallel",)),
    )(dk, dv, dproj)


def _w_in_to_internal(wt):
    return jnp.concatenate([wt[0:512], wt[2304:2816], wt[768:2304], wt[512:768], wt[2816:2824],
                            jnp.zeros((D_IN_PAD - D_IN, wt.shape[1]), wt.dtype)], axis=0)


def _w_in_from_internal(gt):
    return jnp.concatenate([gt[0:512], gt[2560:2816], gt[1024:2560], gt[512:1024], gt[2816:2824]], axis=0)


def _local_step(x, p, target, wts, first_weights, other_weights, ship_early):
    S = x.shape[0]
    cos, sin = _rope_tables(S)
    sinks, a_log, dt_bias = wts["sinks"].reshape(8), wts["a_log"].reshape(4), wts["dt_bias"].reshape(4)
    gn = wts["dn_norm"].reshape(1, DN_DIM)
    add = lambda acc, res: (acc + res,)

    u = _rmsnorm_fwd(x, wts["norm_mix"], "norm_mix_fwd")
    w_in_t, conv_w = first_weights(u)
    proj, = _mm(u, w_in_t, form="nt", name="in_proj", out_dtypes=[F32], tn=512)
    attn = _attn_fwd(proj, cos, sin, sinks)
    qkv = _dn_prep_fwd(proj, conv_w)
    cw, cu, cqg, ckd, ca, ct, gcs = _dn_chunk_fwd(qkv, proj, a_log, dt_bias)
    o, vnew, sst, dn_out = _dn_scan_fwd(cw, cu, cqg, ckd, ca, gcs, proj, gn)
    w_o, = other_weights(("w_o",), dn_out)
    h1, = _mm([attn, dn_out], w_o, form="nn", name="out_proj", out_dtypes=[F32], tn=512, epi=add, extra=[x])

    def relu2(acc):
        r = jnp.maximum(acc, 0.0)
        return r * r, r

    w_up, = other_weights(("w_up",), h1)
    hid, relu, m = _mm(h1, w_up, form="nn", name="mlp_up", out_dtypes=[BF16, BF16], tn=512, epi=relu2, norm=wts["norm_mlp"])
    w_down, = other_weights(("w_down",), hid)
    h2, = _mm(hid, w_down, form="nn", name="mlp_down", out_dtypes=[F32], tn=512, epi=add, extra=[h1])
    w_pg, w_pp = other_weights(("w_ple_gate", "w_ple_proj"), h2)
    n3, dh3, dgl, dpp, loss, d_norm_final = _ple_and_loss(h2, p, target, w_pg, w_pp, wts["norm_ple"],
                                                         wts["norm_final"].reshape(1, D_MODEL))
    g = {"norm_final": d_norm_final}
    early = {"w_ple_gate": _mm_tn(n3, dgl, name="d_w_ple_gate", tm=512, tn=1024, out_dtype=BF16).reshape(N_DEV, 128, 1024),
             "w_ple_proj": _mm_tn(p, dpp, name="d_w_ple_proj", tm=256, tn=128, out_dtype=BF16, column_shards=True)}
    dh2, g["norm_ple"] = _mm(dgl, w_pg, form="nt", name="d_n3", out_dtypes=[F32], tn=512,
                             norm_bwd=(h2, wts["norm_ple"], dh3))
    d_act, = _mm(dh2, w_down, form="nt", name="d_hidden", out_dtypes=[BF16], tn=512,
                 epi=lambda acc, r: (acc * (2.0 * r.astype(F32)),), extra=[relu])
    early["w_down"] = _mm_tn(hid, dh2, name="d_w_down", tm=512, tn=1024, out_dtype=BF16).reshape(N_DEV, 512, 1024)
    early["w_up"] = _mm_tn(m, d_act, name="d_w_up", tm=1024, tn=512, out_dtype=BF16, column_shards=True)
    token = ship_early(early)
    dh1, g["norm_mlp"] = _mm(d_act, w_up, form="nt", name="d_m", out_dtypes=[F32], tn=512, after=token,
                             norm_bwd=(h1, wts["norm_mlp"], dh2))
    dcat, = _mm(dh1, w_o, form="nt", name="d_cat", out_dtypes=[F32], tn=512)
    d_w_o = jnp.concatenate([_mm_tn(attn, dh1, name="d_w_o_attn", tm=512, tn=512, out_dtype=BF16),
                             _mm_tn(dn_out, dh1, name="d_w_o_dn", tm=512, tn=512, out_dtype=BF16)], axis=0)
    token = ship_early({"w_o": d_w_o.reshape(N_DEV, 128, 1024)})
    dproj, dk, dv, dsinks = _attn_bwd(proj, cos, sin, sinks + token[0, 0], dcat)
    g["sinks"] = dsinks[:, 0].reshape(1, 8)
    du_, dw_, dqg, dkd, da, dproj, dsc, g["dn_norm"] = _dn_scan_bwd(dcat, o, proj, gn, sst, vnew, cw, cqg, ckd, ca, gcs, dproj)
    dqkv, dproj, gate_acc = _dn_chunk_bwd(qkv, proj, ct, gcs, du_, dw_, dqg, dkd, da, dsc, a_log, dt_bias, dproj)
    g["a_log"], g["dt_bias"] = gate_acc[0:1, 0:4], gate_acc[1:2, 0:4]
    dproj, g["conv_w"] = _dn_prep_bwd(proj, conv_w, dqkv, dproj)
    dproj = _fill_kv(dk, dv, dproj)
    token = ship_early({"w_in": _mm_tn(dproj, u, name="d_w_in", tm=512, tn=1024)})
    grad_x, g["norm_mix"] = _mm(dproj, w_in_t, form="nn", name="d_u", out_dtypes=[F32], tn=512, after=token,
                                norm_bwd=(x, wts["norm_mix"], dh1))
    return loss, grad_x, g


def _peer(k):
    x, y, c = lax.axis_index("x"), lax.axis_index("y"), lax.axis_index("c")
    px = 1 - x if k & 4 else x
    py = 1 - y if k & 2 else y
    pc = 1 - c if k & 1 else c
    return (px, py, pc), 4 * px + 2 * py + pc


def _exchange(srcs, name, gather):
    n = len(srcs)
    gathers = list(gather) if isinstance(gather, (list, tuple)) else [gather] * n
    shapes = [(N_DEV,) + s.shape if gt else s.shape for s, gt in zip(srcs, gathers)]

    def body(*refs):
        src_refs, out_refs = refs[:n], refs[n:2 * n]
        send_sems, recv_sems, local_sems = refs[2 * n:]
        _, me = _peer(0)
        piece = lambda a, d: src_refs[a] if gathers[a] else src_refs[a].at[d]
        local = [pltpu.make_async_copy(piece(a, me), out_refs[a].at[me], local_sems.at[a]) for a in range(n)]
        for cp in local:
            cp.start()
        copies = []
        for a in range(n):
            for k in range(1, N_DEV):
                dev, idx = _peer(k)
                cp = pltpu.make_async_remote_copy(src_ref=piece(a, idx), dst_ref=out_refs[a].at[me],
                                                  send_sem=send_sems.at[a, k - 1], recv_sem=recv_sems.at[a, k - 1],
                                                  device_id=dev, device_id_type=MESH)
                cp.start()
                copies.append(cp)
        for cp in copies:
            cp.wait_recv()
        for cp in copies:
            cp.wait_send()
        for cp in local:
            cp.wait()

    anywhere = pl.BlockSpec(memory_space=pl.ANY)
    return pl.pallas_call(
        body, name=name, in_specs=[anywhere] * n, out_specs=[anywhere] * n,
        out_shape=[jax.ShapeDtypeStruct(shp, s.dtype) for shp, s in zip(shapes, srcs)],
        scratch_shapes=[pltpu.SemaphoreType.DMA((n, N_DEV - 1)), pltpu.SemaphoreType.DMA((n, N_DEV - 1)),
                        pltpu.SemaphoreType.DMA((n,))],
    )(*srcs)


_HBM = pl.BlockSpec(memory_space=pltpu.HBM)
_SEM = pl.BlockSpec(memory_space=pltpu.SEMAPHORE)
_EFFECT = pltpu.SideEffectType.DATAFLOW_SIDE_EFFECTING


def _split_copies(src_refs, land_refs, send_sems, recv_sems, gather, which=None):
    _, me = _peer(0)
    copies = []
    which = range(len(src_refs)) if which is None else which
    for a, src, land in zip(which, src_refs, land_refs):
        for k in range(1, N_DEV):
            dev, idx = _peer(k)
            sem = a * (N_DEV - 1) + k - 1
            copies.append(pltpu.make_async_remote_copy(
                src_ref=src if gather else src.at[idx], dst_ref=land.at[me], send_sem=send_sems.at[sem],
                recv_sem=recv_sems.at[sem], device_id=dev, device_id_type=MESH))
    return copies


def _exchange_start(srcs, name, gather):
    n = len(srcs)
    me = 4 * lax.axis_index("x") + 2 * lax.axis_index("y") + lax.axis_index("c")
    lands = []
    for s in srcs:
        own = s if gather else lax.dynamic_index_in_dim(s, me, 0, keepdims=False)
        shape = (N_DEV,) + s.shape if gather else s.shape
        lands.append(lax.dynamic_update_index_in_dim(lax.empty(shape, s.dtype), own, me, 0))

    def body(*refs):
        src_refs, land_refs = refs[:n], refs[n:2 * n]
        send_sems, recv_sems = refs[2 * n], refs[2 * n + 1]
        for cp in _split_copies(src_refs, land_refs, send_sems, recv_sems, gather):
            cp.start()
        refs[-1][...] = jnp.zeros_like(refs[-1])

    both = list(srcs) + lands
    sems = pltpu.SemaphoreType.DMA((n * (N_DEV - 1),))
    out = pl.pallas_call(
        body, name=name,
        out_shape=(sems, sems, *[pltpu.HBM(t.shape, t.dtype) for t in both], jax.ShapeDtypeStruct((8, 128), F32)),
        in_specs=[_HBM] * (2 * n), out_specs=(_SEM, _SEM, *[_HBM] * (2 * n), pl.BlockSpec(memory_space=pltpu.VMEM)),
        input_output_aliases={i: 2 + i for i in range(2 * n)},
        compiler_params=pltpu.CompilerParams(has_side_effects=_EFFECT),
    )(*[pltpu.with_memory_space_constraint(t, pltpu.HBM) for t in both])
    return (n, gather, out[:-1]), out[-1]


def _exchange_wait(handle, after, name, which=None):
    n_all, gather, (send_sems, recv_sems, *both_all) = handle
    which = list(range(n_all)) if which is None else list(which)
    n = len(which)
    both = [both_all[a] for a in which] + [both_all[n_all + a] for a in which]

    def body(*refs):
        src_refs, land_refs = refs[:n], refs[n:2 * n]
        for cp in _split_copies(src_refs, land_refs, refs[2 * n], refs[2 * n + 1], gather, which):
            cp.wait_send()
            cp.wait_recv()

    out = pl.pallas_call(
        body, name=name, out_shape=tuple(pltpu.HBM(t.shape, t.dtype) for t in both),
        in_specs=[_HBM] * (2 * n) + [_SEM, _SEM, pl.BlockSpec(memory_space=pl.ANY)], out_specs=tuple([_HBM] * (2 * n)),
        input_output_aliases={i: i for i in range(2 * n)},
        compiler_params=pltpu.CompilerParams(has_side_effects=_EFFECT),
    )(*both, send_sems, recv_sems, after)
    return list(out[n:])


def _adamw(parts, w, m, v, name):
    n, R, W = parts.shape
    tm = 128 if R % 128 == 0 else R

    def body(p_ref, w_ref, m_ref, v_ref, g_ref, d_ref, nm_ref, nv_ref):
        g = p_ref[0].astype(F32)
        for s in range(1, n):
            g = g + p_ref[s].astype(F32)
        nm = ADAM_B1 * m_ref[...] + (1.0 - ADAM_B1) * g
        nv = ADAM_B2 * v_ref[...] + (1.0 - ADAM_B2) * (g * g)
        m_hat = nm / (1.0 - ADAM_B1 ** ADAM_STEP)
        v_hat = nv / (1.0 - ADAM_B2 ** ADAM_STEP)
        g_ref[...] = g
        d_ref[...] = -ADAM_LR * (m_hat / (jnp.sqrt(v_hat) + ADAM_EPS) + ADAM_WD * w_ref[...])
        nm_ref[...] = nm
        nv_ref[...] = nv

    tile = pl.BlockSpec((tm, W), lambda i: (i, 0))
    return pl.pallas_call(
        body, grid=(R // tm,), name=name,
        in_specs=[pl.BlockSpec((n, tm, W), lambda i: (0, i, 0)), tile, tile, tile],
        out_specs=[tile] * 4, out_shape=[jax.ShapeDtypeStruct((R, W), F32)] * 4,
        compiler_params=_params(("parallel",)),
    )(parts, w, m, v)


_MATRICES = ("w_in", "w_o", "w_up", "w_down", "w_ple_gate", "w_ple_proj")


_OTHERS = ("w_o", "w_up", "w_down", "w_ple_gate", "w_ple_proj")


_SMALL_ROWS = 16
_VEC_ROW = {"norm_mix": 0, "norm_mlp": 1, "norm_ple": 2, "norm_final": 3}
_VEC_LANES = {"a_log": (0, 4), "dt_bias": (4, 8), "sinks": (8, 16), "dn_norm": (128, 256)}
_LOSS_ROW, _CONV_ROW = 5, 8


def _pack_small(vals, extra_rows):
    row4 = jnp.zeros((1024,), F32)
    for n, (a, b) in _VEC_LANES.items():
        row4 = row4.at[a:b].set(vals[n].reshape(b - a))
    rows = [vals[n].reshape(1, 1024) for n in ("norm_mix", "norm_mlp", "norm_ple", "norm_final")] + [row4.reshape(1, 1024)]
    return jnp.concatenate(rows + extra_rows, axis=0)


def _unpack_small(buf, like):
    out = {n: buf[r].reshape(like[n].shape) for n, r in _VEC_ROW.items()}
    for n, (a, b) in _VEC_LANES.items():
        out[n] = buf[4, a:b].reshape(like[n].shape)
    return out


_ORDER = ("norm_mix", "w_in", "conv_w", "a_log", "dt_bias", "dn_norm", "sinks", "w_o", "norm_mlp", "w_up", "w_down",
          "norm_ple", "w_ple_gate", "w_ple_proj", "norm_final")


def kernel(x, p, norm_mix, w_in, conv_w, a_log, dt_bias, dn_norm, sinks, w_o, norm_mlp, w_up, w_down, norm_ple, w_ple_gate, w_ple_proj, norm_final, loss_target, m_norm_mix, m_w_in, m_conv_w, m_a_log, m_dt_bias, m_dn_norm, m_sinks, m_w_o, m_norm_mlp, m_w_up, m_w_down, m_norm_ple, m_w_ple_gate, m_w_ple_proj, m_norm_final, v_norm_mix, v_w_in, v_conv_w, v_a_log, v_dt_bias, v_dn_norm, v_sinks, v_w_o, v_norm_mlp, v_w_up, v_w_down, v_norm_ple, v_w_ple_gate, v_w_ple_proj, v_norm_final):
    w = dict(norm_mix=norm_mix, w_in=w_in[0], conv_w=conv_w[0], a_log=a_log, dt_bias=dt_bias, dn_norm=dn_norm, sinks=sinks,
             w_o=w_o[0], norm_mlp=norm_mlp, w_up=w_up[0], w_down=w_down[0], norm_ple=norm_ple, w_ple_gate=w_ple_gate[0],
             w_ple_proj=w_ple_proj[0], norm_final=norm_final)
    m = dict(norm_mix=m_norm_mix, w_in=m_w_in[0], conv_w=m_conv_w[0], a_log=m_a_log, dt_bias=m_dt_bias, dn_norm=m_dn_norm,
             sinks=m_sinks, w_o=m_w_o[0], norm_mlp=m_norm_mlp, w_up=m_w_up[0], w_down=m_w_down[0], norm_ple=m_norm_ple,
             w_ple_gate=m_w_ple_gate[0], w_ple_proj=m_w_ple_proj[0], norm_final=m_norm_final)
    v = dict(norm_mix=v_norm_mix, w_in=v_w_in[0], conv_w=v_conv_w[0], a_log=v_a_log, dt_bias=v_dt_bias, dn_norm=v_dn_norm,
             sinks=v_sinks, w_o=v_w_o[0], norm_mlp=v_norm_mlp, w_up=v_w_up[0], w_down=v_w_down[0], norm_ple=v_norm_ple,
             w_ple_gate=v_w_ple_gate[0], w_ple_proj=v_w_ple_proj[0], norm_final=v_norm_final)
    me = 4 * lax.axis_index("x") + 2 * lax.axis_index("y") + lax.axis_index("c")
    conv_shard = conv_w.shape[2]

    for d in (w, m, v):
        d["w_in"] = d["w_in"].T
    conv_pad = jnp.pad(w["conv_w"], ((0, 8 - DN_CONV), (0, 256 - conv_shard)))
    first, token_first = _exchange_start([_bf(w["w_in"]), conv_pad], "gather_first_start", gather=True)
    later = [_bf(w[n]) for n in _OTHERS]
    later[-1] = _bf(w["w_ple_proj"] + token_first[0:1, 0:1])
    others, token_others = _exchange_start(later, "gather_others_start", gather=True)
    vectors = dict(w)
    vectors["norm_mix"] = w["norm_mix"] + token_others[0:1, 0:1]

    def first_weights(after):
        w_in_all, conv_all = _exchange_wait(first, after, "gather_first_wait")
        conv_all = jnp.transpose(conv_all[:, :DN_CONV, :conv_shard], (1, 0, 2)).reshape(DN_CONV, N_DEV * conv_shard)
        return _w_in_to_internal(w_in_all.reshape(D_IN, D_MODEL)), conv_all

    as_taken = {"w_o": lambda t: t.reshape(1024, 1024), "w_up": lambda t: t, "w_down": lambda t: t.reshape(4096, 1024),
                "w_ple_gate": lambda t: t.reshape(1024, 1024), "w_ple_proj": lambda t: t}

    def other_weights(names, after):
        which = [_OTHERS.index(n) for n in names]
        got = _exchange_wait(others, after, "gather_wait_" + names[0], which)
        return [as_taken[n](t) for n, t in zip(names, got)]

    shipped = []

    def ship_early(pieces):
        names = tuple(pieces)
        if names == ("w_in",):
            pieces = {"w_in": _bf(_w_in_from_internal(pieces["w_in"])).reshape(N_DEV, D_IN // N_DEV, D_MODEL)}
        handle, token = _exchange_start([pieces[n] for n in names], "scatter_start_" + names[0], gather=False)
        shipped.append((names, handle))
        return token

    loss, grad_x, g = _local_step(x[0], p[0, 0], loss_target[0], vectors, first_weights, other_weights, ship_early)

    small = _pack_small(g, [loss[:, :1] * jnp.ones((1, 1024), F32), jnp.zeros((2, 1024), F32),
                            g["conv_w"].reshape(6, 1024), jnp.zeros((2, 1024), F32)])
    small_all, = _exchange([small], "gather_small", gather=True)
    received = {}
    for names, handle in shipped:
        received.update(zip(names, _exchange_wait(handle, grad_x, "scatter_wait_" + names[0])))
    big = {n: _adamw(received[n], w[n], m[n], v[n], "adamw_" + n) for n in _MATRICES}
    zeros16 = jnp.zeros((_SMALL_ROWS, 1024), F32)
    summed = _adamw(small_all, zeros16, zeros16, zeros16, "sum_small")[0]
    conv_g = lax.dynamic_slice(summed[_CONV_ROW:_CONV_ROW + 6].reshape(DN_CONV, N_DEV * conv_shard), (0, me * conv_shard),
                               (DN_CONV, conv_shard))
    pad_conv = lambda t: jnp.pad(t.reshape(1, DN_CONV * conv_shard), ((0, 2), (0, 1024 - DN_CONV * conv_shard)))
    small_g = jnp.concatenate([summed[0:5], pad_conv(conv_g)], axis=0)[None]
    pack8 = lambda d: _pack_small(d, [pad_conv(d["conv_w"])])
    sm = _adamw(small_g, pack8(w), pack8(m), pack8(v), "adamw_vectors")

    outs = []
    for i, small_buf in enumerate(sm):
        d = {n: big[n][i] for n in _MATRICES}
        d.update(_unpack_small(small_buf, w))
        d["conv_w"] = small_buf[5, :DN_CONV * conv_shard].reshape(DN_CONV, conv_shard)
        outs.append(d)
    result = [summed[_LOSS_ROW, 0], grad_x[None]]
    for d in outs:
        d["w_in"] = d["w_in"].T
        for n in _ORDER:
            result.append(d[n][None] if n in _MATRICES or n == "conv_w" else d[n].reshape(w[n].shape))
    return tuple(result)
```

```python
import jax
import jax.numpy as jnp
import numpy as np
from jax import lax
from jax.experimental import pallas as pl
from jax.experimental.pallas import tpu as pltpu

F32, BF16 = jnp.float32, jnp.bfloat16
EPS = 1e-6
D_MODEL = 1024
N_DEV = 8
ATTN_BLOCK = 128
HEAD_PAIR = 128
DN_HEADS = 4
DN_DIM = 128
DN_CHUNK = 64
DN_CONV = 4
ROPE_THETA = 10000.0
D_IN = 2824
D_IN_PAD = 3072
BLK_Q, BLK_Z = 0, 1
BLK_DN, BLK_K, BLK_V, BLK_G = 8, 20, 21, 22
BLK_KV, BLK_G_PAD = 10, 11
VMEM_LIMIT = 56 * 1024 * 1024
NEG = -1e30
ADAM_LR, ADAM_B1, ADAM_B2, ADAM_EPS, ADAM_WD, ADAM_STEP = 0.001, 0.9, 0.999, 1e-08, 0.01, 10
MESH = pl.DeviceIdType.MESH


def _bf(x):
    return x.astype(BF16)


def _dot(a, b):
    return jnp.dot(a, b, preferred_element_type=F32)


def _dot_nt(a, b):
    return lax.dot_general(a, b, (((1,), (1,)), ((), ())), preferred_element_type=F32)


def _dot_tn(a, b):
    return lax.dot_general(a, b, (((0,), (0,)), ((), ())), preferred_element_type=F32)


def _sigmoid(x):
    return 1.0 / (1.0 + jnp.exp(-x))


def _params(sem):
    return pltpu.CompilerParams(dimension_semantics=sem, vmem_limit_bytes=VMEM_LIMIT)


def _mm(x, w, *, form, name, out_dtypes, tn, epi=None, extra=(), tm=512, w_row_block=0, after=None, norm=None,
        norm_bwd=None):
    xs = list(x) if isinstance(x, (list, tuple)) else [x]
    nx = len(xs)
    S, K = xs[0].shape
    shards = w.ndim == 3
    N = (w.shape[2] * N_DEV if shards else w.shape[1]) if form == "nn" else w.shape[-2]
    assert not (shards and form == "nn" and tn != w.shape[2]) and (nx == 1 or (form == "nn" and not shards and norm is None))
    r0 = w_row_block * K
    tm = min(tm, S)
    n_extra, n_out = len(extra), len(out_dtypes)
    tile = lambda width: pl.BlockSpec((tm, width), lambda i: (i, 0))
    whole = lambda a: pl.BlockSpec(a.shape, lambda i, nd=a.ndim: (0,) * nd)
    ins, in_specs = [*xs, w, *extra], [tile(K)] * nx + [whole(w)] + [tile(N)] * n_extra
    if norm is not None:
        ins, in_specs = ins + [norm], in_specs + [whole(norm)]
    if norm_bwd is not None:
        ins, in_specs = ins + list(norm_bwd), in_specs + [tile(N), whole(norm_bwd[1]), tile(N)]
    if after is not None:
        ins, in_specs = ins + [after], in_specs + [whole(after)]
    out_shape = [jax.ShapeDtypeStruct((S, N), dt) for dt in out_dtypes]
    out_specs = [tile(N)] * n_out
    if norm is not None:
        out_shape, out_specs = out_shape + [jax.ShapeDtypeStruct((S, K), BF16)], out_specs + [tile(K)]
    if norm_bwd is not None:
        out_shape, out_specs = out_shape + [jax.ShapeDtypeStruct((1, N), F32)], out_specs + [pl.BlockSpec((1, N), lambda i: (0, 0))]

    def product(xb, w_ref, cols, c):
        if form == "nn" and nx > 1:
            return sum(_dot(part, w_ref[r0 + p * K:r0 + (p + 1) * K, cols]) for p, part in enumerate(xb))
        if form == "nn":
            return _dot(xb, w_ref[c] if shards else w_ref[r0:r0 + K, cols])
        if not shards:
            return _dot_nt(xb, w_ref[cols, :])
        ks = w.shape[2]
        acc = _dot_nt(xb[:, 0:ks], w_ref[0, cols, :])
        for s in range(1, N_DEV):
            acc = acc + _dot_nt(xb[:, s * ks:(s + 1) * ks], w_ref[s, cols, :])
        return acc

    def body(*refs):
        x_ref, w_ref = refs[0], refs[nx]
        extra_refs = refs[nx + 1:nx + 1 + n_extra]
        at = nx + 1 + n_extra
        if norm is not None:
            gain_ref, at = refs[at], at + 1
        if norm_bwd is not None:
            (y_ref, ygain_ref, dres_ref), at = refs[at:at + 3], at + 3
        outs = refs[len(ins):]
        if norm is not None:
            _, xh = _rms_stats(x_ref[...])
            xb = _bf(xh * gain_ref[...])
            outs[n_out][...] = xb
        else:
            xb = _bf(x_ref[...]) if nx == 1 else [_bf(r[...]) for r in refs[:nx]]
        for c in range(N // tn):
            cols = slice(c * tn, (c + 1) * tn)
            acc = product(xb, w_ref, cols, c)
            res = epi(acc, *[r[:, cols] for r in extra_refs]) if epi else (acc,)
            for o, r in zip(outs[:n_out], res):
                o[:, cols] = r.astype(o.dtype)
        if norm_bwd is not None:
            dx, dg = _rms_bwd_tile(y_ref[...], ygain_ref[...], outs[0][...])
            outs[0][...] = dres_ref[...] + dx
            dg_ref = outs[-1]

            @pl.when(pl.program_id(0) == 0)
            def _():
                dg_ref[...] = jnp.zeros_like(dg_ref)

            dg_ref[...] += dg

    return pl.pallas_call(
        body, grid=(S // tm,), name=name, in_specs=in_specs, out_specs=out_specs, out_shape=out_shape,
        compiler_params=_params(("arbitrary",) if norm_bwd is not None else ("parallel",)),
    )(*ins)


def _mm_tn(x, dy, *, name, tm, tn, out_dtype=F32, column_shards=False):
    S, K = x.shape
    N = dy.shape[1]

    def body(x_ref, dy_ref, o_ref):
        o_ref[...] = _dot_tn(_bf(x_ref[...]), _bf(dy_ref[...])).astype(out_dtype)

    if column_shards:
        out_spec = pl.BlockSpec((None, tm, tn), lambda i, j: (j, i, 0))
        out_shape = jax.ShapeDtypeStruct((N // tn, K, tn), out_dtype)
    else:
        out_spec = pl.BlockSpec((tm, tn), lambda i, j: (i, j))
        out_shape = jax.ShapeDtypeStruct((K, N), out_dtype)
    return pl.pallas_call(
        body, grid=(K // tm, N // tn), name=name,
        in_specs=[pl.BlockSpec((S, tm), lambda i, j: (0, i)), pl.BlockSpec((S, tn), lambda i, j: (0, j))],
        out_specs=out_spec, out_shape=out_shape,
        compiler_params=_params(("parallel", "parallel")),
    )(x, dy)


def _rowwise(body, *, tiled, full, out_tiled, out_acc, name, tm=512, smem=()):
    S = tiled[0].shape[0]
    tm = min(tm, S)
    n_in = len(smem) + len(tiled) + len(full)

    def kern(*refs):
        @pl.when(pl.program_id(0) == 0)
        def _():
            for r in refs[n_in + len(out_tiled):]:
                r[...] = jnp.zeros_like(r)
        body(*refs)

    in_specs = [pl.BlockSpec(memory_space=pltpu.SMEM) for _ in smem]
    in_specs += [pl.BlockSpec((tm, a.shape[1]), lambda i: (i, 0)) for a in tiled]
    in_specs += [pl.BlockSpec(a.shape, lambda i, nd=a.ndim: (0,) * nd) for a in full]
    out_specs = [pl.BlockSpec((tm, w), lambda i: (i, 0)) for w, _ in out_tiled]
    out_specs += [pl.BlockSpec(shp, lambda i, nd=len(shp): (0,) * nd) for shp, _ in out_acc]
    out_shape = [jax.ShapeDtypeStruct((S, w), dt) for w, dt in out_tiled]
    out_shape += [jax.ShapeDtypeStruct(shp, dt) for shp, dt in out_acc]
    return pl.pallas_call(
        kern, grid=(S // tm,), name=name, in_specs=in_specs, out_specs=out_specs, out_shape=out_shape,
        compiler_params=_params(("arbitrary",)),
    )(*smem, *tiled, *full)


def _rms_stats(x):
    r = lax.rsqrt(jnp.mean(x * x, axis=-1, keepdims=True) + EPS)
    return r, x * r


def _rmsnorm_fwd(x, g, name):
    def body(x_ref, g_ref, o_ref):
        _, xh = _rms_stats(x_ref[...])
        o_ref[...] = _bf(xh * g_ref[...])

    return _rowwise(body, tiled=[x], full=[g], out_tiled=[(x.shape[1], BF16)], out_acc=[], name=name)[0]


def _rms_bwd_tile(x, g, dxn):
    r, xh = _rms_stats(x)
    dg = jnp.sum(dxn * xh, axis=0, keepdims=True)
    dn = dxn * g
    dx = r * (dn - xh * jnp.mean(dn * xh, axis=-1, keepdims=True))
    return dx, dg


def _ple_and_loss(h2, p, target, w_pg, w_pp, g_ple, g_final):
    S, n = h2.shape
    tm = min(512, S)
    tn = 512

    def body(h2_ref, p_ref, t_ref, wpg_ref, wpp_ref, gple_ref, gfin_ref,
             n3_ref, dh_ref, dgl_ref, dpp_ref, loss_ref, dg_ref, pp, gate, h3):
        @pl.when(pl.program_id(0) == 0)
        def _():
            loss_ref[...] = jnp.zeros_like(loss_ref)
            dg_ref[...] = jnp.zeros_like(dg_ref)

        x = h2_ref[...]
        _, xh = _rms_stats(x)
        n3 = _bf(xh * gple_ref[...])
        n3_ref[...] = n3
        pb = _bf(p_ref[...])
        for c in range(n // tn):
            cols = slice(c * tn, (c + 1) * tn)
            pp[:, cols] = _dot(pb, wpp_ref[:, cols])
            gt = _sigmoid(_dot(n3, wpg_ref[:, cols]))
            gate[:, cols] = gt
            h3[:, cols] = x[:, cols] + gt * pp[:, cols]
        y = h3[...]
        _, yh = _rms_stats(y)
        e = yh * gfin_ref[...] - t_ref[...]
        per_tok = jnp.mean(e * e, axis=-1, keepdims=True)
        loss_ref[...] += 0.5 * jnp.sum(per_tok, axis=0, keepdims=True)
        dh, dg = _rms_bwd_tile(y, gfin_ref[...], e * (1.0 / n))
        dh_ref[...] = dh
        dg_ref[...] += dg
        gt = gate[...]
        dgl_ref[...] = _bf(dh * pp[...] * gt * (1.0 - gt))
        dpp_ref[...] = _bf(dh * gt)

    tile = lambda width: pl.BlockSpec((tm, width), lambda i: (i, 0))
    whole = lambda a: pl.BlockSpec(a.shape, lambda i, nd=a.ndim: (0,) * nd)
    return pl.pallas_call(
        body, grid=(S // tm,), name="ple_and_loss",
        in_specs=[tile(n), tile(p.shape[1]), tile(n), whole(w_pg), whole(w_pp), whole(g_ple), whole(g_final)],
        out_specs=[tile(n), tile(n), tile(n), tile(n), pl.BlockSpec((1, 128), lambda i: (0, 0)), pl.BlockSpec((1, n), lambda i: (0, 0))],
        out_shape=[jax.ShapeDtypeStruct((S, n), BF16), jax.ShapeDtypeStruct((S, n), F32), jax.ShapeDtypeStruct((S, n), BF16),
                   jax.ShapeDtypeStruct((S, n), BF16), jax.ShapeDtypeStruct((1, 128), F32), jax.ShapeDtypeStruct((1, n), F32)],
        scratch_shapes=[pltpu.VMEM((tm, n), F32)] * 3,
        compiler_params=_params(("arbitrary",)),
    )(h2, p, target, w_pg, w_pp, g_ple, g_final)


def _rope_tables(S):
    half = 32
    inv = (1.0 / (np.float32(ROPE_THETA) ** (np.arange(half, dtype=np.float32) * np.float32(2.0 / 64)))).astype(np.float32)
    ang = np.arange(S).astype(np.float32)[:, None] * inv[None, :]
    cos, sin = np.cos(ang), np.sin(ang)
    return jnp.asarray(np.tile(cos, (1, 4))), jnp.asarray(np.concatenate([-sin, sin, -sin, sin], axis=1))


def _attn_common(i, kc, kp, vc, vp, cc, sc, cp, sp):
    lane = lax.broadcasted_iota(jnp.int32, (1, HEAD_PAIR), 1)
    lane_lo = jnp.bitwise_and(lane, 63) < 32
    slot = [lane < 64, lane >= 64]

    def swap_halves(t):
        return jnp.where(lane_lo, pltpu.roll(t, 96, 1), pltpu.roll(t, 32, 1))

    def rope(t, cos, sin):
        return t * cos + swap_halves(t) * sin

    def unrope(d, cos, sin):
        return d * cos + swap_halves(d * sin)

    k2 = jnp.concatenate([rope(kp, cp, sp), rope(kc, cc, sc)], axis=0)
    v2 = jnp.concatenate([vp, vc], axis=0)
    r = lax.broadcasted_iota(jnp.int32, (ATTN_BLOCK, 2 * ATTN_BLOCK), 0)
    c = lax.broadcasted_iota(jnp.int32, (ATTN_BLOCK, 2 * ATTN_BLOCK), 1)
    valid = (c > r) & (c <= r + ATTN_BLOCK) & jnp.logical_or(c >= ATTN_BLOCK, i > 0)
    ks, vs = {}, {}
    for j in range(2):
        kn = jnp.where(slot[j], k2, 0.0)
        vn = jnp.where(slot[j], v2, 0.0)
        for s in range(2):
            ks[j, s] = _bf(kn if s == j else pltpu.roll(kn, 64, 1))
            vs[j, s] = _bf(vn if s == j else pltpu.roll(vn, 64, 1))
    return slot, rope, unrope, valid, ks, vs


def _attn_probs(scores, valid, sink):
    s = jnp.where(valid, scores * 0.125, NEG)
    m = jnp.maximum(jnp.max(s, axis=1, keepdims=True), sink)
    e = jnp.exp(s - m)
    inv_z = 1.0 / (jnp.sum(e, axis=1, keepdims=True) + jnp.exp(sink - m))
    return e * inv_z, jnp.exp(sink - m) * inv_z


def _attn_specs(S):
    nb = S // ATTN_BLOCK
    prev = lambda i: jnp.maximum(i - 1, 0)
    blk = lambda w, col, row=(lambda i: i): pl.BlockSpec((ATTN_BLOCK, w), lambda i: (row(i), col))
    in_specs = [pl.BlockSpec(memory_space=pltpu.SMEM),
                blk(512, BLK_Q), blk(128, BLK_K), blk(128, BLK_K, prev), blk(128, BLK_V), blk(128, BLK_V, prev),
                blk(128, 0), blk(128, 0), blk(128, 0, prev), blk(128, 0, prev)]
    return nb, in_specs


def _attn_fwd(pa, cos, sin, sinks):
    S = pa.shape[0]
    nb, in_specs = _attn_specs(S)

    def body(sinks_ref, q_ref, kc_ref, kp_ref, vc_ref, vp_ref, cc_ref, sc_ref, cp_ref, sp_ref, o_ref):
        i = pl.program_id(0)
        cc, sc = cc_ref[...], sc_ref[...]
        _, rope, _, valid, ks, vs = _attn_common(i, kc_ref[...], kp_ref[...], vc_ref[...], vp_ref[...],
                                                 cc, sc, cp_ref[...], sp_ref[...])
        pair_cols = [slice(HEAD_PAIR * pair, HEAD_PAIR * (pair + 1)) for pair in range(4)]
        qps = [_bf(rope(q_ref[:, cols], cc, sc)) for cols in pair_cols]
        outs = {}

        def head_program(h):
            pair, s = divmod(h, 2)
            j = h // 4
            scores = _dot_nt(qps[pair], ks[j, s])
            yield
            p, _ = _attn_probs(scores, valid, sinks_ref[h])
            outs[h] = _dot(_bf(p), vs[j, s])

        _interleave(head_program(h) for h in range(8))
        for pair, cols in enumerate(pair_cols):
            o_ref[:, cols] = outs[2 * pair] + outs[2 * pair + 1]

    return pl.pallas_call(
        body, grid=(nb,), name="attn_fwd", in_specs=in_specs,
        out_specs=pl.BlockSpec((ATTN_BLOCK, 512), lambda i: (i, 0)),
        out_shape=jax.ShapeDtypeStruct((S, 512), F32),
        compiler_params=_params(("parallel",)),
    )(sinks, pa, pa, pa, pa, pa, cos, sin, cos, sin)


def _attn_bwd(pa, cos, sin, sinks, dcat):
    S = pa.shape[0]
    nb, in_specs = _attn_specs(S)
    in_specs = in_specs + [pl.BlockSpec((ATTN_BLOCK, 512), lambda i: (i, 0))]

    def body(sinks_ref, q_ref, kc_ref, kp_ref, vc_ref, vp_ref, cc_ref, sc_ref, cp_ref, sp_ref, do_ref,
             dq_ref, dk_ref, dv_ref, dsink_ref):
        i = pl.program_id(0)

        @pl.when(i == 0)
        def _():
            dk_ref[...] = jnp.zeros_like(dk_ref)
            dv_ref[...] = jnp.zeros_like(dv_ref)
            dsink_ref[...] = jnp.zeros_like(dsink_ref)

        cc, sc, cp, sp = cc_ref[...], sc_ref[...], cp_ref[...], sp_ref[...]
        slot, rope, unrope, valid, ks, vs = _attn_common(i, kc_ref[...], kp_ref[...], vc_ref[...], vp_ref[...], cc, sc, cp, sp)
        pair_cols = [slice(HEAD_PAIR * pair, HEAD_PAIR * (pair + 1)) for pair in range(4)]
        qps = [_bf(rope(q_ref[:, cols], cc, sc)) for cols in pair_cols]
        dobs = [_bf(do_ref[:, cols]) for cols in pair_cols]
        dqs, dks, dvs = {}, {}, {}

        def head_program(h):
            pair, s = divmod(h, 2)
            j = h // 4
            qp, dob = qps[pair], dobs[pair]
            scores = _dot_nt(qp, ks[j, s])
            dp = _dot_nt(dob, vs[j, s])
            yield
            p, p_sink = _attn_probs(scores, valid, sinks_ref[h])
            dr = jnp.sum(p * dp, axis=1, keepdims=True)
            ds = _bf(p * (dp - dr) * 0.125)
            dsink_ref[h:h + 1, :] += -jnp.sum(p_sink * dr, axis=0, keepdims=True)
            dqs[h] = _dot(ds, ks[j, s])
            dk_h = _dot_tn(ds, qp)
            dv_h = _dot_tn(_bf(p), dob)
            yield
            dk_h, dv_h = jnp.where(slot[s], dk_h, 0.0), jnp.where(slot[s], dv_h, 0.0)
            if s != j:
                dk_h, dv_h = pltpu.roll(dk_h, 64, 1), pltpu.roll(dv_h, 64, 1)
            dks[h], dvs[h] = dk_h, dv_h

        _interleave(head_program(h) for h in range(8))
        dk2 = sum((dks[h] for h in range(1, 8)), dks[0])
        dv2 = sum((dvs[h] for h in range(1, 8)), dvs[0])
        for pair, cols in enumerate(pair_cols):
            dq_ref[:, cols] = _bf(unrope(dqs[2 * pair] + dqs[2 * pair + 1], cc, sc))
        cur = pl.ds(pl.multiple_of(i * ATTN_BLOCK, ATTN_BLOCK), ATTN_BLOCK)
        dk_ref[cur, :] += unrope(dk2[ATTN_BLOCK:], cc, sc)
        dv_ref[cur, :] += dv2[ATTN_BLOCK:]

        @pl.when(i > 0)
        def _():
            prv = pl.ds(pl.multiple_of((i - 1) * ATTN_BLOCK, ATTN_BLOCK), ATTN_BLOCK)
            dk_ref[prv, :] += unrope(dk2[:ATTN_BLOCK], cp, sp)
            dv_ref[prv, :] += dv2[:ATTN_BLOCK]

    whole = lambda w: pl.BlockSpec((S, w), lambda i: (0, 0))
    return pl.pallas_call(
        body, grid=(nb,), name="attn_bwd", in_specs=in_specs,
        out_specs=[pl.BlockSpec((ATTN_BLOCK, 512), lambda i: (i, BLK_Q)), whole(128), whole(128),
                   pl.BlockSpec((8, 128), lambda i: (0, 0))],
        out_shape=[jax.ShapeDtypeStruct((S, D_IN_PAD), BF16), jax.ShapeDtypeStruct((S, 128), F32),
                   jax.ShapeDtypeStruct((S, 128), F32), jax.ShapeDtypeStruct((8, 128), F32)],
        compiler_params=_params(("arbitrary",)),
    )(sinks, pa, pa, pa, pa, pa, cos, sin, cos, sin, dcat)


CONV_ROWS = 512
CONV_PAD = 8


def _conv_silu(scr, w, r0):
    y = w[3:4, :] * scr[pl.ds(CONV_PAD + r0, CONV_ROWS), :]
    for j in range(DN_CONV - 1):
        y = y + w[j:j + 1, :] * scr[pl.ds(CONV_PAD + r0 - 3 + j, CONV_ROWS), :]
    return y


def _dn_prep_fwd(pd, conv_w):
    S = pd.shape[0]
    assert S % CONV_ROWS == 0

    def body(x_ref, w_ref, o_ref, scr):
        b = pl.program_id(0)
        scr[0:CONV_PAD, :] = jnp.zeros((CONV_PAD, DN_DIM), F32)
        scr[pl.ds(CONV_PAD, S), :] = x_ref[...]
        w = w_ref[...]
        q_scale = jnp.where(b < DN_HEADS, DN_DIM ** -0.5, 1.0)
        for r0 in range(0, S, CONV_ROWS):
            y = _conv_silu(scr, w, r0)
            a = y * _sigmoid(y)
            rs = lax.rsqrt(jnp.sum(a * a, axis=1, keepdims=True) + EPS)
            o_ref[pl.ds(r0, CONV_ROWS), :] = a * jnp.where(b < 2 * DN_HEADS, rs * q_scale, 1.0)

    col = pl.BlockSpec((S, DN_DIM), lambda b: (0, b))
    return pl.pallas_call(
        body, grid=(3 * DN_HEADS,), name="dn_prep_fwd",
        in_specs=[pl.BlockSpec((S, DN_DIM), lambda b: (0, BLK_DN + b)), pl.BlockSpec((DN_CONV, DN_DIM), lambda b: (0, b))],
        out_specs=col,
        out_shape=jax.ShapeDtypeStruct((S, 3 * DN_HEADS * DN_DIM), F32),
        scratch_shapes=[pltpu.VMEM((S + CONV_PAD, DN_DIM), F32)],
        compiler_params=_params(("parallel",)),
    )(pd, conv_w)


def _dn_prep_bwd(pd, conv_w, dqkv, dproj):
    S = pd.shape[0]

    def body(x_ref, w_ref, d_ref, _, dx_ref, dw_ref, scr, dscr):
        b = pl.program_id(0)
        scr[0:CONV_PAD, :] = jnp.zeros((CONV_PAD, DN_DIM), F32)
        scr[pl.ds(CONV_PAD, S), :] = x_ref[...]
        dscr[pl.ds(S, CONV_PAD), :] = jnp.zeros((CONV_PAD, DN_DIM), F32)
        w = w_ref[...]
        q_scale = jnp.where(b < DN_HEADS, DN_DIM ** -0.5, 1.0)
        is_qk = b < 2 * DN_HEADS
        dw = [jnp.zeros((1, DN_DIM), F32) for _ in range(DN_CONV)]
        for r0 in range(0, S, CONV_ROWS):
            y = _conv_silu(scr, w, r0)
            sg = _sigmoid(y)
            a = y * sg
            dout = d_ref[pl.ds(r0, CONV_ROWS), :]
            rs = lax.rsqrt(jnp.sum(a * a, axis=1, keepdims=True) + EPS)
            da_qk = q_scale * rs * (dout - a * (rs * rs) * jnp.sum(dout * a, axis=1, keepdims=True))
            dy = jnp.where(is_qk, da_qk, dout) * (sg * (1.0 + y * (1.0 - sg)))
            dscr[pl.ds(r0, CONV_ROWS), :] = dy
            for j in range(DN_CONV):
                dw[j] = dw[j] + jnp.sum(dy * scr[pl.ds(CONV_PAD + r0 - 3 + j, CONV_ROWS), :], axis=0, keepdims=True)
        for j in range(DN_CONV):
            dw_ref[j:j + 1, :] = dw[j]
        for r0 in range(0, S, CONV_ROWS):
            dx = w[3:4, :] * dscr[pl.ds(r0, CONV_ROWS), :]
            for j in range(DN_CONV - 1):
                dx = dx + w[j:j + 1, :] * dscr[pl.ds(r0 + 3 - j, CONV_ROWS), :]
            dx_ref[pl.ds(r0, CONV_ROWS), :] = _bf(dx)

    col = pl.BlockSpec((S, DN_DIM), lambda b: (0, b))
    proj_col = pl.BlockSpec((S, DN_DIM), lambda b: (0, BLK_DN + b))
    wcol = pl.BlockSpec((DN_CONV, DN_DIM), lambda b: (0, b))
    return pl.pallas_call(
        body, grid=(3 * DN_HEADS,), name="dn_prep_bwd",
        in_specs=[proj_col, wcol, col, pl.BlockSpec(memory_space=pl.ANY)], out_specs=[proj_col, wcol],
        out_shape=[jax.ShapeDtypeStruct(dproj.shape, dproj.dtype), jax.ShapeDtypeStruct((DN_CONV, 3 * DN_HEADS * DN_DIM), F32)],
        scratch_shapes=[pltpu.VMEM((S + CONV_PAD, DN_DIM), F32), pltpu.VMEM((S + CONV_PAD, DN_DIM), F32)],
        input_output_aliases={3: 0},
        compiler_params=_params(("parallel",)),
    )(pd, conv_w, dqkv, dproj)


CPAD = 128
CHUNKS_LOCAL = 4
CHUNKS_SCAN = 4


def _chunk_masks():
    ii = lax.broadcasted_iota(jnp.int32, (DN_CHUNK, CPAD), 0)
    jj = lax.broadcasted_iota(jnp.int32, (DN_CHUNK, CPAD), 1)
    return ii, jj


def _rows_pad(a):
    return jnp.concatenate([a, jnp.zeros_like(a)], axis=0)


def _hi_lo(a):
    hi = _bf(a)
    return hi, _bf(a - hi.astype(F32))


def _double_step(t, p):
    C = DN_CHUNK
    th, tl = _hi_lo(t)
    ph, pl_ = _hi_lo(p)
    r1 = _dot(jnp.concatenate([th, tl, ph, pl_], axis=0), _rows_pad(ph))
    r2 = _dot(jnp.concatenate([th, ph], axis=0), _rows_pad(pl_))
    return t + (r1[:C] + r1[C:2 * C] + r2[:C]), r1[2 * C:3 * C] + r1[3 * C:] + r2[C:]


def _dot3_nt(a, b):
    C = DN_CHUNK
    ah, al = _hi_lo(a)
    bh, bl = _hi_lo(b)
    r1 = _dot_nt(jnp.concatenate([ah, al], axis=0), _rows_pad(bh))
    return r1[:C] + r1[C:] + _dot_nt(ah, _rows_pad(bl))


def _dot3_tn(a, b):
    C = DN_CHUNK
    ah, al = _hi_lo(a)
    bh, bl = _hi_lo(b)
    return _dot_tn(jnp.concatenate([ah, al, ah], axis=0), jnp.concatenate([bh, bh, bl], axis=0))[:C]


def _interleave(programs):
    programs = list(programs)
    while programs:
        alive = []
        for prog in programs:
            try:
                next(prog)
                alive.append(prog)
            except StopIteration:
                pass
        programs = alive


def _col_to_row(col, ii, jj):
    return jnp.sum(jnp.where(ii == jj, col, 0.0), axis=0, keepdims=True)


def _row_to_col(row, ii, jj):
    return jnp.sum(jnp.where(ii == jj, row, 0.0), axis=1, keepdims=True)


def _decay(gc_col, ii, jj):
    diff = gc_col - _col_to_row(gc_col, ii, jj)
    return jnp.where(jj <= ii, jnp.exp(jnp.where(jj <= ii, diff, 0.0)), 0.0)


def _softplus(x):
    return jnp.maximum(x, 0.0) + jnp.log(1.0 + jnp.exp(-jnp.abs(x)))


def _head(h):
    return slice(DN_DIM * h, DN_DIM * (h + 1))


def _dn_chunk_fwd(qkv, pg, a_log, dt_bias):
    S = qkv.shape[0]
    C = DN_CHUNK
    G = CHUNKS_LOCAL
    R = G * C
    steps = S // R

    def body(alog_ref, dtb_ref, qkv_ref, pg_ref, w_ref, u_ref, qg_ref, kd_ref, a_ref, t_ref, gcs_ref):
        ii, jj = _chunk_masks()
        lane = lax.broadcasted_iota(jnp.int32, (1, 128), 1)
        eye = (ii == jj).astype(F32)
        gcs_parts = [[] for _ in range(G)]

        def head_program(chunk, h):
            rows = slice(chunk * C, (chunk + 1) * C)
            q, k, v = qkv_ref[rows, _head(h)], qkv_ref[rows, _head(DN_HEADS + h)], qkv_ref[rows, _head(2 * DN_HEADS + h)]
            beta = _sigmoid(pg_ref[rows, h:h + 1])
            g_col = -jnp.exp(alog_ref[h]) * _softplus(pg_ref[rows, DN_HEADS + h:DN_HEADS + h + 1] + dtb_ref[h])
            g_row = _col_to_row(g_col, ii, jj)
            gc_col = jnp.sum(jnp.where(jj <= ii, g_row, 0.0), axis=1, keepdims=True)
            dec = _decay(gc_col, ii, jj)
            eg = jnp.exp(gc_col)
            kb, vb = k * beta, v * beta
            k_rows = _rows_pad(_bf(k))
            kk = _dot_nt(_bf(kb), k_rows)
            qk = _dot_nt(_bf(q), k_rows)
            yield
            t, pw = eye, -jnp.where(jj < ii, kk * dec, 0.0)
            for _ in range(6):
                t, pw = _double_step(t, pw)
                yield
            tb = _bf(t)
            u_ref[rows, _head(h)] = _dot(tb, _rows_pad(_bf(vb)))
            w_ref[rows, _head(h)] = _bf(_dot(tb, _rows_pad(_bf(kb * eg))))
            a_ref[h, rows] = _bf(qk * dec)
            t_ref[h, rows] = t
            qg_ref[rows, _head(h)] = _bf(q * eg)
            kd_ref[rows, _head(h)] = _bf(k * jnp.exp(gc_col[C - 1:C, :] - gc_col))
            gcs_parts[chunk].append(jnp.where(lane == h, gc_col, 0.0) + jnp.where(lane == DN_HEADS + h, beta, 0.0)
                                    + jnp.where(lane == 2 * DN_HEADS + h, g_col, 0.0))

        _interleave(head_program(chunk, h) for chunk in range(G) for h in range(DN_HEADS))
        for chunk in range(G):
            gcs_ref[chunk * C:(chunk + 1) * C, :] = sum(gcs_parts[chunk][1:], gcs_parts[chunk][0])

    smem = pl.BlockSpec(memory_space=pltpu.SMEM)
    wide = pl.BlockSpec((R, 512), lambda n: (n, 0))
    sq = pl.BlockSpec((DN_HEADS, R, CPAD), lambda n: (0, n, 0))
    narrow = pl.BlockSpec((R, 128), lambda n: (n, 0))
    f = lambda *shp: jax.ShapeDtypeStruct(shp, F32)
    b = lambda *shp: jax.ShapeDtypeStruct(shp, BF16)
    return pl.pallas_call(
        body, grid=(steps,), name="dn_chunk_fwd",
        in_specs=[smem, smem, pl.BlockSpec((R, 1536), lambda n: (n, 0)), pl.BlockSpec((R, 128), lambda n: (n, BLK_G))],
        out_specs=[wide, wide, wide, wide, sq, sq, narrow],
        out_shape=[b(S, 512), f(S, 512), b(S, 512), b(S, 512), b(DN_HEADS, S, CPAD), f(DN_HEADS, S, CPAD), f(S, 128)],
        compiler_params=_params(("parallel",)),
    )(a_log, dt_bias, qkv, pg)


def _gated_norm(o, z, gn):
    r, oh = _rms_stats(o)
    return oh * gn * (z * _sigmoid(z))


def _dn_scan_fwd(w, u, qg, kd, a, gcs, pz, gn):
    S = w.shape[0]
    C = DN_CHUNK
    nc = S // C
    G = CHUNKS_SCAN
    R = G * C

    def body(w_ref, u_ref, qg_ref, kd_ref, a_ref, gcs_ref, z_ref, gn_ref, o_ref, vn_ref, sst_ref, out_ref, state):
        @pl.when(pl.program_id(0) == 0)
        def _():
            state[...] = jnp.zeros_like(state)

        def head_program(chunk, h):
            hs = _head(h)
            rows = slice(chunk * C, (chunk + 1) * C)
            s_in = state[h]
            sst_ref[chunk, h] = s_in
            sb = _bf(s_in)
            w_s = _dot(w_ref[rows, hs], sb)
            q_s = _dot(qg_ref[rows, hs], sb)
            yield
            vn = u_ref[rows, hs] - w_s
            vnb = _bf(vn)
            o = q_s + _dot(a_ref[h, rows], _rows_pad(vnb))
            k_v = _dot_tn(kd_ref[rows, hs], vnb)
            yield
            state[h] = s_in * jnp.exp(gcs_ref[(chunk + 1) * C - 1:(chunk + 1) * C, h:h + 1]) + k_v
            o_ref[rows, hs] = o
            vn_ref[rows, hs] = vn
            out_ref[rows, hs] = _gated_norm(o, z_ref[rows, hs], gn_ref[...])

        for chunk in range(G):
            _interleave(head_program(chunk, h) for h in range(DN_HEADS))

    wide = pl.BlockSpec((R, 512), lambda n: (n, 0))
    f = lambda *shp: jax.ShapeDtypeStruct(shp, F32)
    return pl.pallas_call(
        body, grid=(nc // G,), name="dn_scan_fwd",
        in_specs=[wide, wide, wide, wide, pl.BlockSpec((DN_HEADS, R, CPAD), lambda n: (0, n, 0)),
                  pl.BlockSpec((R, 128), lambda n: (n, 0)), pl.BlockSpec((R, 512), lambda n: (n, BLK_Z)),
                  pl.BlockSpec((1, DN_DIM), lambda n: (0, 0))],
        out_specs=[wide, wide, pl.BlockSpec((G, DN_HEADS, DN_DIM, DN_DIM), lambda n: (n, 0, 0, 0)), wide],
        out_shape=[f(S, 512), f(S, 512), f(nc, DN_HEADS, DN_DIM, DN_DIM), f(S, 512)],
        scratch_shapes=[pltpu.VMEM((DN_HEADS, DN_DIM, DN_DIM), F32)],
        compiler_params=_params(("arbitrary",)),
    )(w, u, qg, kd, a, gcs, pz, gn)


def _dn_scan_bwd(dcat, o, pz, gn, sst, vnew, w, qg, kd, a, gcs, dproj):
    S = o.shape[0]
    C = DN_CHUNK
    G = CHUNKS_SCAN
    R = G * C
    steps = S // R

    def body(dy_ref, o_ref, z_ref, gn_ref, sst_ref, vn_ref, w_ref, qg_ref, kd_ref, a_ref, gcs_ref, _,
             du_ref, dw_ref, dqg_ref, dkd_ref, da_ref, dz_ref, dsc_ref, dgn_ref, dstate):
        @pl.when(pl.program_id(0) == 0)
        def _():
            dstate[...] = jnp.zeros_like(dstate)
            dgn_ref[...] = jnp.zeros_like(dgn_ref)

        gn_ = gn_ref[...]
        lane = lax.broadcasted_iota(jnp.int32, (C, 128), 1)
        row = lax.broadcasted_iota(jnp.int32, (C, 128), 0)
        dgn_parts = []

        def head_program(chunk, h, dsc_parts):
            hs = _head(h)
            rows = slice(chunk * C, (chunk + 1) * C)
            ov, z, dout = o_ref[rows, hs], z_ref[rows, hs], dy_ref[rows, hs]
            r, oh = _rms_stats(ov)
            sg = _sigmoid(z)
            don = dout * (z * sg)
            dz_ref[rows, hs] = _bf(dout * (oh * gn_) * (sg * (1.0 + z * (1.0 - sg))))
            dgn_parts.append(jnp.sum(don * oh, axis=0, keepdims=True))
            dn = don * gn_
            do = _bf(r * (dn - oh * jnp.mean(dn * oh, axis=-1, keepdims=True)))
            s_in = sst_ref[chunk, h]
            sb = _bf(s_in)
            ds_out = dstate[h]
            dsb = _bf(ds_out)
            vnb = _bf(vn_ref[rows, hs])
            wb, qgb, kdb, ab = w_ref[rows, hs], qg_ref[rows, hs], kd_ref[rows, hs], a_ref[h, rows]
            dvn = _dot_tn(ab, do)[:C] + _dot(kdb, dsb)
            da_ref[h, rows] = _dot_nt(do, _rows_pad(vnb))
            dqg_ref[rows, hs] = _dot_nt(do, sb)
            dkd_ref[rows, hs] = _dot_nt(vnb, dsb)
            q_do = _dot_tn(qgb, do)
            yield
            dvnb = _bf(dvn)
            dw_ref[rows, hs] = _bf(-_dot_nt(dvnb, sb))
            w_dvn = _dot_tn(wb, dvnb)
            du_ref[rows, hs] = dvnb
            yield
            d_last = jnp.exp(gcs_ref[(chunk + 1) * C - 1:(chunk + 1) * C, h:h + 1])
            dd = jnp.sum(jnp.sum(ds_out * s_in, axis=1, keepdims=True), axis=0, keepdims=True)
            dsc_parts.append(jnp.where((lane == h) & (row == C - 1), dd * d_last, 0.0))
            dstate[h] = ds_out * d_last + q_do - w_dvn

        for chunk in reversed(range(G)):
            dsc_parts = []
            _interleave(head_program(chunk, h, dsc_parts) for h in range(DN_HEADS))
            dsc_ref[chunk * C:(chunk + 1) * C, :] = sum(dsc_parts[1:], dsc_parts[0])
        dgn_ref[...] += sum(dgn_parts[1:], dgn_parts[0])

    rev = lambda n: steps - 1 - n
    wide = pl.BlockSpec((R, 512), lambda n: (rev(n), 0))
    z_spec = pl.BlockSpec((R, 512), lambda n: (rev(n), BLK_Z))
    sq = pl.BlockSpec((DN_HEADS, R, CPAD), lambda n: (0, rev(n), 0))
    narrow = pl.BlockSpec((R, 128), lambda n: (rev(n), 0))
    gn_spec = pl.BlockSpec((1, DN_DIM), lambda n: (0, 0))
    f = lambda *shp: jax.ShapeDtypeStruct(shp, F32)
    b = lambda *shp: jax.ShapeDtypeStruct(shp, BF16)
    return pl.pallas_call(
        body, grid=(steps,), name="dn_scan_bwd",
        in_specs=[pl.BlockSpec((R, 512), lambda n: (rev(n), 1)), wide, z_spec, gn_spec,
                  pl.BlockSpec((G, DN_HEADS, DN_DIM, DN_DIM), lambda n: (rev(n), 0, 0, 0)),
                  wide, wide, wide, wide, sq, narrow, pl.BlockSpec(memory_space=pl.ANY)],
        out_specs=[wide, wide, wide, wide, sq, z_spec, narrow, gn_spec],
        out_shape=[b(S, 512), b(S, 512), f(S, 512), f(S, 512), f(DN_HEADS, S, CPAD),
                   jax.ShapeDtypeStruct(dproj.shape, dproj.dtype), f(S, 128), f(1, DN_DIM)],
        scratch_shapes=[pltpu.VMEM((DN_HEADS, DN_DIM, DN_DIM), F32)],
        input_output_aliases={11: 5},
        compiler_params=_params(("arbitrary",)),
    )(dcat, o, pz, gn, sst, vnew, w, qg, kd, a, gcs, dproj)


def _dn_chunk_bwd(qkv, pg, t_inv, gcs, du, dw, dqg, dkd, da, dsc, a_log, dt_bias, dproj):
    S = qkv.shape[0]
    C = DN_CHUNK
    G = CHUNKS_LOCAL
    R = G * C

    def body(alog_ref, dtb_ref, qkv_ref, pg_ref, t_ref, gcs_ref, du_ref, dw_ref, dqg_ref, dkd_ref, da_ref, dsc_ref, _,
             dqkv_ref, dpg_ref, acc_ref):
        @pl.when(pl.program_id(0) == 0)
        def _():
            acc_ref[...] = jnp.zeros_like(acc_ref)

        ii, jj = _chunk_masks()
        lane = lax.broadcasted_iota(jnp.int32, (1, 128), 1)
        row8 = lax.broadcasted_iota(jnp.int32, (8, 128), 0)
        lane8 = lax.broadcasted_iota(jnp.int32, (8, 128), 1)
        rowc = lax.broadcasted_iota(jnp.int32, (C, 1), 0)
        tril, strict = jj <= ii, jj < ii
        dpg_parts, acc_parts = [[] for _ in range(G)], []

        def head_program(chunk, h):
            rows = slice(chunk * C, (chunk + 1) * C)
            q, k, v = qkv_ref[rows, _head(h)], qkv_ref[rows, _head(DN_HEADS + h)], qkv_ref[rows, _head(2 * DN_HEADS + h)]
            gc_col, beta, g_col = gcs_ref[rows, h:h + 1], gcs_ref[rows, DN_HEADS + h:DN_HEADS + h + 1], \
                gcs_ref[rows, 2 * DN_HEADS + h:2 * DN_HEADS + h + 1]
            dec = _decay(gc_col, ii, jj)
            eg = jnp.exp(gc_col)
            g_last = gc_col[C - 1:C, :]
            ek = jnp.exp(g_last - gc_col)
            kb, vb = k * beta, v * beta
            kbg = kb * eg
            qb, kbb = _bf(q), _bf(kb)
            k_rows = _rows_pad(_bf(k))
            t = t_ref[h, rows]
            tb = _bf(t)
            dub, dwb = du_ref[rows, _head(h)], dw_ref[rows, _head(h)]
            dqg_, dkd_ = dqg_ref[rows, _head(h)], dkd_ref[rows, _head(h)]
            dt = _dot_nt(dub, _rows_pad(_bf(vb))) + _dot_nt(dwb, _rows_pad(_bf(kbg)))
            t_du_dw = _dot_tn(tb, jnp.concatenate([dub, dwb], axis=1))
            dvb, dkbg = t_du_dw[:C, :DN_DIM], t_du_dw[:C, DN_DIM:]
            kk = _dot_nt(kbb, k_rows)
            qk = _dot_nt(qb, k_rows)
            yield
            dt_t = _dot3_nt(dt, t)
            yield
            dl = -_dot3_tn(t, dt_t)
            yield
            dm = jnp.where(strict, dl * dec, 0.0)
            dqk = jnp.where(tril, da_ref[h, rows] * dec, 0.0)
            gmat = dm * kk + dqk * qk
            dgc = jnp.sum(gmat, axis=1, keepdims=True) - _row_to_col(jnp.sum(gmat, axis=0, keepdims=True), ii, jj)
            dmb, dqkb = _bf(dm), _bf(dqk)
            dkb = _dot(dmb, k_rows) + dkbg * eg
            dk = _dot_tn(jnp.concatenate([dmb, dqkb], axis=0), jnp.concatenate([kbb, qb], axis=0))[:C] + dkd_ * ek
            dq = _dot(dqkb, k_rows) + dqg_ * eg
            yield
            tk = jnp.sum(dkd_ * k * ek, axis=1, keepdims=True)
            dgc = dgc + jnp.sum(dqg_ * q * eg, axis=1, keepdims=True) - tk + jnp.sum(dkbg * kbg, axis=1, keepdims=True)
            dgl = jnp.sum(tk, axis=0, keepdims=True) + dsc_ref[(chunk + 1) * C - 1:(chunk + 1) * C, h:h + 1]
            dgc = dgc + jnp.where(rowc == C - 1, dgl, 0.0)
            dk = dk + dkb * beta
            dbeta = jnp.sum(dkb * k, axis=1, keepdims=True) + jnp.sum(dvb * v, axis=1, keepdims=True)
            dqkv_ref[rows, _head(h)] = dq
            dqkv_ref[rows, _head(DN_HEADS + h)] = dk
            dqkv_ref[rows, _head(2 * DN_HEADS + h)] = dvb * beta
            dg_col = jnp.sum(jnp.where(jj >= ii, _col_to_row(dgc, ii, jj), 0.0), axis=1, keepdims=True)
            db = dbeta * beta * (1.0 - beta)
            da_in = dg_col * (-jnp.exp(alog_ref[h])) * _sigmoid(pg_ref[rows, DN_HEADS + h:DN_HEADS + h + 1] + dtb_ref[h])
            dpg_parts[chunk].append(jnp.where(lane == h, db, 0.0) + jnp.where(lane == DN_HEADS + h, da_in, 0.0))
            acc_parts.append(jnp.where((row8 == 0) & (lane8 == h), jnp.sum(dg_col * g_col, axis=0, keepdims=True), 0.0)
                             + jnp.where((row8 == 1) & (lane8 == h), jnp.sum(da_in, axis=0, keepdims=True), 0.0))

        _interleave(head_program(chunk, h) for chunk in range(G) for h in range(DN_HEADS))
        for chunk in range(G):
            dpg = sum(dpg_parts[chunk][1:], dpg_parts[chunk][0])
            dpg_ref[chunk * C:(chunk + 1) * C, :] = _bf(jnp.concatenate([dpg, jnp.zeros_like(dpg)], axis=1))
        acc_ref[...] += sum(acc_parts[1:], acc_parts[0])

    smem = pl.BlockSpec(memory_space=pltpu.SMEM)
    wide = pl.BlockSpec((R, 512), lambda n: (n, 0))
    sq = pl.BlockSpec((DN_HEADS, R, CPAD), lambda n: (0, n, 0))
    narrow = pl.BlockSpec((R, 128), lambda n: (n, 0))
    qkv_spec = pl.BlockSpec((R, 1536), lambda n: (n, 0))
    f = lambda *shp: jax.ShapeDtypeStruct(shp, F32)
    return pl.pallas_call(
        body, grid=(S // R,), name="dn_chunk_bwd",
        in_specs=[smem, smem, qkv_spec, pl.BlockSpec((R, 128), lambda n: (n, BLK_G)), sq, narrow, wide, wide, wide, wide, sq,
                  narrow, pl.BlockSpec(memory_space=pl.ANY)],
        out_specs=[qkv_spec, pl.BlockSpec((R, 256), lambda n: (n, BLK_G_PAD)), pl.BlockSpec((8, 128), lambda n: (0, 0))],
        out_shape=[f(S, 1536), jax.ShapeDtypeStruct(dproj.shape, dproj.dtype), f(8, 128)],
        input_output_aliases={12: 1},
        compiler_params=_params(("arbitrary",)),
    )(a_log, dt_bias, qkv, pg, t_inv, gcs, du, dw, dqg, dkd, da, dsc, dproj)


def _fill_kv(dk, dv, dproj):
    S = dk.shape[0]
    tm = min(512, S)

    def body(dk_ref, dv_ref, _, o_ref):
        o_ref[...] = _bf(jnp.concatenate([dk_ref[...], dv_ref[...]], axis=1))

    tile = pl.BlockSpec((tm, 128), lambda i: (i, 0))
    return pl.pallas_call(
        body, grid=(S // tm,), name="fill_kv",
        in_specs=[tile, tile, pl.BlockSpec(memory_space=pl.ANY)],
        out_specs=pl.BlockSpec((tm, 256), lambda i: (i, BLK_KV)),
        out_shape=jax.ShapeDtypeStruct(dproj.shape, dproj.dtype),
        input_output_aliases={2: 0},
        compiler_params=_params(("parallel",)),
    )(dk, dv, dproj)


def _w_in_to_internal(wt):
    return jnp.concatenate([wt[0:512], wt[2304:2816], wt[768:2304], wt[512:768], wt[2816:2824],
                            jnp.zeros((D_IN_PAD - D_IN, wt.shape[1]), wt.dtype)], axis=0)


def _w_in_from_internal(gt):
    return jnp.concatenate([gt[0:512], gt[2560:2816], gt[1024:2560], gt[512:1024], gt[2816:2824]], axis=0)


def _local_step(x, p, target, wts, first_weights, other_weights, ship_early):
    S = x.shape[0]
    cos, sin = _rope_tables(S)
    sinks, a_log, dt_bias = wts["sinks"].reshape(8), wts["a_log"].reshape(4), wts["dt_bias"].reshape(4)
    gn = wts["dn_norm"].reshape(1, DN_DIM)
    add = lambda acc, res: (acc + res,)

    u = _rmsnorm_fwd(x, wts["norm_mix"], "norm_mix_fwd")
    w_in_t, conv_w = first_weights(u)
    proj, = _mm(u, w_in_t, form="nt", name="in_proj", out_dtypes=[F32], tn=512)
    attn = _attn_fwd(proj, cos, sin, sinks)
    qkv = _dn_prep_fwd(proj, conv_w)
    cw, cu, cqg, ckd, ca, ct, gcs = _dn_chunk_fwd(qkv, proj, a_log, dt_bias)
    o, vnew, sst, dn_out = _dn_scan_fwd(cw, cu, cqg, ckd, ca, gcs, proj, gn)
    w_o, = other_weights(("w_o",), dn_out)
    h1, = _mm([attn, dn_out], w_o, form="nn", name="out_proj", out_dtypes=[F32], tn=512, epi=add, extra=[x])

    def relu2(acc):
        r = jnp.maximum(acc, 0.0)
        return r * r, r

    w_up, = other_weights(("w_up",), h1)
    hid, relu, m = _mm(h1, w_up, form="nn", name="mlp_up", out_dtypes=[BF16, BF16], tn=512, epi=relu2, norm=wts["norm_mlp"])
    w_down, = other_weights(("w_down",), hid)
    h2, = _mm(hid, w_down, form="nn", name="mlp_down", out_dtypes=[F32], tn=512, epi=add, extra=[h1])
    w_pg, w_pp = other_weights(("w_ple_gate", "w_ple_proj"), h2)
    n3, dh3, dgl, dpp, loss, d_norm_final = _ple_and_loss(h2, p, target, w_pg, w_pp, wts["norm_ple"],
                                                         wts["norm_final"].reshape(1, D_MODEL))
    g = {"norm_final": d_norm_final}
    early = {"w_ple_gate": _mm_tn(n3, dgl, name="d_w_ple_gate", tm=512, tn=1024, out_dtype=BF16).reshape(N_DEV, 128, 1024),
             "w_ple_proj": _mm_tn(p, dpp, name="d_w_ple_proj", tm=256, tn=128, out_dtype=BF16, column_shards=True)}
    dh2, g["norm_ple"] = _mm(dgl, w_pg, form="nt", name="d_n3", out_dtypes=[F32], tn=512,
                             norm_bwd=(h2, wts["norm_ple"], dh3))
    d_act, = _mm(dh2, w_down, form="nt", name="d_hidden", out_dtypes=[BF16], tn=512,
                 epi=lambda acc, r: (acc * (2.0 * r.astype(F32)),), extra=[relu])
    early["w_down"] = _mm_tn(hid, dh2, name="d_w_down", tm=512, tn=1024, out_dtype=BF16).reshape(N_DEV, 512, 1024)
    early["w_up"] = _mm_tn(m, d_act, name="d_w_up", tm=1024, tn=512, out_dtype=BF16, column_shards=True)
    token = ship_early(early)
    dh1, g["norm_mlp"] = _mm(d_act, w_up, form="nt", name="d_m", out_dtypes=[F32], tn=512, after=token,
                             norm_bwd=(h1, wts["norm_mlp"], dh2))
    dcat, = _mm(dh1, w_o, form="nt", name="d_cat", out_dtypes=[F32], tn=512)
    d_w_o = jnp.concatenate([_mm_tn(attn, dh1, name="d_w_o_attn", tm=512, tn=512, out_dtype=BF16),
                             _mm_tn(dn_out, dh1, name="d_w_o_dn", tm=512, tn=512, out_dtype=BF16)], axis=0)
    token = ship_early({"w_o": d_w_o.reshape(N_DEV, 128, 1024)})
    dproj, dk, dv, dsinks = _attn_bwd(proj, cos, sin, sinks + token[0, 0], dcat)
    g["sinks"] = dsinks[:, 0].reshape(1, 8)
    du_, dw_, dqg, dkd, da, dproj, dsc, g["dn_norm"] = _dn_scan_bwd(dcat, o, proj, gn, sst, vnew, cw, cqg, ckd, ca, gcs, dproj)
    dqkv, dproj, gate_acc = _dn_chunk_bwd(qkv, proj, ct, gcs, du_, dw_, dqg, dkd, da, dsc, a_log, dt_bias, dproj)
    g["a_log"], g["dt_bias"] = gate_acc[0:1, 0:4], gate_acc[1:2, 0:4]
    dproj, g["conv_w"] = _dn_prep_bwd(proj, conv_w, dqkv, dproj)
    dproj = _fill_kv(dk, dv, dproj)
    token = ship_early({"w_in": _mm_tn(dproj, u, name="d_w_in", tm=512, tn=1024)})
    grad_x, g["norm_mix"] = _mm(dproj, w_in_t, form="nn", name="d_u", out_dtypes=[F32], tn=512, after=token,
                                norm_bwd=(x, wts["norm_mix"], dh1))
    return loss, grad_x, g


def _peer(k):
    x, y, c = lax.axis_index("x"), lax.axis_index("y"), lax.axis_index("c")
    px = 1 - x if k & 4 else x
    py = 1 - y if k & 2 else y
    pc = 1 - c if k & 1 else c
    return (px, py, pc), 4 * px + 2 * py + pc


def _exchange(srcs, name, gather):
    n = len(srcs)
    gathers = list(gather) if isinstance(gather, (list, tuple)) else [gather] * n
    shapes = [(N_DEV,) + s.shape if gt else s.shape for s, gt in zip(srcs, gathers)]

    def body(*refs):
        src_refs, out_refs = refs[:n], refs[n:2 * n]
        send_sems, recv_sems, local_sems = refs[2 * n:]
        _, me = _peer(0)
        piece = lambda a, d: src_refs[a] if gathers[a] else src_refs[a].at[d]
        local = [pltpu.make_async_copy(piece(a, me), out_refs[a].at[me], local_sems.at[a]) for a in range(n)]
        for cp in local:
            cp.start()
        copies = []
        for a in range(n):
            for k in range(1, N_DEV):
                dev, idx = _peer(k)
                cp = pltpu.make_async_remote_copy(src_ref=piece(a, idx), dst_ref=out_refs[a].at[me],
                                                  send_sem=send_sems.at[a, k - 1], recv_sem=recv_sems.at[a, k - 1],
                                                  device_id=dev, device_id_type=MESH)
                cp.start()
                copies.append(cp)
        for cp in copies:
            cp.wait_recv()
        for cp in copies:
            cp.wait_send()
        for cp in local:
            cp.wait()

    anywhere = pl.BlockSpec(memory_space=pl.ANY)
    return pl.pallas_call(
        body, name=name, in_specs=[anywhere] * n, out_specs=[anywhere] * n,
        out_shape=[jax.ShapeDtypeStruct(shp, s.dtype) for shp, s in zip(shapes, srcs)],
        scratch_shapes=[pltpu.SemaphoreType.DMA((n, N_DEV - 1)), pltpu.SemaphoreType.DMA((n, N_DEV - 1)),
                        pltpu.SemaphoreType.DMA((n,))],
    )(*srcs)


_HBM = pl.BlockSpec(memory_space=pltpu.HBM)
_SEM = pl.BlockSpec(memory_space=pltpu.SEMAPHORE)
_EFFECT = pltpu.SideEffectType.DATAFLOW_SIDE_EFFECTING


def _split_copies(src_refs, land_refs, send_sems, recv_sems, modes, which=None):
    _, me = _peer(0)
    copies = []
    which = range(len(src_refs)) if which is None else which
    for a, src, land in zip(which, src_refs, land_refs):
        if modes[a] == "columns":
            n_cols = src.shape[1]
            dst = land.at[:, pl.ds(pl.multiple_of(me * n_cols, n_cols), n_cols)]
        else:
            dst = land.at[me]
        for k in range(1, N_DEV):
            dev, idx = _peer(k)
            sem = a * (N_DEV - 1) + k - 1
            copies.append(pltpu.make_async_remote_copy(
                src_ref=src.at[idx] if modes[a] == "pieces" else src, dst_ref=dst, send_sem=send_sems.at[sem],
                recv_sem=recv_sems.at[sem], device_id=dev, device_id_type=MESH))
    return copies


def _exchange_start(srcs, name, modes):
    n = len(srcs)
    modes = [modes] * n if isinstance(modes, str) else list(modes)
    me = 4 * lax.axis_index("x") + 2 * lax.axis_index("y") + lax.axis_index("c")
    lands = []
    for s, mode in zip(srcs, modes):
        if mode == "columns":
            empty = lax.empty((s.shape[0], N_DEV * s.shape[1]), s.dtype)
            lands.append(lax.dynamic_update_slice(empty, s, (0, me * s.shape[1])))
        else:
            own = s if mode == "slots" else lax.dynamic_index_in_dim(s, me, 0, keepdims=False)
            shape = (N_DEV,) + s.shape if mode == "slots" else s.shape
            lands.append(lax.dynamic_update_index_in_dim(lax.empty(shape, s.dtype), own, me, 0))

    def body(*refs):
        src_refs, land_refs = refs[:n], refs[n:2 * n]
        send_sems, recv_sems = refs[2 * n], refs[2 * n + 1]
        for cp in _split_copies(src_refs, land_refs, send_sems, recv_sems, modes):
            cp.start()
        refs[-1][...] = jnp.zeros_like(refs[-1])

    both = list(srcs) + lands
    sems = pltpu.SemaphoreType.DMA((n * (N_DEV - 1),))
    out = pl.pallas_call(
        body, name=name,
        out_shape=(sems, sems, *[pltpu.HBM(t.shape, t.dtype) for t in both], jax.ShapeDtypeStruct((8, 128), F32)),
        in_specs=[_HBM] * (2 * n), out_specs=(_SEM, _SEM, *[_HBM] * (2 * n), pl.BlockSpec(memory_space=pltpu.VMEM)),
        input_output_aliases={i: 2 + i for i in range(2 * n)},
        compiler_params=pltpu.CompilerParams(has_side_effects=_EFFECT),
    )(*[pltpu.with_memory_space_constraint(t, pltpu.HBM) for t in both])
    return (n, modes, out[:-1]), out[-1]


def _exchange_wait(handle, after, name, which=None):
    n_all, modes, (send_sems, recv_sems, *both_all) = handle
    which = list(range(n_all)) if which is None else list(which)
    n = len(which)
    both = [both_all[a] for a in which] + [both_all[n_all + a] for a in which]

    def body(*refs):
        src_refs, land_refs = refs[:n], refs[n:2 * n]
        for cp in _split_copies(src_refs, land_refs, refs[2 * n], refs[2 * n + 1], modes, which):
            cp.wait_send()
            cp.wait_recv()

    out = pl.pallas_call(
        body, name=name, out_shape=tuple(pltpu.HBM(t.shape, t.dtype) for t in both),
        in_specs=[_HBM] * (2 * n) + [_SEM, _SEM, pl.BlockSpec(memory_space=pl.ANY)], out_specs=tuple([_HBM] * (2 * n)),
        input_output_aliases={i: i for i in range(2 * n)},
        compiler_params=pltpu.CompilerParams(has_side_effects=_EFFECT),
    )(*both, send_sems, recv_sems, after)
    return list(out[n:])


def _adamw(parts, w, m, v, name):
    n, R, W = parts.shape
    tm = 128 if R % 128 == 0 else R

    def body(p_ref, w_ref, m_ref, v_ref, g_ref, d_ref, nm_ref, nv_ref):
        g = p_ref[0].astype(F32)
        for s in range(1, n):
            g = g + p_ref[s].astype(F32)
        nm = ADAM_B1 * m_ref[...] + (1.0 - ADAM_B1) * g
        nv = ADAM_B2 * v_ref[...] + (1.0 - ADAM_B2) * (g * g)
        m_hat = nm / (1.0 - ADAM_B1 ** ADAM_STEP)
        v_hat = nv / (1.0 - ADAM_B2 ** ADAM_STEP)
        g_ref[...] = g
        d_ref[...] = -ADAM_LR * (m_hat / (jnp.sqrt(v_hat) + ADAM_EPS) + ADAM_WD * w_ref[...])
        nm_ref[...] = nm
        nv_ref[...] = nv

    tile = pl.BlockSpec((tm, W), lambda i: (i, 0))
    return pl.pallas_call(
        body, grid=(R // tm,), name=name,
        in_specs=[pl.BlockSpec((n, tm, W), lambda i: (0, i, 0)), tile, tile, tile],
        out_specs=[tile] * 4, out_shape=[jax.ShapeDtypeStruct((R, W), F32)] * 4,
        compiler_params=_params(("parallel",)),
    )(parts, w, m, v)


_MATRICES = ("w_in", "w_o", "w_up", "w_down", "w_ple_gate", "w_ple_proj")


_OTHERS = ("w_o", "w_up", "w_down", "w_ple_gate", "w_ple_proj")
_OTHER_MODES = {"w_o": "slots", "w_up": "columns", "w_down": "slots", "w_ple_gate": "slots", "w_ple_proj": "columns"}


_SMALL_ROWS = 16
_VEC_ROW = {"norm_mix": 0, "norm_mlp": 1, "norm_ple": 2, "norm_final": 3}
_VEC_LANES = {"a_log": (0, 4), "dt_bias": (4, 8), "sinks": (8, 16), "dn_norm": (128, 256)}
_LOSS_ROW, _CONV_ROW = 5, 8


def _pack_small(vals, extra_rows):
    row4 = jnp.zeros((1024,), F32)
    for n, (a, b) in _VEC_LANES.items():
        row4 = row4.at[a:b].set(vals[n].reshape(b - a))
    rows = [vals[n].reshape(1, 1024) for n in ("norm_mix", "norm_mlp", "norm_ple", "norm_final")] + [row4.reshape(1, 1024)]
    return jnp.concatenate(rows + extra_rows, axis=0)


def _unpack_small(buf, like):
    out = {n: buf[r].reshape(like[n].shape) for n, r in _VEC_ROW.items()}
    for n, (a, b) in _VEC_LANES.items():
        out[n] = buf[4, a:b].reshape(like[n].shape)
    return out


_ORDER = ("norm_mix", "w_in", "conv_w", "a_log", "dt_bias", "dn_norm", "sinks", "w_o", "norm_mlp", "w_up", "w_down",
          "norm_ple", "w_ple_gate", "w_ple_proj", "norm_final")


def kernel(x, p, norm_mix, w_in, conv_w, a_log, dt_bias, dn_norm, sinks, w_o, norm_mlp, w_up, w_down, norm_ple, w_ple_gate, w_ple_proj, norm_final, loss_target, m_norm_mix, m_w_in, m_conv_w, m_a_log, m_dt_bias, m_dn_norm, m_sinks, m_w_o, m_norm_mlp, m_w_up, m_w_down, m_norm_ple, m_w_ple_gate, m_w_ple_proj, m_norm_final, v_norm_mix, v_w_in, v_conv_w, v_a_log, v_dt_bias, v_dn_norm, v_sinks, v_w_o, v_norm_mlp, v_w_up, v_w_down, v_norm_ple, v_w_ple_gate, v_w_ple_proj, v_norm_final):
    w = dict(norm_mix=norm_mix, w_in=w_in[0], conv_w=conv_w[0], a_log=a_log, dt_bias=dt_bias, dn_norm=dn_norm, sinks=sinks,
             w_o=w_o[0], norm_mlp=norm_mlp, w_up=w_up[0], w_down=w_down[0], norm_ple=norm_ple, w_ple_gate=w_ple_gate[0],
             w_ple_proj=w_ple_proj[0], norm_final=norm_final)
    m = dict(norm_mix=m_norm_mix, w_in=m_w_in[0], conv_w=m_conv_w[0], a_log=m_a_log, dt_bias=m_dt_bias, dn_norm=m_dn_norm,
             sinks=m_sinks, w_o=m_w_o[0], norm_mlp=m_norm_mlp, w_up=m_w_up[0], w_down=m_w_down[0], norm_ple=m_norm_ple,
             w_ple_gate=m_w_ple_gate[0], w_ple_proj=m_w_ple_proj[0], norm_final=m_norm_final)
    v = dict(norm_mix=v_norm_mix, w_in=v_w_in[0], conv_w=v_conv_w[0], a_log=v_a_log, dt_bias=v_dt_bias, dn_norm=v_dn_norm,
             sinks=v_sinks, w_o=v_w_o[0], norm_mlp=v_norm_mlp, w_up=v_w_up[0], w_down=v_w_down[0], norm_ple=v_norm_ple,
             w_ple_gate=v_w_ple_gate[0], w_ple_proj=v_w_ple_proj[0], norm_final=v_norm_final)
    me = 4 * lax.axis_index("x") + 2 * lax.axis_index("y") + lax.axis_index("c")
    conv_shard = conv_w.shape[2]

    for d in (w, m, v):
        d["w_in"] = d["w_in"].T
    conv_pad = jnp.pad(w["conv_w"], ((0, 8 - DN_CONV), (0, 256 - conv_shard)))
    first, token_first = _exchange_start([_bf(w["w_in"]), conv_pad], "gather_first_start", "slots")
    later = [_bf(w[n]) for n in _OTHERS]
    later[-1] = _bf(w["w_ple_proj"] + token_first[0:1, 0:1])
    others, token_others = _exchange_start(later, "gather_others_start", [_OTHER_MODES[n] for n in _OTHERS])
    vectors = dict(w)
    vectors["norm_mix"] = w["norm_mix"] + token_others[0:1, 0:1]

    def first_weights(after):
        w_in_all, conv_all = _exchange_wait(first, after, "gather_first_wait")
        conv_all = jnp.transpose(conv_all[:, :DN_CONV, :conv_shard], (1, 0, 2)).reshape(DN_CONV, N_DEV * conv_shard)
        return _w_in_to_internal(w_in_all.reshape(D_IN, D_MODEL)), conv_all

    as_taken = {"w_o": lambda t: t.reshape(1024, 1024), "w_up": lambda t: t, "w_down": lambda t: t.reshape(4096, 1024),
                "w_ple_gate": lambda t: t.reshape(1024, 1024), "w_ple_proj": lambda t: t}

    def other_weights(names, after):
        which = [_OTHERS.index(n) for n in names]
        got = _exchange_wait(others, after, "gather_wait_" + names[0], which)
        return [as_taken[n](t) for n, t in zip(names, got)]

    shipped = []

    def ship_early(pieces):
        names = tuple(pieces)
        if names == ("w_in",):
            pieces = {"w_in": _bf(_w_in_from_internal(pieces["w_in"])).reshape(N_DEV, D_IN // N_DEV, D_MODEL)}
        handle, token = _exchange_start([pieces[n] for n in names], "scatter_start_" + names[0], "pieces")
        shipped.append((names, handle))
        return token

    loss, grad_x, g = _local_step(x[0], p[0, 0], loss_target[0], vectors, first_weights, other_weights, ship_early)

    small = _pack_small(g, [loss[:, :1] * jnp.ones((1, 1024), F32), jnp.zeros((2, 1024), F32),
                            g["conv_w"].reshape(6, 1024), jnp.zeros((2, 1024), F32)])
    small_all, = _exchange([small], "gather_small", gather=True)
    received = {}
    for names, handle in shipped:
        received.update(zip(names, _exchange_wait(handle, grad_x, "scatter_wait_" + names[0])))
    big = {n: _adamw(received[n], w[n], m[n], v[n], "adamw_" + n) for n in _MATRICES}
    zeros16 = jnp.zeros((_SMALL_ROWS, 1024), F32)
    summed = _adamw(small_all, zeros16, zeros16, zeros16, "sum_small")[0]
    conv_g = lax.dynamic_slice(summed[_CONV_ROW:_CONV_ROW + 6].reshape(DN_CONV, N_DEV * conv_shard), (0, me * conv_shard),
                               (DN_CONV, conv_shard))
    pad_conv = lambda t: jnp.pad(t.reshape(1, DN_CONV * conv_shard), ((0, 2), (0, 1024 - DN_CONV * conv_shard)))
    small_g = jnp.concatenate([summed[0:5], pad_conv(conv_g)], axis=0)[None]
    pack8 = lambda d: _pack_small(d, [pad_conv(d["conv_w"])])
    sm = _adamw(small_g, pack8(w), pack8(m), pack8(v), "adamw_vectors")

    outs = []
    for i, small_buf in enumerate(sm):
        d = {n: big[n][i] for n in _MATRICES}
        d.update(_unpack_small(small_buf, w))
        d["conv_w"] = small_buf[5, :DN_CONV * conv_shard].reshape(DN_CONV, conv_shard)
        outs.append(d)
    result = [summed[_LOSS_ROW, 0], grad_x[None]]
    for d in outs:
        d["w_in"] = d["w_in"].T
        for n in _ORDER:
            result.append(d[n][None] if n in _MATRICES or n == "conv_w" else d[n].reshape(w[n].shape))
    return tuple(result)
```

```python
import jax
import jax.numpy as jnp
import numpy as np
from jax import lax
from jax.experimental import pallas as pl
from jax.experimental.pallas import tpu as pltpu

F32, BF16 = jnp.float32, jnp.bfloat16
EPS = 1e-6
D_MODEL = 1024
N_DEV = 8
ATTN_BLOCK = 128
HEAD_PAIR = 128
DN_HEADS = 4
DN_DIM = 128
DN_CHUNK = 64
DN_CONV = 4
ROPE_THETA = 10000.0
D_IN = 2824
D_IN_PAD = 3072
BLK_Q, BLK_Z = 0, 1
BLK_DN, BLK_K, BLK_V, BLK_G = 8, 20, 21, 22
BLK_KV, BLK_G_PAD = 10, 11
VMEM_LIMIT = 56 * 1024 * 1024
NEG = -1e30
ADAM_LR, ADAM_B1, ADAM_B2, ADAM_EPS, ADAM_WD, ADAM_STEP = 0.001, 0.9, 0.999, 1e-08, 0.01, 10
MESH = pl.DeviceIdType.MESH


def _bf(x):
    return x.astype(BF16)


def _dot(a, b):
    return jnp.dot(a, b, preferred_element_type=F32)


def _dot_nt(a, b):
    return lax.dot_general(a, b, (((1,), (1,)), ((), ())), preferred_element_type=F32)


def _dot_tn(a, b):
    return lax.dot_general(a, b, (((0,), (0,)), ((), ())), preferred_element_type=F32)


def _sigmoid(x):
    return 1.0 / (1.0 + jnp.exp(-x))


def _params(sem):
    return pltpu.CompilerParams(dimension_semantics=sem, vmem_limit_bytes=VMEM_LIMIT)


def _mm(x, w, *, form, name, out_dtypes, tn, epi=None, extra=(), tm=512, w_row_block=0, after=None, norm=None,
        norm_bwd=None):
    xs = list(x) if isinstance(x, (list, tuple)) else [x]
    nx = len(xs)
    S, K = xs[0].shape
    shards = w.ndim == 3
    N = (w.shape[2] * N_DEV if shards else w.shape[1]) if form == "nn" else w.shape[-2]
    assert not (shards and form == "nn" and tn != w.shape[2]) and (nx == 1 or (form == "nn" and not shards and norm is None))
    r0 = w_row_block * K
    tm = min(tm, S)
    n_extra, n_out = len(extra), len(out_dtypes)
    tile = lambda width: pl.BlockSpec((tm, width), lambda i: (i, 0))
    whole = lambda a: pl.BlockSpec(a.shape, lambda i, nd=a.ndim: (0,) * nd)
    ins, in_specs = [*xs, w, *extra], [tile(K)] * nx + [whole(w)] + [tile(N)] * n_extra
    if norm is not None:
        ins, in_specs = ins + [norm], in_specs + [whole(norm)]
    if norm_bwd is not None:
        ins, in_specs = ins + list(norm_bwd), in_specs + [tile(N), whole(norm_bwd[1]), tile(N)]
    if after is not None:
        ins, in_specs = ins + [after], in_specs + [whole(after)]
    out_shape = [jax.ShapeDtypeStruct((S, N), dt) for dt in out_dtypes]
    out_specs = [tile(N)] * n_out
    if norm is not None:
        out_shape, out_specs = out_shape + [jax.ShapeDtypeStruct((S, K), BF16)], out_specs + [tile(K)]
    if norm_bwd is not None:
        out_shape, out_specs = out_shape + [jax.ShapeDtypeStruct((1, N), F32)], out_specs + [pl.BlockSpec((1, N), lambda i: (0, 0))]

    def product(xb, w_ref, cols, c):
        if form == "nn" and nx > 1:
            return sum(_dot(part, w_ref[r0 + p * K:r0 + (p + 1) * K, cols]) for p, part in enumerate(xb))
        if form == "nn":
            return _dot(xb, w_ref[c] if shards else w_ref[r0:r0 + K, cols])
        if not shards:
            return _dot_nt(xb, w_ref[cols, :])
        ks = w.shape[2]
        acc = _dot_nt(xb[:, 0:ks], w_ref[0, cols, :])
        for s in range(1, N_DEV):
            acc = acc + _dot_nt(xb[:, s * ks:(s + 1) * ks], w_ref[s, cols, :])
        return acc

    def body(*refs):
        x_ref, w_ref = refs[0], refs[nx]
        extra_refs = refs[nx + 1:nx + 1 + n_extra]
        at = nx + 1 + n_extra
        if norm is not None:
            gain_ref, at = refs[at], at + 1
        if norm_bwd is not None:
            (y_ref, ygain_ref, dres_ref), at = refs[at:at + 3], at + 3
        outs = refs[len(ins):]
        if norm is not None:
            _, xh = _rms_stats(x_ref[...])
            xb = _bf(xh * gain_ref[...])
            outs[n_out][...] = xb
        else:
            xb = _bf(x_ref[...]) if nx == 1 else [_bf(r[...]) for r in refs[:nx]]
        for c in range(N // tn):
            cols = slice(c * tn, (c + 1) * tn)
            acc = product(xb, w_ref, cols, c)
            res = epi(acc, *[r[:, cols] for r in extra_refs]) if epi else (acc,)
            for o, r in zip(outs[:n_out], res):
                o[:, cols] = r.astype(o.dtype)
        if norm_bwd is not None:
            dx, dg = _rms_bwd_tile(y_ref[...], ygain_ref[...], outs[0][...])
            outs[0][...] = dres_ref[...] + dx
            dg_ref = outs[-1]

            @pl.when(pl.program_id(0) == 0)
            def _():
                dg_ref[...] = jnp.zeros_like(dg_ref)

            dg_ref[...] += dg

    return pl.pallas_call(
        body, grid=(S // tm,), name=name, in_specs=in_specs, out_specs=out_specs, out_shape=out_shape,
        compiler_params=_params(("arbitrary",) if norm_bwd is not None else ("parallel",)),
    )(*ins)


def _mm_tn(x, dy, *, name, tm, tn, out_dtype=F32, column_shards=False):
    S, K = x.shape
    N = dy.shape[1]

    def body(x_ref, dy_ref, o_ref):
        o_ref[...] = _dot_tn(_bf(x_ref[...]), _bf(dy_ref[...])).astype(out_dtype)

    if column_shards:
        out_spec = pl.BlockSpec((None, tm, tn), lambda i, j: (j, i, 0))
        out_shape = jax.ShapeDtypeStruct((N // tn, K, tn), out_dtype)
    else:
        out_spec = pl.BlockSpec((tm, tn), lambda i, j: (i, j))
        out_shape = jax.ShapeDtypeStruct((K, N), out_dtype)
    return pl.pallas_call(
        body, grid=(K // tm, N // tn), name=name,
        in_specs=[pl.BlockSpec((S, tm), lambda i, j: (0, i)), pl.BlockSpec((S, tn), lambda i, j: (0, j))],
        out_specs=out_spec, out_shape=out_shape,
        compiler_params=_params(("parallel", "parallel")),
    )(x, dy)


def _rowwise(body, *, tiled, full, out_tiled, out_acc, name, tm=512, smem=()):
    S = tiled[0].shape[0]
    tm = min(tm, S)
    n_in = len(smem) + len(tiled) + len(full)

    def kern(*refs):
        @pl.when(pl.program_id(0) == 0)
        def _():
            for r in refs[n_in + len(out_tiled):]:
                r[...] = jnp.zeros_like(r)
        body(*refs)

    in_specs = [pl.BlockSpec(memory_space=pltpu.SMEM) for _ in smem]
    in_specs += [pl.BlockSpec((tm, a.shape[1]), lambda i: (i, 0)) for a in tiled]
    in_specs += [pl.BlockSpec(a.shape, lambda i, nd=a.ndim: (0,) * nd) for a in full]
    out_specs = [pl.BlockSpec((tm, w), lambda i: (i, 0)) for w, _ in out_tiled]
    out_specs += [pl.BlockSpec(shp, lambda i, nd=len(shp): (0,) * nd) for shp, _ in out_acc]
    out_shape = [jax.ShapeDtypeStruct((S, w), dt) for w, dt in out_tiled]
    out_shape += [jax.ShapeDtypeStruct(shp, dt) for shp, dt in out_acc]
    return pl.pallas_call(
        kern, grid=(S // tm,), name=name, in_specs=in_specs, out_specs=out_specs, out_shape=out_shape,
        compiler_params=_params(("arbitrary",)),
    )(*smem, *tiled, *full)


def _rms_stats(x):
    r = lax.rsqrt(jnp.mean(x * x, axis=-1, keepdims=True) + EPS)
    return r, x * r


def _rmsnorm_fwd(x, g, name):
    def body(x_ref, g_ref, o_ref):
        _, xh = _rms_stats(x_ref[...])
        o_ref[...] = _bf(xh * g_ref[...])

    return _rowwise(body, tiled=[x], full=[g], out_tiled=[(x.shape[1], BF16)], out_acc=[], name=name)[0]


def _rms_bwd_tile(x, g, dxn):
    r, xh = _rms_stats(x)
    dg = jnp.sum(dxn * xh, axis=0, keepdims=True)
    dn = dxn * g
    dx = r * (dn - xh * jnp.mean(dn * xh, axis=-1, keepdims=True))
    return dx, dg


def _ple_and_loss(h2, p, target, w_pg, w_pp, g_ple, g_final):
    S, n = h2.shape
    tm = min(512, S)
    tn = 512

    def body(h2_ref, p_ref, t_ref, wpg_ref, wpp_ref, gple_ref, gfin_ref,
             n3_ref, dh_ref, dgl_ref, dpp_ref, loss_ref, dg_ref, pp, gate, h3):
        @pl.when(pl.program_id(0) == 0)
        def _():
            loss_ref[...] = jnp.zeros_like(loss_ref)
            dg_ref[...] = jnp.zeros_like(dg_ref)

        x = h2_ref[...]
        _, xh = _rms_stats(x)
        n3 = _bf(xh * gple_ref[...])
        n3_ref[...] = n3
        pb = _bf(p_ref[...])
        for c in range(n // tn):
            cols = slice(c * tn, (c + 1) * tn)
            pp[:, cols] = _dot(pb, wpp_ref[:, cols])
            gt = _sigmoid(_dot(n3, wpg_ref[:, cols]))
            gate[:, cols] = gt
            h3[:, cols] = x[:, cols] + gt * pp[:, cols]
        y = h3[...]
        _, yh = _rms_stats(y)
        e = yh * gfin_ref[...] - t_ref[...]
        per_tok = jnp.mean(e * e, axis=-1, keepdims=True)
        loss_ref[...] += 0.5 * jnp.sum(per_tok, axis=0, keepdims=True)
        dh, dg = _rms_bwd_tile(y, gfin_ref[...], e * (1.0 / n))
        dh_ref[...] = dh
        dg_ref[...] += dg
        gt = gate[...]
        dgl_ref[...] = _bf(dh * pp[...] * gt * (1.0 - gt))
        dpp_ref[...] = _bf(dh * gt)

    tile = lambda width: pl.BlockSpec((tm, width), lambda i: (i, 0))
    whole = lambda a: pl.BlockSpec(a.shape, lambda i, nd=a.ndim: (0,) * nd)
    return pl.pallas_call(
        body, grid=(S // tm,), name="ple_and_loss",
        in_specs=[tile(n), tile(p.shape[1]), tile(n), whole(w_pg), whole(w_pp), whole(g_ple), whole(g_final)],
        out_specs=[tile(n), tile(n), tile(n), tile(n), pl.BlockSpec((1, 128), lambda i: (0, 0)), pl.BlockSpec((1, n), lambda i: (0, 0))],
        out_shape=[jax.ShapeDtypeStruct((S, n), BF16), jax.ShapeDtypeStruct((S, n), F32), jax.ShapeDtypeStruct((S, n), BF16),
                   jax.ShapeDtypeStruct((S, n), BF16), jax.ShapeDtypeStruct((1, 128), F32), jax.ShapeDtypeStruct((1, n), F32)],
        scratch_shapes=[pltpu.VMEM((tm, n), F32)] * 3,
        compiler_params=_params(("arbitrary",)),
    )(h2, p, target, w_pg, w_pp, g_ple, g_final)


def _rope_tables(S):
    half = 32
    inv = (1.0 / (np.float32(ROPE_THETA) ** (np.arange(half, dtype=np.float32) * np.float32(2.0 / 64)))).astype(np.float32)
    ang = np.arange(S).astype(np.float32)[:, None] * inv[None, :]
    cos, sin = np.cos(ang), np.sin(ang)
    return jnp.asarray(np.tile(cos, (1, 4))), jnp.asarray(np.concatenate([-sin, sin, -sin, sin], axis=1))


def _attn_common(i, kc, kp, vc, vp, cc, sc, cp, sp):
    lane = lax.broadcasted_iota(jnp.int32, (1, HEAD_PAIR), 1)
    lane_lo = jnp.bitwise_and(lane, 63) < 32
    slot = [lane < 64, lane >= 64]

    def swap_halves(t):
        return jnp.where(lane_lo, pltpu.roll(t, 96, 1), pltpu.roll(t, 32, 1))

    def rope(t, cos, sin):
        return t * cos + swap_halves(t) * sin

    def unrope(d, cos, sin):
        return d * cos + swap_halves(d * sin)

    k2 = jnp.concatenate([rope(kp, cp, sp), rope(kc, cc, sc)], axis=0)
    v2 = jnp.concatenate([vp, vc], axis=0)
    r = lax.broadcasted_iota(jnp.int32, (ATTN_BLOCK, 2 * ATTN_BLOCK), 0)
    c = lax.broadcasted_iota(jnp.int32, (ATTN_BLOCK, 2 * ATTN_BLOCK), 1)
    valid = (c > r) & (c <= r + ATTN_BLOCK) & jnp.logical_or(c >= ATTN_BLOCK, i > 0)
    ks, vs = {}, {}
    for j in range(2):
        kn = jnp.where(slot[j], k2, 0.0)
        vn = jnp.where(slot[j], v2, 0.0)
        for s in range(2):
            ks[j, s] = _bf(kn if s == j else pltpu.roll(kn, 64, 1))
            vs[j, s] = _bf(vn if s == j else pltpu.roll(vn, 64, 1))
    return slot, rope, unrope, valid, ks, vs


def _attn_probs(scores, valid, sink):
    s = jnp.where(valid, scores * 0.125, NEG)
    m = jnp.maximum(jnp.max(s, axis=1, keepdims=True), sink)
    e = jnp.exp(s - m)
    inv_z = 1.0 / (jnp.sum(e, axis=1, keepdims=True) + jnp.exp(sink - m))
    return e * inv_z, jnp.exp(sink - m) * inv_z


def _attn_specs(S):
    nb = S // ATTN_BLOCK
    prev = lambda i: jnp.maximum(i - 1, 0)
    blk = lambda w, col, row=(lambda i: i): pl.BlockSpec((ATTN_BLOCK, w), lambda i: (row(i), col))
    in_specs = [pl.BlockSpec(memory_space=pltpu.SMEM),
                blk(512, BLK_Q), blk(128, BLK_K), blk(128, BLK_K, prev), blk(128, BLK_V), blk(128, BLK_V, prev),
                blk(128, 0), blk(128, 0), blk(128, 0, prev), blk(128, 0, prev)]
    return nb, in_specs


def _attn_fwd(pa, cos, sin, sinks):
    S = pa.shape[0]
    nb, in_specs = _attn_specs(S)

    def body(sinks_ref, q_ref, kc_ref, kp_ref, vc_ref, vp_ref, cc_ref, sc_ref, cp_ref, sp_ref, o_ref):
        i = pl.program_id(0)
        cc, sc = cc_ref[...], sc_ref[...]
        _, rope, _, valid, ks, vs = _attn_common(i, kc_ref[...], kp_ref[...], vc_ref[...], vp_ref[...],
                                                 cc, sc, cp_ref[...], sp_ref[...])
        pair_cols = [slice(HEAD_PAIR * pair, HEAD_PAIR * (pair + 1)) for pair in range(4)]
        qps = [_bf(rope(q_ref[:, cols], cc, sc)) for cols in pair_cols]
        outs = {}

        def head_program(h):
            pair, s = divmod(h, 2)
            j = h // 4
            scores = _dot_nt(qps[pair], ks[j, s])
            yield
            p, _ = _attn_probs(scores, valid, sinks_ref[h])
            outs[h] = _dot(_bf(p), vs[j, s])

        _interleave(head_program(h) for h in range(8))
        for pair, cols in enumerate(pair_cols):
            o_ref[:, cols] = outs[2 * pair] + outs[2 * pair + 1]

    return pl.pallas_call(
        body, grid=(nb,), name="attn_fwd", in_specs=in_specs,
        out_specs=pl.BlockSpec((ATTN_BLOCK, 512), lambda i: (i, 0)),
        out_shape=jax.ShapeDtypeStruct((S, 512), F32),
        compiler_params=_params(("parallel",)),
    )(sinks, pa, pa, pa, pa, pa, cos, sin, cos, sin)


def _attn_bwd(pa, cos, sin, sinks, dcat):
    S = pa.shape[0]
    nb, in_specs = _attn_specs(S)
    in_specs = in_specs + [pl.BlockSpec((ATTN_BLOCK, 512), lambda i: (i, 0))]

    def body(sinks_ref, q_ref, kc_ref, kp_ref, vc_ref, vp_ref, cc_ref, sc_ref, cp_ref, sp_ref, do_ref,
             dq_ref, dk_ref, dv_ref, dsink_ref):
        i = pl.program_id(0)

        @pl.when(i == 0)
        def _():
            dk_ref[...] = jnp.zeros_like(dk_ref)
            dv_ref[...] = jnp.zeros_like(dv_ref)
            dsink_ref[...] = jnp.zeros_like(dsink_ref)

        cc, sc, cp, sp = cc_ref[...], sc_ref[...], cp_ref[...], sp_ref[...]
        slot, rope, unrope, valid, ks, vs = _attn_common(i, kc_ref[...], kp_ref[...], vc_ref[...], vp_ref[...], cc, sc, cp, sp)
        pair_cols = [slice(HEAD_PAIR * pair, HEAD_PAIR * (pair + 1)) for pair in range(4)]
        qps = [_bf(rope(q_ref[:, cols], cc, sc)) for cols in pair_cols]
        dobs = [_bf(do_ref[:, cols]) for cols in pair_cols]
        dqs, dks, dvs = {}, {}, {}

        def head_program(h):
            pair, s = divmod(h, 2)
            j = h // 4
            qp, dob = qps[pair], dobs[pair]
            scores = _dot_nt(qp, ks[j, s])
            dp = _dot_nt(dob, vs[j, s])
            yield
            p, p_sink = _attn_probs(scores, valid, sinks_ref[h])
            dr = jnp.sum(p * dp, axis=1, keepdims=True)
            ds = _bf(p * (dp - dr) * 0.125)
            dsink_ref[h:h + 1, :] += -jnp.sum(p_sink * dr, axis=0, keepdims=True)
            dqs[h] = _dot(ds, ks[j, s])
            dk_h = _dot_tn(ds, qp)
            dv_h = _dot_tn(_bf(p), dob)
            yield
            dk_h, dv_h = jnp.where(slot[s], dk_h, 0.0), jnp.where(slot[s], dv_h, 0.0)
            if s != j:
                dk_h, dv_h = pltpu.roll(dk_h, 64, 1), pltpu.roll(dv_h, 64, 1)
            dks[h], dvs[h] = dk_h, dv_h

        _interleave(head_program(h) for h in range(8))
        dk2 = sum((dks[h] for h in range(1, 8)), dks[0])
        dv2 = sum((dvs[h] for h in range(1, 8)), dvs[0])
        for pair, cols in enumerate(pair_cols):
            dq_ref[:, cols] = _bf(unrope(dqs[2 * pair] + dqs[2 * pair + 1], cc, sc))
        cur = pl.ds(pl.multiple_of(i * ATTN_BLOCK, ATTN_BLOCK), ATTN_BLOCK)
        dk_ref[cur, :] += unrope(dk2[ATTN_BLOCK:], cc, sc)
        dv_ref[cur, :] += dv2[ATTN_BLOCK:]

        @pl.when(i > 0)
        def _():
            prv = pl.ds(pl.multiple_of((i - 1) * ATTN_BLOCK, ATTN_BLOCK), ATTN_BLOCK)
            dk_ref[prv, :] += unrope(dk2[:ATTN_BLOCK], cp, sp)
            dv_ref[prv, :] += dv2[:ATTN_BLOCK]

    whole = lambda w: pl.BlockSpec((S, w), lambda i: (0, 0))
    return pl.pallas_call(
        body, grid=(nb,), name="attn_bwd", in_specs=in_specs,
        out_specs=[pl.BlockSpec((ATTN_BLOCK, 512), lambda i: (i, BLK_Q)), whole(128), whole(128),
                   pl.BlockSpec((8, 128), lambda i: (0, 0))],
        out_shape=[jax.ShapeDtypeStruct((S, D_IN_PAD), BF16), jax.ShapeDtypeStruct((S, 128), F32),
                   jax.ShapeDtypeStruct((S, 128), F32), jax.ShapeDtypeStruct((8, 128), F32)],
        compiler_params=_params(("arbitrary",)),
    )(sinks, pa, pa, pa, pa, pa, cos, sin, cos, sin, dcat)


CONV_ROWS = 512
CONV_PAD = 8


def _conv_silu(scr, w, r0):
    y = w[3:4, :] * scr[pl.ds(CONV_PAD + r0, CONV_ROWS), :]
    for j in range(DN_CONV - 1):
        y = y + w[j:j + 1, :] * scr[pl.ds(CONV_PAD + r0 - 3 + j, CONV_ROWS), :]
    return y


def _dn_prep_fwd(pd, conv_w):
    S = pd.shape[0]
    assert S % CONV_ROWS == 0

    def body(x_ref, w_ref, o_ref, scr):
        b = pl.program_id(0)
        scr[0:CONV_PAD, :] = jnp.zeros((CONV_PAD, DN_DIM), F32)
        scr[pl.ds(CONV_PAD, S), :] = x_ref[...]
        w = w_ref[...]
        q_scale = jnp.where(b < DN_HEADS, DN_DIM ** -0.5, 1.0)
        for r0 in range(0, S, CONV_ROWS):
            y = _conv_silu(scr, w, r0)
            a = y * _sigmoid(y)
            rs = lax.rsqrt(jnp.sum(a * a, axis=1, keepdims=True) + EPS)
            o_ref[pl.ds(r0, CONV_ROWS), :] = a * jnp.where(b < 2 * DN_HEADS, rs * q_scale, 1.0)

    col = pl.BlockSpec((S, DN_DIM), lambda b: (0, b))
    return pl.pallas_call(
        body, grid=(3 * DN_HEADS,), name="dn_prep_fwd",
        in_specs=[pl.BlockSpec((S, DN_DIM), lambda b: (0, BLK_DN + b)), pl.BlockSpec((DN_CONV, DN_DIM), lambda b: (0, b))],
        out_specs=col,
        out_shape=jax.ShapeDtypeStruct((S, 3 * DN_HEADS * DN_DIM), F32),
        scratch_shapes=[pltpu.VMEM((S + CONV_PAD, DN_DIM), F32)],
        compiler_params=_params(("parallel",)),
    )(pd, conv_w)


def _dn_prep_bwd(pd, conv_w, dqkv, dproj):
    S = pd.shape[0]

    def body(x_ref, w_ref, d_ref, _, dx_ref, dw_ref, scr, dscr):
        b = pl.program_id(0)
        scr[0:CONV_PAD, :] = jnp.zeros((CONV_PAD, DN_DIM), F32)
        scr[pl.ds(CONV_PAD, S), :] = x_ref[...]
        dscr[pl.ds(S, CONV_PAD), :] = jnp.zeros((CONV_PAD, DN_DIM), F32)
        w = w_ref[...]
        q_scale = jnp.where(b < DN_HEADS, DN_DIM ** -0.5, 1.0)
        is_qk = b < 2 * DN_HEADS
        dw = [jnp.zeros((1, DN_DIM), F32) for _ in range(DN_CONV)]
        for r0 in range(0, S, CONV_ROWS):
            y = _conv_silu(scr, w, r0)
            sg = _sigmoid(y)
            a = y * sg
            dout = d_ref[pl.ds(r0, CONV_ROWS), :]
            rs = lax.rsqrt(jnp.sum(a * a, axis=1, keepdims=True) + EPS)
            da_qk = q_scale * rs * (dout - a * (rs * rs) * jnp.sum(dout * a, axis=1, keepdims=True))
            dy = jnp.where(is_qk, da_qk, dout) * (sg * (1.0 + y * (1.0 - sg)))
            dscr[pl.ds(r0, CONV_ROWS), :] = dy
            for j in range(DN_CONV):
                dw[j] = dw[j] + jnp.sum(dy * scr[pl.ds(CONV_PAD + r0 - 3 + j, CONV_ROWS), :], axis=0, keepdims=True)
        for j in range(DN_CONV):
            dw_ref[j:j + 1, :] = dw[j]
        for r0 in range(0, S, CONV_ROWS):
            dx = w[3:4, :] * dscr[pl.ds(r0, CONV_ROWS), :]
            for j in range(DN_CONV - 1):
                dx = dx + w[j:j + 1, :] * dscr[pl.ds(r0 + 3 - j, CONV_ROWS), :]
            dx_ref[pl.ds(r0, CONV_ROWS), :] = _bf(dx)

    col = pl.BlockSpec((S, DN_DIM), lambda b: (0, b))
    proj_col = pl.BlockSpec((S, DN_DIM), lambda b: (0, BLK_DN + b))
    wcol = pl.BlockSpec((DN_CONV, DN_DIM), lambda b: (0, b))
    return pl.pallas_call(
        body, grid=(3 * DN_HEADS,), name="dn_prep_bwd",
        in_specs=[proj_col, wcol, col, pl.BlockSpec(memory_space=pl.ANY)], out_specs=[proj_col, wcol],
        out_shape=[jax.ShapeDtypeStruct(dproj.shape, dproj.dtype), jax.ShapeDtypeStruct((DN_CONV, 3 * DN_HEADS * DN_DIM), F32)],
        scratch_shapes=[pltpu.VMEM((S + CONV_PAD, DN_DIM), F32), pltpu.VMEM((S + CONV_PAD, DN_DIM), F32)],
        input_output_aliases={3: 0},
        compiler_params=_params(("parallel",)),
    )(pd, conv_w, dqkv, dproj)


CPAD = 128
CHUNKS_LOCAL = 4
CHUNKS_SCAN = 4


def _chunk_masks():
    ii = lax.broadcasted_iota(jnp.int32, (DN_CHUNK, CPAD), 0)
    jj = lax.broadcasted_iota(jnp.int32, (DN_CHUNK, CPAD), 1)
    return ii, jj


def _rows_pad(a):
    return jnp.concatenate([a, jnp.zeros_like(a)], axis=0)


def _hi_lo(a):
    hi = _bf(a)
    return hi, _bf(a - hi.astype(F32))


def _double_step(t, p):
    C = DN_CHUNK
    th, tl = _hi_lo(t)
    ph, pl_ = _hi_lo(p)
    r1 = _dot(jnp.concatenate([th, tl, ph, pl_], axis=0), _rows_pad(ph))
    r2 = _dot(jnp.concatenate([th, ph], axis=0), _rows_pad(pl_))
    return t + (r1[:C] + r1[C:2 * C] + r2[:C]), r1[2 * C:3 * C] + r1[3 * C:] + r2[C:]


def _dot3_nt(a, b):
    C = DN_CHUNK
    ah, al = _hi_lo(a)
    bh, bl = _hi_lo(b)
    r1 = _dot_nt(jnp.concatenate([ah, al], axis=0), _rows_pad(bh))
    return r1[:C] + r1[C:] + _dot_nt(ah, _rows_pad(bl))


def _dot3_tn(a, b):
    C = DN_CHUNK
    ah, al = _hi_lo(a)
    bh, bl = _hi_lo(b)
    return _dot_tn(jnp.concatenate([ah, al, ah], axis=0), jnp.concatenate([bh, bh, bl], axis=0))[:C]


def _interleave(programs):
    programs = list(programs)
    while programs:
        alive = []
        for prog in programs:
            try:
                next(prog)
                alive.append(prog)
            except StopIteration:
                pass
        programs = alive


def _col_to_row(col, ii, jj):
    return jnp.sum(jnp.where(ii == jj, col, 0.0), axis=0, keepdims=True)


def _row_to_col(row, ii, jj):
    return jnp.sum(jnp.where(ii == jj, row, 0.0), axis=1, keepdims=True)


def _decay(gc_col, ii, jj):
    diff = gc_col - _col_to_row(gc_col, ii, jj)
    return jnp.where(jj <= ii, jnp.exp(jnp.where(jj <= ii, diff, 0.0)), 0.0)


def _softplus(x):
    return jnp.maximum(x, 0.0) + jnp.log(1.0 + jnp.exp(-jnp.abs(x)))


def _head(h):
    return slice(DN_DIM * h, DN_DIM * (h + 1))


def _dn_chunk_fwd(qkv, pg, a_log, dt_bias):
    S = qkv.shape[0]
    C = DN_CHUNK
    G = CHUNKS_LOCAL
    R = G * C
    steps = S // R

    def body(alog_ref, dtb_ref, qkv_ref, pg_ref, w_ref, u_ref, qg_ref, kd_ref, a_ref, t_ref, gcs_ref):
        ii, jj = _chunk_masks()
        lane = lax.broadcasted_iota(jnp.int32, (1, 128), 1)
        eye = (ii == jj).astype(F32)
        gcs_parts = [[] for _ in range(G)]

        def head_program(chunk, h):
            rows = slice(chunk * C, (chunk + 1) * C)
            q, k, v = qkv_ref[rows, _head(h)], qkv_ref[rows, _head(DN_HEADS + h)], qkv_ref[rows, _head(2 * DN_HEADS + h)]
            beta = _sigmoid(pg_ref[rows, h:h + 1])
            g_col = -jnp.exp(alog_ref[h]) * _softplus(pg_ref[rows, DN_HEADS + h:DN_HEADS + h + 1] + dtb_ref[h])
            g_row = _col_to_row(g_col, ii, jj)
            gc_col = jnp.sum(jnp.where(jj <= ii, g_row, 0.0), axis=1, keepdims=True)
            dec = _decay(gc_col, ii, jj)
            eg = jnp.exp(gc_col)
            kb, vb = k * beta, v * beta
            k_rows = _rows_pad(_bf(k))
            kk = _dot_nt(_bf(kb), k_rows)
            qk = _dot_nt(_bf(q), k_rows)
            yield
            t, pw = eye, -jnp.where(jj < ii, kk * dec, 0.0)
            for _ in range(6):
                t, pw = _double_step(t, pw)
                yield
            tb = _bf(t)
            u_ref[rows, _head(h)] = _dot(tb, _rows_pad(_bf(vb)))
            w_ref[rows, _head(h)] = _bf(_dot(tb, _rows_pad(_bf(kb * eg))))
            a_ref[h, rows] = _bf(qk * dec)
            t_ref[h, rows] = t
            qg_ref[rows, _head(h)] = _bf(q * eg)
            kd_ref[rows, _head(h)] = _bf(k * jnp.exp(gc_col[C - 1:C, :] - gc_col))
            gcs_parts[chunk].append(jnp.where(lane == h, gc_col, 0.0) + jnp.where(lane == DN_HEADS + h, beta, 0.0)
                                    + jnp.where(lane == 2 * DN_HEADS + h, g_col, 0.0))

        _interleave(head_program(chunk, h) for chunk in range(G) for h in range(DN_HEADS))
        for chunk in range(G):
            gcs_ref[chunk * C:(chunk + 1) * C, :] = sum(gcs_parts[chunk][1:], gcs_parts[chunk][0])

    smem = pl.BlockSpec(memory_space=pltpu.SMEM)
    wide = pl.BlockSpec((R, 512), lambda n: (n, 0))
    sq = pl.BlockSpec((DN_HEADS, R, CPAD), lambda n: (0, n, 0))
    narrow = pl.BlockSpec((R, 128), lambda n: (n, 0))
    f = lambda *shp: jax.ShapeDtypeStruct(shp, F32)
    b = lambda *shp: jax.ShapeDtypeStruct(shp, BF16)
    return pl.pallas_call(
        body, grid=(steps,), name="dn_chunk_fwd",
        in_specs=[smem, smem, pl.BlockSpec((R, 1536), lambda n: (n, 0)), pl.BlockSpec((R, 128), lambda n: (n, BLK_G))],
        out_specs=[wide, wide, wide, wide, sq, sq, narrow],
        out_shape=[b(S, 512), f(S, 512), b(S, 512), b(S, 512), b(DN_HEADS, S, CPAD), f(DN_HEADS, S, CPAD), f(S, 128)],
        compiler_params=_params(("parallel",)),
    )(a_log, dt_bias, qkv, pg)


def _gated_norm(o, z, gn):
    r, oh = _rms_stats(o)
    return oh * gn * (z * _sigmoid(z))


def _dn_scan_fwd(w, u, qg, kd, a, gcs, pz, gn):
    S = w.shape[0]
    C = DN_CHUNK
    nc = S // C
    G = CHUNKS_SCAN
    R = G * C

    def body(w_ref, u_ref, qg_ref, kd_ref, a_ref, gcs_ref, z_ref, gn_ref, o_ref, vn_ref, sst_ref, out_ref, state):
        @pl.when(pl.program_id(0) == 0)
        def _():
            state[...] = jnp.zeros_like(state)

        def head_program(chunk, h):
            hs = _head(h)
            rows = slice(chunk * C, (chunk + 1) * C)
            s_in = state[h]
            sst_ref[chunk, h] = s_in
            sb = _bf(s_in)
            w_s = _dot(w_ref[rows, hs], sb)
            q_s = _dot(qg_ref[rows, hs], sb)
            yield
            vn = u_ref[rows, hs] - w_s
            vnb = _bf(vn)
            o = q_s + _dot(a_ref[h, rows], _rows_pad(vnb))
            k_v = _dot_tn(kd_ref[rows, hs], vnb)
            yield
            state[h] = s_in * jnp.exp(gcs_ref[(chunk + 1) * C - 1:(chunk + 1) * C, h:h + 1]) + k_v
            o_ref[rows, hs] = o
            vn_ref[rows, hs] = vn
            out_ref[rows, hs] = _gated_norm(o, z_ref[rows, hs], gn_ref[...])

        for chunk in range(G):
            _interleave(head_program(chunk, h) for h in range(DN_HEADS))

    wide = pl.BlockSpec((R, 512), lambda n: (n, 0))
    f = lambda *shp: jax.ShapeDtypeStruct(shp, F32)
    return pl.pallas_call(
        body, grid=(nc // G,), name="dn_scan_fwd",
        in_specs=[wide, wide, wide, wide, pl.BlockSpec((DN_HEADS, R, CPAD), lambda n: (0, n, 0)),
                  pl.BlockSpec((R, 128), lambda n: (n, 0)), pl.BlockSpec((R, 512), lambda n: (n, BLK_Z)),
                  pl.BlockSpec((1, DN_DIM), lambda n: (0, 0))],
        out_specs=[wide, wide, pl.BlockSpec((G, DN_HEADS, DN_DIM, DN_DIM), lambda n: (n, 0, 0, 0)), wide],
        out_shape=[f(S, 512), f(S, 512), f(nc, DN_HEADS, DN_DIM, DN_DIM), f(S, 512)],
        scratch_shapes=[pltpu.VMEM((DN_HEADS, DN_DIM, DN_DIM), F32)],
        compiler_params=_params(("arbitrary",)),
    )(w, u, qg, kd, a, gcs, pz, gn)


def _dn_scan_bwd(dcat, o, pz, gn, sst, vnew, w, qg, kd, a, gcs, dproj):
    S = o.shape[0]
    C = DN_CHUNK
    G = CHUNKS_SCAN
    R = G * C
    steps = S // R

    def body(dy_ref, o_ref, z_ref, gn_ref, sst_ref, vn_ref, w_ref, qg_ref, kd_ref, a_ref, gcs_ref, _,
             du_ref, dw_ref, dqg_ref, dkd_ref, da_ref, dz_ref, dsc_ref, dgn_ref, dstate):
        @pl.when(pl.program_id(0) == 0)
        def _():
            dstate[...] = jnp.zeros_like(dstate)
            dgn_ref[...] = jnp.zeros_like(dgn_ref)

        gn_ = gn_ref[...]
        lane = lax.broadcasted_iota(jnp.int32, (C, 128), 1)
        row = lax.broadcasted_iota(jnp.int32, (C, 128), 0)
        dgn_parts = []

        def head_program(chunk, h, dsc_parts):
            hs = _head(h)
            rows = slice(chunk * C, (chunk + 1) * C)
            ov, z, dout = o_ref[rows, hs], z_ref[rows, hs], dy_ref[rows, hs]
            r, oh = _rms_stats(ov)
            sg = _sigmoid(z)
            don = dout * (z * sg)
            dz_ref[rows, hs] = _bf(dout * (oh * gn_) * (sg * (1.0 + z * (1.0 - sg))))
            dgn_parts.append(jnp.sum(don * oh, axis=0, keepdims=True))
            dn = don * gn_
            do = _bf(r * (dn - oh * jnp.mean(dn * oh, axis=-1, keepdims=True)))
            s_in = sst_ref[chunk, h]
            sb = _bf(s_in)
            ds_out = dstate[h]
            dsb = _bf(ds_out)
            vnb = _bf(vn_ref[rows, hs])
            wb, qgb, kdb, ab = w_ref[rows, hs], qg_ref[rows, hs], kd_ref[rows, hs], a_ref[h, rows]
            dvn = _dot_tn(ab, do)[:C] + _dot(kdb, dsb)
            da_ref[h, rows] = _dot_nt(do, _rows_pad(vnb))
            dqg_ref[rows, hs] = _dot_nt(do, sb)
            dkd_ref[rows, hs] = _dot_nt(vnb, dsb)
            q_do = _dot_tn(qgb, do)
            yield
            dvnb = _bf(dvn)
            dw_ref[rows, hs] = _bf(-_dot_nt(dvnb, sb))
            w_dvn = _dot_tn(wb, dvnb)
            du_ref[rows, hs] = dvnb
            yield
            d_last = jnp.exp(gcs_ref[(chunk + 1) * C - 1:(chunk + 1) * C, h:h + 1])
            dd = jnp.sum(jnp.sum(ds_out * s_in, axis=1, keepdims=True), axis=0, keepdims=True)
            dsc_parts.append(jnp.where((lane == h) & (row == C - 1), dd * d_last, 0.0))
            dstate[h] = ds_out * d_last + q_do - w_dvn

        for chunk in reversed(range(G)):
            dsc_parts = []
            _interleave(head_program(chunk, h, dsc_parts) for h in range(DN_HEADS))
            dsc_ref[chunk * C:(chunk + 1) * C, :] = sum(dsc_parts[1:], dsc_parts[0])
        dgn_ref[...] += sum(dgn_parts[1:], dgn_parts[0])

    rev = lambda n: steps - 1 - n
    wide = pl.BlockSpec((R, 512), lambda n: (rev(n), 0))
    z_spec = pl.BlockSpec((R, 512), lambda n: (rev(n), BLK_Z))
    sq = pl.BlockSpec((DN_HEADS, R, CPAD), lambda n: (0, rev(n), 0))
    narrow = pl.BlockSpec((R, 128), lambda n: (rev(n), 0))
    gn_spec = pl.BlockSpec((1, DN_DIM), lambda n: (0, 0))
    f = lambda *shp: jax.ShapeDtypeStruct(shp, F32)
    b = lambda *shp: jax.ShapeDtypeStruct(shp, BF16)
    return pl.pallas_call(
        body, grid=(steps,), name="dn_scan_bwd",
        in_specs=[pl.BlockSpec((R, 512), lambda n: (rev(n), 1)), wide, z_spec, gn_spec,
                  pl.BlockSpec((G, DN_HEADS, DN_DIM, DN_DIM), lambda n: (rev(n), 0, 0, 0)),
                  wide, wide, wide, wide, sq, narrow, pl.BlockSpec(memory_space=pl.ANY)],
        out_specs=[wide, wide, wide, wide, sq, z_spec, narrow, gn_spec],
        out_shape=[b(S, 512), b(S, 512), f(S, 512), f(S, 512), f(DN_HEADS, S, CPAD),
                   jax.ShapeDtypeStruct(dproj.shape, dproj.dtype), f(S, 128), f(1, DN_DIM)],
        scratch_shapes=[pltpu.VMEM((DN_HEADS, DN_DIM, DN_DIM), F32)],
        input_output_aliases={11: 5},
        compiler_params=_params(("arbitrary",)),
    )(dcat, o, pz, gn, sst, vnew, w, qg, kd, a, gcs, dproj)


def _dn_chunk_bwd(qkv, pg, t_inv, gcs, du, dw, dqg, dkd, da, dsc, a_log, dt_bias, dproj):
    S = qkv.shape[0]
    C = DN_CHUNK
    G = CHUNKS_LOCAL
    R = G * C

    def body(alog_ref, dtb_ref, qkv_ref, pg_ref, t_ref, gcs_ref, du_ref, dw_ref, dqg_ref, dkd_ref, da_ref, dsc_ref, _,
             dqkv_ref, dpg_ref, acc_ref):
        @pl.when(pl.program_id(0) == 0)
        def _():
            acc_ref[...] = jnp.zeros_like(acc_ref)

        ii, jj = _chunk_masks()
        lane = lax.broadcasted_iota(jnp.int32, (1, 128), 1)
        row8 = lax.broadcasted_iota(jnp.int32, (8, 128), 0)
        lane8 = lax.broadcasted_iota(jnp.int32, (8, 128), 1)
        rowc = lax.broadcasted_iota(jnp.int32, (C, 1), 0)
        tril, strict = jj <= ii, jj < ii
        dpg_parts, acc_parts = [[] for _ in range(G)], []

        def head_program(chunk, h):
            rows = slice(chunk * C, (chunk + 1) * C)
            q, k, v = qkv_ref[rows, _head(h)], qkv_ref[rows, _head(DN_HEADS + h)], qkv_ref[rows, _head(2 * DN_HEADS + h)]
            gc_col, beta, g_col = gcs_ref[rows, h:h + 1], gcs_ref[rows, DN_HEADS + h:DN_HEADS + h + 1], \
                gcs_ref[rows, 2 * DN_HEADS + h:2 * DN_HEADS + h + 1]
            dec = _decay(gc_col, ii, jj)
            eg = jnp.exp(gc_col)
            g_last = gc_col[C - 1:C, :]
            ek = jnp.exp(g_last - gc_col)
            kb, vb = k * beta, v * beta
            kbg = kb * eg
            qb, kbb = _bf(q), _bf(kb)
            k_rows = _rows_pad(_bf(k))
            t = t_ref[h, rows]
            tb = _bf(t)
            dub, dwb = du_ref[rows, _head(h)], dw_ref[rows, _head(h)]
            dqg_, dkd_ = dqg_ref[rows, _head(h)], dkd_ref[rows, _head(h)]
            dt = _dot_nt(dub, _rows_pad(_bf(vb))) + _dot_nt(dwb, _rows_pad(_bf(kbg)))
            t_du_dw = _dot_tn(tb, jnp.concatenate([dub, dwb], axis=1))
            dvb, dkbg = t_du_dw[:C, :DN_DIM], t_du_dw[:C, DN_DIM:]
            kk = _dot_nt(kbb, k_rows)
            qk = _dot_nt(qb, k_rows)
            yield
            dt_t = _dot3_nt(dt, t)
            yield
            dl = -_dot3_tn(t, dt_t)
            yield
            dm = jnp.where(strict, dl * dec, 0.0)
            dqk = jnp.where(tril, da_ref[h, rows] * dec, 0.0)
            gmat = dm * kk + dqk * qk
            dgc = jnp.sum(gmat, axis=1, keepdims=True) - _row_to_col(jnp.sum(gmat, axis=0, keepdims=True), ii, jj)
            dmb, dqkb = _bf(dm), _bf(dqk)
            dkb = _dot(dmb, k_rows) + dkbg * eg
            dk = _dot_tn(jnp.concatenate([dmb, dqkb], axis=0), jnp.concatenate([kbb, qb], axis=0))[:C] + dkd_ * ek
            dq = _dot(dqkb, k_rows) + dqg_ * eg
            yield
            tk = jnp.sum(dkd_ * k * ek, axis=1, keepdims=True)
            dgc = dgc + jnp.sum(dqg_ * q * eg, axis=1, keepdims=True) - tk + jnp.sum(dkbg * kbg, axis=1, keepdims=True)
            dgl = jnp.sum(tk, axis=0, keepdims=True) + dsc_ref[(chunk + 1) * C - 1:(chunk + 1) * C, h:h + 1]
            dgc = dgc + jnp.where(rowc == C - 1, dgl, 0.0)
            dk = dk + dkb * beta
            dbeta = jnp.sum(dkb * k, axis=1, keepdims=True) + jnp.sum(dvb * v, axis=1, keepdims=True)
            dqkv_ref[rows, _head(h)] = dq
            dqkv_ref[rows, _head(DN_HEADS + h)] = dk
            dqkv_ref[rows, _head(2 * DN_HEADS + h)] = dvb * beta
            dg_col = jnp.sum(jnp.where(jj >= ii, _col_to_row(dgc, ii, jj), 0.0), axis=1, keepdims=True)
            db = dbeta * beta * (1.0 - beta)
            da_in = dg_col * (-jnp.exp(alog_ref[h])) * _sigmoid(pg_ref[rows, DN_HEADS + h:DN_HEADS + h + 1] + dtb_ref[h])
            dpg_parts[chunk].append(jnp.where(lane == h, db, 0.0) + jnp.where(lane == DN_HEADS + h, da_in, 0.0))
            acc_parts.append(jnp.where((row8 == 0) & (lane8 == h), jnp.sum(dg_col * g_col, axis=0, keepdims=True), 0.0)
                             + jnp.where((row8 == 1) & (lane8 == h), jnp.sum(da_in, axis=0, keepdims=True), 0.0))

        _interleave(head_program(chunk, h) for chunk in range(G) for h in range(DN_HEADS))
        for chunk in range(G):
            dpg = sum(dpg_parts[chunk][1:], dpg_parts[chunk][0])
            dpg_ref[chunk * C:(chunk + 1) * C, :] = _bf(jnp.concatenate([dpg, jnp.zeros_like(dpg)], axis=1))
        acc_ref[...] += sum(acc_parts[1:], acc_parts[0])

    smem = pl.BlockSpec(memory_space=pltpu.SMEM)
    wide = pl.BlockSpec((R, 512), lambda n: (n, 0))
    sq = pl.BlockSpec((DN_HEADS, R, CPAD), lambda n: (0, n, 0))
    narrow = pl.BlockSpec((R, 128), lambda n: (n, 0))
    qkv_spec = pl.BlockSpec((R, 1536), lambda n: (n, 0))
    f = lambda *shp: jax.ShapeDtypeStruct(shp, F32)
    return pl.pallas_call(
        body, grid=(S // R,), name="dn_chunk_bwd",
        in_specs=[smem, smem, qkv_spec, pl.BlockSpec((R, 128), lambda n: (n, BLK_G)), sq, narrow, wide, wide, wide, wide, sq,
                  narrow, pl.BlockSpec(memory_space=pl.ANY)],
        out_specs=[qkv_spec, pl.BlockSpec((R, 256), lambda n: (n, BLK_G_PAD)), pl.BlockSpec((8, 128), lambda n: (0, 0))],
        out_shape=[f(S, 1536), jax.ShapeDtypeStruct(dproj.shape, dproj.dtype), f(8, 128)],
        input_output_aliases={12: 1},
        compiler_params=_params(("arbitrary",)),
    )(a_log, dt_bias, qkv, pg, t_inv, gcs, du, dw, dqg, dkd, da, dsc, dproj)


def _fill_kv(dk, dv, dproj):
    S = dk.shape[0]
    tm = min(512, S)

    def body(dk_ref, dv_ref, _, o_ref):
        o_ref[...] = _bf(jnp.concatenate([dk_ref[...], dv_ref[...]], axis=1))

    tile = pl.BlockSpec((tm, 128), lambda i: (i, 0))
    return pl.pallas_call(
        body, grid=(S // tm,), name="fill_kv",
        in_specs=[tile, tile, pl.BlockSpec(memory_space=pl.ANY)],
        out_specs=pl.BlockSpec((tm, 256), lambda i: (i, BLK_KV)),
        out_shape=jax.ShapeDtypeStruct(dproj.shape, dproj.dtype),
        input_output_aliases={2: 0},
        compiler_params=_params(("parallel",)),
    )(dk, dv, dproj)


def _w_in_to_internal(wt):
    return jnp.concatenate([wt[0:512], wt[2304:2816], wt[768:2304], wt[512:768], wt[2816:2824],
                            jnp.zeros((D_IN_PAD - D_IN, wt.shape[1]), wt.dtype)], axis=0)


def _w_in_from_internal(gt):
    return jnp.concatenate([gt[0:512], gt[2560:2816], gt[1024:2560], gt[512:1024], gt[2816:2824]], axis=0)


def _local_step(x, p, target, wts, first_weights, other_weights, ship_early):
    S = x.shape[0]
    cos, sin = _rope_tables(S)
    sinks, a_log, dt_bias = wts["sinks"].reshape(8), wts["a_log"].reshape(4), wts["dt_bias"].reshape(4)
    gn = wts["dn_norm"].reshape(1, DN_DIM)
    add = lambda acc, res: (acc + res,)

    u = _rmsnorm_fwd(x, wts["norm_mix"], "norm_mix_fwd")
    w_in_t, conv_w = first_weights(u)
    proj, = _mm(u, w_in_t, form="nt", name="in_proj", out_dtypes=[F32], tn=512)
    attn = _attn_fwd(proj, cos, sin, sinks)
    qkv = _dn_prep_fwd(proj, conv_w)
    cw, cu, cqg, ckd, ca, ct, gcs = _dn_chunk_fwd(qkv, proj, a_log, dt_bias)
    o, vnew, sst, dn_out = _dn_scan_fwd(cw, cu, cqg, ckd, ca, gcs, proj, gn)
    w_o, = other_weights(("w_o",), dn_out)
    h1, = _mm([attn, dn_out], w_o, form="nn", name="out_proj", out_dtypes=[F32], tn=512, epi=add, extra=[x])

    def relu2(acc):
        r = jnp.maximum(acc, 0.0)
        return r * r, r

    w_up, = other_weights(("w_up",), h1)
    hid, relu, m = _mm(h1, w_up, form="nn", name="mlp_up", out_dtypes=[BF16, BF16], tn=512, epi=relu2, norm=wts["norm_mlp"])
    w_down, = other_weights(("w_down",), hid)
    h2, = _mm(hid, w_down, form="nn", name="mlp_down", out_dtypes=[F32], tn=512, epi=add, extra=[h1])
    w_pg, w_pp = other_weights(("w_ple_gate", "w_ple_proj"), h2)
    n3, dh3, dgl, dpp, loss, d_norm_final = _ple_and_loss(h2, p, target, w_pg, w_pp, wts["norm_ple"],
                                                         wts["norm_final"].reshape(1, D_MODEL))
    g = {"norm_final": d_norm_final}
    early = {"w_ple_gate": _mm_tn(n3, dgl, name="d_w_ple_gate", tm=512, tn=1024, out_dtype=BF16).reshape(N_DEV, 128, 1024),
             "w_ple_proj": _mm_tn(p, dpp, name="d_w_ple_proj", tm=256, tn=128, out_dtype=BF16, column_shards=True)}
    dh2, g["norm_ple"] = _mm(dgl, w_pg, form="nt", name="d_n3", out_dtypes=[F32], tn=512,
                             norm_bwd=(h2, wts["norm_ple"], dh3))
    d_act, = _mm(dh2, w_down, form="nt", name="d_hidden", out_dtypes=[BF16], tn=512,
                 epi=lambda acc, r: (acc * (2.0 * r.astype(F32)),), extra=[relu])
    early["w_down"] = _mm_tn(hid, dh2, name="d_w_down", tm=512, tn=1024, out_dtype=BF16).reshape(N_DEV, 512, 1024)
    early["w_up"] = _mm_tn(m, d_act, name="d_w_up", tm=1024, tn=512, out_dtype=BF16, column_shards=True)
    token = ship_early(early)
    dh1, g["norm_mlp"] = _mm(d_act, w_up, form="nt", name="d_m", out_dtypes=[F32], tn=512, after=token,
                             norm_bwd=(h1, wts["norm_mlp"], dh2))
    dcat, = _mm(dh1, w_o, form="nt", name="d_cat", out_dtypes=[F32], tn=512)
    d_w_o = jnp.concatenate([_mm_tn(attn, dh1, name="d_w_o_attn", tm=512, tn=512, out_dtype=BF16),
                             _mm_tn(dn_out, dh1, name="d_w_o_dn", tm=512, tn=512, out_dtype=BF16)], axis=0)
    token = ship_early({"w_o": d_w_o.reshape(N_DEV, 128, 1024)})
    dproj, dk, dv, dsinks = _attn_bwd(proj, cos, sin, sinks + token[0, 0], dcat)
    g["sinks"] = dsinks[:, 0].reshape(1, 8)
    du_, dw_, dqg, dkd, da, dproj, dsc, g["dn_norm"] = _dn_scan_bwd(dcat, o, proj, gn, sst, vnew, cw, cqg, ckd, ca, gcs, dproj)
    dqkv, dproj, gate_acc = _dn_chunk_bwd(qkv, proj, ct, gcs, du_, dw_, dqg, dkd, da, dsc, a_log, dt_bias, dproj)
    g["a_log"], g["dt_bias"] = gate_acc[0:1, 0:4], gate_acc[1:2, 0:4]
    dproj, g["conv_w"] = _dn_prep_bwd(proj, conv_w, dqkv, dproj)
    dproj = _fill_kv(dk, dv, dproj)
    token = ship_early({"w_in": _mm_tn(dproj, u, name="d_w_in", tm=512, tn=1024)})
    grad_x, g["norm_mix"] = _mm(dproj, w_in_t, form="nn", name="d_u", out_dtypes=[F32], tn=512, after=token,
                                norm_bwd=(x, wts["norm_mix"], dh1))
    return loss, grad_x, g


def _peer(k):
    x, y, c = lax.axis_index("x"), lax.axis_index("y"), lax.axis_index("c")
    px = 1 - x if k & 4 else x
    py = 1 - y if k & 2 else y
    pc = 1 - c if k & 1 else c
    return (px, py, pc), 4 * px + 2 * py + pc


def _exchange(srcs, name, gather):
    n = len(srcs)
    gathers = list(gather) if isinstance(gather, (list, tuple)) else [gather] * n
    shapes = [(N_DEV,) + s.shape if gt else s.shape for s, gt in zip(srcs, gathers)]

    def body(*refs):
        src_refs, out_refs = refs[:n], refs[n:2 * n]
        send_sems, recv_sems, local_sems = refs[2 * n:]
        _, me = _peer(0)
        piece = lambda a, d: src_refs[a] if gathers[a] else src_refs[a].at[d]
        local = [pltpu.make_async_copy(piece(a, me), out_refs[a].at[me], local_sems.at[a]) for a in range(n)]
        for cp in local:
            cp.start()
        copies = []
        for a in range(n):
            for k in range(1, N_DEV):
                dev, idx = _peer(k)
                cp = pltpu.make_async_remote_copy(src_ref=piece(a, idx), dst_ref=out_refs[a].at[me],
                                                  send_sem=send_sems.at[a, k - 1], recv_sem=recv_sems.at[a, k - 1],
                                                  device_id=dev, device_id_type=MESH)
                cp.start()
                copies.append(cp)
        for cp in copies:
            cp.wait_recv()
        for cp in copies:
            cp.wait_send()
        for cp in local:
            cp.wait()

    anywhere = pl.BlockSpec(memory_space=pl.ANY)
    return pl.pallas_call(
        body, name=name, in_specs=[anywhere] * n, out_specs=[anywhere] * n,
        out_shape=[jax.ShapeDtypeStruct(shp, s.dtype) for shp, s in zip(shapes, srcs)],
        scratch_shapes=[pltpu.SemaphoreType.DMA((n, N_DEV - 1)), pltpu.SemaphoreType.DMA((n, N_DEV - 1)),
                        pltpu.SemaphoreType.DMA((n,))],
    )(*srcs)


_HBM = pl.BlockSpec(memory_space=pltpu.HBM)
_SEM = pl.BlockSpec(memory_space=pltpu.SEMAPHORE)
_EFFECT = pltpu.SideEffectType.DATAFLOW_SIDE_EFFECTING


def _split_copies(src_refs, land_refs, send_sems, recv_sems, modes, which=None):
    _, me = _peer(0)
    copies = []
    which = range(len(src_refs)) if which is None else which
    for a, src, land in zip(which, src_refs, land_refs):
        if modes[a] == "columns":
            n_cols = src.shape[1]
            dst = land.at[:, pl.ds(pl.multiple_of(me * n_cols, n_cols), n_cols)]
        else:
            dst = land.at[me]
        for k in range(1, N_DEV):
            dev, idx = _peer(k)
            sem = a * (N_DEV - 1) + k - 1
            copies.append(pltpu.make_async_remote_copy(
                src_ref=src.at[idx] if modes[a] == "pieces" else src, dst_ref=dst, send_sem=send_sems.at[sem],
                recv_sem=recv_sems.at[sem], device_id=dev, device_id_type=MESH))
    return copies


def _exchange_start(srcs, name, modes):
    n = len(srcs)
    modes = [modes] * n if isinstance(modes, str) else list(modes)
    me = 4 * lax.axis_index("x") + 2 * lax.axis_index("y") + lax.axis_index("c")
    lands = []
    for s, mode in zip(srcs, modes):
        if mode == "columns":
            empty = lax.empty((s.shape[0], N_DEV * s.shape[1]), s.dtype)
            lands.append(lax.dynamic_update_slice(empty, s, (0, me * s.shape[1])))
        else:
            own = s if mode == "slots" else lax.dynamic_index_in_dim(s, me, 0, keepdims=False)
            shape = (N_DEV,) + s.shape if mode == "slots" else s.shape
            lands.append(lax.dynamic_update_index_in_dim(lax.empty(shape, s.dtype), own, me, 0))

    def body(*refs):
        src_refs, land_refs = refs[:n], refs[n:2 * n]
        send_sems, recv_sems = refs[2 * n], refs[2 * n + 1]
        for cp in _split_copies(src_refs, land_refs, send_sems, recv_sems, modes):
            cp.start()
        refs[-1][...] = jnp.zeros_like(refs[-1])

    both = list(srcs) + lands
    sems = pltpu.SemaphoreType.DMA((n * (N_DEV - 1),))
    out = pl.pallas_call(
        body, name=name,
        out_shape=(sems, sems, *[pltpu.HBM(t.shape, t.dtype) for t in both], jax.ShapeDtypeStruct((8, 128), F32)),
        in_specs=[_HBM] * (2 * n), out_specs=(_SEM, _SEM, *[_HBM] * (2 * n), pl.BlockSpec(memory_space=pltpu.VMEM)),
        input_output_aliases={i: 2 + i for i in range(2 * n)},
        compiler_params=pltpu.CompilerParams(has_side_effects=_EFFECT),
    )(*[pltpu.with_memory_space_constraint(t, pltpu.HBM) for t in both])
    return (n, modes, out[:-1]), out[-1]


def _exchange_wait(handle, after, name, which=None):
    n_all, modes, (send_sems, recv_sems, *both_all) = handle
    which = list(range(n_all)) if which is None else list(which)
    n = len(which)
    both = [both_all[a] for a in which] + [both_all[n_all + a] for a in which]

    def body(*refs):
        src_refs, land_refs = refs[:n], refs[n:2 * n]
        for cp in _split_copies(src_refs, land_refs, refs[2 * n], refs[2 * n + 1], modes, which):
            cp.wait_send()
            cp.wait_recv()

    out = pl.pallas_call(
        body, name=name, out_shape=tuple(pltpu.HBM(t.shape, t.dtype) for t in both),
        in_specs=[_HBM] * (2 * n) + [_SEM, _SEM, pl.BlockSpec(memory_space=pl.ANY)], out_specs=tuple([_HBM] * (2 * n)),
        input_output_aliases={i: i for i in range(2 * n)},
        compiler_params=pltpu.CompilerParams(has_side_effects=_EFFECT),
    )(*both, send_sems, recv_sems, after)
    return list(out[n:])


def _adam_update(g, w, m, v):
    nm = ADAM_B1 * m + (1.0 - ADAM_B1) * g
    nv = ADAM_B2 * v + (1.0 - ADAM_B2) * (g * g)
    m_hat = nm / (1.0 - ADAM_B1 ** ADAM_STEP)
    v_hat = nv / (1.0 - ADAM_B2 ** ADAM_STEP)
    return -ADAM_LR * (m_hat / (jnp.sqrt(v_hat) + ADAM_EPS) + ADAM_WD * w), nm, nv


def _adamw(parts, w, m, v, name):
    n, R, W = parts.shape
    tm = 128 if R % 128 == 0 else R

    def body(p_ref, w_ref, m_ref, v_ref, g_ref, d_ref, nm_ref, nv_ref):
        g = p_ref[0].astype(F32)
        for s in range(1, n):
            g = g + p_ref[s].astype(F32)
        g_ref[...] = g
        d_ref[...], nm_ref[...], nv_ref[...] = _adam_update(g, w_ref[...], m_ref[...], v_ref[...])

    tile = pl.BlockSpec((tm, W), lambda i: (i, 0))
    return pl.pallas_call(
        body, grid=(R // tm,), name=name,
        in_specs=[pl.BlockSpec((n, tm, W), lambda i: (0, i, 0)), tile, tile, tile],
        out_specs=[tile] * 4, out_shape=[jax.ShapeDtypeStruct((R, W), F32)] * 4,
        compiler_params=_params(("parallel",)),
    )(parts, w, m, v)


_MATRICES = ("w_in", "w_o", "w_up", "w_down", "w_ple_gate", "w_ple_proj")


_OTHERS = ("w_o", "w_up", "w_down", "w_ple_gate", "w_ple_proj")
_OTHER_MODES = {"w_o": "slots", "w_up": "slots", "w_down": "slots", "w_ple_gate": "slots", "w_ple_proj": "columns"}


_VECTORS = ("norm_mix", "norm_mlp", "norm_ple", "norm_final", "a_log", "dt_bias", "sinks", "dn_norm")
_SMALL_ROWS, _LOSS_ROW, _CONV_ROW = 16, 8, 9


def _pack_small(vectors, loss, conv):
    def body(*refs):
        out = refs[-1]
        out[...] = jnp.zeros_like(out)
        for r, ref in enumerate(refs[:len(_VECTORS)]):
            out[r:r + 1, 0:ref.shape[1]] = ref[...]
        out[_LOSS_ROW:_LOSS_ROW + 1, 0:128] = refs[len(_VECTORS)][...]
        out[_CONV_ROW:_CONV_ROW + 6, :] = refs[len(_VECTORS) + 1][...]

    return pl.pallas_call(body, name="pack_small", out_shape=jax.ShapeDtypeStruct((_SMALL_ROWS, 1024), F32))(*vectors, loss, conv)


def _sum_slots(parts):
    def body(p_ref, o_ref):
        acc = p_ref[0]
        for s in range(1, parts.shape[0]):
            acc = acc + p_ref[s]
        o_ref[...] = acc

    return pl.pallas_call(body, name="sum_small", out_shape=jax.ShapeDtypeStruct(parts.shape[1:], parts.dtype))(parts)


def _adamw_vectors(summed, conv_g, wmv):
    names = _VECTORS + ("conv_w",)
    flat = [a for triple in wmv for a in triple]

    def body(*refs):
        sum_ref, conv_ref = refs[0], refs[1]
        ins, outs = refs[2:2 + len(flat)], refs[2 + len(flat):]
        for i in range(len(names)):
            w_ref, m_ref, v_ref = ins[3 * i:3 * i + 3]
            g = conv_ref[...] if i == len(_VECTORS) else sum_ref[i:i + 1, 0:w_ref.shape[1]]
            outs[4 * i][...] = g
            outs[4 * i + 1][...], outs[4 * i + 2][...], outs[4 * i + 3][...] = _adam_update(g, w_ref[...], m_ref[...], v_ref[...])

    out_shape = [jax.ShapeDtypeStruct(t[0].shape, F32) for t in wmv for _ in range(4)]
    res = pl.pallas_call(body, name="adamw_vectors", out_shape=out_shape)(summed, conv_g, *flat)
    return {n: res[4 * i:4 * i + 4] for i, n in enumerate(names)}


_ORDER = ("norm_mix", "w_in", "conv_w", "a_log", "dt_bias", "dn_norm", "sinks", "w_o", "norm_mlp", "w_up", "w_down",
          "norm_ple", "w_ple_gate", "w_ple_proj", "norm_final")


def kernel(x, p, norm_mix, w_in, conv_w, a_log, dt_bias, dn_norm, sinks, w_o, norm_mlp, w_up, w_down, norm_ple, w_ple_gate, w_ple_proj, norm_final, loss_target, m_norm_mix, m_w_in, m_conv_w, m_a_log, m_dt_bias, m_dn_norm, m_sinks, m_w_o, m_norm_mlp, m_w_up, m_w_down, m_norm_ple, m_w_ple_gate, m_w_ple_proj, m_norm_final, v_norm_mix, v_w_in, v_conv_w, v_a_log, v_dt_bias, v_dn_norm, v_sinks, v_w_o, v_norm_mlp, v_w_up, v_w_down, v_norm_ple, v_w_ple_gate, v_w_ple_proj, v_norm_final):
    w = dict(norm_mix=norm_mix, w_in=w_in[0], conv_w=conv_w[0], a_log=a_log, dt_bias=dt_bias, dn_norm=dn_norm, sinks=sinks,
             w_o=w_o[0], norm_mlp=norm_mlp, w_up=w_up[0], w_down=w_down[0], norm_ple=norm_ple, w_ple_gate=w_ple_gate[0],
             w_ple_proj=w_ple_proj[0], norm_final=norm_final)
    m = dict(norm_mix=m_norm_mix, w_in=m_w_in[0], conv_w=m_conv_w[0], a_log=m_a_log, dt_bias=m_dt_bias, dn_norm=m_dn_norm,
             sinks=m_sinks, w_o=m_w_o[0], norm_mlp=m_norm_mlp, w_up=m_w_up[0], w_down=m_w_down[0], norm_ple=m_norm_ple,
             w_ple_gate=m_w_ple_gate[0], w_ple_proj=m_w_ple_proj[0], norm_final=m_norm_final)
    v = dict(norm_mix=v_norm_mix, w_in=v_w_in[0], conv_w=v_conv_w[0], a_log=v_a_log, dt_bias=v_dt_bias, dn_norm=v_dn_norm,
             sinks=v_sinks, w_o=v_w_o[0], norm_mlp=v_norm_mlp, w_up=v_w_up[0], w_down=v_w_down[0], norm_ple=v_norm_ple,
             w_ple_gate=v_w_ple_gate[0], w_ple_proj=v_w_ple_proj[0], norm_final=v_norm_final)
    me = 4 * lax.axis_index("x") + 2 * lax.axis_index("y") + lax.axis_index("c")
    conv_shard = conv_w.shape[2]

    for d in (w, m, v):
        d["w_in"] = d["w_in"].T
    conv_pad = jnp.pad(w["conv_w"], ((0, 8 - DN_CONV), (0, 256 - conv_shard)))
    first, token_first = _exchange_start([_bf(w["w_in"]), conv_pad], "gather_first_start", "slots")
    later = [_bf(w[n]) for n in _OTHERS]
    later[-1] = _bf(w["w_ple_proj"] + token_first[0:1, 0:1])
    others, token_others = _exchange_start(later, "gather_others_start", [_OTHER_MODES[n] for n in _OTHERS])
    vectors = dict(w)
    vectors["norm_mix"] = w["norm_mix"] + token_others[0:1, 0:1]

    def first_weights(after):
        w_in_all, conv_all = _exchange_wait(first, after, "gather_first_wait")
        conv_all = jnp.transpose(conv_all[:, :DN_CONV, :conv_shard], (1, 0, 2)).reshape(DN_CONV, N_DEV * conv_shard)
        return _w_in_to_internal(w_in_all.reshape(D_IN, D_MODEL)), conv_all

    as_taken = {"w_o": lambda t: t.reshape(1024, 1024), "w_up": lambda t: t, "w_down": lambda t: t.reshape(4096, 1024),
                "w_ple_gate": lambda t: t.reshape(1024, 1024), "w_ple_proj": lambda t: t}

    def other_weights(names, after):
        which = [_OTHERS.index(n) for n in names]
        got = _exchange_wait(others, after, "gather_wait_" + names[0], which)
        return [as_taken[n](t) for n, t in zip(names, got)]

    shipped = []

    def ship_early(pieces):
        names = tuple(pieces)
        if names == ("w_in",):
            pieces = {"w_in": _bf(_w_in_from_internal(pieces["w_in"])).reshape(N_DEV, D_IN // N_DEV, D_MODEL)}
        handle, token = _exchange_start([pieces[n] for n in names], "scatter_start_" + names[0], "pieces")
        shipped.append((names, handle))
        return token

    loss, grad_x, g = _local_step(x[0], p[0, 0], loss_target[0], vectors, first_weights, other_weights, ship_early)

    row = lambda t: t.reshape(1, t.size)
    small = _pack_small([row(g[n]) for n in _VECTORS], loss, g["conv_w"].reshape(6, 1024))
    small_all, = _exchange([small], "gather_small", gather=True)
    received = {}
    for names, handle in shipped:
        received.update(zip(names, _exchange_wait(handle, grad_x, "scatter_wait_" + names[0])))
    big = {n: _adamw(received[n], w[n], m[n], v[n], "adamw_" + n) for n in _MATRICES}
    summed = _sum_slots(small_all)
    conv_g = lax.dynamic_slice(summed[_CONV_ROW:_CONV_ROW + 6].reshape(DN_CONV, N_DEV * conv_shard), (0, me * conv_shard),
                               (DN_CONV, conv_shard))
    small_out = _adamw_vectors(summed, conv_g, [(row(w[n]), row(m[n]), row(v[n])) for n in _VECTORS]
                               + [(w["conv_w"], m["conv_w"], v["conv_w"])])

    result = [summed[_LOSS_ROW, 0], grad_x[None]]
    for i in range(4):
        for n in _ORDER:
            if n == "w_in":
                result.append(big[n][i].T[None])
            elif n in _MATRICES:
                result.append(big[n][i][None])
            elif n == "conv_w":
                result.append(small_out[n][i][None])
            else:
                result.append(small_out[n][i].reshape(w[n].shape))
    return tuple(result)
```

```python
import jax
import jax.numpy as jnp
import numpy as np
from jax import lax
from jax.experimental import pallas as pl
from jax.experimental.pallas import tpu as pltpu

F32, BF16 = jnp.float32, jnp.bfloat16
EPS = 1e-6
D_MODEL = 1024
N_DEV = 8
ATTN_BLOCK = 128
HEAD_PAIR = 128
DN_HEADS = 4
DN_DIM = 128
DN_CHUNK = 64
DN_CONV = 4
ROPE_THETA = 10000.0
D_IN = 2824
D_IN_PAD = 3072
BLK_Q, BLK_Z = 0, 1
BLK_DN, BLK_K, BLK_V, BLK_G = 8, 20, 21, 22
BLK_KV, BLK_G_PAD = 10, 11
VMEM_LIMIT = 56 * 1024 * 1024
NEG = -1e30
ADAM_LR, ADAM_B1, ADAM_B2, ADAM_EPS, ADAM_WD, ADAM_STEP = 0.001, 0.9, 0.999, 1e-08, 0.01, 10
MESH = pl.DeviceIdType.MESH


def _bf(x):
    return x.astype(BF16)


def _dot(a, b):
    return jnp.dot(a, b, preferred_element_type=F32)


def _dot_nt(a, b):
    return lax.dot_general(a, b, (((1,), (1,)), ((), ())), preferred_element_type=F32)


def _dot_tn(a, b):
    return lax.dot_general(a, b, (((0,), (0,)), ((), ())), preferred_element_type=F32)


def _sigmoid(x):
    return 1.0 / (1.0 + jnp.exp(-x))


def _params(sem):
    return pltpu.CompilerParams(dimension_semantics=sem, vmem_limit_bytes=VMEM_LIMIT)


def _mm(x, w, *, form, name, out_dtypes, tn, epi=None, extra=(), tm=512, w_row_block=0, after=None, norm=None,
        norm_bwd=None):
    xs = list(x) if isinstance(x, (list, tuple)) else [x]
    nx = len(xs)
    S, K = xs[0].shape
    shards = w.ndim == 3
    N = (w.shape[2] * N_DEV if shards else w.shape[1]) if form == "nn" else w.shape[-2]
    assert not (shards and form == "nn" and tn != w.shape[2]) and (nx == 1 or (form == "nn" and not shards and norm is None))
    r0 = w_row_block * K
    tm = min(tm, S)
    n_extra, n_out = len(extra), len(out_dtypes)
    tile = lambda width: pl.BlockSpec((tm, width), lambda i: (i, 0))
    whole = lambda a: pl.BlockSpec(a.shape, lambda i, nd=a.ndim: (0,) * nd)
    ins, in_specs = [*xs, w, *extra], [tile(K)] * nx + [whole(w)] + [tile(N)] * n_extra
    if norm is not None:
        ins, in_specs = ins + [norm], in_specs + [whole(norm)]
    if norm_bwd is not None:
        ins, in_specs = ins + list(norm_bwd), in_specs + [tile(N), whole(norm_bwd[1]), tile(N)]
    if after is not None:
        ins, in_specs = ins + [after], in_specs + [whole(after)]
    out_shape = [jax.ShapeDtypeStruct((S, N), dt) for dt in out_dtypes]
    out_specs = [tile(N)] * n_out
    if norm is not None:
        out_shape, out_specs = out_shape + [jax.ShapeDtypeStruct((S, K), BF16)], out_specs + [tile(K)]
    if norm_bwd is not None:
        out_shape, out_specs = out_shape + [jax.ShapeDtypeStruct((1, N), F32)], out_specs + [pl.BlockSpec((1, N), lambda i: (0, 0))]

    def product(xb, w_ref, cols, c):
        if form == "nn" and nx > 1:
            return sum(_dot(part, w_ref[r0 + p * K:r0 + (p + 1) * K, cols]) for p, part in enumerate(xb))
        if form == "nn":
            return _dot(xb, w_ref[c] if shards else w_ref[r0:r0 + K, cols])
        if not shards:
            return _dot_nt(xb, w_ref[cols, :])
        ks = w.shape[2]
        acc = _dot_nt(xb[:, 0:ks], w_ref[0, cols, :])
        for s in range(1, N_DEV):
            acc = acc + _dot_nt(xb[:, s * ks:(s + 1) * ks], w_ref[s, cols, :])
        return acc

    def body(*refs):
        x_ref, w_ref = refs[0], refs[nx]
        extra_refs = refs[nx + 1:nx + 1 + n_extra]
        at = nx + 1 + n_extra
        if norm is not None:
            gain_ref, at = refs[at], at + 1
        if norm_bwd is not None:
            (y_ref, ygain_ref, dres_ref), at = refs[at:at + 3], at + 3
        outs = refs[len(ins):]
        if norm is not None:
            _, xh = _rms_stats(x_ref[...])
            xb = _bf(xh * gain_ref[...])
            outs[n_out][...] = xb
        else:
            xb = _bf(x_ref[...]) if nx == 1 else [_bf(r[...]) for r in refs[:nx]]
        for c in range(N // tn):
            cols = slice(c * tn, (c + 1) * tn)
            acc = product(xb, w_ref, cols, c)
            res = epi(acc, *[r[:, cols] for r in extra_refs]) if epi else (acc,)
            for o, r in zip(outs[:n_out], res):
                o[:, cols] = r.astype(o.dtype)
        if norm_bwd is not None:
            dx, dg = _rms_bwd_tile(y_ref[...], ygain_ref[...], outs[0][...])
            outs[0][...] = dres_ref[...] + dx
            dg_ref = outs[-1]

            @pl.when(pl.program_id(0) == 0)
            def _():
                dg_ref[...] = jnp.zeros_like(dg_ref)

            dg_ref[...] += dg

    return pl.pallas_call(
        body, grid=(S // tm,), name=name, in_specs=in_specs, out_specs=out_specs, out_shape=out_shape,
        compiler_params=_params(("arbitrary",) if norm_bwd is not None else ("parallel",)),
    )(*ins)


def _mm_tn(x, dy, *, name, tm, tn, out_dtype=F32, column_shards=False):
    S, K = x.shape
    N = dy.shape[1]

    def body(x_ref, dy_ref, o_ref):
        o_ref[...] = _dot_tn(_bf(x_ref[...]), _bf(dy_ref[...])).astype(out_dtype)

    if column_shards:
        out_spec = pl.BlockSpec((None, tm, tn), lambda i, j: (j, i, 0))
        out_shape = jax.ShapeDtypeStruct((N // tn, K, tn), out_dtype)
    else:
        out_spec = pl.BlockSpec((tm, tn), lambda i, j: (i, j))
        out_shape = jax.ShapeDtypeStruct((K, N), out_dtype)
    return pl.pallas_call(
        body, grid=(K // tm, N // tn), name=name,
        in_specs=[pl.BlockSpec((S, tm), lambda i, j: (0, i)), pl.BlockSpec((S, tn), lambda i, j: (0, j))],
        out_specs=out_spec, out_shape=out_shape,
        compiler_params=_params(("parallel", "parallel")),
    )(x, dy)


def _rowwise(body, *, tiled, full, out_tiled, out_acc, name, tm=512, smem=()):
    S = tiled[0].shape[0]
    tm = min(tm, S)
    n_in = len(smem) + len(tiled) + len(full)

    def kern(*refs):
        @pl.when(pl.program_id(0) == 0)
        def _():
            for r in refs[n_in + len(out_tiled):]:
                r[...] = jnp.zeros_like(r)
        body(*refs)

    in_specs = [pl.BlockSpec(memory_space=pltpu.SMEM) for _ in smem]
    in_specs += [pl.BlockSpec((tm, a.shape[1]), lambda i: (i, 0)) for a in tiled]
    in_specs += [pl.BlockSpec(a.shape, lambda i, nd=a.ndim: (0,) * nd) for a in full]
    out_specs = [pl.BlockSpec((tm, w), lambda i: (i, 0)) for w, _ in out_tiled]
    out_specs += [pl.BlockSpec(shp, lambda i, nd=len(shp): (0,) * nd) for shp, _ in out_acc]
    out_shape = [jax.ShapeDtypeStruct((S, w), dt) for w, dt in out_tiled]
    out_shape += [jax.ShapeDtypeStruct(shp, dt) for shp, dt in out_acc]
    return pl.pallas_call(
        kern, grid=(S // tm,), name=name, in_specs=in_specs, out_specs=out_specs, out_shape=out_shape,
        compiler_params=_params(("arbitrary",)),
    )(*smem, *tiled, *full)


def _rms_stats(x):
    r = lax.rsqrt(jnp.mean(x * x, axis=-1, keepdims=True) + EPS)
    return r, x * r


def _rmsnorm_fwd(x, g, name):
    def body(x_ref, g_ref, o_ref):
        _, xh = _rms_stats(x_ref[...])
        o_ref[...] = _bf(xh * g_ref[...])

    return _rowwise(body, tiled=[x], full=[g], out_tiled=[(x.shape[1], BF16)], out_acc=[], name=name)[0]


def _rms_bwd_tile(x, g, dxn):
    r, xh = _rms_stats(x)
    dg = jnp.sum(dxn * xh, axis=0, keepdims=True)
    dn = dxn * g
    dx = r * (dn - xh * jnp.mean(dn * xh, axis=-1, keepdims=True))
    return dx, dg


def _ple_and_loss(h2, p, target, w_pg, w_pp, g_ple, g_final):
    S, n = h2.shape
    tm = min(512, S)
    tn = 512

    def body(h2_ref, p_ref, t_ref, wpg_ref, wpp_ref, gple_ref, gfin_ref,
             n3_ref, dh_ref, dgl_ref, dpp_ref, loss_ref, dg_ref, pp, gate, h3):
        @pl.when(pl.program_id(0) == 0)
        def _():
            loss_ref[...] = jnp.zeros_like(loss_ref)
            dg_ref[...] = jnp.zeros_like(dg_ref)

        x = h2_ref[...]
        _, xh = _rms_stats(x)
        n3 = _bf(xh * gple_ref[...])
        n3_ref[...] = n3
        pb = _bf(p_ref[...])
        for c in range(n // tn):
            cols = slice(c * tn, (c + 1) * tn)
            pp[:, cols] = _dot(pb, wpp_ref[:, cols])
            gt = _sigmoid(_dot(n3, wpg_ref[:, cols]))
            gate[:, cols] = gt
            h3[:, cols] = x[:, cols] + gt * pp[:, cols]
        y = h3[...]
        _, yh = _rms_stats(y)
        e = yh * gfin_ref[...] - t_ref[...]
        per_tok = jnp.mean(e * e, axis=-1, keepdims=True)
        loss_ref[...] += 0.5 * jnp.sum(per_tok, axis=0, keepdims=True)
        dh, dg = _rms_bwd_tile(y, gfin_ref[...], e * (1.0 / n))
        dh_ref[...] = dh
        dg_ref[...] += dg
        gt = gate[...]
        dgl_ref[...] = _bf(dh * pp[...] * gt * (1.0 - gt))
        dpp_ref[...] = _bf(dh * gt)

    tile = lambda width: pl.BlockSpec((tm, width), lambda i: (i, 0))
    whole = lambda a: pl.BlockSpec(a.shape, lambda i, nd=a.ndim: (0,) * nd)
    return pl.pallas_call(
        body, grid=(S // tm,), name="ple_and_loss",
        in_specs=[tile(n), tile(p.shape[1]), tile(n), whole(w_pg), whole(w_pp), whole(g_ple), whole(g_final)],
        out_specs=[tile(n), tile(n), tile(n), tile(n), pl.BlockSpec((1, 128), lambda i: (0, 0)), pl.BlockSpec((1, n), lambda i: (0, 0))],
        out_shape=[jax.ShapeDtypeStruct((S, n), BF16), jax.ShapeDtypeStruct((S, n), F32), jax.ShapeDtypeStruct((S, n), BF16),
                   jax.ShapeDtypeStruct((S, n), BF16), jax.ShapeDtypeStruct((1, 128), F32), jax.ShapeDtypeStruct((1, n), F32)],
        scratch_shapes=[pltpu.VMEM((tm, n), F32)] * 3,
        compiler_params=_params(("arbitrary",)),
    )(h2, p, target, w_pg, w_pp, g_ple, g_final)


def _rope_tables(S):
    half = 32
    inv = (1.0 / (np.float32(ROPE_THETA) ** (np.arange(half, dtype=np.float32) * np.float32(2.0 / 64)))).astype(np.float32)
    ang = np.arange(S).astype(np.float32)[:, None] * inv[None, :]
    cos, sin = np.cos(ang), np.sin(ang)
    return jnp.asarray(np.tile(cos, (1, 4))), jnp.asarray(np.concatenate([-sin, sin, -sin, sin], axis=1))


def _attn_common(i, kc, kp, vc, vp, cc, sc, cp, sp):
    lane = lax.broadcasted_iota(jnp.int32, (1, HEAD_PAIR), 1)
    lane_lo = jnp.bitwise_and(lane, 63) < 32
    slot = [lane < 64, lane >= 64]

    def swap_halves(t):
        return jnp.where(lane_lo, pltpu.roll(t, 96, 1), pltpu.roll(t, 32, 1))

    def rope(t, cos, sin):
        return t * cos + swap_halves(t) * sin

    def unrope(d, cos, sin):
        return d * cos + swap_halves(d * sin)

    k2 = jnp.concatenate([rope(kp, cp, sp), rope(kc, cc, sc)], axis=0)
    v2 = jnp.concatenate([vp, vc], axis=0)
    r = lax.broadcasted_iota(jnp.int32, (ATTN_BLOCK, 2 * ATTN_BLOCK), 0)
    c = lax.broadcasted_iota(jnp.int32, (ATTN_BLOCK, 2 * ATTN_BLOCK), 1)
    valid = (c > r) & (c <= r + ATTN_BLOCK) & jnp.logical_or(c >= ATTN_BLOCK, i > 0)
    ks, vs = {}, {}
    for j in range(2):
        kn = jnp.where(slot[j], k2, 0.0)
        vn = jnp.where(slot[j], v2, 0.0)
        for s in range(2):
            ks[j, s] = _bf(kn if s == j else pltpu.roll(kn, 64, 1))
            vs[j, s] = _bf(vn if s == j else pltpu.roll(vn, 64, 1))
    return slot, rope, unrope, valid, ks, vs


def _attn_probs(scores, valid, sink):
    s = jnp.where(valid, scores * 0.125, NEG)
    m = jnp.maximum(jnp.max(s, axis=1, keepdims=True), sink)
    e = jnp.exp(s - m)
    z = jnp.sum(e, axis=1, keepdims=True) + jnp.exp(sink - m)
    return e * (1.0 / z), m + jnp.log(z)


def _attn_specs(S):
    nb = S // ATTN_BLOCK
    prev = lambda i: jnp.maximum(i - 1, 0)
    blk = lambda w, col, row=(lambda i: i): pl.BlockSpec((ATTN_BLOCK, w), lambda i: (row(i), col))
    in_specs = [pl.BlockSpec(memory_space=pltpu.SMEM),
                blk(512, BLK_Q), blk(128, BLK_K), blk(128, BLK_K, prev), blk(128, BLK_V), blk(128, BLK_V, prev),
                blk(128, 0), blk(128, 0), blk(128, 0, prev), blk(128, 0, prev)]
    return nb, in_specs


def _attn_fwd(pa, cos, sin, sinks):
    S = pa.shape[0]
    nb, in_specs = _attn_specs(S)

    def body(sinks_ref, q_ref, kc_ref, kp_ref, vc_ref, vp_ref, cc_ref, sc_ref, cp_ref, sp_ref, o_ref, lse_ref):
        i = pl.program_id(0)
        lane = lax.broadcasted_iota(jnp.int32, (1, HEAD_PAIR), 1)
        cc, sc = cc_ref[...], sc_ref[...]
        _, rope, _, valid, ks, vs = _attn_common(i, kc_ref[...], kp_ref[...], vc_ref[...], vp_ref[...],
                                                 cc, sc, cp_ref[...], sp_ref[...])
        pair_cols = [slice(HEAD_PAIR * pair, HEAD_PAIR * (pair + 1)) for pair in range(4)]
        qps = [_bf(rope(q_ref[:, cols], cc, sc)) for cols in pair_cols]
        outs, lses = {}, {}

        def head_program(h):
            pair, s = divmod(h, 2)
            j = h // 4
            scores = _dot_nt(qps[pair], ks[j, s])
            yield
            p, lse = _attn_probs(scores, valid, sinks_ref[h])
            outs[h] = _dot(_bf(p), vs[j, s])
            lses[h] = jnp.where(lane == h, lse, 0.0)

        _interleave(head_program(h) for h in range(8))
        for pair, cols in enumerate(pair_cols):
            o_ref[:, cols] = outs[2 * pair] + outs[2 * pair + 1]
        lse_ref[...] = sum((lses[h] for h in range(1, 8)), lses[0])

    return pl.pallas_call(
        body, grid=(nb,), name="attn_fwd", in_specs=in_specs,
        out_specs=[pl.BlockSpec((ATTN_BLOCK, 512), lambda i: (i, 0)), pl.BlockSpec((ATTN_BLOCK, 128), lambda i: (i, 0))],
        out_shape=[jax.ShapeDtypeStruct((S, 512), F32), jax.ShapeDtypeStruct((S, 128), F32)],
        compiler_params=_params(("parallel",)),
    )(sinks, pa, pa, pa, pa, pa, cos, sin, cos, sin)


def _attn_bwd(pa, cos, sin, sinks, dcat, attn, lse):
    S = pa.shape[0]
    nb, in_specs = _attn_specs(S)
    in_specs = in_specs + [pl.BlockSpec((ATTN_BLOCK, 512), lambda i: (i, 0))] * 2 + [pl.BlockSpec((ATTN_BLOCK, 128), lambda i: (i, 0))]

    def body(sinks_ref, q_ref, kc_ref, kp_ref, vc_ref, vp_ref, cc_ref, sc_ref, cp_ref, sp_ref, do_ref, o_ref, lse_ref,
             dq_ref, dk_ref, dv_ref, dsink_ref):
        i = pl.program_id(0)

        @pl.when(i == 0)
        def _():
            dk_ref[...] = jnp.zeros_like(dk_ref)
            dv_ref[...] = jnp.zeros_like(dv_ref)
            dsink_ref[...] = jnp.zeros_like(dsink_ref)

        cc, sc, cp, sp = cc_ref[...], sc_ref[...], cp_ref[...], sp_ref[...]
        slot, rope, unrope, valid, ks, vs = _attn_common(i, kc_ref[...], kp_ref[...], vc_ref[...], vp_ref[...], cc, sc, cp, sp)
        pair_cols = [slice(HEAD_PAIR * pair, HEAD_PAIR * (pair + 1)) for pair in range(4)]
        qps = [_bf(rope(q_ref[:, cols], cc, sc)) for cols in pair_cols]
        dobs = [_bf(do_ref[:, cols]) for cols in pair_cols]
        do_o = [do_ref[:, cols] * o_ref[:, cols] for cols in pair_cols]
        dqs, dks, dvs = {}, {}, {}

        def head_program(h):
            pair, s = divmod(h, 2)
            j = h // 4
            qp, dob = qps[pair], dobs[pair]
            scores = _dot_nt(qp, ks[j, s])
            dp = _dot_nt(dob, vs[j, s])
            yield
            lse_h = lse_ref[:, h:h + 1]
            p = jnp.exp(jnp.where(valid, scores * 0.125, NEG) - lse_h)
            dr = jnp.sum(jnp.where(slot[s], do_o[pair], 0.0), axis=1, keepdims=True)
            ds = _bf(p * (dp - dr) * 0.125)
            dsink_ref[h:h + 1, :] += -jnp.sum(jnp.exp(sinks_ref[h] - lse_h) * dr, axis=0, keepdims=True)
            dqs[h] = _dot(ds, ks[j, s])
            dk_h = _dot_tn(ds, qp)
            dv_h = _dot_tn(_bf(p), dob)
            yield
            dk_h, dv_h = jnp.where(slot[s], dk_h, 0.0), jnp.where(slot[s], dv_h, 0.0)
            if s != j:
                dk_h, dv_h = pltpu.roll(dk_h, 64, 1), pltpu.roll(dv_h, 64, 1)
            dks[h], dvs[h] = dk_h, dv_h

        _interleave(head_program(h) for h in range(8))
        dk2 = sum((dks[h] for h in range(1, 8)), dks[0])
        dv2 = sum((dvs[h] for h in range(1, 8)), dvs[0])
        for pair, cols in enumerate(pair_cols):
            dq_ref[:, cols] = _bf(unrope(dqs[2 * pair] + dqs[2 * pair + 1], cc, sc))
        cur = pl.ds(pl.multiple_of(i * ATTN_BLOCK, ATTN_BLOCK), ATTN_BLOCK)
        dk_ref[cur, :] += unrope(dk2[ATTN_BLOCK:], cc, sc)
        dv_ref[cur, :] += dv2[ATTN_BLOCK:]

        @pl.when(i > 0)
        def _():
            prv = pl.ds(pl.multiple_of((i - 1) * ATTN_BLOCK, ATTN_BLOCK), ATTN_BLOCK)
            dk_ref[prv, :] += unrope(dk2[:ATTN_BLOCK], cp, sp)
            dv_ref[prv, :] += dv2[:ATTN_BLOCK]

    whole = lambda w: pl.BlockSpec((S, w), lambda i: (0, 0))
    return pl.pallas_call(
        body, grid=(nb,), name="attn_bwd", in_specs=in_specs,
        out_specs=[pl.BlockSpec((ATTN_BLOCK, 512), lambda i: (i, BLK_Q)), whole(128), whole(128),
                   pl.BlockSpec((8, 128), lambda i: (0, 0))],
        out_shape=[jax.ShapeDtypeStruct((S, D_IN_PAD), BF16), jax.ShapeDtypeStruct((S, 128), F32),
                   jax.ShapeDtypeStruct((S, 128), F32), jax.ShapeDtypeStruct((8, 128), F32)],
        compiler_params=_params(("arbitrary",)),
    )(sinks, pa, pa, pa, pa, pa, cos, sin, cos, sin, dcat, attn, lse)


CONV_ROWS = 512
CONV_PAD = 8


def _conv_silu(scr, w, r0):
    y = w[3:4, :] * scr[pl.ds(CONV_PAD + r0, CONV_ROWS), :]
    for j in range(DN_CONV - 1):
        y = y + w[j:j + 1, :] * scr[pl.ds(CONV_PAD + r0 - 3 + j, CONV_ROWS), :]
    return y


def _dn_prep_fwd(pd, conv_w):
    S = pd.shape[0]
    assert S % CONV_ROWS == 0

    def body(x_ref, w_ref, o_ref, scr):
        b = pl.program_id(0)
        scr[0:CONV_PAD, :] = jnp.zeros((CONV_PAD, DN_DIM), F32)
        scr[pl.ds(CONV_PAD, S), :] = x_ref[...]
        w = w_ref[...]
        q_scale = jnp.where(b < DN_HEADS, DN_DIM ** -0.5, 1.0)
        for r0 in range(0, S, CONV_ROWS):
            y = _conv_silu(scr, w, r0)
            a = y * _sigmoid(y)
            rs = lax.rsqrt(jnp.sum(a * a, axis=1, keepdims=True) + EPS)
            o_ref[pl.ds(r0, CONV_ROWS), :] = a * jnp.where(b < 2 * DN_HEADS, rs * q_scale, 1.0)

    col = pl.BlockSpec((S, DN_DIM), lambda b: (0, b))
    return pl.pallas_call(
        body, grid=(3 * DN_HEADS,), name="dn_prep_fwd",
        in_specs=[pl.BlockSpec((S, DN_DIM), lambda b: (0, BLK_DN + b)), pl.BlockSpec((DN_CONV, DN_DIM), lambda b: (0, b))],
        out_specs=col,
        out_shape=jax.ShapeDtypeStruct((S, 3 * DN_HEADS * DN_DIM), F32),
        scratch_shapes=[pltpu.VMEM((S + CONV_PAD, DN_DIM), F32)],
        compiler_params=_params(("parallel",)),
    )(pd, conv_w)


def _dn_prep_bwd(pd, conv_w, dqkv, dproj):
    S = pd.shape[0]

    def body(x_ref, w_ref, d_ref, _, dx_ref, dw_ref, scr, dscr):
        b = pl.program_id(0)
        scr[0:CONV_PAD, :] = jnp.zeros((CONV_PAD, DN_DIM), F32)
        scr[pl.ds(CONV_PAD, S), :] = x_ref[...]
        dscr[pl.ds(S, CONV_PAD), :] = jnp.zeros((CONV_PAD, DN_DIM), F32)
        w = w_ref[...]
        q_scale = jnp.where(b < DN_HEADS, DN_DIM ** -0.5, 1.0)
        is_qk = b < 2 * DN_HEADS
        dw = [jnp.zeros((1, DN_DIM), F32) for _ in range(DN_CONV)]
        for r0 in range(0, S, CONV_ROWS):
            y = _conv_silu(scr, w, r0)
            sg = _sigmoid(y)
            a = y * sg
            dout = d_ref[pl.ds(r0, CONV_ROWS), :]
            rs = lax.rsqrt(jnp.sum(a * a, axis=1, keepdims=True) + EPS)
            da_qk = q_scale * rs * (dout - a * (rs * rs) * jnp.sum(dout * a, axis=1, keepdims=True))
            dy = jnp.where(is_qk, da_qk, dout) * (sg * (1.0 + y * (1.0 - sg)))
            dscr[pl.ds(r0, CONV_ROWS), :] = dy
            for j in range(DN_CONV):
                dw[j] = dw[j] + jnp.sum(dy * scr[pl.ds(CONV_PAD + r0 - 3 + j, CONV_ROWS), :], axis=0, keepdims=True)
        for j in range(DN_CONV):
            dw_ref[j:j + 1, :] = dw[j]
        for r0 in range(0, S, CONV_ROWS):
            dx = w[3:4, :] * dscr[pl.ds(r0, CONV_ROWS), :]
            for j in range(DN_CONV - 1):
                dx = dx + w[j:j + 1, :] * dscr[pl.ds(r0 + 3 - j, CONV_ROWS), :]
            dx_ref[pl.ds(r0, CONV_ROWS), :] = _bf(dx)

    col = pl.BlockSpec((S, DN_DIM), lambda b: (0, b))
    proj_col = pl.BlockSpec((S, DN_DIM), lambda b: (0, BLK_DN + b))
    wcol = pl.BlockSpec((DN_CONV, DN_DIM), lambda b: (0, b))
    return pl.pallas_call(
        body, grid=(3 * DN_HEADS,), name="dn_prep_bwd",
        in_specs=[proj_col, wcol, col, pl.BlockSpec(memory_space=pl.ANY)], out_specs=[proj_col, wcol],
        out_shape=[jax.ShapeDtypeStruct(dproj.shape, dproj.dtype), jax.ShapeDtypeStruct((DN_CONV, 3 * DN_HEADS * DN_DIM), F32)],
        scratch_shapes=[pltpu.VMEM((S + CONV_PAD, DN_DIM), F32), pltpu.VMEM((S + CONV_PAD, DN_DIM), F32)],
        input_output_aliases={3: 0},
        compiler_params=_params(("parallel",)),
    )(pd, conv_w, dqkv, dproj)


CPAD = 128
CHUNKS_LOCAL = 4
CHUNKS_SCAN = 4


def _chunk_masks():
    ii = lax.broadcasted_iota(jnp.int32, (DN_CHUNK, CPAD), 0)
    jj = lax.broadcasted_iota(jnp.int32, (DN_CHUNK, CPAD), 1)
    return ii, jj


def _rows_pad(a):
    return jnp.concatenate([a, jnp.zeros_like(a)], axis=0)


def _hi_lo(a):
    hi = _bf(a)
    return hi, _bf(a - hi.astype(F32))


def _double_step(t, p):
    C = DN_CHUNK
    th, tl = _hi_lo(t)
    ph, pl_ = _hi_lo(p)
    r1 = _dot(jnp.concatenate([th, tl, ph, pl_], axis=0), _rows_pad(ph))
    r2 = _dot(jnp.concatenate([th, ph], axis=0), _rows_pad(pl_))
    return t + (r1[:C] + r1[C:2 * C] + r2[:C]), r1[2 * C:3 * C] + r1[3 * C:] + r2[C:]


def _dot3_nt(a, b):
    C = DN_CHUNK
    ah, al = _hi_lo(a)
    bh, bl = _hi_lo(b)
    r1 = _dot_nt(jnp.concatenate([ah, al], axis=0), _rows_pad(bh))
    return r1[:C] + r1[C:] + _dot_nt(ah, _rows_pad(bl))


def _dot3_tn(a, b):
    C = DN_CHUNK
    ah, al = _hi_lo(a)
    bh, bl = _hi_lo(b)
    return _dot_tn(jnp.concatenate([ah, al, ah], axis=0), jnp.concatenate([bh, bh, bl], axis=0))[:C]


def _interleave(programs):
    programs = list(programs)
    while programs:
        alive = []
        for prog in programs:
            try:
                next(prog)
                alive.append(prog)
            except StopIteration:
                pass
        programs = alive


def _col_to_row(col, ii, jj):
    return jnp.sum(jnp.where(ii == jj, col, 0.0), axis=0, keepdims=True)


def _row_to_col(row, ii, jj):
    return jnp.sum(jnp.where(ii == jj, row, 0.0), axis=1, keepdims=True)


def _decay(gc_col, ii, jj):
    diff = gc_col - _col_to_row(gc_col, ii, jj)
    return jnp.where(jj <= ii, jnp.exp(jnp.where(jj <= ii, diff, 0.0)), 0.0)


def _softplus(x):
    return jnp.maximum(x, 0.0) + jnp.log(1.0 + jnp.exp(-jnp.abs(x)))


def _head(h):
    return slice(DN_DIM * h, DN_DIM * (h + 1))


def _dn_chunk_fwd(qkv, pg, a_log, dt_bias):
    S = qkv.shape[0]
    C = DN_CHUNK
    G = CHUNKS_LOCAL
    R = G * C
    steps = S // R

    def body(alog_ref, dtb_ref, qkv_ref, pg_ref, w_ref, u_ref, qg_ref, kd_ref, a_ref, t_ref, gcs_ref):
        ii, jj = _chunk_masks()
        lane = lax.broadcasted_iota(jnp.int32, (1, 128), 1)
        eye = (ii == jj).astype(F32)
        gcs_parts = [[] for _ in range(G)]

        def head_program(chunk, h):
            rows = slice(chunk * C, (chunk + 1) * C)
            q, k, v = qkv_ref[rows, _head(h)], qkv_ref[rows, _head(DN_HEADS + h)], qkv_ref[rows, _head(2 * DN_HEADS + h)]
            beta = _sigmoid(pg_ref[rows, h:h + 1])
            g_col = -jnp.exp(alog_ref[h]) * _softplus(pg_ref[rows, DN_HEADS + h:DN_HEADS + h + 1] + dtb_ref[h])
            g_row = _col_to_row(g_col, ii, jj)
            gc_col = jnp.sum(jnp.where(jj <= ii, g_row, 0.0), axis=1, keepdims=True)
            dec = _decay(gc_col, ii, jj)
            eg = jnp.exp(gc_col)
            kb, vb = k * beta, v * beta
            k_rows = _rows_pad(_bf(k))
            kk = _dot_nt(_bf(kb), k_rows)
            qk = _dot_nt(_bf(q), k_rows)
            yield
            t, pw = eye, -jnp.where(jj < ii, kk * dec, 0.0)
            for _ in range(6):
                t, pw = _double_step(t, pw)
                yield
            tb = _bf(t)
            u_ref[rows, _head(h)] = _dot(tb, _rows_pad(_bf(vb)))
            w_ref[rows, _head(h)] = _bf(_dot(tb, _rows_pad(_bf(kb * eg))))
            a_ref[h, rows] = _bf(qk * dec)
            t_ref[h, rows] = t
            qg_ref[rows, _head(h)] = _bf(q * eg)
            kd_ref[rows, _head(h)] = _bf(k * jnp.exp(gc_col[C - 1:C, :] - gc_col))
            gcs_parts[chunk].append(jnp.where(lane == h, gc_col, 0.0) + jnp.where(lane == DN_HEADS + h, beta, 0.0)
                                    + jnp.where(lane == 2 * DN_HEADS + h, g_col, 0.0))

        _interleave(head_program(chunk, h) for chunk in range(G) for h in range(DN_HEADS))
        for chunk in range(G):
            gcs_ref[chunk * C:(chunk + 1) * C, :] = sum(gcs_parts[chunk][1:], gcs_parts[chunk][0])

    smem = pl.BlockSpec(memory_space=pltpu.SMEM)
    wide = pl.BlockSpec((R, 512), lambda n: (n, 0))
    sq = pl.BlockSpec((DN_HEADS, R, CPAD), lambda n: (0, n, 0))
    narrow = pl.BlockSpec((R, 128), lambda n: (n, 0))
    f = lambda *shp: jax.ShapeDtypeStruct(shp, F32)
    b = lambda *shp: jax.ShapeDtypeStruct(shp, BF16)
    return pl.pallas_call(
        body, grid=(steps,), name="dn_chunk_fwd",
        in_specs=[smem, smem, pl.BlockSpec((R, 1536), lambda n: (n, 0)), pl.BlockSpec((R, 128), lambda n: (n, BLK_G))],
        out_specs=[wide, wide, wide, wide, sq, sq, narrow],
        out_shape=[b(S, 512), f(S, 512), b(S, 512), b(S, 512), b(DN_HEADS, S, CPAD), f(DN_HEADS, S, CPAD), f(S, 128)],
        compiler_params=_params(("parallel",)),
    )(a_log, dt_bias, qkv, pg)


def _gated_norm(o, z, gn):
    r, oh = _rms_stats(o)
    return oh * gn * (z * _sigmoid(z))


def _dn_scan_fwd(w, u, qg, kd, a, gcs, pz, gn):
    S = w.shape[0]
    C = DN_CHUNK
    nc = S // C
    G = CHUNKS_SCAN
    R = G * C

    def body(w_ref, u_ref, qg_ref, kd_ref, a_ref, gcs_ref, z_ref, gn_ref, o_ref, vn_ref, sst_ref, out_ref, state):
        @pl.when(pl.program_id(0) == 0)
        def _():
            state[...] = jnp.zeros_like(state)

        def head_program(chunk, h):
            hs = _head(h)
            rows = slice(chunk * C, (chunk + 1) * C)
            s_in = state[h]
            sst_ref[chunk, h] = s_in
            sb = _bf(s_in)
            w_s = _dot(w_ref[rows, hs], sb)
            q_s = _dot(qg_ref[rows, hs], sb)
            yield
            vn = u_ref[rows, hs] - w_s
            vnb = _bf(vn)
            o = q_s + _dot(a_ref[h, rows], _rows_pad(vnb))
            k_v = _dot_tn(kd_ref[rows, hs], vnb)
            yield
            state[h] = s_in * jnp.exp(gcs_ref[(chunk + 1) * C - 1:(chunk + 1) * C, h:h + 1]) + k_v
            o_ref[rows, hs] = o
            vn_ref[rows, hs] = vn
            out_ref[rows, hs] = _gated_norm(o, z_ref[rows, hs], gn_ref[...])

        for chunk in range(G):
            _interleave(head_program(chunk, h) for h in range(DN_HEADS))

    wide = pl.BlockSpec((R, 512), lambda n: (n, 0))
    f = lambda *shp: jax.ShapeDtypeStruct(shp, F32)
    return pl.pallas_call(
        body, grid=(nc // G,), name="dn_scan_fwd",
        in_specs=[wide, wide, wide, wide, pl.BlockSpec((DN_HEADS, R, CPAD), lambda n: (0, n, 0)),
                  pl.BlockSpec((R, 128), lambda n: (n, 0)), pl.BlockSpec((R, 512), lambda n: (n, BLK_Z)),
                  pl.BlockSpec((1, DN_DIM), lambda n: (0, 0))],
        out_specs=[wide, wide, pl.BlockSpec((G, DN_HEADS, DN_DIM, DN_DIM), lambda n: (n, 0, 0, 0)), wide],
        out_shape=[f(S, 512), f(S, 512), f(nc, DN_HEADS, DN_DIM, DN_DIM), f(S, 512)],
        scratch_shapes=[pltpu.VMEM((DN_HEADS, DN_DIM, DN_DIM), F32)],
        compiler_params=_params(("arbitrary",)),
    )(w, u, qg, kd, a, gcs, pz, gn)


def _dn_scan_bwd(dcat, o, pz, gn, sst, vnew, w, qg, kd, a, gcs, dproj):
    S = o.shape[0]
    C = DN_CHUNK
    G = CHUNKS_SCAN
    R = G * C
    steps = S // R

    def body(dy_ref, o_ref, z_ref, gn_ref, sst_ref, vn_ref, w_ref, qg_ref, kd_ref, a_ref, gcs_ref, _,
             du_ref, dw_ref, dqg_ref, dkd_ref, da_ref, dz_ref, dsc_ref, dgn_ref, dstate):
        @pl.when(pl.program_id(0) == 0)
        def _():
            dstate[...] = jnp.zeros_like(dstate)
            dgn_ref[...] = jnp.zeros_like(dgn_ref)

        gn_ = gn_ref[...]
        lane = lax.broadcasted_iota(jnp.int32, (C, 128), 1)
        row = lax.broadcasted_iota(jnp.int32, (C, 128), 0)
        dgn_parts = []

        def head_program(chunk, h, dsc_parts):
            hs = _head(h)
            rows = slice(chunk * C, (chunk + 1) * C)
            ov, z, dout = o_ref[rows, hs], z_ref[rows, hs], dy_ref[rows, hs]
            r, oh = _rms_stats(ov)
            sg = _sigmoid(z)
            don = dout * (z * sg)
            dz_ref[rows, hs] = _bf(dout * (oh * gn_) * (sg * (1.0 + z * (1.0 - sg))))
            dgn_parts.append(jnp.sum(don * oh, axis=0, keepdims=True))
            dn = don * gn_
            do = _bf(r * (dn - oh * jnp.mean(dn * oh, axis=-1, keepdims=True)))
            s_in = sst_ref[chunk, h]
            sb = _bf(s_in)
            ds_out = dstate[h]
            dsb = _bf(ds_out)
            vnb = _bf(vn_ref[rows, hs])
            wb, qgb, kdb, ab = w_ref[rows, hs], qg_ref[rows, hs], kd_ref[rows, hs], a_ref[h, rows]
            dvn = _dot_tn(ab, do)[:C] + _dot(kdb, dsb)
            da_ref[h, rows] = _dot_nt(do, _rows_pad(vnb))
            dqg_ref[rows, hs] = _dot_nt(do, sb)
            dkd_ref[rows, hs] = _dot_nt(vnb, dsb)
            q_do = _dot_tn(qgb, do)
            yield
            dvnb = _bf(dvn)
            dw_ref[rows, hs] = _bf(-_dot_nt(dvnb, sb))
            w_dvn = _dot_tn(wb, dvnb)
            du_ref[rows, hs] = dvnb
            yield
            d_last = jnp.exp(gcs_ref[(chunk + 1) * C - 1:(chunk + 1) * C, h:h + 1])
            dd = jnp.sum(jnp.sum(ds_out * s_in, axis=1, keepdims=True), axis=0, keepdims=True)
            dsc_parts.append(jnp.where((lane == h) & (row == C - 1), dd * d_last, 0.0))
            dstate[h] = ds_out * d_last + q_do - w_dvn

        for chunk in reversed(range(G)):
            dsc_parts = []
            _interleave(head_program(chunk, h, dsc_parts) for h in range(DN_HEADS))
            dsc_ref[chunk * C:(chunk + 1) * C, :] = sum(dsc_parts[1:], dsc_parts[0])
        dgn_ref[...] += sum(dgn_parts[1:], dgn_parts[0])

    rev = lambda n: steps - 1 - n
    wide = pl.BlockSpec((R, 512), lambda n: (rev(n), 0))
    z_spec = pl.BlockSpec((R, 512), lambda n: (rev(n), BLK_Z))
    sq = pl.BlockSpec((DN_HEADS, R, CPAD), lambda n: (0, rev(n), 0))
    narrow = pl.BlockSpec((R, 128), lambda n: (rev(n), 0))
    gn_spec = pl.BlockSpec((1, DN_DIM), lambda n: (0, 0))
    f = lambda *shp: jax.ShapeDtypeStruct(shp, F32)
    b = lambda *shp: jax.ShapeDtypeStruct(shp, BF16)
    return pl.pallas_call(
        body, grid=(steps,), name="dn_scan_bwd",
        in_specs=[pl.BlockSpec((R, 512), lambda n: (rev(n), 1)), wide, z_spec, gn_spec,
                  pl.BlockSpec((G, DN_HEADS, DN_DIM, DN_DIM), lambda n: (rev(n), 0, 0, 0)),
                  wide, wide, wide, wide, sq, narrow, pl.BlockSpec(memory_space=pl.ANY)],
        out_specs=[wide, wide, wide, wide, sq, z_spec, narrow, gn_spec],
        out_shape=[b(S, 512), b(S, 512), f(S, 512), f(S, 512), f(DN_HEADS, S, CPAD),
                   jax.ShapeDtypeStruct(dproj.shape, dproj.dtype), f(S, 128), f(1, DN_DIM)],
        scratch_shapes=[pltpu.VMEM((DN_HEADS, DN_DIM, DN_DIM), F32)],
        input_output_aliases={11: 5},
        compiler_params=_params(("arbitrary",)),
    )(dcat, o, pz, gn, sst, vnew, w, qg, kd, a, gcs, dproj)


def _dn_chunk_bwd(qkv, pg, t_inv, gcs, du, dw, dqg, dkd, da, dsc, a_log, dt_bias, dproj):
    S = qkv.shape[0]
    C = DN_CHUNK
    G = CHUNKS_LOCAL
    R = G * C

    def body(alog_ref, dtb_ref, qkv_ref, pg_ref, t_ref, gcs_ref, du_ref, dw_ref, dqg_ref, dkd_ref, da_ref, dsc_ref, _,
             dqkv_ref, dpg_ref, acc_ref):
        @pl.when(pl.program_id(0) == 0)
        def _():
            acc_ref[...] = jnp.zeros_like(acc_ref)

        ii, jj = _chunk_masks()
        lane = lax.broadcasted_iota(jnp.int32, (1, 128), 1)
        row8 = lax.broadcasted_iota(jnp.int32, (8, 128), 0)
        lane8 = lax.broadcasted_iota(jnp.int32, (8, 128), 1)
        rowc = lax.broadcasted_iota(jnp.int32, (C, 1), 0)
        tril, strict = jj <= ii, jj < ii
        dpg_parts, acc_parts = [[] for _ in range(G)], []

        def head_program(chunk, h):
            rows = slice(chunk * C, (chunk + 1) * C)
            q, k, v = qkv_ref[rows, _head(h)], qkv_ref[rows, _head(DN_HEADS + h)], qkv_ref[rows, _head(2 * DN_HEADS + h)]
            gc_col, beta, g_col = gcs_ref[rows, h:h + 1], gcs_ref[rows, DN_HEADS + h:DN_HEADS + h + 1], \
                gcs_ref[rows, 2 * DN_HEADS + h:2 * DN_HEADS + h + 1]
            dec = _decay(gc_col, ii, jj)
            eg = jnp.exp(gc_col)
            g_last = gc_col[C - 1:C, :]
            ek = jnp.exp(g_last - gc_col)
            kb, vb = k * beta, v * beta
            kbg = kb * eg
            qb, kbb = _bf(q), _bf(kb)
            k_rows = _rows_pad(_bf(k))
            t = t_ref[h, rows]
            tb = _bf(t)
            dub, dwb = du_ref[rows, _head(h)], dw_ref[rows, _head(h)]
            dqg_, dkd_ = dqg_ref[rows, _head(h)], dkd_ref[rows, _head(h)]
            dt = _dot_nt(dub, _rows_pad(_bf(vb))) + _dot_nt(dwb, _rows_pad(_bf(kbg)))
            t_du_dw = _dot_tn(tb, jnp.concatenate([dub, dwb], axis=1))
            dvb, dkbg = t_du_dw[:C, :DN_DIM], t_du_dw[:C, DN_DIM:]
            kk = _dot_nt(kbb, k_rows)
            qk = _dot_nt(qb, k_rows)
            yield
            dt_t = _dot3_nt(dt, t)
            yield
            dl = -_dot3_tn(t, dt_t)
            yield
            dm = jnp.where(strict, dl * dec, 0.0)
            dqk = jnp.where(tril, da_ref[h, rows] * dec, 0.0)
            gmat = dm * kk + dqk * qk
            dgc = jnp.sum(gmat, axis=1, keepdims=True) - _row_to_col(jnp.sum(gmat, axis=0, keepdims=True), ii, jj)
            dmb, dqkb = _bf(dm), _bf(dqk)
            dkb = _dot(dmb, k_rows) + dkbg * eg
            dk = _dot_tn(jnp.concatenate([dmb, dqkb], axis=0), jnp.concatenate([kbb, qb], axis=0))[:C] + dkd_ * ek
            dq = _dot(dqkb, k_rows) + dqg_ * eg
            yield
            tk = jnp.sum(dkd_ * k * ek, axis=1, keepdims=True)
            dgc = dgc + jnp.sum(dqg_ * q * eg, axis=1, keepdims=True) - tk + jnp.sum(dkbg * kbg, axis=1, keepdims=True)
            dgl = jnp.sum(tk, axis=0, keepdims=True) + dsc_ref[(chunk + 1) * C - 1:(chunk + 1) * C, h:h + 1]
            dgc = dgc + jnp.where(rowc == C - 1, dgl, 0.0)
            dk = dk + dkb * beta
            dbeta = jnp.sum(dkb * k, axis=1, keepdims=True) + jnp.sum(dvb * v, axis=1, keepdims=True)
            dqkv_ref[rows, _head(h)] = dq
            dqkv_ref[rows, _head(DN_HEADS + h)] = dk
            dqkv_ref[rows, _head(2 * DN_HEADS + h)] = dvb * beta
            dg_col = jnp.sum(jnp.where(jj >= ii, _col_to_row(dgc, ii, jj), 0.0), axis=1, keepdims=True)
            db = dbeta * beta * (1.0 - beta)
            da_in = dg_col * (-jnp.exp(alog_ref[h])) * _sigmoid(pg_ref[rows, DN_HEADS + h:DN_HEADS + h + 1] + dtb_ref[h])
            dpg_parts[chunk].append(jnp.where(lane == h, db, 0.0) + jnp.where(lane == DN_HEADS + h, da_in, 0.0))
            acc_parts.append(jnp.where((row8 == 0) & (lane8 == h), jnp.sum(dg_col * g_col, axis=0, keepdims=True), 0.0)
                             + jnp.where((row8 == 1) & (lane8 == h), jnp.sum(da_in, axis=0, keepdims=True), 0.0))

        _interleave(head_program(chunk, h) for chunk in range(G) for h in range(DN_HEADS))
        for chunk in range(G):
            dpg = sum(dpg_parts[chunk][1:], dpg_parts[chunk][0])
            dpg_ref[chunk * C:(chunk + 1) * C, :] = _bf(jnp.concatenate([dpg, jnp.zeros_like(dpg)], axis=1))
        acc_ref[...] += sum(acc_parts[1:], acc_parts[0])

    smem = pl.BlockSpec(memory_space=pltpu.SMEM)
    wide = pl.BlockSpec((R, 512), lambda n: (n, 0))
    sq = pl.BlockSpec((DN_HEADS, R, CPAD), lambda n: (0, n, 0))
    narrow = pl.BlockSpec((R, 128), lambda n: (n, 0))
    qkv_spec = pl.BlockSpec((R, 1536), lambda n: (n, 0))
    f = lambda *shp: jax.ShapeDtypeStruct(shp, F32)
    return pl.pallas_call(
        body, grid=(S // R,), name="dn_chunk_bwd",
        in_specs=[smem, smem, qkv_spec, pl.BlockSpec((R, 128), lambda n: (n, BLK_G)), sq, narrow, wide, wide, wide, wide, sq,
                  narrow, pl.BlockSpec(memory_space=pl.ANY)],
        out_specs=[qkv_spec, pl.BlockSpec((R, 256), lambda n: (n, BLK_G_PAD)), pl.BlockSpec((8, 128), lambda n: (0, 0))],
        out_shape=[f(S, 1536), jax.ShapeDtypeStruct(dproj.shape, dproj.dtype), f(8, 128)],
        input_output_aliases={12: 1},
        compiler_params=_params(("arbitrary",)),
    )(a_log, dt_bias, qkv, pg, t_inv, gcs, du, dw, dqg, dkd, da, dsc, dproj)


def _fill_kv(dk, dv, dproj):
    S = dk.shape[0]
    tm = min(512, S)

    def body(dk_ref, dv_ref, _, o_ref):
        o_ref[...] = _bf(jnp.concatenate([dk_ref[...], dv_ref[...]], axis=1))

    tile = pl.BlockSpec((tm, 128), lambda i: (i, 0))
    return pl.pallas_call(
        body, grid=(S // tm,), name="fill_kv",
        in_specs=[tile, tile, pl.BlockSpec(memory_space=pl.ANY)],
        out_specs=pl.BlockSpec((tm, 256), lambda i: (i, BLK_KV)),
        out_shape=jax.ShapeDtypeStruct(dproj.shape, dproj.dtype),
        input_output_aliases={2: 0},
        compiler_params=_params(("parallel",)),
    )(dk, dv, dproj)


def _w_in_to_internal(wt):
    return jnp.concatenate([wt[0:512], wt[2304:2816], wt[768:2304], wt[512:768], wt[2816:2824],
                            jnp.zeros((D_IN_PAD - D_IN, wt.shape[1]), wt.dtype)], axis=0)


def _w_in_from_internal(gt):
    return jnp.concatenate([gt[0:512], gt[2560:2816], gt[1024:2560], gt[512:1024], gt[2816:2824]], axis=0)


def _local_step(x, p, target, wts, first_weights, other_weights, ship_early):
    S = x.shape[0]
    cos, sin = _rope_tables(S)
    sinks, a_log, dt_bias = wts["sinks"].reshape(8), wts["a_log"].reshape(4), wts["dt_bias"].reshape(4)
    gn = wts["dn_norm"].reshape(1, DN_DIM)
    add = lambda acc, res: (acc + res,)

    u = _rmsnorm_fwd(x, wts["norm_mix"], "norm_mix_fwd")
    w_in_t, conv_w = first_weights(u)
    proj, = _mm(u, w_in_t, form="nt", name="in_proj", out_dtypes=[F32], tn=512)
    attn, lse = _attn_fwd(proj, cos, sin, sinks)
    qkv = _dn_prep_fwd(proj, conv_w)
    cw, cu, cqg, ckd, ca, ct, gcs = _dn_chunk_fwd(qkv, proj, a_log, dt_bias)
    o, vnew, sst, dn_out = _dn_scan_fwd(cw, cu, cqg, ckd, ca, gcs, proj, gn)
    w_o, = other_weights(("w_o",), dn_out)
    h1, = _mm([attn, dn_out], w_o, form="nn", name="out_proj", out_dtypes=[F32], tn=512, epi=add, extra=[x])

    def relu2(acc):
        r = jnp.maximum(acc, 0.0)
        return r * r, r

    w_up, = other_weights(("w_up",), h1)
    hid, relu, m = _mm(h1, w_up, form="nn", name="mlp_up", out_dtypes=[BF16, BF16], tn=512, epi=relu2, norm=wts["norm_mlp"])
    w_down, = other_weights(("w_down",), hid)
    h2, = _mm(hid, w_down, form="nn", name="mlp_down", out_dtypes=[F32], tn=512, epi=add, extra=[h1])
    w_pg, w_pp = other_weights(("w_ple_gate", "w_ple_proj"), h2)
    n3, dh3, dgl, dpp, loss, d_norm_final = _ple_and_loss(h2, p, target, w_pg, w_pp, wts["norm_ple"],
                                                         wts["norm_final"].reshape(1, D_MODEL))
    g = {"norm_final": d_norm_final}
    early = {"w_ple_gate": _mm_tn(n3, dgl, name="d_w_ple_gate", tm=512, tn=1024, out_dtype=BF16).reshape(N_DEV, 128, 1024),
             "w_ple_proj": _mm_tn(p, dpp, name="d_w_ple_proj", tm=256, tn=128, out_dtype=BF16, column_shards=True)}
    dh2, g["norm_ple"] = _mm(dgl, w_pg, form="nt", name="d_n3", out_dtypes=[F32], tn=512,
                             norm_bwd=(h2, wts["norm_ple"], dh3))
    d_act, = _mm(dh2, w_down, form="nt", name="d_hidden", out_dtypes=[BF16], tn=512,
                 epi=lambda acc, r: (acc * (2.0 * r.astype(F32)),), extra=[relu])
    early["w_down"] = _mm_tn(hid, dh2, name="d_w_down", tm=512, tn=1024, out_dtype=BF16).reshape(N_DEV, 512, 1024)
    early["w_up"] = _mm_tn(m, d_act, name="d_w_up", tm=1024, tn=512, out_dtype=BF16, column_shards=True)
    token = ship_early(early)
    dh1, g["norm_mlp"] = _mm(d_act, w_up, form="nt", name="d_m", out_dtypes=[F32], tn=512, after=token,
                             norm_bwd=(h1, wts["norm_mlp"], dh2))
    dcat, = _mm(dh1, w_o, form="nt", name="d_cat", out_dtypes=[F32], tn=512)
    d_w_o = jnp.concatenate([_mm_tn(attn, dh1, name="d_w_o_attn", tm=512, tn=512, out_dtype=BF16),
                             _mm_tn(dn_out, dh1, name="d_w_o_dn", tm=512, tn=512, out_dtype=BF16)], axis=0)
    token = ship_early({"w_o": d_w_o.reshape(N_DEV, 128, 1024)})
    dproj, dk, dv, dsinks = _attn_bwd(proj, cos, sin, sinks + token[0, 0], dcat, attn, lse)
    g["sinks"] = dsinks[:, 0].reshape(1, 8)
    du_, dw_, dqg, dkd, da, dproj, dsc, g["dn_norm"] = _dn_scan_bwd(dcat, o, proj, gn, sst, vnew, cw, cqg, ckd, ca, gcs, dproj)
    dqkv, dproj, gate_acc = _dn_chunk_bwd(qkv, proj, ct, gcs, du_, dw_, dqg, dkd, da, dsc, a_log, dt_bias, dproj)
    g["a_log"], g["dt_bias"] = gate_acc[0:1, 0:4], gate_acc[1:2, 0:4]
    dproj, g["conv_w"] = _dn_prep_bwd(proj, conv_w, dqkv, dproj)
    dproj = _fill_kv(dk, dv, dproj)
    token = ship_early({"w_in": _mm_tn(dproj, u, name="d_w_in", tm=512, tn=1024, out_dtype=BF16)})
    grad_x, g["norm_mix"] = _mm(dproj, w_in_t, form="nn", name="d_u", out_dtypes=[F32], tn=512, after=token,
                                norm_bwd=(x, wts["norm_mix"], dh1))
    return loss, grad_x, g


def _peer(k):
    x, y, c = lax.axis_index("x"), lax.axis_index("y"), lax.axis_index("c")
    px = 1 - x if k & 4 else x
    py = 1 - y if k & 2 else y
    pc = 1 - c if k & 1 else c
    return (px, py, pc), 4 * px + 2 * py + pc


def _exchange(srcs, name, gather):
    n = len(srcs)
    gathers = list(gather) if isinstance(gather, (list, tuple)) else [gather] * n
    shapes = [(N_DEV,) + s.shape if gt else s.shape for s, gt in zip(srcs, gathers)]

    def body(*refs):
        src_refs, out_refs = refs[:n], refs[n:2 * n]
        send_sems, recv_sems, local_sems = refs[2 * n:]
        _, me = _peer(0)
        piece = lambda a, d: src_refs[a] if gathers[a] else src_refs[a].at[d]
        local = [pltpu.make_async_copy(piece(a, me), out_refs[a].at[me], local_sems.at[a]) for a in range(n)]
        for cp in local:
            cp.start()
        copies = []
        for a in range(n):
            for k in range(1, N_DEV):
                dev, idx = _peer(k)
                cp = pltpu.make_async_remote_copy(src_ref=piece(a, idx), dst_ref=out_refs[a].at[me],
                                                  send_sem=send_sems.at[a, k - 1], recv_sem=recv_sems.at[a, k - 1],
                                                  device_id=dev, device_id_type=MESH)
                cp.start()
                copies.append(cp)
        for cp in copies:
            cp.wait_recv()
        for cp in copies:
            cp.wait_send()
        for cp in local:
            cp.wait()

    anywhere = pl.BlockSpec(memory_space=pl.ANY)
    return pl.pallas_call(
        body, name=name, in_specs=[anywhere] * n, out_specs=[anywhere] * n,
        out_shape=[jax.ShapeDtypeStruct(shp, s.dtype) for shp, s in zip(shapes, srcs)],
        scratch_shapes=[pltpu.SemaphoreType.DMA((n, N_DEV - 1)), pltpu.SemaphoreType.DMA((n, N_DEV - 1)),
                        pltpu.SemaphoreType.DMA((n,))],
    )(*srcs)


_HBM = pl.BlockSpec(memory_space=pltpu.HBM)
_SEM = pl.BlockSpec(memory_space=pltpu.SEMAPHORE)
_EFFECT = pltpu.SideEffectType.DATAFLOW_SIDE_EFFECTING


def _split_copies(src_refs, land_refs, send_sems, recv_sems, modes, which=None):
    _, me = _peer(0)
    copies = []
    which = range(len(src_refs)) if which is None else which
    for a, src, land in zip(which, src_refs, land_refs):
        if modes[a] == "columns":
            n_cols = src.shape[1]
            dst = land.at[:, pl.ds(pl.multiple_of(me * n_cols, n_cols), n_cols)]
        else:
            dst = land.at[me]
        for k in range(1, N_DEV):
            dev, idx = _peer(k)
            sem = a * (N_DEV - 1) + k - 1
            copies.append(pltpu.make_async_remote_copy(
                src_ref=src.at[idx] if modes[a] == "pieces" else src, dst_ref=dst, send_sem=send_sems.at[sem],
                recv_sem=recv_sems.at[sem], device_id=dev, device_id_type=MESH))
    return copies


def _exchange_start(srcs, name, modes):
    n = len(srcs)
    modes = [modes] * n if isinstance(modes, str) else list(modes)
    me = 4 * lax.axis_index("x") + 2 * lax.axis_index("y") + lax.axis_index("c")
    lands = []
    for s, mode in zip(srcs, modes):
        if mode == "columns":
            empty = lax.empty((s.shape[0], N_DEV * s.shape[1]), s.dtype)
            lands.append(lax.dynamic_update_slice(empty, s, (0, me * s.shape[1])))
        else:
            own = s if mode == "slots" else lax.dynamic_index_in_dim(s, me, 0, keepdims=False)
            shape = (N_DEV,) + s.shape if mode == "slots" else s.shape
            lands.append(lax.dynamic_update_index_in_dim(lax.empty(shape, s.dtype), own, me, 0))

    def body(*refs):
        src_refs, land_refs = refs[:n], refs[n:2 * n]
        send_sems, recv_sems = refs[2 * n], refs[2 * n + 1]
        for cp in _split_copies(src_refs, land_refs, send_sems, recv_sems, modes):
            cp.start()
        refs[-1][...] = jnp.zeros_like(refs[-1])

    both = list(srcs) + lands
    sems = pltpu.SemaphoreType.DMA((n * (N_DEV - 1),))
    out = pl.pallas_call(
        body, name=name,
        out_shape=(sems, sems, *[pltpu.HBM(t.shape, t.dtype) for t in both], jax.ShapeDtypeStruct((8, 128), F32)),
        in_specs=[_HBM] * (2 * n), out_specs=(_SEM, _SEM, *[_HBM] * (2 * n), pl.BlockSpec(memory_space=pltpu.VMEM)),
        input_output_aliases={i: 2 + i for i in range(2 * n)},
        compiler_params=pltpu.CompilerParams(has_side_effects=_EFFECT),
    )(*[pltpu.with_memory_space_constraint(t, pltpu.HBM) for t in both])
    return (n, modes, out[:-1]), out[-1]


def _exchange_wait(handle, after, name, which=None):
    n_all, modes, (send_sems, recv_sems, *both_all) = handle
    which = list(range(n_all)) if which is None else list(which)
    n = len(which)
    both = [both_all[a] for a in which] + [both_all[n_all + a] for a in which]

    def body(*refs):
        src_refs, land_refs = refs[:n], refs[n:2 * n]
        for cp in _split_copies(src_refs, land_refs, refs[2 * n], refs[2 * n + 1], modes, which):
            cp.wait_send()
            cp.wait_recv()

    out = pl.pallas_call(
        body, name=name, out_shape=tuple(pltpu.HBM(t.shape, t.dtype) for t in both),
        in_specs=[_HBM] * (2 * n) + [_SEM, _SEM, pl.BlockSpec(memory_space=pl.ANY)], out_specs=tuple([_HBM] * (2 * n)),
        input_output_aliases={i: i for i in range(2 * n)},
        compiler_params=pltpu.CompilerParams(has_side_effects=_EFFECT),
    )(*both, send_sems, recv_sems, after)
    return list(out[n:])


def _adam_update(g, w, m, v):
    nm = ADAM_B1 * m + (1.0 - ADAM_B1) * g
    nv = ADAM_B2 * v + (1.0 - ADAM_B2) * (g * g)
    m_hat = nm / (1.0 - ADAM_B1 ** ADAM_STEP)
    v_hat = nv / (1.0 - ADAM_B2 ** ADAM_STEP)
    return -ADAM_LR * (m_hat / (jnp.sqrt(v_hat) + ADAM_EPS) + ADAM_WD * w), nm, nv


def _adamw(parts, w, m, v, name):
    n, R, W = parts.shape
    tm = 128 if R % 128 == 0 else R

    def body(p_ref, w_ref, m_ref, v_ref, g_ref, d_ref, nm_ref, nv_ref):
        g = p_ref[0].astype(F32)
        for s in range(1, n):
            g = g + p_ref[s].astype(F32)
        g_ref[...] = g
        d_ref[...], nm_ref[...], nv_ref[...] = _adam_update(g, w_ref[...], m_ref[...], v_ref[...])

    tile = pl.BlockSpec((tm, W), lambda i: (i, 0))
    return pl.pallas_call(
        body, grid=(R // tm,), name=name,
        in_specs=[pl.BlockSpec((n, tm, W), lambda i: (0, i, 0)), tile, tile, tile],
        out_specs=[tile] * 4, out_shape=[jax.ShapeDtypeStruct((R, W), F32)] * 4,
        compiler_params=_params(("parallel",)),
    )(parts, w, m, v)


_MATRICES = ("w_in", "w_o", "w_up", "w_down", "w_ple_gate", "w_ple_proj")


_OTHERS = ("w_o", "w_up", "w_down", "w_ple_gate", "w_ple_proj")
_OTHER_MODES = {"w_o": "slots", "w_up": "slots", "w_down": "slots", "w_ple_gate": "slots", "w_ple_proj": "columns"}


_VECTORS = ("norm_mix", "norm_mlp", "norm_ple", "norm_final", "a_log", "dt_bias", "sinks", "dn_norm")
_SMALL_ROWS, _LOSS_ROW, _CONV_ROW = 16, 8, 9


def _pack_small(vectors, loss, conv):
    def body(*refs):
        out = refs[-1]
        out[...] = jnp.zeros_like(out)
        for r, ref in enumerate(refs[:len(_VECTORS)]):
            out[r:r + 1, 0:ref.shape[1]] = ref[...]
        out[_LOSS_ROW:_LOSS_ROW + 1, 0:128] = refs[len(_VECTORS)][...]
        out[_CONV_ROW:_CONV_ROW + 6, :] = refs[len(_VECTORS) + 1][...]

    return pl.pallas_call(body, name="pack_small", out_shape=jax.ShapeDtypeStruct((_SMALL_ROWS, 1024), F32))(*vectors, loss, conv)


def _sum_slots(parts):
    def body(p_ref, o_ref):
        acc = p_ref[0]
        for s in range(1, parts.shape[0]):
            acc = acc + p_ref[s]
        o_ref[...] = acc

    return pl.pallas_call(body, name="sum_small", out_shape=jax.ShapeDtypeStruct(parts.shape[1:], parts.dtype))(parts)


def _adamw_vectors(summed, conv_g, wmv):
    names = _VECTORS + ("conv_w",)
    flat = [a for triple in wmv for a in triple]

    def body(*refs):
        sum_ref, conv_ref = refs[0], refs[1]
        ins, outs = refs[2:2 + len(flat)], refs[2 + len(flat):]
        for i in range(len(names)):
            w_ref, m_ref, v_ref = ins[3 * i:3 * i + 3]
            g = conv_ref[...] if i == len(_VECTORS) else sum_ref[i:i + 1, 0:w_ref.shape[1]]
            outs[4 * i][...] = g
            outs[4 * i + 1][...], outs[4 * i + 2][...], outs[4 * i + 3][...] = _adam_update(g, w_ref[...], m_ref[...], v_ref[...])

    out_shape = [jax.ShapeDtypeStruct(t[0].shape, F32) for t in wmv for _ in range(4)]
    res = pl.pallas_call(body, name="adamw_vectors", out_shape=out_shape)(summed, conv_g, *flat)
    return {n: res[4 * i:4 * i + 4] for i, n in enumerate(names)}


_ORDER = ("norm_mix", "w_in", "conv_w", "a_log", "dt_bias", "dn_norm", "sinks", "w_o", "norm_mlp", "w_up", "w_down",
          "norm_ple", "w_ple_gate", "w_ple_proj", "norm_final")


def kernel(x, p, norm_mix, w_in, conv_w, a_log, dt_bias, dn_norm, sinks, w_o, norm_mlp, w_up, w_down, norm_ple, w_ple_gate, w_ple_proj, norm_final, loss_target, m_norm_mix, m_w_in, m_conv_w, m_a_log, m_dt_bias, m_dn_norm, m_sinks, m_w_o, m_norm_mlp, m_w_up, m_w_down, m_norm_ple, m_w_ple_gate, m_w_ple_proj, m_norm_final, v_norm_mix, v_w_in, v_conv_w, v_a_log, v_dt_bias, v_dn_norm, v_sinks, v_w_o, v_norm_mlp, v_w_up, v_w_down, v_norm_ple, v_w_ple_gate, v_w_ple_proj, v_norm_final):
    w = dict(norm_mix=norm_mix, w_in=w_in[0], conv_w=conv_w[0], a_log=a_log, dt_bias=dt_bias, dn_norm=dn_norm, sinks=sinks,
             w_o=w_o[0], norm_mlp=norm_mlp, w_up=w_up[0], w_down=w_down[0], norm_ple=norm_ple, w_ple_gate=w_ple_gate[0],
             w_ple_proj=w_ple_proj[0], norm_final=norm_final)
    m = dict(norm_mix=m_norm_mix, w_in=m_w_in[0], conv_w=m_conv_w[0], a_log=m_a_log, dt_bias=m_dt_bias, dn_norm=m_dn_norm,
             sinks=m_sinks, w_o=m_w_o[0], norm_mlp=m_norm_mlp, w_up=m_w_up[0], w_down=m_w_down[0], norm_ple=m_norm_ple,
             w_ple_gate=m_w_ple_gate[0], w_ple_proj=m_w_ple_proj[0], norm_final=m_norm_final)
    v = dict(norm_mix=v_norm_mix, w_in=v_w_in[0], conv_w=v_conv_w[0], a_log=v_a_log, dt_bias=v_dt_bias, dn_norm=v_dn_norm,
             sinks=v_sinks, w_o=v_w_o[0], norm_mlp=v_norm_mlp, w_up=v_w_up[0], w_down=v_w_down[0], norm_ple=v_norm_ple,
             w_ple_gate=v_w_ple_gate[0], w_ple_proj=v_w_ple_proj[0], norm_final=v_norm_final)
    me = 4 * lax.axis_index("x") + 2 * lax.axis_index("y") + lax.axis_index("c")
    conv_shard = conv_w.shape[2]

    for d in (w, m, v):
        d["w_in"] = d["w_in"].T
    conv_pad = jnp.pad(w["conv_w"], ((0, 8 - DN_CONV), (0, 256 - conv_shard)))
    first, token_first = _exchange_start([_bf(w["w_in"]), conv_pad], "gather_first_start", "slots")
    later = [_bf(w[n]) for n in _OTHERS]
    later[-1] = _bf(w["w_ple_proj"] + token_first[0:1, 0:1])
    others, token_others = _exchange_start(later, "gather_others_start", [_OTHER_MODES[n] for n in _OTHERS])
    vectors = dict(w)
    vectors["norm_mix"] = w["norm_mix"] + token_others[0:1, 0:1]

    def first_weights(after):
        w_in_all, conv_all = _exchange_wait(first, after, "gather_first_wait")
        conv_all = jnp.transpose(conv_all[:, :DN_CONV, :conv_shard], (1, 0, 2)).reshape(DN_CONV, N_DEV * conv_shard)
        return _w_in_to_internal(w_in_all.reshape(D_IN, D_MODEL)), conv_all

    as_taken = {"w_o": lambda t: t.reshape(1024, 1024), "w_up": lambda t: t, "w_down": lambda t: t.reshape(4096, 1024),
                "w_ple_gate": lambda t: t.reshape(1024, 1024), "w_ple_proj": lambda t: t}

    def other_weights(names, after):
        which = [_OTHERS.index(n) for n in names]
        got = _exchange_wait(others, after, "gather_wait_" + names[0], which)
        return [as_taken[n](t) for n, t in zip(names, got)]

    shipped = []

    def ship_early(pieces):
        names = tuple(pieces)
        if names == ("w_in",):
            pieces = {"w_in": _w_in_from_internal(pieces["w_in"]).reshape(N_DEV, D_IN // N_DEV, D_MODEL)}
        handle, token = _exchange_start([pieces[n] for n in names], "scatter_start_" + names[0], "pieces")
        shipped.append((names, handle))
        return token

    loss, grad_x, g = _local_step(x[0], p[0, 0], loss_target[0], vectors, first_weights, other_weights, ship_early)

    row = lambda t: t.reshape(1, t.size)
    small = _pack_small([row(g[n]) for n in _VECTORS], loss, g["conv_w"].reshape(6, 1024))
    small_all, = _exchange([small], "gather_small", gather=True)
    received = {}
    for names, handle in shipped:
        received.update(zip(names, _exchange_wait(handle, grad_x, "scatter_wait_" + names[0])))
    big = {n: _adamw(received[n], w[n], m[n], v[n], "adamw_" + n) for n in _MATRICES}
    summed = _sum_slots(small_all)
    conv_g = lax.dynamic_slice(summed[_CONV_ROW:_CONV_ROW + 6].reshape(DN_CONV, N_DEV * conv_shard), (0, me * conv_shard),
                               (DN_CONV, conv_shard))
    small_out = _adamw_vectors(summed, conv_g, [(row(w[n]), row(m[n]), row(v[n])) for n in _VECTORS]
                               + [(w["conv_w"], m["conv_w"], v["conv_w"])])

    result = [summed[_LOSS_ROW, 0], grad_x[None]]
    for i in range(4):
        for n in _ORDER:
            if n == "w_in":
                result.append(big[n][i].T[None])
            elif n in _MATRICES:
                result.append(big[n][i][None])
            elif n == "conv_w":
                result.append(small_out[n][i][None])
            else:
                result.append(small_out[n][i].reshape(w[n].shape))
    return tuple(result)
```

```python
import jax
import jax.numpy as jnp
import numpy as np
from jax import lax
from jax.experimental import pallas as pl
from jax.experimental.pallas import tpu as pltpu

F32, BF16 = jnp.float32, jnp.bfloat16
EPS = 1e-6
D_MODEL = 1024
N_DEV = 8
ATTN_BLOCK = 128
HEAD_PAIR = 128
DN_HEADS = 4
DN_DIM = 128
DN_CHUNK = 64
DN_CONV = 4
ROPE_THETA = 10000.0
D_IN = 2824
D_IN_PAD = 3072
BLK_Q, BLK_Z = 0, 1
BLK_DN, BLK_K, BLK_V, BLK_G = 8, 20, 21, 22
BLK_KV, BLK_G_PAD = 10, 11
VMEM_LIMIT = 56 * 1024 * 1024
NEG = -1e30
ADAM_LR, ADAM_B1, ADAM_B2, ADAM_EPS, ADAM_WD, ADAM_STEP = 0.001, 0.9, 0.999, 1e-08, 0.01, 10
MESH = pl.DeviceIdType.MESH


def _bf(x):
    return x.astype(BF16)


def _dot(a, b):
    return jnp.dot(a, b, preferred_element_type=F32)


def _dot_nt(a, b):
    return lax.dot_general(a, b, (((1,), (1,)), ((), ())), preferred_element_type=F32)


def _dot_tn(a, b):
    return lax.dot_general(a, b, (((0,), (0,)), ((), ())), preferred_element_type=F32)


def _sigmoid(x):
    return 1.0 / (1.0 + jnp.exp(-x))


def _params(sem):
    return pltpu.CompilerParams(dimension_semantics=sem, vmem_limit_bytes=VMEM_LIMIT)


def _mm(x, w, *, form, name, out_dtypes, tn, epi=None, extra=(), tm=512, w_row_block=0, after=None, norm=None,
        norm_bwd=None, then_nt=None):
    assert norm is None or norm_bwd is None
    xs = list(x) if isinstance(x, (list, tuple)) else [x]
    nx = len(xs)
    S, K = xs[0].shape
    shards = w.ndim == 3
    N = (w.shape[2] * N_DEV if shards else w.shape[1]) if form == "nn" else w.shape[-2]
    assert not (shards and form == "nn" and tn != w.shape[2]) and (nx == 1 or (form == "nn" and not shards and norm is None))
    r0 = w_row_block * K
    tm = min(tm, S)
    n_extra, n_out = len(extra), len(out_dtypes)
    tile = lambda width: pl.BlockSpec((tm, width), lambda i: (i, 0))
    whole = lambda a: pl.BlockSpec(a.shape, lambda i, nd=a.ndim: (0,) * nd)
    ins, in_specs = [*xs, w, *extra], [tile(K)] * nx + [whole(w)] + [tile(N)] * n_extra
    if norm is not None:
        ins, in_specs = ins + [norm], in_specs + [whole(norm)]
    if norm_bwd is not None:
        ins, in_specs = ins + list(norm_bwd), in_specs + [tile(N), whole(norm_bwd[1]), tile(N)]
    if then_nt is not None:
        ins, in_specs = ins + [then_nt], in_specs + [whole(then_nt)]
    if after is not None:
        ins, in_specs = ins + [after], in_specs + [whole(after)]
    out_shape = [jax.ShapeDtypeStruct((S, N), dt) for dt in out_dtypes]
    out_specs = [tile(N)] * n_out
    if norm is not None:
        out_shape, out_specs = out_shape + [jax.ShapeDtypeStruct((S, K), BF16)], out_specs + [tile(K)]
    if norm_bwd is not None:
        out_shape, out_specs = out_shape + [jax.ShapeDtypeStruct((1, N), F32)], out_specs + [pl.BlockSpec((1, N), lambda i: (0, 0))]
    if then_nt is not None:
        out_shape, out_specs = out_shape + [jax.ShapeDtypeStruct((S, then_nt.shape[0]), F32)], out_specs + [tile(then_nt.shape[0])]

    def product(xb, w_ref, cols, c):
        if form == "nn" and nx > 1:
            return sum(_dot(part, w_ref[r0 + p * K:r0 + (p + 1) * K, cols]) for p, part in enumerate(xb))
        if form == "nn":
            return _dot(xb, w_ref[c] if shards else w_ref[r0:r0 + K, cols])
        if not shards:
            return _dot_nt(xb, w_ref[cols, :])
        ks = w.shape[2]
        acc = _dot_nt(xb[:, 0:ks], w_ref[0, cols, :])
        for s in range(1, N_DEV):
            acc = acc + _dot_nt(xb[:, s * ks:(s + 1) * ks], w_ref[s, cols, :])
        return acc

    def body(*refs):
        x_ref, w_ref = refs[0], refs[nx]
        extra_refs = refs[nx + 1:nx + 1 + n_extra]
        at = nx + 1 + n_extra
        if norm is not None:
            gain_ref, at = refs[at], at + 1
        if norm_bwd is not None:
            (y_ref, ygain_ref, dres_ref), at = refs[at:at + 3], at + 3
        if then_nt is not None:
            w2_ref, at = refs[at], at + 1
        outs = refs[len(ins):]
        if norm is not None:
            _, xh = _rms_stats(x_ref[...])
            xb = _bf(xh * gain_ref[...])
            outs[n_out][...] = xb
        else:
            xb = _bf(x_ref[...]) if nx == 1 else [_bf(r[...]) for r in refs[:nx]]
        for c in range(N // tn):
            cols = slice(c * tn, (c + 1) * tn)
            acc = product(xb, w_ref, cols, c)
            res = epi(acc, *[r[:, cols] for r in extra_refs]) if epi else (acc,)
            for o, r in zip(outs[:n_out], res):
                o[:, cols] = r.astype(o.dtype)
        if norm_bwd is not None:
            dx, dg = _rms_bwd_tile(y_ref[...], ygain_ref[...], outs[0][...])
            outs[0][...] = dres_ref[...] + dx
            dg_ref = outs[n_out]

            @pl.when(pl.program_id(0) == 0)
            def _():
                dg_ref[...] = jnp.zeros_like(dg_ref)

            dg_ref[...] += dg
        if then_nt is not None:
            yb = _bf(outs[0][...])
            for c in range(then_nt.shape[0] // tn):
                cols = slice(c * tn, (c + 1) * tn)
                outs[-1][:, cols] = _dot_nt(yb, w2_ref[cols, :])

    return pl.pallas_call(
        body, grid=(S // tm,), name=name, in_specs=in_specs, out_specs=out_specs, out_shape=out_shape,
        compiler_params=_params(("arbitrary",) if norm_bwd is not None else ("parallel",)),
    )(*ins)


def _mm_tn(x, dy, *, name, tm, tn, out_dtype=F32, column_shards=False):
    S, K = x.shape
    N = dy.shape[1]

    def body(x_ref, dy_ref, o_ref):
        o_ref[...] = _dot_tn(_bf(x_ref[...]), _bf(dy_ref[...])).astype(out_dtype)

    if column_shards:
        out_spec = pl.BlockSpec((None, tm, tn), lambda i, j: (j, i, 0))
        out_shape = jax.ShapeDtypeStruct((N // tn, K, tn), out_dtype)
    else:
        out_spec = pl.BlockSpec((tm, tn), lambda i, j: (i, j))
        out_shape = jax.ShapeDtypeStruct((K, N), out_dtype)
    return pl.pallas_call(
        body, grid=(K // tm, N // tn), name=name,
        in_specs=[pl.BlockSpec((S, tm), lambda i, j: (0, i)), pl.BlockSpec((S, tn), lambda i, j: (0, j))],
        out_specs=out_spec, out_shape=out_shape,
        compiler_params=_params(("parallel", "parallel")),
    )(x, dy)


def _rowwise(body, *, tiled, full, out_tiled, out_acc, name, tm=512, smem=()):
    S = tiled[0].shape[0]
    tm = min(tm, S)
    n_in = len(smem) + len(tiled) + len(full)

    def kern(*refs):
        @pl.when(pl.program_id(0) == 0)
        def _():
            for r in refs[n_in + len(out_tiled):]:
                r[...] = jnp.zeros_like(r)
        body(*refs)

    in_specs = [pl.BlockSpec(memory_space=pltpu.SMEM) for _ in smem]
    in_specs += [pl.BlockSpec((tm, a.shape[1]), lambda i: (i, 0)) for a in tiled]
    in_specs += [pl.BlockSpec(a.shape, lambda i, nd=a.ndim: (0,) * nd) for a in full]
    out_specs = [pl.BlockSpec((tm, w), lambda i: (i, 0)) for w, _ in out_tiled]
    out_specs += [pl.BlockSpec(shp, lambda i, nd=len(shp): (0,) * nd) for shp, _ in out_acc]
    out_shape = [jax.ShapeDtypeStruct((S, w), dt) for w, dt in out_tiled]
    out_shape += [jax.ShapeDtypeStruct(shp, dt) for shp, dt in out_acc]
    return pl.pallas_call(
        kern, grid=(S // tm,), name=name, in_specs=in_specs, out_specs=out_specs, out_shape=out_shape,
        compiler_params=_params(("arbitrary",)),
    )(*smem, *tiled, *full)


def _rms_stats(x):
    r = lax.rsqrt(jnp.mean(x * x, axis=-1, keepdims=True) + EPS)
    return r, x * r


def _rmsnorm_fwd(x, g, name):
    def body(x_ref, g_ref, o_ref):
        _, xh = _rms_stats(x_ref[...])
        o_ref[...] = _bf(xh * g_ref[...])

    return _rowwise(body, tiled=[x], full=[g], out_tiled=[(x.shape[1], BF16)], out_acc=[], name=name)[0]


def _rms_bwd_tile(x, g, dxn):
    r, xh = _rms_stats(x)
    dg = jnp.sum(dxn * xh, axis=0, keepdims=True)
    dn = dxn * g
    dx = r * (dn - xh * jnp.mean(dn * xh, axis=-1, keepdims=True))
    return dx, dg


def _ple_and_loss(h2, p, target, w_pg, w_pp, g_ple, g_final):
    S, n = h2.shape
    tm = min(512, S)
    tn = 512

    def body(h2_ref, p_ref, t_ref, wpg_ref, wpp_ref, gple_ref, gfin_ref,
             n3_ref, dh_ref, dgl_ref, dpp_ref, loss_ref, dg_ref, dgple_ref, pp, gate, h3):
        @pl.when(pl.program_id(0) == 0)
        def _():
            loss_ref[...] = jnp.zeros_like(loss_ref)
            dg_ref[...] = jnp.zeros_like(dg_ref)
            dgple_ref[...] = jnp.zeros_like(dgple_ref)

        x = h2_ref[...]
        _, xh = _rms_stats(x)
        n3 = _bf(xh * gple_ref[...])
        n3_ref[...] = n3
        pb = _bf(p_ref[...])
        for c in range(n // tn):
            cols = slice(c * tn, (c + 1) * tn)
            pp[:, cols] = _dot(pb, wpp_ref[:, cols])
            gt = _sigmoid(_dot(n3, wpg_ref[:, cols]))
            gate[:, cols] = gt
            h3[:, cols] = x[:, cols] + gt * pp[:, cols]
        y = h3[...]
        _, yh = _rms_stats(y)
        e = yh * gfin_ref[...] - t_ref[...]
        per_tok = jnp.mean(e * e, axis=-1, keepdims=True)
        loss_ref[...] += 0.5 * jnp.sum(per_tok, axis=0, keepdims=True)
        dh, dg = _rms_bwd_tile(y, gfin_ref[...], e * (1.0 / n))
        dg_ref[...] += dg
        gt = gate[...]
        dgl = _bf(dh * pp[...] * gt * (1.0 - gt))
        dgl_ref[...] = dgl
        dpp_ref[...] = _bf(dh * gt)
        for c in range(n // tn):
            cols = slice(c * tn, (c + 1) * tn)
            h3[:, cols] = _dot_nt(dgl, wpg_ref[cols, :])
        dx, dgp = _rms_bwd_tile(x, gple_ref[...], h3[...])
        dh_ref[...] = dh + dx
        dgple_ref[...] += dgp

    tile = lambda width: pl.BlockSpec((tm, width), lambda i: (i, 0))
    whole = lambda a: pl.BlockSpec(a.shape, lambda i, nd=a.ndim: (0,) * nd)
    return pl.pallas_call(
        body, grid=(S // tm,), name="ple_and_loss",
        in_specs=[tile(n), tile(p.shape[1]), tile(n), whole(w_pg), whole(w_pp), whole(g_ple), whole(g_final)],
        out_specs=[tile(n), tile(n), tile(n), tile(n), pl.BlockSpec((1, 128), lambda i: (0, 0)),
                   pl.BlockSpec((1, n), lambda i: (0, 0)), pl.BlockSpec((1, n), lambda i: (0, 0))],
        out_shape=[jax.ShapeDtypeStruct((S, n), BF16), jax.ShapeDtypeStruct((S, n), F32), jax.ShapeDtypeStruct((S, n), BF16),
                   jax.ShapeDtypeStruct((S, n), BF16), jax.ShapeDtypeStruct((1, 128), F32), jax.ShapeDtypeStruct((1, n), F32),
                   jax.ShapeDtypeStruct((1, n), F32)],
        scratch_shapes=[pltpu.VMEM((tm, n), F32)] * 3,
        compiler_params=_params(("arbitrary",)),
    )(h2, p, target, w_pg, w_pp, g_ple, g_final)


def _rope_tables(S):
    half = 32
    inv = (1.0 / (np.float32(ROPE_THETA) ** (np.arange(half, dtype=np.float32) * np.float32(2.0 / 64)))).astype(np.float32)
    ang = np.arange(S).astype(np.float32)[:, None] * inv[None, :]
    cos, sin = np.cos(ang), np.sin(ang)
    return jnp.asarray(np.tile(cos, (1, 4))), jnp.asarray(np.concatenate([-sin, sin, -sin, sin], axis=1))


def _attn_common(i, kc, kp, vc, vp, cc, sc, cp, sp):
    lane = lax.broadcasted_iota(jnp.int32, (1, HEAD_PAIR), 1)
    lane_lo = jnp.bitwise_and(lane, 63) < 32
    slot = [lane < 64, lane >= 64]

    def swap_halves(t):
        return jnp.where(lane_lo, pltpu.roll(t, 96, 1), pltpu.roll(t, 32, 1))

    def rope(t, cos, sin):
        return t * cos + swap_halves(t) * sin

    def unrope(d, cos, sin):
        return d * cos + swap_halves(d * sin)

    k2 = jnp.concatenate([rope(kp, cp, sp), rope(kc, cc, sc)], axis=0)
    v2 = jnp.concatenate([vp, vc], axis=0)
    r = lax.broadcasted_iota(jnp.int32, (ATTN_BLOCK, 2 * ATTN_BLOCK), 0)
    c = lax.broadcasted_iota(jnp.int32, (ATTN_BLOCK, 2 * ATTN_BLOCK), 1)
    valid = (c > r) & (c <= r + ATTN_BLOCK) & jnp.logical_or(c >= ATTN_BLOCK, i > 0)
    ks, vs = {}, {}
    for j in range(2):
        kn = jnp.where(slot[j], k2, 0.0)
        vn = jnp.where(slot[j], v2, 0.0)
        for s in range(2):
            ks[j, s] = _bf(kn if s == j else pltpu.roll(kn, 64, 1))
            vs[j, s] = _bf(vn if s == j else pltpu.roll(vn, 64, 1))
    return slot, rope, unrope, valid, ks, vs


def _attn_probs(scores, valid, sink):
    s = jnp.where(valid, scores * 0.125, NEG)
    m = jnp.maximum(jnp.max(s, axis=1, keepdims=True), sink)
    e = jnp.exp(s - m)
    z = jnp.sum(e, axis=1, keepdims=True) + jnp.exp(sink - m)
    return e * (1.0 / z), m + jnp.log(z)


def _attn_specs(S):
    nb = S // ATTN_BLOCK
    prev = lambda i: jnp.maximum(i - 1, 0)
    blk = lambda w, col, row=(lambda i: i): pl.BlockSpec((ATTN_BLOCK, w), lambda i: (row(i), col))
    in_specs = [pl.BlockSpec(memory_space=pltpu.SMEM),
                blk(512, BLK_Q), blk(128, BLK_K), blk(128, BLK_K, prev), blk(128, BLK_V), blk(128, BLK_V, prev),
                blk(128, 0), blk(128, 0), blk(128, 0, prev), blk(128, 0, prev)]
    return nb, in_specs


def _attn_fwd(pa, cos, sin, sinks):
    S = pa.shape[0]
    nb, in_specs = _attn_specs(S)

    def body(sinks_ref, q_ref, kc_ref, kp_ref, vc_ref, vp_ref, cc_ref, sc_ref, cp_ref, sp_ref, o_ref, lse_ref):
        i = pl.program_id(0)
        lane = lax.broadcasted_iota(jnp.int32, (1, HEAD_PAIR), 1)
        cc, sc = cc_ref[...], sc_ref[...]
        _, rope, _, valid, ks, vs = _attn_common(i, kc_ref[...], kp_ref[...], vc_ref[...], vp_ref[...],
                                                 cc, sc, cp_ref[...], sp_ref[...])
        pair_cols = [slice(HEAD_PAIR * pair, HEAD_PAIR * (pair + 1)) for pair in range(4)]
        qps = [_bf(rope(q_ref[:, cols], cc, sc)) for cols in pair_cols]
        outs, lses = {}, {}

        def head_program(h):
            pair, s = divmod(h, 2)
            j = h // 4
            scores = _dot_nt(qps[pair], ks[j, s])
            yield
            p, lse = _attn_probs(scores, valid, sinks_ref[h])
            outs[h] = _dot(_bf(p), vs[j, s])
            lses[h] = jnp.where(lane == h, lse, 0.0)

        _interleave(head_program(h) for h in range(8))
        for pair, cols in enumerate(pair_cols):
            o_ref[:, cols] = outs[2 * pair] + outs[2 * pair + 1]
        lse_ref[...] = sum((lses[h] for h in range(1, 8)), lses[0])

    return pl.pallas_call(
        body, grid=(nb,), name="attn_fwd", in_specs=in_specs,
        out_specs=[pl.BlockSpec((ATTN_BLOCK, 512), lambda i: (i, 0)), pl.BlockSpec((ATTN_BLOCK, 128), lambda i: (i, 0))],
        out_shape=[jax.ShapeDtypeStruct((S, 512), F32), jax.ShapeDtypeStruct((S, 128), F32)],
        compiler_params=_params(("parallel",)),
    )(sinks, pa, pa, pa, pa, pa, cos, sin, cos, sin)


def _attn_bwd(pa, cos, sin, sinks, dcat, attn, lse):
    S = pa.shape[0]
    nb, in_specs = _attn_specs(S)
    in_specs = in_specs + [pl.BlockSpec((ATTN_BLOCK, 512), lambda i: (i, 0))] * 2 + [pl.BlockSpec((ATTN_BLOCK, 128), lambda i: (i, 0))]

    def body(sinks_ref, q_ref, kc_ref, kp_ref, vc_ref, vp_ref, cc_ref, sc_ref, cp_ref, sp_ref, do_ref, o_ref, lse_ref,
             dq_ref, dk_ref, dv_ref, dsink_ref):
        i = pl.program_id(0)

        @pl.when(i == 0)
        def _():
            dk_ref[...] = jnp.zeros_like(dk_ref)
            dv_ref[...] = jnp.zeros_like(dv_ref)
            dsink_ref[...] = jnp.zeros_like(dsink_ref)

        cc, sc, cp, sp = cc_ref[...], sc_ref[...], cp_ref[...], sp_ref[...]
        slot, rope, unrope, valid, ks, vs = _attn_common(i, kc_ref[...], kp_ref[...], vc_ref[...], vp_ref[...], cc, sc, cp, sp)
        pair_cols = [slice(HEAD_PAIR * pair, HEAD_PAIR * (pair + 1)) for pair in range(4)]
        qps = [_bf(rope(q_ref[:, cols], cc, sc)) for cols in pair_cols]
        dobs = [_bf(do_ref[:, cols]) for cols in pair_cols]
        do_o = [do_ref[:, cols] * o_ref[:, cols] for cols in pair_cols]
        dqs, dks, dvs = {}, {}, {}

        def head_program(h):
            pair, s = divmod(h, 2)
            j = h // 4
            qp, dob = qps[pair], dobs[pair]
            scores = _dot_nt(qp, ks[j, s])
            dp = _dot_nt(dob, vs[j, s])
            yield
            lse_h = lse_ref[:, h:h + 1]
            p = jnp.exp(jnp.where(valid, scores * 0.125, NEG) - lse_h)
            dr = jnp.sum(jnp.where(slot[s], do_o[pair], 0.0), axis=1, keepdims=True)
            ds = _bf(p * (dp - dr) * 0.125)
            dsink_ref[h:h + 1, :] += -jnp.sum(jnp.exp(sinks_ref[h] - lse_h) * dr, axis=0, keepdims=True)
            dqs[h] = _dot(ds, ks[j, s])
            dk_h = _dot_tn(ds, qp)
            dv_h = _dot_tn(_bf(p), dob)
            yield
            dk_h, dv_h = jnp.where(slot[s], dk_h, 0.0), jnp.where(slot[s], dv_h, 0.0)
            if s != j:
                dk_h, dv_h = pltpu.roll(dk_h, 64, 1), pltpu.roll(dv_h, 64, 1)
            dks[h], dvs[h] = dk_h, dv_h

        _interleave(head_program(h) for h in range(8))
        dk2 = sum((dks[h] for h in range(1, 8)), dks[0])
        dv2 = sum((dvs[h] for h in range(1, 8)), dvs[0])
        for pair, cols in enumerate(pair_cols):
            dq_ref[:, cols] = _bf(unrope(dqs[2 * pair] + dqs[2 * pair + 1], cc, sc))
        cur = pl.ds(pl.multiple_of(i * ATTN_BLOCK, ATTN_BLOCK), ATTN_BLOCK)
        dk_ref[cur, :] += unrope(dk2[ATTN_BLOCK:], cc, sc)
        dv_ref[cur, :] += dv2[ATTN_BLOCK:]

        @pl.when(i > 0)
        def _():
            prv = pl.ds(pl.multiple_of((i - 1) * ATTN_BLOCK, ATTN_BLOCK), ATTN_BLOCK)
            dk_ref[prv, :] += unrope(dk2[:ATTN_BLOCK], cp, sp)
            dv_ref[prv, :] += dv2[:ATTN_BLOCK]

    whole = lambda w: pl.BlockSpec((S, w), lambda i: (0, 0))
    return pl.pallas_call(
        body, grid=(nb,), name="attn_bwd", in_specs=in_specs,
        out_specs=[pl.BlockSpec((ATTN_BLOCK, 512), lambda i: (i, BLK_Q)), whole(128), whole(128),
                   pl.BlockSpec((8, 128), lambda i: (0, 0))],
        out_shape=[jax.ShapeDtypeStruct((S, D_IN_PAD), BF16), jax.ShapeDtypeStruct((S, 128), F32),
                   jax.ShapeDtypeStruct((S, 128), F32), jax.ShapeDtypeStruct((8, 128), F32)],
        compiler_params=_params(("arbitrary",)),
    )(sinks, pa, pa, pa, pa, pa, cos, sin, cos, sin, dcat, attn, lse)


CONV_ROWS = 512
CONV_PAD = 8


def _conv_silu(scr, w, r0):
    y = w[3:4, :] * scr[pl.ds(CONV_PAD + r0, CONV_ROWS), :]
    for j in range(DN_CONV - 1):
        y = y + w[j:j + 1, :] * scr[pl.ds(CONV_PAD + r0 - 3 + j, CONV_ROWS), :]
    return y


def _dn_prep_fwd(pd, conv_w):
    S = pd.shape[0]
    assert S % CONV_ROWS == 0

    def body(x_ref, w_ref, o_ref, scr):
        b = pl.program_id(0)
        scr[0:CONV_PAD, :] = jnp.zeros((CONV_PAD, DN_DIM), F32)
        scr[pl.ds(CONV_PAD, S), :] = x_ref[...]
        w = w_ref[...]
        q_scale = jnp.where(b < DN_HEADS, DN_DIM ** -0.5, 1.0)
        for r0 in range(0, S, CONV_ROWS):
            y = _conv_silu(scr, w, r0)
            a = y * _sigmoid(y)
            rs = lax.rsqrt(jnp.sum(a * a, axis=1, keepdims=True) + EPS)
            o_ref[pl.ds(r0, CONV_ROWS), :] = a * jnp.where(b < 2 * DN_HEADS, rs * q_scale, 1.0)

    col = pl.BlockSpec((S, DN_DIM), lambda b: (0, b))
    return pl.pallas_call(
        body, grid=(3 * DN_HEADS,), name="dn_prep_fwd",
        in_specs=[pl.BlockSpec((S, DN_DIM), lambda b: (0, BLK_DN + b)), pl.BlockSpec((DN_CONV, DN_DIM), lambda b: (0, b))],
        out_specs=col,
        out_shape=jax.ShapeDtypeStruct((S, 3 * DN_HEADS * DN_DIM), F32),
        scratch_shapes=[pltpu.VMEM((S + CONV_PAD, DN_DIM), F32)],
        compiler_params=_params(("parallel",)),
    )(pd, conv_w)


def _dn_prep_bwd(pd, conv_w, dqkv, dproj):
    S = pd.shape[0]

    def body(x_ref, w_ref, d_ref, _, dx_ref, dw_ref, scr, dscr):
        b = pl.program_id(0)
        scr[0:CONV_PAD, :] = jnp.zeros((CONV_PAD, DN_DIM), F32)
        scr[pl.ds(CONV_PAD, S), :] = x_ref[...]
        dscr[pl.ds(S, CONV_PAD), :] = jnp.zeros((CONV_PAD, DN_DIM), F32)
        w = w_ref[...]
        q_scale = jnp.where(b < DN_HEADS, DN_DIM ** -0.5, 1.0)
        is_qk = b < 2 * DN_HEADS
        dw = [jnp.zeros((1, DN_DIM), F32) for _ in range(DN_CONV)]
        for r0 in range(0, S, CONV_ROWS):
            y = _conv_silu(scr, w, r0)
            sg = _sigmoid(y)
            a = y * sg
            dout = d_ref[pl.ds(r0, CONV_ROWS), :]
            rs = lax.rsqrt(jnp.sum(a * a, axis=1, keepdims=True) + EPS)
            da_qk = q_scale * rs * (dout - a * (rs * rs) * jnp.sum(dout * a, axis=1, keepdims=True))
            dy = jnp.where(is_qk, da_qk, dout) * (sg * (1.0 + y * (1.0 - sg)))
            dscr[pl.ds(r0, CONV_ROWS), :] = dy
            for j in range(DN_CONV):
                dw[j] = dw[j] + jnp.sum(dy * scr[pl.ds(CONV_PAD + r0 - 3 + j, CONV_ROWS), :], axis=0, keepdims=True)
        for j in range(DN_CONV):
            dw_ref[j:j + 1, :] = dw[j]
        for r0 in range(0, S, CONV_ROWS):
            dx = w[3:4, :] * dscr[pl.ds(r0, CONV_ROWS), :]
            for j in range(DN_CONV - 1):
                dx = dx + w[j:j + 1, :] * dscr[pl.ds(r0 + 3 - j, CONV_ROWS), :]
            dx_ref[pl.ds(r0, CONV_ROWS), :] = _bf(dx)

    col = pl.BlockSpec((S, DN_DIM), lambda b: (0, b))
    proj_col = pl.BlockSpec((S, DN_DIM), lambda b: (0, BLK_DN + b))
    wcol = pl.BlockSpec((DN_CONV, DN_DIM), lambda b: (0, b))
    return pl.pallas_call(
        body, grid=(3 * DN_HEADS,), name="dn_prep_bwd",
        in_specs=[proj_col, wcol, col, pl.BlockSpec(memory_space=pl.ANY)], out_specs=[proj_col, wcol],
        out_shape=[jax.ShapeDtypeStruct(dproj.shape, dproj.dtype), jax.ShapeDtypeStruct((DN_CONV, 3 * DN_HEADS * DN_DIM), F32)],
        scratch_shapes=[pltpu.VMEM((S + CONV_PAD, DN_DIM), F32), pltpu.VMEM((S + CONV_PAD, DN_DIM), F32)],
        input_output_aliases={3: 0},
        compiler_params=_params(("parallel",)),
    )(pd, conv_w, dqkv, dproj)


CPAD = 128
CHUNKS_LOCAL = 4
CHUNKS_SCAN = 4


def _chunk_masks():
    ii = lax.broadcasted_iota(jnp.int32, (DN_CHUNK, CPAD), 0)
    jj = lax.broadcasted_iota(jnp.int32, (DN_CHUNK, CPAD), 1)
    return ii, jj


def _rows_pad(a):
    return jnp.concatenate([a, jnp.zeros_like(a)], axis=0)


def _hi_lo(a):
    hi = _bf(a)
    return hi, _bf(a - hi.astype(F32))


def _double_step(t, p):
    C = DN_CHUNK
    th, tl = _hi_lo(t)
    ph, pl_ = _hi_lo(p)
    r1 = _dot(jnp.concatenate([th, tl, ph, pl_], axis=0), _rows_pad(ph))
    r2 = _dot(jnp.concatenate([th, ph], axis=0), _rows_pad(pl_))
    return t + (r1[:C] + r1[C:2 * C] + r2[:C]), r1[2 * C:3 * C] + r1[3 * C:] + r2[C:]


def _dot3_nt(a, b):
    C = DN_CHUNK
    ah, al = _hi_lo(a)
    bh, bl = _hi_lo(b)
    r1 = _dot_nt(jnp.concatenate([ah, al], axis=0), _rows_pad(bh))
    return r1[:C] + r1[C:] + _dot_nt(ah, _rows_pad(bl))


def _dot3_tn(a, b):
    C = DN_CHUNK
    ah, al = _hi_lo(a)
    bh, bl = _hi_lo(b)
    return _dot_tn(jnp.concatenate([ah, al, ah], axis=0), jnp.concatenate([bh, bh, bl], axis=0))[:C]


def _interleave(programs):
    programs = list(programs)
    while programs:
        alive = []
        for prog in programs:
            try:
                next(prog)
                alive.append(prog)
            except StopIteration:
                pass
        programs = alive


def _col_to_row(col, ii, jj):
    return jnp.sum(jnp.where(ii == jj, col, 0.0), axis=0, keepdims=True)


def _row_to_col(row, ii, jj):
    return jnp.sum(jnp.where(ii == jj, row, 0.0), axis=1, keepdims=True)


def _decay(gc_col, ii, jj):
    diff = gc_col - _col_to_row(gc_col, ii, jj)
    return jnp.where(jj <= ii, jnp.exp(jnp.where(jj <= ii, diff, 0.0)), 0.0)


def _softplus(x):
    return jnp.maximum(x, 0.0) + jnp.log(1.0 + jnp.exp(-jnp.abs(x)))


def _head(h):
    return slice(DN_DIM * h, DN_DIM * (h + 1))


def _dn_chunk_fwd(qkv, pg, a_log, dt_bias):
    S = qkv.shape[0]
    C = DN_CHUNK
    G = CHUNKS_LOCAL
    R = G * C
    steps = S // R

    def body(alog_ref, dtb_ref, qkv_ref, pg_ref, w_ref, u_ref, qg_ref, kd_ref, a_ref, t_ref, gcs_ref):
        ii, jj = _chunk_masks()
        lane = lax.broadcasted_iota(jnp.int32, (1, 128), 1)
        eye = (ii == jj).astype(F32)
        gcs_parts = [[] for _ in range(G)]

        def head_program(chunk, h):
            rows = slice(chunk * C, (chunk + 1) * C)
            q, k, v = qkv_ref[rows, _head(h)], qkv_ref[rows, _head(DN_HEADS + h)], qkv_ref[rows, _head(2 * DN_HEADS + h)]
            beta = _sigmoid(pg_ref[rows, h:h + 1])
            g_col = -jnp.exp(alog_ref[h]) * _softplus(pg_ref[rows, DN_HEADS + h:DN_HEADS + h + 1] + dtb_ref[h])
            g_row = _col_to_row(g_col, ii, jj)
            gc_col = jnp.sum(jnp.where(jj <= ii, g_row, 0.0), axis=1, keepdims=True)
            dec = _decay(gc_col, ii, jj)
            eg = jnp.exp(gc_col)
            kb, vb = k * beta, v * beta
            k_rows = _rows_pad(_bf(k))
            kk = _dot_nt(_bf(kb), k_rows)
            qk = _dot_nt(_bf(q), k_rows)
            yield
            t, pw = eye, -jnp.where(jj < ii, kk * dec, 0.0)
            for _ in range(6):
                t, pw = _double_step(t, pw)
                yield
            tb = _bf(t)
            u_ref[rows, _head(h)] = _dot(tb, _rows_pad(_bf(vb)))
            w_ref[rows, _head(h)] = _bf(_dot(tb, _rows_pad(_bf(kb * eg))))
            a_ref[h, rows] = _bf(qk * dec)
            t_ref[h, rows] = t
            qg_ref[rows, _head(h)] = _bf(q * eg)
            kd_ref[rows, _head(h)] = _bf(k * jnp.exp(gc_col[C - 1:C, :] - gc_col))
            gcs_parts[chunk].append(jnp.where(lane == h, gc_col, 0.0) + jnp.where(lane == DN_HEADS + h, beta, 0.0)
                                    + jnp.where(lane == 2 * DN_HEADS + h, g_col, 0.0))

        _interleave(head_program(chunk, h) for chunk in range(G) for h in range(DN_HEADS))
        for chunk in range(G):
            gcs_ref[chunk * C:(chunk + 1) * C, :] = sum(gcs_parts[chunk][1:], gcs_parts[chunk][0])

    smem = pl.BlockSpec(memory_space=pltpu.SMEM)
    wide = pl.BlockSpec((R, 512), lambda n: (n, 0))
    sq = pl.BlockSpec((DN_HEADS, R, CPAD), lambda n: (0, n, 0))
    narrow = pl.BlockSpec((R, 128), lambda n: (n, 0))
    f = lambda *shp: jax.ShapeDtypeStruct(shp, F32)
    b = lambda *shp: jax.ShapeDtypeStruct(shp, BF16)
    return pl.pallas_call(
        body, grid=(steps,), name="dn_chunk_fwd",
        in_specs=[smem, smem, pl.BlockSpec((R, 1536), lambda n: (n, 0)), pl.BlockSpec((R, 128), lambda n: (n, BLK_G))],
        out_specs=[wide, wide, wide, wide, sq, sq, narrow],
        out_shape=[b(S, 512), f(S, 512), b(S, 512), b(S, 512), b(DN_HEADS, S, CPAD), f(DN_HEADS, S, CPAD), f(S, 128)],
        compiler_params=_params(("parallel",)),
    )(a_log, dt_bias, qkv, pg)


def _gated_norm(o, z, gn):
    r, oh = _rms_stats(o)
    return oh * gn * (z * _sigmoid(z))


def _dn_scan_fwd(w, u, qg, kd, a, gcs, pz, gn):
    S = w.shape[0]
    C = DN_CHUNK
    nc = S // C
    G = CHUNKS_SCAN
    R = G * C

    def body(w_ref, u_ref, qg_ref, kd_ref, a_ref, gcs_ref, z_ref, gn_ref, o_ref, vn_ref, sst_ref, out_ref, state):
        @pl.when(pl.program_id(0) == 0)
        def _():
            state[...] = jnp.zeros_like(state)

        def head_program(chunk, h):
            hs = _head(h)
            rows = slice(chunk * C, (chunk + 1) * C)
            s_in = state[h]
            sst_ref[chunk, h] = s_in
            sb = _bf(s_in)
            w_s = _dot(w_ref[rows, hs], sb)
            q_s = _dot(qg_ref[rows, hs], sb)
            yield
            vn = u_ref[rows, hs] - w_s
            vnb = _bf(vn)
            o = q_s + _dot(a_ref[h, rows], _rows_pad(vnb))
            k_v = _dot_tn(kd_ref[rows, hs], vnb)
            yield
            state[h] = s_in * jnp.exp(gcs_ref[(chunk + 1) * C - 1:(chunk + 1) * C, h:h + 1]) + k_v
            o_ref[rows, hs] = o
            vn_ref[rows, hs] = vn
            out_ref[rows, hs] = _gated_norm(o, z_ref[rows, hs], gn_ref[...])

        for chunk in range(G):
            _interleave(head_program(chunk, h) for h in range(DN_HEADS))

    wide = pl.BlockSpec((R, 512), lambda n: (n, 0))
    f = lambda *shp: jax.ShapeDtypeStruct(shp, F32)
    return pl.pallas_call(
        body, grid=(nc // G,), name="dn_scan_fwd",
        in_specs=[wide, wide, wide, wide, pl.BlockSpec((DN_HEADS, R, CPAD), lambda n: (0, n, 0)),
                  pl.BlockSpec((R, 128), lambda n: (n, 0)), pl.BlockSpec((R, 512), lambda n: (n, BLK_Z)),
                  pl.BlockSpec((1, DN_DIM), lambda n: (0, 0))],
        out_specs=[wide, wide, pl.BlockSpec((G, DN_HEADS, DN_DIM, DN_DIM), lambda n: (n, 0, 0, 0)), wide],
        out_shape=[f(S, 512), f(S, 512), f(nc, DN_HEADS, DN_DIM, DN_DIM), f(S, 512)],
        scratch_shapes=[pltpu.VMEM((DN_HEADS, DN_DIM, DN_DIM), F32)],
        compiler_params=_params(("arbitrary",)),
    )(w, u, qg, kd, a, gcs, pz, gn)


def _dn_scan_bwd(dcat, o, pz, gn, sst, vnew, w, qg, kd, a, gcs, dproj):
    S = o.shape[0]
    C = DN_CHUNK
    G = CHUNKS_SCAN
    R = G * C
    steps = S // R

    def body(dy_ref, o_ref, z_ref, gn_ref, sst_ref, vn_ref, w_ref, qg_ref, kd_ref, a_ref, gcs_ref, _,
             du_ref, dw_ref, dqg_ref, dkd_ref, da_ref, dz_ref, dsc_ref, dgn_ref, dstate):
        @pl.when(pl.program_id(0) == 0)
        def _():
            dstate[...] = jnp.zeros_like(dstate)
            dgn_ref[...] = jnp.zeros_like(dgn_ref)

        gn_ = gn_ref[...]
        lane = lax.broadcasted_iota(jnp.int32, (C, 128), 1)
        row = lax.broadcasted_iota(jnp.int32, (C, 128), 0)
        dgn_parts = []

        def head_program(chunk, h, dsc_parts):
            hs = _head(h)
            rows = slice(chunk * C, (chunk + 1) * C)
            ov, z, dout = o_ref[rows, hs], z_ref[rows, hs], dy_ref[rows, hs]
            r, oh = _rms_stats(ov)
            sg = _sigmoid(z)
            don = dout * (z * sg)
            dz_ref[rows, hs] = _bf(dout * (oh * gn_) * (sg * (1.0 + z * (1.0 - sg))))
            dgn_parts.append(jnp.sum(don * oh, axis=0, keepdims=True))
            dn = don * gn_
            do = _bf(r * (dn - oh * jnp.mean(dn * oh, axis=-1, keepdims=True)))
            s_in = sst_ref[chunk, h]
            sb = _bf(s_in)
            ds_out = dstate[h]
            dsb = _bf(ds_out)
            vnb = _bf(vn_ref[rows, hs])
            wb, qgb, kdb, ab = w_ref[rows, hs], qg_ref[rows, hs], kd_ref[rows, hs], a_ref[h, rows]
            dvn = _dot_tn(ab, do)[:C] + _dot(kdb, dsb)
            da_ref[h, rows] = _dot_nt(do, _rows_pad(vnb))
            dqg_ref[rows, hs] = _dot_nt(do, sb)
            dkd_ref[rows, hs] = _dot_nt(vnb, dsb)
            q_do = _dot_tn(qgb, do)
            yield
            dvnb = _bf(dvn)
            dw_ref[rows, hs] = _bf(-_dot_nt(dvnb, sb))
            w_dvn = _dot_tn(wb, dvnb)
            du_ref[rows, hs] = dvnb
            yield
            d_last = jnp.exp(gcs_ref[(chunk + 1) * C - 1:(chunk + 1) * C, h:h + 1])
            dd = jnp.sum(jnp.sum(ds_out * s_in, axis=1, keepdims=True), axis=0, keepdims=True)
            dsc_parts.append(jnp.where((lane == h) & (row == C - 1), dd * d_last, 0.0))
            dstate[h] = ds_out * d_last + q_do - w_dvn

        for chunk in reversed(range(G)):
            dsc_parts = []
            _interleave(head_program(chunk, h, dsc_parts) for h in range(DN_HEADS))
            dsc_ref[chunk * C:(chunk + 1) * C, :] = sum(dsc_parts[1:], dsc_parts[0])
        dgn_ref[...] += sum(dgn_parts[1:], dgn_parts[0])

    rev = lambda n: steps - 1 - n
    wide = pl.BlockSpec((R, 512), lambda n: (rev(n), 0))
    z_spec = pl.BlockSpec((R, 512), lambda n: (rev(n), BLK_Z))
    sq = pl.BlockSpec((DN_HEADS, R, CPAD), lambda n: (0, rev(n), 0))
    narrow = pl.BlockSpec((R, 128), lambda n: (rev(n), 0))
    gn_spec = pl.BlockSpec((1, DN_DIM), lambda n: (0, 0))
    f = lambda *shp: jax.ShapeDtypeStruct(shp, F32)
    b = lambda *shp: jax.ShapeDtypeStruct(shp, BF16)
    return pl.pallas_call(
        body, grid=(steps,), name="dn_scan_bwd",
        in_specs=[pl.BlockSpec((R, 512), lambda n: (rev(n), 1)), wide, z_spec, gn_spec,
                  pl.BlockSpec((G, DN_HEADS, DN_DIM, DN_DIM), lambda n: (rev(n), 0, 0, 0)),
                  wide, wide, wide, wide, sq, narrow, pl.BlockSpec(memory_space=pl.ANY)],
        out_specs=[wide, wide, wide, wide, sq, z_spec, narrow, gn_spec],
        out_shape=[b(S, 512), b(S, 512), f(S, 512), f(S, 512), f(DN_HEADS, S, CPAD),
                   jax.ShapeDtypeStruct(dproj.shape, dproj.dtype), f(S, 128), f(1, DN_DIM)],
        scratch_shapes=[pltpu.VMEM((DN_HEADS, DN_DIM, DN_DIM), F32)],
        input_output_aliases={11: 5},
        compiler_params=_params(("arbitrary",)),
    )(dcat, o, pz, gn, sst, vnew, w, qg, kd, a, gcs, dproj)


def _dn_chunk_bwd(qkv, pg, t_inv, gcs, du, dw, dqg, dkd, da, dsc, a_log, dt_bias, dproj):
    S = qkv.shape[0]
    C = DN_CHUNK
    G = CHUNKS_LOCAL
    R = G * C

    def body(alog_ref, dtb_ref, qkv_ref, pg_ref, t_ref, gcs_ref, du_ref, dw_ref, dqg_ref, dkd_ref, da_ref, dsc_ref, _,
             dqkv_ref, dpg_ref, acc_ref):
        @pl.when(pl.program_id(0) == 0)
        def _():
            acc_ref[...] = jnp.zeros_like(acc_ref)

        ii, jj = _chunk_masks()
        lane = lax.broadcasted_iota(jnp.int32, (1, 128), 1)
        row8 = lax.broadcasted_iota(jnp.int32, (8, 128), 0)
        lane8 = lax.broadcasted_iota(jnp.int32, (8, 128), 1)
        rowc = lax.broadcasted_iota(jnp.int32, (C, 1), 0)
        tril, strict = jj <= ii, jj < ii
        dpg_parts, acc_parts = [[] for _ in range(G)], []

        def head_program(chunk, h):
            rows = slice(chunk * C, (chunk + 1) * C)
            q, k, v = qkv_ref[rows, _head(h)], qkv_ref[rows, _head(DN_HEADS + h)], qkv_ref[rows, _head(2 * DN_HEADS + h)]
            gc_col, beta, g_col = gcs_ref[rows, h:h + 1], gcs_ref[rows, DN_HEADS + h:DN_HEADS + h + 1], \
                gcs_ref[rows, 2 * DN_HEADS + h:2 * DN_HEADS + h + 1]
            dec = _decay(gc_col, ii, jj)
            eg = jnp.exp(gc_col)
            g_last = gc_col[C - 1:C, :]
            ek = jnp.exp(g_last - gc_col)
            kb, vb = k * beta, v * beta
            kbg = kb * eg
            qb, kbb = _bf(q), _bf(kb)
            k_rows = _rows_pad(_bf(k))
            t = t_ref[h, rows]
            tb = _bf(t)
            dub, dwb = du_ref[rows, _head(h)], dw_ref[rows, _head(h)]
            dqg_, dkd_ = dqg_ref[rows, _head(h)], dkd_ref[rows, _head(h)]
            dt = _dot_nt(dub, _rows_pad(_bf(vb))) + _dot_nt(dwb, _rows_pad(_bf(kbg)))
            t_du_dw = _dot_tn(tb, jnp.concatenate([dub, dwb], axis=1))
            dvb, dkbg = t_du_dw[:C, :DN_DIM], t_du_dw[:C, DN_DIM:]
            kk = _dot_nt(kbb, k_rows)
            qk = _dot_nt(qb, k_rows)
            yield
            dt_t = _dot3_nt(dt, t)
            yield
            dl = -_dot3_tn(t, dt_t)
            yield
            dm = jnp.where(strict, dl * dec, 0.0)
            dqk = jnp.where(tril, da_ref[h, rows] * dec, 0.0)
            gmat = dm * kk + dqk * qk
            dgc = jnp.sum(gmat, axis=1, keepdims=True) - _row_to_col(jnp.sum(gmat, axis=0, keepdims=True), ii, jj)
            dmb, dqkb = _bf(dm), _bf(dqk)
            dkb = _dot(dmb, k_rows) + dkbg * eg
            dk = _dot_tn(jnp.concatenate([dmb, dqkb], axis=0), jnp.concatenate([kbb, qb], axis=0))[:C] + dkd_ * ek
            dq = _dot(dqkb, k_rows) + dqg_ * eg
            yield
            tk = jnp.sum(dkd_ * k * ek, axis=1, keepdims=True)
            dgc = dgc + jnp.sum(dqg_ * q * eg, axis=1, keepdims=True) - tk + jnp.sum(dkbg * kbg, axis=1, keepdims=True)
            dgl = jnp.sum(tk, axis=0, keepdims=True) + dsc_ref[(chunk + 1) * C - 1:(chunk + 1) * C, h:h + 1]
            dgc = dgc + jnp.where(rowc == C - 1, dgl, 0.0)
            dk = dk + dkb * beta
            dbeta = jnp.sum(dkb * k, axis=1, keepdims=True) + jnp.sum(dvb * v, axis=1, keepdims=True)
            dqkv_ref[rows, _head(h)] = dq
            dqkv_ref[rows, _head(DN_HEADS + h)] = dk
            dqkv_ref[rows, _head(2 * DN_HEADS + h)] = dvb * beta
            dg_col = jnp.sum(jnp.where(jj >= ii, _col_to_row(dgc, ii, jj), 0.0), axis=1, keepdims=True)
            db = dbeta * beta * (1.0 - beta)
            da_in = dg_col * (-jnp.exp(alog_ref[h])) * _sigmoid(pg_ref[rows, DN_HEADS + h:DN_HEADS + h + 1] + dtb_ref[h])
            dpg_parts[chunk].append(jnp.where(lane == h, db, 0.0) + jnp.where(lane == DN_HEADS + h, da_in, 0.0))
            acc_parts.append(jnp.where((row8 == 0) & (lane8 == h), jnp.sum(dg_col * g_col, axis=0, keepdims=True), 0.0)
                             + jnp.where((row8 == 1) & (lane8 == h), jnp.sum(da_in, axis=0, keepdims=True), 0.0))

        _interleave(head_program(chunk, h) for chunk in range(G) for h in range(DN_HEADS))
        for chunk in range(G):
            dpg = sum(dpg_parts[chunk][1:], dpg_parts[chunk][0])
            dpg_ref[chunk * C:(chunk + 1) * C, :] = _bf(jnp.concatenate([dpg, jnp.zeros_like(dpg)], axis=1))
        acc_ref[...] += sum(acc_parts[1:], acc_parts[0])

    smem = pl.BlockSpec(memory_space=pltpu.SMEM)
    wide = pl.BlockSpec((R, 512), lambda n: (n, 0))
    sq = pl.BlockSpec((DN_HEADS, R, CPAD), lambda n: (0, n, 0))
    narrow = pl.BlockSpec((R, 128), lambda n: (n, 0))
    qkv_spec = pl.BlockSpec((R, 1536), lambda n: (n, 0))
    f = lambda *shp: jax.ShapeDtypeStruct(shp, F32)
    return pl.pallas_call(
        body, grid=(S // R,), name="dn_chunk_bwd",
        in_specs=[smem, smem, qkv_spec, pl.BlockSpec((R, 128), lambda n: (n, BLK_G)), sq, narrow, wide, wide, wide, wide, sq,
                  narrow, pl.BlockSpec(memory_space=pl.ANY)],
        out_specs=[qkv_spec, pl.BlockSpec((R, 256), lambda n: (n, BLK_G_PAD)), pl.BlockSpec((8, 128), lambda n: (0, 0))],
        out_shape=[f(S, 1536), jax.ShapeDtypeStruct(dproj.shape, dproj.dtype), f(8, 128)],
        input_output_aliases={12: 1},
        compiler_params=_params(("arbitrary",)),
    )(a_log, dt_bias, qkv, pg, t_inv, gcs, du, dw, dqg, dkd, da, dsc, dproj)


def _fill_kv(dk, dv, dproj):
    S = dk.shape[0]
    tm = min(512, S)

    def body(dk_ref, dv_ref, _, o_ref):
        o_ref[...] = _bf(jnp.concatenate([dk_ref[...], dv_ref[...]], axis=1))

    tile = pl.BlockSpec((tm, 128), lambda i: (i, 0))
    return pl.pallas_call(
        body, grid=(S // tm,), name="fill_kv",
        in_specs=[tile, tile, pl.BlockSpec(memory_space=pl.ANY)],
        out_specs=pl.BlockSpec((tm, 256), lambda i: (i, BLK_KV)),
        out_shape=jax.ShapeDtypeStruct(dproj.shape, dproj.dtype),
        input_output_aliases={2: 0},
        compiler_params=_params(("parallel",)),
    )(dk, dv, dproj)


def _w_in_to_internal(wt):
    return jnp.concatenate([wt[0:512], wt[2304:2816], wt[768:2304], wt[512:768], wt[2816:2824],
                            jnp.zeros((D_IN_PAD - D_IN, wt.shape[1]), wt.dtype)], axis=0)


def _w_in_from_internal(gt):
    return jnp.concatenate([gt[0:512], gt[2560:2816], gt[1024:2560], gt[512:1024], gt[2816:2824]], axis=0)


def _local_step(x, p, target, wts, first_weights, other_weights, ship_early):
    S = x.shape[0]
    cos, sin = _rope_tables(S)
    sinks, a_log, dt_bias = wts["sinks"].reshape(8), wts["a_log"].reshape(4), wts["dt_bias"].reshape(4)
    gn = wts["dn_norm"].reshape(1, DN_DIM)
    add = lambda acc, res: (acc + res,)

    u = _rmsnorm_fwd(x, wts["norm_mix"], "norm_mix_fwd")
    w_in_t, conv_w = first_weights(u)
    proj, = _mm(u, w_in_t, form="nt", name="in_proj", out_dtypes=[F32], tn=512)
    attn, lse = _attn_fwd(proj, cos, sin, sinks)
    qkv = _dn_prep_fwd(proj, conv_w)
    cw, cu, cqg, ckd, ca, ct, gcs = _dn_chunk_fwd(qkv, proj, a_log, dt_bias)
    o, vnew, sst, dn_out = _dn_scan_fwd(cw, cu, cqg, ckd, ca, gcs, proj, gn)
    w_o, = other_weights(("w_o",), dn_out)
    h1, = _mm([attn, dn_out], w_o, form="nn", name="out_proj", out_dtypes=[F32], tn=512, epi=add, extra=[x])

    def relu2(acc):
        r = jnp.maximum(acc, 0.0)
        return r * r, r

    w_up, = other_weights(("w_up",), h1)
    hid, relu, m = _mm(h1, w_up, form="nn", name="mlp_up", out_dtypes=[BF16, BF16], tn=512, epi=relu2, norm=wts["norm_mlp"])
    w_down, = other_weights(("w_down",), hid)
    h2, = _mm(hid, w_down, form="nn", name="mlp_down", out_dtypes=[F32], tn=512, epi=add, extra=[h1])
    w_pg, w_pp = other_weights(("w_ple_gate", "w_ple_proj"), h2)
    n3, dh2, dgl, dpp, loss, d_norm_final, d_norm_ple = _ple_and_loss(h2, p, target, w_pg, w_pp, wts["norm_ple"],
                                                                     wts["norm_final"].reshape(1, D_MODEL))
    g = {"norm_final": d_norm_final, "norm_ple": d_norm_ple}
    early = {"w_ple_gate": _mm_tn(n3, dgl, name="d_w_ple_gate", tm=512, tn=1024, out_dtype=BF16).reshape(N_DEV, 128, 1024),
             "w_ple_proj": _mm_tn(p, dpp, name="d_w_ple_proj", tm=256, tn=128, out_dtype=BF16, column_shards=True)}
    d_act, = _mm(dh2, w_down, form="nt", name="d_hidden", out_dtypes=[BF16], tn=512,
                 epi=lambda acc, r: (acc * (2.0 * r.astype(F32)),), extra=[relu])
    early["w_down"] = _mm_tn(hid, dh2, name="d_w_down", tm=512, tn=1024, out_dtype=BF16).reshape(N_DEV, 512, 1024)
    early["w_up"] = _mm_tn(m, d_act, name="d_w_up", tm=1024, tn=512, out_dtype=BF16, column_shards=True)
    token = ship_early(early)
    dh1, g["norm_mlp"], dcat = _mm(d_act, w_up, form="nt", name="d_m", out_dtypes=[F32], tn=512, after=token,
                                   norm_bwd=(h1, wts["norm_mlp"], dh2), then_nt=w_o)
    d_w_o = jnp.concatenate([_mm_tn(attn, dh1, name="d_w_o_attn", tm=512, tn=512, out_dtype=BF16),
                             _mm_tn(dn_out, dh1, name="d_w_o_dn", tm=512, tn=512, out_dtype=BF16)], axis=0)
    token = ship_early({"w_o": d_w_o.reshape(N_DEV, 128, 1024)})
    dproj, dk, dv, dsinks = _attn_bwd(proj, cos, sin, sinks + token[0, 0], dcat, attn, lse)
    g["sinks"] = dsinks[:, 0].reshape(1, 8)
    du_, dw_, dqg, dkd, da, dproj, dsc, g["dn_norm"] = _dn_scan_bwd(dcat, o, proj, gn, sst, vnew, cw, cqg, ckd, ca, gcs, dproj)
    dqkv, dproj, gate_acc = _dn_chunk_bwd(qkv, proj, ct, gcs, du_, dw_, dqg, dkd, da, dsc, a_log, dt_bias, dproj)
    g["a_log"], g["dt_bias"] = gate_acc[0:1, 0:4], gate_acc[1:2, 0:4]
    dproj, g["conv_w"] = _dn_prep_bwd(proj, conv_w, dqkv, dproj)
    dproj = _fill_kv(dk, dv, dproj)
    token = ship_early({"w_in": _mm_tn(dproj, u, name="d_w_in", tm=512, tn=1024, out_dtype=BF16)})
    grad_x, g["norm_mix"] = _mm(dproj, w_in_t, form="nn", name="d_u", out_dtypes=[F32], tn=512, after=token,
                                norm_bwd=(x, wts["norm_mix"], dh1))
    return loss, grad_x, g


def _peer(k):
    x, y, c = lax.axis_index("x"), lax.axis_index("y"), lax.axis_index("c")
    px = 1 - x if k & 4 else x
    py = 1 - y if k & 2 else y
    pc = 1 - c if k & 1 else c
    return (px, py, pc), 4 * px + 2 * py + pc


def _exchange(srcs, name, gather):
    n = len(srcs)
    gathers = list(gather) if isinstance(gather, (list, tuple)) else [gather] * n
    shapes = [(N_DEV,) + s.shape if gt else s.shape for s, gt in zip(srcs, gathers)]

    def body(*refs):
        src_refs, out_refs = refs[:n], refs[n:2 * n]
        send_sems, recv_sems, local_sems = refs[2 * n:]
        _, me = _peer(0)
        piece = lambda a, d: src_refs[a] if gathers[a] else src_refs[a].at[d]
        local = [pltpu.make_async_copy(piece(a, me), out_refs[a].at[me], local_sems.at[a]) for a in range(n)]
        for cp in local:
            cp.start()
        copies = []
        for a in range(n):
            for k in range(1, N_DEV):
                dev, idx = _peer(k)
                cp = pltpu.make_async_remote_copy(src_ref=piece(a, idx), dst_ref=out_refs[a].at[me],
                                                  send_sem=send_sems.at[a, k - 1], recv_sem=recv_sems.at[a, k - 1],
                                                  device_id=dev, device_id_type=MESH)
                cp.start()
                copies.append(cp)
        for cp in copies:
            cp.wait_recv()
        for cp in copies:
            cp.wait_send()
        for cp in local:
            cp.wait()

    anywhere = pl.BlockSpec(memory_space=pl.ANY)
    return pl.pallas_call(
        body, name=name, in_specs=[anywhere] * n, out_specs=[anywhere] * n,
        out_shape=[jax.ShapeDtypeStruct(shp, s.dtype) for shp, s in zip(shapes, srcs)],
        scratch_shapes=[pltpu.SemaphoreType.DMA((n, N_DEV - 1)), pltpu.SemaphoreType.DMA((n, N_DEV - 1)),
                        pltpu.SemaphoreType.DMA((n,))],
    )(*srcs)


_HBM = pl.BlockSpec(memory_space=pltpu.HBM)
_SEM = pl.BlockSpec(memory_space=pltpu.SEMAPHORE)
_EFFECT = pltpu.SideEffectType.DATAFLOW_SIDE_EFFECTING


def _split_copies(src_refs, land_refs, send_sems, recv_sems, modes, which=None):
    _, me = _peer(0)
    copies = []
    which = range(len(src_refs)) if which is None else which
    for a, src, land in zip(which, src_refs, land_refs):
        if modes[a] == "columns":
            n_cols = src.shape[1]
            dst = land.at[:, pl.ds(pl.multiple_of(me * n_cols, n_cols), n_cols)]
        else:
            dst = land.at[me]
        for k in range(1, N_DEV):
            dev, idx = _peer(k)
            sem = a * (N_DEV - 1) + k - 1
            copies.append(pltpu.make_async_remote_copy(
                src_ref=src.at[idx] if modes[a] == "pieces" else src, dst_ref=dst, send_sem=send_sems.at[sem],
                recv_sem=recv_sems.at[sem], device_id=dev, device_id_type=MESH))
    return copies


def _exchange_start(srcs, name, modes):
    n = len(srcs)
    modes = [modes] * n if isinstance(modes, str) else list(modes)
    me = 4 * lax.axis_index("x") + 2 * lax.axis_index("y") + lax.axis_index("c")
    lands = []
    for s, mode in zip(srcs, modes):
        if mode == "columns":
            empty = lax.empty((s.shape[0], N_DEV * s.shape[1]), s.dtype)
            lands.append(lax.dynamic_update_slice(empty, s, (0, me * s.shape[1])))
        else:
            own = s if mode == "slots" else lax.dynamic_index_in_dim(s, me, 0, keepdims=False)
            shape = (N_DEV,) + s.shape if mode == "slots" else s.shape
            lands.append(lax.dynamic_update_index_in_dim(lax.empty(shape, s.dtype), own, me, 0))

    def body(*refs):
        src_refs, land_refs = refs[:n], refs[n:2 * n]
        send_sems, recv_sems = refs[2 * n], refs[2 * n + 1]
        for cp in _split_copies(src_refs, land_refs, send_sems, recv_sems, modes):
            cp.start()
        refs[-1][...] = jnp.zeros_like(refs[-1])

    both = list(srcs) + lands
    sems = pltpu.SemaphoreType.DMA((n * (N_DEV - 1),))
    out = pl.pallas_call(
        body, name=name,
        out_shape=(sems, sems, *[pltpu.HBM(t.shape, t.dtype) for t in both], jax.ShapeDtypeStruct((8, 128), F32)),
        in_specs=[_HBM] * (2 * n), out_specs=(_SEM, _SEM, *[_HBM] * (2 * n), pl.BlockSpec(memory_space=pltpu.VMEM)),
        input_output_aliases={i: 2 + i for i in range(2 * n)},
        compiler_params=pltpu.CompilerParams(has_side_effects=_EFFECT),
    )(*[pltpu.with_memory_space_constraint(t, pltpu.HBM) for t in both])
    return (n, modes, out[:-1]), out[-1]


def _exchange_wait(handle, after, name, which=None):
    n_all, modes, (send_sems, recv_sems, *both_all) = handle
    which = list(range(n_all)) if which is None else list(which)
    n = len(which)
    both = [both_all[a] for a in which] + [both_all[n_all + a] for a in which]

    def body(*refs):
        src_refs, land_refs = refs[:n], refs[n:2 * n]
        for cp in _split_copies(src_refs, land_refs, refs[2 * n], refs[2 * n + 1], modes, which):
            cp.wait_send()
            cp.wait_recv()

    out = pl.pallas_call(
        body, name=name, out_shape=tuple(pltpu.HBM(t.shape, t.dtype) for t in both),
        in_specs=[_HBM] * (2 * n) + [_SEM, _SEM, pl.BlockSpec(memory_space=pl.ANY)], out_specs=tuple([_HBM] * (2 * n)),
        input_output_aliases={i: i for i in range(2 * n)},
        compiler_params=pltpu.CompilerParams(has_side_effects=_EFFECT),
    )(*both, send_sems, recv_sems, after)
    return list(out[n:])


def _adam_update(g, w, m, v):
    nm = ADAM_B1 * m + (1.0 - ADAM_B1) * g
    nv = ADAM_B2 * v + (1.0 - ADAM_B2) * (g * g)
    m_hat = nm / (1.0 - ADAM_B1 ** ADAM_STEP)
    v_hat = nv / (1.0 - ADAM_B2 ** ADAM_STEP)
    return -ADAM_LR * (m_hat / (jnp.sqrt(v_hat) + ADAM_EPS) + ADAM_WD * w), nm, nv


def _adamw(parts, w, m, v, name):
    n, R, W = parts.shape
    tm = 128 if R % 128 == 0 else R

    def body(p_ref, w_ref, m_ref, v_ref, g_ref, d_ref, nm_ref, nv_ref):
        g = p_ref[0].astype(F32)
        for s in range(1, n):
            g = g + p_ref[s].astype(F32)
        g_ref[...] = g
        d_ref[...], nm_ref[...], nv_ref[...] = _adam_update(g, w_ref[...], m_ref[...], v_ref[...])

    tile = pl.BlockSpec((tm, W), lambda i: (i, 0))
    return pl.pallas_call(
        body, grid=(R // tm,), name=name,
        in_specs=[pl.BlockSpec((n, tm, W), lambda i: (0, i, 0)), tile, tile, tile],
        out_specs=[tile] * 4, out_shape=[jax.ShapeDtypeStruct((R, W), F32)] * 4,
        compiler_params=_params(("parallel",)),
    )(parts, w, m, v)


_MATRICES = ("w_in", "w_o", "w_up", "w_down", "w_ple_gate", "w_ple_proj")


_OTHERS = ("w_o", "w_up", "w_down", "w_ple_gate", "w_ple_proj")
_OTHER_MODES = {"w_o": "slots", "w_up": "slots", "w_down": "slots", "w_ple_gate": "slots", "w_ple_proj": "columns"}


_VECTORS = ("norm_mix", "norm_mlp", "norm_ple", "norm_final", "a_log", "dt_bias", "sinks", "dn_norm")
_SMALL_ROWS, _LOSS_ROW, _CONV_ROW = 16, 8, 9


def _pack_small(vectors, loss, conv):
    def body(*refs):
        out = refs[-1]
        out[...] = jnp.zeros_like(out)
        for r, ref in enumerate(refs[:len(_VECTORS)]):
            out[r:r + 1, 0:ref.shape[1]] = ref[...]
        out[_LOSS_ROW:_LOSS_ROW + 1, 0:128] = refs[len(_VECTORS)][...]
        out[_CONV_ROW:_CONV_ROW + 6, :] = refs[len(_VECTORS) + 1][...]

    return pl.pallas_call(body, name="pack_small", out_shape=jax.ShapeDtypeStruct((_SMALL_ROWS, 1024), F32))(*vectors, loss, conv)


def _sum_slots(parts):
    def body(p_ref, o_ref):
        acc = p_ref[0]
        for s in range(1, parts.shape[0]):
            acc = acc + p_ref[s]
        o_ref[...] = acc

    return pl.pallas_call(body, name="sum_small", out_shape=jax.ShapeDtypeStruct(parts.shape[1:], parts.dtype))(parts)


def _adamw_vectors(summed, conv_g, wmv):
    names = _VECTORS + ("conv_w",)
    flat = [a for triple in wmv for a in triple]

    def body(*refs):
        sum_ref, conv_ref = refs[0], refs[1]
        ins, outs = refs[2:2 + len(flat)], refs[2 + len(flat):]
        for i in range(len(names)):
            w_ref, m_ref, v_ref = ins[3 * i:3 * i + 3]
            g = conv_ref[...] if i == len(_VECTORS) else sum_ref[i:i + 1, 0:w_ref.shape[1]]
            outs[4 * i][...] = g
            outs[4 * i + 1][...], outs[4 * i + 2][...], outs[4 * i + 3][...] = _adam_update(g, w_ref[...], m_ref[...], v_ref[...])

    out_shape = [jax.ShapeDtypeStruct(t[0].shape, F32) for t in wmv for _ in range(4)]
    res = pl.pallas_call(body, name="adamw_vectors", out_shape=out_shape)(summed, conv_g, *flat)
    return {n: res[4 * i:4 * i + 4] for i, n in enumerate(names)}


_ORDER = ("norm_mix", "w_in", "conv_w", "a_log", "dt_bias", "dn_norm", "sinks", "w_o", "norm_mlp", "w_up", "w_down",
          "norm_ple", "w_ple_gate", "w_ple_proj", "norm_final")


def kernel(x, p, norm_mix, w_in, conv_w, a_log, dt_bias, dn_norm, sinks, w_o, norm_mlp, w_up, w_down, norm_ple, w_ple_gate, w_ple_proj, norm_final, loss_target, m_norm_mix, m_w_in, m_conv_w, m_a_log, m_dt_bias, m_dn_norm, m_sinks, m_w_o, m_norm_mlp, m_w_up, m_w_down, m_norm_ple, m_w_ple_gate, m_w_ple_proj, m_norm_final, v_norm_mix, v_w_in, v_conv_w, v_a_log, v_dt_bias, v_dn_norm, v_sinks, v_w_o, v_norm_mlp, v_w_up, v_w_down, v_norm_ple, v_w_ple_gate, v_w_ple_proj, v_norm_final):
    w = dict(norm_mix=norm_mix, w_in=w_in[0], conv_w=conv_w[0], a_log=a_log, dt_bias=dt_bias, dn_norm=dn_norm, sinks=sinks,
             w_o=w_o[0], norm_mlp=norm_mlp, w_up=w_up[0], w_down=w_down[0], norm_ple=norm_ple, w_ple_gate=w_ple_gate[0],
             w_ple_proj=w_ple_proj[0], norm_final=norm_final)
    m = dict(norm_mix=m_norm_mix, w_in=m_w_in[0], conv_w=m_conv_w[0], a_log=m_a_log, dt_bias=m_dt_bias, dn_norm=m_dn_norm,
             sinks=m_sinks, w_o=m_w_o[0], norm_mlp=m_norm_mlp, w_up=m_w_up[0], w_down=m_w_down[0], norm_ple=m_norm_ple,
             w_ple_gate=m_w_ple_gate[0], w_ple_proj=m_w_ple_proj[0], norm_final=m_norm_final)
    v = dict(norm_mix=v_norm_mix, w_in=v_w_in[0], conv_w=v_conv_w[0], a_log=v_a_log, dt_bias=v_dt_bias, dn_norm=v_dn_norm,
             sinks=v_sinks, w_o=v_w_o[0], norm_mlp=v_norm_mlp, w_up=v_w_up[0], w_down=v_w_down[0], norm_ple=v_norm_ple,
             w_ple_gate=v_w_ple_gate[0], w_ple_proj=v_w_ple_proj[0], norm_final=v_norm_final)
    me = 4 * lax.axis_index("x") + 2 * lax.axis_index("y") + lax.axis_index("c")
    conv_shard = conv_w.shape[2]

    for d in (w, m, v):
        d["w_in"] = d["w_in"].T
    conv_pad = jnp.pad(w["conv_w"], ((0, 8 - DN_CONV), (0, 256 - conv_shard)))
    first, token_first = _exchange_start([_bf(w["w_in"]), conv_pad], "gather_first_start", "slots")
    later = [_bf(w[n]) for n in _OTHERS]
    later[-1] = _bf(w["w_ple_proj"] + token_first[0:1, 0:1])
    others, token_others = _exchange_start(later, "gather_others_start", [_OTHER_MODES[n] for n in _OTHERS])
    vectors = dict(w)
    vectors["norm_mix"] = w["norm_mix"] + token_others[0:1, 0:1]

    def first_weights(after):
        w_in_all, conv_all = _exchange_wait(first, after, "gather_first_wait")
        conv_all = jnp.transpose(conv_all[:, :DN_CONV, :conv_shard], (1, 0, 2)).reshape(DN_CONV, N_DEV * conv_shard)
        return _w_in_to_internal(w_in_all.reshape(D_IN, D_MODEL)), conv_all

    as_taken = {"w_o": lambda t: t.reshape(1024, 1024), "w_up": lambda t: t, "w_down": lambda t: t.reshape(4096, 1024),
                "w_ple_gate": lambda t: t.reshape(1024, 1024), "w_ple_proj": lambda t: t}

    def other_weights(names, after):
        which = [_OTHERS.index(n) for n in names]
        got = _exchange_wait(others, after, "gather_wait_" + names[0], which)
        return [as_taken[n](t) for n, t in zip(names, got)]

    shipped = []

    def ship_early(pieces):
        names = tuple(pieces)
        if names == ("w_in",):
            pieces = {"w_in": _w_in_from_internal(pieces["w_in"]).reshape(N_DEV, D_IN // N_DEV, D_MODEL)}
        handle, token = _exchange_start([pieces[n] for n in names], "scatter_start_" + names[0], "pieces")
        shipped.append((names, handle))
        return token

    loss, grad_x, g = _local_step(x[0], p[0, 0], loss_target[0], vectors, first_weights, other_weights, ship_early)

    row = lambda t: t.reshape(1, t.size)
    small = _pack_small([row(g[n]) for n in _VECTORS], loss, g["conv_w"].reshape(6, 1024))
    small_all, = _exchange([small], "gather_small", gather=True)
    received = {}
    for names, handle in shipped:
        received.update(zip(names, _exchange_wait(handle, grad_x, "scatter_wait_" + names[0])))
    big = {n: _adamw(received[n], w[n], m[n], v[n], "adamw_" + n) for n in _MATRICES}
    summed = _sum_slots(small_all)
    conv_g = lax.dynamic_slice(summed[_CONV_ROW:_CONV_ROW + 6].reshape(DN_CONV, N_DEV * conv_shard), (0, me * conv_shard),
                               (DN_CONV, conv_shard))
    small_out = _adamw_vectors(summed, conv_g, [(row(w[n]), row(m[n]), row(v[n])) for n in _VECTORS]
                               + [(w["conv_w"], m["conv_w"], v["conv_w"])])

    result = [summed[_LOSS_ROW, 0], grad_x[None]]
    for i in range(4):
        for n in _ORDER:
            if n == "w_in":
                result.append(big[n][i].T[None])
            elif n in _MATRICES:
                result.append(big[n][i][None])
            elif n == "conv_w":
                result.append(small_out[n][i][None])
            else:
                result.append(small_out[n][i].reshape(w[n].shape))
    return tuple(result)
```

```python
import jax
import jax.numpy as jnp
import numpy as np
from jax import lax
from jax.experimental import pallas as pl
from jax.experimental.pallas import tpu as pltpu

F32, BF16 = jnp.float32, jnp.bfloat16
EPS = 1e-6
D_MODEL = 1024
N_DEV = 8
ATTN_BLOCK = 128
HEAD_PAIR = 128
DN_HEADS = 4
DN_DIM = 128
DN_CHUNK = 64
DN_CONV = 4
ROPE_THETA = 10000.0
D_IN = 2824
D_IN_PAD = 3072
BLK_Q, BLK_Z = 0, 1
BLK_DN, BLK_K, BLK_V, BLK_G = 8, 20, 21, 22
BLK_KV, BLK_G_PAD = 10, 11
VMEM_LIMIT = 56 * 1024 * 1024
NEG = -1e30
ADAM_LR, ADAM_B1, ADAM_B2, ADAM_EPS, ADAM_WD, ADAM_STEP = 0.001, 0.9, 0.999, 1e-08, 0.01, 10
MESH = pl.DeviceIdType.MESH


def _bf(x):
    return x.astype(BF16)


def _dot(a, b):
    return jnp.dot(a, b, preferred_element_type=F32)


def _dot_nt(a, b):
    return lax.dot_general(a, b, (((1,), (1,)), ((), ())), preferred_element_type=F32)


def _dot_tn(a, b):
    return lax.dot_general(a, b, (((0,), (0,)), ((), ())), preferred_element_type=F32)


def _sigmoid(x):
    return 1.0 / (1.0 + jnp.exp(-x))


def _params(sem):
    return pltpu.CompilerParams(dimension_semantics=sem, vmem_limit_bytes=VMEM_LIMIT)


def _mm(x, w, *, form, name, out_dtypes, tn, epi=None, extra=(), tm=512, w_row_block=0, after=None, norm=None,
        norm_bwd=None, then_nt=None):
    assert norm is None or norm_bwd is None
    xs = list(x) if isinstance(x, (list, tuple)) else [x]
    nx = len(xs)
    S, K = xs[0].shape
    shards = w.ndim == 3
    N = (w.shape[2] * N_DEV if shards else w.shape[1]) if form == "nn" else w.shape[-2]
    assert not (shards and form == "nn" and tn != w.shape[2]) and (nx == 1 or (form == "nn" and not shards and norm is None))
    r0 = w_row_block * K
    tm = min(tm, S)
    n_extra, n_out = len(extra), len(out_dtypes)
    tile = lambda width: pl.BlockSpec((tm, width), lambda i: (i, 0))
    whole = lambda a: pl.BlockSpec(a.shape, lambda i, nd=a.ndim: (0,) * nd)
    ins, in_specs = [*xs, w, *extra], [tile(K)] * nx + [whole(w)] + [tile(N)] * n_extra
    if norm is not None:
        ins, in_specs = ins + [norm], in_specs + [whole(norm)]
    if norm_bwd is not None:
        ins, in_specs = ins + list(norm_bwd), in_specs + [tile(N), whole(norm_bwd[1]), tile(N)]
    if then_nt is not None:
        ins, in_specs = ins + [then_nt], in_specs + [whole(then_nt)]
    if after is not None:
        ins, in_specs = ins + [after], in_specs + [whole(after)]
    out_shape = [jax.ShapeDtypeStruct((S, N), dt) for dt in out_dtypes]
    out_specs = [tile(N)] * n_out
    if norm is not None:
        out_shape, out_specs = out_shape + [jax.ShapeDtypeStruct((S, K), BF16)], out_specs + [tile(K)]
    if norm_bwd is not None:
        out_shape, out_specs = out_shape + [jax.ShapeDtypeStruct((1, N), F32)], out_specs + [pl.BlockSpec((1, N), lambda i: (0, 0))]
    if then_nt is not None:
        out_shape, out_specs = out_shape + [jax.ShapeDtypeStruct((S, then_nt.shape[0]), F32)], out_specs + [tile(then_nt.shape[0])]

    def product(xb, w_ref, cols, c):
        if form == "nn" and nx > 1:
            return sum(_dot(part, w_ref[r0 + p * K:r0 + (p + 1) * K, cols]) for p, part in enumerate(xb))
        if form == "nn":
            return _dot(xb, w_ref[c] if shards else w_ref[r0:r0 + K, cols])
        if not shards:
            return _dot_nt(xb, w_ref[cols, :])
        ks = w.shape[2]
        acc = _dot_nt(xb[:, 0:ks], w_ref[0, cols, :])
        for s in range(1, N_DEV):
            acc = acc + _dot_nt(xb[:, s * ks:(s + 1) * ks], w_ref[s, cols, :])
        return acc

    def body(*refs):
        x_ref, w_ref = refs[0], refs[nx]
        extra_refs = refs[nx + 1:nx + 1 + n_extra]
        at = nx + 1 + n_extra
        if norm is not None:
            gain_ref, at = refs[at], at + 1
        if norm_bwd is not None:
            (y_ref, ygain_ref, dres_ref), at = refs[at:at + 3], at + 3
        if then_nt is not None:
            w2_ref, at = refs[at], at + 1
        outs = refs[len(ins):]
        if norm is not None:
            _, xh = _rms_stats(x_ref[...])
            xb = _bf(xh * gain_ref[...])
            outs[n_out][...] = xb
        else:
            xb = _bf(x_ref[...]) if nx == 1 else [_bf(r[...]) for r in refs[:nx]]
        for c in range(N // tn):
            cols = slice(c * tn, (c + 1) * tn)
            acc = product(xb, w_ref, cols, c)
            res = epi(acc, *[r[:, cols] for r in extra_refs]) if epi else (acc,)
            for o, r in zip(outs[:n_out], res):
                o[:, cols] = r.astype(o.dtype)
        if norm_bwd is not None:
            dx, dg = _rms_bwd_tile(y_ref[...], ygain_ref[...], outs[0][...])
            outs[0][...] = dres_ref[...] + dx
            dg_ref = outs[n_out]

            @pl.when(pl.program_id(0) == 0)
            def _():
                dg_ref[...] = jnp.zeros_like(dg_ref)

            dg_ref[...] += dg
        if then_nt is not None:
            yb = _bf(outs[0][...])
            for c in range(then_nt.shape[0] // tn):
                cols = slice(c * tn, (c + 1) * tn)
                outs[-1][:, cols] = _dot_nt(yb, w2_ref[cols, :])

    return pl.pallas_call(
        body, grid=(S // tm,), name=name, in_specs=in_specs, out_specs=out_specs, out_shape=out_shape,
        compiler_params=_params(("arbitrary",) if norm_bwd is not None else ("parallel",)),
    )(*ins)


def _mm_tn(x, dy, *, name, tm, tn, out_dtype=F32, column_shards=False):
    S, K = x.shape
    N = dy.shape[1]

    def body(x_ref, dy_ref, o_ref):
        o_ref[...] = _dot_tn(_bf(x_ref[...]), _bf(dy_ref[...])).astype(out_dtype)

    if column_shards:
        out_spec = pl.BlockSpec((None, tm, tn), lambda i, j: (j, i, 0))
        out_shape = jax.ShapeDtypeStruct((N // tn, K, tn), out_dtype)
    else:
        out_spec = pl.BlockSpec((tm, tn), lambda i, j: (i, j))
        out_shape = jax.ShapeDtypeStruct((K, N), out_dtype)
    return pl.pallas_call(
        body, grid=(K // tm, N // tn), name=name,
        in_specs=[pl.BlockSpec((S, tm), lambda i, j: (0, i)), pl.BlockSpec((S, tn), lambda i, j: (0, j))],
        out_specs=out_spec, out_shape=out_shape,
        compiler_params=_params(("parallel", "parallel")),
    )(x, dy)


def _rowwise(body, *, tiled, full, out_tiled, out_acc, name, tm=512, smem=()):
    S = tiled[0].shape[0]
    tm = min(tm, S)
    n_in = len(smem) + len(tiled) + len(full)

    def kern(*refs):
        @pl.when(pl.program_id(0) == 0)
        def _():
            for r in refs[n_in + len(out_tiled):]:
                r[...] = jnp.zeros_like(r)
        body(*refs)

    in_specs = [pl.BlockSpec(memory_space=pltpu.SMEM) for _ in smem]
    in_specs += [pl.BlockSpec((tm, a.shape[1]), lambda i: (i, 0)) for a in tiled]
    in_specs += [pl.BlockSpec(a.shape, lambda i, nd=a.ndim: (0,) * nd) for a in full]
    out_specs = [pl.BlockSpec((tm, w), lambda i: (i, 0)) for w, _ in out_tiled]
    out_specs += [pl.BlockSpec(shp, lambda i, nd=len(shp): (0,) * nd) for shp, _ in out_acc]
    out_shape = [jax.ShapeDtypeStruct((S, w), dt) for w, dt in out_tiled]
    out_shape += [jax.ShapeDtypeStruct(shp, dt) for shp, dt in out_acc]
    return pl.pallas_call(
        kern, grid=(S // tm,), name=name, in_specs=in_specs, out_specs=out_specs, out_shape=out_shape,
        compiler_params=_params(("arbitrary",)),
    )(*smem, *tiled, *full)


def _rms_stats(x):
    r = lax.rsqrt(jnp.mean(x * x, axis=-1, keepdims=True) + EPS)
    return r, x * r


def _rmsnorm_fwd(x, g, name):
    def body(x_ref, g_ref, o_ref):
        _, xh = _rms_stats(x_ref[...])
        o_ref[...] = _bf(xh * g_ref[...])

    return _rowwise(body, tiled=[x], full=[g], out_tiled=[(x.shape[1], BF16)], out_acc=[], name=name)[0]


def _rms_bwd_tile(x, g, dxn):
    r, xh = _rms_stats(x)
    dg = jnp.sum(dxn * xh, axis=0, keepdims=True)
    dn = dxn * g
    dx = r * (dn - xh * jnp.mean(dn * xh, axis=-1, keepdims=True))
    return dx, dg


def _ple_and_loss(h2, p, target, w_pg, w_pp, g_ple, g_final):
    S, n = h2.shape
    tm = min(512, S)
    tn = 512

    def body(h2_ref, p_ref, t_ref, wpg_ref, wpp_ref, gple_ref, gfin_ref,
             n3_ref, dh_ref, dgl_ref, dpp_ref, loss_ref, dg_ref, dgple_ref, pp, gate, h3):
        @pl.when(pl.program_id(0) == 0)
        def _():
            loss_ref[...] = jnp.zeros_like(loss_ref)
            dg_ref[...] = jnp.zeros_like(dg_ref)
            dgple_ref[...] = jnp.zeros_like(dgple_ref)

        x = h2_ref[...]
        _, xh = _rms_stats(x)
        n3 = _bf(xh * gple_ref[...])
        n3_ref[...] = n3
        pb = _bf(p_ref[...])
        for c in range(n // tn):
            cols = slice(c * tn, (c + 1) * tn)
            pp[:, cols] = _dot(pb, wpp_ref[:, cols])
            gt = _sigmoid(_dot(n3, wpg_ref[:, cols]))
            gate[:, cols] = gt
            h3[:, cols] = x[:, cols] + gt * pp[:, cols]
        y = h3[...]
        _, yh = _rms_stats(y)
        e = yh * gfin_ref[...] - t_ref[...]
        per_tok = jnp.mean(e * e, axis=-1, keepdims=True)
        loss_ref[...] += 0.5 * jnp.sum(per_tok, axis=0, keepdims=True)
        dh, dg = _rms_bwd_tile(y, gfin_ref[...], e * (1.0 / n))
        dg_ref[...] += dg
        gt = gate[...]
        dgl = _bf(dh * pp[...] * gt * (1.0 - gt))
        dgl_ref[...] = dgl
        dpp_ref[...] = _bf(dh * gt)
        for c in range(n // tn):
            cols = slice(c * tn, (c + 1) * tn)
            h3[:, cols] = _dot_nt(dgl, wpg_ref[cols, :])
        dx, dgp = _rms_bwd_tile(x, gple_ref[...], h3[...])
        dh_ref[...] = dh + dx
        dgple_ref[...] += dgp

    tile = lambda width: pl.BlockSpec((tm, width), lambda i: (i, 0))
    whole = lambda a: pl.BlockSpec(a.shape, lambda i, nd=a.ndim: (0,) * nd)
    return pl.pallas_call(
        body, grid=(S // tm,), name="ple_and_loss",
        in_specs=[tile(n), tile(p.shape[1]), tile(n), whole(w_pg), whole(w_pp), whole(g_ple), whole(g_final)],
        out_specs=[tile(n), tile(n), tile(n), tile(n), pl.BlockSpec((1, 128), lambda i: (0, 0)),
                   pl.BlockSpec((1, n), lambda i: (0, 0)), pl.BlockSpec((1, n), lambda i: (0, 0))],
        out_shape=[jax.ShapeDtypeStruct((S, n), BF16), jax.ShapeDtypeStruct((S, n), F32), jax.ShapeDtypeStruct((S, n), BF16),
                   jax.ShapeDtypeStruct((S, n), BF16), jax.ShapeDtypeStruct((1, 128), F32), jax.ShapeDtypeStruct((1, n), F32),
                   jax.ShapeDtypeStruct((1, n), F32)],
        scratch_shapes=[pltpu.VMEM((tm, n), F32)] * 3,
        compiler_params=_params(("arbitrary",)),
    )(h2, p, target, w_pg, w_pp, g_ple, g_final)


def _rope_tables(S):
    half = 32
    inv = (1.0 / (np.float32(ROPE_THETA) ** (np.arange(half, dtype=np.float32) * np.float32(2.0 / 64)))).astype(np.float32)
    ang = np.arange(S).astype(np.float32)[:, None] * inv[None, :]
    cos, sin = np.cos(ang), np.sin(ang)
    return jnp.asarray(np.tile(cos, (1, 4))), jnp.asarray(np.concatenate([-sin, sin, -sin, sin], axis=1))


def _attn_common(i, kc, kp, vc, vp, cc, sc, cp, sp):
    lane = lax.broadcasted_iota(jnp.int32, (1, HEAD_PAIR), 1)
    lane_lo = jnp.bitwise_and(lane, 63) < 32
    slot = [lane < 64, lane >= 64]

    def swap_halves(t):
        return jnp.where(lane_lo, pltpu.roll(t, 96, 1), pltpu.roll(t, 32, 1))

    def rope(t, cos, sin):
        return t * cos + swap_halves(t) * sin

    def unrope(d, cos, sin):
        return d * cos + swap_halves(d * sin)

    k2 = jnp.concatenate([rope(kp, cp, sp), rope(kc, cc, sc)], axis=0)
    v2 = jnp.concatenate([vp, vc], axis=0)
    r = lax.broadcasted_iota(jnp.int32, (ATTN_BLOCK, 2 * ATTN_BLOCK), 0)
    c = lax.broadcasted_iota(jnp.int32, (ATTN_BLOCK, 2 * ATTN_BLOCK), 1)
    valid = (c > r) & (c <= r + ATTN_BLOCK) & jnp.logical_or(c >= ATTN_BLOCK, i > 0)
    ks, vs = {}, {}
    for j in range(2):
        kn = jnp.where(slot[j], k2, 0.0)
        vn = jnp.where(slot[j], v2, 0.0)
        for s in range(2):
            ks[j, s] = _bf(kn if s == j else pltpu.roll(kn, 64, 1))
            vs[j, s] = _bf(vn if s == j else pltpu.roll(vn, 64, 1))
    return slot, rope, unrope, valid, ks, vs


def _attn_probs(scores, valid, sink):
    s = jnp.where(valid, scores * 0.125, NEG)
    m = jnp.maximum(jnp.max(s, axis=1, keepdims=True), sink)
    e = jnp.exp(s - m)
    z = jnp.sum(e, axis=1, keepdims=True) + jnp.exp(sink - m)
    return e * (1.0 / z), m + jnp.log(z)


def _attn_specs(S):
    nb = S // ATTN_BLOCK
    prev = lambda i: jnp.maximum(i - 1, 0)
    blk = lambda w, col, row=(lambda i: i): pl.BlockSpec((ATTN_BLOCK, w), lambda i: (row(i), col))
    in_specs = [pl.BlockSpec(memory_space=pltpu.SMEM),
                blk(512, BLK_Q), blk(128, BLK_K), blk(128, BLK_K, prev), blk(128, BLK_V), blk(128, BLK_V, prev),
                blk(128, 0), blk(128, 0), blk(128, 0, prev), blk(128, 0, prev)]
    return nb, in_specs


def _attn_fwd(pa, cos, sin, sinks):
    S = pa.shape[0]
    nb, in_specs = _attn_specs(S)

    def body(sinks_ref, q_ref, kc_ref, kp_ref, vc_ref, vp_ref, cc_ref, sc_ref, cp_ref, sp_ref, o_ref, lse_ref):
        i = pl.program_id(0)
        lane = lax.broadcasted_iota(jnp.int32, (1, HEAD_PAIR), 1)
        cc, sc = cc_ref[...], sc_ref[...]
        _, rope, _, valid, ks, vs = _attn_common(i, kc_ref[...], kp_ref[...], vc_ref[...], vp_ref[...],
                                                 cc, sc, cp_ref[...], sp_ref[...])
        pair_cols = [slice(HEAD_PAIR * pair, HEAD_PAIR * (pair + 1)) for pair in range(4)]
        qps = [_bf(rope(q_ref[:, cols], cc, sc)) for cols in pair_cols]
        outs, lses = {}, {}

        def head_program(h):
            pair, s = divmod(h, 2)
            j = h // 4
            scores = _dot_nt(qps[pair], ks[j, s])
            yield
            p, lse = _attn_probs(scores, valid, sinks_ref[h])
            outs[h] = _dot(_bf(p), vs[j, s])
            lses[h] = jnp.where(lane == h, lse, 0.0)

        _interleave(head_program(h) for h in range(8))
        for pair, cols in enumerate(pair_cols):
            o_ref[:, cols] = outs[2 * pair] + outs[2 * pair + 1]
        lse_ref[...] = sum((lses[h] for h in range(1, 8)), lses[0])

    return pl.pallas_call(
        body, grid=(nb,), name="attn_fwd", in_specs=in_specs,
        out_specs=[pl.BlockSpec((ATTN_BLOCK, 512), lambda i: (i, 0)), pl.BlockSpec((ATTN_BLOCK, 128), lambda i: (i, 0))],
        out_shape=[jax.ShapeDtypeStruct((S, 512), F32), jax.ShapeDtypeStruct((S, 128), F32)],
        compiler_params=_params(("parallel",)),
    )(sinks, pa, pa, pa, pa, pa, cos, sin, cos, sin)


def _attn_bwd(pa, cos, sin, sinks, dcat, attn, lse):
    S = pa.shape[0]
    nb, in_specs = _attn_specs(S)
    in_specs = in_specs + [pl.BlockSpec((ATTN_BLOCK, 512), lambda i: (i, 0))] * 2 + [pl.BlockSpec((ATTN_BLOCK, 128), lambda i: (i, 0))]

    def body(sinks_ref, q_ref, kc_ref, kp_ref, vc_ref, vp_ref, cc_ref, sc_ref, cp_ref, sp_ref, do_ref, o_ref, lse_ref,
             dq_ref, dk_ref, dv_ref, dsink_ref):
        i = pl.program_id(0)

        @pl.when(i == 0)
        def _():
            dk_ref[...] = jnp.zeros_like(dk_ref)
            dv_ref[...] = jnp.zeros_like(dv_ref)
            dsink_ref[...] = jnp.zeros_like(dsink_ref)

        cc, sc, cp, sp = cc_ref[...], sc_ref[...], cp_ref[...], sp_ref[...]
        slot, rope, unrope, valid, ks, vs = _attn_common(i, kc_ref[...], kp_ref[...], vc_ref[...], vp_ref[...], cc, sc, cp, sp)
        pair_cols = [slice(HEAD_PAIR * pair, HEAD_PAIR * (pair + 1)) for pair in range(4)]
        qps = [_bf(rope(q_ref[:, cols], cc, sc)) for cols in pair_cols]
        dobs = [_bf(do_ref[:, cols]) for cols in pair_cols]
        do_o = [do_ref[:, cols] * o_ref[:, cols] for cols in pair_cols]
        dqs, dks, dvs = {}, {}, {}

        def head_program(h):
            pair, s = divmod(h, 2)
            j = h // 4
            qp, dob = qps[pair], dobs[pair]
            scores = _dot_nt(qp, ks[j, s])
            dp = _dot_nt(dob, vs[j, s])
            yield
            lse_h = lse_ref[:, h:h + 1]
            p = jnp.exp(jnp.where(valid, scores * 0.125, NEG) - lse_h)
            dr = jnp.sum(jnp.where(slot[s], do_o[pair], 0.0), axis=1, keepdims=True)
            ds = _bf(p * (dp - dr) * 0.125)
            dsink_ref[h:h + 1, :] += -jnp.sum(jnp.exp(sinks_ref[h] - lse_h) * dr, axis=0, keepdims=True)
            dqs[h] = _dot(ds, ks[j, s])
            dk_h = _dot_tn(ds, qp)
            dv_h = _dot_tn(_bf(p), dob)
            yield
            dk_h, dv_h = jnp.where(slot[s], dk_h, 0.0), jnp.where(slot[s], dv_h, 0.0)
            if s != j:
                dk_h, dv_h = pltpu.roll(dk_h, 64, 1), pltpu.roll(dv_h, 64, 1)
            dks[h], dvs[h] = dk_h, dv_h

        _interleave(head_program(h) for h in range(8))
        dk2 = sum((dks[h] for h in range(1, 8)), dks[0])
        dv2 = sum((dvs[h] for h in range(1, 8)), dvs[0])
        for pair, cols in enumerate(pair_cols):
            dq_ref[:, cols] = _bf(unrope(dqs[2 * pair] + dqs[2 * pair + 1], cc, sc))
        cur = pl.ds(pl.multiple_of(i * ATTN_BLOCK, ATTN_BLOCK), ATTN_BLOCK)
        dk_ref[cur, :] += unrope(dk2[ATTN_BLOCK:], cc, sc)
        dv_ref[cur, :] += dv2[ATTN_BLOCK:]

        @pl.when(i > 0)
        def _():
            prv = pl.ds(pl.multiple_of((i - 1) * ATTN_BLOCK, ATTN_BLOCK), ATTN_BLOCK)
            dk_ref[prv, :] += unrope(dk2[:ATTN_BLOCK], cp, sp)
            dv_ref[prv, :] += dv2[:ATTN_BLOCK]

    whole = lambda w: pl.BlockSpec((S, w), lambda i: (0, 0))
    return pl.pallas_call(
        body, grid=(nb,), name="attn_bwd", in_specs=in_specs,
        out_specs=[pl.BlockSpec((ATTN_BLOCK, 512), lambda i: (i, BLK_Q)), whole(128), whole(128),
                   pl.BlockSpec((8, 128), lambda i: (0, 0))],
        out_shape=[jax.ShapeDtypeStruct((S, D_IN_PAD), BF16), jax.ShapeDtypeStruct((S, 128), F32),
                   jax.ShapeDtypeStruct((S, 128), F32), jax.ShapeDtypeStruct((8, 128), F32)],
        compiler_params=_params(("arbitrary",)),
    )(sinks, pa, pa, pa, pa, pa, cos, sin, cos, sin, dcat, attn, lse)


CONV_ROWS = 512
CONV_PAD = 8


def _conv_silu(scr, w, r0):
    y = w[3:4, :] * scr[pl.ds(CONV_PAD + r0, CONV_ROWS), :]
    for j in range(DN_CONV - 1):
        y = y + w[j:j + 1, :] * scr[pl.ds(CONV_PAD + r0 - 3 + j, CONV_ROWS), :]
    return y


def _dn_prep_fwd(pd, conv_w):
    S = pd.shape[0]
    assert S % CONV_ROWS == 0

    def body(x_ref, w_ref, o_ref, scr):
        b = pl.program_id(0)
        scr[0:CONV_PAD, :] = jnp.zeros((CONV_PAD, DN_DIM), F32)
        scr[pl.ds(CONV_PAD, S), :] = x_ref[...]
        w = w_ref[...]
        q_scale = jnp.where(b < DN_HEADS, DN_DIM ** -0.5, 1.0)
        for r0 in range(0, S, CONV_ROWS):
            y = _conv_silu(scr, w, r0)
            a = y * _sigmoid(y)
            rs = lax.rsqrt(jnp.sum(a * a, axis=1, keepdims=True) + EPS)
            o_ref[pl.ds(r0, CONV_ROWS), :] = a * jnp.where(b < 2 * DN_HEADS, rs * q_scale, 1.0)

    col = pl.BlockSpec((S, DN_DIM), lambda b: (0, b))
    return pl.pallas_call(
        body, grid=(3 * DN_HEADS,), name="dn_prep_fwd",
        in_specs=[pl.BlockSpec((S, DN_DIM), lambda b: (0, BLK_DN + b)), pl.BlockSpec((DN_CONV, DN_DIM), lambda b: (0, b))],
        out_specs=col,
        out_shape=jax.ShapeDtypeStruct((S, 3 * DN_HEADS * DN_DIM), F32),
        scratch_shapes=[pltpu.VMEM((S + CONV_PAD, DN_DIM), F32)],
        compiler_params=_params(("parallel",)),
    )(pd, conv_w)


def _dn_prep_bwd(pd, conv_w, dqkv, dproj):
    S = pd.shape[0]

    def body(x_ref, w_ref, d_ref, _, dx_ref, dw_ref, scr, dscr):
        b = pl.program_id(0)
        scr[0:CONV_PAD, :] = jnp.zeros((CONV_PAD, DN_DIM), F32)
        scr[pl.ds(CONV_PAD, S), :] = x_ref[...]
        dscr[pl.ds(S, CONV_PAD), :] = jnp.zeros((CONV_PAD, DN_DIM), F32)
        w = w_ref[...]
        q_scale = jnp.where(b < DN_HEADS, DN_DIM ** -0.5, 1.0)
        is_qk = b < 2 * DN_HEADS
        dw = [jnp.zeros((1, DN_DIM), F32) for _ in range(DN_CONV)]
        for r0 in range(0, S, CONV_ROWS):
            y = _conv_silu(scr, w, r0)
            sg = _sigmoid(y)
            a = y * sg
            dout = d_ref[pl.ds(r0, CONV_ROWS), :]
            rs = lax.rsqrt(jnp.sum(a * a, axis=1, keepdims=True) + EPS)
            da_qk = q_scale * rs * (dout - a * (rs * rs) * jnp.sum(dout * a, axis=1, keepdims=True))
            dy = jnp.where(is_qk, da_qk, dout) * (sg * (1.0 + y * (1.0 - sg)))
            dscr[pl.ds(r0, CONV_ROWS), :] = dy
            for j in range(DN_CONV):
                dw[j] = dw[j] + jnp.sum(dy * scr[pl.ds(CONV_PAD + r0 - 3 + j, CONV_ROWS), :], axis=0, keepdims=True)
        for j in range(DN_CONV):
            dw_ref[j:j + 1, :] = dw[j]
        for r0 in range(0, S, CONV_ROWS):
            dx = w[3:4, :] * dscr[pl.ds(r0, CONV_ROWS), :]
            for j in range(DN_CONV - 1):
                dx = dx + w[j:j + 1, :] * dscr[pl.ds(r0 + 3 - j, CONV_ROWS), :]
            dx_ref[pl.ds(r0, CONV_ROWS), :] = _bf(dx)

    col = pl.BlockSpec((S, DN_DIM), lambda b: (0, b))
    proj_col = pl.BlockSpec((S, DN_DIM), lambda b: (0, BLK_DN + b))
    wcol = pl.BlockSpec((DN_CONV, DN_DIM), lambda b: (0, b))
    return pl.pallas_call(
        body, grid=(3 * DN_HEADS,), name="dn_prep_bwd",
        in_specs=[proj_col, wcol, col, pl.BlockSpec(memory_space=pl.ANY)], out_specs=[proj_col, wcol],
        out_shape=[jax.ShapeDtypeStruct(dproj.shape, dproj.dtype), jax.ShapeDtypeStruct((DN_CONV, 3 * DN_HEADS * DN_DIM), F32)],
        scratch_shapes=[pltpu.VMEM((S + CONV_PAD, DN_DIM), F32), pltpu.VMEM((S + CONV_PAD, DN_DIM), F32)],
        input_output_aliases={3: 0},
        compiler_params=_params(("parallel",)),
    )(pd, conv_w, dqkv, dproj)


CPAD = 128
CHUNKS_LOCAL = 4
CHUNKS_SCAN = 4


def _chunk_masks():
    ii = lax.broadcasted_iota(jnp.int32, (DN_CHUNK, CPAD), 0)
    jj = lax.broadcasted_iota(jnp.int32, (DN_CHUNK, CPAD), 1)
    return ii, jj


def _rows_pad(a):
    return jnp.concatenate([a, jnp.zeros_like(a)], axis=0)


def _hi_lo(a):
    hi = _bf(a)
    return hi, _bf(a - hi.astype(F32))


def _double_step(t, p):
    C = DN_CHUNK
    th, tl = _hi_lo(t)
    ph, pl_ = _hi_lo(p)
    r1 = _dot(jnp.concatenate([th, tl, ph, pl_], axis=0), _rows_pad(ph))
    r2 = _dot(jnp.concatenate([th, ph], axis=0), _rows_pad(pl_))
    return t + (r1[:C] + r1[C:2 * C] + r2[:C]), r1[2 * C:3 * C] + r1[3 * C:] + r2[C:]


def _dot3_nt(a, b):
    C = DN_CHUNK
    ah, al = _hi_lo(a)
    bh, bl = _hi_lo(b)
    r1 = _dot_nt(jnp.concatenate([ah, al], axis=0), _rows_pad(bh))
    return r1[:C] + r1[C:] + _dot_nt(ah, _rows_pad(bl))


def _dot3_tn(a, b):
    C = DN_CHUNK
    ah, al = _hi_lo(a)
    bh, bl = _hi_lo(b)
    return _dot_tn(jnp.concatenate([ah, al, ah], axis=0), jnp.concatenate([bh, bh, bl], axis=0))[:C]


def _interleave(programs):
    programs = list(programs)
    while programs:
        alive = []
        for prog in programs:
            try:
                next(prog)
                alive.append(prog)
            except StopIteration:
                pass
        programs = alive


def _col_to_row(col, ii, jj):
    return jnp.sum(jnp.where(ii == jj, col, 0.0), axis=0, keepdims=True)


def _row_to_col(row, ii, jj):
    return jnp.sum(jnp.where(ii == jj, row, 0.0), axis=1, keepdims=True)


def _decay(gc_col, ii, jj):
    diff = gc_col - _col_to_row(gc_col, ii, jj)
    return jnp.where(jj <= ii, jnp.exp(jnp.where(jj <= ii, diff, 0.0)), 0.0)


def _softplus(x):
    return jnp.maximum(x, 0.0) + jnp.log(1.0 + jnp.exp(-jnp.abs(x)))


def _head(h):
    return slice(DN_DIM * h, DN_DIM * (h + 1))


def _dn_chunk_fwd(qkv, pg, a_log, dt_bias):
    S = qkv.shape[0]
    C = DN_CHUNK
    G = CHUNKS_LOCAL
    R = G * C
    steps = S // R

    def body(alog_ref, dtb_ref, qkv_ref, pg_ref, w_ref, u_ref, qg_ref, kd_ref, a_ref, t_ref, gcs_ref):
        ii, jj = _chunk_masks()
        lane = lax.broadcasted_iota(jnp.int32, (1, 128), 1)
        eye = (ii == jj).astype(F32)
        gcs_parts = [[] for _ in range(G)]

        def head_program(chunk, h):
            rows = slice(chunk * C, (chunk + 1) * C)
            q, k, v = qkv_ref[rows, _head(h)], qkv_ref[rows, _head(DN_HEADS + h)], qkv_ref[rows, _head(2 * DN_HEADS + h)]
            beta = _sigmoid(pg_ref[rows, h:h + 1])
            g_col = -jnp.exp(alog_ref[h]) * _softplus(pg_ref[rows, DN_HEADS + h:DN_HEADS + h + 1] + dtb_ref[h])
            g_row = _col_to_row(g_col, ii, jj)
            gc_col = jnp.sum(jnp.where(jj <= ii, g_row, 0.0), axis=1, keepdims=True)
            dec = _decay(gc_col, ii, jj)
            eg = jnp.exp(gc_col)
            kb, vb = k * beta, v * beta
            k_rows = _rows_pad(_bf(k))
            kk = _dot_nt(_bf(kb), k_rows)
            qk = _dot_nt(_bf(q), k_rows)
            yield
            t, pw = eye, -jnp.where(jj < ii, kk * dec, 0.0)
            for _ in range(6):
                t, pw = _double_step(t, pw)
                yield
            tb = _bf(t)
            u_ref[rows, _head(h)] = _dot(tb, _rows_pad(_bf(vb)))
            w_ref[rows, _head(h)] = _bf(_dot(tb, _rows_pad(_bf(kb * eg))))
            a_ref[h, rows] = _bf(qk * dec)
            t_ref[h, rows] = t
            qg_ref[rows, _head(h)] = _bf(q * eg)
            kd_ref[rows, _head(h)] = _bf(k * jnp.exp(gc_col[C - 1:C, :] - gc_col))
            gcs_parts[chunk].append(jnp.where(lane == h, gc_col, 0.0) + jnp.where(lane == DN_HEADS + h, beta, 0.0)
                                    + jnp.where(lane == 2 * DN_HEADS + h, g_col, 0.0))

        _interleave(head_program(chunk, h) for chunk in range(G) for h in range(DN_HEADS))
        for chunk in range(G):
            gcs_ref[chunk * C:(chunk + 1) * C, :] = sum(gcs_parts[chunk][1:], gcs_parts[chunk][0])

    smem = pl.BlockSpec(memory_space=pltpu.SMEM)
    wide = pl.BlockSpec((R, 512), lambda n: (n, 0))
    sq = pl.BlockSpec((DN_HEADS, R, CPAD), lambda n: (0, n, 0))
    narrow = pl.BlockSpec((R, 128), lambda n: (n, 0))
    f = lambda *shp: jax.ShapeDtypeStruct(shp, F32)
    b = lambda *shp: jax.ShapeDtypeStruct(shp, BF16)
    return pl.pallas_call(
        body, grid=(steps,), name="dn_chunk_fwd",
        in_specs=[smem, smem, pl.BlockSpec((R, 1536), lambda n: (n, 0)), pl.BlockSpec((R, 128), lambda n: (n, BLK_G))],
        out_specs=[wide, wide, wide, wide, sq, sq, narrow],
        out_shape=[b(S, 512), f(S, 512), b(S, 512), b(S, 512), b(DN_HEADS, S, CPAD), f(DN_HEADS, S, CPAD), f(S, 128)],
        compiler_params=_params(("parallel",)),
    )(a_log, dt_bias, qkv, pg)


def _gated_norm(o, z, gn):
    r, oh = _rms_stats(o)
    return oh * gn * (z * _sigmoid(z))


def _dn_scan_fwd(w, u, qg, kd, a, gcs, pz, gn):
    S = w.shape[0]
    C = DN_CHUNK
    nc = S // C
    G = CHUNKS_SCAN
    R = G * C

    def body(w_ref, u_ref, qg_ref, kd_ref, a_ref, gcs_ref, z_ref, gn_ref, o_ref, vn_ref, sst_ref, out_ref, state):
        @pl.when(pl.program_id(0) == 0)
        def _():
            state[...] = jnp.zeros_like(state)

        def head_program(chunk, h):
            hs = _head(h)
            rows = slice(chunk * C, (chunk + 1) * C)
            s_in = state[h]
            sst_ref[chunk, h] = s_in
            sb = _bf(s_in)
            w_s = _dot(w_ref[rows, hs], sb)
            q_s = _dot(qg_ref[rows, hs], sb)
            yield
            vn = u_ref[rows, hs] - w_s
            vnb = _bf(vn)
            o = q_s + _dot(a_ref[h, rows], _rows_pad(vnb))
            k_v = _dot_tn(kd_ref[rows, hs], vnb)
            yield
            state[h] = s_in * jnp.exp(gcs_ref[(chunk + 1) * C - 1:(chunk + 1) * C, h:h + 1]) + k_v
            o_ref[rows, hs] = o
            vn_ref[rows, hs] = vn
            out_ref[rows, hs] = _gated_norm(o, z_ref[rows, hs], gn_ref[...])

        for chunk in range(G):
            _interleave(head_program(chunk, h) for h in range(DN_HEADS))

    wide = pl.BlockSpec((R, 512), lambda n: (n, 0))
    f = lambda *shp: jax.ShapeDtypeStruct(shp, F32)
    return pl.pallas_call(
        body, grid=(nc // G,), name="dn_scan_fwd",
        in_specs=[wide, wide, wide, wide, pl.BlockSpec((DN_HEADS, R, CPAD), lambda n: (0, n, 0)),
                  pl.BlockSpec((R, 128), lambda n: (n, 0)), pl.BlockSpec((R, 512), lambda n: (n, BLK_Z)),
                  pl.BlockSpec((1, DN_DIM), lambda n: (0, 0))],
        out_specs=[wide, wide, pl.BlockSpec((G, DN_HEADS, DN_DIM, DN_DIM), lambda n: (n, 0, 0, 0)), wide],
        out_shape=[f(S, 512), f(S, 512), f(nc, DN_HEADS, DN_DIM, DN_DIM), f(S, 512)],
        scratch_shapes=[pltpu.VMEM((DN_HEADS, DN_DIM, DN_DIM), F32)],
        compiler_params=_params(("arbitrary",)),
    )(w, u, qg, kd, a, gcs, pz, gn)


def _dn_scan_bwd(dcat, o, pz, gn, sst, vnew, w, qg, kd, a, gcs, dproj):
    S = o.shape[0]
    C = DN_CHUNK
    G = CHUNKS_SCAN
    R = G * C
    steps = S // R

    def body(dy_ref, o_ref, z_ref, gn_ref, sst_ref, vn_ref, w_ref, qg_ref, kd_ref, a_ref, gcs_ref, _,
             du_ref, dw_ref, dqg_ref, dkd_ref, da_ref, dz_ref, dsc_ref, dgn_ref, dstate):
        @pl.when(pl.program_id(0) == 0)
        def _():
            dstate[...] = jnp.zeros_like(dstate)
            dgn_ref[...] = jnp.zeros_like(dgn_ref)

        gn_ = gn_ref[...]
        lane = lax.broadcasted_iota(jnp.int32, (C, 128), 1)
        row = lax.broadcasted_iota(jnp.int32, (C, 128), 0)
        dgn_parts = []

        def head_program(chunk, h, dsc_parts):
            hs = _head(h)
            rows = slice(chunk * C, (chunk + 1) * C)
            ov, z, dout = o_ref[rows, hs], z_ref[rows, hs], dy_ref[rows, hs]
            r, oh = _rms_stats(ov)
            sg = _sigmoid(z)
            don = dout * (z * sg)
            dz_ref[rows, hs] = _bf(dout * (oh * gn_) * (sg * (1.0 + z * (1.0 - sg))))
            dgn_parts.append(jnp.sum(don * oh, axis=0, keepdims=True))
            dn = don * gn_
            do = _bf(r * (dn - oh * jnp.mean(dn * oh, axis=-1, keepdims=True)))
            s_in = sst_ref[chunk, h]
            sb = _bf(s_in)
            ds_out = dstate[h]
            dsb = _bf(ds_out)
            vnb = _bf(vn_ref[rows, hs])
            wb, qgb, kdb, ab = w_ref[rows, hs], qg_ref[rows, hs], kd_ref[rows, hs], a_ref[h, rows]
            dvn = _dot_tn(ab, do)[:C] + _dot(kdb, dsb)
            da_ref[h, rows] = _dot_nt(do, _rows_pad(vnb))
            dqg_ref[rows, hs] = _dot_nt(do, sb)
            dkd_ref[rows, hs] = _dot_nt(vnb, dsb)
            q_do = _dot_tn(qgb, do)
            yield
            dvnb = _bf(dvn)
            dw_ref[rows, hs] = _bf(-_dot_nt(dvnb, sb))
            w_dvn = _dot_tn(wb, dvnb)
            du_ref[rows, hs] = dvnb
            yield
            d_last = jnp.exp(gcs_ref[(chunk + 1) * C - 1:(chunk + 1) * C, h:h + 1])
            dd = jnp.sum(jnp.sum(ds_out * s_in, axis=1, keepdims=True), axis=0, keepdims=True)
            dsc_parts.append(jnp.where((lane == h) & (row == C - 1), dd * d_last, 0.0))
            dstate[h] = ds_out * d_last + q_do - w_dvn

        for chunk in reversed(range(G)):
            dsc_parts = []
            _interleave(head_program(chunk, h, dsc_parts) for h in range(DN_HEADS))
            dsc_ref[chunk * C:(chunk + 1) * C, :] = sum(dsc_parts[1:], dsc_parts[0])
        dgn_ref[...] += sum(dgn_parts[1:], dgn_parts[0])

    rev = lambda n: steps - 1 - n
    wide = pl.BlockSpec((R, 512), lambda n: (rev(n), 0))
    z_spec = pl.BlockSpec((R, 512), lambda n: (rev(n), BLK_Z))
    sq = pl.BlockSpec((DN_HEADS, R, CPAD), lambda n: (0, rev(n), 0))
    narrow = pl.BlockSpec((R, 128), lambda n: (rev(n), 0))
    gn_spec = pl.BlockSpec((1, DN_DIM), lambda n: (0, 0))
    f = lambda *shp: jax.ShapeDtypeStruct(shp, F32)
    b = lambda *shp: jax.ShapeDtypeStruct(shp, BF16)
    return pl.pallas_call(
        body, grid=(steps,), name="dn_scan_bwd",
        in_specs=[pl.BlockSpec((R, 512), lambda n: (rev(n), 1)), wide, z_spec, gn_spec,
                  pl.BlockSpec((G, DN_HEADS, DN_DIM, DN_DIM), lambda n: (rev(n), 0, 0, 0)),
                  wide, wide, wide, wide, sq, narrow, pl.BlockSpec(memory_space=pl.ANY)],
        out_specs=[wide, wide, wide, wide, sq, z_spec, narrow, gn_spec],
        out_shape=[b(S, 512), b(S, 512), f(S, 512), f(S, 512), f(DN_HEADS, S, CPAD),
                   jax.ShapeDtypeStruct(dproj.shape, dproj.dtype), f(S, 128), f(1, DN_DIM)],
        scratch_shapes=[pltpu.VMEM((DN_HEADS, DN_DIM, DN_DIM), F32)],
        input_output_aliases={11: 5},
        compiler_params=_params(("arbitrary",)),
    )(dcat, o, pz, gn, sst, vnew, w, qg, kd, a, gcs, dproj)


def _dn_chunk_bwd(qkv, pg, t_inv, gcs, du, dw, dqg, dkd, da, dsc, a_log, dt_bias, dproj):
    S = qkv.shape[0]
    C = DN_CHUNK
    G = CHUNKS_LOCAL
    R = G * C

    def body(alog_ref, dtb_ref, qkv_ref, pg_ref, t_ref, gcs_ref, du_ref, dw_ref, dqg_ref, dkd_ref, da_ref, dsc_ref, _,
             dqkv_ref, dpg_ref, acc_ref):
        @pl.when(pl.program_id(0) == 0)
        def _():
            acc_ref[...] = jnp.zeros_like(acc_ref)

        ii, jj = _chunk_masks()
        lane = lax.broadcasted_iota(jnp.int32, (1, 128), 1)
        row8 = lax.broadcasted_iota(jnp.int32, (8, 128), 0)
        lane8 = lax.broadcasted_iota(jnp.int32, (8, 128), 1)
        rowc = lax.broadcasted_iota(jnp.int32, (C, 1), 0)
        tril, strict = jj <= ii, jj < ii
        dpg_parts, acc_parts = [[] for _ in range(G)], []

        def head_program(chunk, h):
            rows = slice(chunk * C, (chunk + 1) * C)
            q, k, v = qkv_ref[rows, _head(h)], qkv_ref[rows, _head(DN_HEADS + h)], qkv_ref[rows, _head(2 * DN_HEADS + h)]
            gc_col, beta, g_col = gcs_ref[rows, h:h + 1], gcs_ref[rows, DN_HEADS + h:DN_HEADS + h + 1], \
                gcs_ref[rows, 2 * DN_HEADS + h:2 * DN_HEADS + h + 1]
            dec = _decay(gc_col, ii, jj)
            eg = jnp.exp(gc_col)
            g_last = gc_col[C - 1:C, :]
            ek = jnp.exp(g_last - gc_col)
            kb, vb = k * beta, v * beta
            kbg = kb * eg
            qb, kbb = _bf(q), _bf(kb)
            k_rows = _rows_pad(_bf(k))
            t = t_ref[h, rows]
            tb = _bf(t)
            dub, dwb = du_ref[rows, _head(h)], dw_ref[rows, _head(h)]
            dqg_, dkd_ = dqg_ref[rows, _head(h)], dkd_ref[rows, _head(h)]
            dt = _dot_nt(dub, _rows_pad(_bf(vb))) + _dot_nt(dwb, _rows_pad(_bf(kbg)))
            t_du_dw = _dot_tn(tb, jnp.concatenate([dub, dwb], axis=1))
            dvb, dkbg = t_du_dw[:C, :DN_DIM], t_du_dw[:C, DN_DIM:]
            kk = _dot_nt(kbb, k_rows)
            qk = _dot_nt(qb, k_rows)
            yield
            dt_t = _dot3_nt(dt, t)
            yield
            dl = -_dot3_tn(t, dt_t)
            yield
            dm = jnp.where(strict, dl * dec, 0.0)
            dqk = jnp.where(tril, da_ref[h, rows] * dec, 0.0)
            gmat = dm * kk + dqk * qk
            dgc = jnp.sum(gmat, axis=1, keepdims=True) - _row_to_col(jnp.sum(gmat, axis=0, keepdims=True), ii, jj)
            dmb, dqkb = _bf(dm), _bf(dqk)
            dkb = _dot(dmb, k_rows) + dkbg * eg
            dk = _dot_tn(jnp.concatenate([dmb, dqkb], axis=0), jnp.concatenate([kbb, qb], axis=0))[:C] + dkd_ * ek
            dq = _dot(dqkb, k_rows) + dqg_ * eg
            yield
            tk = jnp.sum(dkd_ * k * ek, axis=1, keepdims=True)
            dgc = dgc + jnp.sum(dqg_ * q * eg, axis=1, keepdims=True) - tk + jnp.sum(dkbg * kbg, axis=1, keepdims=True)
            dgl = jnp.sum(tk, axis=0, keepdims=True) + dsc_ref[(chunk + 1) * C - 1:(chunk + 1) * C, h:h + 1]
            dgc = dgc + jnp.where(rowc == C - 1, dgl, 0.0)
            dk = dk + dkb * beta
            dbeta = jnp.sum(dkb * k, axis=1, keepdims=True) + jnp.sum(dvb * v, axis=1, keepdims=True)
            dqkv_ref[rows, _head(h)] = dq
            dqkv_ref[rows, _head(DN_HEADS + h)] = dk
            dqkv_ref[rows, _head(2 * DN_HEADS + h)] = dvb * beta
            dg_col = jnp.sum(jnp.where(jj >= ii, _col_to_row(dgc, ii, jj), 0.0), axis=1, keepdims=True)
            db = dbeta * beta * (1.0 - beta)
            da_in = dg_col * (-jnp.exp(alog_ref[h])) * _sigmoid(pg_ref[rows, DN_HEADS + h:DN_HEADS + h + 1] + dtb_ref[h])
            dpg_parts[chunk].append(jnp.where(lane == h, db, 0.0) + jnp.where(lane == DN_HEADS + h, da_in, 0.0))
            acc_parts.append(jnp.where((row8 == 0) & (lane8 == h), jnp.sum(dg_col * g_col, axis=0, keepdims=True), 0.0)
                             + jnp.where((row8 == 1) & (lane8 == h), jnp.sum(da_in, axis=0, keepdims=True), 0.0))

        _interleave(head_program(chunk, h) for chunk in range(G) for h in range(DN_HEADS))
        for chunk in range(G):
            dpg = sum(dpg_parts[chunk][1:], dpg_parts[chunk][0])
            dpg_ref[chunk * C:(chunk + 1) * C, :] = _bf(jnp.concatenate([dpg, jnp.zeros_like(dpg)], axis=1))
        acc_ref[...] += sum(acc_parts[1:], acc_parts[0])

    smem = pl.BlockSpec(memory_space=pltpu.SMEM)
    wide = pl.BlockSpec((R, 512), lambda n: (n, 0))
    sq = pl.BlockSpec((DN_HEADS, R, CPAD), lambda n: (0, n, 0))
    narrow = pl.BlockSpec((R, 128), lambda n: (n, 0))
    qkv_spec = pl.BlockSpec((R, 1536), lambda n: (n, 0))
    f = lambda *shp: jax.ShapeDtypeStruct(shp, F32)
    return pl.pallas_call(
        body, grid=(S // R,), name="dn_chunk_bwd",
        in_specs=[smem, smem, qkv_spec, pl.BlockSpec((R, 128), lambda n: (n, BLK_G)), sq, narrow, wide, wide, wide, wide, sq,
                  narrow, pl.BlockSpec(memory_space=pl.ANY)],
        out_specs=[qkv_spec, pl.BlockSpec((R, 256), lambda n: (n, BLK_G_PAD)), pl.BlockSpec((8, 128), lambda n: (0, 0))],
        out_shape=[f(S, 1536), jax.ShapeDtypeStruct(dproj.shape, dproj.dtype), f(8, 128)],
        input_output_aliases={12: 1},
        compiler_params=_params(("arbitrary",)),
    )(a_log, dt_bias, qkv, pg, t_inv, gcs, du, dw, dqg, dkd, da, dsc, dproj)


def _fill_kv(dk, dv, dproj):
    S = dk.shape[0]
    tm = min(512, S)

    def body(dk_ref, dv_ref, _, o_ref):
        o_ref[...] = _bf(jnp.concatenate([dk_ref[...], dv_ref[...]], axis=1))

    tile = pl.BlockSpec((tm, 128), lambda i: (i, 0))
    return pl.pallas_call(
        body, grid=(S // tm,), name="fill_kv",
        in_specs=[tile, tile, pl.BlockSpec(memory_space=pl.ANY)],
        out_specs=pl.BlockSpec((tm, 256), lambda i: (i, BLK_KV)),
        out_shape=jax.ShapeDtypeStruct(dproj.shape, dproj.dtype),
        input_output_aliases={2: 0},
        compiler_params=_params(("parallel",)),
    )(dk, dv, dproj)


def _w_in_to_internal(wt):
    return jnp.concatenate([wt[0:512], wt[2304:2816], wt[768:2304], wt[512:768], wt[2816:2824],
                            jnp.zeros((D_IN_PAD - D_IN, wt.shape[1]), wt.dtype)], axis=0)


def _w_in_from_internal(gt):
    return jnp.concatenate([gt[0:512], gt[2560:2816], gt[1024:2560], gt[512:1024], gt[2816:2824]], axis=0)


def _local_step(x, p, target, wts, first_weights, other_weights, ship_early):
    S = x.shape[0]
    cos, sin = _rope_tables(S)
    sinks, a_log, dt_bias = wts["sinks"].reshape(8), wts["a_log"].reshape(4), wts["dt_bias"].reshape(4)
    gn = wts["dn_norm"].reshape(1, DN_DIM)
    add = lambda acc, res: (acc + res,)

    u = _rmsnorm_fwd(x, wts["norm_mix"], "norm_mix_fwd")
    w_in_t, conv_w = first_weights(u)
    proj, = _mm(u, w_in_t, form="nt", name="in_proj", out_dtypes=[F32], tn=512)
    attn, lse = _attn_fwd(proj, cos, sin, sinks)
    qkv = _dn_prep_fwd(proj, conv_w)
    cw, cu, cqg, ckd, ca, ct, gcs = _dn_chunk_fwd(qkv, proj, a_log, dt_bias)
    o, vnew, sst, dn_out = _dn_scan_fwd(cw, cu, cqg, ckd, ca, gcs, proj, gn)
    w_o, = other_weights(("w_o",), dn_out)
    h1, = _mm([attn, dn_out], w_o, form="nn", name="out_proj", out_dtypes=[F32], tn=512, epi=add, extra=[x])

    def relu2(acc):
        r = jnp.maximum(acc, 0.0)
        return r * r, r

    w_up, = other_weights(("w_up",), h1)
    hid, relu, m = _mm(h1, w_up, form="nn", name="mlp_up", out_dtypes=[BF16, BF16], tn=512, epi=relu2, norm=wts["norm_mlp"])
    w_down, = other_weights(("w_down",), hid)
    h2, = _mm(hid, w_down, form="nn", name="mlp_down", out_dtypes=[F32], tn=512, epi=add, extra=[h1])
    w_pg, w_pp = other_weights(("w_ple_gate", "w_ple_proj"), h2)
    n3, dh2, dgl, dpp, loss, d_norm_final, d_norm_ple = _ple_and_loss(h2, p, target, w_pg, w_pp, wts["norm_ple"],
                                                                     wts["norm_final"].reshape(1, D_MODEL))
    g = {"norm_final": d_norm_final, "norm_ple": d_norm_ple}
    early = {"w_ple_gate": _mm_tn(n3, dgl, name="d_w_ple_gate", tm=512, tn=1024, out_dtype=BF16).reshape(N_DEV, 128, 1024),
             "w_ple_proj": _mm_tn(p, dpp, name="d_w_ple_proj", tm=256, tn=128, out_dtype=BF16, column_shards=True)}
    d_act, = _mm(dh2, w_down, form="nt", name="d_hidden", out_dtypes=[BF16], tn=512,
                 epi=lambda acc, r: (acc * (2.0 * r.astype(F32)),), extra=[relu])
    early["w_down"] = _mm_tn(hid, dh2, name="d_w_down", tm=512, tn=1024, out_dtype=BF16).reshape(N_DEV, 512, 1024)
    early["w_up"] = _mm_tn(m, d_act, name="d_w_up", tm=1024, tn=512, out_dtype=BF16, column_shards=True)
    token = ship_early(early)
    dh1, g["norm_mlp"], dcat = _mm(d_act, w_up, form="nt", name="d_m", out_dtypes=[F32], tn=512, after=token,
                                   norm_bwd=(h1, wts["norm_mlp"], dh2), then_nt=w_o)
    d_w_o = jnp.concatenate([_mm_tn(attn, dh1, name="d_w_o_attn", tm=512, tn=512, out_dtype=BF16),
                             _mm_tn(dn_out, dh1, name="d_w_o_dn", tm=512, tn=512, out_dtype=BF16)], axis=0)
    token = ship_early({"w_o": d_w_o.reshape(N_DEV, 128, 1024)})
    dproj, dk, dv, dsinks = _attn_bwd(proj, cos, sin, sinks + token[0, 0], dcat, attn, lse)
    g["sinks"] = dsinks[:, 0].reshape(1, 8)
    du_, dw_, dqg, dkd, da, dproj, dsc, g["dn_norm"] = _dn_scan_bwd(dcat, o, proj, gn, sst, vnew, cw, cqg, ckd, ca, gcs, dproj)
    dqkv, dproj, gate_acc = _dn_chunk_bwd(qkv, proj, ct, gcs, du_, dw_, dqg, dkd, da, dsc, a_log, dt_bias, dproj)
    g["a_log"], g["dt_bias"] = gate_acc[0:1, 0:4], gate_acc[1:2, 0:4]
    dproj, g["conv_w"] = _dn_prep_bwd(proj, conv_w, dqkv, dproj)
    dproj = _fill_kv(dk, dv, dproj)
    token = ship_early({"w_in": _mm_tn(dproj, u, name="d_w_in", tm=512, tn=1024, out_dtype=BF16)})
    grad_x, g["norm_mix"] = _mm(dproj, w_in_t, form="nn", name="d_u", out_dtypes=[F32], tn=512, after=token,
                                norm_bwd=(x, wts["norm_mix"], dh1))
    return loss, grad_x, g


def _peer(k):
    x, y, c = lax.axis_index("x"), lax.axis_index("y"), lax.axis_index("c")
    px = 1 - x if k & 4 else x
    py = 1 - y if k & 2 else y
    pc = 1 - c if k & 1 else c
    return (px, py, pc), 4 * px + 2 * py + pc


def _exchange(srcs, name, gather):
    n = len(srcs)
    gathers = list(gather) if isinstance(gather, (list, tuple)) else [gather] * n
    shapes = [(N_DEV,) + s.shape if gt else s.shape for s, gt in zip(srcs, gathers)]

    def body(*refs):
        src_refs, out_refs = refs[:n], refs[n:2 * n]
        send_sems, recv_sems, local_sems = refs[2 * n:]
        _, me = _peer(0)
        piece = lambda a, d: src_refs[a] if gathers[a] else src_refs[a].at[d]
        local = [pltpu.make_async_copy(piece(a, me), out_refs[a].at[me], local_sems.at[a]) for a in range(n)]
        for cp in local:
            cp.start()
        copies = []
        for a in range(n):
            for k in range(1, N_DEV):
                dev, idx = _peer(k)
                cp = pltpu.make_async_remote_copy(src_ref=piece(a, idx), dst_ref=out_refs[a].at[me],
                                                  send_sem=send_sems.at[a, k - 1], recv_sem=recv_sems.at[a, k - 1],
                                                  device_id=dev, device_id_type=MESH)
                cp.start()
                copies.append(cp)
        for cp in copies:
            cp.wait_recv()
        for cp in copies:
            cp.wait_send()
        for cp in local:
            cp.wait()

    anywhere = pl.BlockSpec(memory_space=pl.ANY)
    return pl.pallas_call(
        body, name=name, in_specs=[anywhere] * n, out_specs=[anywhere] * n,
        out_shape=[jax.ShapeDtypeStruct(shp, s.dtype) for shp, s in zip(shapes, srcs)],
        scratch_shapes=[pltpu.SemaphoreType.DMA((n, N_DEV - 1)), pltpu.SemaphoreType.DMA((n, N_DEV - 1)),
                        pltpu.SemaphoreType.DMA((n,))],
    )(*srcs)


_HBM = pl.BlockSpec(memory_space=pltpu.HBM)
_SEM = pl.BlockSpec(memory_space=pltpu.SEMAPHORE)
_EFFECT = pltpu.SideEffectType.DATAFLOW_SIDE_EFFECTING


def _split_copies(src_refs, land_refs, send_sems, recv_sems, modes, which=None):
    _, me = _peer(0)
    copies = []
    which = range(len(src_refs)) if which is None else which
    for a, src, land in zip(which, src_refs, land_refs):
        if modes[a] == "columns":
            n_cols = src.shape[1]
            dst = land.at[:, pl.ds(pl.multiple_of(me * n_cols, n_cols), n_cols)]
        else:
            dst = land.at[me]
        for k in range(1, N_DEV):
            dev, idx = _peer(k)
            sem = a * (N_DEV - 1) + k - 1
            copies.append(pltpu.make_async_remote_copy(
                src_ref=src.at[idx] if modes[a] == "pieces" else src, dst_ref=dst, send_sem=send_sems.at[sem],
                recv_sem=recv_sems.at[sem], device_id=dev, device_id_type=MESH))
    return copies


def _exchange_start(srcs, name, modes):
    n = len(srcs)
    modes = [modes] * n if isinstance(modes, str) else list(modes)
    me = 4 * lax.axis_index("x") + 2 * lax.axis_index("y") + lax.axis_index("c")
    lands = []
    for s, mode in zip(srcs, modes):
        if mode == "columns":
            empty = lax.empty((s.shape[0], N_DEV * s.shape[1]), s.dtype)
            lands.append(lax.dynamic_update_slice(empty, s, (0, me * s.shape[1])))
        else:
            own = s if mode == "slots" else lax.dynamic_index_in_dim(s, me, 0, keepdims=False)
            shape = (N_DEV,) + s.shape if mode == "slots" else s.shape
            lands.append(lax.dynamic_update_index_in_dim(lax.empty(shape, s.dtype), own, me, 0))

    def body(*refs):
        src_refs, land_refs = refs[:n], refs[n:2 * n]
        send_sems, recv_sems = refs[2 * n], refs[2 * n + 1]
        for cp in _split_copies(src_refs, land_refs, send_sems, recv_sems, modes):
            cp.start()
        refs[-1][...] = jnp.zeros_like(refs[-1])

    both = list(srcs) + lands
    sems = pltpu.SemaphoreType.DMA((n * (N_DEV - 1),))
    out = pl.pallas_call(
        body, name=name,
        out_shape=(sems, sems, *[pltpu.HBM(t.shape, t.dtype) for t in both], jax.ShapeDtypeStruct((8, 128), F32)),
        in_specs=[_HBM] * (2 * n), out_specs=(_SEM, _SEM, *[_HBM] * (2 * n), pl.BlockSpec(memory_space=pltpu.VMEM)),
        input_output_aliases={i: 2 + i for i in range(2 * n)},
        compiler_params=pltpu.CompilerParams(has_side_effects=_EFFECT),
    )(*[pltpu.with_memory_space_constraint(t, pltpu.HBM) for t in both])
    return (n, modes, out[:-1]), out[-1]


def _exchange_wait(handle, after, name, which=None):
    n_all, modes, (send_sems, recv_sems, *both_all) = handle
    which = list(range(n_all)) if which is None else list(which)
    n = len(which)
    both = [both_all[a] for a in which] + [both_all[n_all + a] for a in which]

    def body(*refs):
        src_refs, land_refs = refs[:n], refs[n:2 * n]
        for cp in _split_copies(src_refs, land_refs, refs[2 * n], refs[2 * n + 1], modes, which):
            cp.wait_send()
            cp.wait_recv()

    out = pl.pallas_call(
        body, name=name, out_shape=tuple(pltpu.HBM(t.shape, t.dtype) for t in both),
        in_specs=[_HBM] * (2 * n) + [_SEM, _SEM, pl.BlockSpec(memory_space=pl.ANY)], out_specs=tuple([_HBM] * (2 * n)),
        input_output_aliases={i: i for i in range(2 * n)},
        compiler_params=pltpu.CompilerParams(has_side_effects=_EFFECT),
    )(*both, send_sems, recv_sems, after)
    return list(out[n:])


def _adam_update(g, w, m, v):
    nm = ADAM_B1 * m + (1.0 - ADAM_B1) * g
    nv = ADAM_B2 * v + (1.0 - ADAM_B2) * (g * g)
    m_hat = nm / (1.0 - ADAM_B1 ** ADAM_STEP)
    v_hat = nv / (1.0 - ADAM_B2 ** ADAM_STEP)
    return -ADAM_LR * (m_hat / (jnp.sqrt(v_hat) + ADAM_EPS) + ADAM_WD * w), nm, nv


def _adamw(parts, w, m, v, name):
    n, R, W = parts.shape
    tm = 128 if R % 128 == 0 else R

    def body(p_ref, w_ref, m_ref, v_ref, g_ref, d_ref, nm_ref, nv_ref):
        g = p_ref[0].astype(F32)
        for s in range(1, n):
            g = g + p_ref[s].astype(F32)
        g_ref[...] = g
        d_ref[...], nm_ref[...], nv_ref[...] = _adam_update(g, w_ref[...], m_ref[...], v_ref[...])

    tile = pl.BlockSpec((tm, W), lambda i: (i, 0))
    return pl.pallas_call(
        body, grid=(R // tm,), name=name,
        in_specs=[pl.BlockSpec((n, tm, W), lambda i: (0, i, 0)), tile, tile, tile],
        out_specs=[tile] * 4, out_shape=[jax.ShapeDtypeStruct((R, W), F32)] * 4,
        compiler_params=_params(("parallel",)),
    )(parts, w, m, v)


_MATRICES = ("w_in", "w_o", "w_up", "w_down", "w_ple_gate", "w_ple_proj")


_OTHERS = ("w_o", "w_up", "w_down", "w_ple_gate", "w_ple_proj")
_OTHER_MODES = {"w_o": "slots", "w_up": "slots", "w_down": "slots", "w_ple_gate": "slots", "w_ple_proj": "columns"}


_VECTORS = ("norm_mix", "norm_mlp", "norm_ple", "norm_final", "a_log", "dt_bias", "sinks", "dn_norm")
_SMALL_ROWS, _LOSS_ROW, _CONV_ROW = 16, 8, 9


def _pack_small(vectors, loss, conv):
    def body(*refs):
        out = refs[-1]
        out[...] = jnp.zeros_like(out)
        for r, ref in enumerate(refs[:len(_VECTORS)]):
            out[r:r + 1, 0:ref.shape[1]] = ref[...]
        out[_LOSS_ROW:_LOSS_ROW + 1, 0:128] = refs[len(_VECTORS)][...]
        out[_CONV_ROW:_CONV_ROW + 6, :] = refs[len(_VECTORS) + 1][...]

    return pl.pallas_call(body, name="pack_small", out_shape=jax.ShapeDtypeStruct((_SMALL_ROWS, 1024), F32))(*vectors, loss, conv)


def _sum_slots(parts):
    def body(p_ref, o_ref):
        acc = p_ref[0]
        for s in range(1, parts.shape[0]):
            acc = acc + p_ref[s]
        o_ref[...] = acc

    return pl.pallas_call(body, name="sum_small", out_shape=jax.ShapeDtypeStruct(parts.shape[1:], parts.dtype))(parts)


def _adamw_vectors(summed, conv_g, wmv):
    names = _VECTORS + ("conv_w",)
    flat = [a for triple in wmv for a in triple]

    def body(*refs):
        sum_ref, conv_ref = refs[0], refs[1]
        ins, outs = refs[2:2 + len(flat)], refs[2 + len(flat):]
        for i in range(len(names)):
            w_ref, m_ref, v_ref = ins[3 * i:3 * i + 3]
            g = conv_ref[...] if i == len(_VECTORS) else sum_ref[i:i + 1, 0:w_ref.shape[1]]
            outs[4 * i][...] = g
            outs[4 * i + 1][...], outs[4 * i + 2][...], outs[4 * i + 3][...] = _adam_update(g, w_ref[...], m_ref[...], v_ref[...])

    out_shape = [jax.ShapeDtypeStruct(t[0].shape, F32) for t in wmv for _ in range(4)]
    res = pl.pallas_call(body, name="adamw_vectors", out_shape=out_shape)(summed, conv_g, *flat)
    return {n: res[4 * i:4 * i + 4] for i, n in enumerate(names)}


_ORDER = ("norm_mix", "w_in", "conv_w", "a_log", "dt_bias", "dn_norm", "sinks", "w_o", "norm_mlp", "w_up", "w_down",
          "norm_ple", "w_ple_gate", "w_ple_proj", "norm_final")


def kernel(x, p, norm_mix, w_in, conv_w, a_log, dt_bias, dn_norm, sinks, w_o, norm_mlp, w_up, w_down, norm_ple, w_ple_gate, w_ple_proj, norm_final, loss_target, m_norm_mix, m_w_in, m_conv_w, m_a_log, m_dt_bias, m_dn_norm, m_sinks, m_w_o, m_norm_mlp, m_w_up, m_w_down, m_norm_ple, m_w_ple_gate, m_w_ple_proj, m_norm_final, v_norm_mix, v_w_in, v_conv_w, v_a_log, v_dt_bias, v_dn_norm, v_sinks, v_w_o, v_norm_mlp, v_w_up, v_w_down, v_norm_ple, v_w_ple_gate, v_w_ple_proj, v_norm_final):
    w = dict(norm_mix=norm_mix, w_in=w_in[0], conv_w=conv_w[0], a_log=a_log, dt_bias=dt_bias, dn_norm=dn_norm, sinks=sinks,
             w_o=w_o[0], norm_mlp=norm_mlp, w_up=w_up[0], w_down=w_down[0], norm_ple=norm_ple, w_ple_gate=w_ple_gate[0],
             w_ple_proj=w_ple_proj[0], norm_final=norm_final)
    m = dict(norm_mix=m_norm_mix, w_in=m_w_in[0], conv_w=m_conv_w[0], a_log=m_a_log, dt_bias=m_dt_bias, dn_norm=m_dn_norm,
             sinks=m_sinks, w_o=m_w_o[0], norm_mlp=m_norm_mlp, w_up=m_w_up[0], w_down=m_w_down[0], norm_ple=m_norm_ple,
             w_ple_gate=m_w_ple_gate[0], w_ple_proj=m_w_ple_proj[0], norm_final=m_norm_final)
    v = dict(norm_mix=v_norm_mix, w_in=v_w_in[0], conv_w=v_conv_w[0], a_log=v_a_log, dt_bias=v_dt_bias, dn_norm=v_dn_norm,
             sinks=v_sinks, w_o=v_w_o[0], norm_mlp=v_norm_mlp, w_up=v_w_up[0], w_down=v_w_down[0], norm_ple=v_norm_ple,
             w_ple_gate=v_w_ple_gate[0], w_ple_proj=v_w_ple_proj[0], norm_final=v_norm_final)
    me = 4 * lax.axis_index("x") + 2 * lax.axis_index("y") + lax.axis_index("c")
    conv_shard = conv_w.shape[2]

    for d in (w, m, v):
        d["w_in"] = d["w_in"].T
    conv_pad = jnp.pad(w["conv_w"], ((0, 8 - DN_CONV), (0, 256 - conv_shard)))
    first, token_first = _exchange_start([_bf(w["w_in"]), conv_pad], "gather_first_start", "slots")
    later = [_bf(w[n]) for n in _OTHERS]
    later[-1] = _bf(w["w_ple_proj"] + token_first[0:1, 0:1])
    others, token_others = _exchange_start(later, "gather_others_start", [_OTHER_MODES[n] for n in _OTHERS])
    vectors = dict(w)
    vectors["norm_mix"] = w["norm_mix"] + token_others[0:1, 0:1]

    def first_weights(after):
        w_in_all, conv_all = _exchange_wait(first, after, "gather_first_wait")
        conv_all = jnp.transpose(conv_all[:, :DN_CONV, :conv_shard], (1, 0, 2)).reshape(DN_CONV, N_DEV * conv_shard)
        return _w_in_to_internal(w_in_all.reshape(D_IN, D_MODEL)), conv_all

    as_taken = {"w_o": lambda t: t.reshape(1024, 1024), "w_up": lambda t: t, "w_down": lambda t: t.reshape(4096, 1024),
                "w_ple_gate": lambda t: t.reshape(1024, 1024), "w_ple_proj": lambda t: t}

    def other_weights(names, after):
        which = [_OTHERS.index(n) for n in names]
        got = _exchange_wait(others, after, "gather_wait_" + names[0], which)
        return [as_taken[n](t) for n, t in zip(names, got)]

    shipped = []

    def ship_early(pieces):
        names = tuple(pieces)
        if names == ("w_in",):
            pieces = {"w_in": _w_in_from_internal(pieces["w_in"]).reshape(N_DEV, D_IN // N_DEV, D_MODEL)}
        handle, token = _exchange_start([pieces[n] for n in names], "scatter_start_" + names[0], "pieces")
        shipped.append((names, handle))
        return token

    loss, grad_x, g = _local_step(x[0], p[0, 0], loss_target[0], vectors, first_weights, other_weights, ship_early)

    row = lambda t: t.reshape(1, t.size)
    small = _pack_small([row(g[n]) for n in _VECTORS], loss, g["conv_w"].reshape(6, 1024))
    small_all, = _exchange([small], "gather_small", gather=True)
    summed = _sum_slots(small_all)
    conv_g = lax.dynamic_slice(summed[_CONV_ROW:_CONV_ROW + 6].reshape(DN_CONV, N_DEV * conv_shard), (0, me * conv_shard),
                               (DN_CONV, conv_shard))
    small_out = _adamw_vectors(summed, conv_g, [(row(w[n]), row(m[n]), row(v[n])) for n in _VECTORS]
                               + [(w["conv_w"], m["conv_w"], v["conv_w"])])
    big, after = {}, small_out["conv_w"][0]
    for names, handle in shipped:
        for n, r in zip(names, _exchange_wait(handle, after, "scatter_wait_" + names[0])):
            big[n] = _adamw(r, w[n], m[n], v[n], "adamw_" + n)
            after = big[n][1]

    result = [summed[_LOSS_ROW, 0], grad_x[None]]
    for i in range(4):
        for n in _ORDER:
            if n == "w_in":
                result.append(big[n][i].T[None])
            elif n in _MATRICES:
                result.append(big[n][i][None])
            elif n == "conv_w":
                result.append(small_out[n][i][None])
            else:
                result.append(small_out[n][i].reshape(w[n].shape))
    return tuple(result)
```

```python
import jax
import jax.numpy as jnp
import numpy as np
from jax import lax
from jax.experimental import pallas as pl
from jax.experimental.pallas import tpu as pltpu

F32, BF16 = jnp.float32, jnp.bfloat16
EPS = 1e-6
D_MODEL = 1024
N_DEV = 8
ATTN_BLOCK = 128
HEAD_PAIR = 128
DN_HEADS = 4
DN_DIM = 128
DN_CHUNK = 64
DN_CONV = 4
ROPE_THETA = 10000.0
D_IN = 2824
D_IN_PAD = 3072
BLK_Q, BLK_Z = 0, 1
BLK_DN, BLK_K, BLK_V, BLK_G = 8, 20, 21, 22
BLK_KV, BLK_G_PAD = 10, 11
VMEM_LIMIT = 56 * 1024 * 1024
NEG = -1e30
ADAM_LR, ADAM_B1, ADAM_B2, ADAM_EPS, ADAM_WD, ADAM_STEP = 0.001, 0.9, 0.999, 1e-08, 0.01, 10
MESH = pl.DeviceIdType.MESH


def _bf(x):
    return x.astype(BF16)


def _dot(a, b):
    return jnp.dot(a, b, preferred_element_type=F32)


def _dot_nt(a, b):
    return lax.dot_general(a, b, (((1,), (1,)), ((), ())), preferred_element_type=F32)


def _dot_tn(a, b):
    return lax.dot_general(a, b, (((0,), (0,)), ((), ())), preferred_element_type=F32)


def _sigmoid(x):
    return 1.0 / (1.0 + jnp.exp(-x))


def _params(sem):
    return pltpu.CompilerParams(dimension_semantics=sem, vmem_limit_bytes=VMEM_LIMIT)


def _mm(x, w, *, form, name, out_dtypes, tn, epi=None, extra=(), tm=512, w_row_block=0, after=None, norm=None,
        norm_bwd=None, then_nt=None):
    assert norm is None or norm_bwd is None
    xs = list(x) if isinstance(x, (list, tuple)) else [x]
    nx = len(xs)
    S, K = xs[0].shape
    shards = w.ndim == 3
    N = (w.shape[2] * N_DEV if shards else w.shape[1]) if form == "nn" else w.shape[-2]
    assert not (shards and form == "nn" and tn != w.shape[2]) and (nx == 1 or (form == "nn" and not shards and norm is None))
    r0 = w_row_block * K
    tm = min(tm, S)
    n_extra, n_out = len(extra), len(out_dtypes)
    tile = lambda width: pl.BlockSpec((tm, width), lambda i: (i, 0))
    whole = lambda a: pl.BlockSpec(a.shape, lambda i, nd=a.ndim: (0,) * nd)
    ins, in_specs = [*xs, w, *extra], [tile(K)] * nx + [whole(w)] + [tile(N)] * n_extra
    if norm is not None:
        ins, in_specs = ins + [norm], in_specs + [whole(norm)]
    if norm_bwd is not None:
        ins, in_specs = ins + list(norm_bwd), in_specs + [tile(N), whole(norm_bwd[1]), tile(N)]
    if then_nt is not None:
        ins, in_specs = ins + [then_nt], in_specs + [whole(then_nt)]
    if after is not None:
        ins, in_specs = ins + [after], in_specs + [whole(after)]
    out_shape = [jax.ShapeDtypeStruct((S, N), dt) for dt in out_dtypes]
    out_specs = [tile(N)] * n_out
    if norm is not None:
        out_shape, out_specs = out_shape + [jax.ShapeDtypeStruct((S, K), BF16)], out_specs + [tile(K)]
    if norm_bwd is not None:
        out_shape, out_specs = out_shape + [jax.ShapeDtypeStruct((1, N), F32)], out_specs + [pl.BlockSpec((1, N), lambda i: (0, 0))]
    if then_nt is not None:
        out_shape, out_specs = out_shape + [jax.ShapeDtypeStruct((S, then_nt.shape[0]), F32)], out_specs + [tile(then_nt.shape[0])]

    def product(xb, w_ref, cols, c):
        if form == "nn" and nx > 1:
            return sum(_dot(part, w_ref[r0 + p * K:r0 + (p + 1) * K, cols]) for p, part in enumerate(xb))
        if form == "nn":
            return _dot(xb, w_ref[c] if shards else w_ref[r0:r0 + K, cols])
        if not shards:
            return _dot_nt(xb, w_ref[cols, :])
        ks = w.shape[2]
        acc = _dot_nt(xb[:, 0:ks], w_ref[0, cols, :])
        for s in range(1, N_DEV):
            acc = acc + _dot_nt(xb[:, s * ks:(s + 1) * ks], w_ref[s, cols, :])
        return acc

    def body(*refs):
        x_ref, w_ref = refs[0], refs[nx]
        extra_refs = refs[nx + 1:nx + 1 + n_extra]
        at = nx + 1 + n_extra
        if norm is not None:
            gain_ref, at = refs[at], at + 1
        if norm_bwd is not None:
            (y_ref, ygain_ref, dres_ref), at = refs[at:at + 3], at + 3
        if then_nt is not None:
            w2_ref, at = refs[at], at + 1
        outs = refs[len(ins):]
        if norm is not None:
            _, xh = _rms_stats(x_ref[...])
            xb = _bf(xh * gain_ref[...])
            outs[n_out][...] = xb
        else:
            xb = _bf(x_ref[...]) if nx == 1 else [_bf(r[...]) for r in refs[:nx]]
        for c in range(N // tn):
            cols = slice(c * tn, (c + 1) * tn)
            acc = product(xb, w_ref, cols, c)
            res = epi(acc, *[r[:, cols] for r in extra_refs]) if epi else (acc,)
            for o, r in zip(outs[:n_out], res):
                o[:, cols] = r.astype(o.dtype)
        if norm_bwd is not None:
            dx, dg = _rms_bwd_tile(y_ref[...], ygain_ref[...], outs[0][...])
            outs[0][...] = dres_ref[...] + dx
            dg_ref = outs[n_out]

            @pl.when(pl.program_id(0) == 0)
            def _():
                dg_ref[...] = jnp.zeros_like(dg_ref)

            dg_ref[...] += dg
        if then_nt is not None:
            yb = _bf(outs[0][...])
            for c in range(then_nt.shape[0] // tn):
                cols = slice(c * tn, (c + 1) * tn)
                outs[-1][:, cols] = _dot_nt(yb, w2_ref[cols, :])

    return pl.pallas_call(
        body, grid=(S // tm,), name=name, in_specs=in_specs, out_specs=out_specs, out_shape=out_shape,
        compiler_params=_params(("arbitrary",) if norm_bwd is not None else ("parallel",)),
    )(*ins)


def _mm_tn(x, dy, *, name, tm, tn, out_dtype=F32, column_shards=False, after=None):
    S, K = x.shape
    N = dy.shape[1]
    waits = [] if after is None else [after]

    def body(x_ref, dy_ref, *rest):
        rest[-1][...] = _dot_tn(_bf(x_ref[...]), _bf(dy_ref[...])).astype(out_dtype)

    if column_shards:
        out_spec = pl.BlockSpec((None, tm, tn), lambda i, j: (j, i, 0))
        out_shape = jax.ShapeDtypeStruct((N // tn, K, tn), out_dtype)
    else:
        out_spec = pl.BlockSpec((tm, tn), lambda i, j: (i, j))
        out_shape = jax.ShapeDtypeStruct((K, N), out_dtype)
    return pl.pallas_call(
        body, grid=(K // tm, N // tn), name=name,
        in_specs=[pl.BlockSpec((S, tm), lambda i, j: (0, i)), pl.BlockSpec((S, tn), lambda i, j: (0, j))]
        + [pl.BlockSpec(memory_space=pl.ANY)] * len(waits),
        out_specs=out_spec, out_shape=out_shape,
        compiler_params=_params(("parallel", "parallel")),
    )(x, dy, *waits)


def _rowwise(body, *, tiled, full, out_tiled, out_acc, name, tm=512, smem=()):
    S = tiled[0].shape[0]
    tm = min(tm, S)
    n_in = len(smem) + len(tiled) + len(full)

    def kern(*refs):
        @pl.when(pl.program_id(0) == 0)
        def _():
            for r in refs[n_in + len(out_tiled):]:
                r[...] = jnp.zeros_like(r)
        body(*refs)

    in_specs = [pl.BlockSpec(memory_space=pltpu.SMEM) for _ in smem]
    in_specs += [pl.BlockSpec((tm, a.shape[1]), lambda i: (i, 0)) for a in tiled]
    in_specs += [pl.BlockSpec(a.shape, lambda i, nd=a.ndim: (0,) * nd) for a in full]
    out_specs = [pl.BlockSpec((tm, w), lambda i: (i, 0)) for w, _ in out_tiled]
    out_specs += [pl.BlockSpec(shp, lambda i, nd=len(shp): (0,) * nd) for shp, _ in out_acc]
    out_shape = [jax.ShapeDtypeStruct((S, w), dt) for w, dt in out_tiled]
    out_shape += [jax.ShapeDtypeStruct(shp, dt) for shp, dt in out_acc]
    return pl.pallas_call(
        kern, grid=(S // tm,), name=name, in_specs=in_specs, out_specs=out_specs, out_shape=out_shape,
        compiler_params=_params(("arbitrary",)),
    )(*smem, *tiled, *full)


def _rms_stats(x):
    r = lax.rsqrt(jnp.mean(x * x, axis=-1, keepdims=True) + EPS)
    return r, x * r


def _rmsnorm_fwd(x, g, name):
    def body(x_ref, g_ref, o_ref):
        _, xh = _rms_stats(x_ref[...])
        o_ref[...] = _bf(xh * g_ref[...])

    return _rowwise(body, tiled=[x], full=[g], out_tiled=[(x.shape[1], BF16)], out_acc=[], name=name)[0]


def _rms_bwd_tile(x, g, dxn):
    r, xh = _rms_stats(x)
    dg = jnp.sum(dxn * xh, axis=0, keepdims=True)
    dn = dxn * g
    dx = r * (dn - xh * jnp.mean(dn * xh, axis=-1, keepdims=True))
    return dx, dg


def _ple_and_loss(h2, p, target, w_pg, w_pp, g_ple, g_final):
    S, n = h2.shape
    tm = min(512, S)
    tn = 512

    def body(h2_ref, p_ref, t_ref, wpg_ref, wpp_ref, gple_ref, gfin_ref,
             n3_ref, dh_ref, dgl_ref, dpp_ref, loss_ref, dg_ref, dgple_ref, pp, gate, h3):
        @pl.when(pl.program_id(0) == 0)
        def _():
            loss_ref[...] = jnp.zeros_like(loss_ref)
            dg_ref[...] = jnp.zeros_like(dg_ref)
            dgple_ref[...] = jnp.zeros_like(dgple_ref)

        x = h2_ref[...]
        _, xh = _rms_stats(x)
        n3 = _bf(xh * gple_ref[...])
        n3_ref[...] = n3
        pb = _bf(p_ref[...])
        for c in range(n // tn):
            cols = slice(c * tn, (c + 1) * tn)
            pp[:, cols] = _dot(pb, wpp_ref[:, cols])
            gt = _sigmoid(_dot(n3, wpg_ref[:, cols]))
            gate[:, cols] = gt
            h3[:, cols] = x[:, cols] + gt * pp[:, cols]
        y = h3[...]
        _, yh = _rms_stats(y)
        e = yh * gfin_ref[...] - t_ref[...]
        per_tok = jnp.mean(e * e, axis=-1, keepdims=True)
        loss_ref[...] += 0.5 * jnp.sum(per_tok, axis=0, keepdims=True)
        dh, dg = _rms_bwd_tile(y, gfin_ref[...], e * (1.0 / n))
        dg_ref[...] += dg
        gt = gate[...]
        dgl = _bf(dh * pp[...] * gt * (1.0 - gt))
        dgl_ref[...] = dgl
        dpp_ref[...] = _bf(dh * gt)
        for c in range(n // tn):
            cols = slice(c * tn, (c + 1) * tn)
            h3[:, cols] = _dot_nt(dgl, wpg_ref[cols, :])
        dx, dgp = _rms_bwd_tile(x, gple_ref[...], h3[...])
        dh_ref[...] = dh + dx
        dgple_ref[...] += dgp

    tile = lambda width: pl.BlockSpec((tm, width), lambda i: (i, 0))
    whole = lambda a: pl.BlockSpec(a.shape, lambda i, nd=a.ndim: (0,) * nd)
    return pl.pallas_call(
        body, grid=(S // tm,), name="ple_and_loss",
        in_specs=[tile(n), tile(p.shape[1]), tile(n), whole(w_pg), whole(w_pp), whole(g_ple), whole(g_final)],
        out_specs=[tile(n), tile(n), tile(n), tile(n), pl.BlockSpec((1, 128), lambda i: (0, 0)),
                   pl.BlockSpec((1, n), lambda i: (0, 0)), pl.BlockSpec((1, n), lambda i: (0, 0))],
        out_shape=[jax.ShapeDtypeStruct((S, n), BF16), jax.ShapeDtypeStruct((S, n), F32), jax.ShapeDtypeStruct((S, n), BF16),
                   jax.ShapeDtypeStruct((S, n), BF16), jax.ShapeDtypeStruct((1, 128), F32), jax.ShapeDtypeStruct((1, n), F32),
                   jax.ShapeDtypeStruct((1, n), F32)],
        scratch_shapes=[pltpu.VMEM((tm, n), F32)] * 3,
        compiler_params=_params(("arbitrary",)),
    )(h2, p, target, w_pg, w_pp, g_ple, g_final)


def _rope_tables(S):
    half = 32
    inv = (1.0 / (np.float32(ROPE_THETA) ** (np.arange(half, dtype=np.float32) * np.float32(2.0 / 64)))).astype(np.float32)
    ang = np.arange(S).astype(np.float32)[:, None] * inv[None, :]
    cos, sin = np.cos(ang), np.sin(ang)
    return jnp.asarray(np.tile(cos, (1, 4))), jnp.asarray(np.concatenate([-sin, sin, -sin, sin], axis=1))


def _attn_common(i, kc, kp, vc, vp, cc, sc, cp, sp):
    lane = lax.broadcasted_iota(jnp.int32, (1, HEAD_PAIR), 1)
    lane_lo = jnp.bitwise_and(lane, 63) < 32
    slot = [lane < 64, lane >= 64]

    def swap_halves(t):
        return jnp.where(lane_lo, pltpu.roll(t, 96, 1), pltpu.roll(t, 32, 1))

    def rope(t, cos, sin):
        return t * cos + swap_halves(t) * sin

    def unrope(d, cos, sin):
        return d * cos + swap_halves(d * sin)

    k2 = jnp.concatenate([rope(kp, cp, sp), rope(kc, cc, sc)], axis=0)
    v2 = jnp.concatenate([vp, vc], axis=0)
    r = lax.broadcasted_iota(jnp.int32, (ATTN_BLOCK, 2 * ATTN_BLOCK), 0)
    c = lax.broadcasted_iota(jnp.int32, (ATTN_BLOCK, 2 * ATTN_BLOCK), 1)
    valid = (c > r) & (c <= r + ATTN_BLOCK) & jnp.logical_or(c >= ATTN_BLOCK, i > 0)
    ks, vs = {}, {}
    for j in range(2):
        kn = jnp.where(slot[j], k2, 0.0)
        vn = jnp.where(slot[j], v2, 0.0)
        for s in range(2):
            ks[j, s] = _bf(kn if s == j else pltpu.roll(kn, 64, 1))
            vs[j, s] = _bf(vn if s == j else pltpu.roll(vn, 64, 1))
    return slot, rope, unrope, valid, ks, vs


def _attn_probs(scores, valid, sink):
    s = jnp.where(valid, scores * 0.125, NEG)
    m = jnp.maximum(jnp.max(s, axis=1, keepdims=True), sink)
    e = jnp.exp(s - m)
    z = jnp.sum(e, axis=1, keepdims=True) + jnp.exp(sink - m)
    return e * (1.0 / z), m + jnp.log(z)


def _attn_specs(S):
    nb = S // ATTN_BLOCK
    prev = lambda i: jnp.maximum(i - 1, 0)
    blk = lambda w, col, row=(lambda i: i): pl.BlockSpec((ATTN_BLOCK, w), lambda i: (row(i), col))
    in_specs = [pl.BlockSpec(memory_space=pltpu.SMEM),
                blk(512, BLK_Q), blk(128, BLK_K), blk(128, BLK_K, prev), blk(128, BLK_V), blk(128, BLK_V, prev),
                blk(128, 0), blk(128, 0), blk(128, 0, prev), blk(128, 0, prev)]
    return nb, in_specs


def _attn_fwd(pa, cos, sin, sinks):
    S = pa.shape[0]
    nb, in_specs = _attn_specs(S)

    def body(sinks_ref, q_ref, kc_ref, kp_ref, vc_ref, vp_ref, cc_ref, sc_ref, cp_ref, sp_ref, o_ref, lse_ref):
        i = pl.program_id(0)
        lane = lax.broadcasted_iota(jnp.int32, (1, HEAD_PAIR), 1)
        cc, sc = cc_ref[...], sc_ref[...]
        _, rope, _, valid, ks, vs = _attn_common(i, kc_ref[...], kp_ref[...], vc_ref[...], vp_ref[...],
                                                 cc, sc, cp_ref[...], sp_ref[...])
        pair_cols = [slice(HEAD_PAIR * pair, HEAD_PAIR * (pair + 1)) for pair in range(4)]
        qps = [_bf(rope(q_ref[:, cols], cc, sc)) for cols in pair_cols]
        outs, lses = {}, {}

        def head_program(h):
            pair, s = divmod(h, 2)
            j = h // 4
            scores = _dot_nt(qps[pair], ks[j, s])
            yield
            p, lse = _attn_probs(scores, valid, sinks_ref[h])
            outs[h] = _dot(_bf(p), vs[j, s])
            lses[h] = jnp.where(lane == h, lse, 0.0)

        _interleave(head_program(h) for h in range(8))
        for pair, cols in enumerate(pair_cols):
            o_ref[:, cols] = outs[2 * pair] + outs[2 * pair + 1]
        lse_ref[...] = sum((lses[h] for h in range(1, 8)), lses[0])

    return pl.pallas_call(
        body, grid=(nb,), name="attn_fwd", in_specs=in_specs,
        out_specs=[pl.BlockSpec((ATTN_BLOCK, 512), lambda i: (i, 0)), pl.BlockSpec((ATTN_BLOCK, 128), lambda i: (i, 0))],
        out_shape=[jax.ShapeDtypeStruct((S, 512), F32), jax.ShapeDtypeStruct((S, 128), F32)],
        compiler_params=_params(("parallel",)),
    )(sinks, pa, pa, pa, pa, pa, cos, sin, cos, sin)


def _attn_bwd(pa, cos, sin, sinks, dcat, attn, lse):
    S = pa.shape[0]
    nb, in_specs = _attn_specs(S)
    in_specs = in_specs + [pl.BlockSpec((ATTN_BLOCK, 512), lambda i: (i, 0))] * 2 + [pl.BlockSpec((ATTN_BLOCK, 128), lambda i: (i, 0))]

    def body(sinks_ref, q_ref, kc_ref, kp_ref, vc_ref, vp_ref, cc_ref, sc_ref, cp_ref, sp_ref, do_ref, o_ref, lse_ref,
             dq_ref, dk_ref, dv_ref, dsink_ref):
        i = pl.program_id(0)

        @pl.when(i == 0)
        def _():
            dk_ref[...] = jnp.zeros_like(dk_ref)
            dv_ref[...] = jnp.zeros_like(dv_ref)
            dsink_ref[...] = jnp.zeros_like(dsink_ref)

        cc, sc, cp, sp = cc_ref[...], sc_ref[...], cp_ref[...], sp_ref[...]
        slot, rope, unrope, valid, ks, vs = _attn_common(i, kc_ref[...], kp_ref[...], vc_ref[...], vp_ref[...], cc, sc, cp, sp)
        pair_cols = [slice(HEAD_PAIR * pair, HEAD_PAIR * (pair + 1)) for pair in range(4)]
        qps = [_bf(rope(q_ref[:, cols], cc, sc)) for cols in pair_cols]
        dobs = [_bf(do_ref[:, cols]) for cols in pair_cols]
        do_o = [do_ref[:, cols] * o_ref[:, cols] for cols in pair_cols]
        dqs, dks, dvs = {}, {}, {}

        def head_program(h):
            pair, s = divmod(h, 2)
            j = h // 4
            qp, dob = qps[pair], dobs[pair]
            scores = _dot_nt(qp, ks[j, s])
            dp = _dot_nt(dob, vs[j, s])
            yield
            lse_h = lse_ref[:, h:h + 1]
            p = jnp.exp(jnp.where(valid, scores * 0.125, NEG) - lse_h)
            dr = jnp.sum(jnp.where(slot[s], do_o[pair], 0.0), axis=1, keepdims=True)
            ds = _bf(p * (dp - dr) * 0.125)
            dsink_ref[h:h + 1, :] += -jnp.sum(jnp.exp(sinks_ref[h] - lse_h) * dr, axis=0, keepdims=True)
            dqs[h] = _dot(ds, ks[j, s])
            dk_h = _dot_tn(ds, qp)
            dv_h = _dot_tn(_bf(p), dob)
            yield
            dk_h, dv_h = jnp.where(slot[s], dk_h, 0.0), jnp.where(slot[s], dv_h, 0.0)
            if s != j:
                dk_h, dv_h = pltpu.roll(dk_h, 64, 1), pltpu.roll(dv_h, 64, 1)
            dks[h], dvs[h] = dk_h, dv_h

        _interleave(head_program(h) for h in range(8))
        dk2 = sum((dks[h] for h in range(1, 8)), dks[0])
        dv2 = sum((dvs[h] for h in range(1, 8)), dvs[0])
        for pair, cols in enumerate(pair_cols):
            dq_ref[:, cols] = _bf(unrope(dqs[2 * pair] + dqs[2 * pair + 1], cc, sc))
        cur = pl.ds(pl.multiple_of(i * ATTN_BLOCK, ATTN_BLOCK), ATTN_BLOCK)
        dk_ref[cur, :] += unrope(dk2[ATTN_BLOCK:], cc, sc)
        dv_ref[cur, :] += dv2[ATTN_BLOCK:]

        @pl.when(i > 0)
        def _():
            prv = pl.ds(pl.multiple_of((i - 1) * ATTN_BLOCK, ATTN_BLOCK), ATTN_BLOCK)
            dk_ref[prv, :] += unrope(dk2[:ATTN_BLOCK], cp, sp)
            dv_ref[prv, :] += dv2[:ATTN_BLOCK]

    whole = lambda w: pl.BlockSpec((S, w), lambda i: (0, 0))
    return pl.pallas_call(
        body, grid=(nb,), name="attn_bwd", in_specs=in_specs,
        out_specs=[pl.BlockSpec((ATTN_BLOCK, 512), lambda i: (i, BLK_Q)), whole(128), whole(128),
                   pl.BlockSpec((8, 128), lambda i: (0, 0))],
        out_shape=[jax.ShapeDtypeStruct((S, D_IN_PAD), BF16), jax.ShapeDtypeStruct((S, 128), F32),
                   jax.ShapeDtypeStruct((S, 128), F32), jax.ShapeDtypeStruct((8, 128), F32)],
        compiler_params=_params(("arbitrary",)),
    )(sinks, pa, pa, pa, pa, pa, cos, sin, cos, sin, dcat, attn, lse)


CONV_ROWS = 512
CONV_PAD = 8


def _conv_silu(scr, w, r0):
    y = w[3:4, :] * scr[pl.ds(CONV_PAD + r0, CONV_ROWS), :]
    for j in range(DN_CONV - 1):
        y = y + w[j:j + 1, :] * scr[pl.ds(CONV_PAD + r0 - 3 + j, CONV_ROWS), :]
    return y


def _dn_prep_fwd(pd, conv_w):
    S = pd.shape[0]
    assert S % CONV_ROWS == 0

    def body(x_ref, w_ref, o_ref, scr):
        b = pl.program_id(0)
        scr[0:CONV_PAD, :] = jnp.zeros((CONV_PAD, DN_DIM), F32)
        scr[pl.ds(CONV_PAD, S), :] = x_ref[...]
        w = w_ref[...]
        q_scale = jnp.where(b < DN_HEADS, DN_DIM ** -0.5, 1.0)
        for r0 in range(0, S, CONV_ROWS):
            y = _conv_silu(scr, w, r0)
            a = y * _sigmoid(y)
            rs = lax.rsqrt(jnp.sum(a * a, axis=1, keepdims=True) + EPS)
            o_ref[pl.ds(r0, CONV_ROWS), :] = a * jnp.where(b < 2 * DN_HEADS, rs * q_scale, 1.0)

    col = pl.BlockSpec((S, DN_DIM), lambda b: (0, b))
    return pl.pallas_call(
        body, grid=(3 * DN_HEADS,), name="dn_prep_fwd",
        in_specs=[pl.BlockSpec((S, DN_DIM), lambda b: (0, BLK_DN + b)), pl.BlockSpec((DN_CONV, DN_DIM), lambda b: (0, b))],
        out_specs=col,
        out_shape=jax.ShapeDtypeStruct((S, 3 * DN_HEADS * DN_DIM), F32),
        scratch_shapes=[pltpu.VMEM((S + CONV_PAD, DN_DIM), F32)],
        compiler_params=_params(("parallel",)),
    )(pd, conv_w)


def _dn_prep_bwd(pd, conv_w, dqkv, dproj):
    S = pd.shape[0]

    def body(x_ref, w_ref, d_ref, _, dx_ref, dw_ref, scr, dscr):
        b = pl.program_id(0)
        scr[0:CONV_PAD, :] = jnp.zeros((CONV_PAD, DN_DIM), F32)
        scr[pl.ds(CONV_PAD, S), :] = x_ref[...]
        dscr[pl.ds(S, CONV_PAD), :] = jnp.zeros((CONV_PAD, DN_DIM), F32)
        w = w_ref[...]
        q_scale = jnp.where(b < DN_HEADS, DN_DIM ** -0.5, 1.0)
        is_qk = b < 2 * DN_HEADS
        dw = [jnp.zeros((1, DN_DIM), F32) for _ in range(DN_CONV)]
        for r0 in range(0, S, CONV_ROWS):
            y = _conv_silu(scr, w, r0)
            sg = _sigmoid(y)
            a = y * sg
            dout = d_ref[pl.ds(r0, CONV_ROWS), :]
            rs = lax.rsqrt(jnp.sum(a * a, axis=1, keepdims=True) + EPS)
            da_qk = q_scale * rs * (dout - a * (rs * rs) * jnp.sum(dout * a, axis=1, keepdims=True))
            dy = jnp.where(is_qk, da_qk, dout) * (sg * (1.0 + y * (1.0 - sg)))
            dscr[pl.ds(r0, CONV_ROWS), :] = dy
            for j in range(DN_CONV):
                dw[j] = dw[j] + jnp.sum(dy * scr[pl.ds(CONV_PAD + r0 - 3 + j, CONV_ROWS), :], axis=0, keepdims=True)
        for j in range(DN_CONV):
            dw_ref[j:j + 1, :] = dw[j]
        for r0 in range(0, S, CONV_ROWS):
            dx = w[3:4, :] * dscr[pl.ds(r0, CONV_ROWS), :]
            for j in range(DN_CONV - 1):
                dx = dx + w[j:j + 1, :] * dscr[pl.ds(r0 + 3 - j, CONV_ROWS), :]
            dx_ref[pl.ds(r0, CONV_ROWS), :] = _bf(dx)

    col = pl.BlockSpec((S, DN_DIM), lambda b: (0, b))
    proj_col = pl.BlockSpec((S, DN_DIM), lambda b: (0, BLK_DN + b))
    wcol = pl.BlockSpec((DN_CONV, DN_DIM), lambda b: (0, b))
    return pl.pallas_call(
        body, grid=(3 * DN_HEADS,), name="dn_prep_bwd",
        in_specs=[proj_col, wcol, col, pl.BlockSpec(memory_space=pl.ANY)], out_specs=[proj_col, wcol],
        out_shape=[jax.ShapeDtypeStruct(dproj.shape, dproj.dtype), jax.ShapeDtypeStruct((DN_CONV, 3 * DN_HEADS * DN_DIM), F32)],
        scratch_shapes=[pltpu.VMEM((S + CONV_PAD, DN_DIM), F32), pltpu.VMEM((S + CONV_PAD, DN_DIM), F32)],
        input_output_aliases={3: 0},
        compiler_params=_params(("parallel",)),
    )(pd, conv_w, dqkv, dproj)


CPAD = 128
CHUNKS_LOCAL = 4
CHUNKS_SCAN = 4


def _chunk_masks():
    ii = lax.broadcasted_iota(jnp.int32, (DN_CHUNK, CPAD), 0)
    jj = lax.broadcasted_iota(jnp.int32, (DN_CHUNK, CPAD), 1)
    return ii, jj


def _rows_pad(a):
    return jnp.concatenate([a, jnp.zeros_like(a)], axis=0)


def _hi_lo(a):
    hi = _bf(a)
    return hi, _bf(a - hi.astype(F32))


def _double_step(t, p):
    C = DN_CHUNK
    th, tl = _hi_lo(t)
    ph, pl_ = _hi_lo(p)
    r1 = _dot(jnp.concatenate([th, tl, ph, pl_], axis=0), _rows_pad(ph))
    r2 = _dot(jnp.concatenate([th, ph], axis=0), _rows_pad(pl_))
    return t + (r1[:C] + r1[C:2 * C] + r2[:C]), r1[2 * C:3 * C] + r1[3 * C:] + r2[C:]


def _dot3_nt(a, b):
    C = DN_CHUNK
    ah, al = _hi_lo(a)
    bh, bl = _hi_lo(b)
    r1 = _dot_nt(jnp.concatenate([ah, al], axis=0), _rows_pad(bh))
    return r1[:C] + r1[C:] + _dot_nt(ah, _rows_pad(bl))


def _dot3_tn(a, b):
    C = DN_CHUNK
    ah, al = _hi_lo(a)
    bh, bl = _hi_lo(b)
    return _dot_tn(jnp.concatenate([ah, al, ah], axis=0), jnp.concatenate([bh, bh, bl], axis=0))[:C]


def _interleave(programs):
    programs = list(programs)
    while programs:
        alive = []
        for prog in programs:
            try:
                next(prog)
                alive.append(prog)
            except StopIteration:
                pass
        programs = alive


def _col_to_row(col, ii, jj):
    return jnp.sum(jnp.where(ii == jj, col, 0.0), axis=0, keepdims=True)


def _row_to_col(row, ii, jj):
    return jnp.sum(jnp.where(ii == jj, row, 0.0), axis=1, keepdims=True)


def _decay(gc_col, ii, jj):
    diff = gc_col - _col_to_row(gc_col, ii, jj)
    return jnp.where(jj <= ii, jnp.exp(jnp.where(jj <= ii, diff, 0.0)), 0.0)


def _softplus(x):
    return jnp.maximum(x, 0.0) + jnp.log(1.0 + jnp.exp(-jnp.abs(x)))


def _head(h):
    return slice(DN_DIM * h, DN_DIM * (h + 1))


def _dn_chunk_fwd(qkv, pg, a_log, dt_bias):
    S = qkv.shape[0]
    C = DN_CHUNK
    G = CHUNKS_LOCAL
    R = G * C
    steps = S // R

    def body(alog_ref, dtb_ref, qkv_ref, pg_ref, w_ref, u_ref, qg_ref, kd_ref, a_ref, t_ref, gcs_ref):
        ii, jj = _chunk_masks()
        lane = lax.broadcasted_iota(jnp.int32, (1, 128), 1)
        eye = (ii == jj).astype(F32)
        gcs_parts = [[] for _ in range(G)]

        def head_program(chunk, h):
            rows = slice(chunk * C, (chunk + 1) * C)
            q, k, v = qkv_ref[rows, _head(h)], qkv_ref[rows, _head(DN_HEADS + h)], qkv_ref[rows, _head(2 * DN_HEADS + h)]
            beta = _sigmoid(pg_ref[rows, h:h + 1])
            g_col = -jnp.exp(alog_ref[h]) * _softplus(pg_ref[rows, DN_HEADS + h:DN_HEADS + h + 1] + dtb_ref[h])
            g_row = _col_to_row(g_col, ii, jj)
            gc_col = jnp.sum(jnp.where(jj <= ii, g_row, 0.0), axis=1, keepdims=True)
            dec = _decay(gc_col, ii, jj)
            eg = jnp.exp(gc_col)
            kb, vb = k * beta, v * beta
            k_rows = _rows_pad(_bf(k))
            kk = _dot_nt(_bf(kb), k_rows)
            qk = _dot_nt(_bf(q), k_rows)
            yield
            t, pw = eye, -jnp.where(jj < ii, kk * dec, 0.0)
            for _ in range(6):
                t, pw = _double_step(t, pw)
                yield
            tb = _bf(t)
            u_ref[rows, _head(h)] = _dot(tb, _rows_pad(_bf(vb)))
            w_ref[rows, _head(h)] = _bf(_dot(tb, _rows_pad(_bf(kb * eg))))
            a_ref[h, rows] = _bf(qk * dec)
            t_ref[h, rows] = t
            qg_ref[rows, _head(h)] = _bf(q * eg)
            kd_ref[rows, _head(h)] = _bf(k * jnp.exp(gc_col[C - 1:C, :] - gc_col))
            gcs_parts[chunk].append(jnp.where(lane == h, gc_col, 0.0) + jnp.where(lane == DN_HEADS + h, beta, 0.0)
                                    + jnp.where(lane == 2 * DN_HEADS + h, g_col, 0.0))

        _interleave(head_program(chunk, h) for chunk in range(G) for h in range(DN_HEADS))
        for chunk in range(G):
            gcs_ref[chunk * C:(chunk + 1) * C, :] = sum(gcs_parts[chunk][1:], gcs_parts[chunk][0])

    smem = pl.BlockSpec(memory_space=pltpu.SMEM)
    wide = pl.BlockSpec((R, 512), lambda n: (n, 0))
    sq = pl.BlockSpec((DN_HEADS, R, CPAD), lambda n: (0, n, 0))
    narrow = pl.BlockSpec((R, 128), lambda n: (n, 0))
    f = lambda *shp: jax.ShapeDtypeStruct(shp, F32)
    b = lambda *shp: jax.ShapeDtypeStruct(shp, BF16)
    return pl.pallas_call(
        body, grid=(steps,), name="dn_chunk_fwd",
        in_specs=[smem, smem, pl.BlockSpec((R, 1536), lambda n: (n, 0)), pl.BlockSpec((R, 128), lambda n: (n, BLK_G))],
        out_specs=[wide, wide, wide, wide, sq, sq, narrow],
        out_shape=[b(S, 512), f(S, 512), b(S, 512), b(S, 512), b(DN_HEADS, S, CPAD), f(DN_HEADS, S, CPAD), f(S, 128)],
        compiler_params=_params(("parallel",)),
    )(a_log, dt_bias, qkv, pg)


def _gated_norm(o, z, gn):
    r, oh = _rms_stats(o)
    return oh * gn * (z * _sigmoid(z))


def _dn_scan_fwd(w, u, qg, kd, a, gcs, pz, gn):
    S = w.shape[0]
    C = DN_CHUNK
    nc = S // C
    G = CHUNKS_SCAN
    R = G * C

    def body(w_ref, u_ref, qg_ref, kd_ref, a_ref, gcs_ref, z_ref, gn_ref, o_ref, vn_ref, sst_ref, out_ref, state):
        @pl.when(pl.program_id(0) == 0)
        def _():
            state[...] = jnp.zeros_like(state)

        def head_program(chunk, h):
            hs = _head(h)
            rows = slice(chunk * C, (chunk + 1) * C)
            s_in = state[h]
            sst_ref[chunk, h] = s_in
            sb = _bf(s_in)
            w_s = _dot(w_ref[rows, hs], sb)
            q_s = _dot(qg_ref[rows, hs], sb)
            yield
            vn = u_ref[rows, hs] - w_s
            vnb = _bf(vn)
            o = q_s + _dot(a_ref[h, rows], _rows_pad(vnb))
            k_v = _dot_tn(kd_ref[rows, hs], vnb)
            yield
            state[h] = s_in * jnp.exp(gcs_ref[(chunk + 1) * C - 1:(chunk + 1) * C, h:h + 1]) + k_v
            o_ref[rows, hs] = o
            vn_ref[rows, hs] = vn
            out_ref[rows, hs] = _gated_norm(o, z_ref[rows, hs], gn_ref[...])

        for chunk in range(G):
            _interleave(head_program(chunk, h) for h in range(DN_HEADS))

    wide = pl.BlockSpec((R, 512), lambda n: (n, 0))
    f = lambda *shp: jax.ShapeDtypeStruct(shp, F32)
    return pl.pallas_call(
        body, grid=(nc // G,), name="dn_scan_fwd",
        in_specs=[wide, wide, wide, wide, pl.BlockSpec((DN_HEADS, R, CPAD), lambda n: (0, n, 0)),
                  pl.BlockSpec((R, 128), lambda n: (n, 0)), pl.BlockSpec((R, 512), lambda n: (n, BLK_Z)),
                  pl.BlockSpec((1, DN_DIM), lambda n: (0, 0))],
        out_specs=[wide, wide, pl.BlockSpec((G, DN_HEADS, DN_DIM, DN_DIM), lambda n: (n, 0, 0, 0)), wide],
        out_shape=[f(S, 512), f(S, 512), f(nc, DN_HEADS, DN_DIM, DN_DIM), f(S, 512)],
        scratch_shapes=[pltpu.VMEM((DN_HEADS, DN_DIM, DN_DIM), F32)],
        compiler_params=_params(("arbitrary",)),
    )(w, u, qg, kd, a, gcs, pz, gn)


def _dn_scan_bwd(dcat, o, pz, gn, sst, vnew, w, qg, kd, a, gcs, dproj):
    S = o.shape[0]
    C = DN_CHUNK
    G = CHUNKS_SCAN
    R = G * C
    steps = S // R

    def body(dy_ref, o_ref, z_ref, gn_ref, sst_ref, vn_ref, w_ref, qg_ref, kd_ref, a_ref, gcs_ref, _,
             du_ref, dw_ref, dqg_ref, dkd_ref, da_ref, dz_ref, dsc_ref, dgn_ref, dstate):
        @pl.when(pl.program_id(0) == 0)
        def _():
            dstate[...] = jnp.zeros_like(dstate)
            dgn_ref[...] = jnp.zeros_like(dgn_ref)

        gn_ = gn_ref[...]
        lane = lax.broadcasted_iota(jnp.int32, (C, 128), 1)
        row = lax.broadcasted_iota(jnp.int32, (C, 128), 0)
        dgn_parts = []

        def head_program(chunk, h, dsc_parts):
            hs = _head(h)
            rows = slice(chunk * C, (chunk + 1) * C)
            ov, z, dout = o_ref[rows, hs], z_ref[rows, hs], dy_ref[rows, hs]
            r, oh = _rms_stats(ov)
            sg = _sigmoid(z)
            don = dout * (z * sg)
            dz_ref[rows, hs] = _bf(dout * (oh * gn_) * (sg * (1.0 + z * (1.0 - sg))))
            dgn_parts.append(jnp.sum(don * oh, axis=0, keepdims=True))
            dn = don * gn_
            do = _bf(r * (dn - oh * jnp.mean(dn * oh, axis=-1, keepdims=True)))
            s_in = sst_ref[chunk, h]
            sb = _bf(s_in)
            ds_out = dstate[h]
            dsb = _bf(ds_out)
            vnb = _bf(vn_ref[rows, hs])
            wb, qgb, kdb, ab = w_ref[rows, hs], qg_ref[rows, hs], kd_ref[rows, hs], a_ref[h, rows]
            dvn = _dot_tn(ab, do)[:C] + _dot(kdb, dsb)
            da_ref[h, rows] = _dot_nt(do, _rows_pad(vnb))
            dqg_ref[rows, hs] = _dot_nt(do, sb)
            dkd_ref[rows, hs] = _dot_nt(vnb, dsb)
            q_do = _dot_tn(qgb, do)
            yield
            dvnb = _bf(dvn)
            dw_ref[rows, hs] = _bf(-_dot_nt(dvnb, sb))
            w_dvn = _dot_tn(wb, dvnb)
            du_ref[rows, hs] = dvnb
            yield
            d_last = jnp.exp(gcs_ref[(chunk + 1) * C - 1:(chunk + 1) * C, h:h + 1])
            dd = jnp.sum(jnp.sum(ds_out * s_in, axis=1, keepdims=True), axis=0, keepdims=True)
            dsc_parts.append(jnp.where((lane == h) & (row == C - 1), dd * d_last, 0.0))
            dstate[h] = ds_out * d_last + q_do - w_dvn

        for chunk in reversed(range(G)):
            dsc_parts = []
            _interleave(head_program(chunk, h, dsc_parts) for h in range(DN_HEADS))
            dsc_ref[chunk * C:(chunk + 1) * C, :] = sum(dsc_parts[1:], dsc_parts[0])
        dgn_ref[...] += sum(dgn_parts[1:], dgn_parts[0])

    rev = lambda n: steps - 1 - n
    wide = pl.BlockSpec((R, 512), lambda n: (rev(n), 0))
    z_spec = pl.BlockSpec((R, 512), lambda n: (rev(n), BLK_Z))
    sq = pl.BlockSpec((DN_HEADS, R, CPAD), lambda n: (0, rev(n), 0))
    narrow = pl.BlockSpec((R, 128), lambda n: (rev(n), 0))
    gn_spec = pl.BlockSpec((1, DN_DIM), lambda n: (0, 0))
    f = lambda *shp: jax.ShapeDtypeStruct(shp, F32)
    b = lambda *shp: jax.ShapeDtypeStruct(shp, BF16)
    return pl.pallas_call(
        body, grid=(steps,), name="dn_scan_bwd",
        in_specs=[pl.BlockSpec((R, 512), lambda n: (rev(n), 1)), wide, z_spec, gn_spec,
                  pl.BlockSpec((G, DN_HEADS, DN_DIM, DN_DIM), lambda n: (rev(n), 0, 0, 0)),
                  wide, wide, wide, wide, sq, narrow, pl.BlockSpec(memory_space=pl.ANY)],
        out_specs=[wide, wide, wide, wide, sq, z_spec, narrow, gn_spec],
        out_shape=[b(S, 512), b(S, 512), f(S, 512), f(S, 512), f(DN_HEADS, S, CPAD),
                   jax.ShapeDtypeStruct(dproj.shape, dproj.dtype), f(S, 128), f(1, DN_DIM)],
        scratch_shapes=[pltpu.VMEM((DN_HEADS, DN_DIM, DN_DIM), F32)],
        input_output_aliases={11: 5},
        compiler_params=_params(("arbitrary",)),
    )(dcat, o, pz, gn, sst, vnew, w, qg, kd, a, gcs, dproj)


def _dn_chunk_bwd(qkv, pg, t_inv, gcs, du, dw, dqg, dkd, da, dsc, a_log, dt_bias, dproj):
    S = qkv.shape[0]
    C = DN_CHUNK
    G = CHUNKS_LOCAL
    R = G * C

    def body(alog_ref, dtb_ref, qkv_ref, pg_ref, t_ref, gcs_ref, du_ref, dw_ref, dqg_ref, dkd_ref, da_ref, dsc_ref, _,
             dqkv_ref, dpg_ref, acc_ref):
        @pl.when(pl.program_id(0) == 0)
        def _():
            acc_ref[...] = jnp.zeros_like(acc_ref)

        ii, jj = _chunk_masks()
        lane = lax.broadcasted_iota(jnp.int32, (1, 128), 1)
        row8 = lax.broadcasted_iota(jnp.int32, (8, 128), 0)
        lane8 = lax.broadcasted_iota(jnp.int32, (8, 128), 1)
        rowc = lax.broadcasted_iota(jnp.int32, (C, 1), 0)
        tril, strict = jj <= ii, jj < ii
        dpg_parts, acc_parts = [[] for _ in range(G)], []

        def head_program(chunk, h):
            rows = slice(chunk * C, (chunk + 1) * C)
            q, k, v = qkv_ref[rows, _head(h)], qkv_ref[rows, _head(DN_HEADS + h)], qkv_ref[rows, _head(2 * DN_HEADS + h)]
            gc_col, beta, g_col = gcs_ref[rows, h:h + 1], gcs_ref[rows, DN_HEADS + h:DN_HEADS + h + 1], \
                gcs_ref[rows, 2 * DN_HEADS + h:2 * DN_HEADS + h + 1]
            dec = _decay(gc_col, ii, jj)
            eg = jnp.exp(gc_col)
            g_last = gc_col[C - 1:C, :]
            ek = jnp.exp(g_last - gc_col)
            kb, vb = k * beta, v * beta
            kbg = kb * eg
            qb, kbb = _bf(q), _bf(kb)
            k_rows = _rows_pad(_bf(k))
            t = t_ref[h, rows]
            tb = _bf(t)
            dub, dwb = du_ref[rows, _head(h)], dw_ref[rows, _head(h)]
            dqg_, dkd_ = dqg_ref[rows, _head(h)], dkd_ref[rows, _head(h)]
            dt = _dot_nt(dub, _rows_pad(_bf(vb))) + _dot_nt(dwb, _rows_pad(_bf(kbg)))
            t_du_dw = _dot_tn(tb, jnp.concatenate([dub, dwb], axis=1))
            dvb, dkbg = t_du_dw[:C, :DN_DIM], t_du_dw[:C, DN_DIM:]
            kk = _dot_nt(kbb, k_rows)
            qk = _dot_nt(qb, k_rows)
            yield
            dt_t = _dot3_nt(dt, t)
            yield
            dl = -_dot3_tn(t, dt_t)
            yield
            dm = jnp.where(strict, dl * dec, 0.0)
            dqk = jnp.where(tril, da_ref[h, rows] * dec, 0.0)
            gmat = dm * kk + dqk * qk
            dgc = jnp.sum(gmat, axis=1, keepdims=True) - _row_to_col(jnp.sum(gmat, axis=0, keepdims=True), ii, jj)
            dmb, dqkb = _bf(dm), _bf(dqk)
            dkb = _dot(dmb, k_rows) + dkbg * eg
            dk = _dot_tn(jnp.concatenate([dmb, dqkb], axis=0), jnp.concatenate([kbb, qb], axis=0))[:C] + dkd_ * ek
            dq = _dot(dqkb, k_rows) + dqg_ * eg
            yield
            tk = jnp.sum(dkd_ * k * ek, axis=1, keepdims=True)
            dgc = dgc + jnp.sum(dqg_ * q * eg, axis=1, keepdims=True) - tk + jnp.sum(dkbg * kbg, axis=1, keepdims=True)
            dgl = jnp.sum(tk, axis=0, keepdims=True) + dsc_ref[(chunk + 1) * C - 1:(chunk + 1) * C, h:h + 1]
            dgc = dgc + jnp.where(rowc == C - 1, dgl, 0.0)
            dk = dk + dkb * beta
            dbeta = jnp.sum(dkb * k, axis=1, keepdims=True) + jnp.sum(dvb * v, axis=1, keepdims=True)
            dqkv_ref[rows, _head(h)] = dq
            dqkv_ref[rows, _head(DN_HEADS + h)] = dk
            dqkv_ref[rows, _head(2 * DN_HEADS + h)] = dvb * beta
            dg_col = jnp.sum(jnp.where(jj >= ii, _col_to_row(dgc, ii, jj), 0.0), axis=1, keepdims=True)
            db = dbeta * beta * (1.0 - beta)
            da_in = dg_col * (-jnp.exp(alog_ref[h])) * _sigmoid(pg_ref[rows, DN_HEADS + h:DN_HEADS + h + 1] + dtb_ref[h])
            dpg_parts[chunk].append(jnp.where(lane == h, db, 0.0) + jnp.where(lane == DN_HEADS + h, da_in, 0.0))
            acc_parts.append(jnp.where((row8 == 0) & (lane8 == h), jnp.sum(dg_col * g_col, axis=0, keepdims=True), 0.0)
                             + jnp.where((row8 == 1) & (lane8 == h), jnp.sum(da_in, axis=0, keepdims=True), 0.0))

        _interleave(head_program(chunk, h) for chunk in range(G) for h in range(DN_HEADS))
        for chunk in range(G):
            dpg = sum(dpg_parts[chunk][1:], dpg_parts[chunk][0])
            dpg_ref[chunk * C:(chunk + 1) * C, :] = _bf(jnp.concatenate([dpg, jnp.zeros_like(dpg)], axis=1))
        acc_ref[...] += sum(acc_parts[1:], acc_parts[0])

    smem = pl.BlockSpec(memory_space=pltpu.SMEM)
    wide = pl.BlockSpec((R, 512), lambda n: (n, 0))
    sq = pl.BlockSpec((DN_HEADS, R, CPAD), lambda n: (0, n, 0))
    narrow = pl.BlockSpec((R, 128), lambda n: (n, 0))
    qkv_spec = pl.BlockSpec((R, 1536), lambda n: (n, 0))
    f = lambda *shp: jax.ShapeDtypeStruct(shp, F32)
    return pl.pallas_call(
        body, grid=(S // R,), name="dn_chunk_bwd",
        in_specs=[smem, smem, qkv_spec, pl.BlockSpec((R, 128), lambda n: (n, BLK_G)), sq, narrow, wide, wide, wide, wide, sq,
                  narrow, pl.BlockSpec(memory_space=pl.ANY)],
        out_specs=[qkv_spec, pl.BlockSpec((R, 256), lambda n: (n, BLK_G_PAD)), pl.BlockSpec((8, 128), lambda n: (0, 0))],
        out_shape=[f(S, 1536), jax.ShapeDtypeStruct(dproj.shape, dproj.dtype), f(8, 128)],
        input_output_aliases={12: 1},
        compiler_params=_params(("arbitrary",)),
    )(a_log, dt_bias, qkv, pg, t_inv, gcs, du, dw, dqg, dkd, da, dsc, dproj)


def _fill_kv(dk, dv, dproj):
    S = dk.shape[0]
    tm = min(512, S)

    def body(dk_ref, dv_ref, _, o_ref):
        o_ref[...] = _bf(jnp.concatenate([dk_ref[...], dv_ref[...]], axis=1))

    tile = pl.BlockSpec((tm, 128), lambda i: (i, 0))
    return pl.pallas_call(
        body, grid=(S // tm,), name="fill_kv",
        in_specs=[tile, tile, pl.BlockSpec(memory_space=pl.ANY)],
        out_specs=pl.BlockSpec((tm, 256), lambda i: (i, BLK_KV)),
        out_shape=jax.ShapeDtypeStruct(dproj.shape, dproj.dtype),
        input_output_aliases={2: 0},
        compiler_params=_params(("parallel",)),
    )(dk, dv, dproj)


def _w_in_to_internal(wt):
    return jnp.concatenate([wt[0:512], wt[2304:2816], wt[768:2304], wt[512:768], wt[2816:2824],
                            jnp.zeros((D_IN_PAD - D_IN, wt.shape[1]), wt.dtype)], axis=0)


def _w_in_from_internal(gt):
    return jnp.concatenate([gt[0:512], gt[2560:2816], gt[1024:2560], gt[512:1024], gt[2816:2824]], axis=0)


def _local_step(x, p, target, wts, first_weights, other_weights, ship_early):
    S = x.shape[0]
    cos, sin = _rope_tables(S)
    sinks, a_log, dt_bias = wts["sinks"].reshape(8), wts["a_log"].reshape(4), wts["dt_bias"].reshape(4)
    gn = wts["dn_norm"].reshape(1, DN_DIM)
    add = lambda acc, res: (acc + res,)

    u = _rmsnorm_fwd(x, wts["norm_mix"], "norm_mix_fwd")
    w_in_t, conv_w = first_weights(u)
    proj, = _mm(u, w_in_t, form="nt", name="in_proj", out_dtypes=[F32], tn=512)
    attn, lse = _attn_fwd(proj, cos, sin, sinks)
    qkv = _dn_prep_fwd(proj, conv_w)
    cw, cu, cqg, ckd, ca, ct, gcs = _dn_chunk_fwd(qkv, proj, a_log, dt_bias)
    o, vnew, sst, dn_out = _dn_scan_fwd(cw, cu, cqg, ckd, ca, gcs, proj, gn)
    w_o, = other_weights(("w_o",), dn_out)
    h1, = _mm([attn, dn_out], w_o, form="nn", name="out_proj", out_dtypes=[F32], tn=512, epi=add, extra=[x])

    def relu2(acc):
        r = jnp.maximum(acc, 0.0)
        return r * r, r

    w_up, = other_weights(("w_up",), h1)
    hid, relu, m = _mm(h1, w_up, form="nn", name="mlp_up", out_dtypes=[BF16, BF16], tn=512, epi=relu2, norm=wts["norm_mlp"])
    w_down, = other_weights(("w_down",), hid)
    h2, = _mm(hid, w_down, form="nn", name="mlp_down", out_dtypes=[F32], tn=512, epi=add, extra=[h1])
    w_pg, w_pp = other_weights(("w_ple_gate", "w_ple_proj"), h2)
    n3, dh2, dgl, dpp, loss, d_norm_final, d_norm_ple = _ple_and_loss(h2, p, target, w_pg, w_pp, wts["norm_ple"],
                                                                     wts["norm_final"].reshape(1, D_MODEL))
    g = {"norm_final": d_norm_final, "norm_ple": d_norm_ple}
    early = {}
    d_act, = _mm(dh2, w_down, form="nt", name="d_hidden", out_dtypes=[BF16], tn=512,
                 epi=lambda acc, r: (acc * (2.0 * r.astype(F32)),), extra=[relu])
    early["w_down"] = _mm_tn(hid, dh2, name="d_w_down", tm=512, tn=1024, out_dtype=BF16).reshape(N_DEV, 512, 1024)
    early["w_up"] = _mm_tn(m, d_act, name="d_w_up", tm=1024, tn=512, out_dtype=BF16, column_shards=True)
    token = ship_early(early)
    dh1, g["norm_mlp"], dcat = _mm(d_act, w_up, form="nt", name="d_m", out_dtypes=[F32], tn=512, after=token,
                                   norm_bwd=(h1, wts["norm_mlp"], dh2), then_nt=w_o)
    d_w_o = jnp.concatenate([_mm_tn(attn, dh1, name="d_w_o_attn", tm=512, tn=512, out_dtype=BF16),
                             _mm_tn(dn_out, dh1, name="d_w_o_dn", tm=512, tn=512, out_dtype=BF16)], axis=0)
    token = ship_early({"w_o": d_w_o.reshape(N_DEV, 128, 1024)})
    dproj, dk, dv, dsinks = _attn_bwd(proj, cos, sin, sinks + token[0, 0], dcat, attn, lse)
    g["sinks"] = dsinks[:, 0].reshape(1, 8)
    du_, dw_, dqg, dkd, da, dproj, dsc, g["dn_norm"] = _dn_scan_bwd(dcat, o, proj, gn, sst, vnew, cw, cqg, ckd, ca, gcs, dproj)
    dqkv, dproj, gate_acc = _dn_chunk_bwd(qkv, proj, ct, gcs, du_, dw_, dqg, dkd, da, dsc, a_log, dt_bias, dproj)
    g["a_log"], g["dt_bias"] = gate_acc[0:1, 0:4], gate_acc[1:2, 0:4]
    dproj, g["conv_w"] = _dn_prep_bwd(proj, conv_w, dqkv, dproj)
    dproj = _fill_kv(dk, dv, dproj)
    token = ship_early({"w_in": _mm_tn(dproj, u, name="d_w_in", tm=512, tn=1024, out_dtype=BF16)})
    grad_x, g["norm_mix"] = _mm(dproj, w_in_t, form="nn", name="d_u", out_dtypes=[F32], tn=512, after=token,
                                norm_bwd=(x, wts["norm_mix"], dh1))
    d_w_pg = _mm_tn(n3, dgl, name="d_w_ple_gate", tm=512, tn=1024, out_dtype=BF16, after=grad_x)
    d_w_pp = _mm_tn(p, dpp, name="d_w_ple_proj", tm=256, tn=128, out_dtype=BF16, column_shards=True, after=grad_x)
    ship_early({"w_ple_gate": d_w_pg.reshape(N_DEV, 128, 1024), "w_ple_proj": d_w_pp})
    return loss, grad_x, g


def _peer(k):
    x, y, c = lax.axis_index("x"), lax.axis_index("y"), lax.axis_index("c")
    px = 1 - x if k & 4 else x
    py = 1 - y if k & 2 else y
    pc = 1 - c if k & 1 else c
    return (px, py, pc), 4 * px + 2 * py + pc


def _exchange(srcs, name, gather):
    n = len(srcs)
    gathers = list(gather) if isinstance(gather, (list, tuple)) else [gather] * n
    shapes = [(N_DEV,) + s.shape if gt else s.shape for s, gt in zip(srcs, gathers)]

    def body(*refs):
        src_refs, out_refs = refs[:n], refs[n:2 * n]
        send_sems, recv_sems, local_sems = refs[2 * n:]
        _, me = _peer(0)
        piece = lambda a, d: src_refs[a] if gathers[a] else src_refs[a].at[d]
        local = [pltpu.make_async_copy(piece(a, me), out_refs[a].at[me], local_sems.at[a]) for a in range(n)]
        for cp in local:
            cp.start()
        copies = []
        for a in range(n):
            for k in range(1, N_DEV):
                dev, idx = _peer(k)
                cp = pltpu.make_async_remote_copy(src_ref=piece(a, idx), dst_ref=out_refs[a].at[me],
                                                  send_sem=send_sems.at[a, k - 1], recv_sem=recv_sems.at[a, k - 1],
                                                  device_id=dev, device_id_type=MESH)
                cp.start()
                copies.append(cp)
        for cp in copies:
            cp.wait_recv()
        for cp in copies:
            cp.wait_send()
        for cp in local:
            cp.wait()

    anywhere = pl.BlockSpec(memory_space=pl.ANY)
    return pl.pallas_call(
        body, name=name, in_specs=[anywhere] * n, out_specs=[anywhere] * n,
        out_shape=[jax.ShapeDtypeStruct(shp, s.dtype) for shp, s in zip(shapes, srcs)],
        scratch_shapes=[pltpu.SemaphoreType.DMA((n, N_DEV - 1)), pltpu.SemaphoreType.DMA((n, N_DEV - 1)),
                        pltpu.SemaphoreType.DMA((n,))],
    )(*srcs)


_HBM = pl.BlockSpec(memory_space=pltpu.HBM)
_SEM = pl.BlockSpec(memory_space=pltpu.SEMAPHORE)
_EFFECT = pltpu.SideEffectType.DATAFLOW_SIDE_EFFECTING


def _split_copies(src_refs, land_refs, send_sems, recv_sems, modes, which=None):
    _, me = _peer(0)
    copies = []
    which = range(len(src_refs)) if which is None else which
    for a, src, land in zip(which, src_refs, land_refs):
        if modes[a] == "columns":
            n_cols = src.shape[1]
            dst = land.at[:, pl.ds(pl.multiple_of(me * n_cols, n_cols), n_cols)]
        else:
            dst = land.at[me]
        for k in range(1, N_DEV):
            dev, idx = _peer(k)
            sem = a * (N_DEV - 1) + k - 1
            copies.append(pltpu.make_async_remote_copy(
                src_ref=src.at[idx] if modes[a] == "pieces" else src, dst_ref=dst, send_sem=send_sems.at[sem],
                recv_sem=recv_sems.at[sem], device_id=dev, device_id_type=MESH))
    return copies


def _exchange_start(srcs, name, modes):
    n = len(srcs)
    modes = [modes] * n if isinstance(modes, str) else list(modes)
    me = 4 * lax.axis_index("x") + 2 * lax.axis_index("y") + lax.axis_index("c")
    lands = []
    for s, mode in zip(srcs, modes):
        if mode == "columns":
            empty = lax.empty((s.shape[0], N_DEV * s.shape[1]), s.dtype)
            lands.append(lax.dynamic_update_slice(empty, s, (0, me * s.shape[1])))
        else:
            own = s if mode == "slots" else lax.dynamic_index_in_dim(s, me, 0, keepdims=False)
            shape = (N_DEV,) + s.shape if mode == "slots" else s.shape
            lands.append(lax.dynamic_update_index_in_dim(lax.empty(shape, s.dtype), own, me, 0))

    def body(*refs):
        src_refs, land_refs = refs[:n], refs[n:2 * n]
        send_sems, recv_sems = refs[2 * n], refs[2 * n + 1]
        for cp in _split_copies(src_refs, land_refs, send_sems, recv_sems, modes):
            cp.start()
        refs[-1][...] = jnp.zeros_like(refs[-1])

    both = list(srcs) + lands
    sems = pltpu.SemaphoreType.DMA((n * (N_DEV - 1),))
    out = pl.pallas_call(
        body, name=name,
        out_shape=(sems, sems, *[pltpu.HBM(t.shape, t.dtype) for t in both], jax.ShapeDtypeStruct((8, 128), F32)),
        in_specs=[_HBM] * (2 * n), out_specs=(_SEM, _SEM, *[_HBM] * (2 * n), pl.BlockSpec(memory_space=pltpu.VMEM)),
        input_output_aliases={i: 2 + i for i in range(2 * n)},
        compiler_params=pltpu.CompilerParams(has_side_effects=_EFFECT),
    )(*[pltpu.with_memory_space_constraint(t, pltpu.HBM) for t in both])
    return (n, modes, out[:-1]), out[-1]


def _exchange_wait(handle, after, name, which=None):
    n_all, modes, (send_sems, recv_sems, *both_all) = handle
    which = list(range(n_all)) if which is None else list(which)
    n = len(which)
    both = [both_all[a] for a in which] + [both_all[n_all + a] for a in which]

    def body(*refs):
        src_refs, land_refs = refs[:n], refs[n:2 * n]
        for cp in _split_copies(src_refs, land_refs, refs[2 * n], refs[2 * n + 1], modes, which):
            cp.wait_send()
            cp.wait_recv()

    out = pl.pallas_call(
        body, name=name, out_shape=tuple(pltpu.HBM(t.shape, t.dtype) for t in both),
        in_specs=[_HBM] * (2 * n) + [_SEM, _SEM, pl.BlockSpec(memory_space=pl.ANY)], out_specs=tuple([_HBM] * (2 * n)),
        input_output_aliases={i: i for i in range(2 * n)},
        compiler_params=pltpu.CompilerParams(has_side_effects=_EFFECT),
    )(*both, send_sems, recv_sems, after)
    return list(out[n:])


def _adam_update(g, w, m, v):
    nm = ADAM_B1 * m + (1.0 - ADAM_B1) * g
    nv = ADAM_B2 * v + (1.0 - ADAM_B2) * (g * g)
    m_hat = nm / (1.0 - ADAM_B1 ** ADAM_STEP)
    v_hat = nv / (1.0 - ADAM_B2 ** ADAM_STEP)
    return -ADAM_LR * (m_hat / (jnp.sqrt(v_hat) + ADAM_EPS) + ADAM_WD * w), nm, nv


def _adamw(parts, w, m, v, name):
    n, R, W = parts.shape
    tm = 128 if R % 128 == 0 else R

    def body(p_ref, w_ref, m_ref, v_ref, g_ref, d_ref, nm_ref, nv_ref):
        g = p_ref[0].astype(F32)
        for s in range(1, n):
            g = g + p_ref[s].astype(F32)
        g_ref[...] = g
        d_ref[...], nm_ref[...], nv_ref[...] = _adam_update(g, w_ref[...], m_ref[...], v_ref[...])

    tile = pl.BlockSpec((tm, W), lambda i: (i, 0))
    return pl.pallas_call(
        body, grid=(R // tm,), name=name,
        in_specs=[pl.BlockSpec((n, tm, W), lambda i: (0, i, 0)), tile, tile, tile],
        out_specs=[tile] * 4, out_shape=[jax.ShapeDtypeStruct((R, W), F32)] * 4,
        compiler_params=_params(("parallel",)),
    )(parts, w, m, v)


_MATRICES = ("w_in", "w_o", "w_up", "w_down", "w_ple_gate", "w_ple_proj")


_OTHERS = ("w_o", "w_up", "w_down", "w_ple_gate", "w_ple_proj")
_OTHER_MODES = {"w_o": "slots", "w_up": "slots", "w_down": "slots", "w_ple_gate": "slots", "w_ple_proj": "columns"}


_VECTORS = ("norm_mix", "norm_mlp", "norm_ple", "norm_final", "a_log", "dt_bias", "sinks", "dn_norm")
_SMALL_ROWS, _LOSS_ROW, _CONV_ROW = 16, 8, 9


def _pack_small(vectors, loss, conv):
    def body(*refs):
        out = refs[-1]
        out[...] = jnp.zeros_like(out)
        for r, ref in enumerate(refs[:len(_VECTORS)]):
            out[r:r + 1, 0:ref.shape[1]] = ref[...]
        out[_LOSS_ROW:_LOSS_ROW + 1, 0:128] = refs[len(_VECTORS)][...]
        out[_CONV_ROW:_CONV_ROW + 6, :] = refs[len(_VECTORS) + 1][...]

    return pl.pallas_call(body, name="pack_small", out_shape=jax.ShapeDtypeStruct((_SMALL_ROWS, 1024), F32))(*vectors, loss, conv)


def _sum_slots(parts):
    def body(p_ref, o_ref):
        acc = p_ref[0]
        for s in range(1, parts.shape[0]):
            acc = acc + p_ref[s]
        o_ref[...] = acc

    return pl.pallas_call(body, name="sum_small", out_shape=jax.ShapeDtypeStruct(parts.shape[1:], parts.dtype))(parts)


def _adamw_vectors(summed, conv_g, wmv):
    names = _VECTORS + ("conv_w",)
    flat = [a for triple in wmv for a in triple]

    def body(*refs):
        sum_ref, conv_ref = refs[0], refs[1]
        ins, outs = refs[2:2 + len(flat)], refs[2 + len(flat):]
        for i in range(len(names)):
            w_ref, m_ref, v_ref = ins[3 * i:3 * i + 3]
            g = conv_ref[...] if i == len(_VECTORS) else sum_ref[i:i + 1, 0:w_ref.shape[1]]
            outs[4 * i][...] = g
            outs[4 * i + 1][...], outs[4 * i + 2][...], outs[4 * i + 3][...] = _adam_update(g, w_ref[...], m_ref[...], v_ref[...])

    out_shape = [jax.ShapeDtypeStruct(t[0].shape, F32) for t in wmv for _ in range(4)]
    res = pl.pallas_call(body, name="adamw_vectors", out_shape=out_shape)(summed, conv_g, *flat)
    return {n: res[4 * i:4 * i + 4] for i, n in enumerate(names)}


_ORDER = ("norm_mix", "w_in", "conv_w", "a_log", "dt_bias", "dn_norm", "sinks", "w_o", "norm_mlp", "w_up", "w_down",
          "norm_ple", "w_ple_gate", "w_ple_proj", "norm_final")


def kernel(x, p, norm_mix, w_in, conv_w, a_log, dt_bias, dn_norm, sinks, w_o, norm_mlp, w_up, w_down, norm_ple, w_ple_gate, w_ple_proj, norm_final, loss_target, m_norm_mix, m_w_in, m_conv_w, m_a_log, m_dt_bias, m_dn_norm, m_sinks, m_w_o, m_norm_mlp, m_w_up, m_w_down, m_norm_ple, m_w_ple_gate, m_w_ple_proj, m_norm_final, v_norm_mix, v_w_in, v_conv_w, v_a_log, v_dt_bias, v_dn_norm, v_sinks, v_w_o, v_norm_mlp, v_w_up, v_w_down, v_norm_ple, v_w_ple_gate, v_w_ple_proj, v_norm_final):
    w = dict(norm_mix=norm_mix, w_in=w_in[0], conv_w=conv_w[0], a_log=a_log, dt_bias=dt_bias, dn_norm=dn_norm, sinks=sinks,
             w_o=w_o[0], norm_mlp=norm_mlp, w_up=w_up[0], w_down=w_down[0], norm_ple=norm_ple, w_ple_gate=w_ple_gate[0],
             w_ple_proj=w_ple_proj[0], norm_final=norm_final)
    m = dict(norm_mix=m_norm_mix, w_in=m_w_in[0], conv_w=m_conv_w[0], a_log=m_a_log, dt_bias=m_dt_bias, dn_norm=m_dn_norm,
             sinks=m_sinks, w_o=m_w_o[0], norm_mlp=m_norm_mlp, w_up=m_w_up[0], w_down=m_w_down[0], norm_ple=m_norm_ple,
             w_ple_gate=m_w_ple_gate[0], w_ple_proj=m_w_ple_proj[0], norm_final=m_norm_final)
    v = dict(norm_mix=v_norm_mix, w_in=v_w_in[0], conv_w=v_conv_w[0], a_log=v_a_log, dt_bias=v_dt_bias, dn_norm=v_dn_norm,
             sinks=v_sinks, w_o=v_w_o[0], norm_mlp=v_norm_mlp, w_up=v_w_up[0], w_down=v_w_down[0], norm_ple=v_norm_ple,
             w_ple_gate=v_w_ple_gate[0], w_ple_proj=v_w_ple_proj[0], norm_final=v_norm_final)
    me = 4 * lax.axis_index("x") + 2 * lax.axis_index("y") + lax.axis_index("c")
    conv_shard = conv_w.shape[2]

    for d in (w, m, v):
        d["w_in"] = d["w_in"].T
    conv_pad = jnp.pad(w["conv_w"], ((0, 8 - DN_CONV), (0, 256 - conv_shard)))
    first, token_first = _exchange_start([_bf(w["w_in"]), conv_pad], "gather_first_start", "slots")
    later = [_bf(w[n]) for n in _OTHERS]
    later[-1] = _bf(w["w_ple_proj"] + token_first[0:1, 0:1])
    others, token_others = _exchange_start(later, "gather_others_start", [_OTHER_MODES[n] for n in _OTHERS])
    vectors = dict(w)
    vectors["norm_mix"] = w["norm_mix"] + token_others[0:1, 0:1]

    def first_weights(after):
        w_in_all, conv_all = _exchange_wait(first, after, "gather_first_wait")
        conv_all = jnp.transpose(conv_all[:, :DN_CONV, :conv_shard], (1, 0, 2)).reshape(DN_CONV, N_DEV * conv_shard)
        return _w_in_to_internal(w_in_all.reshape(D_IN, D_MODEL)), conv_all

    as_taken = {"w_o": lambda t: t.reshape(1024, 1024), "w_up": lambda t: t, "w_down": lambda t: t.reshape(4096, 1024),
                "w_ple_gate": lambda t: t.reshape(1024, 1024), "w_ple_proj": lambda t: t}

    def other_weights(names, after):
        which = [_OTHERS.index(n) for n in names]
        got = _exchange_wait(others, after, "gather_wait_" + names[0], which)
        return [as_taken[n](t) for n, t in zip(names, got)]

    shipped = []

    def ship_early(pieces):
        names = tuple(pieces)
        if names == ("w_in",):
            pieces = {"w_in": _w_in_from_internal(pieces["w_in"]).reshape(N_DEV, D_IN // N_DEV, D_MODEL)}
        handle, token = _exchange_start([pieces[n] for n in names], "scatter_start_" + names[0], "pieces")
        shipped.append((names, handle))
        return token

    loss, grad_x, g = _local_step(x[0], p[0, 0], loss_target[0], vectors, first_weights, other_weights, ship_early)

    row = lambda t: t.reshape(1, t.size)
    small = _pack_small([row(g[n]) for n in _VECTORS], loss, g["conv_w"].reshape(6, 1024))
    small_all, = _exchange([small], "gather_small", gather=True)
    summed = _sum_slots(small_all)
    conv_g = lax.dynamic_slice(summed[_CONV_ROW:_CONV_ROW + 6].reshape(DN_CONV, N_DEV * conv_shard), (0, me * conv_shard),
                               (DN_CONV, conv_shard))
    small_out = _adamw_vectors(summed, conv_g, [(row(w[n]), row(m[n]), row(v[n])) for n in _VECTORS]
                               + [(w["conv_w"], m["conv_w"], v["conv_w"])])
    big, after = {}, small_out["conv_w"][0]
    for names, handle in shipped:
        for n, r in zip(names, _exchange_wait(handle, after, "scatter_wait_" + names[0])):
            big[n] = _adamw(r, w[n], m[n], v[n], "adamw_" + n)
            after = big[n][1]

    result = [summed[_LOSS_ROW, 0], grad_x[None]]
    for i in range(4):
        for n in _ORDER:
            if n == "w_in":
                result.append(big[n][i].T[None])
            elif n in _MATRICES:
                result.append(big[n][i][None])
            elif n == "conv_w":
                result.append(small_out[n][i][None])
            else:
                result.append(small_out[n][i].reshape(w[n].shape))
    return tuple(result)
```

```python
import jax
import jax.numpy as jnp
import numpy as np
from jax import lax
from jax.experimental import pallas as pl
from jax.experimental.pallas import tpu as pltpu

F32, BF16 = jnp.float32, jnp.bfloat16
EPS = 1e-6
D_MODEL = 1024
N_DEV = 8
ATTN_BLOCK = 128
HEAD_PAIR = 128
DN_HEADS = 4
DN_DIM = 128
DN_CHUNK = 64
DN_CONV = 4
ROPE_THETA = 10000.0
D_IN = 2824
D_IN_PAD = 3072
BLK_Q, BLK_Z = 0, 1
BLK_DN, BLK_K, BLK_V, BLK_G = 8, 20, 21, 22
BLK_KV, BLK_G_PAD = 10, 11
VMEM_LIMIT = 56 * 1024 * 1024
NEG = -1e30
ADAM_LR, ADAM_B1, ADAM_B2, ADAM_EPS, ADAM_WD, ADAM_STEP = 0.001, 0.9, 0.999, 1e-08, 0.01, 10
MESH = pl.DeviceIdType.MESH


def _bf(x):
    return x.astype(BF16)


def _dot(a, b):
    return jnp.dot(a, b, preferred_element_type=F32)


def _dot_nt(a, b):
    return lax.dot_general(a, b, (((1,), (1,)), ((), ())), preferred_element_type=F32)


def _dot_tn(a, b):
    return lax.dot_general(a, b, (((0,), (0,)), ((), ())), preferred_element_type=F32)


def _sigmoid(x):
    return 1.0 / (1.0 + jnp.exp(-x))


def _params(sem):
    return pltpu.CompilerParams(dimension_semantics=sem, vmem_limit_bytes=VMEM_LIMIT)


def _mm(x, w, *, form, name, out_dtypes, tn, epi=None, extra=(), tm=512, w_row_block=0, after=None, norm=None,
        norm_bwd=None, then_nt=None):
    assert norm is None or norm_bwd is None
    xs = list(x) if isinstance(x, (list, tuple)) else [x]
    nx = len(xs)
    S, K = xs[0].shape
    shards = w.ndim == 3
    N = (w.shape[2] * N_DEV if shards else w.shape[1]) if form == "nn" else w.shape[-2]
    assert not (shards and form == "nn" and tn != w.shape[2]) and (nx == 1 or (form == "nn" and not shards and norm is None))
    r0 = w_row_block * K
    tm = min(tm, S)
    n_extra, n_out = len(extra), len(out_dtypes)
    tile = lambda width: pl.BlockSpec((tm, width), lambda i: (i, 0))
    whole = lambda a: pl.BlockSpec(a.shape, lambda i, nd=a.ndim: (0,) * nd)
    ins, in_specs = [*xs, w, *extra], [tile(K)] * nx + [whole(w)] + [tile(N)] * n_extra
    if norm is not None:
        ins, in_specs = ins + [norm], in_specs + [whole(norm)]
    if norm_bwd is not None:
        ins, in_specs = ins + list(norm_bwd), in_specs + [tile(N), whole(norm_bwd[1]), tile(N)]
    if then_nt is not None:
        ins, in_specs = ins + [then_nt], in_specs + [whole(then_nt)]
    if after is not None:
        ins, in_specs = ins + [after], in_specs + [whole(after)]
    out_shape = [jax.ShapeDtypeStruct((S, N), dt) for dt in out_dtypes]
    out_specs = [tile(N)] * n_out
    if norm is not None:
        out_shape, out_specs = out_shape + [jax.ShapeDtypeStruct((S, K), BF16)], out_specs + [tile(K)]
    if norm_bwd is not None:
        out_shape, out_specs = out_shape + [jax.ShapeDtypeStruct((1, N), F32)], out_specs + [pl.BlockSpec((1, N), lambda i: (0, 0))]
    if then_nt is not None:
        out_shape, out_specs = out_shape + [jax.ShapeDtypeStruct((S, then_nt.shape[0]), F32)], out_specs + [tile(then_nt.shape[0])]

    def product(xb, w_ref, cols, c):
        if form == "nn" and nx > 1:
            return sum(_dot(part, w_ref[r0 + p * K:r0 + (p + 1) * K, cols]) for p, part in enumerate(xb))
        if form == "nn":
            return _dot(xb, w_ref[c] if shards else w_ref[r0:r0 + K, cols])
        if not shards:
            return _dot_nt(xb, w_ref[cols, :])
        ks = w.shape[2]
        acc = _dot_nt(xb[:, 0:ks], w_ref[0, cols, :])
        for s in range(1, N_DEV):
            acc = acc + _dot_nt(xb[:, s * ks:(s + 1) * ks], w_ref[s, cols, :])
        return acc

    def body(*refs):
        x_ref, w_ref = refs[0], refs[nx]
        extra_refs = refs[nx + 1:nx + 1 + n_extra]
        at = nx + 1 + n_extra
        if norm is not None:
            gain_ref, at = refs[at], at + 1
        if norm_bwd is not None:
            (y_ref, ygain_ref, dres_ref), at = refs[at:at + 3], at + 3
        if then_nt is not None:
            w2_ref, at = refs[at], at + 1
        outs = refs[len(ins):]
        if norm is not None:
            _, xh = _rms_stats(x_ref[...])
            xb = _bf(xh * gain_ref[...])
            outs[n_out][...] = xb
        else:
            xb = _bf(x_ref[...]) if nx == 1 else [_bf(r[...]) for r in refs[:nx]]
        for c in range(N // tn):
            cols = slice(c * tn, (c + 1) * tn)
            acc = product(xb, w_ref, cols, c)
            res = epi(acc, *[r[:, cols] for r in extra_refs]) if epi else (acc,)
            for o, r in zip(outs[:n_out], res):
                o[:, cols] = r.astype(o.dtype)
        if norm_bwd is not None:
            dx, dg = _rms_bwd_tile(y_ref[...], ygain_ref[...], outs[0][...])
            outs[0][...] = dres_ref[...] + dx
            dg_ref = outs[n_out]

            @pl.when(pl.program_id(0) == 0)
            def _():
                dg_ref[...] = jnp.zeros_like(dg_ref)

            dg_ref[...] += dg
        if then_nt is not None:
            yb = _bf(outs[0][...])
            for c in range(then_nt.shape[0] // tn):
                cols = slice(c * tn, (c + 1) * tn)
                outs[-1][:, cols] = _dot_nt(yb, w2_ref[cols, :])

    return pl.pallas_call(
        body, grid=(S // tm,), name=name, in_specs=in_specs, out_specs=out_specs, out_shape=out_shape,
        compiler_params=_params(("arbitrary",) if norm_bwd is not None else ("parallel",)),
    )(*ins)


def _mm_tn(x, dy, *, name, tm, tn, out_dtype=F32, column_shards=False, after=None):
    S, K = x.shape
    N = dy.shape[1]
    waits = [] if after is None else [after]

    def body(x_ref, dy_ref, *rest):
        rest[-1][...] = _dot_tn(_bf(x_ref[...]), _bf(dy_ref[...])).astype(out_dtype)

    if column_shards:
        out_spec = pl.BlockSpec((None, tm, tn), lambda i, j: (j, i, 0))
        out_shape = jax.ShapeDtypeStruct((N // tn, K, tn), out_dtype)
    else:
        out_spec = pl.BlockSpec((tm, tn), lambda i, j: (i, j))
        out_shape = jax.ShapeDtypeStruct((K, N), out_dtype)
    return pl.pallas_call(
        body, grid=(K // tm, N // tn), name=name,
        in_specs=[pl.BlockSpec((S, tm), lambda i, j: (0, i)), pl.BlockSpec((S, tn), lambda i, j: (0, j))]
        + [pl.BlockSpec(memory_space=pl.ANY)] * len(waits),
        out_specs=out_spec, out_shape=out_shape,
        compiler_params=_params(("parallel", "parallel")),
    )(x, dy, *waits)


def _rowwise(body, *, tiled, full, out_tiled, out_acc, name, tm=512, smem=()):
    S = tiled[0].shape[0]
    tm = min(tm, S)
    n_in = len(smem) + len(tiled) + len(full)

    def kern(*refs):
        @pl.when(pl.program_id(0) == 0)
        def _():
            for r in refs[n_in + len(out_tiled):]:
                r[...] = jnp.zeros_like(r)
        body(*refs)

    in_specs = [pl.BlockSpec(memory_space=pltpu.SMEM) for _ in smem]
    in_specs += [pl.BlockSpec((tm, a.shape[1]), lambda i: (i, 0)) for a in tiled]
    in_specs += [pl.BlockSpec(a.shape, lambda i, nd=a.ndim: (0,) * nd) for a in full]
    out_specs = [pl.BlockSpec((tm, w), lambda i: (i, 0)) for w, _ in out_tiled]
    out_specs += [pl.BlockSpec(shp, lambda i, nd=len(shp): (0,) * nd) for shp, _ in out_acc]
    out_shape = [jax.ShapeDtypeStruct((S, w), dt) for w, dt in out_tiled]
    out_shape += [jax.ShapeDtypeStruct(shp, dt) for shp, dt in out_acc]
    return pl.pallas_call(
        kern, grid=(S // tm,), name=name, in_specs=in_specs, out_specs=out_specs, out_shape=out_shape,
        compiler_params=_params(("arbitrary",)),
    )(*smem, *tiled, *full)


def _rms_stats(x):
    r = lax.rsqrt(jnp.mean(x * x, axis=-1, keepdims=True) + EPS)
    return r, x * r


def _rmsnorm_fwd(x, g, name):
    def body(x_ref, g_ref, o_ref):
        _, xh = _rms_stats(x_ref[...])
        o_ref[...] = _bf(xh * g_ref[...])

    return _rowwise(body, tiled=[x], full=[g], out_tiled=[(x.shape[1], BF16)], out_acc=[], name=name)[0]


def _rms_bwd_tile(x, g, dxn):
    r, xh = _rms_stats(x)
    dg = jnp.sum(dxn * xh, axis=0, keepdims=True)
    dn = dxn * g
    dx = r * (dn - xh * jnp.mean(dn * xh, axis=-1, keepdims=True))
    return dx, dg


def _ple_and_loss(h2, p, target, w_pg, w_pp, g_ple, g_final):
    S, n = h2.shape
    tm = min(512, S)
    tn = 512

    def body(h2_ref, p_ref, t_ref, wpg_ref, wpp_ref, gple_ref, gfin_ref,
             n3_ref, dh_ref, dgl_ref, dpp_ref, loss_ref, dg_ref, dgple_ref, pp, gate, h3):
        @pl.when(pl.program_id(0) == 0)
        def _():
            loss_ref[...] = jnp.zeros_like(loss_ref)
            dg_ref[...] = jnp.zeros_like(dg_ref)
            dgple_ref[...] = jnp.zeros_like(dgple_ref)

        x = h2_ref[...]
        _, xh = _rms_stats(x)
        n3 = _bf(xh * gple_ref[...])
        n3_ref[...] = n3
        pb = _bf(p_ref[...])
        for c in range(n // tn):
            cols = slice(c * tn, (c + 1) * tn)
            pp[:, cols] = _dot(pb, wpp_ref[:, cols])
            gt = _sigmoid(_dot(n3, wpg_ref[:, cols]))
            gate[:, cols] = gt
            h3[:, cols] = x[:, cols] + gt * pp[:, cols]
        y = h3[...]
        _, yh = _rms_stats(y)
        e = yh * gfin_ref[...] - t_ref[...]
        per_tok = jnp.mean(e * e, axis=-1, keepdims=True)
        loss_ref[...] += 0.5 * jnp.sum(per_tok, axis=0, keepdims=True)
        dh, dg = _rms_bwd_tile(y, gfin_ref[...], e * (1.0 / n))
        dg_ref[...] += dg
        gt = gate[...]
        dgl = _bf(dh * pp[...] * gt * (1.0 - gt))
        dgl_ref[...] = dgl
        dpp_ref[...] = _bf(dh * gt)
        for c in range(n // tn):
            cols = slice(c * tn, (c + 1) * tn)
            h3[:, cols] = _dot_nt(dgl, wpg_ref[cols, :])
        dx, dgp = _rms_bwd_tile(x, gple_ref[...], h3[...])
        dh_ref[...] = dh + dx
        dgple_ref[...] += dgp

    tile = lambda width: pl.BlockSpec((tm, width), lambda i: (i, 0))
    whole = lambda a: pl.BlockSpec(a.shape, lambda i, nd=a.ndim: (0,) * nd)
    return pl.pallas_call(
        body, grid=(S // tm,), name="ple_and_loss",
        in_specs=[tile(n), tile(p.shape[1]), tile(n), whole(w_pg), whole(w_pp), whole(g_ple), whole(g_final)],
        out_specs=[tile(n), tile(n), tile(n), tile(n), pl.BlockSpec((1, 128), lambda i: (0, 0)),
                   pl.BlockSpec((1, n), lambda i: (0, 0)), pl.BlockSpec((1, n), lambda i: (0, 0))],
        out_shape=[jax.ShapeDtypeStruct((S, n), BF16), jax.ShapeDtypeStruct((S, n), F32), jax.ShapeDtypeStruct((S, n), BF16),
                   jax.ShapeDtypeStruct((S, n), BF16), jax.ShapeDtypeStruct((1, 128), F32), jax.ShapeDtypeStruct((1, n), F32),
                   jax.ShapeDtypeStruct((1, n), F32)],
        scratch_shapes=[pltpu.VMEM((tm, n), F32)] * 3,
        compiler_params=_params(("arbitrary",)),
    )(h2, p, target, w_pg, w_pp, g_ple, g_final)


def _rope_tables(S):
    half = 32
    inv = (1.0 / (np.float32(ROPE_THETA) ** (np.arange(half, dtype=np.float32) * np.float32(2.0 / 64)))).astype(np.float32)
    ang = np.arange(S).astype(np.float32)[:, None] * inv[None, :]
    cos, sin = np.cos(ang), np.sin(ang)
    return jnp.asarray(np.tile(cos, (1, 4))), jnp.asarray(np.concatenate([-sin, sin, -sin, sin], axis=1))


def _attn_common(i, kc, kp, vc, vp, cc, sc, cp, sp):
    lane = lax.broadcasted_iota(jnp.int32, (1, HEAD_PAIR), 1)
    lane_lo = jnp.bitwise_and(lane, 63) < 32
    slot = [lane < 64, lane >= 64]

    def swap_halves(t):
        return jnp.where(lane_lo, pltpu.roll(t, 96, 1), pltpu.roll(t, 32, 1))

    def rope(t, cos, sin):
        return t * cos + swap_halves(t) * sin

    def unrope(d, cos, sin):
        return d * cos + swap_halves(d * sin)

    k2 = jnp.concatenate([rope(kp, cp, sp), rope(kc, cc, sc)], axis=0)
    v2 = jnp.concatenate([vp, vc], axis=0)
    r = lax.broadcasted_iota(jnp.int32, (ATTN_BLOCK, 2 * ATTN_BLOCK), 0)
    c = lax.broadcasted_iota(jnp.int32, (ATTN_BLOCK, 2 * ATTN_BLOCK), 1)
    valid = (c > r) & (c <= r + ATTN_BLOCK) & jnp.logical_or(c >= ATTN_BLOCK, i > 0)
    ks, vs = {}, {}
    for j in range(2):
        kn = jnp.where(slot[j], k2, 0.0)
        vn = jnp.where(slot[j], v2, 0.0)
        for s in range(2):
            ks[j, s] = _bf(kn if s == j else pltpu.roll(kn, 64, 1))
            vs[j, s] = _bf(vn if s == j else pltpu.roll(vn, 64, 1))
    return slot, rope, unrope, valid, ks, vs


def _attn_probs(scores, valid, sink):
    s = jnp.where(valid, scores * 0.125, NEG)
    m = jnp.maximum(jnp.max(s, axis=1, keepdims=True), sink)
    e = jnp.exp(s - m)
    z = jnp.sum(e, axis=1, keepdims=True) + jnp.exp(sink - m)
    return e * (1.0 / z), m + jnp.log(z)


def _attn_specs(S):
    nb = S // ATTN_BLOCK
    prev = lambda i: jnp.maximum(i - 1, 0)
    blk = lambda w, col, row=(lambda i: i): pl.BlockSpec((ATTN_BLOCK, w), lambda i: (row(i), col))
    in_specs = [pl.BlockSpec(memory_space=pltpu.SMEM),
                blk(512, BLK_Q), blk(128, BLK_K), blk(128, BLK_K, prev), blk(128, BLK_V), blk(128, BLK_V, prev),
                blk(128, 0), blk(128, 0), blk(128, 0, prev), blk(128, 0, prev)]
    return nb, in_specs


def _attn_fwd(pa, cos, sin, sinks):
    S = pa.shape[0]
    nb, in_specs = _attn_specs(S)

    def body(sinks_ref, q_ref, kc_ref, kp_ref, vc_ref, vp_ref, cc_ref, sc_ref, cp_ref, sp_ref, o_ref, lse_ref):
        i = pl.program_id(0)
        lane = lax.broadcasted_iota(jnp.int32, (1, HEAD_PAIR), 1)
        cc, sc = cc_ref[...], sc_ref[...]
        _, rope, _, valid, ks, vs = _attn_common(i, kc_ref[...], kp_ref[...], vc_ref[...], vp_ref[...],
                                                 cc, sc, cp_ref[...], sp_ref[...])
        pair_cols = [slice(HEAD_PAIR * pair, HEAD_PAIR * (pair + 1)) for pair in range(4)]
        qps = [_bf(rope(q_ref[:, cols], cc, sc)) for cols in pair_cols]
        outs, lses = {}, {}

        def head_program(h):
            pair, s = divmod(h, 2)
            j = h // 4
            scores = _dot_nt(qps[pair], ks[j, s])
            yield
            p, lse = _attn_probs(scores, valid, sinks_ref[h])
            outs[h] = _dot(_bf(p), vs[j, s])
            lses[h] = jnp.where(lane == h, lse, 0.0)

        _interleave(head_program(h) for h in range(8))
        for pair, cols in enumerate(pair_cols):
            o_ref[:, cols] = outs[2 * pair] + outs[2 * pair + 1]
        lse_ref[...] = sum((lses[h] for h in range(1, 8)), lses[0])

    return pl.pallas_call(
        body, grid=(nb,), name="attn_fwd", in_specs=in_specs,
        out_specs=[pl.BlockSpec((ATTN_BLOCK, 512), lambda i: (i, 0)), pl.BlockSpec((ATTN_BLOCK, 128), lambda i: (i, 0))],
        out_shape=[jax.ShapeDtypeStruct((S, 512), F32), jax.ShapeDtypeStruct((S, 128), F32)],
        compiler_params=_params(("parallel",)),
    )(sinks, pa, pa, pa, pa, pa, cos, sin, cos, sin)


def _attn_bwd(pa, cos, sin, sinks, dcat, attn, lse):
    S = pa.shape[0]
    nb, in_specs = _attn_specs(S)
    in_specs = in_specs + [pl.BlockSpec((ATTN_BLOCK, 512), lambda i: (i, 0))] * 2 + [pl.BlockSpec((ATTN_BLOCK, 128), lambda i: (i, 0))]

    def body(sinks_ref, q_ref, kc_ref, kp_ref, vc_ref, vp_ref, cc_ref, sc_ref, cp_ref, sp_ref, do_ref, o_ref, lse_ref,
             dq_ref, dk_ref, dv_ref, dsink_ref):
        i = pl.program_id(0)

        @pl.when(i == 0)
        def _():
            dk_ref[...] = jnp.zeros_like(dk_ref)
            dv_ref[...] = jnp.zeros_like(dv_ref)
            dsink_ref[...] = jnp.zeros_like(dsink_ref)

        cc, sc, cp, sp = cc_ref[...], sc_ref[...], cp_ref[...], sp_ref[...]
        slot, rope, unrope, valid, ks, vs = _attn_common(i, kc_ref[...], kp_ref[...], vc_ref[...], vp_ref[...], cc, sc, cp, sp)
        pair_cols = [slice(HEAD_PAIR * pair, HEAD_PAIR * (pair + 1)) for pair in range(4)]
        qps = [_bf(rope(q_ref[:, cols], cc, sc)) for cols in pair_cols]
        dobs = [_bf(do_ref[:, cols]) for cols in pair_cols]
        do_o = [do_ref[:, cols] * o_ref[:, cols] for cols in pair_cols]
        dqs, dks, dvs = {}, {}, {}

        def head_program(h):
            pair, s = divmod(h, 2)
            j = h // 4
            qp, dob = qps[pair], dobs[pair]
            scores = _dot_nt(qp, ks[j, s])
            dp = _dot_nt(dob, vs[j, s])
            yield
            lse_h = lse_ref[:, h:h + 1]
            p = jnp.exp(jnp.where(valid, scores * 0.125, NEG) - lse_h)
            dr = jnp.sum(jnp.where(slot[s], do_o[pair], 0.0), axis=1, keepdims=True)
            ds = _bf(p * (dp - dr) * 0.125)
            dsink_ref[h:h + 1, :] += -jnp.sum(jnp.exp(sinks_ref[h] - lse_h) * dr, axis=0, keepdims=True)
            dqs[h] = _dot(ds, ks[j, s])
            dk_h = _dot_tn(ds, qp)
            dv_h = _dot_tn(_bf(p), dob)
            yield
            dk_h, dv_h = jnp.where(slot[s], dk_h, 0.0), jnp.where(slot[s], dv_h, 0.0)
            if s != j:
                dk_h, dv_h = pltpu.roll(dk_h, 64, 1), pltpu.roll(dv_h, 64, 1)
            dks[h], dvs[h] = dk_h, dv_h

        _interleave(head_program(h) for h in range(8))
        dk2 = sum((dks[h] for h in range(1, 8)), dks[0])
        dv2 = sum((dvs[h] for h in range(1, 8)), dvs[0])
        for pair, cols in enumerate(pair_cols):
            dq_ref[:, cols] = _bf(unrope(dqs[2 * pair] + dqs[2 * pair + 1], cc, sc))
        cur = pl.ds(pl.multiple_of(i * ATTN_BLOCK, ATTN_BLOCK), ATTN_BLOCK)
        dk_ref[cur, :] += unrope(dk2[ATTN_BLOCK:], cc, sc)
        dv_ref[cur, :] += dv2[ATTN_BLOCK:]

        @pl.when(i > 0)
        def _():
            prv = pl.ds(pl.multiple_of((i - 1) * ATTN_BLOCK, ATTN_BLOCK), ATTN_BLOCK)
            dk_ref[prv, :] += unrope(dk2[:ATTN_BLOCK], cp, sp)
            dv_ref[prv, :] += dv2[:ATTN_BLOCK]

    whole = lambda w: pl.BlockSpec((S, w), lambda i: (0, 0))
    return pl.pallas_call(
        body, grid=(nb,), name="attn_bwd", in_specs=in_specs,
        out_specs=[pl.BlockSpec((ATTN_BLOCK, 512), lambda i: (i, BLK_Q)), whole(128), whole(128),
                   pl.BlockSpec((8, 128), lambda i: (0, 0))],
        out_shape=[jax.ShapeDtypeStruct((S, D_IN_PAD), BF16), jax.ShapeDtypeStruct((S, 128), F32),
                   jax.ShapeDtypeStruct((S, 128), F32), jax.ShapeDtypeStruct((8, 128), F32)],
        compiler_params=_params(("arbitrary",)),
    )(sinks, pa, pa, pa, pa, pa, cos, sin, cos, sin, dcat, attn, lse)


CONV_ROWS = 512
CONV_PAD = 8


def _conv_silu(scr, w, r0):
    y = w[3:4, :] * scr[pl.ds(CONV_PAD + r0, CONV_ROWS), :]
    for j in range(DN_CONV - 1):
        y = y + w[j:j + 1, :] * scr[pl.ds(CONV_PAD + r0 - 3 + j, CONV_ROWS), :]
    return y


def _dn_prep_fwd(pd, conv_w):
    S = pd.shape[0]
    assert S % CONV_ROWS == 0

    def body(x_ref, w_ref, o_ref, scr):
        b = pl.program_id(0)
        scr[0:CONV_PAD, :] = jnp.zeros((CONV_PAD, DN_DIM), F32)
        scr[pl.ds(CONV_PAD, S), :] = x_ref[...]
        w = w_ref[...]
        q_scale = jnp.where(b < DN_HEADS, DN_DIM ** -0.5, 1.0)
        for r0 in range(0, S, CONV_ROWS):
            y = _conv_silu(scr, w, r0)
            a = y * _sigmoid(y)
            rs = lax.rsqrt(jnp.sum(a * a, axis=1, keepdims=True) + EPS)
            o_ref[pl.ds(r0, CONV_ROWS), :] = a * jnp.where(b < 2 * DN_HEADS, rs * q_scale, 1.0)

    col = pl.BlockSpec((S, DN_DIM), lambda b: (0, b))
    return pl.pallas_call(
        body, grid=(3 * DN_HEADS,), name="dn_prep_fwd",
        in_specs=[pl.BlockSpec((S, DN_DIM), lambda b: (0, BLK_DN + b)), pl.BlockSpec((DN_CONV, DN_DIM), lambda b: (0, b))],
        out_specs=col,
        out_shape=jax.ShapeDtypeStruct((S, 3 * DN_HEADS * DN_DIM), F32),
        scratch_shapes=[pltpu.VMEM((S + CONV_PAD, DN_DIM), F32)],
        compiler_params=_params(("parallel",)),
    )(pd, conv_w)


def _dn_prep_bwd(pd, conv_w, dqkv, dproj):
    S = pd.shape[0]

    def body(x_ref, w_ref, d_ref, _, dx_ref, dw_ref, scr, dscr):
        b = pl.program_id(0)
        scr[0:CONV_PAD, :] = jnp.zeros((CONV_PAD, DN_DIM), F32)
        scr[pl.ds(CONV_PAD, S), :] = x_ref[...]
        dscr[pl.ds(S, CONV_PAD), :] = jnp.zeros((CONV_PAD, DN_DIM), F32)
        w = w_ref[...]
        q_scale = jnp.where(b < DN_HEADS, DN_DIM ** -0.5, 1.0)
        is_qk = b < 2 * DN_HEADS
        dw = [jnp.zeros((1, DN_DIM), F32) for _ in range(DN_CONV)]
        for r0 in range(0, S, CONV_ROWS):
            y = _conv_silu(scr, w, r0)
            sg = _sigmoid(y)
            a = y * sg
            dout = d_ref[pl.ds(r0, CONV_ROWS), :]
            rs = lax.rsqrt(jnp.sum(a * a, axis=1, keepdims=True) + EPS)
            da_qk = q_scale * rs * (dout - a * (rs * rs) * jnp.sum(dout * a, axis=1, keepdims=True))
            dy = jnp.where(is_qk, da_qk, dout) * (sg * (1.0 + y * (1.0 - sg)))
            dscr[pl.ds(r0, CONV_ROWS), :] = dy
            for j in range(DN_CONV):
                dw[j] = dw[j] + jnp.sum(dy * scr[pl.ds(CONV_PAD + r0 - 3 + j, CONV_ROWS), :], axis=0, keepdims=True)
        for j in range(DN_CONV):
            dw_ref[j:j + 1, :] = dw[j]
        for r0 in range(0, S, CONV_ROWS):
            dx = w[3:4, :] * dscr[pl.ds(r0, CONV_ROWS), :]
            for j in range(DN_CONV - 1):
                dx = dx + w[j:j + 1, :] * dscr[pl.ds(r0 + 3 - j, CONV_ROWS), :]
            dx_ref[pl.ds(r0, CONV_ROWS), :] = _bf(dx)

    col = pl.BlockSpec((S, DN_DIM), lambda b: (0, b))
    proj_col = pl.BlockSpec((S, DN_DIM), lambda b: (0, BLK_DN + b))
    wcol = pl.BlockSpec((DN_CONV, DN_DIM), lambda b: (0, b))
    return pl.pallas_call(
        body, grid=(3 * DN_HEADS,), name="dn_prep_bwd",
        in_specs=[proj_col, wcol, col, pl.BlockSpec(memory_space=pl.ANY)], out_specs=[proj_col, wcol],
        out_shape=[jax.ShapeDtypeStruct(dproj.shape, dproj.dtype), jax.ShapeDtypeStruct((DN_CONV, 3 * DN_HEADS * DN_DIM), F32)],
        scratch_shapes=[pltpu.VMEM((S + CONV_PAD, DN_DIM), F32), pltpu.VMEM((S + CONV_PAD, DN_DIM), F32)],
        input_output_aliases={3: 0},
        compiler_params=_params(("parallel",)),
    )(pd, conv_w, dqkv, dproj)


CPAD = 128
CHUNKS_LOCAL = 4
CHUNKS_SCAN = 4


def _chunk_masks():
    ii = lax.broadcasted_iota(jnp.int32, (DN_CHUNK, CPAD), 0)
    jj = lax.broadcasted_iota(jnp.int32, (DN_CHUNK, CPAD), 1)
    return ii, jj


def _rows_pad(a):
    return jnp.concatenate([a, jnp.zeros_like(a)], axis=0)


def _hi_lo(a):
    hi = _bf(a)
    return hi, _bf(a - hi.astype(F32))


def _double_step(t, p):
    C = DN_CHUNK
    th, tl = _hi_lo(t)
    ph, pl_ = _hi_lo(p)
    r1 = _dot(jnp.concatenate([th, tl, ph, pl_], axis=0), _rows_pad(ph))
    r2 = _dot(jnp.concatenate([th, ph], axis=0), _rows_pad(pl_))
    return t + (r1[:C] + r1[C:2 * C] + r2[:C]), r1[2 * C:3 * C] + r1[3 * C:] + r2[C:]


def _dot3_nt(a, b):
    C = DN_CHUNK
    ah, al = _hi_lo(a)
    bh, bl = _hi_lo(b)
    r1 = _dot_nt(jnp.concatenate([ah, al], axis=0), _rows_pad(bh))
    return r1[:C] + r1[C:] + _dot_nt(ah, _rows_pad(bl))


def _dot3_tn(a, b):
    C = DN_CHUNK
    ah, al = _hi_lo(a)
    bh, bl = _hi_lo(b)
    return _dot_tn(jnp.concatenate([ah, al, ah], axis=0), jnp.concatenate([bh, bh, bl], axis=0))[:C]


def _interleave(programs):
    programs = list(programs)
    while programs:
        alive = []
        for prog in programs:
            try:
                next(prog)
                alive.append(prog)
            except StopIteration:
                pass
        programs = alive


def _col_to_row(col, ii, jj):
    return jnp.sum(jnp.where(ii == jj, col, 0.0), axis=0, keepdims=True)


def _row_to_col(row, ii, jj):
    return jnp.sum(jnp.where(ii == jj, row, 0.0), axis=1, keepdims=True)


def _decay(gc_col, ii, jj):
    diff = gc_col - _col_to_row(gc_col, ii, jj)
    return jnp.where(jj <= ii, jnp.exp(jnp.where(jj <= ii, diff, 0.0)), 0.0)


def _softplus(x):
    return jnp.maximum(x, 0.0) + jnp.log(1.0 + jnp.exp(-jnp.abs(x)))


def _head(h):
    return slice(DN_DIM * h, DN_DIM * (h + 1))


def _dn_chunk_fwd(qkv, pg, a_log, dt_bias):
    S = qkv.shape[0]
    C = DN_CHUNK
    G = CHUNKS_LOCAL
    R = G * C
    steps = S // R

    def body(alog_ref, dtb_ref, qkv_ref, pg_ref, w_ref, u_ref, qg_ref, kd_ref, a_ref, t_ref, gcs_ref):
        ii, jj = _chunk_masks()
        lane = lax.broadcasted_iota(jnp.int32, (1, 128), 1)
        eye = (ii == jj).astype(F32)
        gcs_parts = [[] for _ in range(G)]

        def head_program(chunk, h):
            rows = slice(chunk * C, (chunk + 1) * C)
            q, k, v = qkv_ref[rows, _head(h)], qkv_ref[rows, _head(DN_HEADS + h)], qkv_ref[rows, _head(2 * DN_HEADS + h)]
            beta = _sigmoid(pg_ref[rows, h:h + 1])
            g_col = -jnp.exp(alog_ref[h]) * _softplus(pg_ref[rows, DN_HEADS + h:DN_HEADS + h + 1] + dtb_ref[h])
            g_row = _col_to_row(g_col, ii, jj)
            gc_col = jnp.sum(jnp.where(jj <= ii, g_row, 0.0), axis=1, keepdims=True)
            dec = _decay(gc_col, ii, jj)
            eg = jnp.exp(gc_col)
            kb, vb = k * beta, v * beta
            k_rows = _rows_pad(_bf(k))
            kk = _dot_nt(_bf(kb), k_rows)
            qk = _dot_nt(_bf(q), k_rows)
            yield
            t, pw = eye, -jnp.where(jj < ii, kk * dec, 0.0)
            for _ in range(6):
                t, pw = _double_step(t, pw)
                yield
            tb = _bf(t)
            u_ref[rows, _head(h)] = _dot(tb, _rows_pad(_bf(vb)))
            w_ref[rows, _head(h)] = _bf(_dot(tb, _rows_pad(_bf(kb * eg))))
            a_ref[h, rows] = _bf(qk * dec)
            t_ref[h, rows] = t
            qg_ref[rows, _head(h)] = _bf(q * eg)
            kd_ref[rows, _head(h)] = _bf(k * jnp.exp(gc_col[C - 1:C, :] - gc_col))
            gcs_parts[chunk].append(jnp.where(lane == h, gc_col, 0.0) + jnp.where(lane == DN_HEADS + h, beta, 0.0)
                                    + jnp.where(lane == 2 * DN_HEADS + h, g_col, 0.0))

        _interleave(head_program(chunk, h) for chunk in range(G) for h in range(DN_HEADS))
        for chunk in range(G):
            gcs_ref[chunk * C:(chunk + 1) * C, :] = sum(gcs_parts[chunk][1:], gcs_parts[chunk][0])

    smem = pl.BlockSpec(memory_space=pltpu.SMEM)
    wide = pl.BlockSpec((R, 512), lambda n: (n, 0))
    sq = pl.BlockSpec((DN_HEADS, R, CPAD), lambda n: (0, n, 0))
    narrow = pl.BlockSpec((R, 128), lambda n: (n, 0))
    f = lambda *shp: jax.ShapeDtypeStruct(shp, F32)
    b = lambda *shp: jax.ShapeDtypeStruct(shp, BF16)
    return pl.pallas_call(
        body, grid=(steps,), name="dn_chunk_fwd",
        in_specs=[smem, smem, pl.BlockSpec((R, 1536), lambda n: (n, 0)), pl.BlockSpec((R, 128), lambda n: (n, BLK_G))],
        out_specs=[wide, wide, wide, wide, sq, sq, narrow],
        out_shape=[b(S, 512), f(S, 512), b(S, 512), b(S, 512), b(DN_HEADS, S, CPAD), f(DN_HEADS, S, CPAD), f(S, 128)],
        compiler_params=_params(("parallel",)),
    )(a_log, dt_bias, qkv, pg)


def _gated_norm(o, z, gn):
    r, oh = _rms_stats(o)
    return oh * gn * (z * _sigmoid(z))


def _dn_scan_fwd(w, u, qg, kd, a, gcs, pz, gn):
    S = w.shape[0]
    C = DN_CHUNK
    nc = S // C
    G = CHUNKS_SCAN
    R = G * C

    def body(w_ref, u_ref, qg_ref, kd_ref, a_ref, gcs_ref, z_ref, gn_ref, o_ref, vn_ref, sst_ref, out_ref, state):
        @pl.when(pl.program_id(0) == 0)
        def _():
            state[...] = jnp.zeros_like(state)

        def head_program(chunk, h):
            hs = _head(h)
            rows = slice(chunk * C, (chunk + 1) * C)
            s_in = state[h]
            sb = _bf(s_in)
            sst_ref[chunk, h] = sb
            w_s = _dot(w_ref[rows, hs], sb)
            q_s = _dot(qg_ref[rows, hs], sb)
            yield
            vn = u_ref[rows, hs] - w_s
            vnb = _bf(vn)
            o = q_s + _dot(a_ref[h, rows], _rows_pad(vnb))
            k_v = _dot_tn(kd_ref[rows, hs], vnb)
            yield
            state[h] = s_in * jnp.exp(gcs_ref[(chunk + 1) * C - 1:(chunk + 1) * C, h:h + 1]) + k_v
            o_ref[rows, hs] = o
            vn_ref[rows, hs] = vnb
            out_ref[rows, hs] = _bf(_gated_norm(o, z_ref[rows, hs], gn_ref[...]))

        for chunk in range(G):
            _interleave(head_program(chunk, h) for h in range(DN_HEADS))

    wide = pl.BlockSpec((R, 512), lambda n: (n, 0))
    f = lambda *shp: jax.ShapeDtypeStruct(shp, F32)
    b = lambda *shp: jax.ShapeDtypeStruct(shp, BF16)
    return pl.pallas_call(
        body, grid=(nc // G,), name="dn_scan_fwd",
        in_specs=[wide, wide, wide, wide, pl.BlockSpec((DN_HEADS, R, CPAD), lambda n: (0, n, 0)),
                  pl.BlockSpec((R, 128), lambda n: (n, 0)), pl.BlockSpec((R, 512), lambda n: (n, BLK_Z)),
                  pl.BlockSpec((1, DN_DIM), lambda n: (0, 0))],
        out_specs=[wide, wide, pl.BlockSpec((G, DN_HEADS, DN_DIM, DN_DIM), lambda n: (n, 0, 0, 0)), wide],
        out_shape=[f(S, 512), b(S, 512), b(nc, DN_HEADS, DN_DIM, DN_DIM), b(S, 512)],
        scratch_shapes=[pltpu.VMEM((DN_HEADS, DN_DIM, DN_DIM), F32)],
        compiler_params=_params(("arbitrary",)),
    )(w, u, qg, kd, a, gcs, pz, gn)


def _dn_scan_bwd(dcat, o, pz, gn, sst, vnew, w, qg, kd, a, gcs, dproj):
    S = o.shape[0]
    C = DN_CHUNK
    G = CHUNKS_SCAN
    R = G * C
    steps = S // R

    def body(dy_ref, o_ref, z_ref, gn_ref, sst_ref, vn_ref, w_ref, qg_ref, kd_ref, a_ref, gcs_ref, _,
             du_ref, dw_ref, dqg_ref, dkd_ref, da_ref, dz_ref, dsc_ref, dgn_ref, dstate):
        @pl.when(pl.program_id(0) == 0)
        def _():
            dstate[...] = jnp.zeros_like(dstate)
            dgn_ref[...] = jnp.zeros_like(dgn_ref)

        gn_ = gn_ref[...]
        lane = lax.broadcasted_iota(jnp.int32, (C, 128), 1)
        row = lax.broadcasted_iota(jnp.int32, (C, 128), 0)
        dgn_parts = []

        def head_program(chunk, h, dsc_parts):
            hs = _head(h)
            rows = slice(chunk * C, (chunk + 1) * C)
            ov, z, dout = o_ref[rows, hs], z_ref[rows, hs], dy_ref[rows, hs]
            r, oh = _rms_stats(ov)
            sg = _sigmoid(z)
            don = dout * (z * sg)
            dz_ref[rows, hs] = _bf(dout * (oh * gn_) * (sg * (1.0 + z * (1.0 - sg))))
            dgn_parts.append(jnp.sum(don * oh, axis=0, keepdims=True))
            dn = don * gn_
            do = _bf(r * (dn - oh * jnp.mean(dn * oh, axis=-1, keepdims=True)))
            sb = sst_ref[chunk, h]
            s_in = sb.astype(F32)
            ds_out = dstate[h]
            dsb = _bf(ds_out)
            vnb = vn_ref[rows, hs]
            wb, qgb, kdb, ab = w_ref[rows, hs], qg_ref[rows, hs], kd_ref[rows, hs], a_ref[h, rows]
            dvn = _dot_tn(ab, do)[:C] + _dot(kdb, dsb)
            da_ref[h, rows] = _dot_nt(do, _rows_pad(vnb))
            dqg_ref[rows, hs] = _dot_nt(do, sb)
            dkd_ref[rows, hs] = _dot_nt(vnb, dsb)
            q_do = _dot_tn(qgb, do)
            yield
            dvnb = _bf(dvn)
            dw_ref[rows, hs] = _bf(-_dot_nt(dvnb, sb))
            w_dvn = _dot_tn(wb, dvnb)
            du_ref[rows, hs] = dvnb
            yield
            d_last = jnp.exp(gcs_ref[(chunk + 1) * C - 1:(chunk + 1) * C, h:h + 1])
            dd = jnp.sum(jnp.sum(ds_out * s_in, axis=1, keepdims=True), axis=0, keepdims=True)
            dsc_parts.append(jnp.where((lane == h) & (row == C - 1), dd * d_last, 0.0))
            dstate[h] = ds_out * d_last + q_do - w_dvn

        for chunk in reversed(range(G)):
            dsc_parts = []
            _interleave(head_program(chunk, h, dsc_parts) for h in range(DN_HEADS))
            dsc_ref[chunk * C:(chunk + 1) * C, :] = sum(dsc_parts[1:], dsc_parts[0])
        dgn_ref[...] += sum(dgn_parts[1:], dgn_parts[0])

    rev = lambda n: steps - 1 - n
    wide = pl.BlockSpec((R, 512), lambda n: (rev(n), 0))
    z_spec = pl.BlockSpec((R, 512), lambda n: (rev(n), BLK_Z))
    sq = pl.BlockSpec((DN_HEADS, R, CPAD), lambda n: (0, rev(n), 0))
    narrow = pl.BlockSpec((R, 128), lambda n: (rev(n), 0))
    gn_spec = pl.BlockSpec((1, DN_DIM), lambda n: (0, 0))
    f = lambda *shp: jax.ShapeDtypeStruct(shp, F32)
    b = lambda *shp: jax.ShapeDtypeStruct(shp, BF16)
    return pl.pallas_call(
        body, grid=(steps,), name="dn_scan_bwd",
        in_specs=[pl.BlockSpec((R, 512), lambda n: (rev(n), 1)), wide, z_spec, gn_spec,
                  pl.BlockSpec((G, DN_HEADS, DN_DIM, DN_DIM), lambda n: (rev(n), 0, 0, 0)),
                  wide, wide, wide, wide, sq, narrow, pl.BlockSpec(memory_space=pl.ANY)],
        out_specs=[wide, wide, wide, wide, sq, z_spec, narrow, gn_spec],
        out_shape=[b(S, 512), b(S, 512), f(S, 512), f(S, 512), f(DN_HEADS, S, CPAD),
                   jax.ShapeDtypeStruct(dproj.shape, dproj.dtype), f(S, 128), f(1, DN_DIM)],
        scratch_shapes=[pltpu.VMEM((DN_HEADS, DN_DIM, DN_DIM), F32)],
        input_output_aliases={11: 5},
        compiler_params=_params(("arbitrary",)),
    )(dcat, o, pz, gn, sst, vnew, w, qg, kd, a, gcs, dproj)


def _dn_chunk_bwd(qkv, pg, t_inv, gcs, du, dw, dqg, dkd, da, dsc, a_log, dt_bias, dproj):
    S = qkv.shape[0]
    C = DN_CHUNK
    G = CHUNKS_LOCAL
    R = G * C

    def body(alog_ref, dtb_ref, qkv_ref, pg_ref, t_ref, gcs_ref, du_ref, dw_ref, dqg_ref, dkd_ref, da_ref, dsc_ref, _,
             dqkv_ref, dpg_ref, acc_ref):
        @pl.when(pl.program_id(0) == 0)
        def _():
            acc_ref[...] = jnp.zeros_like(acc_ref)

        ii, jj = _chunk_masks()
        lane = lax.broadcasted_iota(jnp.int32, (1, 128), 1)
        row8 = lax.broadcasted_iota(jnp.int32, (8, 128), 0)
        lane8 = lax.broadcasted_iota(jnp.int32, (8, 128), 1)
        rowc = lax.broadcasted_iota(jnp.int32, (C, 1), 0)
        tril, strict = jj <= ii, jj < ii
        dpg_parts, acc_parts = [[] for _ in range(G)], []

        def head_program(chunk, h):
            rows = slice(chunk * C, (chunk + 1) * C)
            q, k, v = qkv_ref[rows, _head(h)], qkv_ref[rows, _head(DN_HEADS + h)], qkv_ref[rows, _head(2 * DN_HEADS + h)]
            gc_col, beta, g_col = gcs_ref[rows, h:h + 1], gcs_ref[rows, DN_HEADS + h:DN_HEADS + h + 1], \
                gcs_ref[rows, 2 * DN_HEADS + h:2 * DN_HEADS + h + 1]
            dec = _decay(gc_col, ii, jj)
            eg = jnp.exp(gc_col)
            g_last = gc_col[C - 1:C, :]
            ek = jnp.exp(g_last - gc_col)
            kb, vb = k * beta, v * beta
            kbg = kb * eg
            qb, kbb = _bf(q), _bf(kb)
            k_rows = _rows_pad(_bf(k))
            t = t_ref[h, rows]
            tb = _bf(t)
            dub, dwb = du_ref[rows, _head(h)], dw_ref[rows, _head(h)]
            dqg_, dkd_ = dqg_ref[rows, _head(h)], dkd_ref[rows, _head(h)]
            dt = _dot_nt(dub, _rows_pad(_bf(vb))) + _dot_nt(dwb, _rows_pad(_bf(kbg)))
            t_du_dw = _dot_tn(tb, jnp.concatenate([dub, dwb], axis=1))
            dvb, dkbg = t_du_dw[:C, :DN_DIM], t_du_dw[:C, DN_DIM:]
            kk = _dot_nt(kbb, k_rows)
            qk = _dot_nt(qb, k_rows)
            yield
            dt_t = _dot3_nt(dt, t)
            yield
            dl = -_dot3_tn(t, dt_t)
            yield
            dm = jnp.where(strict, dl * dec, 0.0)
            dqk = jnp.where(tril, da_ref[h, rows] * dec, 0.0)
            gmat = dm * kk + dqk * qk
            dgc = jnp.sum(gmat, axis=1, keepdims=True) - _row_to_col(jnp.sum(gmat, axis=0, keepdims=True), ii, jj)
            dmb, dqkb = _bf(dm), _bf(dqk)
            dkb = _dot(dmb, k_rows) + dkbg * eg
            dk = _dot_tn(jnp.concatenate([dmb, dqkb], axis=0), jnp.concatenate([kbb, qb], axis=0))[:C] + dkd_ * ek
            dq = _dot(dqkb, k_rows) + dqg_ * eg
            yield
            tk = jnp.sum(dkd_ * k * ek, axis=1, keepdims=True)
            dgc = dgc + jnp.sum(dqg_ * q * eg, axis=1, keepdims=True) - tk + jnp.sum(dkbg * kbg, axis=1, keepdims=True)
            dgl = jnp.sum(tk, axis=0, keepdims=True) + dsc_ref[(chunk + 1) * C - 1:(chunk + 1) * C, h:h + 1]
            dgc = dgc + jnp.where(rowc == C - 1, dgl, 0.0)
            dk = dk + dkb * beta
            dbeta = jnp.sum(dkb * k, axis=1, keepdims=True) + jnp.sum(dvb * v, axis=1, keepdims=True)
            dqkv_ref[rows, _head(h)] = dq
            dqkv_ref[rows, _head(DN_HEADS + h)] = dk
            dqkv_ref[rows, _head(2 * DN_HEADS + h)] = dvb * beta
            dg_col = jnp.sum(jnp.where(jj >= ii, _col_to_row(dgc, ii, jj), 0.0), axis=1, keepdims=True)
            db = dbeta * beta * (1.0 - beta)
            da_in = dg_col * (-jnp.exp(alog_ref[h])) * _sigmoid(pg_ref[rows, DN_HEADS + h:DN_HEADS + h + 1] + dtb_ref[h])
            dpg_parts[chunk].append(jnp.where(lane == h, db, 0.0) + jnp.where(lane == DN_HEADS + h, da_in, 0.0))
            acc_parts.append(jnp.where((row8 == 0) & (lane8 == h), jnp.sum(dg_col * g_col, axis=0, keepdims=True), 0.0)
                             + jnp.where((row8 == 1) & (lane8 == h), jnp.sum(da_in, axis=0, keepdims=True), 0.0))

        _interleave(head_program(chunk, h) for chunk in range(G) for h in range(DN_HEADS))
        for chunk in range(G):
            dpg = sum(dpg_parts[chunk][1:], dpg_parts[chunk][0])
            dpg_ref[chunk * C:(chunk + 1) * C, :] = _bf(jnp.concatenate([dpg, jnp.zeros_like(dpg)], axis=1))
        acc_ref[...] += sum(acc_parts[1:], acc_parts[0])

    smem = pl.BlockSpec(memory_space=pltpu.SMEM)
    wide = pl.BlockSpec((R, 512), lambda n: (n, 0))
    sq = pl.BlockSpec((DN_HEADS, R, CPAD), lambda n: (0, n, 0))
    narrow = pl.BlockSpec((R, 128), lambda n: (n, 0))
    qkv_spec = pl.BlockSpec((R, 1536), lambda n: (n, 0))
    f = lambda *shp: jax.ShapeDtypeStruct(shp, F32)
    return pl.pallas_call(
        body, grid=(S // R,), name="dn_chunk_bwd",
        in_specs=[smem, smem, qkv_spec, pl.BlockSpec((R, 128), lambda n: (n, BLK_G)), sq, narrow, wide, wide, wide, wide, sq,
                  narrow, pl.BlockSpec(memory_space=pl.ANY)],
        out_specs=[qkv_spec, pl.BlockSpec((R, 256), lambda n: (n, BLK_G_PAD)), pl.BlockSpec((8, 128), lambda n: (0, 0))],
        out_shape=[f(S, 1536), jax.ShapeDtypeStruct(dproj.shape, dproj.dtype), f(8, 128)],
        input_output_aliases={12: 1},
        compiler_params=_params(("arbitrary",)),
    )(a_log, dt_bias, qkv, pg, t_inv, gcs, du, dw, dqg, dkd, da, dsc, dproj)


def _fill_kv(dk, dv, dproj):
    S = dk.shape[0]
    tm = min(512, S)

    def body(dk_ref, dv_ref, _, o_ref):
        o_ref[...] = _bf(jnp.concatenate([dk_ref[...], dv_ref[...]], axis=1))

    tile = pl.BlockSpec((tm, 128), lambda i: (i, 0))
    return pl.pallas_call(
        body, grid=(S // tm,), name="fill_kv",
        in_specs=[tile, tile, pl.BlockSpec(memory_space=pl.ANY)],
        out_specs=pl.BlockSpec((tm, 256), lambda i: (i, BLK_KV)),
        out_shape=jax.ShapeDtypeStruct(dproj.shape, dproj.dtype),
        input_output_aliases={2: 0},
        compiler_params=_params(("parallel",)),
    )(dk, dv, dproj)


def _w_in_to_internal(wt):
    return jnp.concatenate([wt[0:512], wt[2304:2816], wt[768:2304], wt[512:768], wt[2816:2824],
                            jnp.zeros((D_IN_PAD - D_IN, wt.shape[1]), wt.dtype)], axis=0)


def _w_in_from_internal(gt):
    return jnp.concatenate([gt[0:512], gt[2560:2816], gt[1024:2560], gt[512:1024], gt[2816:2824]], axis=0)


def _local_step(x, p, target, wts, first_weights, other_weights, ship_early):
    S = x.shape[0]
    cos, sin = _rope_tables(S)
    sinks, a_log, dt_bias = wts["sinks"].reshape(8), wts["a_log"].reshape(4), wts["dt_bias"].reshape(4)
    gn = wts["dn_norm"].reshape(1, DN_DIM)
    add = lambda acc, res: (acc + res,)

    u = _rmsnorm_fwd(x, wts["norm_mix"], "norm_mix_fwd")
    w_in_t, conv_w = first_weights(u)
    proj, = _mm(u, w_in_t, form="nt", name="in_proj", out_dtypes=[F32], tn=512)
    attn, lse = _attn_fwd(proj, cos, sin, sinks)
    qkv = _dn_prep_fwd(proj, conv_w)
    cw, cu, cqg, ckd, ca, ct, gcs = _dn_chunk_fwd(qkv, proj, a_log, dt_bias)
    o, vnew, sst, dn_out = _dn_scan_fwd(cw, cu, cqg, ckd, ca, gcs, proj, gn)
    w_o, = other_weights(("w_o",), dn_out)
    h1, = _mm([attn, dn_out], w_o, form="nn", name="out_proj", out_dtypes=[F32], tn=512, epi=add, extra=[x])

    def relu2(acc):
        r = jnp.maximum(acc, 0.0)
        return r * r, r

    w_up, = other_weights(("w_up",), h1)
    hid, relu, m = _mm(h1, w_up, form="nn", name="mlp_up", out_dtypes=[BF16, BF16], tn=512, epi=relu2, norm=wts["norm_mlp"])
    w_down, = other_weights(("w_down",), hid)
    h2, = _mm(hid, w_down, form="nn", name="mlp_down", out_dtypes=[F32], tn=512, epi=add, extra=[h1])
    w_pg, w_pp = other_weights(("w_ple_gate", "w_ple_proj"), h2)
    n3, dh2, dgl, dpp, loss, d_norm_final, d_norm_ple = _ple_and_loss(h2, p, target, w_pg, w_pp, wts["norm_ple"],
                                                                     wts["norm_final"].reshape(1, D_MODEL))
    g = {"norm_final": d_norm_final, "norm_ple": d_norm_ple}
    early = {"w_ple_gate": _mm_tn(n3, dgl, name="d_w_ple_gate", tm=512, tn=1024, out_dtype=BF16).reshape(N_DEV, 128, 1024),
             "w_ple_proj": _mm_tn(p, dpp, name="d_w_ple_proj", tm=256, tn=128, out_dtype=BF16, column_shards=True)}
    d_act, = _mm(dh2, w_down, form="nt", name="d_hidden", out_dtypes=[BF16], tn=512,
                 epi=lambda acc, r: (acc * (2.0 * r.astype(F32)),), extra=[relu])
    early["w_down"] = _mm_tn(hid, dh2, name="d_w_down", tm=512, tn=1024, out_dtype=BF16).reshape(N_DEV, 512, 1024)
    early["w_up"] = _mm_tn(m, d_act, name="d_w_up", tm=1024, tn=512, out_dtype=BF16, column_shards=True)
    token = ship_early(early)
    dh1, g["norm_mlp"], dcat = _mm(d_act, w_up, form="nt", name="d_m", out_dtypes=[F32], tn=512, after=token,
                                   norm_bwd=(h1, wts["norm_mlp"], dh2), then_nt=w_o)
    d_w_o = jnp.concatenate([_mm_tn(attn, dh1, name="d_w_o_attn", tm=512, tn=512, out_dtype=BF16),
                             _mm_tn(dn_out, dh1, name="d_w_o_dn", tm=512, tn=512, out_dtype=BF16)], axis=0)
    token = ship_early({"w_o": d_w_o.reshape(N_DEV, 128, 1024)})
    dproj, dk, dv, dsinks = _attn_bwd(proj, cos, sin, sinks + token[0, 0], dcat, attn, lse)
    g["sinks"] = dsinks[:, 0].reshape(1, 8)
    du_, dw_, dqg, dkd, da, dproj, dsc, g["dn_norm"] = _dn_scan_bwd(dcat, o, proj, gn, sst, vnew, cw, cqg, ckd, ca, gcs, dproj)
    dqkv, dproj, gate_acc = _dn_chunk_bwd(qkv, proj, ct, gcs, du_, dw_, dqg, dkd, da, dsc, a_log, dt_bias, dproj)
    g["a_log"], g["dt_bias"] = gate_acc[0:1, 0:4], gate_acc[1:2, 0:4]
    dproj, g["conv_w"] = _dn_prep_bwd(proj, conv_w, dqkv, dproj)
    dproj = _fill_kv(dk, dv, dproj)
    token = ship_early({"w_in": _mm_tn(dproj, u, name="d_w_in", tm=512, tn=1024, out_dtype=BF16)})
    grad_x, g["norm_mix"] = _mm(dproj, w_in_t, form="nn", name="d_u", out_dtypes=[F32], tn=512, after=token,
                                norm_bwd=(x, wts["norm_mix"], dh1))
    return loss, grad_x, g


def _peer(k):
    x, y, c = lax.axis_index("x"), lax.axis_index("y"), lax.axis_index("c")
    px = 1 - x if k & 4 else x
    py = 1 - y if k & 2 else y
    pc = 1 - c if k & 1 else c
    return (px, py, pc), 4 * px + 2 * py + pc


def _exchange(srcs, name, gather):
    n = len(srcs)
    gathers = list(gather) if isinstance(gather, (list, tuple)) else [gather] * n
    shapes = [(N_DEV,) + s.shape if gt else s.shape for s, gt in zip(srcs, gathers)]

    def body(*refs):
        src_refs, out_refs = refs[:n], refs[n:2 * n]
        send_sems, recv_sems, local_sems = refs[2 * n:]
        _, me = _peer(0)
        piece = lambda a, d: src_refs[a] if gathers[a] else src_refs[a].at[d]
        local = [pltpu.make_async_copy(piece(a, me), out_refs[a].at[me], local_sems.at[a]) for a in range(n)]
        for cp in local:
            cp.start()
        copies = []
        for a in range(n):
            for k in range(1, N_DEV):
                dev, idx = _peer(k)
                cp = pltpu.make_async_remote_copy(src_ref=piece(a, idx), dst_ref=out_refs[a].at[me],
                                                  send_sem=send_sems.at[a, k - 1], recv_sem=recv_sems.at[a, k - 1],
                                                  device_id=dev, device_id_type=MESH)
                cp.start()
                copies.append(cp)
        for cp in copies:
            cp.wait_recv()
        for cp in copies:
            cp.wait_send()
        for cp in local:
            cp.wait()

    anywhere = pl.BlockSpec(memory_space=pl.ANY)
    return pl.pallas_call(
        body, name=name, in_specs=[anywhere] * n, out_specs=[anywhere] * n,
        out_shape=[jax.ShapeDtypeStruct(shp, s.dtype) for shp, s in zip(shapes, srcs)],
        scratch_shapes=[pltpu.SemaphoreType.DMA((n, N_DEV - 1)), pltpu.SemaphoreType.DMA((n, N_DEV - 1)),
                        pltpu.SemaphoreType.DMA((n,))],
    )(*srcs)


_HBM = pl.BlockSpec(memory_space=pltpu.HBM)
_SEM = pl.BlockSpec(memory_space=pltpu.SEMAPHORE)
_EFFECT = pltpu.SideEffectType.DATAFLOW_SIDE_EFFECTING


def _split_copies(src_refs, land_refs, send_sems, recv_sems, modes, which=None):
    _, me = _peer(0)
    copies = []
    which = range(len(src_refs)) if which is None else which
    for a, src, land in zip(which, src_refs, land_refs):
        if modes[a] == "columns":
            n_cols = src.shape[1]
            dst = land.at[:, pl.ds(pl.multiple_of(me * n_cols, n_cols), n_cols)]
        else:
            dst = land.at[me]
        for k in range(1, N_DEV):
            dev, idx = _peer(k)
            sem = a * (N_DEV - 1) + k - 1
            copies.append(pltpu.make_async_remote_copy(
                src_ref=src.at[idx] if modes[a] == "pieces" else src, dst_ref=dst, send_sem=send_sems.at[sem],
                recv_sem=recv_sems.at[sem], device_id=dev, device_id_type=MESH))
    return copies


def _exchange_start(srcs, name, modes):
    n = len(srcs)
    modes = [modes] * n if isinstance(modes, str) else list(modes)
    me = 4 * lax.axis_index("x") + 2 * lax.axis_index("y") + lax.axis_index("c")
    lands = []
    for s, mode in zip(srcs, modes):
        if mode == "columns":
            empty = lax.empty((s.shape[0], N_DEV * s.shape[1]), s.dtype)
            lands.append(lax.dynamic_update_slice(empty, s, (0, me * s.shape[1])))
        else:
            own = s if mode == "slots" else lax.dynamic_index_in_dim(s, me, 0, keepdims=False)
            shape = (N_DEV,) + s.shape if mode == "slots" else s.shape
            lands.append(lax.dynamic_update_index_in_dim(lax.empty(shape, s.dtype), own, me, 0))

    def body(*refs):
        src_refs, land_refs = refs[:n], refs[n:2 * n]
        send_sems, recv_sems = refs[2 * n], refs[2 * n + 1]
        for cp in _split_copies(src_refs, land_refs, send_sems, recv_sems, modes):
            cp.start()
        refs[-1][...] = jnp.zeros_like(refs[-1])

    both = list(srcs) + lands
    sems = pltpu.SemaphoreType.DMA((n * (N_DEV - 1),))
    out = pl.pallas_call(
        body, name=name,
        out_shape=(sems, sems, *[pltpu.HBM(t.shape, t.dtype) for t in both], jax.ShapeDtypeStruct((8, 128), F32)),
        in_specs=[_HBM] * (2 * n), out_specs=(_SEM, _SEM, *[_HBM] * (2 * n), pl.BlockSpec(memory_space=pltpu.VMEM)),
        input_output_aliases={i: 2 + i for i in range(2 * n)},
        compiler_params=pltpu.CompilerParams(has_side_effects=_EFFECT),
    )(*[pltpu.with_memory_space_constraint(t, pltpu.HBM) for t in both])
    return (n, modes, out[:-1]), out[-1]


def _exchange_wait(handle, after, name, which=None):
    n_all, modes, (send_sems, recv_sems, *both_all) = handle
    which = list(range(n_all)) if which is None else list(which)
    n = len(which)
    both = [both_all[a] for a in which] + [both_all[n_all + a] for a in which]

    def body(*refs):
        src_refs, land_refs = refs[:n], refs[n:2 * n]
        for cp in _split_copies(src_refs, land_refs, refs[2 * n], refs[2 * n + 1], modes, which):
            cp.wait_send()
            cp.wait_recv()

    out = pl.pallas_call(
        body, name=name, out_shape=tuple(pltpu.HBM(t.shape, t.dtype) for t in both),
        in_specs=[_HBM] * (2 * n) + [_SEM, _SEM, pl.BlockSpec(memory_space=pl.ANY)], out_specs=tuple([_HBM] * (2 * n)),
        input_output_aliases={i: i for i in range(2 * n)},
        compiler_params=pltpu.CompilerParams(has_side_effects=_EFFECT),
    )(*both, send_sems, recv_sems, after)
    return list(out[n:])


def _adam_update(g, w, m, v):
    nm = ADAM_B1 * m + (1.0 - ADAM_B1) * g
    nv = ADAM_B2 * v + (1.0 - ADAM_B2) * (g * g)
    m_hat = nm / (1.0 - ADAM_B1 ** ADAM_STEP)
    v_hat = nv / (1.0 - ADAM_B2 ** ADAM_STEP)
    return -ADAM_LR * (m_hat / (jnp.sqrt(v_hat) + ADAM_EPS) + ADAM_WD * w), nm, nv


def _adamw(parts, w, m, v, name):
    n, R, W = parts.shape
    tm = 128 if R % 128 == 0 else R

    def body(p_ref, w_ref, m_ref, v_ref, g_ref, d_ref, nm_ref, nv_ref):
        g = p_ref[0].astype(F32)
        for s in range(1, n):
            g = g + p_ref[s].astype(F32)
        g_ref[...] = g
        d_ref[...], nm_ref[...], nv_ref[...] = _adam_update(g, w_ref[...], m_ref[...], v_ref[...])

    tile = pl.BlockSpec((tm, W), lambda i: (i, 0))
    return pl.pallas_call(
        body, grid=(R // tm,), name=name,
        in_specs=[pl.BlockSpec((n, tm, W), lambda i: (0, i, 0)), tile, tile, tile],
        out_specs=[tile] * 4, out_shape=[jax.ShapeDtypeStruct((R, W), F32)] * 4,
        compiler_params=_params(("parallel",)),
    )(parts, w, m, v)


_MATRICES = ("w_in", "w_o", "w_up", "w_down", "w_ple_gate", "w_ple_proj")


_OTHERS = ("w_o", "w_up", "w_down", "w_ple_gate", "w_ple_proj")
_OTHER_MODES = {"w_o": "slots", "w_up": "slots", "w_down": "slots", "w_ple_gate": "slots", "w_ple_proj": "columns"}


_VECTORS = ("norm_mix", "norm_mlp", "norm_ple", "norm_final", "a_log", "dt_bias", "sinks", "dn_norm")
_SMALL_ROWS, _LOSS_ROW, _CONV_ROW = 16, 8, 9


def _pack_small(vectors, loss, conv):
    def body(*refs):
        out = refs[-1]
        out[...] = jnp.zeros_like(out)
        for r, ref in enumerate(refs[:len(_VECTORS)]):
            out[r:r + 1, 0:ref.shape[1]] = ref[...]
        out[_LOSS_ROW:_LOSS_ROW + 1, 0:128] = refs[len(_VECTORS)][...]
        out[_CONV_ROW:_CONV_ROW + 6, :] = refs[len(_VECTORS) + 1][...]

    return pl.pallas_call(body, name="pack_small", out_shape=jax.ShapeDtypeStruct((_SMALL_ROWS, 1024), F32))(*vectors, loss, conv)


def _sum_slots(parts):
    def body(p_ref, o_ref):
        acc = p_ref[0]
        for s in range(1, parts.shape[0]):
            acc = acc + p_ref[s]
        o_ref[...] = acc

    return pl.pallas_call(body, name="sum_small", out_shape=jax.ShapeDtypeStruct(parts.shape[1:], parts.dtype))(parts)


def _adamw_vectors(summed, conv_g, wmv):
    names = _VECTORS + ("conv_w",)
    flat = [a for triple in wmv for a in triple]

    def body(*refs):
        sum_ref, conv_ref = refs[0], refs[1]
        ins, outs = refs[2:2 + len(flat)], refs[2 + len(flat):]
        for i in range(len(names)):
            w_ref, m_ref, v_ref = ins[3 * i:3 * i + 3]
            g = conv_ref[...] if i == len(_VECTORS) else sum_ref[i:i + 1, 0:w_ref.shape[1]]
            outs[4 * i][...] = g
            outs[4 * i + 1][...], outs[4 * i + 2][...], outs[4 * i + 3][...] = _adam_update(g, w_ref[...], m_ref[...], v_ref[...])

    out_shape = [jax.ShapeDtypeStruct(t[0].shape, F32) for t in wmv for _ in range(4)]
    res = pl.pallas_call(body, name="adamw_vectors", out_shape=out_shape)(summed, conv_g, *flat)
    return {n: res[4 * i:4 * i + 4] for i, n in enumerate(names)}


_ORDER = ("norm_mix", "w_in", "conv_w", "a_log", "dt_bias", "dn_norm", "sinks", "w_o", "norm_mlp", "w_up", "w_down",
          "norm_ple", "w_ple_gate", "w_ple_proj", "norm_final")


def kernel(x, p, norm_mix, w_in, conv_w, a_log, dt_bias, dn_norm, sinks, w_o, norm_mlp, w_up, w_down, norm_ple, w_ple_gate, w_ple_proj, norm_final, loss_target, m_norm_mix, m_w_in, m_conv_w, m_a_log, m_dt_bias, m_dn_norm, m_sinks, m_w_o, m_norm_mlp, m_w_up, m_w_down, m_norm_ple, m_w_ple_gate, m_w_ple_proj, m_norm_final, v_norm_mix, v_w_in, v_conv_w, v_a_log, v_dt_bias, v_dn_norm, v_sinks, v_w_o, v_norm_mlp, v_w_up, v_w_down, v_norm_ple, v_w_ple_gate, v_w_ple_proj, v_norm_final):
    w = dict(norm_mix=norm_mix, w_in=w_in[0], conv_w=conv_w[0], a_log=a_log, dt_bias=dt_bias, dn_norm=dn_norm, sinks=sinks,
             w_o=w_o[0], norm_mlp=norm_mlp, w_up=w_up[0], w_down=w_down[0], norm_ple=norm_ple, w_ple_gate=w_ple_gate[0],
             w_ple_proj=w_ple_proj[0], norm_final=norm_final)
    m = dict(norm_mix=m_norm_mix, w_in=m_w_in[0], conv_w=m_conv_w[0], a_log=m_a_log, dt_bias=m_dt_bias, dn_norm=m_dn_norm,
             sinks=m_sinks, w_o=m_w_o[0], norm_mlp=m_norm_mlp, w_up=m_w_up[0], w_down=m_w_down[0], norm_ple=m_norm_ple,
             w_ple_gate=m_w_ple_gate[0], w_ple_proj=m_w_ple_proj[0], norm_final=m_norm_final)
    v = dict(norm_mix=v_norm_mix, w_in=v_w_in[0], conv_w=v_conv_w[0], a_log=v_a_log, dt_bias=v_dt_bias, dn_norm=v_dn_norm,
             sinks=v_sinks, w_o=v_w_o[0], norm_mlp=v_norm_mlp, w_up=v_w_up[0], w_down=v_w_down[0], norm_ple=v_norm_ple,
             w_ple_gate=v_w_ple_gate[0], w_ple_proj=v_w_ple_proj[0], norm_final=v_norm_final)
    me = 4 * lax.axis_index("x") + 2 * lax.axis_index("y") + lax.axis_index("c")
    conv_shard = conv_w.shape[2]

    for d in (w, m, v):
        d["w_in"] = d["w_in"].T
    conv_pad = jnp.pad(w["conv_w"], ((0, 8 - DN_CONV), (0, 256 - conv_shard)))
    first, token_first = _exchange_start([_bf(w["w_in"]), conv_pad], "gather_first_start", "slots")
    later = [_bf(w[n]) for n in _OTHERS]
    later[-1] = _bf(w["w_ple_proj"] + token_first[0:1, 0:1])
    others, token_others = _exchange_start(later, "gather_others_start", [_OTHER_MODES[n] for n in _OTHERS])
    vectors = dict(w)
    vectors["norm_mix"] = w["norm_mix"] + token_others[0:1, 0:1]

    def first_weights(after):
        w_in_all, conv_all = _exchange_wait(first, after, "gather_first_wait")
        conv_all = jnp.transpose(conv_all[:, :DN_CONV, :conv_shard], (1, 0, 2)).reshape(DN_CONV, N_DEV * conv_shard)
        return _w_in_to_internal(w_in_all.reshape(D_IN, D_MODEL)), conv_all

    as_taken = {"w_o": lambda t: t.reshape(1024, 1024), "w_up": lambda t: t, "w_down": lambda t: t.reshape(4096, 1024),
                "w_ple_gate": lambda t: t.reshape(1024, 1024), "w_ple_proj": lambda t: t}

    def other_weights(names, after):
        which = [_OTHERS.index(n) for n in names]
        got = _exchange_wait(others, after, "gather_wait_" + names[0], which)
        return [as_taken[n](t) for n, t in zip(names, got)]

    shipped = []

    def ship_early(pieces):
        names = tuple(pieces)
        if names == ("w_in",):
            pieces = {"w_in": _w_in_from_internal(pieces["w_in"]).reshape(N_DEV, D_IN // N_DEV, D_MODEL)}
        handle, token = _exchange_start([pieces[n] for n in names], "scatter_start_" + names[0], "pieces")
        shipped.append((names, handle))
        return token

    loss, grad_x, g = _local_step(x[0], p[0, 0], loss_target[0], vectors, first_weights, other_weights, ship_early)

    row = lambda t: t.reshape(1, t.size)
    small = _pack_small([row(g[n]) for n in _VECTORS], loss, g["conv_w"].reshape(6, 1024))
    small_all, = _exchange([small], "gather_small", gather=True)
    summed = _sum_slots(small_all)
    conv_g = lax.dynamic_slice(summed[_CONV_ROW:_CONV_ROW + 6].reshape(DN_CONV, N_DEV * conv_shard), (0, me * conv_shard),
                               (DN_CONV, conv_shard))
    small_out = _adamw_vectors(summed, conv_g, [(row(w[n]), row(m[n]), row(v[n])) for n in _VECTORS]
                               + [(w["conv_w"], m["conv_w"], v["conv_w"])])
    big, after = {}, small_out["conv_w"][0]
    for names, handle in shipped:
        for n, r in zip(names, _exchange_wait(handle, after, "scatter_wait_" + names[0])):
            big[n] = _adamw(r, w[n], m[n], v[n], "adamw_" + n)
            after = big[n][1]

    result = [summed[_LOSS_ROW, 0], grad_x[None]]
    for i in range(4):
        for n in _ORDER:
            if n == "w_in":
                result.append(big[n][i].T[None])
            elif n in _MATRICES:
                result.append(big[n][i][None])
            elif n == "conv_w":
                result.append(small_out[n][i][None])
            else:
                result.append(small_out[n][i].reshape(w[n].shape))
    return tuple(result)
```

```python
import jax
import jax.numpy as jnp
import numpy as np
from jax import lax
from jax.experimental import pallas as pl
from jax.experimental.pallas import tpu as pltpu

F32, BF16 = jnp.float32, jnp.bfloat16
EPS = 1e-6
D_MODEL = 1024
N_DEV = 8
ATTN_BLOCK = 128
HEAD_PAIR = 128
DN_HEADS = 4
DN_DIM = 128
DN_CHUNK = 64
DN_CONV = 4
ROPE_THETA = 10000.0
D_IN = 2824
D_IN_PAD = 3072
BLK_Q, BLK_Z = 0, 1
BLK_DN, BLK_K, BLK_V, BLK_G = 8, 20, 21, 22
BLK_KV, BLK_G_PAD = 10, 11
VMEM_LIMIT = 56 * 1024 * 1024
NEG = -1e30
ADAM_LR, ADAM_B1, ADAM_B2, ADAM_EPS, ADAM_WD, ADAM_STEP = 0.001, 0.9, 0.999, 1e-08, 0.01, 10
MESH = pl.DeviceIdType.MESH


def _bf(x):
    return x.astype(BF16)


def _dot(a, b):
    return jnp.dot(a, b, preferred_element_type=F32)


def _dot_nt(a, b):
    return lax.dot_general(a, b, (((1,), (1,)), ((), ())), preferred_element_type=F32)


def _dot_tn(a, b):
    return lax.dot_general(a, b, (((0,), (0,)), ((), ())), preferred_element_type=F32)


def _sigmoid(x):
    return 1.0 / (1.0 + jnp.exp(-x))


def _params(sem):
    return pltpu.CompilerParams(dimension_semantics=sem, vmem_limit_bytes=VMEM_LIMIT)


def _mm(x, w, *, form, name, out_dtypes, tn, epi=None, extra=(), tm=512, w_row_block=0, after=None, norm=None,
        norm_bwd=None, then_nt=None):
    assert norm is None or norm_bwd is None
    xs = list(x) if isinstance(x, (list, tuple)) else [x]
    nx = len(xs)
    S, K = xs[0].shape
    shards = w.ndim == 3
    N = (w.shape[2] * N_DEV if shards else w.shape[1]) if form == "nn" else w.shape[-2]
    assert not (shards and form == "nn" and tn != w.shape[2]) and (nx == 1 or (form == "nn" and not shards and norm is None))
    r0 = w_row_block * K
    tm = min(tm, S)
    n_extra, n_out = len(extra), len(out_dtypes)
    tile = lambda width: pl.BlockSpec((tm, width), lambda i: (i, 0))
    whole = lambda a: pl.BlockSpec(a.shape, lambda i, nd=a.ndim: (0,) * nd)
    ins, in_specs = [*xs, w, *extra], [tile(K)] * nx + [whole(w)] + [tile(N)] * n_extra
    if norm is not None:
        ins, in_specs = ins + [norm], in_specs + [whole(norm)]
    if norm_bwd is not None:
        ins, in_specs = ins + list(norm_bwd), in_specs + [tile(N), whole(norm_bwd[1]), tile(N)]
    if then_nt is not None:
        ins, in_specs = ins + [then_nt], in_specs + [whole(then_nt)]
    if after is not None:
        ins, in_specs = ins + [after], in_specs + [whole(after)]
    out_shape = [jax.ShapeDtypeStruct((S, N), dt) for dt in out_dtypes]
    out_specs = [tile(N)] * n_out
    if norm is not None:
        out_shape, out_specs = out_shape + [jax.ShapeDtypeStruct((S, K), BF16)], out_specs + [tile(K)]
    if norm_bwd is not None:
        out_shape, out_specs = out_shape + [jax.ShapeDtypeStruct((1, N), F32)], out_specs + [pl.BlockSpec((1, N), lambda i: (0, 0))]
    if then_nt is not None:
        out_shape, out_specs = out_shape + [jax.ShapeDtypeStruct((S, then_nt.shape[0]), F32)], out_specs + [tile(then_nt.shape[0])]

    def product(xb, w_ref, cols, c):
        if form == "nn" and nx > 1:
            return sum(_dot(part, w_ref[r0 + p * K:r0 + (p + 1) * K, cols]) for p, part in enumerate(xb))
        if form == "nn":
            return _dot(xb, w_ref[c] if shards else w_ref[r0:r0 + K, cols])
        if not shards:
            return _dot_nt(xb, w_ref[cols, :])
        ks = w.shape[2]
        acc = _dot_nt(xb[:, 0:ks], w_ref[0, cols, :])
        for s in range(1, N_DEV):
            acc = acc + _dot_nt(xb[:, s * ks:(s + 1) * ks], w_ref[s, cols, :])
        return acc

    def body(*refs):
        x_ref, w_ref = refs[0], refs[nx]
        extra_refs = refs[nx + 1:nx + 1 + n_extra]
        at = nx + 1 + n_extra
        if norm is not None:
            gain_ref, at = refs[at], at + 1
        if norm_bwd is not None:
            (y_ref, ygain_ref, dres_ref), at = refs[at:at + 3], at + 3
        if then_nt is not None:
            w2_ref, at = refs[at], at + 1
        outs = refs[len(ins):]
        if norm is not None:
            _, xh = _rms_stats(x_ref[...])
            xb = _bf(xh * gain_ref[...])
            outs[n_out][...] = xb
        else:
            xb = _bf(x_ref[...]) if nx == 1 else [_bf(r[...]) for r in refs[:nx]]
        for c in range(N // tn):
            cols = slice(c * tn, (c + 1) * tn)
            acc = product(xb, w_ref, cols, c)
            res = epi(acc, *[r[:, cols] for r in extra_refs]) if epi else (acc,)
            for o, r in zip(outs[:n_out], res):
                o[:, cols] = r.astype(o.dtype)
        if norm_bwd is not None:
            dx, dg = _rms_bwd_tile(y_ref[...], ygain_ref[...], outs[0][...])
            outs[0][...] = dres_ref[...] + dx
            dg_ref = outs[n_out]

            @pl.when(pl.program_id(0) == 0)
            def _():
                dg_ref[...] = jnp.zeros_like(dg_ref)

            dg_ref[...] += dg
        if then_nt is not None:
            yb = _bf(outs[0][...])
            for c in range(then_nt.shape[0] // tn):
                cols = slice(c * tn, (c + 1) * tn)
                outs[-1][:, cols] = _dot_nt(yb, w2_ref[cols, :])

    return pl.pallas_call(
        body, grid=(S // tm,), name=name, in_specs=in_specs, out_specs=out_specs, out_shape=out_shape,
        compiler_params=_params(("arbitrary",) if norm_bwd is not None else ("parallel",)),
    )(*ins)


def _mm_tn(x, dy, *, name, tm, tn, out_dtype=F32, column_shards=False, after=None):
    S, K = x.shape
    N = dy.shape[1]
    waits = [] if after is None else [after]

    def body(x_ref, dy_ref, *rest):
        rest[-1][...] = _dot_tn(_bf(x_ref[...]), _bf(dy_ref[...])).astype(out_dtype)

    if column_shards:
        out_spec = pl.BlockSpec((None, tm, tn), lambda i, j: (j, i, 0))
        out_shape = jax.ShapeDtypeStruct((N // tn, K, tn), out_dtype)
    else:
        out_spec = pl.BlockSpec((tm, tn), lambda i, j: (i, j))
        out_shape = jax.ShapeDtypeStruct((K, N), out_dtype)
    return pl.pallas_call(
        body, grid=(K // tm, N // tn), name=name,
        in_specs=[pl.BlockSpec((S, tm), lambda i, j: (0, i)), pl.BlockSpec((S, tn), lambda i, j: (0, j))]
        + [pl.BlockSpec(memory_space=pl.ANY)] * len(waits),
        out_specs=out_spec, out_shape=out_shape,
        compiler_params=_params(("parallel", "parallel")),
    )(x, dy, *waits)


def _rowwise(body, *, tiled, full, out_tiled, out_acc, name, tm=512, smem=()):
    S = tiled[0].shape[0]
    tm = min(tm, S)
    n_in = len(smem) + len(tiled) + len(full)

    def kern(*refs):
        @pl.when(pl.program_id(0) == 0)
        def _():
            for r in refs[n_in + len(out_tiled):]:
                r[...] = jnp.zeros_like(r)
        body(*refs)

    in_specs = [pl.BlockSpec(memory_space=pltpu.SMEM) for _ in smem]
    in_specs += [pl.BlockSpec((tm, a.shape[1]), lambda i: (i, 0)) for a in tiled]
    in_specs += [pl.BlockSpec(a.shape, lambda i, nd=a.ndim: (0,) * nd) for a in full]
    out_specs = [pl.BlockSpec((tm, w), lambda i: (i, 0)) for w, _ in out_tiled]
    out_specs += [pl.BlockSpec(shp, lambda i, nd=len(shp): (0,) * nd) for shp, _ in out_acc]
    out_shape = [jax.ShapeDtypeStruct((S, w), dt) for w, dt in out_tiled]
    out_shape += [jax.ShapeDtypeStruct(shp, dt) for shp, dt in out_acc]
    return pl.pallas_call(
        kern, grid=(S // tm,), name=name, in_specs=in_specs, out_specs=out_specs, out_shape=out_shape,
        compiler_params=_params(("arbitrary",)),
    )(*smem, *tiled, *full)


def _rms_stats(x):
    r = lax.rsqrt(jnp.mean(x * x, axis=-1, keepdims=True) + EPS)
    return r, x * r


def _rmsnorm_fwd(x, g, name):
    def body(x_ref, g_ref, o_ref):
        _, xh = _rms_stats(x_ref[...])
        o_ref[...] = _bf(xh * g_ref[...])

    return _rowwise(body, tiled=[x], full=[g], out_tiled=[(x.shape[1], BF16)], out_acc=[], name=name)[0]


def _rms_bwd_tile(x, g, dxn):
    r, xh = _rms_stats(x)
    dg = jnp.sum(dxn * xh, axis=0, keepdims=True)
    dn = dxn * g
    dx = r * (dn - xh * jnp.mean(dn * xh, axis=-1, keepdims=True))
    return dx, dg


def _ple_and_loss(h2, p, target, w_pg, w_pp, g_ple, g_final):
    S, n = h2.shape
    tm = min(512, S)
    tn = 512

    def body(h2_ref, p_ref, t_ref, wpg_ref, wpp_ref, gple_ref, gfin_ref,
             n3_ref, dh_ref, dgl_ref, dpp_ref, loss_ref, dg_ref, dgple_ref, pp, gate, h3):
        @pl.when(pl.program_id(0) == 0)
        def _():
            loss_ref[...] = jnp.zeros_like(loss_ref)
            dg_ref[...] = jnp.zeros_like(dg_ref)
            dgple_ref[...] = jnp.zeros_like(dgple_ref)

        x = h2_ref[...]
        _, xh = _rms_stats(x)
        n3 = _bf(xh * gple_ref[...])
        n3_ref[...] = n3
        pb = _bf(p_ref[...])
        for c in range(n // tn):
            cols = slice(c * tn, (c + 1) * tn)
            pp[:, cols] = _dot(pb, wpp_ref[:, cols])
            gt = _sigmoid(_dot(n3, wpg_ref[:, cols]))
            gate[:, cols] = gt
            h3[:, cols] = x[:, cols] + gt * pp[:, cols]
        y = h3[...]
        _, yh = _rms_stats(y)
        e = yh * gfin_ref[...] - t_ref[...]
        per_tok = jnp.mean(e * e, axis=-1, keepdims=True)
        loss_ref[...] += 0.5 * jnp.sum(per_tok, axis=0, keepdims=True)
        dh, dg = _rms_bwd_tile(y, gfin_ref[...], e * (1.0 / n))
        dg_ref[...] += dg
        gt = gate[...]
        dgl = _bf(dh * pp[...] * gt * (1.0 - gt))
        dgl_ref[...] = dgl
        dpp_ref[...] = _bf(dh * gt)
        for c in range(n // tn):
            cols = slice(c * tn, (c + 1) * tn)
            h3[:, cols] = _dot_nt(dgl, wpg_ref[cols, :])
        dx, dgp = _rms_bwd_tile(x, gple_ref[...], h3[...])
        dh_ref[...] = dh + dx
        dgple_ref[...] += dgp

    tile = lambda width: pl.BlockSpec((tm, width), lambda i: (i, 0))
    whole = lambda a: pl.BlockSpec(a.shape, lambda i, nd=a.ndim: (0,) * nd)
    return pl.pallas_call(
        body, grid=(S // tm,), name="ple_and_loss",
        in_specs=[tile(n), tile(p.shape[1]), tile(n), whole(w_pg), whole(w_pp), whole(g_ple), whole(g_final)],
        out_specs=[tile(n), tile(n), tile(n), tile(n), pl.BlockSpec((1, 128), lambda i: (0, 0)),
                   pl.BlockSpec((1, n), lambda i: (0, 0)), pl.BlockSpec((1, n), lambda i: (0, 0))],
        out_shape=[jax.ShapeDtypeStruct((S, n), BF16), jax.ShapeDtypeStruct((S, n), F32), jax.ShapeDtypeStruct((S, n), BF16),
                   jax.ShapeDtypeStruct((S, n), BF16), jax.ShapeDtypeStruct((1, 128), F32), jax.ShapeDtypeStruct((1, n), F32),
                   jax.ShapeDtypeStruct((1, n), F32)],
        scratch_shapes=[pltpu.VMEM((tm, n), F32)] * 3,
        compiler_params=_params(("arbitrary",)),
    )(h2, p, target, w_pg, w_pp, g_ple, g_final)


def _rope_tables(S):
    half = 32
    inv = (1.0 / (np.float32(ROPE_THETA) ** (np.arange(half, dtype=np.float32) * np.float32(2.0 / 64)))).astype(np.float32)
    ang = np.arange(S).astype(np.float32)[:, None] * inv[None, :]
    cos, sin = np.cos(ang), np.sin(ang)
    return jnp.asarray(np.tile(cos, (1, 4))), jnp.asarray(np.concatenate([-sin, sin, -sin, sin], axis=1))


def _attn_common(i, kc, kp, vc, vp, cc, sc, cp, sp):
    lane = lax.broadcasted_iota(jnp.int32, (1, HEAD_PAIR), 1)
    lane_lo = jnp.bitwise_and(lane, 63) < 32
    slot = [lane < 64, lane >= 64]

    def swap_halves(t):
        return jnp.where(lane_lo, pltpu.roll(t, 96, 1), pltpu.roll(t, 32, 1))

    def rope(t, cos, sin):
        return t * cos + swap_halves(t) * sin

    def unrope(d, cos, sin):
        return d * cos + swap_halves(d * sin)

    k2 = jnp.concatenate([rope(kp, cp, sp), rope(kc, cc, sc)], axis=0)
    v2 = jnp.concatenate([vp, vc], axis=0)
    r = lax.broadcasted_iota(jnp.int32, (ATTN_BLOCK, 2 * ATTN_BLOCK), 0)
    c = lax.broadcasted_iota(jnp.int32, (ATTN_BLOCK, 2 * ATTN_BLOCK), 1)
    valid = (c > r) & (c <= r + ATTN_BLOCK) & jnp.logical_or(c >= ATTN_BLOCK, i > 0)
    ks, vs = {}, {}
    for j in range(2):
        kn = jnp.where(slot[j], k2, 0.0)
        vn = jnp.where(slot[j], v2, 0.0)
        for s in range(2):
            ks[j, s] = _bf(kn if s == j else pltpu.roll(kn, 64, 1))
            vs[j, s] = _bf(vn if s == j else pltpu.roll(vn, 64, 1))
    return slot, rope, unrope, valid, ks, vs


def _attn_probs(scores, valid, sink):
    s = jnp.where(valid, scores * 0.125, NEG)
    m = jnp.maximum(jnp.max(s, axis=1, keepdims=True), sink)
    e = jnp.exp(s - m)
    z = jnp.sum(e, axis=1, keepdims=True) + jnp.exp(sink - m)
    return e * (1.0 / z), m + jnp.log(z)


def _attn_specs(S):
    nb = S // ATTN_BLOCK
    prev = lambda i: jnp.maximum(i - 1, 0)
    blk = lambda w, col, row=(lambda i: i): pl.BlockSpec((ATTN_BLOCK, w), lambda i: (row(i), col))
    in_specs = [pl.BlockSpec(memory_space=pltpu.SMEM),
                blk(512, BLK_Q), blk(128, BLK_K), blk(128, BLK_K, prev), blk(128, BLK_V), blk(128, BLK_V, prev),
                blk(128, 0), blk(128, 0), blk(128, 0, prev), blk(128, 0, prev)]
    return nb, in_specs


def _attn_fwd(pa, cos, sin, sinks):
    S = pa.shape[0]
    nb, in_specs = _attn_specs(S)

    def body(sinks_ref, q_ref, kc_ref, kp_ref, vc_ref, vp_ref, cc_ref, sc_ref, cp_ref, sp_ref, o_ref, lse_ref):
        i = pl.program_id(0)
        lane = lax.broadcasted_iota(jnp.int32, (1, HEAD_PAIR), 1)
        cc, sc = cc_ref[...], sc_ref[...]
        _, rope, _, valid, ks, vs = _attn_common(i, kc_ref[...], kp_ref[...], vc_ref[...], vp_ref[...],
                                                 cc, sc, cp_ref[...], sp_ref[...])
        pair_cols = [slice(HEAD_PAIR * pair, HEAD_PAIR * (pair + 1)) for pair in range(4)]
        qps = [_bf(rope(q_ref[:, cols], cc, sc)) for cols in pair_cols]
        outs, lses = {}, {}

        def head_program(h):
            pair, s = divmod(h, 2)
            j = h // 4
            scores = _dot_nt(qps[pair], ks[j, s])
            yield
            p, lse = _attn_probs(scores, valid, sinks_ref[h])
            outs[h] = _dot(_bf(p), vs[j, s])
            lses[h] = jnp.where(lane == h, lse, 0.0)

        _interleave(head_program(h) for h in range(8))
        for pair, cols in enumerate(pair_cols):
            o_ref[:, cols] = outs[2 * pair] + outs[2 * pair + 1]
        lse_ref[...] = sum((lses[h] for h in range(1, 8)), lses[0])

    return pl.pallas_call(
        body, grid=(nb,), name="attn_fwd", in_specs=in_specs,
        out_specs=[pl.BlockSpec((ATTN_BLOCK, 512), lambda i: (i, 0)), pl.BlockSpec((ATTN_BLOCK, 128), lambda i: (i, 0))],
        out_shape=[jax.ShapeDtypeStruct((S, 512), F32), jax.ShapeDtypeStruct((S, 128), F32)],
        compiler_params=_params(("parallel",)),
    )(sinks, pa, pa, pa, pa, pa, cos, sin, cos, sin)


def _attn_bwd(pa, cos, sin, sinks, dcat, attn, lse):
    S = pa.shape[0]
    nb, in_specs = _attn_specs(S)
    in_specs = in_specs + [pl.BlockSpec((ATTN_BLOCK, 512), lambda i: (i, 0))] * 2 + [pl.BlockSpec((ATTN_BLOCK, 128), lambda i: (i, 0))]

    def body(sinks_ref, q_ref, kc_ref, kp_ref, vc_ref, vp_ref, cc_ref, sc_ref, cp_ref, sp_ref, do_ref, o_ref, lse_ref,
             dq_ref, dk_ref, dv_ref, dsink_ref):
        i = pl.program_id(0)

        @pl.when(i == 0)
        def _():
            dk_ref[...] = jnp.zeros_like(dk_ref)
            dv_ref[...] = jnp.zeros_like(dv_ref)
            dsink_ref[...] = jnp.zeros_like(dsink_ref)

        cc, sc, cp, sp = cc_ref[...], sc_ref[...], cp_ref[...], sp_ref[...]
        slot, rope, unrope, valid, ks, vs = _attn_common(i, kc_ref[...], kp_ref[...], vc_ref[...], vp_ref[...], cc, sc, cp, sp)
        pair_cols = [slice(HEAD_PAIR * pair, HEAD_PAIR * (pair + 1)) for pair in range(4)]
        qps = [_bf(rope(q_ref[:, cols], cc, sc)) for cols in pair_cols]
        dobs = [_bf(do_ref[:, cols]) for cols in pair_cols]
        do_o = [do_ref[:, cols] * o_ref[:, cols] for cols in pair_cols]
        dqs, dks, dvs = {}, {}, {}

        def head_program(h):
            pair, s = divmod(h, 2)
            j = h // 4
            qp, dob = qps[pair], dobs[pair]
            scores = _dot_nt(qp, ks[j, s])
            dp = _dot_nt(dob, vs[j, s])
            yield
            lse_h = lse_ref[:, h:h + 1]
            p = jnp.exp(jnp.where(valid, scores * 0.125, NEG) - lse_h)
            yield
            dr = jnp.sum(jnp.where(slot[s], do_o[pair], 0.0), axis=1, keepdims=True)
            ds = _bf(p * (dp - dr) * 0.125)
            yield
            dsink_ref[h:h + 1, :] += -jnp.sum(jnp.exp(sinks_ref[h] - lse_h) * dr, axis=0, keepdims=True)
            dqs[h] = _dot(ds, ks[j, s])
            dk_h = _dot_tn(ds, qp)
            dv_h = _dot_tn(_bf(p), dob)
            yield
            dk_h, dv_h = jnp.where(slot[s], dk_h, 0.0), jnp.where(slot[s], dv_h, 0.0)
            if s != j:
                dk_h, dv_h = pltpu.roll(dk_h, 64, 1), pltpu.roll(dv_h, 64, 1)
            dks[h], dvs[h] = dk_h, dv_h

        _interleave(head_program(h) for h in range(8))
        dk2 = sum((dks[h] for h in range(1, 8)), dks[0])
        dv2 = sum((dvs[h] for h in range(1, 8)), dvs[0])
        for pair, cols in enumerate(pair_cols):
            dq_ref[:, cols] = _bf(unrope(dqs[2 * pair] + dqs[2 * pair + 1], cc, sc))
        cur = pl.ds(pl.multiple_of(i * ATTN_BLOCK, ATTN_BLOCK), ATTN_BLOCK)
        dk_ref[cur, :] += unrope(dk2[ATTN_BLOCK:], cc, sc)
        dv_ref[cur, :] += dv2[ATTN_BLOCK:]

        @pl.when(i > 0)
        def _():
            prv = pl.ds(pl.multiple_of((i - 1) * ATTN_BLOCK, ATTN_BLOCK), ATTN_BLOCK)
            dk_ref[prv, :] += unrope(dk2[:ATTN_BLOCK], cp, sp)
            dv_ref[prv, :] += dv2[:ATTN_BLOCK]

    whole = lambda w: pl.BlockSpec((S, w), lambda i: (0, 0))
    return pl.pallas_call(
        body, grid=(nb,), name="attn_bwd", in_specs=in_specs,
        out_specs=[pl.BlockSpec((ATTN_BLOCK, 512), lambda i: (i, BLK_Q)), whole(128), whole(128),
                   pl.BlockSpec((8, 128), lambda i: (0, 0))],
        out_shape=[jax.ShapeDtypeStruct((S, D_IN_PAD), BF16), jax.ShapeDtypeStruct((S, 128), F32),
                   jax.ShapeDtypeStruct((S, 128), F32), jax.ShapeDtypeStruct((8, 128), F32)],
        compiler_params=_params(("arbitrary",)),
    )(sinks, pa, pa, pa, pa, pa, cos, sin, cos, sin, dcat, attn, lse)


CONV_ROWS = 512
CONV_PAD = 8


def _conv_silu(scr, w, r0):
    y = w[3:4, :] * scr[pl.ds(CONV_PAD + r0, CONV_ROWS), :]
    for j in range(DN_CONV - 1):
        y = y + w[j:j + 1, :] * scr[pl.ds(CONV_PAD + r0 - 3 + j, CONV_ROWS), :]
    return y


def _dn_prep_fwd(pd, conv_w):
    S = pd.shape[0]
    assert S % CONV_ROWS == 0

    def body(x_ref, w_ref, o_ref, scr):
        b = pl.program_id(0)
        scr[0:CONV_PAD, :] = jnp.zeros((CONV_PAD, DN_DIM), F32)
        scr[pl.ds(CONV_PAD, S), :] = x_ref[...]
        w = w_ref[...]
        q_scale = jnp.where(b < DN_HEADS, DN_DIM ** -0.5, 1.0)
        for r0 in range(0, S, CONV_ROWS):
            y = _conv_silu(scr, w, r0)
            a = y * _sigmoid(y)
            rs = lax.rsqrt(jnp.sum(a * a, axis=1, keepdims=True) + EPS)
            o_ref[pl.ds(r0, CONV_ROWS), :] = a * jnp.where(b < 2 * DN_HEADS, rs * q_scale, 1.0)

    col = pl.BlockSpec((S, DN_DIM), lambda b: (0, b))
    return pl.pallas_call(
        body, grid=(3 * DN_HEADS,), name="dn_prep_fwd",
        in_specs=[pl.BlockSpec((S, DN_DIM), lambda b: (0, BLK_DN + b)), pl.BlockSpec((DN_CONV, DN_DIM), lambda b: (0, b))],
        out_specs=col,
        out_shape=jax.ShapeDtypeStruct((S, 3 * DN_HEADS * DN_DIM), F32),
        scratch_shapes=[pltpu.VMEM((S + CONV_PAD, DN_DIM), F32)],
        compiler_params=_params(("parallel",)),
    )(pd, conv_w)


def _dn_prep_bwd(pd, conv_w, dqkv, dproj):
    S = pd.shape[0]

    def body(x_ref, w_ref, d_ref, _, dx_ref, dw_ref, scr, dscr):
        b = pl.program_id(0)
        scr[0:CONV_PAD, :] = jnp.zeros((CONV_PAD, DN_DIM), F32)
        scr[pl.ds(CONV_PAD, S), :] = x_ref[...]
        dscr[pl.ds(S, CONV_PAD), :] = jnp.zeros((CONV_PAD, DN_DIM), F32)
        w = w_ref[...]
        q_scale = jnp.where(b < DN_HEADS, DN_DIM ** -0.5, 1.0)
        is_qk = b < 2 * DN_HEADS
        dw = [jnp.zeros((1, DN_DIM), F32) for _ in range(DN_CONV)]
        for r0 in range(0, S, CONV_ROWS):
            y = _conv_silu(scr, w, r0)
            sg = _sigmoid(y)
            a = y * sg
            dout = d_ref[pl.ds(r0, CONV_ROWS), :]
            rs = lax.rsqrt(jnp.sum(a * a, axis=1, keepdims=True) + EPS)
            da_qk = q_scale * rs * (dout - a * (rs * rs) * jnp.sum(dout * a, axis=1, keepdims=True))
            dy = jnp.where(is_qk, da_qk, dout) * (sg * (1.0 + y * (1.0 - sg)))
            dscr[pl.ds(r0, CONV_ROWS), :] = dy
            for j in range(DN_CONV):
                dw[j] = dw[j] + jnp.sum(dy * scr[pl.ds(CONV_PAD + r0 - 3 + j, CONV_ROWS), :], axis=0, keepdims=True)
        for j in range(DN_CONV):
            dw_ref[j:j + 1, :] = dw[j]
        for r0 in range(0, S, CONV_ROWS):
            dx = w[3:4, :] * dscr[pl.ds(r0, CONV_ROWS), :]
            for j in range(DN_CONV - 1):
                dx = dx + w[j:j + 1, :] * dscr[pl.ds(r0 + 3 - j, CONV_ROWS), :]
            dx_ref[pl.ds(r0, CONV_ROWS), :] = _bf(dx)

    col = pl.BlockSpec((S, DN_DIM), lambda b: (0, b))
    proj_col = pl.BlockSpec((S, DN_DIM), lambda b: (0, BLK_DN + b))
    wcol = pl.BlockSpec((DN_CONV, DN_DIM), lambda b: (0, b))
    return pl.pallas_call(
        body, grid=(3 * DN_HEADS,), name="dn_prep_bwd",
        in_specs=[proj_col, wcol, col, pl.BlockSpec(memory_space=pl.ANY)], out_specs=[proj_col, wcol],
        out_shape=[jax.ShapeDtypeStruct(dproj.shape, dproj.dtype), jax.ShapeDtypeStruct((DN_CONV, 3 * DN_HEADS * DN_DIM), F32)],
        scratch_shapes=[pltpu.VMEM((S + CONV_PAD, DN_DIM), F32), pltpu.VMEM((S + CONV_PAD, DN_DIM), F32)],
        input_output_aliases={3: 0},
        compiler_params=_params(("parallel",)),
    )(pd, conv_w, dqkv, dproj)


CPAD = 128
CHUNKS_LOCAL = 4
CHUNKS_SCAN = 4


def _chunk_masks():
    ii = lax.broadcasted_iota(jnp.int32, (DN_CHUNK, CPAD), 0)
    jj = lax.broadcasted_iota(jnp.int32, (DN_CHUNK, CPAD), 1)
    return ii, jj


def _rows_pad(a):
    return jnp.concatenate([a, jnp.zeros_like(a)], axis=0)


def _hi_lo(a):
    hi = _bf(a)
    return hi, _bf(a - hi.astype(F32))


def _double_step(t, p):
    C = DN_CHUNK
    th, tl = _hi_lo(t)
    ph, pl_ = _hi_lo(p)
    r1 = _dot(jnp.concatenate([th, tl, ph, pl_], axis=0), _rows_pad(ph))
    r2 = _dot(jnp.concatenate([th, ph], axis=0), _rows_pad(pl_))
    return t + (r1[:C] + r1[C:2 * C] + r2[:C]), r1[2 * C:3 * C] + r1[3 * C:] + r2[C:]


def _dot3_nt(a, b):
    C = DN_CHUNK
    ah, al = _hi_lo(a)
    bh, bl = _hi_lo(b)
    r1 = _dot_nt(jnp.concatenate([ah, al], axis=0), _rows_pad(bh))
    return r1[:C] + r1[C:] + _dot_nt(ah, _rows_pad(bl))


def _dot3_tn(a, b):
    C = DN_CHUNK
    ah, al = _hi_lo(a)
    bh, bl = _hi_lo(b)
    return _dot_tn(jnp.concatenate([ah, al, ah], axis=0), jnp.concatenate([bh, bh, bl], axis=0))[:C]


def _interleave(programs):
    programs = list(programs)
    while programs:
        alive = []
        for prog in programs:
            try:
                next(prog)
                alive.append(prog)
            except StopIteration:
                pass
        programs = alive


def _col_to_row(col, ii, jj):
    return jnp.sum(jnp.where(ii == jj, col, 0.0), axis=0, keepdims=True)


def _row_to_col(row, ii, jj):
    return jnp.sum(jnp.where(ii == jj, row, 0.0), axis=1, keepdims=True)


def _decay(gc_col, ii, jj):
    diff = gc_col - _col_to_row(gc_col, ii, jj)
    return jnp.where(jj <= ii, jnp.exp(jnp.where(jj <= ii, diff, 0.0)), 0.0)


def _softplus(x):
    return jnp.maximum(x, 0.0) + jnp.log(1.0 + jnp.exp(-jnp.abs(x)))


def _head(h):
    return slice(DN_DIM * h, DN_DIM * (h + 1))


def _dn_chunk_fwd(qkv, pg, a_log, dt_bias):
    S = qkv.shape[0]
    C = DN_CHUNK
    G = CHUNKS_LOCAL
    R = G * C
    steps = S // R

    def body(alog_ref, dtb_ref, qkv_ref, pg_ref, w_ref, u_ref, qg_ref, kd_ref, a_ref, t_ref, gcs_ref):
        ii, jj = _chunk_masks()
        lane = lax.broadcasted_iota(jnp.int32, (1, 128), 1)
        eye = (ii == jj).astype(F32)
        gcs_parts = [[] for _ in range(G)]

        def head_program(chunk, h):
            rows = slice(chunk * C, (chunk + 1) * C)
            q, k, v = qkv_ref[rows, _head(h)], qkv_ref[rows, _head(DN_HEADS + h)], qkv_ref[rows, _head(2 * DN_HEADS + h)]
            beta = _sigmoid(pg_ref[rows, h:h + 1])
            g_col = -jnp.exp(alog_ref[h]) * _softplus(pg_ref[rows, DN_HEADS + h:DN_HEADS + h + 1] + dtb_ref[h])
            g_row = _col_to_row(g_col, ii, jj)
            gc_col = jnp.sum(jnp.where(jj <= ii, g_row, 0.0), axis=1, keepdims=True)
            dec = _decay(gc_col, ii, jj)
            eg = jnp.exp(gc_col)
            kb, vb = k * beta, v * beta
            k_rows = _rows_pad(_bf(k))
            kk = _dot_nt(_bf(kb), k_rows)
            qk = _dot_nt(_bf(q), k_rows)
            yield
            t, pw = eye, -jnp.where(jj < ii, kk * dec, 0.0)
            for _ in range(6):
                t, pw = _double_step(t, pw)
                yield
            tb = _bf(t)
            u_ref[rows, _head(h)] = _dot(tb, _rows_pad(_bf(vb)))
            w_ref[rows, _head(h)] = _bf(_dot(tb, _rows_pad(_bf(kb * eg))))
            a_ref[h, rows] = _bf(qk * dec)
            t_ref[h, rows] = t
            qg_ref[rows, _head(h)] = _bf(q * eg)
            kd_ref[rows, _head(h)] = _bf(k * jnp.exp(gc_col[C - 1:C, :] - gc_col))
            gcs_parts[chunk].append(jnp.where(lane == h, gc_col, 0.0) + jnp.where(lane == DN_HEADS + h, beta, 0.0)
                                    + jnp.where(lane == 2 * DN_HEADS + h, g_col, 0.0))

        _interleave(head_program(chunk, h) for chunk in range(G) for h in range(DN_HEADS))
        for chunk in range(G):
            gcs_ref[chunk * C:(chunk + 1) * C, :] = sum(gcs_parts[chunk][1:], gcs_parts[chunk][0])

    smem = pl.BlockSpec(memory_space=pltpu.SMEM)
    wide = pl.BlockSpec((R, 512), lambda n: (n, 0))
    sq = pl.BlockSpec((DN_HEADS, R, CPAD), lambda n: (0, n, 0))
    narrow = pl.BlockSpec((R, 128), lambda n: (n, 0))
    f = lambda *shp: jax.ShapeDtypeStruct(shp, F32)
    b = lambda *shp: jax.ShapeDtypeStruct(shp, BF16)
    return pl.pallas_call(
        body, grid=(steps,), name="dn_chunk_fwd",
        in_specs=[smem, smem, pl.BlockSpec((R, 1536), lambda n: (n, 0)), pl.BlockSpec((R, 128), lambda n: (n, BLK_G))],
        out_specs=[wide, wide, wide, wide, sq, sq, narrow],
        out_shape=[b(S, 512), f(S, 512), b(S, 512), b(S, 512), b(DN_HEADS, S, CPAD), f(DN_HEADS, S, CPAD), f(S, 128)],
        compiler_params=_params(("parallel",)),
    )(a_log, dt_bias, qkv, pg)


def _gated_norm(o, z, gn):
    r, oh = _rms_stats(o)
    return oh * gn * (z * _sigmoid(z))


def _dn_scan_fwd(w, u, qg, kd, a, gcs, pz, gn):
    S = w.shape[0]
    C = DN_CHUNK
    nc = S // C
    G = CHUNKS_SCAN
    R = G * C

    def body(w_ref, u_ref, qg_ref, kd_ref, a_ref, gcs_ref, z_ref, gn_ref, o_ref, vn_ref, sst_ref, out_ref, state):
        @pl.when(pl.program_id(0) == 0)
        def _():
            state[...] = jnp.zeros_like(state)

        def head_program(chunk, h):
            hs = _head(h)
            rows = slice(chunk * C, (chunk + 1) * C)
            s_in = state[h]
            sb = _bf(s_in)
            sst_ref[chunk, h] = sb
            w_s = _dot(w_ref[rows, hs], sb)
            q_s = _dot(qg_ref[rows, hs], sb)
            yield
            vn = u_ref[rows, hs] - w_s
            vnb = _bf(vn)
            o = q_s + _dot(a_ref[h, rows], _rows_pad(vnb))
            k_v = _dot_tn(kd_ref[rows, hs], vnb)
            yield
            state[h] = s_in * jnp.exp(gcs_ref[(chunk + 1) * C - 1:(chunk + 1) * C, h:h + 1]) + k_v
            o_ref[rows, hs] = o
            vn_ref[rows, hs] = vnb
            out_ref[rows, hs] = _bf(_gated_norm(o, z_ref[rows, hs], gn_ref[...]))

        for chunk in range(G):
            _interleave(head_program(chunk, h) for h in range(DN_HEADS))

    wide = pl.BlockSpec((R, 512), lambda n: (n, 0))
    f = lambda *shp: jax.ShapeDtypeStruct(shp, F32)
    b = lambda *shp: jax.ShapeDtypeStruct(shp, BF16)
    return pl.pallas_call(
        body, grid=(nc // G,), name="dn_scan_fwd",
        in_specs=[wide, wide, wide, wide, pl.BlockSpec((DN_HEADS, R, CPAD), lambda n: (0, n, 0)),
                  pl.BlockSpec((R, 128), lambda n: (n, 0)), pl.BlockSpec((R, 512), lambda n: (n, BLK_Z)),
                  pl.BlockSpec((1, DN_DIM), lambda n: (0, 0))],
        out_specs=[wide, wide, pl.BlockSpec((G, DN_HEADS, DN_DIM, DN_DIM), lambda n: (n, 0, 0, 0)), wide],
        out_shape=[f(S, 512), b(S, 512), b(nc, DN_HEADS, DN_DIM, DN_DIM), b(S, 512)],
        scratch_shapes=[pltpu.VMEM((DN_HEADS, DN_DIM, DN_DIM), F32)],
        compiler_params=_params(("arbitrary",)),
    )(w, u, qg, kd, a, gcs, pz, gn)


def _dn_scan_bwd(dcat, o, pz, gn, sst, vnew, w, qg, kd, a, gcs, dproj):
    S = o.shape[0]
    C = DN_CHUNK
    G = CHUNKS_SCAN
    R = G * C
    steps = S // R

    def body(dy_ref, o_ref, z_ref, gn_ref, sst_ref, vn_ref, w_ref, qg_ref, kd_ref, a_ref, gcs_ref, _,
             du_ref, dw_ref, dqg_ref, dkd_ref, da_ref, dz_ref, dsc_ref, dgn_ref, dstate):
        @pl.when(pl.program_id(0) == 0)
        def _():
            dstate[...] = jnp.zeros_like(dstate)
            dgn_ref[...] = jnp.zeros_like(dgn_ref)

        gn_ = gn_ref[...]
        lane = lax.broadcasted_iota(jnp.int32, (C, 128), 1)
        row = lax.broadcasted_iota(jnp.int32, (C, 128), 0)
        dgn_parts = []

        def head_program(chunk, h, dsc_parts):
            hs = _head(h)
            rows = slice(chunk * C, (chunk + 1) * C)
            ov, z, dout = o_ref[rows, hs], z_ref[rows, hs], dy_ref[rows, hs]
            r, oh = _rms_stats(ov)
            sg = _sigmoid(z)
            don = dout * (z * sg)
            dz_ref[rows, hs] = _bf(dout * (oh * gn_) * (sg * (1.0 + z * (1.0 - sg))))
            dgn_parts.append(jnp.sum(don * oh, axis=0, keepdims=True))
            dn = don * gn_
            do = _bf(r * (dn - oh * jnp.mean(dn * oh, axis=-1, keepdims=True)))
            sb = sst_ref[chunk, h]
            s_in = sb.astype(F32)
            ds_out = dstate[h]
            dsb = _bf(ds_out)
            vnb = vn_ref[rows, hs]
            wb, qgb, kdb, ab = w_ref[rows, hs], qg_ref[rows, hs], kd_ref[rows, hs], a_ref[h, rows]
            dvn = _dot_tn(ab, do)[:C] + _dot(kdb, dsb)
            yield
            da_ref[h, rows] = _dot_nt(do, _rows_pad(vnb))
            dqg_ref[rows, hs] = _dot_nt(do, sb)
            dkd_ref[rows, hs] = _dot_nt(vnb, dsb)
            q_do = _dot_tn(qgb, do)
            yield
            dvnb = _bf(dvn)
            dw_ref[rows, hs] = _bf(-_dot_nt(dvnb, sb))
            w_dvn = _dot_tn(wb, dvnb)
            du_ref[rows, hs] = dvnb
            yield
            d_last = jnp.exp(gcs_ref[(chunk + 1) * C - 1:(chunk + 1) * C, h:h + 1])
            dd = jnp.sum(jnp.sum(ds_out * s_in, axis=1, keepdims=True), axis=0, keepdims=True)
            dsc_parts.append(jnp.where((lane == h) & (row == C - 1), dd * d_last, 0.0))
            dstate[h] = ds_out * d_last + q_do - w_dvn

        for chunk in reversed(range(G)):
            dsc_parts = []
            _interleave(head_program(chunk, h, dsc_parts) for h in range(DN_HEADS))
            dsc_ref[chunk * C:(chunk + 1) * C, :] = sum(dsc_parts[1:], dsc_parts[0])
        dgn_ref[...] += sum(dgn_parts[1:], dgn_parts[0])

    rev = lambda n: steps - 1 - n
    wide = pl.BlockSpec((R, 512), lambda n: (rev(n), 0))
    z_spec = pl.BlockSpec((R, 512), lambda n: (rev(n), BLK_Z))
    sq = pl.BlockSpec((DN_HEADS, R, CPAD), lambda n: (0, rev(n), 0))
    narrow = pl.BlockSpec((R, 128), lambda n: (rev(n), 0))
    gn_spec = pl.BlockSpec((1, DN_DIM), lambda n: (0, 0))
    f = lambda *shp: jax.ShapeDtypeStruct(shp, F32)
    b = lambda *shp: jax.ShapeDtypeStruct(shp, BF16)
    return pl.pallas_call(
        body, grid=(steps,), name="dn_scan_bwd",
        in_specs=[pl.BlockSpec((R, 512), lambda n: (rev(n), 1)), wide, z_spec, gn_spec,
                  pl.BlockSpec((G, DN_HEADS, DN_DIM, DN_DIM), lambda n: (rev(n), 0, 0, 0)),
                  wide, wide, wide, wide, sq, narrow, pl.BlockSpec(memory_space=pl.ANY)],
        out_specs=[wide, wide, wide, wide, sq, z_spec, narrow, gn_spec],
        out_shape=[b(S, 512), b(S, 512), f(S, 512), f(S, 512), f(DN_HEADS, S, CPAD),
                   jax.ShapeDtypeStruct(dproj.shape, dproj.dtype), f(S, 128), f(1, DN_DIM)],
        scratch_shapes=[pltpu.VMEM((DN_HEADS, DN_DIM, DN_DIM), F32)],
        input_output_aliases={11: 5},
        compiler_params=_params(("arbitrary",)),
    )(dcat, o, pz, gn, sst, vnew, w, qg, kd, a, gcs, dproj)


def _dn_chunk_bwd(qkv, pg, t_inv, gcs, du, dw, dqg, dkd, da, dsc, a_log, dt_bias, dproj):
    S = qkv.shape[0]
    C = DN_CHUNK
    G = CHUNKS_LOCAL
    R = G * C

    def body(alog_ref, dtb_ref, qkv_ref, pg_ref, t_ref, gcs_ref, du_ref, dw_ref, dqg_ref, dkd_ref, da_ref, dsc_ref, _,
             dqkv_ref, dpg_ref, acc_ref):
        @pl.when(pl.program_id(0) == 0)
        def _():
            acc_ref[...] = jnp.zeros_like(acc_ref)

        ii, jj = _chunk_masks()
        lane = lax.broadcasted_iota(jnp.int32, (1, 128), 1)
        row8 = lax.broadcasted_iota(jnp.int32, (8, 128), 0)
        lane8 = lax.broadcasted_iota(jnp.int32, (8, 128), 1)
        rowc = lax.broadcasted_iota(jnp.int32, (C, 1), 0)
        tril, strict = jj <= ii, jj < ii
        dpg_parts, acc_parts = [[] for _ in range(G)], []

        def head_program(chunk, h):
            rows = slice(chunk * C, (chunk + 1) * C)
            q, k, v = qkv_ref[rows, _head(h)], qkv_ref[rows, _head(DN_HEADS + h)], qkv_ref[rows, _head(2 * DN_HEADS + h)]
            gc_col, beta, g_col = gcs_ref[rows, h:h + 1], gcs_ref[rows, DN_HEADS + h:DN_HEADS + h + 1], \
                gcs_ref[rows, 2 * DN_HEADS + h:2 * DN_HEADS + h + 1]
            dec = _decay(gc_col, ii, jj)
            eg = jnp.exp(gc_col)
            g_last = gc_col[C - 1:C, :]
            ek = jnp.exp(g_last - gc_col)
            kb, vb = k * beta, v * beta
            kbg = kb * eg
            qb, kbb = _bf(q), _bf(kb)
            k_rows = _rows_pad(_bf(k))
            t = t_ref[h, rows]
            tb = _bf(t)
            dub, dwb = du_ref[rows, _head(h)], dw_ref[rows, _head(h)]
            dqg_, dkd_ = dqg_ref[rows, _head(h)], dkd_ref[rows, _head(h)]
            dt = _dot_nt(dub, _rows_pad(_bf(vb))) + _dot_nt(dwb, _rows_pad(_bf(kbg)))
            t_du_dw = _dot_tn(tb, jnp.concatenate([dub, dwb], axis=1))
            dvb, dkbg = t_du_dw[:C, :DN_DIM], t_du_dw[:C, DN_DIM:]
            kk = _dot_nt(kbb, k_rows)
            qk = _dot_nt(qb, k_rows)
            yield
            dt_t = _dot3_nt(dt, t)
            yield
            dl = -_dot3_tn(t, dt_t)
            yield
            dm = jnp.where(strict, dl * dec, 0.0)
            dqk = jnp.where(tril, da_ref[h, rows] * dec, 0.0)
            gmat = dm * kk + dqk * qk
            dgc = jnp.sum(gmat, axis=1, keepdims=True) - _row_to_col(jnp.sum(gmat, axis=0, keepdims=True), ii, jj)
            dmb, dqkb = _bf(dm), _bf(dqk)
            yield
            dkb = _dot(dmb, k_rows) + dkbg * eg
            dk = _dot_tn(jnp.concatenate([dmb, dqkb], axis=0), jnp.concatenate([kbb, qb], axis=0))[:C] + dkd_ * ek
            dq = _dot(dqkb, k_rows) + dqg_ * eg
            yield
            tk = jnp.sum(dkd_ * k * ek, axis=1, keepdims=True)
            dgc = dgc + jnp.sum(dqg_ * q * eg, axis=1, keepdims=True) - tk + jnp.sum(dkbg * kbg, axis=1, keepdims=True)
            dgl = jnp.sum(tk, axis=0, keepdims=True) + dsc_ref[(chunk + 1) * C - 1:(chunk + 1) * C, h:h + 1]
            dgc = dgc + jnp.where(rowc == C - 1, dgl, 0.0)
            yield
            dk = dk + dkb * beta
            dbeta = jnp.sum(dkb * k, axis=1, keepdims=True) + jnp.sum(dvb * v, axis=1, keepdims=True)
            dqkv_ref[rows, _head(h)] = dq
            dqkv_ref[rows, _head(DN_HEADS + h)] = dk
            dqkv_ref[rows, _head(2 * DN_HEADS + h)] = dvb * beta
            dg_col = jnp.sum(jnp.where(jj >= ii, _col_to_row(dgc, ii, jj), 0.0), axis=1, keepdims=True)
            yield
            db = dbeta * beta * (1.0 - beta)
            da_in = dg_col * (-jnp.exp(alog_ref[h])) * _sigmoid(pg_ref[rows, DN_HEADS + h:DN_HEADS + h + 1] + dtb_ref[h])
            dpg_parts[chunk].append(jnp.where(lane == h, db, 0.0) + jnp.where(lane == DN_HEADS + h, da_in, 0.0))
            acc_parts.append(jnp.where((row8 == 0) & (lane8 == h), jnp.sum(dg_col * g_col, axis=0, keepdims=True), 0.0)
                             + jnp.where((row8 == 1) & (lane8 == h), jnp.sum(da_in, axis=0, keepdims=True), 0.0))

        _interleave(head_program(chunk, h) for chunk in range(G) for h in range(DN_HEADS))
        for chunk in range(G):
            dpg = sum(dpg_parts[chunk][1:], dpg_parts[chunk][0])
            dpg_ref[chunk * C:(chunk + 1) * C, :] = _bf(jnp.concatenate([dpg, jnp.zeros_like(dpg)], axis=1))
        acc_ref[...] += sum(acc_parts[1:], acc_parts[0])

    smem = pl.BlockSpec(memory_space=pltpu.SMEM)
    wide = pl.BlockSpec((R, 512), lambda n: (n, 0))
    sq = pl.BlockSpec((DN_HEADS, R, CPAD), lambda n: (0, n, 0))
    narrow = pl.BlockSpec((R, 128), lambda n: (n, 0))
    qkv_spec = pl.BlockSpec((R, 1536), lambda n: (n, 0))
    f = lambda *shp: jax.ShapeDtypeStruct(shp, F32)
    return pl.pallas_call(
        body, grid=(S // R,), name="dn_chunk_bwd",
        in_specs=[smem, smem, qkv_spec, pl.BlockSpec((R, 128), lambda n: (n, BLK_G)), sq, narrow, wide, wide, wide, wide, sq,
                  narrow, pl.BlockSpec(memory_space=pl.ANY)],
        out_specs=[qkv_spec, pl.BlockSpec((R, 256), lambda n: (n, BLK_G_PAD)), pl.BlockSpec((8, 128), lambda n: (0, 0))],
        out_shape=[f(S, 1536), jax.ShapeDtypeStruct(dproj.shape, dproj.dtype), f(8, 128)],
        input_output_aliases={12: 1},
        compiler_params=_params(("arbitrary",)),
    )(a_log, dt_bias, qkv, pg, t_inv, gcs, du, dw, dqg, dkd, da, dsc, dproj)


def _fill_kv(dk, dv, dproj):
    S = dk.shape[0]
    tm = min(512, S)

    def body(dk_ref, dv_ref, _, o_ref):
        o_ref[...] = _bf(jnp.concatenate([dk_ref[...], dv_ref[...]], axis=1))

    tile = pl.BlockSpec((tm, 128), lambda i: (i, 0))
    return pl.pallas_call(
        body, grid=(S // tm,), name="fill_kv",
        in_specs=[tile, tile, pl.BlockSpec(memory_space=pl.ANY)],
        out_specs=pl.BlockSpec((tm, 256), lambda i: (i, BLK_KV)),
        out_shape=jax.ShapeDtypeStruct(dproj.shape, dproj.dtype),
        input_output_aliases={2: 0},
        compiler_params=_params(("parallel",)),
    )(dk, dv, dproj)


def _w_in_to_internal(wt):
    return jnp.concatenate([wt[0:512], wt[2304:2816], wt[768:2304], wt[512:768], wt[2816:2824],
                            jnp.zeros((D_IN_PAD - D_IN, wt.shape[1]), wt.dtype)], axis=0)


def _w_in_from_internal(gt):
    return jnp.concatenate([gt[0:512], gt[2560:2816], gt[1024:2560], gt[512:1024], gt[2816:2824]], axis=0)


def _local_step(x, p, target, wts, first_weights, other_weights, ship_early):
    S = x.shape[0]
    cos, sin = _rope_tables(S)
    sinks, a_log, dt_bias = wts["sinks"].reshape(8), wts["a_log"].reshape(4), wts["dt_bias"].reshape(4)
    gn = wts["dn_norm"].reshape(1, DN_DIM)
    add = lambda acc, res: (acc + res,)

    u = _rmsnorm_fwd(x, wts["norm_mix"], "norm_mix_fwd")
    w_in_t, conv_w = first_weights(u)
    proj, = _mm(u, w_in_t, form="nt", name="in_proj", out_dtypes=[F32], tn=512)
    attn, lse = _attn_fwd(proj, cos, sin, sinks)
    qkv = _dn_prep_fwd(proj, conv_w)
    cw, cu, cqg, ckd, ca, ct, gcs = _dn_chunk_fwd(qkv, proj, a_log, dt_bias)
    o, vnew, sst, dn_out = _dn_scan_fwd(cw, cu, cqg, ckd, ca, gcs, proj, gn)
    w_o, = other_weights(("w_o",), dn_out)
    h1, = _mm([attn, dn_out], w_o, form="nn", name="out_proj", out_dtypes=[F32], tn=512, epi=add, extra=[x])

    def relu2(acc):
        r = jnp.maximum(acc, 0.0)
        return r * r, r

    w_up, = other_weights(("w_up",), h1)
    hid, relu, m = _mm(h1, w_up, form="nn", name="mlp_up", out_dtypes=[BF16, BF16], tn=512, epi=relu2, norm=wts["norm_mlp"])
    w_down, = other_weights(("w_down",), hid)
    h2, = _mm(hid, w_down, form="nn", name="mlp_down", out_dtypes=[F32], tn=512, epi=add, extra=[h1])
    w_pg, w_pp = other_weights(("w_ple_gate", "w_ple_proj"), h2)
    n3, dh2, dgl, dpp, loss, d_norm_final, d_norm_ple = _ple_and_loss(h2, p, target, w_pg, w_pp, wts["norm_ple"],
                                                                     wts["norm_final"].reshape(1, D_MODEL))
    g = {"norm_final": d_norm_final, "norm_ple": d_norm_ple}
    early = {"w_ple_gate": _mm_tn(n3, dgl, name="d_w_ple_gate", tm=512, tn=1024, out_dtype=BF16).reshape(N_DEV, 128, 1024),
             "w_ple_proj": _mm_tn(p, dpp, name="d_w_ple_proj", tm=256, tn=128, out_dtype=BF16, column_shards=True)}
    d_act, = _mm(dh2, w_down, form="nt", name="d_hidden", out_dtypes=[BF16], tn=512,
                 epi=lambda acc, r: (acc * (2.0 * r.astype(F32)),), extra=[relu])
    early["w_down"] = _mm_tn(hid, dh2, name="d_w_down", tm=512, tn=1024, out_dtype=BF16).reshape(N_DEV, 512, 1024)
    early["w_up"] = _mm_tn(m, d_act, name="d_w_up", tm=1024, tn=512, out_dtype=BF16, column_shards=True)
    token = ship_early(early)
    dh1, g["norm_mlp"], dcat = _mm(d_act, w_up, form="nt", name="d_m", out_dtypes=[F32], tn=512, after=token,
                                   norm_bwd=(h1, wts["norm_mlp"], dh2), then_nt=w_o)
    d_w_o = jnp.concatenate([_mm_tn(attn, dh1, name="d_w_o_attn", tm=512, tn=512, out_dtype=BF16),
                             _mm_tn(dn_out, dh1, name="d_w_o_dn", tm=512, tn=512, out_dtype=BF16)], axis=0)
    token = ship_early({"w_o": d_w_o.reshape(N_DEV, 128, 1024)})
    dproj, dk, dv, dsinks = _attn_bwd(proj, cos, sin, sinks + token[0, 0], dcat, attn, lse)
    g["sinks"] = dsinks[:, 0].reshape(1, 8)
    du_, dw_, dqg, dkd, da, dproj, dsc, g["dn_norm"] = _dn_scan_bwd(dcat, o, proj, gn, sst, vnew, cw, cqg, ckd, ca, gcs, dproj)
    dqkv, dproj, gate_acc = _dn_chunk_bwd(qkv, proj, ct, gcs, du_, dw_, dqg, dkd, da, dsc, a_log, dt_bias, dproj)
    g["a_log"], g["dt_bias"] = gate_acc[0:1, 0:4], gate_acc[1:2, 0:4]
    dproj, g["conv_w"] = _dn_prep_bwd(proj, conv_w, dqkv, dproj)
    dproj = _fill_kv(dk, dv, dproj)
    token = ship_early({"w_in": _mm_tn(dproj, u, name="d_w_in", tm=512, tn=1024, out_dtype=BF16)})
    grad_x, g["norm_mix"] = _mm(dproj, w_in_t, form="nn", name="d_u", out_dtypes=[F32], tn=512, after=token,
                                norm_bwd=(x, wts["norm_mix"], dh1))
    return loss, grad_x, g


def _peer(k):
    x, y, c = lax.axis_index("x"), lax.axis_index("y"), lax.axis_index("c")
    px = 1 - x if k & 4 else x
    py = 1 - y if k & 2 else y
    pc = 1 - c if k & 1 else c
    return (px, py, pc), 4 * px + 2 * py + pc


def _exchange(srcs, name, gather):
    n = len(srcs)
    gathers = list(gather) if isinstance(gather, (list, tuple)) else [gather] * n
    shapes = [(N_DEV,) + s.shape if gt else s.shape for s, gt in zip(srcs, gathers)]

    def body(*refs):
        src_refs, out_refs = refs[:n], refs[n:2 * n]
        send_sems, recv_sems, local_sems = refs[2 * n:]
        _, me = _peer(0)
        piece = lambda a, d: src_refs[a] if gathers[a] else src_refs[a].at[d]
        local = [pltpu.make_async_copy(piece(a, me), out_refs[a].at[me], local_sems.at[a]) for a in range(n)]
        for cp in local:
            cp.start()
        copies = []
        for a in range(n):
            for k in range(1, N_DEV):
                dev, idx = _peer(k)
                cp = pltpu.make_async_remote_copy(src_ref=piece(a, idx), dst_ref=out_refs[a].at[me],
                                                  send_sem=send_sems.at[a, k - 1], recv_sem=recv_sems.at[a, k - 1],
                                                  device_id=dev, device_id_type=MESH)
                cp.start()
                copies.append(cp)
        for cp in copies:
            cp.wait_recv()
        for cp in copies:
            cp.wait_send()
        for cp in local:
            cp.wait()

    anywhere = pl.BlockSpec(memory_space=pl.ANY)
    return pl.pallas_call(
        body, name=name, in_specs=[anywhere] * n, out_specs=[anywhere] * n,
        out_shape=[jax.ShapeDtypeStruct(shp, s.dtype) for shp, s in zip(shapes, srcs)],
        scratch_shapes=[pltpu.SemaphoreType.DMA((n, N_DEV - 1)), pltpu.SemaphoreType.DMA((n, N_DEV - 1)),
                        pltpu.SemaphoreType.DMA((n,))],
    )(*srcs)


_HBM = pl.BlockSpec(memory_space=pltpu.HBM)
_SEM = pl.BlockSpec(memory_space=pltpu.SEMAPHORE)
_EFFECT = pltpu.SideEffectType.DATAFLOW_SIDE_EFFECTING


def _split_copies(src_refs, land_refs, send_sems, recv_sems, modes, which=None):
    _, me = _peer(0)
    copies = []
    which = range(len(src_refs)) if which is None else which
    for a, src, land in zip(which, src_refs, land_refs):
        if modes[a] == "columns":
            n_cols = src.shape[1]
            dst = land.at[:, pl.ds(pl.multiple_of(me * n_cols, n_cols), n_cols)]
        else:
            dst = land.at[me]
        for k in range(1, N_DEV):
            dev, idx = _peer(k)
            sem = a * (N_DEV - 1) + k - 1
            copies.append(pltpu.make_async_remote_copy(
                src_ref=src.at[idx] if modes[a] == "pieces" else src, dst_ref=dst, send_sem=send_sems.at[sem],
                recv_sem=recv_sems.at[sem], device_id=dev, device_id_type=MESH))
    return copies


def _exchange_start(srcs, name, modes):
    n = len(srcs)
    modes = [modes] * n if isinstance(modes, str) else list(modes)
    me = 4 * lax.axis_index("x") + 2 * lax.axis_index("y") + lax.axis_index("c")
    lands = []
    for s, mode in zip(srcs, modes):
        if mode == "columns":
            empty = lax.empty((s.shape[0], N_DEV * s.shape[1]), s.dtype)
            lands.append(lax.dynamic_update_slice(empty, s, (0, me * s.shape[1])))
        else:
            own = s if mode == "slots" else lax.dynamic_index_in_dim(s, me, 0, keepdims=False)
            shape = (N_DEV,) + s.shape if mode == "slots" else s.shape
            lands.append(lax.dynamic_update_index_in_dim(lax.empty(shape, s.dtype), own, me, 0))

    def body(*refs):
        src_refs, land_refs = refs[:n], refs[n:2 * n]
        send_sems, recv_sems = refs[2 * n], refs[2 * n + 1]
        for cp in _split_copies(src_refs, land_refs, send_sems, recv_sems, modes):
            cp.start()
        refs[-1][...] = jnp.zeros_like(refs[-1])

    both = list(srcs) + lands
    sems = pltpu.SemaphoreType.DMA((n * (N_DEV - 1),))
    out = pl.pallas_call(
        body, name=name,
        out_shape=(sems, sems, *[pltpu.HBM(t.shape, t.dtype) for t in both], jax.ShapeDtypeStruct((8, 128), F32)),
        in_specs=[_HBM] * (2 * n), out_specs=(_SEM, _SEM, *[_HBM] * (2 * n), pl.BlockSpec(memory_space=pltpu.VMEM)),
        input_output_aliases={i: 2 + i for i in range(2 * n)},
        compiler_params=pltpu.CompilerParams(has_side_effects=_EFFECT),
    )(*[pltpu.with_memory_space_constraint(t, pltpu.HBM) for t in both])
    return (n, modes, out[:-1]), out[-1]


def _exchange_wait(handle, after, name, which=None):
    n_all, modes, (send_sems, recv_sems, *both_all) = handle
    which = list(range(n_all)) if which is None else list(which)
    n = len(which)
    both = [both_all[a] for a in which] + [both_all[n_all + a] for a in which]

    def body(*refs):
        src_refs, land_refs = refs[:n], refs[n:2 * n]
        for cp in _split_copies(src_refs, land_refs, refs[2 * n], refs[2 * n + 1], modes, which):
            cp.wait_send()
            cp.wait_recv()

    out = pl.pallas_call(
        body, name=name, out_shape=tuple(pltpu.HBM(t.shape, t.dtype) for t in both),
        in_specs=[_HBM] * (2 * n) + [_SEM, _SEM, pl.BlockSpec(memory_space=pl.ANY)], out_specs=tuple([_HBM] * (2 * n)),
        input_output_aliases={i: i for i in range(2 * n)},
        compiler_params=pltpu.CompilerParams(has_side_effects=_EFFECT),
    )(*both, send_sems, recv_sems, after)
    return list(out[n:])


def _adam_update(g, w, m, v):
    nm = ADAM_B1 * m + (1.0 - ADAM_B1) * g
    nv = ADAM_B2 * v + (1.0 - ADAM_B2) * (g * g)
    m_hat = nm / (1.0 - ADAM_B1 ** ADAM_STEP)
    v_hat = nv / (1.0 - ADAM_B2 ** ADAM_STEP)
    return -ADAM_LR * (m_hat / (jnp.sqrt(v_hat) + ADAM_EPS) + ADAM_WD * w), nm, nv


def _adamw(parts, w, m, v, name):
    n, R, W = parts.shape
    tm = 128 if R % 128 == 0 else R

    def body(p_ref, w_ref, m_ref, v_ref, g_ref, d_ref, nm_ref, nv_ref):
        g = p_ref[0].astype(F32)
        for s in range(1, n):
            g = g + p_ref[s].astype(F32)
        g_ref[...] = g
        d_ref[...], nm_ref[...], nv_ref[...] = _adam_update(g, w_ref[...], m_ref[...], v_ref[...])

    tile = pl.BlockSpec((tm, W), lambda i: (i, 0))
    return pl.pallas_call(
        body, grid=(R // tm,), name=name,
        in_specs=[pl.BlockSpec((n, tm, W), lambda i: (0, i, 0)), tile, tile, tile],
        out_specs=[tile] * 4, out_shape=[jax.ShapeDtypeStruct((R, W), F32)] * 4,
        compiler_params=_params(("parallel",)),
    )(parts, w, m, v)


_MATRICES = ("w_in", "w_o", "w_up", "w_down", "w_ple_gate", "w_ple_proj")


_OTHERS = ("w_o", "w_up", "w_down", "w_ple_gate", "w_ple_proj")
_OTHER_MODES = {"w_o": "slots", "w_up": "slots", "w_down": "slots", "w_ple_gate": "slots", "w_ple_proj": "columns"}


_VECTORS = ("norm_mix", "norm_mlp", "norm_ple", "norm_final", "a_log", "dt_bias", "sinks", "dn_norm")
_SMALL_ROWS, _LOSS_ROW, _CONV_ROW = 16, 8, 9


def _pack_small(vectors, loss, conv):
    def body(*refs):
        out = refs[-1]
        out[...] = jnp.zeros_like(out)
        for r, ref in enumerate(refs[:len(_VECTORS)]):
            out[r:r + 1, 0:ref.shape[1]] = ref[...]
        out[_LOSS_ROW:_LOSS_ROW + 1, 0:128] = refs[len(_VECTORS)][...]
        out[_CONV_ROW:_CONV_ROW + 6, :] = refs[len(_VECTORS) + 1][...]

    return pl.pallas_call(body, name="pack_small", out_shape=jax.ShapeDtypeStruct((_SMALL_ROWS, 1024), F32))(*vectors, loss, conv)


def _sum_slots(parts):
    def body(p_ref, o_ref):
        acc = p_ref[0]
        for s in range(1, parts.shape[0]):
            acc = acc + p_ref[s]
        o_ref[...] = acc

    return pl.pallas_call(body, name="sum_small", out_shape=jax.ShapeDtypeStruct(parts.shape[1:], parts.dtype))(parts)


def _adamw_vectors(summed, conv_g, wmv):
    names = _VECTORS + ("conv_w",)
    flat = [a for triple in wmv for a in triple]

    def body(*refs):
        sum_ref, conv_ref = refs[0], refs[1]
        ins, outs = refs[2:2 + len(flat)], refs[2 + len(flat):]
        for i in range(len(names)):
            w_ref, m_ref, v_ref = ins[3 * i:3 * i + 3]
            g = conv_ref[...] if i == len(_VECTORS) else sum_ref[i:i + 1, 0:w_ref.shape[1]]
            outs[4 * i][...] = g
            outs[4 * i + 1][...], outs[4 * i + 2][...], outs[4 * i + 3][...] = _adam_update(g, w_ref[...], m_ref[...], v_ref[...])

    out_shape = [jax.ShapeDtypeStruct(t[0].shape, F32) for t in wmv for _ in range(4)]
    res = pl.pallas_call(body, name="adamw_vectors", out_shape=out_shape)(summed, conv_g, *flat)
    return {n: res[4 * i:4 * i + 4] for i, n in enumerate(names)}


_ORDER = ("norm_mix", "w_in", "conv_w", "a_log", "dt_bias", "dn_norm", "sinks", "w_o", "norm_mlp", "w_up", "w_down",
          "norm_ple", "w_ple_gate", "w_ple_proj", "norm_final")


def kernel(x, p, norm_mix, w_in, conv_w, a_log, dt_bias, dn_norm, sinks, w_o, norm_mlp, w_up, w_down, norm_ple, w_ple_gate, w_ple_proj, norm_final, loss_target, m_norm_mix, m_w_in, m_conv_w, m_a_log, m_dt_bias, m_dn_norm, m_sinks, m_w_o, m_norm_mlp, m_w_up, m_w_down, m_norm_ple, m_w_ple_gate, m_w_ple_proj, m_norm_final, v_norm_mix, v_w_in, v_conv_w, v_a_log, v_dt_bias, v_dn_norm, v_sinks, v_w_o, v_norm_mlp, v_w_up, v_w_down, v_norm_ple, v_w_ple_gate, v_w_ple_proj, v_norm_final):
    w = dict(norm_mix=norm_mix, w_in=w_in[0], conv_w=conv_w[0], a_log=a_log, dt_bias=dt_bias, dn_norm=dn_norm, sinks=sinks,
             w_o=w_o[0], norm_mlp=norm_mlp, w_up=w_up[0], w_down=w_down[0], norm_ple=norm_ple, w_ple_gate=w_ple_gate[0],
             w_ple_proj=w_ple_proj[0], norm_final=norm_final)
    m = dict(norm_mix=m_norm_mix, w_in=m_w_in[0], conv_w=m_conv_w[0], a_log=m_a_log, dt_bias=m_dt_bias, dn_norm=m_dn_norm,
             sinks=m_sinks, w_o=m_w_o[0], norm_mlp=m_norm_mlp, w_up=m_w_up[0], w_down=m_w_down[0], norm_ple=m_norm_ple,
             w_ple_gate=m_w_ple_gate[0], w_ple_proj=m_w_ple_proj[0], norm_final=m_norm_final)
    v = dict(norm_mix=v_norm_mix, w_in=v_w_in[0], conv_w=v_conv_w[0], a_log=v_a_log, dt_bias=v_dt_bias, dn_norm=v_dn_norm,
             sinks=v_sinks, w_o=v_w_o[0], norm_mlp=v_norm_mlp, w_up=v_w_up[0], w_down=v_w_down[0], norm_ple=v_norm_ple,
             w_ple_gate=v_w_ple_gate[0], w_ple_proj=v_w_ple_proj[0], norm_final=v_norm_final)
    me = 4 * lax.axis_index("x") + 2 * lax.axis_index("y") + lax.axis_index("c")
    conv_shard = conv_w.shape[2]

    for d in (w, m, v):
        d["w_in"] = d["w_in"].T
    conv_pad = jnp.pad(w["conv_w"], ((0, 8 - DN_CONV), (0, 256 - conv_shard)))
    first, token_first = _exchange_start([_bf(w["w_in"]), conv_pad], "gather_first_start", "slots")
    later = [_bf(w[n]) for n in _OTHERS]
    later[-1] = _bf(w["w_ple_proj"] + token_first[0:1, 0:1])
    others, token_others = _exchange_start(later, "gather_others_start", [_OTHER_MODES[n] for n in _OTHERS])
    vectors = dict(w)
    vectors["norm_mix"] = w["norm_mix"] + token_others[0:1, 0:1]

    def first_weights(after):
        w_in_all, conv_all = _exchange_wait(first, after, "gather_first_wait")
        conv_all = jnp.transpose(conv_all[:, :DN_CONV, :conv_shard], (1, 0, 2)).reshape(DN_CONV, N_DEV * conv_shard)
        return _w_in_to_internal(w_in_all.reshape(D_IN, D_MODEL)), conv_all

    as_taken = {"w_o": lambda t: t.reshape(1024, 1024), "w_up": lambda t: t, "w_down": lambda t: t.reshape(4096, 1024),
                "w_ple_gate": lambda t: t.reshape(1024, 1024), "w_ple_proj": lambda t: t}

    def other_weights(names, after):
        which = [_OTHERS.index(n) for n in names]
        got = _exchange_wait(others, after, "gather_wait_" + names[0], which)
        return [as_taken[n](t) for n, t in zip(names, got)]

    shipped = []

    def ship_early(pieces):
        names = tuple(pieces)
        if names == ("w_in",):
            pieces = {"w_in": _w_in_from_internal(pieces["w_in"]).reshape(N_DEV, D_IN // N_DEV, D_MODEL)}
        handle, token = _exchange_start([pieces[n] for n in names], "scatter_start_" + names[0], "pieces")
        shipped.append((names, handle))
        return token

    loss, grad_x, g = _local_step(x[0], p[0, 0], loss_target[0], vectors, first_weights, other_weights, ship_early)

    row = lambda t: t.reshape(1, t.size)
    small = _pack_small([row(g[n]) for n in _VECTORS], loss, g["conv_w"].reshape(6, 1024))
    small_all, = _exchange([small], "gather_small", gather=True)
    summed = _sum_slots(small_all)
    conv_g = lax.dynamic_slice(summed[_CONV_ROW:_CONV_ROW + 6].reshape(DN_CONV, N_DEV * conv_shard), (0, me * conv_shard),
                               (DN_CONV, conv_shard))
    small_out = _adamw_vectors(summed, conv_g, [(row(w[n]), row(m[n]), row(v[n])) for n in _VECTORS]
                               + [(w["conv_w"], m["conv_w"], v["conv_w"])])
    big, after = {}, small_out["conv_w"][0]
    for names, handle in shipped:
        for n, r in zip(names, _exchange_wait(handle, after, "scatter_wait_" + names[0])):
            big[n] = _adamw(r, w[n], m[n], v[n], "adamw_" + n)
            after = big[n][1]

    result = [summed[_LOSS_ROW, 0], grad_x[None]]
    for i in range(4):
        for n in _ORDER:
            if n == "w_in":
                result.append(big[n][i].T[None])
            elif n in _MATRICES:
                result.append(big[n][i][None])
            elif n == "conv_w":
                result.append(small_out[n][i][None])
            else:
                result.append(small_out[n][i].reshape(w[n].shape))
    return tuple(result)
```

```python
import jax
import jax.numpy as jnp
import numpy as np
from jax import lax
from jax.experimental import pallas as pl
from jax.experimental.pallas import tpu as pltpu

F32, BF16 = jnp.float32, jnp.bfloat16
EPS = 1e-6
D_MODEL = 1024
N_DEV = 8
ATTN_BLOCK = 128
HEAD_PAIR = 128
DN_HEADS = 4
DN_DIM = 128
DN_CHUNK = 64
DN_CONV = 4
ROPE_THETA = 10000.0
D_IN = 2824
D_IN_PAD = 3072
BLK_Q, BLK_Z = 0, 1
BLK_DN, BLK_K, BLK_V, BLK_G = 8, 20, 21, 22
BLK_KV, BLK_G_PAD = 10, 11
VMEM_LIMIT = 56 * 1024 * 1024
NEG = -1e30
ADAM_LR, ADAM_B1, ADAM_B2, ADAM_EPS, ADAM_WD, ADAM_STEP = 0.001, 0.9, 0.999, 1e-08, 0.01, 10
MESH = pl.DeviceIdType.MESH


def _bf(x):
    return x.astype(BF16)


def _dot(a, b):
    return jnp.dot(a, b, preferred_element_type=F32)


def _dot_nt(a, b):
    return lax.dot_general(a, b, (((1,), (1,)), ((), ())), preferred_element_type=F32)


def _dot_tn(a, b):
    return lax.dot_general(a, b, (((0,), (0,)), ((), ())), preferred_element_type=F32)


def _sigmoid(x):
    return 1.0 / (1.0 + jnp.exp(-x))


def _params(sem):
    return pltpu.CompilerParams(dimension_semantics=sem, vmem_limit_bytes=VMEM_LIMIT)


def _mm(x, w, *, form, name, out_dtypes, tn, epi=None, extra=(), tm=512, w_row_block=0, after=None, norm=None,
        norm_bwd=None, then_nt=None):
    assert norm is None or norm_bwd is None
    xs = list(x) if isinstance(x, (list, tuple)) else [x]
    nx = len(xs)
    S, K = xs[0].shape
    shards = w.ndim == 3
    N = (w.shape[2] * N_DEV if shards else w.shape[1]) if form == "nn" else w.shape[-2]
    assert not (shards and form == "nn" and tn != w.shape[2]) and (nx == 1 or (form == "nn" and not shards and norm is None))
    r0 = w_row_block * K
    tm = min(tm, S)
    n_extra, n_out = len(extra), len(out_dtypes)
    tile = lambda width: pl.BlockSpec((tm, width), lambda i: (i, 0))
    whole = lambda a: pl.BlockSpec(a.shape, lambda i, nd=a.ndim: (0,) * nd)
    ins, in_specs = [*xs, w, *extra], [tile(K)] * nx + [whole(w)] + [tile(N)] * n_extra
    if norm is not None:
        ins, in_specs = ins + [norm], in_specs + [whole(norm)]
    if norm_bwd is not None:
        ins, in_specs = ins + list(norm_bwd), in_specs + [tile(N), whole(norm_bwd[1]), tile(N)]
    if then_nt is not None:
        ins, in_specs = ins + [then_nt], in_specs + [whole(then_nt)]
    if after is not None:
        ins, in_specs = ins + [after], in_specs + [whole(after)]
    out_shape = [jax.ShapeDtypeStruct((S, N), dt) for dt in out_dtypes]
    out_specs = [tile(N)] * n_out
    if norm is not None:
        out_shape, out_specs = out_shape + [jax.ShapeDtypeStruct((S, K), BF16)], out_specs + [tile(K)]
    if norm_bwd is not None:
        out_shape, out_specs = out_shape + [jax.ShapeDtypeStruct((1, N), F32)], out_specs + [pl.BlockSpec((1, N), lambda i: (0, 0))]
    if then_nt is not None:
        out_shape, out_specs = out_shape + [jax.ShapeDtypeStruct((S, then_nt.shape[0]), F32)], out_specs + [tile(then_nt.shape[0])]

    def product(xb, w_ref, cols, c):
        if form == "nn" and nx > 1:
            return sum(_dot(part, w_ref[r0 + p * K:r0 + (p + 1) * K, cols]) for p, part in enumerate(xb))
        if form == "nn":
            return _dot(xb, w_ref[c] if shards else w_ref[r0:r0 + K, cols])
        if not shards:
            return _dot_nt(xb, w_ref[cols, :])
        ks = w.shape[2]
        acc = _dot_nt(xb[:, 0:ks], w_ref[0, cols, :])
        for s in range(1, N_DEV):
            acc = acc + _dot_nt(xb[:, s * ks:(s + 1) * ks], w_ref[s, cols, :])
        return acc

    def body(*refs):
        x_ref, w_ref = refs[0], refs[nx]
        extra_refs = refs[nx + 1:nx + 1 + n_extra]
        at = nx + 1 + n_extra
        if norm is not None:
            gain_ref, at = refs[at], at + 1
        if norm_bwd is not None:
            (y_ref, ygain_ref, dres_ref), at = refs[at:at + 3], at + 3
        if then_nt is not None:
            w2_ref, at = refs[at], at + 1
        outs = refs[len(ins):]
        if norm is not None:
            _, xh = _rms_stats(x_ref[...])
            xb = _bf(xh * gain_ref[...])
            outs[n_out][...] = xb
        else:
            xb = _bf(x_ref[...]) if nx == 1 else [_bf(r[...]) for r in refs[:nx]]
        for c in range(N // tn):
            cols = slice(c * tn, (c + 1) * tn)
            acc = product(xb, w_ref, cols, c)
            res = epi(acc, *[r[:, cols] for r in extra_refs]) if epi else (acc,)
            for o, r in zip(outs[:n_out], res):
                o[:, cols] = r.astype(o.dtype)
        if norm_bwd is not None:
            dx, dg = _rms_bwd_tile(y_ref[...], ygain_ref[...], outs[0][...])
            outs[0][...] = dres_ref[...] + dx
            dg_ref = outs[n_out]

            @pl.when(pl.program_id(0) == 0)
            def _():
                dg_ref[...] = jnp.zeros_like(dg_ref)

            dg_ref[...] += dg
        if then_nt is not None:
            yb = _bf(outs[0][...])
            for c in range(then_nt.shape[0] // tn):
                cols = slice(c * tn, (c + 1) * tn)
                outs[-1][:, cols] = _dot_nt(yb, w2_ref[cols, :])

    return pl.pallas_call(
        body, grid=(S // tm,), name=name, in_specs=in_specs, out_specs=out_specs, out_shape=out_shape,
        compiler_params=_params(("arbitrary",) if norm_bwd is not None else ("parallel",)),
    )(*ins)


def _mm_tn(x, dy, *, name, tm, tn, out_dtype=F32, column_shards=False, after=None):
    S, K = x.shape
    N = dy.shape[1]
    waits = [] if after is None else [after]

    def body(x_ref, dy_ref, *rest):
        rest[-1][...] = _dot_tn(_bf(x_ref[...]), _bf(dy_ref[...])).astype(out_dtype)

    if column_shards:
        out_spec = pl.BlockSpec((None, tm, tn), lambda i, j: (j, i, 0))
        out_shape = jax.ShapeDtypeStruct((N // tn, K, tn), out_dtype)
    else:
        out_spec = pl.BlockSpec((tm, tn), lambda i, j: (i, j))
        out_shape = jax.ShapeDtypeStruct((K, N), out_dtype)
    return pl.pallas_call(
        body, grid=(K // tm, N // tn), name=name,
        in_specs=[pl.BlockSpec((S, tm), lambda i, j: (0, i)), pl.BlockSpec((S, tn), lambda i, j: (0, j))]
        + [pl.BlockSpec(memory_space=pl.ANY)] * len(waits),
        out_specs=out_spec, out_shape=out_shape,
        compiler_params=_params(("parallel", "parallel")),
    )(x, dy, *waits)


def _rowwise(body, *, tiled, full, out_tiled, out_acc, name, tm=512, smem=()):
    S = tiled[0].shape[0]
    tm = min(tm, S)
    n_in = len(smem) + len(tiled) + len(full)

    def kern(*refs):
        @pl.when(pl.program_id(0) == 0)
        def _():
            for r in refs[n_in + len(out_tiled):]:
                r[...] = jnp.zeros_like(r)
        body(*refs)

    in_specs = [pl.BlockSpec(memory_space=pltpu.SMEM) for _ in smem]
    in_specs += [pl.BlockSpec((tm, a.shape[1]), lambda i: (i, 0)) for a in tiled]
    in_specs += [pl.BlockSpec(a.shape, lambda i, nd=a.ndim: (0,) * nd) for a in full]
    out_specs = [pl.BlockSpec((tm, w), lambda i: (i, 0)) for w, _ in out_tiled]
    out_specs += [pl.BlockSpec(shp, lambda i, nd=len(shp): (0,) * nd) for shp, _ in out_acc]
    out_shape = [jax.ShapeDtypeStruct((S, w), dt) for w, dt in out_tiled]
    out_shape += [jax.ShapeDtypeStruct(shp, dt) for shp, dt in out_acc]
    return pl.pallas_call(
        kern, grid=(S // tm,), name=name, in_specs=in_specs, out_specs=out_specs, out_shape=out_shape,
        compiler_params=_params(("arbitrary",)),
    )(*smem, *tiled, *full)


def _rms_stats(x):
    r = lax.rsqrt(jnp.mean(x * x, axis=-1, keepdims=True) + EPS)
    return r, x * r


def _rmsnorm_fwd(x, g, name):
    def body(x_ref, g_ref, o_ref):
        _, xh = _rms_stats(x_ref[...])
        o_ref[...] = _bf(xh * g_ref[...])

    return _rowwise(body, tiled=[x], full=[g], out_tiled=[(x.shape[1], BF16)], out_acc=[], name=name)[0]


def _rms_bwd_tile(x, g, dxn):
    r, xh = _rms_stats(x)
    dg = jnp.sum(dxn * xh, axis=0, keepdims=True)
    dn = dxn * g
    dx = r * (dn - xh * jnp.mean(dn * xh, axis=-1, keepdims=True))
    return dx, dg


def _ple_and_loss(h2, p, target, w_pg, w_pp, g_ple, g_final):
    S, n = h2.shape
    tm = min(512, S)
    tn = 512

    def body(h2_ref, p_ref, t_ref, wpg_ref, wpp_ref, gple_ref, gfin_ref,
             n3_ref, dh_ref, dgl_ref, dpp_ref, loss_ref, dg_ref, dgple_ref, pp, gate, h3):
        @pl.when(pl.program_id(0) == 0)
        def _():
            loss_ref[...] = jnp.zeros_like(loss_ref)
            dg_ref[...] = jnp.zeros_like(dg_ref)
            dgple_ref[...] = jnp.zeros_like(dgple_ref)

        x = h2_ref[...]
        _, xh = _rms_stats(x)
        n3 = _bf(xh * gple_ref[...])
        n3_ref[...] = n3
        pb = _bf(p_ref[...])
        for c in range(n // tn):
            cols = slice(c * tn, (c + 1) * tn)
            pp[:, cols] = _dot(pb, wpp_ref[:, cols])
            gt = _sigmoid(_dot(n3, wpg_ref[:, cols]))
            gate[:, cols] = gt
            h3[:, cols] = x[:, cols] + gt * pp[:, cols]
        y = h3[...]
        _, yh = _rms_stats(y)
        e = yh * gfin_ref[...] - t_ref[...]
        per_tok = jnp.mean(e * e, axis=-1, keepdims=True)
        loss_ref[...] += 0.5 * jnp.sum(per_tok, axis=0, keepdims=True)
        dh, dg = _rms_bwd_tile(y, gfin_ref[...], e * (1.0 / n))
        dg_ref[...] += dg
        gt = gate[...]
        dgl = _bf(dh * pp[...] * gt * (1.0 - gt))
        dgl_ref[...] = dgl
        dpp_ref[...] = _bf(dh * gt)
        for c in range(n // tn):
            cols = slice(c * tn, (c + 1) * tn)
            h3[:, cols] = _dot_nt(dgl, wpg_ref[cols, :])
        dx, dgp = _rms_bwd_tile(x, gple_ref[...], h3[...])
        dh_ref[...] = dh + dx
        dgple_ref[...] += dgp

    tile = lambda width: pl.BlockSpec((tm, width), lambda i: (i, 0))
    whole = lambda a: pl.BlockSpec(a.shape, lambda i, nd=a.ndim: (0,) * nd)
    return pl.pallas_call(
        body, grid=(S // tm,), name="ple_and_loss",
        in_specs=[tile(n), tile(p.shape[1]), tile(n), whole(w_pg), whole(w_pp), whole(g_ple), whole(g_final)],
        out_specs=[tile(n), tile(n), tile(n), tile(n), pl.BlockSpec((1, 128), lambda i: (0, 0)),
                   pl.BlockSpec((1, n), lambda i: (0, 0)), pl.BlockSpec((1, n), lambda i: (0, 0))],
        out_shape=[jax.ShapeDtypeStruct((S, n), BF16), jax.ShapeDtypeStruct((S, n), F32), jax.ShapeDtypeStruct((S, n), BF16),
                   jax.ShapeDtypeStruct((S, n), BF16), jax.ShapeDtypeStruct((1, 128), F32), jax.ShapeDtypeStruct((1, n), F32),
                   jax.ShapeDtypeStruct((1, n), F32)],
        scratch_shapes=[pltpu.VMEM((tm, n), F32)] * 3,
        compiler_params=_params(("arbitrary",)),
    )(h2, p, target, w_pg, w_pp, g_ple, g_final)


def _rope_tables(S):
    half = 32
    inv = (1.0 / (np.float32(ROPE_THETA) ** (np.arange(half, dtype=np.float32) * np.float32(2.0 / 64)))).astype(np.float32)
    ang = np.arange(S).astype(np.float32)[:, None] * inv[None, :]
    cos, sin = np.cos(ang), np.sin(ang)
    return jnp.asarray(np.tile(cos, (1, 4))), jnp.asarray(np.concatenate([-sin, sin, -sin, sin], axis=1))


def _attn_common(i, kc, kp, vc, vp, cc, sc, cp, sp):
    lane = lax.broadcasted_iota(jnp.int32, (1, HEAD_PAIR), 1)
    lane_lo = jnp.bitwise_and(lane, 63) < 32
    slot = [lane < 64, lane >= 64]

    def swap_halves(t):
        return jnp.where(lane_lo, pltpu.roll(t, 96, 1), pltpu.roll(t, 32, 1))

    def rope(t, cos, sin):
        return t * cos + swap_halves(t) * sin

    def unrope(d, cos, sin):
        return d * cos + swap_halves(d * sin)

    k2 = jnp.concatenate([rope(kp, cp, sp), rope(kc, cc, sc)], axis=0)
    v2 = jnp.concatenate([vp, vc], axis=0)
    r = lax.broadcasted_iota(jnp.int32, (ATTN_BLOCK, 2 * ATTN_BLOCK), 0)
    c = lax.broadcasted_iota(jnp.int32, (ATTN_BLOCK, 2 * ATTN_BLOCK), 1)
    valid = (c > r) & (c <= r + ATTN_BLOCK) & jnp.logical_or(c >= ATTN_BLOCK, i > 0)
    ks, vs = {}, {}
    for j in range(2):
        kn = jnp.where(slot[j], k2, 0.0)
        vn = jnp.where(slot[j], v2, 0.0)
        for s in range(2):
            ks[j, s] = _bf(kn if s == j else pltpu.roll(kn, 64, 1))
            vs[j, s] = _bf(vn if s == j else pltpu.roll(vn, 64, 1))
    return slot, rope, unrope, valid, ks, vs


def _attn_probs(scores, valid, sink):
    s = jnp.where(valid, scores * 0.125, NEG)
    m = jnp.maximum(jnp.max(s, axis=1, keepdims=True), sink)
    e = jnp.exp(s - m)
    z = jnp.sum(e, axis=1, keepdims=True) + jnp.exp(sink - m)
    return e * (1.0 / z), m + jnp.log(z)


def _attn_specs(S):
    nb = S // ATTN_BLOCK
    prev = lambda i: jnp.maximum(i - 1, 0)
    blk = lambda w, col, row=(lambda i: i): pl.BlockSpec((ATTN_BLOCK, w), lambda i: (row(i), col))
    in_specs = [pl.BlockSpec(memory_space=pltpu.SMEM),
                blk(512, BLK_Q), blk(128, BLK_K), blk(128, BLK_K, prev), blk(128, BLK_V), blk(128, BLK_V, prev),
                blk(128, 0), blk(128, 0), blk(128, 0, prev), blk(128, 0, prev)]
    return nb, in_specs


def _attn_fwd(pa, cos, sin, sinks):
    S = pa.shape[0]
    nb, in_specs = _attn_specs(S)

    def body(sinks_ref, q_ref, kc_ref, kp_ref, vc_ref, vp_ref, cc_ref, sc_ref, cp_ref, sp_ref, o_ref, lse_ref):
        i = pl.program_id(0)
        lane = lax.broadcasted_iota(jnp.int32, (1, HEAD_PAIR), 1)
        cc, sc = cc_ref[...], sc_ref[...]
        _, rope, _, valid, ks, vs = _attn_common(i, kc_ref[...], kp_ref[...], vc_ref[...], vp_ref[...],
                                                 cc, sc, cp_ref[...], sp_ref[...])
        pair_cols = [slice(HEAD_PAIR * pair, HEAD_PAIR * (pair + 1)) for pair in range(4)]
        qps = [_bf(rope(q_ref[:, cols], cc, sc)) for cols in pair_cols]
        outs, lses = {}, {}

        def head_program(h):
            pair, s = divmod(h, 2)
            j = h // 4
            scores = _dot_nt(qps[pair], ks[j, s])
            yield
            p, lse = _attn_probs(scores, valid, sinks_ref[h])
            outs[h] = _dot(_bf(p), vs[j, s])
            lses[h] = jnp.where(lane == h, lse, 0.0)

        _interleave(head_program(h) for h in range(8))
        for pair, cols in enumerate(pair_cols):
            o_ref[:, cols] = outs[2 * pair] + outs[2 * pair + 1]
        lse_ref[...] = sum((lses[h] for h in range(1, 8)), lses[0])

    return pl.pallas_call(
        body, grid=(nb,), name="attn_fwd", in_specs=in_specs,
        out_specs=[pl.BlockSpec((ATTN_BLOCK, 512), lambda i: (i, 0)), pl.BlockSpec((ATTN_BLOCK, 128), lambda i: (i, 0))],
        out_shape=[jax.ShapeDtypeStruct((S, 512), F32), jax.ShapeDtypeStruct((S, 128), F32)],
        compiler_params=_params(("parallel",)),
    )(sinks, pa, pa, pa, pa, pa, cos, sin, cos, sin)


def _attn_bwd(pa, cos, sin, sinks, dcat, attn, lse):
    S = pa.shape[0]
    nb, in_specs = _attn_specs(S)
    in_specs = in_specs + [pl.BlockSpec((ATTN_BLOCK, 512), lambda i: (i, 0))] * 2 + [pl.BlockSpec((ATTN_BLOCK, 128), lambda i: (i, 0))]

    def body(sinks_ref, q_ref, kc_ref, kp_ref, vc_ref, vp_ref, cc_ref, sc_ref, cp_ref, sp_ref, do_ref, o_ref, lse_ref,
             dq_ref, dk_ref, dv_ref, dsink_ref):
        i = pl.program_id(0)

        @pl.when(i == 0)
        def _():
            dk_ref[...] = jnp.zeros_like(dk_ref)
            dv_ref[...] = jnp.zeros_like(dv_ref)
            dsink_ref[...] = jnp.zeros_like(dsink_ref)

        cc, sc, cp, sp = cc_ref[...], sc_ref[...], cp_ref[...], sp_ref[...]
        slot, rope, unrope, valid, ks, vs = _attn_common(i, kc_ref[...], kp_ref[...], vc_ref[...], vp_ref[...], cc, sc, cp, sp)
        pair_cols = [slice(HEAD_PAIR * pair, HEAD_PAIR * (pair + 1)) for pair in range(4)]
        qps = [_bf(rope(q_ref[:, cols], cc, sc)) for cols in pair_cols]
        dobs = [_bf(do_ref[:, cols]) for cols in pair_cols]
        do_o = [do_ref[:, cols] * o_ref[:, cols] for cols in pair_cols]
        dqs, dks, dvs = {}, {}, {}

        def head_program(h):
            pair, s = divmod(h, 2)
            j = h // 4
            qp, dob = qps[pair], dobs[pair]
            scores = _dot_nt(qp, ks[j, s])
            dp = _dot_nt(dob, vs[j, s])
            yield
            lse_h = lse_ref[:, h:h + 1]
            p = jnp.exp(jnp.where(valid, scores * 0.125, NEG) - lse_h)
            yield
            dr = jnp.sum(jnp.where(slot[s], do_o[pair], 0.0), axis=1, keepdims=True)
            ds = _bf(p * (dp - dr) * 0.125)
            yield
            dsink_ref[h:h + 1, :] += -jnp.sum(jnp.exp(sinks_ref[h] - lse_h) * dr, axis=0, keepdims=True)
            dqs[h] = _dot(ds, ks[j, s])
            dk_h = _dot_tn(ds, qp)
            dv_h = _dot_tn(_bf(p), dob)
            yield
            dk_h, dv_h = jnp.where(slot[s], dk_h, 0.0), jnp.where(slot[s], dv_h, 0.0)
            if s != j:
                dk_h, dv_h = pltpu.roll(dk_h, 64, 1), pltpu.roll(dv_h, 64, 1)
            dks[h], dvs[h] = dk_h, dv_h

        _interleave(head_program(h) for h in range(8))
        dk2 = sum((dks[h] for h in range(1, 8)), dks[0])
        dv2 = sum((dvs[h] for h in range(1, 8)), dvs[0])
        for pair, cols in enumerate(pair_cols):
            dq_ref[:, cols] = _bf(unrope(dqs[2 * pair] + dqs[2 * pair + 1], cc, sc))
        cur = pl.ds(pl.multiple_of(i * ATTN_BLOCK, ATTN_BLOCK), ATTN_BLOCK)
        dk_ref[cur, :] += unrope(dk2[ATTN_BLOCK:], cc, sc)
        dv_ref[cur, :] += dv2[ATTN_BLOCK:]

        @pl.when(i > 0)
        def _():
            prv = pl.ds(pl.multiple_of((i - 1) * ATTN_BLOCK, ATTN_BLOCK), ATTN_BLOCK)
            dk_ref[prv, :] += unrope(dk2[:ATTN_BLOCK], cp, sp)
            dv_ref[prv, :] += dv2[:ATTN_BLOCK]

    whole = lambda w: pl.BlockSpec((S, w), lambda i: (0, 0))
    return pl.pallas_call(
        body, grid=(nb,), name="attn_bwd", in_specs=in_specs,
        out_specs=[pl.BlockSpec((ATTN_BLOCK, 512), lambda i: (i, BLK_Q)), whole(128), whole(128),
                   pl.BlockSpec((8, 128), lambda i: (0, 0))],
        out_shape=[jax.ShapeDtypeStruct((S, D_IN_PAD), BF16), jax.ShapeDtypeStruct((S, 128), F32),
                   jax.ShapeDtypeStruct((S, 128), F32), jax.ShapeDtypeStruct((8, 128), F32)],
        compiler_params=_params(("arbitrary",)),
    )(sinks, pa, pa, pa, pa, pa, cos, sin, cos, sin, dcat, attn, lse)


CONV_ROWS = 512
CONV_PAD = 8


def _conv_silu(scr, w, r0):
    y = w[3:4, :] * scr[pl.ds(CONV_PAD + r0, CONV_ROWS), :]
    for j in range(DN_CONV - 1):
        y = y + w[j:j + 1, :] * scr[pl.ds(CONV_PAD + r0 - 3 + j, CONV_ROWS), :]
    return y


def _dn_prep_fwd(pd, conv_w):
    S = pd.shape[0]
    assert S % CONV_ROWS == 0

    def body(x_ref, w_ref, o_ref, scr):
        b = pl.program_id(0)
        scr[0:CONV_PAD, :] = jnp.zeros((CONV_PAD, DN_DIM), F32)
        scr[pl.ds(CONV_PAD, S), :] = x_ref[...]
        w = w_ref[...]
        q_scale = jnp.where(b < DN_HEADS, DN_DIM ** -0.5, 1.0)
        for r0 in range(0, S, CONV_ROWS):
            y = _conv_silu(scr, w, r0)
            a = y * _sigmoid(y)
            rs = lax.rsqrt(jnp.sum(a * a, axis=1, keepdims=True) + EPS)
            o_ref[pl.ds(r0, CONV_ROWS), :] = a * jnp.where(b < 2 * DN_HEADS, rs * q_scale, 1.0)

    col = pl.BlockSpec((S, DN_DIM), lambda b: (0, b))
    return pl.pallas_call(
        body, grid=(3 * DN_HEADS,), name="dn_prep_fwd",
        in_specs=[pl.BlockSpec((S, DN_DIM), lambda b: (0, BLK_DN + b)), pl.BlockSpec((DN_CONV, DN_DIM), lambda b: (0, b))],
        out_specs=col,
        out_shape=jax.ShapeDtypeStruct((S, 3 * DN_HEADS * DN_DIM), F32),
        scratch_shapes=[pltpu.VMEM((S + CONV_PAD, DN_DIM), F32)],
        compiler_params=_params(("parallel",)),
    )(pd, conv_w)


def _dn_prep_bwd(pd, conv_w, dqkv, dproj):
    S = pd.shape[0]

    def body(x_ref, w_ref, d_ref, _, dx_ref, dw_ref, scr, dscr):
        b = pl.program_id(0)
        scr[0:CONV_PAD, :] = jnp.zeros((CONV_PAD, DN_DIM), F32)
        scr[pl.ds(CONV_PAD, S), :] = x_ref[...]
        dscr[pl.ds(S, CONV_PAD), :] = jnp.zeros((CONV_PAD, DN_DIM), F32)
        w = w_ref[...]
        q_scale = jnp.where(b < DN_HEADS, DN_DIM ** -0.5, 1.0)
        is_qk = b < 2 * DN_HEADS
        dw = [jnp.zeros((1, DN_DIM), F32) for _ in range(DN_CONV)]
        for r0 in range(0, S, CONV_ROWS):
            y = _conv_silu(scr, w, r0)
            sg = _sigmoid(y)
            a = y * sg
            dout = d_ref[pl.ds(r0, CONV_ROWS), :]
            rs = lax.rsqrt(jnp.sum(a * a, axis=1, keepdims=True) + EPS)
            da_qk = q_scale * rs * (dout - a * (rs * rs) * jnp.sum(dout * a, axis=1, keepdims=True))
            dy = jnp.where(is_qk, da_qk, dout) * (sg * (1.0 + y * (1.0 - sg)))
            dscr[pl.ds(r0, CONV_ROWS), :] = dy
            for j in range(DN_CONV):
                dw[j] = dw[j] + jnp.sum(dy * scr[pl.ds(CONV_PAD + r0 - 3 + j, CONV_ROWS), :], axis=0, keepdims=True)
        for j in range(DN_CONV):
            dw_ref[j:j + 1, :] = dw[j]
        for r0 in range(0, S, CONV_ROWS):
            dx = w[3:4, :] * dscr[pl.ds(r0, CONV_ROWS), :]
            for j in range(DN_CONV - 1):
                dx = dx + w[j:j + 1, :] * dscr[pl.ds(r0 + 3 - j, CONV_ROWS), :]
            dx_ref[pl.ds(r0, CONV_ROWS), :] = _bf(dx)

    col = pl.BlockSpec((S, DN_DIM), lambda b: (0, b))
    proj_col = pl.BlockSpec((S, DN_DIM), lambda b: (0, BLK_DN + b))
    wcol = pl.BlockSpec((DN_CONV, DN_DIM), lambda b: (0, b))
    return pl.pallas_call(
        body, grid=(3 * DN_HEADS,), name="dn_prep_bwd",
        in_specs=[proj_col, wcol, col, pl.BlockSpec(memory_space=pl.ANY)], out_specs=[proj_col, wcol],
        out_shape=[jax.ShapeDtypeStruct(dproj.shape, dproj.dtype), jax.ShapeDtypeStruct((DN_CONV, 3 * DN_HEADS * DN_DIM), F32)],
        scratch_shapes=[pltpu.VMEM((S + CONV_PAD, DN_DIM), F32), pltpu.VMEM((S + CONV_PAD, DN_DIM), F32)],
        input_output_aliases={3: 0},
        compiler_params=_params(("parallel",)),
    )(pd, conv_w, dqkv, dproj)


CPAD = 128
CHUNKS_LOCAL = 4
CHUNKS_SCAN = 8


def _chunk_masks():
    ii = lax.broadcasted_iota(jnp.int32, (DN_CHUNK, CPAD), 0)
    jj = lax.broadcasted_iota(jnp.int32, (DN_CHUNK, CPAD), 1)
    return ii, jj


def _rows_pad(a):
    return jnp.concatenate([a, jnp.zeros_like(a)], axis=0)


def _hi_lo(a):
    hi = _bf(a)
    return hi, _bf(a - hi.astype(F32))


def _double_step(t, p):
    C = DN_CHUNK
    th, tl = _hi_lo(t)
    ph, pl_ = _hi_lo(p)
    r1 = _dot(jnp.concatenate([th, tl, ph, pl_], axis=0), _rows_pad(ph))
    r2 = _dot(jnp.concatenate([th, ph], axis=0), _rows_pad(pl_))
    return t + (r1[:C] + r1[C:2 * C] + r2[:C]), r1[2 * C:3 * C] + r1[3 * C:] + r2[C:]


def _dot3_nt(a, b):
    C = DN_CHUNK
    ah, al = _hi_lo(a)
    bh, bl = _hi_lo(b)
    r1 = _dot_nt(jnp.concatenate([ah, al], axis=0), _rows_pad(bh))
    return r1[:C] + r1[C:] + _dot_nt(ah, _rows_pad(bl))


def _dot3_tn(a, b):
    C = DN_CHUNK
    ah, al = _hi_lo(a)
    bh, bl = _hi_lo(b)
    return _dot_tn(jnp.concatenate([ah, al, ah], axis=0), jnp.concatenate([bh, bh, bl], axis=0))[:C]


def _interleave(programs):
    programs = list(programs)
    while programs:
        alive = []
        for prog in programs:
            try:
                next(prog)
                alive.append(prog)
            except StopIteration:
                pass
        programs = alive


def _col_to_row(col, ii, jj):
    return jnp.sum(jnp.where(ii == jj, col, 0.0), axis=0, keepdims=True)


def _row_to_col(row, ii, jj):
    return jnp.sum(jnp.where(ii == jj, row, 0.0), axis=1, keepdims=True)


def _decay(gc_col, ii, jj):
    diff = gc_col - _col_to_row(gc_col, ii, jj)
    return jnp.where(jj <= ii, jnp.exp(jnp.where(jj <= ii, diff, 0.0)), 0.0)


def _softplus(x):
    return jnp.maximum(x, 0.0) + jnp.log(1.0 + jnp.exp(-jnp.abs(x)))


def _head(h):
    return slice(DN_DIM * h, DN_DIM * (h + 1))


def _dn_chunk_fwd(qkv, pg, a_log, dt_bias):
    S = qkv.shape[0]
    C = DN_CHUNK
    G = CHUNKS_LOCAL
    R = G * C
    steps = S // R

    def body(alog_ref, dtb_ref, qkv_ref, pg_ref, w_ref, u_ref, qg_ref, kd_ref, a_ref, t_ref, gcs_ref):
        ii, jj = _chunk_masks()
        lane = lax.broadcasted_iota(jnp.int32, (1, 128), 1)
        eye = (ii == jj).astype(F32)
        gcs_parts = [[] for _ in range(G)]

        def head_program(chunk, h):
            rows = slice(chunk * C, (chunk + 1) * C)
            q, k, v = qkv_ref[rows, _head(h)], qkv_ref[rows, _head(DN_HEADS + h)], qkv_ref[rows, _head(2 * DN_HEADS + h)]
            beta = _sigmoid(pg_ref[rows, h:h + 1])
            g_col = -jnp.exp(alog_ref[h]) * _softplus(pg_ref[rows, DN_HEADS + h:DN_HEADS + h + 1] + dtb_ref[h])
            g_row = _col_to_row(g_col, ii, jj)
            gc_col = jnp.sum(jnp.where(jj <= ii, g_row, 0.0), axis=1, keepdims=True)
            dec = _decay(gc_col, ii, jj)
            eg = jnp.exp(gc_col)
            kb, vb = k * beta, v * beta
            k_rows = _rows_pad(_bf(k))
            kk = _dot_nt(_bf(kb), k_rows)
            qk = _dot_nt(_bf(q), k_rows)
            yield
            t, pw = eye, -jnp.where(jj < ii, kk * dec, 0.0)
            for _ in range(6):
                t, pw = _double_step(t, pw)
                yield
            tb = _bf(t)
            u_ref[rows, _head(h)] = _dot(tb, _rows_pad(_bf(vb)))
            w_ref[rows, _head(h)] = _bf(_dot(tb, _rows_pad(_bf(kb * eg))))
            a_ref[h, rows] = _bf(qk * dec)
            t_ref[h, rows] = t
            qg_ref[rows, _head(h)] = _bf(q * eg)
            kd_ref[rows, _head(h)] = _bf(k * jnp.exp(gc_col[C - 1:C, :] - gc_col))
            gcs_parts[chunk].append(jnp.where(lane == h, gc_col, 0.0) + jnp.where(lane == DN_HEADS + h, beta, 0.0)
                                    + jnp.where(lane == 2 * DN_HEADS + h, g_col, 0.0))

        _interleave(head_program(chunk, h) for chunk in range(G) for h in range(DN_HEADS))
        for chunk in range(G):
            gcs_ref[chunk * C:(chunk + 1) * C, :] = sum(gcs_parts[chunk][1:], gcs_parts[chunk][0])

    smem = pl.BlockSpec(memory_space=pltpu.SMEM)
    wide = pl.BlockSpec((R, 512), lambda n: (n, 0))
    sq = pl.BlockSpec((DN_HEADS, R, CPAD), lambda n: (0, n, 0))
    narrow = pl.BlockSpec((R, 128), lambda n: (n, 0))
    f = lambda *shp: jax.ShapeDtypeStruct(shp, F32)
    b = lambda *shp: jax.ShapeDtypeStruct(shp, BF16)
    return pl.pallas_call(
        body, grid=(steps,), name="dn_chunk_fwd",
        in_specs=[smem, smem, pl.BlockSpec((R, 1536), lambda n: (n, 0)), pl.BlockSpec((R, 128), lambda n: (n, BLK_G))],
        out_specs=[wide, wide, wide, wide, sq, sq, narrow],
        out_shape=[b(S, 512), f(S, 512), b(S, 512), b(S, 512), b(DN_HEADS, S, CPAD), f(DN_HEADS, S, CPAD), f(S, 128)],
        compiler_params=_params(("parallel",)),
    )(a_log, dt_bias, qkv, pg)


def _gated_norm(o, z, gn):
    r, oh = _rms_stats(o)
    return oh * gn * (z * _sigmoid(z))


def _dn_scan_fwd(w, u, qg, kd, a, gcs, pz, gn):
    S = w.shape[0]
    C = DN_CHUNK
    nc = S // C
    G = CHUNKS_SCAN
    R = G * C

    def body(w_ref, u_ref, qg_ref, kd_ref, a_ref, gcs_ref, z_ref, gn_ref, o_ref, vn_ref, sst_ref, out_ref, state):
        @pl.when(pl.program_id(0) == 0)
        def _():
            state[...] = jnp.zeros_like(state)

        def head_program(chunk, h):
            hs = _head(h)
            rows = slice(chunk * C, (chunk + 1) * C)
            s_in = state[h]
            sb = _bf(s_in)
            sst_ref[chunk, h] = sb
            w_s = _dot(w_ref[rows, hs], sb)
            q_s = _dot(qg_ref[rows, hs], sb)
            yield
            vn = u_ref[rows, hs] - w_s
            vnb = _bf(vn)
            o = q_s + _dot(a_ref[h, rows], _rows_pad(vnb))
            k_v = _dot_tn(kd_ref[rows, hs], vnb)
            yield
            state[h] = s_in * jnp.exp(gcs_ref[(chunk + 1) * C - 1:(chunk + 1) * C, h:h + 1]) + k_v
            o_ref[rows, hs] = o
            vn_ref[rows, hs] = vnb
            out_ref[rows, hs] = _bf(_gated_norm(o, z_ref[rows, hs], gn_ref[...]))

        for chunk in range(G):
            _interleave(head_program(chunk, h) for h in range(DN_HEADS))

    wide = pl.BlockSpec((R, 512), lambda n: (n, 0))
    f = lambda *shp: jax.ShapeDtypeStruct(shp, F32)
    b = lambda *shp: jax.ShapeDtypeStruct(shp, BF16)
    return pl.pallas_call(
        body, grid=(nc // G,), name="dn_scan_fwd",
        in_specs=[wide, wide, wide, wide, pl.BlockSpec((DN_HEADS, R, CPAD), lambda n: (0, n, 0)),
                  pl.BlockSpec((R, 128), lambda n: (n, 0)), pl.BlockSpec((R, 512), lambda n: (n, BLK_Z)),
                  pl.BlockSpec((1, DN_DIM), lambda n: (0, 0))],
        out_specs=[wide, wide, pl.BlockSpec((G, DN_HEADS, DN_DIM, DN_DIM), lambda n: (n, 0, 0, 0)), wide],
        out_shape=[f(S, 512), b(S, 512), b(nc, DN_HEADS, DN_DIM, DN_DIM), b(S, 512)],
        scratch_shapes=[pltpu.VMEM((DN_HEADS, DN_DIM, DN_DIM), F32)],
        compiler_params=_params(("arbitrary",)),
    )(w, u, qg, kd, a, gcs, pz, gn)


def _dn_scan_bwd(dcat, o, pz, gn, sst, vnew, w, qg, kd, a, gcs, dproj):
    S = o.shape[0]
    C = DN_CHUNK
    G = CHUNKS_SCAN
    R = G * C
    steps = S // R

    def body(dy_ref, o_ref, z_ref, gn_ref, sst_ref, vn_ref, w_ref, qg_ref, kd_ref, a_ref, gcs_ref, _,
             du_ref, dw_ref, dqg_ref, dkd_ref, da_ref, dz_ref, dsc_ref, dgn_ref, dstate):
        @pl.when(pl.program_id(0) == 0)
        def _():
            dstate[...] = jnp.zeros_like(dstate)
            dgn_ref[...] = jnp.zeros_like(dgn_ref)

        gn_ = gn_ref[...]
        lane = lax.broadcasted_iota(jnp.int32, (C, 128), 1)
        row = lax.broadcasted_iota(jnp.int32, (C, 128), 0)
        dgn_parts = []

        def head_program(chunk, h, dsc_parts):
            hs = _head(h)
            rows = slice(chunk * C, (chunk + 1) * C)
            ov, z, dout = o_ref[rows, hs], z_ref[rows, hs], dy_ref[rows, hs]
            r, oh = _rms_stats(ov)
            sg = _sigmoid(z)
            don = dout * (z * sg)
            dz_ref[rows, hs] = _bf(dout * (oh * gn_) * (sg * (1.0 + z * (1.0 - sg))))
            dgn_parts.append(jnp.sum(don * oh, axis=0, keepdims=True))
            dn = don * gn_
            do = _bf(r * (dn - oh * jnp.mean(dn * oh, axis=-1, keepdims=True)))
            sb = sst_ref[chunk, h]
            s_in = sb.astype(F32)
            ds_out = dstate[h]
            dsb = _bf(ds_out)
            vnb = vn_ref[rows, hs]
            wb, qgb, kdb, ab = w_ref[rows, hs], qg_ref[rows, hs], kd_ref[rows, hs], a_ref[h, rows]
            dvn = _dot_tn(ab, do)[:C] + _dot(kdb, dsb)
            yield
            da_ref[h, rows] = _dot_nt(do, _rows_pad(vnb))
            dqg_ref[rows, hs] = _dot_nt(do, sb)
            dkd_ref[rows, hs] = _dot_nt(vnb, dsb)
            q_do = _dot_tn(qgb, do)
            yield
            dvnb = _bf(dvn)
            dw_ref[rows, hs] = _bf(-_dot_nt(dvnb, sb))
            w_dvn = _dot_tn(wb, dvnb)
            du_ref[rows, hs] = dvnb
            yield
            d_last = jnp.exp(gcs_ref[(chunk + 1) * C - 1:(chunk + 1) * C, h:h + 1])
            dd = jnp.sum(jnp.sum(ds_out * s_in, axis=1, keepdims=True), axis=0, keepdims=True)
            dsc_parts.append(jnp.where((lane == h) & (row == C - 1), dd * d_last, 0.0))
            dstate[h] = ds_out * d_last + q_do - w_dvn

        for chunk in reversed(range(G)):
            dsc_parts = []
            _interleave(head_program(chunk, h, dsc_parts) for h in range(DN_HEADS))
            dsc_ref[chunk * C:(chunk + 1) * C, :] = sum(dsc_parts[1:], dsc_parts[0])
        dgn_ref[...] += sum(dgn_parts[1:], dgn_parts[0])

    rev = lambda n: steps - 1 - n
    wide = pl.BlockSpec((R, 512), lambda n: (rev(n), 0))
    z_spec = pl.BlockSpec((R, 512), lambda n: (rev(n), BLK_Z))
    sq = pl.BlockSpec((DN_HEADS, R, CPAD), lambda n: (0, rev(n), 0))
    narrow = pl.BlockSpec((R, 128), lambda n: (rev(n), 0))
    gn_spec = pl.BlockSpec((1, DN_DIM), lambda n: (0, 0))
    f = lambda *shp: jax.ShapeDtypeStruct(shp, F32)
    b = lambda *shp: jax.ShapeDtypeStruct(shp, BF16)
    return pl.pallas_call(
        body, grid=(steps,), name="dn_scan_bwd",
        in_specs=[pl.BlockSpec((R, 512), lambda n: (rev(n), 1)), wide, z_spec, gn_spec,
                  pl.BlockSpec((G, DN_HEADS, DN_DIM, DN_DIM), lambda n: (rev(n), 0, 0, 0)),
                  wide, wide, wide, wide, sq, narrow, pl.BlockSpec(memory_space=pl.ANY)],
        out_specs=[wide, wide, wide, wide, sq, z_spec, narrow, gn_spec],
        out_shape=[b(S, 512), b(S, 512), f(S, 512), f(S, 512), f(DN_HEADS, S, CPAD),
                   jax.ShapeDtypeStruct(dproj.shape, dproj.dtype), f(S, 128), f(1, DN_DIM)],
        scratch_shapes=[pltpu.VMEM((DN_HEADS, DN_DIM, DN_DIM), F32)],
        input_output_aliases={11: 5},
        compiler_params=_params(("arbitrary",)),
    )(dcat, o, pz, gn, sst, vnew, w, qg, kd, a, gcs, dproj)


def _dn_chunk_bwd(qkv, pg, t_inv, gcs, du, dw, dqg, dkd, da, dsc, a_log, dt_bias, dproj):
    S = qkv.shape[0]
    C = DN_CHUNK
    G = CHUNKS_LOCAL
    R = G * C

    def body(alog_ref, dtb_ref, qkv_ref, pg_ref, t_ref, gcs_ref, du_ref, dw_ref, dqg_ref, dkd_ref, da_ref, dsc_ref, _,
             dqkv_ref, dpg_ref, acc_ref):
        @pl.when(pl.program_id(0) == 0)
        def _():
            acc_ref[...] = jnp.zeros_like(acc_ref)

        ii, jj = _chunk_masks()
        lane = lax.broadcasted_iota(jnp.int32, (1, 128), 1)
        row8 = lax.broadcasted_iota(jnp.int32, (8, 128), 0)
        lane8 = lax.broadcasted_iota(jnp.int32, (8, 128), 1)
        rowc = lax.broadcasted_iota(jnp.int32, (C, 1), 0)
        tril, strict = jj <= ii, jj < ii
        dpg_parts, acc_parts = [[] for _ in range(G)], []

        def head_program(chunk, h):
            rows = slice(chunk * C, (chunk + 1) * C)
            q, k, v = qkv_ref[rows, _head(h)], qkv_ref[rows, _head(DN_HEADS + h)], qkv_ref[rows, _head(2 * DN_HEADS + h)]
            gc_col, beta, g_col = gcs_ref[rows, h:h + 1], gcs_ref[rows, DN_HEADS + h:DN_HEADS + h + 1], \
                gcs_ref[rows, 2 * DN_HEADS + h:2 * DN_HEADS + h + 1]
            dec = _decay(gc_col, ii, jj)
            eg = jnp.exp(gc_col)
            g_last = gc_col[C - 1:C, :]
            ek = jnp.exp(g_last - gc_col)
            kb, vb = k * beta, v * beta
            kbg = kb * eg
            qb, kbb = _bf(q), _bf(kb)
            k_rows = _rows_pad(_bf(k))
            t = t_ref[h, rows]
            tb = _bf(t)
            dub, dwb = du_ref[rows, _head(h)], dw_ref[rows, _head(h)]
            dqg_, dkd_ = dqg_ref[rows, _head(h)], dkd_ref[rows, _head(h)]
            dt = _dot_nt(dub, _rows_pad(_bf(vb))) + _dot_nt(dwb, _rows_pad(_bf(kbg)))
            t_du_dw = _dot_tn(tb, jnp.concatenate([dub, dwb], axis=1))
            dvb, dkbg = t_du_dw[:C, :DN_DIM], t_du_dw[:C, DN_DIM:]
            kk = _dot_nt(kbb, k_rows)
            qk = _dot_nt(qb, k_rows)
            yield
            dt_t = _dot3_nt(dt, t)
            yield
            dl = -_dot3_tn(t, dt_t)
            yield
            dm = jnp.where(strict, dl * dec, 0.0)
            dqk = jnp.where(tril, da_ref[h, rows] * dec, 0.0)
            gmat = dm * kk + dqk * qk
            dgc = jnp.sum(gmat, axis=1, keepdims=True) - _row_to_col(jnp.sum(gmat, axis=0, keepdims=True), ii, jj)
            dmb, dqkb = _bf(dm), _bf(dqk)
            yield
            dkb = _dot(dmb, k_rows) + dkbg * eg
            dk = _dot_tn(jnp.concatenate([dmb, dqkb], axis=0), jnp.concatenate([kbb, qb], axis=0))[:C] + dkd_ * ek
            dq = _dot(dqkb, k_rows) + dqg_ * eg
            yield
            tk = jnp.sum(dkd_ * k * ek, axis=1, keepdims=True)
            dgc = dgc + jnp.sum(dqg_ * q * eg, axis=1, keepdims=True) - tk + jnp.sum(dkbg * kbg, axis=1, keepdims=True)
            dgl = jnp.sum(tk, axis=0, keepdims=True) + dsc_ref[(chunk + 1) * C - 1:(chunk + 1) * C, h:h + 1]
            dgc = dgc + jnp.where(rowc == C - 1, dgl, 0.0)
            yield
            dk = dk + dkb * beta
            dbeta = jnp.sum(dkb * k, axis=1, keepdims=True) + jnp.sum(dvb * v, axis=1, keepdims=True)
            dqkv_ref[rows, _head(h)] = dq
            dqkv_ref[rows, _head(DN_HEADS + h)] = dk
            dqkv_ref[rows, _head(2 * DN_HEADS + h)] = dvb * beta
            dg_col = jnp.sum(jnp.where(jj >= ii, _col_to_row(dgc, ii, jj), 0.0), axis=1, keepdims=True)
            yield
            db = dbeta * beta * (1.0 - beta)
            da_in = dg_col * (-jnp.exp(alog_ref[h])) * _sigmoid(pg_ref[rows, DN_HEADS + h:DN_HEADS + h + 1] + dtb_ref[h])
            dpg_parts[chunk].append(jnp.where(lane == h, db, 0.0) + jnp.where(lane == DN_HEADS + h, da_in, 0.0))
            acc_parts.append(jnp.where((row8 == 0) & (lane8 == h), jnp.sum(dg_col * g_col, axis=0, keepdims=True), 0.0)
                             + jnp.where((row8 == 1) & (lane8 == h), jnp.sum(da_in, axis=0, keepdims=True), 0.0))

        _interleave(head_program(chunk, h) for chunk in range(G) for h in range(DN_HEADS))
        for chunk in range(G):
            dpg = sum(dpg_parts[chunk][1:], dpg_parts[chunk][0])
            dpg_ref[chunk * C:(chunk + 1) * C, :] = _bf(jnp.concatenate([dpg, jnp.zeros_like(dpg)], axis=1))
        acc_ref[...] += sum(acc_parts[1:], acc_parts[0])

    smem = pl.BlockSpec(memory_space=pltpu.SMEM)
    wide = pl.BlockSpec((R, 512), lambda n: (n, 0))
    sq = pl.BlockSpec((DN_HEADS, R, CPAD), lambda n: (0, n, 0))
    narrow = pl.BlockSpec((R, 128), lambda n: (n, 0))
    qkv_spec = pl.BlockSpec((R, 1536), lambda n: (n, 0))
    f = lambda *shp: jax.ShapeDtypeStruct(shp, F32)
    return pl.pallas_call(
        body, grid=(S // R,), name="dn_chunk_bwd",
        in_specs=[smem, smem, qkv_spec, pl.BlockSpec((R, 128), lambda n: (n, BLK_G)), sq, narrow, wide, wide, wide, wide, sq,
                  narrow, pl.BlockSpec(memory_space=pl.ANY)],
        out_specs=[qkv_spec, pl.BlockSpec((R, 256), lambda n: (n, BLK_G_PAD)), pl.BlockSpec((8, 128), lambda n: (0, 0))],
        out_shape=[f(S, 1536), jax.ShapeDtypeStruct(dproj.shape, dproj.dtype), f(8, 128)],
        input_output_aliases={12: 1},
        compiler_params=_params(("arbitrary",)),
    )(a_log, dt_bias, qkv, pg, t_inv, gcs, du, dw, dqg, dkd, da, dsc, dproj)


def _fill_kv(dk, dv, dproj):
    S = dk.shape[0]
    tm = min(512, S)

    def body(dk_ref, dv_ref, _, o_ref):
        o_ref[...] = _bf(jnp.concatenate([dk_ref[...], dv_ref[...]], axis=1))

    tile = pl.BlockSpec((tm, 128), lambda i: (i, 0))
    return pl.pallas_call(
        body, grid=(S // tm,), name="fill_kv",
        in_specs=[tile, tile, pl.BlockSpec(memory_space=pl.ANY)],
        out_specs=pl.BlockSpec((tm, 256), lambda i: (i, BLK_KV)),
        out_shape=jax.ShapeDtypeStruct(dproj.shape, dproj.dtype),
        input_output_aliases={2: 0},
        compiler_params=_params(("parallel",)),
    )(dk, dv, dproj)


def _w_in_to_internal(wt):
    return jnp.concatenate([wt[0:512], wt[2304:2816], wt[768:2304], wt[512:768], wt[2816:2824],
                            jnp.zeros((D_IN_PAD - D_IN, wt.shape[1]), wt.dtype)], axis=0)


def _w_in_from_internal(gt):
    return jnp.concatenate([gt[0:512], gt[2560:2816], gt[1024:2560], gt[512:1024], gt[2816:2824]], axis=0)


def _local_step(x, p, target, wts, first_weights, other_weights, ship_early):
    S = x.shape[0]
    cos, sin = _rope_tables(S)
    sinks, a_log, dt_bias = wts["sinks"].reshape(8), wts["a_log"].reshape(4), wts["dt_bias"].reshape(4)
    gn = wts["dn_norm"].reshape(1, DN_DIM)
    add = lambda acc, res: (acc + res,)

    u = _rmsnorm_fwd(x, wts["norm_mix"], "norm_mix_fwd")
    w_in_t, conv_w = first_weights(u)
    proj, = _mm(u, w_in_t, form="nt", name="in_proj", out_dtypes=[F32], tn=512)
    attn, lse = _attn_fwd(proj, cos, sin, sinks)
    qkv = _dn_prep_fwd(proj, conv_w)
    cw, cu, cqg, ckd, ca, ct, gcs = _dn_chunk_fwd(qkv, proj, a_log, dt_bias)
    o, vnew, sst, dn_out = _dn_scan_fwd(cw, cu, cqg, ckd, ca, gcs, proj, gn)
    w_o, = other_weights(("w_o",), dn_out)
    h1, = _mm([attn, dn_out], w_o, form="nn", name="out_proj", out_dtypes=[F32], tn=512, epi=add, extra=[x])

    def relu2(acc):
        r = jnp.maximum(acc, 0.0)
        return r * r, r

    w_up, = other_weights(("w_up",), h1)
    hid, relu, m = _mm(h1, w_up, form="nn", name="mlp_up", out_dtypes=[BF16, BF16], tn=512, epi=relu2, norm=wts["norm_mlp"])
    w_down, = other_weights(("w_down",), hid)
    h2, = _mm(hid, w_down, form="nn", name="mlp_down", out_dtypes=[F32], tn=512, epi=add, extra=[h1])
    w_pg, w_pp = other_weights(("w_ple_gate", "w_ple_proj"), h2)
    n3, dh2, dgl, dpp, loss, d_norm_final, d_norm_ple = _ple_and_loss(h2, p, target, w_pg, w_pp, wts["norm_ple"],
                                                                     wts["norm_final"].reshape(1, D_MODEL))
    g = {"norm_final": d_norm_final, "norm_ple": d_norm_ple}
    early = {"w_ple_gate": _mm_tn(n3, dgl, name="d_w_ple_gate", tm=512, tn=1024, out_dtype=BF16).reshape(N_DEV, 128, 1024),
             "w_ple_proj": _mm_tn(p, dpp, name="d_w_ple_proj", tm=256, tn=128, out_dtype=BF16, column_shards=True)}
    d_act, = _mm(dh2, w_down, form="nt", name="d_hidden", out_dtypes=[BF16], tn=512,
                 epi=lambda acc, r: (acc * (2.0 * r.astype(F32)),), extra=[relu])
    early["w_down"] = _mm_tn(hid, dh2, name="d_w_down", tm=512, tn=1024, out_dtype=BF16).reshape(N_DEV, 512, 1024)
    early["w_up"] = _mm_tn(m, d_act, name="d_w_up", tm=1024, tn=512, out_dtype=BF16, column_shards=True)
    token = ship_early(early)
    dh1, g["norm_mlp"], dcat = _mm(d_act, w_up, form="nt", name="d_m", out_dtypes=[F32], tn=512, after=token,
                                   norm_bwd=(h1, wts["norm_mlp"], dh2), then_nt=w_o)
    d_w_o = jnp.concatenate([_mm_tn(attn, dh1, name="d_w_o_attn", tm=512, tn=512, out_dtype=BF16),
                             _mm_tn(dn_out, dh1, name="d_w_o_dn", tm=512, tn=512, out_dtype=BF16)], axis=0)
    token = ship_early({"w_o": d_w_o.reshape(N_DEV, 128, 1024)})
    dproj, dk, dv, dsinks = _attn_bwd(proj, cos, sin, sinks + token[0, 0], dcat, attn, lse)
    g["sinks"] = dsinks[:, 0].reshape(1, 8)
    du_, dw_, dqg, dkd, da, dproj, dsc, g["dn_norm"] = _dn_scan_bwd(dcat, o, proj, gn, sst, vnew, cw, cqg, ckd, ca, gcs, dproj)
    dqkv, dproj, gate_acc = _dn_chunk_bwd(qkv, proj, ct, gcs, du_, dw_, dqg, dkd, da, dsc, a_log, dt_bias, dproj)
    g["a_log"], g["dt_bias"] = gate_acc[0:1, 0:4], gate_acc[1:2, 0:4]
    dproj, g["conv_w"] = _dn_prep_bwd(proj, conv_w, dqkv, dproj)
    dproj = _fill_kv(dk, dv, dproj)
    token = ship_early({"w_in": _mm_tn(dproj, u, name="d_w_in", tm=512, tn=1024, out_dtype=BF16)})
    grad_x, g["norm_mix"] = _mm(dproj, w_in_t, form="nn", name="d_u", out_dtypes=[F32], tn=512, after=token,
                                norm_bwd=(x, wts["norm_mix"], dh1))
    return loss, grad_x, g


def _peer(k):
    x, y, c = lax.axis_index("x"), lax.axis_index("y"), lax.axis_index("c")
    px = 1 - x if k & 4 else x
    py = 1 - y if k & 2 else y
    pc = 1 - c if k & 1 else c
    return (px, py, pc), 4 * px + 2 * py + pc


def _exchange(srcs, name, gather):
    n = len(srcs)
    gathers = list(gather) if isinstance(gather, (list, tuple)) else [gather] * n
    shapes = [(N_DEV,) + s.shape if gt else s.shape for s, gt in zip(srcs, gathers)]

    def body(*refs):
        src_refs, out_refs = refs[:n], refs[n:2 * n]
        send_sems, recv_sems, local_sems = refs[2 * n:]
        _, me = _peer(0)
        piece = lambda a, d: src_refs[a] if gathers[a] else src_refs[a].at[d]
        local = [pltpu.make_async_copy(piece(a, me), out_refs[a].at[me], local_sems.at[a]) for a in range(n)]
        for cp in local:
            cp.start()
        copies = []
        for a in range(n):
            for k in range(1, N_DEV):
                dev, idx = _peer(k)
                cp = pltpu.make_async_remote_copy(src_ref=piece(a, idx), dst_ref=out_refs[a].at[me],
                                                  send_sem=send_sems.at[a, k - 1], recv_sem=recv_sems.at[a, k - 1],
                                                  device_id=dev, device_id_type=MESH)
                cp.start()
                copies.append(cp)
        for cp in copies:
            cp.wait_recv()
        for cp in copies:
            cp.wait_send()
        for cp in local:
            cp.wait()

    anywhere = pl.BlockSpec(memory_space=pl.ANY)
    return pl.pallas_call(
        body, name=name, in_specs=[anywhere] * n, out_specs=[anywhere] * n,
        out_shape=[jax.ShapeDtypeStruct(shp, s.dtype) for shp, s in zip(shapes, srcs)],
        scratch_shapes=[pltpu.SemaphoreType.DMA((n, N_DEV - 1)), pltpu.SemaphoreType.DMA((n, N_DEV - 1)),
                        pltpu.SemaphoreType.DMA((n,))],
    )(*srcs)


_HBM = pl.BlockSpec(memory_space=pltpu.HBM)
_SEM = pl.BlockSpec(memory_space=pltpu.SEMAPHORE)
_EFFECT = pltpu.SideEffectType.DATAFLOW_SIDE_EFFECTING


def _split_copies(src_refs, land_refs, send_sems, recv_sems, modes, which=None):
    _, me = _peer(0)
    copies = []
    which = range(len(src_refs)) if which is None else which
    for a, src, land in zip(which, src_refs, land_refs):
        if modes[a] == "columns":
            n_cols = src.shape[1]
            dst = land.at[:, pl.ds(pl.multiple_of(me * n_cols, n_cols), n_cols)]
        else:
            dst = land.at[me]
        for k in range(1, N_DEV):
            dev, idx = _peer(k)
            sem = a * (N_DEV - 1) + k - 1
            copies.append(pltpu.make_async_remote_copy(
                src_ref=src.at[idx] if modes[a] == "pieces" else src, dst_ref=dst, send_sem=send_sems.at[sem],
                recv_sem=recv_sems.at[sem], device_id=dev, device_id_type=MESH))
    return copies


def _exchange_start(srcs, name, modes):
    n = len(srcs)
    modes = [modes] * n if isinstance(modes, str) else list(modes)
    me = 4 * lax.axis_index("x") + 2 * lax.axis_index("y") + lax.axis_index("c")
    lands = []
    for s, mode in zip(srcs, modes):
        if mode == "columns":
            empty = lax.empty((s.shape[0], N_DEV * s.shape[1]), s.dtype)
            lands.append(lax.dynamic_update_slice(empty, s, (0, me * s.shape[1])))
        else:
            own = s if mode == "slots" else lax.dynamic_index_in_dim(s, me, 0, keepdims=False)
            shape = (N_DEV,) + s.shape if mode == "slots" else s.shape
            lands.append(lax.dynamic_update_index_in_dim(lax.empty(shape, s.dtype), own, me, 0))

    def body(*refs):
        src_refs, land_refs = refs[:n], refs[n:2 * n]
        send_sems, recv_sems = refs[2 * n], refs[2 * n + 1]
        for cp in _split_copies(src_refs, land_refs, send_sems, recv_sems, modes):
            cp.start()
        refs[-1][...] = jnp.zeros_like(refs[-1])

    both = list(srcs) + lands
    sems = pltpu.SemaphoreType.DMA((n * (N_DEV - 1),))
    out = pl.pallas_call(
        body, name=name,
        out_shape=(sems, sems, *[pltpu.HBM(t.shape, t.dtype) for t in both], jax.ShapeDtypeStruct((8, 128), F32)),
        in_specs=[_HBM] * (2 * n), out_specs=(_SEM, _SEM, *[_HBM] * (2 * n), pl.BlockSpec(memory_space=pltpu.VMEM)),
        input_output_aliases={i: 2 + i for i in range(2 * n)},
        compiler_params=pltpu.CompilerParams(has_side_effects=_EFFECT),
    )(*[pltpu.with_memory_space_constraint(t, pltpu.HBM) for t in both])
    return (n, modes, out[:-1]), out[-1]


def _exchange_wait(handle, after, name, which=None):
    n_all, modes, (send_sems, recv_sems, *both_all) = handle
    which = list(range(n_all)) if which is None else list(which)
    n = len(which)
    both = [both_all[a] for a in which] + [both_all[n_all + a] for a in which]

    def body(*refs):
        src_refs, land_refs = refs[:n], refs[n:2 * n]
        for cp in _split_copies(src_refs, land_refs, refs[2 * n], refs[2 * n + 1], modes, which):
            cp.wait_send()
            cp.wait_recv()

    out = pl.pallas_call(
        body, name=name, out_shape=tuple(pltpu.HBM(t.shape, t.dtype) for t in both),
        in_specs=[_HBM] * (2 * n) + [_SEM, _SEM, pl.BlockSpec(memory_space=pl.ANY)], out_specs=tuple([_HBM] * (2 * n)),
        input_output_aliases={i: i for i in range(2 * n)},
        compiler_params=pltpu.CompilerParams(has_side_effects=_EFFECT),
    )(*both, send_sems, recv_sems, after)
    return list(out[n:])


def _adam_update(g, w, m, v):
    nm = ADAM_B1 * m + (1.0 - ADAM_B1) * g
    nv = ADAM_B2 * v + (1.0 - ADAM_B2) * (g * g)
    m_hat = nm / (1.0 - ADAM_B1 ** ADAM_STEP)
    v_hat = nv / (1.0 - ADAM_B2 ** ADAM_STEP)
    return -ADAM_LR * (m_hat / (jnp.sqrt(v_hat) + ADAM_EPS) + ADAM_WD * w), nm, nv


def _adamw(parts, w, m, v, name):
    n, R, W = parts.shape
    tm = 128 if R % 128 == 0 else R

    def body(p_ref, w_ref, m_ref, v_ref, g_ref, d_ref, nm_ref, nv_ref):
        g = p_ref[0].astype(F32)
        for s in range(1, n):
            g = g + p_ref[s].astype(F32)
        g_ref[...] = g
        d_ref[...], nm_ref[...], nv_ref[...] = _adam_update(g, w_ref[...], m_ref[...], v_ref[...])

    tile = pl.BlockSpec((tm, W), lambda i: (i, 0))
    return pl.pallas_call(
        body, grid=(R // tm,), name=name,
        in_specs=[pl.BlockSpec((n, tm, W), lambda i: (0, i, 0)), tile, tile, tile],
        out_specs=[tile] * 4, out_shape=[jax.ShapeDtypeStruct((R, W), F32)] * 4,
        compiler_params=_params(("parallel",)),
    )(parts, w, m, v)


_MATRICES = ("w_in", "w_o", "w_up", "w_down", "w_ple_gate", "w_ple_proj")


_OTHERS = ("w_o", "w_up", "w_down", "w_ple_gate", "w_ple_proj")
_OTHER_MODES = {"w_o": "slots", "w_up": "slots", "w_down": "slots", "w_ple_gate": "slots", "w_ple_proj": "columns"}


_VECTORS = ("norm_mix", "norm_mlp", "norm_ple", "norm_final", "a_log", "dt_bias", "sinks", "dn_norm")
_SMALL_ROWS, _LOSS_ROW, _CONV_ROW = 16, 8, 9


def _pack_small(vectors, loss, conv):
    def body(*refs):
        out = refs[-1]
        out[...] = jnp.zeros_like(out)
        for r, ref in enumerate(refs[:len(_VECTORS)]):
            out[r:r + 1, 0:ref.shape[1]] = ref[...]
        out[_LOSS_ROW:_LOSS_ROW + 1, 0:128] = refs[len(_VECTORS)][...]
        out[_CONV_ROW:_CONV_ROW + 6, :] = refs[len(_VECTORS) + 1][...]

    return pl.pallas_call(body, name="pack_small", out_shape=jax.ShapeDtypeStruct((_SMALL_ROWS, 1024), F32))(*vectors, loss, conv)


def _sum_slots(parts):
    def body(p_ref, o_ref):
        acc = p_ref[0]
        for s in range(1, parts.shape[0]):
            acc = acc + p_ref[s]
        o_ref[...] = acc

    return pl.pallas_call(body, name="sum_small", out_shape=jax.ShapeDtypeStruct(parts.shape[1:], parts.dtype))(parts)


def _adamw_vectors(summed, conv_g, wmv):
    names = _VECTORS + ("conv_w",)
    flat = [a for triple in wmv for a in triple]

    def body(*refs):
        sum_ref, conv_ref = refs[0], refs[1]
        ins, outs = refs[2:2 + len(flat)], refs[2 + len(flat):]
        for i in range(len(names)):
            w_ref, m_ref, v_ref = ins[3 * i:3 * i + 3]
            g = conv_ref[...] if i == len(_VECTORS) else sum_ref[i:i + 1, 0:w_ref.shape[1]]
            outs[4 * i][...] = g
            outs[4 * i + 1][...], outs[4 * i + 2][...], outs[4 * i + 3][...] = _adam_update(g, w_ref[...], m_ref[...], v_ref[...])

    out_shape = [jax.ShapeDtypeStruct(t[0].shape, F32) for t in wmv for _ in range(4)]
    res = pl.pallas_call(body, name="adamw_vectors", out_shape=out_shape)(summed, conv_g, *flat)
    return {n: res[4 * i:4 * i + 4] for i, n in enumerate(names)}


_ORDER = ("norm_mix", "w_in", "conv_w", "a_log", "dt_bias", "dn_norm", "sinks", "w_o", "norm_mlp", "w_up", "w_down",
          "norm_ple", "w_ple_gate", "w_ple_proj", "norm_final")


def kernel(x, p, norm_mix, w_in, conv_w, a_log, dt_bias, dn_norm, sinks, w_o, norm_mlp, w_up, w_down, norm_ple, w_ple_gate, w_ple_proj, norm_final, loss_target, m_norm_mix, m_w_in, m_conv_w, m_a_log, m_dt_bias, m_dn_norm, m_sinks, m_w_o, m_norm_mlp, m_w_up, m_w_down, m_norm_ple, m_w_ple_gate, m_w_ple_proj, m_norm_final, v_norm_mix, v_w_in, v_conv_w, v_a_log, v_dt_bias, v_dn_norm, v_sinks, v_w_o, v_norm_mlp, v_w_up, v_w_down, v_norm_ple, v_w_ple_gate, v_w_ple_proj, v_norm_final):
    w = dict(norm_mix=norm_mix, w_in=w_in[0], conv_w=conv_w[0], a_log=a_log, dt_bias=dt_bias, dn_norm=dn_norm, sinks=sinks,
             w_o=w_o[0], norm_mlp=norm_mlp, w_up=w_up[0], w_down=w_down[0], norm_ple=norm_ple, w_ple_gate=w_ple_gate[0],
             w_ple_proj=w_ple_proj[0], norm_final=norm_final)
    m = dict(norm_mix=m_norm_mix, w_in=m_w_in[0], conv_w=m_conv_w[0], a_log=m_a_log, dt_bias=m_dt_bias, dn_norm=m_dn_norm,
             sinks=m_sinks, w_o=m_w_o[0], norm_mlp=m_norm_mlp, w_up=m_w_up[0], w_down=m_w_down[0], norm_ple=m_norm_ple,
             w_ple_gate=m_w_ple_gate[0], w_ple_proj=m_w_ple_proj[0], norm_final=m_norm_final)
    v = dict(norm_mix=v_norm_mix, w_in=v_w_in[0], conv_w=v_conv_w[0], a_log=v_a_log, dt_bias=v_dt_bias, dn_norm=v_dn_norm,
             sinks=v_sinks, w_o=v_w_o[0], norm_mlp=v_norm_mlp, w_up=v_w_up[0], w_down=v_w_down[0], norm_ple=v_norm_ple,
             w_ple_gate=v_w_ple_gate[0], w_ple_proj=v_w_ple_proj[0], norm_final=v_norm_final)
    me = 4 * lax.axis_index("x") + 2 * lax.axis_index("y") + lax.axis_index("c")
    conv_shard = conv_w.shape[2]

    for d in (w, m, v):
        d["w_in"] = d["w_in"].T
    conv_pad = jnp.pad(w["conv_w"], ((0, 8 - DN_CONV), (0, 256 - conv_shard)))
    gathers, token_gather = _exchange_start([_bf(w["w_in"]), conv_pad] + [_bf(w[n]) for n in _OTHERS], "gather_start",
                                            ["slots", "slots"] + [_OTHER_MODES[n] for n in _OTHERS])
    vectors = dict(w)
    vectors["norm_mix"] = w["norm_mix"] + token_gather[0:1, 0:1]

    def first_weights(after):
        w_in_all, conv_all = _exchange_wait(gathers, after, "gather_first_wait", [0, 1])
        conv_all = jnp.transpose(conv_all[:, :DN_CONV, :conv_shard], (1, 0, 2)).reshape(DN_CONV, N_DEV * conv_shard)
        return _w_in_to_internal(w_in_all.reshape(D_IN, D_MODEL)), conv_all

    as_taken = {"w_o": lambda t: t.reshape(1024, 1024), "w_up": lambda t: t, "w_down": lambda t: t.reshape(4096, 1024),
                "w_ple_gate": lambda t: t.reshape(1024, 1024), "w_ple_proj": lambda t: t}

    def other_weights(names, after):
        which = [2 + _OTHERS.index(n) for n in names]
        got = _exchange_wait(gathers, after, "gather_wait_" + names[0], which)
        return [as_taken[n](t) for n, t in zip(names, got)]

    shipped = []

    def ship_early(pieces):
        names = tuple(pieces)
        if names == ("w_in",):
            pieces = {"w_in": _w_in_from_internal(pieces["w_in"]).reshape(N_DEV, D_IN // N_DEV, D_MODEL)}
        handle, token = _exchange_start([pieces[n] for n in names], "scatter_start_" + names[0], "pieces")
        shipped.append((names, handle))
        return token

    loss, grad_x, g = _local_step(x[0], p[0, 0], loss_target[0], vectors, first_weights, other_weights, ship_early)

    row = lambda t: t.reshape(1, t.size)
    small = _pack_small([row(g[n]) for n in _VECTORS], loss, g["conv_w"].reshape(6, 1024))
    small_all, = _exchange([small], "gather_small", gather=True)
    summed = _sum_slots(small_all)
    conv_g = lax.dynamic_slice(summed[_CONV_ROW:_CONV_ROW + 6].reshape(DN_CONV, N_DEV * conv_shard), (0, me * conv_shard),
                               (DN_CONV, conv_shard))
    small_out = _adamw_vectors(summed, conv_g, [(row(w[n]), row(m[n]), row(v[n])) for n in _VECTORS]
                               + [(w["conv_w"], m["conv_w"], v["conv_w"])])
    big, after = {}, small_out["conv_w"][0]
    for names, handle in shipped:
        for n, r in zip(names, _exchange_wait(handle, after, "scatter_wait_" + names[0])):
            big[n] = _adamw(r, w[n], m[n], v[n], "adamw_" + n)
            after = big[n][1]

    result = [summed[_LOSS_ROW, 0], grad_x[None]]
    for i in range(4):
        for n in _ORDER:
            if n == "w_in":
                result.append(big[n][i].T[None])
            elif n in _MATRICES:
                result.append(big[n][i][None])
            elif n == "conv_w":
                result.append(small_out[n][i][None])
            else:
                result.append(small_out[n][i].reshape(w[n].shape))
    return tuple(result)
```

```python
import jax
import jax.numpy as jnp
import numpy as np
from jax import lax
from jax.experimental import pallas as pl
from jax.experimental.pallas import tpu as pltpu

F32, BF16 = jnp.float32, jnp.bfloat16
EPS = 1e-6
D_MODEL = 1024
N_DEV = 8
ATTN_BLOCK = 128
HEAD_PAIR = 128
DN_HEADS = 4
DN_DIM = 128
DN_CHUNK = 64
DN_CONV = 4
ROPE_THETA = 10000.0
D_IN = 2824
D_IN_PAD = 3072
BLK_Q, BLK_Z = 0, 1
BLK_DN, BLK_K, BLK_V, BLK_G = 8, 20, 21, 22
BLK_KV, BLK_G_PAD = 10, 11
VMEM_LIMIT = 56 * 1024 * 1024
NEG = -1e30
ADAM_LR, ADAM_B1, ADAM_B2, ADAM_EPS, ADAM_WD, ADAM_STEP = 0.001, 0.9, 0.999, 1e-08, 0.01, 10
MESH = pl.DeviceIdType.MESH


def _bf(x):
    return x.astype(BF16)


def _dot(a, b):
    return jnp.dot(a, b, preferred_element_type=F32)


def _dot_nt(a, b):
    return lax.dot_general(a, b, (((1,), (1,)), ((), ())), preferred_element_type=F32)


def _dot_tn(a, b):
    return lax.dot_general(a, b, (((0,), (0,)), ((), ())), preferred_element_type=F32)


def _sigmoid(x):
    return 1.0 / (1.0 + jnp.exp(-x))


def _params(sem):
    return pltpu.CompilerParams(dimension_semantics=sem, vmem_limit_bytes=VMEM_LIMIT)


def _mm(x, w, *, form, name, out_dtypes, tn, epi=None, extra=(), tm=512, w_row_block=0, after=None, norm=None,
        norm_bwd=None, then_nt=None):
    assert norm is None or norm_bwd is None
    xs = list(x) if isinstance(x, (list, tuple)) else [x]
    nx = len(xs)
    S, K = xs[0].shape
    shards = w.ndim == 3
    N = (w.shape[2] * N_DEV if shards else w.shape[1]) if form == "nn" else w.shape[-2]
    assert not (shards and form == "nn" and tn != w.shape[2]) and (nx == 1 or (form == "nn" and not shards and norm is None))
    r0 = w_row_block * K
    tm = min(tm, S)
    n_extra, n_out = len(extra), len(out_dtypes)
    tile = lambda width: pl.BlockSpec((tm, width), lambda i: (i, 0))
    whole = lambda a: pl.BlockSpec(a.shape, lambda i, nd=a.ndim: (0,) * nd)
    ins, in_specs = [*xs, w, *extra], [tile(K)] * nx + [whole(w)] + [tile(N)] * n_extra
    if norm is not None:
        ins, in_specs = ins + [norm], in_specs + [whole(norm)]
    if norm_bwd is not None:
        ins, in_specs = ins + list(norm_bwd), in_specs + [tile(N), whole(norm_bwd[1]), tile(N)]
    if then_nt is not None:
        ins, in_specs = ins + [then_nt], in_specs + [whole(then_nt)]
    if after is not None:
        ins, in_specs = ins + [after], in_specs + [whole(after)]
    out_shape = [jax.ShapeDtypeStruct((S, N), dt) for dt in out_dtypes]
    out_specs = [tile(N)] * n_out
    if norm is not None:
        out_shape, out_specs = out_shape + [jax.ShapeDtypeStruct((S, K), BF16)], out_specs + [tile(K)]
    if norm_bwd is not None:
        out_shape, out_specs = out_shape + [jax.ShapeDtypeStruct((1, N), F32)], out_specs + [pl.BlockSpec((1, N), lambda i: (0, 0))]
    if then_nt is not None:
        out_shape, out_specs = out_shape + [jax.ShapeDtypeStruct((S, then_nt.shape[0]), F32)], out_specs + [tile(then_nt.shape[0])]

    def product(xb, w_ref, cols, c):
        if form == "nn" and nx > 1:
            return sum(_dot(part, w_ref[r0 + p * K:r0 + (p + 1) * K, cols]) for p, part in enumerate(xb))
        if form == "nn":
            return _dot(xb, w_ref[c] if shards else w_ref[r0:r0 + K, cols])
        if not shards:
            return _dot_nt(xb, w_ref[cols, :])
        ks = w.shape[2]
        acc = _dot_nt(xb[:, 0:ks], w_ref[0, cols, :])
        for s in range(1, N_DEV):
            acc = acc + _dot_nt(xb[:, s * ks:(s + 1) * ks], w_ref[s, cols, :])
        return acc

    def body(*refs):
        x_ref, w_ref = refs[0], refs[nx]
        extra_refs = refs[nx + 1:nx + 1 + n_extra]
        at = nx + 1 + n_extra
        if norm is not None:
            gain_ref, at = refs[at], at + 1
        if norm_bwd is not None:
            (y_ref, ygain_ref, dres_ref), at = refs[at:at + 3], at + 3
        if then_nt is not None:
            w2_ref, at = refs[at], at + 1
        outs = refs[len(ins):]
        if norm is not None:
            _, xh = _rms_stats(x_ref[...])
            xb = _bf(xh * gain_ref[...])
            outs[n_out][...] = xb
        else:
            xb = _bf(x_ref[...]) if nx == 1 else [_bf(r[...]) for r in refs[:nx]]
        for c in range(N // tn):
            cols = slice(c * tn, (c + 1) * tn)
            acc = product(xb, w_ref, cols, c)
            res = epi(acc, *[r[:, cols] for r in extra_refs]) if epi else (acc,)
            for o, r in zip(outs[:n_out], res):
                o[:, cols] = r.astype(o.dtype)
        if norm_bwd is not None:
            dx, dg = _rms_bwd_tile(y_ref[...], ygain_ref[...], outs[0][...])
            outs[0][...] = dres_ref[...] + dx
            dg_ref = outs[n_out]

            @pl.when(pl.program_id(0) == 0)
            def _():
                dg_ref[...] = jnp.zeros_like(dg_ref)

            dg_ref[...] += dg
        if then_nt is not None:
            yb = _bf(outs[0][...])
            for c in range(then_nt.shape[0] // tn):
                cols = slice(c * tn, (c + 1) * tn)
                outs[-1][:, cols] = _dot_nt(yb, w2_ref[cols, :])

    return pl.pallas_call(
        body, grid=(S // tm,), name=name, in_specs=in_specs, out_specs=out_specs, out_shape=out_shape,
        compiler_params=_params(("arbitrary",) if norm_bwd is not None else ("parallel",)),
    )(*ins)


def _mm_tn(x, dy, *, name, tm, tn, out_dtype=F32, column_shards=False, after=None):
    S, K = x.shape
    N = dy.shape[1]
    waits = [] if after is None else [after]

    def body(x_ref, dy_ref, *rest):
        rest[-1][...] = _dot_tn(_bf(x_ref[...]), _bf(dy_ref[...])).astype(out_dtype)

    if column_shards:
        out_spec = pl.BlockSpec((None, tm, tn), lambda i, j: (j, i, 0))
        out_shape = jax.ShapeDtypeStruct((N // tn, K, tn), out_dtype)
    else:
        out_spec = pl.BlockSpec((tm, tn), lambda i, j: (i, j))
        out_shape = jax.ShapeDtypeStruct((K, N), out_dtype)
    return pl.pallas_call(
        body, grid=(K // tm, N // tn), name=name,
        in_specs=[pl.BlockSpec((S, tm), lambda i, j: (0, i)), pl.BlockSpec((S, tn), lambda i, j: (0, j))]
        + [pl.BlockSpec(memory_space=pl.ANY)] * len(waits),
        out_specs=out_spec, out_shape=out_shape,
        compiler_params=_params(("parallel", "parallel")),
    )(x, dy, *waits)


def _rowwise(body, *, tiled, full, out_tiled, out_acc, name, tm=512, smem=()):
    S = tiled[0].shape[0]
    tm = min(tm, S)
    n_in = len(smem) + len(tiled) + len(full)

    def kern(*refs):
        @pl.when(pl.program_id(0) == 0)
        def _():
            for r in refs[n_in + len(out_tiled):]:
                r[...] = jnp.zeros_like(r)
        body(*refs)

    in_specs = [pl.BlockSpec(memory_space=pltpu.SMEM) for _ in smem]
    in_specs += [pl.BlockSpec((tm, a.shape[1]), lambda i: (i, 0)) for a in tiled]
    in_specs += [pl.BlockSpec(a.shape, lambda i, nd=a.ndim: (0,) * nd) for a in full]
    out_specs = [pl.BlockSpec((tm, w), lambda i: (i, 0)) for w, _ in out_tiled]
    out_specs += [pl.BlockSpec(shp, lambda i, nd=len(shp): (0,) * nd) for shp, _ in out_acc]
    out_shape = [jax.ShapeDtypeStruct((S, w), dt) for w, dt in out_tiled]
    out_shape += [jax.ShapeDtypeStruct(shp, dt) for shp, dt in out_acc]
    return pl.pallas_call(
        kern, grid=(S // tm,), name=name, in_specs=in_specs, out_specs=out_specs, out_shape=out_shape,
        compiler_params=_params(("arbitrary",)),
    )(*smem, *tiled, *full)


def _rms_stats(x):
    r = lax.rsqrt(jnp.mean(x * x, axis=-1, keepdims=True) + EPS)
    return r, x * r


def _rmsnorm_fwd(x, g, name):
    def body(x_ref, g_ref, o_ref):
        _, xh = _rms_stats(x_ref[...])
        o_ref[...] = _bf(xh * g_ref[...])

    return _rowwise(body, tiled=[x], full=[g], out_tiled=[(x.shape[1], BF16)], out_acc=[], name=name)[0]


def _rms_bwd_tile(x, g, dxn):
    r, xh = _rms_stats(x)
    dg = jnp.sum(dxn * xh, axis=0, keepdims=True)
    dn = dxn * g
    dx = r * (dn - xh * jnp.mean(dn * xh, axis=-1, keepdims=True))
    return dx, dg


def _ple_and_loss(h2, p, target, w_pg, w_pp, g_ple, g_final):
    S, n = h2.shape
    tm = min(512, S)
    tn = 512

    def body(h2_ref, p_ref, t_ref, wpg_ref, wpp_ref, gple_ref, gfin_ref,
             n3_ref, dh_ref, dgl_ref, dpp_ref, loss_ref, dg_ref, dgple_ref, pp, gate, h3):
        @pl.when(pl.program_id(0) == 0)
        def _():
            loss_ref[...] = jnp.zeros_like(loss_ref)
            dg_ref[...] = jnp.zeros_like(dg_ref)
            dgple_ref[...] = jnp.zeros_like(dgple_ref)

        x = h2_ref[...]
        _, xh = _rms_stats(x)
        n3 = _bf(xh * gple_ref[...])
        n3_ref[...] = n3
        pb = _bf(p_ref[...])
        for c in range(n // tn):
            cols = slice(c * tn, (c + 1) * tn)
            pp[:, cols] = _dot(pb, wpp_ref[:, cols])
            gt = _sigmoid(_dot(n3, wpg_ref[:, cols]))
            gate[:, cols] = gt
            h3[:, cols] = x[:, cols] + gt * pp[:, cols]
        y = h3[...]
        _, yh = _rms_stats(y)
        e = yh * gfin_ref[...] - t_ref[...]
        per_tok = jnp.mean(e * e, axis=-1, keepdims=True)
        loss_ref[...] += 0.5 * jnp.sum(per_tok, axis=0, keepdims=True)
        dh, dg = _rms_bwd_tile(y, gfin_ref[...], e * (1.0 / n))
        dg_ref[...] += dg
        gt = gate[...]
        dgl = _bf(dh * pp[...] * gt * (1.0 - gt))
        dgl_ref[...] = dgl
        dpp_ref[...] = _bf(dh * gt)
        for c in range(n // tn):
            cols = slice(c * tn, (c + 1) * tn)
            h3[:, cols] = _dot_nt(dgl, wpg_ref[cols, :])
        dx, dgp = _rms_bwd_tile(x, gple_ref[...], h3[...])
        dh_ref[...] = dh + dx
        dgple_ref[...] += dgp

    tile = lambda width: pl.BlockSpec((tm, width), lambda i: (i, 0))
    whole = lambda a: pl.BlockSpec(a.shape, lambda i, nd=a.ndim: (0,) * nd)
    return pl.pallas_call(
        body, grid=(S // tm,), name="ple_and_loss",
        in_specs=[tile(n), tile(p.shape[1]), tile(n), whole(w_pg), whole(w_pp), whole(g_ple), whole(g_final)],
        out_specs=[tile(n), tile(n), tile(n), tile(n), pl.BlockSpec((1, 128), lambda i: (0, 0)),
                   pl.BlockSpec((1, n), lambda i: (0, 0)), pl.BlockSpec((1, n), lambda i: (0, 0))],
        out_shape=[jax.ShapeDtypeStruct((S, n), BF16), jax.ShapeDtypeStruct((S, n), F32), jax.ShapeDtypeStruct((S, n), BF16),
                   jax.ShapeDtypeStruct((S, n), BF16), jax.ShapeDtypeStruct((1, 128), F32), jax.ShapeDtypeStruct((1, n), F32),
                   jax.ShapeDtypeStruct((1, n), F32)],
        scratch_shapes=[pltpu.VMEM((tm, n), F32)] * 3,
        compiler_params=_params(("arbitrary",)),
    )(h2, p, target, w_pg, w_pp, g_ple, g_final)


def _rope_tables(S):
    half = 32
    inv = (1.0 / (np.float32(ROPE_THETA) ** (np.arange(half, dtype=np.float32) * np.float32(2.0 / 64)))).astype(np.float32)
    ang = np.arange(S).astype(np.float32)[:, None] * inv[None, :]
    cos, sin = np.cos(ang), np.sin(ang)
    return jnp.asarray(np.tile(cos, (1, 4))), jnp.asarray(np.concatenate([-sin, sin, -sin, sin], axis=1))


def _attn_common(i, kc, kp, vc, vp, cc, sc, cp, sp):
    lane = lax.broadcasted_iota(jnp.int32, (1, HEAD_PAIR), 1)
    lane_lo = jnp.bitwise_and(lane, 63) < 32
    slot = [lane < 64, lane >= 64]

    def swap_halves(t):
        return jnp.where(lane_lo, pltpu.roll(t, 96, 1), pltpu.roll(t, 32, 1))

    def rope(t, cos, sin):
        return t * cos + swap_halves(t) * sin

    def unrope(d, cos, sin):
        return d * cos + swap_halves(d * sin)

    k2 = jnp.concatenate([rope(kp, cp, sp), rope(kc, cc, sc)], axis=0)
    v2 = jnp.concatenate([vp, vc], axis=0)
    r = lax.broadcasted_iota(jnp.int32, (ATTN_BLOCK, 2 * ATTN_BLOCK), 0)
    c = lax.broadcasted_iota(jnp.int32, (ATTN_BLOCK, 2 * ATTN_BLOCK), 1)
    valid = (c > r) & (c <= r + ATTN_BLOCK) & jnp.logical_or(c >= ATTN_BLOCK, i > 0)
    ks, vs = {}, {}
    for j in range(2):
        kn = jnp.where(slot[j], k2, 0.0)
        vn = jnp.where(slot[j], v2, 0.0)
        for s in range(2):
            ks[j, s] = _bf(kn if s == j else pltpu.roll(kn, 64, 1))
            vs[j, s] = _bf(vn if s == j else pltpu.roll(vn, 64, 1))
    return slot, rope, unrope, valid, ks, vs


def _attn_probs(scores, valid, sink):
    s = jnp.where(valid, scores * 0.125, NEG)
    m = jnp.maximum(jnp.max(s, axis=1, keepdims=True), sink)
    e = jnp.exp(s - m)
    z = jnp.sum(e, axis=1, keepdims=True) + jnp.exp(sink - m)
    return e * (1.0 / z), m + jnp.log(z)


def _attn_specs(S):
    nb = S // ATTN_BLOCK
    prev = lambda i: jnp.maximum(i - 1, 0)
    blk = lambda w, col, row=(lambda i: i): pl.BlockSpec((ATTN_BLOCK, w), lambda i: (row(i), col))
    in_specs = [pl.BlockSpec(memory_space=pltpu.SMEM),
                blk(512, BLK_Q), blk(128, BLK_K), blk(128, BLK_K, prev), blk(128, BLK_V), blk(128, BLK_V, prev),
                blk(128, 0), blk(128, 0), blk(128, 0, prev), blk(128, 0, prev)]
    return nb, in_specs


def _attn_fwd(pa, cos, sin, sinks):
    S = pa.shape[0]
    nb, in_specs = _attn_specs(S)

    def body(sinks_ref, q_ref, kc_ref, kp_ref, vc_ref, vp_ref, cc_ref, sc_ref, cp_ref, sp_ref, o_ref, lse_ref):
        i = pl.program_id(0)
        lane = lax.broadcasted_iota(jnp.int32, (1, HEAD_PAIR), 1)
        cc, sc = cc_ref[...], sc_ref[...]
        _, rope, _, valid, ks, vs = _attn_common(i, kc_ref[...], kp_ref[...], vc_ref[...], vp_ref[...],
                                                 cc, sc, cp_ref[...], sp_ref[...])
        pair_cols = [slice(HEAD_PAIR * pair, HEAD_PAIR * (pair + 1)) for pair in range(4)]
        qps = [_bf(rope(q_ref[:, cols], cc, sc)) for cols in pair_cols]
        outs, lses = {}, {}

        def head_program(h):
            pair, s = divmod(h, 2)
            j = h // 4
            scores = _dot_nt(qps[pair], ks[j, s])
            yield
            p, lse = _attn_probs(scores, valid, sinks_ref[h])
            outs[h] = _dot(_bf(p), vs[j, s])
            lses[h] = jnp.where(lane == h, lse, 0.0)

        _interleave(head_program(h) for h in range(8))
        for pair, cols in enumerate(pair_cols):
            o_ref[:, cols] = outs[2 * pair] + outs[2 * pair + 1]
        lse_ref[...] = sum((lses[h] for h in range(1, 8)), lses[0])

    return pl.pallas_call(
        body, grid=(nb,), name="attn_fwd", in_specs=in_specs,
        out_specs=[pl.BlockSpec((ATTN_BLOCK, 512), lambda i: (i, 0)), pl.BlockSpec((ATTN_BLOCK, 128), lambda i: (i, 0))],
        out_shape=[jax.ShapeDtypeStruct((S, 512), F32), jax.ShapeDtypeStruct((S, 128), F32)],
        compiler_params=_params(("parallel",)),
    )(sinks, pa, pa, pa, pa, pa, cos, sin, cos, sin)


def _attn_bwd(pa, cos, sin, sinks, dcat, attn, lse):
    S = pa.shape[0]
    nb, in_specs = _attn_specs(S)
    in_specs = in_specs + [pl.BlockSpec((ATTN_BLOCK, 512), lambda i: (i, 0))] * 2 + [pl.BlockSpec((ATTN_BLOCK, 128), lambda i: (i, 0))]

    def body(sinks_ref, q_ref, kc_ref, kp_ref, vc_ref, vp_ref, cc_ref, sc_ref, cp_ref, sp_ref, do_ref, o_ref, lse_ref,
             dq_ref, dk_ref, dv_ref, dsink_ref):
        i = pl.program_id(0)

        @pl.when(i == 0)
        def _():
            dk_ref[...] = jnp.zeros_like(dk_ref)
            dv_ref[...] = jnp.zeros_like(dv_ref)
            dsink_ref[...] = jnp.zeros_like(dsink_ref)

        cc, sc, cp, sp = cc_ref[...], sc_ref[...], cp_ref[...], sp_ref[...]
        slot, rope, unrope, valid, ks, vs = _attn_common(i, kc_ref[...], kp_ref[...], vc_ref[...], vp_ref[...], cc, sc, cp, sp)
        pair_cols = [slice(HEAD_PAIR * pair, HEAD_PAIR * (pair + 1)) for pair in range(4)]
        qps = [_bf(rope(q_ref[:, cols], cc, sc)) for cols in pair_cols]
        dobs = [_bf(do_ref[:, cols]) for cols in pair_cols]
        do_o = [do_ref[:, cols] * o_ref[:, cols] for cols in pair_cols]
        dqs, dks, dvs = {}, {}, {}

        def head_program(h):
            pair, s = divmod(h, 2)
            j = h // 4
            qp, dob = qps[pair], dobs[pair]
            scores = _dot_nt(qp, ks[j, s])
            dp = _dot_nt(dob, vs[j, s])
            yield
            lse_h = lse_ref[:, h:h + 1]
            p = jnp.exp(jnp.where(valid, scores * 0.125, NEG) - lse_h)
            yield
            dr = jnp.sum(jnp.where(slot[s], do_o[pair], 0.0), axis=1, keepdims=True)
            ds = _bf(p * (dp - dr) * 0.125)
            yield
            dsink_ref[h:h + 1, :] += -jnp.sum(jnp.exp(sinks_ref[h] - lse_h) * dr, axis=0, keepdims=True)
            dqs[h] = _dot(ds, ks[j, s])
            dk_h = _dot_tn(ds, qp)
            dv_h = _dot_tn(_bf(p), dob)
            yield
            dk_h, dv_h = jnp.where(slot[s], dk_h, 0.0), jnp.where(slot[s], dv_h, 0.0)
            if s != j:
                dk_h, dv_h = pltpu.roll(dk_h, 64, 1), pltpu.roll(dv_h, 64, 1)
            dks[h], dvs[h] = dk_h, dv_h

        _interleave(head_program(h) for h in range(8))
        dk2 = sum((dks[h] for h in range(1, 8)), dks[0])
        dv2 = sum((dvs[h] for h in range(1, 8)), dvs[0])
        for pair, cols in enumerate(pair_cols):
            dq_ref[:, cols] = _bf(unrope(dqs[2 * pair] + dqs[2 * pair + 1], cc, sc))
        cur = pl.ds(pl.multiple_of(i * ATTN_BLOCK, ATTN_BLOCK), ATTN_BLOCK)
        dk_ref[cur, :] += unrope(dk2[ATTN_BLOCK:], cc, sc)
        dv_ref[cur, :] += dv2[ATTN_BLOCK:]

        @pl.when(i > 0)
        def _():
            prv = pl.ds(pl.multiple_of((i - 1) * ATTN_BLOCK, ATTN_BLOCK), ATTN_BLOCK)
            dk_ref[prv, :] += unrope(dk2[:ATTN_BLOCK], cp, sp)
            dv_ref[prv, :] += dv2[:ATTN_BLOCK]

    whole = lambda w: pl.BlockSpec((S, w), lambda i: (0, 0))
    return pl.pallas_call(
        body, grid=(nb,), name="attn_bwd", in_specs=in_specs,
        out_specs=[pl.BlockSpec((ATTN_BLOCK, 512), lambda i: (i, BLK_Q)), whole(128), whole(128),
                   pl.BlockSpec((8, 128), lambda i: (0, 0))],
        out_shape=[jax.ShapeDtypeStruct((S, D_IN_PAD), BF16), jax.ShapeDtypeStruct((S, 128), F32),
                   jax.ShapeDtypeStruct((S, 128), F32), jax.ShapeDtypeStruct((8, 128), F32)],
        compiler_params=_params(("arbitrary",)),
    )(sinks, pa, pa, pa, pa, pa, cos, sin, cos, sin, dcat, attn, lse)


CONV_ROWS = 512
CONV_PAD = 8


def _conv_silu(scr, w, r0):
    y = w[3:4, :] * scr[pl.ds(CONV_PAD + r0, CONV_ROWS), :]
    for j in range(DN_CONV - 1):
        y = y + w[j:j + 1, :] * scr[pl.ds(CONV_PAD + r0 - 3 + j, CONV_ROWS), :]
    return y


def _dn_prep_fwd(pd, conv_w):
    S = pd.shape[0]
    assert S % CONV_ROWS == 0

    def body(x_ref, w_ref, o_ref, scr):
        b = pl.program_id(0)
        scr[0:CONV_PAD, :] = jnp.zeros((CONV_PAD, DN_DIM), F32)
        scr[pl.ds(CONV_PAD, S), :] = x_ref[...]
        w = w_ref[...]
        q_scale = jnp.where(b < DN_HEADS, DN_DIM ** -0.5, 1.0)
        for r0 in range(0, S, CONV_ROWS):
            y = _conv_silu(scr, w, r0)
            a = y * _sigmoid(y)
            rs = lax.rsqrt(jnp.sum(a * a, axis=1, keepdims=True) + EPS)
            o_ref[pl.ds(r0, CONV_ROWS), :] = a * jnp.where(b < 2 * DN_HEADS, rs * q_scale, 1.0)

    col = pl.BlockSpec((S, DN_DIM), lambda b: (0, b))
    return pl.pallas_call(
        body, grid=(3 * DN_HEADS,), name="dn_prep_fwd",
        in_specs=[pl.BlockSpec((S, DN_DIM), lambda b: (0, BLK_DN + b)), pl.BlockSpec((DN_CONV, DN_DIM), lambda b: (0, b))],
        out_specs=col,
        out_shape=jax.ShapeDtypeStruct((S, 3 * DN_HEADS * DN_DIM), F32),
        scratch_shapes=[pltpu.VMEM((S + CONV_PAD, DN_DIM), F32)],
        compiler_params=_params(("parallel",)),
    )(pd, conv_w)


def _dn_prep_bwd(pd, conv_w, dqkv, dproj):
    S = pd.shape[0]

    def body(x_ref, w_ref, d_ref, _, dx_ref, dw_ref, scr, dscr):
        b = pl.program_id(0)
        scr[0:CONV_PAD, :] = jnp.zeros((CONV_PAD, DN_DIM), F32)
        scr[pl.ds(CONV_PAD, S), :] = x_ref[...]
        dscr[pl.ds(S, CONV_PAD), :] = jnp.zeros((CONV_PAD, DN_DIM), F32)
        w = w_ref[...]
        q_scale = jnp.where(b < DN_HEADS, DN_DIM ** -0.5, 1.0)
        is_qk = b < 2 * DN_HEADS
        dw = [jnp.zeros((1, DN_DIM), F32) for _ in range(DN_CONV)]
        for r0 in range(0, S, CONV_ROWS):
            y = _conv_silu(scr, w, r0)
            sg = _sigmoid(y)
            a = y * sg
            dout = d_ref[pl.ds(r0, CONV_ROWS), :]
            rs = lax.rsqrt(jnp.sum(a * a, axis=1, keepdims=True) + EPS)
            da_qk = q_scale * rs * (dout - a * (rs * rs) * jnp.sum(dout * a, axis=1, keepdims=True))
            dy = jnp.where(is_qk, da_qk, dout) * (sg * (1.0 + y * (1.0 - sg)))
            dscr[pl.ds(r0, CONV_ROWS), :] = dy
            for j in range(DN_CONV):
                dw[j] = dw[j] + jnp.sum(dy * scr[pl.ds(CONV_PAD + r0 - 3 + j, CONV_ROWS), :], axis=0, keepdims=True)
        for j in range(DN_CONV):
            dw_ref[j:j + 1, :] = dw[j]
        for r0 in range(0, S, CONV_ROWS):
            dx = w[3:4, :] * dscr[pl.ds(r0, CONV_ROWS), :]
            for j in range(DN_CONV - 1):
                dx = dx + w[j:j + 1, :] * dscr[pl.ds(r0 + 3 - j, CONV_ROWS), :]
            dx_ref[pl.ds(r0, CONV_ROWS), :] = _bf(dx)

    col = pl.BlockSpec((S, DN_DIM), lambda b: (0, b))
    proj_col = pl.BlockSpec((S, DN_DIM), lambda b: (0, BLK_DN + b))
    wcol = pl.BlockSpec((DN_CONV, DN_DIM), lambda b: (0, b))
    return pl.pallas_call(
        body, grid=(3 * DN_HEADS,), name="dn_prep_bwd",
        in_specs=[proj_col, wcol, col, pl.BlockSpec(memory_space=pl.ANY)], out_specs=[proj_col, wcol],
        out_shape=[jax.ShapeDtypeStruct(dproj.shape, dproj.dtype), jax.ShapeDtypeStruct((DN_CONV, 3 * DN_HEADS * DN_DIM), F32)],
        scratch_shapes=[pltpu.VMEM((S + CONV_PAD, DN_DIM), F32), pltpu.VMEM((S + CONV_PAD, DN_DIM), F32)],
        input_output_aliases={3: 0},
        compiler_params=_params(("parallel",)),
    )(pd, conv_w, dqkv, dproj)


CPAD = 128
CHUNKS_LOCAL = 4
CHUNKS_SCAN = 8


def _chunk_masks():
    ii = lax.broadcasted_iota(jnp.int32, (DN_CHUNK, CPAD), 0)
    jj = lax.broadcasted_iota(jnp.int32, (DN_CHUNK, CPAD), 1)
    return ii, jj


def _rows_pad(a):
    return jnp.concatenate([a, jnp.zeros_like(a)], axis=0)


def _hi_lo(a):
    hi = _bf(a)
    return hi, _bf(a - hi.astype(F32))


def _double_step(t, p):
    C = DN_CHUNK
    th, tl = _hi_lo(t)
    ph, pl_ = _hi_lo(p)
    r1 = _dot(jnp.concatenate([th, tl, ph, pl_], axis=0), _rows_pad(ph))
    r2 = _dot(jnp.concatenate([th, ph], axis=0), _rows_pad(pl_))
    return t + (r1[:C] + r1[C:2 * C] + r2[:C]), r1[2 * C:3 * C] + r1[3 * C:] + r2[C:]


def _dot3_nt(a, b):
    C = DN_CHUNK
    ah, al = _hi_lo(a)
    bh, bl = _hi_lo(b)
    r1 = _dot_nt(jnp.concatenate([ah, al], axis=0), _rows_pad(bh))
    return r1[:C] + r1[C:] + _dot_nt(ah, _rows_pad(bl))


def _dot3_tn(a, b):
    C = DN_CHUNK
    ah, al = _hi_lo(a)
    bh, bl = _hi_lo(b)
    return _dot_tn(jnp.concatenate([ah, al, ah], axis=0), jnp.concatenate([bh, bh, bl], axis=0))[:C]


def _interleave(programs):
    programs = list(programs)
    while programs:
        alive = []
        for prog in programs:
            try:
                next(prog)
                alive.append(prog)
            except StopIteration:
                pass
        programs = alive


def _col_to_row(col, ii, jj):
    return jnp.sum(jnp.where(ii == jj, col, 0.0), axis=0, keepdims=True)


def _row_to_col(row, ii, jj):
    return jnp.sum(jnp.where(ii == jj, row, 0.0), axis=1, keepdims=True)


def _decay(gc_col, ii, jj):
    diff = gc_col - _col_to_row(gc_col, ii, jj)
    return jnp.where(jj <= ii, jnp.exp(jnp.where(jj <= ii, diff, 0.0)), 0.0)


def _softplus(x):
    return jnp.maximum(x, 0.0) + jnp.log(1.0 + jnp.exp(-jnp.abs(x)))


def _head(h):
    return slice(DN_DIM * h, DN_DIM * (h + 1))


def _dn_chunk_fwd(qkv, pg, a_log, dt_bias):
    S = qkv.shape[0]
    C = DN_CHUNK
    G = CHUNKS_LOCAL
    R = G * C
    steps = S // R

    def body(alog_ref, dtb_ref, qkv_ref, pg_ref, w_ref, u_ref, qg_ref, kd_ref, a_ref, t_ref, gcs_ref):
        ii, jj = _chunk_masks()
        lane = lax.broadcasted_iota(jnp.int32, (1, 128), 1)
        eye = (ii == jj).astype(F32)
        gcs_parts = [[] for _ in range(G)]

        def head_program(chunk, h):
            rows = slice(chunk * C, (chunk + 1) * C)
            q, k, v = qkv_ref[rows, _head(h)], qkv_ref[rows, _head(DN_HEADS + h)], qkv_ref[rows, _head(2 * DN_HEADS + h)]
            beta = _sigmoid(pg_ref[rows, h:h + 1])
            g_col = -jnp.exp(alog_ref[h]) * _softplus(pg_ref[rows, DN_HEADS + h:DN_HEADS + h + 1] + dtb_ref[h])
            g_row = _col_to_row(g_col, ii, jj)
            gc_col = jnp.sum(jnp.where(jj <= ii, g_row, 0.0), axis=1, keepdims=True)
            dec = _decay(gc_col, ii, jj)
            eg = jnp.exp(gc_col)
            kb, vb = k * beta, v * beta
            k_rows = _rows_pad(_bf(k))
            kk = _dot_nt(_bf(kb), k_rows)
            qk = _dot_nt(_bf(q), k_rows)
            yield
            t, pw = eye, -jnp.where(jj < ii, kk * dec, 0.0)
            for _ in range(6):
                t, pw = _double_step(t, pw)
                yield
            tb = _bf(t)
            u_ref[rows, _head(h)] = _dot(tb, _rows_pad(_bf(vb)))
            w_ref[rows, _head(h)] = _bf(_dot(tb, _rows_pad(_bf(kb * eg))))
            a_ref[h, rows] = _bf(qk * dec)
            t_ref[h, rows] = t
            qg_ref[rows, _head(h)] = _bf(q * eg)
            kd_ref[rows, _head(h)] = _bf(k * jnp.exp(gc_col[C - 1:C, :] - gc_col))
            gcs_parts[chunk].append(jnp.where(lane == h, gc_col, 0.0) + jnp.where(lane == DN_HEADS + h, beta, 0.0)
                                    + jnp.where(lane == 2 * DN_HEADS + h, g_col, 0.0))

        _interleave(head_program(chunk, h) for chunk in range(G) for h in range(DN_HEADS))
        for chunk in range(G):
            gcs_ref[chunk * C:(chunk + 1) * C, :] = sum(gcs_parts[chunk][1:], gcs_parts[chunk][0])

    smem = pl.BlockSpec(memory_space=pltpu.SMEM)
    wide = pl.BlockSpec((R, 512), lambda n: (n, 0))
    sq = pl.BlockSpec((DN_HEADS, R, CPAD), lambda n: (0, n, 0))
    narrow = pl.BlockSpec((R, 128), lambda n: (n, 0))
    f = lambda *shp: jax.ShapeDtypeStruct(shp, F32)
    b = lambda *shp: jax.ShapeDtypeStruct(shp, BF16)
    return pl.pallas_call(
        body, grid=(steps,), name="dn_chunk_fwd",
        in_specs=[smem, smem, pl.BlockSpec((R, 1536), lambda n: (n, 0)), pl.BlockSpec((R, 128), lambda n: (n, BLK_G))],
        out_specs=[wide, wide, wide, wide, sq, sq, narrow],
        out_shape=[b(S, 512), f(S, 512), b(S, 512), b(S, 512), b(DN_HEADS, S, CPAD), f(DN_HEADS, S, CPAD), f(S, 128)],
        compiler_params=_params(("parallel",)),
    )(a_log, dt_bias, qkv, pg)


def _gated_norm(o, z, gn):
    r, oh = _rms_stats(o)
    return oh * gn * (z * _sigmoid(z))


def _dn_scan_fwd(w, u, qg, kd, a, gcs, pz, gn):
    S = w.shape[0]
    C = DN_CHUNK
    nc = S // C
    G = CHUNKS_SCAN
    R = G * C

    def body(w_ref, u_ref, qg_ref, kd_ref, a_ref, gcs_ref, z_ref, gn_ref, o_ref, vn_ref, sst_ref, out_ref, state):
        @pl.when(pl.program_id(0) == 0)
        def _():
            state[...] = jnp.zeros_like(state)

        def head_program(chunk, h):
            hs = _head(h)
            rows = slice(chunk * C, (chunk + 1) * C)
            s_in = state[h]
            sb = _bf(s_in)
            sst_ref[chunk, h] = sb
            w_s = _dot(w_ref[rows, hs], sb)
            q_s = _dot(qg_ref[rows, hs], sb)
            yield
            vn = u_ref[rows, hs] - w_s
            vnb = _bf(vn)
            o = q_s + _dot(a_ref[h, rows], _rows_pad(vnb))
            k_v = _dot_tn(kd_ref[rows, hs], vnb)
            yield
            state[h] = s_in * jnp.exp(gcs_ref[(chunk + 1) * C - 1:(chunk + 1) * C, h:h + 1]) + k_v
            o_ref[rows, hs] = o
            vn_ref[rows, hs] = vnb
            out_ref[rows, hs] = _bf(_gated_norm(o, z_ref[rows, hs], gn_ref[...]))

        for chunk in range(G):
            _interleave(head_program(chunk, h) for h in range(DN_HEADS))

    wide = pl.BlockSpec((R, 512), lambda n: (n, 0))
    f = lambda *shp: jax.ShapeDtypeStruct(shp, F32)
    b = lambda *shp: jax.ShapeDtypeStruct(shp, BF16)
    return pl.pallas_call(
        body, grid=(nc // G,), name="dn_scan_fwd",
        in_specs=[wide, wide, wide, wide, pl.BlockSpec((DN_HEADS, R, CPAD), lambda n: (0, n, 0)),
                  pl.BlockSpec((R, 128), lambda n: (n, 0)), pl.BlockSpec((R, 512), lambda n: (n, BLK_Z)),
                  pl.BlockSpec((1, DN_DIM), lambda n: (0, 0))],
        out_specs=[wide, wide, pl.BlockSpec((G, DN_HEADS, DN_DIM, DN_DIM), lambda n: (n, 0, 0, 0)), wide],
        out_shape=[f(S, 512), b(S, 512), b(nc, DN_HEADS, DN_DIM, DN_DIM), b(S, 512)],
        scratch_shapes=[pltpu.VMEM((DN_HEADS, DN_DIM, DN_DIM), F32)],
        compiler_params=_params(("arbitrary",)),
    )(w, u, qg, kd, a, gcs, pz, gn)


def _dn_scan_bwd(dcat, o, pz, gn, sst, vnew, w, qg, kd, a, gcs, dproj):
    S = o.shape[0]
    C = DN_CHUNK
    G = CHUNKS_SCAN
    R = G * C
    steps = S // R

    def body(dy_ref, o_ref, z_ref, gn_ref, sst_ref, vn_ref, w_ref, qg_ref, kd_ref, a_ref, gcs_ref, _,
             du_ref, dw_ref, dqg_ref, dkd_ref, da_ref, dz_ref, dsc_ref, dgn_ref, dstate):
        @pl.when(pl.program_id(0) == 0)
        def _():
            dstate[...] = jnp.zeros_like(dstate)
            dgn_ref[...] = jnp.zeros_like(dgn_ref)

        gn_ = gn_ref[...]
        lane = lax.broadcasted_iota(jnp.int32, (C, 128), 1)
        row = lax.broadcasted_iota(jnp.int32, (C, 128), 0)
        dgn_parts = []

        def head_program(chunk, h, dsc_parts):
            hs = _head(h)
            rows = slice(chunk * C, (chunk + 1) * C)
            ov, z, dout = o_ref[rows, hs], z_ref[rows, hs], dy_ref[rows, hs]
            r, oh = _rms_stats(ov)
            sg = _sigmoid(z)
            don = dout * (z * sg)
            dz_ref[rows, hs] = _bf(dout * (oh * gn_) * (sg * (1.0 + z * (1.0 - sg))))
            dgn_parts.append(jnp.sum(don * oh, axis=0, keepdims=True))
            dn = don * gn_
            do = _bf(r * (dn - oh * jnp.mean(dn * oh, axis=-1, keepdims=True)))
            sb = sst_ref[chunk, h]
            s_in = sb.astype(F32)
            ds_out = dstate[h]
            dsb = _bf(ds_out)
            vnb = vn_ref[rows, hs]
            wb, qgb, kdb, ab = w_ref[rows, hs], qg_ref[rows, hs], kd_ref[rows, hs], a_ref[h, rows]
            dvn = _dot_tn(ab, do)[:C] + _dot(kdb, dsb)
            yield
            da_ref[h, rows] = _dot_nt(do, _rows_pad(vnb))
            dqg_ref[rows, hs] = _dot_nt(do, sb)
            dkd_ref[rows, hs] = _dot_nt(vnb, dsb)
            q_do = _dot_tn(qgb, do)
            yield
            dvnb = _bf(dvn)
            dw_ref[rows, hs] = _bf(-_dot_nt(dvnb, sb))
            w_dvn = _dot_tn(wb, dvnb)
            du_ref[rows, hs] = dvnb
            yield
            d_last = jnp.exp(gcs_ref[(chunk + 1) * C - 1:(chunk + 1) * C, h:h + 1])
            dd = jnp.sum(jnp.sum(ds_out * s_in, axis=1, keepdims=True), axis=0, keepdims=True)
            dsc_parts.append(jnp.where((lane == h) & (row == C - 1), dd * d_last, 0.0))
            dstate[h] = ds_out * d_last + q_do - w_dvn

        for chunk in reversed(range(G)):
            dsc_parts = []
            _interleave(head_program(chunk, h, dsc_parts) for h in range(DN_HEADS))
            dsc_ref[chunk * C:(chunk + 1) * C, :] = sum(dsc_parts[1:], dsc_parts[0])
        dgn_ref[...] += sum(dgn_parts[1:], dgn_parts[0])

    rev = lambda n: steps - 1 - n
    wide = pl.BlockSpec((R, 512), lambda n: (rev(n), 0))
    z_spec = pl.BlockSpec((R, 512), lambda n: (rev(n), BLK_Z))
    sq = pl.BlockSpec((DN_HEADS, R, CPAD), lambda n: (0, rev(n), 0))
    narrow = pl.BlockSpec((R, 128), lambda n: (rev(n), 0))
    gn_spec = pl.BlockSpec((1, DN_DIM), lambda n: (0, 0))
    f = lambda *shp: jax.ShapeDtypeStruct(shp, F32)
    b = lambda *shp: jax.ShapeDtypeStruct(shp, BF16)
    return pl.pallas_call(
        body, grid=(steps,), name="dn_scan_bwd",
        in_specs=[pl.BlockSpec((R, 512), lambda n: (rev(n), 1)), wide, z_spec, gn_spec,
                  pl.BlockSpec((G, DN_HEADS, DN_DIM, DN_DIM), lambda n: (rev(n), 0, 0, 0)),
                  wide, wide, wide, wide, sq, narrow, pl.BlockSpec(memory_space=pl.ANY)],
        out_specs=[wide, wide, wide, wide, sq, z_spec, narrow, gn_spec],
        out_shape=[b(S, 512), b(S, 512), f(S, 512), f(S, 512), f(DN_HEADS, S, CPAD),
                   jax.ShapeDtypeStruct(dproj.shape, dproj.dtype), f(S, 128), f(1, DN_DIM)],
        scratch_shapes=[pltpu.VMEM((DN_HEADS, DN_DIM, DN_DIM), F32)],
        input_output_aliases={11: 5},
        compiler_params=_params(("arbitrary",)),
    )(dcat, o, pz, gn, sst, vnew, w, qg, kd, a, gcs, dproj)


def _dn_chunk_bwd(qkv, pg, t_inv, gcs, du, dw, dqg, dkd, da, dsc, a_log, dt_bias, dproj):
    S = qkv.shape[0]
    C = DN_CHUNK
    G = CHUNKS_LOCAL
    R = G * C

    def body(alog_ref, dtb_ref, qkv_ref, pg_ref, t_ref, gcs_ref, du_ref, dw_ref, dqg_ref, dkd_ref, da_ref, dsc_ref, _,
             dqkv_ref, dpg_ref, acc_ref):
        @pl.when(pl.program_id(0) == 0)
        def _():
            acc_ref[...] = jnp.zeros_like(acc_ref)

        ii, jj = _chunk_masks()
        lane = lax.broadcasted_iota(jnp.int32, (1, 128), 1)
        row8 = lax.broadcasted_iota(jnp.int32, (8, 128), 0)
        lane8 = lax.broadcasted_iota(jnp.int32, (8, 128), 1)
        rowc = lax.broadcasted_iota(jnp.int32, (C, 1), 0)
        tril, strict = jj <= ii, jj < ii
        dpg_parts, acc_parts = [[] for _ in range(G)], []

        def head_program(chunk, h):
            rows = slice(chunk * C, (chunk + 1) * C)
            q, k, v = qkv_ref[rows, _head(h)], qkv_ref[rows, _head(DN_HEADS + h)], qkv_ref[rows, _head(2 * DN_HEADS + h)]
            gc_col, beta, g_col = gcs_ref[rows, h:h + 1], gcs_ref[rows, DN_HEADS + h:DN_HEADS + h + 1], \
                gcs_ref[rows, 2 * DN_HEADS + h:2 * DN_HEADS + h + 1]
            dec = _decay(gc_col, ii, jj)
            eg = jnp.exp(gc_col)
            g_last = gc_col[C - 1:C, :]
            ek = jnp.exp(g_last - gc_col)
            kb, vb = k * beta, v * beta
            kbg = kb * eg
            qb, kbb = _bf(q), _bf(kb)
            k_rows = _rows_pad(_bf(k))
            t = t_ref[h, rows]
            tb = _bf(t)
            dub, dwb = du_ref[rows, _head(h)], dw_ref[rows, _head(h)]
            dqg_, dkd_ = dqg_ref[rows, _head(h)], dkd_ref[rows, _head(h)]
            dt = _dot_nt(dub, _rows_pad(_bf(vb))) + _dot_nt(dwb, _rows_pad(_bf(kbg)))
            t_du_dw = _dot_tn(tb, jnp.concatenate([dub, dwb], axis=1))
            dvb, dkbg = t_du_dw[:C, :DN_DIM], t_du_dw[:C, DN_DIM:]
            kk = _dot_nt(kbb, k_rows)
            qk = _dot_nt(qb, k_rows)
            yield
            dt_t = _dot3_nt(dt, t)
            yield
            dl = -_dot3_tn(t, dt_t)
            yield
            dm = jnp.where(strict, dl * dec, 0.0)
            dqk = jnp.where(tril, da_ref[h, rows] * dec, 0.0)
            gmat = dm * kk + dqk * qk
            dgc = jnp.sum(gmat, axis=1, keepdims=True) - _row_to_col(jnp.sum(gmat, axis=0, keepdims=True), ii, jj)
            dmb, dqkb = _bf(dm), _bf(dqk)
            yield
            dkb = _dot(dmb, k_rows) + dkbg * eg
            dk = _dot_tn(jnp.concatenate([dmb, dqkb], axis=0), jnp.concatenate([kbb, qb], axis=0))[:C] + dkd_ * ek
            dq = _dot(dqkb, k_rows) + dqg_ * eg
            yield
            tk = jnp.sum(dkd_ * k * ek, axis=1, keepdims=True)
            dgc = dgc + jnp.sum(dqg_ * q * eg, axis=1, keepdims=True) - tk + jnp.sum(dkbg * kbg, axis=1, keepdims=True)
            dgl = jnp.sum(tk, axis=0, keepdims=True) + dsc_ref[(chunk + 1) * C - 1:(chunk + 1) * C, h:h + 1]
            dgc = dgc + jnp.where(rowc == C - 1, dgl, 0.0)
            yield
            dk = dk + dkb * beta
            dbeta = jnp.sum(dkb * k, axis=1, keepdims=True) + jnp.sum(dvb * v, axis=1, keepdims=True)
            dqkv_ref[rows, _head(h)] = dq
            dqkv_ref[rows, _head(DN_HEADS + h)] = dk
            dqkv_ref[rows, _head(2 * DN_HEADS + h)] = dvb * beta
            dg_col = jnp.sum(jnp.where(jj >= ii, _col_to_row(dgc, ii, jj), 0.0), axis=1, keepdims=True)
            yield
            db = dbeta * beta * (1.0 - beta)
            da_in = dg_col * (-jnp.exp(alog_ref[h])) * _sigmoid(pg_ref[rows, DN_HEADS + h:DN_HEADS + h + 1] + dtb_ref[h])
            dpg_parts[chunk].append(jnp.where(lane == h, db, 0.0) + jnp.where(lane == DN_HEADS + h, da_in, 0.0))
            acc_parts.append(jnp.where((row8 == 0) & (lane8 == h), jnp.sum(dg_col * g_col, axis=0, keepdims=True), 0.0)
                             + jnp.where((row8 == 1) & (lane8 == h), jnp.sum(da_in, axis=0, keepdims=True), 0.0))

        _interleave(head_program(chunk, h) for chunk in range(G) for h in range(DN_HEADS))
        for chunk in range(G):
            dpg = sum(dpg_parts[chunk][1:], dpg_parts[chunk][0])
            dpg_ref[chunk * C:(chunk + 1) * C, :] = _bf(jnp.concatenate([dpg, jnp.zeros_like(dpg)], axis=1))
        acc_ref[...] += sum(acc_parts[1:], acc_parts[0])

    smem = pl.BlockSpec(memory_space=pltpu.SMEM)
    wide = pl.BlockSpec((R, 512), lambda n: (n, 0))
    sq = pl.BlockSpec((DN_HEADS, R, CPAD), lambda n: (0, n, 0))
    narrow = pl.BlockSpec((R, 128), lambda n: (n, 0))
    qkv_spec = pl.BlockSpec((R, 1536), lambda n: (n, 0))
    f = lambda *shp: jax.ShapeDtypeStruct(shp, F32)
    return pl.pallas_call(
        body, grid=(S // R,), name="dn_chunk_bwd",
        in_specs=[smem, smem, qkv_spec, pl.BlockSpec((R, 128), lambda n: (n, BLK_G)), sq, narrow, wide, wide, wide, wide, sq,
                  narrow, pl.BlockSpec(memory_space=pl.ANY)],
        out_specs=[qkv_spec, pl.BlockSpec((R, 256), lambda n: (n, BLK_G_PAD)), pl.BlockSpec((8, 128), lambda n: (0, 0))],
        out_shape=[f(S, 1536), jax.ShapeDtypeStruct(dproj.shape, dproj.dtype), f(8, 128)],
        input_output_aliases={12: 1},
        compiler_params=_params(("arbitrary",)),
    )(a_log, dt_bias, qkv, pg, t_inv, gcs, du, dw, dqg, dkd, da, dsc, dproj)


def _fill_kv(dk, dv, dproj):
    S = dk.shape[0]
    tm = min(512, S)

    def body(dk_ref, dv_ref, _, o_ref):
        o_ref[...] = _bf(jnp.concatenate([dk_ref[...], dv_ref[...]], axis=1))

    tile = pl.BlockSpec((tm, 128), lambda i: (i, 0))
    return pl.pallas_call(
        body, grid=(S // tm,), name="fill_kv",
        in_specs=[tile, tile, pl.BlockSpec(memory_space=pl.ANY)],
        out_specs=pl.BlockSpec((tm, 256), lambda i: (i, BLK_KV)),
        out_shape=jax.ShapeDtypeStruct(dproj.shape, dproj.dtype),
        input_output_aliases={2: 0},
        compiler_params=_params(("parallel",)),
    )(dk, dv, dproj)


def _w_in_to_internal(wt):
    return jnp.concatenate([wt[0:512], wt[2304:2816], wt[768:2304], wt[512:768], wt[2816:2824],
                            jnp.zeros((D_IN_PAD - D_IN, wt.shape[1]), wt.dtype)], axis=0)


def _w_in_from_internal(gt):
    return jnp.concatenate([gt[0:512], gt[2560:2816], gt[1024:2560], gt[512:1024], gt[2816:2824]], axis=0)


def _local_step(x, p, target, wts, first_weights, other_weights, ship_early):
    S = x.shape[0]
    cos, sin = _rope_tables(S)
    sinks, a_log, dt_bias = wts["sinks"].reshape(8), wts["a_log"].reshape(4), wts["dt_bias"].reshape(4)
    gn = wts["dn_norm"].reshape(1, DN_DIM)
    add = lambda acc, res: (acc + res,)

    u = _rmsnorm_fwd(x, wts["norm_mix"], "norm_mix_fwd")
    w_in_t, conv_w, token = first_weights(u)
    proj, = _mm(u, w_in_t, form="nt", name="in_proj", out_dtypes=[F32], tn=512, after=token)
    attn, lse = _attn_fwd(proj, cos, sin, sinks)
    qkv = _dn_prep_fwd(proj, conv_w)
    cw, cu, cqg, ckd, ca, ct, gcs = _dn_chunk_fwd(qkv, proj, a_log, dt_bias)
    o, vnew, sst, dn_out = _dn_scan_fwd(cw, cu, cqg, ckd, ca, gcs, proj, gn)
    w_o, = other_weights(("w_o",), dn_out)
    h1, = _mm([attn, dn_out], w_o, form="nn", name="out_proj", out_dtypes=[F32], tn=512, epi=add, extra=[x])

    def relu2(acc):
        r = jnp.maximum(acc, 0.0)
        return r * r, r

    w_up, = other_weights(("w_up",), h1)
    hid, relu, m = _mm(h1, w_up, form="nn", name="mlp_up", out_dtypes=[BF16, BF16], tn=512, epi=relu2, norm=wts["norm_mlp"])
    w_down, = other_weights(("w_down",), hid)
    h2, = _mm(hid, w_down, form="nn", name="mlp_down", out_dtypes=[F32], tn=512, epi=add, extra=[h1])
    w_pg, w_pp = other_weights(("w_ple_gate", "w_ple_proj"), h2)
    n3, dh2, dgl, dpp, loss, d_norm_final, d_norm_ple = _ple_and_loss(h2, p, target, w_pg, w_pp, wts["norm_ple"],
                                                                     wts["norm_final"].reshape(1, D_MODEL))
    g = {"norm_final": d_norm_final, "norm_ple": d_norm_ple}
    early = {"w_ple_gate": _mm_tn(n3, dgl, name="d_w_ple_gate", tm=512, tn=1024, out_dtype=BF16).reshape(N_DEV, 128, 1024),
             "w_ple_proj": _mm_tn(p, dpp, name="d_w_ple_proj", tm=256, tn=128, out_dtype=BF16, column_shards=True)}
    d_act, = _mm(dh2, w_down, form="nt", name="d_hidden", out_dtypes=[BF16], tn=512,
                 epi=lambda acc, r: (acc * (2.0 * r.astype(F32)),), extra=[relu])
    early["w_down"] = _mm_tn(hid, dh2, name="d_w_down", tm=512, tn=1024, out_dtype=BF16).reshape(N_DEV, 512, 1024)
    early["w_up"] = _mm_tn(m, d_act, name="d_w_up", tm=1024, tn=512, out_dtype=BF16, column_shards=True)
    token = ship_early(early)
    dh1, g["norm_mlp"], dcat = _mm(d_act, w_up, form="nt", name="d_m", out_dtypes=[F32], tn=512, after=token,
                                   norm_bwd=(h1, wts["norm_mlp"], dh2), then_nt=w_o)
    d_w_o = jnp.concatenate([_mm_tn(attn, dh1, name="d_w_o_attn", tm=512, tn=512, out_dtype=BF16),
                             _mm_tn(dn_out, dh1, name="d_w_o_dn", tm=512, tn=512, out_dtype=BF16)], axis=0)
    token = ship_early({"w_o": d_w_o.reshape(N_DEV, 128, 1024)})
    dproj, dk, dv, dsinks = _attn_bwd(proj, cos, sin, sinks + token[0, 0], dcat, attn, lse)
    g["sinks"] = dsinks[:, 0].reshape(1, 8)
    du_, dw_, dqg, dkd, da, dproj, dsc, g["dn_norm"] = _dn_scan_bwd(dcat, o, proj, gn, sst, vnew, cw, cqg, ckd, ca, gcs, dproj)
    dqkv, dproj, gate_acc = _dn_chunk_bwd(qkv, proj, ct, gcs, du_, dw_, dqg, dkd, da, dsc, a_log, dt_bias, dproj)
    g["a_log"], g["dt_bias"] = gate_acc[0:1, 0:4], gate_acc[1:2, 0:4]
    dproj, g["conv_w"] = _dn_prep_bwd(proj, conv_w, dqkv, dproj)
    dproj = _fill_kv(dk, dv, dproj)
    token = ship_early({"w_in": _mm_tn(dproj, u, name="d_w_in", tm=512, tn=1024, out_dtype=BF16)})
    grad_x, g["norm_mix"] = _mm(dproj, w_in_t, form="nn", name="d_u", out_dtypes=[F32], tn=512, after=token,
                                norm_bwd=(x, wts["norm_mix"], dh1))
    return loss, grad_x, g


def _peer(k):
    x, y, c = lax.axis_index("x"), lax.axis_index("y"), lax.axis_index("c")
    px = 1 - x if k & 4 else x
    py = 1 - y if k & 2 else y
    pc = 1 - c if k & 1 else c
    return (px, py, pc), 4 * px + 2 * py + pc


def _exchange(srcs, name, gather):
    n = len(srcs)
    gathers = list(gather) if isinstance(gather, (list, tuple)) else [gather] * n
    shapes = [(N_DEV,) + s.shape if gt else s.shape for s, gt in zip(srcs, gathers)]

    def body(*refs):
        src_refs, out_refs = refs[:n], refs[n:2 * n]
        send_sems, recv_sems, local_sems = refs[2 * n:]
        _, me = _peer(0)
        piece = lambda a, d: src_refs[a] if gathers[a] else src_refs[a].at[d]
        local = [pltpu.make_async_copy(piece(a, me), out_refs[a].at[me], local_sems.at[a]) for a in range(n)]
        for cp in local:
            cp.start()
        copies = []
        for a in range(n):
            for k in range(1, N_DEV):
                dev, idx = _peer(k)
                cp = pltpu.make_async_remote_copy(src_ref=piece(a, idx), dst_ref=out_refs[a].at[me],
                                                  send_sem=send_sems.at[a, k - 1], recv_sem=recv_sems.at[a, k - 1],
                                                  device_id=dev, device_id_type=MESH)
                cp.start()
                copies.append(cp)
        for cp in copies:
            cp.wait_recv()
        for cp in copies:
            cp.wait_send()
        for cp in local:
            cp.wait()

    anywhere = pl.BlockSpec(memory_space=pl.ANY)
    return pl.pallas_call(
        body, name=name, in_specs=[anywhere] * n, out_specs=[anywhere] * n,
        out_shape=[jax.ShapeDtypeStruct(shp, s.dtype) for shp, s in zip(shapes, srcs)],
        scratch_shapes=[pltpu.SemaphoreType.DMA((n, N_DEV - 1)), pltpu.SemaphoreType.DMA((n, N_DEV - 1)),
                        pltpu.SemaphoreType.DMA((n,))],
    )(*srcs)


_HBM = pl.BlockSpec(memory_space=pltpu.HBM)
_SEM = pl.BlockSpec(memory_space=pltpu.SEMAPHORE)
_EFFECT = pltpu.SideEffectType.DATAFLOW_SIDE_EFFECTING


def _split_copies(src_refs, land_refs, send_sems, recv_sems, modes, which=None):
    _, me = _peer(0)
    copies = []
    which = range(len(src_refs)) if which is None else which
    for a, src, land in zip(which, src_refs, land_refs):
        if modes[a] == "columns":
            n_cols = src.shape[1]
            dst = land.at[:, pl.ds(pl.multiple_of(me * n_cols, n_cols), n_cols)]
        else:
            dst = land.at[me]
        for k in range(1, N_DEV):
            dev, idx = _peer(k)
            sem = a * (N_DEV - 1) + k - 1
            copies.append(pltpu.make_async_remote_copy(
                src_ref=src.at[idx] if modes[a] == "pieces" else src, dst_ref=dst, send_sem=send_sems.at[sem],
                recv_sem=recv_sems.at[sem], device_id=dev, device_id_type=MESH))
    return copies


def _exchange_start(srcs, name, modes):
    n = len(srcs)
    modes = [modes] * n if isinstance(modes, str) else list(modes)
    me = 4 * lax.axis_index("x") + 2 * lax.axis_index("y") + lax.axis_index("c")
    lands = []
    for s, mode in zip(srcs, modes):
        if mode == "columns":
            empty = lax.empty((s.shape[0], N_DEV * s.shape[1]), s.dtype)
            lands.append(lax.dynamic_update_slice(empty, s, (0, me * s.shape[1])))
        else:
            own = s if mode == "slots" else lax.dynamic_index_in_dim(s, me, 0, keepdims=False)
            shape = (N_DEV,) + s.shape if mode == "slots" else s.shape
            lands.append(lax.dynamic_update_index_in_dim(lax.empty(shape, s.dtype), own, me, 0))

    def body(*refs):
        src_refs, land_refs = refs[:n], refs[n:2 * n]
        send_sems, recv_sems = refs[2 * n], refs[2 * n + 1]
        for cp in _split_copies(src_refs, land_refs, send_sems, recv_sems, modes):
            cp.start()
        refs[-1][...] = jnp.zeros_like(refs[-1])

    both = list(srcs) + lands
    sems = pltpu.SemaphoreType.DMA((n * (N_DEV - 1),))
    out = pl.pallas_call(
        body, name=name,
        out_shape=(sems, sems, *[pltpu.HBM(t.shape, t.dtype) for t in both], jax.ShapeDtypeStruct((8, 128), F32)),
        in_specs=[_HBM] * (2 * n), out_specs=(_SEM, _SEM, *[_HBM] * (2 * n), pl.BlockSpec(memory_space=pltpu.VMEM)),
        input_output_aliases={i: 2 + i for i in range(2 * n)},
        compiler_params=pltpu.CompilerParams(has_side_effects=_EFFECT),
    )(*[pltpu.with_memory_space_constraint(t, pltpu.HBM) for t in both])
    return (n, modes, out[:-1]), out[-1]


def _exchange_wait(handle, after, name, which=None):
    n_all, modes, (send_sems, recv_sems, *both_all) = handle
    which = list(range(n_all)) if which is None else list(which)
    n = len(which)
    both = [both_all[a] for a in which] + [both_all[n_all + a] for a in which]

    def body(*refs):
        src_refs, land_refs = refs[:n], refs[n:2 * n]
        for cp in _split_copies(src_refs, land_refs, refs[2 * n], refs[2 * n + 1], modes, which):
            cp.wait_send()
            cp.wait_recv()

    out = pl.pallas_call(
        body, name=name, out_shape=tuple(pltpu.HBM(t.shape, t.dtype) for t in both),
        in_specs=[_HBM] * (2 * n) + [_SEM, _SEM, pl.BlockSpec(memory_space=pl.ANY)], out_specs=tuple([_HBM] * (2 * n)),
        input_output_aliases={i: i for i in range(2 * n)},
        compiler_params=pltpu.CompilerParams(has_side_effects=_EFFECT),
    )(*both, send_sems, recv_sems, after)
    return list(out[n:])


def _adam_update(g, w, m, v):
    nm = ADAM_B1 * m + (1.0 - ADAM_B1) * g
    nv = ADAM_B2 * v + (1.0 - ADAM_B2) * (g * g)
    m_hat = nm / (1.0 - ADAM_B1 ** ADAM_STEP)
    v_hat = nv / (1.0 - ADAM_B2 ** ADAM_STEP)
    return -ADAM_LR * (m_hat / (jnp.sqrt(v_hat) + ADAM_EPS) + ADAM_WD * w), nm, nv


def _adamw(parts, w, m, v, name):
    n, R, W = parts.shape
    tm = 128 if R % 128 == 0 else R

    def body(p_ref, w_ref, m_ref, v_ref, g_ref, d_ref, nm_ref, nv_ref):
        g = p_ref[0].astype(F32)
        for s in range(1, n):
            g = g + p_ref[s].astype(F32)
        g_ref[...] = g
        d_ref[...], nm_ref[...], nv_ref[...] = _adam_update(g, w_ref[...], m_ref[...], v_ref[...])

    tile = pl.BlockSpec((tm, W), lambda i: (i, 0))
    return pl.pallas_call(
        body, grid=(R // tm,), name=name,
        in_specs=[pl.BlockSpec((n, tm, W), lambda i: (0, i, 0)), tile, tile, tile],
        out_specs=[tile] * 4, out_shape=[jax.ShapeDtypeStruct((R, W), F32)] * 4,
        compiler_params=_params(("parallel",)),
    )(parts, w, m, v)


_MATRICES = ("w_in", "w_o", "w_up", "w_down", "w_ple_gate", "w_ple_proj")


_OTHERS = ("w_o", "w_up", "w_down", "w_ple_gate", "w_ple_proj")
_OTHER_MODES = {"w_o": "slots", "w_up": "slots", "w_down": "slots", "w_ple_gate": "slots", "w_ple_proj": "columns"}


_VECTORS = ("norm_mix", "norm_mlp", "norm_ple", "norm_final", "a_log", "dt_bias", "sinks", "dn_norm")
_SMALL_ROWS, _LOSS_ROW, _CONV_ROW = 16, 8, 9


def _pack_small(vectors, loss, conv):
    def body(*refs):
        out = refs[-1]
        out[...] = jnp.zeros_like(out)
        for r, ref in enumerate(refs[:len(_VECTORS)]):
            out[r:r + 1, 0:ref.shape[1]] = ref[...]
        out[_LOSS_ROW:_LOSS_ROW + 1, 0:128] = refs[len(_VECTORS)][...]
        out[_CONV_ROW:_CONV_ROW + 6, :] = refs[len(_VECTORS) + 1][...]

    return pl.pallas_call(body, name="pack_small", out_shape=jax.ShapeDtypeStruct((_SMALL_ROWS, 1024), F32))(*vectors, loss, conv)


def _sum_slots(parts):
    def body(p_ref, o_ref):
        acc = p_ref[0]
        for s in range(1, parts.shape[0]):
            acc = acc + p_ref[s]
        o_ref[...] = acc

    return pl.pallas_call(body, name="sum_small", out_shape=jax.ShapeDtypeStruct(parts.shape[1:], parts.dtype))(parts)


def _adamw_vectors(summed, conv_g, wmv):
    names = _VECTORS + ("conv_w",)
    flat = [a for triple in wmv for a in triple]

    def body(*refs):
        sum_ref, conv_ref = refs[0], refs[1]
        ins, outs = refs[2:2 + len(flat)], refs[2 + len(flat):]
        for i in range(len(names)):
            w_ref, m_ref, v_ref = ins[3 * i:3 * i + 3]
            g = conv_ref[...] if i == len(_VECTORS) else sum_ref[i:i + 1, 0:w_ref.shape[1]]
            outs[4 * i][...] = g
            outs[4 * i + 1][...], outs[4 * i + 2][...], outs[4 * i + 3][...] = _adam_update(g, w_ref[...], m_ref[...], v_ref[...])

    out_shape = [jax.ShapeDtypeStruct(t[0].shape, F32) for t in wmv for _ in range(4)]
    res = pl.pallas_call(body, name="adamw_vectors", out_shape=out_shape)(summed, conv_g, *flat)
    return {n: res[4 * i:4 * i + 4] for i, n in enumerate(names)}


_ORDER = ("norm_mix", "w_in", "conv_w", "a_log", "dt_bias", "dn_norm", "sinks", "w_o", "norm_mlp", "w_up", "w_down",
          "norm_ple", "w_ple_gate", "w_ple_proj", "norm_final")


def kernel(x, p, norm_mix, w_in, conv_w, a_log, dt_bias, dn_norm, sinks, w_o, norm_mlp, w_up, w_down, norm_ple, w_ple_gate, w_ple_proj, norm_final, loss_target, m_norm_mix, m_w_in, m_conv_w, m_a_log, m_dt_bias, m_dn_norm, m_sinks, m_w_o, m_norm_mlp, m_w_up, m_w_down, m_norm_ple, m_w_ple_gate, m_w_ple_proj, m_norm_final, v_norm_mix, v_w_in, v_conv_w, v_a_log, v_dt_bias, v_dn_norm, v_sinks, v_w_o, v_norm_mlp, v_w_up, v_w_down, v_norm_ple, v_w_ple_gate, v_w_ple_proj, v_norm_final):
    w = dict(norm_mix=norm_mix, w_in=w_in[0], conv_w=conv_w[0], a_log=a_log, dt_bias=dt_bias, dn_norm=dn_norm, sinks=sinks,
             w_o=w_o[0], norm_mlp=norm_mlp, w_up=w_up[0], w_down=w_down[0], norm_ple=norm_ple, w_ple_gate=w_ple_gate[0],
             w_ple_proj=w_ple_proj[0], norm_final=norm_final)
    m = dict(norm_mix=m_norm_mix, w_in=m_w_in[0], conv_w=m_conv_w[0], a_log=m_a_log, dt_bias=m_dt_bias, dn_norm=m_dn_norm,
             sinks=m_sinks, w_o=m_w_o[0], norm_mlp=m_norm_mlp, w_up=m_w_up[0], w_down=m_w_down[0], norm_ple=m_norm_ple,
             w_ple_gate=m_w_ple_gate[0], w_ple_proj=m_w_ple_proj[0], norm_final=m_norm_final)
    v = dict(norm_mix=v_norm_mix, w_in=v_w_in[0], conv_w=v_conv_w[0], a_log=v_a_log, dt_bias=v_dt_bias, dn_norm=v_dn_norm,
             sinks=v_sinks, w_o=v_w_o[0], norm_mlp=v_norm_mlp, w_up=v_w_up[0], w_down=v_w_down[0], norm_ple=v_norm_ple,
             w_ple_gate=v_w_ple_gate[0], w_ple_proj=v_w_ple_proj[0], norm_final=v_norm_final)
    me = 4 * lax.axis_index("x") + 2 * lax.axis_index("y") + lax.axis_index("c")
    conv_shard = conv_w.shape[2]

    for d in (w, m, v):
        d["w_in"] = d["w_in"].T
    conv_pad = jnp.pad(w["conv_w"], ((0, 8 - DN_CONV), (0, 256 - conv_shard)))
    first, token_first = _exchange_start([_bf(w["w_in"]), conv_pad], "gather_first_start", "slots")
    vectors = dict(w)
    vectors["norm_mix"] = w["norm_mix"] + token_first[0:1, 0:1]
    started = {}

    def first_weights(after):
        w_in_all, conv_all = _exchange_wait(first, after, "gather_first_wait")
        later = [_bf(w[n]) for n in _OTHERS]
        later[-1] = _bf(w["w_ple_proj"] + 0.0 * conv_all[0, 0:1, 0:1])
        started["others"], token_others = _exchange_start(later, "gather_others_start", [_OTHER_MODES[n] for n in _OTHERS])
        conv_all = jnp.transpose(conv_all[:, :DN_CONV, :conv_shard], (1, 0, 2)).reshape(DN_CONV, N_DEV * conv_shard)
        return _w_in_to_internal(w_in_all.reshape(D_IN, D_MODEL)), conv_all, token_others

    as_taken = {"w_o": lambda t: t.reshape(1024, 1024), "w_up": lambda t: t, "w_down": lambda t: t.reshape(4096, 1024),
                "w_ple_gate": lambda t: t.reshape(1024, 1024), "w_ple_proj": lambda t: t}

    def other_weights(names, after):
        which = [_OTHERS.index(n) for n in names]
        got = _exchange_wait(started["others"], after, "gather_wait_" + names[0], which)
        return [as_taken[n](t) for n, t in zip(names, got)]

    shipped = []

    def ship_early(pieces):
        names = tuple(pieces)
        if names == ("w_in",):
            pieces = {"w_in": _w_in_from_internal(pieces["w_in"]).reshape(N_DEV, D_IN // N_DEV, D_MODEL)}
        handle, token = _exchange_start([pieces[n] for n in names], "scatter_start_" + names[0], "pieces")
        shipped.append((names, handle))
        return token

    loss, grad_x, g = _local_step(x[0], p[0, 0], loss_target[0], vectors, first_weights, other_weights, ship_early)

    row = lambda t: t.reshape(1, t.size)
    small = _pack_small([row(g[n]) for n in _VECTORS], loss, g["conv_w"].reshape(6, 1024))
    small_all, = _exchange([small], "gather_small", gather=True)
    summed = _sum_slots(small_all)
    conv_g = lax.dynamic_slice(summed[_CONV_ROW:_CONV_ROW + 6].reshape(DN_CONV, N_DEV * conv_shard), (0, me * conv_shard),
                               (DN_CONV, conv_shard))
    small_out = _adamw_vectors(summed, conv_g, [(row(w[n]), row(m[n]), row(v[n])) for n in _VECTORS]
                               + [(w["conv_w"], m["conv_w"], v["conv_w"])])
    big, after = {}, small_out["conv_w"][0]
    for names, handle in shipped:
        for n, r in zip(names, _exchange_wait(handle, after, "scatter_wait_" + names[0])):
            big[n] = _adamw(r, w[n], m[n], v[n], "adamw_" + n)
            after = big[n][1]

    result = [summed[_LOSS_ROW, 0], grad_x[None]]
    for i in range(4):
        for n in _ORDER:
            if n == "w_in":
                result.append(big[n][i].T[None])
            elif n in _MATRICES:
                result.append(big[n][i][None])
            elif n == "conv_w":
                result.append(small_out[n][i][None])
            else:
                result.append(small_out[n][i].reshape(w[n].shape))
    return tuple(result)
```

```python
import jax
import jax.numpy as jnp
import numpy as np
from jax import lax
from jax.experimental import pallas as pl
from jax.experimental.pallas import tpu as pltpu

F32, BF16 = jnp.float32, jnp.bfloat16
EPS = 1e-6
D_MODEL = 1024
N_DEV = 8
ATTN_BLOCK = 128
HEAD_PAIR = 128
DN_HEADS = 4
DN_DIM = 128
DN_CHUNK = 64
DN_CONV = 4
ROPE_THETA = 10000.0
D_IN = 2824
D_IN_PAD = 3072
BLK_Q, BLK_Z = 0, 1
BLK_DN, BLK_K, BLK_V, BLK_G = 8, 20, 21, 22
BLK_KV, BLK_G_PAD = 10, 11
VMEM_LIMIT = 56 * 1024 * 1024
NEG = -1e30
ADAM_LR, ADAM_B1, ADAM_B2, ADAM_EPS, ADAM_WD, ADAM_STEP = 0.001, 0.9, 0.999, 1e-08, 0.01, 10
MESH = pl.DeviceIdType.MESH


def _bf(x):
    return x.astype(BF16)


def _dot(a, b):
    return jnp.dot(a, b, preferred_element_type=F32)


def _dot_nt(a, b):
    return lax.dot_general(a, b, (((1,), (1,)), ((), ())), preferred_element_type=F32)


def _dot_tn(a, b):
    return lax.dot_general(a, b, (((0,), (0,)), ((), ())), preferred_element_type=F32)


def _sigmoid(x):
    return 1.0 / (1.0 + jnp.exp(-x))


def _params(sem):
    return pltpu.CompilerParams(dimension_semantics=sem, vmem_limit_bytes=VMEM_LIMIT)


def _mm(x, w, *, form, name, out_dtypes, tn, epi=None, extra=(), tm=512, w_row_block=0, after=None, norm=None,
        norm_bwd=None, then_nt=None):
    assert norm is None or norm_bwd is None
    xs = list(x) if isinstance(x, (list, tuple)) else [x]
    nx = len(xs)
    S, K = xs[0].shape
    shards = w.ndim == 3
    N = (w.shape[2] * N_DEV if shards else w.shape[1]) if form == "nn" else w.shape[-2]
    assert not (shards and form == "nn" and tn != w.shape[2]) and (nx == 1 or (form == "nn" and not shards and norm is None))
    r0 = w_row_block * K
    tm = min(tm, S)
    n_extra, n_out = len(extra), len(out_dtypes)
    tile = lambda width: pl.BlockSpec((tm, width), lambda i: (i, 0))
    whole = lambda a: pl.BlockSpec(a.shape, lambda i, nd=a.ndim: (0,) * nd)
    ins, in_specs = [*xs, w, *extra], [tile(K)] * nx + [whole(w)] + [tile(N)] * n_extra
    if norm is not None:
        ins, in_specs = ins + [norm], in_specs + [whole(norm)]
    if norm_bwd is not None:
        ins, in_specs = ins + list(norm_bwd), in_specs + [tile(N), whole(norm_bwd[1]), tile(N)]
    if then_nt is not None:
        ins, in_specs = ins + [then_nt], in_specs + [whole(then_nt)]
    if after is not None:
        ins, in_specs = ins + [after], in_specs + [whole(after)]
    out_shape = [jax.ShapeDtypeStruct((S, N), dt) for dt in out_dtypes]
    out_specs = [tile(N)] * n_out
    if norm is not None:
        out_shape, out_specs = out_shape + [jax.ShapeDtypeStruct((S, K), BF16)], out_specs + [tile(K)]
    if norm_bwd is not None:
        out_shape, out_specs = out_shape + [jax.ShapeDtypeStruct((1, N), F32)], out_specs + [pl.BlockSpec((1, N), lambda i: (0, 0))]
    if then_nt is not None:
        out_shape, out_specs = out_shape + [jax.ShapeDtypeStruct((S, then_nt.shape[0]), F32)], out_specs + [tile(then_nt.shape[0])]

    def product(xb, w_ref, cols, c):
        if form == "nn" and nx > 1:
            return sum(_dot(part, w_ref[r0 + p * K:r0 + (p + 1) * K, cols]) for p, part in enumerate(xb))
        if form == "nn":
            return _dot(xb, w_ref[c] if shards else w_ref[r0:r0 + K, cols])
        if not shards:
            return _dot_nt(xb, w_ref[cols, :])
        ks = w.shape[2]
        acc = _dot_nt(xb[:, 0:ks], w_ref[0, cols, :])
        for s in range(1, N_DEV):
            acc = acc + _dot_nt(xb[:, s * ks:(s + 1) * ks], w_ref[s, cols, :])
        return acc

    def body(*refs):
        x_ref, w_ref = refs[0], refs[nx]
        extra_refs = refs[nx + 1:nx + 1 + n_extra]
        at = nx + 1 + n_extra
        if norm is not None:
            gain_ref, at = refs[at], at + 1
        if norm_bwd is not None:
            (y_ref, ygain_ref, dres_ref), at = refs[at:at + 3], at + 3
        if then_nt is not None:
            w2_ref, at = refs[at], at + 1
        outs = refs[len(ins):]
        if norm is not None:
            _, xh = _rms_stats(x_ref[...])
            xb = _bf(xh * gain_ref[...])
            outs[n_out][...] = xb
        else:
            xb = _bf(x_ref[...]) if nx == 1 else [_bf(r[...]) for r in refs[:nx]]
        for c in range(N // tn):
            cols = slice(c * tn, (c + 1) * tn)
            acc = product(xb, w_ref, cols, c)
            res = epi(acc, *[r[:, cols] for r in extra_refs]) if epi else (acc,)
            for o, r in zip(outs[:n_out], res):
                o[:, cols] = r.astype(o.dtype)
        if norm_bwd is not None:
            dx, dg = _rms_bwd_tile(y_ref[...], ygain_ref[...], outs[0][...])
            outs[0][...] = dres_ref[...] + dx
            dg_ref = outs[n_out]

            @pl.when(pl.program_id(0) == 0)
            def _():
                dg_ref[...] = jnp.zeros_like(dg_ref)

            dg_ref[...] += dg
        if then_nt is not None:
            yb = _bf(outs[0][...])
            for c in range(then_nt.shape[0] // tn):
                cols = slice(c * tn, (c + 1) * tn)
                outs[-1][:, cols] = _dot_nt(yb, w2_ref[cols, :])

    return pl.pallas_call(
        body, grid=(S // tm,), name=name, in_specs=in_specs, out_specs=out_specs, out_shape=out_shape,
        compiler_params=_params(("arbitrary",) if norm_bwd is not None else ("parallel",)),
    )(*ins)


def _mlp_fwd(h1, w_up, w_down, gain):
    S, K = h1.shape
    n_sh, _, fs = w_up.shape
    tm = min(512, S)

    def body(x_ref, wup_ref, wdown_ref, g_ref, hid_ref, relu_ref, m_ref, h2_ref):
        x = x_ref[...]
        _, xh = _rms_stats(x)
        mb = _bf(xh * g_ref[...])
        m_ref[...] = mb
        h2_ref[...] = x
        for c in range(n_sh):
            cols = slice(c * fs, (c + 1) * fs)
            r = jnp.maximum(_dot(mb, wup_ref[c]), 0.0)
            hd = _bf(r * r)
            hid_ref[:, cols] = hd
            relu_ref[:, cols] = _bf(r)
            h2_ref[...] += _dot(hd, wdown_ref[cols, :])

    tile = lambda width: pl.BlockSpec((tm, width), lambda i: (i, 0))
    once = lambda a: pl.BlockSpec(a.shape, lambda i, nd=a.ndim: (0,) * nd, pipeline_mode=pl.Buffered(1))
    F = n_sh * fs
    return pl.pallas_call(
        body, grid=(S // tm,), name="mlp_fwd",
        in_specs=[tile(K), once(w_up), once(w_down), pl.BlockSpec(gain.shape, lambda i: (0, 0))],
        out_specs=[tile(F), tile(F), tile(K), tile(K)],
        out_shape=[jax.ShapeDtypeStruct((S, F), BF16), jax.ShapeDtypeStruct((S, F), BF16),
                   jax.ShapeDtypeStruct((S, K), BF16), jax.ShapeDtypeStruct((S, K), F32)],
        compiler_params=_params(("parallel",)),
    )(h1, w_up, w_down, gain)


def _mm_tn(x, dy, *, name, tm, tn, out_dtype=F32, column_shards=False, after=None):
    S, K = x.shape
    N = dy.shape[1]
    waits = [] if after is None else [after]

    def body(x_ref, dy_ref, *rest):
        rest[-1][...] = _dot_tn(_bf(x_ref[...]), _bf(dy_ref[...])).astype(out_dtype)

    if column_shards:
        out_spec = pl.BlockSpec((None, tm, tn), lambda i, j: (j, i, 0))
        out_shape = jax.ShapeDtypeStruct((N // tn, K, tn), out_dtype)
    else:
        out_spec = pl.BlockSpec((tm, tn), lambda i, j: (i, j))
        out_shape = jax.ShapeDtypeStruct((K, N), out_dtype)
    return pl.pallas_call(
        body, grid=(K // tm, N // tn), name=name,
        in_specs=[pl.BlockSpec((S, tm), lambda i, j: (0, i)), pl.BlockSpec((S, tn), lambda i, j: (0, j))]
        + [pl.BlockSpec(memory_space=pl.ANY)] * len(waits),
        out_specs=out_spec, out_shape=out_shape,
        compiler_params=_params(("parallel", "parallel")),
    )(x, dy, *waits)


def _rowwise(body, *, tiled, full, out_tiled, out_acc, name, tm=512, smem=()):
    S = tiled[0].shape[0]
    tm = min(tm, S)
    n_in = len(smem) + len(tiled) + len(full)

    def kern(*refs):
        @pl.when(pl.program_id(0) == 0)
        def _():
            for r in refs[n_in + len(out_tiled):]:
                r[...] = jnp.zeros_like(r)
        body(*refs)

    in_specs = [pl.BlockSpec(memory_space=pltpu.SMEM) for _ in smem]
    in_specs += [pl.BlockSpec((tm, a.shape[1]), lambda i: (i, 0)) for a in tiled]
    in_specs += [pl.BlockSpec(a.shape, lambda i, nd=a.ndim: (0,) * nd) for a in full]
    out_specs = [pl.BlockSpec((tm, w), lambda i: (i, 0)) for w, _ in out_tiled]
    out_specs += [pl.BlockSpec(shp, lambda i, nd=len(shp): (0,) * nd) for shp, _ in out_acc]
    out_shape = [jax.ShapeDtypeStruct((S, w), dt) for w, dt in out_tiled]
    out_shape += [jax.ShapeDtypeStruct(shp, dt) for shp, dt in out_acc]
    return pl.pallas_call(
        kern, grid=(S // tm,), name=name, in_specs=in_specs, out_specs=out_specs, out_shape=out_shape,
        compiler_params=_params(("arbitrary",)),
    )(*smem, *tiled, *full)


def _rms_stats(x):
    r = lax.rsqrt(jnp.mean(x * x, axis=-1, keepdims=True) + EPS)
    return r, x * r


def _rmsnorm_fwd(x, g, name):
    def body(x_ref, g_ref, o_ref):
        _, xh = _rms_stats(x_ref[...])
        o_ref[...] = _bf(xh * g_ref[...])

    return _rowwise(body, tiled=[x], full=[g], out_tiled=[(x.shape[1], BF16)], out_acc=[], name=name)[0]


def _rms_bwd_tile(x, g, dxn):
    r, xh = _rms_stats(x)
    dg = jnp.sum(dxn * xh, axis=0, keepdims=True)
    dn = dxn * g
    dx = r * (dn - xh * jnp.mean(dn * xh, axis=-1, keepdims=True))
    return dx, dg


def _ple_and_loss(h2, p, target, w_pg, w_pp, g_ple, g_final):
    S, n = h2.shape
    tm = min(512, S)
    tn = 512

    def body(h2_ref, p_ref, t_ref, wpg_ref, wpp_ref, gple_ref, gfin_ref,
             n3_ref, dh_ref, dgl_ref, dpp_ref, loss_ref, dg_ref, dgple_ref, pp, gate, h3):
        @pl.when(pl.program_id(0) == 0)
        def _():
            loss_ref[...] = jnp.zeros_like(loss_ref)
            dg_ref[...] = jnp.zeros_like(dg_ref)
            dgple_ref[...] = jnp.zeros_like(dgple_ref)

        x = h2_ref[...]
        _, xh = _rms_stats(x)
        n3 = _bf(xh * gple_ref[...])
        n3_ref[...] = n3
        pb = _bf(p_ref[...])
        for c in range(n // tn):
            cols = slice(c * tn, (c + 1) * tn)
            pp[:, cols] = _dot(pb, wpp_ref[:, cols])
            gt = _sigmoid(_dot(n3, wpg_ref[:, cols]))
            gate[:, cols] = gt
            h3[:, cols] = x[:, cols] + gt * pp[:, cols]
        y = h3[...]
        _, yh = _rms_stats(y)
        e = yh * gfin_ref[...] - t_ref[...]
        per_tok = jnp.mean(e * e, axis=-1, keepdims=True)
        loss_ref[...] += 0.5 * jnp.sum(per_tok, axis=0, keepdims=True)
        dh, dg = _rms_bwd_tile(y, gfin_ref[...], e * (1.0 / n))
        dg_ref[...] += dg
        gt = gate[...]
        dgl = _bf(dh * pp[...] * gt * (1.0 - gt))
        dgl_ref[...] = dgl
        dpp_ref[...] = _bf(dh * gt)
        for c in range(n // tn):
            cols = slice(c * tn, (c + 1) * tn)
            h3[:, cols] = _dot_nt(dgl, wpg_ref[cols, :])
        dx, dgp = _rms_bwd_tile(x, gple_ref[...], h3[...])
        dh_ref[...] = dh + dx
        dgple_ref[...] += dgp

    tile = lambda width: pl.BlockSpec((tm, width), lambda i: (i, 0))
    whole = lambda a: pl.BlockSpec(a.shape, lambda i, nd=a.ndim: (0,) * nd)
    return pl.pallas_call(
        body, grid=(S // tm,), name="ple_and_loss",
        in_specs=[tile(n), tile(p.shape[1]), tile(n), whole(w_pg), whole(w_pp), whole(g_ple), whole(g_final)],
        out_specs=[tile(n), tile(n), tile(n), tile(n), pl.BlockSpec((1, 128), lambda i: (0, 0)),
                   pl.BlockSpec((1, n), lambda i: (0, 0)), pl.BlockSpec((1, n), lambda i: (0, 0))],
        out_shape=[jax.ShapeDtypeStruct((S, n), BF16), jax.ShapeDtypeStruct((S, n), F32), jax.ShapeDtypeStruct((S, n), BF16),
                   jax.ShapeDtypeStruct((S, n), BF16), jax.ShapeDtypeStruct((1, 128), F32), jax.ShapeDtypeStruct((1, n), F32),
                   jax.ShapeDtypeStruct((1, n), F32)],
        scratch_shapes=[pltpu.VMEM((tm, n), F32)] * 3,
        compiler_params=_params(("arbitrary",)),
    )(h2, p, target, w_pg, w_pp, g_ple, g_final)


def _rope_tables(S):
    half = 32
    inv = (1.0 / (np.float32(ROPE_THETA) ** (np.arange(half, dtype=np.float32) * np.float32(2.0 / 64)))).astype(np.float32)
    ang = np.arange(S).astype(np.float32)[:, None] * inv[None, :]
    cos, sin = np.cos(ang), np.sin(ang)
    return jnp.asarray(np.tile(cos, (1, 4))), jnp.asarray(np.concatenate([-sin, sin, -sin, sin], axis=1))


def _attn_common(i, kc, kp, vc, vp, cc, sc, cp, sp):
    lane = lax.broadcasted_iota(jnp.int32, (1, HEAD_PAIR), 1)
    lane_lo = jnp.bitwise_and(lane, 63) < 32
    slot = [lane < 64, lane >= 64]

    def swap_halves(t):
        return jnp.where(lane_lo, pltpu.roll(t, 96, 1), pltpu.roll(t, 32, 1))

    def rope(t, cos, sin):
        return t * cos + swap_halves(t) * sin

    def unrope(d, cos, sin):
        return d * cos + swap_halves(d * sin)

    k2 = jnp.concatenate([rope(kp, cp, sp), rope(kc, cc, sc)], axis=0)
    v2 = jnp.concatenate([vp, vc], axis=0)
    r = lax.broadcasted_iota(jnp.int32, (ATTN_BLOCK, 2 * ATTN_BLOCK), 0)
    c = lax.broadcasted_iota(jnp.int32, (ATTN_BLOCK, 2 * ATTN_BLOCK), 1)
    valid = (c > r) & (c <= r + ATTN_BLOCK) & jnp.logical_or(c >= ATTN_BLOCK, i > 0)
    ks, vs = {}, {}
    for j in range(2):
        kn = jnp.where(slot[j], k2, 0.0)
        vn = jnp.where(slot[j], v2, 0.0)
        for s in range(2):
            ks[j, s] = _bf(kn if s == j else pltpu.roll(kn, 64, 1))
            vs[j, s] = _bf(vn if s == j else pltpu.roll(vn, 64, 1))
    return slot, rope, unrope, valid, ks, vs


def _attn_probs(scores, valid, sink):
    s = jnp.where(valid, scores * 0.125, NEG)
    m = jnp.maximum(jnp.max(s, axis=1, keepdims=True), sink)
    e = jnp.exp(s - m)
    z = jnp.sum(e, axis=1, keepdims=True) + jnp.exp(sink - m)
    return e * (1.0 / z), m + jnp.log(z)


def _attn_specs(S):
    nb = S // ATTN_BLOCK
    prev = lambda i: jnp.maximum(i - 1, 0)
    blk = lambda w, col, row=(lambda i: i): pl.BlockSpec((ATTN_BLOCK, w), lambda i: (row(i), col))
    in_specs = [pl.BlockSpec(memory_space=pltpu.SMEM),
                blk(512, BLK_Q), blk(128, BLK_K), blk(128, BLK_K, prev), blk(128, BLK_V), blk(128, BLK_V, prev),
                blk(128, 0), blk(128, 0), blk(128, 0, prev), blk(128, 0, prev)]
    return nb, in_specs


def _attn_fwd(pa, cos, sin, sinks):
    S = pa.shape[0]
    nb, in_specs = _attn_specs(S)

    def body(sinks_ref, q_ref, kc_ref, kp_ref, vc_ref, vp_ref, cc_ref, sc_ref, cp_ref, sp_ref, o_ref, lse_ref):
        i = pl.program_id(0)
        lane = lax.broadcasted_iota(jnp.int32, (1, HEAD_PAIR), 1)
        cc, sc = cc_ref[...], sc_ref[...]
        _, rope, _, valid, ks, vs = _attn_common(i, kc_ref[...], kp_ref[...], vc_ref[...], vp_ref[...],
                                                 cc, sc, cp_ref[...], sp_ref[...])
        pair_cols = [slice(HEAD_PAIR * pair, HEAD_PAIR * (pair + 1)) for pair in range(4)]
        qps = [_bf(rope(q_ref[:, cols], cc, sc)) for cols in pair_cols]
        outs, lses = {}, {}

        def head_program(h):
            pair, s = divmod(h, 2)
            j = h // 4
            scores = _dot_nt(qps[pair], ks[j, s])
            yield
            p, lse = _attn_probs(scores, valid, sinks_ref[h])
            outs[h] = _dot(_bf(p), vs[j, s])
            lses[h] = jnp.where(lane == h, lse, 0.0)

        _interleave(head_program(h) for h in range(8))
        for pair, cols in enumerate(pair_cols):
            o_ref[:, cols] = outs[2 * pair] + outs[2 * pair + 1]
        lse_ref[...] = sum((lses[h] for h in range(1, 8)), lses[0])

    return pl.pallas_call(
        body, grid=(nb,), name="attn_fwd", in_specs=in_specs,
        out_specs=[pl.BlockSpec((ATTN_BLOCK, 512), lambda i: (i, 0)), pl.BlockSpec((ATTN_BLOCK, 128), lambda i: (i, 0))],
        out_shape=[jax.ShapeDtypeStruct((S, 512), F32), jax.ShapeDtypeStruct((S, 128), F32)],
        compiler_params=_params(("parallel",)),
    )(sinks, pa, pa, pa, pa, pa, cos, sin, cos, sin)


def _attn_bwd(pa, cos, sin, sinks, dcat, attn, lse):
    S = pa.shape[0]
    nb, in_specs = _attn_specs(S)
    in_specs = in_specs + [pl.BlockSpec((ATTN_BLOCK, 512), lambda i: (i, 0))] * 2 + [pl.BlockSpec((ATTN_BLOCK, 128), lambda i: (i, 0))]

    def body(sinks_ref, q_ref, kc_ref, kp_ref, vc_ref, vp_ref, cc_ref, sc_ref, cp_ref, sp_ref, do_ref, o_ref, lse_ref,
             dq_ref, dk_ref, dv_ref, dsink_ref):
        i = pl.program_id(0)

        @pl.when(i == 0)
        def _():
            dk_ref[...] = jnp.zeros_like(dk_ref)
            dv_ref[...] = jnp.zeros_like(dv_ref)
            dsink_ref[...] = jnp.zeros_like(dsink_ref)

        cc, sc, cp, sp = cc_ref[...], sc_ref[...], cp_ref[...], sp_ref[...]
        slot, rope, unrope, valid, ks, vs = _attn_common(i, kc_ref[...], kp_ref[...], vc_ref[...], vp_ref[...], cc, sc, cp, sp)
        pair_cols = [slice(HEAD_PAIR * pair, HEAD_PAIR * (pair + 1)) for pair in range(4)]
        qps = [_bf(rope(q_ref[:, cols], cc, sc)) for cols in pair_cols]
        dobs = [_bf(do_ref[:, cols]) for cols in pair_cols]
        do_o = [do_ref[:, cols] * o_ref[:, cols] for cols in pair_cols]
        dqs, dks, dvs = {}, {}, {}

        def head_program(h):
            pair, s = divmod(h, 2)
            j = h // 4
            qp, dob = qps[pair], dobs[pair]
            scores = _dot_nt(qp, ks[j, s])
            dp = _dot_nt(dob, vs[j, s])
            yield
            lse_h = lse_ref[:, h:h + 1]
            p = jnp.exp(jnp.where(valid, scores * 0.125, NEG) - lse_h)
            yield
            dr = jnp.sum(jnp.where(slot[s], do_o[pair], 0.0), axis=1, keepdims=True)
            ds = _bf(p * (dp - dr) * 0.125)
            yield
            dsink_ref[h:h + 1, :] += -jnp.sum(jnp.exp(sinks_ref[h] - lse_h) * dr, axis=0, keepdims=True)
            dqs[h] = _dot(ds, ks[j, s])
            dk_h = _dot_tn(ds, qp)
            dv_h = _dot_tn(_bf(p), dob)
            yield
            dk_h, dv_h = jnp.where(slot[s], dk_h, 0.0), jnp.where(slot[s], dv_h, 0.0)
            if s != j:
                dk_h, dv_h = pltpu.roll(dk_h, 64, 1), pltpu.roll(dv_h, 64, 1)
            dks[h], dvs[h] = dk_h, dv_h

        _interleave(head_program(h) for h in range(8))
        dk2 = sum((dks[h] for h in range(1, 8)), dks[0])
        dv2 = sum((dvs[h] for h in range(1, 8)), dvs[0])
        for pair, cols in enumerate(pair_cols):
            dq_ref[:, cols] = _bf(unrope(dqs[2 * pair] + dqs[2 * pair + 1], cc, sc))
        cur = pl.ds(pl.multiple_of(i * ATTN_BLOCK, ATTN_BLOCK), ATTN_BLOCK)
        dk_ref[cur, :] += unrope(dk2[ATTN_BLOCK:], cc, sc)
        dv_ref[cur, :] += dv2[ATTN_BLOCK:]

        @pl.when(i > 0)
        def _():
            prv = pl.ds(pl.multiple_of((i - 1) * ATTN_BLOCK, ATTN_BLOCK), ATTN_BLOCK)
            dk_ref[prv, :] += unrope(dk2[:ATTN_BLOCK], cp, sp)
            dv_ref[prv, :] += dv2[:ATTN_BLOCK]

    whole = lambda w: pl.BlockSpec((S, w), lambda i: (0, 0))
    return pl.pallas_call(
        body, grid=(nb,), name="attn_bwd", in_specs=in_specs,
        out_specs=[pl.BlockSpec((ATTN_BLOCK, 512), lambda i: (i, BLK_Q)), whole(128), whole(128),
                   pl.BlockSpec((8, 128), lambda i: (0, 0))],
        out_shape=[jax.ShapeDtypeStruct((S, D_IN_PAD), BF16), jax.ShapeDtypeStruct((S, 128), F32),
                   jax.ShapeDtypeStruct((S, 128), F32), jax.ShapeDtypeStruct((8, 128), F32)],
        compiler_params=_params(("arbitrary",)),
    )(sinks, pa, pa, pa, pa, pa, cos, sin, cos, sin, dcat, attn, lse)


CONV_ROWS = 512
CONV_PAD = 8


def _conv_silu(scr, w, r0):
    y = w[3:4, :] * scr[pl.ds(CONV_PAD + r0, CONV_ROWS), :]
    for j in range(DN_CONV - 1):
        y = y + w[j:j + 1, :] * scr[pl.ds(CONV_PAD + r0 - 3 + j, CONV_ROWS), :]
    return y


def _dn_prep_fwd(pd, conv_w):
    S = pd.shape[0]
    assert S % CONV_ROWS == 0

    def body(x_ref, w_ref, o_ref, scr):
        b = pl.program_id(0)
        scr[0:CONV_PAD, :] = jnp.zeros((CONV_PAD, DN_DIM), F32)
        scr[pl.ds(CONV_PAD, S), :] = x_ref[...]
        w = w_ref[...]
        q_scale = jnp.where(b < DN_HEADS, DN_DIM ** -0.5, 1.0)
        for r0 in range(0, S, CONV_ROWS):
            y = _conv_silu(scr, w, r0)
            a = y * _sigmoid(y)
            rs = lax.rsqrt(jnp.sum(a * a, axis=1, keepdims=True) + EPS)
            o_ref[pl.ds(r0, CONV_ROWS), :] = a * jnp.where(b < 2 * DN_HEADS, rs * q_scale, 1.0)

    col = pl.BlockSpec((S, DN_DIM), lambda b: (0, b))
    return pl.pallas_call(
        body, grid=(3 * DN_HEADS,), name="dn_prep_fwd",
        in_specs=[pl.BlockSpec((S, DN_DIM), lambda b: (0, BLK_DN + b)), pl.BlockSpec((DN_CONV, DN_DIM), lambda b: (0, b))],
        out_specs=col,
        out_shape=jax.ShapeDtypeStruct((S, 3 * DN_HEADS * DN_DIM), F32),
        scratch_shapes=[pltpu.VMEM((S + CONV_PAD, DN_DIM), F32)],
        compiler_params=_params(("parallel",)),
    )(pd, conv_w)


def _dn_prep_bwd(pd, conv_w, dqkv, dproj):
    S = pd.shape[0]

    def body(x_ref, w_ref, d_ref, _, dx_ref, dw_ref, scr, dscr):
        b = pl.program_id(0)
        scr[0:CONV_PAD, :] = jnp.zeros((CONV_PAD, DN_DIM), F32)
        scr[pl.ds(CONV_PAD, S), :] = x_ref[...]
        dscr[pl.ds(S, CONV_PAD), :] = jnp.zeros((CONV_PAD, DN_DIM), F32)
        w = w_ref[...]
        q_scale = jnp.where(b < DN_HEADS, DN_DIM ** -0.5, 1.0)
        is_qk = b < 2 * DN_HEADS
        dw = [jnp.zeros((1, DN_DIM), F32) for _ in range(DN_CONV)]
        for r0 in range(0, S, CONV_ROWS):
            y = _conv_silu(scr, w, r0)
            sg = _sigmoid(y)
            a = y * sg
            dout = d_ref[pl.ds(r0, CONV_ROWS), :]
            rs = lax.rsqrt(jnp.sum(a * a, axis=1, keepdims=True) + EPS)
            da_qk = q_scale * rs * (dout - a * (rs * rs) * jnp.sum(dout * a, axis=1, keepdims=True))
            dy = jnp.where(is_qk, da_qk, dout) * (sg * (1.0 + y * (1.0 - sg)))
            dscr[pl.ds(r0, CONV_ROWS), :] = dy
            for j in range(DN_CONV):
                dw[j] = dw[j] + jnp.sum(dy * scr[pl.ds(CONV_PAD + r0 - 3 + j, CONV_ROWS), :], axis=0, keepdims=True)
        for j in range(DN_CONV):
            dw_ref[j:j + 1, :] = dw[j]
        for r0 in range(0, S, CONV_ROWS):
            dx = w[3:4, :] * dscr[pl.ds(r0, CONV_ROWS), :]
            for j in range(DN_CONV - 1):
                dx = dx + w[j:j + 1, :] * dscr[pl.ds(r0 + 3 - j, CONV_ROWS), :]
            dx_ref[pl.ds(r0, CONV_ROWS), :] = _bf(dx)

    col = pl.BlockSpec((S, DN_DIM), lambda b: (0, b))
    proj_col = pl.BlockSpec((S, DN_DIM), lambda b: (0, BLK_DN + b))
    wcol = pl.BlockSpec((DN_CONV, DN_DIM), lambda b: (0, b))
    return pl.pallas_call(
        body, grid=(3 * DN_HEADS,), name="dn_prep_bwd",
        in_specs=[proj_col, wcol, col, pl.BlockSpec(memory_space=pl.ANY)], out_specs=[proj_col, wcol],
        out_shape=[jax.ShapeDtypeStruct(dproj.shape, dproj.dtype), jax.ShapeDtypeStruct((DN_CONV, 3 * DN_HEADS * DN_DIM), F32)],
        scratch_shapes=[pltpu.VMEM((S + CONV_PAD, DN_DIM), F32), pltpu.VMEM((S + CONV_PAD, DN_DIM), F32)],
        input_output_aliases={3: 0},
        compiler_params=_params(("parallel",)),
    )(pd, conv_w, dqkv, dproj)


CPAD = 128
CHUNKS_LOCAL = 4
CHUNKS_SCAN = 8


def _chunk_masks():
    ii = lax.broadcasted_iota(jnp.int32, (DN_CHUNK, CPAD), 0)
    jj = lax.broadcasted_iota(jnp.int32, (DN_CHUNK, CPAD), 1)
    return ii, jj


def _rows_pad(a):
    return jnp.concatenate([a, jnp.zeros_like(a)], axis=0)


def _hi_lo(a):
    hi = _bf(a)
    return hi, _bf(a - hi.astype(F32))


def _double_step(t, p):
    C = DN_CHUNK
    th, tl = _hi_lo(t)
    ph, pl_ = _hi_lo(p)
    r1 = _dot(jnp.concatenate([th, tl, ph, pl_], axis=0), _rows_pad(ph))
    r2 = _dot(jnp.concatenate([th, ph], axis=0), _rows_pad(pl_))
    return t + (r1[:C] + r1[C:2 * C] + r2[:C]), r1[2 * C:3 * C] + r1[3 * C:] + r2[C:]


def _dot3_nt(a, b):
    C = DN_CHUNK
    ah, al = _hi_lo(a)
    bh, bl = _hi_lo(b)
    r1 = _dot_nt(jnp.concatenate([ah, al], axis=0), _rows_pad(bh))
    return r1[:C] + r1[C:] + _dot_nt(ah, _rows_pad(bl))


def _dot3_tn(a, b):
    C = DN_CHUNK
    ah, al = _hi_lo(a)
    bh, bl = _hi_lo(b)
    return _dot_tn(jnp.concatenate([ah, al, ah], axis=0), jnp.concatenate([bh, bh, bl], axis=0))[:C]


def _interleave(programs):
    programs = list(programs)
    while programs:
        alive = []
        for prog in programs:
            try:
                next(prog)
                alive.append(prog)
            except StopIteration:
                pass
        programs = alive


def _col_to_row(col, ii, jj):
    return jnp.sum(jnp.where(ii == jj, col, 0.0), axis=0, keepdims=True)


def _row_to_col(row, ii, jj):
    return jnp.sum(jnp.where(ii == jj, row, 0.0), axis=1, keepdims=True)


def _decay(gc_col, ii, jj):
    diff = gc_col - _col_to_row(gc_col, ii, jj)
    return jnp.where(jj <= ii, jnp.exp(jnp.where(jj <= ii, diff, 0.0)), 0.0)


def _softplus(x):
    return jnp.maximum(x, 0.0) + jnp.log(1.0 + jnp.exp(-jnp.abs(x)))


def _head(h):
    return slice(DN_DIM * h, DN_DIM * (h + 1))


def _dn_chunk_fwd(qkv, pg, a_log, dt_bias):
    S = qkv.shape[0]
    C = DN_CHUNK
    G = CHUNKS_LOCAL
    R = G * C
    steps = S // R

    def body(alog_ref, dtb_ref, qkv_ref, pg_ref, w_ref, u_ref, qg_ref, kd_ref, a_ref, t_ref, gcs_ref):
        ii, jj = _chunk_masks()
        lane = lax.broadcasted_iota(jnp.int32, (1, 128), 1)
        eye = (ii == jj).astype(F32)
        gcs_parts = [[] for _ in range(G)]

        def head_program(chunk, h):
            rows = slice(chunk * C, (chunk + 1) * C)
            q, k, v = qkv_ref[rows, _head(h)], qkv_ref[rows, _head(DN_HEADS + h)], qkv_ref[rows, _head(2 * DN_HEADS + h)]
            beta = _sigmoid(pg_ref[rows, h:h + 1])
            g_col = -jnp.exp(alog_ref[h]) * _softplus(pg_ref[rows, DN_HEADS + h:DN_HEADS + h + 1] + dtb_ref[h])
            g_row = _col_to_row(g_col, ii, jj)
            gc_col = jnp.sum(jnp.where(jj <= ii, g_row, 0.0), axis=1, keepdims=True)
            dec = _decay(gc_col, ii, jj)
            eg = jnp.exp(gc_col)
            kb, vb = k * beta, v * beta
            k_rows = _rows_pad(_bf(k))
            kk = _dot_nt(_bf(kb), k_rows)
            qk = _dot_nt(_bf(q), k_rows)
            yield
            t, pw = eye, -jnp.where(jj < ii, kk * dec, 0.0)
            for _ in range(6):
                t, pw = _double_step(t, pw)
                yield
            tb = _bf(t)
            u_ref[rows, _head(h)] = _dot(tb, _rows_pad(_bf(vb)))
            w_ref[rows, _head(h)] = _bf(_dot(tb, _rows_pad(_bf(kb * eg))))
            a_ref[h, rows] = _bf(qk * dec)
            t_ref[h, rows] = t
            qg_ref[rows, _head(h)] = _bf(q * eg)
            kd_ref[rows, _head(h)] = _bf(k * jnp.exp(gc_col[C - 1:C, :] - gc_col))
            gcs_parts[chunk].append(jnp.where(lane == h, gc_col, 0.0) + jnp.where(lane == DN_HEADS + h, beta, 0.0)
                                    + jnp.where(lane == 2 * DN_HEADS + h, g_col, 0.0))

        _interleave(head_program(chunk, h) for chunk in range(G) for h in range(DN_HEADS))
        for chunk in range(G):
            gcs_ref[chunk * C:(chunk + 1) * C, :] = sum(gcs_parts[chunk][1:], gcs_parts[chunk][0])

    smem = pl.BlockSpec(memory_space=pltpu.SMEM)
    wide = pl.BlockSpec((R, 512), lambda n: (n, 0))
    sq = pl.BlockSpec((DN_HEADS, R, CPAD), lambda n: (0, n, 0))
    narrow = pl.BlockSpec((R, 128), lambda n: (n, 0))
    f = lambda *shp: jax.ShapeDtypeStruct(shp, F32)
    b = lambda *shp: jax.ShapeDtypeStruct(shp, BF16)
    return pl.pallas_call(
        body, grid=(steps,), name="dn_chunk_fwd",
        in_specs=[smem, smem, pl.BlockSpec((R, 1536), lambda n: (n, 0)), pl.BlockSpec((R, 128), lambda n: (n, BLK_G))],
        out_specs=[wide, wide, wide, wide, sq, sq, narrow],
        out_shape=[b(S, 512), f(S, 512), b(S, 512), b(S, 512), b(DN_HEADS, S, CPAD), f(DN_HEADS, S, CPAD), f(S, 128)],
        compiler_params=_params(("parallel",)),
    )(a_log, dt_bias, qkv, pg)


def _gated_norm(o, z, gn):
    r, oh = _rms_stats(o)
    return oh * gn * (z * _sigmoid(z))


def _dn_scan_fwd(w, u, qg, kd, a, gcs, pz, gn):
    S = w.shape[0]
    C = DN_CHUNK
    nc = S // C
    G = CHUNKS_SCAN
    R = G * C

    def body(w_ref, u_ref, qg_ref, kd_ref, a_ref, gcs_ref, z_ref, gn_ref, o_ref, vn_ref, sst_ref, out_ref, state):
        @pl.when(pl.program_id(0) == 0)
        def _():
            state[...] = jnp.zeros_like(state)

        def head_program(chunk, h):
            hs = _head(h)
            rows = slice(chunk * C, (chunk + 1) * C)
            s_in = state[h]
            sb = _bf(s_in)
            sst_ref[chunk, h] = sb
            w_s = _dot(w_ref[rows, hs], sb)
            q_s = _dot(qg_ref[rows, hs], sb)
            yield
            vn = u_ref[rows, hs] - w_s
            vnb = _bf(vn)
            o = q_s + _dot(a_ref[h, rows], _rows_pad(vnb))
            k_v = _dot_tn(kd_ref[rows, hs], vnb)
            yield
            state[h] = s_in * jnp.exp(gcs_ref[(chunk + 1) * C - 1:(chunk + 1) * C, h:h + 1]) + k_v
            o_ref[rows, hs] = o
            vn_ref[rows, hs] = vnb
            out_ref[rows, hs] = _bf(_gated_norm(o, z_ref[rows, hs], gn_ref[...]))

        for chunk in range(G):
            _interleave(head_program(chunk, h) for h in range(DN_HEADS))

    wide = pl.BlockSpec((R, 512), lambda n: (n, 0))
    f = lambda *shp: jax.ShapeDtypeStruct(shp, F32)
    b = lambda *shp: jax.ShapeDtypeStruct(shp, BF16)
    return pl.pallas_call(
        body, grid=(nc // G,), name="dn_scan_fwd",
        in_specs=[wide, wide, wide, wide, pl.BlockSpec((DN_HEADS, R, CPAD), lambda n: (0, n, 0)),
                  pl.BlockSpec((R, 128), lambda n: (n, 0)), pl.BlockSpec((R, 512), lambda n: (n, BLK_Z)),
                  pl.BlockSpec((1, DN_DIM), lambda n: (0, 0))],
        out_specs=[wide, wide, pl.BlockSpec((G, DN_HEADS, DN_DIM, DN_DIM), lambda n: (n, 0, 0, 0)), wide],
        out_shape=[f(S, 512), b(S, 512), b(nc, DN_HEADS, DN_DIM, DN_DIM), b(S, 512)],
        scratch_shapes=[pltpu.VMEM((DN_HEADS, DN_DIM, DN_DIM), F32)],
        compiler_params=_params(("arbitrary",)),
    )(w, u, qg, kd, a, gcs, pz, gn)


def _dn_scan_bwd(dcat, o, pz, gn, sst, vnew, w, qg, kd, a, gcs, dproj):
    S = o.shape[0]
    C = DN_CHUNK
    G = CHUNKS_SCAN
    R = G * C
    steps = S // R

    def body(dy_ref, o_ref, z_ref, gn_ref, sst_ref, vn_ref, w_ref, qg_ref, kd_ref, a_ref, gcs_ref, _,
             du_ref, dw_ref, dqg_ref, dkd_ref, da_ref, dz_ref, dsc_ref, dgn_ref, dstate):
        @pl.when(pl.program_id(0) == 0)
        def _():
            dstate[...] = jnp.zeros_like(dstate)
            dgn_ref[...] = jnp.zeros_like(dgn_ref)

        gn_ = gn_ref[...]
        lane = lax.broadcasted_iota(jnp.int32, (C, 128), 1)
        row = lax.broadcasted_iota(jnp.int32, (C, 128), 0)
        dgn_parts = []

        def head_program(chunk, h, dsc_parts):
            hs = _head(h)
            rows = slice(chunk * C, (chunk + 1) * C)
            ov, z, dout = o_ref[rows, hs], z_ref[rows, hs], dy_ref[rows, hs]
            r, oh = _rms_stats(ov)
            sg = _sigmoid(z)
            don = dout * (z * sg)
            dz_ref[rows, hs] = _bf(dout * (oh * gn_) * (sg * (1.0 + z * (1.0 - sg))))
            dgn_parts.append(jnp.sum(don * oh, axis=0, keepdims=True))
            dn = don * gn_
            do = _bf(r * (dn - oh * jnp.mean(dn * oh, axis=-1, keepdims=True)))
            sb = sst_ref[chunk, h]
            s_in = sb.astype(F32)
            ds_out = dstate[h]
            dsb = _bf(ds_out)
            vnb = vn_ref[rows, hs]
            wb, qgb, kdb, ab = w_ref[rows, hs], qg_ref[rows, hs], kd_ref[rows, hs], a_ref[h, rows]
            dvn = _dot_tn(ab, do)[:C] + _dot(kdb, dsb)
            yield
            da_ref[h, rows] = _dot_nt(do, _rows_pad(vnb))
            dqg_ref[rows, hs] = _dot_nt(do, sb)
            dkd_ref[rows, hs] = _dot_nt(vnb, dsb)
            q_do = _dot_tn(qgb, do)
            yield
            dvnb = _bf(dvn)
            dw_ref[rows, hs] = _bf(-_dot_nt(dvnb, sb))
            w_dvn = _dot_tn(wb, dvnb)
            du_ref[rows, hs] = dvnb
            yield
            d_last = jnp.exp(gcs_ref[(chunk + 1) * C - 1:(chunk + 1) * C, h:h + 1])
            dd = jnp.sum(jnp.sum(ds_out * s_in, axis=1, keepdims=True), axis=0, keepdims=True)
            dsc_parts.append(jnp.where((lane == h) & (row == C - 1), dd * d_last, 0.0))
            dstate[h] = ds_out * d_last + q_do - w_dvn

        for chunk in reversed(range(G)):
            dsc_parts = []
            _interleave(head_program(chunk, h, dsc_parts) for h in range(DN_HEADS))
            dsc_ref[chunk * C:(chunk + 1) * C, :] = sum(dsc_parts[1:], dsc_parts[0])
        dgn_ref[...] += sum(dgn_parts[1:], dgn_parts[0])

    rev = lambda n: steps - 1 - n
    wide = pl.BlockSpec((R, 512), lambda n: (rev(n), 0))
    z_spec = pl.BlockSpec((R, 512), lambda n: (rev(n), BLK_Z))
    sq = pl.BlockSpec((DN_HEADS, R, CPAD), lambda n: (0, rev(n), 0))
    narrow = pl.BlockSpec((R, 128), lambda n: (rev(n), 0))
    gn_spec = pl.BlockSpec((1, DN_DIM), lambda n: (0, 0))
    f = lambda *shp: jax.ShapeDtypeStruct(shp, F32)
    b = lambda *shp: jax.ShapeDtypeStruct(shp, BF16)
    return pl.pallas_call(
        body, grid=(steps,), name="dn_scan_bwd",
        in_specs=[pl.BlockSpec((R, 512), lambda n: (rev(n), 1)), wide, z_spec, gn_spec,
                  pl.BlockSpec((G, DN_HEADS, DN_DIM, DN_DIM), lambda n: (rev(n), 0, 0, 0)),
                  wide, wide, wide, wide, sq, narrow, pl.BlockSpec(memory_space=pl.ANY)],
        out_specs=[wide, wide, wide, wide, sq, z_spec, narrow, gn_spec],
        out_shape=[b(S, 512), b(S, 512), f(S, 512), f(S, 512), f(DN_HEADS, S, CPAD),
                   jax.ShapeDtypeStruct(dproj.shape, dproj.dtype), f(S, 128), f(1, DN_DIM)],
        scratch_shapes=[pltpu.VMEM((DN_HEADS, DN_DIM, DN_DIM), F32)],
        input_output_aliases={11: 5},
        compiler_params=_params(("arbitrary",)),
    )(dcat, o, pz, gn, sst, vnew, w, qg, kd, a, gcs, dproj)


def _dn_chunk_bwd(qkv, pg, t_inv, gcs, du, dw, dqg, dkd, da, dsc, a_log, dt_bias, dproj):
    S = qkv.shape[0]
    C = DN_CHUNK
    G = CHUNKS_LOCAL
    R = G * C

    def body(alog_ref, dtb_ref, qkv_ref, pg_ref, t_ref, gcs_ref, du_ref, dw_ref, dqg_ref, dkd_ref, da_ref, dsc_ref, _,
             dqkv_ref, dpg_ref, acc_ref):
        @pl.when(pl.program_id(0) == 0)
        def _():
            acc_ref[...] = jnp.zeros_like(acc_ref)

        ii, jj = _chunk_masks()
        lane = lax.broadcasted_iota(jnp.int32, (1, 128), 1)
        row8 = lax.broadcasted_iota(jnp.int32, (8, 128), 0)
        lane8 = lax.broadcasted_iota(jnp.int32, (8, 128), 1)
        rowc = lax.broadcasted_iota(jnp.int32, (C, 1), 0)
        tril, strict = jj <= ii, jj < ii
        dpg_parts, acc_parts = [[] for _ in range(G)], []

        def head_program(chunk, h):
            rows = slice(chunk * C, (chunk + 1) * C)
            q, k, v = qkv_ref[rows, _head(h)], qkv_ref[rows, _head(DN_HEADS + h)], qkv_ref[rows, _head(2 * DN_HEADS + h)]
            gc_col, beta, g_col = gcs_ref[rows, h:h + 1], gcs_ref[rows, DN_HEADS + h:DN_HEADS + h + 1], \
                gcs_ref[rows, 2 * DN_HEADS + h:2 * DN_HEADS + h + 1]
            dec = _decay(gc_col, ii, jj)
            eg = jnp.exp(gc_col)
            g_last = gc_col[C - 1:C, :]
            ek = jnp.exp(g_last - gc_col)
            kb, vb = k * beta, v * beta
            kbg = kb * eg
            qb, kbb = _bf(q), _bf(kb)
            k_rows = _rows_pad(_bf(k))
            t = t_ref[h, rows]
            tb = _bf(t)
            dub, dwb = du_ref[rows, _head(h)], dw_ref[rows, _head(h)]
            dqg_, dkd_ = dqg_ref[rows, _head(h)], dkd_ref[rows, _head(h)]
            dt = _dot_nt(dub, _rows_pad(_bf(vb))) + _dot_nt(dwb, _rows_pad(_bf(kbg)))
            t_du_dw = _dot_tn(tb, jnp.concatenate([dub, dwb], axis=1))
            dvb, dkbg = t_du_dw[:C, :DN_DIM], t_du_dw[:C, DN_DIM:]
            kk = _dot_nt(kbb, k_rows)
            qk = _dot_nt(qb, k_rows)
            yield
            dt_t = _dot3_nt(dt, t)
            yield
            dl = -_dot3_tn(t, dt_t)
            yield
            dm = jnp.where(strict, dl * dec, 0.0)
            dqk = jnp.where(tril, da_ref[h, rows] * dec, 0.0)
            gmat = dm * kk + dqk * qk
            dgc = jnp.sum(gmat, axis=1, keepdims=True) - _row_to_col(jnp.sum(gmat, axis=0, keepdims=True), ii, jj)
            dmb, dqkb = _bf(dm), _bf(dqk)
            yield
            dkb = _dot(dmb, k_rows) + dkbg * eg
            dk = _dot_tn(jnp.concatenate([dmb, dqkb], axis=0), jnp.concatenate([kbb, qb], axis=0))[:C] + dkd_ * ek
            dq = _dot(dqkb, k_rows) + dqg_ * eg
            yield
            tk = jnp.sum(dkd_ * k * ek, axis=1, keepdims=True)
            dgc = dgc + jnp.sum(dqg_ * q * eg, axis=1, keepdims=True) - tk + jnp.sum(dkbg * kbg, axis=1, keepdims=True)
            dgl = jnp.sum(tk, axis=0, keepdims=True) + dsc_ref[(chunk + 1) * C - 1:(chunk + 1) * C, h:h + 1]
            dgc = dgc + jnp.where(rowc == C - 1, dgl, 0.0)
            yield
            dk = dk + dkb * beta
            dbeta = jnp.sum(dkb * k, axis=1, keepdims=True) + jnp.sum(dvb * v, axis=1, keepdims=True)
            dqkv_ref[rows, _head(h)] = dq
            dqkv_ref[rows, _head(DN_HEADS + h)] = dk
            dqkv_ref[rows, _head(2 * DN_HEADS + h)] = dvb * beta
            dg_col = jnp.sum(jnp.where(jj >= ii, _col_to_row(dgc, ii, jj), 0.0), axis=1, keepdims=True)
            yield
            db = dbeta * beta * (1.0 - beta)
            da_in = dg_col * (-jnp.exp(alog_ref[h])) * _sigmoid(pg_ref[rows, DN_HEADS + h:DN_HEADS + h + 1] + dtb_ref[h])
            dpg_parts[chunk].append(jnp.where(lane == h, db, 0.0) + jnp.where(lane == DN_HEADS + h, da_in, 0.0))
            acc_parts.append(jnp.where((row8 == 0) & (lane8 == h), jnp.sum(dg_col * g_col, axis=0, keepdims=True), 0.0)
                             + jnp.where((row8 == 1) & (lane8 == h), jnp.sum(da_in, axis=0, keepdims=True), 0.0))

        _interleave(head_program(chunk, h) for chunk in range(G) for h in range(DN_HEADS))
        for chunk in range(G):
            dpg = sum(dpg_parts[chunk][1:], dpg_parts[chunk][0])
            dpg_ref[chunk * C:(chunk + 1) * C, :] = _bf(jnp.concatenate([dpg, jnp.zeros_like(dpg)], axis=1))
        acc_ref[...] += sum(acc_parts[1:], acc_parts[0])

    smem = pl.BlockSpec(memory_space=pltpu.SMEM)
    wide = pl.BlockSpec((R, 512), lambda n: (n, 0))
    sq = pl.BlockSpec((DN_HEADS, R, CPAD), lambda n: (0, n, 0))
    narrow = pl.BlockSpec((R, 128), lambda n: (n, 0))
    qkv_spec = pl.BlockSpec((R, 1536), lambda n: (n, 0))
    f = lambda *shp: jax.ShapeDtypeStruct(shp, F32)
    return pl.pallas_call(
        body, grid=(S // R,), name="dn_chunk_bwd",
        in_specs=[smem, smem, qkv_spec, pl.BlockSpec((R, 128), lambda n: (n, BLK_G)), sq, narrow, wide, wide, wide, wide, sq,
                  narrow, pl.BlockSpec(memory_space=pl.ANY)],
        out_specs=[qkv_spec, pl.BlockSpec((R, 256), lambda n: (n, BLK_G_PAD)), pl.BlockSpec((8, 128), lambda n: (0, 0))],
        out_shape=[f(S, 1536), jax.ShapeDtypeStruct(dproj.shape, dproj.dtype), f(8, 128)],
        input_output_aliases={12: 1},
        compiler_params=_params(("arbitrary",)),
    )(a_log, dt_bias, qkv, pg, t_inv, gcs, du, dw, dqg, dkd, da, dsc, dproj)


def _fill_kv(dk, dv, dproj):
    S = dk.shape[0]
    tm = min(512, S)

    def body(dk_ref, dv_ref, _, o_ref):
        o_ref[...] = _bf(jnp.concatenate([dk_ref[...], dv_ref[...]], axis=1))

    tile = pl.BlockSpec((tm, 128), lambda i: (i, 0))
    return pl.pallas_call(
        body, grid=(S // tm,), name="fill_kv",
        in_specs=[tile, tile, pl.BlockSpec(memory_space=pl.ANY)],
        out_specs=pl.BlockSpec((tm, 256), lambda i: (i, BLK_KV)),
        out_shape=jax.ShapeDtypeStruct(dproj.shape, dproj.dtype),
        input_output_aliases={2: 0},
        compiler_params=_params(("parallel",)),
    )(dk, dv, dproj)


def _w_in_to_internal(wt):
    return jnp.concatenate([wt[0:512], wt[2304:2816], wt[768:2304], wt[512:768], wt[2816:2824],
                            jnp.zeros((D_IN_PAD - D_IN, wt.shape[1]), wt.dtype)], axis=0)


def _w_in_from_internal(gt):
    return jnp.concatenate([gt[0:512], gt[2560:2816], gt[1024:2560], gt[512:1024], gt[2816:2824]], axis=0)


def _local_step(x, p, target, wts, first_weights, other_weights, ship_early):
    S = x.shape[0]
    cos, sin = _rope_tables(S)
    sinks, a_log, dt_bias = wts["sinks"].reshape(8), wts["a_log"].reshape(4), wts["dt_bias"].reshape(4)
    gn = wts["dn_norm"].reshape(1, DN_DIM)
    add = lambda acc, res: (acc + res,)

    u = _rmsnorm_fwd(x, wts["norm_mix"], "norm_mix_fwd")
    w_in_t, conv_w = first_weights(u)
    proj, = _mm(u, w_in_t, form="nt", name="in_proj", out_dtypes=[F32], tn=512)
    attn, lse = _attn_fwd(proj, cos, sin, sinks)
    qkv = _dn_prep_fwd(proj, conv_w)
    cw, cu, cqg, ckd, ca, ct, gcs = _dn_chunk_fwd(qkv, proj, a_log, dt_bias)
    o, vnew, sst, dn_out = _dn_scan_fwd(cw, cu, cqg, ckd, ca, gcs, proj, gn)
    w_o, = other_weights(("w_o",), dn_out)
    h1, = _mm([attn, dn_out], w_o, form="nn", name="out_proj", out_dtypes=[F32], tn=512, epi=add, extra=[x])

    w_up, w_down = other_weights(("w_up", "w_down"), h1)
    hid, relu, m, h2 = _mlp_fwd(h1, w_up, w_down, wts["norm_mlp"])
    w_pg, w_pp = other_weights(("w_ple_gate", "w_ple_proj"), h2)
    n3, dh2, dgl, dpp, loss, d_norm_final, d_norm_ple = _ple_and_loss(h2, p, target, w_pg, w_pp, wts["norm_ple"],
                                                                     wts["norm_final"].reshape(1, D_MODEL))
    g = {"norm_final": d_norm_final, "norm_ple": d_norm_ple}
    early = {"w_ple_gate": _mm_tn(n3, dgl, name="d_w_ple_gate", tm=512, tn=1024, out_dtype=BF16).reshape(N_DEV, 128, 1024),
             "w_ple_proj": _mm_tn(p, dpp, name="d_w_ple_proj", tm=256, tn=128, out_dtype=BF16, column_shards=True)}
    d_act, = _mm(dh2, w_down, form="nt", name="d_hidden", out_dtypes=[BF16], tn=512,
                 epi=lambda acc, r: (acc * (2.0 * r.astype(F32)),), extra=[relu])
    early["w_down"] = _mm_tn(hid, dh2, name="d_w_down", tm=512, tn=1024, out_dtype=BF16).reshape(N_DEV, 512, 1024)
    early["w_up"] = _mm_tn(m, d_act, name="d_w_up", tm=1024, tn=512, out_dtype=BF16, column_shards=True)
    token = ship_early(early)
    dh1, g["norm_mlp"], dcat = _mm(d_act, w_up, form="nt", name="d_m", out_dtypes=[F32], tn=512, after=token,
                                   norm_bwd=(h1, wts["norm_mlp"], dh2), then_nt=w_o)
    d_w_o = jnp.concatenate([_mm_tn(attn, dh1, name="d_w_o_attn", tm=512, tn=512, out_dtype=BF16),
                             _mm_tn(dn_out, dh1, name="d_w_o_dn", tm=512, tn=512, out_dtype=BF16)], axis=0)
    token = ship_early({"w_o": d_w_o.reshape(N_DEV, 128, 1024)})
    dproj, dk, dv, dsinks = _attn_bwd(proj, cos, sin, sinks + token[0, 0], dcat, attn, lse)
    g["sinks"] = dsinks[:, 0].reshape(1, 8)
    du_, dw_, dqg, dkd, da, dproj, dsc, g["dn_norm"] = _dn_scan_bwd(dcat, o, proj, gn, sst, vnew, cw, cqg, ckd, ca, gcs, dproj)
    dqkv, dproj, gate_acc = _dn_chunk_bwd(qkv, proj, ct, gcs, du_, dw_, dqg, dkd, da, dsc, a_log, dt_bias, dproj)
    g["a_log"], g["dt_bias"] = gate_acc[0:1, 0:4], gate_acc[1:2, 0:4]
    dproj, g["conv_w"] = _dn_prep_bwd(proj, conv_w, dqkv, dproj)
    dproj = _fill_kv(dk, dv, dproj)
    token = ship_early({"w_in": _mm_tn(dproj, u, name="d_w_in", tm=512, tn=1024, out_dtype=BF16)})
    grad_x, g["norm_mix"] = _mm(dproj, w_in_t, form="nn", name="d_u", out_dtypes=[F32], tn=512, after=token,
                                norm_bwd=(x, wts["norm_mix"], dh1))
    return loss, grad_x, g


def _peer(k):
    x, y, c = lax.axis_index("x"), lax.axis_index("y"), lax.axis_index("c")
    px = 1 - x if k & 4 else x
    py = 1 - y if k & 2 else y
    pc = 1 - c if k & 1 else c
    return (px, py, pc), 4 * px + 2 * py + pc


def _exchange(srcs, name, gather):
    n = len(srcs)
    gathers = list(gather) if isinstance(gather, (list, tuple)) else [gather] * n
    shapes = [(N_DEV,) + s.shape if gt else s.shape for s, gt in zip(srcs, gathers)]

    def body(*refs):
        src_refs, out_refs = refs[:n], refs[n:2 * n]
        send_sems, recv_sems, local_sems = refs[2 * n:]
        _, me = _peer(0)
        piece = lambda a, d: src_refs[a] if gathers[a] else src_refs[a].at[d]
        local = [pltpu.make_async_copy(piece(a, me), out_refs[a].at[me], local_sems.at[a]) for a in range(n)]
        for cp in local:
            cp.start()
        copies = []
        for a in range(n):
            for k in range(1, N_DEV):
                dev, idx = _peer(k)
                cp = pltpu.make_async_remote_copy(src_ref=piece(a, idx), dst_ref=out_refs[a].at[me],
                                                  send_sem=send_sems.at[a, k - 1], recv_sem=recv_sems.at[a, k - 1],
                                                  device_id=dev, device_id_type=MESH)
                cp.start()
                copies.append(cp)
        for cp in copies:
            cp.wait_recv()
        for cp in copies:
            cp.wait_send()
        for cp in local:
            cp.wait()

    anywhere = pl.BlockSpec(memory_space=pl.ANY)
    return pl.pallas_call(
        body, name=name, in_specs=[anywhere] * n, out_specs=[anywhere] * n,
        out_shape=[jax.ShapeDtypeStruct(shp, s.dtype) for shp, s in zip(shapes, srcs)],
        scratch_shapes=[pltpu.SemaphoreType.DMA((n, N_DEV - 1)), pltpu.SemaphoreType.DMA((n, N_DEV - 1)),
                        pltpu.SemaphoreType.DMA((n,))],
    )(*srcs)


_HBM = pl.BlockSpec(memory_space=pltpu.HBM)
_SEM = pl.BlockSpec(memory_space=pltpu.SEMAPHORE)
_EFFECT = pltpu.SideEffectType.DATAFLOW_SIDE_EFFECTING


def _split_copies(src_refs, land_refs, send_sems, recv_sems, modes, which=None):
    _, me = _peer(0)
    copies = []
    which = range(len(src_refs)) if which is None else which
    for a, src, land in zip(which, src_refs, land_refs):
        if modes[a] == "columns":
            n_cols = src.shape[1]
            dst = land.at[:, pl.ds(pl.multiple_of(me * n_cols, n_cols), n_cols)]
        else:
            dst = land.at[me]
        for k in range(1, N_DEV):
            dev, idx = _peer(k)
            sem = a * (N_DEV - 1) + k - 1
            copies.append(pltpu.make_async_remote_copy(
                src_ref=src.at[idx] if modes[a] == "pieces" else src, dst_ref=dst, send_sem=send_sems.at[sem],
                recv_sem=recv_sems.at[sem], device_id=dev, device_id_type=MESH))
    return copies


def _exchange_start(srcs, name, modes):
    n = len(srcs)
    modes = [modes] * n if isinstance(modes, str) else list(modes)
    me = 4 * lax.axis_index("x") + 2 * lax.axis_index("y") + lax.axis_index("c")
    lands = []
    for s, mode in zip(srcs, modes):
        if mode == "columns":
            empty = lax.empty((s.shape[0], N_DEV * s.shape[1]), s.dtype)
            lands.append(lax.dynamic_update_slice(empty, s, (0, me * s.shape[1])))
        else:
            own = s if mode == "slots" else lax.dynamic_index_in_dim(s, me, 0, keepdims=False)
            shape = (N_DEV,) + s.shape if mode == "slots" else s.shape
            lands.append(lax.dynamic_update_index_in_dim(lax.empty(shape, s.dtype), own, me, 0))

    def body(*refs):
        src_refs, land_refs = refs[:n], refs[n:2 * n]
        send_sems, recv_sems = refs[2 * n], refs[2 * n + 1]
        for cp in _split_copies(src_refs, land_refs, send_sems, recv_sems, modes):
            cp.start()
        refs[-1][...] = jnp.zeros_like(refs[-1])

    both = list(srcs) + lands
    sems = pltpu.SemaphoreType.DMA((n * (N_DEV - 1),))
    out = pl.pallas_call(
        body, name=name,
        out_shape=(sems, sems, *[pltpu.HBM(t.shape, t.dtype) for t in both], jax.ShapeDtypeStruct((8, 128), F32)),
        in_specs=[_HBM] * (2 * n), out_specs=(_SEM, _SEM, *[_HBM] * (2 * n), pl.BlockSpec(memory_space=pltpu.VMEM)),
        input_output_aliases={i: 2 + i for i in range(2 * n)},
        compiler_params=pltpu.CompilerParams(has_side_effects=_EFFECT),
    )(*[pltpu.with_memory_space_constraint(t, pltpu.HBM) for t in both])
    return (n, modes, out[:-1]), out[-1]


def _exchange_wait(handle, after, name, which=None):
    n_all, modes, (send_sems, recv_sems, *both_all) = handle
    which = list(range(n_all)) if which is None else list(which)
    n = len(which)
    both = [both_all[a] for a in which] + [both_all[n_all + a] for a in which]

    def body(*refs):
        src_refs, land_refs = refs[:n], refs[n:2 * n]
        for cp in _split_copies(src_refs, land_refs, refs[2 * n], refs[2 * n + 1], modes, which):
            cp.wait_send()
            cp.wait_recv()

    out = pl.pallas_call(
        body, name=name, out_shape=tuple(pltpu.HBM(t.shape, t.dtype) for t in both),
        in_specs=[_HBM] * (2 * n) + [_SEM, _SEM, pl.BlockSpec(memory_space=pl.ANY)], out_specs=tuple([_HBM] * (2 * n)),
        input_output_aliases={i: i for i in range(2 * n)},
        compiler_params=pltpu.CompilerParams(has_side_effects=_EFFECT),
    )(*both, send_sems, recv_sems, after)
    return list(out[n:])


def _adam_update(g, w, m, v):
    nm = ADAM_B1 * m + (1.0 - ADAM_B1) * g
    nv = ADAM_B2 * v + (1.0 - ADAM_B2) * (g * g)
    m_hat = nm / (1.0 - ADAM_B1 ** ADAM_STEP)
    v_hat = nv / (1.0 - ADAM_B2 ** ADAM_STEP)
    return -ADAM_LR * (m_hat / (jnp.sqrt(v_hat) + ADAM_EPS) + ADAM_WD * w), nm, nv


def _adamw(parts, w, m, v, name):
    n, R, W = parts.shape
    tm = 128 if R % 128 == 0 else R

    def body(p_ref, w_ref, m_ref, v_ref, g_ref, d_ref, nm_ref, nv_ref):
        g = p_ref[0].astype(F32)
        for s in range(1, n):
            g = g + p_ref[s].astype(F32)
        g_ref[...] = g
        d_ref[...], nm_ref[...], nv_ref[...] = _adam_update(g, w_ref[...], m_ref[...], v_ref[...])

    tile = pl.BlockSpec((tm, W), lambda i: (i, 0))
    return pl.pallas_call(
        body, grid=(R // tm,), name=name,
        in_specs=[pl.BlockSpec((n, tm, W), lambda i: (0, i, 0)), tile, tile, tile],
        out_specs=[tile] * 4, out_shape=[jax.ShapeDtypeStruct((R, W), F32)] * 4,
        compiler_params=_params(("parallel",)),
    )(parts, w, m, v)


_MATRICES = ("w_in", "w_o", "w_up", "w_down", "w_ple_gate", "w_ple_proj")


_OTHERS = ("w_o", "w_up", "w_down", "w_ple_gate", "w_ple_proj")
_OTHER_MODES = {"w_o": "slots", "w_up": "slots", "w_down": "slots", "w_ple_gate": "slots", "w_ple_proj": "columns"}


_VECTORS = ("norm_mix", "norm_mlp", "norm_ple", "norm_final", "a_log", "dt_bias", "sinks", "dn_norm")
_SMALL_ROWS, _LOSS_ROW, _CONV_ROW = 16, 8, 9


def _pack_small(vectors, loss, conv):
    def body(*refs):
        out = refs[-1]
        out[...] = jnp.zeros_like(out)
        for r, ref in enumerate(refs[:len(_VECTORS)]):
            out[r:r + 1, 0:ref.shape[1]] = ref[...]
        out[_LOSS_ROW:_LOSS_ROW + 1, 0:128] = refs[len(_VECTORS)][...]
        out[_CONV_ROW:_CONV_ROW + 6, :] = refs[len(_VECTORS) + 1][...]

    return pl.pallas_call(body, name="pack_small", out_shape=jax.ShapeDtypeStruct((_SMALL_ROWS, 1024), F32))(*vectors, loss, conv)


def _sum_slots(parts):
    def body(p_ref, o_ref):
        acc = p_ref[0]
        for s in range(1, parts.shape[0]):
            acc = acc + p_ref[s]
        o_ref[...] = acc

    return pl.pallas_call(body, name="sum_small", out_shape=jax.ShapeDtypeStruct(parts.shape[1:], parts.dtype))(parts)


def _adamw_vectors(summed, conv_g, wmv):
    names = _VECTORS + ("conv_w",)
    flat = [a for triple in wmv for a in triple]

    def body(*refs):
        sum_ref, conv_ref = refs[0], refs[1]
        ins, outs = refs[2:2 + len(flat)], refs[2 + len(flat):]
        for i in range(len(names)):
            w_ref, m_ref, v_ref = ins[3 * i:3 * i + 3]
            g = conv_ref[...] if i == len(_VECTORS) else sum_ref[i:i + 1, 0:w_ref.shape[1]]
            outs[4 * i][...] = g
            outs[4 * i + 1][...], outs[4 * i + 2][...], outs[4 * i + 3][...] = _adam_update(g, w_ref[...], m_ref[...], v_ref[...])

    out_shape = [jax.ShapeDtypeStruct(t[0].shape, F32) for t in wmv for _ in range(4)]
    res = pl.pallas_call(body, name="adamw_vectors", out_shape=out_shape)(summed, conv_g, *flat)
    return {n: res[4 * i:4 * i + 4] for i, n in enumerate(names)}


_ORDER = ("norm_mix", "w_in", "conv_w", "a_log", "dt_bias", "dn_norm", "sinks", "w_o", "norm_mlp", "w_up", "w_down",
          "norm_ple", "w_ple_gate", "w_ple_proj", "norm_final")


def kernel(x, p, norm_mix, w_in, conv_w, a_log, dt_bias, dn_norm, sinks, w_o, norm_mlp, w_up, w_down, norm_ple, w_ple_gate, w_ple_proj, norm_final, loss_target, m_norm_mix, m_w_in, m_conv_w, m_a_log, m_dt_bias, m_dn_norm, m_sinks, m_w_o, m_norm_mlp, m_w_up, m_w_down, m_norm_ple, m_w_ple_gate, m_w_ple_proj, m_norm_final, v_norm_mix, v_w_in, v_conv_w, v_a_log, v_dt_bias, v_dn_norm, v_sinks, v_w_o, v_norm_mlp, v_w_up, v_w_down, v_norm_ple, v_w_ple_gate, v_w_ple_proj, v_norm_final):
    w = dict(norm_mix=norm_mix, w_in=w_in[0], conv_w=conv_w[0], a_log=a_log, dt_bias=dt_bias, dn_norm=dn_norm, sinks=sinks,
             w_o=w_o[0], norm_mlp=norm_mlp, w_up=w_up[0], w_down=w_down[0], norm_ple=norm_ple, w_ple_gate=w_ple_gate[0],
             w_ple_proj=w_ple_proj[0], norm_final=norm_final)
    m = dict(norm_mix=m_norm_mix, w_in=m_w_in[0], conv_w=m_conv_w[0], a_log=m_a_log, dt_bias=m_dt_bias, dn_norm=m_dn_norm,
             sinks=m_sinks, w_o=m_w_o[0], norm_mlp=m_norm_mlp, w_up=m_w_up[0], w_down=m_w_down[0], norm_ple=m_norm_ple,
             w_ple_gate=m_w_ple_gate[0], w_ple_proj=m_w_ple_proj[0], norm_final=m_norm_final)
    v = dict(norm_mix=v_norm_mix, w_in=v_w_in[0], conv_w=v_conv_w[0], a_log=v_a_log, dt_bias=v_dt_bias, dn_norm=v_dn_norm,
             sinks=v_sinks, w_o=v_w_o[0], norm_mlp=v_norm_mlp, w_up=v_w_up[0], w_down=v_w_down[0], norm_ple=v_norm_ple,
             w_ple_gate=v_w_ple_gate[0], w_ple_proj=v_w_ple_proj[0], norm_final=v_norm_final)
    me = 4 * lax.axis_index("x") + 2 * lax.axis_index("y") + lax.axis_index("c")
    conv_shard = conv_w.shape[2]

    for d in (w, m, v):
        d["w_in"] = d["w_in"].T
    conv_pad = jnp.pad(w["conv_w"], ((0, 8 - DN_CONV), (0, 256 - conv_shard)))
    gathers, token_gather = _exchange_start([_bf(w["w_in"]), conv_pad] + [_bf(w[n]) for n in _OTHERS], "gather_start",
                                            ["slots", "slots"] + [_OTHER_MODES[n] for n in _OTHERS])
    vectors = dict(w)
    vectors["norm_mix"] = w["norm_mix"] + token_gather[0:1, 0:1]

    def first_weights(after):
        w_in_all, conv_all = _exchange_wait(gathers, after, "gather_first_wait", [0, 1])
        conv_all = jnp.transpose(conv_all[:, :DN_CONV, :conv_shard], (1, 0, 2)).reshape(DN_CONV, N_DEV * conv_shard)
        return _w_in_to_internal(w_in_all.reshape(D_IN, D_MODEL)), conv_all

    as_taken = {"w_o": lambda t: t.reshape(1024, 1024), "w_up": lambda t: t, "w_down": lambda t: t.reshape(4096, 1024),
                "w_ple_gate": lambda t: t.reshape(1024, 1024), "w_ple_proj": lambda t: t}

    def other_weights(names, after):
        which = [2 + _OTHERS.index(n) for n in names]
        got = _exchange_wait(gathers, after, "gather_wait_" + names[0], which)
        return [as_taken[n](t) for n, t in zip(names, got)]

    shipped = []

    def ship_early(pieces):
        names = tuple(pieces)
        if names == ("w_in",):
            pieces = {"w_in": _w_in_from_internal(pieces["w_in"]).reshape(N_DEV, D_IN // N_DEV, D_MODEL)}
        handle, token = _exchange_start([pieces[n] for n in names], "scatter_start_" + names[0], "pieces")
        shipped.append((names, handle))
        return token

    loss, grad_x, g = _local_step(x[0], p[0, 0], loss_target[0], vectors, first_weights, other_weights, ship_early)

    row = lambda t: t.reshape(1, t.size)
    small = _pack_small([row(g[n]) for n in _VECTORS], loss, g["conv_w"].reshape(6, 1024))
    small_all, = _exchange([small], "gather_small", gather=True)
    summed = _sum_slots(small_all)
    conv_g = lax.dynamic_slice(summed[_CONV_ROW:_CONV_ROW + 6].reshape(DN_CONV, N_DEV * conv_shard), (0, me * conv_shard),
                               (DN_CONV, conv_shard))
    small_out = _adamw_vectors(summed, conv_g, [(row(w[n]), row(m[n]), row(v[n])) for n in _VECTORS]
                               + [(w["conv_w"], m["conv_w"], v["conv_w"])])
    big, after = {}, small_out["conv_w"][0]
    for names, handle in shipped:
        for n, r in zip(names, _exchange_wait(handle, after, "scatter_wait_" + names[0])):
            big[n] = _adamw(r, w[n], m[n], v[n], "adamw_" + n)
            after = big[n][1]

    result = [summed[_LOSS_ROW, 0], grad_x[None]]
    for i in range(4):
        for n in _ORDER:
            if n == "w_in":
                result.append(big[n][i].T[None])
            elif n in _MATRICES:
                result.append(big[n][i][None])
            elif n == "conv_w":
                result.append(small_out[n][i][None])
            else:
                result.append(small_out[n][i].reshape(w[n].shape))
    return tuple(result)
```

```python
import jax
import jax.numpy as jnp
import numpy as np
from jax import lax
from jax.experimental import pallas as pl
from jax.experimental.pallas import tpu as pltpu

F32, BF16 = jnp.float32, jnp.bfloat16
EPS = 1e-6
D_MODEL = 1024
N_DEV = 8
ATTN_BLOCK = 128
HEAD_PAIR = 128
DN_HEADS = 4
DN_DIM = 128
DN_CHUNK = 64
DN_CONV = 4
ROPE_THETA = 10000.0
D_IN = 2824
D_IN_PAD = 3072
BLK_Q, BLK_Z = 0, 1
BLK_DN, BLK_K, BLK_V, BLK_G = 8, 20, 21, 22
BLK_KV, BLK_G_PAD = 10, 11
VMEM_LIMIT = 56 * 1024 * 1024
NEG = -1e30
ADAM_LR, ADAM_B1, ADAM_B2, ADAM_EPS, ADAM_WD, ADAM_STEP = 0.001, 0.9, 0.999, 1e-08, 0.01, 10
MESH = pl.DeviceIdType.MESH


def _bf(x):
    return x.astype(BF16)


def _dot(a, b):
    return jnp.dot(a, b, preferred_element_type=F32)


def _dot_nt(a, b):
    return lax.dot_general(a, b, (((1,), (1,)), ((), ())), preferred_element_type=F32)


def _dot_tn(a, b):
    return lax.dot_general(a, b, (((0,), (0,)), ((), ())), preferred_element_type=F32)


def _sigmoid(x):
    return 1.0 / (1.0 + jnp.exp(-x))


def _params(sem):
    return pltpu.CompilerParams(dimension_semantics=sem, vmem_limit_bytes=VMEM_LIMIT)


def _mm(x, w, *, form, name, out_dtypes, tn, epi=None, extra=(), tm=512, w_row_block=0, after=None, norm=None,
        norm_bwd=None, then_nt=None):
    assert norm is None or norm_bwd is None
    xs = list(x) if isinstance(x, (list, tuple)) else [x]
    nx = len(xs)
    S, K = xs[0].shape
    shards = w.ndim == 3
    N = (w.shape[2] * N_DEV if shards else w.shape[1]) if form == "nn" else w.shape[-2]
    assert not (shards and form == "nn" and tn != w.shape[2]) and (nx == 1 or (form == "nn" and not shards and norm is None))
    r0 = w_row_block * K
    tm = min(tm, S)
    n_extra, n_out = len(extra), len(out_dtypes)
    tile = lambda width: pl.BlockSpec((tm, width), lambda i: (i, 0))
    whole = lambda a: pl.BlockSpec(a.shape, lambda i, nd=a.ndim: (0,) * nd)
    ins, in_specs = [*xs, w, *extra], [tile(K)] * nx + [whole(w)] + [tile(N)] * n_extra
    if norm is not None:
        ins, in_specs = ins + [norm], in_specs + [whole(norm)]
    if norm_bwd is not None:
        ins, in_specs = ins + list(norm_bwd), in_specs + [tile(N), whole(norm_bwd[1]), tile(N)]
    if then_nt is not None:
        ins, in_specs = ins + [then_nt], in_specs + [whole(then_nt)]
    if after is not None:
        ins, in_specs = ins + [after], in_specs + [whole(after)]
    out_shape = [jax.ShapeDtypeStruct((S, N), dt) for dt in out_dtypes]
    out_specs = [tile(N)] * n_out
    if norm is not None:
        out_shape, out_specs = out_shape + [jax.ShapeDtypeStruct((S, K), BF16)], out_specs + [tile(K)]
    if norm_bwd is not None:
        out_shape, out_specs = out_shape + [jax.ShapeDtypeStruct((1, N), F32)], out_specs + [pl.BlockSpec((1, N), lambda i: (0, 0))]
    if then_nt is not None:
        out_shape, out_specs = out_shape + [jax.ShapeDtypeStruct((S, then_nt.shape[0]), F32)], out_specs + [tile(then_nt.shape[0])]

    def product(xb, w_ref, cols, c):
        if form == "nn" and nx > 1:
            return sum(_dot(part, w_ref[r0 + p * K:r0 + (p + 1) * K, cols]) for p, part in enumerate(xb))
        if form == "nn":
            return _dot(xb, w_ref[c] if shards else w_ref[r0:r0 + K, cols])
        if not shards:
            return _dot_nt(xb, w_ref[cols, :])
        ks = w.shape[2]
        acc = _dot_nt(xb[:, 0:ks], w_ref[0, cols, :])
        for s in range(1, N_DEV):
            acc = acc + _dot_nt(xb[:, s * ks:(s + 1) * ks], w_ref[s, cols, :])
        return acc

    def body(*refs):
        x_ref, w_ref = refs[0], refs[nx]
        extra_refs = refs[nx + 1:nx + 1 + n_extra]
        at = nx + 1 + n_extra
        if norm is not None:
            gain_ref, at = refs[at], at + 1
        if norm_bwd is not None:
            (y_ref, ygain_ref, dres_ref), at = refs[at:at + 3], at + 3
        if then_nt is not None:
            w2_ref, at = refs[at], at + 1
        outs = refs[len(ins):]
        if norm is not None:
            _, xh = _rms_stats(x_ref[...])
            xb = _bf(xh * gain_ref[...])
            outs[n_out][...] = xb
        else:
            xb = _bf(x_ref[...]) if nx == 1 else [_bf(r[...]) for r in refs[:nx]]
        for c in range(N // tn):
            cols = slice(c * tn, (c + 1) * tn)
            acc = product(xb, w_ref, cols, c)
            res = epi(acc, *[r[:, cols] for r in extra_refs]) if epi else (acc,)
            for o, r in zip(outs[:n_out], res):
                o[:, cols] = r.astype(o.dtype)
        if norm_bwd is not None:
            dx, dg = _rms_bwd_tile(y_ref[...], ygain_ref[...], outs[0][...])
            outs[0][...] = dres_ref[...] + dx
            dg_ref = outs[n_out]

            @pl.when(pl.program_id(0) == 0)
            def _():
                dg_ref[...] = jnp.zeros_like(dg_ref)

            dg_ref[...] += dg
        if then_nt is not None:
            yb = _bf(outs[0][...])
            for c in range(then_nt.shape[0] // tn):
                cols = slice(c * tn, (c + 1) * tn)
                outs[-1][:, cols] = _dot_nt(yb, w2_ref[cols, :])

    return pl.pallas_call(
        body, grid=(S // tm,), name=name, in_specs=in_specs, out_specs=out_specs, out_shape=out_shape,
        compiler_params=_params(("arbitrary",) if norm_bwd is not None else ("parallel",)),
    )(*ins)


def _mlp_fwd(h1, w_up, w_down, gain):
    S, K = h1.shape
    n_sh, _, fs = w_up.shape
    tm = min(512, S)

    def body(x_ref, wup_ref, wdown_ref, g_ref, hid_ref, relu_ref, m_ref, h2_ref):
        x = x_ref[...]
        _, xh = _rms_stats(x)
        mb = _bf(xh * g_ref[...])
        m_ref[...] = mb
        h2_ref[...] = x
        for c in range(n_sh):
            cols = slice(c * fs, (c + 1) * fs)
            r = jnp.maximum(_dot(mb, wup_ref[c]), 0.0)
            hd = _bf(r * r)
            hid_ref[:, cols] = hd
            relu_ref[:, cols] = _bf(r)
            h2_ref[...] += _dot(hd, wdown_ref[cols, :])

    tile = lambda width: pl.BlockSpec((tm, width), lambda i: (i, 0))
    once = lambda a: pl.BlockSpec(a.shape, lambda i, nd=a.ndim: (0,) * nd, pipeline_mode=pl.Buffered(1))
    F = n_sh * fs
    return pl.pallas_call(
        body, grid=(S // tm,), name="mlp_fwd",
        in_specs=[tile(K), once(w_up), once(w_down), pl.BlockSpec(gain.shape, lambda i: (0, 0))],
        out_specs=[tile(F), tile(F), tile(K), tile(K)],
        out_shape=[jax.ShapeDtypeStruct((S, F), BF16), jax.ShapeDtypeStruct((S, F), BF16),
                   jax.ShapeDtypeStruct((S, K), BF16), jax.ShapeDtypeStruct((S, K), F32)],
        compiler_params=_params(("parallel",)),
    )(h1, w_up, w_down, gain)


def _mm_tn(x, dy, *, name, tm, tn, out_dtype=F32, column_shards=False, after=None):
    S, K = x.shape
    N = dy.shape[1]
    waits = [] if after is None else [after]

    def body(x_ref, dy_ref, *rest):
        rest[-1][...] = _dot_tn(_bf(x_ref[...]), _bf(dy_ref[...])).astype(out_dtype)

    if column_shards:
        out_spec = pl.BlockSpec((None, tm, tn), lambda i, j: (j, i, 0))
        out_shape = jax.ShapeDtypeStruct((N // tn, K, tn), out_dtype)
    else:
        out_spec = pl.BlockSpec((tm, tn), lambda i, j: (i, j))
        out_shape = jax.ShapeDtypeStruct((K, N), out_dtype)
    return pl.pallas_call(
        body, grid=(K // tm, N // tn), name=name,
        in_specs=[pl.BlockSpec((S, tm), lambda i, j: (0, i)), pl.BlockSpec((S, tn), lambda i, j: (0, j))]
        + [pl.BlockSpec(memory_space=pl.ANY)] * len(waits),
        out_specs=out_spec, out_shape=out_shape,
        compiler_params=_params(("parallel", "parallel")),
    )(x, dy, *waits)


def _rowwise(body, *, tiled, full, out_tiled, out_acc, name, tm=512, smem=()):
    S = tiled[0].shape[0]
    tm = min(tm, S)
    n_in = len(smem) + len(tiled) + len(full)

    def kern(*refs):
        @pl.when(pl.program_id(0) == 0)
        def _():
            for r in refs[n_in + len(out_tiled):]:
                r[...] = jnp.zeros_like(r)
        body(*refs)

    in_specs = [pl.BlockSpec(memory_space=pltpu.SMEM) for _ in smem]
    in_specs += [pl.BlockSpec((tm, a.shape[1]), lambda i: (i, 0)) for a in tiled]
    in_specs += [pl.BlockSpec(a.shape, lambda i, nd=a.ndim: (0,) * nd) for a in full]
    out_specs = [pl.BlockSpec((tm, w), lambda i: (i, 0)) for w, _ in out_tiled]
    out_specs += [pl.BlockSpec(shp, lambda i, nd=len(shp): (0,) * nd) for shp, _ in out_acc]
    out_shape = [jax.ShapeDtypeStruct((S, w), dt) for w, dt in out_tiled]
    out_shape += [jax.ShapeDtypeStruct(shp, dt) for shp, dt in out_acc]
    return pl.pallas_call(
        kern, grid=(S // tm,), name=name, in_specs=in_specs, out_specs=out_specs, out_shape=out_shape,
        compiler_params=_params(("arbitrary",)),
    )(*smem, *tiled, *full)


def _rms_stats(x):
    r = lax.rsqrt(jnp.mean(x * x, axis=-1, keepdims=True) + EPS)
    return r, x * r


def _rmsnorm_fwd(x, g, name):
    def body(x_ref, g_ref, o_ref):
        _, xh = _rms_stats(x_ref[...])
        o_ref[...] = _bf(xh * g_ref[...])

    return _rowwise(body, tiled=[x], full=[g], out_tiled=[(x.shape[1], BF16)], out_acc=[], name=name)[0]


def _rms_bwd_tile(x, g, dxn):
    r, xh = _rms_stats(x)
    dg = jnp.sum(dxn * xh, axis=0, keepdims=True)
    dn = dxn * g
    dx = r * (dn - xh * jnp.mean(dn * xh, axis=-1, keepdims=True))
    return dx, dg


def _ple_and_loss(h2, p, target, w_pg, w_pp, g_ple, g_final):
    S, n = h2.shape
    tm = min(512, S)
    tn = 512

    def body(h2_ref, p_ref, t_ref, wpg_ref, wpp_ref, gple_ref, gfin_ref,
             n3_ref, dh_ref, dgl_ref, dpp_ref, loss_ref, dg_ref, dgple_ref, pp, gate, h3):
        @pl.when(pl.program_id(0) == 0)
        def _():
            loss_ref[...] = jnp.zeros_like(loss_ref)
            dg_ref[...] = jnp.zeros_like(dg_ref)
            dgple_ref[...] = jnp.zeros_like(dgple_ref)

        x = h2_ref[...]
        _, xh = _rms_stats(x)
        n3 = _bf(xh * gple_ref[...])
        n3_ref[...] = n3
        pb = _bf(p_ref[...])
        for c in range(n // tn):
            cols = slice(c * tn, (c + 1) * tn)
            pp[:, cols] = _dot(pb, wpp_ref[:, cols])
            gt = _sigmoid(_dot(n3, wpg_ref[:, cols]))
            gate[:, cols] = gt
            h3[:, cols] = x[:, cols] + gt * pp[:, cols]
        y = h3[...]
        _, yh = _rms_stats(y)
        e = yh * gfin_ref[...] - t_ref[...]
        per_tok = jnp.mean(e * e, axis=-1, keepdims=True)
        loss_ref[...] += 0.5 * jnp.sum(per_tok, axis=0, keepdims=True)
        dh, dg = _rms_bwd_tile(y, gfin_ref[...], e * (1.0 / n))
        dg_ref[...] += dg
        gt = gate[...]
        dgl = _bf(dh * pp[...] * gt * (1.0 - gt))
        dgl_ref[...] = dgl
        dpp_ref[...] = _bf(dh * gt)
        for c in range(n // tn):
            cols = slice(c * tn, (c + 1) * tn)
            h3[:, cols] = _dot_nt(dgl, wpg_ref[cols, :])
        dx, dgp = _rms_bwd_tile(x, gple_ref[...], h3[...])
        dh_ref[...] = dh + dx
        dgple_ref[...] += dgp

    tile = lambda width: pl.BlockSpec((tm, width), lambda i: (i, 0))
    whole = lambda a: pl.BlockSpec(a.shape, lambda i, nd=a.ndim: (0,) * nd)
    return pl.pallas_call(
        body, grid=(S // tm,), name="ple_and_loss",
        in_specs=[tile(n), tile(p.shape[1]), tile(n), whole(w_pg), whole(w_pp), whole(g_ple), whole(g_final)],
        out_specs=[tile(n), tile(n), tile(n), tile(n), pl.BlockSpec((1, 128), lambda i: (0, 0)),
                   pl.BlockSpec((1, n), lambda i: (0, 0)), pl.BlockSpec((1, n), lambda i: (0, 0))],
        out_shape=[jax.ShapeDtypeStruct((S, n), BF16), jax.ShapeDtypeStruct((S, n), F32), jax.ShapeDtypeStruct((S, n), BF16),
                   jax.ShapeDtypeStruct((S, n), BF16), jax.ShapeDtypeStruct((1, 128), F32), jax.ShapeDtypeStruct((1, n), F32),
                   jax.ShapeDtypeStruct((1, n), F32)],
        scratch_shapes=[pltpu.VMEM((tm, n), F32)] * 3,
        compiler_params=_params(("arbitrary",)),
    )(h2, p, target, w_pg, w_pp, g_ple, g_final)


def _rope_tables(S):
    half = 32
    inv = (1.0 / (np.float32(ROPE_THETA) ** (np.arange(half, dtype=np.float32) * np.float32(2.0 / 64)))).astype(np.float32)
    ang = np.arange(S).astype(np.float32)[:, None] * inv[None, :]
    cos, sin = np.cos(ang), np.sin(ang)
    return jnp.asarray(np.tile(cos, (1, 4))), jnp.asarray(np.concatenate([-sin, sin, -sin, sin], axis=1))


def _attn_common(i, kc, kp, vc, vp, cc, sc, cp, sp):
    lane = lax.broadcasted_iota(jnp.int32, (1, HEAD_PAIR), 1)
    lane_lo = jnp.bitwise_and(lane, 63) < 32
    slot = [lane < 64, lane >= 64]

    def swap_halves(t):
        return jnp.where(lane_lo, pltpu.roll(t, 96, 1), pltpu.roll(t, 32, 1))

    def rope(t, cos, sin):
        return t * cos + swap_halves(t) * sin

    def unrope(d, cos, sin):
        return d * cos + swap_halves(d * sin)

    k2 = jnp.concatenate([rope(kp, cp, sp), rope(kc, cc, sc)], axis=0)
    v2 = jnp.concatenate([vp, vc], axis=0)
    r = lax.broadcasted_iota(jnp.int32, (ATTN_BLOCK, 2 * ATTN_BLOCK), 0)
    c = lax.broadcasted_iota(jnp.int32, (ATTN_BLOCK, 2 * ATTN_BLOCK), 1)
    valid = (c > r) & (c <= r + ATTN_BLOCK) & jnp.logical_or(c >= ATTN_BLOCK, i > 0)
    ks, vs = {}, {}
    for j in range(2):
        kn = jnp.where(slot[j], k2, 0.0)
        vn = jnp.where(slot[j], v2, 0.0)
        for s in range(2):
            ks[j, s] = _bf(kn if s == j else pltpu.roll(kn, 64, 1))
            vs[j, s] = _bf(vn if s == j else pltpu.roll(vn, 64, 1))
    return slot, rope, unrope, valid, ks, vs


def _attn_probs(scores, valid, sink):
    s = jnp.where(valid, scores * 0.125, NEG)
    m = jnp.maximum(jnp.max(s, axis=1, keepdims=True), sink)
    e = jnp.exp(s - m)
    z = jnp.sum(e, axis=1, keepdims=True) + jnp.exp(sink - m)
    return e * (1.0 / z), m + jnp.log(z)


def _attn_specs(S):
    nb = S // ATTN_BLOCK
    prev = lambda i: jnp.maximum(i - 1, 0)
    blk = lambda w, col, row=(lambda i: i): pl.BlockSpec((ATTN_BLOCK, w), lambda i: (row(i), col))
    in_specs = [pl.BlockSpec(memory_space=pltpu.SMEM),
                blk(512, BLK_Q), blk(128, BLK_K), blk(128, BLK_K, prev), blk(128, BLK_V), blk(128, BLK_V, prev),
                blk(128, 0), blk(128, 0), blk(128, 0, prev), blk(128, 0, prev)]
    return nb, in_specs


def _attn_fwd(pa, cos, sin, sinks):
    S = pa.shape[0]
    nb, in_specs = _attn_specs(S)

    def body(sinks_ref, q_ref, kc_ref, kp_ref, vc_ref, vp_ref, cc_ref, sc_ref, cp_ref, sp_ref, o_ref, lse_ref):
        i = pl.program_id(0)
        lane = lax.broadcasted_iota(jnp.int32, (1, HEAD_PAIR), 1)
        cc, sc = cc_ref[...], sc_ref[...]
        _, rope, _, valid, ks, vs = _attn_common(i, kc_ref[...], kp_ref[...], vc_ref[...], vp_ref[...],
                                                 cc, sc, cp_ref[...], sp_ref[...])
        pair_cols = [slice(HEAD_PAIR * pair, HEAD_PAIR * (pair + 1)) for pair in range(4)]
        qps = [_bf(rope(q_ref[:, cols], cc, sc)) for cols in pair_cols]
        outs, lses = {}, {}

        def head_program(h):
            pair, s = divmod(h, 2)
            j = h // 4
            scores = _dot_nt(qps[pair], ks[j, s])
            yield
            p, lse = _attn_probs(scores, valid, sinks_ref[h])
            outs[h] = _dot(_bf(p), vs[j, s])
            lses[h] = jnp.where(lane == h, lse, 0.0)

        _interleave(head_program(h) for h in range(8))
        for pair, cols in enumerate(pair_cols):
            o_ref[:, cols] = outs[2 * pair] + outs[2 * pair + 1]
        lse_ref[...] = sum((lses[h] for h in range(1, 8)), lses[0])

    return pl.pallas_call(
        body, grid=(nb,), name="attn_fwd", in_specs=in_specs,
        out_specs=[pl.BlockSpec((ATTN_BLOCK, 512), lambda i: (i, 0)), pl.BlockSpec((ATTN_BLOCK, 128), lambda i: (i, 0))],
        out_shape=[jax.ShapeDtypeStruct((S, 512), F32), jax.ShapeDtypeStruct((S, 128), F32)],
        compiler_params=_params(("parallel",)),
    )(sinks, pa, pa, pa, pa, pa, cos, sin, cos, sin)


def _attn_bwd(pa, cos, sin, sinks, dcat, attn, lse):
    S = pa.shape[0]
    nb, in_specs = _attn_specs(S)
    in_specs = in_specs + [pl.BlockSpec((ATTN_BLOCK, 512), lambda i: (i, 0))] * 2 + [pl.BlockSpec((ATTN_BLOCK, 128), lambda i: (i, 0))]

    def body(sinks_ref, q_ref, kc_ref, kp_ref, vc_ref, vp_ref, cc_ref, sc_ref, cp_ref, sp_ref, do_ref, o_ref, lse_ref,
             dq_ref, dk_ref, dv_ref, dsink_ref):
        i = pl.program_id(0)

        @pl.when(i == 0)
        def _():
            dk_ref[...] = jnp.zeros_like(dk_ref)
            dv_ref[...] = jnp.zeros_like(dv_ref)
            dsink_ref[...] = jnp.zeros_like(dsink_ref)

        cc, sc, cp, sp = cc_ref[...], sc_ref[...], cp_ref[...], sp_ref[...]
        slot, rope, unrope, valid, ks, vs = _attn_common(i, kc_ref[...], kp_ref[...], vc_ref[...], vp_ref[...], cc, sc, cp, sp)
        pair_cols = [slice(HEAD_PAIR * pair, HEAD_PAIR * (pair + 1)) for pair in range(4)]
        qps = [_bf(rope(q_ref[:, cols], cc, sc)) for cols in pair_cols]
        dobs = [_bf(do_ref[:, cols]) for cols in pair_cols]
        do_o = [do_ref[:, cols] * o_ref[:, cols] for cols in pair_cols]
        dqs, dks, dvs = {}, {}, {}

        def head_program(h):
            pair, s = divmod(h, 2)
            j = h // 4
            qp, dob = qps[pair], dobs[pair]
            scores = _dot_nt(qp, ks[j, s])
            dp = _dot_nt(dob, vs[j, s])
            yield
            lse_h = lse_ref[:, h:h + 1]
            p = jnp.exp(jnp.where(valid, scores * 0.125, NEG) - lse_h)
            yield
            dr = jnp.sum(jnp.where(slot[s], do_o[pair], 0.0), axis=1, keepdims=True)
            ds = _bf(p * (dp - dr) * 0.125)
            yield
            dsink_ref[h:h + 1, :] += -jnp.sum(jnp.exp(sinks_ref[h] - lse_h) * dr, axis=0, keepdims=True)
            dqs[h] = _dot(ds, ks[j, s])
            dk_h = _dot_tn(ds, qp)
            dv_h = _dot_tn(_bf(p), dob)
            yield
            dk_h, dv_h = jnp.where(slot[s], dk_h, 0.0), jnp.where(slot[s], dv_h, 0.0)
            if s != j:
                dk_h, dv_h = pltpu.roll(dk_h, 64, 1), pltpu.roll(dv_h, 64, 1)
            dks[h], dvs[h] = dk_h, dv_h

        _interleave(head_program(h) for h in range(8))
        dk2 = sum((dks[h] for h in range(1, 8)), dks[0])
        dv2 = sum((dvs[h] for h in range(1, 8)), dvs[0])
        for pair, cols in enumerate(pair_cols):
            dq_ref[:, cols] = _bf(unrope(dqs[2 * pair] + dqs[2 * pair + 1], cc, sc))
        cur = pl.ds(pl.multiple_of(i * ATTN_BLOCK, ATTN_BLOCK), ATTN_BLOCK)
        dk_ref[cur, :] += unrope(dk2[ATTN_BLOCK:], cc, sc)
        dv_ref[cur, :] += dv2[ATTN_BLOCK:]

        @pl.when(i > 0)
        def _():
            prv = pl.ds(pl.multiple_of((i - 1) * ATTN_BLOCK, ATTN_BLOCK), ATTN_BLOCK)
            dk_ref[prv, :] += unrope(dk2[:ATTN_BLOCK], cp, sp)
            dv_ref[prv, :] += dv2[:ATTN_BLOCK]

    whole = lambda w: pl.BlockSpec((S, w), lambda i: (0, 0))
    return pl.pallas_call(
        body, grid=(nb,), name="attn_bwd", in_specs=in_specs,
        out_specs=[pl.BlockSpec((ATTN_BLOCK, 512), lambda i: (i, BLK_Q)), whole(128), whole(128),
                   pl.BlockSpec((8, 128), lambda i: (0, 0))],
        out_shape=[jax.ShapeDtypeStruct((S, D_IN_PAD), BF16), jax.ShapeDtypeStruct((S, 128), F32),
                   jax.ShapeDtypeStruct((S, 128), F32), jax.ShapeDtypeStruct((8, 128), F32)],
        compiler_params=_params(("arbitrary",)),
    )(sinks, pa, pa, pa, pa, pa, cos, sin, cos, sin, dcat, attn, lse)


CONV_ROWS = 512
CONV_PAD = 8


def _conv_silu(scr, w, r0):
    y = w[3:4, :] * scr[pl.ds(CONV_PAD + r0, CONV_ROWS), :]
    for j in range(DN_CONV - 1):
        y = y + w[j:j + 1, :] * scr[pl.ds(CONV_PAD + r0 - 3 + j, CONV_ROWS), :]
    return y


def _dn_prep_fwd(pd, conv_w):
    S = pd.shape[0]
    assert S % CONV_ROWS == 0

    def body(x_ref, w_ref, o_ref, scr):
        b = pl.program_id(0)
        scr[0:CONV_PAD, :] = jnp.zeros((CONV_PAD, DN_DIM), F32)
        scr[pl.ds(CONV_PAD, S), :] = x_ref[...]
        w = w_ref[...]
        q_scale = jnp.where(b < DN_HEADS, DN_DIM ** -0.5, 1.0)
        for r0 in range(0, S, CONV_ROWS):
            y = _conv_silu(scr, w, r0)
            a = y * _sigmoid(y)
            rs = lax.rsqrt(jnp.sum(a * a, axis=1, keepdims=True) + EPS)
            o_ref[pl.ds(r0, CONV_ROWS), :] = a * jnp.where(b < 2 * DN_HEADS, rs * q_scale, 1.0)

    col = pl.BlockSpec((S, DN_DIM), lambda b: (0, b))
    return pl.pallas_call(
        body, grid=(3 * DN_HEADS,), name="dn_prep_fwd",
        in_specs=[pl.BlockSpec((S, DN_DIM), lambda b: (0, BLK_DN + b)), pl.BlockSpec((DN_CONV, DN_DIM), lambda b: (0, b))],
        out_specs=col,
        out_shape=jax.ShapeDtypeStruct((S, 3 * DN_HEADS * DN_DIM), F32),
        scratch_shapes=[pltpu.VMEM((S + CONV_PAD, DN_DIM), F32)],
        compiler_params=_params(("parallel",)),
    )(pd, conv_w)


def _dn_prep_bwd(pd, conv_w, dqkv, dproj):
    S = pd.shape[0]

    def body(x_ref, w_ref, d_ref, _, dx_ref, dw_ref, scr, dscr):
        b = pl.program_id(0)
        scr[0:CONV_PAD, :] = jnp.zeros((CONV_PAD, DN_DIM), F32)
        scr[pl.ds(CONV_PAD, S), :] = x_ref[...]
        dscr[pl.ds(S, CONV_PAD), :] = jnp.zeros((CONV_PAD, DN_DIM), F32)
        w = w_ref[...]
        q_scale = jnp.where(b < DN_HEADS, DN_DIM ** -0.5, 1.0)
        is_qk = b < 2 * DN_HEADS
        dw = [jnp.zeros((1, DN_DIM), F32) for _ in range(DN_CONV)]
        for r0 in range(0, S, CONV_ROWS):
            y = _conv_silu(scr, w, r0)
            sg = _sigmoid(y)
            a = y * sg
            dout = d_ref[pl.ds(r0, CONV_ROWS), :]
            rs = lax.rsqrt(jnp.sum(a * a, axis=1, keepdims=True) + EPS)
            da_qk = q_scale * rs * (dout - a * (rs * rs) * jnp.sum(dout * a, axis=1, keepdims=True))
            dy = jnp.where(is_qk, da_qk, dout) * (sg * (1.0 + y * (1.0 - sg)))
            dscr[pl.ds(r0, CONV_ROWS), :] = dy
            for j in range(DN_CONV):
                dw[j] = dw[j] + jnp.sum(dy * scr[pl.ds(CONV_PAD + r0 - 3 + j, CONV_ROWS), :], axis=0, keepdims=True)
        for j in range(DN_CONV):
            dw_ref[j:j + 1, :] = dw[j]
        for r0 in range(0, S, CONV_ROWS):
            dx = w[3:4, :] * dscr[pl.ds(r0, CONV_ROWS), :]
            for j in range(DN_CONV - 1):
                dx = dx + w[j:j + 1, :] * dscr[pl.ds(r0 + 3 - j, CONV_ROWS), :]
            dx_ref[pl.ds(r0, CONV_ROWS), :] = _bf(dx)

    col = pl.BlockSpec((S, DN_DIM), lambda b: (0, b))
    proj_col = pl.BlockSpec((S, DN_DIM), lambda b: (0, BLK_DN + b))
    wcol = pl.BlockSpec((DN_CONV, DN_DIM), lambda b: (0, b))
    return pl.pallas_call(
        body, grid=(3 * DN_HEADS,), name="dn_prep_bwd",
        in_specs=[proj_col, wcol, col, pl.BlockSpec(memory_space=pl.ANY)], out_specs=[proj_col, wcol],
        out_shape=[jax.ShapeDtypeStruct(dproj.shape, dproj.dtype), jax.ShapeDtypeStruct((DN_CONV, 3 * DN_HEADS * DN_DIM), F32)],
        scratch_shapes=[pltpu.VMEM((S + CONV_PAD, DN_DIM), F32), pltpu.VMEM((S + CONV_PAD, DN_DIM), F32)],
        input_output_aliases={3: 0},
        compiler_params=_params(("parallel",)),
    )(pd, conv_w, dqkv, dproj)


CPAD = 128
CHUNKS_LOCAL = 4
CHUNKS_SCAN = 8


def _chunk_masks():
    ii = lax.broadcasted_iota(jnp.int32, (DN_CHUNK, CPAD), 0)
    jj = lax.broadcasted_iota(jnp.int32, (DN_CHUNK, CPAD), 1)
    return ii, jj


def _rows_pad(a):
    return jnp.concatenate([a, jnp.zeros_like(a)], axis=0)


def _hi_lo(a):
    hi = _bf(a)
    return hi, _bf(a - hi.astype(F32))


def _double_step(t, p):
    C = DN_CHUNK
    th, tl = _hi_lo(t)
    ph, pl_ = _hi_lo(p)
    r1 = _dot(jnp.concatenate([th, tl, ph, pl_], axis=0), _rows_pad(ph))
    r2 = _dot(jnp.concatenate([th, ph], axis=0), _rows_pad(pl_))
    return t + (r1[:C] + r1[C:2 * C] + r2[:C]), r1[2 * C:3 * C] + r1[3 * C:] + r2[C:]


def _dot3_nt(a, b):
    C = DN_CHUNK
    ah, al = _hi_lo(a)
    bh, bl = _hi_lo(b)
    r1 = _dot_nt(jnp.concatenate([ah, al], axis=0), _rows_pad(bh))
    return r1[:C] + r1[C:] + _dot_nt(ah, _rows_pad(bl))


def _dot3_tn(a, b):
    C = DN_CHUNK
    ah, al = _hi_lo(a)
    bh, bl = _hi_lo(b)
    return _dot_tn(jnp.concatenate([ah, al, ah], axis=0), jnp.concatenate([bh, bh, bl], axis=0))[:C]


def _interleave(programs):
    programs = list(programs)
    while programs:
        alive = []
        for prog in programs:
            try:
                next(prog)
                alive.append(prog)
            except StopIteration:
                pass
        programs = alive


def _col_to_row(col, ii, jj):
    return jnp.sum(jnp.where(ii == jj, col, 0.0), axis=0, keepdims=True)


def _row_to_col(row, ii, jj):
    return jnp.sum(jnp.where(ii == jj, row, 0.0), axis=1, keepdims=True)


def _decay(gc_col, ii, jj):
    diff = gc_col - _col_to_row(gc_col, ii, jj)
    return jnp.where(jj <= ii, jnp.exp(jnp.where(jj <= ii, diff, 0.0)), 0.0)


def _softplus(x):
    return jnp.maximum(x, 0.0) + jnp.log(1.0 + jnp.exp(-jnp.abs(x)))


def _head(h):
    return slice(DN_DIM * h, DN_DIM * (h + 1))


def _dn_chunk_fwd(qkv, pg, a_log, dt_bias):
    S = qkv.shape[0]
    C = DN_CHUNK
    G = CHUNKS_LOCAL
    R = G * C
    steps = S // R

    def body(alog_ref, dtb_ref, qkv_ref, pg_ref, w_ref, u_ref, qg_ref, kd_ref, a_ref, t_ref, gcs_ref):
        ii, jj = _chunk_masks()
        lane = lax.broadcasted_iota(jnp.int32, (1, 128), 1)
        eye = (ii == jj).astype(F32)
        gcs_parts = [[] for _ in range(G)]

        def head_program(chunk, h):
            rows = slice(chunk * C, (chunk + 1) * C)
            q, k, v = qkv_ref[rows, _head(h)], qkv_ref[rows, _head(DN_HEADS + h)], qkv_ref[rows, _head(2 * DN_HEADS + h)]
            beta = _sigmoid(pg_ref[rows, h:h + 1])
            g_col = -jnp.exp(alog_ref[h]) * _softplus(pg_ref[rows, DN_HEADS + h:DN_HEADS + h + 1] + dtb_ref[h])
            g_row = _col_to_row(g_col, ii, jj)
            gc_col = jnp.sum(jnp.where(jj <= ii, g_row, 0.0), axis=1, keepdims=True)
            dec = _decay(gc_col, ii, jj)
            eg = jnp.exp(gc_col)
            kb, vb = k * beta, v * beta
            k_rows = _rows_pad(_bf(k))
            kk = _dot_nt(_bf(kb), k_rows)
            qk = _dot_nt(_bf(q), k_rows)
            yield
            t, pw = eye, -jnp.where(jj < ii, kk * dec, 0.0)
            for _ in range(6):
                t, pw = _double_step(t, pw)
                yield
            tb = _bf(t)
            u_ref[rows, _head(h)] = _dot(tb, _rows_pad(_bf(vb)))
            w_ref[rows, _head(h)] = _bf(_dot(tb, _rows_pad(_bf(kb * eg))))
            a_ref[h, rows] = _bf(qk * dec)
            t_ref[h, rows] = t
            qg_ref[rows, _head(h)] = _bf(q * eg)
            kd_ref[rows, _head(h)] = _bf(k * jnp.exp(gc_col[C - 1:C, :] - gc_col))
            gcs_parts[chunk].append(jnp.where(lane == h, gc_col, 0.0) + jnp.where(lane == DN_HEADS + h, beta, 0.0)
                                    + jnp.where(lane == 2 * DN_HEADS + h, g_col, 0.0))

        _interleave(head_program(chunk, h) for chunk in range(G) for h in range(DN_HEADS))
        for chunk in range(G):
            gcs_ref[chunk * C:(chunk + 1) * C, :] = sum(gcs_parts[chunk][1:], gcs_parts[chunk][0])

    smem = pl.BlockSpec(memory_space=pltpu.SMEM)
    wide = pl.BlockSpec((R, 512), lambda n: (n, 0))
    sq = pl.BlockSpec((DN_HEADS, R, CPAD), lambda n: (0, n, 0))
    narrow = pl.BlockSpec((R, 128), lambda n: (n, 0))
    f = lambda *shp: jax.ShapeDtypeStruct(shp, F32)
    b = lambda *shp: jax.ShapeDtypeStruct(shp, BF16)
    return pl.pallas_call(
        body, grid=(steps,), name="dn_chunk_fwd",
        in_specs=[smem, smem, pl.BlockSpec((R, 1536), lambda n: (n, 0)), pl.BlockSpec((R, 128), lambda n: (n, BLK_G))],
        out_specs=[wide, wide, wide, wide, sq, sq, narrow],
        out_shape=[b(S, 512), f(S, 512), b(S, 512), b(S, 512), b(DN_HEADS, S, CPAD), f(DN_HEADS, S, CPAD), f(S, 128)],
        compiler_params=_params(("parallel",)),
    )(a_log, dt_bias, qkv, pg)


def _gated_norm(o, z, gn):
    r, oh = _rms_stats(o)
    return oh * gn * (z * _sigmoid(z))


def _dn_scan_fwd(w, u, qg, kd, a, gcs, pz, gn):
    S = w.shape[0]
    C = DN_CHUNK
    nc = S // C
    G = CHUNKS_SCAN
    R = G * C

    def body(w_ref, u_ref, qg_ref, kd_ref, a_ref, gcs_ref, z_ref, gn_ref, o_ref, vn_ref, sst_ref, out_ref, state):
        @pl.when(pl.program_id(0) == 0)
        def _():
            state[...] = jnp.zeros_like(state)

        def head_program(chunk, h):
            hs = _head(h)
            rows = slice(chunk * C, (chunk + 1) * C)
            s_in = state[h]
            sb = _bf(s_in)
            sst_ref[chunk, h] = sb
            w_s = _dot(w_ref[rows, hs], sb)
            q_s = _dot(qg_ref[rows, hs], sb)
            yield
            vn = u_ref[rows, hs] - w_s
            vnb = _bf(vn)
            o = q_s + _dot(a_ref[h, rows], _rows_pad(vnb))
            k_v = _dot_tn(kd_ref[rows, hs], vnb)
            yield
            state[h] = s_in * jnp.exp(gcs_ref[(chunk + 1) * C - 1:(chunk + 1) * C, h:h + 1]) + k_v
            o_ref[rows, hs] = o
            vn_ref[rows, hs] = vnb
            out_ref[rows, hs] = _bf(_gated_norm(o, z_ref[rows, hs], gn_ref[...]))

        for chunk in range(G):
            _interleave(head_program(chunk, h) for h in range(DN_HEADS))

    wide = pl.BlockSpec((R, 512), lambda n: (n, 0))
    f = lambda *shp: jax.ShapeDtypeStruct(shp, F32)
    b = lambda *shp: jax.ShapeDtypeStruct(shp, BF16)
    return pl.pallas_call(
        body, grid=(nc // G,), name="dn_scan_fwd",
        in_specs=[wide, wide, wide, wide, pl.BlockSpec((DN_HEADS, R, CPAD), lambda n: (0, n, 0)),
                  pl.BlockSpec((R, 128), lambda n: (n, 0)), pl.BlockSpec((R, 512), lambda n: (n, BLK_Z)),
                  pl.BlockSpec((1, DN_DIM), lambda n: (0, 0))],
        out_specs=[wide, wide, pl.BlockSpec((G, DN_HEADS, DN_DIM, DN_DIM), lambda n: (n, 0, 0, 0)), wide],
        out_shape=[f(S, 512), b(S, 512), b(nc, DN_HEADS, DN_DIM, DN_DIM), b(S, 512)],
        scratch_shapes=[pltpu.VMEM((DN_HEADS, DN_DIM, DN_DIM), F32)],
        compiler_params=_params(("arbitrary",)),
    )(w, u, qg, kd, a, gcs, pz, gn)


def _dn_scan_bwd(dcat, o, pz, gn, sst, vnew, w, qg, kd, a, gcs, dproj):
    S = o.shape[0]
    C = DN_CHUNK
    G = CHUNKS_SCAN
    R = G * C
    steps = S // R

    def body(dy_ref, o_ref, z_ref, gn_ref, sst_ref, vn_ref, w_ref, qg_ref, kd_ref, a_ref, gcs_ref, _,
             du_ref, dw_ref, dqg_ref, dkd_ref, da_ref, dz_ref, dsc_ref, dgn_ref, dstate):
        @pl.when(pl.program_id(0) == 0)
        def _():
            dstate[...] = jnp.zeros_like(dstate)
            dgn_ref[...] = jnp.zeros_like(dgn_ref)

        gn_ = gn_ref[...]
        lane = lax.broadcasted_iota(jnp.int32, (C, 128), 1)
        row = lax.broadcasted_iota(jnp.int32, (C, 128), 0)
        dgn_parts = []

        def head_program(chunk, h, dsc_parts):
            hs = _head(h)
            rows = slice(chunk * C, (chunk + 1) * C)
            ov, z, dout = o_ref[rows, hs], z_ref[rows, hs], dy_ref[rows, hs]
            r, oh = _rms_stats(ov)
            sg = _sigmoid(z)
            don = dout * (z * sg)
            dz_ref[rows, hs] = _bf(dout * (oh * gn_) * (sg * (1.0 + z * (1.0 - sg))))
            dgn_parts.append(jnp.sum(don * oh, axis=0, keepdims=True))
            dn = don * gn_
            do = _bf(r * (dn - oh * jnp.mean(dn * oh, axis=-1, keepdims=True)))
            sb = sst_ref[chunk, h]
            s_in = sb.astype(F32)
            ds_out = dstate[h]
            dsb = _bf(ds_out)
            vnb = vn_ref[rows, hs]
            wb, qgb, kdb, ab = w_ref[rows, hs], qg_ref[rows, hs], kd_ref[rows, hs], a_ref[h, rows]
            dvn = _dot_tn(ab, do)[:C] + _dot(kdb, dsb)
            yield
            da_ref[h, rows] = _dot_nt(do, _rows_pad(vnb))
            dqg_ref[rows, hs] = _dot_nt(do, sb)
            dkd_ref[rows, hs] = _dot_nt(vnb, dsb)
            q_do = _dot_tn(qgb, do)
            yield
            dvnb = _bf(dvn)
            dw_ref[rows, hs] = _bf(-_dot_nt(dvnb, sb))
            w_dvn = _dot_tn(wb, dvnb)
            du_ref[rows, hs] = dvnb
            yield
            d_last = jnp.exp(gcs_ref[(chunk + 1) * C - 1:(chunk + 1) * C, h:h + 1])
            dd = jnp.sum(jnp.sum(ds_out * s_in, axis=1, keepdims=True), axis=0, keepdims=True)
            dsc_parts.append(jnp.where((lane == h) & (row == C - 1), dd * d_last, 0.0))
            dstate[h] = ds_out * d_last + q_do - w_dvn

        for chunk in reversed(range(G)):
            dsc_parts = []
            _interleave(head_program(chunk, h, dsc_parts) for h in range(DN_HEADS))
            dsc_ref[chunk * C:(chunk + 1) * C, :] = sum(dsc_parts[1:], dsc_parts[0])
        dgn_ref[...] += sum(dgn_parts[1:], dgn_parts[0])

    rev = lambda n: steps - 1 - n
    wide = pl.BlockSpec((R, 512), lambda n: (rev(n), 0))
    z_spec = pl.BlockSpec((R, 512), lambda n: (rev(n), BLK_Z))
    sq = pl.BlockSpec((DN_HEADS, R, CPAD), lambda n: (0, rev(n), 0))
    narrow = pl.BlockSpec((R, 128), lambda n: (rev(n), 0))
    gn_spec = pl.BlockSpec((1, DN_DIM), lambda n: (0, 0))
    f = lambda *shp: jax.ShapeDtypeStruct(shp, F32)
    b = lambda *shp: jax.ShapeDtypeStruct(shp, BF16)
    return pl.pallas_call(
        body, grid=(steps,), name="dn_scan_bwd",
        in_specs=[pl.BlockSpec((R, 512), lambda n: (rev(n), 1)), wide, z_spec, gn_spec,
                  pl.BlockSpec((G, DN_HEADS, DN_DIM, DN_DIM), lambda n: (rev(n), 0, 0, 0)),
                  wide, wide, wide, wide, sq, narrow, pl.BlockSpec(memory_space=pl.ANY)],
        out_specs=[wide, wide, wide, wide, sq, z_spec, narrow, gn_spec],
        out_shape=[b(S, 512), b(S, 512), f(S, 512), f(S, 512), f(DN_HEADS, S, CPAD),
                   jax.ShapeDtypeStruct(dproj.shape, dproj.dtype), f(S, 128), f(1, DN_DIM)],
        scratch_shapes=[pltpu.VMEM((DN_HEADS, DN_DIM, DN_DIM), F32)],
        input_output_aliases={11: 5},
        compiler_params=_params(("arbitrary",)),
    )(dcat, o, pz, gn, sst, vnew, w, qg, kd, a, gcs, dproj)


def _dn_chunk_bwd(qkv, pg, t_inv, gcs, du, dw, dqg, dkd, da, dsc, a_log, dt_bias, dproj):
    S = qkv.shape[0]
    C = DN_CHUNK
    G = CHUNKS_LOCAL
    R = G * C

    def body(alog_ref, dtb_ref, qkv_ref, pg_ref, t_ref, gcs_ref, du_ref, dw_ref, dqg_ref, dkd_ref, da_ref, dsc_ref, _,
             dqkv_ref, dpg_ref, acc_ref):
        @pl.when(pl.program_id(0) == 0)
        def _():
            acc_ref[...] = jnp.zeros_like(acc_ref)

        ii, jj = _chunk_masks()
        lane = lax.broadcasted_iota(jnp.int32, (1, 128), 1)
        row8 = lax.broadcasted_iota(jnp.int32, (8, 128), 0)
        lane8 = lax.broadcasted_iota(jnp.int32, (8, 128), 1)
        rowc = lax.broadcasted_iota(jnp.int32, (C, 1), 0)
        tril, strict = jj <= ii, jj < ii
        dpg_parts, acc_parts = [[] for _ in range(G)], []

        def head_program(chunk, h):
            rows = slice(chunk * C, (chunk + 1) * C)
            q, k, v = qkv_ref[rows, _head(h)], qkv_ref[rows, _head(DN_HEADS + h)], qkv_ref[rows, _head(2 * DN_HEADS + h)]
            gc_col, beta, g_col = gcs_ref[rows, h:h + 1], gcs_ref[rows, DN_HEADS + h:DN_HEADS + h + 1], \
                gcs_ref[rows, 2 * DN_HEADS + h:2 * DN_HEADS + h + 1]
            dec = _decay(gc_col, ii, jj)
            eg = jnp.exp(gc_col)
            g_last = gc_col[C - 1:C, :]
            ek = jnp.exp(g_last - gc_col)
            kb, vb = k * beta, v * beta
            kbg = kb * eg
            qb, kbb = _bf(q), _bf(kb)
            k_rows = _rows_pad(_bf(k))
            t = t_ref[h, rows]
            tb = _bf(t)
            dub, dwb = du_ref[rows, _head(h)], dw_ref[rows, _head(h)]
            dqg_, dkd_ = dqg_ref[rows, _head(h)], dkd_ref[rows, _head(h)]
            dt = _dot_nt(dub, _rows_pad(_bf(vb))) + _dot_nt(dwb, _rows_pad(_bf(kbg)))
            t_du_dw = _dot_tn(tb, jnp.concatenate([dub, dwb], axis=1))
            dvb, dkbg = t_du_dw[:C, :DN_DIM], t_du_dw[:C, DN_DIM:]
            kk = _dot_nt(kbb, k_rows)
            qk = _dot_nt(qb, k_rows)
            yield
            dt_t = _dot3_nt(dt, t)
            yield
            dl = -_dot3_tn(t, dt_t)
            yield
            dm = jnp.where(strict, dl * dec, 0.0)
            dqk = jnp.where(tril, da_ref[h, rows] * dec, 0.0)
            gmat = dm * kk + dqk * qk
            dgc = jnp.sum(gmat, axis=1, keepdims=True) - _row_to_col(jnp.sum(gmat, axis=0, keepdims=True), ii, jj)
            dmb, dqkb = _bf(dm), _bf(dqk)
            yield
            dkb = _dot(dmb, k_rows) + dkbg * eg
            dk = _dot_tn(jnp.concatenate([dmb, dqkb], axis=0), jnp.concatenate([kbb, qb], axis=0))[:C] + dkd_ * ek
            dq = _dot(dqkb, k_rows) + dqg_ * eg
            yield
            tk = jnp.sum(dkd_ * k * ek, axis=1, keepdims=True)
            dgc = dgc + jnp.sum(dqg_ * q * eg, axis=1, keepdims=True) - tk + jnp.sum(dkbg * kbg, axis=1, keepdims=True)
            dgl = jnp.sum(tk, axis=0, keepdims=True) + dsc_ref[(chunk + 1) * C - 1:(chunk + 1) * C, h:h + 1]
            dgc = dgc + jnp.where(rowc == C - 1, dgl, 0.0)
            yield
            dk = dk + dkb * beta
            dbeta = jnp.sum(dkb * k, axis=1, keepdims=True) + jnp.sum(dvb * v, axis=1, keepdims=True)
            dqkv_ref[rows, _head(h)] = dq
            dqkv_ref[rows, _head(DN_HEADS + h)] = dk
            dqkv_ref[rows, _head(2 * DN_HEADS + h)] = dvb * beta
            dg_col = jnp.sum(jnp.where(jj >= ii, _col_to_row(dgc, ii, jj), 0.0), axis=1, keepdims=True)
            yield
            db = dbeta * beta * (1.0 - beta)
            da_in = dg_col * (-jnp.exp(alog_ref[h])) * _sigmoid(pg_ref[rows, DN_HEADS + h:DN_HEADS + h + 1] + dtb_ref[h])
            dpg_parts[chunk].append(jnp.where(lane == h, db, 0.0) + jnp.where(lane == DN_HEADS + h, da_in, 0.0))
            acc_parts.append(jnp.where((row8 == 0) & (lane8 == h), jnp.sum(dg_col * g_col, axis=0, keepdims=True), 0.0)
                             + jnp.where((row8 == 1) & (lane8 == h), jnp.sum(da_in, axis=0, keepdims=True), 0.0))

        _interleave(head_program(chunk, h) for chunk in range(G) for h in range(DN_HEADS))
        for chunk in range(G):
            dpg = sum(dpg_parts[chunk][1:], dpg_parts[chunk][0])
            dpg_ref[chunk * C:(chunk + 1) * C, :] = _bf(jnp.concatenate([dpg, jnp.zeros_like(dpg)], axis=1))
        acc_ref[...] += sum(acc_parts[1:], acc_parts[0])

    smem = pl.BlockSpec(memory_space=pltpu.SMEM)
    wide = pl.BlockSpec((R, 512), lambda n: (n, 0))
    sq = pl.BlockSpec((DN_HEADS, R, CPAD), lambda n: (0, n, 0))
    narrow = pl.BlockSpec((R, 128), lambda n: (n, 0))
    qkv_spec = pl.BlockSpec((R, 1536), lambda n: (n, 0))
    f = lambda *shp: jax.ShapeDtypeStruct(shp, F32)
    return pl.pallas_call(
        body, grid=(S // R,), name="dn_chunk_bwd",
        in_specs=[smem, smem, qkv_spec, pl.BlockSpec((R, 128), lambda n: (n, BLK_G)), sq, narrow, wide, wide, wide, wide, sq,
                  narrow, pl.BlockSpec(memory_space=pl.ANY)],
        out_specs=[qkv_spec, pl.BlockSpec((R, 256), lambda n: (n, BLK_G_PAD)), pl.BlockSpec((8, 128), lambda n: (0, 0))],
        out_shape=[f(S, 1536), jax.ShapeDtypeStruct(dproj.shape, dproj.dtype), f(8, 128)],
        input_output_aliases={12: 1},
        compiler_params=_params(("arbitrary",)),
    )(a_log, dt_bias, qkv, pg, t_inv, gcs, du, dw, dqg, dkd, da, dsc, dproj)


def _fill_kv(dk, dv, dproj):
    S = dk.shape[0]
    tm = min(512, S)

    def body(dk_ref, dv_ref, _, o_ref):
        o_ref[...] = _bf(jnp.concatenate([dk_ref[...], dv_ref[...]], axis=1))

    tile = pl.BlockSpec((tm, 128), lambda i: (i, 0))
    return pl.pallas_call(
        body, grid=(S // tm,), name="fill_kv",
        in_specs=[tile, tile, pl.BlockSpec(memory_space=pl.ANY)],
        out_specs=pl.BlockSpec((tm, 256), lambda i: (i, BLK_KV)),
        out_shape=jax.ShapeDtypeStruct(dproj.shape, dproj.dtype),
        input_output_aliases={2: 0},
        compiler_params=_params(("parallel",)),
    )(dk, dv, dproj)


def _w_in_to_internal(wt):
    return jnp.concatenate([wt[0:512], wt[2304:2816], wt[768:2304], wt[512:768], wt[2816:2824],
                            jnp.zeros((D_IN_PAD - D_IN, wt.shape[1]), wt.dtype)], axis=0)


def _w_in_from_internal(gt):
    return jnp.concatenate([gt[0:512], gt[2560:2816], gt[1024:2560], gt[512:1024], gt[2816:2824]], axis=0)


def _local_step(x, p, target, wts, first_weights, other_weights, ship_early):
    S = x.shape[0]
    cos, sin = _rope_tables(S)
    sinks, a_log, dt_bias = wts["sinks"].reshape(8), wts["a_log"].reshape(4), wts["dt_bias"].reshape(4)
    gn = wts["dn_norm"].reshape(1, DN_DIM)
    add = lambda acc, res: (acc + res,)

    u = _rmsnorm_fwd(x, wts["norm_mix"], "norm_mix_fwd")
    w_in_t, conv_w = first_weights(u)
    proj, = _mm(u, w_in_t, form="nt", name="in_proj", out_dtypes=[F32], tn=512)
    attn, lse = _attn_fwd(proj, cos, sin, sinks)
    qkv = _dn_prep_fwd(proj, conv_w)
    cw, cu, cqg, ckd, ca, ct, gcs = _dn_chunk_fwd(qkv, proj, a_log, dt_bias)
    o, vnew, sst, dn_out = _dn_scan_fwd(cw, cu, cqg, ckd, ca, gcs, proj, gn)
    w_o, = other_weights(("w_o",), dn_out)
    h1, = _mm([attn, dn_out], w_o, form="nn", name="out_proj", out_dtypes=[F32], tn=512, epi=add, extra=[x])

    w_up, w_down = other_weights(("w_up", "w_down"), h1)
    hid, relu, m, h2 = _mlp_fwd(h1, w_up, w_down, wts["norm_mlp"])
    w_pg, w_pp = other_weights(("w_ple_gate", "w_ple_proj"), h2)
    n3, dh2, dgl, dpp, loss, d_norm_final, d_norm_ple = _ple_and_loss(h2, p, target, w_pg, w_pp, wts["norm_ple"],
                                                                     wts["norm_final"].reshape(1, D_MODEL))
    g = {"norm_final": d_norm_final, "norm_ple": d_norm_ple}
    early = {"w_ple_gate": _mm_tn(n3, dgl, name="d_w_ple_gate", tm=512, tn=1024, out_dtype=BF16).reshape(N_DEV, 128, 1024),
             "w_ple_proj": _mm_tn(p, dpp, name="d_w_ple_proj", tm=256, tn=128, out_dtype=BF16, column_shards=True)}
    d_act, = _mm(dh2, w_down, form="nt", name="d_hidden", out_dtypes=[BF16], tn=512,
                 epi=lambda acc, r: (acc * (2.0 * r.astype(F32)),), extra=[relu])
    early["w_down"] = _mm_tn(hid, dh2, name="d_w_down", tm=512, tn=1024, out_dtype=BF16).reshape(N_DEV, 512, 1024)
    early["w_up"] = _mm_tn(m, d_act, name="d_w_up", tm=1024, tn=512, out_dtype=BF16, column_shards=True)
    token = ship_early(early)
    dh1, g["norm_mlp"], dcat = _mm(d_act, w_up, form="nt", name="d_m", out_dtypes=[F32], tn=512, after=token,
                                   norm_bwd=(h1, wts["norm_mlp"], dh2), then_nt=w_o)
    d_w_o = jnp.concatenate([_mm_tn(attn, dh1, name="d_w_o_attn", tm=512, tn=512, out_dtype=BF16),
                             _mm_tn(dn_out, dh1, name="d_w_o_dn", tm=512, tn=512, out_dtype=BF16)], axis=0)
    token = ship_early({"w_o": d_w_o.reshape(N_DEV, 128, 1024)})
    dproj, dk, dv, dsinks = _attn_bwd(proj, cos, sin, sinks + token[0, 0], dcat, attn, lse)
    g["sinks"] = dsinks[:, 0].reshape(1, 8)
    du_, dw_, dqg, dkd, da, dproj, dsc, g["dn_norm"] = _dn_scan_bwd(dcat, o, proj, gn, sst, vnew, cw, cqg, ckd, ca, gcs, dproj)
    dqkv, dproj, gate_acc = _dn_chunk_bwd(qkv, proj, ct, gcs, du_, dw_, dqg, dkd, da, dsc, a_log, dt_bias, dproj)
    g["a_log"], g["dt_bias"] = gate_acc[0:1, 0:4], gate_acc[1:2, 0:4]
    dproj, g["conv_w"] = _dn_prep_bwd(proj, conv_w, dqkv, dproj)
    dproj = _fill_kv(dk, dv, dproj)
    token = ship_early({"w_in": _mm_tn(dproj, u, name="d_w_in", tm=512, tn=1024, out_dtype=BF16)})
    grad_x, g["norm_mix"] = _mm(dproj, w_in_t, form="nn", name="d_u", out_dtypes=[F32], tn=512, after=token,
                                norm_bwd=(x, wts["norm_mix"], dh1))
    return loss, grad_x, g


def _peer(k):
    x, y, c = lax.axis_index("x"), lax.axis_index("y"), lax.axis_index("c")
    px = 1 - x if k & 4 else x
    py = 1 - y if k & 2 else y
    pc = 1 - c if k & 1 else c
    return (px, py, pc), 4 * px + 2 * py + pc


def _exchange(srcs, name, gather):
    n = len(srcs)
    gathers = list(gather) if isinstance(gather, (list, tuple)) else [gather] * n
    shapes = [(N_DEV,) + s.shape if gt else s.shape for s, gt in zip(srcs, gathers)]

    def body(*refs):
        src_refs, out_refs = refs[:n], refs[n:2 * n]
        send_sems, recv_sems, local_sems = refs[2 * n:]
        _, me = _peer(0)
        piece = lambda a, d: src_refs[a] if gathers[a] else src_refs[a].at[d]
        local = [pltpu.make_async_copy(piece(a, me), out_refs[a].at[me], local_sems.at[a]) for a in range(n)]
        for cp in local:
            cp.start()
        copies = []
        for a in range(n):
            for k in range(1, N_DEV):
                dev, idx = _peer(k)
                cp = pltpu.make_async_remote_copy(src_ref=piece(a, idx), dst_ref=out_refs[a].at[me],
                                                  send_sem=send_sems.at[a, k - 1], recv_sem=recv_sems.at[a, k - 1],
                                                  device_id=dev, device_id_type=MESH)
                cp.start()
                copies.append(cp)
        for cp in copies:
            cp.wait_recv()
        for cp in copies:
            cp.wait_send()
        for cp in local:
            cp.wait()

    anywhere = pl.BlockSpec(memory_space=pl.ANY)
    return pl.pallas_call(
        body, name=name, in_specs=[anywhere] * n, out_specs=[anywhere] * n,
        out_shape=[jax.ShapeDtypeStruct(shp, s.dtype) for shp, s in zip(shapes, srcs)],
        scratch_shapes=[pltpu.SemaphoreType.DMA((n, N_DEV - 1)), pltpu.SemaphoreType.DMA((n, N_DEV - 1)),
                        pltpu.SemaphoreType.DMA((n,))],
    )(*srcs)


_HBM = pl.BlockSpec(memory_space=pltpu.HBM)
_SEM = pl.BlockSpec(memory_space=pltpu.SEMAPHORE)
_EFFECT = pltpu.SideEffectType.DATAFLOW_SIDE_EFFECTING


def _split_copies(src_refs, land_refs, send_sems, recv_sems, modes, which=None):
    _, me = _peer(0)
    copies = []
    which = range(len(src_refs)) if which is None else which
    for a, src, land in zip(which, src_refs, land_refs):
        if modes[a] == "columns":
            n_cols = src.shape[1]
            dst = land.at[:, pl.ds(pl.multiple_of(me * n_cols, n_cols), n_cols)]
        else:
            dst = land.at[me]
        for k in range(1, N_DEV):
            dev, idx = _peer(k)
            sem = a * (N_DEV - 1) + k - 1
            copies.append(pltpu.make_async_remote_copy(
                src_ref=src.at[idx] if modes[a] == "pieces" else src, dst_ref=dst, send_sem=send_sems.at[sem],
                recv_sem=recv_sems.at[sem], device_id=dev, device_id_type=MESH))
    return copies


def _exchange_start(srcs, name, modes):
    n = len(srcs)
    modes = [modes] * n if isinstance(modes, str) else list(modes)
    me = 4 * lax.axis_index("x") + 2 * lax.axis_index("y") + lax.axis_index("c")
    lands = []
    for s, mode in zip(srcs, modes):
        if mode == "columns":
            empty = lax.empty((s.shape[0], N_DEV * s.shape[1]), s.dtype)
            lands.append(lax.dynamic_update_slice(empty, s, (0, me * s.shape[1])))
        else:
            own = s if mode == "slots" else lax.dynamic_index_in_dim(s, me, 0, keepdims=False)
            shape = (N_DEV,) + s.shape if mode == "slots" else s.shape
            lands.append(lax.dynamic_update_index_in_dim(lax.empty(shape, s.dtype), own, me, 0))

    def body(*refs):
        src_refs, land_refs = refs[:n], refs[n:2 * n]
        send_sems, recv_sems = refs[2 * n], refs[2 * n + 1]
        for cp in _split_copies(src_refs, land_refs, send_sems, recv_sems, modes):
            cp.start()
        refs[-1][...] = jnp.zeros_like(refs[-1])

    both = list(srcs) + lands
    sems = pltpu.SemaphoreType.DMA((n * (N_DEV - 1),))
    out = pl.pallas_call(
        body, name=name,
        out_shape=(sems, sems, *[pltpu.HBM(t.shape, t.dtype) for t in both], jax.ShapeDtypeStruct((8, 128), F32)),
        in_specs=[_HBM] * (2 * n), out_specs=(_SEM, _SEM, *[_HBM] * (2 * n), pl.BlockSpec(memory_space=pltpu.VMEM)),
        input_output_aliases={i: 2 + i for i in range(2 * n)},
        compiler_params=pltpu.CompilerParams(has_side_effects=_EFFECT),
    )(*[pltpu.with_memory_space_constraint(t, pltpu.HBM) for t in both])
    return (n, modes, out[:-1]), out[-1]


def _exchange_wait(handle, after, name, which=None):
    n_all, modes, (send_sems, recv_sems, *both_all) = handle
    which = list(range(n_all)) if which is None else list(which)
    n = len(which)
    both = [both_all[a] for a in which] + [both_all[n_all + a] for a in which]

    def body(*refs):
        src_refs, land_refs = refs[:n], refs[n:2 * n]
        for cp in _split_copies(src_refs, land_refs, refs[2 * n], refs[2 * n + 1], modes, which):
            cp.wait_send()
            cp.wait_recv()

    out = pl.pallas_call(
        body, name=name, out_shape=tuple(pltpu.HBM(t.shape, t.dtype) for t in both),
        in_specs=[_HBM] * (2 * n) + [_SEM, _SEM, pl.BlockSpec(memory_space=pl.ANY)], out_specs=tuple([_HBM] * (2 * n)),
        input_output_aliases={i: i for i in range(2 * n)},
        compiler_params=pltpu.CompilerParams(has_side_effects=_EFFECT),
    )(*both, send_sems, recv_sems, after)
    return list(out[n:])


def _adam_update(g, w, m, v):
    nm = ADAM_B1 * m + (1.0 - ADAM_B1) * g
    nv = ADAM_B2 * v + (1.0 - ADAM_B2) * (g * g)
    m_hat = nm / (1.0 - ADAM_B1 ** ADAM_STEP)
    v_hat = nv / (1.0 - ADAM_B2 ** ADAM_STEP)
    return -ADAM_LR * (m_hat / (jnp.sqrt(v_hat) + ADAM_EPS) + ADAM_WD * w), nm, nv


def _adamw(parts, w, m, v, name):
    n, R, W = parts.shape
    tm = 128 if R % 128 == 0 else R

    def body(p_ref, w_ref, m_ref, v_ref, g_ref, d_ref, nm_ref, nv_ref):
        g = p_ref[0].astype(F32)
        for s in range(1, n):
            g = g + p_ref[s].astype(F32)
        g_ref[...] = g
        d_ref[...], nm_ref[...], nv_ref[...] = _adam_update(g, w_ref[...], m_ref[...], v_ref[...])

    tile = pl.BlockSpec((tm, W), lambda i: (i, 0))
    return pl.pallas_call(
        body, grid=(R // tm,), name=name,
        in_specs=[pl.BlockSpec((n, tm, W), lambda i: (0, i, 0)), tile, tile, tile],
        out_specs=[tile] * 4, out_shape=[jax.ShapeDtypeStruct((R, W), F32)] * 4,
        compiler_params=_params(("parallel",)),
    )(parts, w, m, v)


_MATRICES = ("w_in", "w_o", "w_up", "w_down", "w_ple_gate", "w_ple_proj")


_OTHERS = ("w_o", "w_up", "w_down", "w_ple_gate", "w_ple_proj")
_OTHER_MODES = {"w_o": "slots", "w_up": "slots", "w_down": "slots", "w_ple_gate": "slots", "w_ple_proj": "columns"}


_VECTORS = ("norm_mix", "norm_mlp", "norm_ple", "norm_final", "a_log", "dt_bias", "sinks", "dn_norm")
_SMALL_ROWS, _LOSS_ROW, _CONV_ROW = 16, 8, 9


def _pack_small(vectors, loss, conv):
    def body(*refs):
        out = refs[-1]
        out[...] = jnp.zeros_like(out)
        for r, ref in enumerate(refs[:len(_VECTORS)]):
            out[r:r + 1, 0:ref.shape[1]] = ref[...]
        out[_LOSS_ROW:_LOSS_ROW + 1, 0:128] = refs[len(_VECTORS)][...]
        out[_CONV_ROW:_CONV_ROW + 6, :] = refs[len(_VECTORS) + 1][...]

    return pl.pallas_call(body, name="pack_small", out_shape=jax.ShapeDtypeStruct((_SMALL_ROWS, 1024), F32))(*vectors, loss, conv)


def _sum_slots(parts):
    def body(p_ref, o_ref):
        acc = p_ref[0]
        for s in range(1, parts.shape[0]):
            acc = acc + p_ref[s]
        o_ref[...] = acc

    return pl.pallas_call(body, name="sum_small", out_shape=jax.ShapeDtypeStruct(parts.shape[1:], parts.dtype))(parts)


def _adamw_vectors(summed, conv_g, wmv):
    names = _VECTORS + ("conv_w",)
    flat = [a for triple in wmv for a in triple]

    def body(*refs):
        sum_ref, conv_ref = refs[0], refs[1]
        ins, outs = refs[2:2 + len(flat)], refs[2 + len(flat):]
        for i in range(len(names)):
            w_ref, m_ref, v_ref = ins[3 * i:3 * i + 3]
            g = conv_ref[...] if i == len(_VECTORS) else sum_ref[i:i + 1, 0:w_ref.shape[1]]
            outs[4 * i][...] = g
            outs[4 * i + 1][...], outs[4 * i + 2][...], outs[4 * i + 3][...] = _adam_update(g, w_ref[...], m_ref[...], v_ref[...])

    out_shape = [jax.ShapeDtypeStruct(t[0].shape, F32) for t in wmv for _ in range(4)]
    res = pl.pallas_call(body, name="adamw_vectors", out_shape=out_shape)(summed, conv_g, *flat)
    return {n: res[4 * i:4 * i + 4] for i, n in enumerate(names)}


_ORDER = ("norm_mix", "w_in", "conv_w", "a_log", "dt_bias", "dn_norm", "sinks", "w_o", "norm_mlp", "w_up", "w_down",
          "norm_ple", "w_ple_gate", "w_ple_proj", "norm_final")


def kernel(x, p, norm_mix, w_in, conv_w, a_log, dt_bias, dn_norm, sinks, w_o, norm_mlp, w_up, w_down, norm_ple, w_ple_gate, w_ple_proj, norm_final, loss_target, m_norm_mix, m_w_in, m_conv_w, m_a_log, m_dt_bias, m_dn_norm, m_sinks, m_w_o, m_norm_mlp, m_w_up, m_w_down, m_norm_ple, m_w_ple_gate, m_w_ple_proj, m_norm_final, v_norm_mix, v_w_in, v_conv_w, v_a_log, v_dt_bias, v_dn_norm, v_sinks, v_w_o, v_norm_mlp, v_w_up, v_w_down, v_norm_ple, v_w_ple_gate, v_w_ple_proj, v_norm_final):
    w = dict(norm_mix=norm_mix, w_in=w_in[0], conv_w=conv_w[0], a_log=a_log, dt_bias=dt_bias, dn_norm=dn_norm, sinks=sinks,
             w_o=w_o[0], norm_mlp=norm_mlp, w_up=w_up[0], w_down=w_down[0], norm_ple=norm_ple, w_ple_gate=w_ple_gate[0],
             w_ple_proj=w_ple_proj[0], norm_final=norm_final)
    m = dict(norm_mix=m_norm_mix, w_in=m_w_in[0], conv_w=m_conv_w[0], a_log=m_a_log, dt_bias=m_dt_bias, dn_norm=m_dn_norm,
             sinks=m_sinks, w_o=m_w_o[0], norm_mlp=m_norm_mlp, w_up=m_w_up[0], w_down=m_w_down[0], norm_ple=m_norm_ple,
             w_ple_gate=m_w_ple_gate[0], w_ple_proj=m_w_ple_proj[0], norm_final=m_norm_final)
    v = dict(norm_mix=v_norm_mix, w_in=v_w_in[0], conv_w=v_conv_w[0], a_log=v_a_log, dt_bias=v_dt_bias, dn_norm=v_dn_norm,
             sinks=v_sinks, w_o=v_w_o[0], norm_mlp=v_norm_mlp, w_up=v_w_up[0], w_down=v_w_down[0], norm_ple=v_norm_ple,
             w_ple_gate=v_w_ple_gate[0], w_ple_proj=v_w_ple_proj[0], norm_final=v_norm_final)
    me = 4 * lax.axis_index("x") + 2 * lax.axis_index("y") + lax.axis_index("c")
    conv_shard = conv_w.shape[2]

    for d in (w, m, v):
        d["w_in"] = d["w_in"].T
    conv_pad = jnp.pad(w["conv_w"], ((0, 8 - DN_CONV), (0, 256 - conv_shard)))
    gathers, token_gather = _exchange_start([_bf(w["w_in"]), conv_pad] + [_bf(w[n]) for n in _OTHERS], "gather_start",
                                            ["slots", "slots"] + [_OTHER_MODES[n] for n in _OTHERS])
    vectors = dict(w)
    vectors["norm_mix"] = w["norm_mix"] + token_gather[0:1, 0:1]

    def first_weights(after):
        w_in_all, conv_all = _exchange_wait(gathers, after, "gather_first_wait", [0, 1])
        conv_all = jnp.transpose(conv_all[:, :DN_CONV, :conv_shard], (1, 0, 2)).reshape(DN_CONV, N_DEV * conv_shard)
        return _w_in_to_internal(w_in_all.reshape(D_IN, D_MODEL)), conv_all

    as_taken = {"w_o": lambda t: t.reshape(1024, 1024), "w_up": lambda t: t, "w_down": lambda t: t.reshape(4096, 1024),
                "w_ple_gate": lambda t: t.reshape(1024, 1024), "w_ple_proj": lambda t: t}

    def other_weights(names, after):
        which = [2 + _OTHERS.index(n) for n in names]
        got = _exchange_wait(gathers, after, "gather_wait_" + names[0], which)
        return [as_taken[n](t) for n, t in zip(names, got)]

    shipped = []

    def ship_early(pieces):
        names = tuple(pieces)
        if names == ("w_in",):
            pieces = {"w_in": _w_in_from_internal(pieces["w_in"]).reshape(N_DEV, D_IN // N_DEV, D_MODEL)}
        handle, token = _exchange_start([pieces[n] for n in names], "scatter_start_" + names[0], "pieces")
        shipped.append((names, handle))
        return token

    loss, grad_x, g = _local_step(x[0], p[0, 0], loss_target[0], vectors, first_weights, other_weights, ship_early)

    row = lambda t: t.reshape(1, t.size)
    small = _pack_small([row(g[n]) for n in _VECTORS], loss, g["conv_w"].reshape(6, 1024))
    small_handle, token_small = _exchange_start([small], "gather_small_start", "slots")
    big, after = {}, token_small
    for names, handle in shipped[:-1]:
        for n, r in zip(names, _exchange_wait(handle, after, "scatter_wait_" + names[0])):
            big[n] = _adamw(r, w[n], m[n], v[n], "adamw_" + n)
            after = big[n][1]
    small_all, = _exchange_wait(small_handle, after, "gather_small_wait")
    summed = _sum_slots(small_all)
    conv_g = lax.dynamic_slice(summed[_CONV_ROW:_CONV_ROW + 6].reshape(DN_CONV, N_DEV * conv_shard), (0, me * conv_shard),
                               (DN_CONV, conv_shard))
    small_out = _adamw_vectors(summed, conv_g, [(row(w[n]), row(m[n]), row(v[n])) for n in _VECTORS]
                               + [(w["conv_w"], m["conv_w"], v["conv_w"])])
    names, handle = shipped[-1]
    for n, r in zip(names, _exchange_wait(handle, small_out["conv_w"][0], "scatter_wait_" + names[0])):
        big[n] = _adamw(r, w[n], m[n], v[n], "adamw_" + n)

    result = [summed[_LOSS_ROW, 0], grad_x[None]]
    for i in range(4):
        for n in _ORDER:
            if n == "w_in":
                result.append(big[n][i].T[None])
            elif n in _MATRICES:
                result.append(big[n][i][None])
            elif n == "conv_w":
                result.append(small_out[n][i][None])
            else:
                result.append(small_out[n][i].reshape(w[n].shape))
    return tuple(result)
```

```python
import jax
import jax.numpy as jnp
import numpy as np
from jax import lax
from jax.experimental import pallas as pl
from jax.experimental.pallas import tpu as pltpu

F32, BF16 = jnp.float32, jnp.bfloat16
EPS = 1e-6
D_MODEL = 1024
N_DEV = 8
ATTN_BLOCK = 128
HEAD_PAIR = 128
DN_HEADS = 4
DN_DIM = 128
DN_CHUNK = 64
DN_CONV = 4
ROPE_THETA = 10000.0
D_IN = 2824
D_IN_PAD = 3072
BLK_Q, BLK_Z = 0, 1
BLK_DN, BLK_K, BLK_V, BLK_G = 8, 20, 21, 22
BLK_KV, BLK_G_PAD = 10, 11
VMEM_LIMIT = 56 * 1024 * 1024
NEG = -1e30
ADAM_LR, ADAM_B1, ADAM_B2, ADAM_EPS, ADAM_WD, ADAM_STEP = 0.001, 0.9, 0.999, 1e-08, 0.01, 10
MESH = pl.DeviceIdType.MESH


def _bf(x):
    return x.astype(BF16)


def _dot(a, b):
    return jnp.dot(a, b, preferred_element_type=F32)


def _dot_nt(a, b):
    return lax.dot_general(a, b, (((1,), (1,)), ((), ())), preferred_element_type=F32)


def _dot_tn(a, b):
    return lax.dot_general(a, b, (((0,), (0,)), ((), ())), preferred_element_type=F32)


def _sigmoid(x):
    return 1.0 / (1.0 + jnp.exp(-x))


def _params(sem):
    return pltpu.CompilerParams(dimension_semantics=sem, vmem_limit_bytes=VMEM_LIMIT)


def _mm(x, w, *, form, name, out_dtypes, tn, epi=None, extra=(), tm=512, w_row_block=0, after=None, norm=None,
        norm_bwd=None, then_nt=None):
    assert norm is None or norm_bwd is None
    xs = list(x) if isinstance(x, (list, tuple)) else [x]
    nx = len(xs)
    S, K = xs[0].shape
    shards = w.ndim == 3
    N = (w.shape[2] * N_DEV if shards else w.shape[1]) if form == "nn" else w.shape[-2]
    assert not (shards and form == "nn" and tn != w.shape[2]) and (nx == 1 or (form == "nn" and not shards and norm is None))
    r0 = w_row_block * K
    tm = min(tm, S)
    n_extra, n_out = len(extra), len(out_dtypes)
    tile = lambda width: pl.BlockSpec((tm, width), lambda i: (i, 0))
    whole = lambda a: pl.BlockSpec(a.shape, lambda i, nd=a.ndim: (0,) * nd)
    ins, in_specs = [*xs, w, *extra], [tile(K)] * nx + [whole(w)] + [tile(N)] * n_extra
    if norm is not None:
        ins, in_specs = ins + [norm], in_specs + [whole(norm)]
    if norm_bwd is not None:
        ins, in_specs = ins + list(norm_bwd), in_specs + [tile(N), whole(norm_bwd[1]), tile(N)]
    if then_nt is not None:
        ins, in_specs = ins + [then_nt], in_specs + [whole(then_nt)]
    if after is not None:
        ins, in_specs = ins + [after], in_specs + [whole(after)]
    out_shape = [jax.ShapeDtypeStruct((S, N), dt) for dt in out_dtypes]
    out_specs = [tile(N)] * n_out
    if norm is not None:
        out_shape, out_specs = out_shape + [jax.ShapeDtypeStruct((S, K), BF16)], out_specs + [tile(K)]
    if norm_bwd is not None:
        out_shape, out_specs = out_shape + [jax.ShapeDtypeStruct((1, N), F32)], out_specs + [pl.BlockSpec((1, N), lambda i: (0, 0))]
    if then_nt is not None:
        out_shape, out_specs = out_shape + [jax.ShapeDtypeStruct((S, then_nt.shape[0]), F32)], out_specs + [tile(then_nt.shape[0])]

    def product(xb, w_ref, cols, c):
        if form == "nn" and nx > 1:
            return sum(_dot(part, w_ref[r0 + p * K:r0 + (p + 1) * K, cols]) for p, part in enumerate(xb))
        if form == "nn":
            return _dot(xb, w_ref[c] if shards else w_ref[r0:r0 + K, cols])
        if not shards:
            return _dot_nt(xb, w_ref[cols, :])
        ks = w.shape[2]
        acc = _dot_nt(xb[:, 0:ks], w_ref[0, cols, :])
        for s in range(1, N_DEV):
            acc = acc + _dot_nt(xb[:, s * ks:(s + 1) * ks], w_ref[s, cols, :])
        return acc

    def body(*refs):
        x_ref, w_ref = refs[0], refs[nx]
        extra_refs = refs[nx + 1:nx + 1 + n_extra]
        at = nx + 1 + n_extra
        if norm is not None:
            gain_ref, at = refs[at], at + 1
        if norm_bwd is not None:
            (y_ref, ygain_ref, dres_ref), at = refs[at:at + 3], at + 3
        if then_nt is not None:
            w2_ref, at = refs[at], at + 1
        outs = refs[len(ins):]
        if norm is not None:
            _, xh = _rms_stats(x_ref[...])
            xb = _bf(xh * gain_ref[...])
            outs[n_out][...] = xb
        else:
            xb = _bf(x_ref[...]) if nx == 1 else [_bf(r[...]) for r in refs[:nx]]
        for c in range(N // tn):
            cols = slice(c * tn, (c + 1) * tn)
            acc = product(xb, w_ref, cols, c)
            res = epi(acc, *[r[:, cols] for r in extra_refs]) if epi else (acc,)
            for o, r in zip(outs[:n_out], res):
                o[:, cols] = r.astype(o.dtype)
        if norm_bwd is not None:
            dx, dg = _rms_bwd_tile(y_ref[...], ygain_ref[...], outs[0][...])
            outs[0][...] = dres_ref[...] + dx
            dg_ref = outs[n_out]

            @pl.when(pl.program_id(0) == 0)
            def _():
                dg_ref[...] = jnp.zeros_like(dg_ref)

            dg_ref[...] += dg
        if then_nt is not None:
            yb = _bf(outs[0][...])
            for c in range(then_nt.shape[0] // tn):
                cols = slice(c * tn, (c + 1) * tn)
                outs[-1][:, cols] = _dot_nt(yb, w2_ref[cols, :])

    return pl.pallas_call(
        body, grid=(S // tm,), name=name, in_specs=in_specs, out_specs=out_specs, out_shape=out_shape,
        compiler_params=_params(("arbitrary",) if norm_bwd is not None else ("parallel",)),
    )(*ins)


def _mlp_fwd(h1, w_up, w_down, gain):
    S, K = h1.shape
    n_sh, _, fs = w_up.shape
    tm = min(512, S)

    def body(x_ref, wup_ref, wdown_ref, g_ref, hid_ref, relu_ref, m_ref, h2_ref):
        x = x_ref[...]
        _, xh = _rms_stats(x)
        mb = _bf(xh * g_ref[...])
        m_ref[...] = mb
        h2_ref[...] = x
        for c in range(n_sh):
            cols = slice(c * fs, (c + 1) * fs)
            r = jnp.maximum(_dot(mb, wup_ref[c]), 0.0)
            hd = _bf(r * r)
            hid_ref[:, cols] = hd
            relu_ref[:, cols] = _bf(r)
            h2_ref[...] += _dot(hd, wdown_ref[cols, :])

    tile = lambda width: pl.BlockSpec((tm, width), lambda i: (i, 0))
    once = lambda a: pl.BlockSpec(a.shape, lambda i, nd=a.ndim: (0,) * nd, pipeline_mode=pl.Buffered(1))
    F = n_sh * fs
    return pl.pallas_call(
        body, grid=(S // tm,), name="mlp_fwd",
        in_specs=[tile(K), once(w_up), once(w_down), pl.BlockSpec(gain.shape, lambda i: (0, 0))],
        out_specs=[tile(F), tile(F), tile(K), tile(K)],
        out_shape=[jax.ShapeDtypeStruct((S, F), BF16), jax.ShapeDtypeStruct((S, F), BF16),
                   jax.ShapeDtypeStruct((S, K), BF16), jax.ShapeDtypeStruct((S, K), F32)],
        compiler_params=_params(("parallel",)),
    )(h1, w_up, w_down, gain)


def _mm_tn(x, dy, *, name, tm, tn, out_dtype=F32, column_shards=False, after=None):
    S, K = x.shape
    N = dy.shape[1]
    waits = [] if after is None else [after]

    def body(x_ref, dy_ref, *rest):
        rest[-1][...] = _dot_tn(_bf(x_ref[...]), _bf(dy_ref[...])).astype(out_dtype)

    if column_shards:
        out_spec = pl.BlockSpec((None, tm, tn), lambda i, j: (j, i, 0))
        out_shape = jax.ShapeDtypeStruct((N // tn, K, tn), out_dtype)
    else:
        out_spec = pl.BlockSpec((tm, tn), lambda i, j: (i, j))
        out_shape = jax.ShapeDtypeStruct((K, N), out_dtype)
    return pl.pallas_call(
        body, grid=(K // tm, N // tn), name=name,
        in_specs=[pl.BlockSpec((S, tm), lambda i, j: (0, i)), pl.BlockSpec((S, tn), lambda i, j: (0, j))]
        + [pl.BlockSpec(memory_space=pl.ANY)] * len(waits),
        out_specs=out_spec, out_shape=out_shape,
        compiler_params=_params(("parallel", "parallel")),
    )(x, dy, *waits)


def _rowwise(body, *, tiled, full, out_tiled, out_acc, name, tm=512, smem=()):
    S = tiled[0].shape[0]
    tm = min(tm, S)
    n_in = len(smem) + len(tiled) + len(full)

    def kern(*refs):
        @pl.when(pl.program_id(0) == 0)
        def _():
            for r in refs[n_in + len(out_tiled):]:
                r[...] = jnp.zeros_like(r)
        body(*refs)

    in_specs = [pl.BlockSpec(memory_space=pltpu.SMEM) for _ in smem]
    in_specs += [pl.BlockSpec((tm, a.shape[1]), lambda i: (i, 0)) for a in tiled]
    in_specs += [pl.BlockSpec(a.shape, lambda i, nd=a.ndim: (0,) * nd) for a in full]
    out_specs = [pl.BlockSpec((tm, w), lambda i: (i, 0)) for w, _ in out_tiled]
    out_specs += [pl.BlockSpec(shp, lambda i, nd=len(shp): (0,) * nd) for shp, _ in out_acc]
    out_shape = [jax.ShapeDtypeStruct((S, w), dt) for w, dt in out_tiled]
    out_shape += [jax.ShapeDtypeStruct(shp, dt) for shp, dt in out_acc]
    return pl.pallas_call(
        kern, grid=(S // tm,), name=name, in_specs=in_specs, out_specs=out_specs, out_shape=out_shape,
        compiler_params=_params(("arbitrary",)),
    )(*smem, *tiled, *full)


def _rms_stats(x):
    r = lax.rsqrt(jnp.mean(x * x, axis=-1, keepdims=True) + EPS)
    return r, x * r


def _rmsnorm_fwd(x, g, name):
    def body(x_ref, g_ref, o_ref):
        _, xh = _rms_stats(x_ref[...])
        o_ref[...] = _bf(xh * g_ref[...])

    return _rowwise(body, tiled=[x], full=[g], out_tiled=[(x.shape[1], BF16)], out_acc=[], name=name)[0]


def _rms_bwd_tile(x, g, dxn):
    r, xh = _rms_stats(x)
    dg = jnp.sum(dxn * xh, axis=0, keepdims=True)
    dn = dxn * g
    dx = r * (dn - xh * jnp.mean(dn * xh, axis=-1, keepdims=True))
    return dx, dg


def _ple_and_loss(h2, p, target, w_pg, w_pp, g_ple, g_final):
    S, n = h2.shape
    tm = min(512, S)
    tn = 512

    def body(h2_ref, p_ref, t_ref, wpg_ref, wpp_ref, gple_ref, gfin_ref,
             n3_ref, dh_ref, dgl_ref, dpp_ref, loss_ref, dg_ref, dgple_ref, pp, gate, h3):
        @pl.when(pl.program_id(0) == 0)
        def _():
            loss_ref[...] = jnp.zeros_like(loss_ref)
            dg_ref[...] = jnp.zeros_like(dg_ref)
            dgple_ref[...] = jnp.zeros_like(dgple_ref)

        x = h2_ref[...]
        _, xh = _rms_stats(x)
        n3 = _bf(xh * gple_ref[...])
        n3_ref[...] = n3
        pb = _bf(p_ref[...])
        for c in range(n // tn):
            cols = slice(c * tn, (c + 1) * tn)
            pp[:, cols] = _dot(pb, wpp_ref[:, cols])
            gt = _sigmoid(_dot(n3, wpg_ref[:, cols]))
            gate[:, cols] = gt
            h3[:, cols] = x[:, cols] + gt * pp[:, cols]
        y = h3[...]
        _, yh = _rms_stats(y)
        e = yh * gfin_ref[...] - t_ref[...]
        per_tok = jnp.mean(e * e, axis=-1, keepdims=True)
        loss_ref[...] += 0.5 * jnp.sum(per_tok, axis=0, keepdims=True)
        dh, dg = _rms_bwd_tile(y, gfin_ref[...], e * (1.0 / n))
        dg_ref[...] += dg
        gt = gate[...]
        dgl = _bf(dh * pp[...] * gt * (1.0 - gt))
        dgl_ref[...] = dgl
        dpp_ref[...] = _bf(dh * gt)
        for c in range(n // tn):
            cols = slice(c * tn, (c + 1) * tn)
            h3[:, cols] = _dot_nt(dgl, wpg_ref[cols, :])
        dx, dgp = _rms_bwd_tile(x, gple_ref[...], h3[...])
        dh_ref[...] = dh + dx
        dgple_ref[...] += dgp

    tile = lambda width: pl.BlockSpec((tm, width), lambda i: (i, 0))
    whole = lambda a: pl.BlockSpec(a.shape, lambda i, nd=a.ndim: (0,) * nd)
    return pl.pallas_call(
        body, grid=(S // tm,), name="ple_and_loss",
        in_specs=[tile(n), tile(p.shape[1]), tile(n), whole(w_pg), whole(w_pp), whole(g_ple), whole(g_final)],
        out_specs=[tile(n), tile(n), tile(n), tile(n), pl.BlockSpec((1, 128), lambda i: (0, 0)),
                   pl.BlockSpec((1, n), lambda i: (0, 0)), pl.BlockSpec((1, n), lambda i: (0, 0))],
        out_shape=[jax.ShapeDtypeStruct((S, n), BF16), jax.ShapeDtypeStruct((S, n), F32), jax.ShapeDtypeStruct((S, n), BF16),
                   jax.ShapeDtypeStruct((S, n), BF16), jax.ShapeDtypeStruct((1, 128), F32), jax.ShapeDtypeStruct((1, n), F32),
                   jax.ShapeDtypeStruct((1, n), F32)],
        scratch_shapes=[pltpu.VMEM((tm, n), F32)] * 3,
        compiler_params=_params(("arbitrary",)),
    )(h2, p, target, w_pg, w_pp, g_ple, g_final)


def _rope_tables(S):
    half = 32
    inv = (1.0 / (np.float32(ROPE_THETA) ** (np.arange(half, dtype=np.float32) * np.float32(2.0 / 64)))).astype(np.float32)
    ang = np.arange(S).astype(np.float32)[:, None] * inv[None, :]
    cos, sin = np.cos(ang), np.sin(ang)
    return jnp.asarray(np.tile(cos, (1, 4))), jnp.asarray(np.concatenate([-sin, sin, -sin, sin], axis=1))


def _attn_common(i, kc, kp, vc, vp, cc, sc, cp, sp):
    lane = lax.broadcasted_iota(jnp.int32, (1, HEAD_PAIR), 1)
    lane_lo = jnp.bitwise_and(lane, 63) < 32
    slot = [lane < 64, lane >= 64]

    def swap_halves(t):
        return jnp.where(lane_lo, pltpu.roll(t, 96, 1), pltpu.roll(t, 32, 1))

    def rope(t, cos, sin):
        return t * cos + swap_halves(t) * sin

    def unrope(d, cos, sin):
        return d * cos + swap_halves(d * sin)

    k2 = jnp.concatenate([rope(kp, cp, sp), rope(kc, cc, sc)], axis=0)
    v2 = jnp.concatenate([vp, vc], axis=0)
    r = lax.broadcasted_iota(jnp.int32, (ATTN_BLOCK, 2 * ATTN_BLOCK), 0)
    c = lax.broadcasted_iota(jnp.int32, (ATTN_BLOCK, 2 * ATTN_BLOCK), 1)
    valid = (c > r) & (c <= r + ATTN_BLOCK) & jnp.logical_or(c >= ATTN_BLOCK, i > 0)
    ks, vs = {}, {}
    for j in range(2):
        kn = jnp.where(slot[j], k2, 0.0)
        vn = jnp.where(slot[j], v2, 0.0)
        for s in range(2):
            ks[j, s] = _bf(kn if s == j else pltpu.roll(kn, 64, 1))
            vs[j, s] = _bf(vn if s == j else pltpu.roll(vn, 64, 1))
    return slot, rope, unrope, valid, ks, vs


def _attn_probs(scores, valid, sink):
    s = jnp.where(valid, scores * 0.125, NEG)
    m = jnp.maximum(jnp.max(s, axis=1, keepdims=True), sink)
    e = jnp.exp(s - m)
    z = jnp.sum(e, axis=1, keepdims=True) + jnp.exp(sink - m)
    return e * (1.0 / z), m + jnp.log(z)


def _attn_specs(S):
    nb = S // ATTN_BLOCK
    prev = lambda i: jnp.maximum(i - 1, 0)
    blk = lambda w, col, row=(lambda i: i): pl.BlockSpec((ATTN_BLOCK, w), lambda i: (row(i), col))
    in_specs = [pl.BlockSpec(memory_space=pltpu.SMEM),
                blk(512, BLK_Q), blk(128, BLK_K), blk(128, BLK_K, prev), blk(128, BLK_V), blk(128, BLK_V, prev),
                blk(128, 0), blk(128, 0), blk(128, 0, prev), blk(128, 0, prev)]
    return nb, in_specs


def _attn_fwd(pa, cos, sin, sinks):
    S = pa.shape[0]
    nb, in_specs = _attn_specs(S)

    def body(sinks_ref, q_ref, kc_ref, kp_ref, vc_ref, vp_ref, cc_ref, sc_ref, cp_ref, sp_ref, o_ref, lse_ref):
        i = pl.program_id(0)
        lane = lax.broadcasted_iota(jnp.int32, (1, HEAD_PAIR), 1)
        cc, sc = cc_ref[...], sc_ref[...]
        _, rope, _, valid, ks, vs = _attn_common(i, kc_ref[...], kp_ref[...], vc_ref[...], vp_ref[...],
                                                 cc, sc, cp_ref[...], sp_ref[...])
        pair_cols = [slice(HEAD_PAIR * pair, HEAD_PAIR * (pair + 1)) for pair in range(4)]
        qps = [_bf(rope(q_ref[:, cols], cc, sc)) for cols in pair_cols]
        outs, lses = {}, {}

        def head_program(h):
            pair, s = divmod(h, 2)
            j = h // 4
            scores = _dot_nt(qps[pair], ks[j, s])
            yield
            p, lse = _attn_probs(scores, valid, sinks_ref[h])
            outs[h] = _dot(_bf(p), vs[j, s])
            lses[h] = jnp.where(lane == h, lse, 0.0)

        _interleave(head_program(h) for h in range(8))
        for pair, cols in enumerate(pair_cols):
            o_ref[:, cols] = outs[2 * pair] + outs[2 * pair + 1]
        lse_ref[...] = sum((lses[h] for h in range(1, 8)), lses[0])

    return pl.pallas_call(
        body, grid=(nb,), name="attn_fwd", in_specs=in_specs,
        out_specs=[pl.BlockSpec((ATTN_BLOCK, 512), lambda i: (i, 0)), pl.BlockSpec((ATTN_BLOCK, 128), lambda i: (i, 0))],
        out_shape=[jax.ShapeDtypeStruct((S, 512), F32), jax.ShapeDtypeStruct((S, 128), F32)],
        compiler_params=_params(("parallel",)),
    )(sinks, pa, pa, pa, pa, pa, cos, sin, cos, sin)


def _attn_bwd(pa, cos, sin, sinks, dcat, attn, lse):
    S = pa.shape[0]
    nb, in_specs = _attn_specs(S)
    in_specs = in_specs + [pl.BlockSpec((ATTN_BLOCK, 512), lambda i: (i, 0))] * 2 + [pl.BlockSpec((ATTN_BLOCK, 128), lambda i: (i, 0))]

    def body(sinks_ref, q_ref, kc_ref, kp_ref, vc_ref, vp_ref, cc_ref, sc_ref, cp_ref, sp_ref, do_ref, o_ref, lse_ref,
             dq_ref, dk_ref, dv_ref, dsink_ref):
        i = pl.program_id(0)

        @pl.when(i == 0)
        def _():
            dk_ref[...] = jnp.zeros_like(dk_ref)
            dv_ref[...] = jnp.zeros_like(dv_ref)
            dsink_ref[...] = jnp.zeros_like(dsink_ref)

        cc, sc, cp, sp = cc_ref[...], sc_ref[...], cp_ref[...], sp_ref[...]
        slot, rope, unrope, valid, ks, vs = _attn_common(i, kc_ref[...], kp_ref[...], vc_ref[...], vp_ref[...], cc, sc, cp, sp)
        pair_cols = [slice(HEAD_PAIR * pair, HEAD_PAIR * (pair + 1)) for pair in range(4)]
        qps = [_bf(rope(q_ref[:, cols], cc, sc)) for cols in pair_cols]
        dobs = [_bf(do_ref[:, cols]) for cols in pair_cols]
        do_o = [do_ref[:, cols] * o_ref[:, cols] for cols in pair_cols]
        dqs, dks, dvs = {}, {}, {}

        def head_program(h):
            pair, s = divmod(h, 2)
            j = h // 4
            qp, dob = qps[pair], dobs[pair]
            scores = _dot_nt(qp, ks[j, s])
            dp = _dot_nt(dob, vs[j, s])
            yield
            lse_h = lse_ref[:, h:h + 1]
            p = jnp.exp(jnp.where(valid, scores * 0.125, NEG) - lse_h)
            yield
            dr = jnp.sum(jnp.where(slot[s], do_o[pair], 0.0), axis=1, keepdims=True)
            ds = _bf(p * (dp - dr) * 0.125)
            yield
            dsink_ref[h:h + 1, :] += -jnp.sum(jnp.exp(sinks_ref[h] - lse_h) * dr, axis=0, keepdims=True)
            dqs[h] = _dot(ds, ks[j, s])
            dk_h = _dot_tn(ds, qp)
            dv_h = _dot_tn(_bf(p), dob)
            yield
            dk_h, dv_h = jnp.where(slot[s], dk_h, 0.0), jnp.where(slot[s], dv_h, 0.0)
            if s != j:
                dk_h, dv_h = pltpu.roll(dk_h, 64, 1), pltpu.roll(dv_h, 64, 1)
            dks[h], dvs[h] = dk_h, dv_h

        _interleave(head_program(h) for h in range(8))
        dk2 = sum((dks[h] for h in range(1, 8)), dks[0])
        dv2 = sum((dvs[h] for h in range(1, 8)), dvs[0])
        for pair, cols in enumerate(pair_cols):
            dq_ref[:, cols] = _bf(unrope(dqs[2 * pair] + dqs[2 * pair + 1], cc, sc))
        cur = pl.ds(pl.multiple_of(i * ATTN_BLOCK, ATTN_BLOCK), ATTN_BLOCK)
        dk_ref[cur, :] += unrope(dk2[ATTN_BLOCK:], cc, sc)
        dv_ref[cur, :] += dv2[ATTN_BLOCK:]

        @pl.when(i > 0)
        def _():
            prv = pl.ds(pl.multiple_of((i - 1) * ATTN_BLOCK, ATTN_BLOCK), ATTN_BLOCK)
            dk_ref[prv, :] += unrope(dk2[:ATTN_BLOCK], cp, sp)
            dv_ref[prv, :] += dv2[:ATTN_BLOCK]

    whole = lambda w: pl.BlockSpec((S, w), lambda i: (0, 0))
    return pl.pallas_call(
        body, grid=(nb,), name="attn_bwd", in_specs=in_specs,
        out_specs=[pl.BlockSpec((ATTN_BLOCK, 512), lambda i: (i, BLK_Q)), whole(128), whole(128),
                   pl.BlockSpec((8, 128), lambda i: (0, 0))],
        out_shape=[jax.ShapeDtypeStruct((S, D_IN_PAD), BF16), jax.ShapeDtypeStruct((S, 128), F32),
                   jax.ShapeDtypeStruct((S, 128), F32), jax.ShapeDtypeStruct((8, 128), F32)],
        compiler_params=_params(("arbitrary",)),
    )(sinks, pa, pa, pa, pa, pa, cos, sin, cos, sin, dcat, attn, lse)


CONV_ROWS = 512
CONV_PAD = 8


def _conv_silu(scr, w, r0):
    y = w[3:4, :] * scr[pl.ds(CONV_PAD + r0, CONV_ROWS), :]
    for j in range(DN_CONV - 1):
        y = y + w[j:j + 1, :] * scr[pl.ds(CONV_PAD + r0 - 3 + j, CONV_ROWS), :]
    return y


def _dn_prep_fwd(pd, conv_w):
    S = pd.shape[0]
    assert S % CONV_ROWS == 0

    def body(x_ref, w_ref, o_ref, scr):
        b = pl.program_id(0)
        scr[0:CONV_PAD, :] = jnp.zeros((CONV_PAD, DN_DIM), F32)
        scr[pl.ds(CONV_PAD, S), :] = x_ref[...]
        w = w_ref[...]
        q_scale = jnp.where(b < DN_HEADS, DN_DIM ** -0.5, 1.0)
        for r0 in range(0, S, CONV_ROWS):
            y = _conv_silu(scr, w, r0)
            a = y * _sigmoid(y)
            rs = lax.rsqrt(jnp.sum(a * a, axis=1, keepdims=True) + EPS)
            o_ref[pl.ds(r0, CONV_ROWS), :] = a * jnp.where(b < 2 * DN_HEADS, rs * q_scale, 1.0)

    col = pl.BlockSpec((S, DN_DIM), lambda b: (0, b))
    return pl.pallas_call(
        body, grid=(3 * DN_HEADS,), name="dn_prep_fwd",
        in_specs=[pl.BlockSpec((S, DN_DIM), lambda b: (0, BLK_DN + b)), pl.BlockSpec((DN_CONV, DN_DIM), lambda b: (0, b))],
        out_specs=col,
        out_shape=jax.ShapeDtypeStruct((S, 3 * DN_HEADS * DN_DIM), F32),
        scratch_shapes=[pltpu.VMEM((S + CONV_PAD, DN_DIM), F32)],
        compiler_params=_params(("parallel",)),
    )(pd, conv_w)


def _dn_prep_bwd(pd, conv_w, dqkv, dproj):
    S = pd.shape[0]

    def body(x_ref, w_ref, d_ref, _, dx_ref, dw_ref, scr, dscr):
        b = pl.program_id(0)
        scr[0:CONV_PAD, :] = jnp.zeros((CONV_PAD, DN_DIM), F32)
        scr[pl.ds(CONV_PAD, S), :] = x_ref[...]
        dscr[pl.ds(S, CONV_PAD), :] = jnp.zeros((CONV_PAD, DN_DIM), F32)
        w = w_ref[...]
        q_scale = jnp.where(b < DN_HEADS, DN_DIM ** -0.5, 1.0)
        is_qk = b < 2 * DN_HEADS
        dw = [jnp.zeros((1, DN_DIM), F32) for _ in range(DN_CONV)]
        for r0 in range(0, S, CONV_ROWS):
            y = _conv_silu(scr, w, r0)
            sg = _sigmoid(y)
            a = y * sg
            dout = d_ref[pl.ds(r0, CONV_ROWS), :]
            rs = lax.rsqrt(jnp.sum(a * a, axis=1, keepdims=True) + EPS)
            da_qk = q_scale * rs * (dout - a * (rs * rs) * jnp.sum(dout * a, axis=1, keepdims=True))
            dy = jnp.where(is_qk, da_qk, dout) * (sg * (1.0 + y * (1.0 - sg)))
            dscr[pl.ds(r0, CONV_ROWS), :] = dy
            for j in range(DN_CONV):
                dw[j] = dw[j] + jnp.sum(dy * scr[pl.ds(CONV_PAD + r0 - 3 + j, CONV_ROWS), :], axis=0, keepdims=True)
        for j in range(DN_CONV):
            dw_ref[j:j + 1, :] = dw[j]
        for r0 in range(0, S, CONV_ROWS):
            dx = w[3:4, :] * dscr[pl.ds(r0, CONV_ROWS), :]
            for j in range(DN_CONV - 1):
                dx = dx + w[j:j + 1, :] * dscr[pl.ds(r0 + 3 - j, CONV_ROWS), :]
            dx_ref[pl.ds(r0, CONV_ROWS), :] = _bf(dx)

    col = pl.BlockSpec((S, DN_DIM), lambda b: (0, b))
    proj_col = pl.BlockSpec((S, DN_DIM), lambda b: (0, BLK_DN + b))
    wcol = pl.BlockSpec((DN_CONV, DN_DIM), lambda b: (0, b))
    return pl.pallas_call(
        body, grid=(3 * DN_HEADS,), name="dn_prep_bwd",
        in_specs=[proj_col, wcol, col, pl.BlockSpec(memory_space=pl.ANY)], out_specs=[proj_col, wcol],
        out_shape=[jax.ShapeDtypeStruct(dproj.shape, dproj.dtype), jax.ShapeDtypeStruct((DN_CONV, 3 * DN_HEADS * DN_DIM), F32)],
        scratch_shapes=[pltpu.VMEM((S + CONV_PAD, DN_DIM), F32), pltpu.VMEM((S + CONV_PAD, DN_DIM), F32)],
        input_output_aliases={3: 0},
        compiler_params=_params(("parallel",)),
    )(pd, conv_w, dqkv, dproj)


CPAD = 128
CHUNKS_LOCAL = 4
CHUNKS_SCAN = 8


def _chunk_masks():
    ii = lax.broadcasted_iota(jnp.int32, (DN_CHUNK, CPAD), 0)
    jj = lax.broadcasted_iota(jnp.int32, (DN_CHUNK, CPAD), 1)
    return ii, jj


def _rows_pad(a):
    return jnp.concatenate([a, jnp.zeros_like(a)], axis=0)


def _hi_lo(a):
    hi = _bf(a)
    return hi, _bf(a - hi.astype(F32))


def _double_step(t, p):
    C = DN_CHUNK
    th, tl = _hi_lo(t)
    ph, pl_ = _hi_lo(p)
    r1 = _dot(jnp.concatenate([th, tl, ph, pl_], axis=0), _rows_pad(ph))
    r2 = _dot(jnp.concatenate([th, ph], axis=0), _rows_pad(pl_))
    return t + (r1[:C] + r1[C:2 * C] + r2[:C]), r1[2 * C:3 * C] + r1[3 * C:] + r2[C:]


def _dot3_nt(a, b):
    C = DN_CHUNK
    ah, al = _hi_lo(a)
    bh, bl = _hi_lo(b)
    r1 = _dot_nt(jnp.concatenate([ah, al], axis=0), _rows_pad(bh))
    return r1[:C] + r1[C:] + _dot_nt(ah, _rows_pad(bl))


def _dot3_tn(a, b):
    C = DN_CHUNK
    ah, al = _hi_lo(a)
    bh, bl = _hi_lo(b)
    return _dot_tn(jnp.concatenate([ah, al, ah], axis=0), jnp.concatenate([bh, bh, bl], axis=0))[:C]


def _interleave(programs):
    programs = list(programs)
    while programs:
        alive = []
        for prog in programs:
            try:
                next(prog)
                alive.append(prog)
            except StopIteration:
                pass
        programs = alive


def _col_to_row(col, ii, jj):
    return jnp.sum(jnp.where(ii == jj, col, 0.0), axis=0, keepdims=True)


def _row_to_col(row, ii, jj):
    return jnp.sum(jnp.where(ii == jj, row, 0.0), axis=1, keepdims=True)


def _decay(gc_col, ii, jj):
    diff = gc_col - _col_to_row(gc_col, ii, jj)
    return jnp.where(jj <= ii, jnp.exp(jnp.where(jj <= ii, diff, 0.0)), 0.0)


def _softplus(x):
    return jnp.maximum(x, 0.0) + jnp.log(1.0 + jnp.exp(-jnp.abs(x)))


def _head(h):
    return slice(DN_DIM * h, DN_DIM * (h + 1))


def _dn_chunk_fwd(qkv, pg, a_log, dt_bias):
    S = qkv.shape[0]
    C = DN_CHUNK
    G = CHUNKS_LOCAL
    R = G * C
    steps = S // R

    def body(alog_ref, dtb_ref, qkv_ref, pg_ref, w_ref, u_ref, qg_ref, kd_ref, a_ref, t_ref, gcs_ref):
        ii, jj = _chunk_masks()
        lane = lax.broadcasted_iota(jnp.int32, (1, 128), 1)
        eye = (ii == jj).astype(F32)
        gcs_parts = [[] for _ in range(G)]

        def head_program(chunk, h):
            rows = slice(chunk * C, (chunk + 1) * C)
            q, k, v = qkv_ref[rows, _head(h)], qkv_ref[rows, _head(DN_HEADS + h)], qkv_ref[rows, _head(2 * DN_HEADS + h)]
            beta = _sigmoid(pg_ref[rows, h:h + 1])
            g_col = -jnp.exp(alog_ref[h]) * _softplus(pg_ref[rows, DN_HEADS + h:DN_HEADS + h + 1] + dtb_ref[h])
            g_row = _col_to_row(g_col, ii, jj)
            gc_col = jnp.sum(jnp.where(jj <= ii, g_row, 0.0), axis=1, keepdims=True)
            dec = _decay(gc_col, ii, jj)
            eg = jnp.exp(gc_col)
            kb, vb = k * beta, v * beta
            k_rows = _rows_pad(_bf(k))
            kk = _dot_nt(_bf(kb), k_rows)
            qk = _dot_nt(_bf(q), k_rows)
            yield
            t, pw = eye, -jnp.where(jj < ii, kk * dec, 0.0)
            for _ in range(6):
                t, pw = _double_step(t, pw)
                yield
            tb = _bf(t)
            u_ref[rows, _head(h)] = _dot(tb, _rows_pad(_bf(vb)))
            w_ref[rows, _head(h)] = _bf(_dot(tb, _rows_pad(_bf(kb * eg))))
            a_ref[h, rows] = _bf(qk * dec)
            t_ref[h, rows] = t
            qg_ref[rows, _head(h)] = _bf(q * eg)
            kd_ref[rows, _head(h)] = _bf(k * jnp.exp(gc_col[C - 1:C, :] - gc_col))
            gcs_parts[chunk].append(jnp.where(lane == h, gc_col, 0.0) + jnp.where(lane == DN_HEADS + h, beta, 0.0)
                                    + jnp.where(lane == 2 * DN_HEADS + h, g_col, 0.0))

        _interleave(head_program(chunk, h) for chunk in range(G) for h in range(DN_HEADS))
        for chunk in range(G):
            gcs_ref[chunk * C:(chunk + 1) * C, :] = sum(gcs_parts[chunk][1:], gcs_parts[chunk][0])

    smem = pl.BlockSpec(memory_space=pltpu.SMEM)
    wide = pl.BlockSpec((R, 512), lambda n: (n, 0))
    sq = pl.BlockSpec((DN_HEADS, R, CPAD), lambda n: (0, n, 0))
    narrow = pl.BlockSpec((R, 128), lambda n: (n, 0))
    f = lambda *shp: jax.ShapeDtypeStruct(shp, F32)
    b = lambda *shp: jax.ShapeDtypeStruct(shp, BF16)
    return pl.pallas_call(
        body, grid=(steps,), name="dn_chunk_fwd",
        in_specs=[smem, smem, pl.BlockSpec((R, 1536), lambda n: (n, 0)), pl.BlockSpec((R, 128), lambda n: (n, BLK_G))],
        out_specs=[wide, wide, wide, wide, sq, sq, narrow],
        out_shape=[b(S, 512), f(S, 512), b(S, 512), b(S, 512), b(DN_HEADS, S, CPAD), f(DN_HEADS, S, CPAD), f(S, 128)],
        compiler_params=_params(("parallel",)),
    )(a_log, dt_bias, qkv, pg)


def _gated_norm(o, z, gn):
    r, oh = _rms_stats(o)
    return oh * gn * (z * _sigmoid(z))


def _dn_scan_fwd(w, u, qg, kd, a, gcs, pz, gn):
    S = w.shape[0]
    C = DN_CHUNK
    nc = S // C
    G = CHUNKS_SCAN
    R = G * C

    def body(w_ref, u_ref, qg_ref, kd_ref, a_ref, gcs_ref, z_ref, gn_ref, o_ref, vn_ref, sst_ref, out_ref, state):
        @pl.when(pl.program_id(0) == 0)
        def _():
            state[...] = jnp.zeros_like(state)

        def head_program(chunk, h):
            hs = _head(h)
            rows = slice(chunk * C, (chunk + 1) * C)
            s_in = state[h]
            sb = _bf(s_in)
            sst_ref[chunk, h] = sb
            w_s = _dot(w_ref[rows, hs], sb)
            q_s = _dot(qg_ref[rows, hs], sb)
            yield
            vn = u_ref[rows, hs] - w_s
            vnb = _bf(vn)
            o = q_s + _dot(a_ref[h, rows], _rows_pad(vnb))
            k_v = _dot_tn(kd_ref[rows, hs], vnb)
            yield
            state[h] = s_in * jnp.exp(gcs_ref[(chunk + 1) * C - 1:(chunk + 1) * C, h:h + 1]) + k_v
            o_ref[rows, hs] = o
            vn_ref[rows, hs] = vnb
            out_ref[rows, hs] = _bf(_gated_norm(o, z_ref[rows, hs], gn_ref[...]))

        for chunk in range(G):
            _interleave(head_program(chunk, h) for h in range(DN_HEADS))

    wide = pl.BlockSpec((R, 512), lambda n: (n, 0))
    f = lambda *shp: jax.ShapeDtypeStruct(shp, F32)
    b = lambda *shp: jax.ShapeDtypeStruct(shp, BF16)
    return pl.pallas_call(
        body, grid=(nc // G,), name="dn_scan_fwd",
        in_specs=[wide, wide, wide, wide, pl.BlockSpec((DN_HEADS, R, CPAD), lambda n: (0, n, 0)),
                  pl.BlockSpec((R, 128), lambda n: (n, 0)), pl.BlockSpec((R, 512), lambda n: (n, BLK_Z)),
                  pl.BlockSpec((1, DN_DIM), lambda n: (0, 0))],
        out_specs=[wide, wide, pl.BlockSpec((G, DN_HEADS, DN_DIM, DN_DIM), lambda n: (n, 0, 0, 0)), wide],
        out_shape=[f(S, 512), b(S, 512), b(nc, DN_HEADS, DN_DIM, DN_DIM), b(S, 512)],
        scratch_shapes=[pltpu.VMEM((DN_HEADS, DN_DIM, DN_DIM), F32)],
        compiler_params=_params(("arbitrary",)),
    )(w, u, qg, kd, a, gcs, pz, gn)


def _dn_scan_bwd(dcat, o, pz, gn, sst, vnew, w, qg, kd, a, gcs, dproj):
    S = o.shape[0]
    C = DN_CHUNK
    G = CHUNKS_SCAN
    R = G * C
    steps = S // R

    def body(dy_ref, o_ref, z_ref, gn_ref, sst_ref, vn_ref, w_ref, qg_ref, kd_ref, a_ref, gcs_ref, _,
             du_ref, dw_ref, dqg_ref, dkd_ref, da_ref, dz_ref, dsc_ref, dgn_ref, dstate):
        @pl.when(pl.program_id(0) == 0)
        def _():
            dstate[...] = jnp.zeros_like(dstate)
            dgn_ref[...] = jnp.zeros_like(dgn_ref)

        gn_ = gn_ref[...]
        lane = lax.broadcasted_iota(jnp.int32, (C, 128), 1)
        row = lax.broadcasted_iota(jnp.int32, (C, 128), 0)
        dgn_parts = []

        def head_program(chunk, h, dsc_parts):
            hs = _head(h)
            rows = slice(chunk * C, (chunk + 1) * C)
            ov, z, dout = o_ref[rows, hs], z_ref[rows, hs], dy_ref[rows, hs]
            r, oh = _rms_stats(ov)
            sg = _sigmoid(z)
            don = dout * (z * sg)
            dz_ref[rows, hs] = _bf(dout * (oh * gn_) * (sg * (1.0 + z * (1.0 - sg))))
            dgn_parts.append(jnp.sum(don * oh, axis=0, keepdims=True))
            dn = don * gn_
            do = _bf(r * (dn - oh * jnp.mean(dn * oh, axis=-1, keepdims=True)))
            sb = sst_ref[chunk, h]
            s_in = sb.astype(F32)
            ds_out = dstate[h]
            dsb = _bf(ds_out)
            vnb = vn_ref[rows, hs]
            wb, qgb, kdb, ab = w_ref[rows, hs], qg_ref[rows, hs], kd_ref[rows, hs], a_ref[h, rows]
            dvn = _dot_tn(ab, do)[:C] + _dot(kdb, dsb)
            yield
            da_ref[h, rows] = _dot_nt(do, _rows_pad(vnb))
            dqg_ref[rows, hs] = _dot_nt(do, sb)
            dkd_ref[rows, hs] = _dot_nt(vnb, dsb)
            q_do = _dot_tn(qgb, do)
            yield
            dvnb = _bf(dvn)
            dw_ref[rows, hs] = _bf(-_dot_nt(dvnb, sb))
            w_dvn = _dot_tn(wb, dvnb)
            du_ref[rows, hs] = dvnb
            yield
            d_last = jnp.exp(gcs_ref[(chunk + 1) * C - 1:(chunk + 1) * C, h:h + 1])
            dd = jnp.sum(jnp.sum(ds_out * s_in, axis=1, keepdims=True), axis=0, keepdims=True)
            dsc_parts.append(jnp.where((lane == h) & (row == C - 1), dd * d_last, 0.0))
            dstate[h] = ds_out * d_last + q_do - w_dvn

        for chunk in reversed(range(G)):
            dsc_parts = []
            _interleave(head_program(chunk, h, dsc_parts) for h in range(DN_HEADS))
            dsc_ref[chunk * C:(chunk + 1) * C, :] = sum(dsc_parts[1:], dsc_parts[0])
        dgn_ref[...] += sum(dgn_parts[1:], dgn_parts[0])

    rev = lambda n: steps - 1 - n
    wide = pl.BlockSpec((R, 512), lambda n: (rev(n), 0))
    z_spec = pl.BlockSpec((R, 512), lambda n: (rev(n), BLK_Z))
    sq = pl.BlockSpec((DN_HEADS, R, CPAD), lambda n: (0, rev(n), 0))
    narrow = pl.BlockSpec((R, 128), lambda n: (rev(n), 0))
    gn_spec = pl.BlockSpec((1, DN_DIM), lambda n: (0, 0))
    f = lambda *shp: jax.ShapeDtypeStruct(shp, F32)
    b = lambda *shp: jax.ShapeDtypeStruct(shp, BF16)
    return pl.pallas_call(
        body, grid=(steps,), name="dn_scan_bwd",
        in_specs=[pl.BlockSpec((R, 512), lambda n: (rev(n), 1)), wide, z_spec, gn_spec,
                  pl.BlockSpec((G, DN_HEADS, DN_DIM, DN_DIM), lambda n: (rev(n), 0, 0, 0)),
                  wide, wide, wide, wide, sq, narrow, pl.BlockSpec(memory_space=pl.ANY)],
        out_specs=[wide, wide, wide, wide, sq, z_spec, narrow, gn_spec],
        out_shape=[b(S, 512), b(S, 512), f(S, 512), f(S, 512), f(DN_HEADS, S, CPAD),
                   jax.ShapeDtypeStruct(dproj.shape, dproj.dtype), f(S, 128), f(1, DN_DIM)],
        scratch_shapes=[pltpu.VMEM((DN_HEADS, DN_DIM, DN_DIM), F32)],
        input_output_aliases={11: 5},
        compiler_params=_params(("arbitrary",)),
    )(dcat, o, pz, gn, sst, vnew, w, qg, kd, a, gcs, dproj)


def _dn_chunk_bwd(qkv, pg, t_inv, gcs, du, dw, dqg, dkd, da, dsc, a_log, dt_bias, dproj):
    S = qkv.shape[0]
    C = DN_CHUNK
    G = CHUNKS_LOCAL
    R = G * C

    def body(alog_ref, dtb_ref, qkv_ref, pg_ref, t_ref, gcs_ref, du_ref, dw_ref, dqg_ref, dkd_ref, da_ref, dsc_ref, _,
             dqkv_ref, dpg_ref, acc_ref):
        @pl.when(pl.program_id(0) == 0)
        def _():
            acc_ref[...] = jnp.zeros_like(acc_ref)

        ii, jj = _chunk_masks()
        lane = lax.broadcasted_iota(jnp.int32, (1, 128), 1)
        row8 = lax.broadcasted_iota(jnp.int32, (8, 128), 0)
        lane8 = lax.broadcasted_iota(jnp.int32, (8, 128), 1)
        rowc = lax.broadcasted_iota(jnp.int32, (C, 1), 0)
        tril, strict = jj <= ii, jj < ii
        dpg_parts, acc_parts = [[] for _ in range(G)], []

        def head_program(chunk, h):
            rows = slice(chunk * C, (chunk + 1) * C)
            q, k, v = qkv_ref[rows, _head(h)], qkv_ref[rows, _head(DN_HEADS + h)], qkv_ref[rows, _head(2 * DN_HEADS + h)]
            gc_col, beta, g_col = gcs_ref[rows, h:h + 1], gcs_ref[rows, DN_HEADS + h:DN_HEADS + h + 1], \
                gcs_ref[rows, 2 * DN_HEADS + h:2 * DN_HEADS + h + 1]
            dec = _decay(gc_col, ii, jj)
            eg = jnp.exp(gc_col)
            g_last = gc_col[C - 1:C, :]
            ek = jnp.exp(g_last - gc_col)
            kb, vb = k * beta, v * beta
            kbg = kb * eg
            qb, kbb = _bf(q), _bf(kb)
            k_rows = _rows_pad(_bf(k))
            t = t_ref[h, rows]
            tb = _bf(t)
            dub, dwb = du_ref[rows, _head(h)], dw_ref[rows, _head(h)]
            dqg_, dkd_ = dqg_ref[rows, _head(h)], dkd_ref[rows, _head(h)]
            dt = _dot_nt(dub, _rows_pad(_bf(vb))) + _dot_nt(dwb, _rows_pad(_bf(kbg)))
            t_du_dw = _dot_tn(tb, jnp.concatenate([dub, dwb], axis=1))
            dvb, dkbg = t_du_dw[:C, :DN_DIM], t_du_dw[:C, DN_DIM:]
            kk = _dot_nt(kbb, k_rows)
            qk = _dot_nt(qb, k_rows)
            yield
            dt_t = _dot3_nt(dt, t)
            yield
            dl = -_dot3_tn(t, dt_t)
            yield
            dm = jnp.where(strict, dl * dec, 0.0)
            dqk = jnp.where(tril, da_ref[h, rows] * dec, 0.0)
            gmat = dm * kk + dqk * qk
            dgc = jnp.sum(gmat, axis=1, keepdims=True) - _row_to_col(jnp.sum(gmat, axis=0, keepdims=True), ii, jj)
            dmb, dqkb = _bf(dm), _bf(dqk)
            yield
            dkb = _dot(dmb, k_rows) + dkbg * eg
            dk = _dot_tn(jnp.concatenate([dmb, dqkb], axis=0), jnp.concatenate([kbb, qb], axis=0))[:C] + dkd_ * ek
            dq = _dot(dqkb, k_rows) + dqg_ * eg
            yield
            tk = jnp.sum(dkd_ * k * ek, axis=1, keepdims=True)
            dgc = dgc + jnp.sum(dqg_ * q * eg, axis=1, keepdims=True) - tk + jnp.sum(dkbg * kbg, axis=1, keepdims=True)
            dgl = jnp.sum(tk, axis=0, keepdims=True) + dsc_ref[(chunk + 1) * C - 1:(chunk + 1) * C, h:h + 1]
            dgc = dgc + jnp.where(rowc == C - 1, dgl, 0.0)
            yield
            dk = dk + dkb * beta
            dbeta = jnp.sum(dkb * k, axis=1, keepdims=True) + jnp.sum(dvb * v, axis=1, keepdims=True)
            dqkv_ref[rows, _head(h)] = dq
            dqkv_ref[rows, _head(DN_HEADS + h)] = dk
            dqkv_ref[rows, _head(2 * DN_HEADS + h)] = dvb * beta
            dg_col = jnp.sum(jnp.where(jj >= ii, _col_to_row(dgc, ii, jj), 0.0), axis=1, keepdims=True)
            yield
            db = dbeta * beta * (1.0 - beta)
            da_in = dg_col * (-jnp.exp(alog_ref[h])) * _sigmoid(pg_ref[rows, DN_HEADS + h:DN_HEADS + h + 1] + dtb_ref[h])
            dpg_parts[chunk].append(jnp.where(lane == h, db, 0.0) + jnp.where(lane == DN_HEADS + h, da_in, 0.0))
            acc_parts.append(jnp.where((row8 == 0) & (lane8 == h), jnp.sum(dg_col * g_col, axis=0, keepdims=True), 0.0)
                             + jnp.where((row8 == 1) & (lane8 == h), jnp.sum(da_in, axis=0, keepdims=True), 0.0))

        _interleave(head_program(chunk, h) for chunk in range(G) for h in range(DN_HEADS))
        for chunk in range(G):
            dpg = sum(dpg_parts[chunk][1:], dpg_parts[chunk][0])
            dpg_ref[chunk * C:(chunk + 1) * C, :] = _bf(jnp.concatenate([dpg, jnp.zeros_like(dpg)], axis=1))
        acc_ref[...] += sum(acc_parts[1:], acc_parts[0])

    smem = pl.BlockSpec(memory_space=pltpu.SMEM)
    wide = pl.BlockSpec((R, 512), lambda n: (n, 0))
    sq = pl.BlockSpec((DN_HEADS, R, CPAD), lambda n: (0, n, 0))
    narrow = pl.BlockSpec((R, 128), lambda n: (n, 0))
    qkv_spec = pl.BlockSpec((R, 1536), lambda n: (n, 0))
    f = lambda *shp: jax.ShapeDtypeStruct(shp, F32)
    return pl.pallas_call(
        body, grid=(S // R,), name="dn_chunk_bwd",
        in_specs=[smem, smem, qkv_spec, pl.BlockSpec((R, 128), lambda n: (n, BLK_G)), sq, narrow, wide, wide, wide, wide, sq,
                  narrow, pl.BlockSpec(memory_space=pl.ANY)],
        out_specs=[qkv_spec, pl.BlockSpec((R, 256), lambda n: (n, BLK_G_PAD)), pl.BlockSpec((8, 128), lambda n: (0, 0))],
        out_shape=[f(S, 1536), jax.ShapeDtypeStruct(dproj.shape, dproj.dtype), f(8, 128)],
        input_output_aliases={12: 1},
        compiler_params=_params(("arbitrary",)),
    )(a_log, dt_bias, qkv, pg, t_inv, gcs, du, dw, dqg, dkd, da, dsc, dproj)


def _fill_kv(dk, dv, dproj):
    S = dk.shape[0]
    tm = min(512, S)

    def body(dk_ref, dv_ref, _, o_ref):
        o_ref[...] = _bf(jnp.concatenate([dk_ref[...], dv_ref[...]], axis=1))

    tile = pl.BlockSpec((tm, 128), lambda i: (i, 0))
    return pl.pallas_call(
        body, grid=(S // tm,), name="fill_kv",
        in_specs=[tile, tile, pl.BlockSpec(memory_space=pl.ANY)],
        out_specs=pl.BlockSpec((tm, 256), lambda i: (i, BLK_KV)),
        out_shape=jax.ShapeDtypeStruct(dproj.shape, dproj.dtype),
        input_output_aliases={2: 0},
        compiler_params=_params(("parallel",)),
    )(dk, dv, dproj)


def _w_in_to_internal(wt):
    return jnp.concatenate([wt[0:512], wt[2304:2816], wt[768:2304], wt[512:768], wt[2816:2824],
                            jnp.zeros((D_IN_PAD - D_IN, wt.shape[1]), wt.dtype)], axis=0)


def _w_in_from_internal(gt):
    return jnp.concatenate([gt[0:512], gt[2560:2816], gt[1024:2560], gt[512:1024], gt[2816:2824]], axis=0)


def _local_step(x, p, target, wts, first_weights, other_weights, ship_early):
    S = x.shape[0]
    cos, sin = _rope_tables(S)
    sinks, a_log, dt_bias = wts["sinks"].reshape(8), wts["a_log"].reshape(4), wts["dt_bias"].reshape(4)
    gn = wts["dn_norm"].reshape(1, DN_DIM)
    add = lambda acc, res: (acc + res,)

    u = _rmsnorm_fwd(x, wts["norm_mix"], "norm_mix_fwd")
    w_in_t, conv_w = first_weights(u)
    proj, = _mm(u, w_in_t, form="nt", name="in_proj", out_dtypes=[F32], tn=512)
    attn, lse = _attn_fwd(proj, cos, sin, sinks)
    qkv = _dn_prep_fwd(proj, conv_w)
    cw, cu, cqg, ckd, ca, ct, gcs = _dn_chunk_fwd(qkv, proj, a_log, dt_bias)
    o, vnew, sst, dn_out = _dn_scan_fwd(cw, cu, cqg, ckd, ca, gcs, proj, gn)
    w_o, = other_weights(("w_o",), dn_out)
    h1, = _mm([attn, dn_out], w_o, form="nn", name="out_proj", out_dtypes=[F32], tn=512, epi=add, extra=[x])

    w_up, w_down = other_weights(("w_up", "w_down"), h1)
    hid, relu, m, h2 = _mlp_fwd(h1, w_up, w_down, wts["norm_mlp"])
    w_pg, w_pp = other_weights(("w_ple_gate", "w_ple_proj"), h2)
    n3, dh2, dgl, dpp, loss, d_norm_final, d_norm_ple = _ple_and_loss(h2, p, target, w_pg, w_pp, wts["norm_ple"],
                                                                     wts["norm_final"].reshape(1, D_MODEL))
    g = {"norm_final": d_norm_final, "norm_ple": d_norm_ple}
    early = {"w_ple_gate": _mm_tn(n3, dgl, name="d_w_ple_gate", tm=512, tn=1024, out_dtype=BF16).reshape(N_DEV, 128, 1024),
             "w_ple_proj": _mm_tn(p, dpp, name="d_w_ple_proj", tm=256, tn=128, out_dtype=BF16, column_shards=True)}
    d_act, = _mm(dh2, w_down, form="nt", name="d_hidden", out_dtypes=[BF16], tn=512,
                 epi=lambda acc, r: (acc * (2.0 * r.astype(F32)),), extra=[relu])
    early["w_down"] = _mm_tn(hid, dh2, name="d_w_down", tm=512, tn=1024, out_dtype=BF16).reshape(N_DEV, 512, 1024)
    early["w_up"] = _mm_tn(m, d_act, name="d_w_up", tm=1024, tn=512, out_dtype=BF16, column_shards=True)
    token = ship_early(early)
    dh1, g["norm_mlp"], dcat = _mm(d_act, w_up, form="nt", name="d_m", out_dtypes=[F32], tn=512, after=token,
                                   norm_bwd=(h1, wts["norm_mlp"], dh2), then_nt=w_o)
    d_w_o = jnp.concatenate([_mm_tn(attn, dh1, name="d_w_o_attn", tm=512, tn=512, out_dtype=BF16),
                             _mm_tn(dn_out, dh1, name="d_w_o_dn", tm=512, tn=512, out_dtype=BF16)], axis=0)
    token = ship_early({"w_o": d_w_o.reshape(N_DEV, 128, 1024)})
    dproj, dk, dv, dsinks = _attn_bwd(proj, cos, sin, sinks + token[0, 0], dcat, attn, lse)
    g["sinks"] = dsinks[:, 0].reshape(1, 8)
    du_, dw_, dqg, dkd, da, dproj, dsc, g["dn_norm"] = _dn_scan_bwd(dcat, o, proj, gn, sst, vnew, cw, cqg, ckd, ca, gcs, dproj)
    dqkv, dproj, gate_acc = _dn_chunk_bwd(qkv, proj, ct, gcs, du_, dw_, dqg, dkd, da, dsc, a_log, dt_bias, dproj)
    g["a_log"], g["dt_bias"] = gate_acc[0:1, 0:4], gate_acc[1:2, 0:4]
    dproj, g["conv_w"] = _dn_prep_bwd(proj, conv_w, dqkv, dproj)
    dproj = _fill_kv(dk, dv, dproj)
    token = ship_early({"w_in": _mm_tn(dproj, u, name="d_w_in", tm=512, tn=1024, out_dtype=BF16)})
    grad_x, g["norm_mix"] = _mm(dproj, w_in_t, form="nn", name="d_u", out_dtypes=[F32], tn=512, after=token,
                                norm_bwd=(x, wts["norm_mix"], dh1))
    return loss, grad_x, g


def _peer(k):
    x, y, c = lax.axis_index("x"), lax.axis_index("y"), lax.axis_index("c")
    px = 1 - x if k & 4 else x
    py = 1 - y if k & 2 else y
    pc = 1 - c if k & 1 else c
    return (px, py, pc), 4 * px + 2 * py + pc


def _exchange(srcs, name, gather):
    n = len(srcs)
    gathers = list(gather) if isinstance(gather, (list, tuple)) else [gather] * n
    shapes = [(N_DEV,) + s.shape if gt else s.shape for s, gt in zip(srcs, gathers)]

    def body(*refs):
        src_refs, out_refs = refs[:n], refs[n:2 * n]
        send_sems, recv_sems, local_sems = refs[2 * n:]
        _, me = _peer(0)
        piece = lambda a, d: src_refs[a] if gathers[a] else src_refs[a].at[d]
        local = [pltpu.make_async_copy(piece(a, me), out_refs[a].at[me], local_sems.at[a]) for a in range(n)]
        for cp in local:
            cp.start()
        copies = []
        for a in range(n):
            for k in range(1, N_DEV):
                dev, idx = _peer(k)
                cp = pltpu.make_async_remote_copy(src_ref=piece(a, idx), dst_ref=out_refs[a].at[me],
                                                  send_sem=send_sems.at[a, k - 1], recv_sem=recv_sems.at[a, k - 1],
                                                  device_id=dev, device_id_type=MESH)
                cp.start()
                copies.append(cp)
        for cp in copies:
            cp.wait_recv()
        for cp in copies:
            cp.wait_send()
        for cp in local:
            cp.wait()

    anywhere = pl.BlockSpec(memory_space=pl.ANY)
    return pl.pallas_call(
        body, name=name, in_specs=[anywhere] * n, out_specs=[anywhere] * n,
        out_shape=[jax.ShapeDtypeStruct(shp, s.dtype) for shp, s in zip(shapes, srcs)],
        scratch_shapes=[pltpu.SemaphoreType.DMA((n, N_DEV - 1)), pltpu.SemaphoreType.DMA((n, N_DEV - 1)),
                        pltpu.SemaphoreType.DMA((n,))],
    )(*srcs)


_HBM = pl.BlockSpec(memory_space=pltpu.HBM)
_SEM = pl.BlockSpec(memory_space=pltpu.SEMAPHORE)
_EFFECT = pltpu.SideEffectType.DATAFLOW_SIDE_EFFECTING


def _split_copies(src_refs, land_refs, send_sems, recv_sems, modes, which=None):
    _, me = _peer(0)
    copies = []
    which = range(len(src_refs)) if which is None else which
    for a, src, land in zip(which, src_refs, land_refs):
        if modes[a] == "columns":
            n_cols = src.shape[1]
            dst = land.at[:, pl.ds(pl.multiple_of(me * n_cols, n_cols), n_cols)]
        else:
            dst = land.at[me]
        for k in ((2, 4, 6) if modes[a] == "chips" else range(1, N_DEV)):
            dev, idx = _peer(k)
            sem = a * (N_DEV - 1) + k - 1
            copies.append(pltpu.make_async_remote_copy(
                src_ref=src.at[idx] if modes[a] == "pieces" else src, dst_ref=dst, send_sem=send_sems.at[sem],
                recv_sem=recv_sems.at[sem], device_id=dev, device_id_type=MESH))
    return copies


def _forward_copies(land_refs, send_sems, recv_sems):
    c = lax.axis_index("c")
    sibling, _ = _peer(1)
    copies = []
    for a, land in enumerate(land_refs):
        for chip in range(N_DEV // 2):
            slot = 2 * chip + c
            sem = a * (N_DEV // 2) + chip
            copies.append(pltpu.make_async_remote_copy(
                src_ref=land.at[slot], dst_ref=land.at[slot], send_sem=send_sems.at[sem], recv_sem=recv_sems.at[sem],
                device_id=sibling, device_id_type=MESH))
    return copies


def _forward_start(lands, name):
    n = len(lands)

    def body(*refs):
        for cp in _forward_copies(refs[:n], refs[n], refs[n + 1]):
            cp.start()
        refs[-1][...] = jnp.zeros_like(refs[-1])

    sems = pltpu.SemaphoreType.DMA((n * (N_DEV // 2),))
    out = pl.pallas_call(
        body, name=name,
        out_shape=(sems, sems, *[pltpu.HBM(t.shape, t.dtype) for t in lands], jax.ShapeDtypeStruct((8, 128), F32)),
        in_specs=[_HBM] * n, out_specs=(_SEM, _SEM, *[_HBM] * n, pl.BlockSpec(memory_space=pltpu.VMEM)),
        input_output_aliases={i: 2 + i for i in range(n)},
        compiler_params=pltpu.CompilerParams(has_side_effects=_EFFECT),
    )(*[pltpu.with_memory_space_constraint(t, pltpu.HBM) for t in lands])
    return out[:-1], out[-1]


def _forward_wait(handle, after, name):
    send_sems, recv_sems, *lands = handle
    n = len(lands)

    def body(*refs):
        for cp in _forward_copies(refs[:n], refs[n], refs[n + 1]):
            cp.wait_send()
            cp.wait_recv()

    return list(pl.pallas_call(
        body, name=name, out_shape=tuple(pltpu.HBM(t.shape, t.dtype) for t in lands),
        in_specs=[_HBM] * n + [_SEM, _SEM, pl.BlockSpec(memory_space=pl.ANY)], out_specs=tuple([_HBM] * n),
        input_output_aliases={i: i for i in range(n)},
        compiler_params=pltpu.CompilerParams(has_side_effects=_EFFECT),
    )(*lands, send_sems, recv_sems, after))


def _exchange_start(srcs, name, modes):
    n = len(srcs)
    modes = [modes] * n if isinstance(modes, str) else list(modes)
    me = 4 * lax.axis_index("x") + 2 * lax.axis_index("y") + lax.axis_index("c")
    lands = []
    for s, mode in zip(srcs, modes):
        if mode == "columns":
            empty = lax.empty((s.shape[0], N_DEV * s.shape[1]), s.dtype)
            lands.append(lax.dynamic_update_slice(empty, s, (0, me * s.shape[1])))
        else:
            own = s if mode != "pieces" else lax.dynamic_index_in_dim(s, me, 0, keepdims=False)
            shape = (N_DEV,) + s.shape if mode != "pieces" else s.shape
            lands.append(lax.dynamic_update_index_in_dim(lax.empty(shape, s.dtype), own, me, 0))

    def body(*refs):
        src_refs, land_refs = refs[:n], refs[n:2 * n]
        send_sems, recv_sems = refs[2 * n], refs[2 * n + 1]
        for cp in _split_copies(src_refs, land_refs, send_sems, recv_sems, modes):
            cp.start()
        refs[-1][...] = jnp.zeros_like(refs[-1])

    both = list(srcs) + lands
    sems = pltpu.SemaphoreType.DMA((n * (N_DEV - 1),))
    out = pl.pallas_call(
        body, name=name,
        out_shape=(sems, sems, *[pltpu.HBM(t.shape, t.dtype) for t in both], jax.ShapeDtypeStruct((8, 128), F32)),
        in_specs=[_HBM] * (2 * n), out_specs=(_SEM, _SEM, *[_HBM] * (2 * n), pl.BlockSpec(memory_space=pltpu.VMEM)),
        input_output_aliases={i: 2 + i for i in range(2 * n)},
        compiler_params=pltpu.CompilerParams(has_side_effects=_EFFECT),
    )(*[pltpu.with_memory_space_constraint(t, pltpu.HBM) for t in both])
    return (n, modes, out[:-1]), out[-1]


def _exchange_wait(handle, after, name, which=None):
    n_all, modes, (send_sems, recv_sems, *both_all) = handle
    which = list(range(n_all)) if which is None else list(which)
    n = len(which)
    both = [both_all[a] for a in which] + [both_all[n_all + a] for a in which]

    def body(*refs):
        src_refs, land_refs = refs[:n], refs[n:2 * n]
        for cp in _split_copies(src_refs, land_refs, refs[2 * n], refs[2 * n + 1], modes, which):
            cp.wait_send()
            cp.wait_recv()

    out = pl.pallas_call(
        body, name=name, out_shape=tuple(pltpu.HBM(t.shape, t.dtype) for t in both),
        in_specs=[_HBM] * (2 * n) + [_SEM, _SEM, pl.BlockSpec(memory_space=pl.ANY)], out_specs=tuple([_HBM] * (2 * n)),
        input_output_aliases={i: i for i in range(2 * n)},
        compiler_params=pltpu.CompilerParams(has_side_effects=_EFFECT),
    )(*both, send_sems, recv_sems, after)
    return list(out[n:])


def _adam_update(g, w, m, v):
    nm = ADAM_B1 * m + (1.0 - ADAM_B1) * g
    nv = ADAM_B2 * v + (1.0 - ADAM_B2) * (g * g)
    m_hat = nm / (1.0 - ADAM_B1 ** ADAM_STEP)
    v_hat = nv / (1.0 - ADAM_B2 ** ADAM_STEP)
    return -ADAM_LR * (m_hat / (jnp.sqrt(v_hat) + ADAM_EPS) + ADAM_WD * w), nm, nv


def _adamw(parts, w, m, v, name):
    n, R, W = parts.shape
    tm = 128 if R % 128 == 0 else R

    def body(p_ref, w_ref, m_ref, v_ref, g_ref, d_ref, nm_ref, nv_ref):
        g = p_ref[0].astype(F32)
        for s in range(1, n):
            g = g + p_ref[s].astype(F32)
        g_ref[...] = g
        d_ref[...], nm_ref[...], nv_ref[...] = _adam_update(g, w_ref[...], m_ref[...], v_ref[...])

    tile = pl.BlockSpec((tm, W), lambda i: (i, 0))
    return pl.pallas_call(
        body, grid=(R // tm,), name=name,
        in_specs=[pl.BlockSpec((n, tm, W), lambda i: (0, i, 0)), tile, tile, tile],
        out_specs=[tile] * 4, out_shape=[jax.ShapeDtypeStruct((R, W), F32)] * 4,
        compiler_params=_params(("parallel",)),
    )(parts, w, m, v)


_MATRICES = ("w_in", "w_o", "w_up", "w_down", "w_ple_gate", "w_ple_proj")


_OTHERS = ("w_o", "w_up", "w_down", "w_ple_gate", "w_ple_proj")
_OTHER_MODES = {"w_o": "slots", "w_up": "slots", "w_down": "slots", "w_ple_gate": "slots", "w_ple_proj": "columns"}


_VECTORS = ("norm_mix", "norm_mlp", "norm_ple", "norm_final", "a_log", "dt_bias", "sinks", "dn_norm")
_SMALL_ROWS, _LOSS_ROW, _CONV_ROW = 16, 8, 9


def _pack_small(vectors, loss, conv):
    def body(*refs):
        out = refs[-1]
        out[...] = jnp.zeros_like(out)
        for r, ref in enumerate(refs[:len(_VECTORS)]):
            out[r:r + 1, 0:ref.shape[1]] = ref[...]
        out[_LOSS_ROW:_LOSS_ROW + 1, 0:128] = refs[len(_VECTORS)][...]
        out[_CONV_ROW:_CONV_ROW + 6, :] = refs[len(_VECTORS) + 1][...]

    return pl.pallas_call(body, name="pack_small", out_shape=jax.ShapeDtypeStruct((_SMALL_ROWS, 1024), F32))(*vectors, loss, conv)


def _sum_slots(parts):
    def body(p_ref, o_ref):
        acc = p_ref[0]
        for s in range(1, parts.shape[0]):
            acc = acc + p_ref[s]
        o_ref[...] = acc

    return pl.pallas_call(body, name="sum_small", out_shape=jax.ShapeDtypeStruct(parts.shape[1:], parts.dtype))(parts)


def _adamw_vectors(summed, conv_g, wmv):
    names = _VECTORS + ("conv_w",)
    flat = [a for triple in wmv for a in triple]

    def body(*refs):
        sum_ref, conv_ref = refs[0], refs[1]
        ins, outs = refs[2:2 + len(flat)], refs[2 + len(flat):]
        for i in range(len(names)):
            w_ref, m_ref, v_ref = ins[3 * i:3 * i + 3]
            g = conv_ref[...] if i == len(_VECTORS) else sum_ref[i:i + 1, 0:w_ref.shape[1]]
            outs[4 * i][...] = g
            outs[4 * i + 1][...], outs[4 * i + 2][...], outs[4 * i + 3][...] = _adam_update(g, w_ref[...], m_ref[...], v_ref[...])

    out_shape = [jax.ShapeDtypeStruct(t[0].shape, F32) for t in wmv for _ in range(4)]
    res = pl.pallas_call(body, name="adamw_vectors", out_shape=out_shape)(summed, conv_g, *flat)
    return {n: res[4 * i:4 * i + 4] for i, n in enumerate(names)}


_ORDER = ("norm_mix", "w_in", "conv_w", "a_log", "dt_bias", "dn_norm", "sinks", "w_o", "norm_mlp", "w_up", "w_down",
          "norm_ple", "w_ple_gate", "w_ple_proj", "norm_final")


def kernel(x, p, norm_mix, w_in, conv_w, a_log, dt_bias, dn_norm, sinks, w_o, norm_mlp, w_up, w_down, norm_ple, w_ple_gate, w_ple_proj, norm_final, loss_target, m_norm_mix, m_w_in, m_conv_w, m_a_log, m_dt_bias, m_dn_norm, m_sinks, m_w_o, m_norm_mlp, m_w_up, m_w_down, m_norm_ple, m_w_ple_gate, m_w_ple_proj, m_norm_final, v_norm_mix, v_w_in, v_conv_w, v_a_log, v_dt_bias, v_dn_norm, v_sinks, v_w_o, v_norm_mlp, v_w_up, v_w_down, v_norm_ple, v_w_ple_gate, v_w_ple_proj, v_norm_final):
    w = dict(norm_mix=norm_mix, w_in=w_in[0], conv_w=conv_w[0], a_log=a_log, dt_bias=dt_bias, dn_norm=dn_norm, sinks=sinks,
             w_o=w_o[0], norm_mlp=norm_mlp, w_up=w_up[0], w_down=w_down[0], norm_ple=norm_ple, w_ple_gate=w_ple_gate[0],
             w_ple_proj=w_ple_proj[0], norm_final=norm_final)
    m = dict(norm_mix=m_norm_mix, w_in=m_w_in[0], conv_w=m_conv_w[0], a_log=m_a_log, dt_bias=m_dt_bias, dn_norm=m_dn_norm,
             sinks=m_sinks, w_o=m_w_o[0], norm_mlp=m_norm_mlp, w_up=m_w_up[0], w_down=m_w_down[0], norm_ple=m_norm_ple,
             w_ple_gate=m_w_ple_gate[0], w_ple_proj=m_w_ple_proj[0], norm_final=m_norm_final)
    v = dict(norm_mix=v_norm_mix, w_in=v_w_in[0], conv_w=v_conv_w[0], a_log=v_a_log, dt_bias=v_dt_bias, dn_norm=v_dn_norm,
             sinks=v_sinks, w_o=v_w_o[0], norm_mlp=v_norm_mlp, w_up=v_w_up[0], w_down=v_w_down[0], norm_ple=v_norm_ple,
             w_ple_gate=v_w_ple_gate[0], w_ple_proj=v_w_ple_proj[0], norm_final=v_norm_final)
    me = 4 * lax.axis_index("x") + 2 * lax.axis_index("y") + lax.axis_index("c")
    conv_shard = conv_w.shape[2]

    for d in (w, m, v):
        d["w_in"] = d["w_in"].T
    conv_pad = jnp.pad(w["conv_w"], ((0, 8 - DN_CONV), (0, 256 - conv_shard)))
    gathers, token_gather = _exchange_start([_bf(w["w_in"]), conv_pad] + [_bf(w[n]) for n in _OTHERS], "gather_start",
                                            ["chips", "chips"] + [_OTHER_MODES[n] for n in _OTHERS])
    vectors = dict(w)
    vectors["norm_mix"] = w["norm_mix"] + token_gather[0:1, 0:1]

    def first_weights(after):
        over_ici = _exchange_wait(gathers, after, "gather_first_wait", [0, 1])
        handle, token = _forward_start(over_ici, "gather_first_forward")
        w_in_all, conv_all = _forward_wait(handle, token, "gather_first_forward_wait")
        conv_all = jnp.transpose(conv_all[:, :DN_CONV, :conv_shard], (1, 0, 2)).reshape(DN_CONV, N_DEV * conv_shard)
        return _w_in_to_internal(w_in_all.reshape(D_IN, D_MODEL)), conv_all

    as_taken = {"w_o": lambda t: t.reshape(1024, 1024), "w_up": lambda t: t, "w_down": lambda t: t.reshape(4096, 1024),
                "w_ple_gate": lambda t: t.reshape(1024, 1024), "w_ple_proj": lambda t: t}

    def other_weights(names, after):
        which = [2 + _OTHERS.index(n) for n in names]
        got = _exchange_wait(gathers, after, "gather_wait_" + names[0], which)
        return [as_taken[n](t) for n, t in zip(names, got)]

    shipped = []

    def ship_early(pieces):
        names = tuple(pieces)
        if names == ("w_in",):
            pieces = {"w_in": _w_in_from_internal(pieces["w_in"]).reshape(N_DEV, D_IN // N_DEV, D_MODEL)}
        handle, token = _exchange_start([pieces[n] for n in names], "scatter_start_" + names[0], "pieces")
        shipped.append((names, handle))
        return token

    loss, grad_x, g = _local_step(x[0], p[0, 0], loss_target[0], vectors, first_weights, other_weights, ship_early)

    row = lambda t: t.reshape(1, t.size)
    small = _pack_small([row(g[n]) for n in _VECTORS], loss, g["conv_w"].reshape(6, 1024))
    small_handle, token_small = _exchange_start([small], "gather_small_start", "slots")
    big, after = {}, token_small
    for names, handle in shipped[:-1]:
        for n, r in zip(names, _exchange_wait(handle, after, "scatter_wait_" + names[0])):
            big[n] = _adamw(r, w[n], m[n], v[n], "adamw_" + n)
            after = big[n][1]
    small_all, = _exchange_wait(small_handle, after, "gather_small_wait")
    summed = _sum_slots(small_all)
    conv_g = lax.dynamic_slice(summed[_CONV_ROW:_CONV_ROW + 6].reshape(DN_CONV, N_DEV * conv_shard), (0, me * conv_shard),
                               (DN_CONV, conv_shard))
    small_out = _adamw_vectors(summed, conv_g, [(row(w[n]), row(m[n]), row(v[n])) for n in _VECTORS]
                               + [(w["conv_w"], m["conv_w"], v["conv_w"])])
    names, handle = shipped[-1]
    for n, r in zip(names, _exchange_wait(handle, small_out["conv_w"][0], "scatter_wait_" + names[0])):
        big[n] = _adamw(r, w[n], m[n], v[n], "adamw_" + n)

    result = [summed[_LOSS_ROW, 0], grad_x[None]]
    for i in range(4):
        for n in _ORDER:
            if n == "w_in":
                result.append(big[n][i].T[None])
            elif n in _MATRICES:
                result.append(big[n][i][None])
            elif n == "conv_w":
                result.append(small_out[n][i][None])
            else:
                result.append(small_out[n][i].reshape(w[n].shape))
    return tuple(result)
```

```python
import jax
import jax.numpy as jnp
import numpy as np
from jax import lax
from jax.experimental import pallas as pl
from jax.experimental.pallas import tpu as pltpu

F32, BF16 = jnp.float32, jnp.bfloat16
EPS = 1e-6
D_MODEL = 1024
N_DEV = 8
ATTN_BLOCK = 128
HEAD_PAIR = 128
DN_HEADS = 4
DN_DIM = 128
DN_CHUNK = 64
DN_CONV = 4
ROPE_THETA = 10000.0
D_IN = 2824
D_IN_PAD = 3072
BLK_Q, BLK_Z = 0, 1
BLK_DN, BLK_K, BLK_V, BLK_G = 8, 20, 21, 22
BLK_G_PAD = 11
VMEM_LIMIT = 56 * 1024 * 1024
NEG = -1e30
ADAM_LR, ADAM_B1, ADAM_B2, ADAM_EPS, ADAM_WD, ADAM_STEP = 0.001, 0.9, 0.999, 1e-08, 0.01, 10
MESH = pl.DeviceIdType.MESH


def _bf(x):
    return x.astype(BF16)


def _dot(a, b):
    return jnp.dot(a, b, preferred_element_type=F32)


def _dot_nt(a, b):
    return lax.dot_general(a, b, (((1,), (1,)), ((), ())), preferred_element_type=F32)


def _dot_tn(a, b):
    return lax.dot_general(a, b, (((0,), (0,)), ((), ())), preferred_element_type=F32)


def _sigmoid(x):
    return 1.0 / (1.0 + jnp.exp(-x))


def _params(sem):
    return pltpu.CompilerParams(dimension_semantics=sem, vmem_limit_bytes=VMEM_LIMIT)


def _mm(x, w, *, form, name, out_dtypes, tn, epi=None, extra=(), tm=512, w_row_block=0, after=None, norm=None,
        norm_bwd=None, then_nt=None):
    assert norm is None or norm_bwd is None
    xs = list(x) if isinstance(x, (list, tuple)) else [x]
    nx = len(xs)
    S, K = xs[0].shape
    shards = w.ndim == 3
    N = (w.shape[2] * N_DEV if shards else w.shape[1]) if form == "nn" else w.shape[-2]
    assert not (shards and form == "nn" and tn != w.shape[2]) and (nx == 1 or (form == "nn" and not shards and norm is None))
    r0 = w_row_block * K
    tm = min(tm, S)
    n_extra, n_out = len(extra), len(out_dtypes)
    tile = lambda width: pl.BlockSpec((tm, width), lambda i: (i, 0))
    whole = lambda a: pl.BlockSpec(a.shape, lambda i, nd=a.ndim: (0,) * nd)
    ins, in_specs = [*xs, w, *extra], [tile(K)] * nx + [whole(w)] + [tile(N)] * n_extra
    if norm is not None:
        ins, in_specs = ins + [norm], in_specs + [whole(norm)]
    if norm_bwd is not None:
        ins, in_specs = ins + list(norm_bwd), in_specs + [tile(N), whole(norm_bwd[1]), tile(N)]
    if then_nt is not None:
        ins, in_specs = ins + [then_nt], in_specs + [whole(then_nt)]
    if after is not None:
        ins, in_specs = ins + [after], in_specs + [whole(after)]
    out_shape = [jax.ShapeDtypeStruct((S, N), dt) for dt in out_dtypes]
    out_specs = [tile(N)] * n_out
    if norm is not None:
        out_shape, out_specs = out_shape + [jax.ShapeDtypeStruct((S, K), BF16)], out_specs + [tile(K)]
    if norm_bwd is not None:
        out_shape, out_specs = out_shape + [jax.ShapeDtypeStruct((1, N), F32)], out_specs + [pl.BlockSpec((1, N), lambda i: (0, 0))]
    if then_nt is not None:
        out_shape, out_specs = out_shape + [jax.ShapeDtypeStruct((S, then_nt.shape[0]), F32)], out_specs + [tile(then_nt.shape[0])]

    def product(xb, w_ref, cols, c):
        if form == "nn" and nx > 1:
            return sum(_dot(part, w_ref[r0 + p * K:r0 + (p + 1) * K, cols]) for p, part in enumerate(xb))
        if form == "nn":
            return _dot(xb, w_ref[c] if shards else w_ref[r0:r0 + K, cols])
        if not shards:
            return _dot_nt(xb, w_ref[cols, :])
        ks = w.shape[2]
        acc = _dot_nt(xb[:, 0:ks], w_ref[0, cols, :])
        for s in range(1, N_DEV):
            acc = acc + _dot_nt(xb[:, s * ks:(s + 1) * ks], w_ref[s, cols, :])
        return acc

    def body(*refs):
        x_ref, w_ref = refs[0], refs[nx]
        extra_refs = refs[nx + 1:nx + 1 + n_extra]
        at = nx + 1 + n_extra
        if norm is not None:
            gain_ref, at = refs[at], at + 1
        if norm_bwd is not None:
            (y_ref, ygain_ref, dres_ref), at = refs[at:at + 3], at + 3
        if then_nt is not None:
            w2_ref, at = refs[at], at + 1
        outs = refs[len(ins):]
        if norm is not None:
            _, xh = _rms_stats(x_ref[...])
            xb = _bf(xh * gain_ref[...])
            outs[n_out][...] = xb
        else:
            xb = _bf(x_ref[...]) if nx == 1 else [_bf(r[...]) for r in refs[:nx]]
        for c in range(N // tn):
            cols = slice(c * tn, (c + 1) * tn)
            acc = product(xb, w_ref, cols, c)
            res = epi(acc, *[r[:, cols] for r in extra_refs]) if epi else (acc,)
            for o, r in zip(outs[:n_out], res):
                o[:, cols] = r.astype(o.dtype)
        if norm_bwd is not None:
            dx, dg = _rms_bwd_tile(y_ref[...], ygain_ref[...], outs[0][...])
            outs[0][...] = dres_ref[...] + dx
            dg_ref = outs[n_out]

            @pl.when(pl.program_id(0) == 0)
            def _():
                dg_ref[...] = jnp.zeros_like(dg_ref)

            dg_ref[...] += dg
        if then_nt is not None:
            yb = _bf(outs[0][...])
            for c in range(then_nt.shape[0] // tn):
                cols = slice(c * tn, (c + 1) * tn)
                outs[-1][:, cols] = _dot_nt(yb, w2_ref[cols, :])

    return pl.pallas_call(
        body, grid=(S // tm,), name=name, in_specs=in_specs, out_specs=out_specs, out_shape=out_shape,
        compiler_params=_params(("arbitrary",) if norm_bwd is not None else ("parallel",)),
    )(*ins)


def _mlp_fwd(h1, w_up, w_down, gain):
    S, K = h1.shape
    n_sh, _, fs = w_up.shape
    tm = min(512, S)

    def body(x_ref, wup_ref, wdown_ref, g_ref, hid_ref, relu_ref, m_ref, h2_ref):
        x = x_ref[...]
        _, xh = _rms_stats(x)
        mb = _bf(xh * g_ref[...])
        m_ref[...] = mb
        h2_ref[...] = x
        for c in range(n_sh):
            cols = slice(c * fs, (c + 1) * fs)
            r = jnp.maximum(_dot(mb, wup_ref[c]), 0.0)
            hd = _bf(r * r)
            hid_ref[:, cols] = hd
            relu_ref[:, cols] = _bf(r)
            h2_ref[...] += _dot(hd, wdown_ref[cols, :])

    tile = lambda width: pl.BlockSpec((tm, width), lambda i: (i, 0))
    once = lambda a: pl.BlockSpec(a.shape, lambda i, nd=a.ndim: (0,) * nd, pipeline_mode=pl.Buffered(1))
    F = n_sh * fs
    return pl.pallas_call(
        body, grid=(S // tm,), name="mlp_fwd",
        in_specs=[tile(K), once(w_up), once(w_down), pl.BlockSpec(gain.shape, lambda i: (0, 0))],
        out_specs=[tile(F), tile(F), tile(K), tile(K)],
        out_shape=[jax.ShapeDtypeStruct((S, F), BF16), jax.ShapeDtypeStruct((S, F), BF16),
                   jax.ShapeDtypeStruct((S, K), BF16), jax.ShapeDtypeStruct((S, K), F32)],
        compiler_params=_params(("parallel",)),
    )(h1, w_up, w_down, gain)


def _mm_tn(x, dy, *, name, tm, tn, out_dtype=F32, column_shards=False, after=None):
    S, K = x.shape
    N = dy.shape[1]
    waits = [] if after is None else [after]

    def body(x_ref, dy_ref, *rest):
        rest[-1][...] = _dot_tn(_bf(x_ref[...]), _bf(dy_ref[...])).astype(out_dtype)

    if column_shards:
        out_spec = pl.BlockSpec((None, tm, tn), lambda i, j: (j, i, 0))
        out_shape = jax.ShapeDtypeStruct((N // tn, K, tn), out_dtype)
    else:
        out_spec = pl.BlockSpec((tm, tn), lambda i, j: (i, j))
        out_shape = jax.ShapeDtypeStruct((K, N), out_dtype)
    return pl.pallas_call(
        body, grid=(K // tm, N // tn), name=name,
        in_specs=[pl.BlockSpec((S, tm), lambda i, j: (0, i)), pl.BlockSpec((S, tn), lambda i, j: (0, j))]
        + [pl.BlockSpec(memory_space=pl.ANY)] * len(waits),
        out_specs=out_spec, out_shape=out_shape,
        compiler_params=_params(("parallel", "parallel")),
    )(x, dy, *waits)


def _rowwise(body, *, tiled, full, out_tiled, out_acc, name, tm=512, smem=()):
    S = tiled[0].shape[0]
    tm = min(tm, S)
    n_in = len(smem) + len(tiled) + len(full)

    def kern(*refs):
        @pl.when(pl.program_id(0) == 0)
        def _():
            for r in refs[n_in + len(out_tiled):]:
                r[...] = jnp.zeros_like(r)
        body(*refs)

    in_specs = [pl.BlockSpec(memory_space=pltpu.SMEM) for _ in smem]
    in_specs += [pl.BlockSpec((tm, a.shape[1]), lambda i: (i, 0)) for a in tiled]
    in_specs += [pl.BlockSpec(a.shape, lambda i, nd=a.ndim: (0,) * nd) for a in full]
    out_specs = [pl.BlockSpec((tm, w), lambda i: (i, 0)) for w, _ in out_tiled]
    out_specs += [pl.BlockSpec(shp, lambda i, nd=len(shp): (0,) * nd) for shp, _ in out_acc]
    out_shape = [jax.ShapeDtypeStruct((S, w), dt) for w, dt in out_tiled]
    out_shape += [jax.ShapeDtypeStruct(shp, dt) for shp, dt in out_acc]
    return pl.pallas_call(
        kern, grid=(S // tm,), name=name, in_specs=in_specs, out_specs=out_specs, out_shape=out_shape,
        compiler_params=_params(("arbitrary",)),
    )(*smem, *tiled, *full)


def _rms_stats(x):
    r = lax.rsqrt(jnp.mean(x * x, axis=-1, keepdims=True) + EPS)
    return r, x * r


def _rmsnorm_fwd(x, g, name):
    def body(x_ref, g_ref, o_ref):
        _, xh = _rms_stats(x_ref[...])
        o_ref[...] = _bf(xh * g_ref[...])

    return _rowwise(body, tiled=[x], full=[g], out_tiled=[(x.shape[1], BF16)], out_acc=[], name=name)[0]


def _rms_bwd_tile(x, g, dxn):
    r, xh = _rms_stats(x)
    dg = jnp.sum(dxn * xh, axis=0, keepdims=True)
    dn = dxn * g
    dx = r * (dn - xh * jnp.mean(dn * xh, axis=-1, keepdims=True))
    return dx, dg


def _ple_and_loss(h2, p, target, w_pg, w_pp, g_ple, g_final):
    S, n = h2.shape
    tm = min(512, S)
    tn = 512

    def body(h2_ref, p_ref, t_ref, wpg_ref, wpp_ref, gple_ref, gfin_ref,
             n3_ref, dh_ref, dgl_ref, dpp_ref, loss_ref, dg_ref, dgple_ref, pp, gate, h3):
        @pl.when(pl.program_id(0) == 0)
        def _():
            loss_ref[...] = jnp.zeros_like(loss_ref)
            dg_ref[...] = jnp.zeros_like(dg_ref)
            dgple_ref[...] = jnp.zeros_like(dgple_ref)

        x = h2_ref[...]
        _, xh = _rms_stats(x)
        n3 = _bf(xh * gple_ref[...])
        n3_ref[...] = n3
        pb = _bf(p_ref[...])
        for c in range(n // tn):
            cols = slice(c * tn, (c + 1) * tn)
            pp[:, cols] = _dot(pb, wpp_ref[:, cols])
            gt = _sigmoid(_dot(n3, wpg_ref[:, cols]))
            gate[:, cols] = gt
            h3[:, cols] = x[:, cols] + gt * pp[:, cols]
        y = h3[...]
        _, yh = _rms_stats(y)
        e = yh * gfin_ref[...] - t_ref[...]
        per_tok = jnp.mean(e * e, axis=-1, keepdims=True)
        loss_ref[...] += 0.5 * jnp.sum(per_tok, axis=0, keepdims=True)
        dh, dg = _rms_bwd_tile(y, gfin_ref[...], e * (1.0 / n))
        dg_ref[...] += dg
        gt = gate[...]
        dgl = _bf(dh * pp[...] * gt * (1.0 - gt))
        dgl_ref[...] = dgl
        dpp_ref[...] = _bf(dh * gt)
        for c in range(n // tn):
            cols = slice(c * tn, (c + 1) * tn)
            h3[:, cols] = _dot_nt(dgl, wpg_ref[cols, :])
        dx, dgp = _rms_bwd_tile(x, gple_ref[...], h3[...])
        dh_ref[...] = dh + dx
        dgple_ref[...] += dgp

    tile = lambda width: pl.BlockSpec((tm, width), lambda i: (i, 0))
    whole = lambda a: pl.BlockSpec(a.shape, lambda i, nd=a.ndim: (0,) * nd)
    return pl.pallas_call(
        body, grid=(S // tm,), name="ple_and_loss",
        in_specs=[tile(n), tile(p.shape[1]), tile(n), whole(w_pg), whole(w_pp), whole(g_ple), whole(g_final)],
        out_specs=[tile(n), tile(n), tile(n), tile(n), pl.BlockSpec((1, 128), lambda i: (0, 0)),
                   pl.BlockSpec((1, n), lambda i: (0, 0)), pl.BlockSpec((1, n), lambda i: (0, 0))],
        out_shape=[jax.ShapeDtypeStruct((S, n), BF16), jax.ShapeDtypeStruct((S, n), F32), jax.ShapeDtypeStruct((S, n), BF16),
                   jax.ShapeDtypeStruct((S, n), BF16), jax.ShapeDtypeStruct((1, 128), F32), jax.ShapeDtypeStruct((1, n), F32),
                   jax.ShapeDtypeStruct((1, n), F32)],
        scratch_shapes=[pltpu.VMEM((tm, n), F32)] * 3,
        compiler_params=_params(("arbitrary",)),
    )(h2, p, target, w_pg, w_pp, g_ple, g_final)


def _rope_tables(S):
    half = 32
    inv = (1.0 / (np.float32(ROPE_THETA) ** (np.arange(half, dtype=np.float32) * np.float32(2.0 / 64)))).astype(np.float32)
    ang = np.arange(S).astype(np.float32)[:, None] * inv[None, :]
    cos, sin = np.cos(ang), np.sin(ang)
    return jnp.asarray(np.tile(cos, (1, 4))), jnp.asarray(np.concatenate([-sin, sin, -sin, sin], axis=1))


def _attn_common(i, kc, kp, vc, vp, cc, sc, cp, sp):
    lane = lax.broadcasted_iota(jnp.int32, (1, HEAD_PAIR), 1)
    lane_lo = jnp.bitwise_and(lane, 63) < 32
    slot = [lane < 64, lane >= 64]

    def swap_halves(t):
        return jnp.where(lane_lo, pltpu.roll(t, 96, 1), pltpu.roll(t, 32, 1))

    def rope(t, cos, sin):
        return t * cos + swap_halves(t) * sin

    def unrope(d, cos, sin):
        return d * cos + swap_halves(d * sin)

    k2 = jnp.concatenate([rope(kp, cp, sp), rope(kc, cc, sc)], axis=0)
    v2 = jnp.concatenate([vp, vc], axis=0)
    r = lax.broadcasted_iota(jnp.int32, (ATTN_BLOCK, 2 * ATTN_BLOCK), 0)
    c = lax.broadcasted_iota(jnp.int32, (ATTN_BLOCK, 2 * ATTN_BLOCK), 1)
    valid = (c > r) & (c <= r + ATTN_BLOCK) & jnp.logical_or(c >= ATTN_BLOCK, i > 0)
    ks, vs = {}, {}
    for j in range(2):
        kn = jnp.where(slot[j], k2, 0.0)
        vn = jnp.where(slot[j], v2, 0.0)
        for s in range(2):
            ks[j, s] = _bf(kn if s == j else pltpu.roll(kn, 64, 1))
            vs[j, s] = _bf(vn if s == j else pltpu.roll(vn, 64, 1))
    return slot, rope, unrope, valid, ks, vs


def _attn_probs(scores, valid, sink):
    s = jnp.where(valid, scores * 0.125, NEG)
    m = jnp.maximum(jnp.max(s, axis=1, keepdims=True), sink)
    e = jnp.exp(s - m)
    z = jnp.sum(e, axis=1, keepdims=True) + jnp.exp(sink - m)
    return e * (1.0 / z), m + jnp.log(z)


def _attn_specs(S):
    nb = S // ATTN_BLOCK
    prev = lambda i: jnp.maximum(i - 1, 0)
    blk = lambda w, col, row=(lambda i: i): pl.BlockSpec((ATTN_BLOCK, w), lambda i: (row(i), col))
    in_specs = [pl.BlockSpec(memory_space=pltpu.SMEM),
                blk(512, BLK_Q), blk(128, BLK_K), blk(128, BLK_K, prev), blk(128, BLK_V), blk(128, BLK_V, prev),
                blk(128, 0), blk(128, 0), blk(128, 0, prev), blk(128, 0, prev)]
    return nb, in_specs


def _attn_fwd(pa, cos, sin, sinks):
    S = pa.shape[0]
    nb, in_specs = _attn_specs(S)

    def body(sinks_ref, q_ref, kc_ref, kp_ref, vc_ref, vp_ref, cc_ref, sc_ref, cp_ref, sp_ref, o_ref, lse_ref):
        i = pl.program_id(0)
        lane = lax.broadcasted_iota(jnp.int32, (1, HEAD_PAIR), 1)
        cc, sc = cc_ref[...], sc_ref[...]
        _, rope, _, valid, ks, vs = _attn_common(i, kc_ref[...], kp_ref[...], vc_ref[...], vp_ref[...],
                                                 cc, sc, cp_ref[...], sp_ref[...])
        pair_cols = [slice(HEAD_PAIR * pair, HEAD_PAIR * (pair + 1)) for pair in range(4)]
        qps = [_bf(rope(q_ref[:, cols], cc, sc)) for cols in pair_cols]
        outs, lses = {}, {}

        def head_program(h):
            pair, s = divmod(h, 2)
            j = h // 4
            scores = _dot_nt(qps[pair], ks[j, s])
            yield
            p, lse = _attn_probs(scores, valid, sinks_ref[h])
            outs[h] = _dot(_bf(p), vs[j, s])
            lses[h] = jnp.where(lane == h, lse, 0.0)

        _interleave(head_program(h) for h in range(8))
        for pair, cols in enumerate(pair_cols):
            o_ref[:, cols] = outs[2 * pair] + outs[2 * pair + 1]
        lse_ref[...] = sum((lses[h] for h in range(1, 8)), lses[0])

    return pl.pallas_call(
        body, grid=(nb,), name="attn_fwd", in_specs=in_specs,
        out_specs=[pl.BlockSpec((ATTN_BLOCK, 512), lambda i: (i, 0)), pl.BlockSpec((ATTN_BLOCK, 128), lambda i: (i, 0))],
        out_shape=[jax.ShapeDtypeStruct((S, 512), F32), jax.ShapeDtypeStruct((S, 128), F32)],
        compiler_params=_params(("parallel",)),
    )(sinks, pa, pa, pa, pa, pa, cos, sin, cos, sin)


def _attn_bwd(pa, cos, sin, sinks, dcat, attn, lse):
    S = pa.shape[0]
    nb, in_specs = _attn_specs(S)
    in_specs = in_specs + [pl.BlockSpec((ATTN_BLOCK, 512), lambda i: (i, 0))] * 2 + [pl.BlockSpec((ATTN_BLOCK, 128), lambda i: (i, 0))]

    def body(sinks_ref, q_ref, kc_ref, kp_ref, vc_ref, vp_ref, cc_ref, sc_ref, cp_ref, sp_ref, do_ref, o_ref, lse_ref,
             dq_ref, dk_ref, dv_ref, dsink_ref):
        i = pl.program_id(0)

        @pl.when(i == 0)
        def _():
            dk_ref[...] = jnp.zeros_like(dk_ref)
            dv_ref[...] = jnp.zeros_like(dv_ref)
            dsink_ref[...] = jnp.zeros_like(dsink_ref)

        cc, sc, cp, sp = cc_ref[...], sc_ref[...], cp_ref[...], sp_ref[...]
        slot, rope, unrope, valid, ks, vs = _attn_common(i, kc_ref[...], kp_ref[...], vc_ref[...], vp_ref[...], cc, sc, cp, sp)
        pair_cols = [slice(HEAD_PAIR * pair, HEAD_PAIR * (pair + 1)) for pair in range(4)]
        qps = [_bf(rope(q_ref[:, cols], cc, sc)) for cols in pair_cols]
        dobs = [_bf(do_ref[:, cols]) for cols in pair_cols]
        do_o = [do_ref[:, cols] * o_ref[:, cols] for cols in pair_cols]
        dqs, dks, dvs = {}, {}, {}

        def head_program(h):
            pair, s = divmod(h, 2)
            j = h // 4
            qp, dob = qps[pair], dobs[pair]
            scores = _dot_nt(qp, ks[j, s])
            dp = _dot_nt(dob, vs[j, s])
            yield
            lse_h = lse_ref[:, h:h + 1]
            p = jnp.exp(jnp.where(valid, scores * 0.125, NEG) - lse_h)
            yield
            dr = jnp.sum(jnp.where(slot[s], do_o[pair], 0.0), axis=1, keepdims=True)
            ds = _bf(p * (dp - dr) * 0.125)
            yield
            dsink_ref[h:h + 1, :] += -jnp.sum(jnp.exp(sinks_ref[h] - lse_h) * dr, axis=0, keepdims=True)
            dqs[h] = _dot(ds, ks[j, s])
            dk_h = _dot_tn(ds, qp)
            dv_h = _dot_tn(_bf(p), dob)
            yield
            dk_h, dv_h = jnp.where(slot[s], dk_h, 0.0), jnp.where(slot[s], dv_h, 0.0)
            if s != j:
                dk_h, dv_h = pltpu.roll(dk_h, 64, 1), pltpu.roll(dv_h, 64, 1)
            dks[h], dvs[h] = dk_h, dv_h

        _interleave(head_program(h) for h in range(8))
        dk2 = sum((dks[h] for h in range(1, 8)), dks[0])
        dv2 = sum((dvs[h] for h in range(1, 8)), dvs[0])
        for pair, cols in enumerate(pair_cols):
            dq_ref[:, cols] = _bf(unrope(dqs[2 * pair] + dqs[2 * pair + 1], cc, sc))
        cur = pl.ds(pl.multiple_of(i * ATTN_BLOCK, ATTN_BLOCK), ATTN_BLOCK)
        dk_ref[cur, :] += unrope(dk2[ATTN_BLOCK:], cc, sc)
        dv_ref[cur, :] += dv2[ATTN_BLOCK:]

        @pl.when(i > 0)
        def _():
            prv = pl.ds(pl.multiple_of((i - 1) * ATTN_BLOCK, ATTN_BLOCK), ATTN_BLOCK)
            dk_ref[prv, :] += unrope(dk2[:ATTN_BLOCK], cp, sp)
            dv_ref[prv, :] += dv2[:ATTN_BLOCK]

    whole = lambda w: pl.BlockSpec((S, w), lambda i: (0, 0))
    return pl.pallas_call(
        body, grid=(nb,), name="attn_bwd", in_specs=in_specs,
        out_specs=[pl.BlockSpec((ATTN_BLOCK, 512), lambda i: (i, BLK_Q)), whole(128), whole(128),
                   pl.BlockSpec((8, 128), lambda i: (0, 0))],
        out_shape=[jax.ShapeDtypeStruct((S, D_IN_PAD), BF16), jax.ShapeDtypeStruct((S, 128), F32),
                   jax.ShapeDtypeStruct((S, 128), F32), jax.ShapeDtypeStruct((8, 128), F32)],
        compiler_params=_params(("arbitrary",)),
    )(sinks, pa, pa, pa, pa, pa, cos, sin, cos, sin, dcat, attn, lse)


CONV_ROWS = 512
CONV_PAD = 8


def _conv_silu(scr, w, r0):
    y = w[3:4, :] * scr[pl.ds(CONV_PAD + r0, CONV_ROWS), :]
    for j in range(DN_CONV - 1):
        y = y + w[j:j + 1, :] * scr[pl.ds(CONV_PAD + r0 - 3 + j, CONV_ROWS), :]
    return y


def _dn_prep_fwd(pd, conv_w):
    S = pd.shape[0]
    assert S % CONV_ROWS == 0

    def body(x_ref, w_ref, o_ref, scr):
        b = pl.program_id(0)
        scr[0:CONV_PAD, :] = jnp.zeros((CONV_PAD, DN_DIM), F32)
        scr[pl.ds(CONV_PAD, S), :] = x_ref[...]
        w = w_ref[...]
        q_scale = jnp.where(b < DN_HEADS, DN_DIM ** -0.5, 1.0)
        for r0 in range(0, S, CONV_ROWS):
            y = _conv_silu(scr, w, r0)
            a = y * _sigmoid(y)
            rs = lax.rsqrt(jnp.sum(a * a, axis=1, keepdims=True) + EPS)
            o_ref[pl.ds(r0, CONV_ROWS), :] = a * jnp.where(b < 2 * DN_HEADS, rs * q_scale, 1.0)

    col = pl.BlockSpec((S, DN_DIM), lambda b: (0, b))
    return pl.pallas_call(
        body, grid=(3 * DN_HEADS,), name="dn_prep_fwd",
        in_specs=[pl.BlockSpec((S, DN_DIM), lambda b: (0, BLK_DN + b)), pl.BlockSpec((DN_CONV, DN_DIM), lambda b: (0, b))],
        out_specs=col,
        out_shape=jax.ShapeDtypeStruct((S, 3 * DN_HEADS * DN_DIM), F32),
        scratch_shapes=[pltpu.VMEM((S + CONV_PAD, DN_DIM), F32)],
        compiler_params=_params(("parallel",)),
    )(pd, conv_w)


def _dn_prep_bwd(pd, conv_w, dqkv, dproj, dk, dv):
    S = pd.shape[0]
    NB = 3 * DN_HEADS

    def body(x_ref, w_ref, d_ref, _, dk_ref, dv_ref, dx_ref, dw_ref, scr, dscr):
        b = pl.program_id(0)

        @pl.when(b == NB)
        def _():
            dx_ref[...] = _bf(dk_ref[...])

        @pl.when(b == NB + 1)
        def _():
            dx_ref[...] = _bf(dv_ref[...])

        @pl.when(b < NB)
        def _():
            scr[0:CONV_PAD, :] = jnp.zeros((CONV_PAD, DN_DIM), F32)
            scr[pl.ds(CONV_PAD, S), :] = x_ref[...]
            dscr[pl.ds(S, CONV_PAD), :] = jnp.zeros((CONV_PAD, DN_DIM), F32)
            w = w_ref[...]
            q_scale = jnp.where(b < DN_HEADS, DN_DIM ** -0.5, 1.0)
            is_qk = b < 2 * DN_HEADS
            dw = [jnp.zeros((1, DN_DIM), F32) for _ in range(DN_CONV)]
            for r0 in range(0, S, CONV_ROWS):
                y = _conv_silu(scr, w, r0)
                sg = _sigmoid(y)
                a = y * sg
                dout = d_ref[pl.ds(r0, CONV_ROWS), :]
                rs = lax.rsqrt(jnp.sum(a * a, axis=1, keepdims=True) + EPS)
                da_qk = q_scale * rs * (dout - a * (rs * rs) * jnp.sum(dout * a, axis=1, keepdims=True))
                dy = jnp.where(is_qk, da_qk, dout) * (sg * (1.0 + y * (1.0 - sg)))
                dscr[pl.ds(r0, CONV_ROWS), :] = dy
                for j in range(DN_CONV):
                    dw[j] = dw[j] + jnp.sum(dy * scr[pl.ds(CONV_PAD + r0 - 3 + j, CONV_ROWS), :], axis=0, keepdims=True)
            for j in range(DN_CONV):
                dw_ref[j:j + 1, :] = dw[j]
            for r0 in range(0, S, CONV_ROWS):
                dx = w[3:4, :] * dscr[pl.ds(r0, CONV_ROWS), :]
                for j in range(DN_CONV - 1):
                    dx = dx + w[j:j + 1, :] * dscr[pl.ds(r0 + 3 - j, CONV_ROWS), :]
                dx_ref[pl.ds(r0, CONV_ROWS), :] = _bf(dx)

    own = lambda b: jnp.minimum(b, NB - 1)
    col = pl.BlockSpec((S, DN_DIM), lambda b: (0, own(b)))
    proj_col = pl.BlockSpec((S, DN_DIM), lambda b: (0, BLK_DN + own(b)))
    wcol = pl.BlockSpec((DN_CONV, DN_DIM), lambda b: (0, own(b)))
    whole = pl.BlockSpec((S, DN_DIM), lambda b: (0, 0))
    assert BLK_K == BLK_DN + NB and BLK_V == BLK_K + 1
    return pl.pallas_call(
        body, grid=(NB + 2,), name="dn_prep_bwd",
        in_specs=[proj_col, wcol, col, pl.BlockSpec(memory_space=pl.ANY), whole, whole],
        out_specs=[pl.BlockSpec((S, DN_DIM), lambda b: (0, BLK_DN + b)), wcol],
        out_shape=[jax.ShapeDtypeStruct(dproj.shape, dproj.dtype), jax.ShapeDtypeStruct((DN_CONV, 3 * DN_HEADS * DN_DIM), F32)],
        scratch_shapes=[pltpu.VMEM((S + CONV_PAD, DN_DIM), F32), pltpu.VMEM((S + CONV_PAD, DN_DIM), F32)],
        input_output_aliases={3: 0},
        compiler_params=_params(("arbitrary",)),
    )(pd, conv_w, dqkv, dproj, dk, dv)


CPAD = 128
CHUNKS_LOCAL = 4
CHUNKS_SCAN = 8


def _chunk_masks():
    ii = lax.broadcasted_iota(jnp.int32, (DN_CHUNK, CPAD), 0)
    jj = lax.broadcasted_iota(jnp.int32, (DN_CHUNK, CPAD), 1)
    return ii, jj


def _rows_pad(a):
    return jnp.concatenate([a, jnp.zeros_like(a)], axis=0)


def _hi_lo(a):
    hi = _bf(a)
    return hi, _bf(a - hi.astype(F32))


def _double_step(t, p):
    C = DN_CHUNK
    th, tl = _hi_lo(t)
    ph, pl_ = _hi_lo(p)
    r1 = _dot(jnp.concatenate([th, tl, ph, pl_], axis=0), _rows_pad(ph))
    r2 = _dot(jnp.concatenate([th, ph], axis=0), _rows_pad(pl_))
    return t + (r1[:C] + r1[C:2 * C] + r2[:C]), r1[2 * C:3 * C] + r1[3 * C:] + r2[C:]


def _dot3_nt(a, b):
    C = DN_CHUNK
    ah, al = _hi_lo(a)
    bh, bl = _hi_lo(b)
    r1 = _dot_nt(jnp.concatenate([ah, al], axis=0), _rows_pad(bh))
    return r1[:C] + r1[C:] + _dot_nt(ah, _rows_pad(bl))


def _dot3_tn(a, b):
    C = DN_CHUNK
    ah, al = _hi_lo(a)
    bh, bl = _hi_lo(b)
    return _dot_tn(jnp.concatenate([ah, al, ah], axis=0), jnp.concatenate([bh, bh, bl], axis=0))[:C]


def _interleave(programs):
    programs = list(programs)
    while programs:
        alive = []
        for prog in programs:
            try:
                next(prog)
                alive.append(prog)
            except StopIteration:
                pass
        programs = alive


def _col_to_row(col, ii, jj):
    return jnp.sum(jnp.where(ii == jj, col, 0.0), axis=0, keepdims=True)


def _row_to_col(row, ii, jj):
    return jnp.sum(jnp.where(ii == jj, row, 0.0), axis=1, keepdims=True)


def _decay(gc_col, ii, jj):
    diff = gc_col - _col_to_row(gc_col, ii, jj)
    return jnp.where(jj <= ii, jnp.exp(jnp.where(jj <= ii, diff, 0.0)), 0.0)


def _softplus(x):
    return jnp.maximum(x, 0.0) + jnp.log(1.0 + jnp.exp(-jnp.abs(x)))


def _head(h):
    return slice(DN_DIM * h, DN_DIM * (h + 1))


def _dn_chunk_fwd(qkv, pg, a_log, dt_bias):
    S = qkv.shape[0]
    C = DN_CHUNK
    G = CHUNKS_LOCAL
    R = G * C
    steps = S // R

    def body(alog_ref, dtb_ref, qkv_ref, pg_ref, w_ref, u_ref, qg_ref, kd_ref, a_ref, t_ref, gcs_ref):
        ii, jj = _chunk_masks()
        lane = lax.broadcasted_iota(jnp.int32, (1, 128), 1)
        eye = (ii == jj).astype(F32)
        gcs_parts = [[] for _ in range(G)]

        def head_program(chunk, h):
            rows = slice(chunk * C, (chunk + 1) * C)
            q, k, v = qkv_ref[rows, _head(h)], qkv_ref[rows, _head(DN_HEADS + h)], qkv_ref[rows, _head(2 * DN_HEADS + h)]
            beta = _sigmoid(pg_ref[rows, h:h + 1])
            g_col = -jnp.exp(alog_ref[h]) * _softplus(pg_ref[rows, DN_HEADS + h:DN_HEADS + h + 1] + dtb_ref[h])
            g_row = _col_to_row(g_col, ii, jj)
            gc_col = jnp.sum(jnp.where(jj <= ii, g_row, 0.0), axis=1, keepdims=True)
            dec = _decay(gc_col, ii, jj)
            eg = jnp.exp(gc_col)
            kb, vb = k * beta, v * beta
            k_rows = _rows_pad(_bf(k))
            kk = _dot_nt(_bf(kb), k_rows)
            qk = _dot_nt(_bf(q), k_rows)
            yield
            t, pw = eye, -jnp.where(jj < ii, kk * dec, 0.0)
            for _ in range(6):
                t, pw = _double_step(t, pw)
                yield
            tb = _bf(t)
            u_ref[rows, _head(h)] = _dot(tb, _rows_pad(_bf(vb)))
            w_ref[rows, _head(h)] = _bf(_dot(tb, _rows_pad(_bf(kb * eg))))
            a_ref[h, rows] = _bf(qk * dec)
            t_ref[h, rows] = t
            qg_ref[rows, _head(h)] = _bf(q * eg)
            kd_ref[rows, _head(h)] = _bf(k * jnp.exp(gc_col[C - 1:C, :] - gc_col))
            gcs_parts[chunk].append(jnp.where(lane == h, gc_col, 0.0) + jnp.where(lane == DN_HEADS + h, beta, 0.0)
                                    + jnp.where(lane == 2 * DN_HEADS + h, g_col, 0.0))

        _interleave(head_program(chunk, h) for chunk in range(G) for h in range(DN_HEADS))
        for chunk in range(G):
            gcs_ref[chunk * C:(chunk + 1) * C, :] = sum(gcs_parts[chunk][1:], gcs_parts[chunk][0])

    smem = pl.BlockSpec(memory_space=pltpu.SMEM)
    wide = pl.BlockSpec((R, 512), lambda n: (n, 0))
    sq = pl.BlockSpec((DN_HEADS, R, CPAD), lambda n: (0, n, 0))
    narrow = pl.BlockSpec((R, 128), lambda n: (n, 0))
    f = lambda *shp: jax.ShapeDtypeStruct(shp, F32)
    b = lambda *shp: jax.ShapeDtypeStruct(shp, BF16)
    return pl.pallas_call(
        body, grid=(steps,), name="dn_chunk_fwd",
        in_specs=[smem, smem, pl.BlockSpec((R, 1536), lambda n: (n, 0)), pl.BlockSpec((R, 128), lambda n: (n, BLK_G))],
        out_specs=[wide, wide, wide, wide, sq, sq, narrow],
        out_shape=[b(S, 512), f(S, 512), b(S, 512), b(S, 512), b(DN_HEADS, S, CPAD), f(DN_HEADS, S, CPAD), f(S, 128)],
        compiler_params=_params(("parallel",)),
    )(a_log, dt_bias, qkv, pg)


def _gated_norm(o, z, gn):
    r, oh = _rms_stats(o)
    return oh * gn * (z * _sigmoid(z))


def _dn_scan_fwd(w, u, qg, kd, a, gcs, pz, gn):
    S = w.shape[0]
    C = DN_CHUNK
    nc = S // C
    G = CHUNKS_SCAN
    R = G * C

    def body(w_ref, u_ref, qg_ref, kd_ref, a_ref, gcs_ref, z_ref, gn_ref, o_ref, vn_ref, sst_ref, out_ref, state):
        @pl.when(pl.program_id(0) == 0)
        def _():
            state[...] = jnp.zeros_like(state)

        def head_program(chunk, h):
            hs = _head(h)
            rows = slice(chunk * C, (chunk + 1) * C)
            s_in = state[h]
            sb = _bf(s_in)
            sst_ref[chunk, h] = sb
            w_s = _dot(w_ref[rows, hs], sb)
            q_s = _dot(qg_ref[rows, hs], sb)
            yield
            vn = u_ref[rows, hs] - w_s
            vnb = _bf(vn)
            o = q_s + _dot(a_ref[h, rows], _rows_pad(vnb))
            k_v = _dot_tn(kd_ref[rows, hs], vnb)
            yield
            state[h] = s_in * jnp.exp(gcs_ref[(chunk + 1) * C - 1:(chunk + 1) * C, h:h + 1]) + k_v
            o_ref[rows, hs] = o
            vn_ref[rows, hs] = vnb
            out_ref[rows, hs] = _bf(_gated_norm(o, z_ref[rows, hs], gn_ref[...]))

        for chunk in range(G):
            _interleave(head_program(chunk, h) for h in range(DN_HEADS))

    wide = pl.BlockSpec((R, 512), lambda n: (n, 0))
    f = lambda *shp: jax.ShapeDtypeStruct(shp, F32)
    b = lambda *shp: jax.ShapeDtypeStruct(shp, BF16)
    return pl.pallas_call(
        body, grid=(nc // G,), name="dn_scan_fwd",
        in_specs=[wide, wide, wide, wide, pl.BlockSpec((DN_HEADS, R, CPAD), lambda n: (0, n, 0)),
                  pl.BlockSpec((R, 128), lambda n: (n, 0)), pl.BlockSpec((R, 512), lambda n: (n, BLK_Z)),
                  pl.BlockSpec((1, DN_DIM), lambda n: (0, 0))],
        out_specs=[wide, wide, pl.BlockSpec((G, DN_HEADS, DN_DIM, DN_DIM), lambda n: (n, 0, 0, 0)), wide],
        out_shape=[f(S, 512), b(S, 512), b(nc, DN_HEADS, DN_DIM, DN_DIM), b(S, 512)],
        scratch_shapes=[pltpu.VMEM((DN_HEADS, DN_DIM, DN_DIM), F32)],
        compiler_params=_params(("arbitrary",)),
    )(w, u, qg, kd, a, gcs, pz, gn)


def _dn_scan_bwd(dcat, o, pz, gn, sst, vnew, w, qg, kd, a, gcs, dproj):
    S = o.shape[0]
    C = DN_CHUNK
    G = CHUNKS_SCAN
    R = G * C
    steps = S // R

    def body(dy_ref, o_ref, z_ref, gn_ref, sst_ref, vn_ref, w_ref, qg_ref, kd_ref, a_ref, gcs_ref, _,
             du_ref, dw_ref, dqg_ref, dkd_ref, da_ref, dz_ref, dsc_ref, dgn_ref, dstate):
        @pl.when(pl.program_id(0) == 0)
        def _():
            dstate[...] = jnp.zeros_like(dstate)
            dgn_ref[...] = jnp.zeros_like(dgn_ref)

        gn_ = gn_ref[...]
        lane = lax.broadcasted_iota(jnp.int32, (C, 128), 1)
        row = lax.broadcasted_iota(jnp.int32, (C, 128), 0)
        dgn_parts = []

        def head_program(chunk, h, dsc_parts):
            hs = _head(h)
            rows = slice(chunk * C, (chunk + 1) * C)
            ov, z, dout = o_ref[rows, hs], z_ref[rows, hs], dy_ref[rows, hs]
            r, oh = _rms_stats(ov)
            sg = _sigmoid(z)
            don = dout * (z * sg)
            dz_ref[rows, hs] = _bf(dout * (oh * gn_) * (sg * (1.0 + z * (1.0 - sg))))
            dgn_parts.append(jnp.sum(don * oh, axis=0, keepdims=True))
            dn = don * gn_
            do = _bf(r * (dn - oh * jnp.mean(dn * oh, axis=-1, keepdims=True)))
            sb = sst_ref[chunk, h]
            s_in = sb.astype(F32)
            ds_out = dstate[h]
            dsb = _bf(ds_out)
            vnb = vn_ref[rows, hs]
            wb, qgb, kdb, ab = w_ref[rows, hs], qg_ref[rows, hs], kd_ref[rows, hs], a_ref[h, rows]
            dvn = _dot_tn(ab, do)[:C] + _dot(kdb, dsb)
            yield
            da_ref[h, rows] = _dot_nt(do, _rows_pad(vnb))
            dqg_ref[rows, hs] = _dot_nt(do, sb)
            dkd_ref[rows, hs] = _dot_nt(vnb, dsb)
            q_do = _dot_tn(qgb, do)
            yield
            dvnb = _bf(dvn)
            dw_ref[rows, hs] = _bf(-_dot_nt(dvnb, sb))
            w_dvn = _dot_tn(wb, dvnb)
            du_ref[rows, hs] = dvnb
            yield
            d_last = jnp.exp(gcs_ref[(chunk + 1) * C - 1:(chunk + 1) * C, h:h + 1])
            dd = jnp.sum(jnp.sum(ds_out * s_in, axis=1, keepdims=True), axis=0, keepdims=True)
            dsc_parts.append(jnp.where((lane == h) & (row == C - 1), dd * d_last, 0.0))
            dstate[h] = ds_out * d_last + q_do - w_dvn

        for chunk in reversed(range(G)):
            dsc_parts = []
            _interleave(head_program(chunk, h, dsc_parts) for h in range(DN_HEADS))
            dsc_ref[chunk * C:(chunk + 1) * C, :] = sum(dsc_parts[1:], dsc_parts[0])
        dgn_ref[...] += sum(dgn_parts[1:], dgn_parts[0])

    rev = lambda n: steps - 1 - n
    wide = pl.BlockSpec((R, 512), lambda n: (rev(n), 0))
    z_spec = pl.BlockSpec((R, 512), lambda n: (rev(n), BLK_Z))
    sq = pl.BlockSpec((DN_HEADS, R, CPAD), lambda n: (0, rev(n), 0))
    narrow = pl.BlockSpec((R, 128), lambda n: (rev(n), 0))
    gn_spec = pl.BlockSpec((1, DN_DIM), lambda n: (0, 0))
    f = lambda *shp: jax.ShapeDtypeStruct(shp, F32)
    b = lambda *shp: jax.ShapeDtypeStruct(shp, BF16)
    return pl.pallas_call(
        body, grid=(steps,), name="dn_scan_bwd",
        in_specs=[pl.BlockSpec((R, 512), lambda n: (rev(n), 1)), wide, z_spec, gn_spec,
                  pl.BlockSpec((G, DN_HEADS, DN_DIM, DN_DIM), lambda n: (rev(n), 0, 0, 0)),
                  wide, wide, wide, wide, sq, narrow, pl.BlockSpec(memory_space=pl.ANY)],
        out_specs=[wide, wide, wide, wide, sq, z_spec, narrow, gn_spec],
        out_shape=[b(S, 512), b(S, 512), f(S, 512), f(S, 512), f(DN_HEADS, S, CPAD),
                   jax.ShapeDtypeStruct(dproj.shape, dproj.dtype), f(S, 128), f(1, DN_DIM)],
        scratch_shapes=[pltpu.VMEM((DN_HEADS, DN_DIM, DN_DIM), F32)],
        input_output_aliases={11: 5},
        compiler_params=_params(("arbitrary",)),
    )(dcat, o, pz, gn, sst, vnew, w, qg, kd, a, gcs, dproj)


def _dn_chunk_bwd(qkv, pg, t_inv, gcs, du, dw, dqg, dkd, da, dsc, a_log, dt_bias, dproj):
    S = qkv.shape[0]
    C = DN_CHUNK
    G = CHUNKS_LOCAL
    R = G * C

    def body(alog_ref, dtb_ref, qkv_ref, pg_ref, t_ref, gcs_ref, du_ref, dw_ref, dqg_ref, dkd_ref, da_ref, dsc_ref, _,
             dqkv_ref, dpg_ref, acc_ref):
        @pl.when(pl.program_id(0) == 0)
        def _():
            acc_ref[...] = jnp.zeros_like(acc_ref)

        ii, jj = _chunk_masks()
        lane = lax.broadcasted_iota(jnp.int32, (1, 128), 1)
        row8 = lax.broadcasted_iota(jnp.int32, (8, 128), 0)
        lane8 = lax.broadcasted_iota(jnp.int32, (8, 128), 1)
        rowc = lax.broadcasted_iota(jnp.int32, (C, 1), 0)
        tril, strict = jj <= ii, jj < ii
        dpg_parts, acc_parts = [[] for _ in range(G)], []

        def head_program(chunk, h):
            rows = slice(chunk * C, (chunk + 1) * C)
            q, k, v = qkv_ref[rows, _head(h)], qkv_ref[rows, _head(DN_HEADS + h)], qkv_ref[rows, _head(2 * DN_HEADS + h)]
            gc_col, beta, g_col = gcs_ref[rows, h:h + 1], gcs_ref[rows, DN_HEADS + h:DN_HEADS + h + 1], \
                gcs_ref[rows, 2 * DN_HEADS + h:2 * DN_HEADS + h + 1]
            dec = _decay(gc_col, ii, jj)
            eg = jnp.exp(gc_col)
            g_last = gc_col[C - 1:C, :]
            ek = jnp.exp(g_last - gc_col)
            kb, vb = k * beta, v * beta
            kbg = kb * eg
            qb, kbb = _bf(q), _bf(kb)
            k_rows = _rows_pad(_bf(k))
            t = t_ref[h, rows]
            tb = _bf(t)
            dub, dwb = du_ref[rows, _head(h)], dw_ref[rows, _head(h)]
            dqg_, dkd_ = dqg_ref[rows, _head(h)], dkd_ref[rows, _head(h)]
            dt = _dot_nt(dub, _rows_pad(_bf(vb))) + _dot_nt(dwb, _rows_pad(_bf(kbg)))
            t_du_dw = _dot_tn(tb, jnp.concatenate([dub, dwb], axis=1))
            dvb, dkbg = t_du_dw[:C, :DN_DIM], t_du_dw[:C, DN_DIM:]
            kk = _dot_nt(kbb, k_rows)
            qk = _dot_nt(qb, k_rows)
            yield
            dt_t = _dot3_nt(dt, t)
            yield
            dl = -_dot3_tn(t, dt_t)
            yield
            dm = jnp.where(strict, dl * dec, 0.0)
            dqk = jnp.where(tril, da_ref[h, rows] * dec, 0.0)
            gmat = dm * kk + dqk * qk
            dgc = jnp.sum(gmat, axis=1, keepdims=True) - _row_to_col(jnp.sum(gmat, axis=0, keepdims=True), ii, jj)
            dmb, dqkb = _bf(dm), _bf(dqk)
            yield
            dkb = _dot(dmb, k_rows) + dkbg * eg
            dk = _dot_tn(jnp.concatenate([dmb, dqkb], axis=0), jnp.concatenate([kbb, qb], axis=0))[:C] + dkd_ * ek
            dq = _dot(dqkb, k_rows) + dqg_ * eg
            yield
            tk = jnp.sum(dkd_ * k * ek, axis=1, keepdims=True)
            dgc = dgc + jnp.sum(dqg_ * q * eg, axis=1, keepdims=True) - tk + jnp.sum(dkbg * kbg, axis=1, keepdims=True)
            dgl = jnp.sum(tk, axis=0, keepdims=True) + dsc_ref[(chunk + 1) * C - 1:(chunk + 1) * C, h:h + 1]
            dgc = dgc + jnp.where(rowc == C - 1, dgl, 0.0)
            yield
            dk = dk + dkb * beta
            dbeta = jnp.sum(dkb * k, axis=1, keepdims=True) + jnp.sum(dvb * v, axis=1, keepdims=True)
            dqkv_ref[rows, _head(h)] = dq
            dqkv_ref[rows, _head(DN_HEADS + h)] = dk
            dqkv_ref[rows, _head(2 * DN_HEADS + h)] = dvb * beta
            dg_col = jnp.sum(jnp.where(jj >= ii, _col_to_row(dgc, ii, jj), 0.0), axis=1, keepdims=True)
            yield
            db = dbeta * beta * (1.0 - beta)
            da_in = dg_col * (-jnp.exp(alog_ref[h])) * _sigmoid(pg_ref[rows, DN_HEADS + h:DN_HEADS + h + 1] + dtb_ref[h])
            dpg_parts[chunk].append(jnp.where(lane == h, db, 0.0) + jnp.where(lane == DN_HEADS + h, da_in, 0.0))
            acc_parts.append(jnp.where((row8 == 0) & (lane8 == h), jnp.sum(dg_col * g_col, axis=0, keepdims=True), 0.0)
                             + jnp.where((row8 == 1) & (lane8 == h), jnp.sum(da_in, axis=0, keepdims=True), 0.0))

        _interleave(head_program(chunk, h) for chunk in range(G) for h in range(DN_HEADS))
        for chunk in range(G):
            dpg = sum(dpg_parts[chunk][1:], dpg_parts[chunk][0])
            dpg_ref[chunk * C:(chunk + 1) * C, :] = _bf(jnp.concatenate([dpg, jnp.zeros_like(dpg)], axis=1))
        acc_ref[...] += sum(acc_parts[1:], acc_parts[0])

    smem = pl.BlockSpec(memory_space=pltpu.SMEM)
    wide = pl.BlockSpec((R, 512), lambda n: (n, 0))
    sq = pl.BlockSpec((DN_HEADS, R, CPAD), lambda n: (0, n, 0))
    narrow = pl.BlockSpec((R, 128), lambda n: (n, 0))
    qkv_spec = pl.BlockSpec((R, 1536), lambda n: (n, 0))
    f = lambda *shp: jax.ShapeDtypeStruct(shp, F32)
    return pl.pallas_call(
        body, grid=(S // R,), name="dn_chunk_bwd",
        in_specs=[smem, smem, qkv_spec, pl.BlockSpec((R, 128), lambda n: (n, BLK_G)), sq, narrow, wide, wide, wide, wide, sq,
                  narrow, pl.BlockSpec(memory_space=pl.ANY)],
        out_specs=[qkv_spec, pl.BlockSpec((R, 256), lambda n: (n, BLK_G_PAD)), pl.BlockSpec((8, 128), lambda n: (0, 0))],
        out_shape=[f(S, 1536), jax.ShapeDtypeStruct(dproj.shape, dproj.dtype), f(8, 128)],
        input_output_aliases={12: 1},
        compiler_params=_params(("arbitrary",)),
    )(a_log, dt_bias, qkv, pg, t_inv, gcs, du, dw, dqg, dkd, da, dsc, dproj)


def _w_in_to_internal(wt):
    return jnp.concatenate([wt[0:512], wt[2304:2816], wt[768:2304], wt[512:768], wt[2816:2824],
                            jnp.zeros((D_IN_PAD - D_IN, wt.shape[1]), wt.dtype)], axis=0)


def _w_in_from_internal(gt):
    return jnp.concatenate([gt[0:512], gt[2560:2816], gt[1024:2560], gt[512:1024], gt[2816:2824]], axis=0)


def _local_step(x, p, target, wts, first_weights, other_weights, ship_early):
    S = x.shape[0]
    cos, sin = _rope_tables(S)
    sinks, a_log, dt_bias = wts["sinks"].reshape(8), wts["a_log"].reshape(4), wts["dt_bias"].reshape(4)
    gn = wts["dn_norm"].reshape(1, DN_DIM)
    add = lambda acc, res: (acc + res,)

    u = _rmsnorm_fwd(x, wts["norm_mix"], "norm_mix_fwd")
    w_in_t, conv_w = first_weights(u)
    proj, = _mm(u, w_in_t, form="nt", name="in_proj", out_dtypes=[F32], tn=512)
    attn, lse = _attn_fwd(proj, cos, sin, sinks)
    qkv = _dn_prep_fwd(proj, conv_w)
    cw, cu, cqg, ckd, ca, ct, gcs = _dn_chunk_fwd(qkv, proj, a_log, dt_bias)
    o, vnew, sst, dn_out = _dn_scan_fwd(cw, cu, cqg, ckd, ca, gcs, proj, gn)
    w_o, = other_weights(("w_o",), dn_out)
    h1, = _mm([attn, dn_out], w_o, form="nn", name="out_proj", out_dtypes=[F32], tn=512, epi=add, extra=[x])

    w_up, w_down = other_weights(("w_up", "w_down"), h1)
    hid, relu, m, h2 = _mlp_fwd(h1, w_up, w_down, wts["norm_mlp"])
    w_pg, w_pp = other_weights(("w_ple_gate", "w_ple_proj"), h2)
    n3, dh2, dgl, dpp, loss, d_norm_final, d_norm_ple = _ple_and_loss(h2, p, target, w_pg, w_pp, wts["norm_ple"],
                                                                     wts["norm_final"].reshape(1, D_MODEL))
    g = {"norm_final": d_norm_final, "norm_ple": d_norm_ple}
    early = {"w_ple_gate": _mm_tn(n3, dgl, name="d_w_ple_gate", tm=512, tn=1024, out_dtype=BF16).reshape(N_DEV, 128, 1024),
             "w_ple_proj": _mm_tn(p, dpp, name="d_w_ple_proj", tm=256, tn=128, out_dtype=BF16, column_shards=True)}
    d_act, = _mm(dh2, w_down, form="nt", name="d_hidden", out_dtypes=[BF16], tn=512,
                 epi=lambda acc, r: (acc * (2.0 * r.astype(F32)),), extra=[relu])
    early["w_down"] = _mm_tn(hid, dh2, name="d_w_down", tm=512, tn=1024, out_dtype=BF16).reshape(N_DEV, 512, 1024)
    early["w_up"] = _mm_tn(m, d_act, name="d_w_up", tm=1024, tn=512, out_dtype=BF16, column_shards=True)
    token = ship_early(early)
    dh1, g["norm_mlp"], dcat = _mm(d_act, w_up, form="nt", name="d_m", out_dtypes=[F32], tn=512, after=token,
                                   norm_bwd=(h1, wts["norm_mlp"], dh2), then_nt=w_o)
    d_w_o = jnp.concatenate([_mm_tn(attn, dh1, name="d_w_o_attn", tm=512, tn=512, out_dtype=BF16),
                             _mm_tn(dn_out, dh1, name="d_w_o_dn", tm=512, tn=512, out_dtype=BF16)], axis=0)
    token = ship_early({"w_o": d_w_o.reshape(N_DEV, 128, 1024)})
    dproj, dk, dv, dsinks = _attn_bwd(proj, cos, sin, sinks + token[0, 0], dcat, attn, lse)
    g["sinks"] = dsinks[:, 0].reshape(1, 8)
    du_, dw_, dqg, dkd, da, dproj, dsc, g["dn_norm"] = _dn_scan_bwd(dcat, o, proj, gn, sst, vnew, cw, cqg, ckd, ca, gcs, dproj)
    dqkv, dproj, gate_acc = _dn_chunk_bwd(qkv, proj, ct, gcs, du_, dw_, dqg, dkd, da, dsc, a_log, dt_bias, dproj)
    g["a_log"], g["dt_bias"] = gate_acc[0:1, 0:4], gate_acc[1:2, 0:4]
    dproj, g["conv_w"] = _dn_prep_bwd(proj, conv_w, dqkv, dproj, dk, dv)
    token = ship_early({"w_in": _mm_tn(dproj, u, name="d_w_in", tm=512, tn=1024, out_dtype=BF16)})
    grad_x, g["norm_mix"] = _mm(dproj, w_in_t, form="nn", name="d_u", out_dtypes=[F32], tn=512, after=token,
                                norm_bwd=(x, wts["norm_mix"], dh1))
    return loss, grad_x, g


def _peer(k):
    x, y, c = lax.axis_index("x"), lax.axis_index("y"), lax.axis_index("c")
    px = 1 - x if k & 4 else x
    py = 1 - y if k & 2 else y
    pc = 1 - c if k & 1 else c
    return (px, py, pc), 4 * px + 2 * py + pc


def _exchange(srcs, name, gather):
    n = len(srcs)
    gathers = list(gather) if isinstance(gather, (list, tuple)) else [gather] * n
    shapes = [(N_DEV,) + s.shape if gt else s.shape for s, gt in zip(srcs, gathers)]

    def body(*refs):
        src_refs, out_refs = refs[:n], refs[n:2 * n]
        send_sems, recv_sems, local_sems = refs[2 * n:]
        _, me = _peer(0)
        piece = lambda a, d: src_refs[a] if gathers[a] else src_refs[a].at[d]
        local = [pltpu.make_async_copy(piece(a, me), out_refs[a].at[me], local_sems.at[a]) for a in range(n)]
        for cp in local:
            cp.start()
        copies = []
        for a in range(n):
            for k in range(1, N_DEV):
                dev, idx = _peer(k)
                cp = pltpu.make_async_remote_copy(src_ref=piece(a, idx), dst_ref=out_refs[a].at[me],
                                                  send_sem=send_sems.at[a, k - 1], recv_sem=recv_sems.at[a, k - 1],
                                                  device_id=dev, device_id_type=MESH)
                cp.start()
                copies.append(cp)
        for cp in copies:
            cp.wait_recv()
        for cp in copies:
            cp.wait_send()
        for cp in local:
            cp.wait()

    anywhere = pl.BlockSpec(memory_space=pl.ANY)
    return pl.pallas_call(
        body, name=name, in_specs=[anywhere] * n, out_specs=[anywhere] * n,
        out_shape=[jax.ShapeDtypeStruct(shp, s.dtype) for shp, s in zip(shapes, srcs)],
        scratch_shapes=[pltpu.SemaphoreType.DMA((n, N_DEV - 1)), pltpu.SemaphoreType.DMA((n, N_DEV - 1)),
                        pltpu.SemaphoreType.DMA((n,))],
    )(*srcs)


_HBM = pl.BlockSpec(memory_space=pltpu.HBM)
_SEM = pl.BlockSpec(memory_space=pltpu.SEMAPHORE)
_EFFECT = pltpu.SideEffectType.DATAFLOW_SIDE_EFFECTING


def _split_copies(src_refs, land_refs, send_sems, recv_sems, modes, which=None):
    _, me = _peer(0)
    copies = []
    which = range(len(src_refs)) if which is None else which
    for a, src, land in zip(which, src_refs, land_refs):
        if modes[a] == "columns":
            n_cols = src.shape[1]
            dst = land.at[:, pl.ds(pl.multiple_of(me * n_cols, n_cols), n_cols)]
        else:
            dst = land.at[me]
        for k in ((2, 4, 6) if modes[a] == "chips" else range(1, N_DEV)):
            dev, idx = _peer(k)
            sem = a * (N_DEV - 1) + k - 1
            copies.append(pltpu.make_async_remote_copy(
                src_ref=src.at[idx] if modes[a] == "pieces" else src, dst_ref=dst, send_sem=send_sems.at[sem],
                recv_sem=recv_sems.at[sem], device_id=dev, device_id_type=MESH))
    return copies


def _forward_copies(land_refs, send_sems, recv_sems):
    c = lax.axis_index("c")
    sibling, _ = _peer(1)
    copies = []
    for a, land in enumerate(land_refs):
        for chip in range(N_DEV // 2):
            slot = 2 * chip + c
            sem = a * (N_DEV // 2) + chip
            copies.append(pltpu.make_async_remote_copy(
                src_ref=land.at[slot], dst_ref=land.at[slot], send_sem=send_sems.at[sem], recv_sem=recv_sems.at[sem],
                device_id=sibling, device_id_type=MESH))
    return copies


def _forward_start(lands, name):
    n = len(lands)

    def body(*refs):
        for cp in _forward_copies(refs[:n], refs[n], refs[n + 1]):
            cp.start()
        refs[-1][...] = jnp.zeros_like(refs[-1])

    sems = pltpu.SemaphoreType.DMA((n * (N_DEV // 2),))
    out = pl.pallas_call(
        body, name=name,
        out_shape=(sems, sems, *[pltpu.HBM(t.shape, t.dtype) for t in lands], jax.ShapeDtypeStruct((8, 128), F32)),
        in_specs=[_HBM] * n, out_specs=(_SEM, _SEM, *[_HBM] * n, pl.BlockSpec(memory_space=pltpu.VMEM)),
        input_output_aliases={i: 2 + i for i in range(n)},
        compiler_params=pltpu.CompilerParams(has_side_effects=_EFFECT),
    )(*[pltpu.with_memory_space_constraint(t, pltpu.HBM) for t in lands])
    return out[:-1], out[-1]


def _forward_wait(handle, after, name):
    send_sems, recv_sems, *lands = handle
    n = len(lands)

    def body(*refs):
        for cp in _forward_copies(refs[:n], refs[n], refs[n + 1]):
            cp.wait_send()
            cp.wait_recv()

    return list(pl.pallas_call(
        body, name=name, out_shape=tuple(pltpu.HBM(t.shape, t.dtype) for t in lands),
        in_specs=[_HBM] * n + [_SEM, _SEM, pl.BlockSpec(memory_space=pl.ANY)], out_specs=tuple([_HBM] * n),
        input_output_aliases={i: i for i in range(n)},
        compiler_params=pltpu.CompilerParams(has_side_effects=_EFFECT),
    )(*lands, send_sems, recv_sems, after))


def _exchange_start(srcs, name, modes):
    n = len(srcs)
    modes = [modes] * n if isinstance(modes, str) else list(modes)
    me = 4 * lax.axis_index("x") + 2 * lax.axis_index("y") + lax.axis_index("c")
    lands = []
    for s, mode in zip(srcs, modes):
        if mode == "columns":
            empty = lax.empty((s.shape[0], N_DEV * s.shape[1]), s.dtype)
            lands.append(lax.dynamic_update_slice(empty, s, (0, me * s.shape[1])))
        else:
            own = s if mode != "pieces" else lax.dynamic_index_in_dim(s, me, 0, keepdims=False)
            shape = (N_DEV,) + s.shape if mode != "pieces" else s.shape
            lands.append(lax.dynamic_update_index_in_dim(lax.empty(shape, s.dtype), own, me, 0))

    def body(*refs):
        src_refs, land_refs = refs[:n], refs[n:2 * n]
        send_sems, recv_sems = refs[2 * n], refs[2 * n + 1]
        for cp in _split_copies(src_refs, land_refs, send_sems, recv_sems, modes):
            cp.start()
        refs[-1][...] = jnp.zeros_like(refs[-1])

    both = list(srcs) + lands
    sems = pltpu.SemaphoreType.DMA((n * (N_DEV - 1),))
    out = pl.pallas_call(
        body, name=name,
        out_shape=(sems, sems, *[pltpu.HBM(t.shape, t.dtype) for t in both], jax.ShapeDtypeStruct((8, 128), F32)),
        in_specs=[_HBM] * (2 * n), out_specs=(_SEM, _SEM, *[_HBM] * (2 * n), pl.BlockSpec(memory_space=pltpu.VMEM)),
        input_output_aliases={i: 2 + i for i in range(2 * n)},
        compiler_params=pltpu.CompilerParams(has_side_effects=_EFFECT),
    )(*[pltpu.with_memory_space_constraint(t, pltpu.HBM) for t in both])
    return (n, modes, out[:-1]), out[-1]


def _exchange_wait(handle, after, name, which=None):
    n_all, modes, (send_sems, recv_sems, *both_all) = handle
    which = list(range(n_all)) if which is None else list(which)
    n = len(which)
    both = [both_all[a] for a in which] + [both_all[n_all + a] for a in which]

    def body(*refs):
        src_refs, land_refs = refs[:n], refs[n:2 * n]
        for cp in _split_copies(src_refs, land_refs, refs[2 * n], refs[2 * n + 1], modes, which):
            cp.wait_send()
            cp.wait_recv()

    out = pl.pallas_call(
        body, name=name, out_shape=tuple(pltpu.HBM(t.shape, t.dtype) for t in both),
        in_specs=[_HBM] * (2 * n) + [_SEM, _SEM, pl.BlockSpec(memory_space=pl.ANY)], out_specs=tuple([_HBM] * (2 * n)),
        input_output_aliases={i: i for i in range(2 * n)},
        compiler_params=pltpu.CompilerParams(has_side_effects=_EFFECT),
    )(*both, send_sems, recv_sems, after)
    return list(out[n:])


def _cast_all(arrays):
    def body(*refs):
        for src, dst in zip(refs[:len(arrays)], refs[len(arrays):]):
            dst[...] = _bf(src[...])

    return pl.pallas_call(body, name="cast_shards", out_shape=[jax.ShapeDtypeStruct(a.shape, BF16) for a in arrays],
                          compiler_params=pltpu.CompilerParams(vmem_limit_bytes=VMEM_LIMIT))(*arrays)


def _adam_update(g, w, m, v):
    nm = ADAM_B1 * m + (1.0 - ADAM_B1) * g
    nv = ADAM_B2 * v + (1.0 - ADAM_B2) * (g * g)
    m_hat = nm / (1.0 - ADAM_B1 ** ADAM_STEP)
    v_hat = nv / (1.0 - ADAM_B2 ** ADAM_STEP)
    return -ADAM_LR * (m_hat / (jnp.sqrt(v_hat) + ADAM_EPS) + ADAM_WD * w), nm, nv


def _adamw(parts, w, m, v, name):
    n, R, W = parts.shape
    tm = 128 if R % 128 == 0 else R

    def body(p_ref, w_ref, m_ref, v_ref, g_ref, d_ref, nm_ref, nv_ref):
        g = p_ref[0].astype(F32)
        for s in range(1, n):
            g = g + p_ref[s].astype(F32)
        g_ref[...] = g
        d_ref[...], nm_ref[...], nv_ref[...] = _adam_update(g, w_ref[...], m_ref[...], v_ref[...])

    tile = pl.BlockSpec((tm, W), lambda i: (i, 0))
    return pl.pallas_call(
        body, grid=(R // tm,), name=name,
        in_specs=[pl.BlockSpec((n, tm, W), lambda i: (0, i, 0)), tile, tile, tile],
        out_specs=[tile] * 4, out_shape=[jax.ShapeDtypeStruct((R, W), F32)] * 4,
        compiler_params=_params(("parallel",)),
    )(parts, w, m, v)


_MATRICES = ("w_in", "w_o", "w_up", "w_down", "w_ple_gate", "w_ple_proj")


_OTHERS = ("w_o", "w_up", "w_down", "w_ple_gate", "w_ple_proj")
_OTHER_MODES = {"w_o": "slots", "w_up": "slots", "w_down": "slots", "w_ple_gate": "slots", "w_ple_proj": "columns"}


_VECTORS = ("norm_mix", "norm_mlp", "norm_ple", "norm_final", "a_log", "dt_bias", "sinks", "dn_norm")
_SMALL_ROWS, _LOSS_ROW, _CONV_ROW = 16, 8, 9


def _pack_small(vectors, loss, conv):
    def body(*refs):
        out = refs[-1]
        out[...] = jnp.zeros_like(out)
        for r, ref in enumerate(refs[:len(_VECTORS)]):
            out[r:r + 1, 0:ref.shape[1]] = ref[...]
        out[_LOSS_ROW:_LOSS_ROW + 1, 0:128] = refs[len(_VECTORS)][...]
        out[_CONV_ROW:_CONV_ROW + 6, :] = refs[len(_VECTORS) + 1][...]

    return pl.pallas_call(body, name="pack_small", out_shape=jax.ShapeDtypeStruct((_SMALL_ROWS, 1024), F32))(*vectors, loss, conv)


def _sum_slots(parts):
    def body(p_ref, o_ref):
        acc = p_ref[0]
        for s in range(1, parts.shape[0]):
            acc = acc + p_ref[s]
        o_ref[...] = acc

    return pl.pallas_call(body, name="sum_small", out_shape=jax.ShapeDtypeStruct(parts.shape[1:], parts.dtype))(parts)


def _adamw_vectors(summed, conv_g, wmv):
    names = _VECTORS + ("conv_w",)
    flat = [a for triple in wmv for a in triple]

    def body(*refs):
        sum_ref, conv_ref = refs[0], refs[1]
        ins, outs = refs[2:2 + len(flat)], refs[2 + len(flat):]
        for i in range(len(names)):
            w_ref, m_ref, v_ref = ins[3 * i:3 * i + 3]
            g = conv_ref[...] if i == len(_VECTORS) else sum_ref[i:i + 1, 0:w_ref.shape[1]]
            outs[4 * i][...] = g
            outs[4 * i + 1][...], outs[4 * i + 2][...], outs[4 * i + 3][...] = _adam_update(g, w_ref[...], m_ref[...], v_ref[...])

    out_shape = [jax.ShapeDtypeStruct(t[0].shape, F32) for t in wmv for _ in range(4)]
    res = pl.pallas_call(body, name="adamw_vectors", out_shape=out_shape)(summed, conv_g, *flat)
    return {n: res[4 * i:4 * i + 4] for i, n in enumerate(names)}


_ORDER = ("norm_mix", "w_in", "conv_w", "a_log", "dt_bias", "dn_norm", "sinks", "w_o", "norm_mlp", "w_up", "w_down",
          "norm_ple", "w_ple_gate", "w_ple_proj", "norm_final")


def kernel(x, p, norm_mix, w_in, conv_w, a_log, dt_bias, dn_norm, sinks, w_o, norm_mlp, w_up, w_down, norm_ple, w_ple_gate, w_ple_proj, norm_final, loss_target, m_norm_mix, m_w_in, m_conv_w, m_a_log, m_dt_bias, m_dn_norm, m_sinks, m_w_o, m_norm_mlp, m_w_up, m_w_down, m_norm_ple, m_w_ple_gate, m_w_ple_proj, m_norm_final, v_norm_mix, v_w_in, v_conv_w, v_a_log, v_dt_bias, v_dn_norm, v_sinks, v_w_o, v_norm_mlp, v_w_up, v_w_down, v_norm_ple, v_w_ple_gate, v_w_ple_proj, v_norm_final):
    w = dict(norm_mix=norm_mix, w_in=w_in[0], conv_w=conv_w[0], a_log=a_log, dt_bias=dt_bias, dn_norm=dn_norm, sinks=sinks,
             w_o=w_o[0], norm_mlp=norm_mlp, w_up=w_up[0], w_down=w_down[0], norm_ple=norm_ple, w_ple_gate=w_ple_gate[0],
             w_ple_proj=w_ple_proj[0], norm_final=norm_final)
    m = dict(norm_mix=m_norm_mix, w_in=m_w_in[0], conv_w=m_conv_w[0], a_log=m_a_log, dt_bias=m_dt_bias, dn_norm=m_dn_norm,
             sinks=m_sinks, w_o=m_w_o[0], norm_mlp=m_norm_mlp, w_up=m_w_up[0], w_down=m_w_down[0], norm_ple=m_norm_ple,
             w_ple_gate=m_w_ple_gate[0], w_ple_proj=m_w_ple_proj[0], norm_final=m_norm_final)
    v = dict(norm_mix=v_norm_mix, w_in=v_w_in[0], conv_w=v_conv_w[0], a_log=v_a_log, dt_bias=v_dt_bias, dn_norm=v_dn_norm,
             sinks=v_sinks, w_o=v_w_o[0], norm_mlp=v_norm_mlp, w_up=v_w_up[0], w_down=v_w_down[0], norm_ple=v_norm_ple,
             w_ple_gate=v_w_ple_gate[0], w_ple_proj=v_w_ple_proj[0], norm_final=v_norm_final)
    me = 4 * lax.axis_index("x") + 2 * lax.axis_index("y") + lax.axis_index("c")
    conv_shard = conv_w.shape[2]

    for d in (w, m, v):
        d["w_in"] = d["w_in"].T
    conv_pad = jnp.pad(w["conv_w"], ((0, 8 - DN_CONV), (0, 256 - conv_shard)))
    shards = _cast_all([w[n] for n in ("w_in",) + _OTHERS])
    gathers, token_gather = _exchange_start([shards[0], conv_pad] + list(shards[1:]), "gather_start",
                                            ["chips", "chips"] + [_OTHER_MODES[n] for n in _OTHERS])
    vectors = dict(w)
    vectors["norm_mix"] = w["norm_mix"] + token_gather[0:1, 0:1]

    def first_weights(after):
        over_ici = _exchange_wait(gathers, after, "gather_first_wait", [0, 1])
        handle, token = _forward_start(over_ici, "gather_first_forward")
        w_in_all, conv_all = _forward_wait(handle, token, "gather_first_forward_wait")
        conv_all = jnp.transpose(conv_all[:, :DN_CONV, :conv_shard], (1, 0, 2)).reshape(DN_CONV, N_DEV * conv_shard)
        return _w_in_to_internal(w_in_all.reshape(D_IN, D_MODEL)), conv_all

    as_taken = {"w_o": lambda t: t.reshape(1024, 1024), "w_up": lambda t: t, "w_down": lambda t: t.reshape(4096, 1024),
                "w_ple_gate": lambda t: t.reshape(1024, 1024), "w_ple_proj": lambda t: t}

    def other_weights(names, after):
        which = [2 + _OTHERS.index(n) for n in names]
        got = _exchange_wait(gathers, after, "gather_wait_" + names[0], which)
        return [as_taken[n](t) for n, t in zip(names, got)]

    shipped = []

    def ship_early(pieces):
        names = tuple(pieces)
        if names == ("w_in",):
            pieces = {"w_in": _w_in_from_internal(pieces["w_in"]).reshape(N_DEV, D_IN // N_DEV, D_MODEL)}
        handle, token = _exchange_start([pieces[n] for n in names], "scatter_start_" + names[0], "pieces")
        shipped.append((names, handle))
        return token

    loss, grad_x, g = _local_step(x[0], p[0, 0], loss_target[0], vectors, first_weights, other_weights, ship_early)

    row = lambda t: t.reshape(1, t.size)
    small = _pack_small([row(g[n]) for n in _VECTORS], loss, g["conv_w"].reshape(6, 1024))
    small_handle, token_small = _exchange_start([small], "gather_small_start", "slots")
    big, after = {}, token_small
    for names, handle in shipped[:-1]:
        for n, r in zip(names, _exchange_wait(handle, after, "scatter_wait_" + names[0])):
            big[n] = _adamw(r, w[n], m[n], v[n], "adamw_" + n)
            after = big[n][1]
    small_all, = _exchange_wait(small_handle, after, "gather_small_wait")
    summed = _sum_slots(small_all)
    conv_g = lax.dynamic_slice(summed[_CONV_ROW:_CONV_ROW + 6].reshape(DN_CONV, N_DEV * conv_shard), (0, me * conv_shard),
                               (DN_CONV, conv_shard))
    small_out = _adamw_vectors(summed, conv_g, [(row(w[n]), row(m[n]), row(v[n])) for n in _VECTORS]
                               + [(w["conv_w"], m["conv_w"], v["conv_w"])])
    names, handle = shipped[-1]
    for n, r in zip(names, _exchange_wait(handle, small_out["conv_w"][0], "scatter_wait_" + names[0])):
        big[n] = _adamw(r, w[n], m[n], v[n], "adamw_" + n)

    result = [summed[_LOSS_ROW, 0], grad_x[None]]
    for i in range(4):
        for n in _ORDER:
            if n == "w_in":
                result.append(big[n][i].T[None])
            elif n in _MATRICES:
                result.append(big[n][i][None])
            elif n == "conv_w":
                result.append(small_out[n][i][None])
            else:
                result.append(small_out[n][i].reshape(w[n].shape))
    return tuple(result)
```

```python
import jax
import jax.numpy as jnp
import numpy as np
from jax import lax
from jax.experimental import pallas as pl
from jax.experimental.pallas import tpu as pltpu

F32, BF16 = jnp.float32, jnp.bfloat16
EPS = 1e-6
D_MODEL = 1024
N_DEV = 8
ATTN_BLOCK = 128
HEAD_PAIR = 128
DN_HEADS = 4
DN_DIM = 128
DN_CHUNK = 64
DN_CONV = 4
ROPE_THETA = 10000.0
D_IN = 2824
D_IN_PAD = 3072
BLK_Q, BLK_Z = 0, 1
BLK_DN, BLK_K, BLK_V, BLK_G = 8, 20, 21, 22
BLK_G_PAD = 11
VMEM_LIMIT = 56 * 1024 * 1024
NEG = -1e30
ADAM_LR, ADAM_B1, ADAM_B2, ADAM_EPS, ADAM_WD, ADAM_STEP = 0.001, 0.9, 0.999, 1e-08, 0.01, 10
MESH = pl.DeviceIdType.MESH


def _bf(x):
    return x.astype(BF16)


def _dot(a, b):
    return jnp.dot(a, b, preferred_element_type=F32)


def _dot_nt(a, b):
    return lax.dot_general(a, b, (((1,), (1,)), ((), ())), preferred_element_type=F32)


def _dot_tn(a, b):
    return lax.dot_general(a, b, (((0,), (0,)), ((), ())), preferred_element_type=F32)


def _sigmoid(x):
    return 1.0 / (1.0 + jnp.exp(-x))


def _params(sem):
    return pltpu.CompilerParams(dimension_semantics=sem, vmem_limit_bytes=VMEM_LIMIT)


def _mm(x, w, *, form, name, out_dtypes, tn, epi=None, extra=(), tm=512, w_row_block=0, after=None, norm=None,
        norm_bwd=None, then_nt=None):
    assert norm is None or norm_bwd is None
    xs = list(x) if isinstance(x, (list, tuple)) else [x]
    nx = len(xs)
    S, K = xs[0].shape
    shards = w.ndim == 3
    N = (w.shape[2] * N_DEV if shards else w.shape[1]) if form == "nn" else w.shape[-2]
    assert not (shards and form == "nn" and tn != w.shape[2]) and (nx == 1 or (form == "nn" and not shards and norm is None))
    r0 = w_row_block * K
    tm = min(tm, S)
    n_extra, n_out = len(extra), len(out_dtypes)
    tile = lambda width: pl.BlockSpec((tm, width), lambda i: (i, 0))
    whole = lambda a: pl.BlockSpec(a.shape, lambda i, nd=a.ndim: (0,) * nd)
    ins, in_specs = [*xs, w, *extra], [tile(K)] * nx + [whole(w)] + [tile(N)] * n_extra
    if norm is not None:
        ins, in_specs = ins + [norm], in_specs + [whole(norm)]
    if norm_bwd is not None:
        ins, in_specs = ins + list(norm_bwd), in_specs + [tile(N), whole(norm_bwd[1]), tile(N)]
    if then_nt is not None:
        ins, in_specs = ins + [then_nt], in_specs + [whole(then_nt)]
    if after is not None:
        ins, in_specs = ins + [after], in_specs + [whole(after)]
    out_shape = [jax.ShapeDtypeStruct((S, N), dt) for dt in out_dtypes]
    out_specs = [tile(N)] * n_out
    if norm is not None:
        out_shape, out_specs = out_shape + [jax.ShapeDtypeStruct((S, K), BF16)], out_specs + [tile(K)]
    if norm_bwd is not None:
        out_shape, out_specs = out_shape + [jax.ShapeDtypeStruct((1, N), F32)], out_specs + [pl.BlockSpec((1, N), lambda i: (0, 0))]
    if then_nt is not None:
        out_shape, out_specs = out_shape + [jax.ShapeDtypeStruct((S, then_nt.shape[0]), F32)], out_specs + [tile(then_nt.shape[0])]

    def product(xb, w_ref, cols, c):
        if form == "nn" and nx > 1:
            return sum(_dot(part, w_ref[r0 + p * K:r0 + (p + 1) * K, cols]) for p, part in enumerate(xb))
        if form == "nn":
            return _dot(xb, w_ref[c] if shards else w_ref[r0:r0 + K, cols])
        if not shards:
            return _dot_nt(xb, w_ref[cols, :])
        ks = w.shape[2]
        acc = _dot_nt(xb[:, 0:ks], w_ref[0, cols, :])
        for s in range(1, N_DEV):
            acc = acc + _dot_nt(xb[:, s * ks:(s + 1) * ks], w_ref[s, cols, :])
        return acc

    def body(*refs):
        x_ref, w_ref = refs[0], refs[nx]
        extra_refs = refs[nx + 1:nx + 1 + n_extra]
        at = nx + 1 + n_extra
        if norm is not None:
            gain_ref, at = refs[at], at + 1
        if norm_bwd is not None:
            (y_ref, ygain_ref, dres_ref), at = refs[at:at + 3], at + 3
        if then_nt is not None:
            w2_ref, at = refs[at], at + 1
        outs = refs[len(ins):]
        if norm is not None:
            _, xh = _rms_stats(x_ref[...])
            xb = _bf(xh * gain_ref[...])
            outs[n_out][...] = xb
        else:
            xb = _bf(x_ref[...]) if nx == 1 else [_bf(r[...]) for r in refs[:nx]]
        for c in range(N // tn):
            cols = slice(c * tn, (c + 1) * tn)
            acc = product(xb, w_ref, cols, c)
            res = epi(acc, *[r[:, cols] for r in extra_refs]) if epi else (acc,)
            for o, r in zip(outs[:n_out], res):
                o[:, cols] = r.astype(o.dtype)
        if norm_bwd is not None:
            dx, dg = _rms_bwd_tile(y_ref[...], ygain_ref[...], outs[0][...])
            outs[0][...] = dres_ref[...] + dx
            dg_ref = outs[n_out]

            @pl.when(pl.program_id(0) == 0)
            def _():
                dg_ref[...] = jnp.zeros_like(dg_ref)

            dg_ref[...] += dg
        if then_nt is not None:
            yb = _bf(outs[0][...])
            for c in range(then_nt.shape[0] // tn):
                cols = slice(c * tn, (c + 1) * tn)
                outs[-1][:, cols] = _dot_nt(yb, w2_ref[cols, :])

    return pl.pallas_call(
        body, grid=(S // tm,), name=name, in_specs=in_specs, out_specs=out_specs, out_shape=out_shape,
        compiler_params=_params(("arbitrary",) if norm_bwd is not None else ("parallel",)),
    )(*ins)


def _mlp_fwd(h1, w_up, w_down, gain):
    S, K = h1.shape
    n_sh, _, fs = w_up.shape
    tm = min(512, S)

    def body(x_ref, wup_ref, wdown_ref, g_ref, hid_ref, relu_ref, m_ref, h2_ref):
        x = x_ref[...]
        _, xh = _rms_stats(x)
        mb = _bf(xh * g_ref[...])
        m_ref[...] = mb
        h2_ref[...] = x
        for c in range(n_sh):
            cols = slice(c * fs, (c + 1) * fs)
            r = jnp.maximum(_dot(mb, wup_ref[c]), 0.0)
            hd = _bf(r * r)
            hid_ref[:, cols] = hd
            relu_ref[:, cols] = _bf(r)
            h2_ref[...] += _dot(hd, wdown_ref[cols, :])

    tile = lambda width: pl.BlockSpec((tm, width), lambda i: (i, 0))
    once = lambda a: pl.BlockSpec(a.shape, lambda i, nd=a.ndim: (0,) * nd, pipeline_mode=pl.Buffered(1))
    F = n_sh * fs
    return pl.pallas_call(
        body, grid=(S // tm,), name="mlp_fwd",
        in_specs=[tile(K), once(w_up), once(w_down), pl.BlockSpec(gain.shape, lambda i: (0, 0))],
        out_specs=[tile(F), tile(F), tile(K), tile(K)],
        out_shape=[jax.ShapeDtypeStruct((S, F), BF16), jax.ShapeDtypeStruct((S, F), BF16),
                   jax.ShapeDtypeStruct((S, K), BF16), jax.ShapeDtypeStruct((S, K), F32)],
        compiler_params=_params(("parallel",)),
    )(h1, w_up, w_down, gain)


def _mm_tn(x, dy, *, name, tm, tn, out_dtype=F32, column_shards=False, after=None):
    S, K = x.shape
    N = dy.shape[1]
    waits = [] if after is None else [after]

    def body(x_ref, dy_ref, *rest):
        rest[-1][...] = _dot_tn(_bf(x_ref[...]), _bf(dy_ref[...])).astype(out_dtype)

    if column_shards:
        out_spec = pl.BlockSpec((None, tm, tn), lambda i, j: (j, i, 0))
        out_shape = jax.ShapeDtypeStruct((N // tn, K, tn), out_dtype)
    else:
        out_spec = pl.BlockSpec((tm, tn), lambda i, j: (i, j))
        out_shape = jax.ShapeDtypeStruct((K, N), out_dtype)
    return pl.pallas_call(
        body, grid=(K // tm, N // tn), name=name,
        in_specs=[pl.BlockSpec((S, tm), lambda i, j: (0, i)), pl.BlockSpec((S, tn), lambda i, j: (0, j))]
        + [pl.BlockSpec(memory_space=pl.ANY)] * len(waits),
        out_specs=out_spec, out_shape=out_shape,
        compiler_params=_params(("parallel", "parallel")),
    )(x, dy, *waits)


def _rowwise(body, *, tiled, full, out_tiled, out_acc, name, tm=512, smem=()):
    S = tiled[0].shape[0]
    tm = min(tm, S)
    n_in = len(smem) + len(tiled) + len(full)

    def kern(*refs):
        @pl.when(pl.program_id(0) == 0)
        def _():
            for r in refs[n_in + len(out_tiled):]:
                r[...] = jnp.zeros_like(r)
        body(*refs)

    in_specs = [pl.BlockSpec(memory_space=pltpu.SMEM) for _ in smem]
    in_specs += [pl.BlockSpec((tm, a.shape[1]), lambda i: (i, 0)) for a in tiled]
    in_specs += [pl.BlockSpec(a.shape, lambda i, nd=a.ndim: (0,) * nd) for a in full]
    out_specs = [pl.BlockSpec((tm, w), lambda i: (i, 0)) for w, _ in out_tiled]
    out_specs += [pl.BlockSpec(shp, lambda i, nd=len(shp): (0,) * nd) for shp, _ in out_acc]
    out_shape = [jax.ShapeDtypeStruct((S, w), dt) for w, dt in out_tiled]
    out_shape += [jax.ShapeDtypeStruct(shp, dt) for shp, dt in out_acc]
    return pl.pallas_call(
        kern, grid=(S // tm,), name=name, in_specs=in_specs, out_specs=out_specs, out_shape=out_shape,
        compiler_params=_params(("arbitrary",)),
    )(*smem, *tiled, *full)


def _rms_stats(x):
    r = lax.rsqrt(jnp.mean(x * x, axis=-1, keepdims=True) + EPS)
    return r, x * r


def _rmsnorm_fwd(x, g, name):
    def body(x_ref, g_ref, o_ref):
        _, xh = _rms_stats(x_ref[...])
        o_ref[...] = _bf(xh * g_ref[...])

    return _rowwise(body, tiled=[x], full=[g], out_tiled=[(x.shape[1], BF16)], out_acc=[], name=name)[0]


def _rms_bwd_tile(x, g, dxn):
    r, xh = _rms_stats(x)
    dg = jnp.sum(dxn * xh, axis=0, keepdims=True)
    dn = dxn * g
    dx = r * (dn - xh * jnp.mean(dn * xh, axis=-1, keepdims=True))
    return dx, dg


def _ple_and_loss(h2, p, target, w_pg, w_pp, g_ple, g_final):
    S, n = h2.shape
    tm = min(512, S)
    tn = 512

    def body(h2_ref, p_ref, t_ref, wpg_ref, wpp_ref, gple_ref, gfin_ref,
             n3_ref, dh_ref, dgl_ref, dpp_ref, loss_ref, dg_ref, dgple_ref, pp, gate, h3):
        @pl.when(pl.program_id(0) == 0)
        def _():
            loss_ref[...] = jnp.zeros_like(loss_ref)
            dg_ref[...] = jnp.zeros_like(dg_ref)
            dgple_ref[...] = jnp.zeros_like(dgple_ref)

        x = h2_ref[...]
        _, xh = _rms_stats(x)
        n3 = _bf(xh * gple_ref[...])
        n3_ref[...] = n3
        pb = _bf(p_ref[...])
        for c in range(n // tn):
            cols = slice(c * tn, (c + 1) * tn)
            pp[:, cols] = _dot(pb, wpp_ref[:, cols])
            gt = _sigmoid(_dot(n3, wpg_ref[:, cols]))
            gate[:, cols] = gt
            h3[:, cols] = x[:, cols] + gt * pp[:, cols]
        y = h3[...]
        _, yh = _rms_stats(y)
        e = yh * gfin_ref[...] - t_ref[...]
        per_tok = jnp.mean(e * e, axis=-1, keepdims=True)
        loss_ref[...] += 0.5 * jnp.sum(per_tok, axis=0, keepdims=True)
        dh, dg = _rms_bwd_tile(y, gfin_ref[...], e * (1.0 / n))
        dg_ref[...] += dg
        gt = gate[...]
        dgl = _bf(dh * pp[...] * gt * (1.0 - gt))
        dgl_ref[...] = dgl
        dpp_ref[...] = _bf(dh * gt)
        for c in range(n // tn):
            cols = slice(c * tn, (c + 1) * tn)
            h3[:, cols] = _dot_nt(dgl, wpg_ref[cols, :])
        dx, dgp = _rms_bwd_tile(x, gple_ref[...], h3[...])
        dh_ref[...] = dh + dx
        dgple_ref[...] += dgp

    tile = lambda width: pl.BlockSpec((tm, width), lambda i: (i, 0))
    whole = lambda a: pl.BlockSpec(a.shape, lambda i, nd=a.ndim: (0,) * nd)
    return pl.pallas_call(
        body, grid=(S // tm,), name="ple_and_loss",
        in_specs=[tile(n), tile(p.shape[1]), tile(n), whole(w_pg), whole(w_pp), whole(g_ple), whole(g_final)],
        out_specs=[tile(n), tile(n), tile(n), tile(n), pl.BlockSpec((1, 128), lambda i: (0, 0)),
                   pl.BlockSpec((1, n), lambda i: (0, 0)), pl.BlockSpec((1, n), lambda i: (0, 0))],
        out_shape=[jax.ShapeDtypeStruct((S, n), BF16), jax.ShapeDtypeStruct((S, n), F32), jax.ShapeDtypeStruct((S, n), BF16),
                   jax.ShapeDtypeStruct((S, n), BF16), jax.ShapeDtypeStruct((1, 128), F32), jax.ShapeDtypeStruct((1, n), F32),
                   jax.ShapeDtypeStruct((1, n), F32)],
        scratch_shapes=[pltpu.VMEM((tm, n), F32)] * 3,
        compiler_params=_params(("arbitrary",)),
    )(h2, p, target, w_pg, w_pp, g_ple, g_final)


def _rope_tables(S):
    half = 32
    inv = (1.0 / (np.float32(ROPE_THETA) ** (np.arange(half, dtype=np.float32) * np.float32(2.0 / 64)))).astype(np.float32)
    ang = np.arange(S).astype(np.float32)[:, None] * inv[None, :]
    cos, sin = np.cos(ang), np.sin(ang)
    return jnp.asarray(np.tile(cos, (1, 4))), jnp.asarray(np.concatenate([-sin, sin, -sin, sin], axis=1))


def _attn_common(i, kc, kp, vc, vp, cc, sc, cp, sp):
    lane = lax.broadcasted_iota(jnp.int32, (1, HEAD_PAIR), 1)
    lane_lo = jnp.bitwise_and(lane, 63) < 32
    slot = [lane < 64, lane >= 64]

    def swap_halves(t):
        return jnp.where(lane_lo, pltpu.roll(t, 96, 1), pltpu.roll(t, 32, 1))

    def rope(t, cos, sin):
        return t * cos + swap_halves(t) * sin

    def unrope(d, cos, sin):
        return d * cos + swap_halves(d * sin)

    k2 = jnp.concatenate([rope(kp, cp, sp), rope(kc, cc, sc)], axis=0)
    v2 = jnp.concatenate([vp, vc], axis=0)
    r = lax.broadcasted_iota(jnp.int32, (ATTN_BLOCK, 2 * ATTN_BLOCK), 0)
    c = lax.broadcasted_iota(jnp.int32, (ATTN_BLOCK, 2 * ATTN_BLOCK), 1)
    valid = (c > r) & (c <= r + ATTN_BLOCK) & jnp.logical_or(c >= ATTN_BLOCK, i > 0)
    ks, vs = {}, {}
    for j in range(2):
        kn = jnp.where(slot[j], k2, 0.0)
        vn = jnp.where(slot[j], v2, 0.0)
        for s in range(2):
            ks[j, s] = _bf(kn if s == j else pltpu.roll(kn, 64, 1))
            vs[j, s] = _bf(vn if s == j else pltpu.roll(vn, 64, 1))
    return slot, rope, unrope, valid, ks, vs


def _attn_probs(scores, valid, sink):
    s = jnp.where(valid, scores * 0.125, NEG)
    m = jnp.maximum(jnp.max(s, axis=1, keepdims=True), sink)
    e = jnp.exp(s - m)
    z = jnp.sum(e, axis=1, keepdims=True) + jnp.exp(sink - m)
    return e * (1.0 / z), m + jnp.log(z)


def _attn_specs(S):
    nb = S // ATTN_BLOCK
    prev = lambda i: jnp.maximum(i - 1, 0)
    blk = lambda w, col, row=(lambda i: i): pl.BlockSpec((ATTN_BLOCK, w), lambda i: (row(i), col))
    in_specs = [pl.BlockSpec(memory_space=pltpu.SMEM),
                blk(512, BLK_Q), blk(128, BLK_K), blk(128, BLK_K, prev), blk(128, BLK_V), blk(128, BLK_V, prev),
                blk(128, 0), blk(128, 0), blk(128, 0, prev), blk(128, 0, prev)]
    return nb, in_specs


def _attn_fwd(pa, cos, sin, sinks):
    S = pa.shape[0]
    nb, in_specs = _attn_specs(S)

    def body(sinks_ref, q_ref, kc_ref, kp_ref, vc_ref, vp_ref, cc_ref, sc_ref, cp_ref, sp_ref, o_ref, lse_ref):
        i = pl.program_id(0)
        lane = lax.broadcasted_iota(jnp.int32, (1, HEAD_PAIR), 1)
        cc, sc = cc_ref[...], sc_ref[...]
        _, rope, _, valid, ks, vs = _attn_common(i, kc_ref[...], kp_ref[...], vc_ref[...], vp_ref[...],
                                                 cc, sc, cp_ref[...], sp_ref[...])
        pair_cols = [slice(HEAD_PAIR * pair, HEAD_PAIR * (pair + 1)) for pair in range(4)]
        qps = [_bf(rope(q_ref[:, cols], cc, sc)) for cols in pair_cols]
        outs, lses = {}, {}

        def head_program(h):
            pair, s = divmod(h, 2)
            j = h // 4
            scores = _dot_nt(qps[pair], ks[j, s])
            yield
            p, lse = _attn_probs(scores, valid, sinks_ref[h])
            outs[h] = _dot(_bf(p), vs[j, s])
            lses[h] = jnp.where(lane == h, lse, 0.0)

        _interleave(head_program(h) for h in range(8))
        for pair, cols in enumerate(pair_cols):
            o_ref[:, cols] = outs[2 * pair] + outs[2 * pair + 1]
        lse_ref[...] = sum((lses[h] for h in range(1, 8)), lses[0])

    return pl.pallas_call(
        body, grid=(nb,), name="attn_fwd", in_specs=in_specs,
        out_specs=[pl.BlockSpec((ATTN_BLOCK, 512), lambda i: (i, 0)), pl.BlockSpec((ATTN_BLOCK, 128), lambda i: (i, 0))],
        out_shape=[jax.ShapeDtypeStruct((S, 512), F32), jax.ShapeDtypeStruct((S, 128), F32)],
        compiler_params=_params(("parallel",)),
    )(sinks, pa, pa, pa, pa, pa, cos, sin, cos, sin)


def _attn_bwd(pa, cos, sin, sinks, dcat, attn, lse):
    S = pa.shape[0]
    nb, in_specs = _attn_specs(S)
    in_specs = in_specs + [pl.BlockSpec((ATTN_BLOCK, 512), lambda i: (i, 0))] * 2 + [pl.BlockSpec((ATTN_BLOCK, 128), lambda i: (i, 0))]

    def body(sinks_ref, q_ref, kc_ref, kp_ref, vc_ref, vp_ref, cc_ref, sc_ref, cp_ref, sp_ref, do_ref, o_ref, lse_ref,
             dq_ref, dk_ref, dv_ref, dsink_ref):
        i = pl.program_id(0)

        @pl.when(i == 0)
        def _():
            dk_ref[...] = jnp.zeros_like(dk_ref)
            dv_ref[...] = jnp.zeros_like(dv_ref)
            dsink_ref[...] = jnp.zeros_like(dsink_ref)

        cc, sc, cp, sp = cc_ref[...], sc_ref[...], cp_ref[...], sp_ref[...]
        slot, rope, unrope, valid, ks, vs = _attn_common(i, kc_ref[...], kp_ref[...], vc_ref[...], vp_ref[...], cc, sc, cp, sp)
        pair_cols = [slice(HEAD_PAIR * pair, HEAD_PAIR * (pair + 1)) for pair in range(4)]
        qps = [_bf(rope(q_ref[:, cols], cc, sc)) for cols in pair_cols]
        dobs = [_bf(do_ref[:, cols]) for cols in pair_cols]
        do_o = [do_ref[:, cols] * o_ref[:, cols] for cols in pair_cols]
        dqs, dks, dvs = {}, {}, {}

        def head_program(h):
            pair, s = divmod(h, 2)
            j = h // 4
            qp, dob = qps[pair], dobs[pair]
            scores = _dot_nt(qp, ks[j, s])
            dp = _dot_nt(dob, vs[j, s])
            yield
            lse_h = lse_ref[:, h:h + 1]
            p = jnp.exp(jnp.where(valid, scores * 0.125, NEG) - lse_h)
            yield
            dr = jnp.sum(jnp.where(slot[s], do_o[pair], 0.0), axis=1, keepdims=True)
            ds = _bf(p * (dp - dr) * 0.125)
            yield
            dsink_ref[h:h + 1, :] += -jnp.sum(jnp.exp(sinks_ref[h] - lse_h) * dr, axis=0, keepdims=True)
            dqs[h] = _dot(ds, ks[j, s])
            dk_h = _dot_tn(ds, qp)
            dv_h = _dot_tn(_bf(p), dob)
            yield
            dk_h, dv_h = jnp.where(slot[s], dk_h, 0.0), jnp.where(slot[s], dv_h, 0.0)
            if s != j:
                dk_h, dv_h = pltpu.roll(dk_h, 64, 1), pltpu.roll(dv_h, 64, 1)
            dks[h], dvs[h] = dk_h, dv_h

        _interleave(head_program(h) for h in range(8))
        dk2 = sum((dks[h] for h in range(1, 8)), dks[0])
        dv2 = sum((dvs[h] for h in range(1, 8)), dvs[0])
        for pair, cols in enumerate(pair_cols):
            dq_ref[:, cols] = _bf(unrope(dqs[2 * pair] + dqs[2 * pair + 1], cc, sc))
        cur = pl.ds(pl.multiple_of(i * ATTN_BLOCK, ATTN_BLOCK), ATTN_BLOCK)
        dk_ref[cur, :] += unrope(dk2[ATTN_BLOCK:], cc, sc)
        dv_ref[cur, :] += dv2[ATTN_BLOCK:]

        @pl.when(i > 0)
        def _():
            prv = pl.ds(pl.multiple_of((i - 1) * ATTN_BLOCK, ATTN_BLOCK), ATTN_BLOCK)
            dk_ref[prv, :] += unrope(dk2[:ATTN_BLOCK], cp, sp)
            dv_ref[prv, :] += dv2[:ATTN_BLOCK]

    whole = lambda w: pl.BlockSpec((S, w), lambda i: (0, 0))
    return pl.pallas_call(
        body, grid=(nb,), name="attn_bwd", in_specs=in_specs,
        out_specs=[pl.BlockSpec((ATTN_BLOCK, 512), lambda i: (i, BLK_Q)), whole(128), whole(128),
                   pl.BlockSpec((8, 128), lambda i: (0, 0))],
        out_shape=[jax.ShapeDtypeStruct((S, D_IN_PAD), BF16), jax.ShapeDtypeStruct((S, 128), F32),
                   jax.ShapeDtypeStruct((S, 128), F32), jax.ShapeDtypeStruct((8, 128), F32)],
        compiler_params=_params(("arbitrary",)),
    )(sinks, pa, pa, pa, pa, pa, cos, sin, cos, sin, dcat, attn, lse)


CONV_ROWS = 512
CONV_PAD = 8


def _conv_silu(scr, w, r0):
    y = w[3:4, :] * scr[pl.ds(CONV_PAD + r0, CONV_ROWS), :]
    for j in range(DN_CONV - 1):
        y = y + w[j:j + 1, :] * scr[pl.ds(CONV_PAD + r0 - 3 + j, CONV_ROWS), :]
    return y


def _dn_prep_fwd(pd, conv_w):
    S = pd.shape[0]
    assert S % CONV_ROWS == 0

    def body(x_ref, w_ref, o_ref, scr):
        b = pl.program_id(0)
        scr[0:CONV_PAD, :] = jnp.zeros((CONV_PAD, DN_DIM), F32)
        scr[pl.ds(CONV_PAD, S), :] = x_ref[...]
        w = w_ref[...]
        q_scale = jnp.where(b < DN_HEADS, DN_DIM ** -0.5, 1.0)
        for r0 in range(0, S, CONV_ROWS):
            y = _conv_silu(scr, w, r0)
            a = y * _sigmoid(y)
            rs = lax.rsqrt(jnp.sum(a * a, axis=1, keepdims=True) + EPS)
            o_ref[pl.ds(r0, CONV_ROWS), :] = a * jnp.where(b < 2 * DN_HEADS, rs * q_scale, 1.0)

    col = pl.BlockSpec((S, DN_DIM), lambda b: (0, b))
    return pl.pallas_call(
        body, grid=(3 * DN_HEADS,), name="dn_prep_fwd",
        in_specs=[pl.BlockSpec((S, DN_DIM), lambda b: (0, BLK_DN + b)), pl.BlockSpec((DN_CONV, DN_DIM), lambda b: (0, b))],
        out_specs=col,
        out_shape=jax.ShapeDtypeStruct((S, 3 * DN_HEADS * DN_DIM), F32),
        scratch_shapes=[pltpu.VMEM((S + CONV_PAD, DN_DIM), F32)],
        compiler_params=_params(("parallel",)),
    )(pd, conv_w)


def _dn_prep_bwd(pd, conv_w, dqkv, dproj, dk, dv):
    S = pd.shape[0]
    NB = 3 * DN_HEADS

    def body(x_ref, w_ref, d_ref, _, dk_ref, dv_ref, dx_ref, dw_ref, scr, dscr):
        b = pl.program_id(0)

        @pl.when(b == NB)
        def _():
            dx_ref[...] = _bf(dk_ref[...])

        @pl.when(b == NB + 1)
        def _():
            dx_ref[...] = _bf(dv_ref[...])

        @pl.when(b < NB)
        def _():
            scr[0:CONV_PAD, :] = jnp.zeros((CONV_PAD, DN_DIM), F32)
            scr[pl.ds(CONV_PAD, S), :] = x_ref[...]
            dscr[pl.ds(S, CONV_PAD), :] = jnp.zeros((CONV_PAD, DN_DIM), F32)
            w = w_ref[...]
            q_scale = jnp.where(b < DN_HEADS, DN_DIM ** -0.5, 1.0)
            is_qk = b < 2 * DN_HEADS
            dw = [jnp.zeros((1, DN_DIM), F32) for _ in range(DN_CONV)]
            for r0 in range(0, S, CONV_ROWS):
                y = _conv_silu(scr, w, r0)
                sg = _sigmoid(y)
                a = y * sg
                dout = d_ref[pl.ds(r0, CONV_ROWS), :]
                rs = lax.rsqrt(jnp.sum(a * a, axis=1, keepdims=True) + EPS)
                da_qk = q_scale * rs * (dout - a * (rs * rs) * jnp.sum(dout * a, axis=1, keepdims=True))
                dy = jnp.where(is_qk, da_qk, dout) * (sg * (1.0 + y * (1.0 - sg)))
                dscr[pl.ds(r0, CONV_ROWS), :] = dy
                for j in range(DN_CONV):
                    dw[j] = dw[j] + jnp.sum(dy * scr[pl.ds(CONV_PAD + r0 - 3 + j, CONV_ROWS), :], axis=0, keepdims=True)
            for j in range(DN_CONV):
                dw_ref[j:j + 1, :] = dw[j]
            for r0 in range(0, S, CONV_ROWS):
                dx = w[3:4, :] * dscr[pl.ds(r0, CONV_ROWS), :]
                for j in range(DN_CONV - 1):
                    dx = dx + w[j:j + 1, :] * dscr[pl.ds(r0 + 3 - j, CONV_ROWS), :]
                dx_ref[pl.ds(r0, CONV_ROWS), :] = _bf(dx)

    own = lambda b: jnp.minimum(b, NB - 1)
    col = pl.BlockSpec((S, DN_DIM), lambda b: (0, own(b)))
    proj_col = pl.BlockSpec((S, DN_DIM), lambda b: (0, BLK_DN + own(b)))
    wcol = pl.BlockSpec((DN_CONV, DN_DIM), lambda b: (0, own(b)))
    whole = pl.BlockSpec((S, DN_DIM), lambda b: (0, 0))
    assert BLK_K == BLK_DN + NB and BLK_V == BLK_K + 1
    return pl.pallas_call(
        body, grid=(NB + 2,), name="dn_prep_bwd",
        in_specs=[proj_col, wcol, col, pl.BlockSpec(memory_space=pl.ANY), whole, whole],
        out_specs=[pl.BlockSpec((S, DN_DIM), lambda b: (0, BLK_DN + b)), wcol],
        out_shape=[jax.ShapeDtypeStruct(dproj.shape, dproj.dtype), jax.ShapeDtypeStruct((DN_CONV, 3 * DN_HEADS * DN_DIM), F32)],
        scratch_shapes=[pltpu.VMEM((S + CONV_PAD, DN_DIM), F32), pltpu.VMEM((S + CONV_PAD, DN_DIM), F32)],
        input_output_aliases={3: 0},
        compiler_params=_params(("arbitrary",)),
    )(pd, conv_w, dqkv, dproj, dk, dv)


CPAD = 128
CHUNKS_LOCAL = 4
CHUNKS_SCAN = 8


def _chunk_masks():
    ii = lax.broadcasted_iota(jnp.int32, (DN_CHUNK, CPAD), 0)
    jj = lax.broadcasted_iota(jnp.int32, (DN_CHUNK, CPAD), 1)
    return ii, jj


def _rows_pad(a):
    return jnp.concatenate([a, jnp.zeros_like(a)], axis=0)


def _hi_lo(a):
    hi = _bf(a)
    return hi, _bf(a - hi.astype(F32))


def _double_step(t, p):
    C = DN_CHUNK
    th, tl = _hi_lo(t)
    ph, pl_ = _hi_lo(p)
    r1 = _dot(jnp.concatenate([th, tl, ph, pl_], axis=0), _rows_pad(ph))
    r2 = _dot(jnp.concatenate([th, ph], axis=0), _rows_pad(pl_))
    return t + (r1[:C] + r1[C:2 * C] + r2[:C]), r1[2 * C:3 * C] + r1[3 * C:] + r2[C:]


def _dot3_nt(a, b):
    C = DN_CHUNK
    ah, al = _hi_lo(a)
    bh, bl = _hi_lo(b)
    r1 = _dot_nt(jnp.concatenate([ah, al], axis=0), _rows_pad(bh))
    return r1[:C] + r1[C:] + _dot_nt(ah, _rows_pad(bl))


def _dot3_tn(a, b):
    C = DN_CHUNK
    ah, al = _hi_lo(a)
    bh, bl = _hi_lo(b)
    return _dot_tn(jnp.concatenate([ah, al, ah], axis=0), jnp.concatenate([bh, bh, bl], axis=0))[:C]


def _interleave(programs):
    programs = list(programs)
    while programs:
        alive = []
        for prog in programs:
            try:
                next(prog)
                alive.append(prog)
            except StopIteration:
                pass
        programs = alive


def _col_to_row(col, ii, jj):
    return jnp.sum(jnp.where(ii == jj, col, 0.0), axis=0, keepdims=True)


def _row_to_col(row, ii, jj):
    return jnp.sum(jnp.where(ii == jj, row, 0.0), axis=1, keepdims=True)


def _decay(gc_col, ii, jj):
    diff = gc_col - _col_to_row(gc_col, ii, jj)
    return jnp.where(jj <= ii, jnp.exp(jnp.where(jj <= ii, diff, 0.0)), 0.0)


def _softplus(x):
    return jnp.maximum(x, 0.0) + jnp.log(1.0 + jnp.exp(-jnp.abs(x)))


def _head(h):
    return slice(DN_DIM * h, DN_DIM * (h + 1))


def _dn_chunk_fwd(qkv, pg, a_log, dt_bias):
    S = qkv.shape[0]
    C = DN_CHUNK
    G = CHUNKS_LOCAL
    R = G * C
    steps = S // R

    def body(alog_ref, dtb_ref, qkv_ref, pg_ref, w_ref, u_ref, qg_ref, kd_ref, a_ref, t_ref, gcs_ref):
        ii, jj = _chunk_masks()
        lane = lax.broadcasted_iota(jnp.int32, (1, 128), 1)
        eye = (ii == jj).astype(F32)
        gcs_parts = [[] for _ in range(G)]

        def head_program(chunk, h):
            rows = slice(chunk * C, (chunk + 1) * C)
            q, k, v = qkv_ref[rows, _head(h)], qkv_ref[rows, _head(DN_HEADS + h)], qkv_ref[rows, _head(2 * DN_HEADS + h)]
            beta = _sigmoid(pg_ref[rows, h:h + 1])
            g_col = -jnp.exp(alog_ref[h]) * _softplus(pg_ref[rows, DN_HEADS + h:DN_HEADS + h + 1] + dtb_ref[h])
            g_row = _col_to_row(g_col, ii, jj)
            gc_col = jnp.sum(jnp.where(jj <= ii, g_row, 0.0), axis=1, keepdims=True)
            dec = _decay(gc_col, ii, jj)
            eg = jnp.exp(gc_col)
            kb, vb = k * beta, v * beta
            k_rows = _rows_pad(_bf(k))
            kk = _dot_nt(_bf(kb), k_rows)
            qk = _dot_nt(_bf(q), k_rows)
            yield
            t, pw = eye, -jnp.where(jj < ii, kk * dec, 0.0)
            for _ in range(6):
                t, pw = _double_step(t, pw)
                yield
            tb = _bf(t)
            u_ref[rows, _head(h)] = _dot(tb, _rows_pad(_bf(vb)))
            w_ref[rows, _head(h)] = _bf(_dot(tb, _rows_pad(_bf(kb * eg))))
            a_ref[h, rows] = _bf(qk * dec)
            t_ref[h, rows] = t
            qg_ref[rows, _head(h)] = _bf(q * eg)
            kd_ref[rows, _head(h)] = _bf(k * jnp.exp(gc_col[C - 1:C, :] - gc_col))
            gcs_parts[chunk].append(jnp.where(lane == h, gc_col, 0.0) + jnp.where(lane == DN_HEADS + h, beta, 0.0)
                                    + jnp.where(lane == 2 * DN_HEADS + h, g_col, 0.0))

        _interleave(head_program(chunk, h) for chunk in range(G) for h in range(DN_HEADS))
        for chunk in range(G):
            gcs_ref[chunk * C:(chunk + 1) * C, :] = sum(gcs_parts[chunk][1:], gcs_parts[chunk][0])

    smem = pl.BlockSpec(memory_space=pltpu.SMEM)
    wide = pl.BlockSpec((R, 512), lambda n: (n, 0))
    sq = pl.BlockSpec((DN_HEADS, R, CPAD), lambda n: (0, n, 0))
    narrow = pl.BlockSpec((R, 128), lambda n: (n, 0))
    f = lambda *shp: jax.ShapeDtypeStruct(shp, F32)
    b = lambda *shp: jax.ShapeDtypeStruct(shp, BF16)
    return pl.pallas_call(
        body, grid=(steps,), name="dn_chunk_fwd",
        in_specs=[smem, smem, pl.BlockSpec((R, 1536), lambda n: (n, 0)), pl.BlockSpec((R, 128), lambda n: (n, BLK_G))],
        out_specs=[wide, wide, wide, wide, sq, sq, narrow],
        out_shape=[b(S, 512), f(S, 512), b(S, 512), b(S, 512), b(DN_HEADS, S, CPAD), f(DN_HEADS, S, CPAD), f(S, 128)],
        compiler_params=_params(("parallel",)),
    )(a_log, dt_bias, qkv, pg)


def _gated_norm(o, z, gn):
    r, oh = _rms_stats(o)
    return oh * gn * (z * _sigmoid(z))


def _dn_scan_fwd(w, u, qg, kd, a, gcs, pz, gn):
    S = w.shape[0]
    C = DN_CHUNK
    nc = S // C
    G = CHUNKS_SCAN
    R = G * C

    def body(w_ref, u_ref, qg_ref, kd_ref, a_ref, gcs_ref, z_ref, gn_ref, o_ref, vn_ref, sst_ref, out_ref, state):
        @pl.when(pl.program_id(0) == 0)
        def _():
            state[...] = jnp.zeros_like(state)

        def head_program(chunk, h):
            hs = _head(h)
            rows = slice(chunk * C, (chunk + 1) * C)
            s_in = state[h]
            sb = _bf(s_in)
            sst_ref[chunk, h] = sb
            w_s = _dot(w_ref[rows, hs], sb)
            q_s = _dot(qg_ref[rows, hs], sb)
            yield
            vn = u_ref[rows, hs] - w_s
            vnb = _bf(vn)
            o = q_s + _dot(a_ref[h, rows], _rows_pad(vnb))
            k_v = _dot_tn(kd_ref[rows, hs], vnb)
            yield
            state[h] = s_in * jnp.exp(gcs_ref[(chunk + 1) * C - 1:(chunk + 1) * C, h:h + 1]) + k_v
            o_ref[rows, hs] = o
            vn_ref[rows, hs] = vnb
            out_ref[rows, hs] = _bf(_gated_norm(o, z_ref[rows, hs], gn_ref[...]))

        for chunk in range(G):
            _interleave(head_program(chunk, h) for h in range(DN_HEADS))

    wide = pl.BlockSpec((R, 512), lambda n: (n, 0))
    f = lambda *shp: jax.ShapeDtypeStruct(shp, F32)
    b = lambda *shp: jax.ShapeDtypeStruct(shp, BF16)
    return pl.pallas_call(
        body, grid=(nc // G,), name="dn_scan_fwd",
        in_specs=[wide, wide, wide, wide, pl.BlockSpec((DN_HEADS, R, CPAD), lambda n: (0, n, 0)),
                  pl.BlockSpec((R, 128), lambda n: (n, 0)), pl.BlockSpec((R, 512), lambda n: (n, BLK_Z)),
                  pl.BlockSpec((1, DN_DIM), lambda n: (0, 0))],
        out_specs=[wide, wide, pl.BlockSpec((G, DN_HEADS, DN_DIM, DN_DIM), lambda n: (n, 0, 0, 0)), wide],
        out_shape=[f(S, 512), b(S, 512), b(nc, DN_HEADS, DN_DIM, DN_DIM), b(S, 512)],
        scratch_shapes=[pltpu.VMEM((DN_HEADS, DN_DIM, DN_DIM), F32)],
        compiler_params=_params(("arbitrary",)),
    )(w, u, qg, kd, a, gcs, pz, gn)


def _dn_scan_bwd(dcat, o, pz, gn, sst, vnew, w, qg, kd, a, gcs, dproj):
    S = o.shape[0]
    C = DN_CHUNK
    G = CHUNKS_SCAN
    R = G * C
    steps = S // R

    def body(dy_ref, o_ref, z_ref, gn_ref, sst_ref, vn_ref, w_ref, qg_ref, kd_ref, a_ref, gcs_ref, _,
             du_ref, dw_ref, dqg_ref, dkd_ref, da_ref, dz_ref, dsc_ref, dgn_ref, dstate):
        @pl.when(pl.program_id(0) == 0)
        def _():
            dstate[...] = jnp.zeros_like(dstate)
            dgn_ref[...] = jnp.zeros_like(dgn_ref)

        gn_ = gn_ref[...]
        lane = lax.broadcasted_iota(jnp.int32, (C, 128), 1)
        row = lax.broadcasted_iota(jnp.int32, (C, 128), 0)
        dgn_parts = []

        def head_program(chunk, h, dsc_parts):
            hs = _head(h)
            rows = slice(chunk * C, (chunk + 1) * C)
            ov, z, dout = o_ref[rows, hs], z_ref[rows, hs], dy_ref[rows, hs]
            r, oh = _rms_stats(ov)
            sg = _sigmoid(z)
            don = dout * (z * sg)
            dz_ref[rows, hs] = _bf(dout * (oh * gn_) * (sg * (1.0 + z * (1.0 - sg))))
            dgn_parts.append(jnp.sum(don * oh, axis=0, keepdims=True))
            dn = don * gn_
            do = _bf(r * (dn - oh * jnp.mean(dn * oh, axis=-1, keepdims=True)))
            sb = sst_ref[chunk, h]
            s_in = sb.astype(F32)
            ds_out = dstate[h]
            dsb = _bf(ds_out)
            vnb = vn_ref[rows, hs]
            wb, qgb, kdb, ab = w_ref[rows, hs], qg_ref[rows, hs], kd_ref[rows, hs], a_ref[h, rows]
            dvn = _dot_tn(ab, do)[:C] + _dot(kdb, dsb)
            yield
            da_ref[h, rows] = _dot_nt(do, _rows_pad(vnb))
            dqg_ref[rows, hs] = _dot_nt(do, sb)
            dkd_ref[rows, hs] = _dot_nt(vnb, dsb)
            q_do = _dot_tn(qgb, do)
            yield
            dvnb = _bf(dvn)
            dw_ref[rows, hs] = _bf(-_dot_nt(dvnb, sb))
            w_dvn = _dot_tn(wb, dvnb)
            du_ref[rows, hs] = dvnb
            yield
            d_last = jnp.exp(gcs_ref[(chunk + 1) * C - 1:(chunk + 1) * C, h:h + 1])
            dd = jnp.sum(jnp.sum(ds_out * s_in, axis=1, keepdims=True), axis=0, keepdims=True)
            dsc_parts.append(jnp.where((lane == h) & (row == C - 1), dd * d_last, 0.0))
            dstate[h] = ds_out * d_last + q_do - w_dvn

        for chunk in reversed(range(G)):
            dsc_parts = []
            _interleave(head_program(chunk, h, dsc_parts) for h in range(DN_HEADS))
            dsc_ref[chunk * C:(chunk + 1) * C, :] = sum(dsc_parts[1:], dsc_parts[0])
        dgn_ref[...] += sum(dgn_parts[1:], dgn_parts[0])

    rev = lambda n: steps - 1 - n
    wide = pl.BlockSpec((R, 512), lambda n: (rev(n), 0))
    z_spec = pl.BlockSpec((R, 512), lambda n: (rev(n), BLK_Z))
    sq = pl.BlockSpec((DN_HEADS, R, CPAD), lambda n: (0, rev(n), 0))
    narrow = pl.BlockSpec((R, 128), lambda n: (rev(n), 0))
    gn_spec = pl.BlockSpec((1, DN_DIM), lambda n: (0, 0))
    f = lambda *shp: jax.ShapeDtypeStruct(shp, F32)
    b = lambda *shp: jax.ShapeDtypeStruct(shp, BF16)
    return pl.pallas_call(
        body, grid=(steps,), name="dn_scan_bwd",
        in_specs=[pl.BlockSpec((R, 512), lambda n: (rev(n), 1)), wide, z_spec, gn_spec,
                  pl.BlockSpec((G, DN_HEADS, DN_DIM, DN_DIM), lambda n: (rev(n), 0, 0, 0)),
                  wide, wide, wide, wide, sq, narrow, pl.BlockSpec(memory_space=pl.ANY)],
        out_specs=[wide, wide, wide, wide, sq, z_spec, narrow, gn_spec],
        out_shape=[b(S, 512), b(S, 512), f(S, 512), f(S, 512), f(DN_HEADS, S, CPAD),
                   jax.ShapeDtypeStruct(dproj.shape, dproj.dtype), f(S, 128), f(1, DN_DIM)],
        scratch_shapes=[pltpu.VMEM((DN_HEADS, DN_DIM, DN_DIM), F32)],
        input_output_aliases={11: 5},
        compiler_params=_params(("arbitrary",)),
    )(dcat, o, pz, gn, sst, vnew, w, qg, kd, a, gcs, dproj)


def _dn_chunk_bwd(qkv, pg, t_inv, gcs, du, dw, dqg, dkd, da, dsc, a_log, dt_bias, dproj):
    S = qkv.shape[0]
    C = DN_CHUNK
    G = CHUNKS_LOCAL
    R = G * C

    def body(alog_ref, dtb_ref, qkv_ref, pg_ref, t_ref, gcs_ref, du_ref, dw_ref, dqg_ref, dkd_ref, da_ref, dsc_ref, _,
             dqkv_ref, dpg_ref, acc_ref):
        @pl.when(pl.program_id(0) == 0)
        def _():
            acc_ref[...] = jnp.zeros_like(acc_ref)

        ii, jj = _chunk_masks()
        lane = lax.broadcasted_iota(jnp.int32, (1, 128), 1)
        row8 = lax.broadcasted_iota(jnp.int32, (8, 128), 0)
        lane8 = lax.broadcasted_iota(jnp.int32, (8, 128), 1)
        rowc = lax.broadcasted_iota(jnp.int32, (C, 1), 0)
        tril, strict = jj <= ii, jj < ii
        dpg_parts, acc_parts = [[] for _ in range(G)], []

        def head_program(chunk, h):
            rows = slice(chunk * C, (chunk + 1) * C)
            q, k, v = qkv_ref[rows, _head(h)], qkv_ref[rows, _head(DN_HEADS + h)], qkv_ref[rows, _head(2 * DN_HEADS + h)]
            gc_col, beta, g_col = gcs_ref[rows, h:h + 1], gcs_ref[rows, DN_HEADS + h:DN_HEADS + h + 1], \
                gcs_ref[rows, 2 * DN_HEADS + h:2 * DN_HEADS + h + 1]
            dec = _decay(gc_col, ii, jj)
            eg = jnp.exp(gc_col)
            g_last = gc_col[C - 1:C, :]
            ek = jnp.exp(g_last - gc_col)
            kb, vb = k * beta, v * beta
            kbg = kb * eg
            qb, kbb = _bf(q), _bf(kb)
            k_rows = _rows_pad(_bf(k))
            t = t_ref[h, rows]
            tb = _bf(t)
            dub, dwb = du_ref[rows, _head(h)], dw_ref[rows, _head(h)]
            dqg_, dkd_ = dqg_ref[rows, _head(h)], dkd_ref[rows, _head(h)]
            dt = _dot_nt(dub, _rows_pad(_bf(vb))) + _dot_nt(dwb, _rows_pad(_bf(kbg)))
            t_du_dw = _dot_tn(tb, jnp.concatenate([dub, dwb], axis=1))
            dvb, dkbg = t_du_dw[:C, :DN_DIM], t_du_dw[:C, DN_DIM:]
            kk = _dot_nt(kbb, k_rows)
            qk = _dot_nt(qb, k_rows)
            yield
            dt_t = _dot3_nt(dt, t)
            yield
            dl = -_dot3_tn(t, dt_t)
            yield
            dm = jnp.where(strict, dl * dec, 0.0)
            dqk = jnp.where(tril, da_ref[h, rows] * dec, 0.0)
            gmat = dm * kk + dqk * qk
            dgc = jnp.sum(gmat, axis=1, keepdims=True) - _row_to_col(jnp.sum(gmat, axis=0, keepdims=True), ii, jj)
            dmb, dqkb = _bf(dm), _bf(dqk)
            yield
            dkb = _dot(dmb, k_rows) + dkbg * eg
            dk = _dot_tn(jnp.concatenate([dmb, dqkb], axis=0), jnp.concatenate([kbb, qb], axis=0))[:C] + dkd_ * ek
            dq = _dot(dqkb, k_rows) + dqg_ * eg
            yield
            tk = jnp.sum(dkd_ * k * ek, axis=1, keepdims=True)
            dgc = dgc + jnp.sum(dqg_ * q * eg, axis=1, keepdims=True) - tk + jnp.sum(dkbg * kbg, axis=1, keepdims=True)
            dgl = jnp.sum(tk, axis=0, keepdims=True) + dsc_ref[(chunk + 1) * C - 1:(chunk + 1) * C, h:h + 1]
            dgc = dgc + jnp.where(rowc == C - 1, dgl, 0.0)
            yield
            dk = dk + dkb * beta
            dbeta = jnp.sum(dkb * k, axis=1, keepdims=True) + jnp.sum(dvb * v, axis=1, keepdims=True)
            dqkv_ref[rows, _head(h)] = dq
            dqkv_ref[rows, _head(DN_HEADS + h)] = dk
            dqkv_ref[rows, _head(2 * DN_HEADS + h)] = dvb * beta
            dg_col = jnp.sum(jnp.where(jj >= ii, _col_to_row(dgc, ii, jj), 0.0), axis=1, keepdims=True)
            yield
            db = dbeta * beta * (1.0 - beta)
            da_in = dg_col * (-jnp.exp(alog_ref[h])) * _sigmoid(pg_ref[rows, DN_HEADS + h:DN_HEADS + h + 1] + dtb_ref[h])
            dpg_parts[chunk].append(jnp.where(lane == h, db, 0.0) + jnp.where(lane == DN_HEADS + h, da_in, 0.0))
            acc_parts.append(jnp.where((row8 == 0) & (lane8 == h), jnp.sum(dg_col * g_col, axis=0, keepdims=True), 0.0)
                             + jnp.where((row8 == 1) & (lane8 == h), jnp.sum(da_in, axis=0, keepdims=True), 0.0))

        _interleave(head_program(chunk, h) for chunk in range(G) for h in range(DN_HEADS))
        for chunk in range(G):
            dpg = sum(dpg_parts[chunk][1:], dpg_parts[chunk][0])
            dpg_ref[chunk * C:(chunk + 1) * C, :] = _bf(jnp.concatenate([dpg, jnp.zeros_like(dpg)], axis=1))
        acc_ref[...] += sum(acc_parts[1:], acc_parts[0])

    smem = pl.BlockSpec(memory_space=pltpu.SMEM)
    wide = pl.BlockSpec((R, 512), lambda n: (n, 0))
    sq = pl.BlockSpec((DN_HEADS, R, CPAD), lambda n: (0, n, 0))
    narrow = pl.BlockSpec((R, 128), lambda n: (n, 0))
    qkv_spec = pl.BlockSpec((R, 1536), lambda n: (n, 0))
    f = lambda *shp: jax.ShapeDtypeStruct(shp, F32)
    return pl.pallas_call(
        body, grid=(S // R,), name="dn_chunk_bwd",
        in_specs=[smem, smem, qkv_spec, pl.BlockSpec((R, 128), lambda n: (n, BLK_G)), sq, narrow, wide, wide, wide, wide, sq,
                  narrow, pl.BlockSpec(memory_space=pl.ANY)],
        out_specs=[qkv_spec, pl.BlockSpec((R, 256), lambda n: (n, BLK_G_PAD)), pl.BlockSpec((8, 128), lambda n: (0, 0))],
        out_shape=[f(S, 1536), jax.ShapeDtypeStruct(dproj.shape, dproj.dtype), f(8, 128)],
        input_output_aliases={12: 1},
        compiler_params=_params(("arbitrary",)),
    )(a_log, dt_bias, qkv, pg, t_inv, gcs, du, dw, dqg, dkd, da, dsc, dproj)


def _w_in_to_internal(wt):
    return jnp.concatenate([wt[0:512], wt[2304:2816], wt[768:2304], wt[512:768], wt[2816:2824],
                            jnp.zeros((D_IN_PAD - D_IN, wt.shape[1]), wt.dtype)], axis=0)


def _w_in_from_internal(gt):
    return jnp.concatenate([gt[0:512], gt[2560:2816], gt[1024:2560], gt[512:1024], gt[2816:2824]], axis=0)


def _local_step(x, p, target, wts, first_weights, other_weights, ship_early):
    S = x.shape[0]
    cos, sin = _rope_tables(S)
    sinks, a_log, dt_bias = wts["sinks"].reshape(8), wts["a_log"].reshape(4), wts["dt_bias"].reshape(4)
    gn = wts["dn_norm"].reshape(1, DN_DIM)
    add = lambda acc, res: (acc + res,)

    u = _rmsnorm_fwd(x, wts["norm_mix"], "norm_mix_fwd")
    w_in_t, conv_w = first_weights(u)
    proj, = _mm(u, w_in_t, form="nt", name="in_proj", out_dtypes=[F32], tn=512)
    attn, lse = _attn_fwd(proj, cos, sin, sinks)
    qkv = _dn_prep_fwd(proj, conv_w)
    cw, cu, cqg, ckd, ca, ct, gcs = _dn_chunk_fwd(qkv, proj, a_log, dt_bias)
    o, vnew, sst, dn_out = _dn_scan_fwd(cw, cu, cqg, ckd, ca, gcs, proj, gn)
    w_o, = other_weights(("w_o",), dn_out)
    h1, = _mm([attn, dn_out], w_o, form="nn", name="out_proj", out_dtypes=[F32], tn=512, epi=add, extra=[x])

    w_up, w_down = other_weights(("w_up", "w_down"), h1)
    hid, relu, m, h2 = _mlp_fwd(h1, w_up, w_down, wts["norm_mlp"])
    w_pg, w_pp = other_weights(("w_ple_gate", "w_ple_proj"), h2)
    n3, dh2, dgl, dpp, loss, d_norm_final, d_norm_ple = _ple_and_loss(h2, p, target, w_pg, w_pp, wts["norm_ple"],
                                                                     wts["norm_final"].reshape(1, D_MODEL))
    g = {"norm_final": d_norm_final, "norm_ple": d_norm_ple}
    early = {"w_ple_gate": _mm_tn(n3, dgl, name="d_w_ple_gate", tm=512, tn=1024, out_dtype=BF16).reshape(N_DEV, 128, 1024),
             "w_ple_proj": _mm_tn(p, dpp, name="d_w_ple_proj", tm=256, tn=128, out_dtype=BF16, column_shards=True)}
    d_act, = _mm(dh2, w_down, form="nt", name="d_hidden", out_dtypes=[BF16], tn=512,
                 epi=lambda acc, r: (acc * (2.0 * r.astype(F32)),), extra=[relu])
    early["w_down"] = _mm_tn(hid, dh2, name="d_w_down", tm=512, tn=1024, out_dtype=BF16).reshape(N_DEV, 512, 1024)
    early["w_up"] = _mm_tn(m, d_act, name="d_w_up", tm=1024, tn=512, out_dtype=BF16, column_shards=True)
    token = ship_early(early)
    dh1, g["norm_mlp"], dcat = _mm(d_act, w_up, form="nt", name="d_m", out_dtypes=[F32], tn=512, after=token,
                                   norm_bwd=(h1, wts["norm_mlp"], dh2), then_nt=w_o)
    d_w_o = jnp.concatenate([_mm_tn(attn, dh1, name="d_w_o_attn", tm=512, tn=512, out_dtype=BF16),
                             _mm_tn(dn_out, dh1, name="d_w_o_dn", tm=512, tn=512, out_dtype=BF16)], axis=0)
    token = ship_early({"w_o": d_w_o.reshape(N_DEV, 128, 1024)})
    dproj, dk, dv, dsinks = _attn_bwd(proj, cos, sin, sinks + token[0, 0], dcat, attn, lse)
    g["sinks"] = dsinks[:, 0].reshape(1, 8)
    du_, dw_, dqg, dkd, da, dproj, dsc, g["dn_norm"] = _dn_scan_bwd(dcat, o, proj, gn, sst, vnew, cw, cqg, ckd, ca, gcs, dproj)
    dqkv, dproj, gate_acc = _dn_chunk_bwd(qkv, proj, ct, gcs, du_, dw_, dqg, dkd, da, dsc, a_log, dt_bias, dproj)
    g["a_log"], g["dt_bias"] = gate_acc[0:1, 0:4], gate_acc[1:2, 0:4]
    dproj, g["conv_w"] = _dn_prep_bwd(proj, conv_w, dqkv, dproj, dk, dv)
    token = ship_early({"w_in": _mm_tn(dproj, u, name="d_w_in", tm=512, tn=1024, out_dtype=BF16)})
    grad_x, g["norm_mix"] = _mm(dproj, w_in_t, form="nn", name="d_u", out_dtypes=[F32], tn=512, after=token,
                                norm_bwd=(x, wts["norm_mix"], dh1))
    return loss, grad_x, g


def _peer(k):
    x, y, c = lax.axis_index("x"), lax.axis_index("y"), lax.axis_index("c")
    px = 1 - x if k & 4 else x
    py = 1 - y if k & 2 else y
    pc = 1 - c if k & 1 else c
    return (px, py, pc), 4 * px + 2 * py + pc


def _exchange(srcs, name, gather):
    n = len(srcs)
    gathers = list(gather) if isinstance(gather, (list, tuple)) else [gather] * n
    shapes = [(N_DEV,) + s.shape if gt else s.shape for s, gt in zip(srcs, gathers)]

    def body(*refs):
        src_refs, out_refs = refs[:n], refs[n:2 * n]
        send_sems, recv_sems, local_sems = refs[2 * n:]
        _, me = _peer(0)
        piece = lambda a, d: src_refs[a] if gathers[a] else src_refs[a].at[d]
        local = [pltpu.make_async_copy(piece(a, me), out_refs[a].at[me], local_sems.at[a]) for a in range(n)]
        for cp in local:
            cp.start()
        copies = []
        for a in range(n):
            for k in range(1, N_DEV):
                dev, idx = _peer(k)
                cp = pltpu.make_async_remote_copy(src_ref=piece(a, idx), dst_ref=out_refs[a].at[me],
                                                  send_sem=send_sems.at[a, k - 1], recv_sem=recv_sems.at[a, k - 1],
                                                  device_id=dev, device_id_type=MESH)
                cp.start()
                copies.append(cp)
        for cp in copies:
            cp.wait_recv()
        for cp in copies:
            cp.wait_send()
        for cp in local:
            cp.wait()

    anywhere = pl.BlockSpec(memory_space=pl.ANY)
    return pl.pallas_call(
        body, name=name, in_specs=[anywhere] * n, out_specs=[anywhere] * n,
        out_shape=[jax.ShapeDtypeStruct(shp, s.dtype) for shp, s in zip(shapes, srcs)],
        scratch_shapes=[pltpu.SemaphoreType.DMA((n, N_DEV - 1)), pltpu.SemaphoreType.DMA((n, N_DEV - 1)),
                        pltpu.SemaphoreType.DMA((n,))],
    )(*srcs)


_HBM = pl.BlockSpec(memory_space=pltpu.HBM)
_SEM = pl.BlockSpec(memory_space=pltpu.SEMAPHORE)
_EFFECT = pltpu.SideEffectType.DATAFLOW_SIDE_EFFECTING


def _split_copies(src_refs, land_refs, send_sems, recv_sems, modes, which=None):
    _, me = _peer(0)
    local, remote = [], []
    which = range(len(src_refs)) if which is None else which
    for a, src, land in zip(which, src_refs, land_refs):
        if modes[a] == "columns":
            n_cols = src.shape[1]
            dst = land.at[:, pl.ds(pl.multiple_of(me * n_cols, n_cols), n_cols)]
        else:
            dst = land.at[me]
        part = lambda d: src.at[d] if modes[a] == "pieces" else src
        local.append(pltpu.make_async_copy(part(me), dst, recv_sems.at[a * N_DEV]))
        for k in ((2, 4, 6) if modes[a] == "chips" else range(1, N_DEV)):
            dev, idx = _peer(k)
            sem = a * N_DEV + k
            remote.append(pltpu.make_async_remote_copy(
                src_ref=part(idx), dst_ref=dst, send_sem=send_sems.at[sem], recv_sem=recv_sems.at[sem],
                device_id=dev, device_id_type=MESH))
    return local, remote


def _forward_copies(land_refs, send_sems, recv_sems):
    c = lax.axis_index("c")
    sibling, _ = _peer(1)
    copies = []
    for a, land in enumerate(land_refs):
        for chip in range(N_DEV // 2):
            slot = 2 * chip + c
            sem = a * (N_DEV // 2) + chip
            copies.append(pltpu.make_async_remote_copy(
                src_ref=land.at[slot], dst_ref=land.at[slot], send_sem=send_sems.at[sem], recv_sem=recv_sems.at[sem],
                device_id=sibling, device_id_type=MESH))
    return copies


def _forward_start(lands, name):
    n = len(lands)

    def body(*refs):
        for cp in _forward_copies(refs[:n], refs[n], refs[n + 1]):
            cp.start()
        refs[-1][...] = jnp.zeros_like(refs[-1])

    sems = pltpu.SemaphoreType.DMA((n * (N_DEV // 2),))
    out = pl.pallas_call(
        body, name=name,
        out_shape=(sems, sems, *[pltpu.HBM(t.shape, t.dtype) for t in lands], jax.ShapeDtypeStruct((8, 128), F32)),
        in_specs=[_HBM] * n, out_specs=(_SEM, _SEM, *[_HBM] * n, pl.BlockSpec(memory_space=pltpu.VMEM)),
        input_output_aliases={i: 2 + i for i in range(n)},
        compiler_params=pltpu.CompilerParams(has_side_effects=_EFFECT),
    )(*[pltpu.with_memory_space_constraint(t, pltpu.HBM) for t in lands])
    return out[:-1], out[-1]


def _forward_wait(handle, after, name):
    send_sems, recv_sems, *lands = handle
    n = len(lands)

    def body(*refs):
        for cp in _forward_copies(refs[:n], refs[n], refs[n + 1]):
            cp.wait_send()
            cp.wait_recv()

    return list(pl.pallas_call(
        body, name=name, out_shape=tuple(pltpu.HBM(t.shape, t.dtype) for t in lands),
        in_specs=[_HBM] * n + [_SEM, _SEM, pl.BlockSpec(memory_space=pl.ANY)], out_specs=tuple([_HBM] * n),
        input_output_aliases={i: i for i in range(n)},
        compiler_params=pltpu.CompilerParams(has_side_effects=_EFFECT),
    )(*lands, send_sems, recv_sems, after))


def _exchange_start(srcs, name, modes):
    n = len(srcs)
    modes = [modes] * n if isinstance(modes, str) else list(modes)
    lands = []
    for s, mode in zip(srcs, modes):
        shape = {"columns": (s.shape[0], N_DEV * s.shape[1]), "pieces": s.shape}.get(mode, (N_DEV,) + s.shape)
        lands.append(lax.empty(shape, s.dtype))

    def body(*refs):
        src_refs, land_refs = refs[:n], refs[n:2 * n]
        send_sems, recv_sems = refs[2 * n], refs[2 * n + 1]
        local, remote = _split_copies(src_refs, land_refs, send_sems, recv_sems, modes)
        for cp in local + remote:
            cp.start()
        refs[-1][...] = jnp.zeros_like(refs[-1])

    both = list(srcs) + lands
    sems = pltpu.SemaphoreType.DMA((n * N_DEV,))
    out = pl.pallas_call(
        body, name=name,
        out_shape=(sems, sems, *[pltpu.HBM(t.shape, t.dtype) for t in both], jax.ShapeDtypeStruct((8, 128), F32)),
        in_specs=[_HBM] * (2 * n), out_specs=(_SEM, _SEM, *[_HBM] * (2 * n), pl.BlockSpec(memory_space=pltpu.VMEM)),
        input_output_aliases={i: 2 + i for i in range(2 * n)},
        compiler_params=pltpu.CompilerParams(has_side_effects=_EFFECT),
    )(*[pltpu.with_memory_space_constraint(t, pltpu.HBM) for t in both])
    return (n, modes, out[:-1]), out[-1]


def _exchange_wait(handle, after, name, which=None):
    n_all, modes, (send_sems, recv_sems, *both_all) = handle
    which = list(range(n_all)) if which is None else list(which)
    n = len(which)
    both = [both_all[a] for a in which] + [both_all[n_all + a] for a in which]

    def body(*refs):
        src_refs, land_refs = refs[:n], refs[n:2 * n]
        local, remote = _split_copies(src_refs, land_refs, refs[2 * n], refs[2 * n + 1], modes, which)
        for cp in local:
            cp.wait()
        for cp in remote:
            cp.wait_send()
            cp.wait_recv()

    out = pl.pallas_call(
        body, name=name, out_shape=tuple(pltpu.HBM(t.shape, t.dtype) for t in both),
        in_specs=[_HBM] * (2 * n) + [_SEM, _SEM, pl.BlockSpec(memory_space=pl.ANY)], out_specs=tuple([_HBM] * (2 * n)),
        input_output_aliases={i: i for i in range(2 * n)},
        compiler_params=pltpu.CompilerParams(has_side_effects=_EFFECT),
    )(*both, send_sems, recv_sems, after)
    return list(out[n:])


def _cast_all(arrays):
    def body(*refs):
        for src, dst in zip(refs[:len(arrays)], refs[len(arrays):]):
            dst[...] = _bf(src[...])

    return pl.pallas_call(body, name="cast_shards", out_shape=[jax.ShapeDtypeStruct(a.shape, BF16) for a in arrays],
                          compiler_params=pltpu.CompilerParams(vmem_limit_bytes=VMEM_LIMIT))(*arrays)


def _adam_update(g, w, m, v):
    nm = ADAM_B1 * m + (1.0 - ADAM_B1) * g
    nv = ADAM_B2 * v + (1.0 - ADAM_B2) * (g * g)
    m_hat = nm / (1.0 - ADAM_B1 ** ADAM_STEP)
    v_hat = nv / (1.0 - ADAM_B2 ** ADAM_STEP)
    return -ADAM_LR * (m_hat / (jnp.sqrt(v_hat) + ADAM_EPS) + ADAM_WD * w), nm, nv


def _adamw(parts, w, m, v, name):
    n, R, W = parts.shape
    tm = 128 if R % 128 == 0 else R

    def body(p_ref, w_ref, m_ref, v_ref, g_ref, d_ref, nm_ref, nv_ref):
        g = p_ref[0].astype(F32)
        for s in range(1, n):
            g = g + p_ref[s].astype(F32)
        g_ref[...] = g
        d_ref[...], nm_ref[...], nv_ref[...] = _adam_update(g, w_ref[...], m_ref[...], v_ref[...])

    tile = pl.BlockSpec((tm, W), lambda i: (i, 0))
    return pl.pallas_call(
        body, grid=(R // tm,), name=name,
        in_specs=[pl.BlockSpec((n, tm, W), lambda i: (0, i, 0)), tile, tile, tile],
        out_specs=[tile] * 4, out_shape=[jax.ShapeDtypeStruct((R, W), F32)] * 4,
        compiler_params=_params(("parallel",)),
    )(parts, w, m, v)


_MATRICES = ("w_in", "w_o", "w_up", "w_down", "w_ple_gate", "w_ple_proj")


_OTHERS = ("w_o", "w_up", "w_down", "w_ple_gate", "w_ple_proj")
_OTHER_MODES = {"w_o": "slots", "w_up": "slots", "w_down": "slots", "w_ple_gate": "slots", "w_ple_proj": "columns"}


_VECTORS = ("norm_mix", "norm_mlp", "norm_ple", "norm_final", "a_log", "dt_bias", "sinks", "dn_norm")
_SMALL_ROWS, _LOSS_ROW, _CONV_ROW = 16, 8, 9


def _pack_small(vectors, loss, conv):
    def body(*refs):
        out = refs[-1]
        out[...] = jnp.zeros_like(out)
        for r, ref in enumerate(refs[:len(_VECTORS)]):
            out[r:r + 1, 0:ref.shape[1]] = ref[...]
        out[_LOSS_ROW:_LOSS_ROW + 1, 0:128] = refs[len(_VECTORS)][...]
        out[_CONV_ROW:_CONV_ROW + 6, :] = refs[len(_VECTORS) + 1][...]

    return pl.pallas_call(body, name="pack_small", out_shape=jax.ShapeDtypeStruct((_SMALL_ROWS, 1024), F32))(*vectors, loss, conv)


def _sum_slots(parts):
    def body(p_ref, o_ref):
        acc = p_ref[0]
        for s in range(1, parts.shape[0]):
            acc = acc + p_ref[s]
        o_ref[...] = acc

    return pl.pallas_call(body, name="sum_small", out_shape=jax.ShapeDtypeStruct(parts.shape[1:], parts.dtype))(parts)


def _adamw_vectors(summed, conv_g, wmv):
    names = _VECTORS + ("conv_w",)
    flat = [a for triple in wmv for a in triple]

    def body(*refs):
        sum_ref, conv_ref = refs[0], refs[1]
        ins, outs = refs[2:2 + len(flat)], refs[2 + len(flat):]
        for i in range(len(names)):
            w_ref, m_ref, v_ref = ins[3 * i:3 * i + 3]
            g = conv_ref[...] if i == len(_VECTORS) else sum_ref[i:i + 1, 0:w_ref.shape[1]]
            outs[4 * i][...] = g
            outs[4 * i + 1][...], outs[4 * i + 2][...], outs[4 * i + 3][...] = _adam_update(g, w_ref[...], m_ref[...], v_ref[...])

    out_shape = [jax.ShapeDtypeStruct(t[0].shape, F32) for t in wmv for _ in range(4)]
    res = pl.pallas_call(body, name="adamw_vectors", out_shape=out_shape)(summed, conv_g, *flat)
    return {n: res[4 * i:4 * i + 4] for i, n in enumerate(names)}


_ORDER = ("norm_mix", "w_in", "conv_w", "a_log", "dt_bias", "dn_norm", "sinks", "w_o", "norm_mlp", "w_up", "w_down",
          "norm_ple", "w_ple_gate", "w_ple_proj", "norm_final")


def kernel(x, p, norm_mix, w_in, conv_w, a_log, dt_bias, dn_norm, sinks, w_o, norm_mlp, w_up, w_down, norm_ple, w_ple_gate, w_ple_proj, norm_final, loss_target, m_norm_mix, m_w_in, m_conv_w, m_a_log, m_dt_bias, m_dn_norm, m_sinks, m_w_o, m_norm_mlp, m_w_up, m_w_down, m_norm_ple, m_w_ple_gate, m_w_ple_proj, m_norm_final, v_norm_mix, v_w_in, v_conv_w, v_a_log, v_dt_bias, v_dn_norm, v_sinks, v_w_o, v_norm_mlp, v_w_up, v_w_down, v_norm_ple, v_w_ple_gate, v_w_ple_proj, v_norm_final):
    w = dict(norm_mix=norm_mix, w_in=w_in[0], conv_w=conv_w[0], a_log=a_log, dt_bias=dt_bias, dn_norm=dn_norm, sinks=sinks,
             w_o=w_o[0], norm_mlp=norm_mlp, w_up=w_up[0], w_down=w_down[0], norm_ple=norm_ple, w_ple_gate=w_ple_gate[0],
             w_ple_proj=w_ple_proj[0], norm_final=norm_final)
    m = dict(norm_mix=m_norm_mix, w_in=m_w_in[0], conv_w=m_conv_w[0], a_log=m_a_log, dt_bias=m_dt_bias, dn_norm=m_dn_norm,
             sinks=m_sinks, w_o=m_w_o[0], norm_mlp=m_norm_mlp, w_up=m_w_up[0], w_down=m_w_down[0], norm_ple=m_norm_ple,
             w_ple_gate=m_w_ple_gate[0], w_ple_proj=m_w_ple_proj[0], norm_final=m_norm_final)
    v = dict(norm_mix=v_norm_mix, w_in=v_w_in[0], conv_w=v_conv_w[0], a_log=v_a_log, dt_bias=v_dt_bias, dn_norm=v_dn_norm,
             sinks=v_sinks, w_o=v_w_o[0], norm_mlp=v_norm_mlp, w_up=v_w_up[0], w_down=v_w_down[0], norm_ple=v_norm_ple,
             w_ple_gate=v_w_ple_gate[0], w_ple_proj=v_w_ple_proj[0], norm_final=v_norm_final)
    me = 4 * lax.axis_index("x") + 2 * lax.axis_index("y") + lax.axis_index("c")
    conv_shard = conv_w.shape[2]

    for d in (w, m, v):
        d["w_in"] = d["w_in"].T
    conv_pad = jnp.pad(w["conv_w"], ((0, 8 - DN_CONV), (0, 256 - conv_shard)))
    shards = _cast_all([w[n] for n in ("w_in",) + _OTHERS])
    gathers, token_gather = _exchange_start([shards[0], conv_pad] + list(shards[1:]), "gather_start",
                                            ["chips", "chips"] + [_OTHER_MODES[n] for n in _OTHERS])
    vectors = dict(w)
    vectors["norm_mix"] = w["norm_mix"] + token_gather[0:1, 0:1]

    def first_weights(after):
        over_ici = _exchange_wait(gathers, after, "gather_first_wait", [0, 1])
        handle, token = _forward_start(over_ici, "gather_first_forward")
        w_in_all, conv_all = _forward_wait(handle, token, "gather_first_forward_wait")
        conv_all = jnp.transpose(conv_all[:, :DN_CONV, :conv_shard], (1, 0, 2)).reshape(DN_CONV, N_DEV * conv_shard)
        return _w_in_to_internal(w_in_all.reshape(D_IN, D_MODEL)), conv_all

    as_taken = {"w_o": lambda t: t.reshape(1024, 1024), "w_up": lambda t: t, "w_down": lambda t: t.reshape(4096, 1024),
                "w_ple_gate": lambda t: t.reshape(1024, 1024), "w_ple_proj": lambda t: t}

    def other_weights(names, after):
        which = [2 + _OTHERS.index(n) for n in names]
        got = _exchange_wait(gathers, after, "gather_wait_" + names[0], which)
        return [as_taken[n](t) for n, t in zip(names, got)]

    shipped = []

    def ship_early(pieces):
        names = tuple(pieces)
        if names == ("w_in",):
            pieces = {"w_in": _w_in_from_internal(pieces["w_in"]).reshape(N_DEV, D_IN // N_DEV, D_MODEL)}
        handle, token = _exchange_start([pieces[n] for n in names], "scatter_start_" + names[0], "pieces")
        shipped.append((names, handle))
        return token

    loss, grad_x, g = _local_step(x[0], p[0, 0], loss_target[0], vectors, first_weights, other_weights, ship_early)

    row = lambda t: t.reshape(1, t.size)
    small = _pack_small([row(g[n]) for n in _VECTORS], loss, g["conv_w"].reshape(6, 1024))
    small_handle, token_small = _exchange_start([small], "gather_small_start", "slots")
    big, after = {}, token_small
    for names, handle in shipped[:-1]:
        for n, r in zip(names, _exchange_wait(handle, after, "scatter_wait_" + names[0])):
            big[n] = _adamw(r, w[n], m[n], v[n], "adamw_" + n)
            after = big[n][1]
    small_all, = _exchange_wait(small_handle, after, "gather_small_wait")
    summed = _sum_slots(small_all)
    conv_g = lax.dynamic_slice(summed[_CONV_ROW:_CONV_ROW + 6].reshape(DN_CONV, N_DEV * conv_shard), (0, me * conv_shard),
                               (DN_CONV, conv_shard))
    small_out = _adamw_vectors(summed, conv_g, [(row(w[n]), row(m[n]), row(v[n])) for n in _VECTORS]
                               + [(w["conv_w"], m["conv_w"], v["conv_w"])])
    names, handle = shipped[-1]
    for n, r in zip(names, _exchange_wait(handle, small_out["conv_w"][0], "scatter_wait_" + names[0])):
        big[n] = _adamw(r, w[n], m[n], v[n], "adamw_" + n)

    result = [summed[_LOSS_ROW, 0], grad_x[None]]
    for i in range(4):
        for n in _ORDER:
            if n == "w_in":
                result.append(big[n][i].T[None])
            elif n in _MATRICES:
                result.append(big[n][i][None])
            elif n == "conv_w":
                result.append(small_out[n][i][None])
            else:
                result.append(small_out[n][i].reshape(w[n].shape))
    return tuple(result)
```

```python
import jax
import jax.numpy as jnp
import numpy as np
from jax import lax
from jax.experimental import pallas as pl
from jax.experimental.pallas import tpu as pltpu

F32, BF16 = jnp.float32, jnp.bfloat16
EPS = 1e-6
D_MODEL = 1024
N_DEV = 8
ATTN_BLOCK = 128
HEAD_PAIR = 128
DN_HEADS = 4
DN_DIM = 128
DN_CHUNK = 64
DN_CONV = 4
ROPE_THETA = 10000.0
D_IN = 2824
D_IN_PAD = 3072
BLK_Q, BLK_Z = 0, 1
BLK_DN, BLK_K, BLK_V, BLK_G = 8, 20, 21, 22
BLK_G_PAD = 11
VMEM_LIMIT = 56 * 1024 * 1024
NEG = -1e30
ADAM_LR, ADAM_B1, ADAM_B2, ADAM_EPS, ADAM_WD, ADAM_STEP = 0.001, 0.9, 0.999, 1e-08, 0.01, 10
MESH = pl.DeviceIdType.MESH


def _bf(x):
    return x.astype(BF16)


def _dot(a, b):
    return jnp.dot(a, b, preferred_element_type=F32)


def _dot_nt(a, b):
    return lax.dot_general(a, b, (((1,), (1,)), ((), ())), preferred_element_type=F32)


def _dot_tn(a, b):
    return lax.dot_general(a, b, (((0,), (0,)), ((), ())), preferred_element_type=F32)


def _sigmoid(x):
    return 1.0 / (1.0 + jnp.exp(-x))


def _params(sem):
    return pltpu.CompilerParams(dimension_semantics=sem, vmem_limit_bytes=VMEM_LIMIT)


def _mm(x, w, *, form, name, out_dtypes, tn, epi=None, extra=(), tm=512, w_row_block=0, after=None, norm=None,
        norm_bwd=None, then_nt=None):
    assert norm is None or norm_bwd is None
    xs = list(x) if isinstance(x, (list, tuple)) else [x]
    nx = len(xs)
    S, K = xs[0].shape
    shards = w.ndim == 3
    N = (w.shape[2] * N_DEV if shards else w.shape[1]) if form == "nn" else w.shape[-2]
    assert not (shards and form == "nn" and tn != w.shape[2]) and (nx == 1 or (form == "nn" and not shards and norm is None))
    r0 = w_row_block * K
    tm = min(tm, S)
    n_extra, n_out = len(extra), len(out_dtypes)
    tile = lambda width: pl.BlockSpec((tm, width), lambda i: (i, 0))
    whole = lambda a: pl.BlockSpec(a.shape, lambda i, nd=a.ndim: (0,) * nd)
    ins, in_specs = [*xs, w, *extra], [tile(K)] * nx + [whole(w)] + [tile(N)] * n_extra
    if norm is not None:
        ins, in_specs = ins + [norm], in_specs + [whole(norm)]
    if norm_bwd is not None:
        ins, in_specs = ins + list(norm_bwd), in_specs + [tile(N), whole(norm_bwd[1]), tile(N)]
    if then_nt is not None:
        ins, in_specs = ins + [then_nt], in_specs + [whole(then_nt)]
    if after is not None:
        ins, in_specs = ins + [after], in_specs + [whole(after)]
    out_shape = [jax.ShapeDtypeStruct((S, N), dt) for dt in out_dtypes]
    out_specs = [tile(N)] * n_out
    if norm is not None:
        out_shape, out_specs = out_shape + [jax.ShapeDtypeStruct((S, K), BF16)], out_specs + [tile(K)]
    if norm_bwd is not None:
        out_shape, out_specs = out_shape + [jax.ShapeDtypeStruct((1, N), F32)], out_specs + [pl.BlockSpec((1, N), lambda i: (0, 0))]
    if then_nt is not None:
        out_shape, out_specs = out_shape + [jax.ShapeDtypeStruct((S, then_nt.shape[0]), F32)], out_specs + [tile(then_nt.shape[0])]

    def product(xb, w_ref, cols, c):
        if form == "nn" and nx > 1:
            return sum(_dot(part, w_ref[r0 + p * K:r0 + (p + 1) * K, cols]) for p, part in enumerate(xb))
        if form == "nn":
            return _dot(xb, w_ref[c] if shards else w_ref[r0:r0 + K, cols])
        if not shards:
            return _dot_nt(xb, w_ref[cols, :])
        ks = w.shape[2]
        acc = _dot_nt(xb[:, 0:ks], w_ref[0, cols, :])
        for s in range(1, N_DEV):
            acc = acc + _dot_nt(xb[:, s * ks:(s + 1) * ks], w_ref[s, cols, :])
        return acc

    def body(*refs):
        x_ref, w_ref = refs[0], refs[nx]
        extra_refs = refs[nx + 1:nx + 1 + n_extra]
        at = nx + 1 + n_extra
        if norm is not None:
            gain_ref, at = refs[at], at + 1
        if norm_bwd is not None:
            (y_ref, ygain_ref, dres_ref), at = refs[at:at + 3], at + 3
        if then_nt is not None:
            w2_ref, at = refs[at], at + 1
        outs = refs[len(ins):]
        if norm is not None:
            _, xh = _rms_stats(x_ref[...])
            xb = _bf(xh * gain_ref[...])
            outs[n_out][...] = xb
        else:
            xb = _bf(x_ref[...]) if nx == 1 else [_bf(r[...]) for r in refs[:nx]]
        for c in range(N // tn):
            cols = slice(c * tn, (c + 1) * tn)
            acc = product(xb, w_ref, cols, c)
            res = epi(acc, *[r[:, cols] for r in extra_refs]) if epi else (acc,)
            for o, r in zip(outs[:n_out], res):
                o[:, cols] = r.astype(o.dtype)
        if norm_bwd is not None:
            dx, dg = _rms_bwd_tile(y_ref[...], ygain_ref[...], outs[0][...])
            outs[0][...] = dres_ref[...] + dx
            dg_ref = outs[n_out]

            @pl.when(pl.program_id(0) == 0)
            def _():
                dg_ref[...] = jnp.zeros_like(dg_ref)

            dg_ref[...] += dg
        if then_nt is not None:
            yb = _bf(outs[0][...])
            for c in range(then_nt.shape[0] // tn):
                cols = slice(c * tn, (c + 1) * tn)
                outs[-1][:, cols] = _dot_nt(yb, w2_ref[cols, :])

    return pl.pallas_call(
        body, grid=(S // tm,), name=name, in_specs=in_specs, out_specs=out_specs, out_shape=out_shape,
        compiler_params=_params(("arbitrary",) if norm_bwd is not None else ("parallel",)),
    )(*ins)


def _mlp_fwd(h1, w_up, w_down, gain):
    S, K = h1.shape
    n_sh, _, fs = w_up.shape
    tm = min(512, S)

    def body(x_ref, wup_ref, wdown_ref, g_ref, hid_ref, relu_ref, m_ref, h2_ref):
        x = x_ref[...]
        _, xh = _rms_stats(x)
        mb = _bf(xh * g_ref[...])
        m_ref[...] = mb
        h2_ref[...] = x
        for c in range(n_sh):
            cols = slice(c * fs, (c + 1) * fs)
            r = jnp.maximum(_dot(mb, wup_ref[c]), 0.0)
            hd = _bf(r * r)
            hid_ref[:, cols] = hd
            relu_ref[:, cols] = _bf(r)
            h2_ref[...] += _dot(hd, wdown_ref[cols, :])

    tile = lambda width: pl.BlockSpec((tm, width), lambda i: (i, 0))
    once = lambda a: pl.BlockSpec(a.shape, lambda i, nd=a.ndim: (0,) * nd, pipeline_mode=pl.Buffered(1))
    F = n_sh * fs
    return pl.pallas_call(
        body, grid=(S // tm,), name="mlp_fwd",
        in_specs=[tile(K), once(w_up), once(w_down), pl.BlockSpec(gain.shape, lambda i: (0, 0))],
        out_specs=[tile(F), tile(F), tile(K), tile(K)],
        out_shape=[jax.ShapeDtypeStruct((S, F), BF16), jax.ShapeDtypeStruct((S, F), BF16),
                   jax.ShapeDtypeStruct((S, K), BF16), jax.ShapeDtypeStruct((S, K), F32)],
        compiler_params=_params(("parallel",)),
    )(h1, w_up, w_down, gain)


def _mm_tn(x, dy, *, name, tm, tn, out_dtype=F32, column_shards=False, after=None):
    S, K = x.shape
    N = dy.shape[1]
    waits = [] if after is None else [after]

    def body(x_ref, dy_ref, *rest):
        rest[-1][...] = _dot_tn(_bf(x_ref[...]), _bf(dy_ref[...])).astype(out_dtype)

    if column_shards:
        out_spec = pl.BlockSpec((None, tm, tn), lambda i, j: (j, i, 0))
        out_shape = jax.ShapeDtypeStruct((N // tn, K, tn), out_dtype)
    else:
        out_spec = pl.BlockSpec((tm, tn), lambda i, j: (i, j))
        out_shape = jax.ShapeDtypeStruct((K, N), out_dtype)
    return pl.pallas_call(
        body, grid=(K // tm, N // tn), name=name,
        in_specs=[pl.BlockSpec((S, tm), lambda i, j: (0, i)), pl.BlockSpec((S, tn), lambda i, j: (0, j))]
        + [pl.BlockSpec(memory_space=pl.ANY)] * len(waits),
        out_specs=out_spec, out_shape=out_shape,
        compiler_params=_params(("parallel", "parallel")),
    )(x, dy, *waits)


def _rowwise(body, *, tiled, full, out_tiled, out_acc, name, tm=512, smem=()):
    S = tiled[0].shape[0]
    tm = min(tm, S)
    n_in = len(smem) + len(tiled) + len(full)

    def kern(*refs):
        @pl.when(pl.program_id(0) == 0)
        def _():
            for r in refs[n_in + len(out_tiled):]:
                r[...] = jnp.zeros_like(r)
        body(*refs)

    in_specs = [pl.BlockSpec(memory_space=pltpu.SMEM) for _ in smem]
    in_specs += [pl.BlockSpec((tm, a.shape[1]), lambda i: (i, 0)) for a in tiled]
    in_specs += [pl.BlockSpec(a.shape, lambda i, nd=a.ndim: (0,) * nd) for a in full]
    out_specs = [pl.BlockSpec((tm, w), lambda i: (i, 0)) for w, _ in out_tiled]
    out_specs += [pl.BlockSpec(shp, lambda i, nd=len(shp): (0,) * nd) for shp, _ in out_acc]
    out_shape = [jax.ShapeDtypeStruct((S, w), dt) for w, dt in out_tiled]
    out_shape += [jax.ShapeDtypeStruct(shp, dt) for shp, dt in out_acc]
    return pl.pallas_call(
        kern, grid=(S // tm,), name=name, in_specs=in_specs, out_specs=out_specs, out_shape=out_shape,
        compiler_params=_params(("arbitrary",)),
    )(*smem, *tiled, *full)


def _rms_stats(x):
    r = lax.rsqrt(jnp.mean(x * x, axis=-1, keepdims=True) + EPS)
    return r, x * r


def _rmsnorm_fwd(x, g, name):
    def body(x_ref, g_ref, o_ref):
        _, xh = _rms_stats(x_ref[...])
        o_ref[...] = _bf(xh * g_ref[...])

    return _rowwise(body, tiled=[x], full=[g], out_tiled=[(x.shape[1], BF16)], out_acc=[], name=name)[0]


def _rms_bwd_tile(x, g, dxn):
    r, xh = _rms_stats(x)
    dg = jnp.sum(dxn * xh, axis=0, keepdims=True)
    dn = dxn * g
    dx = r * (dn - xh * jnp.mean(dn * xh, axis=-1, keepdims=True))
    return dx, dg


def _ple_and_loss(h2, p, target, w_pg, w_pp, g_ple, g_final):
    S, n = h2.shape
    tm = min(512, S)
    tn = 512

    def body(h2_ref, p_ref, t_ref, wpg_ref, wpp_ref, gple_ref, gfin_ref,
             n3_ref, dh_ref, dgl_ref, dpp_ref, loss_ref, dg_ref, dgple_ref, pp, gate, h3):
        @pl.when(pl.program_id(0) == 0)
        def _():
            loss_ref[...] = jnp.zeros_like(loss_ref)
            dg_ref[...] = jnp.zeros_like(dg_ref)
            dgple_ref[...] = jnp.zeros_like(dgple_ref)

        x = h2_ref[...]
        _, xh = _rms_stats(x)
        n3 = _bf(xh * gple_ref[...])
        n3_ref[...] = n3
        pb = _bf(p_ref[...])
        for c in range(n // tn):
            cols = slice(c * tn, (c + 1) * tn)
            pp[:, cols] = _dot(pb, wpp_ref[:, cols])
            gt = _sigmoid(_dot(n3, wpg_ref[:, cols]))
            gate[:, cols] = gt
            h3[:, cols] = x[:, cols] + gt * pp[:, cols]
        y = h3[...]
        _, yh = _rms_stats(y)
        e = yh * gfin_ref[...] - t_ref[...]
        per_tok = jnp.mean(e * e, axis=-1, keepdims=True)
        loss_ref[...] += 0.5 * jnp.sum(per_tok, axis=0, keepdims=True)
        dh, dg = _rms_bwd_tile(y, gfin_ref[...], e * (1.0 / n))
        dg_ref[...] += dg
        gt = gate[...]
        dgl = _bf(dh * pp[...] * gt * (1.0 - gt))
        dgl_ref[...] = dgl
        dpp_ref[...] = _bf(dh * gt)
        for c in range(n // tn):
            cols = slice(c * tn, (c + 1) * tn)
            h3[:, cols] = _dot_nt(dgl, wpg_ref[cols, :])
        dx, dgp = _rms_bwd_tile(x, gple_ref[...], h3[...])
        dh_ref[...] = dh + dx
        dgple_ref[...] += dgp

    tile = lambda width: pl.BlockSpec((tm, width), lambda i: (i, 0))
    whole = lambda a: pl.BlockSpec(a.shape, lambda i, nd=a.ndim: (0,) * nd)
    return pl.pallas_call(
        body, grid=(S // tm,), name="ple_and_loss",
        in_specs=[tile(n), tile(p.shape[1]), tile(n), whole(w_pg), whole(w_pp), whole(g_ple), whole(g_final)],
        out_specs=[tile(n), tile(n), tile(n), tile(n), pl.BlockSpec((1, 128), lambda i: (0, 0)),
                   pl.BlockSpec((1, n), lambda i: (0, 0)), pl.BlockSpec((1, n), lambda i: (0, 0))],
        out_shape=[jax.ShapeDtypeStruct((S, n), BF16), jax.ShapeDtypeStruct((S, n), F32), jax.ShapeDtypeStruct((S, n), BF16),
                   jax.ShapeDtypeStruct((S, n), BF16), jax.ShapeDtypeStruct((1, 128), F32), jax.ShapeDtypeStruct((1, n), F32),
                   jax.ShapeDtypeStruct((1, n), F32)],
        scratch_shapes=[pltpu.VMEM((tm, n), F32)] * 3,
        compiler_params=_params(("arbitrary",)),
    )(h2, p, target, w_pg, w_pp, g_ple, g_final)


def _rope_tables(S):
    half = 32
    inv = (1.0 / (np.float32(ROPE_THETA) ** (np.arange(half, dtype=np.float32) * np.float32(2.0 / 64)))).astype(np.float32)
    ang = np.arange(S).astype(np.float32)[:, None] * inv[None, :]
    cos, sin = np.cos(ang), np.sin(ang)
    return jnp.asarray(np.tile(cos, (1, 4))), jnp.asarray(np.concatenate([-sin, sin, -sin, sin], axis=1))


def _attn_common(i, kc, kp, vc, vp, cc, sc, cp, sp):
    lane = lax.broadcasted_iota(jnp.int32, (1, HEAD_PAIR), 1)
    lane_lo = jnp.bitwise_and(lane, 63) < 32
    slot = [lane < 64, lane >= 64]

    def swap_halves(t):
        return jnp.where(lane_lo, pltpu.roll(t, 96, 1), pltpu.roll(t, 32, 1))

    def rope(t, cos, sin):
        return t * cos + swap_halves(t) * sin

    def unrope(d, cos, sin):
        return d * cos + swap_halves(d * sin)

    k2 = jnp.concatenate([rope(kp, cp, sp), rope(kc, cc, sc)], axis=0)
    v2 = jnp.concatenate([vp, vc], axis=0)
    r = lax.broadcasted_iota(jnp.int32, (ATTN_BLOCK, 2 * ATTN_BLOCK), 0)
    c = lax.broadcasted_iota(jnp.int32, (ATTN_BLOCK, 2 * ATTN_BLOCK), 1)
    valid = (c > r) & (c <= r + ATTN_BLOCK) & jnp.logical_or(c >= ATTN_BLOCK, i > 0)
    ks, vs = {}, {}
    for j in range(2):
        kn = jnp.where(slot[j], k2, 0.0)
        vn = jnp.where(slot[j], v2, 0.0)
        for s in range(2):
            ks[j, s] = _bf(kn if s == j else pltpu.roll(kn, 64, 1))
            vs[j, s] = _bf(vn if s == j else pltpu.roll(vn, 64, 1))
    return slot, rope, unrope, valid, ks, vs


def _attn_probs(scores, valid, sink):
    s = jnp.where(valid, scores * 0.125, NEG)
    m = jnp.maximum(jnp.max(s, axis=1, keepdims=True), sink)
    e = jnp.exp(s - m)
    z = jnp.sum(e, axis=1, keepdims=True) + jnp.exp(sink - m)
    return e * (1.0 / z), m + jnp.log(z)


def _attn_specs(S):
    nb = S // ATTN_BLOCK
    prev = lambda i: jnp.maximum(i - 1, 0)
    blk = lambda w, col, row=(lambda i: i): pl.BlockSpec((ATTN_BLOCK, w), lambda i: (row(i), col))
    in_specs = [pl.BlockSpec(memory_space=pltpu.SMEM),
                blk(512, BLK_Q), blk(128, BLK_K), blk(128, BLK_K, prev), blk(128, BLK_V), blk(128, BLK_V, prev),
                blk(128, 0), blk(128, 0), blk(128, 0, prev), blk(128, 0, prev)]
    return nb, in_specs


def _attn_fwd(pa, cos, sin, sinks):
    S = pa.shape[0]
    nb, in_specs = _attn_specs(S)

    def body(sinks_ref, q_ref, kc_ref, kp_ref, vc_ref, vp_ref, cc_ref, sc_ref, cp_ref, sp_ref, o_ref, lse_ref):
        i = pl.program_id(0)
        lane = lax.broadcasted_iota(jnp.int32, (1, HEAD_PAIR), 1)
        cc, sc = cc_ref[...], sc_ref[...]
        _, rope, _, valid, ks, vs = _attn_common(i, kc_ref[...], kp_ref[...], vc_ref[...], vp_ref[...],
                                                 cc, sc, cp_ref[...], sp_ref[...])
        pair_cols = [slice(HEAD_PAIR * pair, HEAD_PAIR * (pair + 1)) for pair in range(4)]
        qps = [_bf(rope(q_ref[:, cols], cc, sc)) for cols in pair_cols]
        outs, lses = {}, {}

        def head_program(h):
            pair, s = divmod(h, 2)
            j = h // 4
            scores = _dot_nt(qps[pair], ks[j, s])
            yield
            p, lse = _attn_probs(scores, valid, sinks_ref[h])
            outs[h] = _dot(_bf(p), vs[j, s])
            lses[h] = jnp.where(lane == h, lse, 0.0)

        _interleave(head_program(h) for h in range(8))
        for pair, cols in enumerate(pair_cols):
            o_ref[:, cols] = outs[2 * pair] + outs[2 * pair + 1]
        lse_ref[...] = sum((lses[h] for h in range(1, 8)), lses[0])

    return pl.pallas_call(
        body, grid=(nb,), name="attn_fwd", in_specs=in_specs,
        out_specs=[pl.BlockSpec((ATTN_BLOCK, 512), lambda i: (i, 0)), pl.BlockSpec((ATTN_BLOCK, 128), lambda i: (i, 0))],
        out_shape=[jax.ShapeDtypeStruct((S, 512), F32), jax.ShapeDtypeStruct((S, 128), F32)],
        compiler_params=_params(("parallel",)),
    )(sinks, pa, pa, pa, pa, pa, cos, sin, cos, sin)


def _attn_bwd(pa, cos, sin, sinks, dcat, attn, lse):
    S = pa.shape[0]
    nb, in_specs = _attn_specs(S)
    in_specs = in_specs + [pl.BlockSpec((ATTN_BLOCK, 512), lambda i: (i, 0))] * 2 + [pl.BlockSpec((ATTN_BLOCK, 128), lambda i: (i, 0))]

    def body(sinks_ref, q_ref, kc_ref, kp_ref, vc_ref, vp_ref, cc_ref, sc_ref, cp_ref, sp_ref, do_ref, o_ref, lse_ref,
             dq_ref, dk_ref, dv_ref, dsink_ref):
        i = pl.program_id(0)

        @pl.when(i == 0)
        def _():
            dk_ref[...] = jnp.zeros_like(dk_ref)
            dv_ref[...] = jnp.zeros_like(dv_ref)
            dsink_ref[...] = jnp.zeros_like(dsink_ref)

        cc, sc, cp, sp = cc_ref[...], sc_ref[...], cp_ref[...], sp_ref[...]
        slot, rope, unrope, valid, ks, vs = _attn_common(i, kc_ref[...], kp_ref[...], vc_ref[...], vp_ref[...], cc, sc, cp, sp)
        pair_cols = [slice(HEAD_PAIR * pair, HEAD_PAIR * (pair + 1)) for pair in range(4)]
        qps = [_bf(rope(q_ref[:, cols], cc, sc)) for cols in pair_cols]
        dobs = [_bf(do_ref[:, cols]) for cols in pair_cols]
        do_o = [do_ref[:, cols] * o_ref[:, cols] for cols in pair_cols]
        dqs, dks, dvs = {}, {}, {}

        def head_program(h):
            pair, s = divmod(h, 2)
            j = h // 4
            qp, dob = qps[pair], dobs[pair]
            scores = _dot_nt(qp, ks[j, s])
            dp = _dot_nt(dob, vs[j, s])
            yield
            lse_h = lse_ref[:, h:h + 1]
            p = jnp.exp(jnp.where(valid, scores * 0.125, NEG) - lse_h)
            yield
            dr = jnp.sum(jnp.where(slot[s], do_o[pair], 0.0), axis=1, keepdims=True)
            ds = _bf(p * (dp - dr) * 0.125)
            yield
            dsink_ref[h:h + 1, :] += -jnp.sum(jnp.exp(sinks_ref[h] - lse_h) * dr, axis=0, keepdims=True)
            dqs[h] = _dot(ds, ks[j, s])
            dk_h = _dot_tn(ds, qp)
            dv_h = _dot_tn(_bf(p), dob)
            yield
            dk_h, dv_h = jnp.where(slot[s], dk_h, 0.0), jnp.where(slot[s], dv_h, 0.0)
            if s != j:
                dk_h, dv_h = pltpu.roll(dk_h, 64, 1), pltpu.roll(dv_h, 64, 1)
            dks[h], dvs[h] = dk_h, dv_h

        _interleave(head_program(h) for h in range(8))
        dk2 = sum((dks[h] for h in range(1, 8)), dks[0])
        dv2 = sum((dvs[h] for h in range(1, 8)), dvs[0])
        for pair, cols in enumerate(pair_cols):
            dq_ref[:, cols] = _bf(unrope(dqs[2 * pair] + dqs[2 * pair + 1], cc, sc))
        cur = pl.ds(pl.multiple_of(i * ATTN_BLOCK, ATTN_BLOCK), ATTN_BLOCK)
        dk_ref[cur, :] += unrope(dk2[ATTN_BLOCK:], cc, sc)
        dv_ref[cur, :] += dv2[ATTN_BLOCK:]

        @pl.when(i > 0)
        def _():
            prv = pl.ds(pl.multiple_of((i - 1) * ATTN_BLOCK, ATTN_BLOCK), ATTN_BLOCK)
            dk_ref[prv, :] += unrope(dk2[:ATTN_BLOCK], cp, sp)
            dv_ref[prv, :] += dv2[:ATTN_BLOCK]

    whole = lambda w: pl.BlockSpec((S, w), lambda i: (0, 0))
    return pl.pallas_call(
        body, grid=(nb,), name="attn_bwd", in_specs=in_specs,
        out_specs=[pl.BlockSpec((ATTN_BLOCK, 512), lambda i: (i, BLK_Q)), whole(128), whole(128),
                   pl.BlockSpec((8, 128), lambda i: (0, 0))],
        out_shape=[jax.ShapeDtypeStruct((S, D_IN_PAD), BF16), jax.ShapeDtypeStruct((S, 128), F32),
                   jax.ShapeDtypeStruct((S, 128), F32), jax.ShapeDtypeStruct((8, 128), F32)],
        compiler_params=_params(("arbitrary",)),
    )(sinks, pa, pa, pa, pa, pa, cos, sin, cos, sin, dcat, attn, lse)


CONV_ROWS = 512
CONV_PAD = 8


def _conv_silu(scr, w, r0):
    y = w[3:4, :] * scr[pl.ds(CONV_PAD + r0, CONV_ROWS), :]
    for j in range(DN_CONV - 1):
        y = y + w[j:j + 1, :] * scr[pl.ds(CONV_PAD + r0 - 3 + j, CONV_ROWS), :]
    return y


def _dn_prep_fwd(pd, conv_w):
    S = pd.shape[0]
    assert S % CONV_ROWS == 0

    def body(x_ref, w_ref, o_ref, scr):
        b = pl.program_id(0)
        scr[0:CONV_PAD, :] = jnp.zeros((CONV_PAD, DN_DIM), F32)
        scr[pl.ds(CONV_PAD, S), :] = x_ref[...]
        w = w_ref[...]
        q_scale = jnp.where(b < DN_HEADS, DN_DIM ** -0.5, 1.0)
        for r0 in range(0, S, CONV_ROWS):
            y = _conv_silu(scr, w, r0)
            a = y * _sigmoid(y)
            rs = lax.rsqrt(jnp.sum(a * a, axis=1, keepdims=True) + EPS)
            o_ref[pl.ds(r0, CONV_ROWS), :] = a * jnp.where(b < 2 * DN_HEADS, rs * q_scale, 1.0)

    col = pl.BlockSpec((S, DN_DIM), lambda b: (0, b))
    return pl.pallas_call(
        body, grid=(3 * DN_HEADS,), name="dn_prep_fwd",
        in_specs=[pl.BlockSpec((S, DN_DIM), lambda b: (0, BLK_DN + b)), pl.BlockSpec((DN_CONV, DN_DIM), lambda b: (0, b))],
        out_specs=col,
        out_shape=jax.ShapeDtypeStruct((S, 3 * DN_HEADS * DN_DIM), F32),
        scratch_shapes=[pltpu.VMEM((S + CONV_PAD, DN_DIM), F32)],
        compiler_params=_params(("parallel",)),
    )(pd, conv_w)


def _dn_prep_bwd(pd, conv_w, dqkv, dproj, dk, dv):
    S = pd.shape[0]
    NB = 3 * DN_HEADS

    def body(x_ref, w_ref, d_ref, _, dk_ref, dv_ref, dx_ref, dw_ref, scr, dscr):
        b = pl.program_id(0)

        @pl.when(b == NB)
        def _():
            dx_ref[...] = _bf(dk_ref[...])

        @pl.when(b == NB + 1)
        def _():
            dx_ref[...] = _bf(dv_ref[...])

        @pl.when(b < NB)
        def _():
            scr[0:CONV_PAD, :] = jnp.zeros((CONV_PAD, DN_DIM), F32)
            scr[pl.ds(CONV_PAD, S), :] = x_ref[...]
            dscr[pl.ds(S, CONV_PAD), :] = jnp.zeros((CONV_PAD, DN_DIM), F32)
            w = w_ref[...]
            q_scale = jnp.where(b < DN_HEADS, DN_DIM ** -0.5, 1.0)
            is_qk = b < 2 * DN_HEADS
            dw = [jnp.zeros((1, DN_DIM), F32) for _ in range(DN_CONV)]
            for r0 in range(0, S, CONV_ROWS):
                y = _conv_silu(scr, w, r0)
                sg = _sigmoid(y)
                a = y * sg
                dout = d_ref[pl.ds(r0, CONV_ROWS), :]
                rs = lax.rsqrt(jnp.sum(a * a, axis=1, keepdims=True) + EPS)
                da_qk = q_scale * rs * (dout - a * (rs * rs) * jnp.sum(dout * a, axis=1, keepdims=True))
                dy = jnp.where(is_qk, da_qk, dout) * (sg * (1.0 + y * (1.0 - sg)))
                dscr[pl.ds(r0, CONV_ROWS), :] = dy
                for j in range(DN_CONV):
                    dw[j] = dw[j] + jnp.sum(dy * scr[pl.ds(CONV_PAD + r0 - 3 + j, CONV_ROWS), :], axis=0, keepdims=True)
            for j in range(DN_CONV):
                dw_ref[j:j + 1, :] = dw[j]
            for r0 in range(0, S, CONV_ROWS):
                dx = w[3:4, :] * dscr[pl.ds(r0, CONV_ROWS), :]
                for j in range(DN_CONV - 1):
                    dx = dx + w[j:j + 1, :] * dscr[pl.ds(r0 + 3 - j, CONV_ROWS), :]
                dx_ref[pl.ds(r0, CONV_ROWS), :] = _bf(dx)

    own = lambda b: jnp.minimum(b, NB - 1)
    col = pl.BlockSpec((S, DN_DIM), lambda b: (0, own(b)))
    proj_col = pl.BlockSpec((S, DN_DIM), lambda b: (0, BLK_DN + own(b)))
    wcol = pl.BlockSpec((DN_CONV, DN_DIM), lambda b: (0, own(b)))
    whole = pl.BlockSpec((S, DN_DIM), lambda b: (0, 0))
    assert BLK_K == BLK_DN + NB and BLK_V == BLK_K + 1
    return pl.pallas_call(
        body, grid=(NB + 2,), name="dn_prep_bwd",
        in_specs=[proj_col, wcol, col, pl.BlockSpec(memory_space=pl.ANY), whole, whole],
        out_specs=[pl.BlockSpec((S, DN_DIM), lambda b: (0, BLK_DN + b)), wcol],
        out_shape=[jax.ShapeDtypeStruct(dproj.shape, dproj.dtype), jax.ShapeDtypeStruct((DN_CONV, 3 * DN_HEADS * DN_DIM), F32)],
        scratch_shapes=[pltpu.VMEM((S + CONV_PAD, DN_DIM), F32), pltpu.VMEM((S + CONV_PAD, DN_DIM), F32)],
        input_output_aliases={3: 0},
        compiler_params=_params(("arbitrary",)),
    )(pd, conv_w, dqkv, dproj, dk, dv)


CPAD = 128
CHUNKS_LOCAL = 4
CHUNKS_SCAN = 8


def _chunk_masks():
    ii = lax.broadcasted_iota(jnp.int32, (DN_CHUNK, CPAD), 0)
    jj = lax.broadcasted_iota(jnp.int32, (DN_CHUNK, CPAD), 1)
    return ii, jj


def _rows_pad(a):
    return jnp.concatenate([a, jnp.zeros_like(a)], axis=0)


def _hi_lo(a):
    hi = _bf(a)
    return hi, _bf(a - hi.astype(F32))


def _double_step(t, p):
    C = DN_CHUNK
    th, tl = _hi_lo(t)
    ph, pl_ = _hi_lo(p)
    r1 = _dot(jnp.concatenate([th, tl, ph, pl_], axis=0), _rows_pad(ph))
    r2 = _dot(jnp.concatenate([th, ph], axis=0), _rows_pad(pl_))
    return t + (r1[:C] + r1[C:2 * C] + r2[:C]), r1[2 * C:3 * C] + r1[3 * C:] + r2[C:]


def _dot3_nt(a, b):
    C = DN_CHUNK
    ah, al = _hi_lo(a)
    bh, bl = _hi_lo(b)
    r1 = _dot_nt(jnp.concatenate([ah, al], axis=0), _rows_pad(bh))
    return r1[:C] + r1[C:] + _dot_nt(ah, _rows_pad(bl))


def _dot3_tn(a, b):
    C = DN_CHUNK
    ah, al = _hi_lo(a)
    bh, bl = _hi_lo(b)
    return _dot_tn(jnp.concatenate([ah, al, ah], axis=0), jnp.concatenate([bh, bh, bl], axis=0))[:C]


def _interleave(programs):
    programs = list(programs)
    while programs:
        alive = []
        for prog in programs:
            try:
                next(prog)
                alive.append(prog)
            except StopIteration:
                pass
        programs = alive


def _col_to_row(col, ii, jj):
    return jnp.sum(jnp.where(ii == jj, col, 0.0), axis=0, keepdims=True)


def _row_to_col(row, ii, jj):
    return jnp.sum(jnp.where(ii == jj, row, 0.0), axis=1, keepdims=True)


def _decay(gc_col, ii, jj):
    diff = gc_col - _col_to_row(gc_col, ii, jj)
    return jnp.where(jj <= ii, jnp.exp(jnp.where(jj <= ii, diff, 0.0)), 0.0)


def _softplus(x):
    return jnp.maximum(x, 0.0) + jnp.log(1.0 + jnp.exp(-jnp.abs(x)))


def _head(h):
    return slice(DN_DIM * h, DN_DIM * (h + 1))


def _dn_chunk_fwd(qkv, pg, a_log, dt_bias):
    S = qkv.shape[0]
    C = DN_CHUNK
    G = CHUNKS_LOCAL
    R = G * C
    steps = S // R

    def body(alog_ref, dtb_ref, qkv_ref, pg_ref, w_ref, u_ref, qg_ref, kd_ref, a_ref, t_ref, gcs_ref):
        ii, jj = _chunk_masks()
        lane = lax.broadcasted_iota(jnp.int32, (1, 128), 1)
        eye = (ii == jj).astype(F32)
        gcs_parts = [[] for _ in range(G)]

        def head_program(chunk, h):
            rows = slice(chunk * C, (chunk + 1) * C)
            q, k, v = qkv_ref[rows, _head(h)], qkv_ref[rows, _head(DN_HEADS + h)], qkv_ref[rows, _head(2 * DN_HEADS + h)]
            beta = _sigmoid(pg_ref[rows, h:h + 1])
            g_col = -jnp.exp(alog_ref[h]) * _softplus(pg_ref[rows, DN_HEADS + h:DN_HEADS + h + 1] + dtb_ref[h])
            g_row = _col_to_row(g_col, ii, jj)
            gc_col = jnp.sum(jnp.where(jj <= ii, g_row, 0.0), axis=1, keepdims=True)
            dec = _decay(gc_col, ii, jj)
            eg = jnp.exp(gc_col)
            kb, vb = k * beta, v * beta
            k_rows = _rows_pad(_bf(k))
            kk = _dot_nt(_bf(kb), k_rows)
            qk = _dot_nt(_bf(q), k_rows)
            yield
            t, pw = eye, -jnp.where(jj < ii, kk * dec, 0.0)
            for _ in range(6):
                t, pw = _double_step(t, pw)
                yield
            tb = _bf(t)
            u_ref[rows, _head(h)] = _dot(tb, _rows_pad(_bf(vb)))
            w_ref[rows, _head(h)] = _bf(_dot(tb, _rows_pad(_bf(kb * eg))))
            a_ref[h, rows] = _bf(qk * dec)
            t_ref[h, rows] = t
            qg_ref[rows, _head(h)] = _bf(q * eg)
            kd_ref[rows, _head(h)] = _bf(k * jnp.exp(gc_col[C - 1:C, :] - gc_col))
            gcs_parts[chunk].append(jnp.where(lane == h, gc_col, 0.0) + jnp.where(lane == DN_HEADS + h, beta, 0.0)
                                    + jnp.where(lane == 2 * DN_HEADS + h, g_col, 0.0))

        _interleave(head_program(chunk, h) for chunk in range(G) for h in range(DN_HEADS))
        for chunk in range(G):
            gcs_ref[chunk * C:(chunk + 1) * C, :] = sum(gcs_parts[chunk][1:], gcs_parts[chunk][0])

    smem = pl.BlockSpec(memory_space=pltpu.SMEM)
    wide = pl.BlockSpec((R, 512), lambda n: (n, 0))
    sq = pl.BlockSpec((DN_HEADS, R, CPAD), lambda n: (0, n, 0))
    narrow = pl.BlockSpec((R, 128), lambda n: (n, 0))
    f = lambda *shp: jax.ShapeDtypeStruct(shp, F32)
    b = lambda *shp: jax.ShapeDtypeStruct(shp, BF16)
    return pl.pallas_call(
        body, grid=(steps,), name="dn_chunk_fwd",
        in_specs=[smem, smem, pl.BlockSpec((R, 1536), lambda n: (n, 0)), pl.BlockSpec((R, 128), lambda n: (n, BLK_G))],
        out_specs=[wide, wide, wide, wide, sq, sq, narrow],
        out_shape=[b(S, 512), f(S, 512), b(S, 512), b(S, 512), b(DN_HEADS, S, CPAD), f(DN_HEADS, S, CPAD), f(S, 128)],
        compiler_params=_params(("parallel",)),
    )(a_log, dt_bias, qkv, pg)


def _gated_norm(o, z, gn):
    r, oh = _rms_stats(o)
    return oh * gn * (z * _sigmoid(z))


def _dn_scan_fwd(w, u, qg, kd, a, gcs, pz, gn):
    S = w.shape[0]
    C = DN_CHUNK
    nc = S // C
    G = CHUNKS_SCAN
    R = G * C

    def body(w_ref, u_ref, qg_ref, kd_ref, a_ref, gcs_ref, z_ref, gn_ref, o_ref, vn_ref, sst_ref, out_ref, state):
        @pl.when(pl.program_id(0) == 0)
        def _():
            state[...] = jnp.zeros_like(state)

        def head_program(chunk, h):
            hs = _head(h)
            rows = slice(chunk * C, (chunk + 1) * C)
            s_in = state[h]
            sb = _bf(s_in)
            sst_ref[chunk, h] = sb
            w_s = _dot(w_ref[rows, hs], sb)
            q_s = _dot(qg_ref[rows, hs], sb)
            yield
            vn = u_ref[rows, hs] - w_s
            vnb = _bf(vn)
            o = q_s + _dot(a_ref[h, rows], _rows_pad(vnb))
            k_v = _dot_tn(kd_ref[rows, hs], vnb)
            yield
            state[h] = s_in * jnp.exp(gcs_ref[(chunk + 1) * C - 1:(chunk + 1) * C, h:h + 1]) + k_v
            o_ref[rows, hs] = o
            vn_ref[rows, hs] = vnb
            out_ref[rows, hs] = _bf(_gated_norm(o, z_ref[rows, hs], gn_ref[...]))

        for chunk in range(G):
            _interleave(head_program(chunk, h) for h in range(DN_HEADS))

    wide = pl.BlockSpec((R, 512), lambda n: (n, 0))
    f = lambda *shp: jax.ShapeDtypeStruct(shp, F32)
    b = lambda *shp: jax.ShapeDtypeStruct(shp, BF16)
    return pl.pallas_call(
        body, grid=(nc // G,), name="dn_scan_fwd",
        in_specs=[wide, wide, wide, wide, pl.BlockSpec((DN_HEADS, R, CPAD), lambda n: (0, n, 0)),
                  pl.BlockSpec((R, 128), lambda n: (n, 0)), pl.BlockSpec((R, 512), lambda n: (n, BLK_Z)),
                  pl.BlockSpec((1, DN_DIM), lambda n: (0, 0))],
        out_specs=[wide, wide, pl.BlockSpec((G, DN_HEADS, DN_DIM, DN_DIM), lambda n: (n, 0, 0, 0)), wide],
        out_shape=[f(S, 512), b(S, 512), b(nc, DN_HEADS, DN_DIM, DN_DIM), b(S, 512)],
        scratch_shapes=[pltpu.VMEM((DN_HEADS, DN_DIM, DN_DIM), F32)],
        compiler_params=_params(("arbitrary",)),
    )(w, u, qg, kd, a, gcs, pz, gn)


def _dn_scan_bwd(dcat, o, pz, gn, sst, vnew, w, qg, kd, a, gcs, dproj):
    S = o.shape[0]
    C = DN_CHUNK
    G = CHUNKS_SCAN
    R = G * C
    steps = S // R

    def body(dy_ref, o_ref, z_ref, gn_ref, sst_ref, vn_ref, w_ref, qg_ref, kd_ref, a_ref, gcs_ref, _,
             du_ref, dw_ref, dqg_ref, dkd_ref, da_ref, dz_ref, dsc_ref, dgn_ref, dstate):
        @pl.when(pl.program_id(0) == 0)
        def _():
            dstate[...] = jnp.zeros_like(dstate)
            dgn_ref[...] = jnp.zeros_like(dgn_ref)

        gn_ = gn_ref[...]
        lane = lax.broadcasted_iota(jnp.int32, (C, 128), 1)
        row = lax.broadcasted_iota(jnp.int32, (C, 128), 0)
        dgn_parts = []

        def head_program(chunk, h, dsc_parts):
            hs = _head(h)
            rows = slice(chunk * C, (chunk + 1) * C)
            ov, z, dout = o_ref[rows, hs], z_ref[rows, hs], dy_ref[rows, hs]
            r, oh = _rms_stats(ov)
            sg = _sigmoid(z)
            don = dout * (z * sg)
            dz_ref[rows, hs] = _bf(dout * (oh * gn_) * (sg * (1.0 + z * (1.0 - sg))))
            dgn_parts.append(jnp.sum(don * oh, axis=0, keepdims=True))
            dn = don * gn_
            do = _bf(r * (dn - oh * jnp.mean(dn * oh, axis=-1, keepdims=True)))
            sb = sst_ref[chunk, h]
            s_in = sb.astype(F32)
            ds_out = dstate[h]
            dsb = _bf(ds_out)
            vnb = vn_ref[rows, hs]
            wb, qgb, kdb, ab = w_ref[rows, hs], qg_ref[rows, hs], kd_ref[rows, hs], a_ref[h, rows]
            dvn = _dot_tn(ab, do)[:C] + _dot(kdb, dsb)
            yield
            da_ref[h, rows] = _dot_nt(do, _rows_pad(vnb))
            dqg_ref[rows, hs] = _dot_nt(do, sb)
            dkd_ref[rows, hs] = _dot_nt(vnb, dsb)
            q_do = _dot_tn(qgb, do)
            yield
            dvnb = _bf(dvn)
            dw_ref[rows, hs] = _bf(-_dot_nt(dvnb, sb))
            w_dvn = _dot_tn(wb, dvnb)
            du_ref[rows, hs] = dvnb
            yield
            d_last = jnp.exp(gcs_ref[(chunk + 1) * C - 1:(chunk + 1) * C, h:h + 1])
            dd = jnp.sum(jnp.sum(ds_out * s_in, axis=1, keepdims=True), axis=0, keepdims=True)
            dsc_parts.append(jnp.where((lane == h) & (row == C - 1), dd * d_last, 0.0))
            dstate[h] = ds_out * d_last + q_do - w_dvn

        for chunk in reversed(range(G)):
            dsc_parts = []
            _interleave(head_program(chunk, h, dsc_parts) for h in range(DN_HEADS))
            dsc_ref[chunk * C:(chunk + 1) * C, :] = sum(dsc_parts[1:], dsc_parts[0])
        dgn_ref[...] += sum(dgn_parts[1:], dgn_parts[0])

    rev = lambda n: steps - 1 - n
    wide = pl.BlockSpec((R, 512), lambda n: (rev(n), 0))
    z_spec = pl.BlockSpec((R, 512), lambda n: (rev(n), BLK_Z))
    sq = pl.BlockSpec((DN_HEADS, R, CPAD), lambda n: (0, rev(n), 0))
    narrow = pl.BlockSpec((R, 128), lambda n: (rev(n), 0))
    gn_spec = pl.BlockSpec((1, DN_DIM), lambda n: (0, 0))
    f = lambda *shp: jax.ShapeDtypeStruct(shp, F32)
    b = lambda *shp: jax.ShapeDtypeStruct(shp, BF16)
    return pl.pallas_call(
        body, grid=(steps,), name="dn_scan_bwd",
        in_specs=[pl.BlockSpec((R, 512), lambda n: (rev(n), 1)), wide, z_spec, gn_spec,
                  pl.BlockSpec((G, DN_HEADS, DN_DIM, DN_DIM), lambda n: (rev(n), 0, 0, 0)),
                  wide, wide, wide, wide, sq, narrow, pl.BlockSpec(memory_space=pl.ANY)],
        out_specs=[wide, wide, wide, wide, sq, z_spec, narrow, gn_spec],
        out_shape=[b(S, 512), b(S, 512), f(S, 512), f(S, 512), f(DN_HEADS, S, CPAD),
                   jax.ShapeDtypeStruct(dproj.shape, dproj.dtype), f(S, 128), f(1, DN_DIM)],
        scratch_shapes=[pltpu.VMEM((DN_HEADS, DN_DIM, DN_DIM), F32)],
        input_output_aliases={11: 5},
        compiler_params=_params(("arbitrary",)),
    )(dcat, o, pz, gn, sst, vnew, w, qg, kd, a, gcs, dproj)


def _dn_chunk_bwd(qkv, pg, t_inv, gcs, du, dw, dqg, dkd, da, dsc, a_log, dt_bias, dproj):
    S = qkv.shape[0]
    C = DN_CHUNK
    G = CHUNKS_LOCAL
    R = G * C

    def body(alog_ref, dtb_ref, qkv_ref, pg_ref, t_ref, gcs_ref, du_ref, dw_ref, dqg_ref, dkd_ref, da_ref, dsc_ref, _,
             dqkv_ref, dpg_ref, acc_ref):
        @pl.when(pl.program_id(0) == 0)
        def _():
            acc_ref[...] = jnp.zeros_like(acc_ref)

        ii, jj = _chunk_masks()
        lane = lax.broadcasted_iota(jnp.int32, (1, 128), 1)
        row8 = lax.broadcasted_iota(jnp.int32, (8, 128), 0)
        lane8 = lax.broadcasted_iota(jnp.int32, (8, 128), 1)
        rowc = lax.broadcasted_iota(jnp.int32, (C, 1), 0)
        tril, strict = jj <= ii, jj < ii
        dpg_parts, acc_parts = [[] for _ in range(G)], []

        def head_program(chunk, h):
            rows = slice(chunk * C, (chunk + 1) * C)
            q, k, v = qkv_ref[rows, _head(h)], qkv_ref[rows, _head(DN_HEADS + h)], qkv_ref[rows, _head(2 * DN_HEADS + h)]
            gc_col, beta, g_col = gcs_ref[rows, h:h + 1], gcs_ref[rows, DN_HEADS + h:DN_HEADS + h + 1], \
                gcs_ref[rows, 2 * DN_HEADS + h:2 * DN_HEADS + h + 1]
            dec = _decay(gc_col, ii, jj)
            eg = jnp.exp(gc_col)
            g_last = gc_col[C - 1:C, :]
            ek = jnp.exp(g_last - gc_col)
            kb, vb = k * beta, v * beta
            kbg = kb * eg
            qb, kbb = _bf(q), _bf(kb)
            k_rows = _rows_pad(_bf(k))
            t = t_ref[h, rows]
            tb = _bf(t)
            dub, dwb = du_ref[rows, _head(h)], dw_ref[rows, _head(h)]
            dqg_, dkd_ = dqg_ref[rows, _head(h)], dkd_ref[rows, _head(h)]
            dt = _dot_nt(dub, _rows_pad(_bf(vb))) + _dot_nt(dwb, _rows_pad(_bf(kbg)))
            t_du_dw = _dot_tn(tb, jnp.concatenate([dub, dwb], axis=1))
            dvb, dkbg = t_du_dw[:C, :DN_DIM], t_du_dw[:C, DN_DIM:]
            kk = _dot_nt(kbb, k_rows)
            qk = _dot_nt(qb, k_rows)
            yield
            dt_t = _dot3_nt(dt, t)
            yield
            dl = -_dot3_tn(t, dt_t)
            yield
            dm = jnp.where(strict, dl * dec, 0.0)
            dqk = jnp.where(tril, da_ref[h, rows] * dec, 0.0)
            gmat = dm * kk + dqk * qk
            dgc = jnp.sum(gmat, axis=1, keepdims=True) - _row_to_col(jnp.sum(gmat, axis=0, keepdims=True), ii, jj)
            dmb, dqkb = _bf(dm), _bf(dqk)
            yield
            dkb = _dot(dmb, k_rows) + dkbg * eg
            dk = _dot_tn(jnp.concatenate([dmb, dqkb], axis=0), jnp.concatenate([kbb, qb], axis=0))[:C] + dkd_ * ek
            dq = _dot(dqkb, k_rows) + dqg_ * eg
            yield
            tk = jnp.sum(dkd_ * k * ek, axis=1, keepdims=True)
            dgc = dgc + jnp.sum(dqg_ * q * eg, axis=1, keepdims=True) - tk + jnp.sum(dkbg * kbg, axis=1, keepdims=True)
            dgl = jnp.sum(tk, axis=0, keepdims=True) + dsc_ref[(chunk + 1) * C - 1:(chunk + 1) * C, h:h + 1]
            dgc = dgc + jnp.where(rowc == C - 1, dgl, 0.0)
            yield
            dk = dk + dkb * beta
            dbeta = jnp.sum(dkb * k, axis=1, keepdims=True) + jnp.sum(dvb * v, axis=1, keepdims=True)
            dqkv_ref[rows, _head(h)] = dq
            dqkv_ref[rows, _head(DN_HEADS + h)] = dk
            dqkv_ref[rows, _head(2 * DN_HEADS + h)] = dvb * beta
            dg_col = jnp.sum(jnp.where(jj >= ii, _col_to_row(dgc, ii, jj), 0.0), axis=1, keepdims=True)
            yield
            db = dbeta * beta * (1.0 - beta)
            da_in = dg_col * (-jnp.exp(alog_ref[h])) * _sigmoid(pg_ref[rows, DN_HEADS + h:DN_HEADS + h + 1] + dtb_ref[h])
            dpg_parts[chunk].append(jnp.where(lane == h, db, 0.0) + jnp.where(lane == DN_HEADS + h, da_in, 0.0))
            acc_parts.append(jnp.where((row8 == 0) & (lane8 == h), jnp.sum(dg_col * g_col, axis=0, keepdims=True), 0.0)
                             + jnp.where((row8 == 1) & (lane8 == h), jnp.sum(da_in, axis=0, keepdims=True), 0.0))

        _interleave(head_program(chunk, h) for chunk in range(G) for h in range(DN_HEADS))
        for chunk in range(G):
            dpg = sum(dpg_parts[chunk][1:], dpg_parts[chunk][0])
            dpg_ref[chunk * C:(chunk + 1) * C, :] = _bf(jnp.concatenate([dpg, jnp.zeros_like(dpg)], axis=1))
        acc_ref[...] += sum(acc_parts[1:], acc_parts[0])

    smem = pl.BlockSpec(memory_space=pltpu.SMEM)
    wide = pl.BlockSpec((R, 512), lambda n: (n, 0))
    sq = pl.BlockSpec((DN_HEADS, R, CPAD), lambda n: (0, n, 0))
    narrow = pl.BlockSpec((R, 128), lambda n: (n, 0))
    qkv_spec = pl.BlockSpec((R, 1536), lambda n: (n, 0))
    f = lambda *shp: jax.ShapeDtypeStruct(shp, F32)
    return pl.pallas_call(
        body, grid=(S // R,), name="dn_chunk_bwd",
        in_specs=[smem, smem, qkv_spec, pl.BlockSpec((R, 128), lambda n: (n, BLK_G)), sq, narrow, wide, wide, wide, wide, sq,
                  narrow, pl.BlockSpec(memory_space=pl.ANY)],
        out_specs=[qkv_spec, pl.BlockSpec((R, 256), lambda n: (n, BLK_G_PAD)), pl.BlockSpec((8, 128), lambda n: (0, 0))],
        out_shape=[f(S, 1536), jax.ShapeDtypeStruct(dproj.shape, dproj.dtype), f(8, 128)],
        input_output_aliases={12: 1},
        compiler_params=_params(("arbitrary",)),
    )(a_log, dt_bias, qkv, pg, t_inv, gcs, du, dw, dqg, dkd, da, dsc, dproj)


def _w_in_to_internal(wt):
    return jnp.concatenate([wt[0:512], wt[2304:2816], wt[768:2304], wt[512:768], wt[2816:2824],
                            jnp.zeros((D_IN_PAD - D_IN, wt.shape[1]), wt.dtype)], axis=0)


def _w_in_from_internal(gt):
    return jnp.concatenate([gt[0:512], gt[2560:2816], gt[1024:2560], gt[512:1024], gt[2816:2824]], axis=0)


def _local_step(x, p, target, wts, first_weights, other_weights, ship_early):
    S = x.shape[0]
    cos, sin = _rope_tables(S)
    sinks, a_log, dt_bias = wts["sinks"].reshape(8), wts["a_log"].reshape(4), wts["dt_bias"].reshape(4)
    gn = wts["dn_norm"].reshape(1, DN_DIM)
    add = lambda acc, res: (acc + res,)

    u = _rmsnorm_fwd(x, wts["norm_mix"], "norm_mix_fwd")
    w_in_t, conv_w = first_weights(u)
    proj, = _mm(u, w_in_t, form="nt", name="in_proj", out_dtypes=[F32], tn=512)
    attn, lse = _attn_fwd(proj, cos, sin, sinks)
    qkv = _dn_prep_fwd(proj, conv_w)
    cw, cu, cqg, ckd, ca, ct, gcs = _dn_chunk_fwd(qkv, proj, a_log, dt_bias)
    o, vnew, sst, dn_out = _dn_scan_fwd(cw, cu, cqg, ckd, ca, gcs, proj, gn)
    w_o, = other_weights(("w_o",), dn_out)
    h1, = _mm([attn, dn_out], w_o, form="nn", name="out_proj", out_dtypes=[F32], tn=512, epi=add, extra=[x])

    w_up, w_down = other_weights(("w_up", "w_down"), h1)
    hid, relu, m, h2 = _mlp_fwd(h1, w_up, w_down, wts["norm_mlp"])
    w_pg, w_pp = other_weights(("w_ple_gate", "w_ple_proj"), h2)
    n3, dh2, dgl, dpp, loss, d_norm_final, d_norm_ple = _ple_and_loss(h2, p, target, w_pg, w_pp, wts["norm_ple"],
                                                                     wts["norm_final"].reshape(1, D_MODEL))
    g = {"norm_final": d_norm_final, "norm_ple": d_norm_ple}
    early = {"w_ple_gate": _mm_tn(n3, dgl, name="d_w_ple_gate", tm=512, tn=1024, out_dtype=BF16).reshape(N_DEV, 128, 1024),
             "w_ple_proj": _mm_tn(p, dpp, name="d_w_ple_proj", tm=256, tn=128, out_dtype=BF16, column_shards=True)}
    d_act, = _mm(dh2, w_down, form="nt", name="d_hidden", out_dtypes=[BF16], tn=512,
                 epi=lambda acc, r: (acc * (2.0 * r.astype(F32)),), extra=[relu])
    early["w_down"] = _mm_tn(hid, dh2, name="d_w_down", tm=512, tn=1024, out_dtype=BF16).reshape(N_DEV, 512, 1024)
    early["w_up"] = _mm_tn(m, d_act, name="d_w_up", tm=1024, tn=512, out_dtype=BF16, column_shards=True)
    token = ship_early(early)
    dh1, g["norm_mlp"], dcat = _mm(d_act, w_up, form="nt", name="d_m", out_dtypes=[F32], tn=512, after=token,
                                   norm_bwd=(h1, wts["norm_mlp"], dh2), then_nt=w_o)
    d_w_o = jnp.concatenate([_mm_tn(attn, dh1, name="d_w_o_attn", tm=512, tn=512, out_dtype=BF16),
                             _mm_tn(dn_out, dh1, name="d_w_o_dn", tm=512, tn=512, out_dtype=BF16)], axis=0)
    token = ship_early({"w_o": d_w_o.reshape(N_DEV, 128, 1024)})
    dproj, dk, dv, dsinks = _attn_bwd(proj, cos, sin, sinks + token[0, 0], dcat, attn, lse)
    g["sinks"] = dsinks[:, 0].reshape(1, 8)
    du_, dw_, dqg, dkd, da, dproj, dsc, g["dn_norm"] = _dn_scan_bwd(dcat, o, proj, gn, sst, vnew, cw, cqg, ckd, ca, gcs, dproj)
    dqkv, dproj, gate_acc = _dn_chunk_bwd(qkv, proj, ct, gcs, du_, dw_, dqg, dkd, da, dsc, a_log, dt_bias, dproj)
    g["a_log"], g["dt_bias"] = gate_acc[0:1, 0:4], gate_acc[1:2, 0:4]
    dproj, g["conv_w"] = _dn_prep_bwd(proj, conv_w, dqkv, dproj, dk, dv)
    token = ship_early({"w_in": _mm_tn(dproj, u, name="d_w_in", tm=512, tn=1024, out_dtype=BF16)})
    grad_x, g["norm_mix"] = _mm(dproj, w_in_t, form="nn", name="d_u", out_dtypes=[F32], tn=512, after=token,
                                norm_bwd=(x, wts["norm_mix"], dh1))
    return loss, grad_x, g


def _peer(k):
    x, y, c = lax.axis_index("x"), lax.axis_index("y"), lax.axis_index("c")
    px = 1 - x if k & 4 else x
    py = 1 - y if k & 2 else y
    pc = 1 - c if k & 1 else c
    return (px, py, pc), 4 * px + 2 * py + pc


def _exchange(srcs, name, gather):
    n = len(srcs)
    gathers = list(gather) if isinstance(gather, (list, tuple)) else [gather] * n
    shapes = [(N_DEV,) + s.shape if gt else s.shape for s, gt in zip(srcs, gathers)]

    def body(*refs):
        src_refs, out_refs = refs[:n], refs[n:2 * n]
        send_sems, recv_sems, local_sems = refs[2 * n:]
        _, me = _peer(0)
        piece = lambda a, d: src_refs[a] if gathers[a] else src_refs[a].at[d]
        local = [pltpu.make_async_copy(piece(a, me), out_refs[a].at[me], local_sems.at[a]) for a in range(n)]
        for cp in local:
            cp.start()
        copies = []
        for a in range(n):
            for k in range(1, N_DEV):
                dev, idx = _peer(k)
                cp = pltpu.make_async_remote_copy(src_ref=piece(a, idx), dst_ref=out_refs[a].at[me],
                                                  send_sem=send_sems.at[a, k - 1], recv_sem=recv_sems.at[a, k - 1],
                                                  device_id=dev, device_id_type=MESH)
                cp.start()
                copies.append(cp)
        for cp in copies:
            cp.wait_recv()
        for cp in copies:
            cp.wait_send()
        for cp in local:
            cp.wait()

    anywhere = pl.BlockSpec(memory_space=pl.ANY)
    return pl.pallas_call(
        body, name=name, in_specs=[anywhere] * n, out_specs=[anywhere] * n,
        out_shape=[jax.ShapeDtypeStruct(shp, s.dtype) for shp, s in zip(shapes, srcs)],
        scratch_shapes=[pltpu.SemaphoreType.DMA((n, N_DEV - 1)), pltpu.SemaphoreType.DMA((n, N_DEV - 1)),
                        pltpu.SemaphoreType.DMA((n,))],
    )(*srcs)


_HBM = pl.BlockSpec(memory_space=pltpu.HBM)
_SEM = pl.BlockSpec(memory_space=pltpu.SEMAPHORE)
_EFFECT = pltpu.SideEffectType.DATAFLOW_SIDE_EFFECTING


def _split_copies(src_refs, land_refs, send_sems, recv_sems, modes, which=None):
    _, me = _peer(0)
    local, remote = [], []
    which = range(len(src_refs)) if which is None else which
    for a, src, land in zip(which, src_refs, land_refs):
        if modes[a] == "columns":
            n_cols = src.shape[1]
            dst = land.at[:, pl.ds(pl.multiple_of(me * n_cols, n_cols), n_cols)]
        else:
            dst = land.at[me]
        part = lambda d: src.at[d] if modes[a] == "pieces" else src
        local.append(pltpu.make_async_copy(part(me), dst, recv_sems.at[a * N_DEV]))
        for k in ((2, 4, 6) if modes[a] == "chips" else range(1, N_DEV)):
            dev, idx = _peer(k)
            sem = a * N_DEV + k
            remote.append(pltpu.make_async_remote_copy(
                src_ref=part(idx), dst_ref=dst, send_sem=send_sems.at[sem], recv_sem=recv_sems.at[sem],
                device_id=dev, device_id_type=MESH))
    return local, remote


def _forward_copies(land_refs, send_sems, recv_sems):
    c = lax.axis_index("c")
    sibling, _ = _peer(1)
    copies = []
    for a, land in enumerate(land_refs):
        for chip in range(N_DEV // 2):
            slot = 2 * chip + c
            sem = a * (N_DEV // 2) + chip
            copies.append(pltpu.make_async_remote_copy(
                src_ref=land.at[slot], dst_ref=land.at[slot], send_sem=send_sems.at[sem], recv_sem=recv_sems.at[sem],
                device_id=sibling, device_id_type=MESH))
    return copies


def _forward_start(lands, name):
    n = len(lands)

    def body(*refs):
        for cp in _forward_copies(refs[:n], refs[n], refs[n + 1]):
            cp.start()
        refs[-1][...] = jnp.zeros_like(refs[-1])

    sems = pltpu.SemaphoreType.DMA((n * (N_DEV // 2),))
    out = pl.pallas_call(
        body, name=name,
        out_shape=(sems, sems, *[pltpu.HBM(t.shape, t.dtype) for t in lands], jax.ShapeDtypeStruct((8, 128), F32)),
        in_specs=[_HBM] * n, out_specs=(_SEM, _SEM, *[_HBM] * n, pl.BlockSpec(memory_space=pltpu.VMEM)),
        input_output_aliases={i: 2 + i for i in range(n)},
        compiler_params=pltpu.CompilerParams(has_side_effects=_EFFECT),
    )(*[pltpu.with_memory_space_constraint(t, pltpu.HBM) for t in lands])
    return out[:-1], out[-1]


def _forward_wait(handle, after, name):
    send_sems, recv_sems, *lands = handle
    n = len(lands)

    def body(*refs):
        for cp in _forward_copies(refs[:n], refs[n], refs[n + 1]):
            cp.wait_send()
            cp.wait_recv()

    return list(pl.pallas_call(
        body, name=name, out_shape=tuple(pltpu.HBM(t.shape, t.dtype) for t in lands),
        in_specs=[_HBM] * n + [_SEM, _SEM, pl.BlockSpec(memory_space=pl.ANY)], out_specs=tuple([_HBM] * n),
        input_output_aliases={i: i for i in range(n)},
        compiler_params=pltpu.CompilerParams(has_side_effects=_EFFECT),
    )(*lands, send_sems, recv_sems, after))


def _exchange_start(srcs, name, modes):
    n = len(srcs)
    modes = [modes] * n if isinstance(modes, str) else list(modes)
    lands = []
    for s, mode in zip(srcs, modes):
        shape = {"columns": (s.shape[0], N_DEV * s.shape[1]), "pieces": s.shape}.get(mode, (N_DEV,) + s.shape)
        lands.append(lax.empty(shape, s.dtype))

    def body(*refs):
        src_refs, land_refs = refs[:n], refs[n:2 * n]
        send_sems, recv_sems = refs[2 * n], refs[2 * n + 1]
        local, remote = _split_copies(src_refs, land_refs, send_sems, recv_sems, modes)
        for cp in local + remote:
            cp.start()
        refs[-1][...] = jnp.zeros_like(refs[-1])

    both = list(srcs) + lands
    sems = pltpu.SemaphoreType.DMA((n * N_DEV,))
    out = pl.pallas_call(
        body, name=name,
        out_shape=(sems, sems, *[pltpu.HBM(t.shape, t.dtype) for t in both], jax.ShapeDtypeStruct((8, 128), F32)),
        in_specs=[_HBM] * (2 * n), out_specs=(_SEM, _SEM, *[_HBM] * (2 * n), pl.BlockSpec(memory_space=pltpu.VMEM)),
        input_output_aliases={i: 2 + i for i in range(2 * n)},
        compiler_params=pltpu.CompilerParams(has_side_effects=_EFFECT),
    )(*[pltpu.with_memory_space_constraint(t, pltpu.HBM) for t in both])
    return (n, modes, out[:-1]), out[-1]


def _exchange_wait(handle, after, name, which=None):
    n_all, modes, (send_sems, recv_sems, *both_all) = handle
    which = list(range(n_all)) if which is None else list(which)
    n = len(which)
    both = [both_all[a] for a in which] + [both_all[n_all + a] for a in which]

    def body(*refs):
        src_refs, land_refs = refs[:n], refs[n:2 * n]
        local, remote = _split_copies(src_refs, land_refs, refs[2 * n], refs[2 * n + 1], modes, which)
        for cp in local:
            cp.wait()
        for cp in remote:
            cp.wait_send()
            cp.wait_recv()

    out = pl.pallas_call(
        body, name=name, out_shape=tuple(pltpu.HBM(t.shape, t.dtype) for t in both),
        in_specs=[_HBM] * (2 * n) + [_SEM, _SEM, pl.BlockSpec(memory_space=pl.ANY)], out_specs=tuple([_HBM] * (2 * n)),
        input_output_aliases={i: i for i in range(2 * n)},
        compiler_params=pltpu.CompilerParams(has_side_effects=_EFFECT),
    )(*both, send_sems, recv_sems, after)
    return list(out[n:])


def _cast_all(arrays):
    def body(*refs):
        for src, dst in zip(refs[:len(arrays)], refs[len(arrays):]):
            dst[...] = _bf(src[...] if len(src.shape) == 2 else src[:, 0, :])

    return pl.pallas_call(body, name="cast_shards", out_shape=[jax.ShapeDtypeStruct((a.shape[0], a.shape[-1]), BF16) for a in arrays],
                          compiler_params=pltpu.CompilerParams(vmem_limit_bytes=VMEM_LIMIT))(*arrays)


def _adam_update(g, w, m, v):
    nm = ADAM_B1 * m + (1.0 - ADAM_B1) * g
    nv = ADAM_B2 * v + (1.0 - ADAM_B2) * (g * g)
    m_hat = nm / (1.0 - ADAM_B1 ** ADAM_STEP)
    v_hat = nv / (1.0 - ADAM_B2 ** ADAM_STEP)
    return -ADAM_LR * (m_hat / (jnp.sqrt(v_hat) + ADAM_EPS) + ADAM_WD * w), nm, nv


def _adamw(parts, w, m, v, name):
    n, R, W = parts.shape
    tm = 128 if R % 128 == 0 else R
    rows_apart = w.ndim == 3
    get = (lambda ref: ref[:, 0, :]) if rows_apart else (lambda ref: ref[...])

    def body(p_ref, w_ref, m_ref, v_ref, *out_refs):
        g = p_ref[0].astype(F32)
        for s in range(1, n):
            g = g + p_ref[s].astype(F32)
        for ref, val in zip(out_refs, (g,) + _adam_update(g, get(w_ref), get(m_ref), get(v_ref))):
            if rows_apart:
                ref[:, 0, :] = val
            else:
                ref[...] = val

    tile = pl.BlockSpec((tm, 1, W), lambda i: (i, 0, 0)) if rows_apart else pl.BlockSpec((tm, W), lambda i: (i, 0))
    return pl.pallas_call(
        body, grid=(R // tm,), name=name,
        in_specs=[pl.BlockSpec((n, tm, W), lambda i: (0, i, 0)), tile, tile, tile],
        out_specs=[tile] * 4, out_shape=[jax.ShapeDtypeStruct(w.shape, F32)] * 4,
        compiler_params=_params(("parallel",)),
    )(parts, w, m, v)


_MATRICES = ("w_in", "w_o", "w_up", "w_down", "w_ple_gate", "w_ple_proj")


_OTHERS = ("w_o", "w_up", "w_down", "w_ple_gate", "w_ple_proj")
_OTHER_MODES = {"w_o": "slots", "w_up": "slots", "w_down": "slots", "w_ple_gate": "slots", "w_ple_proj": "columns"}


_VECTORS = ("norm_mix", "norm_mlp", "norm_ple", "norm_final", "a_log", "dt_bias", "sinks", "dn_norm")
_SMALL_ROWS, _LOSS_ROW, _CONV_ROW = 16, 8, 9


def _pack_small(vectors, loss, conv):
    def body(*refs):
        out = refs[-1]
        out[...] = jnp.zeros_like(out)
        for r, ref in enumerate(refs[:len(_VECTORS)]):
            out[r:r + 1, 0:ref.shape[1]] = ref[...]
        out[_LOSS_ROW:_LOSS_ROW + 1, 0:128] = refs[len(_VECTORS)][...]
        out[_CONV_ROW:_CONV_ROW + 6, :] = refs[len(_VECTORS) + 1][...]

    return pl.pallas_call(body, name="pack_small", out_shape=jax.ShapeDtypeStruct((_SMALL_ROWS, 1024), F32))(*vectors, loss, conv)


def _sum_slots(parts):
    def body(p_ref, o_ref):
        acc = p_ref[0]
        for s in range(1, parts.shape[0]):
            acc = acc + p_ref[s]
        o_ref[...] = acc

    return pl.pallas_call(body, name="sum_small", out_shape=jax.ShapeDtypeStruct(parts.shape[1:], parts.dtype))(parts)


def _adamw_vectors(summed, conv_g, wmv):
    names = _VECTORS + ("conv_w",)
    flat = [a for triple in wmv for a in triple]

    def body(*refs):
        sum_ref, conv_ref = refs[0], refs[1]
        ins, outs = refs[2:2 + len(flat)], refs[2 + len(flat):]
        for i in range(len(names)):
            w_ref, m_ref, v_ref = ins[3 * i:3 * i + 3]
            g = conv_ref[...] if i == len(_VECTORS) else sum_ref[i:i + 1, 0:w_ref.shape[1]]
            outs[4 * i][...] = g
            outs[4 * i + 1][...], outs[4 * i + 2][...], outs[4 * i + 3][...] = _adam_update(g, w_ref[...], m_ref[...], v_ref[...])

    out_shape = [jax.ShapeDtypeStruct(t[0].shape, F32) for t in wmv for _ in range(4)]
    res = pl.pallas_call(body, name="adamw_vectors", out_shape=out_shape)(summed, conv_g, *flat)
    return {n: res[4 * i:4 * i + 4] for i, n in enumerate(names)}


_ORDER = ("norm_mix", "w_in", "conv_w", "a_log", "dt_bias", "dn_norm", "sinks", "w_o", "norm_mlp", "w_up", "w_down",
          "norm_ple", "w_ple_gate", "w_ple_proj", "norm_final")


def kernel(x, p, norm_mix, w_in, conv_w, a_log, dt_bias, dn_norm, sinks, w_o, norm_mlp, w_up, w_down, norm_ple, w_ple_gate, w_ple_proj, norm_final, loss_target, m_norm_mix, m_w_in, m_conv_w, m_a_log, m_dt_bias, m_dn_norm, m_sinks, m_w_o, m_norm_mlp, m_w_up, m_w_down, m_norm_ple, m_w_ple_gate, m_w_ple_proj, m_norm_final, v_norm_mix, v_w_in, v_conv_w, v_a_log, v_dt_bias, v_dn_norm, v_sinks, v_w_o, v_norm_mlp, v_w_up, v_w_down, v_norm_ple, v_w_ple_gate, v_w_ple_proj, v_norm_final):
    w = dict(norm_mix=norm_mix, w_in=w_in, conv_w=conv_w[0], a_log=a_log, dt_bias=dt_bias, dn_norm=dn_norm, sinks=sinks,
             w_o=w_o[0], norm_mlp=norm_mlp, w_up=w_up[0], w_down=w_down[0], norm_ple=norm_ple, w_ple_gate=w_ple_gate[0],
             w_ple_proj=w_ple_proj[0], norm_final=norm_final)
    m = dict(norm_mix=m_norm_mix, w_in=m_w_in, conv_w=m_conv_w[0], a_log=m_a_log, dt_bias=m_dt_bias, dn_norm=m_dn_norm,
             sinks=m_sinks, w_o=m_w_o[0], norm_mlp=m_norm_mlp, w_up=m_w_up[0], w_down=m_w_down[0], norm_ple=m_norm_ple,
             w_ple_gate=m_w_ple_gate[0], w_ple_proj=m_w_ple_proj[0], norm_final=m_norm_final)
    v = dict(norm_mix=v_norm_mix, w_in=v_w_in, conv_w=v_conv_w[0], a_log=v_a_log, dt_bias=v_dt_bias, dn_norm=v_dn_norm,
             sinks=v_sinks, w_o=v_w_o[0], norm_mlp=v_norm_mlp, w_up=v_w_up[0], w_down=v_w_down[0], norm_ple=v_norm_ple,
             w_ple_gate=v_w_ple_gate[0], w_ple_proj=v_w_ple_proj[0], norm_final=v_norm_final)
    me = 4 * lax.axis_index("x") + 2 * lax.axis_index("y") + lax.axis_index("c")
    conv_shard = conv_w.shape[2]

    for d in (w, m, v):
        d["w_in"] = jnp.transpose(d["w_in"], (2, 0, 1))
    conv_pad = jnp.pad(w["conv_w"], ((0, 8 - DN_CONV), (0, 256 - conv_shard)))
    shards = _cast_all([w[n] for n in ("w_in",) + _OTHERS])
    gathers, token_gather = _exchange_start([shards[0], conv_pad] + list(shards[1:]), "gather_start",
                                            ["chips", "chips"] + [_OTHER_MODES[n] for n in _OTHERS])
    vectors = dict(w)
    vectors["norm_mix"] = w["norm_mix"] + token_gather[0:1, 0:1]

    def first_weights(after):
        over_ici = _exchange_wait(gathers, after, "gather_first_wait", [0, 1])
        handle, token = _forward_start(over_ici, "gather_first_forward")
        w_in_all, conv_all = _forward_wait(handle, token, "gather_first_forward_wait")
        conv_all = jnp.transpose(conv_all[:, :DN_CONV, :conv_shard], (1, 0, 2)).reshape(DN_CONV, N_DEV * conv_shard)
        return _w_in_to_internal(w_in_all.reshape(D_IN, D_MODEL)), conv_all

    as_taken = {"w_o": lambda t: t.reshape(1024, 1024), "w_up": lambda t: t, "w_down": lambda t: t.reshape(4096, 1024),
                "w_ple_gate": lambda t: t.reshape(1024, 1024), "w_ple_proj": lambda t: t}

    def other_weights(names, after):
        which = [2 + _OTHERS.index(n) for n in names]
        got = _exchange_wait(gathers, after, "gather_wait_" + names[0], which)
        return [as_taken[n](t) for n, t in zip(names, got)]

    shipped = []

    def ship_early(pieces):
        names = tuple(pieces)
        if names == ("w_in",):
            pieces = {"w_in": _w_in_from_internal(pieces["w_in"]).reshape(N_DEV, D_IN // N_DEV, D_MODEL)}
        handle, token = _exchange_start([pieces[n] for n in names], "scatter_start_" + names[0], "pieces")
        shipped.append((names, handle))
        return token

    loss, grad_x, g = _local_step(x[0], p[0, 0], loss_target[0], vectors, first_weights, other_weights, ship_early)

    row = lambda t: t.reshape(1, t.size)
    small = _pack_small([row(g[n]) for n in _VECTORS], loss, g["conv_w"].reshape(6, 1024))
    small_handle, token_small = _exchange_start([small], "gather_small_start", "slots")
    big, after = {}, token_small
    for names, handle in shipped[:-1]:
        for n, r in zip(names, _exchange_wait(handle, after, "scatter_wait_" + names[0])):
            big[n] = _adamw(r, w[n], m[n], v[n], "adamw_" + n)
            after = big[n][1]
    small_all, = _exchange_wait(small_handle, after, "gather_small_wait")
    summed = _sum_slots(small_all)
    conv_g = lax.dynamic_slice(summed[_CONV_ROW:_CONV_ROW + 6].reshape(DN_CONV, N_DEV * conv_shard), (0, me * conv_shard),
                               (DN_CONV, conv_shard))
    small_out = _adamw_vectors(summed, conv_g, [(row(w[n]), row(m[n]), row(v[n])) for n in _VECTORS]
                               + [(w["conv_w"], m["conv_w"], v["conv_w"])])
    names, handle = shipped[-1]
    for n, r in zip(names, _exchange_wait(handle, small_out["conv_w"][0], "scatter_wait_" + names[0])):
        big[n] = _adamw(r, w[n], m[n], v[n], "adamw_" + n)

    result = [summed[_LOSS_ROW, 0], grad_x[None]]
    for i in range(4):
        for n in _ORDER:
            if n == "w_in":
                result.append(jnp.transpose(big[n][i], (1, 2, 0)))
            elif n in _MATRICES:
                result.append(big[n][i][None])
            elif n == "conv_w":
                result.append(small_out[n][i][None])
            else:
                result.append(small_out[n][i].reshape(w[n].shape))
    return tuple(result)
```

```python
import jax
import jax.numpy as jnp
import numpy as np
from jax import lax
from jax.experimental import pallas as pl
from jax.experimental.pallas import tpu as pltpu

F32, BF16 = jnp.float32, jnp.bfloat16
EPS = 1e-6
D_MODEL = 1024
N_DEV = 8
ATTN_BLOCK = 128
HEAD_PAIR = 128
DN_HEADS = 4
DN_DIM = 128
DN_CHUNK = 64
DN_CONV = 4
ROPE_THETA = 10000.0
D_IN = 2824
D_IN_PAD = 3072
BLK_Q, BLK_Z = 0, 1
BLK_DN, BLK_K, BLK_V, BLK_G = 8, 20, 21, 22
BLK_G_PAD = 11
VMEM_LIMIT = 56 * 1024 * 1024
NEG = -1e30
ADAM_LR, ADAM_B1, ADAM_B2, ADAM_EPS, ADAM_WD, ADAM_STEP = 0.001, 0.9, 0.999, 1e-08, 0.01, 10
MESH = pl.DeviceIdType.MESH


def _bf(x):
    return x.astype(BF16)


def _dot(a, b):
    return jnp.dot(a, b, preferred_element_type=F32)


def _dot_nt(a, b):
    return lax.dot_general(a, b, (((1,), (1,)), ((), ())), preferred_element_type=F32)


def _dot_tn(a, b):
    return lax.dot_general(a, b, (((0,), (0,)), ((), ())), preferred_element_type=F32)


def _sigmoid(x):
    return 1.0 / (1.0 + jnp.exp(-x))


def _params(sem):
    return pltpu.CompilerParams(dimension_semantics=sem, vmem_limit_bytes=VMEM_LIMIT)


def _mm(x, w, *, form, name, out_dtypes, tn, epi=None, extra=(), tm=512, w_row_block=0, after=None, norm=None,
        norm_bwd=None, then_nt=None):
    assert norm is None or norm_bwd is None
    xs = list(x) if isinstance(x, (list, tuple)) else [x]
    nx = len(xs)
    S, K = xs[0].shape
    shards = w.ndim == 3
    N = (w.shape[2] * N_DEV if shards else w.shape[1]) if form == "nn" else w.shape[-2]
    assert not (shards and form == "nn" and tn != w.shape[2]) and (nx == 1 or (form == "nn" and not shards and norm is None))
    r0 = w_row_block * K
    tm = min(tm, S)
    n_extra, n_out = len(extra), len(out_dtypes)
    tile = lambda width: pl.BlockSpec((tm, width), lambda i: (i, 0))
    whole = lambda a: pl.BlockSpec(a.shape, lambda i, nd=a.ndim: (0,) * nd)
    ins, in_specs = [*xs, w, *extra], [tile(K)] * nx + [whole(w)] + [tile(N)] * n_extra
    if norm is not None:
        ins, in_specs = ins + [norm], in_specs + [whole(norm)]
    if norm_bwd is not None:
        ins, in_specs = ins + list(norm_bwd), in_specs + [tile(N), whole(norm_bwd[1]), tile(N)]
    if then_nt is not None:
        ins, in_specs = ins + [then_nt], in_specs + [whole(then_nt)]
    if after is not None:
        ins, in_specs = ins + [after], in_specs + [whole(after)]
    out_shape = [jax.ShapeDtypeStruct((S, N), dt) for dt in out_dtypes]
    out_specs = [tile(N)] * n_out
    if norm is not None:
        out_shape, out_specs = out_shape + [jax.ShapeDtypeStruct((S, K), BF16)], out_specs + [tile(K)]
    if norm_bwd is not None:
        out_shape, out_specs = out_shape + [jax.ShapeDtypeStruct((1, N), F32)], out_specs + [pl.BlockSpec((1, N), lambda i: (0, 0))]
    if then_nt is not None:
        out_shape, out_specs = out_shape + [jax.ShapeDtypeStruct((S, then_nt.shape[0]), F32)], out_specs + [tile(then_nt.shape[0])]

    def product(xb, w_ref, cols, c):
        if form == "nn" and nx > 1:
            return sum(_dot(part, w_ref[r0 + p * K:r0 + (p + 1) * K, cols]) for p, part in enumerate(xb))
        if form == "nn":
            return _dot(xb, w_ref[c] if shards else w_ref[r0:r0 + K, cols])
        if not shards:
            return _dot_nt(xb, w_ref[cols, :])
        ks = w.shape[2]
        acc = _dot_nt(xb[:, 0:ks], w_ref[0, cols, :])
        for s in range(1, N_DEV):
            acc = acc + _dot_nt(xb[:, s * ks:(s + 1) * ks], w_ref[s, cols, :])
        return acc

    def body(*refs):
        x_ref, w_ref = refs[0], refs[nx]
        extra_refs = refs[nx + 1:nx + 1 + n_extra]
        at = nx + 1 + n_extra
        if norm is not None:
            gain_ref, at = refs[at], at + 1
        if norm_bwd is not None:
            (y_ref, ygain_ref, dres_ref), at = refs[at:at + 3], at + 3
        if then_nt is not None:
            w2_ref, at = refs[at], at + 1
        outs = refs[len(ins):]
        if norm is not None:
            _, xh = _rms_stats(x_ref[...])
            xb = _bf(xh * gain_ref[...])
            outs[n_out][...] = xb
        else:
            xb = _bf(x_ref[...]) if nx == 1 else [_bf(r[...]) for r in refs[:nx]]
        for c in range(N // tn):
            cols = slice(c * tn, (c + 1) * tn)
            acc = product(xb, w_ref, cols, c)
            res = epi(acc, *[r[:, cols] for r in extra_refs]) if epi else (acc,)
            for o, r in zip(outs[:n_out], res):
                o[:, cols] = r.astype(o.dtype)
        if norm_bwd is not None:
            dx, dg = _rms_bwd_tile(y_ref[...], ygain_ref[...], outs[0][...])
            outs[0][...] = dres_ref[...] + dx
            dg_ref = outs[n_out]

            @pl.when(pl.program_id(0) == 0)
            def _():
                dg_ref[...] = jnp.zeros_like(dg_ref)

            dg_ref[...] += dg
        if then_nt is not None:
            yb = _bf(outs[0][...])
            for c in range(then_nt.shape[0] // tn):
                cols = slice(c * tn, (c + 1) * tn)
                outs[-1][:, cols] = _dot_nt(yb, w2_ref[cols, :])

    return pl.pallas_call(
        body, grid=(S // tm,), name=name, in_specs=in_specs, out_specs=out_specs, out_shape=out_shape,
        compiler_params=_params(("arbitrary",) if norm_bwd is not None else ("parallel",)),
    )(*ins)


def _mlp_fwd(h1, w_up, w_down, gain):
    S, K = h1.shape
    n_sh, _, fs = w_up.shape
    tm = min(512, S)

    def body(x_ref, wup_ref, wdown_ref, g_ref, hid_ref, relu_ref, m_ref, h2_ref):
        x = x_ref[...]
        _, xh = _rms_stats(x)
        mb = _bf(xh * g_ref[...])
        m_ref[...] = mb
        h2_ref[...] = x
        for c in range(n_sh):
            cols = slice(c * fs, (c + 1) * fs)
            r = jnp.maximum(_dot(mb, wup_ref[c]), 0.0)
            hd = _bf(r * r)
            hid_ref[:, cols] = hd
            relu_ref[:, cols] = _bf(r)
            h2_ref[...] += _dot(hd, wdown_ref[cols, :])

    tile = lambda width: pl.BlockSpec((tm, width), lambda i: (i, 0))
    once = lambda a: pl.BlockSpec(a.shape, lambda i, nd=a.ndim: (0,) * nd, pipeline_mode=pl.Buffered(1))
    F = n_sh * fs
    return pl.pallas_call(
        body, grid=(S // tm,), name="mlp_fwd",
        in_specs=[tile(K), once(w_up), once(w_down), pl.BlockSpec(gain.shape, lambda i: (0, 0))],
        out_specs=[tile(F), tile(F), tile(K), tile(K)],
        out_shape=[jax.ShapeDtypeStruct((S, F), BF16), jax.ShapeDtypeStruct((S, F), BF16),
                   jax.ShapeDtypeStruct((S, K), BF16), jax.ShapeDtypeStruct((S, K), F32)],
        compiler_params=_params(("parallel",)),
    )(h1, w_up, w_down, gain)


def _mm_tn(x, dy, *, name, tm, tn, out_dtype=F32, column_shards=False, after=None):
    S, K = x.shape
    N = dy.shape[1]
    waits = [] if after is None else [after]

    def body(x_ref, dy_ref, *rest):
        rest[-1][...] = _dot_tn(_bf(x_ref[...]), _bf(dy_ref[...])).astype(out_dtype)

    if column_shards:
        out_spec = pl.BlockSpec((None, tm, tn), lambda i, j: (j, i, 0))
        out_shape = jax.ShapeDtypeStruct((N // tn, K, tn), out_dtype)
    else:
        out_spec = pl.BlockSpec((tm, tn), lambda i, j: (i, j))
        out_shape = jax.ShapeDtypeStruct((K, N), out_dtype)
    return pl.pallas_call(
        body, grid=(K // tm, N // tn), name=name,
        in_specs=[pl.BlockSpec((S, tm), lambda i, j: (0, i)), pl.BlockSpec((S, tn), lambda i, j: (0, j))]
        + [pl.BlockSpec(memory_space=pl.ANY)] * len(waits),
        out_specs=out_spec, out_shape=out_shape,
        compiler_params=_params(("parallel", "parallel")),
    )(x, dy, *waits)


def _rowwise(body, *, tiled, full, out_tiled, out_acc, name, tm=512, smem=()):
    S = tiled[0].shape[0]
    tm = min(tm, S)
    n_in = len(smem) + len(tiled) + len(full)

    def kern(*refs):
        @pl.when(pl.program_id(0) == 0)
        def _():
            for r in refs[n_in + len(out_tiled):]:
                r[...] = jnp.zeros_like(r)
        body(*refs)

    in_specs = [pl.BlockSpec(memory_space=pltpu.SMEM) for _ in smem]
    in_specs += [pl.BlockSpec((tm, a.shape[1]), lambda i: (i, 0)) for a in tiled]
    in_specs += [pl.BlockSpec(a.shape, lambda i, nd=a.ndim: (0,) * nd) for a in full]
    out_specs = [pl.BlockSpec((tm, w), lambda i: (i, 0)) for w, _ in out_tiled]
    out_specs += [pl.BlockSpec(shp, lambda i, nd=len(shp): (0,) * nd) for shp, _ in out_acc]
    out_shape = [jax.ShapeDtypeStruct((S, w), dt) for w, dt in out_tiled]
    out_shape += [jax.ShapeDtypeStruct(shp, dt) for shp, dt in out_acc]
    return pl.pallas_call(
        kern, grid=(S // tm,), name=name, in_specs=in_specs, out_specs=out_specs, out_shape=out_shape,
        compiler_params=_params(("arbitrary",)),
    )(*smem, *tiled, *full)


def _rms_stats(x):
    r = lax.rsqrt(jnp.mean(x * x, axis=-1, keepdims=True) + EPS)
    return r, x * r


def _rmsnorm_fwd(x, g, name):
    def body(x_ref, g_ref, o_ref):
        _, xh = _rms_stats(x_ref[...])
        o_ref[...] = _bf(xh * g_ref[...])

    return _rowwise(body, tiled=[x], full=[g], out_tiled=[(x.shape[1], BF16)], out_acc=[], name=name)[0]


def _rms_bwd_tile(x, g, dxn):
    r, xh = _rms_stats(x)
    dg = jnp.sum(dxn * xh, axis=0, keepdims=True)
    dn = dxn * g
    dx = r * (dn - xh * jnp.mean(dn * xh, axis=-1, keepdims=True))
    return dx, dg


def _ple_and_loss(h2, p, target, w_pg, w_pp, g_ple, g_final):
    S, n = h2.shape
    tm = min(512, S)
    tn = 512

    def body(h2_ref, p_ref, t_ref, wpg_ref, wpp_ref, gple_ref, gfin_ref,
             n3_ref, dh_ref, dgl_ref, dpp_ref, loss_ref, dg_ref, dgple_ref, pp, gate, h3):
        @pl.when(pl.program_id(0) == 0)
        def _():
            loss_ref[...] = jnp.zeros_like(loss_ref)
            dg_ref[...] = jnp.zeros_like(dg_ref)
            dgple_ref[...] = jnp.zeros_like(dgple_ref)

        x = h2_ref[...]
        _, xh = _rms_stats(x)
        n3 = _bf(xh * gple_ref[...])
        n3_ref[...] = n3
        pb = _bf(p_ref[...])
        for c in range(n // tn):
            cols = slice(c * tn, (c + 1) * tn)
            pp[:, cols] = _dot(pb, wpp_ref[:, cols])
            gt = _sigmoid(_dot(n3, wpg_ref[:, cols]))
            gate[:, cols] = gt
            h3[:, cols] = x[:, cols] + gt * pp[:, cols]
        y = h3[...]
        _, yh = _rms_stats(y)
        e = yh * gfin_ref[...] - t_ref[...]
        per_tok = jnp.mean(e * e, axis=-1, keepdims=True)
        loss_ref[...] += 0.5 * jnp.sum(per_tok, axis=0, keepdims=True)
        dh, dg = _rms_bwd_tile(y, gfin_ref[...], e * (1.0 / n))
        dg_ref[...] += dg
        gt = gate[...]
        dgl = _bf(dh * pp[...] * gt * (1.0 - gt))
        dgl_ref[...] = dgl
        dpp_ref[...] = _bf(dh * gt)
        for c in range(n // tn):
            cols = slice(c * tn, (c + 1) * tn)
            h3[:, cols] = _dot_nt(dgl, wpg_ref[cols, :])
        dx, dgp = _rms_bwd_tile(x, gple_ref[...], h3[...])
        dh_ref[...] = dh + dx
        dgple_ref[...] += dgp

    tile = lambda width: pl.BlockSpec((tm, width), lambda i: (i, 0))
    whole = lambda a: pl.BlockSpec(a.shape, lambda i, nd=a.ndim: (0,) * nd)
    return pl.pallas_call(
        body, grid=(S // tm,), name="ple_and_loss",
        in_specs=[tile(n), tile(p.shape[1]), tile(n), whole(w_pg), whole(w_pp), whole(g_ple), whole(g_final)],
        out_specs=[tile(n), tile(n), tile(n), tile(n), pl.BlockSpec((1, 128), lambda i: (0, 0)),
                   pl.BlockSpec((1, n), lambda i: (0, 0)), pl.BlockSpec((1, n), lambda i: (0, 0))],
        out_shape=[jax.ShapeDtypeStruct((S, n), BF16), jax.ShapeDtypeStruct((S, n), F32), jax.ShapeDtypeStruct((S, n), BF16),
                   jax.ShapeDtypeStruct((S, n), BF16), jax.ShapeDtypeStruct((1, 128), F32), jax.ShapeDtypeStruct((1, n), F32),
                   jax.ShapeDtypeStruct((1, n), F32)],
        scratch_shapes=[pltpu.VMEM((tm, n), F32)] * 3,
        compiler_params=_params(("arbitrary",)),
    )(h2, p, target, w_pg, w_pp, g_ple, g_final)


def _rope_tables(S):
    half = 32
    inv = (1.0 / (np.float32(ROPE_THETA) ** (np.arange(half, dtype=np.float32) * np.float32(2.0 / 64)))).astype(np.float32)
    ang = np.arange(S).astype(np.float32)[:, None] * inv[None, :]
    cos, sin = np.cos(ang), np.sin(ang)
    return jnp.asarray(np.tile(cos, (1, 4))), jnp.asarray(np.concatenate([-sin, sin, -sin, sin], axis=1))


def _attn_common(i, kc, kp, vc, vp, cc, sc, cp, sp):
    lane = lax.broadcasted_iota(jnp.int32, (1, HEAD_PAIR), 1)
    lane_lo = jnp.bitwise_and(lane, 63) < 32
    slot = [lane < 64, lane >= 64]

    def swap_halves(t):
        return jnp.where(lane_lo, pltpu.roll(t, 96, 1), pltpu.roll(t, 32, 1))

    def rope(t, cos, sin):
        return t * cos + swap_halves(t) * sin

    def unrope(d, cos, sin):
        return d * cos + swap_halves(d * sin)

    k2 = jnp.concatenate([rope(kp, cp, sp), rope(kc, cc, sc)], axis=0)
    v2 = jnp.concatenate([vp, vc], axis=0)
    r = lax.broadcasted_iota(jnp.int32, (ATTN_BLOCK, 2 * ATTN_BLOCK), 0)
    c = lax.broadcasted_iota(jnp.int32, (ATTN_BLOCK, 2 * ATTN_BLOCK), 1)
    valid = (c > r) & (c <= r + ATTN_BLOCK) & jnp.logical_or(c >= ATTN_BLOCK, i > 0)
    ks, vs = {}, {}
    for j in range(2):
        kn = jnp.where(slot[j], k2, 0.0)
        vn = jnp.where(slot[j], v2, 0.0)
        for s in range(2):
            ks[j, s] = _bf(kn if s == j else pltpu.roll(kn, 64, 1))
            vs[j, s] = _bf(vn if s == j else pltpu.roll(vn, 64, 1))
    return slot, rope, unrope, valid, ks, vs


def _attn_probs(scores, valid, sink):
    s = jnp.where(valid, scores * 0.125, NEG)
    m = jnp.maximum(jnp.max(s, axis=1, keepdims=True), sink)
    e = jnp.exp(s - m)
    z = jnp.sum(e, axis=1, keepdims=True) + jnp.exp(sink - m)
    return e * (1.0 / z), m + jnp.log(z)


def _attn_specs(S):
    nb = S // ATTN_BLOCK
    prev = lambda i: jnp.maximum(i - 1, 0)
    blk = lambda w, col, row=(lambda i: i): pl.BlockSpec((ATTN_BLOCK, w), lambda i: (row(i), col))
    in_specs = [pl.BlockSpec(memory_space=pltpu.SMEM),
                blk(512, BLK_Q), blk(128, BLK_K), blk(128, BLK_K, prev), blk(128, BLK_V), blk(128, BLK_V, prev),
                blk(128, 0), blk(128, 0), blk(128, 0, prev), blk(128, 0, prev)]
    return nb, in_specs


def _attn_fwd(pa, cos, sin, sinks):
    S = pa.shape[0]
    nb, in_specs = _attn_specs(S)

    def body(sinks_ref, q_ref, kc_ref, kp_ref, vc_ref, vp_ref, cc_ref, sc_ref, cp_ref, sp_ref, o_ref, lse_ref):
        i = pl.program_id(0)
        lane = lax.broadcasted_iota(jnp.int32, (1, HEAD_PAIR), 1)
        cc, sc = cc_ref[...], sc_ref[...]
        _, rope, _, valid, ks, vs = _attn_common(i, kc_ref[...], kp_ref[...], vc_ref[...], vp_ref[...],
                                                 cc, sc, cp_ref[...], sp_ref[...])
        pair_cols = [slice(HEAD_PAIR * pair, HEAD_PAIR * (pair + 1)) for pair in range(4)]
        qps = [_bf(rope(q_ref[:, cols], cc, sc)) for cols in pair_cols]
        outs, lses = {}, {}

        def head_program(h):
            pair, s = divmod(h, 2)
            j = h // 4
            scores = _dot_nt(qps[pair], ks[j, s])
            yield
            p, lse = _attn_probs(scores, valid, sinks_ref[h])
            outs[h] = _dot(_bf(p), vs[j, s])
            lses[h] = jnp.where(lane == h, lse, 0.0)

        _interleave(head_program(h) for h in range(8))
        for pair, cols in enumerate(pair_cols):
            o_ref[:, cols] = outs[2 * pair] + outs[2 * pair + 1]
        lse_ref[...] = sum((lses[h] for h in range(1, 8)), lses[0])

    return pl.pallas_call(
        body, grid=(nb,), name="attn_fwd", in_specs=in_specs,
        out_specs=[pl.BlockSpec((ATTN_BLOCK, 512), lambda i: (i, 0)), pl.BlockSpec((ATTN_BLOCK, 128), lambda i: (i, 0))],
        out_shape=[jax.ShapeDtypeStruct((S, 512), F32), jax.ShapeDtypeStruct((S, 128), F32)],
        compiler_params=_params(("parallel",)),
    )(sinks, pa, pa, pa, pa, pa, cos, sin, cos, sin)


def _attn_bwd(pa, cos, sin, sinks, dcat, attn, lse):
    S = pa.shape[0]
    nb, in_specs = _attn_specs(S)
    in_specs = in_specs + [pl.BlockSpec((ATTN_BLOCK, 512), lambda i: (i, 0))] * 2 + [pl.BlockSpec((ATTN_BLOCK, 128), lambda i: (i, 0))]

    def body(sinks_ref, q_ref, kc_ref, kp_ref, vc_ref, vp_ref, cc_ref, sc_ref, cp_ref, sp_ref, do_ref, o_ref, lse_ref,
             dq_ref, dk_ref, dv_ref, dsink_ref):
        i = pl.program_id(0)

        @pl.when(i == 0)
        def _():
            dk_ref[...] = jnp.zeros_like(dk_ref)
            dv_ref[...] = jnp.zeros_like(dv_ref)
            dsink_ref[...] = jnp.zeros_like(dsink_ref)

        cc, sc, cp, sp = cc_ref[...], sc_ref[...], cp_ref[...], sp_ref[...]
        slot, rope, unrope, valid, ks, vs = _attn_common(i, kc_ref[...], kp_ref[...], vc_ref[...], vp_ref[...], cc, sc, cp, sp)
        pair_cols = [slice(HEAD_PAIR * pair, HEAD_PAIR * (pair + 1)) for pair in range(4)]
        qps = [_bf(rope(q_ref[:, cols], cc, sc)) for cols in pair_cols]
        dobs = [_bf(do_ref[:, cols]) for cols in pair_cols]
        do_o = [do_ref[:, cols] * o_ref[:, cols] for cols in pair_cols]
        dqs, dks, dvs = {}, {}, {}

        def head_program(h):
            pair, s = divmod(h, 2)
            j = h // 4
            qp, dob = qps[pair], dobs[pair]
            scores = _dot_nt(qp, ks[j, s])
            dp = _dot_nt(dob, vs[j, s])
            yield
            lse_h = lse_ref[:, h:h + 1]
            p = jnp.exp(jnp.where(valid, scores * 0.125, NEG) - lse_h)
            yield
            dr = jnp.sum(jnp.where(slot[s], do_o[pair], 0.0), axis=1, keepdims=True)
            ds = _bf(p * (dp - dr) * 0.125)
            yield
            dsink_ref[h:h + 1, :] += -jnp.sum(jnp.exp(sinks_ref[h] - lse_h) * dr, axis=0, keepdims=True)
            dqs[h] = _dot(ds, ks[j, s])
            dk_h = _dot_tn(ds, qp)
            dv_h = _dot_tn(_bf(p), dob)
            yield
            dk_h, dv_h = jnp.where(slot[s], dk_h, 0.0), jnp.where(slot[s], dv_h, 0.0)
            if s != j:
                dk_h, dv_h = pltpu.roll(dk_h, 64, 1), pltpu.roll(dv_h, 64, 1)
            dks[h], dvs[h] = dk_h, dv_h

        _interleave(head_program(h) for h in range(8))
        dk2 = sum((dks[h] for h in range(1, 8)), dks[0])
        dv2 = sum((dvs[h] for h in range(1, 8)), dvs[0])
        for pair, cols in enumerate(pair_cols):
            dq_ref[:, cols] = _bf(unrope(dqs[2 * pair] + dqs[2 * pair + 1], cc, sc))
        cur = pl.ds(pl.multiple_of(i * ATTN_BLOCK, ATTN_BLOCK), ATTN_BLOCK)
        dk_ref[cur, :] += unrope(dk2[ATTN_BLOCK:], cc, sc)
        dv_ref[cur, :] += dv2[ATTN_BLOCK:]

        @pl.when(i > 0)
        def _():
            prv = pl.ds(pl.multiple_of((i - 1) * ATTN_BLOCK, ATTN_BLOCK), ATTN_BLOCK)
            dk_ref[prv, :] += unrope(dk2[:ATTN_BLOCK], cp, sp)
            dv_ref[prv, :] += dv2[:ATTN_BLOCK]

    whole = lambda w: pl.BlockSpec((S, w), lambda i: (0, 0))
    return pl.pallas_call(
        body, grid=(nb,), name="attn_bwd", in_specs=in_specs,
        out_specs=[pl.BlockSpec((ATTN_BLOCK, 512), lambda i: (i, BLK_Q)), whole(128), whole(128),
                   pl.BlockSpec((8, 128), lambda i: (0, 0))],
        out_shape=[jax.ShapeDtypeStruct((S, D_IN_PAD), BF16), jax.ShapeDtypeStruct((S, 128), F32),
                   jax.ShapeDtypeStruct((S, 128), F32), jax.ShapeDtypeStruct((8, 128), F32)],
        compiler_params=_params(("arbitrary",)),
    )(sinks, pa, pa, pa, pa, pa, cos, sin, cos, sin, dcat, attn, lse)


CONV_ROWS = 512
CONV_PAD = 8


def _conv_silu(scr, w, r0):
    y = w[3:4, :] * scr[pl.ds(CONV_PAD + r0, CONV_ROWS), :]
    for j in range(DN_CONV - 1):
        y = y + w[j:j + 1, :] * scr[pl.ds(CONV_PAD + r0 - 3 + j, CONV_ROWS), :]
    return y


def _dn_prep_fwd(pd, conv_w):
    S = pd.shape[0]
    assert S % CONV_ROWS == 0

    def body(x_ref, w_ref, o_ref, scr):
        b = pl.program_id(0)
        scr[0:CONV_PAD, :] = jnp.zeros((CONV_PAD, DN_DIM), F32)
        scr[pl.ds(CONV_PAD, S), :] = x_ref[...]
        w = w_ref[...]
        q_scale = jnp.where(b < DN_HEADS, DN_DIM ** -0.5, 1.0)
        for r0 in range(0, S, CONV_ROWS):
            y = _conv_silu(scr, w, r0)
            a = y * _sigmoid(y)
            rs = lax.rsqrt(jnp.sum(a * a, axis=1, keepdims=True) + EPS)
            o_ref[pl.ds(r0, CONV_ROWS), :] = a * jnp.where(b < 2 * DN_HEADS, rs * q_scale, 1.0)

    col = pl.BlockSpec((S, DN_DIM), lambda b: (0, b))
    return pl.pallas_call(
        body, grid=(3 * DN_HEADS,), name="dn_prep_fwd",
        in_specs=[pl.BlockSpec((S, DN_DIM), lambda b: (0, BLK_DN + b)), pl.BlockSpec((DN_CONV, DN_DIM), lambda b: (0, b))],
        out_specs=col,
        out_shape=jax.ShapeDtypeStruct((S, 3 * DN_HEADS * DN_DIM), F32),
        scratch_shapes=[pltpu.VMEM((S + CONV_PAD, DN_DIM), F32)],
        compiler_params=_params(("parallel",)),
    )(pd, conv_w)


def _dn_prep_bwd(pd, conv_w, dqkv, dproj, dk, dv):
    S = pd.shape[0]
    NB = 3 * DN_HEADS

    def body(x_ref, w_ref, d_ref, _, dk_ref, dv_ref, dx_ref, dw_ref, scr, dscr):
        b = pl.program_id(0)

        @pl.when(b == NB)
        def _():
            dx_ref[...] = _bf(dk_ref[...])

        @pl.when(b == NB + 1)
        def _():
            dx_ref[...] = _bf(dv_ref[...])

        @pl.when(b < NB)
        def _():
            scr[0:CONV_PAD, :] = jnp.zeros((CONV_PAD, DN_DIM), F32)
            scr[pl.ds(CONV_PAD, S), :] = x_ref[...]
            dscr[pl.ds(S, CONV_PAD), :] = jnp.zeros((CONV_PAD, DN_DIM), F32)
            w = w_ref[...]
            q_scale = jnp.where(b < DN_HEADS, DN_DIM ** -0.5, 1.0)
            is_qk = b < 2 * DN_HEADS
            dw = [jnp.zeros((1, DN_DIM), F32) for _ in range(DN_CONV)]
            for r0 in range(0, S, CONV_ROWS):
                y = _conv_silu(scr, w, r0)
                sg = _sigmoid(y)
                a = y * sg
                dout = d_ref[pl.ds(r0, CONV_ROWS), :]
                rs = lax.rsqrt(jnp.sum(a * a, axis=1, keepdims=True) + EPS)
                da_qk = q_scale * rs * (dout - a * (rs * rs) * jnp.sum(dout * a, axis=1, keepdims=True))
                dy = jnp.where(is_qk, da_qk, dout) * (sg * (1.0 + y * (1.0 - sg)))
                dscr[pl.ds(r0, CONV_ROWS), :] = dy
                for j in range(DN_CONV):
                    dw[j] = dw[j] + jnp.sum(dy * scr[pl.ds(CONV_PAD + r0 - 3 + j, CONV_ROWS), :], axis=0, keepdims=True)
            for j in range(DN_CONV):
                dw_ref[j:j + 1, :] = dw[j]
            for r0 in range(0, S, CONV_ROWS):
                dx = w[3:4, :] * dscr[pl.ds(r0, CONV_ROWS), :]
                for j in range(DN_CONV - 1):
                    dx = dx + w[j:j + 1, :] * dscr[pl.ds(r0 + 3 - j, CONV_ROWS), :]
                dx_ref[pl.ds(r0, CONV_ROWS), :] = _bf(dx)

    own = lambda b: jnp.minimum(b, NB - 1)
    col = pl.BlockSpec((S, DN_DIM), lambda b: (0, own(b)))
    proj_col = pl.BlockSpec((S, DN_DIM), lambda b: (0, BLK_DN + own(b)))
    wcol = pl.BlockSpec((DN_CONV, DN_DIM), lambda b: (0, own(b)))
    whole = pl.BlockSpec((S, DN_DIM), lambda b: (0, 0))
    assert BLK_K == BLK_DN + NB and BLK_V == BLK_K + 1
    return pl.pallas_call(
        body, grid=(NB + 2,), name="dn_prep_bwd",
        in_specs=[proj_col, wcol, col, pl.BlockSpec(memory_space=pl.ANY), whole, whole],
        out_specs=[pl.BlockSpec((S, DN_DIM), lambda b: (0, BLK_DN + b)), wcol],
        out_shape=[jax.ShapeDtypeStruct(dproj.shape, dproj.dtype), jax.ShapeDtypeStruct((DN_CONV, 3 * DN_HEADS * DN_DIM), F32)],
        scratch_shapes=[pltpu.VMEM((S + CONV_PAD, DN_DIM), F32), pltpu.VMEM((S + CONV_PAD, DN_DIM), F32)],
        input_output_aliases={3: 0},
        compiler_params=_params(("arbitrary",)),
    )(pd, conv_w, dqkv, dproj, dk, dv)


CPAD = 128
CHUNKS_LOCAL = 4
CHUNKS_SCAN = 8


def _chunk_masks():
    ii = lax.broadcasted_iota(jnp.int32, (DN_CHUNK, CPAD), 0)
    jj = lax.broadcasted_iota(jnp.int32, (DN_CHUNK, CPAD), 1)
    return ii, jj


def _rows_pad(a):
    return jnp.concatenate([a, jnp.zeros_like(a)], axis=0)


def _hi_lo(a):
    hi = _bf(a)
    return hi, _bf(a - hi.astype(F32))


def _double_step(t, p):
    C = DN_CHUNK
    th, tl = _hi_lo(t)
    ph, pl_ = _hi_lo(p)
    r1 = _dot(jnp.concatenate([th, tl, ph, pl_], axis=0), _rows_pad(ph))
    r2 = _dot(jnp.concatenate([th, ph], axis=0), _rows_pad(pl_))
    return t + (r1[:C] + r1[C:2 * C] + r2[:C]), r1[2 * C:3 * C] + r1[3 * C:] + r2[C:]


def _dot3_nt(a, b):
    C = DN_CHUNK
    ah, al = _hi_lo(a)
    bh, bl = _hi_lo(b)
    r1 = _dot_nt(jnp.concatenate([ah, al], axis=0), _rows_pad(bh))
    return r1[:C] + r1[C:] + _dot_nt(ah, _rows_pad(bl))


def _dot3_tn(a, b):
    C = DN_CHUNK
    ah, al = _hi_lo(a)
    bh, bl = _hi_lo(b)
    return _dot_tn(jnp.concatenate([ah, al, ah], axis=0), jnp.concatenate([bh, bh, bl], axis=0))[:C]


def _interleave(programs):
    programs = list(programs)
    while programs:
        alive = []
        for prog in programs:
            try:
                next(prog)
                alive.append(prog)
            except StopIteration:
                pass
        programs = alive


def _col_to_row(col, ii, jj):
    return jnp.sum(jnp.where(ii == jj, col, 0.0), axis=0, keepdims=True)


def _row_to_col(row, ii, jj):
    return jnp.sum(jnp.where(ii == jj, row, 0.0), axis=1, keepdims=True)


def _decay(gc_col, ii, jj):
    diff = gc_col - _col_to_row(gc_col, ii, jj)
    return jnp.where(jj <= ii, jnp.exp(jnp.where(jj <= ii, diff, 0.0)), 0.0)


def _softplus(x):
    return jnp.maximum(x, 0.0) + jnp.log(1.0 + jnp.exp(-jnp.abs(x)))


def _head(h):
    return slice(DN_DIM * h, DN_DIM * (h + 1))


def _dn_chunk_fwd(qkv, pg, a_log, dt_bias):
    S = qkv.shape[0]
    C = DN_CHUNK
    G = CHUNKS_LOCAL
    R = G * C
    steps = S // R

    def body(alog_ref, dtb_ref, qkv_ref, pg_ref, w_ref, u_ref, qg_ref, kd_ref, a_ref, t_ref, gcs_ref):
        ii, jj = _chunk_masks()
        lane = lax.broadcasted_iota(jnp.int32, (1, 128), 1)
        eye = (ii == jj).astype(F32)
        gcs_parts = [[] for _ in range(G)]

        def head_program(chunk, h):
            rows = slice(chunk * C, (chunk + 1) * C)
            q, k, v = qkv_ref[rows, _head(h)], qkv_ref[rows, _head(DN_HEADS + h)], qkv_ref[rows, _head(2 * DN_HEADS + h)]
            beta = _sigmoid(pg_ref[rows, h:h + 1])
            g_col = -jnp.exp(alog_ref[h]) * _softplus(pg_ref[rows, DN_HEADS + h:DN_HEADS + h + 1] + dtb_ref[h])
            g_row = _col_to_row(g_col, ii, jj)
            gc_col = jnp.sum(jnp.where(jj <= ii, g_row, 0.0), axis=1, keepdims=True)
            dec = _decay(gc_col, ii, jj)
            eg = jnp.exp(gc_col)
            kb, vb = k * beta, v * beta
            k_rows = _rows_pad(_bf(k))
            kk = _dot_nt(_bf(kb), k_rows)
            qk = _dot_nt(_bf(q), k_rows)
            yield
            t, pw = eye, -jnp.where(jj < ii, kk * dec, 0.0)
            for _ in range(6):
                t, pw = _double_step(t, pw)
                yield
            tb = _bf(t)
            u_ref[rows, _head(h)] = _dot(tb, _rows_pad(_bf(vb)))
            w_ref[rows, _head(h)] = _bf(_dot(tb, _rows_pad(_bf(kb * eg))))
            a_ref[h, rows] = _bf(qk * dec)
            t_ref[h, rows] = t
            qg_ref[rows, _head(h)] = _bf(q * eg)
            kd_ref[rows, _head(h)] = _bf(k * jnp.exp(gc_col[C - 1:C, :] - gc_col))
            gcs_parts[chunk].append(jnp.where(lane == h, gc_col, 0.0) + jnp.where(lane == DN_HEADS + h, beta, 0.0)
                                    + jnp.where(lane == 2 * DN_HEADS + h, g_col, 0.0))

        _interleave(head_program(chunk, h) for chunk in range(G) for h in range(DN_HEADS))
        for chunk in range(G):
            gcs_ref[chunk * C:(chunk + 1) * C, :] = sum(gcs_parts[chunk][1:], gcs_parts[chunk][0])

    smem = pl.BlockSpec(memory_space=pltpu.SMEM)
    wide = pl.BlockSpec((R, 512), lambda n: (n, 0))
    sq = pl.BlockSpec((DN_HEADS, R, CPAD), lambda n: (0, n, 0))
    narrow = pl.BlockSpec((R, 128), lambda n: (n, 0))
    f = lambda *shp: jax.ShapeDtypeStruct(shp, F32)
    b = lambda *shp: jax.ShapeDtypeStruct(shp, BF16)
    return pl.pallas_call(
        body, grid=(steps,), name="dn_chunk_fwd",
        in_specs=[smem, smem, pl.BlockSpec((R, 1536), lambda n: (n, 0)), pl.BlockSpec((R, 128), lambda n: (n, BLK_G))],
        out_specs=[wide, wide, wide, wide, sq, sq, narrow],
        out_shape=[b(S, 512), f(S, 512), b(S, 512), b(S, 512), b(DN_HEADS, S, CPAD), f(DN_HEADS, S, CPAD), f(S, 128)],
        compiler_params=_params(("parallel",)),
    )(a_log, dt_bias, qkv, pg)


def _gated_norm(o, z, gn):
    r, oh = _rms_stats(o)
    return oh * gn * (z * _sigmoid(z))


def _dn_scan_fwd(w, u, qg, kd, a, gcs, pz, gn):
    S = w.shape[0]
    C = DN_CHUNK
    nc = S // C
    G = CHUNKS_SCAN
    R = G * C

    def body(w_ref, u_ref, qg_ref, kd_ref, a_ref, gcs_ref, z_ref, gn_ref, o_ref, vn_ref, sst_ref, out_ref, state):
        @pl.when(pl.program_id(0) == 0)
        def _():
            state[...] = jnp.zeros_like(state)

        def head_program(chunk, h):
            hs = _head(h)
            rows = slice(chunk * C, (chunk + 1) * C)
            s_in = state[h]
            sb = _bf(s_in)
            sst_ref[chunk, h] = sb
            w_s = _dot(w_ref[rows, hs], sb)
            q_s = _dot(qg_ref[rows, hs], sb)
            yield
            vn = u_ref[rows, hs] - w_s
            vnb = _bf(vn)
            o = q_s + _dot(a_ref[h, rows], _rows_pad(vnb))
            k_v = _dot_tn(kd_ref[rows, hs], vnb)
            yield
            state[h] = s_in * jnp.exp(gcs_ref[(chunk + 1) * C - 1:(chunk + 1) * C, h:h + 1]) + k_v
            o_ref[rows, hs] = o
            vn_ref[rows, hs] = vnb
            out_ref[rows, hs] = _bf(_gated_norm(o, z_ref[rows, hs], gn_ref[...]))

        for chunk in range(G):
            _interleave(head_program(chunk, h) for h in range(DN_HEADS))

    wide = pl.BlockSpec((R, 512), lambda n: (n, 0))
    f = lambda *shp: jax.ShapeDtypeStruct(shp, F32)
    b = lambda *shp: jax.ShapeDtypeStruct(shp, BF16)
    return pl.pallas_call(
        body, grid=(nc // G,), name="dn_scan_fwd",
        in_specs=[wide, wide, wide, wide, pl.BlockSpec((DN_HEADS, R, CPAD), lambda n: (0, n, 0)),
                  pl.BlockSpec((R, 128), lambda n: (n, 0)), pl.BlockSpec((R, 512), lambda n: (n, BLK_Z)),
                  pl.BlockSpec((1, DN_DIM), lambda n: (0, 0))],
        out_specs=[wide, wide, pl.BlockSpec((G, DN_HEADS, DN_DIM, DN_DIM), lambda n: (n, 0, 0, 0)), wide],
        out_shape=[f(S, 512), b(S, 512), b(nc, DN_HEADS, DN_DIM, DN_DIM), b(S, 512)],
        scratch_shapes=[pltpu.VMEM((DN_HEADS, DN_DIM, DN_DIM), F32)],
        compiler_params=_params(("arbitrary",)),
    )(w, u, qg, kd, a, gcs, pz, gn)


def _dn_scan_bwd(dcat, o, pz, gn, sst, vnew, w, qg, kd, a, gcs, dproj):
    S = o.shape[0]
    C = DN_CHUNK
    G = CHUNKS_SCAN
    R = G * C
    steps = S // R

    def body(dy_ref, o_ref, z_ref, gn_ref, sst_ref, vn_ref, w_ref, qg_ref, kd_ref, a_ref, gcs_ref, _,
             du_ref, dw_ref, dqg_ref, dkd_ref, da_ref, dz_ref, dsc_ref, dgn_ref, dstate):
        @pl.when(pl.program_id(0) == 0)
        def _():
            dstate[...] = jnp.zeros_like(dstate)
            dgn_ref[...] = jnp.zeros_like(dgn_ref)

        gn_ = gn_ref[...]
        lane = lax.broadcasted_iota(jnp.int32, (C, 128), 1)
        row = lax.broadcasted_iota(jnp.int32, (C, 128), 0)
        dgn_parts = []

        def head_program(chunk, h, dsc_parts):
            hs = _head(h)
            rows = slice(chunk * C, (chunk + 1) * C)
            ov, z, dout = o_ref[rows, hs], z_ref[rows, hs], dy_ref[rows, hs]
            r, oh = _rms_stats(ov)
            sg = _sigmoid(z)
            don = dout * (z * sg)
            dz_ref[rows, hs] = _bf(dout * (oh * gn_) * (sg * (1.0 + z * (1.0 - sg))))
            dgn_parts.append(jnp.sum(don * oh, axis=0, keepdims=True))
            dn = don * gn_
            do = _bf(r * (dn - oh * jnp.mean(dn * oh, axis=-1, keepdims=True)))
            sb = sst_ref[chunk, h]
            s_in = sb.astype(F32)
            ds_out = dstate[h]
            dsb = _bf(ds_out)
            vnb = vn_ref[rows, hs]
            wb, qgb, kdb, ab = w_ref[rows, hs], qg_ref[rows, hs], kd_ref[rows, hs], a_ref[h, rows]
            dvn = _dot_tn(ab, do)[:C] + _dot(kdb, dsb)
            yield
            da_ref[h, rows] = _dot_nt(do, _rows_pad(vnb))
            dqg_ref[rows, hs] = _dot_nt(do, sb)
            dkd_ref[rows, hs] = _dot_nt(vnb, dsb)
            q_do = _dot_tn(qgb, do)
            yield
            dvnb = _bf(dvn)
            dw_ref[rows, hs] = _bf(-_dot_nt(dvnb, sb))
            w_dvn = _dot_tn(wb, dvnb)
            du_ref[rows, hs] = dvnb
            yield
            d_last = jnp.exp(gcs_ref[(chunk + 1) * C - 1:(chunk + 1) * C, h:h + 1])
            dd = jnp.sum(jnp.sum(ds_out * s_in, axis=1, keepdims=True), axis=0, keepdims=True)
            dsc_parts.append(jnp.where((lane == h) & (row == C - 1), dd * d_last, 0.0))
            dstate[h] = ds_out * d_last + q_do - w_dvn

        for chunk in reversed(range(G)):
            dsc_parts = []
            _interleave(head_program(chunk, h, dsc_parts) for h in range(DN_HEADS))
            dsc_ref[chunk * C:(chunk + 1) * C, :] = sum(dsc_parts[1:], dsc_parts[0])
        dgn_ref[...] += sum(dgn_parts[1:], dgn_parts[0])

    rev = lambda n: steps - 1 - n
    wide = pl.BlockSpec((R, 512), lambda n: (rev(n), 0))
    z_spec = pl.BlockSpec((R, 512), lambda n: (rev(n), BLK_Z))
    sq = pl.BlockSpec((DN_HEADS, R, CPAD), lambda n: (0, rev(n), 0))
    narrow = pl.BlockSpec((R, 128), lambda n: (rev(n), 0))
    gn_spec = pl.BlockSpec((1, DN_DIM), lambda n: (0, 0))
    f = lambda *shp: jax.ShapeDtypeStruct(shp, F32)
    b = lambda *shp: jax.ShapeDtypeStruct(shp, BF16)
    return pl.pallas_call(
        body, grid=(steps,), name="dn_scan_bwd",
        in_specs=[pl.BlockSpec((R, 512), lambda n: (rev(n), 1)), wide, z_spec, gn_spec,
                  pl.BlockSpec((G, DN_HEADS, DN_DIM, DN_DIM), lambda n: (rev(n), 0, 0, 0)),
                  wide, wide, wide, wide, sq, narrow, pl.BlockSpec(memory_space=pl.ANY)],
        out_specs=[wide, wide, wide, wide, sq, z_spec, narrow, gn_spec],
        out_shape=[b(S, 512), b(S, 512), f(S, 512), f(S, 512), f(DN_HEADS, S, CPAD),
                   jax.ShapeDtypeStruct(dproj.shape, dproj.dtype), f(S, 128), f(1, DN_DIM)],
        scratch_shapes=[pltpu.VMEM((DN_HEADS, DN_DIM, DN_DIM), F32)],
        input_output_aliases={11: 5},
        compiler_params=_params(("arbitrary",)),
    )(dcat, o, pz, gn, sst, vnew, w, qg, kd, a, gcs, dproj)


def _dn_chunk_bwd(qkv, pg, t_inv, gcs, du, dw, dqg, dkd, da, dsc, a_log, dt_bias, dproj):
    S = qkv.shape[0]
    C = DN_CHUNK
    G = CHUNKS_LOCAL
    R = G * C

    def body(alog_ref, dtb_ref, qkv_ref, pg_ref, t_ref, gcs_ref, du_ref, dw_ref, dqg_ref, dkd_ref, da_ref, dsc_ref, _,
             dqkv_ref, dpg_ref, acc_ref):
        @pl.when(pl.program_id(0) == 0)
        def _():
            acc_ref[...] = jnp.zeros_like(acc_ref)

        ii, jj = _chunk_masks()
        lane = lax.broadcasted_iota(jnp.int32, (1, 128), 1)
        row8 = lax.broadcasted_iota(jnp.int32, (8, 128), 0)
        lane8 = lax.broadcasted_iota(jnp.int32, (8, 128), 1)
        rowc = lax.broadcasted_iota(jnp.int32, (C, 1), 0)
        tril, strict = jj <= ii, jj < ii
        dpg_parts, acc_parts = [[] for _ in range(G)], []

        def head_program(chunk, h):
            rows = slice(chunk * C, (chunk + 1) * C)
            q, k, v = qkv_ref[rows, _head(h)], qkv_ref[rows, _head(DN_HEADS + h)], qkv_ref[rows, _head(2 * DN_HEADS + h)]
            gc_col, beta, g_col = gcs_ref[rows, h:h + 1], gcs_ref[rows, DN_HEADS + h:DN_HEADS + h + 1], \
                gcs_ref[rows, 2 * DN_HEADS + h:2 * DN_HEADS + h + 1]
            dec = _decay(gc_col, ii, jj)
            eg = jnp.exp(gc_col)
            g_last = gc_col[C - 1:C, :]
            ek = jnp.exp(g_last - gc_col)
            kb, vb = k * beta, v * beta
            kbg = kb * eg
            qb, kbb = _bf(q), _bf(kb)
            k_rows = _rows_pad(_bf(k))
            t = t_ref[h, rows]
            tb = _bf(t)
            dub, dwb = du_ref[rows, _head(h)], dw_ref[rows, _head(h)]
            dqg_, dkd_ = dqg_ref[rows, _head(h)], dkd_ref[rows, _head(h)]
            dt = _dot_nt(dub, _rows_pad(_bf(vb))) + _dot_nt(dwb, _rows_pad(_bf(kbg)))
            t_du_dw = _dot_tn(tb, jnp.concatenate([dub, dwb], axis=1))
            dvb, dkbg = t_du_dw[:C, :DN_DIM], t_du_dw[:C, DN_DIM:]
            kk = _dot_nt(kbb, k_rows)
            qk = _dot_nt(qb, k_rows)
            yield
            dt_t = _dot3_nt(dt, t)
            yield
            dl = -_dot3_tn(t, dt_t)
            yield
            dm = jnp.where(strict, dl * dec, 0.0)
            dqk = jnp.where(tril, da_ref[h, rows] * dec, 0.0)
            gmat = dm * kk + dqk * qk
            dgc = jnp.sum(gmat, axis=1, keepdims=True) - _row_to_col(jnp.sum(gmat, axis=0, keepdims=True), ii, jj)
            dmb, dqkb = _bf(dm), _bf(dqk)
            yield
            dkb = _dot(dmb, k_rows) + dkbg * eg
            dk = _dot_tn(jnp.concatenate([dmb, dqkb], axis=0), jnp.concatenate([kbb, qb], axis=0))[:C] + dkd_ * ek
            dq = _dot(dqkb, k_rows) + dqg_ * eg
            yield
            tk = jnp.sum(dkd_ * k * ek, axis=1, keepdims=True)
            dgc = dgc + jnp.sum(dqg_ * q * eg, axis=1, keepdims=True) - tk + jnp.sum(dkbg * kbg, axis=1, keepdims=True)
            dgl = jnp.sum(tk, axis=0, keepdims=True) + dsc_ref[(chunk + 1) * C - 1:(chunk + 1) * C, h:h + 1]
            dgc = dgc + jnp.where(rowc == C - 1, dgl, 0.0)
            yield
            dk = dk + dkb * beta
            dbeta = jnp.sum(dkb * k, axis=1, keepdims=True) + jnp.sum(dvb * v, axis=1, keepdims=True)
            dqkv_ref[rows, _head(h)] = dq
            dqkv_ref[rows, _head(DN_HEADS + h)] = dk
            dqkv_ref[rows, _head(2 * DN_HEADS + h)] = dvb * beta
            dg_col = jnp.sum(jnp.where(jj >= ii, _col_to_row(dgc, ii, jj), 0.0), axis=1, keepdims=True)
            yield
            db = dbeta * beta * (1.0 - beta)
            da_in = dg_col * (-jnp.exp(alog_ref[h])) * _sigmoid(pg_ref[rows, DN_HEADS + h:DN_HEADS + h + 1] + dtb_ref[h])
            dpg_parts[chunk].append(jnp.where(lane == h, db, 0.0) + jnp.where(lane == DN_HEADS + h, da_in, 0.0))
            acc_parts.append(jnp.where((row8 == 0) & (lane8 == h), jnp.sum(dg_col * g_col, axis=0, keepdims=True), 0.0)
                             + jnp.where((row8 == 1) & (lane8 == h), jnp.sum(da_in, axis=0, keepdims=True), 0.0))

        _interleave(head_program(chunk, h) for chunk in range(G) for h in range(DN_HEADS))
        for chunk in range(G):
            dpg = sum(dpg_parts[chunk][1:], dpg_parts[chunk][0])
            dpg_ref[chunk * C:(chunk + 1) * C, :] = _bf(jnp.concatenate([dpg, jnp.zeros_like(dpg)], axis=1))
        acc_ref[...] += sum(acc_parts[1:], acc_parts[0])

    smem = pl.BlockSpec(memory_space=pltpu.SMEM)
    wide = pl.BlockSpec((R, 512), lambda n: (n, 0))
    sq = pl.BlockSpec((DN_HEADS, R, CPAD), lambda n: (0, n, 0))
    narrow = pl.BlockSpec((R, 128), lambda n: (n, 0))
    qkv_spec = pl.BlockSpec((R, 1536), lambda n: (n, 0))
    f = lambda *shp: jax.ShapeDtypeStruct(shp, F32)
    return pl.pallas_call(
        body, grid=(S // R,), name="dn_chunk_bwd",
        in_specs=[smem, smem, qkv_spec, pl.BlockSpec((R, 128), lambda n: (n, BLK_G)), sq, narrow, wide, wide, wide, wide, sq,
                  narrow, pl.BlockSpec(memory_space=pl.ANY)],
        out_specs=[qkv_spec, pl.BlockSpec((R, 256), lambda n: (n, BLK_G_PAD)), pl.BlockSpec((8, 128), lambda n: (0, 0))],
        out_shape=[f(S, 1536), jax.ShapeDtypeStruct(dproj.shape, dproj.dtype), f(8, 128)],
        input_output_aliases={12: 1},
        compiler_params=_params(("arbitrary",)),
    )(a_log, dt_bias, qkv, pg, t_inv, gcs, du, dw, dqg, dkd, da, dsc, dproj)


_W_IN_SECTIONS = ((0, 0, 512), (2304, 512, 512), (768, 1024, 1536), (512, 2560, 256), (2816, 2816, 8))
_HALF = D_MODEL // 2
_SECTION_ROWS = 256


def _pack_pairs(x):
    bits = lax.bitcast_convert_type(x, jnp.uint32)
    return lax.bitcast_convert_type(bits[:, _HALF:] | (bits[:, :_HALF] >> 16), F32)


def _unpack_pairs(words):
    bits = lax.bitcast_convert_type(words, jnp.uint32)
    return (lax.bitcast_convert_type(bits << 16, F32),
            lax.bitcast_convert_type(bits & jnp.uint32(0xFFFF0000), F32))


def _w_in_to_internal(packed):
    def body(x_ref, o_ref):
        for src, dst, rows in _W_IN_SECTIONS:
            for r in range(0, rows, _SECTION_ROWS):
                n = min(_SECTION_ROWS, rows - r)
                halves = _unpack_pairs(x_ref[pl.ds(src + r, n), 0, :])
                if dst + r + n == D_IN:
                    halves = [jnp.concatenate([h, jnp.zeros((D_IN_PAD - D_IN, _HALF), F32)], axis=0) for h in halves]
                for c, h in enumerate(halves):
                    o_ref[pl.ds(dst + r, h.shape[0]), c * _HALF:(c + 1) * _HALF] = _bf(h)

    assert max(dst + rows for _, dst, rows in _W_IN_SECTIONS) == D_IN
    return pl.pallas_call(body, name="w_in_to_internal", out_shape=jax.ShapeDtypeStruct((D_IN_PAD, D_MODEL), BF16),
                          compiler_params=pltpu.CompilerParams(vmem_limit_bytes=VMEM_LIMIT))(packed)


def _w_in_from_internal(gt):
    def body(g_ref, o_ref):
        for dst, src, rows in _W_IN_SECTIONS:
            for r in range(0, rows, _SECTION_ROWS):
                n = min(_SECTION_ROWS, rows - r)
                whole_tiles = max(n, 16)
                words = _pack_pairs(g_ref[pl.ds(src + r, whole_tiles), :].astype(F32))
                o_ref[pl.ds(dst + r, n), 0, :] = words[:n]

    return pl.pallas_call(body, name="w_in_from_internal", out_shape=jax.ShapeDtypeStruct((D_IN, 1, _HALF), F32),
                          compiler_params=pltpu.CompilerParams(vmem_limit_bytes=VMEM_LIMIT))(gt)


def _local_step(x, p, target, wts, first_weights, other_weights, ship_early):
    S = x.shape[0]
    cos, sin = _rope_tables(S)
    sinks, a_log, dt_bias = wts["sinks"].reshape(8), wts["a_log"].reshape(4), wts["dt_bias"].reshape(4)
    gn = wts["dn_norm"].reshape(1, DN_DIM)
    add = lambda acc, res: (acc + res,)

    u = _rmsnorm_fwd(x, wts["norm_mix"], "norm_mix_fwd")
    w_in_t, conv_w = first_weights(u)
    proj, = _mm(u, w_in_t, form="nt", name="in_proj", out_dtypes=[F32], tn=512)
    attn, lse = _attn_fwd(proj, cos, sin, sinks)
    qkv = _dn_prep_fwd(proj, conv_w)
    cw, cu, cqg, ckd, ca, ct, gcs = _dn_chunk_fwd(qkv, proj, a_log, dt_bias)
    o, vnew, sst, dn_out = _dn_scan_fwd(cw, cu, cqg, ckd, ca, gcs, proj, gn)
    w_o, = other_weights(("w_o",), dn_out)
    h1, = _mm([attn, dn_out], w_o, form="nn", name="out_proj", out_dtypes=[F32], tn=512, epi=add, extra=[x])

    w_up, w_down = other_weights(("w_up", "w_down"), h1)
    hid, relu, m, h2 = _mlp_fwd(h1, w_up, w_down, wts["norm_mlp"])
    w_pg, w_pp = other_weights(("w_ple_gate", "w_ple_proj"), h2)
    n3, dh2, dgl, dpp, loss, d_norm_final, d_norm_ple = _ple_and_loss(h2, p, target, w_pg, w_pp, wts["norm_ple"],
                                                                     wts["norm_final"].reshape(1, D_MODEL))
    g = {"norm_final": d_norm_final, "norm_ple": d_norm_ple}
    early = {"w_ple_gate": _mm_tn(n3, dgl, name="d_w_ple_gate", tm=512, tn=1024, out_dtype=BF16).reshape(N_DEV, 128, 1024),
             "w_ple_proj": _mm_tn(p, dpp, name="d_w_ple_proj", tm=256, tn=128, out_dtype=BF16, column_shards=True)}
    d_act, = _mm(dh2, w_down, form="nt", name="d_hidden", out_dtypes=[BF16], tn=512,
                 epi=lambda acc, r: (acc * (2.0 * r.astype(F32)),), extra=[relu])
    early["w_down"] = _mm_tn(hid, dh2, name="d_w_down", tm=512, tn=1024, out_dtype=BF16).reshape(N_DEV, 512, 1024)
    early["w_up"] = _mm_tn(m, d_act, name="d_w_up", tm=1024, tn=512, out_dtype=BF16, column_shards=True)
    token = ship_early(early)
    dh1, g["norm_mlp"], dcat = _mm(d_act, w_up, form="nt", name="d_m", out_dtypes=[F32], tn=512, after=token,
                                   norm_bwd=(h1, wts["norm_mlp"], dh2), then_nt=w_o)
    d_w_o = jnp.concatenate([_mm_tn(attn, dh1, name="d_w_o_attn", tm=512, tn=512, out_dtype=BF16),
                             _mm_tn(dn_out, dh1, name="d_w_o_dn", tm=512, tn=512, out_dtype=BF16)], axis=0)
    token = ship_early({"w_o": d_w_o.reshape(N_DEV, 128, 1024)})
    dproj, dk, dv, dsinks = _attn_bwd(proj, cos, sin, sinks + token[0, 0], dcat, attn, lse)
    g["sinks"] = dsinks[:, 0].reshape(1, 8)
    du_, dw_, dqg, dkd, da, dproj, dsc, g["dn_norm"] = _dn_scan_bwd(dcat, o, proj, gn, sst, vnew, cw, cqg, ckd, ca, gcs, dproj)
    dqkv, dproj, gate_acc = _dn_chunk_bwd(qkv, proj, ct, gcs, du_, dw_, dqg, dkd, da, dsc, a_log, dt_bias, dproj)
    g["a_log"], g["dt_bias"] = gate_acc[0:1, 0:4], gate_acc[1:2, 0:4]
    dproj, g["conv_w"] = _dn_prep_bwd(proj, conv_w, dqkv, dproj, dk, dv)
    token = ship_early({"w_in": _mm_tn(dproj, u, name="d_w_in", tm=512, tn=1024, out_dtype=BF16)})
    grad_x, g["norm_mix"] = _mm(dproj, w_in_t, form="nn", name="d_u", out_dtypes=[F32], tn=512, after=token,
                                norm_bwd=(x, wts["norm_mix"], dh1))
    return loss, grad_x, g


def _peer(k):
    x, y, c = lax.axis_index("x"), lax.axis_index("y"), lax.axis_index("c")
    px = 1 - x if k & 4 else x
    py = 1 - y if k & 2 else y
    pc = 1 - c if k & 1 else c
    return (px, py, pc), 4 * px + 2 * py + pc


def _exchange(srcs, name, gather):
    n = len(srcs)
    gathers = list(gather) if isinstance(gather, (list, tuple)) else [gather] * n
    shapes = [(N_DEV,) + s.shape if gt else s.shape for s, gt in zip(srcs, gathers)]

    def body(*refs):
        src_refs, out_refs = refs[:n], refs[n:2 * n]
        send_sems, recv_sems, local_sems = refs[2 * n:]
        _, me = _peer(0)
        piece = lambda a, d: src_refs[a] if gathers[a] else src_refs[a].at[d]
        local = [pltpu.make_async_copy(piece(a, me), out_refs[a].at[me], local_sems.at[a]) for a in range(n)]
        for cp in local:
            cp.start()
        copies = []
        for a in range(n):
            for k in range(1, N_DEV):
                dev, idx = _peer(k)
                cp = pltpu.make_async_remote_copy(src_ref=piece(a, idx), dst_ref=out_refs[a].at[me],
                                                  send_sem=send_sems.at[a, k - 1], recv_sem=recv_sems.at[a, k - 1],
                                                  device_id=dev, device_id_type=MESH)
                cp.start()
                copies.append(cp)
        for cp in copies:
            cp.wait_recv()
        for cp in copies:
            cp.wait_send()
        for cp in local:
            cp.wait()

    anywhere = pl.BlockSpec(memory_space=pl.ANY)
    return pl.pallas_call(
        body, name=name, in_specs=[anywhere] * n, out_specs=[anywhere] * n,
        out_shape=[jax.ShapeDtypeStruct(shp, s.dtype) for shp, s in zip(shapes, srcs)],
        scratch_shapes=[pltpu.SemaphoreType.DMA((n, N_DEV - 1)), pltpu.SemaphoreType.DMA((n, N_DEV - 1)),
                        pltpu.SemaphoreType.DMA((n,))],
    )(*srcs)


_HBM = pl.BlockSpec(memory_space=pltpu.HBM)
_SEM = pl.BlockSpec(memory_space=pltpu.SEMAPHORE)
_EFFECT = pltpu.SideEffectType.DATAFLOW_SIDE_EFFECTING


def _split_copies(src_refs, land_refs, send_sems, recv_sems, modes, which=None):
    _, me = _peer(0)
    local, remote = [], []
    which = range(len(src_refs)) if which is None else which
    for a, src, land in zip(which, src_refs, land_refs):
        if modes[a] == "columns":
            n_cols = src.shape[1]
            dst = land.at[:, pl.ds(pl.multiple_of(me * n_cols, n_cols), n_cols)]
        else:
            dst = land.at[me]
        part = lambda d: src.at[d] if modes[a] == "pieces" else src
        local.append(pltpu.make_async_copy(part(me), dst, recv_sems.at[a * N_DEV]))
        for k in ((2, 4, 6) if modes[a] == "chips" else range(1, N_DEV)):
            dev, idx = _peer(k)
            sem = a * N_DEV + k
            remote.append(pltpu.make_async_remote_copy(
                src_ref=part(idx), dst_ref=dst, send_sem=send_sems.at[sem], recv_sem=recv_sems.at[sem],
                device_id=dev, device_id_type=MESH))
    return local, remote


def _forward_copies(land_refs, send_sems, recv_sems):
    c = lax.axis_index("c")
    sibling, _ = _peer(1)
    copies = []
    for a, land in enumerate(land_refs):
        for chip in range(N_DEV // 2):
            slot = 2 * chip + c
            sem = a * (N_DEV // 2) + chip
            copies.append(pltpu.make_async_remote_copy(
                src_ref=land.at[slot], dst_ref=land.at[slot], send_sem=send_sems.at[sem], recv_sem=recv_sems.at[sem],
                device_id=sibling, device_id_type=MESH))
    return copies


def _forward_start(lands, name):
    n = len(lands)

    def body(*refs):
        for cp in _forward_copies(refs[:n], refs[n], refs[n + 1]):
            cp.start()
        refs[-1][...] = jnp.zeros_like(refs[-1])

    sems = pltpu.SemaphoreType.DMA((n * (N_DEV // 2),))
    out = pl.pallas_call(
        body, name=name,
        out_shape=(sems, sems, *[pltpu.HBM(t.shape, t.dtype) for t in lands], jax.ShapeDtypeStruct((8, 128), F32)),
        in_specs=[_HBM] * n, out_specs=(_SEM, _SEM, *[_HBM] * n, pl.BlockSpec(memory_space=pltpu.VMEM)),
        input_output_aliases={i: 2 + i for i in range(n)},
        compiler_params=pltpu.CompilerParams(has_side_effects=_EFFECT),
    )(*[pltpu.with_memory_space_constraint(t, pltpu.HBM) for t in lands])
    return out[:-1], out[-1]


def _forward_wait(handle, after, name):
    send_sems, recv_sems, *lands = handle
    n = len(lands)

    def body(*refs):
        for cp in _forward_copies(refs[:n], refs[n], refs[n + 1]):
            cp.wait_send()
            cp.wait_recv()

    return list(pl.pallas_call(
        body, name=name, out_shape=tuple(pltpu.HBM(t.shape, t.dtype) for t in lands),
        in_specs=[_HBM] * n + [_SEM, _SEM, pl.BlockSpec(memory_space=pl.ANY)], out_specs=tuple([_HBM] * n),
        input_output_aliases={i: i for i in range(n)},
        compiler_params=pltpu.CompilerParams(has_side_effects=_EFFECT),
    )(*lands, send_sems, recv_sems, after))


def _exchange_start(srcs, name, modes):
    n = len(srcs)
    modes = [modes] * n if isinstance(modes, str) else list(modes)
    lands = []
    for s, mode in zip(srcs, modes):
        shape = {"columns": (s.shape[0], N_DEV * s.shape[1]), "pieces": s.shape}.get(mode, (N_DEV,) + s.shape)
        lands.append(lax.empty(shape, s.dtype))

    def body(*refs):
        src_refs, land_refs = refs[:n], refs[n:2 * n]
        send_sems, recv_sems = refs[2 * n], refs[2 * n + 1]
        local, remote = _split_copies(src_refs, land_refs, send_sems, recv_sems, modes)
        for cp in local + remote:
            cp.start()
        refs[-1][...] = jnp.zeros_like(refs[-1])

    both = list(srcs) + lands
    sems = pltpu.SemaphoreType.DMA((n * N_DEV,))
    out = pl.pallas_call(
        body, name=name,
        out_shape=(sems, sems, *[pltpu.HBM(t.shape, t.dtype) for t in both], jax.ShapeDtypeStruct((8, 128), F32)),
        in_specs=[_HBM] * (2 * n), out_specs=(_SEM, _SEM, *[_HBM] * (2 * n), pl.BlockSpec(memory_space=pltpu.VMEM)),
        input_output_aliases={i: 2 + i for i in range(2 * n)},
        compiler_params=pltpu.CompilerParams(has_side_effects=_EFFECT),
    )(*[pltpu.with_memory_space_constraint(t, pltpu.HBM) for t in both])
    return (n, modes, out[:-1]), out[-1]


def _exchange_wait(handle, after, name, which=None):
    n_all, modes, (send_sems, recv_sems, *both_all) = handle
    which = list(range(n_all)) if which is None else list(which)
    n = len(which)
    both = [both_all[a] for a in which] + [both_all[n_all + a] for a in which]

    def body(*refs):
        src_refs, land_refs = refs[:n], refs[n:2 * n]
        local, remote = _split_copies(src_refs, land_refs, refs[2 * n], refs[2 * n + 1], modes, which)
        for cp in local:
            cp.wait()
        for cp in remote:
            cp.wait_send()
            cp.wait_recv()

    out = pl.pallas_call(
        body, name=name, out_shape=tuple(pltpu.HBM(t.shape, t.dtype) for t in both),
        in_specs=[_HBM] * (2 * n) + [_SEM, _SEM, pl.BlockSpec(memory_space=pl.ANY)], out_specs=tuple([_HBM] * (2 * n)),
        input_output_aliases={i: i for i in range(2 * n)},
        compiler_params=pltpu.CompilerParams(has_side_effects=_EFFECT),
    )(*both, send_sems, recv_sems, after)
    return list(out[n:])


def _cast_all(arrays):
    def body(*refs):
        for src, dst in zip(refs[:len(arrays)], refs[len(arrays):]):
            if len(src.shape) == 2:
                dst[...] = _bf(src[...])
            else:
                dst[:, 0, :] = _pack_pairs(_bf(src[:, 0, :]).astype(F32))

    shapes = [jax.ShapeDtypeStruct(a.shape, BF16) if a.ndim == 2 else jax.ShapeDtypeStruct((a.shape[0], 1, a.shape[2] // 2), F32)
              for a in arrays]
    return pl.pallas_call(body, name="cast_shards", out_shape=shapes,
                          compiler_params=pltpu.CompilerParams(vmem_limit_bytes=VMEM_LIMIT))(*arrays)


def _adam_update(g, w, m, v):
    nm = ADAM_B1 * m + (1.0 - ADAM_B1) * g
    nv = ADAM_B2 * v + (1.0 - ADAM_B2) * (g * g)
    m_hat = nm / (1.0 - ADAM_B1 ** ADAM_STEP)
    v_hat = nv / (1.0 - ADAM_B2 ** ADAM_STEP)
    return -ADAM_LR * (m_hat / (jnp.sqrt(v_hat) + ADAM_EPS) + ADAM_WD * w), nm, nv


def _adamw(parts, w, m, v, name):
    rows_apart = w.ndim == 3
    n, R, W = (parts.shape[0], parts.shape[1], 2 * parts.shape[3]) if rows_apart else parts.shape
    tm = 128 if R % 128 == 0 else R
    get = (lambda ref: ref[:, 0, :]) if rows_apart else (lambda ref: ref[...])

    def body(p_ref, w_ref, m_ref, v_ref, *out_refs):
        if rows_apart:
            part = lambda s: jnp.concatenate(_unpack_pairs(p_ref[s, :, 0, :]), axis=1)
        else:
            part = lambda s: p_ref[s].astype(F32)
        g = part(0)
        for s in range(1, n):
            g = g + part(s)
        for ref, val in zip(out_refs, (g,) + _adam_update(g, get(w_ref), get(m_ref), get(v_ref))):
            if rows_apart:
                ref[:, 0, :] = val
            else:
                ref[...] = val

    tile = pl.BlockSpec((tm, 1, W), lambda i: (i, 0, 0)) if rows_apart else pl.BlockSpec((tm, W), lambda i: (i, 0))
    return pl.pallas_call(
        body, grid=(R // tm,), name=name,
        in_specs=[pl.BlockSpec((n, tm, 1, W // 2), lambda i: (0, i, 0, 0)) if rows_apart else pl.BlockSpec((n, tm, W), lambda i: (0, i, 0)),
                  tile, tile, tile],
        out_specs=[tile] * 4, out_shape=[jax.ShapeDtypeStruct(w.shape, F32)] * 4,
        compiler_params=_params(("parallel",)),
    )(parts, w, m, v)


_MATRICES = ("w_in", "w_o", "w_up", "w_down", "w_ple_gate", "w_ple_proj")


_OTHERS = ("w_o", "w_up", "w_down", "w_ple_gate", "w_ple_proj")
_OTHER_MODES = {"w_o": "slots", "w_up": "slots", "w_down": "slots", "w_ple_gate": "slots", "w_ple_proj": "columns"}


_VECTORS = ("norm_mix", "norm_mlp", "norm_ple", "norm_final", "a_log", "dt_bias", "sinks", "dn_norm")
_SMALL_ROWS, _LOSS_ROW, _CONV_ROW = 16, 8, 9


def _pack_small(vectors, loss, conv):
    def body(*refs):
        out = refs[-1]
        out[...] = jnp.zeros_like(out)
        for r, ref in enumerate(refs[:len(_VECTORS)]):
            out[r:r + 1, 0:ref.shape[1]] = ref[...]
        out[_LOSS_ROW:_LOSS_ROW + 1, 0:128] = refs[len(_VECTORS)][...]
        out[_CONV_ROW:_CONV_ROW + 6, :] = refs[len(_VECTORS) + 1][...]

    return pl.pallas_call(body, name="pack_small", out_shape=jax.ShapeDtypeStruct((_SMALL_ROWS, 1024), F32))(*vectors, loss, conv)


def _sum_slots(parts):
    def body(p_ref, o_ref):
        acc = p_ref[0]
        for s in range(1, parts.shape[0]):
            acc = acc + p_ref[s]
        o_ref[...] = acc

    return pl.pallas_call(body, name="sum_small", out_shape=jax.ShapeDtypeStruct(parts.shape[1:], parts.dtype))(parts)


def _adamw_vectors(summed, conv_g, wmv):
    names = _VECTORS + ("conv_w",)
    flat = [a for triple in wmv for a in triple]

    def body(*refs):
        sum_ref, conv_ref = refs[0], refs[1]
        ins, outs = refs[2:2 + len(flat)], refs[2 + len(flat):]
        for i in range(len(names)):
            w_ref, m_ref, v_ref = ins[3 * i:3 * i + 3]
            g = conv_ref[...] if i == len(_VECTORS) else sum_ref[i:i + 1, 0:w_ref.shape[1]]
            outs[4 * i][...] = g
            outs[4 * i + 1][...], outs[4 * i + 2][...], outs[4 * i + 3][...] = _adam_update(g, w_ref[...], m_ref[...], v_ref[...])

    out_shape = [jax.ShapeDtypeStruct(t[0].shape, F32) for t in wmv for _ in range(4)]
    res = pl.pallas_call(body, name="adamw_vectors", out_shape=out_shape)(summed, conv_g, *flat)
    return {n: res[4 * i:4 * i + 4] for i, n in enumerate(names)}


_ORDER = ("norm_mix", "w_in", "conv_w", "a_log", "dt_bias", "dn_norm", "sinks", "w_o", "norm_mlp", "w_up", "w_down",
          "norm_ple", "w_ple_gate", "w_ple_proj", "norm_final")


def kernel(x, p, norm_mix, w_in, conv_w, a_log, dt_bias, dn_norm, sinks, w_o, norm_mlp, w_up, w_down, norm_ple, w_ple_gate, w_ple_proj, norm_final, loss_target, m_norm_mix, m_w_in, m_conv_w, m_a_log, m_dt_bias, m_dn_norm, m_sinks, m_w_o, m_norm_mlp, m_w_up, m_w_down, m_norm_ple, m_w_ple_gate, m_w_ple_proj, m_norm_final, v_norm_mix, v_w_in, v_conv_w, v_a_log, v_dt_bias, v_dn_norm, v_sinks, v_w_o, v_norm_mlp, v_w_up, v_w_down, v_norm_ple, v_w_ple_gate, v_w_ple_proj, v_norm_final):
    w = dict(norm_mix=norm_mix, w_in=w_in, conv_w=conv_w[0], a_log=a_log, dt_bias=dt_bias, dn_norm=dn_norm, sinks=sinks,
             w_o=w_o[0], norm_mlp=norm_mlp, w_up=w_up[0], w_down=w_down[0], norm_ple=norm_ple, w_ple_gate=w_ple_gate[0],
             w_ple_proj=w_ple_proj[0], norm_final=norm_final)
    m = dict(norm_mix=m_norm_mix, w_in=m_w_in, conv_w=m_conv_w[0], a_log=m_a_log, dt_bias=m_dt_bias, dn_norm=m_dn_norm,
             sinks=m_sinks, w_o=m_w_o[0], norm_mlp=m_norm_mlp, w_up=m_w_up[0], w_down=m_w_down[0], norm_ple=m_norm_ple,
             w_ple_gate=m_w_ple_gate[0], w_ple_proj=m_w_ple_proj[0], norm_final=m_norm_final)
    v = dict(norm_mix=v_norm_mix, w_in=v_w_in, conv_w=v_conv_w[0], a_log=v_a_log, dt_bias=v_dt_bias, dn_norm=v_dn_norm,
             sinks=v_sinks, w_o=v_w_o[0], norm_mlp=v_norm_mlp, w_up=v_w_up[0], w_down=v_w_down[0], norm_ple=v_norm_ple,
             w_ple_gate=v_w_ple_gate[0], w_ple_proj=v_w_ple_proj[0], norm_final=v_norm_final)
    me = 4 * lax.axis_index("x") + 2 * lax.axis_index("y") + lax.axis_index("c")
    conv_shard = conv_w.shape[2]

    for d in (w, m, v):
        d["w_in"] = jnp.transpose(d["w_in"], (2, 0, 1))
    conv_pad = jnp.pad(w["conv_w"], ((0, 8 - DN_CONV), (0, 256 - conv_shard)))
    shards = _cast_all([w[n] for n in ("w_in",) + _OTHERS])
    gathers, token_gather = _exchange_start([shards[0], conv_pad] + list(shards[1:]), "gather_start",
                                            ["chips", "chips"] + [_OTHER_MODES[n] for n in _OTHERS])
    vectors = dict(w)
    vectors["norm_mix"] = w["norm_mix"] + token_gather[0:1, 0:1]

    def first_weights(after):
        over_ici = _exchange_wait(gathers, after, "gather_first_wait", [0, 1])
        handle, token = _forward_start(over_ici, "gather_first_forward")
        w_in_all, conv_all = _forward_wait(handle, token, "gather_first_forward_wait")
        conv_all = jnp.transpose(conv_all[:, :DN_CONV, :conv_shard], (1, 0, 2)).reshape(DN_CONV, N_DEV * conv_shard)
        return _w_in_to_internal(w_in_all.reshape(D_IN, 1, _HALF)), conv_all

    as_taken = {"w_o": lambda t: t.reshape(1024, 1024), "w_up": lambda t: t, "w_down": lambda t: t.reshape(4096, 1024),
                "w_ple_gate": lambda t: t.reshape(1024, 1024), "w_ple_proj": lambda t: t}

    def other_weights(names, after):
        which = [2 + _OTHERS.index(n) for n in names]
        got = _exchange_wait(gathers, after, "gather_wait_" + names[0], which)
        return [as_taken[n](t) for n, t in zip(names, got)]

    shipped = []

    def ship_early(pieces):
        names = tuple(pieces)
        if names == ("w_in",):
            pieces = {"w_in": _w_in_from_internal(pieces["w_in"]).reshape(N_DEV, D_IN // N_DEV, 1, _HALF)}
        handle, token = _exchange_start([pieces[n] for n in names], "scatter_start_" + names[0], "pieces")
        shipped.append((names, handle))
        return token

    loss, grad_x, g = _local_step(x[0], p[0, 0], loss_target[0], vectors, first_weights, other_weights, ship_early)

    row = lambda t: t.reshape(1, t.size)
    small = _pack_small([row(g[n]) for n in _VECTORS], loss, g["conv_w"].reshape(6, 1024))
    small_handle, token_small = _exchange_start([small], "gather_small_start", "slots")
    big, after = {}, token_small
    for names, handle in shipped[:-1]:
        for n, r in zip(names, _exchange_wait(handle, after, "scatter_wait_" + names[0])):
            big[n] = _adamw(r, w[n], m[n], v[n], "adamw_" + n)
            after = big[n][1]
    small_all, = _exchange_wait(small_handle, after, "gather_small_wait")
    summed = _sum_slots(small_all)
    conv_g = lax.dynamic_slice(summed[_CONV_ROW:_CONV_ROW + 6].reshape(DN_CONV, N_DEV * conv_shard), (0, me * conv_shard),
                               (DN_CONV, conv_shard))
    small_out = _adamw_vectors(summed, conv_g, [(row(w[n]), row(m[n]), row(v[n])) for n in _VECTORS]
                               + [(w["conv_w"], m["conv_w"], v["conv_w"])])
    names, handle = shipped[-1]
    for n, r in zip(names, _exchange_wait(handle, small_out["conv_w"][0], "scatter_wait_" + names[0])):
        big[n] = _adamw(r, w[n], m[n], v[n], "adamw_" + n)

    result = [summed[_LOSS_ROW, 0], grad_x[None]]
    for i in range(4):
        for n in _ORDER:
            if n == "w_in":
                result.append(jnp.transpose(big[n][i], (1, 2, 0)))
            elif n in _MATRICES:
                result.append(big[n][i][None])
            elif n == "conv_w":
                result.append(small_out[n][i][None])
            else:
                result.append(small_out[n][i].reshape(w[n].shape))
    return tuple(result)
```

```python
import jax
import jax.numpy as jnp
import numpy as np
from jax import lax
from jax.experimental import pallas as pl
from jax.experimental.pallas import tpu as pltpu

F32, BF16 = jnp.float32, jnp.bfloat16
EPS = 1e-6
D_MODEL = 1024
N_DEV = 8
ATTN_BLOCK = 128
HEAD_PAIR = 128
DN_HEADS = 4
DN_DIM = 128
DN_CHUNK = 64
DN_CONV = 4
ROPE_THETA = 10000.0
D_IN = 2824
D_IN_PAD = 3072
BLK_Q, BLK_Z = 0, 1
BLK_DN, BLK_K, BLK_V, BLK_G = 8, 20, 21, 22
BLK_G_PAD = 11
VMEM_LIMIT = 56 * 1024 * 1024
NEG = -1e30
ADAM_LR, ADAM_B1, ADAM_B2, ADAM_EPS, ADAM_WD, ADAM_STEP = 0.001, 0.9, 0.999, 1e-08, 0.01, 10
MESH = pl.DeviceIdType.MESH


def _bf(x):
    return x.astype(BF16)


def _dot(a, b):
    return jnp.dot(a, b, preferred_element_type=F32)


def _dot_nt(a, b):
    return lax.dot_general(a, b, (((1,), (1,)), ((), ())), preferred_element_type=F32)


def _dot_tn(a, b):
    return lax.dot_general(a, b, (((0,), (0,)), ((), ())), preferred_element_type=F32)


def _sigmoid(x):
    return 1.0 / (1.0 + jnp.exp(-x))


def _params(sem):
    return pltpu.CompilerParams(dimension_semantics=sem, vmem_limit_bytes=VMEM_LIMIT)


def _mm(x, w, *, form, name, out_dtypes, tn, epi=None, extra=(), tm=512, w_row_block=0, after=None, norm=None,
        norm_bwd=None, then_nt=None):
    assert norm is None or norm_bwd is None
    xs = list(x) if isinstance(x, (list, tuple)) else [x]
    nx = len(xs)
    S, K = xs[0].shape
    shards = w.ndim == 3
    N = (w.shape[2] * N_DEV if shards else w.shape[1]) if form == "nn" else w.shape[-2]
    assert not (shards and form == "nn" and tn != w.shape[2]) and (nx == 1 or (form == "nn" and not shards and norm is None))
    r0 = w_row_block * K
    tm = min(tm, S)
    n_extra, n_out = len(extra), len(out_dtypes)
    tile = lambda width: pl.BlockSpec((tm, width), lambda i: (i, 0))
    whole = lambda a: pl.BlockSpec(a.shape, lambda i, nd=a.ndim: (0,) * nd)
    ins, in_specs = [*xs, w, *extra], [tile(K)] * nx + [whole(w)] + [tile(N)] * n_extra
    if norm is not None:
        ins, in_specs = ins + [norm], in_specs + [whole(norm)]
    if norm_bwd is not None:
        ins, in_specs = ins + list(norm_bwd), in_specs + [tile(N), whole(norm_bwd[1]), tile(N)]
    if then_nt is not None:
        ins, in_specs = ins + [then_nt], in_specs + [whole(then_nt)]
    if after is not None:
        ins, in_specs = ins + [after], in_specs + [whole(after)]
    out_shape = [jax.ShapeDtypeStruct((S, N), dt) for dt in out_dtypes]
    out_specs = [tile(N)] * n_out
    if norm is not None:
        out_shape, out_specs = out_shape + [jax.ShapeDtypeStruct((S, K), BF16)], out_specs + [tile(K)]
    if norm_bwd is not None:
        out_shape, out_specs = out_shape + [jax.ShapeDtypeStruct((1, N), F32)], out_specs + [pl.BlockSpec((1, N), lambda i: (0, 0))]
    if then_nt is not None:
        out_shape, out_specs = out_shape + [jax.ShapeDtypeStruct((S, then_nt.shape[0]), F32)], out_specs + [tile(then_nt.shape[0])]

    def product(xb, w_ref, cols, c):
        if form == "nn" and nx > 1:
            return sum(_dot(part, w_ref[r0 + p * K:r0 + (p + 1) * K, cols]) for p, part in enumerate(xb))
        if form == "nn":
            return _dot(xb, w_ref[c] if shards else w_ref[r0:r0 + K, cols])
        if not shards:
            return _dot_nt(xb, w_ref[cols, :])
        ks = w.shape[2]
        acc = _dot_nt(xb[:, 0:ks], w_ref[0, cols, :])
        for s in range(1, N_DEV):
            acc = acc + _dot_nt(xb[:, s * ks:(s + 1) * ks], w_ref[s, cols, :])
        return acc

    def body(*refs):
        x_ref, w_ref = refs[0], refs[nx]
        extra_refs = refs[nx + 1:nx + 1 + n_extra]
        at = nx + 1 + n_extra
        if norm is not None:
            gain_ref, at = refs[at], at + 1
        if norm_bwd is not None:
            (y_ref, ygain_ref, dres_ref), at = refs[at:at + 3], at + 3
        if then_nt is not None:
            w2_ref, at = refs[at], at + 1
        outs = refs[len(ins):]
        if norm is not None:
            _, xh = _rms_stats(x_ref[...])
            xb = _bf(xh * gain_ref[...])
            outs[n_out][...] = xb
        else:
            xb = _bf(x_ref[...]) if nx == 1 else [_bf(r[...]) for r in refs[:nx]]
        for c in range(N // tn):
            cols = slice(c * tn, (c + 1) * tn)
            acc = product(xb, w_ref, cols, c)
            res = epi(acc, *[r[:, cols] for r in extra_refs]) if epi else (acc,)
            for o, r in zip(outs[:n_out], res):
                o[:, cols] = r.astype(o.dtype)
        if norm_bwd is not None:
            dx, dg = _rms_bwd_tile(y_ref[...], ygain_ref[...], outs[0][...])
            outs[0][...] = dres_ref[...] + dx
            dg_ref = outs[n_out]

            @pl.when(pl.program_id(0) == 0)
            def _():
                dg_ref[...] = jnp.zeros_like(dg_ref)

            dg_ref[...] += dg
        if then_nt is not None:
            yb = _bf(outs[0][...])
            for c in range(then_nt.shape[0] // tn):
                cols = slice(c * tn, (c + 1) * tn)
                outs[-1][:, cols] = _dot_nt(yb, w2_ref[cols, :])

    return pl.pallas_call(
        body, grid=(S // tm,), name=name, in_specs=in_specs, out_specs=out_specs, out_shape=out_shape,
        compiler_params=_params(("arbitrary",) if norm_bwd is not None else ("parallel",)),
    )(*ins)


def _mlp_fwd(h1, w_up, w_down, gain):
    S, K = h1.shape
    n_sh, _, fs = w_up.shape
    tm = min(512, S)

    def body(x_ref, wup_ref, wdown_ref, g_ref, hid_ref, relu_ref, m_ref, h2_ref):
        x = x_ref[...]
        _, xh = _rms_stats(x)
        mb = _bf(xh * g_ref[...])
        m_ref[...] = mb
        h2_ref[...] = x
        for c in range(n_sh):
            cols = slice(c * fs, (c + 1) * fs)
            r = jnp.maximum(_dot(mb, wup_ref[c]), 0.0)
            hd = _bf(r * r)
            hid_ref[:, cols] = hd
            relu_ref[:, cols] = _bf(r)
            h2_ref[...] += _dot(hd, wdown_ref[cols, :])

    tile = lambda width: pl.BlockSpec((tm, width), lambda i: (i, 0))
    once = lambda a: pl.BlockSpec(a.shape, lambda i, nd=a.ndim: (0,) * nd, pipeline_mode=pl.Buffered(1))
    F = n_sh * fs
    return pl.pallas_call(
        body, grid=(S // tm,), name="mlp_fwd",
        in_specs=[tile(K), once(w_up), once(w_down), pl.BlockSpec(gain.shape, lambda i: (0, 0))],
        out_specs=[tile(F), tile(F), tile(K), tile(K)],
        out_shape=[jax.ShapeDtypeStruct((S, F), BF16), jax.ShapeDtypeStruct((S, F), BF16),
                   jax.ShapeDtypeStruct((S, K), BF16), jax.ShapeDtypeStruct((S, K), F32)],
        compiler_params=_params(("parallel",)),
    )(h1, w_up, w_down, gain)


def _mm_tn(x, dy, *, name, tm, tn, out_dtype=F32, column_shards=False, after=None):
    S, K = x.shape
    N = dy.shape[1]
    waits = [] if after is None else [after]

    def body(x_ref, dy_ref, *rest):
        rest[-1][...] = _dot_tn(_bf(x_ref[...]), _bf(dy_ref[...])).astype(out_dtype)

    if column_shards:
        out_spec = pl.BlockSpec((None, tm, tn), lambda i, j: (j, i, 0))
        out_shape = jax.ShapeDtypeStruct((N // tn, K, tn), out_dtype)
    else:
        out_spec = pl.BlockSpec((tm, tn), lambda i, j: (i, j))
        out_shape = jax.ShapeDtypeStruct((K, N), out_dtype)
    return pl.pallas_call(
        body, grid=(K // tm, N // tn), name=name,
        in_specs=[pl.BlockSpec((S, tm), lambda i, j: (0, i)), pl.BlockSpec((S, tn), lambda i, j: (0, j))]
        + [pl.BlockSpec(memory_space=pl.ANY)] * len(waits),
        out_specs=out_spec, out_shape=out_shape,
        compiler_params=_params(("parallel", "parallel")),
    )(x, dy, *waits)


def _rowwise(body, *, tiled, full, out_tiled, out_acc, name, tm=512, smem=()):
    S = tiled[0].shape[0]
    tm = min(tm, S)
    n_in = len(smem) + len(tiled) + len(full)

    def kern(*refs):
        @pl.when(pl.program_id(0) == 0)
        def _():
            for r in refs[n_in + len(out_tiled):]:
                r[...] = jnp.zeros_like(r)
        body(*refs)

    in_specs = [pl.BlockSpec(memory_space=pltpu.SMEM) for _ in smem]
    in_specs += [pl.BlockSpec((tm, a.shape[1]), lambda i: (i, 0)) for a in tiled]
    in_specs += [pl.BlockSpec(a.shape, lambda i, nd=a.ndim: (0,) * nd) for a in full]
    out_specs = [pl.BlockSpec((tm, w), lambda i: (i, 0)) for w, _ in out_tiled]
    out_specs += [pl.BlockSpec(shp, lambda i, nd=len(shp): (0,) * nd) for shp, _ in out_acc]
    out_shape = [jax.ShapeDtypeStruct((S, w), dt) for w, dt in out_tiled]
    out_shape += [jax.ShapeDtypeStruct(shp, dt) for shp, dt in out_acc]
    return pl.pallas_call(
        kern, grid=(S // tm,), name=name, in_specs=in_specs, out_specs=out_specs, out_shape=out_shape,
        compiler_params=_params(("arbitrary",)),
    )(*smem, *tiled, *full)


def _rms_stats(x):
    r = lax.rsqrt(jnp.mean(x * x, axis=-1, keepdims=True) + EPS)
    return r, x * r


def _rmsnorm_fwd(x, g, name):
    def body(x_ref, g_ref, o_ref):
        _, xh = _rms_stats(x_ref[...])
        o_ref[...] = _bf(xh * g_ref[...])

    return _rowwise(body, tiled=[x], full=[g], out_tiled=[(x.shape[1], BF16)], out_acc=[], name=name)[0]


def _rms_bwd_tile(x, g, dxn):
    r, xh = _rms_stats(x)
    dg = jnp.sum(dxn * xh, axis=0, keepdims=True)
    dn = dxn * g
    dx = r * (dn - xh * jnp.mean(dn * xh, axis=-1, keepdims=True))
    return dx, dg


def _ple_and_loss(h2, p, target, w_pg, w_pp, g_ple, g_final):
    S, n = h2.shape
    tm = min(512, S)
    tn = 512

    def body(h2_ref, p_ref, t_ref, wpg_ref, wpp_ref, gple_ref, gfin_ref,
             n3_ref, dh_ref, dgl_ref, dpp_ref, loss_ref, dg_ref, dgple_ref, pp, gate, h3):
        @pl.when(pl.program_id(0) == 0)
        def _():
            loss_ref[...] = jnp.zeros_like(loss_ref)
            dg_ref[...] = jnp.zeros_like(dg_ref)
            dgple_ref[...] = jnp.zeros_like(dgple_ref)

        x = h2_ref[...]
        _, xh = _rms_stats(x)
        n3 = _bf(xh * gple_ref[...])
        n3_ref[...] = n3
        pb = _bf(p_ref[...])
        for c in range(n // tn):
            cols = slice(c * tn, (c + 1) * tn)
            pp[:, cols] = _dot(pb, wpp_ref[:, cols])
            gt = _sigmoid(_dot(n3, wpg_ref[:, cols]))
            gate[:, cols] = gt
            h3[:, cols] = x[:, cols] + gt * pp[:, cols]
        y = h3[...]
        _, yh = _rms_stats(y)
        e = yh * gfin_ref[...] - t_ref[...]
        per_tok = jnp.mean(e * e, axis=-1, keepdims=True)
        loss_ref[...] += 0.5 * jnp.sum(per_tok, axis=0, keepdims=True)
        dh, dg = _rms_bwd_tile(y, gfin_ref[...], e * (1.0 / n))
        dg_ref[...] += dg
        gt = gate[...]
        dgl = _bf(dh * pp[...] * gt * (1.0 - gt))
        dgl_ref[...] = dgl
        dpp_ref[...] = _bf(dh * gt)
        for c in range(n // tn):
            cols = slice(c * tn, (c + 1) * tn)
            h3[:, cols] = _dot_nt(dgl, wpg_ref[cols, :])
        dx, dgp = _rms_bwd_tile(x, gple_ref[...], h3[...])
        dh_ref[...] = dh + dx
        dgple_ref[...] += dgp

    tile = lambda width: pl.BlockSpec((tm, width), lambda i: (i, 0))
    whole = lambda a: pl.BlockSpec(a.shape, lambda i, nd=a.ndim: (0,) * nd)
    return pl.pallas_call(
        body, grid=(S // tm,), name="ple_and_loss",
        in_specs=[tile(n), tile(p.shape[1]), tile(n), whole(w_pg), whole(w_pp), whole(g_ple), whole(g_final)],
        out_specs=[tile(n), tile(n), tile(n), tile(n), pl.BlockSpec((1, 128), lambda i: (0, 0)),
                   pl.BlockSpec((1, n), lambda i: (0, 0)), pl.BlockSpec((1, n), lambda i: (0, 0))],
        out_shape=[jax.ShapeDtypeStruct((S, n), BF16), jax.ShapeDtypeStruct((S, n), F32), jax.ShapeDtypeStruct((S, n), BF16),
                   jax.ShapeDtypeStruct((S, n), BF16), jax.ShapeDtypeStruct((1, 128), F32), jax.ShapeDtypeStruct((1, n), F32),
                   jax.ShapeDtypeStruct((1, n), F32)],
        scratch_shapes=[pltpu.VMEM((tm, n), F32)] * 3,
        compiler_params=_params(("arbitrary",)),
    )(h2, p, target, w_pg, w_pp, g_ple, g_final)


def _rope_tables(S):
    half = 32
    inv = (1.0 / (np.float32(ROPE_THETA) ** (np.arange(half, dtype=np.float32) * np.float32(2.0 / 64)))).astype(np.float32)
    ang = np.arange(S).astype(np.float32)[:, None] * inv[None, :]
    cos, sin = np.cos(ang), np.sin(ang)
    return jnp.asarray(np.tile(cos, (1, 4))), jnp.asarray(np.concatenate([-sin, sin, -sin, sin], axis=1))


def _attn_common(i, kc, kp, vc, vp, cc, sc, cp, sp):
    lane = lax.broadcasted_iota(jnp.int32, (1, HEAD_PAIR), 1)
    lane_lo = jnp.bitwise_and(lane, 63) < 32
    slot = [lane < 64, lane >= 64]

    def swap_halves(t):
        return jnp.where(lane_lo, pltpu.roll(t, 96, 1), pltpu.roll(t, 32, 1))

    def rope(t, cos, sin):
        return t * cos + swap_halves(t) * sin

    def unrope(d, cos, sin):
        return d * cos + swap_halves(d * sin)

    k2 = jnp.concatenate([rope(kp, cp, sp), rope(kc, cc, sc)], axis=0)
    v2 = jnp.concatenate([vp, vc], axis=0)
    r = lax.broadcasted_iota(jnp.int32, (ATTN_BLOCK, 2 * ATTN_BLOCK), 0)
    c = lax.broadcasted_iota(jnp.int32, (ATTN_BLOCK, 2 * ATTN_BLOCK), 1)
    valid = (c > r) & (c <= r + ATTN_BLOCK) & jnp.logical_or(c >= ATTN_BLOCK, i > 0)
    ks, vs = {}, {}
    for j in range(2):
        kn = jnp.where(slot[j], k2, 0.0)
        vn = jnp.where(slot[j], v2, 0.0)
        for s in range(2):
            ks[j, s] = _bf(kn if s == j else pltpu.roll(kn, 64, 1))
            vs[j, s] = _bf(vn if s == j else pltpu.roll(vn, 64, 1))
    return slot, rope, unrope, valid, ks, vs


def _attn_probs(scores, valid, sink):
    s = jnp.where(valid, scores * 0.125, NEG)
    m = jnp.maximum(jnp.max(s, axis=1, keepdims=True), sink)
    e = jnp.exp(s - m)
    z = jnp.sum(e, axis=1, keepdims=True) + jnp.exp(sink - m)
    return e * (1.0 / z), m + jnp.log(z)


def _attn_specs(S):
    nb = S // ATTN_BLOCK
    prev = lambda i: jnp.maximum(i - 1, 0)
    blk = lambda w, col, row=(lambda i: i): pl.BlockSpec((ATTN_BLOCK, w), lambda i: (row(i), col))
    in_specs = [pl.BlockSpec(memory_space=pltpu.SMEM),
                blk(512, BLK_Q), blk(128, BLK_K), blk(128, BLK_K, prev), blk(128, BLK_V), blk(128, BLK_V, prev),
                blk(128, 0), blk(128, 0), blk(128, 0, prev), blk(128, 0, prev)]
    return nb, in_specs


def _attn_fwd(pa, cos, sin, sinks):
    S = pa.shape[0]
    nb, in_specs = _attn_specs(S)

    def body(sinks_ref, q_ref, kc_ref, kp_ref, vc_ref, vp_ref, cc_ref, sc_ref, cp_ref, sp_ref, o_ref, lse_ref):
        i = pl.program_id(0)
        lane = lax.broadcasted_iota(jnp.int32, (1, HEAD_PAIR), 1)
        cc, sc = cc_ref[...], sc_ref[...]
        _, rope, _, valid, ks, vs = _attn_common(i, kc_ref[...], kp_ref[...], vc_ref[...], vp_ref[...],
                                                 cc, sc, cp_ref[...], sp_ref[...])
        pair_cols = [slice(HEAD_PAIR * pair, HEAD_PAIR * (pair + 1)) for pair in range(4)]
        qps = [_bf(rope(q_ref[:, cols], cc, sc)) for cols in pair_cols]
        outs, lses = {}, {}

        def head_program(h):
            pair, s = divmod(h, 2)
            j = h // 4
            scores = _dot_nt(qps[pair], ks[j, s])
            yield
            p, lse = _attn_probs(scores, valid, sinks_ref[h])
            outs[h] = _dot(_bf(p), vs[j, s])
            lses[h] = jnp.where(lane == h, lse, 0.0)

        _interleave(head_program(h) for h in range(8))
        for pair, cols in enumerate(pair_cols):
            o_ref[:, cols] = outs[2 * pair] + outs[2 * pair + 1]
        lse_ref[...] = sum((lses[h] for h in range(1, 8)), lses[0])

    return pl.pallas_call(
        body, grid=(nb,), name="attn_fwd", in_specs=in_specs,
        out_specs=[pl.BlockSpec((ATTN_BLOCK, 512), lambda i: (i, 0)), pl.BlockSpec((ATTN_BLOCK, 128), lambda i: (i, 0))],
        out_shape=[jax.ShapeDtypeStruct((S, 512), F32), jax.ShapeDtypeStruct((S, 128), F32)],
        compiler_params=_params(("parallel",)),
    )(sinks, pa, pa, pa, pa, pa, cos, sin, cos, sin)


def _attn_bwd(pa, cos, sin, sinks, dcat, attn, lse):
    S = pa.shape[0]
    nb, in_specs = _attn_specs(S)
    in_specs = in_specs + [pl.BlockSpec((ATTN_BLOCK, 512), lambda i: (i, 0))] * 2 + [pl.BlockSpec((ATTN_BLOCK, 128), lambda i: (i, 0))]

    def body(sinks_ref, q_ref, kc_ref, kp_ref, vc_ref, vp_ref, cc_ref, sc_ref, cp_ref, sp_ref, do_ref, o_ref, lse_ref,
             dq_ref, dk_ref, dv_ref, dsink_ref):
        i = pl.program_id(0)

        @pl.when(i == 0)
        def _():
            dk_ref[...] = jnp.zeros_like(dk_ref)
            dv_ref[...] = jnp.zeros_like(dv_ref)
            dsink_ref[...] = jnp.zeros_like(dsink_ref)

        cc, sc, cp, sp = cc_ref[...], sc_ref[...], cp_ref[...], sp_ref[...]
        slot, rope, unrope, valid, ks, vs = _attn_common(i, kc_ref[...], kp_ref[...], vc_ref[...], vp_ref[...], cc, sc, cp, sp)
        pair_cols = [slice(HEAD_PAIR * pair, HEAD_PAIR * (pair + 1)) for pair in range(4)]
        qps = [_bf(rope(q_ref[:, cols], cc, sc)) for cols in pair_cols]
        dobs = [_bf(do_ref[:, cols]) for cols in pair_cols]
        do_o = [do_ref[:, cols] * o_ref[:, cols] for cols in pair_cols]
        dqs, dks, dvs = {}, {}, {}

        def head_program(h):
            pair, s = divmod(h, 2)
            j = h // 4
            qp, dob = qps[pair], dobs[pair]
            scores = _dot_nt(qp, ks[j, s])
            dp = _dot_nt(dob, vs[j, s])
            yield
            lse_h = lse_ref[:, h:h + 1]
            p = jnp.exp(jnp.where(valid, scores * 0.125, NEG) - lse_h)
            yield
            dr = jnp.sum(jnp.where(slot[s], do_o[pair], 0.0), axis=1, keepdims=True)
            ds = _bf(p * (dp - dr) * 0.125)
            yield
            dsink_ref[h:h + 1, :] += -jnp.sum(jnp.exp(sinks_ref[h] - lse_h) * dr, axis=0, keepdims=True)
            dqs[h] = _dot(ds, ks[j, s])
            dk_h = _dot_tn(ds, qp)
            dv_h = _dot_tn(_bf(p), dob)
            yield
            dk_h, dv_h = jnp.where(slot[s], dk_h, 0.0), jnp.where(slot[s], dv_h, 0.0)
            if s != j:
                dk_h, dv_h = pltpu.roll(dk_h, 64, 1), pltpu.roll(dv_h, 64, 1)
            dks[h], dvs[h] = dk_h, dv_h

        _interleave(head_program(h) for h in range(8))
        dk2 = sum((dks[h] for h in range(1, 8)), dks[0])
        dv2 = sum((dvs[h] for h in range(1, 8)), dvs[0])
        for pair, cols in enumerate(pair_cols):
            dq_ref[:, cols] = _bf(unrope(dqs[2 * pair] + dqs[2 * pair + 1], cc, sc))
        cur = pl.ds(pl.multiple_of(i * ATTN_BLOCK, ATTN_BLOCK), ATTN_BLOCK)
        dk_ref[cur, :] += unrope(dk2[ATTN_BLOCK:], cc, sc)
        dv_ref[cur, :] += dv2[ATTN_BLOCK:]

        @pl.when(i > 0)
        def _():
            prv = pl.ds(pl.multiple_of((i - 1) * ATTN_BLOCK, ATTN_BLOCK), ATTN_BLOCK)
            dk_ref[prv, :] += unrope(dk2[:ATTN_BLOCK], cp, sp)
            dv_ref[prv, :] += dv2[:ATTN_BLOCK]

    whole = lambda w: pl.BlockSpec((S, w), lambda i: (0, 0))
    return pl.pallas_call(
        body, grid=(nb,), name="attn_bwd", in_specs=in_specs,
        out_specs=[pl.BlockSpec((ATTN_BLOCK, 512), lambda i: (i, BLK_Q)), whole(128), whole(128),
                   pl.BlockSpec((8, 128), lambda i: (0, 0))],
        out_shape=[jax.ShapeDtypeStruct((S, D_IN_PAD), BF16), jax.ShapeDtypeStruct((S, 128), F32),
                   jax.ShapeDtypeStruct((S, 128), F32), jax.ShapeDtypeStruct((8, 128), F32)],
        compiler_params=_params(("arbitrary",)),
    )(sinks, pa, pa, pa, pa, pa, cos, sin, cos, sin, dcat, attn, lse)


CONV_ROWS = 512
CONV_PAD = 8


def _conv_silu(scr, w, r0):
    y = w[3:4, :] * scr[pl.ds(CONV_PAD + r0, CONV_ROWS), :]
    for j in range(DN_CONV - 1):
        y = y + w[j:j + 1, :] * scr[pl.ds(CONV_PAD + r0 - 3 + j, CONV_ROWS), :]
    return y


def _dn_prep_fwd(pd, conv_w):
    S = pd.shape[0]
    assert S % CONV_ROWS == 0

    def body(x_ref, w_ref, o_ref, scr):
        b = pl.program_id(0)
        scr[0:CONV_PAD, :] = jnp.zeros((CONV_PAD, DN_DIM), F32)
        scr[pl.ds(CONV_PAD, S), :] = x_ref[...]
        w = w_ref[...]
        q_scale = jnp.where(b < DN_HEADS, DN_DIM ** -0.5, 1.0)
        for r0 in range(0, S, CONV_ROWS):
            y = _conv_silu(scr, w, r0)
            a = y * _sigmoid(y)
            rs = lax.rsqrt(jnp.sum(a * a, axis=1, keepdims=True) + EPS)
            o_ref[pl.ds(r0, CONV_ROWS), :] = a * jnp.where(b < 2 * DN_HEADS, rs * q_scale, 1.0)

    col = pl.BlockSpec((S, DN_DIM), lambda b: (0, b))
    return pl.pallas_call(
        body, grid=(3 * DN_HEADS,), name="dn_prep_fwd",
        in_specs=[pl.BlockSpec((S, DN_DIM), lambda b: (0, BLK_DN + b)), pl.BlockSpec((DN_CONV, DN_DIM), lambda b: (0, b))],
        out_specs=col,
        out_shape=jax.ShapeDtypeStruct((S, 3 * DN_HEADS * DN_DIM), F32),
        scratch_shapes=[pltpu.VMEM((S + CONV_PAD, DN_DIM), F32)],
        compiler_params=_params(("parallel",)),
    )(pd, conv_w)


def _dn_prep_bwd(pd, conv_w, dqkv, dproj, dk, dv):
    S = pd.shape[0]
    NB = 3 * DN_HEADS

    def body(x_ref, w_ref, d_ref, _, dk_ref, dv_ref, dx_ref, dw_ref, scr, dscr):
        b = pl.program_id(0)

        @pl.when(b == NB)
        def _():
            dx_ref[...] = _bf(dk_ref[...])

        @pl.when(b == NB + 1)
        def _():
            dx_ref[...] = _bf(dv_ref[...])

        @pl.when(b < NB)
        def _():
            scr[0:CONV_PAD, :] = jnp.zeros((CONV_PAD, DN_DIM), F32)
            scr[pl.ds(CONV_PAD, S), :] = x_ref[...]
            dscr[pl.ds(S, CONV_PAD), :] = jnp.zeros((CONV_PAD, DN_DIM), F32)
            w = w_ref[...]
            q_scale = jnp.where(b < DN_HEADS, DN_DIM ** -0.5, 1.0)
            is_qk = b < 2 * DN_HEADS
            dw = [jnp.zeros((1, DN_DIM), F32) for _ in range(DN_CONV)]
            for r0 in range(0, S, CONV_ROWS):
                y = _conv_silu(scr, w, r0)
                sg = _sigmoid(y)
                a = y * sg
                dout = d_ref[pl.ds(r0, CONV_ROWS), :]
                rs = lax.rsqrt(jnp.sum(a * a, axis=1, keepdims=True) + EPS)
                da_qk = q_scale * rs * (dout - a * (rs * rs) * jnp.sum(dout * a, axis=1, keepdims=True))
                dy = jnp.where(is_qk, da_qk, dout) * (sg * (1.0 + y * (1.0 - sg)))
                dscr[pl.ds(r0, CONV_ROWS), :] = dy
                for j in range(DN_CONV):
                    dw[j] = dw[j] + jnp.sum(dy * scr[pl.ds(CONV_PAD + r0 - 3 + j, CONV_ROWS), :], axis=0, keepdims=True)
            for j in range(DN_CONV):
                dw_ref[j:j + 1, :] = dw[j]
            for r0 in range(0, S, CONV_ROWS):
                dx = w[3:4, :] * dscr[pl.ds(r0, CONV_ROWS), :]
                for j in range(DN_CONV - 1):
                    dx = dx + w[j:j + 1, :] * dscr[pl.ds(r0 + 3 - j, CONV_ROWS), :]
                dx_ref[pl.ds(r0, CONV_ROWS), :] = _bf(dx)

    own = lambda b: jnp.minimum(b, NB - 1)
    col = pl.BlockSpec((S, DN_DIM), lambda b: (0, own(b)))
    proj_col = pl.BlockSpec((S, DN_DIM), lambda b: (0, BLK_DN + own(b)))
    wcol = pl.BlockSpec((DN_CONV, DN_DIM), lambda b: (0, own(b)))
    whole = pl.BlockSpec((S, DN_DIM), lambda b: (0, 0))
    assert BLK_K == BLK_DN + NB and BLK_V == BLK_K + 1
    return pl.pallas_call(
        body, grid=(NB + 2,), name="dn_prep_bwd",
        in_specs=[proj_col, wcol, col, pl.BlockSpec(memory_space=pl.ANY), whole, whole],
        out_specs=[pl.BlockSpec((S, DN_DIM), lambda b: (0, BLK_DN + b)), wcol],
        out_shape=[jax.ShapeDtypeStruct(dproj.shape, dproj.dtype), jax.ShapeDtypeStruct((DN_CONV, 3 * DN_HEADS * DN_DIM), F32)],
        scratch_shapes=[pltpu.VMEM((S + CONV_PAD, DN_DIM), F32), pltpu.VMEM((S + CONV_PAD, DN_DIM), F32)],
        input_output_aliases={3: 0},
        compiler_params=_params(("arbitrary",)),
    )(pd, conv_w, dqkv, dproj, dk, dv)


CPAD = 128
CHUNKS_LOCAL = 4
CHUNKS_SCAN = 8


def _chunk_masks():
    ii = lax.broadcasted_iota(jnp.int32, (DN_CHUNK, CPAD), 0)
    jj = lax.broadcasted_iota(jnp.int32, (DN_CHUNK, CPAD), 1)
    return ii, jj


def _rows_pad(a):
    return jnp.concatenate([a, jnp.zeros_like(a)], axis=0)


def _hi_lo(a):
    hi = _bf(a)
    return hi, _bf(a - hi.astype(F32))


def _double_step(t, p):
    C = DN_CHUNK
    th, tl = _hi_lo(t)
    ph, pl_ = _hi_lo(p)
    r1 = _dot(jnp.concatenate([th, tl, ph, pl_], axis=0), _rows_pad(ph))
    r2 = _dot(jnp.concatenate([th, ph], axis=0), _rows_pad(pl_))
    return t + (r1[:C] + r1[C:2 * C] + r2[:C]), r1[2 * C:3 * C] + r1[3 * C:] + r2[C:]


def _dot3_nt(a, b):
    C = DN_CHUNK
    ah, al = _hi_lo(a)
    bh, bl = _hi_lo(b)
    r1 = _dot_nt(jnp.concatenate([ah, al], axis=0), _rows_pad(bh))
    return r1[:C] + r1[C:] + _dot_nt(ah, _rows_pad(bl))


def _dot3_tn(a, b):
    C = DN_CHUNK
    ah, al = _hi_lo(a)
    bh, bl = _hi_lo(b)
    return _dot_tn(jnp.concatenate([ah, al, ah], axis=0), jnp.concatenate([bh, bh, bl], axis=0))[:C]


def _interleave(programs):
    programs = list(programs)
    while programs:
        alive = []
        for prog in programs:
            try:
                next(prog)
                alive.append(prog)
            except StopIteration:
                pass
        programs = alive


def _col_to_row(col, ii, jj):
    return jnp.sum(jnp.where(ii == jj, col, 0.0), axis=0, keepdims=True)


def _row_to_col(row, ii, jj):
    return jnp.sum(jnp.where(ii == jj, row, 0.0), axis=1, keepdims=True)


def _decay(gc_col, ii, jj):
    diff = gc_col - _col_to_row(gc_col, ii, jj)
    return jnp.where(jj <= ii, jnp.exp(jnp.where(jj <= ii, diff, 0.0)), 0.0)


def _softplus(x):
    return jnp.maximum(x, 0.0) + jnp.log(1.0 + jnp.exp(-jnp.abs(x)))


def _head(h):
    return slice(DN_DIM * h, DN_DIM * (h + 1))


def _dn_chunk_fwd(qkv, pg, a_log, dt_bias):
    S = qkv.shape[0]
    C = DN_CHUNK
    G = CHUNKS_LOCAL
    R = G * C
    steps = S // R

    def body(alog_ref, dtb_ref, qkv_ref, pg_ref, w_ref, u_ref, qg_ref, kd_ref, a_ref, t_ref, gcs_ref):
        ii, jj = _chunk_masks()
        lane = lax.broadcasted_iota(jnp.int32, (1, 128), 1)
        eye = (ii == jj).astype(F32)
        gcs_parts = [[] for _ in range(G)]

        def head_program(chunk, h):
            rows = slice(chunk * C, (chunk + 1) * C)
            q, k, v = qkv_ref[rows, _head(h)], qkv_ref[rows, _head(DN_HEADS + h)], qkv_ref[rows, _head(2 * DN_HEADS + h)]
            beta = _sigmoid(pg_ref[rows, h:h + 1])
            g_col = -jnp.exp(alog_ref[h]) * _softplus(pg_ref[rows, DN_HEADS + h:DN_HEADS + h + 1] + dtb_ref[h])
            g_row = _col_to_row(g_col, ii, jj)
            gc_col = jnp.sum(jnp.where(jj <= ii, g_row, 0.0), axis=1, keepdims=True)
            dec = _decay(gc_col, ii, jj)
            eg = jnp.exp(gc_col)
            kb, vb = k * beta, v * beta
            k_rows = _rows_pad(_bf(k))
            kk = _dot_nt(_bf(kb), k_rows)
            qk = _dot_nt(_bf(q), k_rows)
            yield
            t, pw = eye, -jnp.where(jj < ii, kk * dec, 0.0)
            for _ in range(6):
                t, pw = _double_step(t, pw)
                yield
            tb = _bf(t)
            u_ref[rows, _head(h)] = _dot(tb, _rows_pad(_bf(vb)))
            w_ref[rows, _head(h)] = _bf(_dot(tb, _rows_pad(_bf(kb * eg))))
            a_ref[h, rows] = _bf(qk * dec)
            t_ref[h, rows] = t
            qg_ref[rows, _head(h)] = _bf(q * eg)
            kd_ref[rows, _head(h)] = _bf(k * jnp.exp(gc_col[C - 1:C, :] - gc_col))
            gcs_parts[chunk].append(jnp.where(lane == h, gc_col, 0.0) + jnp.where(lane == DN_HEADS + h, beta, 0.0)
                                    + jnp.where(lane == 2 * DN_HEADS + h, g_col, 0.0))

        _interleave(head_program(chunk, h) for chunk in range(G) for h in range(DN_HEADS))
        for chunk in range(G):
            gcs_ref[chunk * C:(chunk + 1) * C, :] = sum(gcs_parts[chunk][1:], gcs_parts[chunk][0])

    smem = pl.BlockSpec(memory_space=pltpu.SMEM)
    wide = pl.BlockSpec((R, 512), lambda n: (n, 0))
    sq = pl.BlockSpec((DN_HEADS, R, CPAD), lambda n: (0, n, 0))
    narrow = pl.BlockSpec((R, 128), lambda n: (n, 0))
    f = lambda *shp: jax.ShapeDtypeStruct(shp, F32)
    b = lambda *shp: jax.ShapeDtypeStruct(shp, BF16)
    return pl.pallas_call(
        body, grid=(steps,), name="dn_chunk_fwd",
        in_specs=[smem, smem, pl.BlockSpec((R, 1536), lambda n: (n, 0)), pl.BlockSpec((R, 128), lambda n: (n, BLK_G))],
        out_specs=[wide, wide, wide, wide, sq, sq, narrow],
        out_shape=[b(S, 512), f(S, 512), b(S, 512), b(S, 512), b(DN_HEADS, S, CPAD), f(DN_HEADS, S, CPAD), f(S, 128)],
        compiler_params=_params(("parallel",)),
    )(a_log, dt_bias, qkv, pg)


def _gated_norm(o, z, gn):
    r, oh = _rms_stats(o)
    return oh * gn * (z * _sigmoid(z))


def _dn_scan_fwd(w, u, qg, kd, a, gcs, pz, gn):
    S = w.shape[0]
    C = DN_CHUNK
    nc = S // C
    G = CHUNKS_SCAN
    R = G * C

    def body(w_ref, u_ref, qg_ref, kd_ref, a_ref, gcs_ref, z_ref, gn_ref, o_ref, vn_ref, sst_ref, out_ref, state):
        @pl.when(pl.program_id(0) == 0)
        def _():
            state[...] = jnp.zeros_like(state)

        def head_program(chunk, h):
            hs = _head(h)
            rows = slice(chunk * C, (chunk + 1) * C)
            s_in = state[h]
            sb = _bf(s_in)
            sst_ref[chunk, h] = sb
            w_s = _dot(w_ref[rows, hs], sb)
            q_s = _dot(qg_ref[rows, hs], sb)
            yield
            vn = u_ref[rows, hs] - w_s
            vnb = _bf(vn)
            o = q_s + _dot(a_ref[h, rows], _rows_pad(vnb))
            k_v = _dot_tn(kd_ref[rows, hs], vnb)
            yield
            state[h] = s_in * jnp.exp(gcs_ref[(chunk + 1) * C - 1:(chunk + 1) * C, h:h + 1]) + k_v
            o_ref[rows, hs] = o
            vn_ref[rows, hs] = vnb
            out_ref[rows, hs] = _bf(_gated_norm(o, z_ref[rows, hs], gn_ref[...]))

        for chunk in range(G):
            _interleave(head_program(chunk, h) for h in range(DN_HEADS))

    wide = pl.BlockSpec((R, 512), lambda n: (n, 0))
    f = lambda *shp: jax.ShapeDtypeStruct(shp, F32)
    b = lambda *shp: jax.ShapeDtypeStruct(shp, BF16)
    return pl.pallas_call(
        body, grid=(nc // G,), name="dn_scan_fwd",
        in_specs=[wide, wide, wide, wide, pl.BlockSpec((DN_HEADS, R, CPAD), lambda n: (0, n, 0)),
                  pl.BlockSpec((R, 128), lambda n: (n, 0)), pl.BlockSpec((R, 512), lambda n: (n, BLK_Z)),
                  pl.BlockSpec((1, DN_DIM), lambda n: (0, 0))],
        out_specs=[wide, wide, pl.BlockSpec((G, DN_HEADS, DN_DIM, DN_DIM), lambda n: (n, 0, 0, 0)), wide],
        out_shape=[f(S, 512), b(S, 512), b(nc, DN_HEADS, DN_DIM, DN_DIM), b(S, 512)],
        scratch_shapes=[pltpu.VMEM((DN_HEADS, DN_DIM, DN_DIM), F32)],
        compiler_params=_params(("arbitrary",)),
    )(w, u, qg, kd, a, gcs, pz, gn)


def _dn_scan_bwd(dcat, o, pz, gn, sst, vnew, w, qg, kd, a, gcs, dproj):
    S = o.shape[0]
    C = DN_CHUNK
    G = CHUNKS_SCAN
    R = G * C
    steps = S // R

    def body(dy_ref, o_ref, z_ref, gn_ref, sst_ref, vn_ref, w_ref, qg_ref, kd_ref, a_ref, gcs_ref, _,
             du_ref, dw_ref, dqg_ref, dkd_ref, da_ref, dz_ref, dsc_ref, dgn_ref, dstate):
        @pl.when(pl.program_id(0) == 0)
        def _():
            dstate[...] = jnp.zeros_like(dstate)
            dgn_ref[...] = jnp.zeros_like(dgn_ref)

        gn_ = gn_ref[...]
        lane = lax.broadcasted_iota(jnp.int32, (C, 128), 1)
        row = lax.broadcasted_iota(jnp.int32, (C, 128), 0)
        dgn_parts = []

        def head_program(chunk, h, dsc_parts):
            hs = _head(h)
            rows = slice(chunk * C, (chunk + 1) * C)
            ov, z, dout = o_ref[rows, hs], z_ref[rows, hs], dy_ref[rows, hs]
            r, oh = _rms_stats(ov)
            sg = _sigmoid(z)
            don = dout * (z * sg)
            dz_ref[rows, hs] = _bf(dout * (oh * gn_) * (sg * (1.0 + z * (1.0 - sg))))
            dgn_parts.append(jnp.sum(don * oh, axis=0, keepdims=True))
            dn = don * gn_
            do = _bf(r * (dn - oh * jnp.mean(dn * oh, axis=-1, keepdims=True)))
            sb = sst_ref[chunk, h]
            s_in = sb.astype(F32)
            ds_out = dstate[h]
            dsb = _bf(ds_out)
            vnb = vn_ref[rows, hs]
            wb, qgb, kdb, ab = w_ref[rows, hs], qg_ref[rows, hs], kd_ref[rows, hs], a_ref[h, rows]
            dvn = _dot_tn(ab, do)[:C] + _dot(kdb, dsb)
            yield
            da_ref[h, rows] = _dot_nt(do, _rows_pad(vnb))
            dqg_ref[rows, hs] = _dot_nt(do, sb)
            dkd_ref[rows, hs] = _dot_nt(vnb, dsb)
            q_do = _dot_tn(qgb, do)
            yield
            dvnb = _bf(dvn)
            dw_ref[rows, hs] = _bf(-_dot_nt(dvnb, sb))
            w_dvn = _dot_tn(wb, dvnb)
            du_ref[rows, hs] = dvnb
            yield
            d_last = jnp.exp(gcs_ref[(chunk + 1) * C - 1:(chunk + 1) * C, h:h + 1])
            dd = jnp.sum(jnp.sum(ds_out * s_in, axis=1, keepdims=True), axis=0, keepdims=True)
            dsc_parts.append(jnp.where((lane == h) & (row == C - 1), dd * d_last, 0.0))
            dstate[h] = ds_out * d_last + q_do - w_dvn

        for chunk in reversed(range(G)):
            dsc_parts = []
            _interleave(head_program(chunk, h, dsc_parts) for h in range(DN_HEADS))
            dsc_ref[chunk * C:(chunk + 1) * C, :] = sum(dsc_parts[1:], dsc_parts[0])
        dgn_ref[...] += sum(dgn_parts[1:], dgn_parts[0])

    rev = lambda n: steps - 1 - n
    wide = pl.BlockSpec((R, 512), lambda n: (rev(n), 0))
    z_spec = pl.BlockSpec((R, 512), lambda n: (rev(n), BLK_Z))
    sq = pl.BlockSpec((DN_HEADS, R, CPAD), lambda n: (0, rev(n), 0))
    narrow = pl.BlockSpec((R, 128), lambda n: (rev(n), 0))
    gn_spec = pl.BlockSpec((1, DN_DIM), lambda n: (0, 0))
    f = lambda *shp: jax.ShapeDtypeStruct(shp, F32)
    b = lambda *shp: jax.ShapeDtypeStruct(shp, BF16)
    return pl.pallas_call(
        body, grid=(steps,), name="dn_scan_bwd",
        in_specs=[pl.BlockSpec((R, 512), lambda n: (rev(n), 1)), wide, z_spec, gn_spec,
                  pl.BlockSpec((G, DN_HEADS, DN_DIM, DN_DIM), lambda n: (rev(n), 0, 0, 0)),
                  wide, wide, wide, wide, sq, narrow, pl.BlockSpec(memory_space=pl.ANY)],
        out_specs=[wide, wide, wide, wide, sq, z_spec, narrow, gn_spec],
        out_shape=[b(S, 512), b(S, 512), f(S, 512), f(S, 512), f(DN_HEADS, S, CPAD),
                   jax.ShapeDtypeStruct(dproj.shape, dproj.dtype), f(S, 128), f(1, DN_DIM)],
        scratch_shapes=[pltpu.VMEM((DN_HEADS, DN_DIM, DN_DIM), F32)],
        input_output_aliases={11: 5},
        compiler_params=_params(("arbitrary",)),
    )(dcat, o, pz, gn, sst, vnew, w, qg, kd, a, gcs, dproj)


def _dn_chunk_bwd(qkv, pg, t_inv, gcs, du, dw, dqg, dkd, da, dsc, a_log, dt_bias, dproj):
    S = qkv.shape[0]
    C = DN_CHUNK
    G = CHUNKS_LOCAL
    R = G * C

    def body(alog_ref, dtb_ref, qkv_ref, pg_ref, t_ref, gcs_ref, du_ref, dw_ref, dqg_ref, dkd_ref, da_ref, dsc_ref, _,
             dqkv_ref, dpg_ref, acc_ref):
        @pl.when(pl.program_id(0) == 0)
        def _():
            acc_ref[...] = jnp.zeros_like(acc_ref)

        ii, jj = _chunk_masks()
        lane = lax.broadcasted_iota(jnp.int32, (1, 128), 1)
        row8 = lax.broadcasted_iota(jnp.int32, (8, 128), 0)
        lane8 = lax.broadcasted_iota(jnp.int32, (8, 128), 1)
        rowc = lax.broadcasted_iota(jnp.int32, (C, 1), 0)
        tril, strict = jj <= ii, jj < ii
        dpg_parts, acc_parts = [[] for _ in range(G)], []

        def head_program(chunk, h):
            rows = slice(chunk * C, (chunk + 1) * C)
            q, k, v = qkv_ref[rows, _head(h)], qkv_ref[rows, _head(DN_HEADS + h)], qkv_ref[rows, _head(2 * DN_HEADS + h)]
            gc_col, beta, g_col = gcs_ref[rows, h:h + 1], gcs_ref[rows, DN_HEADS + h:DN_HEADS + h + 1], \
                gcs_ref[rows, 2 * DN_HEADS + h:2 * DN_HEADS + h + 1]
            dec = _decay(gc_col, ii, jj)
            eg = jnp.exp(gc_col)
            g_last = gc_col[C - 1:C, :]
            ek = jnp.exp(g_last - gc_col)
            kb, vb = k * beta, v * beta
            kbg = kb * eg
            qb, kbb = _bf(q), _bf(kb)
            k_rows = _rows_pad(_bf(k))
            t = t_ref[h, rows]
            tb = _bf(t)
            dub, dwb = du_ref[rows, _head(h)], dw_ref[rows, _head(h)]
            dqg_, dkd_ = dqg_ref[rows, _head(h)], dkd_ref[rows, _head(h)]
            dt = _dot_nt(dub, _rows_pad(_bf(vb))) + _dot_nt(dwb, _rows_pad(_bf(kbg)))
            t_du_dw = _dot_tn(tb, jnp.concatenate([dub, dwb], axis=1))
            dvb, dkbg = t_du_dw[:C, :DN_DIM], t_du_dw[:C, DN_DIM:]
            kk = _dot_nt(kbb, k_rows)
            qk = _dot_nt(qb, k_rows)
            yield
            dt_t = _dot3_nt(dt, t)
            yield
            dl = -_dot3_tn(t, dt_t)
            yield
            dm = jnp.where(strict, dl * dec, 0.0)
            dqk = jnp.where(tril, da_ref[h, rows] * dec, 0.0)
            gmat = dm * kk + dqk * qk
            dgc = jnp.sum(gmat, axis=1, keepdims=True) - _row_to_col(jnp.sum(gmat, axis=0, keepdims=True), ii, jj)
            dmb, dqkb = _bf(dm), _bf(dqk)
            yield
            dkb = _dot(dmb, k_rows) + dkbg * eg
            dk = _dot_tn(jnp.concatenate([dmb, dqkb], axis=0), jnp.concatenate([kbb, qb], axis=0))[:C] + dkd_ * ek
            dq = _dot(dqkb, k_rows) + dqg_ * eg
            yield
            tk = jnp.sum(dkd_ * k * ek, axis=1, keepdims=True)
            dgc = dgc + jnp.sum(dqg_ * q * eg, axis=1, keepdims=True) - tk + jnp.sum(dkbg * kbg, axis=1, keepdims=True)
            dgl = jnp.sum(tk, axis=0, keepdims=True) + dsc_ref[(chunk + 1) * C - 1:(chunk + 1) * C, h:h + 1]
            dgc = dgc + jnp.where(rowc == C - 1, dgl, 0.0)
            yield
            dk = dk + dkb * beta
            dbeta = jnp.sum(dkb * k, axis=1, keepdims=True) + jnp.sum(dvb * v, axis=1, keepdims=True)
            dqkv_ref[rows, _head(h)] = dq
            dqkv_ref[rows, _head(DN_HEADS + h)] = dk
            dqkv_ref[rows, _head(2 * DN_HEADS + h)] = dvb * beta
            dg_col = jnp.sum(jnp.where(jj >= ii, _col_to_row(dgc, ii, jj), 0.0), axis=1, keepdims=True)
            yield
            db = dbeta * beta * (1.0 - beta)
            da_in = dg_col * (-jnp.exp(alog_ref[h])) * _sigmoid(pg_ref[rows, DN_HEADS + h:DN_HEADS + h + 1] + dtb_ref[h])
            dpg_parts[chunk].append(jnp.where(lane == h, db, 0.0) + jnp.where(lane == DN_HEADS + h, da_in, 0.0))
            acc_parts.append(jnp.where((row8 == 0) & (lane8 == h), jnp.sum(dg_col * g_col, axis=0, keepdims=True), 0.0)
                             + jnp.where((row8 == 1) & (lane8 == h), jnp.sum(da_in, axis=0, keepdims=True), 0.0))

        _interleave(head_program(chunk, h) for chunk in range(G) for h in range(DN_HEADS))
        for chunk in range(G):
            dpg = sum(dpg_parts[chunk][1:], dpg_parts[chunk][0])
            dpg_ref[chunk * C:(chunk + 1) * C, :] = _bf(jnp.concatenate([dpg, jnp.zeros_like(dpg)], axis=1))
        acc_ref[...] += sum(acc_parts[1:], acc_parts[0])

    smem = pl.BlockSpec(memory_space=pltpu.SMEM)
    wide = pl.BlockSpec((R, 512), lambda n: (n, 0))
    sq = pl.BlockSpec((DN_HEADS, R, CPAD), lambda n: (0, n, 0))
    narrow = pl.BlockSpec((R, 128), lambda n: (n, 0))
    qkv_spec = pl.BlockSpec((R, 1536), lambda n: (n, 0))
    f = lambda *shp: jax.ShapeDtypeStruct(shp, F32)
    return pl.pallas_call(
        body, grid=(S // R,), name="dn_chunk_bwd",
        in_specs=[smem, smem, qkv_spec, pl.BlockSpec((R, 128), lambda n: (n, BLK_G)), sq, narrow, wide, wide, wide, wide, sq,
                  narrow, pl.BlockSpec(memory_space=pl.ANY)],
        out_specs=[qkv_spec, pl.BlockSpec((R, 256), lambda n: (n, BLK_G_PAD)), pl.BlockSpec((8, 128), lambda n: (0, 0))],
        out_shape=[f(S, 1536), jax.ShapeDtypeStruct(dproj.shape, dproj.dtype), f(8, 128)],
        input_output_aliases={12: 1},
        compiler_params=_params(("arbitrary",)),
    )(a_log, dt_bias, qkv, pg, t_inv, gcs, du, dw, dqg, dkd, da, dsc, dproj)


_W_IN_SECTIONS = ((0, 0, 512), (2304, 512, 512), (768, 1024, 1536), (512, 2560, 256), (2816, 2816, 8))
_HALF = D_MODEL // 2
_SECTION_ROWS = 256


def _pack_pairs(x):
    bits = lax.bitcast_convert_type(x, jnp.uint32)
    return lax.bitcast_convert_type(bits[:, _HALF:] | (bits[:, :_HALF] >> 16), F32)


def _unpack_pairs(words):
    bits = lax.bitcast_convert_type(words, jnp.uint32)
    return (lax.bitcast_convert_type(bits << 16, F32),
            lax.bitcast_convert_type(bits & jnp.uint32(0xFFFF0000), F32))


def _w_in_to_internal(packed):
    starts = [dst for _, dst, _ in _W_IN_SECTIONS] + [D_IN_PAD]

    def body(x_hbm, o_ref, words, sems):
        copies = [pltpu.make_async_copy(x_hbm.at[pl.ds(src, rows), 0, :], words.at[pl.ds(dst, rows)], sems.at[i])
                  for i, (src, dst, rows) in enumerate(_W_IN_SECTIONS)]
        for cp in copies:
            cp.start()
        words[pl.ds(D_IN, D_IN_PAD - D_IN), :] = jnp.zeros((D_IN_PAD - D_IN, _HALF), F32)
        for i, cp in enumerate(copies):
            cp.wait()
            for r in range(starts[i], starts[i + 1], _SECTION_ROWS):
                for c, h in enumerate(_unpack_pairs(words[pl.ds(r, _SECTION_ROWS), :])):
                    o_ref[pl.ds(r, _SECTION_ROWS), c * _HALF:(c + 1) * _HALF] = _bf(h)

    assert all((b - a) % _SECTION_ROWS == 0 for a, b in zip(starts, starts[1:])) and starts[-2] + _W_IN_SECTIONS[-1][2] == D_IN
    return pl.pallas_call(body, name="w_in_to_internal", out_shape=jax.ShapeDtypeStruct((D_IN_PAD, D_MODEL), BF16),
                          in_specs=[pl.BlockSpec(memory_space=pl.ANY)],
                          scratch_shapes=[pltpu.VMEM((D_IN_PAD, _HALF), F32), pltpu.SemaphoreType.DMA((len(_W_IN_SECTIONS),))],
                          compiler_params=pltpu.CompilerParams(vmem_limit_bytes=VMEM_LIMIT))(packed)


def _w_in_from_internal(gt):
    def body(g_ref, o_ref):
        for dst, src, rows in _W_IN_SECTIONS:
            for r in range(0, rows, _SECTION_ROWS):
                n = min(_SECTION_ROWS, rows - r)
                whole_tiles = max(n, 16)
                words = _pack_pairs(g_ref[pl.ds(src + r, whole_tiles), :].astype(F32))
                o_ref[pl.ds(dst + r, n), 0, :] = words[:n]

    return pl.pallas_call(body, name="w_in_from_internal", out_shape=jax.ShapeDtypeStruct((D_IN, 1, _HALF), F32),
                          compiler_params=pltpu.CompilerParams(vmem_limit_bytes=VMEM_LIMIT))(gt)


def _local_step(x, p, target, wts, first_weights, other_weights, ship_early):
    S = x.shape[0]
    cos, sin = _rope_tables(S)
    sinks, a_log, dt_bias = wts["sinks"].reshape(8), wts["a_log"].reshape(4), wts["dt_bias"].reshape(4)
    gn = wts["dn_norm"].reshape(1, DN_DIM)
    add = lambda acc, res: (acc + res,)

    u = _rmsnorm_fwd(x, wts["norm_mix"], "norm_mix_fwd")
    w_in_t, conv_w = first_weights(u)
    proj, = _mm(u, w_in_t, form="nt", name="in_proj", out_dtypes=[F32], tn=512)
    attn, lse = _attn_fwd(proj, cos, sin, sinks)
    qkv = _dn_prep_fwd(proj, conv_w)
    cw, cu, cqg, ckd, ca, ct, gcs = _dn_chunk_fwd(qkv, proj, a_log, dt_bias)
    o, vnew, sst, dn_out = _dn_scan_fwd(cw, cu, cqg, ckd, ca, gcs, proj, gn)
    w_o, = other_weights(("w_o",), dn_out)
    h1, = _mm([attn, dn_out], w_o, form="nn", name="out_proj", out_dtypes=[F32], tn=512, epi=add, extra=[x])

    w_up, w_down = other_weights(("w_up", "w_down"), h1)
    hid, relu, m, h2 = _mlp_fwd(h1, w_up, w_down, wts["norm_mlp"])
    w_pg, w_pp = other_weights(("w_ple_gate", "w_ple_proj"), h2)
    n3, dh2, dgl, dpp, loss, d_norm_final, d_norm_ple = _ple_and_loss(h2, p, target, w_pg, w_pp, wts["norm_ple"],
                                                                     wts["norm_final"].reshape(1, D_MODEL))
    g = {"norm_final": d_norm_final, "norm_ple": d_norm_ple}
    early = {"w_ple_gate": _mm_tn(n3, dgl, name="d_w_ple_gate", tm=512, tn=1024, out_dtype=BF16).reshape(N_DEV, 128, 1024),
             "w_ple_proj": _mm_tn(p, dpp, name="d_w_ple_proj", tm=256, tn=128, out_dtype=BF16, column_shards=True)}
    d_act, = _mm(dh2, w_down, form="nt", name="d_hidden", out_dtypes=[BF16], tn=512,
                 epi=lambda acc, r: (acc * (2.0 * r.astype(F32)),), extra=[relu])
    early["w_down"] = _mm_tn(hid, dh2, name="d_w_down", tm=512, tn=1024, out_dtype=BF16).reshape(N_DEV, 512, 1024)
    early["w_up"] = _mm_tn(m, d_act, name="d_w_up", tm=1024, tn=512, out_dtype=BF16, column_shards=True)
    token = ship_early(early)
    dh1, g["norm_mlp"], dcat = _mm(d_act, w_up, form="nt", name="d_m", out_dtypes=[F32], tn=512, after=token,
                                   norm_bwd=(h1, wts["norm_mlp"], dh2), then_nt=w_o)
    d_w_o = jnp.concatenate([_mm_tn(attn, dh1, name="d_w_o_attn", tm=512, tn=512, out_dtype=BF16),
                             _mm_tn(dn_out, dh1, name="d_w_o_dn", tm=512, tn=512, out_dtype=BF16)], axis=0)
    token = ship_early({"w_o": d_w_o.reshape(N_DEV, 128, 1024)})
    dproj, dk, dv, dsinks = _attn_bwd(proj, cos, sin, sinks + token[0, 0], dcat, attn, lse)
    g["sinks"] = dsinks[:, 0].reshape(1, 8)
    du_, dw_, dqg, dkd, da, dproj, dsc, g["dn_norm"] = _dn_scan_bwd(dcat, o, proj, gn, sst, vnew, cw, cqg, ckd, ca, gcs, dproj)
    dqkv, dproj, gate_acc = _dn_chunk_bwd(qkv, proj, ct, gcs, du_, dw_, dqg, dkd, da, dsc, a_log, dt_bias, dproj)
    g["a_log"], g["dt_bias"] = gate_acc[0:1, 0:4], gate_acc[1:2, 0:4]
    dproj, g["conv_w"] = _dn_prep_bwd(proj, conv_w, dqkv, dproj, dk, dv)
    token = ship_early({"w_in": _mm_tn(dproj, u, name="d_w_in", tm=512, tn=1024, out_dtype=BF16)})
    grad_x, g["norm_mix"] = _mm(dproj, w_in_t, form="nn", name="d_u", out_dtypes=[F32], tn=512, after=token,
                                norm_bwd=(x, wts["norm_mix"], dh1))
    return loss, grad_x, g


def _peer(k):
    x, y, c = lax.axis_index("x"), lax.axis_index("y"), lax.axis_index("c")
    px = 1 - x if k & 4 else x
    py = 1 - y if k & 2 else y
    pc = 1 - c if k & 1 else c
    return (px, py, pc), 4 * px + 2 * py + pc


def _exchange(srcs, name, gather):
    n = len(srcs)
    gathers = list(gather) if isinstance(gather, (list, tuple)) else [gather] * n
    shapes = [(N_DEV,) + s.shape if gt else s.shape for s, gt in zip(srcs, gathers)]

    def body(*refs):
        src_refs, out_refs = refs[:n], refs[n:2 * n]
        send_sems, recv_sems, local_sems = refs[2 * n:]
        _, me = _peer(0)
        piece = lambda a, d: src_refs[a] if gathers[a] else src_refs[a].at[d]
        local = [pltpu.make_async_copy(piece(a, me), out_refs[a].at[me], local_sems.at[a]) for a in range(n)]
        for cp in local:
            cp.start()
        copies = []
        for a in range(n):
            for k in range(1, N_DEV):
                dev, idx = _peer(k)
                cp = pltpu.make_async_remote_copy(src_ref=piece(a, idx), dst_ref=out_refs[a].at[me],
                                                  send_sem=send_sems.at[a, k - 1], recv_sem=recv_sems.at[a, k - 1],
                                                  device_id=dev, device_id_type=MESH)
                cp.start()
                copies.append(cp)
        for cp in copies:
            cp.wait_recv()
        for cp in copies:
            cp.wait_send()
        for cp in local:
            cp.wait()

    anywhere = pl.BlockSpec(memory_space=pl.ANY)
    return pl.pallas_call(
        body, name=name, in_specs=[anywhere] * n, out_specs=[anywhere] * n,
        out_shape=[jax.ShapeDtypeStruct(shp, s.dtype) for shp, s in zip(shapes, srcs)],
        scratch_shapes=[pltpu.SemaphoreType.DMA((n, N_DEV - 1)), pltpu.SemaphoreType.DMA((n, N_DEV - 1)),
                        pltpu.SemaphoreType.DMA((n,))],
    )(*srcs)


_HBM = pl.BlockSpec(memory_space=pltpu.HBM)
_SEM = pl.BlockSpec(memory_space=pltpu.SEMAPHORE)
_EFFECT = pltpu.SideEffectType.DATAFLOW_SIDE_EFFECTING


def _split_copies(src_refs, land_refs, send_sems, recv_sems, modes, which=None):
    _, me = _peer(0)
    local, remote = [], []
    which = range(len(src_refs)) if which is None else which
    for a, src, land in zip(which, src_refs, land_refs):
        if modes[a] == "columns":
            n_cols = src.shape[1]
            dst = land.at[:, pl.ds(pl.multiple_of(me * n_cols, n_cols), n_cols)]
        else:
            dst = land.at[me]
        part = lambda d: src.at[d] if modes[a] == "pieces" else src
        local.append(pltpu.make_async_copy(part(me), dst, recv_sems.at[a * N_DEV]))
        for k in ((2, 4, 6) if modes[a] == "chips" else range(1, N_DEV)):
            dev, idx = _peer(k)
            sem = a * N_DEV + k
            remote.append(pltpu.make_async_remote_copy(
                src_ref=part(idx), dst_ref=dst, send_sem=send_sems.at[sem], recv_sem=recv_sems.at[sem],
                device_id=dev, device_id_type=MESH))
    return local, remote


def _forward_copies(land_refs, send_sems, recv_sems):
    c = lax.axis_index("c")
    sibling, _ = _peer(1)
    copies = []
    for a, land in enumerate(land_refs):
        for chip in range(N_DEV // 2):
            slot = 2 * chip + c
            sem = a * (N_DEV // 2) + chip
            copies.append(pltpu.make_async_remote_copy(
                src_ref=land.at[slot], dst_ref=land.at[slot], send_sem=send_sems.at[sem], recv_sem=recv_sems.at[sem],
                device_id=sibling, device_id_type=MESH))
    return copies


def _forward_start(lands, name):
    n = len(lands)

    def body(*refs):
        for cp in _forward_copies(refs[:n], refs[n], refs[n + 1]):
            cp.start()
        refs[-1][...] = jnp.zeros_like(refs[-1])

    sems = pltpu.SemaphoreType.DMA((n * (N_DEV // 2),))
    out = pl.pallas_call(
        body, name=name,
        out_shape=(sems, sems, *[pltpu.HBM(t.shape, t.dtype) for t in lands], jax.ShapeDtypeStruct((8, 128), F32)),
        in_specs=[_HBM] * n, out_specs=(_SEM, _SEM, *[_HBM] * n, pl.BlockSpec(memory_space=pltpu.VMEM)),
        input_output_aliases={i: 2 + i for i in range(n)},
        compiler_params=pltpu.CompilerParams(has_side_effects=_EFFECT),
    )(*[pltpu.with_memory_space_constraint(t, pltpu.HBM) for t in lands])
    return out[:-1], out[-1]


def _forward_wait(handle, after, name):
    send_sems, recv_sems, *lands = handle
    n = len(lands)

    def body(*refs):
        for cp in _forward_copies(refs[:n], refs[n], refs[n + 1]):
            cp.wait_send()
            cp.wait_recv()

    return list(pl.pallas_call(
        body, name=name, out_shape=tuple(pltpu.HBM(t.shape, t.dtype) for t in lands),
        in_specs=[_HBM] * n + [_SEM, _SEM, pl.BlockSpec(memory_space=pl.ANY)], out_specs=tuple([_HBM] * n),
        input_output_aliases={i: i for i in range(n)},
        compiler_params=pltpu.CompilerParams(has_side_effects=_EFFECT),
    )(*lands, send_sems, recv_sems, after))


def _exchange_start(srcs, name, modes):
    n = len(srcs)
    modes = [modes] * n if isinstance(modes, str) else list(modes)
    lands = []
    for s, mode in zip(srcs, modes):
        shape = {"columns": (s.shape[0], N_DEV * s.shape[1]), "pieces": s.shape}.get(mode, (N_DEV,) + s.shape)
        lands.append(lax.empty(shape, s.dtype))

    def body(*refs):
        src_refs, land_refs = refs[:n], refs[n:2 * n]
        send_sems, recv_sems = refs[2 * n], refs[2 * n + 1]
        local, remote = _split_copies(src_refs, land_refs, send_sems, recv_sems, modes)
        for cp in local + remote:
            cp.start()
        refs[-1][...] = jnp.zeros_like(refs[-1])

    both = list(srcs) + lands
    sems = pltpu.SemaphoreType.DMA((n * N_DEV,))
    out = pl.pallas_call(
        body, name=name,
        out_shape=(sems, sems, *[pltpu.HBM(t.shape, t.dtype) for t in both], jax.ShapeDtypeStruct((8, 128), F32)),
        in_specs=[_HBM] * (2 * n), out_specs=(_SEM, _SEM, *[_HBM] * (2 * n), pl.BlockSpec(memory_space=pltpu.VMEM)),
        input_output_aliases={i: 2 + i for i in range(2 * n)},
        compiler_params=pltpu.CompilerParams(has_side_effects=_EFFECT),
    )(*[pltpu.with_memory_space_constraint(t, pltpu.HBM) for t in both])
    return (n, modes, out[:-1]), out[-1]


def _exchange_wait(handle, after, name, which=None):
    n_all, modes, (send_sems, recv_sems, *both_all) = handle
    which = list(range(n_all)) if which is None else list(which)
    n = len(which)
    both = [both_all[a] for a in which] + [both_all[n_all + a] for a in which]

    def body(*refs):
        src_refs, land_refs = refs[:n], refs[n:2 * n]
        local, remote = _split_copies(src_refs, land_refs, refs[2 * n], refs[2 * n + 1], modes, which)
        for cp in local:
            cp.wait()
        for cp in remote:
            cp.wait_send()
            cp.wait_recv()

    out = pl.pallas_call(
        body, name=name, out_shape=tuple(pltpu.HBM(t.shape, t.dtype) for t in both),
        in_specs=[_HBM] * (2 * n) + [_SEM, _SEM, pl.BlockSpec(memory_space=pl.ANY)], out_specs=tuple([_HBM] * (2 * n)),
        input_output_aliases={i: i for i in range(2 * n)},
        compiler_params=pltpu.CompilerParams(has_side_effects=_EFFECT),
    )(*both, send_sems, recv_sems, after)
    return list(out[n:])


def _cast_all(arrays):
    def body(*refs):
        for src, dst in zip(refs[:len(arrays)], refs[len(arrays):]):
            if len(src.shape) == 2:
                dst[...] = _bf(src[...])
            else:
                dst[:, 0, :] = _pack_pairs(_bf(src[:, 0, :]).astype(F32))

    shapes = [jax.ShapeDtypeStruct(a.shape, BF16) if a.ndim == 2 else jax.ShapeDtypeStruct((a.shape[0], 1, a.shape[2] // 2), F32)
              for a in arrays]
    return pl.pallas_call(body, name="cast_shards", out_shape=shapes,
                          compiler_params=pltpu.CompilerParams(vmem_limit_bytes=VMEM_LIMIT))(*arrays)


def _adam_update(g, w, m, v):
    nm = ADAM_B1 * m + (1.0 - ADAM_B1) * g
    nv = ADAM_B2 * v + (1.0 - ADAM_B2) * (g * g)
    m_hat = nm / (1.0 - ADAM_B1 ** ADAM_STEP)
    v_hat = nv / (1.0 - ADAM_B2 ** ADAM_STEP)
    return -ADAM_LR * (m_hat / (jnp.sqrt(v_hat) + ADAM_EPS) + ADAM_WD * w), nm, nv


def _adamw(parts, w, m, v, name):
    n, R, W = parts.shape
    tm = 128 if R % 128 == 0 else R

    def body(p_ref, w_ref, m_ref, v_ref, g_ref, d_ref, nm_ref, nv_ref):
        g = p_ref[0].astype(F32)
        for s in range(1, n):
            g = g + p_ref[s].astype(F32)
        g_ref[...] = g
        d_ref[...], nm_ref[...], nv_ref[...] = _adam_update(g, w_ref[...], m_ref[...], v_ref[...])

    tile = pl.BlockSpec((tm, W), lambda i: (i, 0))
    return pl.pallas_call(
        body, grid=(R // tm,), name=name,
        in_specs=[pl.BlockSpec((n, tm, W), lambda i: (0, i, 0)), tile, tile, tile],
        out_specs=[tile] * 4, out_shape=[jax.ShapeDtypeStruct((R, W), F32)] * 4,
        compiler_params=_params(("parallel",)),
    )(parts, w, m, v)


def _adamw_rows_apart(parts, w, m, v, name):
    n, R, _, half = parts.shape

    def body(p_hbm, w_hbm, m_hbm, v_hbm, *rest):
        out_hbm, (words, given, results, sems) = rest[:4], rest[4:]
        loads = [pltpu.make_async_copy(p_hbm.at[s, :, 0, :], words.at[s], sems.at[s]) for s in range(n)]
        loads += [pltpu.make_async_copy(h.at[:, 0, :], given.at[i], sems.at[n + i]) for i, h in enumerate((w_hbm, m_hbm, v_hbm))]
        for cp in loads:
            cp.start()
        for cp in loads:
            cp.wait()
        part = lambda s: jnp.concatenate(_unpack_pairs(words[s]), axis=1)
        g = part(0)
        for s in range(1, n):
            g = g + part(s)
        stores = []
        for i, val in enumerate((g,) + _adam_update(g, given[0], given[1], given[2])):
            results[i] = val
            stores.append(pltpu.make_async_copy(results.at[i], out_hbm[i].at[:, 0, :], sems.at[n + 3 + i]))
            stores[-1].start()
        for cp in stores:
            cp.wait()

    anywhere = pl.BlockSpec(memory_space=pl.ANY)
    return pl.pallas_call(
        body, name=name, in_specs=[anywhere] * 4, out_specs=[anywhere] * 4,
        out_shape=[jax.ShapeDtypeStruct(w.shape, F32)] * 4,
        scratch_shapes=[pltpu.VMEM((n, R, half), F32), pltpu.VMEM((3, R, 2 * half), F32), pltpu.VMEM((4, R, 2 * half), F32),
                        pltpu.SemaphoreType.DMA((n + 7,))],
        compiler_params=pltpu.CompilerParams(vmem_limit_bytes=VMEM_LIMIT),
    )(parts, w, m, v)


_MATRICES = ("w_in", "w_o", "w_up", "w_down", "w_ple_gate", "w_ple_proj")


_OTHERS = ("w_o", "w_up", "w_down", "w_ple_gate", "w_ple_proj")
_OTHER_MODES = {"w_o": "slots", "w_up": "slots", "w_down": "slots", "w_ple_gate": "slots", "w_ple_proj": "columns"}


_VECTORS = ("norm_mix", "norm_mlp", "norm_ple", "norm_final", "a_log", "dt_bias", "sinks", "dn_norm")
_SMALL_ROWS, _LOSS_ROW, _CONV_ROW = 16, 8, 9


def _pack_small(vectors, loss, conv):
    def body(*refs):
        out = refs[-1]
        out[...] = jnp.zeros_like(out)
        for r, ref in enumerate(refs[:len(_VECTORS)]):
            out[r:r + 1, 0:ref.shape[1]] = ref[...]
        out[_LOSS_ROW:_LOSS_ROW + 1, 0:128] = refs[len(_VECTORS)][...]
        out[_CONV_ROW:_CONV_ROW + 6, :] = refs[len(_VECTORS) + 1][...]

    return pl.pallas_call(body, name="pack_small", out_shape=jax.ShapeDtypeStruct((_SMALL_ROWS, 1024), F32))(*vectors, loss, conv)


def _sum_slots(parts):
    def body(p_ref, o_ref):
        acc = p_ref[0]
        for s in range(1, parts.shape[0]):
            acc = acc + p_ref[s]
        o_ref[...] = acc

    return pl.pallas_call(body, name="sum_small", out_shape=jax.ShapeDtypeStruct(parts.shape[1:], parts.dtype))(parts)


def _adamw_vectors(summed, conv_g, wmv):
    names = _VECTORS + ("conv_w",)
    flat = [a for triple in wmv for a in triple]

    def body(*refs):
        sum_ref, conv_ref = refs[0], refs[1]
        ins, outs = refs[2:2 + len(flat)], refs[2 + len(flat):]
        for i in range(len(names)):
            w_ref, m_ref, v_ref = ins[3 * i:3 * i + 3]
            g = conv_ref[...] if i == len(_VECTORS) else sum_ref[i:i + 1, 0:w_ref.shape[1]]
            outs[4 * i][...] = g
            outs[4 * i + 1][...], outs[4 * i + 2][...], outs[4 * i + 3][...] = _adam_update(g, w_ref[...], m_ref[...], v_ref[...])

    out_shape = [jax.ShapeDtypeStruct(t[0].shape, F32) for t in wmv for _ in range(4)]
    res = pl.pallas_call(body, name="adamw_vectors", out_shape=out_shape)(summed, conv_g, *flat)
    return {n: res[4 * i:4 * i + 4] for i, n in enumerate(names)}


_ORDER = ("norm_mix", "w_in", "conv_w", "a_log", "dt_bias", "dn_norm", "sinks", "w_o", "norm_mlp", "w_up", "w_down",
          "norm_ple", "w_ple_gate", "w_ple_proj", "norm_final")


def kernel(x, p, norm_mix, w_in, conv_w, a_log, dt_bias, dn_norm, sinks, w_o, norm_mlp, w_up, w_down, norm_ple, w_ple_gate, w_ple_proj, norm_final, loss_target, m_norm_mix, m_w_in, m_conv_w, m_a_log, m_dt_bias, m_dn_norm, m_sinks, m_w_o, m_norm_mlp, m_w_up, m_w_down, m_norm_ple, m_w_ple_gate, m_w_ple_proj, m_norm_final, v_norm_mix, v_w_in, v_conv_w, v_a_log, v_dt_bias, v_dn_norm, v_sinks, v_w_o, v_norm_mlp, v_w_up, v_w_down, v_norm_ple, v_w_ple_gate, v_w_ple_proj, v_norm_final):
    w = dict(norm_mix=norm_mix, w_in=w_in, conv_w=conv_w[0], a_log=a_log, dt_bias=dt_bias, dn_norm=dn_norm, sinks=sinks,
             w_o=w_o[0], norm_mlp=norm_mlp, w_up=w_up[0], w_down=w_down[0], norm_ple=norm_ple, w_ple_gate=w_ple_gate[0],
             w_ple_proj=w_ple_proj[0], norm_final=norm_final)
    m = dict(norm_mix=m_norm_mix, w_in=m_w_in, conv_w=m_conv_w[0], a_log=m_a_log, dt_bias=m_dt_bias, dn_norm=m_dn_norm,
             sinks=m_sinks, w_o=m_w_o[0], norm_mlp=m_norm_mlp, w_up=m_w_up[0], w_down=m_w_down[0], norm_ple=m_norm_ple,
             w_ple_gate=m_w_ple_gate[0], w_ple_proj=m_w_ple_proj[0], norm_final=m_norm_final)
    v = dict(norm_mix=v_norm_mix, w_in=v_w_in, conv_w=v_conv_w[0], a_log=v_a_log, dt_bias=v_dt_bias, dn_norm=v_dn_norm,
             sinks=v_sinks, w_o=v_w_o[0], norm_mlp=v_norm_mlp, w_up=v_w_up[0], w_down=v_w_down[0], norm_ple=v_norm_ple,
             w_ple_gate=v_w_ple_gate[0], w_ple_proj=v_w_ple_proj[0], norm_final=v_norm_final)
    me = 4 * lax.axis_index("x") + 2 * lax.axis_index("y") + lax.axis_index("c")
    conv_shard = conv_w.shape[2]

    for d in (w, m, v):
        d["w_in"] = jnp.transpose(d["w_in"], (2, 0, 1))
    conv_pad = jnp.pad(w["conv_w"], ((0, 8 - DN_CONV), (0, 256 - conv_shard)))
    shards = _cast_all([w[n] for n in ("w_in",) + _OTHERS])
    gathers, token_gather = _exchange_start([shards[0], conv_pad] + list(shards[1:]), "gather_start",
                                            ["chips", "chips"] + [_OTHER_MODES[n] for n in _OTHERS])
    vectors = dict(w)
    vectors["norm_mix"] = w["norm_mix"] + token_gather[0:1, 0:1]

    def first_weights(after):
        over_ici = _exchange_wait(gathers, after, "gather_first_wait", [0, 1])
        handle, token = _forward_start(over_ici, "gather_first_forward")
        w_in_all, conv_all = _forward_wait(handle, token, "gather_first_forward_wait")
        conv_all = jnp.transpose(conv_all[:, :DN_CONV, :conv_shard], (1, 0, 2)).reshape(DN_CONV, N_DEV * conv_shard)
        return _w_in_to_internal(w_in_all.reshape(D_IN, 1, _HALF)), conv_all

    as_taken = {"w_o": lambda t: t.reshape(1024, 1024), "w_up": lambda t: t, "w_down": lambda t: t.reshape(4096, 1024),
                "w_ple_gate": lambda t: t.reshape(1024, 1024), "w_ple_proj": lambda t: t}

    def other_weights(names, after):
        which = [2 + _OTHERS.index(n) for n in names]
        got = _exchange_wait(gathers, after, "gather_wait_" + names[0], which)
        return [as_taken[n](t) for n, t in zip(names, got)]

    shipped = []

    def ship_early(pieces):
        names = tuple(pieces)
        if names == ("w_in",):
            pieces = {"w_in": _w_in_from_internal(pieces["w_in"]).reshape(N_DEV, D_IN // N_DEV, 1, _HALF)}
        handle, token = _exchange_start([pieces[n] for n in names], "scatter_start_" + names[0], "pieces")
        shipped.append((names, handle))
        return token

    loss, grad_x, g = _local_step(x[0], p[0, 0], loss_target[0], vectors, first_weights, other_weights, ship_early)

    row = lambda t: t.reshape(1, t.size)
    small = _pack_small([row(g[n]) for n in _VECTORS], loss, g["conv_w"].reshape(6, 1024))
    small_handle, token_small = _exchange_start([small], "gather_small_start", "slots")
    big, after = {}, token_small
    for names, handle in shipped[:-1]:
        for n, r in zip(names, _exchange_wait(handle, after, "scatter_wait_" + names[0])):
            big[n] = _adamw(r, w[n], m[n], v[n], "adamw_" + n)
            after = big[n][1]
    small_all, = _exchange_wait(small_handle, after, "gather_small_wait")
    summed = _sum_slots(small_all)
    conv_g = lax.dynamic_slice(summed[_CONV_ROW:_CONV_ROW + 6].reshape(DN_CONV, N_DEV * conv_shard), (0, me * conv_shard),
                               (DN_CONV, conv_shard))
    small_out = _adamw_vectors(summed, conv_g, [(row(w[n]), row(m[n]), row(v[n])) for n in _VECTORS]
                               + [(w["conv_w"], m["conv_w"], v["conv_w"])])
    names, handle = shipped[-1]
    for n, r in zip(names, _exchange_wait(handle, small_out["conv_w"][0], "scatter_wait_" + names[0])):
        big[n] = _adamw_rows_apart(r, w[n], m[n], v[n], "adamw_" + n)

    result = [summed[_LOSS_ROW, 0], grad_x[None]]
    for i in range(4):
        for n in _ORDER:
            if n == "w_in":
                result.append(jnp.transpose(big[n][i], (1, 2, 0)))
            elif n in _MATRICES:
                result.append(big[n][i][None])
            elif n == "conv_w":
                result.append(small_out[n][i][None])
            else:
                result.append(small_out[n][i].reshape(w[n].shape))
    return tuple(result)
```

```python
import jax
import jax.numpy as jnp
import numpy as np
from jax import lax
from jax.experimental import pallas as pl
from jax.experimental.pallas import tpu as pltpu

F32, BF16 = jnp.float32, jnp.bfloat16
EPS = 1e-6
D_MODEL = 1024
N_DEV = 8
ATTN_BLOCK = 128
HEAD_PAIR = 128
DN_HEADS = 4
DN_DIM = 128
DN_CHUNK = 64
DN_CONV = 4
ROPE_THETA = 10000.0
D_IN = 2824
D_IN_PAD = 3072
BLK_Q, BLK_Z = 0, 1
BLK_DN, BLK_K, BLK_V, BLK_G = 8, 20, 21, 22
BLK_G_PAD = 11
VMEM_LIMIT = 56 * 1024 * 1024
NEG = -1e30
ADAM_LR, ADAM_B1, ADAM_B2, ADAM_EPS, ADAM_WD, ADAM_STEP = 0.001, 0.9, 0.999, 1e-08, 0.01, 10
MESH = pl.DeviceIdType.MESH


def _bf(x):
    return x.astype(BF16)


def _dot(a, b):
    return jnp.dot(a, b, preferred_element_type=F32)


def _dot_nt(a, b):
    return lax.dot_general(a, b, (((1,), (1,)), ((), ())), preferred_element_type=F32)


def _dot_tn(a, b):
    return lax.dot_general(a, b, (((0,), (0,)), ((), ())), preferred_element_type=F32)


def _sigmoid(x):
    return 1.0 / (1.0 + jnp.exp(-x))


def _params(sem):
    return pltpu.CompilerParams(dimension_semantics=sem, vmem_limit_bytes=VMEM_LIMIT)


def _mm(x, w, *, form, name, out_dtypes, tn, epi=None, extra=(), tm=512, w_row_block=0, after=None, norm=None,
        norm_bwd=None, then_nt=None):
    assert norm is None or norm_bwd is None
    xs = list(x) if isinstance(x, (list, tuple)) else [x]
    nx = len(xs)
    S, K = xs[0].shape
    shards = w.ndim == 3
    N = (w.shape[2] * N_DEV if shards else w.shape[1]) if form == "nn" else w.shape[-2]
    assert not (shards and form == "nn" and tn != w.shape[2]) and (nx == 1 or (form == "nn" and not shards and norm is None))
    r0 = w_row_block * K
    tm = min(tm, S)
    n_extra, n_out = len(extra), len(out_dtypes)
    tile = lambda width: pl.BlockSpec((tm, width), lambda i: (i, 0))
    whole = lambda a: pl.BlockSpec(a.shape, lambda i, nd=a.ndim: (0,) * nd)
    ins, in_specs = [*xs, w, *extra], [tile(K)] * nx + [whole(w)] + [tile(N)] * n_extra
    if norm is not None:
        ins, in_specs = ins + [norm], in_specs + [whole(norm)]
    if norm_bwd is not None:
        ins, in_specs = ins + list(norm_bwd), in_specs + [tile(N), whole(norm_bwd[1]), tile(N)]
    if then_nt is not None:
        ins, in_specs = ins + [then_nt], in_specs + [whole(then_nt)]
    if after is not None:
        ins, in_specs = ins + [after], in_specs + [whole(after)]
    out_shape = [jax.ShapeDtypeStruct((S, N), dt) for dt in out_dtypes]
    out_specs = [tile(N)] * n_out
    if norm is not None:
        out_shape, out_specs = out_shape + [jax.ShapeDtypeStruct((S, K), BF16)], out_specs + [tile(K)]
    if norm_bwd is not None:
        out_shape, out_specs = out_shape + [jax.ShapeDtypeStruct((1, N), F32)], out_specs + [pl.BlockSpec((1, N), lambda i: (0, 0))]
    if then_nt is not None:
        out_shape, out_specs = out_shape + [jax.ShapeDtypeStruct((S, then_nt.shape[0]), F32)], out_specs + [tile(then_nt.shape[0])]

    def product(xb, w_ref, cols, c):
        if form == "nn" and nx > 1:
            return sum(_dot(part, w_ref[r0 + p * K:r0 + (p + 1) * K, cols]) for p, part in enumerate(xb))
        if form == "nn":
            return _dot(xb, w_ref[c] if shards else w_ref[r0:r0 + K, cols])
        if not shards:
            return _dot_nt(xb, w_ref[cols, :])
        ks = w.shape[2]
        acc = _dot_nt(xb[:, 0:ks], w_ref[0, cols, :])
        for s in range(1, N_DEV):
            acc = acc + _dot_nt(xb[:, s * ks:(s + 1) * ks], w_ref[s, cols, :])
        return acc

    def body(*refs):
        x_ref, w_ref = refs[0], refs[nx]
        extra_refs = refs[nx + 1:nx + 1 + n_extra]
        at = nx + 1 + n_extra
        if norm is not None:
            gain_ref, at = refs[at], at + 1
        if norm_bwd is not None:
            (y_ref, ygain_ref, dres_ref), at = refs[at:at + 3], at + 3
        if then_nt is not None:
            w2_ref, at = refs[at], at + 1
        outs = refs[len(ins):]
        if norm is not None:
            _, xh = _rms_stats(x_ref[...])
            xb = _bf(xh * gain_ref[...])
            outs[n_out][...] = xb
        else:
            xb = _bf(x_ref[...]) if nx == 1 else [_bf(r[...]) for r in refs[:nx]]
        for c in range(N // tn):
            cols = slice(c * tn, (c + 1) * tn)
            acc = product(xb, w_ref, cols, c)
            res = epi(acc, *[r[:, cols] for r in extra_refs]) if epi else (acc,)
            for o, r in zip(outs[:n_out], res):
                o[:, cols] = r.astype(o.dtype)
        if norm_bwd is not None:
            dx, dg = _rms_bwd_tile(y_ref[...], ygain_ref[...], outs[0][...])
            outs[0][...] = dres_ref[...] + dx
            dg_ref = outs[n_out]

            @pl.when(pl.program_id(0) == 0)
            def _():
                dg_ref[...] = jnp.zeros_like(dg_ref)

            dg_ref[...] += dg
        if then_nt is not None:
            yb = _bf(outs[0][...])
            for c in range(then_nt.shape[0] // tn):
                cols = slice(c * tn, (c + 1) * tn)
                outs[-1][:, cols] = _dot_nt(yb, w2_ref[cols, :])

    return pl.pallas_call(
        body, grid=(S // tm,), name=name, in_specs=in_specs, out_specs=out_specs, out_shape=out_shape,
        compiler_params=_params(("arbitrary",) if norm_bwd is not None else ("parallel",)),
    )(*ins)


def _mlp_fwd(h1, w_up, w_down, gain):
    S, K = h1.shape
    n_sh, _, fs = w_up.shape
    tm = min(512, S)

    def body(x_ref, wup_ref, wdown_ref, g_ref, hid_ref, relu_ref, m_ref, h2_ref):
        x = x_ref[...]
        _, xh = _rms_stats(x)
        mb = _bf(xh * g_ref[...])
        m_ref[...] = mb
        h2_ref[...] = x
        for c in range(n_sh):
            cols = slice(c * fs, (c + 1) * fs)
            r = jnp.maximum(_dot(mb, wup_ref[c]), 0.0)
            hd = _bf(r * r)
            hid_ref[:, cols] = hd
            relu_ref[:, cols] = _bf(r)
            h2_ref[...] += _dot(hd, wdown_ref[cols, :])

    tile = lambda width: pl.BlockSpec((tm, width), lambda i: (i, 0))
    once = lambda a: pl.BlockSpec(a.shape, lambda i, nd=a.ndim: (0,) * nd, pipeline_mode=pl.Buffered(1))
    F = n_sh * fs
    return pl.pallas_call(
        body, grid=(S // tm,), name="mlp_fwd",
        in_specs=[tile(K), once(w_up), once(w_down), pl.BlockSpec(gain.shape, lambda i: (0, 0))],
        out_specs=[tile(F), tile(F), tile(K), tile(K)],
        out_shape=[jax.ShapeDtypeStruct((S, F), BF16), jax.ShapeDtypeStruct((S, F), BF16),
                   jax.ShapeDtypeStruct((S, K), BF16), jax.ShapeDtypeStruct((S, K), F32)],
        compiler_params=_params(("parallel",)),
    )(h1, w_up, w_down, gain)


def _mm_tn(x, dy, *, name, tm, tn, out_dtype=F32, column_shards=False, after=None):
    S, K = x.shape
    N = dy.shape[1]
    waits = [] if after is None else [after]

    def body(x_ref, dy_ref, *rest):
        rest[-1][...] = _dot_tn(_bf(x_ref[...]), _bf(dy_ref[...])).astype(out_dtype)

    if column_shards:
        out_spec = pl.BlockSpec((None, tm, tn), lambda i, j: (j, i, 0))
        out_shape = jax.ShapeDtypeStruct((N // tn, K, tn), out_dtype)
    else:
        out_spec = pl.BlockSpec((tm, tn), lambda i, j: (i, j))
        out_shape = jax.ShapeDtypeStruct((K, N), out_dtype)
    return pl.pallas_call(
        body, grid=(K // tm, N // tn), name=name,
        in_specs=[pl.BlockSpec((S, tm), lambda i, j: (0, i)), pl.BlockSpec((S, tn), lambda i, j: (0, j))]
        + [pl.BlockSpec(memory_space=pl.ANY)] * len(waits),
        out_specs=out_spec, out_shape=out_shape,
        compiler_params=_params(("parallel", "parallel")),
    )(x, dy, *waits)


def _rowwise(body, *, tiled, full, out_tiled, out_acc, name, tm=512, smem=()):
    S = tiled[0].shape[0]
    tm = min(tm, S)
    n_in = len(smem) + len(tiled) + len(full)

    def kern(*refs):
        @pl.when(pl.program_id(0) == 0)
        def _():
            for r in refs[n_in + len(out_tiled):]:
                r[...] = jnp.zeros_like(r)
        body(*refs)

    in_specs = [pl.BlockSpec(memory_space=pltpu.SMEM) for _ in smem]
    in_specs += [pl.BlockSpec((tm, a.shape[1]), lambda i: (i, 0)) for a in tiled]
    in_specs += [pl.BlockSpec(a.shape, lambda i, nd=a.ndim: (0,) * nd) for a in full]
    out_specs = [pl.BlockSpec((tm, w), lambda i: (i, 0)) for w, _ in out_tiled]
    out_specs += [pl.BlockSpec(shp, lambda i, nd=len(shp): (0,) * nd) for shp, _ in out_acc]
    out_shape = [jax.ShapeDtypeStruct((S, w), dt) for w, dt in out_tiled]
    out_shape += [jax.ShapeDtypeStruct(shp, dt) for shp, dt in out_acc]
    return pl.pallas_call(
        kern, grid=(S // tm,), name=name, in_specs=in_specs, out_specs=out_specs, out_shape=out_shape,
        compiler_params=_params(("arbitrary",)),
    )(*smem, *tiled, *full)


def _rms_stats(x):
    r = lax.rsqrt(jnp.mean(x * x, axis=-1, keepdims=True) + EPS)
    return r, x * r


def _rmsnorm_fwd(x, g, name, after):
    def body(x_ref, g_ref, _, o_ref):
        _, xh = _rms_stats(x_ref[...])
        o_ref[...] = _bf(xh * g_ref[...])

    return _rowwise(body, tiled=[x], full=[g, after], out_tiled=[(x.shape[1], BF16)], out_acc=[], name=name)[0]


def _rms_bwd_tile(x, g, dxn):
    r, xh = _rms_stats(x)
    dg = jnp.sum(dxn * xh, axis=0, keepdims=True)
    dn = dxn * g
    dx = r * (dn - xh * jnp.mean(dn * xh, axis=-1, keepdims=True))
    return dx, dg


def _ple_and_loss(h2, p, target, w_pg, w_pp, g_ple, g_final):
    S, n = h2.shape
    tm = min(512, S)
    tn = 512

    def body(h2_ref, p_ref, t_ref, wpg_ref, wpp_ref, gple_ref, gfin_ref,
             n3_ref, dh_ref, dgl_ref, dpp_ref, loss_ref, dg_ref, dgple_ref, pp, gate, h3):
        @pl.when(pl.program_id(0) == 0)
        def _():
            loss_ref[...] = jnp.zeros_like(loss_ref)
            dg_ref[...] = jnp.zeros_like(dg_ref)
            dgple_ref[...] = jnp.zeros_like(dgple_ref)

        x = h2_ref[...]
        _, xh = _rms_stats(x)
        n3 = _bf(xh * gple_ref[...])
        n3_ref[...] = n3
        pb = _bf(p_ref[...])
        for c in range(n // tn):
            cols = slice(c * tn, (c + 1) * tn)
            pp[:, cols] = _dot(pb, wpp_ref[:, cols])
            gt = _sigmoid(_dot(n3, wpg_ref[:, cols]))
            gate[:, cols] = gt
            h3[:, cols] = x[:, cols] + gt * pp[:, cols]
        y = h3[...]
        _, yh = _rms_stats(y)
        e = yh * gfin_ref[...] - t_ref[...]
        per_tok = jnp.mean(e * e, axis=-1, keepdims=True)
        loss_ref[...] += 0.5 * jnp.sum(per_tok, axis=0, keepdims=True)
        dh, dg = _rms_bwd_tile(y, gfin_ref[...], e * (1.0 / n))
        dg_ref[...] += dg
        gt = gate[...]
        dgl = _bf(dh * pp[...] * gt * (1.0 - gt))
        dgl_ref[...] = dgl
        dpp_ref[...] = _bf(dh * gt)
        for c in range(n // tn):
            cols = slice(c * tn, (c + 1) * tn)
            h3[:, cols] = _dot_nt(dgl, wpg_ref[cols, :])
        dx, dgp = _rms_bwd_tile(x, gple_ref[...], h3[...])
        dh_ref[...] = dh + dx
        dgple_ref[...] += dgp

    tile = lambda width: pl.BlockSpec((tm, width), lambda i: (i, 0))
    whole = lambda a: pl.BlockSpec(a.shape, lambda i, nd=a.ndim: (0,) * nd)
    return pl.pallas_call(
        body, grid=(S // tm,), name="ple_and_loss",
        in_specs=[tile(n), tile(p.shape[1]), tile(n), whole(w_pg), whole(w_pp), whole(g_ple), whole(g_final)],
        out_specs=[tile(n), tile(n), tile(n), tile(n), pl.BlockSpec((1, 128), lambda i: (0, 0)),
                   pl.BlockSpec((1, n), lambda i: (0, 0)), pl.BlockSpec((1, n), lambda i: (0, 0))],
        out_shape=[jax.ShapeDtypeStruct((S, n), BF16), jax.ShapeDtypeStruct((S, n), F32), jax.ShapeDtypeStruct((S, n), BF16),
                   jax.ShapeDtypeStruct((S, n), BF16), jax.ShapeDtypeStruct((1, 128), F32), jax.ShapeDtypeStruct((1, n), F32),
                   jax.ShapeDtypeStruct((1, n), F32)],
        scratch_shapes=[pltpu.VMEM((tm, n), F32)] * 3,
        compiler_params=_params(("arbitrary",)),
    )(h2, p, target, w_pg, w_pp, g_ple, g_final)


def _rope_tables(S):
    half = 32
    inv = (1.0 / (np.float32(ROPE_THETA) ** (np.arange(half, dtype=np.float32) * np.float32(2.0 / 64)))).astype(np.float32)
    ang = np.arange(S).astype(np.float32)[:, None] * inv[None, :]
    cos, sin = np.cos(ang), np.sin(ang)
    return jnp.asarray(np.tile(cos, (1, 4))), jnp.asarray(np.concatenate([-sin, sin, -sin, sin], axis=1))


def _attn_common(i, kc, kp, vc, vp, cc, sc, cp, sp):
    lane = lax.broadcasted_iota(jnp.int32, (1, HEAD_PAIR), 1)
    lane_lo = jnp.bitwise_and(lane, 63) < 32
    slot = [lane < 64, lane >= 64]

    def swap_halves(t):
        return jnp.where(lane_lo, pltpu.roll(t, 96, 1), pltpu.roll(t, 32, 1))

    def rope(t, cos, sin):
        return t * cos + swap_halves(t) * sin

    def unrope(d, cos, sin):
        return d * cos + swap_halves(d * sin)

    k2 = jnp.concatenate([rope(kp, cp, sp), rope(kc, cc, sc)], axis=0)
    v2 = jnp.concatenate([vp, vc], axis=0)
    r = lax.broadcasted_iota(jnp.int32, (ATTN_BLOCK, 2 * ATTN_BLOCK), 0)
    c = lax.broadcasted_iota(jnp.int32, (ATTN_BLOCK, 2 * ATTN_BLOCK), 1)
    valid = (c > r) & (c <= r + ATTN_BLOCK) & jnp.logical_or(c >= ATTN_BLOCK, i > 0)
    ks, vs = {}, {}
    for j in range(2):
        kn = jnp.where(slot[j], k2, 0.0)
        vn = jnp.where(slot[j], v2, 0.0)
        for s in range(2):
            ks[j, s] = _bf(kn if s == j else pltpu.roll(kn, 64, 1))
            vs[j, s] = _bf(vn if s == j else pltpu.roll(vn, 64, 1))
    return slot, rope, unrope, valid, ks, vs


def _attn_probs(scores, valid, sink):
    s = jnp.where(valid, scores * 0.125, NEG)
    m = jnp.maximum(jnp.max(s, axis=1, keepdims=True), sink)
    e = jnp.exp(s - m)
    z = jnp.sum(e, axis=1, keepdims=True) + jnp.exp(sink - m)
    return e * (1.0 / z), m + jnp.log(z)


def _attn_specs(S):
    nb = S // ATTN_BLOCK
    prev = lambda i: jnp.maximum(i - 1, 0)
    blk = lambda w, col, row=(lambda i: i): pl.BlockSpec((ATTN_BLOCK, w), lambda i: (row(i), col))
    in_specs = [pl.BlockSpec(memory_space=pltpu.SMEM),
                blk(512, BLK_Q), blk(128, BLK_K), blk(128, BLK_K, prev), blk(128, BLK_V), blk(128, BLK_V, prev),
                blk(128, 0), blk(128, 0), blk(128, 0, prev), blk(128, 0, prev)]
    return nb, in_specs


def _attn_fwd(pa, cos, sin, sinks):
    S = pa.shape[0]
    nb, in_specs = _attn_specs(S)

    def body(sinks_ref, q_ref, kc_ref, kp_ref, vc_ref, vp_ref, cc_ref, sc_ref, cp_ref, sp_ref, o_ref, lse_ref):
        i = pl.program_id(0)
        lane = lax.broadcasted_iota(jnp.int32, (1, HEAD_PAIR), 1)
        cc, sc = cc_ref[...], sc_ref[...]
        _, rope, _, valid, ks, vs = _attn_common(i, kc_ref[...], kp_ref[...], vc_ref[...], vp_ref[...],
                                                 cc, sc, cp_ref[...], sp_ref[...])
        pair_cols = [slice(HEAD_PAIR * pair, HEAD_PAIR * (pair + 1)) for pair in range(4)]
        qps = [_bf(rope(q_ref[:, cols], cc, sc)) for cols in pair_cols]
        outs, lses = {}, {}

        def head_program(h):
            pair, s = divmod(h, 2)
            j = h // 4
            scores = _dot_nt(qps[pair], ks[j, s])
            yield
            p, lse = _attn_probs(scores, valid, sinks_ref[h])
            outs[h] = _dot(_bf(p), vs[j, s])
            lses[h] = jnp.where(lane == h, lse, 0.0)

        _interleave(head_program(h) for h in range(8))
        for pair, cols in enumerate(pair_cols):
            o_ref[:, cols] = outs[2 * pair] + outs[2 * pair + 1]
        lse_ref[...] = sum((lses[h] for h in range(1, 8)), lses[0])

    return pl.pallas_call(
        body, grid=(nb,), name="attn_fwd", in_specs=in_specs,
        out_specs=[pl.BlockSpec((ATTN_BLOCK, 512), lambda i: (i, 0)), pl.BlockSpec((ATTN_BLOCK, 128), lambda i: (i, 0))],
        out_shape=[jax.ShapeDtypeStruct((S, 512), F32), jax.ShapeDtypeStruct((S, 128), F32)],
        compiler_params=_params(("parallel",)),
    )(sinks, pa, pa, pa, pa, pa, cos, sin, cos, sin)


def _attn_bwd(pa, cos, sin, sinks, dcat, attn, lse, after):
    S = pa.shape[0]
    nb, in_specs = _attn_specs(S)
    in_specs = in_specs + [pl.BlockSpec((ATTN_BLOCK, 512), lambda i: (i, 0))] * 2 + [pl.BlockSpec((ATTN_BLOCK, 128), lambda i: (i, 0))]
    in_specs = in_specs + [pl.BlockSpec(memory_space=pl.ANY)]

    def body(sinks_ref, q_ref, kc_ref, kp_ref, vc_ref, vp_ref, cc_ref, sc_ref, cp_ref, sp_ref, do_ref, o_ref, lse_ref, _,
             dq_ref, dk_ref, dv_ref, dsink_ref):
        i = pl.program_id(0)

        @pl.when(i == 0)
        def _():
            dk_ref[...] = jnp.zeros_like(dk_ref)
            dv_ref[...] = jnp.zeros_like(dv_ref)
            dsink_ref[...] = jnp.zeros_like(dsink_ref)

        cc, sc, cp, sp = cc_ref[...], sc_ref[...], cp_ref[...], sp_ref[...]
        slot, rope, unrope, valid, ks, vs = _attn_common(i, kc_ref[...], kp_ref[...], vc_ref[...], vp_ref[...], cc, sc, cp, sp)
        pair_cols = [slice(HEAD_PAIR * pair, HEAD_PAIR * (pair + 1)) for pair in range(4)]
        qps = [_bf(rope(q_ref[:, cols], cc, sc)) for cols in pair_cols]
        dobs = [_bf(do_ref[:, cols]) for cols in pair_cols]
        do_o = [do_ref[:, cols] * o_ref[:, cols] for cols in pair_cols]
        dqs, dks, dvs = {}, {}, {}

        def head_program(h):
            pair, s = divmod(h, 2)
            j = h // 4
            qp, dob = qps[pair], dobs[pair]
            scores = _dot_nt(qp, ks[j, s])
            dp = _dot_nt(dob, vs[j, s])
            yield
            lse_h = lse_ref[:, h:h + 1]
            p = jnp.exp(jnp.where(valid, scores * 0.125, NEG) - lse_h)
            yield
            dr = jnp.sum(jnp.where(slot[s], do_o[pair], 0.0), axis=1, keepdims=True)
            ds = _bf(p * (dp - dr) * 0.125)
            yield
            dsink_ref[h:h + 1, :] += -jnp.sum(jnp.exp(sinks_ref[h] - lse_h) * dr, axis=0, keepdims=True)
            dqs[h] = _dot(ds, ks[j, s])
            dk_h = _dot_tn(ds, qp)
            dv_h = _dot_tn(_bf(p), dob)
            yield
            dk_h, dv_h = jnp.where(slot[s], dk_h, 0.0), jnp.where(slot[s], dv_h, 0.0)
            if s != j:
                dk_h, dv_h = pltpu.roll(dk_h, 64, 1), pltpu.roll(dv_h, 64, 1)
            dks[h], dvs[h] = dk_h, dv_h

        _interleave(head_program(h) for h in range(8))
        dk2 = sum((dks[h] for h in range(1, 8)), dks[0])
        dv2 = sum((dvs[h] for h in range(1, 8)), dvs[0])
        for pair, cols in enumerate(pair_cols):
            dq_ref[:, cols] = _bf(unrope(dqs[2 * pair] + dqs[2 * pair + 1], cc, sc))
        cur = pl.ds(pl.multiple_of(i * ATTN_BLOCK, ATTN_BLOCK), ATTN_BLOCK)
        dk_ref[cur, :] += unrope(dk2[ATTN_BLOCK:], cc, sc)
        dv_ref[cur, :] += dv2[ATTN_BLOCK:]

        @pl.when(i > 0)
        def _():
            prv = pl.ds(pl.multiple_of((i - 1) * ATTN_BLOCK, ATTN_BLOCK), ATTN_BLOCK)
            dk_ref[prv, :] += unrope(dk2[:ATTN_BLOCK], cp, sp)
            dv_ref[prv, :] += dv2[:ATTN_BLOCK]

    whole = lambda w: pl.BlockSpec((S, w), lambda i: (0, 0))
    return pl.pallas_call(
        body, grid=(nb,), name="attn_bwd", in_specs=in_specs,
        out_specs=[pl.BlockSpec((ATTN_BLOCK, 512), lambda i: (i, BLK_Q)), whole(128), whole(128),
                   pl.BlockSpec((8, 128), lambda i: (0, 0))],
        out_shape=[jax.ShapeDtypeStruct((S, D_IN_PAD), BF16), jax.ShapeDtypeStruct((S, 128), F32),
                   jax.ShapeDtypeStruct((S, 128), F32), jax.ShapeDtypeStruct((8, 128), F32)],
        compiler_params=_params(("arbitrary",)),
    )(sinks, pa, pa, pa, pa, pa, cos, sin, cos, sin, dcat, attn, lse, after)


CONV_ROWS = 512
CONV_PAD = 8


def _conv_silu(scr, w, r0):
    y = w[3:4, :] * scr[pl.ds(CONV_PAD + r0, CONV_ROWS), :]
    for j in range(DN_CONV - 1):
        y = y + w[j:j + 1, :] * scr[pl.ds(CONV_PAD + r0 - 3 + j, CONV_ROWS), :]
    return y


def _dn_prep_fwd(pd, conv_w):
    S = pd.shape[0]
    assert S % CONV_ROWS == 0

    def body(x_ref, w_ref, o_ref, scr):
        b = pl.program_id(0)
        scr[0:CONV_PAD, :] = jnp.zeros((CONV_PAD, DN_DIM), F32)
        scr[pl.ds(CONV_PAD, S), :] = x_ref[...]
        w = w_ref[...]
        q_scale = jnp.where(b < DN_HEADS, DN_DIM ** -0.5, 1.0)
        for r0 in range(0, S, CONV_ROWS):
            y = _conv_silu(scr, w, r0)
            a = y * _sigmoid(y)
            rs = lax.rsqrt(jnp.sum(a * a, axis=1, keepdims=True) + EPS)
            o_ref[pl.ds(r0, CONV_ROWS), :] = a * jnp.where(b < 2 * DN_HEADS, rs * q_scale, 1.0)

    col = pl.BlockSpec((S, DN_DIM), lambda b: (0, b))
    return pl.pallas_call(
        body, grid=(3 * DN_HEADS,), name="dn_prep_fwd",
        in_specs=[pl.BlockSpec((S, DN_DIM), lambda b: (0, BLK_DN + b)), pl.BlockSpec((DN_CONV, DN_DIM), lambda b: (0, b))],
        out_specs=col,
        out_shape=jax.ShapeDtypeStruct((S, 3 * DN_HEADS * DN_DIM), F32),
        scratch_shapes=[pltpu.VMEM((S + CONV_PAD, DN_DIM), F32)],
        compiler_params=_params(("parallel",)),
    )(pd, conv_w)


def _dn_prep_bwd(pd, conv_w, dqkv, dproj, dk, dv):
    S = pd.shape[0]
    NB = 3 * DN_HEADS

    def body(x_ref, w_ref, d_ref, _, dk_ref, dv_ref, dx_ref, dw_ref, scr, dscr):
        b = pl.program_id(0)

        @pl.when(b == NB)
        def _():
            dx_ref[...] = _bf(dk_ref[...])

        @pl.when(b == NB + 1)
        def _():
            dx_ref[...] = _bf(dv_ref[...])

        @pl.when(b < NB)
        def _():
            scr[0:CONV_PAD, :] = jnp.zeros((CONV_PAD, DN_DIM), F32)
            scr[pl.ds(CONV_PAD, S), :] = x_ref[...]
            dscr[pl.ds(S, CONV_PAD), :] = jnp.zeros((CONV_PAD, DN_DIM), F32)
            w = w_ref[...]
            q_scale = jnp.where(b < DN_HEADS, DN_DIM ** -0.5, 1.0)
            is_qk = b < 2 * DN_HEADS
            dw = [jnp.zeros((1, DN_DIM), F32) for _ in range(DN_CONV)]
            for r0 in range(0, S, CONV_ROWS):
                y = _conv_silu(scr, w, r0)
                sg = _sigmoid(y)
                a = y * sg
                dout = d_ref[pl.ds(r0, CONV_ROWS), :]
                rs = lax.rsqrt(jnp.sum(a * a, axis=1, keepdims=True) + EPS)
                da_qk = q_scale * rs * (dout - a * (rs * rs) * jnp.sum(dout * a, axis=1, keepdims=True))
                dy = jnp.where(is_qk, da_qk, dout) * (sg * (1.0 + y * (1.0 - sg)))
                dscr[pl.ds(r0, CONV_ROWS), :] = dy
                for j in range(DN_CONV):
                    dw[j] = dw[j] + jnp.sum(dy * scr[pl.ds(CONV_PAD + r0 - 3 + j, CONV_ROWS), :], axis=0, keepdims=True)
            for j in range(DN_CONV):
                dw_ref[j:j + 1, :] = dw[j]
            for r0 in range(0, S, CONV_ROWS):
                dx = w[3:4, :] * dscr[pl.ds(r0, CONV_ROWS), :]
                for j in range(DN_CONV - 1):
                    dx = dx + w[j:j + 1, :] * dscr[pl.ds(r0 + 3 - j, CONV_ROWS), :]
                dx_ref[pl.ds(r0, CONV_ROWS), :] = _bf(dx)

    own = lambda b: jnp.minimum(b, NB - 1)
    col = pl.BlockSpec((S, DN_DIM), lambda b: (0, own(b)))
    proj_col = pl.BlockSpec((S, DN_DIM), lambda b: (0, BLK_DN + own(b)))
    wcol = pl.BlockSpec((DN_CONV, DN_DIM), lambda b: (0, own(b)))
    whole = pl.BlockSpec((S, DN_DIM), lambda b: (0, 0))
    assert BLK_K == BLK_DN + NB and BLK_V == BLK_K + 1
    return pl.pallas_call(
        body, grid=(NB + 2,), name="dn_prep_bwd",
        in_specs=[proj_col, wcol, col, pl.BlockSpec(memory_space=pl.ANY), whole, whole],
        out_specs=[pl.BlockSpec((S, DN_DIM), lambda b: (0, BLK_DN + b)), wcol],
        out_shape=[jax.ShapeDtypeStruct(dproj.shape, dproj.dtype), jax.ShapeDtypeStruct((DN_CONV, 3 * DN_HEADS * DN_DIM), F32)],
        scratch_shapes=[pltpu.VMEM((S + CONV_PAD, DN_DIM), F32), pltpu.VMEM((S + CONV_PAD, DN_DIM), F32)],
        input_output_aliases={3: 0},
        compiler_params=_params(("arbitrary",)),
    )(pd, conv_w, dqkv, dproj, dk, dv)


CPAD = 128
CHUNKS_LOCAL = 4
CHUNKS_SCAN = 8


def _chunk_masks():
    ii = lax.broadcasted_iota(jnp.int32, (DN_CHUNK, CPAD), 0)
    jj = lax.broadcasted_iota(jnp.int32, (DN_CHUNK, CPAD), 1)
    return ii, jj


def _rows_pad(a):
    return jnp.concatenate([a, jnp.zeros_like(a)], axis=0)


def _hi_lo(a):
    hi = _bf(a)
    return hi, _bf(a - hi.astype(F32))


def _double_step(t, p):
    C = DN_CHUNK
    th, tl = _hi_lo(t)
    ph, pl_ = _hi_lo(p)
    r1 = _dot(jnp.concatenate([th, tl, ph, pl_], axis=0), _rows_pad(ph))
    r2 = _dot(jnp.concatenate([th, ph], axis=0), _rows_pad(pl_))
    return t + (r1[:C] + r1[C:2 * C] + r2[:C]), r1[2 * C:3 * C] + r1[3 * C:] + r2[C:]


def _dot3_nt(a, b):
    C = DN_CHUNK
    ah, al = _hi_lo(a)
    bh, bl = _hi_lo(b)
    r1 = _dot_nt(jnp.concatenate([ah, al], axis=0), _rows_pad(bh))
    return r1[:C] + r1[C:] + _dot_nt(ah, _rows_pad(bl))


def _dot3_tn(a, b):
    C = DN_CHUNK
    ah, al = _hi_lo(a)
    bh, bl = _hi_lo(b)
    return _dot_tn(jnp.concatenate([ah, al, ah], axis=0), jnp.concatenate([bh, bh, bl], axis=0))[:C]


def _interleave(programs):
    programs = list(programs)
    while programs:
        alive = []
        for prog in programs:
            try:
                next(prog)
                alive.append(prog)
            except StopIteration:
                pass
        programs = alive


def _col_to_row(col, ii, jj):
    return jnp.sum(jnp.where(ii == jj, col, 0.0), axis=0, keepdims=True)


def _row_to_col(row, ii, jj):
    return jnp.sum(jnp.where(ii == jj, row, 0.0), axis=1, keepdims=True)


def _decay(gc_col, ii, jj):
    diff = gc_col - _col_to_row(gc_col, ii, jj)
    return jnp.where(jj <= ii, jnp.exp(jnp.where(jj <= ii, diff, 0.0)), 0.0)


def _softplus(x):
    return jnp.maximum(x, 0.0) + jnp.log(1.0 + jnp.exp(-jnp.abs(x)))


def _head(h):
    return slice(DN_DIM * h, DN_DIM * (h + 1))


def _dn_chunk_fwd(qkv, pg, a_log, dt_bias):
    S = qkv.shape[0]
    C = DN_CHUNK
    G = CHUNKS_LOCAL
    R = G * C
    steps = S // R

    def body(alog_ref, dtb_ref, qkv_ref, pg_ref, w_ref, u_ref, qg_ref, kd_ref, a_ref, t_ref, gcs_ref):
        ii, jj = _chunk_masks()
        lane = lax.broadcasted_iota(jnp.int32, (1, 128), 1)
        eye = (ii == jj).astype(F32)
        gcs_parts = [[] for _ in range(G)]

        def head_program(chunk, h):
            rows = slice(chunk * C, (chunk + 1) * C)
            q, k, v = qkv_ref[rows, _head(h)], qkv_ref[rows, _head(DN_HEADS + h)], qkv_ref[rows, _head(2 * DN_HEADS + h)]
            beta = _sigmoid(pg_ref[rows, h:h + 1])
            g_col = -jnp.exp(alog_ref[h]) * _softplus(pg_ref[rows, DN_HEADS + h:DN_HEADS + h + 1] + dtb_ref[h])
            g_row = _col_to_row(g_col, ii, jj)
            gc_col = jnp.sum(jnp.where(jj <= ii, g_row, 0.0), axis=1, keepdims=True)
            dec = _decay(gc_col, ii, jj)
            eg = jnp.exp(gc_col)
            kb, vb = k * beta, v * beta
            k_rows = _rows_pad(_bf(k))
            kk = _dot_nt(_bf(kb), k_rows)
            qk = _dot_nt(_bf(q), k_rows)
            yield
            t, pw = eye, -jnp.where(jj < ii, kk * dec, 0.0)
            for _ in range(6):
                t, pw = _double_step(t, pw)
                yield
            tb = _bf(t)
            u_ref[rows, _head(h)] = _dot(tb, _rows_pad(_bf(vb)))
            w_ref[rows, _head(h)] = _bf(_dot(tb, _rows_pad(_bf(kb * eg))))
            a_ref[h, rows] = _bf(qk * dec)
            t_ref[h, rows] = t
            qg_ref[rows, _head(h)] = _bf(q * eg)
            kd_ref[rows, _head(h)] = _bf(k * jnp.exp(gc_col[C - 1:C, :] - gc_col))
            gcs_parts[chunk].append(jnp.where(lane == h, gc_col, 0.0) + jnp.where(lane == DN_HEADS + h, beta, 0.0)
                                    + jnp.where(lane == 2 * DN_HEADS + h, g_col, 0.0))

        _interleave(head_program(chunk, h) for chunk in range(G) for h in range(DN_HEADS))
        for chunk in range(G):
            gcs_ref[chunk * C:(chunk + 1) * C, :] = sum(gcs_parts[chunk][1:], gcs_parts[chunk][0])

    smem = pl.BlockSpec(memory_space=pltpu.SMEM)
    wide = pl.BlockSpec((R, 512), lambda n: (n, 0))
    sq = pl.BlockSpec((DN_HEADS, R, CPAD), lambda n: (0, n, 0))
    narrow = pl.BlockSpec((R, 128), lambda n: (n, 0))
    f = lambda *shp: jax.ShapeDtypeStruct(shp, F32)
    b = lambda *shp: jax.ShapeDtypeStruct(shp, BF16)
    return pl.pallas_call(
        body, grid=(steps,), name="dn_chunk_fwd",
        in_specs=[smem, smem, pl.BlockSpec((R, 1536), lambda n: (n, 0)), pl.BlockSpec((R, 128), lambda n: (n, BLK_G))],
        out_specs=[wide, wide, wide, wide, sq, sq, narrow],
        out_shape=[b(S, 512), f(S, 512), b(S, 512), b(S, 512), b(DN_HEADS, S, CPAD), f(DN_HEADS, S, CPAD), f(S, 128)],
        compiler_params=_params(("parallel",)),
    )(a_log, dt_bias, qkv, pg)


def _gated_norm(o, z, gn):
    r, oh = _rms_stats(o)
    return oh * gn * (z * _sigmoid(z))


def _dn_scan_fwd(w, u, qg, kd, a, gcs, pz, gn):
    S = w.shape[0]
    C = DN_CHUNK
    nc = S // C
    G = CHUNKS_SCAN
    R = G * C

    def body(w_ref, u_ref, qg_ref, kd_ref, a_ref, gcs_ref, z_ref, gn_ref, o_ref, vn_ref, sst_ref, out_ref, state):
        @pl.when(pl.program_id(0) == 0)
        def _():
            state[...] = jnp.zeros_like(state)

        def head_program(chunk, h):
            hs = _head(h)
            rows = slice(chunk * C, (chunk + 1) * C)
            s_in = state[h]
            sb = _bf(s_in)
            sst_ref[chunk, h] = sb
            w_s = _dot(w_ref[rows, hs], sb)
            q_s = _dot(qg_ref[rows, hs], sb)
            yield
            vn = u_ref[rows, hs] - w_s
            vnb = _bf(vn)
            o = q_s + _dot(a_ref[h, rows], _rows_pad(vnb))
            k_v = _dot_tn(kd_ref[rows, hs], vnb)
            yield
            state[h] = s_in * jnp.exp(gcs_ref[(chunk + 1) * C - 1:(chunk + 1) * C, h:h + 1]) + k_v
            o_ref[rows, hs] = o
            vn_ref[rows, hs] = vnb
            out_ref[rows, hs] = _bf(_gated_norm(o, z_ref[rows, hs], gn_ref[...]))

        for chunk in range(G):
            _interleave(head_program(chunk, h) for h in range(DN_HEADS))

    wide = pl.BlockSpec((R, 512), lambda n: (n, 0))
    f = lambda *shp: jax.ShapeDtypeStruct(shp, F32)
    b = lambda *shp: jax.ShapeDtypeStruct(shp, BF16)
    return pl.pallas_call(
        body, grid=(nc // G,), name="dn_scan_fwd",
        in_specs=[wide, wide, wide, wide, pl.BlockSpec((DN_HEADS, R, CPAD), lambda n: (0, n, 0)),
                  pl.BlockSpec((R, 128), lambda n: (n, 0)), pl.BlockSpec((R, 512), lambda n: (n, BLK_Z)),
                  pl.BlockSpec((1, DN_DIM), lambda n: (0, 0))],
        out_specs=[wide, wide, pl.BlockSpec((G, DN_HEADS, DN_DIM, DN_DIM), lambda n: (n, 0, 0, 0)), wide],
        out_shape=[f(S, 512), b(S, 512), b(nc, DN_HEADS, DN_DIM, DN_DIM), b(S, 512)],
        scratch_shapes=[pltpu.VMEM((DN_HEADS, DN_DIM, DN_DIM), F32)],
        compiler_params=_params(("arbitrary",)),
    )(w, u, qg, kd, a, gcs, pz, gn)


def _dn_scan_bwd(dcat, o, pz, gn, sst, vnew, w, qg, kd, a, gcs, dproj):
    S = o.shape[0]
    C = DN_CHUNK
    G = CHUNKS_SCAN
    R = G * C
    steps = S // R

    def body(dy_ref, o_ref, z_ref, gn_ref, sst_ref, vn_ref, w_ref, qg_ref, kd_ref, a_ref, gcs_ref, _,
             du_ref, dw_ref, dqg_ref, dkd_ref, da_ref, dz_ref, dsc_ref, dgn_ref, dstate):
        @pl.when(pl.program_id(0) == 0)
        def _():
            dstate[...] = jnp.zeros_like(dstate)
            dgn_ref[...] = jnp.zeros_like(dgn_ref)

        gn_ = gn_ref[...]
        lane = lax.broadcasted_iota(jnp.int32, (C, 128), 1)
        row = lax.broadcasted_iota(jnp.int32, (C, 128), 0)
        dgn_parts = []

        def head_program(chunk, h, dsc_parts):
            hs = _head(h)
            rows = slice(chunk * C, (chunk + 1) * C)
            ov, z, dout = o_ref[rows, hs], z_ref[rows, hs], dy_ref[rows, hs]
            r, oh = _rms_stats(ov)
            sg = _sigmoid(z)
            don = dout * (z * sg)
            dz_ref[rows, hs] = _bf(dout * (oh * gn_) * (sg * (1.0 + z * (1.0 - sg))))
            dgn_parts.append(jnp.sum(don * oh, axis=0, keepdims=True))
            dn = don * gn_
            do = _bf(r * (dn - oh * jnp.mean(dn * oh, axis=-1, keepdims=True)))
            sb = sst_ref[chunk, h]
            s_in = sb.astype(F32)
            ds_out = dstate[h]
            dsb = _bf(ds_out)
            vnb = vn_ref[rows, hs]
            wb, qgb, kdb, ab = w_ref[rows, hs], qg_ref[rows, hs], kd_ref[rows, hs], a_ref[h, rows]
            dvn = _dot_tn(ab, do)[:C] + _dot(kdb, dsb)
            yield
            da_ref[h, rows] = _dot_nt(do, _rows_pad(vnb))
            dqg_ref[rows, hs] = _dot_nt(do, sb)
            dkd_ref[rows, hs] = _dot_nt(vnb, dsb)
            q_do = _dot_tn(qgb, do)
            yield
            dvnb = _bf(dvn)
            dw_ref[rows, hs] = _bf(-_dot_nt(dvnb, sb))
            w_dvn = _dot_tn(wb, dvnb)
            du_ref[rows, hs] = dvnb
            yield
            d_last = jnp.exp(gcs_ref[(chunk + 1) * C - 1:(chunk + 1) * C, h:h + 1])
            dd = jnp.sum(jnp.sum(ds_out * s_in, axis=1, keepdims=True), axis=0, keepdims=True)
            dsc_parts.append(jnp.where((lane == h) & (row == C - 1), dd * d_last, 0.0))
            dstate[h] = ds_out * d_last + q_do - w_dvn

        for chunk in reversed(range(G)):
            dsc_parts = []
            _interleave(head_program(chunk, h, dsc_parts) for h in range(DN_HEADS))
            dsc_ref[chunk * C:(chunk + 1) * C, :] = sum(dsc_parts[1:], dsc_parts[0])
        dgn_ref[...] += sum(dgn_parts[1:], dgn_parts[0])

    rev = lambda n: steps - 1 - n
    wide = pl.BlockSpec((R, 512), lambda n: (rev(n), 0))
    z_spec = pl.BlockSpec((R, 512), lambda n: (rev(n), BLK_Z))
    sq = pl.BlockSpec((DN_HEADS, R, CPAD), lambda n: (0, rev(n), 0))
    narrow = pl.BlockSpec((R, 128), lambda n: (rev(n), 0))
    gn_spec = pl.BlockSpec((1, DN_DIM), lambda n: (0, 0))
    f = lambda *shp: jax.ShapeDtypeStruct(shp, F32)
    b = lambda *shp: jax.ShapeDtypeStruct(shp, BF16)
    return pl.pallas_call(
        body, grid=(steps,), name="dn_scan_bwd",
        in_specs=[pl.BlockSpec((R, 512), lambda n: (rev(n), 1)), wide, z_spec, gn_spec,
                  pl.BlockSpec((G, DN_HEADS, DN_DIM, DN_DIM), lambda n: (rev(n), 0, 0, 0)),
                  wide, wide, wide, wide, sq, narrow, pl.BlockSpec(memory_space=pl.ANY)],
        out_specs=[wide, wide, wide, wide, sq, z_spec, narrow, gn_spec],
        out_shape=[b(S, 512), b(S, 512), f(S, 512), f(S, 512), f(DN_HEADS, S, CPAD),
                   jax.ShapeDtypeStruct(dproj.shape, dproj.dtype), f(S, 128), f(1, DN_DIM)],
        scratch_shapes=[pltpu.VMEM((DN_HEADS, DN_DIM, DN_DIM), F32)],
        input_output_aliases={11: 5},
        compiler_params=_params(("arbitrary",)),
    )(dcat, o, pz, gn, sst, vnew, w, qg, kd, a, gcs, dproj)


def _dn_chunk_bwd(qkv, pg, t_inv, gcs, du, dw, dqg, dkd, da, dsc, a_log, dt_bias, dproj):
    S = qkv.shape[0]
    C = DN_CHUNK
    G = CHUNKS_LOCAL
    R = G * C

    def body(alog_ref, dtb_ref, qkv_ref, pg_ref, t_ref, gcs_ref, du_ref, dw_ref, dqg_ref, dkd_ref, da_ref, dsc_ref, _,
             dqkv_ref, dpg_ref, acc_ref):
        @pl.when(pl.program_id(0) == 0)
        def _():
            acc_ref[...] = jnp.zeros_like(acc_ref)

        ii, jj = _chunk_masks()
        lane = lax.broadcasted_iota(jnp.int32, (1, 128), 1)
        row8 = lax.broadcasted_iota(jnp.int32, (8, 128), 0)
        lane8 = lax.broadcasted_iota(jnp.int32, (8, 128), 1)
        rowc = lax.broadcasted_iota(jnp.int32, (C, 1), 0)
        tril, strict = jj <= ii, jj < ii
        dpg_parts, acc_parts = [[] for _ in range(G)], []

        def head_program(chunk, h):
            rows = slice(chunk * C, (chunk + 1) * C)
            q, k, v = qkv_ref[rows, _head(h)], qkv_ref[rows, _head(DN_HEADS + h)], qkv_ref[rows, _head(2 * DN_HEADS + h)]
            gc_col, beta, g_col = gcs_ref[rows, h:h + 1], gcs_ref[rows, DN_HEADS + h:DN_HEADS + h + 1], \
                gcs_ref[rows, 2 * DN_HEADS + h:2 * DN_HEADS + h + 1]
            dec = _decay(gc_col, ii, jj)
            eg = jnp.exp(gc_col)
            g_last = gc_col[C - 1:C, :]
            ek = jnp.exp(g_last - gc_col)
            kb, vb = k * beta, v * beta
            kbg = kb * eg
            qb, kbb = _bf(q), _bf(kb)
            k_rows = _rows_pad(_bf(k))
            t = t_ref[h, rows]
            tb = _bf(t)
            dub, dwb = du_ref[rows, _head(h)], dw_ref[rows, _head(h)]
            dqg_, dkd_ = dqg_ref[rows, _head(h)], dkd_ref[rows, _head(h)]
            dt = _dot_nt(dub, _rows_pad(_bf(vb))) + _dot_nt(dwb, _rows_pad(_bf(kbg)))
            t_du_dw = _dot_tn(tb, jnp.concatenate([dub, dwb], axis=1))
            dvb, dkbg = t_du_dw[:C, :DN_DIM], t_du_dw[:C, DN_DIM:]
            kk = _dot_nt(kbb, k_rows)
            qk = _dot_nt(qb, k_rows)
            yield
            dt_t = _dot3_nt(dt, t)
            yield
            dl = -_dot3_tn(t, dt_t)
            yield
            dm = jnp.where(strict, dl * dec, 0.0)
            dqk = jnp.where(tril, da_ref[h, rows] * dec, 0.0)
            gmat = dm * kk + dqk * qk
            dgc = jnp.sum(gmat, axis=1, keepdims=True) - _row_to_col(jnp.sum(gmat, axis=0, keepdims=True), ii, jj)
            dmb, dqkb = _bf(dm), _bf(dqk)
            yield
            dkb = _dot(dmb, k_rows) + dkbg * eg
            dk = _dot_tn(jnp.concatenate([dmb, dqkb], axis=0), jnp.concatenate([kbb, qb], axis=0))[:C] + dkd_ * ek
            dq = _dot(dqkb, k_rows) + dqg_ * eg
            yield
            tk = jnp.sum(dkd_ * k * ek, axis=1, keepdims=True)
            dgc = dgc + jnp.sum(dqg_ * q * eg, axis=1, keepdims=True) - tk + jnp.sum(dkbg * kbg, axis=1, keepdims=True)
            dgl = jnp.sum(tk, axis=0, keepdims=True) + dsc_ref[(chunk + 1) * C - 1:(chunk + 1) * C, h:h + 1]
            dgc = dgc + jnp.where(rowc == C - 1, dgl, 0.0)
            yield
            dk = dk + dkb * beta
            dbeta = jnp.sum(dkb * k, axis=1, keepdims=True) + jnp.sum(dvb * v, axis=1, keepdims=True)
            dqkv_ref[rows, _head(h)] = dq
            dqkv_ref[rows, _head(DN_HEADS + h)] = dk
            dqkv_ref[rows, _head(2 * DN_HEADS + h)] = dvb * beta
            dg_col = jnp.sum(jnp.where(jj >= ii, _col_to_row(dgc, ii, jj), 0.0), axis=1, keepdims=True)
            yield
            db = dbeta * beta * (1.0 - beta)
            da_in = dg_col * (-jnp.exp(alog_ref[h])) * _sigmoid(pg_ref[rows, DN_HEADS + h:DN_HEADS + h + 1] + dtb_ref[h])
            dpg_parts[chunk].append(jnp.where(lane == h, db, 0.0) + jnp.where(lane == DN_HEADS + h, da_in, 0.0))
            acc_parts.append(jnp.where((row8 == 0) & (lane8 == h), jnp.sum(dg_col * g_col, axis=0, keepdims=True), 0.0)
                             + jnp.where((row8 == 1) & (lane8 == h), jnp.sum(da_in, axis=0, keepdims=True), 0.0))

        _interleave(head_program(chunk, h) for chunk in range(G) for h in range(DN_HEADS))
        for chunk in range(G):
            dpg = sum(dpg_parts[chunk][1:], dpg_parts[chunk][0])
            dpg_ref[chunk * C:(chunk + 1) * C, :] = _bf(jnp.concatenate([dpg, jnp.zeros_like(dpg)], axis=1))
        acc_ref[...] += sum(acc_parts[1:], acc_parts[0])

    smem = pl.BlockSpec(memory_space=pltpu.SMEM)
    wide = pl.BlockSpec((R, 512), lambda n: (n, 0))
    sq = pl.BlockSpec((DN_HEADS, R, CPAD), lambda n: (0, n, 0))
    narrow = pl.BlockSpec((R, 128), lambda n: (n, 0))
    qkv_spec = pl.BlockSpec((R, 1536), lambda n: (n, 0))
    f = lambda *shp: jax.ShapeDtypeStruct(shp, F32)
    return pl.pallas_call(
        body, grid=(S // R,), name="dn_chunk_bwd",
        in_specs=[smem, smem, qkv_spec, pl.BlockSpec((R, 128), lambda n: (n, BLK_G)), sq, narrow, wide, wide, wide, wide, sq,
                  narrow, pl.BlockSpec(memory_space=pl.ANY)],
        out_specs=[qkv_spec, pl.BlockSpec((R, 256), lambda n: (n, BLK_G_PAD)), pl.BlockSpec((8, 128), lambda n: (0, 0))],
        out_shape=[f(S, 1536), jax.ShapeDtypeStruct(dproj.shape, dproj.dtype), f(8, 128)],
        input_output_aliases={12: 1},
        compiler_params=_params(("arbitrary",)),
    )(a_log, dt_bias, qkv, pg, t_inv, gcs, du, dw, dqg, dkd, da, dsc, dproj)


_W_IN_SECTIONS = ((0, 0, 512), (2304, 512, 512), (768, 1024, 1536), (512, 2560, 256), (2816, 2816, 8))
_HALF = D_MODEL // 2
_SECTION_ROWS = 256


def _pack_pairs(x):
    bits = lax.bitcast_convert_type(x, jnp.uint32)
    return lax.bitcast_convert_type(bits[:, _HALF:] | (bits[:, :_HALF] >> 16), F32)


def _unpack_pairs(words):
    bits = lax.bitcast_convert_type(words, jnp.uint32)
    return (lax.bitcast_convert_type(bits << 16, F32),
            lax.bitcast_convert_type(bits & jnp.uint32(0xFFFF0000), F32))


def _w_in_to_internal(packed):
    starts = [dst for _, dst, _ in _W_IN_SECTIONS] + [D_IN_PAD]

    def body(x_hbm, o_ref, words, sems):
        copies = [pltpu.make_async_copy(x_hbm.at[pl.ds(src, rows), 0, :], words.at[pl.ds(dst, rows)], sems.at[i])
                  for i, (src, dst, rows) in enumerate(_W_IN_SECTIONS)]
        for cp in copies:
            cp.start()
        words[pl.ds(D_IN, D_IN_PAD - D_IN), :] = jnp.zeros((D_IN_PAD - D_IN, _HALF), F32)
        for i, cp in enumerate(copies):
            cp.wait()
            for r in range(starts[i], starts[i + 1], _SECTION_ROWS):
                for c, h in enumerate(_unpack_pairs(words[pl.ds(r, _SECTION_ROWS), :])):
                    o_ref[pl.ds(r, _SECTION_ROWS), c * _HALF:(c + 1) * _HALF] = _bf(h)

    assert all((b - a) % _SECTION_ROWS == 0 for a, b in zip(starts, starts[1:])) and starts[-2] + _W_IN_SECTIONS[-1][2] == D_IN
    return pl.pallas_call(body, name="w_in_to_internal", out_shape=jax.ShapeDtypeStruct((D_IN_PAD, D_MODEL), BF16),
                          in_specs=[pl.BlockSpec(memory_space=pl.ANY)],
                          scratch_shapes=[pltpu.VMEM((D_IN_PAD, _HALF), F32), pltpu.SemaphoreType.DMA((len(_W_IN_SECTIONS),))],
                          compiler_params=pltpu.CompilerParams(vmem_limit_bytes=VMEM_LIMIT))(packed)


def _w_in_from_internal(gt):
    def body(g_ref, o_hbm, words, sems):
        copies = []
        for i, (dst, src, rows) in enumerate(_W_IN_SECTIONS):
            for r in range(src, src + rows, _SECTION_ROWS):
                n = max(min(_SECTION_ROWS, src + rows - r), 16)
                words[pl.ds(r, n), :] = _pack_pairs(g_ref[pl.ds(r, n), :].astype(F32))
            copies.append(pltpu.make_async_copy(words.at[pl.ds(src, rows)], o_hbm.at[pl.ds(dst, rows), 0, :], sems.at[i]))
            copies[-1].start()
        for cp in copies:
            cp.wait()

    return pl.pallas_call(body, name="w_in_from_internal", out_shape=jax.ShapeDtypeStruct((D_IN, 1, _HALF), F32),
                          out_specs=pl.BlockSpec(memory_space=pl.ANY),
                          scratch_shapes=[pltpu.VMEM((D_IN_PAD, _HALF), F32), pltpu.SemaphoreType.DMA((len(_W_IN_SECTIONS),))],
                          compiler_params=pltpu.CompilerParams(vmem_limit_bytes=VMEM_LIMIT))(gt)


def _local_step(x, p, target, wts, first_weights, other_weights, ship_early, after):
    S = x.shape[0]
    cos, sin = _rope_tables(S)
    sinks, a_log, dt_bias = wts["sinks"].reshape(8), wts["a_log"].reshape(4), wts["dt_bias"].reshape(4)
    gn = wts["dn_norm"].reshape(1, DN_DIM)
    add = lambda acc, res: (acc + res,)

    u = _rmsnorm_fwd(x, wts["norm_mix"], "norm_mix_fwd", after)
    w_in_t, conv_w = first_weights(u)
    proj, = _mm(u, w_in_t, form="nt", name="in_proj", out_dtypes=[F32], tn=512)
    attn, lse = _attn_fwd(proj, cos, sin, sinks)
    qkv = _dn_prep_fwd(proj, conv_w)
    cw, cu, cqg, ckd, ca, ct, gcs = _dn_chunk_fwd(qkv, proj, a_log, dt_bias)
    o, vnew, sst, dn_out = _dn_scan_fwd(cw, cu, cqg, ckd, ca, gcs, proj, gn)
    w_o, = other_weights(("w_o",), dn_out)
    h1, = _mm([attn, dn_out], w_o, form="nn", name="out_proj", out_dtypes=[F32], tn=512, epi=add, extra=[x])

    w_up, w_down = other_weights(("w_up", "w_down"), h1)
    hid, relu, m, h2 = _mlp_fwd(h1, w_up, w_down, wts["norm_mlp"])
    w_pg, w_pp = other_weights(("w_ple_gate", "w_ple_proj"), h2)
    n3, dh2, dgl, dpp, loss, d_norm_final, d_norm_ple = _ple_and_loss(h2, p, target, w_pg, w_pp, wts["norm_ple"],
                                                                     wts["norm_final"].reshape(1, D_MODEL))
    g = {"norm_final": d_norm_final, "norm_ple": d_norm_ple}
    early = {"w_ple_gate": _mm_tn(n3, dgl, name="d_w_ple_gate", tm=512, tn=1024, out_dtype=BF16).reshape(N_DEV, 128, 1024),
             "w_ple_proj": _mm_tn(p, dpp, name="d_w_ple_proj", tm=256, tn=128, out_dtype=BF16, column_shards=True)}
    d_act, = _mm(dh2, w_down, form="nt", name="d_hidden", out_dtypes=[BF16], tn=512,
                 epi=lambda acc, r: (acc * (2.0 * r.astype(F32)),), extra=[relu])
    early["w_down"] = _mm_tn(hid, dh2, name="d_w_down", tm=512, tn=1024, out_dtype=BF16).reshape(N_DEV, 512, 1024)
    early["w_up"] = _mm_tn(m, d_act, name="d_w_up", tm=1024, tn=512, out_dtype=BF16, column_shards=True)
    token = ship_early(early)
    dh1, g["norm_mlp"], dcat = _mm(d_act, w_up, form="nt", name="d_m", out_dtypes=[F32], tn=512, after=token,
                                   norm_bwd=(h1, wts["norm_mlp"], dh2), then_nt=w_o)
    d_w_o = jnp.concatenate([_mm_tn(attn, dh1, name="d_w_o_attn", tm=512, tn=512, out_dtype=BF16),
                             _mm_tn(dn_out, dh1, name="d_w_o_dn", tm=512, tn=512, out_dtype=BF16)], axis=0)
    token = ship_early({"w_o": d_w_o.reshape(N_DEV, 128, 1024)})
    dproj, dk, dv, dsinks = _attn_bwd(proj, cos, sin, sinks, dcat, attn, lse, token)
    g["sinks"] = dsinks[:, 0].reshape(1, 8)
    du_, dw_, dqg, dkd, da, dproj, dsc, g["dn_norm"] = _dn_scan_bwd(dcat, o, proj, gn, sst, vnew, cw, cqg, ckd, ca, gcs, dproj)
    dqkv, dproj, gate_acc = _dn_chunk_bwd(qkv, proj, ct, gcs, du_, dw_, dqg, dkd, da, dsc, a_log, dt_bias, dproj)
    g["a_log"], g["dt_bias"] = gate_acc[0:1, 0:4], gate_acc[1:2, 0:4]
    dproj, g["conv_w"] = _dn_prep_bwd(proj, conv_w, dqkv, dproj, dk, dv)
    token = ship_early({"w_in": _mm_tn(dproj, u, name="d_w_in", tm=512, tn=1024, out_dtype=BF16)})
    grad_x, g["norm_mix"] = _mm(dproj, w_in_t, form="nn", name="d_u", out_dtypes=[F32], tn=512, after=token,
                                norm_bwd=(x, wts["norm_mix"], dh1))
    return loss, grad_x, g


def _peer(k):
    x, y, c = lax.axis_index("x"), lax.axis_index("y"), lax.axis_index("c")
    px = 1 - x if k & 4 else x
    py = 1 - y if k & 2 else y
    pc = 1 - c if k & 1 else c
    return (px, py, pc), 4 * px + 2 * py + pc


def _exchange(srcs, name, gather):
    n = len(srcs)
    gathers = list(gather) if isinstance(gather, (list, tuple)) else [gather] * n
    shapes = [(N_DEV,) + s.shape if gt else s.shape for s, gt in zip(srcs, gathers)]

    def body(*refs):
        src_refs, out_refs = refs[:n], refs[n:2 * n]
        send_sems, recv_sems, local_sems = refs[2 * n:]
        _, me = _peer(0)
        piece = lambda a, d: src_refs[a] if gathers[a] else src_refs[a].at[d]
        local = [pltpu.make_async_copy(piece(a, me), out_refs[a].at[me], local_sems.at[a]) for a in range(n)]
        for cp in local:
            cp.start()
        copies = []
        for a in range(n):
            for k in range(1, N_DEV):
                dev, idx = _peer(k)
                cp = pltpu.make_async_remote_copy(src_ref=piece(a, idx), dst_ref=out_refs[a].at[me],
                                                  send_sem=send_sems.at[a, k - 1], recv_sem=recv_sems.at[a, k - 1],
                                                  device_id=dev, device_id_type=MESH)
                cp.start()
                copies.append(cp)
        for cp in copies:
            cp.wait_recv()
        for cp in copies:
            cp.wait_send()
        for cp in local:
            cp.wait()

    anywhere = pl.BlockSpec(memory_space=pl.ANY)
    return pl.pallas_call(
        body, name=name, in_specs=[anywhere] * n, out_specs=[anywhere] * n,
        out_shape=[jax.ShapeDtypeStruct(shp, s.dtype) for shp, s in zip(shapes, srcs)],
        scratch_shapes=[pltpu.SemaphoreType.DMA((n, N_DEV - 1)), pltpu.SemaphoreType.DMA((n, N_DEV - 1)),
                        pltpu.SemaphoreType.DMA((n,))],
    )(*srcs)


_HBM = pl.BlockSpec(memory_space=pltpu.HBM)
_SEM = pl.BlockSpec(memory_space=pltpu.SEMAPHORE)
_EFFECT = pltpu.SideEffectType.DATAFLOW_SIDE_EFFECTING


def _split_copies(src_refs, land_refs, send_sems, recv_sems, modes, which=None):
    _, me = _peer(0)
    local, remote = [], []
    which = range(len(src_refs)) if which is None else which
    for a, src, land in zip(which, src_refs, land_refs):
        if modes[a] == "columns":
            n_cols = src.shape[1]
            dst = land.at[:, pl.ds(pl.multiple_of(me * n_cols, n_cols), n_cols)]
        else:
            dst = land.at[me]
        part = lambda d: src.at[d] if modes[a] == "pieces" else src
        local.append(pltpu.make_async_copy(part(me), dst, recv_sems.at[a * N_DEV]))
        for k in ((2, 4, 6) if modes[a] == "chips" else range(1, N_DEV)):
            dev, idx = _peer(k)
            sem = a * N_DEV + k
            remote.append(pltpu.make_async_remote_copy(
                src_ref=part(idx), dst_ref=dst, send_sem=send_sems.at[sem], recv_sem=recv_sems.at[sem],
                device_id=dev, device_id_type=MESH))
    return local, remote


def _forward_copies(land_refs, send_sems, recv_sems):
    c = lax.axis_index("c")
    sibling, _ = _peer(1)
    copies = []
    for a, land in enumerate(land_refs):
        for chip in range(N_DEV // 2):
            slot = 2 * chip + c
            sem = a * (N_DEV // 2) + chip
            copies.append(pltpu.make_async_remote_copy(
                src_ref=land.at[slot], dst_ref=land.at[slot], send_sem=send_sems.at[sem], recv_sem=recv_sems.at[sem],
                device_id=sibling, device_id_type=MESH))
    return copies


def _forward_start(lands, name):
    n = len(lands)

    def body(*refs):
        for cp in _forward_copies(refs[:n], refs[n], refs[n + 1]):
            cp.start()
        refs[-1][...] = jnp.zeros_like(refs[-1])

    sems = pltpu.SemaphoreType.DMA((n * (N_DEV // 2),))
    out = pl.pallas_call(
        body, name=name,
        out_shape=(sems, sems, *[pltpu.HBM(t.shape, t.dtype) for t in lands], jax.ShapeDtypeStruct((8, 128), F32)),
        in_specs=[_HBM] * n, out_specs=(_SEM, _SEM, *[_HBM] * n, pl.BlockSpec(memory_space=pltpu.VMEM)),
        input_output_aliases={i: 2 + i for i in range(n)},
        compiler_params=pltpu.CompilerParams(has_side_effects=_EFFECT),
    )(*[pltpu.with_memory_space_constraint(t, pltpu.HBM) for t in lands])
    return out[:-1], out[-1]


def _forward_wait(handle, after, name):
    send_sems, recv_sems, *lands = handle
    n = len(lands)

    def body(*refs):
        for cp in _forward_copies(refs[:n], refs[n], refs[n + 1]):
            cp.wait_send()
            cp.wait_recv()

    return list(pl.pallas_call(
        body, name=name, out_shape=tuple(pltpu.HBM(t.shape, t.dtype) for t in lands),
        in_specs=[_HBM] * n + [_SEM, _SEM, pl.BlockSpec(memory_space=pl.ANY)], out_specs=tuple([_HBM] * n),
        input_output_aliases={i: i for i in range(n)},
        compiler_params=pltpu.CompilerParams(has_side_effects=_EFFECT),
    )(*lands, send_sems, recv_sems, after))


def _exchange_start(srcs, name, modes):
    n = len(srcs)
    modes = [modes] * n if isinstance(modes, str) else list(modes)
    lands = []
    for s, mode in zip(srcs, modes):
        shape = {"columns": (s.shape[0], N_DEV * s.shape[1]), "pieces": s.shape}.get(mode, (N_DEV,) + s.shape)
        lands.append(lax.empty(shape, s.dtype))

    def body(*refs):
        src_refs, land_refs = refs[:n], refs[n:2 * n]
        send_sems, recv_sems = refs[2 * n], refs[2 * n + 1]
        local, remote = _split_copies(src_refs, land_refs, send_sems, recv_sems, modes)
        for cp in local + remote:
            cp.start()
        refs[-1][...] = jnp.zeros_like(refs[-1])

    both = list(srcs) + lands
    sems = pltpu.SemaphoreType.DMA((n * N_DEV,))
    out = pl.pallas_call(
        body, name=name,
        out_shape=(sems, sems, *[pltpu.HBM(t.shape, t.dtype) for t in both], jax.ShapeDtypeStruct((8, 128), F32)),
        in_specs=[_HBM] * (2 * n), out_specs=(_SEM, _SEM, *[_HBM] * (2 * n), pl.BlockSpec(memory_space=pltpu.VMEM)),
        input_output_aliases={i: 2 + i for i in range(2 * n)},
        compiler_params=pltpu.CompilerParams(has_side_effects=_EFFECT),
    )(*[pltpu.with_memory_space_constraint(t, pltpu.HBM) for t in both])
    return (n, modes, out[:-1]), out[-1]


def _exchange_wait(handle, after, name, which=None):
    n_all, modes, (send_sems, recv_sems, *both_all) = handle
    which = list(range(n_all)) if which is None else list(which)
    n = len(which)
    both = [both_all[a] for a in which] + [both_all[n_all + a] for a in which]

    def body(*refs):
        src_refs, land_refs = refs[:n], refs[n:2 * n]
        local, remote = _split_copies(src_refs, land_refs, refs[2 * n], refs[2 * n + 1], modes, which)
        for cp in local:
            cp.wait()
        for cp in remote:
            cp.wait_send()
            cp.wait_recv()

    out = pl.pallas_call(
        body, name=name, out_shape=tuple(pltpu.HBM(t.shape, t.dtype) for t in both),
        in_specs=[_HBM] * (2 * n) + [_SEM, _SEM, pl.BlockSpec(memory_space=pl.ANY)], out_specs=tuple([_HBM] * (2 * n)),
        input_output_aliases={i: i for i in range(2 * n)},
        compiler_params=pltpu.CompilerParams(has_side_effects=_EFFECT),
    )(*both, send_sems, recv_sems, after)
    return list(out[n:])


def _cast_all(arrays):
    def body(*refs):
        for src, dst in zip(refs[:len(arrays)], refs[len(arrays):]):
            if len(src.shape) == 2:
                dst[...] = _bf(src[...])
            else:
                dst[:, 0, :] = _pack_pairs(_bf(src[:, 0, :]).astype(F32))

    shapes = [jax.ShapeDtypeStruct(a.shape, BF16) if a.ndim == 2 else jax.ShapeDtypeStruct((a.shape[0], 1, a.shape[2] // 2), F32)
              for a in arrays]
    return pl.pallas_call(body, name="cast_shards", out_shape=shapes,
                          compiler_params=pltpu.CompilerParams(vmem_limit_bytes=VMEM_LIMIT))(*arrays)


def _adam_update(g, w, m, v):
    nm = ADAM_B1 * m + (1.0 - ADAM_B1) * g
    nv = ADAM_B2 * v + (1.0 - ADAM_B2) * (g * g)
    m_hat = nm / (1.0 - ADAM_B1 ** ADAM_STEP)
    v_hat = nv / (1.0 - ADAM_B2 ** ADAM_STEP)
    return -ADAM_LR * (m_hat / (jnp.sqrt(v_hat) + ADAM_EPS) + ADAM_WD * w), nm, nv


def _adamw(parts, w, m, v, name):
    n, R, W = parts.shape
    tm = 128 if R % 128 == 0 else R

    def body(p_ref, w_ref, m_ref, v_ref, g_ref, d_ref, nm_ref, nv_ref):
        g = p_ref[0].astype(F32)
        for s in range(1, n):
            g = g + p_ref[s].astype(F32)
        g_ref[...] = g
        d_ref[...], nm_ref[...], nv_ref[...] = _adam_update(g, w_ref[...], m_ref[...], v_ref[...])

    tile = pl.BlockSpec((tm, W), lambda i: (i, 0))
    return pl.pallas_call(
        body, grid=(R // tm,), name=name,
        in_specs=[pl.BlockSpec((n, tm, W), lambda i: (0, i, 0)), tile, tile, tile],
        out_specs=[tile] * 4, out_shape=[jax.ShapeDtypeStruct((R, W), F32)] * 4,
        compiler_params=_params(("parallel",)),
    )(parts, w, m, v)


def _adamw_rows_apart(parts, w, m, v, name):
    n, R, _, half = parts.shape

    def body(p_hbm, w_hbm, m_hbm, v_hbm, *rest):
        out_hbm, (words, given, results, sems) = rest[:4], rest[4:]
        loads = [pltpu.make_async_copy(p_hbm.at[s, :, 0, :], words.at[s], sems.at[s]) for s in range(n)]
        loads += [pltpu.make_async_copy(h.at[:, 0, :], given.at[i], sems.at[n + i]) for i, h in enumerate((w_hbm, m_hbm, v_hbm))]
        for cp in loads:
            cp.start()
        for cp in loads:
            cp.wait()
        part = lambda s: jnp.concatenate(_unpack_pairs(words[s]), axis=1)
        g = part(0)
        for s in range(1, n):
            g = g + part(s)
        stores = []
        for i, val in enumerate((g,) + _adam_update(g, given[0], given[1], given[2])):
            results[i] = val
            stores.append(pltpu.make_async_copy(results.at[i], out_hbm[i].at[:, 0, :], sems.at[n + 3 + i]))
            stores[-1].start()
        for cp in stores:
            cp.wait()

    anywhere = pl.BlockSpec(memory_space=pl.ANY)
    return pl.pallas_call(
        body, name=name, in_specs=[anywhere] * 4, out_specs=[anywhere] * 4,
        out_shape=[jax.ShapeDtypeStruct(w.shape, F32)] * 4,
        scratch_shapes=[pltpu.VMEM((n, R, half), F32), pltpu.VMEM((3, R, 2 * half), F32), pltpu.VMEM((4, R, 2 * half), F32),
                        pltpu.SemaphoreType.DMA((n + 7,))],
        compiler_params=pltpu.CompilerParams(vmem_limit_bytes=VMEM_LIMIT),
    )(parts, w, m, v)


_MATRICES = ("w_in", "w_o", "w_up", "w_down", "w_ple_gate", "w_ple_proj")


_OTHERS = ("w_o", "w_up", "w_down", "w_ple_gate", "w_ple_proj")
_OTHER_MODES = {"w_o": "slots", "w_up": "slots", "w_down": "slots", "w_ple_gate": "slots", "w_ple_proj": "columns"}


_VECTORS = ("norm_mix", "norm_mlp", "norm_ple", "norm_final", "a_log", "dt_bias", "sinks", "dn_norm")
_SMALL_ROWS, _LOSS_ROW, _CONV_ROW = 16, 8, 9


def _pack_small(vectors, loss, conv):
    def body(*refs):
        out = refs[-1]
        out[...] = jnp.zeros_like(out)
        for r, ref in enumerate(refs[:len(_VECTORS)]):
            out[r:r + 1, 0:ref.shape[1]] = ref[...]
        out[_LOSS_ROW:_LOSS_ROW + 1, 0:128] = refs[len(_VECTORS)][...]
        out[_CONV_ROW:_CONV_ROW + 6, :] = refs[len(_VECTORS) + 1][...]

    return pl.pallas_call(body, name="pack_small", out_shape=jax.ShapeDtypeStruct((_SMALL_ROWS, 1024), F32))(*vectors, loss, conv)


def _sum_slots(parts):
    def body(p_ref, o_ref):
        acc = p_ref[0]
        for s in range(1, parts.shape[0]):
            acc = acc + p_ref[s]
        o_ref[...] = acc

    return pl.pallas_call(body, name="sum_small", out_shape=jax.ShapeDtypeStruct(parts.shape[1:], parts.dtype))(parts)


def _adamw_vectors(summed, conv_g, wmv):
    names = _VECTORS + ("conv_w",)
    flat = [a for triple in wmv for a in triple]

    def body(*refs):
        sum_ref, conv_ref = refs[0], refs[1]
        ins, outs = refs[2:2 + len(flat)], refs[2 + len(flat):]
        for i in range(len(names)):
            w_ref, m_ref, v_ref = ins[3 * i:3 * i + 3]
            g = conv_ref[...] if i == len(_VECTORS) else sum_ref[i:i + 1, 0:w_ref.shape[1]]
            outs[4 * i][...] = g
            outs[4 * i + 1][...], outs[4 * i + 2][...], outs[4 * i + 3][...] = _adam_update(g, w_ref[...], m_ref[...], v_ref[...])

    out_shape = [jax.ShapeDtypeStruct(t[0].shape, F32) for t in wmv for _ in range(4)]
    res = pl.pallas_call(body, name="adamw_vectors", out_shape=out_shape)(summed, conv_g, *flat)
    return {n: res[4 * i:4 * i + 4] for i, n in enumerate(names)}


_ORDER = ("norm_mix", "w_in", "conv_w", "a_log", "dt_bias", "dn_norm", "sinks", "w_o", "norm_mlp", "w_up", "w_down",
          "norm_ple", "w_ple_gate", "w_ple_proj", "norm_final")


def kernel(x, p, norm_mix, w_in, conv_w, a_log, dt_bias, dn_norm, sinks, w_o, norm_mlp, w_up, w_down, norm_ple, w_ple_gate, w_ple_proj, norm_final, loss_target, m_norm_mix, m_w_in, m_conv_w, m_a_log, m_dt_bias, m_dn_norm, m_sinks, m_w_o, m_norm_mlp, m_w_up, m_w_down, m_norm_ple, m_w_ple_gate, m_w_ple_proj, m_norm_final, v_norm_mix, v_w_in, v_conv_w, v_a_log, v_dt_bias, v_dn_norm, v_sinks, v_w_o, v_norm_mlp, v_w_up, v_w_down, v_norm_ple, v_w_ple_gate, v_w_ple_proj, v_norm_final):
    w = dict(norm_mix=norm_mix, w_in=w_in, conv_w=conv_w[0], a_log=a_log, dt_bias=dt_bias, dn_norm=dn_norm, sinks=sinks,
             w_o=w_o[0], norm_mlp=norm_mlp, w_up=w_up[0], w_down=w_down[0], norm_ple=norm_ple, w_ple_gate=w_ple_gate[0],
             w_ple_proj=w_ple_proj[0], norm_final=norm_final)
    m = dict(norm_mix=m_norm_mix, w_in=m_w_in, conv_w=m_conv_w[0], a_log=m_a_log, dt_bias=m_dt_bias, dn_norm=m_dn_norm,
             sinks=m_sinks, w_o=m_w_o[0], norm_mlp=m_norm_mlp, w_up=m_w_up[0], w_down=m_w_down[0], norm_ple=m_norm_ple,
             w_ple_gate=m_w_ple_gate[0], w_ple_proj=m_w_ple_proj[0], norm_final=m_norm_final)
    v = dict(norm_mix=v_norm_mix, w_in=v_w_in, conv_w=v_conv_w[0], a_log=v_a_log, dt_bias=v_dt_bias, dn_norm=v_dn_norm,
             sinks=v_sinks, w_o=v_w_o[0], norm_mlp=v_norm_mlp, w_up=v_w_up[0], w_down=v_w_down[0], norm_ple=v_norm_ple,
             w_ple_gate=v_w_ple_gate[0], w_ple_proj=v_w_ple_proj[0], norm_final=v_norm_final)
    me = 4 * lax.axis_index("x") + 2 * lax.axis_index("y") + lax.axis_index("c")
    conv_shard = conv_w.shape[2]

    for d in (w, m, v):
        d["w_in"] = jnp.transpose(d["w_in"], (2, 0, 1))
    conv_pad = jnp.pad(w["conv_w"], ((0, 8 - DN_CONV), (0, 256 - conv_shard)))
    shards = _cast_all([w[n] for n in ("w_in",) + _OTHERS])
    gathers, token_gather = _exchange_start([shards[0], conv_pad] + list(shards[1:]), "gather_start",
                                            ["chips", "chips"] + [_OTHER_MODES[n] for n in _OTHERS])

    def first_weights(after):
        over_ici = _exchange_wait(gathers, after, "gather_first_wait", [0, 1])
        handle, token = _forward_start(over_ici, "gather_first_forward")
        w_in_all, conv_all = _forward_wait(handle, token, "gather_first_forward_wait")
        conv_all = jnp.transpose(conv_all[:, :DN_CONV, :conv_shard], (1, 0, 2)).reshape(DN_CONV, N_DEV * conv_shard)
        return _w_in_to_internal(w_in_all.reshape(D_IN, 1, _HALF)), conv_all

    as_taken = {"w_o": lambda t: t.reshape(1024, 1024), "w_up": lambda t: t, "w_down": lambda t: t.reshape(4096, 1024),
                "w_ple_gate": lambda t: t.reshape(1024, 1024), "w_ple_proj": lambda t: t}

    def other_weights(names, after):
        which = [2 + _OTHERS.index(n) for n in names]
        got = _exchange_wait(gathers, after, "gather_wait_" + names[0], which)
        return [as_taken[n](t) for n, t in zip(names, got)]

    shipped = []

    def ship_early(pieces):
        names = tuple(pieces)
        if names == ("w_in",):
            pieces = {"w_in": _w_in_from_internal(pieces["w_in"]).reshape(N_DEV, D_IN // N_DEV, 1, _HALF)}
        handle, token = _exchange_start([pieces[n] for n in names], "scatter_start_" + names[0], "pieces")
        shipped.append((names, handle))
        return token

    loss, grad_x, g = _local_step(x[0], p[0, 0], loss_target[0], w, first_weights, other_weights, ship_early, token_gather)

    row = lambda t: t.reshape(1, t.size)
    small = _pack_small([row(g[n]) for n in _VECTORS], loss, g["conv_w"].reshape(6, 1024))
    small_handle, token_small = _exchange_start([small], "gather_small_start", "slots")
    big, after = {}, token_small
    for names, handle in shipped[:-1]:
        for n, r in zip(names, _exchange_wait(handle, after, "scatter_wait_" + names[0])):
            big[n] = _adamw(r, w[n], m[n], v[n], "adamw_" + n)
            after = big[n][1]
    small_all, = _exchange_wait(small_handle, after, "gather_small_wait")
    summed = _sum_slots(small_all)
    conv_g = lax.dynamic_slice(summed[_CONV_ROW:_CONV_ROW + 6].reshape(DN_CONV, N_DEV * conv_shard), (0, me * conv_shard),
                               (DN_CONV, conv_shard))
    small_out = _adamw_vectors(summed, conv_g, [(row(w[n]), row(m[n]), row(v[n])) for n in _VECTORS]
                               + [(w["conv_w"], m["conv_w"], v["conv_w"])])
    names, handle = shipped[-1]
    for n, r in zip(names, _exchange_wait(handle, small_out["conv_w"][0], "scatter_wait_" + names[0])):
        big[n] = _adamw_rows_apart(r, w[n], m[n], v[n], "adamw_" + n)

    result = [summed[_LOSS_ROW, 0], grad_x[None]]
    for i in range(4):
        for n in _ORDER:
            if n == "w_in":
                result.append(jnp.transpose(big[n][i], (1, 2, 0)))
            elif n in _MATRICES:
                result.append(big[n][i][None])
            elif n == "conv_w":
                result.append(small_out[n][i][None])
            else:
                result.append(small_out[n][i].reshape(w[n].shape))
    return tuple(result)
```

```python
import jax
import jax.numpy as jnp
import numpy as np
from jax import lax
from jax.experimental import pallas as pl
from jax.experimental.pallas import tpu as pltpu

F32, BF16 = jnp.float32, jnp.bfloat16
EPS = 1e-6
D_MODEL = 1024
N_DEV = 8
ATTN_BLOCK = 128
HEAD_PAIR = 128
DN_HEADS = 4
DN_DIM = 128
DN_CHUNK = 64
DN_CONV = 4
ROPE_THETA = 10000.0
D_IN = 2824
D_IN_PAD = 3072
BLK_Q, BLK_Z = 0, 1
BLK_DN, BLK_K, BLK_V, BLK_G = 8, 20, 21, 22
BLK_G_PAD = 11
VMEM_LIMIT = 56 * 1024 * 1024
NEG = -1e30
ADAM_LR, ADAM_B1, ADAM_B2, ADAM_EPS, ADAM_WD, ADAM_STEP = 0.001, 0.9, 0.999, 1e-08, 0.01, 10
MESH = pl.DeviceIdType.MESH


def _bf(x):
    return x.astype(BF16)


def _dot(a, b):
    return jnp.dot(a, b, preferred_element_type=F32)


def _dot_nt(a, b):
    return lax.dot_general(a, b, (((1,), (1,)), ((), ())), preferred_element_type=F32)


def _dot_tn(a, b):
    return lax.dot_general(a, b, (((0,), (0,)), ((), ())), preferred_element_type=F32)


def _sigmoid(x):
    return 1.0 / (1.0 + jnp.exp(-x))


def _params(sem):
    return pltpu.CompilerParams(dimension_semantics=sem, vmem_limit_bytes=VMEM_LIMIT)


def _mm(x, w, *, form, name, out_dtypes, tn, epi=None, extra=(), tm=512, w_row_block=0, after=None, norm=None,
        norm_bwd=None, then_nt=None):
    assert norm is None or norm_bwd is None
    xs = list(x) if isinstance(x, (list, tuple)) else [x]
    nx = len(xs)
    S, K = xs[0].shape
    shards = w.ndim == 3
    N = (w.shape[2] * N_DEV if shards else w.shape[1]) if form == "nn" else w.shape[-2]
    assert not (shards and form == "nn" and tn != w.shape[2]) and (nx == 1 or (form == "nn" and not shards and norm is None))
    r0 = w_row_block * K
    tm = min(tm, S)
    n_extra, n_out = len(extra), len(out_dtypes)
    tile = lambda width: pl.BlockSpec((tm, width), lambda i: (i, 0))
    whole = lambda a: pl.BlockSpec(a.shape, lambda i, nd=a.ndim: (0,) * nd)
    ins, in_specs = [*xs, w, *extra], [tile(K)] * nx + [whole(w)] + [tile(N)] * n_extra
    if norm is not None:
        ins, in_specs = ins + [norm], in_specs + [whole(norm)]
    if norm_bwd is not None:
        ins, in_specs = ins + list(norm_bwd), in_specs + [tile(N), whole(norm_bwd[1]), tile(N)]
    if then_nt is not None:
        ins, in_specs = ins + [then_nt], in_specs + [whole(then_nt)]
    if after is not None:
        ins, in_specs = ins + [after], in_specs + [whole(after)]
    out_shape = [jax.ShapeDtypeStruct((S, N), dt) for dt in out_dtypes]
    out_specs = [tile(N)] * n_out
    if norm is not None:
        out_shape, out_specs = out_shape + [jax.ShapeDtypeStruct((S, K), BF16)], out_specs + [tile(K)]
    if norm_bwd is not None:
        out_shape, out_specs = out_shape + [jax.ShapeDtypeStruct((1, N), F32)], out_specs + [pl.BlockSpec((1, N), lambda i: (0, 0))]
    if then_nt is not None:
        out_shape, out_specs = out_shape + [jax.ShapeDtypeStruct((S, then_nt.shape[0]), F32)], out_specs + [tile(then_nt.shape[0])]

    def product(xb, w_ref, cols, c):
        if form == "nn" and nx > 1:
            return sum(_dot(part, w_ref[r0 + p * K:r0 + (p + 1) * K, cols]) for p, part in enumerate(xb))
        if form == "nn":
            return _dot(xb, w_ref[c] if shards else w_ref[r0:r0 + K, cols])
        if not shards:
            return _dot_nt(xb, w_ref[cols, :])
        ks = w.shape[2]
        acc = _dot_nt(xb[:, 0:ks], w_ref[0, cols, :])
        for s in range(1, N_DEV):
            acc = acc + _dot_nt(xb[:, s * ks:(s + 1) * ks], w_ref[s, cols, :])
        return acc

    def body(*refs):
        x_ref, w_ref = refs[0], refs[nx]
        extra_refs = refs[nx + 1:nx + 1 + n_extra]
        at = nx + 1 + n_extra
        if norm is not None:
            gain_ref, at = refs[at], at + 1
        if norm_bwd is not None:
            (y_ref, ygain_ref, dres_ref), at = refs[at:at + 3], at + 3
        if then_nt is not None:
            w2_ref, at = refs[at], at + 1
        outs = refs[len(ins):]
        if norm is not None:
            _, xh = _rms_stats(x_ref[...])
            xb = _bf(xh * gain_ref[...])
            outs[n_out][...] = xb
        else:
            xb = _bf(x_ref[...]) if nx == 1 else [_bf(r[...]) for r in refs[:nx]]
        for c in range(N // tn):
            cols = slice(c * tn, (c + 1) * tn)
            acc = product(xb, w_ref, cols, c)
            res = epi(acc, *[r[:, cols] for r in extra_refs]) if epi else (acc,)
            for o, r in zip(outs[:n_out], res):
                o[:, cols] = r.astype(o.dtype)
        if norm_bwd is not None:
            dx, dg = _rms_bwd_tile(y_ref[...], ygain_ref[...], outs[0][...])
            outs[0][...] = dres_ref[...] + dx
            dg_ref = outs[n_out]

            @pl.when(pl.program_id(0) == 0)
            def _():
                dg_ref[...] = jnp.zeros_like(dg_ref)

            dg_ref[...] += dg
        if then_nt is not None:
            yb = _bf(outs[0][...])
            for c in range(then_nt.shape[0] // tn):
                cols = slice(c * tn, (c + 1) * tn)
                outs[-1][:, cols] = _dot_nt(yb, w2_ref[cols, :])

    return pl.pallas_call(
        body, grid=(S // tm,), name=name, in_specs=in_specs, out_specs=out_specs, out_shape=out_shape,
        compiler_params=_params(("arbitrary",) if norm_bwd is not None else ("parallel",)),
    )(*ins)


def _mlp_fwd(h1, w_up, w_down, gain):
    S, K = h1.shape
    n_sh, _, fs = w_up.shape
    tm = min(512, S)

    def body(x_ref, wup_ref, wdown_ref, g_ref, hid_ref, relu_ref, m_ref, h2_ref):
        x = x_ref[...]
        _, xh = _rms_stats(x)
        mb = _bf(xh * g_ref[...])
        m_ref[...] = mb
        h2_ref[...] = x
        for c in range(n_sh):
            cols = slice(c * fs, (c + 1) * fs)
            r = jnp.maximum(_dot(mb, wup_ref[c]), 0.0)
            hd = _bf(r * r)
            hid_ref[:, cols] = hd
            relu_ref[:, cols] = _bf(r)
            h2_ref[...] += _dot(hd, wdown_ref[cols, :])

    tile = lambda width: pl.BlockSpec((tm, width), lambda i: (i, 0))
    once = lambda a: pl.BlockSpec(a.shape, lambda i, nd=a.ndim: (0,) * nd, pipeline_mode=pl.Buffered(1))
    F = n_sh * fs
    return pl.pallas_call(
        body, grid=(S // tm,), name="mlp_fwd",
        in_specs=[tile(K), once(w_up), once(w_down), pl.BlockSpec(gain.shape, lambda i: (0, 0))],
        out_specs=[tile(F), tile(F), tile(K), tile(K)],
        out_shape=[jax.ShapeDtypeStruct((S, F), BF16), jax.ShapeDtypeStruct((S, F), BF16),
                   jax.ShapeDtypeStruct((S, K), BF16), jax.ShapeDtypeStruct((S, K), F32)],
        compiler_params=_params(("parallel",)),
    )(h1, w_up, w_down, gain)


def _mm_tn(x, dy, *, name, tm, tn, out_dtype=F32, column_shards=False, after=None):
    xs = list(x) if isinstance(x, (list, tuple)) else [x]
    S, N = dy.shape
    K = x.shape[1] if len(xs) == 1 else tm * len(xs)
    waits = [] if after is None else [after]

    def body(*refs):
        dy_ref, out_ref = refs[len(xs)], refs[-1]
        if len(xs) == 1:
            out_ref[...] = _dot_tn(_bf(refs[0][...]), _bf(dy_ref[...])).astype(out_dtype)
        for k in range(len(xs) if len(xs) > 1 else 0):
            @pl.when(pl.program_id(0) == k)
            def _(k=k):
                out_ref[...] = _dot_tn(_bf(refs[k][...]), _bf(dy_ref[...])).astype(out_dtype)

    if column_shards:
        out_spec = pl.BlockSpec((None, tm, tn), lambda i, j: (j, i, 0))
        out_shape = jax.ShapeDtypeStruct((N // tn, K, tn), out_dtype)
    else:
        out_spec = pl.BlockSpec((tm, tn), lambda i, j: (i, j))
        out_shape = jax.ShapeDtypeStruct((K, N), out_dtype)
    return pl.pallas_call(
        body, grid=(K // tm, N // tn), name=name,
        in_specs=([pl.BlockSpec((S, tm), lambda i, j: (0, i))] if len(xs) == 1 else [pl.BlockSpec((S, tm), lambda i, j: (0, 0))] * len(xs))
        + [pl.BlockSpec((S, tn), lambda i, j: (0, j))] + [pl.BlockSpec(memory_space=pl.ANY)] * len(waits),
        out_specs=out_spec, out_shape=out_shape,
        compiler_params=_params(("parallel", "parallel")),
    )(*xs, dy, *waits)


def _rowwise(body, *, tiled, full, out_tiled, out_acc, name, tm=512, smem=()):
    S = tiled[0].shape[0]
    tm = min(tm, S)
    n_in = len(smem) + len(tiled) + len(full)

    def kern(*refs):
        @pl.when(pl.program_id(0) == 0)
        def _():
            for r in refs[n_in + len(out_tiled):]:
                r[...] = jnp.zeros_like(r)
        body(*refs)

    in_specs = [pl.BlockSpec(memory_space=pltpu.SMEM) for _ in smem]
    in_specs += [pl.BlockSpec((tm, a.shape[1]), lambda i: (i, 0)) for a in tiled]
    in_specs += [pl.BlockSpec(a.shape, lambda i, nd=a.ndim: (0,) * nd) for a in full]
    out_specs = [pl.BlockSpec((tm, w), lambda i: (i, 0)) for w, _ in out_tiled]
    out_specs += [pl.BlockSpec(shp, lambda i, nd=len(shp): (0,) * nd) for shp, _ in out_acc]
    out_shape = [jax.ShapeDtypeStruct((S, w), dt) for w, dt in out_tiled]
    out_shape += [jax.ShapeDtypeStruct(shp, dt) for shp, dt in out_acc]
    return pl.pallas_call(
        kern, grid=(S // tm,), name=name, in_specs=in_specs, out_specs=out_specs, out_shape=out_shape,
        compiler_params=_params(("arbitrary",)),
    )(*smem, *tiled, *full)


def _rms_stats(x):
    r = lax.rsqrt(jnp.mean(x * x, axis=-1, keepdims=True) + EPS)
    return r, x * r


def _rmsnorm_fwd(x, g, name, after):
    def body(x_ref, g_ref, _, o_ref):
        _, xh = _rms_stats(x_ref[...])
        o_ref[...] = _bf(xh * g_ref[...])

    return _rowwise(body, tiled=[x], full=[g, after], out_tiled=[(x.shape[1], BF16)], out_acc=[], name=name)[0]


def _rms_bwd_tile(x, g, dxn):
    r, xh = _rms_stats(x)
    dg = jnp.sum(dxn * xh, axis=0, keepdims=True)
    dn = dxn * g
    dx = r * (dn - xh * jnp.mean(dn * xh, axis=-1, keepdims=True))
    return dx, dg


def _ple_and_loss(h2, p, target, w_pg, w_pp, g_ple, g_final):
    S, n = h2.shape
    tm = min(512, S)
    tn = 512

    def body(h2_ref, p_ref, t_ref, wpg_ref, wpp_ref, gple_ref, gfin_ref,
             n3_ref, dh_ref, dgl_ref, dpp_ref, loss_ref, dg_ref, dgple_ref, pp, gate, h3):
        @pl.when(pl.program_id(0) == 0)
        def _():
            loss_ref[...] = jnp.zeros_like(loss_ref)
            dg_ref[...] = jnp.zeros_like(dg_ref)
            dgple_ref[...] = jnp.zeros_like(dgple_ref)

        x = h2_ref[...]
        _, xh = _rms_stats(x)
        n3 = _bf(xh * gple_ref[...])
        n3_ref[...] = n3
        pb = _bf(p_ref[...])
        for c in range(n // tn):
            cols = slice(c * tn, (c + 1) * tn)
            pp[:, cols] = _dot(pb, wpp_ref[:, cols])
            gt = _sigmoid(_dot(n3, wpg_ref[:, cols]))
            gate[:, cols] = gt
            h3[:, cols] = x[:, cols] + gt * pp[:, cols]
        y = h3[...]
        _, yh = _rms_stats(y)
        e = yh * gfin_ref[...] - t_ref[...]
        per_tok = jnp.mean(e * e, axis=-1, keepdims=True)
        loss_ref[...] += 0.5 * jnp.sum(per_tok, axis=0, keepdims=True)
        dh, dg = _rms_bwd_tile(y, gfin_ref[...], e * (1.0 / n))
        dg_ref[...] += dg
        gt = gate[...]
        dgl = _bf(dh * pp[...] * gt * (1.0 - gt))
        dgl_ref[...] = dgl
        dpp_ref[...] = _bf(dh * gt)
        for c in range(n // tn):
            cols = slice(c * tn, (c + 1) * tn)
            h3[:, cols] = _dot_nt(dgl, wpg_ref[cols, :])
        dx, dgp = _rms_bwd_tile(x, gple_ref[...], h3[...])
        dh_ref[...] = dh + dx
        dgple_ref[...] += dgp

    tile = lambda width: pl.BlockSpec((tm, width), lambda i: (i, 0))
    whole = lambda a: pl.BlockSpec(a.shape, lambda i, nd=a.ndim: (0,) * nd)
    return pl.pallas_call(
        body, grid=(S // tm,), name="ple_and_loss",
        in_specs=[tile(n), tile(p.shape[1]), tile(n), whole(w_pg), whole(w_pp), whole(g_ple), whole(g_final)],
        out_specs=[tile(n), tile(n), tile(n), tile(n), pl.BlockSpec((1, 128), lambda i: (0, 0)),
                   pl.BlockSpec((1, n), lambda i: (0, 0)), pl.BlockSpec((1, n), lambda i: (0, 0))],
        out_shape=[jax.ShapeDtypeStruct((S, n), BF16), jax.ShapeDtypeStruct((S, n), F32), jax.ShapeDtypeStruct((S, n), BF16),
                   jax.ShapeDtypeStruct((S, n), BF16), jax.ShapeDtypeStruct((1, 128), F32), jax.ShapeDtypeStruct((1, n), F32),
                   jax.ShapeDtypeStruct((1, n), F32)],
        scratch_shapes=[pltpu.VMEM((tm, n), F32)] * 3,
        compiler_params=_params(("arbitrary",)),
    )(h2, p, target, w_pg, w_pp, g_ple, g_final)


def _rope_tables(S):
    half = 32
    inv = (1.0 / (np.float32(ROPE_THETA) ** (np.arange(half, dtype=np.float32) * np.float32(2.0 / 64)))).astype(np.float32)
    ang = np.arange(S).astype(np.float32)[:, None] * inv[None, :]
    cos, sin = np.cos(ang), np.sin(ang)
    return jnp.asarray(np.tile(cos, (1, 4))), jnp.asarray(np.concatenate([-sin, sin, -sin, sin], axis=1))


def _attn_common(i, kc, kp, vc, vp, cc, sc, cp, sp):
    lane = lax.broadcasted_iota(jnp.int32, (1, HEAD_PAIR), 1)
    lane_lo = jnp.bitwise_and(lane, 63) < 32
    slot = [lane < 64, lane >= 64]

    def swap_halves(t):
        return jnp.where(lane_lo, pltpu.roll(t, 96, 1), pltpu.roll(t, 32, 1))

    def rope(t, cos, sin):
        return t * cos + swap_halves(t) * sin

    def unrope(d, cos, sin):
        return d * cos + swap_halves(d * sin)

    k2 = jnp.concatenate([rope(kp, cp, sp), rope(kc, cc, sc)], axis=0)
    v2 = jnp.concatenate([vp, vc], axis=0)
    r = lax.broadcasted_iota(jnp.int32, (ATTN_BLOCK, 2 * ATTN_BLOCK), 0)
    c = lax.broadcasted_iota(jnp.int32, (ATTN_BLOCK, 2 * ATTN_BLOCK), 1)
    valid = (c > r) & (c <= r + ATTN_BLOCK) & jnp.logical_or(c >= ATTN_BLOCK, i > 0)
    ks, vs = {}, {}
    for j in range(2):
        kn = jnp.where(slot[j], k2, 0.0)
        vn = jnp.where(slot[j], v2, 0.0)
        for s in range(2):
            ks[j, s] = _bf(kn if s == j else pltpu.roll(kn, 64, 1))
            vs[j, s] = _bf(vn if s == j else pltpu.roll(vn, 64, 1))
    return slot, rope, unrope, valid, ks, vs


def _attn_probs(scores, valid, sink):
    s = jnp.where(valid, scores * 0.125, NEG)
    m = jnp.maximum(jnp.max(s, axis=1, keepdims=True), sink)
    e = jnp.exp(s - m)
    z = jnp.sum(e, axis=1, keepdims=True) + jnp.exp(sink - m)
    return e * (1.0 / z), m + jnp.log(z)


def _attn_specs(S):
    nb = S // ATTN_BLOCK
    prev = lambda i: jnp.maximum(i - 1, 0)
    blk = lambda w, col, row=(lambda i: i): pl.BlockSpec((ATTN_BLOCK, w), lambda i: (row(i), col))
    in_specs = [pl.BlockSpec(memory_space=pltpu.SMEM),
                blk(512, BLK_Q), blk(128, BLK_K), blk(128, BLK_K, prev), blk(128, BLK_V), blk(128, BLK_V, prev),
                blk(128, 0), blk(128, 0), blk(128, 0, prev), blk(128, 0, prev)]
    return nb, in_specs


def _attn_fwd(pa, cos, sin, sinks):
    S = pa.shape[0]
    nb, in_specs = _attn_specs(S)

    def body(sinks_ref, q_ref, kc_ref, kp_ref, vc_ref, vp_ref, cc_ref, sc_ref, cp_ref, sp_ref, o_ref, lse_ref):
        i = pl.program_id(0)
        lane = lax.broadcasted_iota(jnp.int32, (1, HEAD_PAIR), 1)
        cc, sc = cc_ref[...], sc_ref[...]
        _, rope, _, valid, ks, vs = _attn_common(i, kc_ref[...], kp_ref[...], vc_ref[...], vp_ref[...],
                                                 cc, sc, cp_ref[...], sp_ref[...])
        pair_cols = [slice(HEAD_PAIR * pair, HEAD_PAIR * (pair + 1)) for pair in range(4)]
        qps = [_bf(rope(q_ref[:, cols], cc, sc)) for cols in pair_cols]
        outs, lses = {}, {}

        def head_program(h):
            pair, s = divmod(h, 2)
            j = h // 4
            scores = _dot_nt(qps[pair], ks[j, s])
            yield
            p, lse = _attn_probs(scores, valid, sinks_ref[h])
            outs[h] = _dot(_bf(p), vs[j, s])
            lses[h] = jnp.where(lane == h, lse, 0.0)

        _interleave(head_program(h) for h in range(8))
        for pair, cols in enumerate(pair_cols):
            o_ref[:, cols] = outs[2 * pair] + outs[2 * pair + 1]
        lse_ref[...] = sum((lses[h] for h in range(1, 8)), lses[0])

    return pl.pallas_call(
        body, grid=(nb,), name="attn_fwd", in_specs=in_specs,
        out_specs=[pl.BlockSpec((ATTN_BLOCK, 512), lambda i: (i, 0)), pl.BlockSpec((ATTN_BLOCK, 128), lambda i: (i, 0))],
        out_shape=[jax.ShapeDtypeStruct((S, 512), F32), jax.ShapeDtypeStruct((S, 128), F32)],
        compiler_params=_params(("parallel",)),
    )(sinks, pa, pa, pa, pa, pa, cos, sin, cos, sin)


def _attn_bwd(pa, cos, sin, sinks, dcat, attn, lse, after):
    S = pa.shape[0]
    nb, in_specs = _attn_specs(S)
    in_specs = in_specs + [pl.BlockSpec((ATTN_BLOCK, 512), lambda i: (i, 0))] * 2 + [pl.BlockSpec((ATTN_BLOCK, 128), lambda i: (i, 0))]
    in_specs = in_specs + [pl.BlockSpec(memory_space=pl.ANY)]

    def body(sinks_ref, q_ref, kc_ref, kp_ref, vc_ref, vp_ref, cc_ref, sc_ref, cp_ref, sp_ref, do_ref, o_ref, lse_ref, _,
             dq_ref, dk_ref, dv_ref, dsink_ref):
        i = pl.program_id(0)

        @pl.when(i == 0)
        def _():
            dk_ref[...] = jnp.zeros_like(dk_ref)
            dv_ref[...] = jnp.zeros_like(dv_ref)
            dsink_ref[...] = jnp.zeros_like(dsink_ref)

        cc, sc, cp, sp = cc_ref[...], sc_ref[...], cp_ref[...], sp_ref[...]
        slot, rope, unrope, valid, ks, vs = _attn_common(i, kc_ref[...], kp_ref[...], vc_ref[...], vp_ref[...], cc, sc, cp, sp)
        pair_cols = [slice(HEAD_PAIR * pair, HEAD_PAIR * (pair + 1)) for pair in range(4)]
        qps = [_bf(rope(q_ref[:, cols], cc, sc)) for cols in pair_cols]
        dobs = [_bf(do_ref[:, cols]) for cols in pair_cols]
        do_o = [do_ref[:, cols] * o_ref[:, cols] for cols in pair_cols]
        dqs, dks, dvs = {}, {}, {}

        def head_program(h):
            pair, s = divmod(h, 2)
            j = h // 4
            qp, dob = qps[pair], dobs[pair]
            scores = _dot_nt(qp, ks[j, s])
            dp = _dot_nt(dob, vs[j, s])
            yield
            lse_h = lse_ref[:, h:h + 1]
            p = jnp.exp(jnp.where(valid, scores * 0.125, NEG) - lse_h)
            yield
            dr = jnp.sum(jnp.where(slot[s], do_o[pair], 0.0), axis=1, keepdims=True)
            ds = _bf(p * (dp - dr) * 0.125)
            yield
            dsink_ref[h:h + 1, :] += -jnp.sum(jnp.exp(sinks_ref[h] - lse_h) * dr, axis=0, keepdims=True)
            dqs[h] = _dot(ds, ks[j, s])
            dk_h = _dot_tn(ds, qp)
            dv_h = _dot_tn(_bf(p), dob)
            yield
            dk_h, dv_h = jnp.where(slot[s], dk_h, 0.0), jnp.where(slot[s], dv_h, 0.0)
            if s != j:
                dk_h, dv_h = pltpu.roll(dk_h, 64, 1), pltpu.roll(dv_h, 64, 1)
            dks[h], dvs[h] = dk_h, dv_h

        _interleave(head_program(h) for h in range(8))
        dk2 = sum((dks[h] for h in range(1, 8)), dks[0])
        dv2 = sum((dvs[h] for h in range(1, 8)), dvs[0])
        for pair, cols in enumerate(pair_cols):
            dq_ref[:, cols] = _bf(unrope(dqs[2 * pair] + dqs[2 * pair + 1], cc, sc))
        cur = pl.ds(pl.multiple_of(i * ATTN_BLOCK, ATTN_BLOCK), ATTN_BLOCK)
        dk_ref[cur, :] += unrope(dk2[ATTN_BLOCK:], cc, sc)
        dv_ref[cur, :] += dv2[ATTN_BLOCK:]

        @pl.when(i > 0)
        def _():
            prv = pl.ds(pl.multiple_of((i - 1) * ATTN_BLOCK, ATTN_BLOCK), ATTN_BLOCK)
            dk_ref[prv, :] += unrope(dk2[:ATTN_BLOCK], cp, sp)
            dv_ref[prv, :] += dv2[:ATTN_BLOCK]

    whole = lambda w: pl.BlockSpec((S, w), lambda i: (0, 0))
    return pl.pallas_call(
        body, grid=(nb,), name="attn_bwd", in_specs=in_specs,
        out_specs=[pl.BlockSpec((ATTN_BLOCK, 512), lambda i: (i, BLK_Q)), whole(128), whole(128),
                   pl.BlockSpec((8, 128), lambda i: (0, 0))],
        out_shape=[jax.ShapeDtypeStruct((S, D_IN_PAD), BF16), jax.ShapeDtypeStruct((S, 128), F32),
                   jax.ShapeDtypeStruct((S, 128), F32), jax.ShapeDtypeStruct((8, 128), F32)],
        compiler_params=_params(("arbitrary",)),
    )(sinks, pa, pa, pa, pa, pa, cos, sin, cos, sin, dcat, attn, lse, after)


CONV_ROWS = 512
CONV_PAD = 8


def _conv_silu(scr, w, r0):
    y = w[3:4, :] * scr[pl.ds(CONV_PAD + r0, CONV_ROWS), :]
    for j in range(DN_CONV - 1):
        y = y + w[j:j + 1, :] * scr[pl.ds(CONV_PAD + r0 - 3 + j, CONV_ROWS), :]
    return y


def _dn_prep_fwd(pd, conv_w):
    S = pd.shape[0]
    assert S % CONV_ROWS == 0

    def body(x_ref, w_ref, o_ref, scr):
        b = pl.program_id(0)
        scr[0:CONV_PAD, :] = jnp.zeros((CONV_PAD, DN_DIM), F32)
        scr[pl.ds(CONV_PAD, S), :] = x_ref[...]
        w = w_ref[...]
        q_scale = jnp.where(b < DN_HEADS, DN_DIM ** -0.5, 1.0)
        for r0 in range(0, S, CONV_ROWS):
            y = _conv_silu(scr, w, r0)
            a = y * _sigmoid(y)
            rs = lax.rsqrt(jnp.sum(a * a, axis=1, keepdims=True) + EPS)
            o_ref[pl.ds(r0, CONV_ROWS), :] = a * jnp.where(b < 2 * DN_HEADS, rs * q_scale, 1.0)

    col = pl.BlockSpec((S, DN_DIM), lambda b: (0, b))
    return pl.pallas_call(
        body, grid=(3 * DN_HEADS,), name="dn_prep_fwd",
        in_specs=[pl.BlockSpec((S, DN_DIM), lambda b: (0, BLK_DN + b)), pl.BlockSpec((DN_CONV, DN_DIM), lambda b: (0, b))],
        out_specs=col,
        out_shape=jax.ShapeDtypeStruct((S, 3 * DN_HEADS * DN_DIM), F32),
        scratch_shapes=[pltpu.VMEM((S + CONV_PAD, DN_DIM), F32)],
        compiler_params=_params(("parallel",)),
    )(pd, conv_w)


def _dn_prep_bwd(pd, conv_w, dqkv, dproj, dk, dv):
    S = pd.shape[0]
    NB = 3 * DN_HEADS

    def body(x_ref, w_ref, d_ref, _, dk_ref, dv_ref, dx_ref, dw_ref, scr, dscr):
        b = pl.program_id(0)

        @pl.when(b == NB)
        def _():
            dx_ref[...] = _bf(dk_ref[...])

        @pl.when(b == NB + 1)
        def _():
            dx_ref[...] = _bf(dv_ref[...])

        @pl.when(b < NB)
        def _():
            scr[0:CONV_PAD, :] = jnp.zeros((CONV_PAD, DN_DIM), F32)
            scr[pl.ds(CONV_PAD, S), :] = x_ref[...]
            dscr[pl.ds(S, CONV_PAD), :] = jnp.zeros((CONV_PAD, DN_DIM), F32)
            w = w_ref[...]
            q_scale = jnp.where(b < DN_HEADS, DN_DIM ** -0.5, 1.0)
            is_qk = b < 2 * DN_HEADS
            dw = [jnp.zeros((1, DN_DIM), F32) for _ in range(DN_CONV)]
            for r0 in range(0, S, CONV_ROWS):
                y = _conv_silu(scr, w, r0)
                sg = _sigmoid(y)
                a = y * sg
                dout = d_ref[pl.ds(r0, CONV_ROWS), :]
                rs = lax.rsqrt(jnp.sum(a * a, axis=1, keepdims=True) + EPS)
                da_qk = q_scale * rs * (dout - a * (rs * rs) * jnp.sum(dout * a, axis=1, keepdims=True))
                dy = jnp.where(is_qk, da_qk, dout) * (sg * (1.0 + y * (1.0 - sg)))
                dscr[pl.ds(r0, CONV_ROWS), :] = dy
                for j in range(DN_CONV):
                    dw[j] = dw[j] + jnp.sum(dy * scr[pl.ds(CONV_PAD + r0 - 3 + j, CONV_ROWS), :], axis=0, keepdims=True)
            for j in range(DN_CONV):
                dw_ref[j:j + 1, :] = dw[j]
            for r0 in range(0, S, CONV_ROWS):
                dx = w[3:4, :] * dscr[pl.ds(r0, CONV_ROWS), :]
                for j in range(DN_CONV - 1):
                    dx = dx + w[j:j + 1, :] * dscr[pl.ds(r0 + 3 - j, CONV_ROWS), :]
                dx_ref[pl.ds(r0, CONV_ROWS), :] = _bf(dx)

    own = lambda b: jnp.minimum(b, NB - 1)
    col = pl.BlockSpec((S, DN_DIM), lambda b: (0, own(b)))
    proj_col = pl.BlockSpec((S, DN_DIM), lambda b: (0, BLK_DN + own(b)))
    wcol = pl.BlockSpec((DN_CONV, DN_DIM), lambda b: (0, own(b)))
    whole = pl.BlockSpec((S, DN_DIM), lambda b: (0, 0))
    assert BLK_K == BLK_DN + NB and BLK_V == BLK_K + 1
    return pl.pallas_call(
        body, grid=(NB + 2,), name="dn_prep_bwd",
        in_specs=[proj_col, wcol, col, pl.BlockSpec(memory_space=pl.ANY), whole, whole],
        out_specs=[pl.BlockSpec((S, DN_DIM), lambda b: (0, BLK_DN + b)), wcol],
        out_shape=[jax.ShapeDtypeStruct(dproj.shape, dproj.dtype), jax.ShapeDtypeStruct((DN_CONV, 3 * DN_HEADS * DN_DIM), F32)],
        scratch_shapes=[pltpu.VMEM((S + CONV_PAD, DN_DIM), F32), pltpu.VMEM((S + CONV_PAD, DN_DIM), F32)],
        input_output_aliases={3: 0},
        compiler_params=_params(("arbitrary",)),
    )(pd, conv_w, dqkv, dproj, dk, dv)


CPAD = 128
CHUNKS_LOCAL = 4
CHUNKS_SCAN = 8


def _chunk_masks():
    ii = lax.broadcasted_iota(jnp.int32, (DN_CHUNK, CPAD), 0)
    jj = lax.broadcasted_iota(jnp.int32, (DN_CHUNK, CPAD), 1)
    return ii, jj


def _rows_pad(a):
    return jnp.concatenate([a, jnp.zeros_like(a)], axis=0)


def _hi_lo(a):
    hi = _bf(a)
    return hi, _bf(a - hi.astype(F32))


def _double_step(t, p):
    C = DN_CHUNK
    th, tl = _hi_lo(t)
    ph, pl_ = _hi_lo(p)
    r1 = _dot(jnp.concatenate([th, tl, ph, pl_], axis=0), _rows_pad(ph))
    r2 = _dot(jnp.concatenate([th, ph], axis=0), _rows_pad(pl_))
    return t + (r1[:C] + r1[C:2 * C] + r2[:C]), r1[2 * C:3 * C] + r1[3 * C:] + r2[C:]


def _dot3_nt(a, b):
    C = DN_CHUNK
    ah, al = _hi_lo(a)
    bh, bl = _hi_lo(b)
    r1 = _dot_nt(jnp.concatenate([ah, al], axis=0), _rows_pad(bh))
    return r1[:C] + r1[C:] + _dot_nt(ah, _rows_pad(bl))


def _dot3_tn(a, b):
    C = DN_CHUNK
    ah, al = _hi_lo(a)
    bh, bl = _hi_lo(b)
    return _dot_tn(jnp.concatenate([ah, al, ah], axis=0), jnp.concatenate([bh, bh, bl], axis=0))[:C]


def _interleave(programs):
    programs = list(programs)
    while programs:
        alive = []
        for prog in programs:
            try:
                next(prog)
                alive.append(prog)
            except StopIteration:
                pass
        programs = alive


def _col_to_row(col, ii, jj):
    return jnp.sum(jnp.where(ii == jj, col, 0.0), axis=0, keepdims=True)


def _row_to_col(row, ii, jj):
    return jnp.sum(jnp.where(ii == jj, row, 0.0), axis=1, keepdims=True)


def _decay(gc_col, ii, jj):
    diff = gc_col - _col_to_row(gc_col, ii, jj)
    return jnp.where(jj <= ii, jnp.exp(jnp.where(jj <= ii, diff, 0.0)), 0.0)


def _softplus(x):
    return jnp.maximum(x, 0.0) + jnp.log(1.0 + jnp.exp(-jnp.abs(x)))


def _head(h):
    return slice(DN_DIM * h, DN_DIM * (h + 1))


def _dn_chunk_fwd(qkv, pg, a_log, dt_bias):
    S = qkv.shape[0]
    C = DN_CHUNK
    G = CHUNKS_LOCAL
    R = G * C
    steps = S // R

    def body(alog_ref, dtb_ref, qkv_ref, pg_ref, w_ref, u_ref, qg_ref, kd_ref, a_ref, t_ref, gcs_ref):
        ii, jj = _chunk_masks()
        lane = lax.broadcasted_iota(jnp.int32, (1, 128), 1)
        eye = (ii == jj).astype(F32)
        gcs_parts = [[] for _ in range(G)]

        def head_program(chunk, h):
            rows = slice(chunk * C, (chunk + 1) * C)
            q, k, v = qkv_ref[rows, _head(h)], qkv_ref[rows, _head(DN_HEADS + h)], qkv_ref[rows, _head(2 * DN_HEADS + h)]
            beta = _sigmoid(pg_ref[rows, h:h + 1])
            g_col = -jnp.exp(alog_ref[h]) * _softplus(pg_ref[rows, DN_HEADS + h:DN_HEADS + h + 1] + dtb_ref[h])
            g_row = _col_to_row(g_col, ii, jj)
            gc_col = jnp.sum(jnp.where(jj <= ii, g_row, 0.0), axis=1, keepdims=True)
            dec = _decay(gc_col, ii, jj)
            eg = jnp.exp(gc_col)
            kb, vb = k * beta, v * beta
            k_rows = _rows_pad(_bf(k))
            kk = _dot_nt(_bf(kb), k_rows)
            qk = _dot_nt(_bf(q), k_rows)
            yield
            t, pw = eye, -jnp.where(jj < ii, kk * dec, 0.0)
            for _ in range(6):
                t, pw = _double_step(t, pw)
                yield
            tb = _bf(t)
            u_ref[rows, _head(h)] = _dot(tb, _rows_pad(_bf(vb)))
            w_ref[rows, _head(h)] = _bf(_dot(tb, _rows_pad(_bf(kb * eg))))
            a_ref[h, rows] = _bf(qk * dec)
            t_ref[h, rows] = t
            qg_ref[rows, _head(h)] = _bf(q * eg)
            kd_ref[rows, _head(h)] = _bf(k * jnp.exp(gc_col[C - 1:C, :] - gc_col))
            gcs_parts[chunk].append(jnp.where(lane == h, gc_col, 0.0) + jnp.where(lane == DN_HEADS + h, beta, 0.0)
                                    + jnp.where(lane == 2 * DN_HEADS + h, g_col, 0.0))

        _interleave(head_program(chunk, h) for chunk in range(G) for h in range(DN_HEADS))
        for chunk in range(G):
            gcs_ref[chunk * C:(chunk + 1) * C, :] = sum(gcs_parts[chunk][1:], gcs_parts[chunk][0])

    smem = pl.BlockSpec(memory_space=pltpu.SMEM)
    wide = pl.BlockSpec((R, 512), lambda n: (n, 0))
    sq = pl.BlockSpec((DN_HEADS, R, CPAD), lambda n: (0, n, 0))
    narrow = pl.BlockSpec((R, 128), lambda n: (n, 0))
    f = lambda *shp: jax.ShapeDtypeStruct(shp, F32)
    b = lambda *shp: jax.ShapeDtypeStruct(shp, BF16)
    return pl.pallas_call(
        body, grid=(steps,), name="dn_chunk_fwd",
        in_specs=[smem, smem, pl.BlockSpec((R, 1536), lambda n: (n, 0)), pl.BlockSpec((R, 128), lambda n: (n, BLK_G))],
        out_specs=[wide, wide, wide, wide, sq, sq, narrow],
        out_shape=[b(S, 512), f(S, 512), b(S, 512), b(S, 512), b(DN_HEADS, S, CPAD), f(DN_HEADS, S, CPAD), f(S, 128)],
        compiler_params=_params(("parallel",)),
    )(a_log, dt_bias, qkv, pg)


def _gated_norm(o, z, gn):
    r, oh = _rms_stats(o)
    return oh * gn * (z * _sigmoid(z))


def _dn_scan_fwd(w, u, qg, kd, a, gcs, pz, gn):
    S = w.shape[0]
    C = DN_CHUNK
    nc = S // C
    G = CHUNKS_SCAN
    R = G * C

    def body(w_ref, u_ref, qg_ref, kd_ref, a_ref, gcs_ref, z_ref, gn_ref, o_ref, vn_ref, sst_ref, out_ref, state):
        @pl.when(pl.program_id(0) == 0)
        def _():
            state[...] = jnp.zeros_like(state)

        def head_program(chunk, h):
            hs = _head(h)
            rows = slice(chunk * C, (chunk + 1) * C)
            s_in = state[h]
            sb = _bf(s_in)
            sst_ref[chunk, h] = sb
            w_s = _dot(w_ref[rows, hs], sb)
            q_s = _dot(qg_ref[rows, hs], sb)
            yield
            vn = u_ref[rows, hs] - w_s
            vnb = _bf(vn)
            o = q_s + _dot(a_ref[h, rows], _rows_pad(vnb))
            k_v = _dot_tn(kd_ref[rows, hs], vnb)
            yield
            state[h] = s_in * jnp.exp(gcs_ref[(chunk + 1) * C - 1:(chunk + 1) * C, h:h + 1]) + k_v
            o_ref[rows, hs] = o
            vn_ref[rows, hs] = vnb
            out_ref[rows, hs] = _bf(_gated_norm(o, z_ref[rows, hs], gn_ref[...]))

        for chunk in range(G):
            _interleave(head_program(chunk, h) for h in range(DN_HEADS))

    wide = pl.BlockSpec((R, 512), lambda n: (n, 0))
    f = lambda *shp: jax.ShapeDtypeStruct(shp, F32)
    b = lambda *shp: jax.ShapeDtypeStruct(shp, BF16)
    return pl.pallas_call(
        body, grid=(nc // G,), name="dn_scan_fwd",
        in_specs=[wide, wide, wide, wide, pl.BlockSpec((DN_HEADS, R, CPAD), lambda n: (0, n, 0)),
                  pl.BlockSpec((R, 128), lambda n: (n, 0)), pl.BlockSpec((R, 512), lambda n: (n, BLK_Z)),
                  pl.BlockSpec((1, DN_DIM), lambda n: (0, 0))],
        out_specs=[wide, wide, pl.BlockSpec((G, DN_HEADS, DN_DIM, DN_DIM), lambda n: (n, 0, 0, 0)), wide],
        out_shape=[f(S, 512), b(S, 512), b(nc, DN_HEADS, DN_DIM, DN_DIM), b(S, 512)],
        scratch_shapes=[pltpu.VMEM((DN_HEADS, DN_DIM, DN_DIM), F32)],
        compiler_params=_params(("arbitrary",)),
    )(w, u, qg, kd, a, gcs, pz, gn)


def _dn_scan_bwd(dcat, o, pz, gn, sst, vnew, w, qg, kd, a, gcs, dproj):
    S = o.shape[0]
    C = DN_CHUNK
    G = CHUNKS_SCAN
    R = G * C
    steps = S // R

    def body(dy_ref, o_ref, z_ref, gn_ref, sst_ref, vn_ref, w_ref, qg_ref, kd_ref, a_ref, gcs_ref, _,
             du_ref, dw_ref, dqg_ref, dkd_ref, da_ref, dz_ref, dsc_ref, dgn_ref, dstate):
        @pl.when(pl.program_id(0) == 0)
        def _():
            dstate[...] = jnp.zeros_like(dstate)
            dgn_ref[...] = jnp.zeros_like(dgn_ref)

        gn_ = gn_ref[...]
        lane = lax.broadcasted_iota(jnp.int32, (C, 128), 1)
        row = lax.broadcasted_iota(jnp.int32, (C, 128), 0)
        dgn_parts = []

        def head_program(chunk, h, dsc_parts):
            hs = _head(h)
            rows = slice(chunk * C, (chunk + 1) * C)
            ov, z, dout = o_ref[rows, hs], z_ref[rows, hs], dy_ref[rows, hs]
            r, oh = _rms_stats(ov)
            sg = _sigmoid(z)
            don = dout * (z * sg)
            dz_ref[rows, hs] = _bf(dout * (oh * gn_) * (sg * (1.0 + z * (1.0 - sg))))
            dgn_parts.append(jnp.sum(don * oh, axis=0, keepdims=True))
            dn = don * gn_
            do = _bf(r * (dn - oh * jnp.mean(dn * oh, axis=-1, keepdims=True)))
            sb = sst_ref[chunk, h]
            s_in = sb.astype(F32)
            ds_out = dstate[h]
            dsb = _bf(ds_out)
            vnb = vn_ref[rows, hs]
            wb, qgb, kdb, ab = w_ref[rows, hs], qg_ref[rows, hs], kd_ref[rows, hs], a_ref[h, rows]
            dvn = _dot_tn(ab, do)[:C] + _dot(kdb, dsb)
            yield
            da_ref[h, rows] = _dot_nt(do, _rows_pad(vnb))
            dqg_ref[rows, hs] = _dot_nt(do, sb)
            dkd_ref[rows, hs] = _dot_nt(vnb, dsb)
            q_do = _dot_tn(qgb, do)
            yield
            dvnb = _bf(dvn)
            dw_ref[rows, hs] = _bf(-_dot_nt(dvnb, sb))
            w_dvn = _dot_tn(wb, dvnb)
            du_ref[rows, hs] = dvnb
            yield
            d_last = jnp.exp(gcs_ref[(chunk + 1) * C - 1:(chunk + 1) * C, h:h + 1])
            dd = jnp.sum(jnp.sum(ds_out * s_in, axis=1, keepdims=True), axis=0, keepdims=True)
            dsc_parts.append(jnp.where((lane == h) & (row == C - 1), dd * d_last, 0.0))
            dstate[h] = ds_out * d_last + q_do - w_dvn

        for chunk in reversed(range(G)):
            dsc_parts = []
            _interleave(head_program(chunk, h, dsc_parts) for h in range(DN_HEADS))
            dsc_ref[chunk * C:(chunk + 1) * C, :] = sum(dsc_parts[1:], dsc_parts[0])
        dgn_ref[...] += sum(dgn_parts[1:], dgn_parts[0])

    rev = lambda n: steps - 1 - n
    wide = pl.BlockSpec((R, 512), lambda n: (rev(n), 0))
    z_spec = pl.BlockSpec((R, 512), lambda n: (rev(n), BLK_Z))
    sq = pl.BlockSpec((DN_HEADS, R, CPAD), lambda n: (0, rev(n), 0))
    narrow = pl.BlockSpec((R, 128), lambda n: (rev(n), 0))
    gn_spec = pl.BlockSpec((1, DN_DIM), lambda n: (0, 0))
    f = lambda *shp: jax.ShapeDtypeStruct(shp, F32)
    b = lambda *shp: jax.ShapeDtypeStruct(shp, BF16)
    return pl.pallas_call(
        body, grid=(steps,), name="dn_scan_bwd",
        in_specs=[pl.BlockSpec((R, 512), lambda n: (rev(n), 1)), wide, z_spec, gn_spec,
                  pl.BlockSpec((G, DN_HEADS, DN_DIM, DN_DIM), lambda n: (rev(n), 0, 0, 0)),
                  wide, wide, wide, wide, sq, narrow, pl.BlockSpec(memory_space=pl.ANY)],
        out_specs=[wide, wide, wide, wide, sq, z_spec, narrow, gn_spec],
        out_shape=[b(S, 512), b(S, 512), f(S, 512), f(S, 512), f(DN_HEADS, S, CPAD),
                   jax.ShapeDtypeStruct(dproj.shape, dproj.dtype), f(S, 128), f(1, DN_DIM)],
        scratch_shapes=[pltpu.VMEM((DN_HEADS, DN_DIM, DN_DIM), F32)],
        input_output_aliases={11: 5},
        compiler_params=_params(("arbitrary",)),
    )(dcat, o, pz, gn, sst, vnew, w, qg, kd, a, gcs, dproj)


def _dn_chunk_bwd(qkv, pg, t_inv, gcs, du, dw, dqg, dkd, da, dsc, a_log, dt_bias, dproj):
    S = qkv.shape[0]
    C = DN_CHUNK
    G = CHUNKS_LOCAL
    R = G * C

    def body(alog_ref, dtb_ref, qkv_ref, pg_ref, t_ref, gcs_ref, du_ref, dw_ref, dqg_ref, dkd_ref, da_ref, dsc_ref, _,
             dqkv_ref, dpg_ref, acc_ref):
        @pl.when(pl.program_id(0) == 0)
        def _():
            acc_ref[...] = jnp.zeros_like(acc_ref)

        ii, jj = _chunk_masks()
        lane = lax.broadcasted_iota(jnp.int32, (1, 128), 1)
        row8 = lax.broadcasted_iota(jnp.int32, (8, 128), 0)
        lane8 = lax.broadcasted_iota(jnp.int32, (8, 128), 1)
        rowc = lax.broadcasted_iota(jnp.int32, (C, 1), 0)
        tril, strict = jj <= ii, jj < ii
        dpg_parts, acc_parts = [[] for _ in range(G)], []

        def head_program(chunk, h):
            rows = slice(chunk * C, (chunk + 1) * C)
            q, k, v = qkv_ref[rows, _head(h)], qkv_ref[rows, _head(DN_HEADS + h)], qkv_ref[rows, _head(2 * DN_HEADS + h)]
            gc_col, beta, g_col = gcs_ref[rows, h:h + 1], gcs_ref[rows, DN_HEADS + h:DN_HEADS + h + 1], \
                gcs_ref[rows, 2 * DN_HEADS + h:2 * DN_HEADS + h + 1]
            dec = _decay(gc_col, ii, jj)
            eg = jnp.exp(gc_col)
            g_last = gc_col[C - 1:C, :]
            ek = jnp.exp(g_last - gc_col)
            kb, vb = k * beta, v * beta
            kbg = kb * eg
            qb, kbb = _bf(q), _bf(kb)
            k_rows = _rows_pad(_bf(k))
            t = t_ref[h, rows]
            tb = _bf(t)
            dub, dwb = du_ref[rows, _head(h)], dw_ref[rows, _head(h)]
            dqg_, dkd_ = dqg_ref[rows, _head(h)], dkd_ref[rows, _head(h)]
            dt = _dot_nt(dub, _rows_pad(_bf(vb))) + _dot_nt(dwb, _rows_pad(_bf(kbg)))
            t_du_dw = _dot_tn(tb, jnp.concatenate([dub, dwb], axis=1))
            dvb, dkbg = t_du_dw[:C, :DN_DIM], t_du_dw[:C, DN_DIM:]
            kk = _dot_nt(kbb, k_rows)
            qk = _dot_nt(qb, k_rows)
            yield
            dt_t = _dot3_nt(dt, t)
            yield
            dl = -_dot3_tn(t, dt_t)
            yield
            dm = jnp.where(strict, dl * dec, 0.0)
            dqk = jnp.where(tril, da_ref[h, rows] * dec, 0.0)
            gmat = dm * kk + dqk * qk
            dgc = jnp.sum(gmat, axis=1, keepdims=True) - _row_to_col(jnp.sum(gmat, axis=0, keepdims=True), ii, jj)
            dmb, dqkb = _bf(dm), _bf(dqk)
            yield
            dkb = _dot(dmb, k_rows) + dkbg * eg
            dk = _dot_tn(jnp.concatenate([dmb, dqkb], axis=0), jnp.concatenate([kbb, qb], axis=0))[:C] + dkd_ * ek
            dq = _dot(dqkb, k_rows) + dqg_ * eg
            yield
            tk = jnp.sum(dkd_ * k * ek, axis=1, keepdims=True)
            dgc = dgc + jnp.sum(dqg_ * q * eg, axis=1, keepdims=True) - tk + jnp.sum(dkbg * kbg, axis=1, keepdims=True)
            dgl = jnp.sum(tk, axis=0, keepdims=True) + dsc_ref[(chunk + 1) * C - 1:(chunk + 1) * C, h:h + 1]
            dgc = dgc + jnp.where(rowc == C - 1, dgl, 0.0)
            yield
            dk = dk + dkb * beta
            dbeta = jnp.sum(dkb * k, axis=1, keepdims=True) + jnp.sum(dvb * v, axis=1, keepdims=True)
            dqkv_ref[rows, _head(h)] = dq
            dqkv_ref[rows, _head(DN_HEADS + h)] = dk
            dqkv_ref[rows, _head(2 * DN_HEADS + h)] = dvb * beta
            dg_col = jnp.sum(jnp.where(jj >= ii, _col_to_row(dgc, ii, jj), 0.0), axis=1, keepdims=True)
            yield
            db = dbeta * beta * (1.0 - beta)
            da_in = dg_col * (-jnp.exp(alog_ref[h])) * _sigmoid(pg_ref[rows, DN_HEADS + h:DN_HEADS + h + 1] + dtb_ref[h])
            dpg_parts[chunk].append(jnp.where(lane == h, db, 0.0) + jnp.where(lane == DN_HEADS + h, da_in, 0.0))
            acc_parts.append(jnp.where((row8 == 0) & (lane8 == h), jnp.sum(dg_col * g_col, axis=0, keepdims=True), 0.0)
                             + jnp.where((row8 == 1) & (lane8 == h), jnp.sum(da_in, axis=0, keepdims=True), 0.0))

        _interleave(head_program(chunk, h) for chunk in range(G) for h in range(DN_HEADS))
        for chunk in range(G):
            dpg = sum(dpg_parts[chunk][1:], dpg_parts[chunk][0])
            dpg_ref[chunk * C:(chunk + 1) * C, :] = _bf(jnp.concatenate([dpg, jnp.zeros_like(dpg)], axis=1))
        acc_ref[...] += sum(acc_parts[1:], acc_parts[0])

    smem = pl.BlockSpec(memory_space=pltpu.SMEM)
    wide = pl.BlockSpec((R, 512), lambda n: (n, 0))
    sq = pl.BlockSpec((DN_HEADS, R, CPAD), lambda n: (0, n, 0))
    narrow = pl.BlockSpec((R, 128), lambda n: (n, 0))
    qkv_spec = pl.BlockSpec((R, 1536), lambda n: (n, 0))
    f = lambda *shp: jax.ShapeDtypeStruct(shp, F32)
    return pl.pallas_call(
        body, grid=(S // R,), name="dn_chunk_bwd",
        in_specs=[smem, smem, qkv_spec, pl.BlockSpec((R, 128), lambda n: (n, BLK_G)), sq, narrow, wide, wide, wide, wide, sq,
                  narrow, pl.BlockSpec(memory_space=pl.ANY)],
        out_specs=[qkv_spec, pl.BlockSpec((R, 256), lambda n: (n, BLK_G_PAD)), pl.BlockSpec((8, 128), lambda n: (0, 0))],
        out_shape=[f(S, 1536), jax.ShapeDtypeStruct(dproj.shape, dproj.dtype), f(8, 128)],
        input_output_aliases={12: 1},
        compiler_params=_params(("arbitrary",)),
    )(a_log, dt_bias, qkv, pg, t_inv, gcs, du, dw, dqg, dkd, da, dsc, dproj)


_W_IN_SECTIONS = ((0, 0, 512), (2304, 512, 512), (768, 1024, 1536), (512, 2560, 256), (2816, 2816, 8))
_HALF = D_MODEL // 2
_SECTION_ROWS = 256


def _pack_pairs(x):
    bits = lax.bitcast_convert_type(x, jnp.uint32)
    return lax.bitcast_convert_type(bits[:, _HALF:] | (bits[:, :_HALF] >> 16), F32)


def _unpack_pairs(words):
    bits = lax.bitcast_convert_type(words, jnp.uint32)
    return (lax.bitcast_convert_type(bits << 16, F32),
            lax.bitcast_convert_type(bits & jnp.uint32(0xFFFF0000), F32))


def _w_in_to_internal(packed):
    starts = [dst for _, dst, _ in _W_IN_SECTIONS] + [D_IN_PAD]

    def body(x_hbm, o_ref, words, sems):
        copies = [pltpu.make_async_copy(x_hbm.at[pl.ds(src, rows), 0, :], words.at[pl.ds(dst, rows)], sems.at[i])
                  for i, (src, dst, rows) in enumerate(_W_IN_SECTIONS)]
        for cp in copies:
            cp.start()
        words[pl.ds(D_IN, D_IN_PAD - D_IN), :] = jnp.zeros((D_IN_PAD - D_IN, _HALF), F32)
        for i, cp in enumerate(copies):
            cp.wait()
            for r in range(starts[i], starts[i + 1], _SECTION_ROWS):
                for c, h in enumerate(_unpack_pairs(words[pl.ds(r, _SECTION_ROWS), :])):
                    o_ref[pl.ds(r, _SECTION_ROWS), c * _HALF:(c + 1) * _HALF] = _bf(h)

    assert all((b - a) % _SECTION_ROWS == 0 for a, b in zip(starts, starts[1:])) and starts[-2] + _W_IN_SECTIONS[-1][2] == D_IN
    return pl.pallas_call(body, name="w_in_to_internal", out_shape=jax.ShapeDtypeStruct((D_IN_PAD, D_MODEL), BF16),
                          in_specs=[pl.BlockSpec(memory_space=pl.ANY)],
                          scratch_shapes=[pltpu.VMEM((D_IN_PAD, _HALF), F32), pltpu.SemaphoreType.DMA((len(_W_IN_SECTIONS),))],
                          compiler_params=pltpu.CompilerParams(vmem_limit_bytes=VMEM_LIMIT))(packed)


def _w_in_from_internal(gt):
    def body(g_ref, o_hbm, words, sems):
        copies = []
        for i, (dst, src, rows) in enumerate(_W_IN_SECTIONS):
            for r in range(src, src + rows, _SECTION_ROWS):
                n = max(min(_SECTION_ROWS, src + rows - r), 16)
                words[pl.ds(r, n), :] = _pack_pairs(g_ref[pl.ds(r, n), :].astype(F32))
            copies.append(pltpu.make_async_copy(words.at[pl.ds(src, rows)], o_hbm.at[pl.ds(dst, rows), 0, :], sems.at[i]))
            copies[-1].start()
        for cp in copies:
            cp.wait()

    return pl.pallas_call(body, name="w_in_from_internal", out_shape=jax.ShapeDtypeStruct((D_IN, 1, _HALF), F32),
                          out_specs=pl.BlockSpec(memory_space=pl.ANY),
                          scratch_shapes=[pltpu.VMEM((D_IN_PAD, _HALF), F32), pltpu.SemaphoreType.DMA((len(_W_IN_SECTIONS),))],
                          compiler_params=pltpu.CompilerParams(vmem_limit_bytes=VMEM_LIMIT))(gt)


def _local_step(x, p, target, wts, first_weights, other_weights, ship_early, after):
    S = x.shape[0]
    cos, sin = _rope_tables(S)
    sinks, a_log, dt_bias = wts["sinks"].reshape(8), wts["a_log"].reshape(4), wts["dt_bias"].reshape(4)
    gn = wts["dn_norm"].reshape(1, DN_DIM)
    add = lambda acc, res: (acc + res,)

    u = _rmsnorm_fwd(x, wts["norm_mix"], "norm_mix_fwd", after)
    w_in_t, conv_w = first_weights(u)
    proj, = _mm(u, w_in_t, form="nt", name="in_proj", out_dtypes=[F32], tn=512)
    attn, lse = _attn_fwd(proj, cos, sin, sinks)
    qkv = _dn_prep_fwd(proj, conv_w)
    cw, cu, cqg, ckd, ca, ct, gcs = _dn_chunk_fwd(qkv, proj, a_log, dt_bias)
    o, vnew, sst, dn_out = _dn_scan_fwd(cw, cu, cqg, ckd, ca, gcs, proj, gn)
    w_o, = other_weights(("w_o",), dn_out)
    h1, = _mm([attn, dn_out], w_o, form="nn", name="out_proj", out_dtypes=[F32], tn=512, epi=add, extra=[x])

    w_up, w_down = other_weights(("w_up", "w_down"), h1)
    hid, relu, m, h2 = _mlp_fwd(h1, w_up, w_down, wts["norm_mlp"])
    w_pg, w_pp = other_weights(("w_ple_gate", "w_ple_proj"), h2)
    n3, dh2, dgl, dpp, loss, d_norm_final, d_norm_ple = _ple_and_loss(h2, p, target, w_pg, w_pp, wts["norm_ple"],
                                                                     wts["norm_final"].reshape(1, D_MODEL))
    g = {"norm_final": d_norm_final, "norm_ple": d_norm_ple}
    early = {"w_ple_gate": _mm_tn(n3, dgl, name="d_w_ple_gate", tm=512, tn=1024, out_dtype=BF16).reshape(N_DEV, 128, 1024),
             "w_ple_proj": _mm_tn(p, dpp, name="d_w_ple_proj", tm=256, tn=128, out_dtype=BF16, column_shards=True)}
    d_act, = _mm(dh2, w_down, form="nt", name="d_hidden", out_dtypes=[BF16], tn=512,
                 epi=lambda acc, r: (acc * (2.0 * r.astype(F32)),), extra=[relu])
    early["w_down"] = _mm_tn(hid, dh2, name="d_w_down", tm=512, tn=1024, out_dtype=BF16).reshape(N_DEV, 512, 1024)
    early["w_up"] = _mm_tn(m, d_act, name="d_w_up", tm=1024, tn=512, out_dtype=BF16, column_shards=True)
    token = ship_early(early)
    dh1, g["norm_mlp"], dcat = _mm(d_act, w_up, form="nt", name="d_m", out_dtypes=[F32], tn=512, after=token,
                                   norm_bwd=(h1, wts["norm_mlp"], dh2), then_nt=w_o)
    d_w_o = _mm_tn([attn, dn_out], dh1, name="d_w_o", tm=512, tn=512, out_dtype=BF16)
    token = ship_early({"w_o": d_w_o.reshape(N_DEV, 128, 1024)})
    dproj, dk, dv, dsinks = _attn_bwd(proj, cos, sin, sinks, dcat, attn, lse, token)
    g["sinks"] = dsinks[:, 0].reshape(1, 8)
    du_, dw_, dqg, dkd, da, dproj, dsc, g["dn_norm"] = _dn_scan_bwd(dcat, o, proj, gn, sst, vnew, cw, cqg, ckd, ca, gcs, dproj)
    dqkv, dproj, gate_acc = _dn_chunk_bwd(qkv, proj, ct, gcs, du_, dw_, dqg, dkd, da, dsc, a_log, dt_bias, dproj)
    g["a_log"], g["dt_bias"] = gate_acc[0:1, 0:4], gate_acc[1:2, 0:4]
    dproj, g["conv_w"] = _dn_prep_bwd(proj, conv_w, dqkv, dproj, dk, dv)
    token = ship_early({"w_in": _mm_tn(dproj, u, name="d_w_in", tm=512, tn=1024, out_dtype=BF16)})
    grad_x, g["norm_mix"] = _mm(dproj, w_in_t, form="nn", name="d_u", out_dtypes=[F32], tn=512, after=token,
                                norm_bwd=(x, wts["norm_mix"], dh1))
    return loss, grad_x, g


def _peer(k):
    x, y, c = lax.axis_index("x"), lax.axis_index("y"), lax.axis_index("c")
    px = 1 - x if k & 4 else x
    py = 1 - y if k & 2 else y
    pc = 1 - c if k & 1 else c
    return (px, py, pc), 4 * px + 2 * py + pc


def _exchange(srcs, name, gather):
    n = len(srcs)
    gathers = list(gather) if isinstance(gather, (list, tuple)) else [gather] * n
    shapes = [(N_DEV,) + s.shape if gt else s.shape for s, gt in zip(srcs, gathers)]

    def body(*refs):
        src_refs, out_refs = refs[:n], refs[n:2 * n]
        send_sems, recv_sems, local_sems = refs[2 * n:]
        _, me = _peer(0)
        piece = lambda a, d: src_refs[a] if gathers[a] else src_refs[a].at[d]
        local = [pltpu.make_async_copy(piece(a, me), out_refs[a].at[me], local_sems.at[a]) for a in range(n)]
        for cp in local:
            cp.start()
        copies = []
        for a in range(n):
            for k in range(1, N_DEV):
                dev, idx = _peer(k)
                cp = pltpu.make_async_remote_copy(src_ref=piece(a, idx), dst_ref=out_refs[a].at[me],
                                                  send_sem=send_sems.at[a, k - 1], recv_sem=recv_sems.at[a, k - 1],
                                                  device_id=dev, device_id_type=MESH)
                cp.start()
                copies.append(cp)
        for cp in copies:
            cp.wait_recv()
        for cp in copies:
            cp.wait_send()
        for cp in local:
            cp.wait()

    anywhere = pl.BlockSpec(memory_space=pl.ANY)
    return pl.pallas_call(
        body, name=name, in_specs=[anywhere] * n, out_specs=[anywhere] * n,
        out_shape=[jax.ShapeDtypeStruct(shp, s.dtype) for shp, s in zip(shapes, srcs)],
        scratch_shapes=[pltpu.SemaphoreType.DMA((n, N_DEV - 1)), pltpu.SemaphoreType.DMA((n, N_DEV - 1)),
                        pltpu.SemaphoreType.DMA((n,))],
    )(*srcs)


_HBM = pl.BlockSpec(memory_space=pltpu.HBM)
_SEM = pl.BlockSpec(memory_space=pltpu.SEMAPHORE)
_EFFECT = pltpu.SideEffectType.DATAFLOW_SIDE_EFFECTING


def _split_copies(src_refs, land_refs, send_sems, recv_sems, modes, which=None):
    _, me = _peer(0)
    local, remote = [], []
    which = range(len(src_refs)) if which is None else which
    for a, src, land in zip(which, src_refs, land_refs):
        if modes[a] == "columns":
            n_cols = src.shape[1]
            dst = land.at[:, pl.ds(pl.multiple_of(me * n_cols, n_cols), n_cols)]
        else:
            dst = land.at[me]
        part = lambda d: src.at[d] if modes[a] == "pieces" else src
        local.append(pltpu.make_async_copy(part(me), dst, recv_sems.at[a * N_DEV]))
        for k in ((2, 4, 6) if modes[a] == "chips" else range(1, N_DEV)):
            dev, idx = _peer(k)
            sem = a * N_DEV + k
            remote.append(pltpu.make_async_remote_copy(
                src_ref=part(idx), dst_ref=dst, send_sem=send_sems.at[sem], recv_sem=recv_sems.at[sem],
                device_id=dev, device_id_type=MESH))
    return local, remote


def _forward_copies(land_refs, send_sems, recv_sems):
    c = lax.axis_index("c")
    sibling, _ = _peer(1)
    copies = []
    for a, land in enumerate(land_refs):
        for chip in range(N_DEV // 2):
            slot = 2 * chip + c
            sem = a * (N_DEV // 2) + chip
            copies.append(pltpu.make_async_remote_copy(
                src_ref=land.at[slot], dst_ref=land.at[slot], send_sem=send_sems.at[sem], recv_sem=recv_sems.at[sem],
                device_id=sibling, device_id_type=MESH))
    return copies


def _forward_start(lands, name):
    n = len(lands)

    def body(*refs):
        for cp in _forward_copies(refs[:n], refs[n], refs[n + 1]):
            cp.start()
        refs[-1][...] = jnp.zeros_like(refs[-1])

    sems = pltpu.SemaphoreType.DMA((n * (N_DEV // 2),))
    out = pl.pallas_call(
        body, name=name,
        out_shape=(sems, sems, *[pltpu.HBM(t.shape, t.dtype) for t in lands], jax.ShapeDtypeStruct((8, 128), F32)),
        in_specs=[_HBM] * n, out_specs=(_SEM, _SEM, *[_HBM] * n, pl.BlockSpec(memory_space=pltpu.VMEM)),
        input_output_aliases={i: 2 + i for i in range(n)},
        compiler_params=pltpu.CompilerParams(has_side_effects=_EFFECT),
    )(*[pltpu.with_memory_space_constraint(t, pltpu.HBM) for t in lands])
    return out[:-1], out[-1]


def _forward_wait(handle, after, name):
    send_sems, recv_sems, *lands = handle
    n = len(lands)

    def body(*refs):
        for cp in _forward_copies(refs[:n], refs[n], refs[n + 1]):
            cp.wait_send()
            cp.wait_recv()

    return list(pl.pallas_call(
        body, name=name, out_shape=tuple(pltpu.HBM(t.shape, t.dtype) for t in lands),
        in_specs=[_HBM] * n + [_SEM, _SEM, pl.BlockSpec(memory_space=pl.ANY)], out_specs=tuple([_HBM] * n),
        input_output_aliases={i: i for i in range(n)},
        compiler_params=pltpu.CompilerParams(has_side_effects=_EFFECT),
    )(*lands, send_sems, recv_sems, after))


def _exchange_start(srcs, name, modes):
    n = len(srcs)
    modes = [modes] * n if isinstance(modes, str) else list(modes)
    lands = []
    for s, mode in zip(srcs, modes):
        shape = {"columns": (s.shape[0], N_DEV * s.shape[1]), "pieces": s.shape}.get(mode, (N_DEV,) + s.shape)
        lands.append(lax.empty(shape, s.dtype))

    def body(*refs):
        src_refs, land_refs = refs[:n], refs[n:2 * n]
        send_sems, recv_sems = refs[2 * n], refs[2 * n + 1]
        local, remote = _split_copies(src_refs, land_refs, send_sems, recv_sems, modes)
        for cp in local + remote:
            cp.start()
        refs[-1][...] = jnp.zeros_like(refs[-1])

    both = list(srcs) + lands
    sems = pltpu.SemaphoreType.DMA((n * N_DEV,))
    out = pl.pallas_call(
        body, name=name,
        out_shape=(sems, sems, *[pltpu.HBM(t.shape, t.dtype) for t in both], jax.ShapeDtypeStruct((8, 128), F32)),
        in_specs=[_HBM] * (2 * n), out_specs=(_SEM, _SEM, *[_HBM] * (2 * n), pl.BlockSpec(memory_space=pltpu.VMEM)),
        input_output_aliases={i: 2 + i for i in range(2 * n)},
        compiler_params=pltpu.CompilerParams(has_side_effects=_EFFECT),
    )(*[pltpu.with_memory_space_constraint(t, pltpu.HBM) for t in both])
    return (n, modes, out[:-1]), out[-1]


def _exchange_wait(handle, after, name, which=None):
    n_all, modes, (send_sems, recv_sems, *both_all) = handle
    which = list(range(n_all)) if which is None else list(which)
    n = len(which)
    both = [both_all[a] for a in which] + [both_all[n_all + a] for a in which]

    def body(*refs):
        src_refs, land_refs = refs[:n], refs[n:2 * n]
        local, remote = _split_copies(src_refs, land_refs, refs[2 * n], refs[2 * n + 1], modes, which)
        for cp in local:
            cp.wait()
        for cp in remote:
            cp.wait_send()
            cp.wait_recv()

    out = pl.pallas_call(
        body, name=name, out_shape=tuple(pltpu.HBM(t.shape, t.dtype) for t in both),
        in_specs=[_HBM] * (2 * n) + [_SEM, _SEM, pl.BlockSpec(memory_space=pl.ANY)], out_specs=tuple([_HBM] * (2 * n)),
        input_output_aliases={i: i for i in range(2 * n)},
        compiler_params=pltpu.CompilerParams(has_side_effects=_EFFECT),
    )(*both, send_sems, recv_sems, after)
    return list(out[n:])


def _cast_all(arrays, name, after):
    waits = [] if after is None else [after]

    def body(*refs):
        for src, dst in zip(refs[:len(arrays)], refs[len(arrays) + len(waits):]):
            if len(src.shape) == 2:
                dst[...] = _bf(src[...])
            else:
                dst[:, 0, :] = _pack_pairs(_bf(src[:, 0, :]).astype(F32))

    shapes = [jax.ShapeDtypeStruct(a.shape, BF16) if a.ndim == 2 else jax.ShapeDtypeStruct((a.shape[0], 1, a.shape[2] // 2), F32)
              for a in arrays]
    return pl.pallas_call(body, name=name, out_shape=shapes,
                          compiler_params=pltpu.CompilerParams(vmem_limit_bytes=VMEM_LIMIT))(*arrays, *waits)


def _adam_update(g, w, m, v):
    nm = ADAM_B1 * m + (1.0 - ADAM_B1) * g
    nv = ADAM_B2 * v + (1.0 - ADAM_B2) * (g * g)
    m_hat = nm / (1.0 - ADAM_B1 ** ADAM_STEP)
    v_hat = nv / (1.0 - ADAM_B2 ** ADAM_STEP)
    return -ADAM_LR * (m_hat / (jnp.sqrt(v_hat) + ADAM_EPS) + ADAM_WD * w), nm, nv


def _adamw(parts, w, m, v, name):
    n, R, W = parts.shape
    tm = min(128, R // 4)

    def body(p_ref, w_ref, m_ref, v_ref, g_ref, d_ref, nm_ref, nv_ref):
        g = p_ref[0].astype(F32)
        for s in range(1, n):
            g = g + p_ref[s].astype(F32)
        g_ref[...] = g
        d_ref[...], nm_ref[...], nv_ref[...] = _adam_update(g, w_ref[...], m_ref[...], v_ref[...])

    tile = pl.BlockSpec((tm, W), lambda i: (i, 0))
    return pl.pallas_call(
        body, grid=(R // tm,), name=name,
        in_specs=[pl.BlockSpec((n, tm, W), lambda i: (0, i, 0)), tile, tile, tile],
        out_specs=[tile] * 4, out_shape=[jax.ShapeDtypeStruct((R, W), F32)] * 4,
        compiler_params=_params(("parallel",)),
    )(parts, w, m, v)


def _adamw_rows_apart(parts, w, m, v, name):
    n, R, _, half = parts.shape

    def body(p_hbm, w_hbm, m_hbm, v_hbm, *rest):
        out_hbm, (words, given, results, sems) = rest[:4], rest[4:]
        loads = [pltpu.make_async_copy(p_hbm.at[s, :, 0, :], words.at[s], sems.at[s]) for s in range(n)]
        loads += [pltpu.make_async_copy(h.at[:, 0, :], given.at[i], sems.at[n + i]) for i, h in enumerate((w_hbm, m_hbm, v_hbm))]
        for cp in loads:
            cp.start()
        for cp in loads:
            cp.wait()
        part = lambda s: jnp.concatenate(_unpack_pairs(words[s]), axis=1)
        g = part(0)
        for s in range(1, n):
            g = g + part(s)
        stores = []
        for i, val in enumerate((g,) + _adam_update(g, given[0], given[1], given[2])):
            results[i] = val
            stores.append(pltpu.make_async_copy(results.at[i], out_hbm[i].at[:, 0, :], sems.at[n + 3 + i]))
            stores[-1].start()
        for cp in stores:
            cp.wait()

    anywhere = pl.BlockSpec(memory_space=pl.ANY)
    return pl.pallas_call(
        body, name=name, in_specs=[anywhere] * 4, out_specs=[anywhere] * 4,
        out_shape=[jax.ShapeDtypeStruct(w.shape, F32)] * 4,
        scratch_shapes=[pltpu.VMEM((n, R, half), F32), pltpu.VMEM((3, R, 2 * half), F32), pltpu.VMEM((4, R, 2 * half), F32),
                        pltpu.SemaphoreType.DMA((n + 7,))],
        compiler_params=pltpu.CompilerParams(vmem_limit_bytes=VMEM_LIMIT),
    )(parts, w, m, v)


_MATRICES = ("w_in", "w_o", "w_up", "w_down", "w_ple_gate", "w_ple_proj")


_OTHERS = ("w_o", "w_up", "w_down", "w_ple_gate", "w_ple_proj")
_OTHER_MODES = {"w_o": "slots", "w_up": "slots", "w_down": "slots", "w_ple_gate": "slots", "w_ple_proj": "columns"}


_VECTORS = ("norm_mix", "norm_mlp", "norm_ple", "norm_final", "a_log", "dt_bias", "sinks", "dn_norm")
_SMALL_ROWS, _LOSS_ROW, _CONV_ROW = 16, 8, 9


def _pack_small(vectors, loss, conv):
    def body(*refs):
        out = refs[-1]
        out[...] = jnp.zeros_like(out)
        for r, ref in enumerate(refs[:len(_VECTORS)]):
            out[r:r + 1, 0:ref.shape[1]] = ref[...]
        out[_LOSS_ROW:_LOSS_ROW + 1, 0:128] = refs[len(_VECTORS)][...]
        out[_CONV_ROW:_CONV_ROW + 6, :] = refs[len(_VECTORS) + 1][...]

    return pl.pallas_call(body, name="pack_small", out_shape=jax.ShapeDtypeStruct((_SMALL_ROWS, 1024), F32))(*vectors, loss, conv)


def _sum_slots(parts):
    def body(p_ref, o_ref):
        acc = p_ref[0]
        for s in range(1, parts.shape[0]):
            acc = acc + p_ref[s]
        o_ref[...] = acc

    return pl.pallas_call(body, name="sum_small", out_shape=jax.ShapeDtypeStruct(parts.shape[1:], parts.dtype))(parts)


def _adamw_vectors(summed, conv_g, wmv):
    names = _VECTORS + ("conv_w",)
    flat = [a for triple in wmv for a in triple]

    def body(*refs):
        sum_ref, conv_ref = refs[0], refs[1]
        ins, outs = refs[2:2 + len(flat)], refs[2 + len(flat):]
        for i in range(len(names)):
            w_ref, m_ref, v_ref = ins[3 * i:3 * i + 3]
            g = conv_ref[...] if i == len(_VECTORS) else sum_ref[i:i + 1, 0:w_ref.shape[1]]
            outs[4 * i][...] = g
            outs[4 * i + 1][...], outs[4 * i + 2][...], outs[4 * i + 3][...] = _adam_update(g, w_ref[...], m_ref[...], v_ref[...])

    out_shape = [jax.ShapeDtypeStruct(t[0].shape, F32) for t in wmv for _ in range(4)]
    res = pl.pallas_call(body, name="adamw_vectors", out_shape=out_shape)(summed, conv_g, *flat)
    return {n: res[4 * i:4 * i + 4] for i, n in enumerate(names)}


_ORDER = ("norm_mix", "w_in", "conv_w", "a_log", "dt_bias", "dn_norm", "sinks", "w_o", "norm_mlp", "w_up", "w_down",
          "norm_ple", "w_ple_gate", "w_ple_proj", "norm_final")


def kernel(x, p, norm_mix, w_in, conv_w, a_log, dt_bias, dn_norm, sinks, w_o, norm_mlp, w_up, w_down, norm_ple, w_ple_gate, w_ple_proj, norm_final, loss_target, m_norm_mix, m_w_in, m_conv_w, m_a_log, m_dt_bias, m_dn_norm, m_sinks, m_w_o, m_norm_mlp, m_w_up, m_w_down, m_norm_ple, m_w_ple_gate, m_w_ple_proj, m_norm_final, v_norm_mix, v_w_in, v_conv_w, v_a_log, v_dt_bias, v_dn_norm, v_sinks, v_w_o, v_norm_mlp, v_w_up, v_w_down, v_norm_ple, v_w_ple_gate, v_w_ple_proj, v_norm_final):
    w = dict(norm_mix=norm_mix, w_in=w_in, conv_w=conv_w[0], a_log=a_log, dt_bias=dt_bias, dn_norm=dn_norm, sinks=sinks,
             w_o=w_o[0], norm_mlp=norm_mlp, w_up=w_up[0], w_down=w_down[0], norm_ple=norm_ple, w_ple_gate=w_ple_gate[0],
             w_ple_proj=w_ple_proj[0], norm_final=norm_final)
    m = dict(norm_mix=m_norm_mix, w_in=m_w_in, conv_w=m_conv_w[0], a_log=m_a_log, dt_bias=m_dt_bias, dn_norm=m_dn_norm,
             sinks=m_sinks, w_o=m_w_o[0], norm_mlp=m_norm_mlp, w_up=m_w_up[0], w_down=m_w_down[0], norm_ple=m_norm_ple,
             w_ple_gate=m_w_ple_gate[0], w_ple_proj=m_w_ple_proj[0], norm_final=m_norm_final)
    v = dict(norm_mix=v_norm_mix, w_in=v_w_in, conv_w=v_conv_w[0], a_log=v_a_log, dt_bias=v_dt_bias, dn_norm=v_dn_norm,
             sinks=v_sinks, w_o=v_w_o[0], norm_mlp=v_norm_mlp, w_up=v_w_up[0], w_down=v_w_down[0], norm_ple=v_norm_ple,
             w_ple_gate=v_w_ple_gate[0], w_ple_proj=v_w_ple_proj[0], norm_final=v_norm_final)
    me = 4 * lax.axis_index("x") + 2 * lax.axis_index("y") + lax.axis_index("c")
    conv_shard = conv_w.shape[2]

    for d in (w, m, v):
        d["w_in"] = jnp.transpose(d["w_in"], (2, 0, 1))
    conv_pad = jnp.pad(w["conv_w"], ((0, 8 - DN_CONV), (0, 256 - conv_shard)))
    w_in_shard, = _cast_all([w["w_in"]], "cast_w_in", None)
    gathers_first, token_first = _exchange_start([w_in_shard, conv_pad], "gather_first_start", "chips")
    shards = _cast_all([w[n] for n in _OTHERS], "cast_others", token_first)
    gathers, token_gather = _exchange_start(list(shards), "gather_start", [_OTHER_MODES[n] for n in _OTHERS])

    def first_weights(after):
        over_ici = _exchange_wait(gathers_first, after, "gather_first_wait")
        handle, token = _forward_start(over_ici, "gather_first_forward")
        w_in_all, conv_all = _forward_wait(handle, token, "gather_first_forward_wait")
        conv_all = jnp.transpose(conv_all[:, :DN_CONV, :conv_shard], (1, 0, 2)).reshape(DN_CONV, N_DEV * conv_shard)
        return _w_in_to_internal(w_in_all.reshape(D_IN, 1, _HALF)), conv_all

    as_taken = {"w_o": lambda t: t.reshape(1024, 1024), "w_up": lambda t: t, "w_down": lambda t: t.reshape(4096, 1024),
                "w_ple_gate": lambda t: t.reshape(1024, 1024), "w_ple_proj": lambda t: t}

    def other_weights(names, after):
        which = [_OTHERS.index(n) for n in names]
        got = _exchange_wait(gathers, after, "gather_wait_" + names[0], which)
        return [as_taken[n](t) for n, t in zip(names, got)]

    shipped = []

    def ship_early(pieces):
        names = tuple(pieces)
        if names == ("w_in",):
            pieces = {"w_in": _w_in_from_internal(pieces["w_in"]).reshape(N_DEV, D_IN // N_DEV, 1, _HALF)}
        handle, token = _exchange_start([pieces[n] for n in names], "scatter_start_" + names[0], "pieces")
        shipped.append((names, handle))
        return token

    loss, grad_x, g = _local_step(x[0], p[0, 0], loss_target[0], w, first_weights, other_weights, ship_early, token_gather)

    row = lambda t: t.reshape(1, t.size)
    small = _pack_small([row(g[n]) for n in _VECTORS], loss, g["conv_w"].reshape(6, 1024))
    small_handle, token_small = _exchange_start([small], "gather_small_start", "slots")
    big, after = {}, token_small
    for names, handle in shipped[:-1]:
        for n, r in zip(names, _exchange_wait(handle, after, "scatter_wait_" + names[0])):
            big[n] = _adamw(r, w[n], m[n], v[n], "adamw_" + n)
            after = big[n][1]
    small_all, = _exchange_wait(small_handle, after, "gather_small_wait")
    summed = _sum_slots(small_all)
    conv_g = lax.dynamic_slice(summed[_CONV_ROW:_CONV_ROW + 6].reshape(DN_CONV, N_DEV * conv_shard), (0, me * conv_shard),
                               (DN_CONV, conv_shard))
    small_out = _adamw_vectors(summed, conv_g, [(row(w[n]), row(m[n]), row(v[n])) for n in _VECTORS]
                               + [(w["conv_w"], m["conv_w"], v["conv_w"])])
    names, handle = shipped[-1]
    for n, r in zip(names, _exchange_wait(handle, small_out["conv_w"][0], "scatter_wait_" + names[0])):
        big[n] = _adamw_rows_apart(r, w[n], m[n], v[n], "adamw_" + n)

    result = [summed[_LOSS_ROW, 0], grad_x[None]]
    for i in range(4):
        for n in _ORDER:
            if n == "w_in":
                result.append(jnp.transpose(big[n][i], (1, 2, 0)))
            elif n in _MATRICES:
                result.append(big[n][i][None])
            elif n == "conv_w":
                result.append(small_out[n][i][None])
            else:
                result.append(small_out[n][i].reshape(w[n].shape))
    return tuple(result)
```

```python
import jax
import jax.numpy as jnp
import numpy as np
from jax import lax
from jax.experimental import pallas as pl
from jax.experimental.pallas import tpu as pltpu

F32, BF16 = jnp.float32, jnp.bfloat16
EPS = 1e-6
D_MODEL = 1024
N_DEV = 8
ATTN_BLOCK = 128
HEAD_PAIR = 128
DN_HEADS = 4
DN_DIM = 128
DN_CHUNK = 64
DN_CONV = 4
ROPE_THETA = 10000.0
D_IN = 2824
D_IN_PAD = 3072
BLK_Q, BLK_Z = 0, 1
BLK_DN, BLK_K, BLK_V, BLK_G = 8, 20, 21, 22
BLK_G_PAD = 11
VMEM_LIMIT = 56 * 1024 * 1024
NEG = -1e30
ADAM_LR, ADAM_B1, ADAM_B2, ADAM_EPS, ADAM_WD, ADAM_STEP = 0.001, 0.9, 0.999, 1e-08, 0.01, 10
MESH = pl.DeviceIdType.MESH


def _bf(x):
    return x.astype(BF16)


def _dot(a, b):
    return jnp.dot(a, b, preferred_element_type=F32)


def _dot_nt(a, b):
    return lax.dot_general(a, b, (((1,), (1,)), ((), ())), preferred_element_type=F32)


def _dot_tn(a, b):
    return lax.dot_general(a, b, (((0,), (0,)), ((), ())), preferred_element_type=F32)


def _sigmoid(x):
    return 1.0 / (1.0 + jnp.exp(-x))


def _params(sem):
    return pltpu.CompilerParams(dimension_semantics=sem, vmem_limit_bytes=VMEM_LIMIT)


def _mm(x, w, *, form, name, out_dtypes, tn, epi=None, extra=(), tm=512, w_row_block=0, after=None, norm=None,
        norm_bwd=None, then_nt=None):
    assert norm is None or norm_bwd is None
    xs = list(x) if isinstance(x, (list, tuple)) else [x]
    nx = len(xs)
    S, K = xs[0].shape
    shards = w.ndim == 3
    N = (w.shape[2] * N_DEV if shards else w.shape[1]) if form == "nn" else w.shape[-2]
    assert not (shards and form == "nn" and tn != w.shape[2]) and (nx == 1 or (form == "nn" and not shards and norm is None))
    r0 = w_row_block * K
    tm = min(tm, S)
    n_extra, n_out = len(extra), len(out_dtypes)
    tile = lambda width: pl.BlockSpec((tm, width), lambda i: (i, 0))
    whole = lambda a: pl.BlockSpec(a.shape, lambda i, nd=a.ndim: (0,) * nd)
    ins, in_specs = [*xs, w, *extra], [tile(K)] * nx + [whole(w)] + [tile(N)] * n_extra
    if norm is not None:
        ins, in_specs = ins + [norm], in_specs + [whole(norm)]
    if norm_bwd is not None:
        ins, in_specs = ins + list(norm_bwd), in_specs + [tile(N), whole(norm_bwd[1]), tile(N)]
    if then_nt is not None:
        ins, in_specs = ins + [then_nt], in_specs + [whole(then_nt)]
    if after is not None:
        ins, in_specs = ins + [after], in_specs + [whole(after)]
    out_shape = [jax.ShapeDtypeStruct((S, N), dt) for dt in out_dtypes]
    out_specs = [tile(N)] * n_out
    if norm is not None:
        out_shape, out_specs = out_shape + [jax.ShapeDtypeStruct((S, K), BF16)], out_specs + [tile(K)]
    if norm_bwd is not None:
        out_shape, out_specs = out_shape + [jax.ShapeDtypeStruct((1, N), F32)], out_specs + [pl.BlockSpec((1, N), lambda i: (0, 0))]
    if then_nt is not None:
        out_shape, out_specs = out_shape + [jax.ShapeDtypeStruct((S, then_nt.shape[0]), F32)], out_specs + [tile(then_nt.shape[0])]

    def product(xb, w_ref, cols, c):
        if form == "nn" and nx > 1:
            return sum(_dot(part, w_ref[r0 + p * K:r0 + (p + 1) * K, cols]) for p, part in enumerate(xb))
        if form == "nn":
            return _dot(xb, w_ref[c] if shards else w_ref[r0:r0 + K, cols])
        if not shards:
            return _dot_nt(xb, w_ref[cols, :])
        ks = w.shape[2]
        acc = _dot_nt(xb[:, 0:ks], w_ref[0, cols, :])
        for s in range(1, N_DEV):
            acc = acc + _dot_nt(xb[:, s * ks:(s + 1) * ks], w_ref[s, cols, :])
        return acc

    def body(*refs):
        x_ref, w_ref = refs[0], refs[nx]
        extra_refs = refs[nx + 1:nx + 1 + n_extra]
        at = nx + 1 + n_extra
        if norm is not None:
            gain_ref, at = refs[at], at + 1
        if norm_bwd is not None:
            (y_ref, ygain_ref, dres_ref), at = refs[at:at + 3], at + 3
        if then_nt is not None:
            w2_ref, at = refs[at], at + 1
        outs = refs[len(ins):]
        if norm is not None:
            _, xh = _rms_stats(x_ref[...])
            xb = _bf(xh * gain_ref[...])
            outs[n_out][...] = xb
        else:
            xb = _bf(x_ref[...]) if nx == 1 else [_bf(r[...]) for r in refs[:nx]]
        for c in range(N // tn):
            cols = slice(c * tn, (c + 1) * tn)
            acc = product(xb, w_ref, cols, c)
            res = epi(acc, *[r[:, cols] for r in extra_refs]) if epi else (acc,)
            for o, r in zip(outs[:n_out], res):
                o[:, cols] = r.astype(o.dtype)
        if norm_bwd is not None:
            dx, dg = _rms_bwd_tile(y_ref[...], ygain_ref[...], outs[0][...])
            outs[0][...] = dres_ref[...] + dx
            dg_ref = outs[n_out]

            @pl.when(pl.program_id(0) == 0)
            def _():
                dg_ref[...] = jnp.zeros_like(dg_ref)

            dg_ref[...] += dg
        if then_nt is not None:
            yb = _bf(outs[0][...])
            for c in range(then_nt.shape[0] // tn):
                cols = slice(c * tn, (c + 1) * tn)
                outs[-1][:, cols] = _dot_nt(yb, w2_ref[cols, :])

    return pl.pallas_call(
        body, grid=(S // tm,), name=name, in_specs=in_specs, out_specs=out_specs, out_shape=out_shape,
        compiler_params=_params(("arbitrary",) if norm_bwd is not None else ("parallel",)),
    )(*ins)


def _mlp_fwd(h1, w_up, w_down, gain):
    S, K = h1.shape
    n_sh, _, fs = w_up.shape
    tm = min(512, S)

    def body(x_ref, wup_ref, wdown_ref, g_ref, hid_ref, relu_ref, m_ref, h2_ref):
        x = x_ref[...]
        _, xh = _rms_stats(x)
        mb = _bf(xh * g_ref[...])
        m_ref[...] = mb
        h2_ref[...] = x
        for c in range(n_sh):
            cols = slice(c * fs, (c + 1) * fs)
            r = jnp.maximum(_dot(mb, wup_ref[c]), 0.0)
            hd = _bf(r * r)
            hid_ref[:, cols] = hd
            relu_ref[:, cols] = _bf(r)
            h2_ref[...] += _dot(hd, wdown_ref[cols, :])

    tile = lambda width: pl.BlockSpec((tm, width), lambda i: (i, 0))
    once = lambda a: pl.BlockSpec(a.shape, lambda i, nd=a.ndim: (0,) * nd, pipeline_mode=pl.Buffered(1))
    F = n_sh * fs
    return pl.pallas_call(
        body, grid=(S // tm,), name="mlp_fwd",
        in_specs=[tile(K), once(w_up), once(w_down), pl.BlockSpec(gain.shape, lambda i: (0, 0))],
        out_specs=[tile(F), tile(F), tile(K), tile(K)],
        out_shape=[jax.ShapeDtypeStruct((S, F), BF16), jax.ShapeDtypeStruct((S, F), BF16),
                   jax.ShapeDtypeStruct((S, K), BF16), jax.ShapeDtypeStruct((S, K), F32)],
        compiler_params=_params(("parallel",)),
    )(h1, w_up, w_down, gain)


def _mm_tn(x, dy, *, name, tm, tn, out_dtype=F32, column_shards=False, after=None):
    xs = list(x) if isinstance(x, (list, tuple)) else [x]
    S, N = dy.shape
    K = x.shape[1] if len(xs) == 1 else tm * len(xs)
    waits = [] if after is None else [after]

    def body(*refs):
        dy_ref, out_ref = refs[len(xs)], refs[-1]
        if len(xs) == 1:
            out_ref[...] = _dot_tn(_bf(refs[0][...]), _bf(dy_ref[...])).astype(out_dtype)
        for k in range(len(xs) if len(xs) > 1 else 0):
            @pl.when(pl.program_id(0) == k)
            def _(k=k):
                out_ref[...] = _dot_tn(_bf(refs[k][...]), _bf(dy_ref[...])).astype(out_dtype)

    if column_shards:
        out_spec = pl.BlockSpec((None, tm, tn), lambda i, j: (j, i, 0))
        out_shape = jax.ShapeDtypeStruct((N // tn, K, tn), out_dtype)
    else:
        out_spec = pl.BlockSpec((tm, tn), lambda i, j: (i, j))
        out_shape = jax.ShapeDtypeStruct((K, N), out_dtype)
    return pl.pallas_call(
        body, grid=(K // tm, N // tn), name=name,
        in_specs=([pl.BlockSpec((S, tm), lambda i, j: (0, i))] if len(xs) == 1 else [pl.BlockSpec((S, tm), lambda i, j: (0, 0))] * len(xs))
        + [pl.BlockSpec((S, tn), lambda i, j: (0, j))] + [pl.BlockSpec(memory_space=pl.ANY)] * len(waits),
        out_specs=out_spec, out_shape=out_shape,
        compiler_params=_params(("parallel", "parallel")),
    )(*xs, dy, *waits)


def _rowwise(body, *, tiled, full, out_tiled, out_acc, name, tm=512, smem=()):
    S = tiled[0].shape[0]
    tm = min(tm, S)
    n_in = len(smem) + len(tiled) + len(full)

    def kern(*refs):
        @pl.when(pl.program_id(0) == 0)
        def _():
            for r in refs[n_in + len(out_tiled):]:
                r[...] = jnp.zeros_like(r)
        body(*refs)

    in_specs = [pl.BlockSpec(memory_space=pltpu.SMEM) for _ in smem]
    in_specs += [pl.BlockSpec((tm, a.shape[1]), lambda i: (i, 0)) for a in tiled]
    in_specs += [pl.BlockSpec(a.shape, lambda i, nd=a.ndim: (0,) * nd) for a in full]
    out_specs = [pl.BlockSpec((tm, w), lambda i: (i, 0)) for w, _ in out_tiled]
    out_specs += [pl.BlockSpec(shp, lambda i, nd=len(shp): (0,) * nd) for shp, _ in out_acc]
    out_shape = [jax.ShapeDtypeStruct((S, w), dt) for w, dt in out_tiled]
    out_shape += [jax.ShapeDtypeStruct(shp, dt) for shp, dt in out_acc]
    return pl.pallas_call(
        kern, grid=(S // tm,), name=name, in_specs=in_specs, out_specs=out_specs, out_shape=out_shape,
        compiler_params=_params(("arbitrary",)),
    )(*smem, *tiled, *full)


def _rms_stats(x):
    r = lax.rsqrt(jnp.mean(x * x, axis=-1, keepdims=True) + EPS)
    return r, x * r


def _rmsnorm_fwd(x, g, name, after):
    def body(x_ref, g_ref, _, o_ref):
        _, xh = _rms_stats(x_ref[...])
        o_ref[...] = _bf(xh * g_ref[...])

    return _rowwise(body, tiled=[x], full=[g, after], out_tiled=[(x.shape[1], BF16)], out_acc=[], name=name)[0]


def _rms_bwd_tile(x, g, dxn):
    r, xh = _rms_stats(x)
    dg = jnp.sum(dxn * xh, axis=0, keepdims=True)
    dn = dxn * g
    dx = r * (dn - xh * jnp.mean(dn * xh, axis=-1, keepdims=True))
    return dx, dg


def _ple_and_loss(h2, p, target, w_pg, w_pp, g_ple, g_final):
    S, n = h2.shape
    tm = min(512, S)
    tn = 512

    def body(h2_ref, p_ref, t_ref, wpg_ref, wpp_ref, gple_ref, gfin_ref,
             n3_ref, dh_ref, dgl_ref, dpp_ref, loss_ref, dg_ref, dgple_ref, pp, gate, h3):
        @pl.when(pl.program_id(0) == 0)
        def _():
            loss_ref[...] = jnp.zeros_like(loss_ref)
            dg_ref[...] = jnp.zeros_like(dg_ref)
            dgple_ref[...] = jnp.zeros_like(dgple_ref)

        x = h2_ref[...]
        _, xh = _rms_stats(x)
        n3 = _bf(xh * gple_ref[...])
        n3_ref[...] = n3
        pb = _bf(p_ref[...])
        for c in range(n // tn):
            cols = slice(c * tn, (c + 1) * tn)
            pp[:, cols] = _dot(pb, wpp_ref[:, cols])
            gt = _sigmoid(_dot(n3, wpg_ref[:, cols]))
            gate[:, cols] = gt
            h3[:, cols] = x[:, cols] + gt * pp[:, cols]
        y = h3[...]
        _, yh = _rms_stats(y)
        e = yh * gfin_ref[...] - t_ref[...]
        per_tok = jnp.mean(e * e, axis=-1, keepdims=True)
        loss_ref[...] += 0.5 * jnp.sum(per_tok, axis=0, keepdims=True)
        dh, dg = _rms_bwd_tile(y, gfin_ref[...], e * (1.0 / n))
        dg_ref[...] += dg
        gt = gate[...]
        dgl = _bf(dh * pp[...] * gt * (1.0 - gt))
        dgl_ref[...] = dgl
        dpp_ref[...] = _bf(dh * gt)
        for c in range(n // tn):
            cols = slice(c * tn, (c + 1) * tn)
            h3[:, cols] = _dot_nt(dgl, wpg_ref[cols, :])
        dx, dgp = _rms_bwd_tile(x, gple_ref[...], h3[...])
        dh_ref[...] = dh + dx
        dgple_ref[...] += dgp

    tile = lambda width: pl.BlockSpec((tm, width), lambda i: (i, 0))
    whole = lambda a: pl.BlockSpec(a.shape, lambda i, nd=a.ndim: (0,) * nd)
    return pl.pallas_call(
        body, grid=(S // tm,), name="ple_and_loss",
        in_specs=[tile(n), tile(p.shape[1]), tile(n), whole(w_pg), whole(w_pp), whole(g_ple), whole(g_final)],
        out_specs=[tile(n), tile(n), tile(n), tile(n), pl.BlockSpec((1, 128), lambda i: (0, 0)),
                   pl.BlockSpec((1, n), lambda i: (0, 0)), pl.BlockSpec((1, n), lambda i: (0, 0))],
        out_shape=[jax.ShapeDtypeStruct((S, n), BF16), jax.ShapeDtypeStruct((S, n), F32), jax.ShapeDtypeStruct((S, n), BF16),
                   jax.ShapeDtypeStruct((S, n), BF16), jax.ShapeDtypeStruct((1, 128), F32), jax.ShapeDtypeStruct((1, n), F32),
                   jax.ShapeDtypeStruct((1, n), F32)],
        scratch_shapes=[pltpu.VMEM((tm, n), F32)] * 3,
        compiler_params=_params(("arbitrary",)),
    )(h2, p, target, w_pg, w_pp, g_ple, g_final)


def _rope_tables(S):
    half = 32
    inv = (1.0 / (np.float32(ROPE_THETA) ** (np.arange(half, dtype=np.float32) * np.float32(2.0 / 64)))).astype(np.float32)
    ang = np.arange(S).astype(np.float32)[:, None] * inv[None, :]
    cos, sin = np.cos(ang), np.sin(ang)
    return jnp.asarray(np.tile(cos, (1, 4))), jnp.asarray(np.concatenate([-sin, sin, -sin, sin], axis=1))


def _attn_common(i, kc, kp, vc, vp, cc, sc, cp, sp):
    lane = lax.broadcasted_iota(jnp.int32, (1, HEAD_PAIR), 1)
    lane_lo = jnp.bitwise_and(lane, 63) < 32
    slot = [lane < 64, lane >= 64]

    def swap_halves(t):
        return jnp.where(lane_lo, pltpu.roll(t, 96, 1), pltpu.roll(t, 32, 1))

    def rope(t, cos, sin):
        return t * cos + swap_halves(t) * sin

    def unrope(d, cos, sin):
        return d * cos + swap_halves(d * sin)

    k2 = jnp.concatenate([rope(kp, cp, sp), rope(kc, cc, sc)], axis=0)
    v2 = jnp.concatenate([vp, vc], axis=0)
    r = lax.broadcasted_iota(jnp.int32, (ATTN_BLOCK, 2 * ATTN_BLOCK), 0)
    c = lax.broadcasted_iota(jnp.int32, (ATTN_BLOCK, 2 * ATTN_BLOCK), 1)
    valid = (c > r) & (c <= r + ATTN_BLOCK) & jnp.logical_or(c >= ATTN_BLOCK, i > 0)
    ks, vs = {}, {}
    for j in range(2):
        kn = jnp.where(slot[j], k2, 0.0)
        vn = jnp.where(slot[j], v2, 0.0)
        for s in range(2):
            ks[j, s] = _bf(kn if s == j else pltpu.roll(kn, 64, 1))
            vs[j, s] = _bf(vn if s == j else pltpu.roll(vn, 64, 1))
    return slot, rope, unrope, valid, ks, vs


def _attn_probs(scores, valid, sink):
    s = jnp.where(valid, scores * 0.125, NEG)
    m = jnp.maximum(jnp.max(s, axis=1, keepdims=True), sink)
    e = jnp.exp(s - m)
    z = jnp.sum(e, axis=1, keepdims=True) + jnp.exp(sink - m)
    return e * (1.0 / z), m + jnp.log(z)


def _attn_specs(S):
    nb = S // ATTN_BLOCK
    prev = lambda i: jnp.maximum(i - 1, 0)
    blk = lambda w, col, row=(lambda i: i): pl.BlockSpec((ATTN_BLOCK, w), lambda i: (row(i), col))
    in_specs = [pl.BlockSpec(memory_space=pltpu.SMEM),
                blk(512, BLK_Q), blk(128, BLK_K), blk(128, BLK_K, prev), blk(128, BLK_V), blk(128, BLK_V, prev),
                blk(128, 0), blk(128, 0), blk(128, 0, prev), blk(128, 0, prev)]
    return nb, in_specs


def _attn_fwd(pa, cos, sin, sinks):
    S = pa.shape[0]
    nb, in_specs = _attn_specs(S)

    def body(sinks_ref, q_ref, kc_ref, kp_ref, vc_ref, vp_ref, cc_ref, sc_ref, cp_ref, sp_ref, o_ref, lse_ref):
        i = pl.program_id(0)
        lane = lax.broadcasted_iota(jnp.int32, (1, HEAD_PAIR), 1)
        cc, sc = cc_ref[...], sc_ref[...]
        _, rope, _, valid, ks, vs = _attn_common(i, kc_ref[...], kp_ref[...], vc_ref[...], vp_ref[...],
                                                 cc, sc, cp_ref[...], sp_ref[...])
        pair_cols = [slice(HEAD_PAIR * pair, HEAD_PAIR * (pair + 1)) for pair in range(4)]
        qps = [_bf(rope(q_ref[:, cols], cc, sc)) for cols in pair_cols]
        outs, lses = {}, {}

        def head_program(h):
            pair, s = divmod(h, 2)
            j = h // 4
            scores = _dot_nt(qps[pair], ks[j, s])
            yield
            p, lse = _attn_probs(scores, valid, sinks_ref[h])
            outs[h] = _dot(_bf(p), vs[j, s])
            lses[h] = jnp.where(lane == h, lse, 0.0)

        _interleave(head_program(h) for h in range(8))
        for pair, cols in enumerate(pair_cols):
            o_ref[:, cols] = outs[2 * pair] + outs[2 * pair + 1]
        lse_ref[...] = sum((lses[h] for h in range(1, 8)), lses[0])

    return pl.pallas_call(
        body, grid=(nb,), name="attn_fwd", in_specs=in_specs,
        out_specs=[pl.BlockSpec((ATTN_BLOCK, 512), lambda i: (i, 0)), pl.BlockSpec((ATTN_BLOCK, 128), lambda i: (i, 0))],
        out_shape=[jax.ShapeDtypeStruct((S, 512), F32), jax.ShapeDtypeStruct((S, 128), F32)],
        compiler_params=_params(("parallel",)),
    )(sinks, pa, pa, pa, pa, pa, cos, sin, cos, sin)


def _attn_bwd(pa, cos, sin, sinks, dcat, attn, lse, after):
    S = pa.shape[0]
    nb, in_specs = _attn_specs(S)
    in_specs = in_specs + [pl.BlockSpec((ATTN_BLOCK, 512), lambda i: (i, 0))] * 2 + [pl.BlockSpec((ATTN_BLOCK, 128), lambda i: (i, 0))]
    in_specs = in_specs + [pl.BlockSpec(memory_space=pl.ANY)]

    def body(sinks_ref, q_ref, kc_ref, kp_ref, vc_ref, vp_ref, cc_ref, sc_ref, cp_ref, sp_ref, do_ref, o_ref, lse_ref, _,
             dq_ref, dk_ref, dv_ref, dsink_ref):
        i = pl.program_id(0)

        @pl.when(i == 0)
        def _():
            dk_ref[...] = jnp.zeros_like(dk_ref)
            dv_ref[...] = jnp.zeros_like(dv_ref)
            dsink_ref[...] = jnp.zeros_like(dsink_ref)

        cc, sc, cp, sp = cc_ref[...], sc_ref[...], cp_ref[...], sp_ref[...]
        slot, rope, unrope, valid, ks, vs = _attn_common(i, kc_ref[...], kp_ref[...], vc_ref[...], vp_ref[...], cc, sc, cp, sp)
        pair_cols = [slice(HEAD_PAIR * pair, HEAD_PAIR * (pair + 1)) for pair in range(4)]
        qps = [_bf(rope(q_ref[:, cols], cc, sc)) for cols in pair_cols]
        dobs = [_bf(do_ref[:, cols]) for cols in pair_cols]
        do_o = [do_ref[:, cols] * o_ref[:, cols] for cols in pair_cols]
        dqs, dks, dvs = {}, {}, {}

        def head_program(h):
            pair, s = divmod(h, 2)
            j = h // 4
            qp, dob = qps[pair], dobs[pair]
            scores = _dot_nt(qp, ks[j, s])
            dp = _dot_nt(dob, vs[j, s])
            yield
            lse_h = lse_ref[:, h:h + 1]
            p = jnp.exp(jnp.where(valid, scores * 0.125, NEG) - lse_h)
            yield
            dr = jnp.sum(jnp.where(slot[s], do_o[pair], 0.0), axis=1, keepdims=True)
            ds = _bf(p * (dp - dr) * 0.125)
            yield
            dsink_ref[h:h + 1, :] += -jnp.sum(jnp.exp(sinks_ref[h] - lse_h) * dr, axis=0, keepdims=True)
            dqs[h] = _dot(ds, ks[j, s])
            dk_h = _dot_tn(ds, qp)
            dv_h = _dot_tn(_bf(p), dob)
            yield
            dk_h, dv_h = jnp.where(slot[s], dk_h, 0.0), jnp.where(slot[s], dv_h, 0.0)
            if s != j:
                dk_h, dv_h = pltpu.roll(dk_h, 64, 1), pltpu.roll(dv_h, 64, 1)
            dks[h], dvs[h] = dk_h, dv_h

        _interleave(head_program(h) for h in range(8))
        dk2 = sum((dks[h] for h in range(1, 8)), dks[0])
        dv2 = sum((dvs[h] for h in range(1, 8)), dvs[0])
        for pair, cols in enumerate(pair_cols):
            dq_ref[:, cols] = _bf(unrope(dqs[2 * pair] + dqs[2 * pair + 1], cc, sc))
        cur = pl.ds(pl.multiple_of(i * ATTN_BLOCK, ATTN_BLOCK), ATTN_BLOCK)
        dk_ref[cur, :] += unrope(dk2[ATTN_BLOCK:], cc, sc)
        dv_ref[cur, :] += dv2[ATTN_BLOCK:]

        @pl.when(i > 0)
        def _():
            prv = pl.ds(pl.multiple_of((i - 1) * ATTN_BLOCK, ATTN_BLOCK), ATTN_BLOCK)
            dk_ref[prv, :] += unrope(dk2[:ATTN_BLOCK], cp, sp)
            dv_ref[prv, :] += dv2[:ATTN_BLOCK]

    whole = lambda w: pl.BlockSpec((S, w), lambda i: (0, 0))
    return pl.pallas_call(
        body, grid=(nb,), name="attn_bwd", in_specs=in_specs,
        out_specs=[pl.BlockSpec((ATTN_BLOCK, 512), lambda i: (i, BLK_Q)), whole(128), whole(128),
                   pl.BlockSpec((8, 128), lambda i: (0, 0))],
        out_shape=[jax.ShapeDtypeStruct((S, D_IN_PAD), BF16), jax.ShapeDtypeStruct((S, 128), F32),
                   jax.ShapeDtypeStruct((S, 128), F32), jax.ShapeDtypeStruct((8, 128), F32)],
        compiler_params=_params(("arbitrary",)),
    )(sinks, pa, pa, pa, pa, pa, cos, sin, cos, sin, dcat, attn, lse, after)


CONV_ROWS = 512
CONV_PAD = 8


def _conv_silu(scr, w, r0):
    y = w[3:4, :] * scr[pl.ds(CONV_PAD + r0, CONV_ROWS), :]
    for j in range(DN_CONV - 1):
        y = y + w[j:j + 1, :] * scr[pl.ds(CONV_PAD + r0 - 3 + j, CONV_ROWS), :]
    return y


def _dn_prep_fwd(pd, conv_w):
    S = pd.shape[0]
    assert S % CONV_ROWS == 0

    def body(x_ref, w_ref, o_ref, scr):
        b = pl.program_id(0)
        scr[0:CONV_PAD, :] = jnp.zeros((CONV_PAD, DN_DIM), F32)
        scr[pl.ds(CONV_PAD, S), :] = x_ref[...]
        w = w_ref[...]
        q_scale = jnp.where(b < DN_HEADS, DN_DIM ** -0.5, 1.0)
        for r0 in range(0, S, CONV_ROWS):
            y = _conv_silu(scr, w, r0)
            a = y * _sigmoid(y)
            rs = lax.rsqrt(jnp.sum(a * a, axis=1, keepdims=True) + EPS)
            o_ref[pl.ds(r0, CONV_ROWS), :] = a * jnp.where(b < 2 * DN_HEADS, rs * q_scale, 1.0)

    col = pl.BlockSpec((S, DN_DIM), lambda b: (0, b))
    return pl.pallas_call(
        body, grid=(3 * DN_HEADS,), name="dn_prep_fwd",
        in_specs=[pl.BlockSpec((S, DN_DIM), lambda b: (0, BLK_DN + b)), pl.BlockSpec((DN_CONV, DN_DIM), lambda b: (0, b))],
        out_specs=col,
        out_shape=jax.ShapeDtypeStruct((S, 3 * DN_HEADS * DN_DIM), F32),
        scratch_shapes=[pltpu.VMEM((S + CONV_PAD, DN_DIM), F32)],
        compiler_params=_params(("parallel",)),
    )(pd, conv_w)


def _dn_prep_bwd(pd, conv_w, dqkv, dproj, dk, dv):
    S = pd.shape[0]
    NB = 3 * DN_HEADS

    def body(x_ref, w_ref, d_ref, _, dk_ref, dv_ref, dx_ref, dw_ref, scr, dscr):
        b = pl.program_id(0)

        @pl.when(b == NB)
        def _():
            dx_ref[...] = _bf(dk_ref[...])

        @pl.when(b == NB + 1)
        def _():
            dx_ref[...] = _bf(dv_ref[...])

        @pl.when(b < NB)
        def _():
            scr[0:CONV_PAD, :] = jnp.zeros((CONV_PAD, DN_DIM), F32)
            scr[pl.ds(CONV_PAD, S), :] = x_ref[...]
            dscr[pl.ds(S, CONV_PAD), :] = jnp.zeros((CONV_PAD, DN_DIM), F32)
            w = w_ref[...]
            q_scale = jnp.where(b < DN_HEADS, DN_DIM ** -0.5, 1.0)
            is_qk = b < 2 * DN_HEADS
            dw = [jnp.zeros((1, DN_DIM), F32) for _ in range(DN_CONV)]
            for r0 in range(0, S, CONV_ROWS):
                y = _conv_silu(scr, w, r0)
                sg = _sigmoid(y)
                a = y * sg
                dout = d_ref[pl.ds(r0, CONV_ROWS), :]
                rs = lax.rsqrt(jnp.sum(a * a, axis=1, keepdims=True) + EPS)
                da_qk = q_scale * rs * (dout - a * (rs * rs) * jnp.sum(dout * a, axis=1, keepdims=True))
                dy = jnp.where(is_qk, da_qk, dout) * (sg * (1.0 + y * (1.0 - sg)))
                dscr[pl.ds(r0, CONV_ROWS), :] = dy
                for j in range(DN_CONV):
                    dw[j] = dw[j] + jnp.sum(dy * scr[pl.ds(CONV_PAD + r0 - 3 + j, CONV_ROWS), :], axis=0, keepdims=True)
            for j in range(DN_CONV):
                dw_ref[j:j + 1, :] = dw[j]
            for r0 in range(0, S, CONV_ROWS):
                dx = w[3:4, :] * dscr[pl.ds(r0, CONV_ROWS), :]
                for j in range(DN_CONV - 1):
                    dx = dx + w[j:j + 1, :] * dscr[pl.ds(r0 + 3 - j, CONV_ROWS), :]
                dx_ref[pl.ds(r0, CONV_ROWS), :] = _bf(dx)

    own = lambda b: jnp.minimum(b, NB - 1)
    col = pl.BlockSpec((S, DN_DIM), lambda b: (0, own(b)))
    proj_col = pl.BlockSpec((S, DN_DIM), lambda b: (0, BLK_DN + own(b)))
    wcol = pl.BlockSpec((DN_CONV, DN_DIM), lambda b: (0, own(b)))
    whole = pl.BlockSpec((S, DN_DIM), lambda b: (0, 0))
    assert BLK_K == BLK_DN + NB and BLK_V == BLK_K + 1
    return pl.pallas_call(
        body, grid=(NB + 2,), name="dn_prep_bwd",
        in_specs=[proj_col, wcol, col, pl.BlockSpec(memory_space=pl.ANY), whole, whole],
        out_specs=[pl.BlockSpec((S, DN_DIM), lambda b: (0, BLK_DN + b)), wcol],
        out_shape=[jax.ShapeDtypeStruct(dproj.shape, dproj.dtype), jax.ShapeDtypeStruct((DN_CONV, 3 * DN_HEADS * DN_DIM), F32)],
        scratch_shapes=[pltpu.VMEM((S + CONV_PAD, DN_DIM), F32), pltpu.VMEM((S + CONV_PAD, DN_DIM), F32)],
        input_output_aliases={3: 0},
        compiler_params=_params(("arbitrary",)),
    )(pd, conv_w, dqkv, dproj, dk, dv)


CPAD = 128
CHUNKS_LOCAL = 4
CHUNKS_SCAN = 8


def _chunk_masks():
    ii = lax.broadcasted_iota(jnp.int32, (DN_CHUNK, CPAD), 0)
    jj = lax.broadcasted_iota(jnp.int32, (DN_CHUNK, CPAD), 1)
    return ii, jj


def _rows_pad(a):
    return jnp.concatenate([a, jnp.zeros_like(a)], axis=0)


def _hi_lo(a):
    hi = _bf(a)
    return hi, _bf(a - hi.astype(F32))


def _double_step(t, p):
    C = DN_CHUNK
    th, tl = _hi_lo(t)
    ph, pl_ = _hi_lo(p)
    r1 = _dot(jnp.concatenate([th, tl, ph, pl_], axis=0), _rows_pad(ph))
    r2 = _dot(jnp.concatenate([th, ph], axis=0), _rows_pad(pl_))
    return t + (r1[:C] + r1[C:2 * C] + r2[:C]), r1[2 * C:3 * C] + r1[3 * C:] + r2[C:]


def _dot3_nt(a, b):
    C = DN_CHUNK
    ah, al = _hi_lo(a)
    bh, bl = _hi_lo(b)
    r1 = _dot_nt(jnp.concatenate([ah, al], axis=0), _rows_pad(bh))
    return r1[:C] + r1[C:] + _dot_nt(ah, _rows_pad(bl))


def _dot3_tn(a, b):
    C = DN_CHUNK
    ah, al = _hi_lo(a)
    bh, bl = _hi_lo(b)
    return _dot_tn(jnp.concatenate([ah, al, ah], axis=0), jnp.concatenate([bh, bh, bl], axis=0))[:C]


def _interleave(programs):
    programs = list(programs)
    while programs:
        alive = []
        for prog in programs:
            try:
                next(prog)
                alive.append(prog)
            except StopIteration:
                pass
        programs = alive


def _col_to_row(col, ii, jj):
    return jnp.sum(jnp.where(ii == jj, col, 0.0), axis=0, keepdims=True)


def _row_to_col(row, ii, jj):
    return jnp.sum(jnp.where(ii == jj, row, 0.0), axis=1, keepdims=True)


def _decay(gc_col, ii, jj):
    diff = gc_col - _col_to_row(gc_col, ii, jj)
    return jnp.where(jj <= ii, jnp.exp(jnp.where(jj <= ii, diff, 0.0)), 0.0)


def _softplus(x):
    return jnp.maximum(x, 0.0) + jnp.log(1.0 + jnp.exp(-jnp.abs(x)))


def _head(h):
    return slice(DN_DIM * h, DN_DIM * (h + 1))


def _dn_chunk_fwd(qkv, pg, a_log, dt_bias):
    S = qkv.shape[0]
    C = DN_CHUNK
    G = CHUNKS_LOCAL
    R = G * C
    steps = S // R

    def body(alog_ref, dtb_ref, qkv_ref, pg_ref, w_ref, u_ref, qg_ref, kd_ref, a_ref, t_ref, gcs_ref):
        ii, jj = _chunk_masks()
        lane = lax.broadcasted_iota(jnp.int32, (1, 128), 1)
        eye = (ii == jj).astype(F32)
        gcs_parts = [[] for _ in range(G)]

        def head_program(chunk, h):
            rows = slice(chunk * C, (chunk + 1) * C)
            q, k, v = qkv_ref[rows, _head(h)], qkv_ref[rows, _head(DN_HEADS + h)], qkv_ref[rows, _head(2 * DN_HEADS + h)]
            beta = _sigmoid(pg_ref[rows, h:h + 1])
            g_col = -jnp.exp(alog_ref[h]) * _softplus(pg_ref[rows, DN_HEADS + h:DN_HEADS + h + 1] + dtb_ref[h])
            g_row = _col_to_row(g_col, ii, jj)
            gc_col = jnp.sum(jnp.where(jj <= ii, g_row, 0.0), axis=1, keepdims=True)
            dec = _decay(gc_col, ii, jj)
            eg = jnp.exp(gc_col)
            kb, vb = k * beta, v * beta
            k_rows = _rows_pad(_bf(k))
            kk = _dot_nt(_bf(kb), k_rows)
            qk = _dot_nt(_bf(q), k_rows)
            yield
            t, pw = eye, -jnp.where(jj < ii, kk * dec, 0.0)
            for _ in range(6):
                t, pw = _double_step(t, pw)
                yield
            tb = _bf(t)
            u_ref[rows, _head(h)] = _dot(tb, _rows_pad(_bf(vb)))
            w_ref[rows, _head(h)] = _bf(_dot(tb, _rows_pad(_bf(kb * eg))))
            a_ref[h, rows] = _bf(qk * dec)
            t_ref[h, rows] = t
            qg_ref[rows, _head(h)] = _bf(q * eg)
            kd_ref[rows, _head(h)] = _bf(k * jnp.exp(gc_col[C - 1:C, :] - gc_col))
            gcs_parts[chunk].append(jnp.where(lane == h, gc_col, 0.0) + jnp.where(lane == DN_HEADS + h, beta, 0.0)
                                    + jnp.where(lane == 2 * DN_HEADS + h, g_col, 0.0))

        _interleave(head_program(chunk, h) for chunk in range(G) for h in range(DN_HEADS))
        for chunk in range(G):
            gcs_ref[chunk * C:(chunk + 1) * C, :] = sum(gcs_parts[chunk][1:], gcs_parts[chunk][0])

    smem = pl.BlockSpec(memory_space=pltpu.SMEM)
    wide = pl.BlockSpec((R, 512), lambda n: (n, 0))
    sq = pl.BlockSpec((DN_HEADS, R, CPAD), lambda n: (0, n, 0))
    narrow = pl.BlockSpec((R, 128), lambda n: (n, 0))
    f = lambda *shp: jax.ShapeDtypeStruct(shp, F32)
    b = lambda *shp: jax.ShapeDtypeStruct(shp, BF16)
    return pl.pallas_call(
        body, grid=(steps,), name="dn_chunk_fwd",
        in_specs=[smem, smem, pl.BlockSpec((R, 1536), lambda n: (n, 0)), pl.BlockSpec((R, 128), lambda n: (n, BLK_G))],
        out_specs=[wide, wide, wide, wide, sq, sq, narrow],
        out_shape=[b(S, 512), f(S, 512), b(S, 512), b(S, 512), b(DN_HEADS, S, CPAD), f(DN_HEADS, S, CPAD), f(S, 128)],
        compiler_params=_params(("parallel",)),
    )(a_log, dt_bias, qkv, pg)


def _gated_norm(o, z, gn):
    r, oh = _rms_stats(o)
    return oh * gn * (z * _sigmoid(z))


def _dn_scan_fwd(w, u, qg, kd, a, gcs, pz, gn):
    S = w.shape[0]
    C = DN_CHUNK
    nc = S // C
    G = CHUNKS_SCAN
    R = G * C

    def body(w_ref, u_ref, qg_ref, kd_ref, a_ref, gcs_ref, z_ref, gn_ref, o_ref, vn_ref, sst_ref, out_ref, state):
        @pl.when(pl.program_id(0) == 0)
        def _():
            state[...] = jnp.zeros_like(state)

        def head_program(chunk, h):
            hs = _head(h)
            rows = slice(chunk * C, (chunk + 1) * C)
            s_in = state[h]
            sb = _bf(s_in)
            sst_ref[chunk, h] = sb
            w_s = _dot(w_ref[rows, hs], sb)
            q_s = _dot(qg_ref[rows, hs], sb)
            yield
            vn = u_ref[rows, hs] - w_s
            vnb = _bf(vn)
            o = q_s + _dot(a_ref[h, rows], _rows_pad(vnb))
            k_v = _dot_tn(kd_ref[rows, hs], vnb)
            yield
            state[h] = s_in * jnp.exp(gcs_ref[(chunk + 1) * C - 1:(chunk + 1) * C, h:h + 1]) + k_v
            o_ref[rows, hs] = o
            vn_ref[rows, hs] = vnb
            out_ref[rows, hs] = _bf(_gated_norm(o, z_ref[rows, hs], gn_ref[...]))

        for chunk in range(G):
            _interleave(head_program(chunk, h) for h in range(DN_HEADS))

    wide = pl.BlockSpec((R, 512), lambda n: (n, 0))
    f = lambda *shp: jax.ShapeDtypeStruct(shp, F32)
    b = lambda *shp: jax.ShapeDtypeStruct(shp, BF16)
    return pl.pallas_call(
        body, grid=(nc // G,), name="dn_scan_fwd",
        in_specs=[wide, wide, wide, wide, pl.BlockSpec((DN_HEADS, R, CPAD), lambda n: (0, n, 0)),
                  pl.BlockSpec((R, 128), lambda n: (n, 0)), pl.BlockSpec((R, 512), lambda n: (n, BLK_Z)),
                  pl.BlockSpec((1, DN_DIM), lambda n: (0, 0))],
        out_specs=[wide, wide, pl.BlockSpec((G, DN_HEADS, DN_DIM, DN_DIM), lambda n: (n, 0, 0, 0)), wide],
        out_shape=[f(S, 512), b(S, 512), b(nc, DN_HEADS, DN_DIM, DN_DIM), b(S, 512)],
        scratch_shapes=[pltpu.VMEM((DN_HEADS, DN_DIM, DN_DIM), F32)],
        compiler_params=_params(("arbitrary",)),
    )(w, u, qg, kd, a, gcs, pz, gn)


def _dn_scan_bwd(dcat, o, pz, gn, sst, vnew, w, qg, kd, a, gcs, dproj):
    S = o.shape[0]
    C = DN_CHUNK
    G = CHUNKS_SCAN
    R = G * C
    steps = S // R

    def body(dy_ref, o_ref, z_ref, gn_ref, sst_ref, vn_ref, w_ref, qg_ref, kd_ref, a_ref, gcs_ref, _,
             du_ref, dw_ref, dqg_ref, dkd_ref, da_ref, dz_ref, dsc_ref, dgn_ref, dstate):
        @pl.when(pl.program_id(0) == 0)
        def _():
            dstate[...] = jnp.zeros_like(dstate)
            dgn_ref[...] = jnp.zeros_like(dgn_ref)

        gn_ = gn_ref[...]
        lane = lax.broadcasted_iota(jnp.int32, (C, 128), 1)
        row = lax.broadcasted_iota(jnp.int32, (C, 128), 0)
        dgn_parts = []

        def head_program(chunk, h, dsc_parts):
            hs = _head(h)
            rows = slice(chunk * C, (chunk + 1) * C)
            ov, z, dout = o_ref[rows, hs], z_ref[rows, hs], dy_ref[rows, hs]
            r, oh = _rms_stats(ov)
            sg = _sigmoid(z)
            don = dout * (z * sg)
            dz_ref[rows, hs] = _bf(dout * (oh * gn_) * (sg * (1.0 + z * (1.0 - sg))))
            dgn_parts.append(jnp.sum(don * oh, axis=0, keepdims=True))
            dn = don * gn_
            do = _bf(r * (dn - oh * jnp.mean(dn * oh, axis=-1, keepdims=True)))
            sb = sst_ref[chunk, h]
            s_in = sb.astype(F32)
            ds_out = dstate[h]
            dsb = _bf(ds_out)
            vnb = vn_ref[rows, hs]
            wb, qgb, kdb, ab = w_ref[rows, hs], qg_ref[rows, hs], kd_ref[rows, hs], a_ref[h, rows]
            dvn = _dot_tn(ab, do)[:C] + _dot(kdb, dsb)
            yield
            da_ref[h, rows] = _dot_nt(do, _rows_pad(vnb))
            dqg_ref[rows, hs] = _dot_nt(do, sb)
            dkd_ref[rows, hs] = _dot_nt(vnb, dsb)
            q_do = _dot_tn(qgb, do)
            yield
            dvnb = _bf(dvn)
            dw_ref[rows, hs] = _bf(-_dot_nt(dvnb, sb))
            w_dvn = _dot_tn(wb, dvnb)
            du_ref[rows, hs] = dvnb
            yield
            d_last = jnp.exp(gcs_ref[(chunk + 1) * C - 1:(chunk + 1) * C, h:h + 1])
            dd = jnp.sum(jnp.sum(ds_out * s_in, axis=1, keepdims=True), axis=0, keepdims=True)
            dsc_parts.append(jnp.where((lane == h) & (row == C - 1), dd * d_last, 0.0))
            dstate[h] = ds_out * d_last + q_do - w_dvn

        for chunk in reversed(range(G)):
            dsc_parts = []
            _interleave(head_program(chunk, h, dsc_parts) for h in range(DN_HEADS))
            dsc_ref[chunk * C:(chunk + 1) * C, :] = sum(dsc_parts[1:], dsc_parts[0])
        dgn_ref[...] += sum(dgn_parts[1:], dgn_parts[0])

    rev = lambda n: steps - 1 - n
    wide = pl.BlockSpec((R, 512), lambda n: (rev(n), 0))
    z_spec = pl.BlockSpec((R, 512), lambda n: (rev(n), BLK_Z))
    sq = pl.BlockSpec((DN_HEADS, R, CPAD), lambda n: (0, rev(n), 0))
    narrow = pl.BlockSpec((R, 128), lambda n: (rev(n), 0))
    gn_spec = pl.BlockSpec((1, DN_DIM), lambda n: (0, 0))
    f = lambda *shp: jax.ShapeDtypeStruct(shp, F32)
    b = lambda *shp: jax.ShapeDtypeStruct(shp, BF16)
    return pl.pallas_call(
        body, grid=(steps,), name="dn_scan_bwd",
        in_specs=[pl.BlockSpec((R, 512), lambda n: (rev(n), 1)), wide, z_spec, gn_spec,
                  pl.BlockSpec((G, DN_HEADS, DN_DIM, DN_DIM), lambda n: (rev(n), 0, 0, 0)),
                  wide, wide, wide, wide, sq, narrow, pl.BlockSpec(memory_space=pl.ANY)],
        out_specs=[wide, wide, wide, wide, sq, z_spec, narrow, gn_spec],
        out_shape=[b(S, 512), b(S, 512), f(S, 512), f(S, 512), f(DN_HEADS, S, CPAD),
                   jax.ShapeDtypeStruct(dproj.shape, dproj.dtype), f(S, 128), f(1, DN_DIM)],
        scratch_shapes=[pltpu.VMEM((DN_HEADS, DN_DIM, DN_DIM), F32)],
        input_output_aliases={11: 5},
        compiler_params=_params(("arbitrary",)),
    )(dcat, o, pz, gn, sst, vnew, w, qg, kd, a, gcs, dproj)


def _dn_chunk_bwd(qkv, pg, t_inv, gcs, du, dw, dqg, dkd, da, dsc, a_log, dt_bias, dproj):
    S = qkv.shape[0]
    C = DN_CHUNK
    G = CHUNKS_LOCAL
    R = G * C

    def body(alog_ref, dtb_ref, qkv_ref, pg_ref, t_ref, gcs_ref, du_ref, dw_ref, dqg_ref, dkd_ref, da_ref, dsc_ref, _,
             dqkv_ref, dpg_ref, acc_ref):
        @pl.when(pl.program_id(0) == 0)
        def _():
            acc_ref[...] = jnp.zeros_like(acc_ref)

        ii, jj = _chunk_masks()
        lane = lax.broadcasted_iota(jnp.int32, (1, 128), 1)
        row8 = lax.broadcasted_iota(jnp.int32, (8, 128), 0)
        lane8 = lax.broadcasted_iota(jnp.int32, (8, 128), 1)
        rowc = lax.broadcasted_iota(jnp.int32, (C, 1), 0)
        tril, strict = jj <= ii, jj < ii
        dpg_parts, acc_parts = [[] for _ in range(G)], []

        def head_program(chunk, h):
            rows = slice(chunk * C, (chunk + 1) * C)
            q, k, v = qkv_ref[rows, _head(h)], qkv_ref[rows, _head(DN_HEADS + h)], qkv_ref[rows, _head(2 * DN_HEADS + h)]
            gc_col, beta, g_col = gcs_ref[rows, h:h + 1], gcs_ref[rows, DN_HEADS + h:DN_HEADS + h + 1], \
                gcs_ref[rows, 2 * DN_HEADS + h:2 * DN_HEADS + h + 1]
            dec = _decay(gc_col, ii, jj)
            eg = jnp.exp(gc_col)
            g_last = gc_col[C - 1:C, :]
            ek = jnp.exp(g_last - gc_col)
            kb, vb = k * beta, v * beta
            kbg = kb * eg
            qb, kbb = _bf(q), _bf(kb)
            k_rows = _rows_pad(_bf(k))
            t = t_ref[h, rows]
            tb = _bf(t)
            dub, dwb = du_ref[rows, _head(h)], dw_ref[rows, _head(h)]
            dqg_, dkd_ = dqg_ref[rows, _head(h)], dkd_ref[rows, _head(h)]
            dt = _dot_nt(dub, _rows_pad(_bf(vb))) + _dot_nt(dwb, _rows_pad(_bf(kbg)))
            t_du_dw = _dot_tn(tb, jnp.concatenate([dub, dwb], axis=1))
            dvb, dkbg = t_du_dw[:C, :DN_DIM], t_du_dw[:C, DN_DIM:]
            kk = _dot_nt(kbb, k_rows)
            qk = _dot_nt(qb, k_rows)
            yield
            dt_t = _dot3_nt(dt, t)
            yield
            dl = -_dot3_tn(t, dt_t)
            yield
            dm = jnp.where(strict, dl * dec, 0.0)
            dqk = jnp.where(tril, da_ref[h, rows] * dec, 0.0)
            gmat = dm * kk + dqk * qk
            dgc = jnp.sum(gmat, axis=1, keepdims=True) - _row_to_col(jnp.sum(gmat, axis=0, keepdims=True), ii, jj)
            dmb, dqkb = _bf(dm), _bf(dqk)
            yield
            dkb = _dot(dmb, k_rows) + dkbg * eg
            dk = _dot_tn(jnp.concatenate([dmb, dqkb], axis=0), jnp.concatenate([kbb, qb], axis=0))[:C] + dkd_ * ek
            dq = _dot(dqkb, k_rows) + dqg_ * eg
            yield
            tk = jnp.sum(dkd_ * k * ek, axis=1, keepdims=True)
            dgc = dgc + jnp.sum(dqg_ * q * eg, axis=1, keepdims=True) - tk + jnp.sum(dkbg * kbg, axis=1, keepdims=True)
            dgl = jnp.sum(tk, axis=0, keepdims=True) + dsc_ref[(chunk + 1) * C - 1:(chunk + 1) * C, h:h + 1]
            dgc = dgc + jnp.where(rowc == C - 1, dgl, 0.0)
            yield
            dk = dk + dkb * beta
            dbeta = jnp.sum(dkb * k, axis=1, keepdims=True) + jnp.sum(dvb * v, axis=1, keepdims=True)
            dqkv_ref[rows, _head(h)] = dq
            dqkv_ref[rows, _head(DN_HEADS + h)] = dk
            dqkv_ref[rows, _head(2 * DN_HEADS + h)] = dvb * beta
            dg_col = jnp.sum(jnp.where(jj >= ii, _col_to_row(dgc, ii, jj), 0.0), axis=1, keepdims=True)
            yield
            db = dbeta * beta * (1.0 - beta)
            da_in = dg_col * (-jnp.exp(alog_ref[h])) * _sigmoid(pg_ref[rows, DN_HEADS + h:DN_HEADS + h + 1] + dtb_ref[h])
            dpg_parts[chunk].append(jnp.where(lane == h, db, 0.0) + jnp.where(lane == DN_HEADS + h, da_in, 0.0))
            acc_parts.append(jnp.where((row8 == 0) & (lane8 == h), jnp.sum(dg_col * g_col, axis=0, keepdims=True), 0.0)
                             + jnp.where((row8 == 1) & (lane8 == h), jnp.sum(da_in, axis=0, keepdims=True), 0.0))

        _interleave(head_program(chunk, h) for chunk in range(G) for h in range(DN_HEADS))
        for chunk in range(G):
            dpg = sum(dpg_parts[chunk][1:], dpg_parts[chunk][0])
            dpg_ref[chunk * C:(chunk + 1) * C, :] = _bf(jnp.concatenate([dpg, jnp.zeros_like(dpg)], axis=1))
        acc_ref[...] += sum(acc_parts[1:], acc_parts[0])

    smem = pl.BlockSpec(memory_space=pltpu.SMEM)
    wide = pl.BlockSpec((R, 512), lambda n: (n, 0))
    sq = pl.BlockSpec((DN_HEADS, R, CPAD), lambda n: (0, n, 0))
    narrow = pl.BlockSpec((R, 128), lambda n: (n, 0))
    qkv_spec = pl.BlockSpec((R, 1536), lambda n: (n, 0))
    f = lambda *shp: jax.ShapeDtypeStruct(shp, F32)
    return pl.pallas_call(
        body, grid=(S // R,), name="dn_chunk_bwd",
        in_specs=[smem, smem, qkv_spec, pl.BlockSpec((R, 128), lambda n: (n, BLK_G)), sq, narrow, wide, wide, wide, wide, sq,
                  narrow, pl.BlockSpec(memory_space=pl.ANY)],
        out_specs=[qkv_spec, pl.BlockSpec((R, 256), lambda n: (n, BLK_G_PAD)), pl.BlockSpec((8, 128), lambda n: (0, 0))],
        out_shape=[f(S, 1536), jax.ShapeDtypeStruct(dproj.shape, dproj.dtype), f(8, 128)],
        input_output_aliases={12: 1},
        compiler_params=_params(("arbitrary",)),
    )(a_log, dt_bias, qkv, pg, t_inv, gcs, du, dw, dqg, dkd, da, dsc, dproj)


_W_IN_SECTIONS = ((0, 0, 512), (2304, 512, 512), (768, 1024, 1536), (512, 2560, 256), (2816, 2816, 8))
_HALF = D_MODEL // 2
_SECTION_ROWS = 256


def _pack_pairs(x):
    bits = lax.bitcast_convert_type(x, jnp.uint32)
    return lax.bitcast_convert_type(bits[:, _HALF:] | (bits[:, :_HALF] >> 16), F32)


def _unpack_pairs(words):
    bits = lax.bitcast_convert_type(words, jnp.uint32)
    return (lax.bitcast_convert_type(bits << 16, F32),
            lax.bitcast_convert_type(bits & jnp.uint32(0xFFFF0000), F32))


def _w_in_to_internal(packed):
    starts = [dst for _, dst, _ in _W_IN_SECTIONS] + [D_IN_PAD]

    def body(x_hbm, o_ref, words, sems):
        copies = [pltpu.make_async_copy(x_hbm.at[pl.ds(src, rows), 0, :], words.at[pl.ds(dst, rows)], sems.at[i])
                  for i, (src, dst, rows) in enumerate(_W_IN_SECTIONS)]
        for cp in copies:
            cp.start()
        words[pl.ds(D_IN, D_IN_PAD - D_IN), :] = jnp.zeros((D_IN_PAD - D_IN, _HALF), F32)
        for i, cp in enumerate(copies):
            cp.wait()
            for r in range(starts[i], starts[i + 1], _SECTION_ROWS):
                for c, h in enumerate(_unpack_pairs(words[pl.ds(r, _SECTION_ROWS), :])):
                    o_ref[pl.ds(r, _SECTION_ROWS), c * _HALF:(c + 1) * _HALF] = _bf(h)

    assert all((b - a) % _SECTION_ROWS == 0 for a, b in zip(starts, starts[1:])) and starts[-2] + _W_IN_SECTIONS[-1][2] == D_IN
    return pl.pallas_call(body, name="w_in_to_internal", out_shape=jax.ShapeDtypeStruct((D_IN_PAD, D_MODEL), BF16),
                          in_specs=[pl.BlockSpec(memory_space=pl.ANY)],
                          scratch_shapes=[pltpu.VMEM((D_IN_PAD, _HALF), F32), pltpu.SemaphoreType.DMA((len(_W_IN_SECTIONS),))],
                          compiler_params=pltpu.CompilerParams(vmem_limit_bytes=VMEM_LIMIT))(packed)


def _w_in_from_internal(gt):
    def body(g_ref, o_hbm, words, sems):
        copies = []
        for i, (dst, src, rows) in enumerate(_W_IN_SECTIONS):
            for r in range(src, src + rows, _SECTION_ROWS):
                n = max(min(_SECTION_ROWS, src + rows - r), 16)
                words[pl.ds(r, n), :] = _pack_pairs(g_ref[pl.ds(r, n), :].astype(F32))
            copies.append(pltpu.make_async_copy(words.at[pl.ds(src, rows)], o_hbm.at[pl.ds(dst, rows), 0, :], sems.at[i]))
            copies[-1].start()
        for cp in copies:
            cp.wait()

    return pl.pallas_call(body, name="w_in_from_internal", out_shape=jax.ShapeDtypeStruct((D_IN, 1, _HALF), F32),
                          out_specs=pl.BlockSpec(memory_space=pl.ANY),
                          scratch_shapes=[pltpu.VMEM((D_IN_PAD, _HALF), F32), pltpu.SemaphoreType.DMA((len(_W_IN_SECTIONS),))],
                          compiler_params=pltpu.CompilerParams(vmem_limit_bytes=VMEM_LIMIT))(gt)


def _local_step(x, p, target, wts, first_weights, other_weights, ship_early, after):
    S = x.shape[0]
    cos, sin = _rope_tables(S)
    sinks, a_log, dt_bias = wts["sinks"].reshape(8), wts["a_log"].reshape(4), wts["dt_bias"].reshape(4)
    gn = wts["dn_norm"].reshape(1, DN_DIM)
    add = lambda acc, res: (acc + res,)

    u = _rmsnorm_fwd(x, wts["norm_mix"], "norm_mix_fwd", after)
    w_in_t, conv_w = first_weights(u)
    proj, = _mm(u, w_in_t, form="nt", name="in_proj", out_dtypes=[F32], tn=512)
    attn, lse = _attn_fwd(proj, cos, sin, sinks)
    qkv = _dn_prep_fwd(proj, conv_w)
    cw, cu, cqg, ckd, ca, ct, gcs = _dn_chunk_fwd(qkv, proj, a_log, dt_bias)
    o, vnew, sst, dn_out = _dn_scan_fwd(cw, cu, cqg, ckd, ca, gcs, proj, gn)
    w_o, = other_weights(("w_o",), dn_out)
    h1, = _mm([attn, dn_out], w_o, form="nn", name="out_proj", out_dtypes=[F32], tn=512, epi=add, extra=[x])

    w_up, w_down = other_weights(("w_up", "w_down"), h1)
    hid, relu, m, h2 = _mlp_fwd(h1, w_up, w_down, wts["norm_mlp"])
    w_pg, w_pp = other_weights(("w_ple_gate", "w_ple_proj"), h2)
    n3, dh2, dgl, dpp, loss, d_norm_final, d_norm_ple = _ple_and_loss(h2, p, target, w_pg, w_pp, wts["norm_ple"],
                                                                     wts["norm_final"].reshape(1, D_MODEL))
    g = {"norm_final": d_norm_final, "norm_ple": d_norm_ple}
    early = {"w_ple_gate": _mm_tn(n3, dgl, name="d_w_ple_gate", tm=512, tn=1024, out_dtype=BF16).reshape(N_DEV, 128, 1024),
             "w_ple_proj": _mm_tn(p, dpp, name="d_w_ple_proj", tm=256, tn=128, out_dtype=BF16, column_shards=True)}
    d_act, = _mm(dh2, w_down, form="nt", name="d_hidden", out_dtypes=[BF16], tn=512,
                 epi=lambda acc, r: (acc * (2.0 * r.astype(F32)),), extra=[relu])
    early["w_down"] = _mm_tn(hid, dh2, name="d_w_down", tm=512, tn=1024, out_dtype=BF16).reshape(N_DEV, 512, 1024)
    early["w_up"] = _mm_tn(m, d_act, name="d_w_up", tm=1024, tn=512, out_dtype=BF16, column_shards=True)
    dh1, g["norm_mlp"], dcat = _mm(d_act, w_up, form="nt", name="d_m", out_dtypes=[F32], tn=512,
                                   norm_bwd=(h1, wts["norm_mlp"], dh2), then_nt=w_o)
    early["w_o"] = _mm_tn([attn, dn_out], dh1, name="d_w_o", tm=512, tn=512, out_dtype=BF16).reshape(N_DEV, 128, 1024)
    token = ship_early(early)
    dproj, dk, dv, dsinks = _attn_bwd(proj, cos, sin, sinks, dcat, attn, lse, token)
    g["sinks"] = dsinks[:, 0].reshape(1, 8)
    du_, dw_, dqg, dkd, da, dproj, dsc, g["dn_norm"] = _dn_scan_bwd(dcat, o, proj, gn, sst, vnew, cw, cqg, ckd, ca, gcs, dproj)
    dqkv, dproj, gate_acc = _dn_chunk_bwd(qkv, proj, ct, gcs, du_, dw_, dqg, dkd, da, dsc, a_log, dt_bias, dproj)
    g["a_log"], g["dt_bias"] = gate_acc[0:1, 0:4], gate_acc[1:2, 0:4]
    dproj, g["conv_w"] = _dn_prep_bwd(proj, conv_w, dqkv, dproj, dk, dv)
    token = ship_early({"w_in": _mm_tn(dproj, u, name="d_w_in", tm=512, tn=1024, out_dtype=BF16)})
    grad_x, g["norm_mix"] = _mm(dproj, w_in_t, form="nn", name="d_u", out_dtypes=[F32], tn=512, after=token,
                                norm_bwd=(x, wts["norm_mix"], dh1))
    return loss, grad_x, g


def _peer(k):
    x, y, c = lax.axis_index("x"), lax.axis_index("y"), lax.axis_index("c")
    px = 1 - x if k & 4 else x
    py = 1 - y if k & 2 else y
    pc = 1 - c if k & 1 else c
    return (px, py, pc), 4 * px + 2 * py + pc


def _exchange(srcs, name, gather):
    n = len(srcs)
    gathers = list(gather) if isinstance(gather, (list, tuple)) else [gather] * n
    shapes = [(N_DEV,) + s.shape if gt else s.shape for s, gt in zip(srcs, gathers)]

    def body(*refs):
        src_refs, out_refs = refs[:n], refs[n:2 * n]
        send_sems, recv_sems, local_sems = refs[2 * n:]
        _, me = _peer(0)
        piece = lambda a, d: src_refs[a] if gathers[a] else src_refs[a].at[d]
        local = [pltpu.make_async_copy(piece(a, me), out_refs[a].at[me], local_sems.at[a]) for a in range(n)]
        for cp in local:
            cp.start()
        copies = []
        for a in range(n):
            for k in range(1, N_DEV):
                dev, idx = _peer(k)
                cp = pltpu.make_async_remote_copy(src_ref=piece(a, idx), dst_ref=out_refs[a].at[me],
                                                  send_sem=send_sems.at[a, k - 1], recv_sem=recv_sems.at[a, k - 1],
                                                  device_id=dev, device_id_type=MESH)
                cp.start()
                copies.append(cp)
        for cp in copies:
            cp.wait_recv()
        for cp in copies:
            cp.wait_send()
        for cp in local:
            cp.wait()

    anywhere = pl.BlockSpec(memory_space=pl.ANY)
    return pl.pallas_call(
        body, name=name, in_specs=[anywhere] * n, out_specs=[anywhere] * n,
        out_shape=[jax.ShapeDtypeStruct(shp, s.dtype) for shp, s in zip(shapes, srcs)],
        scratch_shapes=[pltpu.SemaphoreType.DMA((n, N_DEV - 1)), pltpu.SemaphoreType.DMA((n, N_DEV - 1)),
                        pltpu.SemaphoreType.DMA((n,))],
    )(*srcs)


_HBM = pl.BlockSpec(memory_space=pltpu.HBM)
_SEM = pl.BlockSpec(memory_space=pltpu.SEMAPHORE)
_EFFECT = pltpu.SideEffectType.DATAFLOW_SIDE_EFFECTING


def _split_copies(src_refs, land_refs, send_sems, recv_sems, modes, which=None):
    _, me = _peer(0)
    local, remote = [], []
    which = range(len(src_refs)) if which is None else which
    for a, src, land in zip(which, src_refs, land_refs):
        if modes[a] == "columns":
            n_cols = src.shape[1]
            dst = land.at[:, pl.ds(pl.multiple_of(me * n_cols, n_cols), n_cols)]
        else:
            dst = land.at[me]
        part = lambda d: src.at[d] if modes[a] == "pieces" else src
        local.append(pltpu.make_async_copy(part(me), dst, recv_sems.at[a * N_DEV]))
        for k in ((2, 4, 6) if modes[a] == "chips" else range(1, N_DEV)):
            dev, idx = _peer(k)
            sem = a * N_DEV + k
            remote.append(pltpu.make_async_remote_copy(
                src_ref=part(idx), dst_ref=dst, send_sem=send_sems.at[sem], recv_sem=recv_sems.at[sem],
                device_id=dev, device_id_type=MESH))
    return local, remote


def _forward_copies(land_refs, send_sems, recv_sems):
    c = lax.axis_index("c")
    sibling, _ = _peer(1)
    copies = []
    for a, land in enumerate(land_refs):
        for chip in range(N_DEV // 2):
            slot = 2 * chip + c
            sem = a * (N_DEV // 2) + chip
            copies.append(pltpu.make_async_remote_copy(
                src_ref=land.at[slot], dst_ref=land.at[slot], send_sem=send_sems.at[sem], recv_sem=recv_sems.at[sem],
                device_id=sibling, device_id_type=MESH))
    return copies


def _forward_start(lands, name):
    n = len(lands)

    def body(*refs):
        for cp in _forward_copies(refs[:n], refs[n], refs[n + 1]):
            cp.start()
        refs[-1][...] = jnp.zeros_like(refs[-1])

    sems = pltpu.SemaphoreType.DMA((n * (N_DEV // 2),))
    out = pl.pallas_call(
        body, name=name,
        out_shape=(sems, sems, *[pltpu.HBM(t.shape, t.dtype) for t in lands], jax.ShapeDtypeStruct((8, 128), F32)),
        in_specs=[_HBM] * n, out_specs=(_SEM, _SEM, *[_HBM] * n, pl.BlockSpec(memory_space=pltpu.VMEM)),
        input_output_aliases={i: 2 + i for i in range(n)},
        compiler_params=pltpu.CompilerParams(has_side_effects=_EFFECT),
    )(*[pltpu.with_memory_space_constraint(t, pltpu.HBM) for t in lands])
    return out[:-1], out[-1]


def _forward_wait(handle, after, name):
    send_sems, recv_sems, *lands = handle
    n = len(lands)

    def body(*refs):
        for cp in _forward_copies(refs[:n], refs[n], refs[n + 1]):
            cp.wait_send()
            cp.wait_recv()

    return list(pl.pallas_call(
        body, name=name, out_shape=tuple(pltpu.HBM(t.shape, t.dtype) for t in lands),
        in_specs=[_HBM] * n + [_SEM, _SEM, pl.BlockSpec(memory_space=pl.ANY)], out_specs=tuple([_HBM] * n),
        input_output_aliases={i: i for i in range(n)},
        compiler_params=pltpu.CompilerParams(has_side_effects=_EFFECT),
    )(*lands, send_sems, recv_sems, after))


def _exchange_start(srcs, name, modes):
    n = len(srcs)
    modes = [modes] * n if isinstance(modes, str) else list(modes)
    lands = []
    for s, mode in zip(srcs, modes):
        shape = {"columns": (s.shape[0], N_DEV * s.shape[1]), "pieces": s.shape}.get(mode, (N_DEV,) + s.shape)
        lands.append(lax.empty(shape, s.dtype))

    def body(*refs):
        src_refs, land_refs = refs[:n], refs[n:2 * n]
        send_sems, recv_sems = refs[2 * n], refs[2 * n + 1]
        local, remote = _split_copies(src_refs, land_refs, send_sems, recv_sems, modes)
        for cp in local + remote:
            cp.start()
        refs[-1][...] = jnp.zeros_like(refs[-1])

    both = list(srcs) + lands
    sems = pltpu.SemaphoreType.DMA((n * N_DEV,))
    out = pl.pallas_call(
        body, name=name,
        out_shape=(sems, sems, *[pltpu.HBM(t.shape, t.dtype) for t in both], jax.ShapeDtypeStruct((8, 128), F32)),
        in_specs=[_HBM] * (2 * n), out_specs=(_SEM, _SEM, *[_HBM] * (2 * n), pl.BlockSpec(memory_space=pltpu.VMEM)),
        input_output_aliases={i: 2 + i for i in range(2 * n)},
        compiler_params=pltpu.CompilerParams(has_side_effects=_EFFECT),
    )(*[pltpu.with_memory_space_constraint(t, pltpu.HBM) for t in both])
    return (n, modes, out[:-1]), out[-1]


def _exchange_wait(handle, after, name, which=None):
    n_all, modes, (send_sems, recv_sems, *both_all) = handle
    which = list(range(n_all)) if which is None else list(which)
    n = len(which)
    both = [both_all[a] for a in which] + [both_all[n_all + a] for a in which]

    def body(*refs):
        src_refs, land_refs = refs[:n], refs[n:2 * n]
        local, remote = _split_copies(src_refs, land_refs, refs[2 * n], refs[2 * n + 1], modes, which)
        for cp in local:
            cp.wait()
        for cp in remote:
            cp.wait_send()
            cp.wait_recv()

    out = pl.pallas_call(
        body, name=name, out_shape=tuple(pltpu.HBM(t.shape, t.dtype) for t in both),
        in_specs=[_HBM] * (2 * n) + [_SEM, _SEM, pl.BlockSpec(memory_space=pl.ANY)], out_specs=tuple([_HBM] * (2 * n)),
        input_output_aliases={i: i for i in range(2 * n)},
        compiler_params=pltpu.CompilerParams(has_side_effects=_EFFECT),
    )(*both, send_sems, recv_sems, after)
    return list(out[n:])


def _cast_all(arrays, name, after):
    waits = [] if after is None else [after]

    def body(*refs):
        for src, dst in zip(refs[:len(arrays)], refs[len(arrays) + len(waits):]):
            if len(src.shape) == 2:
                dst[...] = _bf(src[...])
            else:
                dst[:, 0, :] = _pack_pairs(_bf(src[:, 0, :]).astype(F32))

    shapes = [jax.ShapeDtypeStruct(a.shape, BF16) if a.ndim == 2 else jax.ShapeDtypeStruct((a.shape[0], 1, a.shape[2] // 2), F32)
              for a in arrays]
    return pl.pallas_call(body, name=name, out_shape=shapes,
                          compiler_params=pltpu.CompilerParams(vmem_limit_bytes=VMEM_LIMIT))(*arrays, *waits)


def _adam_update(g, w, m, v):
    nm = ADAM_B1 * m + (1.0 - ADAM_B1) * g
    nv = ADAM_B2 * v + (1.0 - ADAM_B2) * (g * g)
    m_hat = nm / (1.0 - ADAM_B1 ** ADAM_STEP)
    v_hat = nv / (1.0 - ADAM_B2 ** ADAM_STEP)
    return -ADAM_LR * (m_hat / (jnp.sqrt(v_hat) + ADAM_EPS) + ADAM_WD * w), nm, nv


def _adamw(parts, w, m, v, name):
    n, R, W = parts.shape
    tm = 128 if R % 128 == 0 else R

    def body(p_ref, w_ref, m_ref, v_ref, g_ref, d_ref, nm_ref, nv_ref):
        g = p_ref[0].astype(F32)
        for s in range(1, n):
            g = g + p_ref[s].astype(F32)
        g_ref[...] = g
        d_ref[...], nm_ref[...], nv_ref[...] = _adam_update(g, w_ref[...], m_ref[...], v_ref[...])

    tile = pl.BlockSpec((tm, W), lambda i: (i, 0))
    return pl.pallas_call(
        body, grid=(R // tm,), name=name,
        in_specs=[pl.BlockSpec((n, tm, W), lambda i: (0, i, 0)), tile, tile, tile],
        out_specs=[tile] * 4, out_shape=[jax.ShapeDtypeStruct((R, W), F32)] * 4,
        compiler_params=_params(("parallel",)),
    )(parts, w, m, v)


def _adamw_rows_apart(parts, w, m, v, name):
    n, R, _, half = parts.shape

    def body(p_hbm, w_hbm, m_hbm, v_hbm, *rest):
        out_hbm, (words, given, results, sems) = rest[:4], rest[4:]
        loads = [pltpu.make_async_copy(p_hbm.at[s, :, 0, :], words.at[s], sems.at[s]) for s in range(n)]
        loads += [pltpu.make_async_copy(h.at[:, 0, :], given.at[i], sems.at[n + i]) for i, h in enumerate((w_hbm, m_hbm, v_hbm))]
        for cp in loads:
            cp.start()
        for cp in loads:
            cp.wait()
        part = lambda s: jnp.concatenate(_unpack_pairs(words[s]), axis=1)
        g = part(0)
        for s in range(1, n):
            g = g + part(s)
        stores = []
        for i, val in enumerate((g,) + _adam_update(g, given[0], given[1], given[2])):
            results[i] = val
            stores.append(pltpu.make_async_copy(results.at[i], out_hbm[i].at[:, 0, :], sems.at[n + 3 + i]))
            stores[-1].start()
        for cp in stores:
            cp.wait()

    anywhere = pl.BlockSpec(memory_space=pl.ANY)
    return pl.pallas_call(
        body, name=name, in_specs=[anywhere] * 4, out_specs=[anywhere] * 4,
        out_shape=[jax.ShapeDtypeStruct(w.shape, F32)] * 4,
        scratch_shapes=[pltpu.VMEM((n, R, half), F32), pltpu.VMEM((3, R, 2 * half), F32), pltpu.VMEM((4, R, 2 * half), F32),
                        pltpu.SemaphoreType.DMA((n + 7,))],
        compiler_params=pltpu.CompilerParams(vmem_limit_bytes=VMEM_LIMIT),
    )(parts, w, m, v)


_MATRICES = ("w_in", "w_o", "w_up", "w_down", "w_ple_gate", "w_ple_proj")


_OTHERS = ("w_o", "w_up", "w_down", "w_ple_gate", "w_ple_proj")
_OTHER_MODES = {"w_o": "slots", "w_up": "slots", "w_down": "slots", "w_ple_gate": "slots", "w_ple_proj": "columns"}


_VECTORS = ("norm_mix", "norm_mlp", "norm_ple", "norm_final", "a_log", "dt_bias", "sinks", "dn_norm")
_SMALL_ROWS, _LOSS_ROW, _CONV_ROW = 16, 8, 9


def _pack_small(vectors, loss, conv):
    def body(*refs):
        out = refs[-1]
        out[...] = jnp.zeros_like(out)
        for r, ref in enumerate(refs[:len(_VECTORS)]):
            out[r:r + 1, 0:ref.shape[1]] = ref[...]
        out[_LOSS_ROW:_LOSS_ROW + 1, 0:128] = refs[len(_VECTORS)][...]
        out[_CONV_ROW:_CONV_ROW + 6, :] = refs[len(_VECTORS) + 1][...]

    return pl.pallas_call(body, name="pack_small", out_shape=jax.ShapeDtypeStruct((_SMALL_ROWS, 1024), F32))(*vectors, loss, conv)


def _sum_slots(parts):
    def body(p_ref, o_ref):
        acc = p_ref[0]
        for s in range(1, parts.shape[0]):
            acc = acc + p_ref[s]
        o_ref[...] = acc

    return pl.pallas_call(body, name="sum_small", out_shape=jax.ShapeDtypeStruct(parts.shape[1:], parts.dtype))(parts)


def _adamw_vectors(summed, conv_g, wmv):
    names = _VECTORS + ("conv_w",)
    flat = [a for triple in wmv for a in triple]

    def body(*refs):
        sum_ref, conv_ref = refs[0], refs[1]
        ins, outs = refs[2:2 + len(flat)], refs[2 + len(flat):]
        for i in range(len(names)):
            w_ref, m_ref, v_ref = ins[3 * i:3 * i + 3]
            g = conv_ref[...] if i == len(_VECTORS) else sum_ref[i:i + 1, 0:w_ref.shape[1]]
            outs[4 * i][...] = g
            outs[4 * i + 1][...], outs[4 * i + 2][...], outs[4 * i + 3][...] = _adam_update(g, w_ref[...], m_ref[...], v_ref[...])

    out_shape = [jax.ShapeDtypeStruct(t[0].shape, F32) for t in wmv for _ in range(4)]
    res = pl.pallas_call(body, name="adamw_vectors", out_shape=out_shape)(summed, conv_g, *flat)
    return {n: res[4 * i:4 * i + 4] for i, n in enumerate(names)}


_ORDER = ("norm_mix", "w_in", "conv_w", "a_log", "dt_bias", "dn_norm", "sinks", "w_o", "norm_mlp", "w_up", "w_down",
          "norm_ple", "w_ple_gate", "w_ple_proj", "norm_final")


def kernel(x, p, norm_mix, w_in, conv_w, a_log, dt_bias, dn_norm, sinks, w_o, norm_mlp, w_up, w_down, norm_ple, w_ple_gate, w_ple_proj, norm_final, loss_target, m_norm_mix, m_w_in, m_conv_w, m_a_log, m_dt_bias, m_dn_norm, m_sinks, m_w_o, m_norm_mlp, m_w_up, m_w_down, m_norm_ple, m_w_ple_gate, m_w_ple_proj, m_norm_final, v_norm_mix, v_w_in, v_conv_w, v_a_log, v_dt_bias, v_dn_norm, v_sinks, v_w_o, v_norm_mlp, v_w_up, v_w_down, v_norm_ple, v_w_ple_gate, v_w_ple_proj, v_norm_final):
    w = dict(norm_mix=norm_mix, w_in=w_in, conv_w=conv_w[0], a_log=a_log, dt_bias=dt_bias, dn_norm=dn_norm, sinks=sinks,
             w_o=w_o[0], norm_mlp=norm_mlp, w_up=w_up[0], w_down=w_down[0], norm_ple=norm_ple, w_ple_gate=w_ple_gate[0],
             w_ple_proj=w_ple_proj[0], norm_final=norm_final)
    m = dict(norm_mix=m_norm_mix, w_in=m_w_in, conv_w=m_conv_w[0], a_log=m_a_log, dt_bias=m_dt_bias, dn_norm=m_dn_norm,
             sinks=m_sinks, w_o=m_w_o[0], norm_mlp=m_norm_mlp, w_up=m_w_up[0], w_down=m_w_down[0], norm_ple=m_norm_ple,
             w_ple_gate=m_w_ple_gate[0], w_ple_proj=m_w_ple_proj[0], norm_final=m_norm_final)
    v = dict(norm_mix=v_norm_mix, w_in=v_w_in, conv_w=v_conv_w[0], a_log=v_a_log, dt_bias=v_dt_bias, dn_norm=v_dn_norm,
             sinks=v_sinks, w_o=v_w_o[0], norm_mlp=v_norm_mlp, w_up=v_w_up[0], w_down=v_w_down[0], norm_ple=v_norm_ple,
             w_ple_gate=v_w_ple_gate[0], w_ple_proj=v_w_ple_proj[0], norm_final=v_norm_final)
    me = 4 * lax.axis_index("x") + 2 * lax.axis_index("y") + lax.axis_index("c")
    conv_shard = conv_w.shape[2]

    for d in (w, m, v):
        d["w_in"] = jnp.transpose(d["w_in"], (2, 0, 1))
    conv_pad = jnp.pad(w["conv_w"], ((0, 8 - DN_CONV), (0, 256 - conv_shard)))
    w_in_shard, = _cast_all([w["w_in"]], "cast_w_in", None)
    gathers_first, token_first = _exchange_start([w_in_shard, conv_pad], "gather_first_start", "chips")
    shards = _cast_all([w[n] for n in _OTHERS], "cast_others", token_first)
    gathers, token_gather = _exchange_start(list(shards), "gather_start", [_OTHER_MODES[n] for n in _OTHERS])

    def first_weights(after):
        over_ici = _exchange_wait(gathers_first, after, "gather_first_wait")
        handle, token = _forward_start(over_ici, "gather_first_forward")
        w_in_all, conv_all = _forward_wait(handle, token, "gather_first_forward_wait")
        conv_all = jnp.transpose(conv_all[:, :DN_CONV, :conv_shard], (1, 0, 2)).reshape(DN_CONV, N_DEV * conv_shard)
        return _w_in_to_internal(w_in_all.reshape(D_IN, 1, _HALF)), conv_all

    as_taken = {"w_o": lambda t: t.reshape(1024, 1024), "w_up": lambda t: t, "w_down": lambda t: t.reshape(4096, 1024),
                "w_ple_gate": lambda t: t.reshape(1024, 1024), "w_ple_proj": lambda t: t}

    def other_weights(names, after):
        which = [_OTHERS.index(n) for n in names]
        got = _exchange_wait(gathers, after, "gather_wait_" + names[0], which)
        return [as_taken[n](t) for n, t in zip(names, got)]

    shipped = []

    def ship_early(pieces):
        names = tuple(pieces)
        if names == ("w_in",):
            pieces = {"w_in": _w_in_from_internal(pieces["w_in"]).reshape(N_DEV, D_IN // N_DEV, 1, _HALF)}
        handle, token = _exchange_start([pieces[n] for n in names], "scatter_start_" + names[0], "pieces")
        shipped.append((names, handle))
        return token

    loss, grad_x, g = _local_step(x[0], p[0, 0], loss_target[0], w, first_weights, other_weights, ship_early, token_gather)

    row = lambda t: t.reshape(1, t.size)
    small = _pack_small([row(g[n]) for n in _VECTORS], loss, g["conv_w"].reshape(6, 1024))
    small_handle, token_small = _exchange_start([small], "gather_small_start", "slots")
    big, after = {}, token_small
    for names, handle in shipped[:-1]:
        for n, r in zip(names, _exchange_wait(handle, after, "scatter_wait_" + names[0])):
            big[n] = _adamw(r, w[n], m[n], v[n], "adamw_" + n)
            after = big[n][1]
    small_all, = _exchange_wait(small_handle, after, "gather_small_wait")
    summed = _sum_slots(small_all)
    conv_g = lax.dynamic_slice(summed[_CONV_ROW:_CONV_ROW + 6].reshape(DN_CONV, N_DEV * conv_shard), (0, me * conv_shard),
                               (DN_CONV, conv_shard))
    small_out = _adamw_vectors(summed, conv_g, [(row(w[n]), row(m[n]), row(v[n])) for n in _VECTORS]
                               + [(w["conv_w"], m["conv_w"], v["conv_w"])])
    names, handle = shipped[-1]
    for n, r in zip(names, _exchange_wait(handle, small_out["conv_w"][0], "scatter_wait_" + names[0])):
        big[n] = _adamw_rows_apart(r, w[n], m[n], v[n], "adamw_" + n)

    result = [summed[_LOSS_ROW, 0], grad_x[None]]
    for i in range(4):
        for n in _ORDER:
            if n == "w_in":
                result.append(jnp.transpose(big[n][i], (1, 2, 0)))
            elif n in _MATRICES:
                result.append(big[n][i][None])
            elif n == "conv_w":
                result.append(small_out[n][i][None])
            else:
                result.append(small_out[n][i].reshape(w[n].shape))
    return tuple(result)
```

```python
import jax
import jax.numpy as jnp
import numpy as np
from jax import lax
from jax.experimental import pallas as pl
from jax.experimental.pallas import tpu as pltpu

F32, BF16 = jnp.float32, jnp.bfloat16
EPS = 1e-6
D_MODEL = 1024
N_DEV = 8
ATTN_BLOCK = 128
HEAD_PAIR = 128
DN_HEADS = 4
DN_DIM = 128
DN_CHUNK = 64
DN_CONV = 4
ROPE_THETA = 10000.0
D_IN = 2824
D_IN_PAD = 3072
BLK_Q, BLK_Z = 0, 1
BLK_DN, BLK_K, BLK_V, BLK_G = 8, 20, 21, 22
BLK_G_PAD = 11
VMEM_LIMIT = 56 * 1024 * 1024
NEG = -1e30
ADAM_LR, ADAM_B1, ADAM_B2, ADAM_EPS, ADAM_WD, ADAM_STEP = 0.001, 0.9, 0.999, 1e-08, 0.01, 10
MESH = pl.DeviceIdType.MESH


def _bf(x):
    return x.astype(BF16)


def _dot(a, b):
    return jnp.dot(a, b, preferred_element_type=F32)


def _dot_nt(a, b):
    return lax.dot_general(a, b, (((1,), (1,)), ((), ())), preferred_element_type=F32)


def _dot_tn(a, b):
    return lax.dot_general(a, b, (((0,), (0,)), ((), ())), preferred_element_type=F32)


def _sigmoid(x):
    return 1.0 / (1.0 + jnp.exp(-x))


def _params(sem):
    return pltpu.CompilerParams(dimension_semantics=sem, vmem_limit_bytes=VMEM_LIMIT)


def _mm(x, w, *, form, name, out_dtypes, tn, epi=None, extra=(), tm=512, w_row_block=0, after=None, norm=None,
        norm_bwd=None, then_nt=None):
    assert norm is None or norm_bwd is None
    xs = list(x) if isinstance(x, (list, tuple)) else [x]
    nx = len(xs)
    S, K = xs[0].shape
    shards = w.ndim == 3
    N = (w.shape[2] * N_DEV if shards else w.shape[1]) if form == "nn" else w.shape[-2]
    assert not (shards and form == "nn" and tn != w.shape[2]) and (nx == 1 or (form == "nn" and not shards and norm is None))
    r0 = w_row_block * K
    tm = min(tm, S)
    n_extra, n_out = len(extra), len(out_dtypes)
    tile = lambda width: pl.BlockSpec((tm, width), lambda i: (i, 0))
    whole = lambda a: pl.BlockSpec(a.shape, lambda i, nd=a.ndim: (0,) * nd)
    ins, in_specs = [*xs, w, *extra], [tile(K)] * nx + [whole(w)] + [tile(N)] * n_extra
    if norm is not None:
        ins, in_specs = ins + [norm], in_specs + [whole(norm)]
    if norm_bwd is not None:
        ins, in_specs = ins + list(norm_bwd), in_specs + [tile(N), whole(norm_bwd[1]), tile(N)]
    if then_nt is not None:
        ins, in_specs = ins + [then_nt], in_specs + [whole(then_nt)]
    if after is not None:
        ins, in_specs = ins + [after], in_specs + [whole(after)]
    out_shape = [jax.ShapeDtypeStruct((S, N), dt) for dt in out_dtypes]
    out_specs = [tile(N)] * n_out
    if norm is not None:
        out_shape, out_specs = out_shape + [jax.ShapeDtypeStruct((S, K), BF16)], out_specs + [tile(K)]
    if norm_bwd is not None:
        out_shape, out_specs = out_shape + [jax.ShapeDtypeStruct((1, N), F32)], out_specs + [pl.BlockSpec((1, N), lambda i: (0, 0))]
    if then_nt is not None:
        out_shape, out_specs = out_shape + [jax.ShapeDtypeStruct((S, then_nt.shape[0]), F32)], out_specs + [tile(then_nt.shape[0])]

    def product(xb, w_ref, cols, c):
        if form == "nn" and nx > 1:
            return sum(_dot(part, w_ref[r0 + p * K:r0 + (p + 1) * K, cols]) for p, part in enumerate(xb))
        if form == "nn":
            return _dot(xb, w_ref[c] if shards else w_ref[r0:r0 + K, cols])
        if not shards:
            return _dot_nt(xb, w_ref[cols, :])
        ks = w.shape[2]
        acc = _dot_nt(xb[:, 0:ks], w_ref[0, cols, :])
        for s in range(1, N_DEV):
            acc = acc + _dot_nt(xb[:, s * ks:(s + 1) * ks], w_ref[s, cols, :])
        return acc

    def body(*refs):
        x_ref, w_ref = refs[0], refs[nx]
        extra_refs = refs[nx + 1:nx + 1 + n_extra]
        at = nx + 1 + n_extra
        if norm is not None:
            gain_ref, at = refs[at], at + 1
        if norm_bwd is not None:
            (y_ref, ygain_ref, dres_ref), at = refs[at:at + 3], at + 3
        if then_nt is not None:
            w2_ref, at = refs[at], at + 1
        outs = refs[len(ins):]
        if norm is not None:
            _, xh = _rms_stats(x_ref[...])
            xb = _bf(xh * gain_ref[...])
            outs[n_out][...] = xb
        else:
            xb = _bf(x_ref[...]) if nx == 1 else [_bf(r[...]) for r in refs[:nx]]
        for c in range(N // tn):
            cols = slice(c * tn, (c + 1) * tn)
            acc = product(xb, w_ref, cols, c)
            res = epi(acc, *[r[:, cols] for r in extra_refs]) if epi else (acc,)
            for o, r in zip(outs[:n_out], res):
                o[:, cols] = r.astype(o.dtype)
        if norm_bwd is not None:
            dx, dg = _rms_bwd_tile(y_ref[...], ygain_ref[...], outs[0][...])
            outs[0][...] = dres_ref[...] + dx
            dg_ref = outs[n_out]

            @pl.when(pl.program_id(0) == 0)
            def _():
                dg_ref[...] = jnp.zeros_like(dg_ref)

            dg_ref[...] += dg
        if then_nt is not None:
            yb = _bf(outs[0][...])
            for c in range(then_nt.shape[0] // tn):
                cols = slice(c * tn, (c + 1) * tn)
                outs[-1][:, cols] = _dot_nt(yb, w2_ref[cols, :])

    return pl.pallas_call(
        body, grid=(S // tm,), name=name, in_specs=in_specs, out_specs=out_specs, out_shape=out_shape,
        compiler_params=_params(("arbitrary",) if norm_bwd is not None else ("parallel",)),
    )(*ins)


def _mlp_fwd(h1, w_up, w_down, gain):
    S, K = h1.shape
    n_sh, _, fs = w_up.shape
    tm = min(512, S)

    def body(x_ref, wup_ref, wdown_ref, g_ref, hid_ref, relu_ref, m_ref, h2_ref):
        x = x_ref[...]
        _, xh = _rms_stats(x)
        mb = _bf(xh * g_ref[...])
        m_ref[...] = mb
        h2_ref[...] = x
        for c in range(n_sh):
            cols = slice(c * fs, (c + 1) * fs)
            r = jnp.maximum(_dot(mb, wup_ref[c]), 0.0)
            hd = _bf(r * r)
            hid_ref[:, cols] = hd
            relu_ref[:, cols] = _bf(r)
            h2_ref[...] += _dot(hd, wdown_ref[cols, :])

    tile = lambda width: pl.BlockSpec((tm, width), lambda i: (i, 0))
    once = lambda a: pl.BlockSpec(a.shape, lambda i, nd=a.ndim: (0,) * nd, pipeline_mode=pl.Buffered(1))
    F = n_sh * fs
    return pl.pallas_call(
        body, grid=(S // tm,), name="mlp_fwd",
        in_specs=[tile(K), once(w_up), once(w_down), pl.BlockSpec(gain.shape, lambda i: (0, 0))],
        out_specs=[tile(F), tile(F), tile(K), tile(K)],
        out_shape=[jax.ShapeDtypeStruct((S, F), BF16), jax.ShapeDtypeStruct((S, F), BF16),
                   jax.ShapeDtypeStruct((S, K), BF16), jax.ShapeDtypeStruct((S, K), F32)],
        compiler_params=_params(("parallel",)),
    )(h1, w_up, w_down, gain)


def _mm_tn(x, dy, *, name, tm, tn, out_dtype=F32, column_shards=False, after=None):
    xs = list(x) if isinstance(x, (list, tuple)) else [x]
    S, N = dy.shape
    K = x.shape[1] if len(xs) == 1 else tm * len(xs)
    waits = [] if after is None else [after]

    def body(*refs):
        dy_ref, out_ref = refs[len(xs)], refs[-1]
        if len(xs) == 1:
            out_ref[...] = _dot_tn(_bf(refs[0][...]), _bf(dy_ref[...])).astype(out_dtype)
        for k in range(len(xs) if len(xs) > 1 else 0):
            @pl.when(pl.program_id(0) == k)
            def _(k=k):
                out_ref[...] = _dot_tn(_bf(refs[k][...]), _bf(dy_ref[...])).astype(out_dtype)

    if column_shards:
        out_spec = pl.BlockSpec((None, tm, tn), lambda i, j: (j, i, 0))
        out_shape = jax.ShapeDtypeStruct((N // tn, K, tn), out_dtype)
    else:
        out_spec = pl.BlockSpec((tm, tn), lambda i, j: (i, j))
        out_shape = jax.ShapeDtypeStruct((K, N), out_dtype)
    return pl.pallas_call(
        body, grid=(K // tm, N // tn), name=name,
        in_specs=([pl.BlockSpec((S, tm), lambda i, j: (0, i))] if len(xs) == 1 else [pl.BlockSpec((S, tm), lambda i, j: (0, 0))] * len(xs))
        + [pl.BlockSpec((S, tn), lambda i, j: (0, j))] + [pl.BlockSpec(memory_space=pl.ANY)] * len(waits),
        out_specs=out_spec, out_shape=out_shape,
        compiler_params=_params(("parallel", "parallel")),
    )(*xs, dy, *waits)


def _rowwise(body, *, tiled, full, out_tiled, out_acc, name, tm=512, smem=()):
    S = tiled[0].shape[0]
    tm = min(tm, S)
    n_in = len(smem) + len(tiled) + len(full)

    def kern(*refs):
        @pl.when(pl.program_id(0) == 0)
        def _():
            for r in refs[n_in + len(out_tiled):]:
                r[...] = jnp.zeros_like(r)
        body(*refs)

    in_specs = [pl.BlockSpec(memory_space=pltpu.SMEM) for _ in smem]
    in_specs += [pl.BlockSpec((tm, a.shape[1]), lambda i: (i, 0)) for a in tiled]
    in_specs += [pl.BlockSpec(a.shape, lambda i, nd=a.ndim: (0,) * nd) for a in full]
    out_specs = [pl.BlockSpec((tm, w), lambda i: (i, 0)) for w, _ in out_tiled]
    out_specs += [pl.BlockSpec(shp, lambda i, nd=len(shp): (0,) * nd) for shp, _ in out_acc]
    out_shape = [jax.ShapeDtypeStruct((S, w), dt) for w, dt in out_tiled]
    out_shape += [jax.ShapeDtypeStruct(shp, dt) for shp, dt in out_acc]
    return pl.pallas_call(
        kern, grid=(S // tm,), name=name, in_specs=in_specs, out_specs=out_specs, out_shape=out_shape,
        compiler_params=_params(("arbitrary",)),
    )(*smem, *tiled, *full)


def _rms_stats(x):
    r = lax.rsqrt(jnp.mean(x * x, axis=-1, keepdims=True) + EPS)
    return r, x * r


def _rmsnorm_fwd(x, g, name, after):
    def body(x_ref, g_ref, _, o_ref):
        _, xh = _rms_stats(x_ref[...])
        o_ref[...] = _bf(xh * g_ref[...])

    return _rowwise(body, tiled=[x], full=[g, after], out_tiled=[(x.shape[1], BF16)], out_acc=[], name=name)[0]


def _rms_bwd_tile(x, g, dxn):
    r, xh = _rms_stats(x)
    dg = jnp.sum(dxn * xh, axis=0, keepdims=True)
    dn = dxn * g
    dx = r * (dn - xh * jnp.mean(dn * xh, axis=-1, keepdims=True))
    return dx, dg


def _ple_and_loss(h2, p, target, w_pg, w_pp, g_ple, g_final):
    S, n = h2.shape
    tm = min(512, S)
    tn = 512

    def body(h2_ref, p_ref, t_ref, wpg_ref, wpp_ref, gple_ref, gfin_ref,
             n3_ref, dh_ref, dgl_ref, dpp_ref, loss_ref, dg_ref, dgple_ref, pp, gate, h3):
        @pl.when(pl.program_id(0) == 0)
        def _():
            loss_ref[...] = jnp.zeros_like(loss_ref)
            dg_ref[...] = jnp.zeros_like(dg_ref)
            dgple_ref[...] = jnp.zeros_like(dgple_ref)

        x = h2_ref[...]
        _, xh = _rms_stats(x)
        n3 = _bf(xh * gple_ref[...])
        n3_ref[...] = n3
        pb = _bf(p_ref[...])
        for c in range(n // tn):
            cols = slice(c * tn, (c + 1) * tn)
            pp[:, cols] = _dot(pb, wpp_ref[:, cols])
            gt = _sigmoid(_dot(n3, wpg_ref[:, cols]))
            gate[:, cols] = gt
            h3[:, cols] = x[:, cols] + gt * pp[:, cols]
        y = h3[...]
        _, yh = _rms_stats(y)
        e = yh * gfin_ref[...] - t_ref[...]
        per_tok = jnp.mean(e * e, axis=-1, keepdims=True)
        loss_ref[...] += 0.5 * jnp.sum(per_tok, axis=0, keepdims=True)
        dh, dg = _rms_bwd_tile(y, gfin_ref[...], e * (1.0 / n))
        dg_ref[...] += dg
        gt = gate[...]
        dgl = _bf(dh * pp[...] * gt * (1.0 - gt))
        dgl_ref[...] = dgl
        dpp_ref[...] = _bf(dh * gt)
        for c in range(n // tn):
            cols = slice(c * tn, (c + 1) * tn)
            h3[:, cols] = _dot_nt(dgl, wpg_ref[cols, :])
        dx, dgp = _rms_bwd_tile(x, gple_ref[...], h3[...])
        dh_ref[...] = dh + dx
        dgple_ref[...] += dgp

    tile = lambda width: pl.BlockSpec((tm, width), lambda i: (i, 0))
    whole = lambda a: pl.BlockSpec(a.shape, lambda i, nd=a.ndim: (0,) * nd)
    return pl.pallas_call(
        body, grid=(S // tm,), name="ple_and_loss",
        in_specs=[tile(n), tile(p.shape[1]), tile(n), whole(w_pg), whole(w_pp), whole(g_ple), whole(g_final)],
        out_specs=[tile(n), tile(n), tile(n), tile(n), pl.BlockSpec((1, 128), lambda i: (0, 0)),
                   pl.BlockSpec((1, n), lambda i: (0, 0)), pl.BlockSpec((1, n), lambda i: (0, 0))],
        out_shape=[jax.ShapeDtypeStruct((S, n), BF16), jax.ShapeDtypeStruct((S, n), F32), jax.ShapeDtypeStruct((S, n), BF16),
                   jax.ShapeDtypeStruct((S, n), BF16), jax.ShapeDtypeStruct((1, 128), F32), jax.ShapeDtypeStruct((1, n), F32),
                   jax.ShapeDtypeStruct((1, n), F32)],
        scratch_shapes=[pltpu.VMEM((tm, n), F32)] * 3,
        compiler_params=_params(("arbitrary",)),
    )(h2, p, target, w_pg, w_pp, g_ple, g_final)


def _rope_tables(S):
    half = 32
    inv = (1.0 / (np.float32(ROPE_THETA) ** (np.arange(half, dtype=np.float32) * np.float32(2.0 / 64)))).astype(np.float32)
    ang = np.arange(S).astype(np.float32)[:, None] * inv[None, :]
    cos, sin = np.cos(ang), np.sin(ang)
    return jnp.asarray(np.tile(cos, (1, 4))), jnp.asarray(np.concatenate([-sin, sin, -sin, sin], axis=1))


def _attn_common(i, kc, kp, vc, vp, cc, sc, cp, sp):
    lane = lax.broadcasted_iota(jnp.int32, (1, HEAD_PAIR), 1)
    lane_lo = jnp.bitwise_and(lane, 63) < 32
    slot = [lane < 64, lane >= 64]

    def swap_halves(t):
        return jnp.where(lane_lo, pltpu.roll(t, 96, 1), pltpu.roll(t, 32, 1))

    def rope(t, cos, sin):
        return t * cos + swap_halves(t) * sin

    def unrope(d, cos, sin):
        return d * cos + swap_halves(d * sin)

    k2 = jnp.concatenate([rope(kp, cp, sp), rope(kc, cc, sc)], axis=0)
    v2 = jnp.concatenate([vp, vc], axis=0)
    r = lax.broadcasted_iota(jnp.int32, (ATTN_BLOCK, 2 * ATTN_BLOCK), 0)
    c = lax.broadcasted_iota(jnp.int32, (ATTN_BLOCK, 2 * ATTN_BLOCK), 1)
    valid = (c > r) & (c <= r + ATTN_BLOCK) & jnp.logical_or(c >= ATTN_BLOCK, i > 0)
    ks, vs = {}, {}
    for j in range(2):
        kn = jnp.where(slot[j], k2, 0.0)
        vn = jnp.where(slot[j], v2, 0.0)
        for s in range(2):
            ks[j, s] = _bf(kn if s == j else pltpu.roll(kn, 64, 1))
            vs[j, s] = _bf(vn if s == j else pltpu.roll(vn, 64, 1))
    return slot, rope, unrope, valid, ks, vs


def _attn_probs(scores, valid, sink):
    s = jnp.where(valid, scores * 0.125, NEG)
    m = jnp.maximum(jnp.max(s, axis=1, keepdims=True), sink)
    e = jnp.exp(s - m)
    z = jnp.sum(e, axis=1, keepdims=True) + jnp.exp(sink - m)
    return e * (1.0 / z), m + jnp.log(z)


def _attn_specs(S):
    nb = S // ATTN_BLOCK
    prev = lambda i: jnp.maximum(i - 1, 0)
    blk = lambda w, col, row=(lambda i: i): pl.BlockSpec((ATTN_BLOCK, w), lambda i: (row(i), col))
    in_specs = [pl.BlockSpec(memory_space=pltpu.SMEM),
                blk(512, BLK_Q), blk(128, BLK_K), blk(128, BLK_K, prev), blk(128, BLK_V), blk(128, BLK_V, prev),
                blk(128, 0), blk(128, 0), blk(128, 0, prev), blk(128, 0, prev)]
    return nb, in_specs


def _attn_fwd(pa, cos, sin, sinks):
    S = pa.shape[0]
    nb, in_specs = _attn_specs(S)

    def body(sinks_ref, q_ref, kc_ref, kp_ref, vc_ref, vp_ref, cc_ref, sc_ref, cp_ref, sp_ref, o_ref, lse_ref):
        i = pl.program_id(0)
        lane = lax.broadcasted_iota(jnp.int32, (1, HEAD_PAIR), 1)
        cc, sc = cc_ref[...], sc_ref[...]
        _, rope, _, valid, ks, vs = _attn_common(i, kc_ref[...], kp_ref[...], vc_ref[...], vp_ref[...],
                                                 cc, sc, cp_ref[...], sp_ref[...])
        pair_cols = [slice(HEAD_PAIR * pair, HEAD_PAIR * (pair + 1)) for pair in range(4)]
        qps = [_bf(rope(q_ref[:, cols], cc, sc)) for cols in pair_cols]
        outs, lses = {}, {}

        def head_program(h):
            pair, s = divmod(h, 2)
            j = h // 4
            scores = _dot_nt(qps[pair], ks[j, s])
            yield
            p, lse = _attn_probs(scores, valid, sinks_ref[h])
            outs[h] = _dot(_bf(p), vs[j, s])
            lses[h] = jnp.where(lane == h, lse, 0.0)

        _interleave(head_program(h) for h in range(8))
        for pair, cols in enumerate(pair_cols):
            o_ref[:, cols] = outs[2 * pair] + outs[2 * pair + 1]
        lse_ref[...] = sum((lses[h] for h in range(1, 8)), lses[0])

    return pl.pallas_call(
        body, grid=(nb,), name="attn_fwd", in_specs=in_specs,
        out_specs=[pl.BlockSpec((ATTN_BLOCK, 512), lambda i: (i, 0)), pl.BlockSpec((ATTN_BLOCK, 128), lambda i: (i, 0))],
        out_shape=[jax.ShapeDtypeStruct((S, 512), F32), jax.ShapeDtypeStruct((S, 128), F32)],
        compiler_params=_params(("parallel",)),
    )(sinks, pa, pa, pa, pa, pa, cos, sin, cos, sin)


def _attn_bwd(pa, cos, sin, sinks, dcat, attn, lse, after):
    S = pa.shape[0]
    nb, in_specs = _attn_specs(S)
    in_specs = in_specs + [pl.BlockSpec((ATTN_BLOCK, 512), lambda i: (i, 0))] * 2 + [pl.BlockSpec((ATTN_BLOCK, 128), lambda i: (i, 0))]
    in_specs = in_specs + [pl.BlockSpec(memory_space=pl.ANY)]

    def body(sinks_ref, q_ref, kc_ref, kp_ref, vc_ref, vp_ref, cc_ref, sc_ref, cp_ref, sp_ref, do_ref, o_ref, lse_ref, _,
             dq_ref, dk_ref, dv_ref, dsink_ref):
        i = pl.program_id(0)

        @pl.when(i == 0)
        def _():
            dk_ref[...] = jnp.zeros_like(dk_ref)
            dv_ref[...] = jnp.zeros_like(dv_ref)
            dsink_ref[...] = jnp.zeros_like(dsink_ref)

        cc, sc, cp, sp = cc_ref[...], sc_ref[...], cp_ref[...], sp_ref[...]
        slot, rope, unrope, valid, ks, vs = _attn_common(i, kc_ref[...], kp_ref[...], vc_ref[...], vp_ref[...], cc, sc, cp, sp)
        pair_cols = [slice(HEAD_PAIR * pair, HEAD_PAIR * (pair + 1)) for pair in range(4)]
        qps = [_bf(rope(q_ref[:, cols], cc, sc)) for cols in pair_cols]
        dobs = [_bf(do_ref[:, cols]) for cols in pair_cols]
        do_o = [do_ref[:, cols] * o_ref[:, cols] for cols in pair_cols]
        dqs, dks, dvs = {}, {}, {}

        def head_program(h):
            pair, s = divmod(h, 2)
            j = h // 4
            qp, dob = qps[pair], dobs[pair]
            scores = _dot_nt(qp, ks[j, s])
            dp = _dot_nt(dob, vs[j, s])
            yield
            lse_h = lse_ref[:, h:h + 1]
            p = jnp.exp(jnp.where(valid, scores * 0.125, NEG) - lse_h)
            yield
            dr = jnp.sum(jnp.where(slot[s], do_o[pair], 0.0), axis=1, keepdims=True)
            ds = _bf(p * (dp - dr) * 0.125)
            yield
            dsink_ref[h:h + 1, :] += -jnp.sum(jnp.exp(sinks_ref[h] - lse_h) * dr, axis=0, keepdims=True)
            dqs[h] = _dot(ds, ks[j, s])
            dk_h = _dot_tn(ds, qp)
            dv_h = _dot_tn(_bf(p), dob)
            yield
            dk_h, dv_h = jnp.where(slot[s], dk_h, 0.0), jnp.where(slot[s], dv_h, 0.0)
            if s != j:
                dk_h, dv_h = pltpu.roll(dk_h, 64, 1), pltpu.roll(dv_h, 64, 1)
            dks[h], dvs[h] = dk_h, dv_h

        _interleave(head_program(h) for h in range(8))
        dk2 = sum((dks[h] for h in range(1, 8)), dks[0])
        dv2 = sum((dvs[h] for h in range(1, 8)), dvs[0])
        for pair, cols in enumerate(pair_cols):
            dq_ref[:, cols] = _bf(unrope(dqs[2 * pair] + dqs[2 * pair + 1], cc, sc))
        cur = pl.ds(pl.multiple_of(i * ATTN_BLOCK, ATTN_BLOCK), ATTN_BLOCK)
        dk_ref[cur, :] += unrope(dk2[ATTN_BLOCK:], cc, sc)
        dv_ref[cur, :] += dv2[ATTN_BLOCK:]

        @pl.when(i > 0)
        def _():
            prv = pl.ds(pl.multiple_of((i - 1) * ATTN_BLOCK, ATTN_BLOCK), ATTN_BLOCK)
            dk_ref[prv, :] += unrope(dk2[:ATTN_BLOCK], cp, sp)
            dv_ref[prv, :] += dv2[:ATTN_BLOCK]

    whole = lambda w: pl.BlockSpec((S, w), lambda i: (0, 0))
    return pl.pallas_call(
        body, grid=(nb,), name="attn_bwd", in_specs=in_specs,
        out_specs=[pl.BlockSpec((ATTN_BLOCK, 512), lambda i: (i, BLK_Q)), whole(128), whole(128),
                   pl.BlockSpec((8, 128), lambda i: (0, 0))],
        out_shape=[jax.ShapeDtypeStruct((S, D_IN_PAD), BF16), jax.ShapeDtypeStruct((S, 128), F32),
                   jax.ShapeDtypeStruct((S, 128), F32), jax.ShapeDtypeStruct((8, 128), F32)],
        compiler_params=_params(("arbitrary",)),
    )(sinks, pa, pa, pa, pa, pa, cos, sin, cos, sin, dcat, attn, lse, after)


CONV_ROWS = 512
CONV_PAD = 8


def _conv_silu(scr, w, r0):
    y = w[3:4, :] * scr[pl.ds(CONV_PAD + r0, CONV_ROWS), :]
    for j in range(DN_CONV - 1):
        y = y + w[j:j + 1, :] * scr[pl.ds(CONV_PAD + r0 - 3 + j, CONV_ROWS), :]
    return y


def _dn_prep_fwd(pd, conv_w):
    S = pd.shape[0]
    assert S % CONV_ROWS == 0

    def body(x_ref, w_ref, o_ref, scr):
        b = pl.program_id(0)
        scr[0:CONV_PAD, :] = jnp.zeros((CONV_PAD, DN_DIM), F32)
        scr[pl.ds(CONV_PAD, S), :] = x_ref[...]
        w = w_ref[...]
        q_scale = jnp.where(b < DN_HEADS, DN_DIM ** -0.5, 1.0)
        for r0 in range(0, S, CONV_ROWS):
            y = _conv_silu(scr, w, r0)
            a = y * _sigmoid(y)
            rs = lax.rsqrt(jnp.sum(a * a, axis=1, keepdims=True) + EPS)
            o_ref[pl.ds(r0, CONV_ROWS), :] = a * jnp.where(b < 2 * DN_HEADS, rs * q_scale, 1.0)

    col = pl.BlockSpec((S, DN_DIM), lambda b: (0, b))
    return pl.pallas_call(
        body, grid=(3 * DN_HEADS,), name="dn_prep_fwd",
        in_specs=[pl.BlockSpec((S, DN_DIM), lambda b: (0, BLK_DN + b)), pl.BlockSpec((DN_CONV, DN_DIM), lambda b: (0, b))],
        out_specs=col,
        out_shape=jax.ShapeDtypeStruct((S, 3 * DN_HEADS * DN_DIM), F32),
        scratch_shapes=[pltpu.VMEM((S + CONV_PAD, DN_DIM), F32)],
        compiler_params=_params(("parallel",)),
    )(pd, conv_w)


def _dn_prep_bwd(pd, conv_w, dqkv, dproj, dk, dv):
    S = pd.shape[0]
    NB = 3 * DN_HEADS

    def body(x_ref, w_ref, d_ref, _, dk_ref, dv_ref, dx_ref, dw_ref, scr, dscr):
        b = pl.program_id(0)

        @pl.when(b == NB)
        def _():
            dx_ref[...] = _bf(dk_ref[...])

        @pl.when(b == NB + 1)
        def _():
            dx_ref[...] = _bf(dv_ref[...])

        @pl.when(b < NB)
        def _():
            scr[0:CONV_PAD, :] = jnp.zeros((CONV_PAD, DN_DIM), F32)
            scr[pl.ds(CONV_PAD, S), :] = x_ref[...]
            dscr[pl.ds(S, CONV_PAD), :] = jnp.zeros((CONV_PAD, DN_DIM), F32)
            w = w_ref[...]
            q_scale = jnp.where(b < DN_HEADS, DN_DIM ** -0.5, 1.0)
            is_qk = b < 2 * DN_HEADS
            dw = [jnp.zeros((1, DN_DIM), F32) for _ in range(DN_CONV)]
            for r0 in range(0, S, CONV_ROWS):
                y = _conv_silu(scr, w, r0)
                sg = _sigmoid(y)
                a = y * sg
                dout = d_ref[pl.ds(r0, CONV_ROWS), :]
                rs = lax.rsqrt(jnp.sum(a * a, axis=1, keepdims=True) + EPS)
                da_qk = q_scale * rs * (dout - a * (rs * rs) * jnp.sum(dout * a, axis=1, keepdims=True))
                dy = jnp.where(is_qk, da_qk, dout) * (sg * (1.0 + y * (1.0 - sg)))
                dscr[pl.ds(r0, CONV_ROWS), :] = dy
                for j in range(DN_CONV):
                    dw[j] = dw[j] + jnp.sum(dy * scr[pl.ds(CONV_PAD + r0 - 3 + j, CONV_ROWS), :], axis=0, keepdims=True)
            for j in range(DN_CONV):
                dw_ref[j:j + 1, :] = dw[j]
            for r0 in range(0, S, CONV_ROWS):
                dx = w[3:4, :] * dscr[pl.ds(r0, CONV_ROWS), :]
                for j in range(DN_CONV - 1):
                    dx = dx + w[j:j + 1, :] * dscr[pl.ds(r0 + 3 - j, CONV_ROWS), :]
                dx_ref[pl.ds(r0, CONV_ROWS), :] = _bf(dx)

    own = lambda b: jnp.minimum(b, NB - 1)
    col = pl.BlockSpec((S, DN_DIM), lambda b: (0, own(b)))
    proj_col = pl.BlockSpec((S, DN_DIM), lambda b: (0, BLK_DN + own(b)))
    wcol = pl.BlockSpec((DN_CONV, DN_DIM), lambda b: (0, own(b)))
    whole = pl.BlockSpec((S, DN_DIM), lambda b: (0, 0))
    assert BLK_K == BLK_DN + NB and BLK_V == BLK_K + 1
    return pl.pallas_call(
        body, grid=(NB + 2,), name="dn_prep_bwd",
        in_specs=[proj_col, wcol, col, pl.BlockSpec(memory_space=pl.ANY), whole, whole],
        out_specs=[pl.BlockSpec((S, DN_DIM), lambda b: (0, BLK_DN + b)), wcol],
        out_shape=[jax.ShapeDtypeStruct(dproj.shape, dproj.dtype), jax.ShapeDtypeStruct((DN_CONV, 3 * DN_HEADS * DN_DIM), F32)],
        scratch_shapes=[pltpu.VMEM((S + CONV_PAD, DN_DIM), F32), pltpu.VMEM((S + CONV_PAD, DN_DIM), F32)],
        input_output_aliases={3: 0},
        compiler_params=_params(("arbitrary",)),
    )(pd, conv_w, dqkv, dproj, dk, dv)


CPAD = 128
CHUNKS_LOCAL = 4
CHUNKS_SCAN = 8


def _chunk_masks():
    ii = lax.broadcasted_iota(jnp.int32, (DN_CHUNK, CPAD), 0)
    jj = lax.broadcasted_iota(jnp.int32, (DN_CHUNK, CPAD), 1)
    return ii, jj


def _rows_pad(a):
    return jnp.concatenate([a, jnp.zeros_like(a)], axis=0)


def _hi_lo(a):
    hi = _bf(a)
    return hi, _bf(a - hi.astype(F32))


def _double_step(t, p):
    C = DN_CHUNK
    th, tl = _hi_lo(t)
    ph, pl_ = _hi_lo(p)
    r1 = _dot(jnp.concatenate([th, tl, ph, pl_], axis=0), _rows_pad(ph))
    r2 = _dot(jnp.concatenate([th, ph], axis=0), _rows_pad(pl_))
    return t + (r1[:C] + r1[C:2 * C] + r2[:C]), r1[2 * C:3 * C] + r1[3 * C:] + r2[C:]


def _dot3_nt(a, b):
    C = DN_CHUNK
    ah, al = _hi_lo(a)
    bh, bl = _hi_lo(b)
    r1 = _dot_nt(jnp.concatenate([ah, al], axis=0), _rows_pad(bh))
    return r1[:C] + r1[C:] + _dot_nt(ah, _rows_pad(bl))


def _dot3_tn(a, b):
    C = DN_CHUNK
    ah, al = _hi_lo(a)
    bh, bl = _hi_lo(b)
    return _dot_tn(jnp.concatenate([ah, al, ah], axis=0), jnp.concatenate([bh, bh, bl], axis=0))[:C]


def _interleave(programs):
    programs = list(programs)
    while programs:
        alive = []
        for prog in programs:
            try:
                next(prog)
                alive.append(prog)
            except StopIteration:
                pass
        programs = alive


def _col_to_row(col, ii, jj):
    return jnp.sum(jnp.where(ii == jj, col, 0.0), axis=0, keepdims=True)


def _row_to_col(row, ii, jj):
    return jnp.sum(jnp.where(ii == jj, row, 0.0), axis=1, keepdims=True)


def _decay(gc_col, ii, jj):
    diff = gc_col - _col_to_row(gc_col, ii, jj)
    return jnp.where(jj <= ii, jnp.exp(jnp.where(jj <= ii, diff, 0.0)), 0.0)


def _softplus(x):
    return jnp.maximum(x, 0.0) + jnp.log(1.0 + jnp.exp(-jnp.abs(x)))


def _head(h):
    return slice(DN_DIM * h, DN_DIM * (h + 1))


def _dn_chunk_fwd(qkv, pg, a_log, dt_bias):
    S = qkv.shape[0]
    C = DN_CHUNK
    G = CHUNKS_LOCAL
    R = G * C
    steps = S // R

    def body(alog_ref, dtb_ref, qkv_ref, pg_ref, w_ref, u_ref, qg_ref, kd_ref, a_ref, t_ref, gcs_ref):
        ii, jj = _chunk_masks()
        lane = lax.broadcasted_iota(jnp.int32, (1, 128), 1)
        eye = (ii == jj).astype(F32)
        gcs_parts = [[] for _ in range(G)]

        def head_program(chunk, h):
            rows = slice(chunk * C, (chunk + 1) * C)
            q, k, v = qkv_ref[rows, _head(h)], qkv_ref[rows, _head(DN_HEADS + h)], qkv_ref[rows, _head(2 * DN_HEADS + h)]
            beta = _sigmoid(pg_ref[rows, h:h + 1])
            g_col = -jnp.exp(alog_ref[h]) * _softplus(pg_ref[rows, DN_HEADS + h:DN_HEADS + h + 1] + dtb_ref[h])
            g_row = _col_to_row(g_col, ii, jj)
            gc_col = jnp.sum(jnp.where(jj <= ii, g_row, 0.0), axis=1, keepdims=True)
            dec = _decay(gc_col, ii, jj)
            eg = jnp.exp(gc_col)
            kb, vb = k * beta, v * beta
            k_rows = _rows_pad(_bf(k))
            kk = _dot_nt(_bf(kb), k_rows)
            qk = _dot_nt(_bf(q), k_rows)
            yield
            t, pw = eye, -jnp.where(jj < ii, kk * dec, 0.0)
            for _ in range(6):
                t, pw = _double_step(t, pw)
                yield
            tb = _bf(t)
            u_ref[rows, _head(h)] = _dot(tb, _rows_pad(_bf(vb)))
            w_ref[rows, _head(h)] = _bf(_dot(tb, _rows_pad(_bf(kb * eg))))
            a_ref[h, rows] = _bf(qk * dec)
            t_ref[h, rows] = t
            qg_ref[rows, _head(h)] = _bf(q * eg)
            kd_ref[rows, _head(h)] = _bf(k * jnp.exp(gc_col[C - 1:C, :] - gc_col))
            gcs_parts[chunk].append(jnp.where(lane == h, gc_col, 0.0) + jnp.where(lane == DN_HEADS + h, beta, 0.0)
                                    + jnp.where(lane == 2 * DN_HEADS + h, g_col, 0.0))

        _interleave(head_program(chunk, h) for chunk in range(G) for h in range(DN_HEADS))
        for chunk in range(G):
            gcs_ref[chunk * C:(chunk + 1) * C, :] = sum(gcs_parts[chunk][1:], gcs_parts[chunk][0])

    smem = pl.BlockSpec(memory_space=pltpu.SMEM)
    wide = pl.BlockSpec((R, 512), lambda n: (n, 0))
    sq = pl.BlockSpec((DN_HEADS, R, CPAD), lambda n: (0, n, 0))
    narrow = pl.BlockSpec((R, 128), lambda n: (n, 0))
    f = lambda *shp: jax.ShapeDtypeStruct(shp, F32)
    b = lambda *shp: jax.ShapeDtypeStruct(shp, BF16)
    return pl.pallas_call(
        body, grid=(steps,), name="dn_chunk_fwd",
        in_specs=[smem, smem, pl.BlockSpec((R, 1536), lambda n: (n, 0)), pl.BlockSpec((R, 128), lambda n: (n, BLK_G))],
        out_specs=[wide, wide, wide, wide, sq, sq, narrow],
        out_shape=[b(S, 512), f(S, 512), b(S, 512), b(S, 512), b(DN_HEADS, S, CPAD), f(DN_HEADS, S, CPAD), f(S, 128)],
        compiler_params=_params(("parallel",)),
    )(a_log, dt_bias, qkv, pg)


def _gated_norm(o, z, gn):
    r, oh = _rms_stats(o)
    return oh * gn * (z * _sigmoid(z))


def _dn_scan_fwd(w, u, qg, kd, a, gcs, pz, gn):
    S = w.shape[0]
    C = DN_CHUNK
    nc = S // C
    G = CHUNKS_SCAN
    R = G * C

    def body(w_ref, u_ref, qg_ref, kd_ref, a_ref, gcs_ref, z_ref, gn_ref, o_ref, vn_ref, sst_ref, out_ref, state):
        @pl.when(pl.program_id(0) == 0)
        def _():
            state[...] = jnp.zeros_like(state)

        def head_program(chunk, h):
            hs = _head(h)
            rows = slice(chunk * C, (chunk + 1) * C)
            s_in = state[h]
            sb = _bf(s_in)
            sst_ref[chunk, h] = sb
            w_s = _dot(w_ref[rows, hs], sb)
            q_s = _dot(qg_ref[rows, hs], sb)
            yield
            vn = u_ref[rows, hs] - w_s
            vnb = _bf(vn)
            o = q_s + _dot(a_ref[h, rows], _rows_pad(vnb))
            k_v = _dot_tn(kd_ref[rows, hs], vnb)
            yield
            state[h] = s_in * jnp.exp(gcs_ref[(chunk + 1) * C - 1:(chunk + 1) * C, h:h + 1]) + k_v
            o_ref[rows, hs] = o
            vn_ref[rows, hs] = vnb
            out_ref[rows, hs] = _bf(_gated_norm(o, z_ref[rows, hs], gn_ref[...]))

        for chunk in range(G):
            _interleave(head_program(chunk, h) for h in range(DN_HEADS))

    wide = pl.BlockSpec((R, 512), lambda n: (n, 0))
    f = lambda *shp: jax.ShapeDtypeStruct(shp, F32)
    b = lambda *shp: jax.ShapeDtypeStruct(shp, BF16)
    return pl.pallas_call(
        body, grid=(nc // G,), name="dn_scan_fwd",
        in_specs=[wide, wide, wide, wide, pl.BlockSpec((DN_HEADS, R, CPAD), lambda n: (0, n, 0)),
                  pl.BlockSpec((R, 128), lambda n: (n, 0)), pl.BlockSpec((R, 512), lambda n: (n, BLK_Z)),
                  pl.BlockSpec((1, DN_DIM), lambda n: (0, 0))],
        out_specs=[wide, wide, pl.BlockSpec((G, DN_HEADS, DN_DIM, DN_DIM), lambda n: (n, 0, 0, 0)), wide],
        out_shape=[f(S, 512), b(S, 512), b(nc, DN_HEADS, DN_DIM, DN_DIM), b(S, 512)],
        scratch_shapes=[pltpu.VMEM((DN_HEADS, DN_DIM, DN_DIM), F32)],
        compiler_params=_params(("arbitrary",)),
    )(w, u, qg, kd, a, gcs, pz, gn)


def _dn_scan_bwd(dcat, o, pz, gn, sst, vnew, w, qg, kd, a, gcs, dproj):
    S = o.shape[0]
    C = DN_CHUNK
    G = CHUNKS_SCAN
    R = G * C
    steps = S // R

    def body(dy_ref, o_ref, z_ref, gn_ref, sst_ref, vn_ref, w_ref, qg_ref, kd_ref, a_ref, gcs_ref, _,
             du_ref, dw_ref, dqg_ref, dkd_ref, da_ref, dz_ref, dsc_ref, dgn_ref, dstate):
        @pl.when(pl.program_id(0) == 0)
        def _():
            dstate[...] = jnp.zeros_like(dstate)
            dgn_ref[...] = jnp.zeros_like(dgn_ref)

        gn_ = gn_ref[...]
        lane = lax.broadcasted_iota(jnp.int32, (C, 128), 1)
        row = lax.broadcasted_iota(jnp.int32, (C, 128), 0)
        dgn_parts = []

        def head_program(chunk, h, dsc_parts):
            hs = _head(h)
            rows = slice(chunk * C, (chunk + 1) * C)
            ov, z, dout = o_ref[rows, hs], z_ref[rows, hs], dy_ref[rows, hs]
            r, oh = _rms_stats(ov)
            sg = _sigmoid(z)
            don = dout * (z * sg)
            dz_ref[rows, hs] = _bf(dout * (oh * gn_) * (sg * (1.0 + z * (1.0 - sg))))
            dgn_parts.append(jnp.sum(don * oh, axis=0, keepdims=True))
            dn = don * gn_
            do = _bf(r * (dn - oh * jnp.mean(dn * oh, axis=-1, keepdims=True)))
            sb = sst_ref[chunk, h]
            s_in = sb.astype(F32)
            ds_out = dstate[h]
            dsb = _bf(ds_out)
            vnb = vn_ref[rows, hs]
            wb, qgb, kdb, ab = w_ref[rows, hs], qg_ref[rows, hs], kd_ref[rows, hs], a_ref[h, rows]
            dvn = _dot_tn(ab, do)[:C] + _dot(kdb, dsb)
            yield
            da_ref[h, rows] = _dot_nt(do, _rows_pad(vnb))
            dqg_ref[rows, hs] = _dot_nt(do, sb)
            dkd_ref[rows, hs] = _dot_nt(vnb, dsb)
            q_do = _dot_tn(qgb, do)
            yield
            dvnb = _bf(dvn)
            dw_ref[rows, hs] = _bf(-_dot_nt(dvnb, sb))
            w_dvn = _dot_tn(wb, dvnb)
            du_ref[rows, hs] = dvnb
            yield
            d_last = jnp.exp(gcs_ref[(chunk + 1) * C - 1:(chunk + 1) * C, h:h + 1])
            dd = jnp.sum(jnp.sum(ds_out * s_in, axis=1, keepdims=True), axis=0, keepdims=True)
            dsc_parts.append(jnp.where((lane == h) & (row == C - 1), dd * d_last, 0.0))
            dstate[h] = ds_out * d_last + q_do - w_dvn

        for chunk in reversed(range(G)):
            dsc_parts = []
            _interleave(head_program(chunk, h, dsc_parts) for h in range(DN_HEADS))
            dsc_ref[chunk * C:(chunk + 1) * C, :] = sum(dsc_parts[1:], dsc_parts[0])
        dgn_ref[...] += sum(dgn_parts[1:], dgn_parts[0])

    rev = lambda n: steps - 1 - n
    wide = pl.BlockSpec((R, 512), lambda n: (rev(n), 0))
    z_spec = pl.BlockSpec((R, 512), lambda n: (rev(n), BLK_Z))
    sq = pl.BlockSpec((DN_HEADS, R, CPAD), lambda n: (0, rev(n), 0))
    narrow = pl.BlockSpec((R, 128), lambda n: (rev(n), 0))
    gn_spec = pl.BlockSpec((1, DN_DIM), lambda n: (0, 0))
    f = lambda *shp: jax.ShapeDtypeStruct(shp, F32)
    b = lambda *shp: jax.ShapeDtypeStruct(shp, BF16)
    return pl.pallas_call(
        body, grid=(steps,), name="dn_scan_bwd",
        in_specs=[pl.BlockSpec((R, 512), lambda n: (rev(n), 1)), wide, z_spec, gn_spec,
                  pl.BlockSpec((G, DN_HEADS, DN_DIM, DN_DIM), lambda n: (rev(n), 0, 0, 0)),
                  wide, wide, wide, wide, sq, narrow, pl.BlockSpec(memory_space=pl.ANY)],
        out_specs=[wide, wide, wide, wide, sq, z_spec, narrow, gn_spec],
        out_shape=[b(S, 512), b(S, 512), f(S, 512), f(S, 512), f(DN_HEADS, S, CPAD),
                   jax.ShapeDtypeStruct(dproj.shape, dproj.dtype), f(S, 128), f(1, DN_DIM)],
        scratch_shapes=[pltpu.VMEM((DN_HEADS, DN_DIM, DN_DIM), F32)],
        input_output_aliases={11: 5},
        compiler_params=_params(("arbitrary",)),
    )(dcat, o, pz, gn, sst, vnew, w, qg, kd, a, gcs, dproj)


def _dn_chunk_bwd(qkv, pg, t_inv, gcs, du, dw, dqg, dkd, da, dsc, a_log, dt_bias, dproj):
    S = qkv.shape[0]
    C = DN_CHUNK
    G = CHUNKS_LOCAL
    R = G * C

    def body(alog_ref, dtb_ref, qkv_ref, pg_ref, t_ref, gcs_ref, du_ref, dw_ref, dqg_ref, dkd_ref, da_ref, dsc_ref, _,
             dqkv_ref, dpg_ref, acc_ref):
        @pl.when(pl.program_id(0) == 0)
        def _():
            acc_ref[...] = jnp.zeros_like(acc_ref)

        ii, jj = _chunk_masks()
        lane = lax.broadcasted_iota(jnp.int32, (1, 128), 1)
        row8 = lax.broadcasted_iota(jnp.int32, (8, 128), 0)
        lane8 = lax.broadcasted_iota(jnp.int32, (8, 128), 1)
        rowc = lax.broadcasted_iota(jnp.int32, (C, 1), 0)
        tril, strict = jj <= ii, jj < ii
        dpg_parts, acc_parts = [[] for _ in range(G)], []

        def head_program(chunk, h):
            rows = slice(chunk * C, (chunk + 1) * C)
            q, k, v = qkv_ref[rows, _head(h)], qkv_ref[rows, _head(DN_HEADS + h)], qkv_ref[rows, _head(2 * DN_HEADS + h)]
            gc_col, beta, g_col = gcs_ref[rows, h:h + 1], gcs_ref[rows, DN_HEADS + h:DN_HEADS + h + 1], \
                gcs_ref[rows, 2 * DN_HEADS + h:2 * DN_HEADS + h + 1]
            dec = _decay(gc_col, ii, jj)
            eg = jnp.exp(gc_col)
            g_last = gc_col[C - 1:C, :]
            ek = jnp.exp(g_last - gc_col)
            kb, vb = k * beta, v * beta
            kbg = kb * eg
            qb, kbb = _bf(q), _bf(kb)
            k_rows = _rows_pad(_bf(k))
            t = t_ref[h, rows]
            tb = _bf(t)
            dub, dwb = du_ref[rows, _head(h)], dw_ref[rows, _head(h)]
            dqg_, dkd_ = dqg_ref[rows, _head(h)], dkd_ref[rows, _head(h)]
            dt = _dot_nt(dub, _rows_pad(_bf(vb))) + _dot_nt(dwb, _rows_pad(_bf(kbg)))
            t_du_dw = _dot_tn(tb, jnp.concatenate([dub, dwb], axis=1))
            dvb, dkbg = t_du_dw[:C, :DN_DIM], t_du_dw[:C, DN_DIM:]
            kk = _dot_nt(kbb, k_rows)
            qk = _dot_nt(qb, k_rows)
            yield
            dt_t = _dot3_nt(dt, t)
            yield
            dl = -_dot3_tn(t, dt_t)
            yield
            dm = jnp.where(strict, dl * dec, 0.0)
            dqk = jnp.where(tril, da_ref[h, rows] * dec, 0.0)
            gmat = dm * kk + dqk * qk
            dgc = jnp.sum(gmat, axis=1, keepdims=True) - _row_to_col(jnp.sum(gmat, axis=0, keepdims=True), ii, jj)
            dmb, dqkb = _bf(dm), _bf(dqk)
            yield
            dkb = _dot(dmb, k_rows) + dkbg * eg
            dk = _dot_tn(jnp.concatenate([dmb, dqkb], axis=0), jnp.concatenate([kbb, qb], axis=0))[:C] + dkd_ * ek
            dq = _dot(dqkb, k_rows) + dqg_ * eg
            yield
            tk = jnp.sum(dkd_ * k * ek, axis=1, keepdims=True)
            dgc = dgc + jnp.sum(dqg_ * q * eg, axis=1, keepdims=True) - tk + jnp.sum(dkbg * kbg, axis=1, keepdims=True)
            dgl = jnp.sum(tk, axis=0, keepdims=True) + dsc_ref[(chunk + 1) * C - 1:(chunk + 1) * C, h:h + 1]
            dgc = dgc + jnp.where(rowc == C - 1, dgl, 0.0)
            yield
            dk = dk + dkb * beta
            dbeta = jnp.sum(dkb * k, axis=1, keepdims=True) + jnp.sum(dvb * v, axis=1, keepdims=True)
            dqkv_ref[rows, _head(h)] = dq
            dqkv_ref[rows, _head(DN_HEADS + h)] = dk
            dqkv_ref[rows, _head(2 * DN_HEADS + h)] = dvb * beta
            dg_col = jnp.sum(jnp.where(jj >= ii, _col_to_row(dgc, ii, jj), 0.0), axis=1, keepdims=True)
            yield
            db = dbeta * beta * (1.0 - beta)
            da_in = dg_col * (-jnp.exp(alog_ref[h])) * _sigmoid(pg_ref[rows, DN_HEADS + h:DN_HEADS + h + 1] + dtb_ref[h])
            dpg_parts[chunk].append(jnp.where(lane == h, db, 0.0) + jnp.where(lane == DN_HEADS + h, da_in, 0.0))
            acc_parts.append(jnp.where((row8 == 0) & (lane8 == h), jnp.sum(dg_col * g_col, axis=0, keepdims=True), 0.0)
                             + jnp.where((row8 == 1) & (lane8 == h), jnp.sum(da_in, axis=0, keepdims=True), 0.0))

        _interleave(head_program(chunk, h) for chunk in range(G) for h in range(DN_HEADS))
        for chunk in range(G):
            dpg = sum(dpg_parts[chunk][1:], dpg_parts[chunk][0])
            dpg_ref[chunk * C:(chunk + 1) * C, :] = _bf(jnp.concatenate([dpg, jnp.zeros_like(dpg)], axis=1))
        acc_ref[...] += sum(acc_parts[1:], acc_parts[0])

    smem = pl.BlockSpec(memory_space=pltpu.SMEM)
    wide = pl.BlockSpec((R, 512), lambda n: (n, 0))
    sq = pl.BlockSpec((DN_HEADS, R, CPAD), lambda n: (0, n, 0))
    narrow = pl.BlockSpec((R, 128), lambda n: (n, 0))
    qkv_spec = pl.BlockSpec((R, 1536), lambda n: (n, 0))
    f = lambda *shp: jax.ShapeDtypeStruct(shp, F32)
    return pl.pallas_call(
        body, grid=(S // R,), name="dn_chunk_bwd",
        in_specs=[smem, smem, qkv_spec, pl.BlockSpec((R, 128), lambda n: (n, BLK_G)), sq, narrow, wide, wide, wide, wide, sq,
                  narrow, pl.BlockSpec(memory_space=pl.ANY)],
        out_specs=[qkv_spec, pl.BlockSpec((R, 256), lambda n: (n, BLK_G_PAD)), pl.BlockSpec((8, 128), lambda n: (0, 0))],
        out_shape=[f(S, 1536), jax.ShapeDtypeStruct(dproj.shape, dproj.dtype), f(8, 128)],
        input_output_aliases={12: 1},
        compiler_params=_params(("arbitrary",)),
    )(a_log, dt_bias, qkv, pg, t_inv, gcs, du, dw, dqg, dkd, da, dsc, dproj)


_W_IN_SECTIONS = ((0, 0, 512), (2304, 512, 512), (768, 1024, 1536), (512, 2560, 256), (2816, 2816, 8))
_HALF = D_MODEL // 2
_SECTION_ROWS = 256


def _pack_pairs(x):
    bits = lax.bitcast_convert_type(x, jnp.uint32)
    return lax.bitcast_convert_type(bits[:, _HALF:] | (bits[:, :_HALF] >> 16), F32)


def _unpack_pairs(words):
    bits = lax.bitcast_convert_type(words, jnp.uint32)
    return (lax.bitcast_convert_type(bits << 16, F32),
            lax.bitcast_convert_type(bits & jnp.uint32(0xFFFF0000), F32))


def _w_in_to_internal(packed):
    starts = [dst for _, dst, _ in _W_IN_SECTIONS] + [D_IN_PAD]

    def body(x_hbm, o_ref, words, sems):
        copies = [pltpu.make_async_copy(x_hbm.at[pl.ds(src, rows), 0, :], words.at[pl.ds(dst, rows)], sems.at[i])
                  for i, (src, dst, rows) in enumerate(_W_IN_SECTIONS)]
        for cp in copies:
            cp.start()
        words[pl.ds(D_IN, D_IN_PAD - D_IN), :] = jnp.zeros((D_IN_PAD - D_IN, _HALF), F32)
        for i, cp in enumerate(copies):
            cp.wait()
            for r in range(starts[i], starts[i + 1], _SECTION_ROWS):
                for c, h in enumerate(_unpack_pairs(words[pl.ds(r, _SECTION_ROWS), :])):
                    o_ref[pl.ds(r, _SECTION_ROWS), c * _HALF:(c + 1) * _HALF] = _bf(h)

    assert all((b - a) % _SECTION_ROWS == 0 for a, b in zip(starts, starts[1:])) and starts[-2] + _W_IN_SECTIONS[-1][2] == D_IN
    return pl.pallas_call(body, name="w_in_to_internal", out_shape=jax.ShapeDtypeStruct((D_IN_PAD, D_MODEL), BF16),
                          in_specs=[pl.BlockSpec(memory_space=pl.ANY)],
                          scratch_shapes=[pltpu.VMEM((D_IN_PAD, _HALF), F32), pltpu.SemaphoreType.DMA((len(_W_IN_SECTIONS),))],
                          compiler_params=pltpu.CompilerParams(vmem_limit_bytes=VMEM_LIMIT))(packed)


def _w_in_from_internal(gt):
    def body(g_ref, o_hbm, words, sems):
        copies = []
        for i, (dst, src, rows) in enumerate(_W_IN_SECTIONS):
            for r in range(src, src + rows, _SECTION_ROWS):
                n = max(min(_SECTION_ROWS, src + rows - r), 16)
                words[pl.ds(r, n), :] = _pack_pairs(g_ref[pl.ds(r, n), :].astype(F32))
            copies.append(pltpu.make_async_copy(words.at[pl.ds(src, rows)], o_hbm.at[pl.ds(dst, rows), 0, :], sems.at[i]))
            copies[-1].start()
        for cp in copies:
            cp.wait()

    return pl.pallas_call(body, name="w_in_from_internal", out_shape=jax.ShapeDtypeStruct((D_IN, 1, _HALF), F32),
                          out_specs=pl.BlockSpec(memory_space=pl.ANY),
                          scratch_shapes=[pltpu.VMEM((D_IN_PAD, _HALF), F32), pltpu.SemaphoreType.DMA((len(_W_IN_SECTIONS),))],
                          compiler_params=pltpu.CompilerParams(vmem_limit_bytes=VMEM_LIMIT))(gt)


def _local_step(x, p, target, wts, first_weights, other_weights, ship_early, after):
    S = x.shape[0]
    cos, sin = _rope_tables(S)
    sinks, a_log, dt_bias = wts["sinks"].reshape(8), wts["a_log"].reshape(4), wts["dt_bias"].reshape(4)
    gn = wts["dn_norm"].reshape(1, DN_DIM)
    add = lambda acc, res: (acc + res,)

    u = _rmsnorm_fwd(x, wts["norm_mix"], "norm_mix_fwd", after)
    w_in_t, conv_w = first_weights(u)
    proj, = _mm(u, w_in_t, form="nt", name="in_proj", out_dtypes=[F32], tn=512)
    attn, lse = _attn_fwd(proj, cos, sin, sinks)
    qkv = _dn_prep_fwd(proj, conv_w)
    cw, cu, cqg, ckd, ca, ct, gcs = _dn_chunk_fwd(qkv, proj, a_log, dt_bias)
    o, vnew, sst, dn_out = _dn_scan_fwd(cw, cu, cqg, ckd, ca, gcs, proj, gn)
    w_o, = other_weights(("w_o",), dn_out)
    h1, = _mm([attn, dn_out], w_o, form="nn", name="out_proj", out_dtypes=[F32], tn=512, epi=add, extra=[x])

    w_up, w_down = other_weights(("w_up", "w_down"), h1)
    hid, relu, m, h2 = _mlp_fwd(h1, w_up, w_down, wts["norm_mlp"])
    w_pg, w_pp = other_weights(("w_ple_gate", "w_ple_proj"), h2)
    n3, dh2, dgl, dpp, loss, d_norm_final, d_norm_ple = _ple_and_loss(h2, p, target, w_pg, w_pp, wts["norm_ple"],
                                                                     wts["norm_final"].reshape(1, D_MODEL))
    g = {"norm_final": d_norm_final, "norm_ple": d_norm_ple}
    early = {"w_ple_gate": _mm_tn(n3, dgl, name="d_w_ple_gate", tm=512, tn=1024, out_dtype=BF16).reshape(N_DEV, 128, 1024),
             "w_ple_proj": _mm_tn(p, dpp, name="d_w_ple_proj", tm=256, tn=128, out_dtype=BF16, column_shards=True)}
    d_act, = _mm(dh2, w_down, form="nt", name="d_hidden", out_dtypes=[BF16], tn=512,
                 epi=lambda acc, r: (acc * (2.0 * r.astype(F32)),), extra=[relu])
    early["w_down"] = _mm_tn(hid, dh2, name="d_w_down", tm=512, tn=1024, out_dtype=BF16).reshape(N_DEV, 512, 1024)
    early["w_up"] = _mm_tn(m, d_act, name="d_w_up", tm=1024, tn=512, out_dtype=BF16, column_shards=True)
    token = ship_early(early)
    dh1, g["norm_mlp"], dcat = _mm(d_act, w_up, form="nt", name="d_m", out_dtypes=[F32], tn=512, after=token,
                                   norm_bwd=(h1, wts["norm_mlp"], dh2), then_nt=w_o)
    d_w_o = _mm_tn([attn, dn_out], dh1, name="d_w_o", tm=512, tn=512, out_dtype=BF16)
    token = ship_early({"w_o": d_w_o.reshape(N_DEV, 128, 1024)})
    dproj, dk, dv, dsinks = _attn_bwd(proj, cos, sin, sinks, dcat, attn, lse, token)
    g["sinks"] = dsinks[:, 0].reshape(1, 8)
    du_, dw_, dqg, dkd, da, dproj, dsc, g["dn_norm"] = _dn_scan_bwd(dcat, o, proj, gn, sst, vnew, cw, cqg, ckd, ca, gcs, dproj)
    dqkv, dproj, gate_acc = _dn_chunk_bwd(qkv, proj, ct, gcs, du_, dw_, dqg, dkd, da, dsc, a_log, dt_bias, dproj)
    g["a_log"], g["dt_bias"] = gate_acc[0:1, 0:4], gate_acc[1:2, 0:4]
    dproj, g["conv_w"] = _dn_prep_bwd(proj, conv_w, dqkv, dproj, dk, dv)
    token = ship_early({"w_in": _mm_tn(dproj, u, name="d_w_in", tm=512, tn=1024, out_dtype=BF16)})
    grad_x, g["norm_mix"] = _mm(dproj, w_in_t, form="nn", name="d_u", out_dtypes=[F32], tn=512, after=token,
                                norm_bwd=(x, wts["norm_mix"], dh1))
    return loss, grad_x, g


def _peer(k):
    x, y, c = lax.axis_index("x"), lax.axis_index("y"), lax.axis_index("c")
    px = 1 - x if k & 4 else x
    py = 1 - y if k & 2 else y
    pc = 1 - c if k & 1 else c
    return (px, py, pc), 4 * px + 2 * py + pc


def _exchange(srcs, name, gather):
    n = len(srcs)
    gathers = list(gather) if isinstance(gather, (list, tuple)) else [gather] * n
    shapes = [(N_DEV,) + s.shape if gt else s.shape for s, gt in zip(srcs, gathers)]

    def body(*refs):
        src_refs, out_refs = refs[:n], refs[n:2 * n]
        send_sems, recv_sems, local_sems = refs[2 * n:]
        _, me = _peer(0)
        piece = lambda a, d: src_refs[a] if gathers[a] else src_refs[a].at[d]
        local = [pltpu.make_async_copy(piece(a, me), out_refs[a].at[me], local_sems.at[a]) for a in range(n)]
        for cp in local:
            cp.start()
        copies = []
        for a in range(n):
            for k in range(1, N_DEV):
                dev, idx = _peer(k)
                cp = pltpu.make_async_remote_copy(src_ref=piece(a, idx), dst_ref=out_refs[a].at[me],
                                                  send_sem=send_sems.at[a, k - 1], recv_sem=recv_sems.at[a, k - 1],
                                                  device_id=dev, device_id_type=MESH)
                cp.start()
                copies.append(cp)
        for cp in copies:
            cp.wait_recv()
        for cp in copies:
            cp.wait_send()
        for cp in local:
            cp.wait()

    anywhere = pl.BlockSpec(memory_space=pl.ANY)
    return pl.pallas_call(
        body, name=name, in_specs=[anywhere] * n, out_specs=[anywhere] * n,
        out_shape=[jax.ShapeDtypeStruct(shp, s.dtype) for shp, s in zip(shapes, srcs)],
        scratch_shapes=[pltpu.SemaphoreType.DMA((n, N_DEV - 1)), pltpu.SemaphoreType.DMA((n, N_DEV - 1)),
                        pltpu.SemaphoreType.DMA((n,))],
    )(*srcs)


_HBM = pl.BlockSpec(memory_space=pltpu.HBM)
_SEM = pl.BlockSpec(memory_space=pltpu.SEMAPHORE)
_EFFECT = pltpu.SideEffectType.DATAFLOW_SIDE_EFFECTING


def _split_copies(src_refs, land_refs, send_sems, recv_sems, modes, which=None):
    _, me = _peer(0)
    local, remote = [], []
    which = range(len(src_refs)) if which is None else which
    for a, src, land in zip(which, src_refs, land_refs):
        if modes[a] == "columns":
            n_cols = src.shape[1]
            dst = land.at[:, pl.ds(pl.multiple_of(me * n_cols, n_cols), n_cols)]
        else:
            dst = land.at[me]
        part = lambda d: src.at[d] if modes[a] == "pieces" else src
        local.append(pltpu.make_async_copy(part(me), dst, recv_sems.at[a * N_DEV]))
        for k in ((2, 4, 6) if modes[a] == "chips" else range(1, N_DEV)):
            dev, idx = _peer(k)
            sem = a * N_DEV + k
            remote.append(pltpu.make_async_remote_copy(
                src_ref=part(idx), dst_ref=dst, send_sem=send_sems.at[sem], recv_sem=recv_sems.at[sem],
                device_id=dev, device_id_type=MESH))
    return local, remote


def _forward_copies(land_refs, send_sems, recv_sems):
    c = lax.axis_index("c")
    sibling, _ = _peer(1)
    copies = []
    for a, land in enumerate(land_refs):
        for chip in range(N_DEV // 2):
            slot = 2 * chip + c
            sem = a * (N_DEV // 2) + chip
            copies.append(pltpu.make_async_remote_copy(
                src_ref=land.at[slot], dst_ref=land.at[slot], send_sem=send_sems.at[sem], recv_sem=recv_sems.at[sem],
                device_id=sibling, device_id_type=MESH))
    return copies


def _forward_start(lands, name):
    n = len(lands)

    def body(*refs):
        for cp in _forward_copies(refs[:n], refs[n], refs[n + 1]):
            cp.start()
        refs[-1][...] = jnp.zeros_like(refs[-1])

    sems = pltpu.SemaphoreType.DMA((n * (N_DEV // 2),))
    out = pl.pallas_call(
        body, name=name,
        out_shape=(sems, sems, *[pltpu.HBM(t.shape, t.dtype) for t in lands], jax.ShapeDtypeStruct((8, 128), F32)),
        in_specs=[_HBM] * n, out_specs=(_SEM, _SEM, *[_HBM] * n, pl.BlockSpec(memory_space=pltpu.VMEM)),
        input_output_aliases={i: 2 + i for i in range(n)},
        compiler_params=pltpu.CompilerParams(has_side_effects=_EFFECT),
    )(*[pltpu.with_memory_space_constraint(t, pltpu.HBM) for t in lands])
    return out[:-1], out[-1]


def _forward_wait(handle, after, name):
    send_sems, recv_sems, *lands = handle
    n = len(lands)

    def body(*refs):
        for cp in _forward_copies(refs[:n], refs[n], refs[n + 1]):
            cp.wait_send()
            cp.wait_recv()

    return list(pl.pallas_call(
        body, name=name, out_shape=tuple(pltpu.HBM(t.shape, t.dtype) for t in lands),
        in_specs=[_HBM] * n + [_SEM, _SEM, pl.BlockSpec(memory_space=pl.ANY)], out_specs=tuple([_HBM] * n),
        input_output_aliases={i: i for i in range(n)},
        compiler_params=pltpu.CompilerParams(has_side_effects=_EFFECT),
    )(*lands, send_sems, recv_sems, after))


def _exchange_start(srcs, name, modes):
    n = len(srcs)
    modes = [modes] * n if isinstance(modes, str) else list(modes)
    lands = []
    for s, mode in zip(srcs, modes):
        shape = {"columns": (s.shape[0], N_DEV * s.shape[1]), "pieces": s.shape}.get(mode, (N_DEV,) + s.shape)
        lands.append(lax.empty(shape, s.dtype))

    def body(*refs):
        src_refs, land_refs = refs[:n], refs[n:2 * n]
        send_sems, recv_sems = refs[2 * n], refs[2 * n + 1]
        local, remote = _split_copies(src_refs, land_refs, send_sems, recv_sems, modes)
        for cp in local + remote:
            cp.start()
        refs[-1][...] = jnp.zeros_like(refs[-1])

    both = list(srcs) + lands
    sems = pltpu.SemaphoreType.DMA((n * N_DEV,))
    out = pl.pallas_call(
        body, name=name,
        out_shape=(sems, sems, *[pltpu.HBM(t.shape, t.dtype) for t in both], jax.ShapeDtypeStruct((8, 128), F32)),
        in_specs=[_HBM] * (2 * n), out_specs=(_SEM, _SEM, *[_HBM] * (2 * n), pl.BlockSpec(memory_space=pltpu.VMEM)),
        input_output_aliases={i: 2 + i for i in range(2 * n)},
        compiler_params=pltpu.CompilerParams(has_side_effects=_EFFECT),
    )(*[pltpu.with_memory_space_constraint(t, pltpu.HBM) for t in both])
    return (n, modes, out[:-1]), out[-1]


def _exchange_wait(handle, after, name, which=None):
    n_all, modes, (send_sems, recv_sems, *both_all) = handle
    which = list(range(n_all)) if which is None else list(which)
    n = len(which)
    both = [both_all[a] for a in which] + [both_all[n_all + a] for a in which]

    def body(*refs):
        src_refs, land_refs = refs[:n], refs[n:2 * n]
        local, remote = _split_copies(src_refs, land_refs, refs[2 * n], refs[2 * n + 1], modes, which)
        for cp in local:
            cp.wait()
        for cp in remote:
            cp.wait_send()
            cp.wait_recv()

    out = pl.pallas_call(
        body, name=name, out_shape=tuple(pltpu.HBM(t.shape, t.dtype) for t in both),
        in_specs=[_HBM] * (2 * n) + [_SEM, _SEM, pl.BlockSpec(memory_space=pl.ANY)], out_specs=tuple([_HBM] * (2 * n)),
        input_output_aliases={i: i for i in range(2 * n)},
        compiler_params=pltpu.CompilerParams(has_side_effects=_EFFECT),
    )(*both, send_sems, recv_sems, after)
    return list(out[n:])


def _cast_all(arrays, name, after):
    waits = [] if after is None else [after]

    def body(*refs):
        for src, dst in zip(refs[:len(arrays)], refs[len(arrays) + len(waits):]):
            if len(src.shape) == 2:
                dst[...] = _bf(src[...])
            else:
                dst[:, 0, :] = _pack_pairs(_bf(src[:, 0, :]).astype(F32))

    shapes = [jax.ShapeDtypeStruct(a.shape, BF16) if a.ndim == 2 else jax.ShapeDtypeStruct((a.shape[0], 1, a.shape[2] // 2), F32)
              for a in arrays]
    return pl.pallas_call(body, name=name, out_shape=shapes,
                          compiler_params=pltpu.CompilerParams(vmem_limit_bytes=VMEM_LIMIT))(*arrays, *waits)


def _adam_update(g, w, m, v):
    nm = ADAM_B1 * m + (1.0 - ADAM_B1) * g
    nv = ADAM_B2 * v + (1.0 - ADAM_B2) * (g * g)
    m_hat = nm / (1.0 - ADAM_B1 ** ADAM_STEP)
    v_hat = nv / (1.0 - ADAM_B2 ** ADAM_STEP)
    return -ADAM_LR * (m_hat / (jnp.sqrt(v_hat) + ADAM_EPS) + ADAM_WD * w), nm, nv


def _adamw(parts, w, m, v, name):
    n, R, W = parts.shape
    tm = 128 if R % 128 == 0 else R

    def body(p_ref, w_ref, m_ref, v_ref, g_ref, d_ref, nm_ref, nv_ref):
        g = p_ref[0].astype(F32)
        for s in range(1, n):
            g = g + p_ref[s].astype(F32)
        g_ref[...] = g
        d_ref[...], nm_ref[...], nv_ref[...] = _adam_update(g, w_ref[...], m_ref[...], v_ref[...])

    tile = pl.BlockSpec((tm, W), lambda i: (i, 0))
    return pl.pallas_call(
        body, grid=(R // tm,), name=name,
        in_specs=[pl.BlockSpec((n, tm, W), lambda i: (0, i, 0)), tile, tile, tile],
        out_specs=[tile] * 4, out_shape=[jax.ShapeDtypeStruct((R, W), F32)] * 4,
        compiler_params=_params(("parallel",)),
    )(parts, w, m, v)


def _adamw_rows_apart(parts, w, m, v, name):
    n, R, _, half = parts.shape

    def body(p_hbm, w_hbm, m_hbm, v_hbm, *rest):
        out_hbm, (words, given, results, sems) = rest[:4], rest[4:]
        loads = [pltpu.make_async_copy(p_hbm.at[s, :, 0, :], words.at[s], sems.at[s]) for s in range(n)]
        loads += [pltpu.make_async_copy(h.at[:, 0, :], given.at[i], sems.at[n + i]) for i, h in enumerate((w_hbm, m_hbm, v_hbm))]
        for cp in loads:
            cp.start()
        for cp in loads:
            cp.wait()
        part = lambda s: jnp.concatenate(_unpack_pairs(words[s]), axis=1)
        g = part(0)
        for s in range(1, n):
            g = g + part(s)
        stores = []
        for i, val in enumerate((g,) + _adam_update(g, given[0], given[1], given[2])):
            results[i] = val
            stores.append(pltpu.make_async_copy(results.at[i], out_hbm[i].at[:, 0, :], sems.at[n + 3 + i]))
            stores[-1].start()
        for cp in stores:
            cp.wait()

    anywhere = pl.BlockSpec(memory_space=pl.ANY)
    return pl.pallas_call(
        body, name=name, in_specs=[anywhere] * 4, out_specs=[anywhere] * 4,
        out_shape=[jax.ShapeDtypeStruct(w.shape, F32)] * 4,
        scratch_shapes=[pltpu.VMEM((n, R, half), F32), pltpu.VMEM((3, R, 2 * half), F32), pltpu.VMEM((4, R, 2 * half), F32),
                        pltpu.SemaphoreType.DMA((n + 7,))],
        compiler_params=pltpu.CompilerParams(vmem_limit_bytes=VMEM_LIMIT),
    )(parts, w, m, v)


_MATRICES = ("w_in", "w_o", "w_up", "w_down", "w_ple_gate", "w_ple_proj")


_OTHERS = ("w_o", "w_up", "w_down", "w_ple_gate", "w_ple_proj")
_OTHER_MODES = {"w_o": "slots", "w_up": "slots", "w_down": "slots", "w_ple_gate": "slots", "w_ple_proj": "columns"}


_VECTORS = ("norm_mix", "norm_mlp", "norm_ple", "norm_final", "a_log", "dt_bias", "sinks", "dn_norm")
_SMALL_ROWS, _LOSS_ROW, _CONV_ROW = 16, 8, 9


def _pack_small(vectors, loss, conv):
    def body(*refs):
        out = refs[-1]
        out[...] = jnp.zeros_like(out)
        for r, ref in enumerate(refs[:len(_VECTORS)]):
            out[r:r + 1, 0:ref.shape[1]] = ref[...]
        out[_LOSS_ROW:_LOSS_ROW + 1, 0:128] = refs[len(_VECTORS)][...]
        out[_CONV_ROW:_CONV_ROW + 6, :] = refs[len(_VECTORS) + 1][...]

    return pl.pallas_call(body, name="pack_small", out_shape=jax.ShapeDtypeStruct((_SMALL_ROWS, 1024), F32))(*vectors, loss, conv)


def _sum_slots(parts):
    def body(p_ref, o_ref):
        acc = p_ref[0]
        for s in range(1, parts.shape[0]):
            acc = acc + p_ref[s]
        o_ref[...] = acc

    return pl.pallas_call(body, name="sum_small", out_shape=jax.ShapeDtypeStruct(parts.shape[1:], parts.dtype))(parts)


def _adamw_vectors(summed, conv_g, wmv):
    names = _VECTORS + ("conv_w",)
    flat = [a for triple in wmv for a in triple]

    def body(*refs):
        sum_ref, conv_ref = refs[0], refs[1]
        ins, outs = refs[2:2 + len(flat)], refs[2 + len(flat):]
        for i in range(len(names)):
            w_ref, m_ref, v_ref = ins[3 * i:3 * i + 3]
            g = conv_ref[...] if i == len(_VECTORS) else sum_ref[i:i + 1, 0:w_ref.shape[1]]
            outs[4 * i][...] = g
            outs[4 * i + 1][...], outs[4 * i + 2][...], outs[4 * i + 3][...] = _adam_update(g, w_ref[...], m_ref[...], v_ref[...])

    out_shape = [jax.ShapeDtypeStruct(t[0].shape, F32) for t in wmv for _ in range(4)]
    res = pl.pallas_call(body, name="adamw_vectors", out_shape=out_shape)(summed, conv_g, *flat)
    return {n: res[4 * i:4 * i + 4] for i, n in enumerate(names)}


_ORDER = ("norm_mix", "w_in", "conv_w", "a_log", "dt_bias", "dn_norm", "sinks", "w_o", "norm_mlp", "w_up", "w_down",
          "norm_ple", "w_ple_gate", "w_ple_proj", "norm_final")


def kernel(x, p, norm_mix, w_in, conv_w, a_log, dt_bias, dn_norm, sinks, w_o, norm_mlp, w_up, w_down, norm_ple, w_ple_gate, w_ple_proj, norm_final, loss_target, m_norm_mix, m_w_in, m_conv_w, m_a_log, m_dt_bias, m_dn_norm, m_sinks, m_w_o, m_norm_mlp, m_w_up, m_w_down, m_norm_ple, m_w_ple_gate, m_w_ple_proj, m_norm_final, v_norm_mix, v_w_in, v_conv_w, v_a_log, v_dt_bias, v_dn_norm, v_sinks, v_w_o, v_norm_mlp, v_w_up, v_w_down, v_norm_ple, v_w_ple_gate, v_w_ple_proj, v_norm_final):
    w = dict(norm_mix=norm_mix, w_in=w_in, conv_w=conv_w[0], a_log=a_log, dt_bias=dt_bias, dn_norm=dn_norm, sinks=sinks,
             w_o=w_o[0], norm_mlp=norm_mlp, w_up=w_up[0], w_down=w_down[0], norm_ple=norm_ple, w_ple_gate=w_ple_gate[0],
             w_ple_proj=w_ple_proj[0], norm_final=norm_final)
    m = dict(norm_mix=m_norm_mix, w_in=m_w_in, conv_w=m_conv_w[0], a_log=m_a_log, dt_bias=m_dt_bias, dn_norm=m_dn_norm,
             sinks=m_sinks, w_o=m_w_o[0], norm_mlp=m_norm_mlp, w_up=m_w_up[0], w_down=m_w_down[0], norm_ple=m_norm_ple,
             w_ple_gate=m_w_ple_gate[0], w_ple_proj=m_w_ple_proj[0], norm_final=m_norm_final)
    v = dict(norm_mix=v_norm_mix, w_in=v_w_in, conv_w=v_conv_w[0], a_log=v_a_log, dt_bias=v_dt_bias, dn_norm=v_dn_norm,
             sinks=v_sinks, w_o=v_w_o[0], norm_mlp=v_norm_mlp, w_up=v_w_up[0], w_down=v_w_down[0], norm_ple=v_norm_ple,
             w_ple_gate=v_w_ple_gate[0], w_ple_proj=v_w_ple_proj[0], norm_final=v_norm_final)
    me = 4 * lax.axis_index("x") + 2 * lax.axis_index("y") + lax.axis_index("c")
    conv_shard = conv_w.shape[2]

    for d in (w, m, v):
        d["w_in"] = jnp.transpose(d["w_in"], (2, 0, 1))
    conv_pad = jnp.pad(w["conv_w"], ((0, 8 - DN_CONV), (0, 256 - conv_shard)))
    w_in_shard, = _cast_all([w["w_in"]], "cast_w_in", None)
    gathers_first, token_first = _exchange_start([w_in_shard, conv_pad], "gather_first_start", "chips")
    shards = _cast_all([w[n] for n in _OTHERS], "cast_others", token_first)
    gathers, token_gather = _exchange_start(list(shards), "gather_start", [_OTHER_MODES[n] for n in _OTHERS])

    def first_weights(after):
        over_ici = _exchange_wait(gathers_first, after, "gather_first_wait")
        handle, token = _forward_start(over_ici, "gather_first_forward")
        w_in_all, conv_all = _forward_wait(handle, token, "gather_first_forward_wait")
        conv_all = jnp.transpose(conv_all[:, :DN_CONV, :conv_shard], (1, 0, 2)).reshape(DN_CONV, N_DEV * conv_shard)
        return _w_in_to_internal(w_in_all.reshape(D_IN, 1, _HALF)), conv_all

    as_taken = {"w_o": lambda t: t.reshape(1024, 1024), "w_up": lambda t: t, "w_down": lambda t: t.reshape(4096, 1024),
                "w_ple_gate": lambda t: t.reshape(1024, 1024), "w_ple_proj": lambda t: t}

    def other_weights(names, after):
        which = [_OTHERS.index(n) for n in names]
        got = _exchange_wait(gathers, after, "gather_wait_" + names[0], which)
        return [as_taken[n](t) for n, t in zip(names, got)]

    shipped = []

    def ship_early(pieces):
        names = tuple(pieces)
        if names == ("w_in",):
            pieces = {"w_in": _w_in_from_internal(pieces["w_in"]).reshape(N_DEV, D_IN // N_DEV, 1, _HALF)}
        handle, token = _exchange_start([pieces[n] for n in names], "scatter_start_" + names[0], "pieces")
        shipped.append((names, handle))
        return token

    loss, grad_x, g = _local_step(x[0], p[0, 0], loss_target[0], w, first_weights, other_weights, ship_early, token_gather)

    row = lambda t: t.reshape(1, t.size)
    small = _pack_small([row(g[n]) for n in _VECTORS], loss, g["conv_w"].reshape(6, 1024))
    small_handle, token_small = _exchange_start([small], "gather_small_start", "slots")
    big, after = {}, token_small
    for names, handle in shipped[:-1]:
        for n, r in zip(names, _exchange_wait(handle, after, "scatter_wait_" + names[0])):
            big[n] = _adamw(r, w[n], m[n], v[n], "adamw_" + n)
            after = big[n][1]
    small_all, = _exchange_wait(small_handle, after, "gather_small_wait")
    summed = _sum_slots(small_all)
    conv_g = lax.dynamic_slice(summed[_CONV_ROW:_CONV_ROW + 6].reshape(DN_CONV, N_DEV * conv_shard), (0, me * conv_shard),
                               (DN_CONV, conv_shard))
    small_out = _adamw_vectors(summed, conv_g, [(row(w[n]), row(m[n]), row(v[n])) for n in _VECTORS]
                               + [(w["conv_w"], m["conv_w"], v["conv_w"])])
    names, handle = shipped[-1]
    for n, r in zip(names, _exchange_wait(handle, small_out["conv_w"][0], "scatter_wait_" + names[0])):
        big[n] = _adamw_rows_apart(r, w[n], m[n], v[n], "adamw_" + n)

    result = [summed[_LOSS_ROW, 0], grad_x[None]]
    for i in range(4):
        for n in _ORDER:
            if n == "w_in":
                result.append(jnp.transpose(big[n][i], (1, 2, 0)))
            elif n in _MATRICES:
                result.append(big[n][i][None])
            elif n == "conv_w":
                result.append(small_out[n][i][None])
            else:
                result.append(small_out[n][i].reshape(w[n].shape))
    return tuple(result)
```

```python
import jax
import jax.numpy as jnp
import numpy as np
from jax import lax
from jax.experimental import pallas as pl
from jax.experimental.pallas import tpu as pltpu

F32, BF16 = jnp.float32, jnp.bfloat16
EPS = 1e-6
D_MODEL = 1024
N_DEV = 8
ATTN_BLOCK = 128
HEAD_PAIR = 128
DN_HEADS = 4
DN_DIM = 128
DN_CHUNK = 64
DN_CONV = 4
ROPE_THETA = 10000.0
D_IN = 2824
D_IN_PAD = 3072
BLK_Q, BLK_Z = 0, 1
BLK_DN, BLK_K, BLK_V, BLK_G = 8, 20, 21, 22
BLK_G_PAD = 11
VMEM_LIMIT = 56 * 1024 * 1024
NEG = -1e30
ADAM_LR, ADAM_B1, ADAM_B2, ADAM_EPS, ADAM_WD, ADAM_STEP = 0.001, 0.9, 0.999, 1e-08, 0.01, 10
MESH = pl.DeviceIdType.MESH


def _bf(x):
    return x.astype(BF16)


def _dot(a, b):
    return jnp.dot(a, b, preferred_element_type=F32)


def _dot_nt(a, b):
    return lax.dot_general(a, b, (((1,), (1,)), ((), ())), preferred_element_type=F32)


def _dot_tn(a, b):
    return lax.dot_general(a, b, (((0,), (0,)), ((), ())), preferred_element_type=F32)


def _sigmoid(x):
    return 1.0 / (1.0 + jnp.exp(-x))


def _params(sem):
    return pltpu.CompilerParams(dimension_semantics=sem, vmem_limit_bytes=VMEM_LIMIT)


def _mm(x, w, *, form, name, out_dtypes, tn, epi=None, extra=(), tm=512, w_row_block=0, after=None, norm=None,
        norm_bwd=None, then_nt=None):
    assert norm is None or norm_bwd is None
    xs = list(x) if isinstance(x, (list, tuple)) else [x]
    nx = len(xs)
    S, K = xs[0].shape
    shards = w.ndim == 3
    N = (w.shape[2] * N_DEV if shards else w.shape[1]) if form == "nn" else w.shape[-2]
    assert not (shards and form == "nn" and tn != w.shape[2]) and (nx == 1 or (form == "nn" and not shards and norm is None))
    r0 = w_row_block * K
    tm = min(tm, S)
    n_extra, n_out = len(extra), len(out_dtypes)
    tile = lambda width: pl.BlockSpec((tm, width), lambda i: (i, 0))
    whole = lambda a: pl.BlockSpec(a.shape, lambda i, nd=a.ndim: (0,) * nd)
    ins, in_specs = [*xs, w, *extra], [tile(K)] * nx + [whole(w)] + [tile(N)] * n_extra
    if norm is not None:
        ins, in_specs = ins + [norm], in_specs + [whole(norm)]
    if norm_bwd is not None:
        ins, in_specs = ins + list(norm_bwd), in_specs + [tile(N), whole(norm_bwd[1]), tile(N)]
    if then_nt is not None:
        ins, in_specs = ins + [then_nt], in_specs + [whole(then_nt)]
    if after is not None:
        ins, in_specs = ins + [after], in_specs + [whole(after)]
    out_shape = [jax.ShapeDtypeStruct((S, N), dt) for dt in out_dtypes]
    out_specs = [tile(N)] * n_out
    if norm is not None:
        out_shape, out_specs = out_shape + [jax.ShapeDtypeStruct((S, K), BF16)], out_specs + [tile(K)]
    if norm_bwd is not None:
        out_shape, out_specs = out_shape + [jax.ShapeDtypeStruct((1, N), F32)], out_specs + [pl.BlockSpec((1, N), lambda i: (0, 0))]
    if then_nt is not None:
        out_shape, out_specs = out_shape + [jax.ShapeDtypeStruct((S, then_nt.shape[0]), F32)], out_specs + [tile(then_nt.shape[0])]

    def product(xb, w_ref, cols, c):
        if form == "nn" and nx > 1:
            return sum(_dot(part, w_ref[r0 + p * K:r0 + (p + 1) * K, cols]) for p, part in enumerate(xb))
        if form == "nn":
            return _dot(xb, w_ref[c] if shards else w_ref[r0:r0 + K, cols])
        if not shards:
            return _dot_nt(xb, w_ref[cols, :])
        ks = w.shape[2]
        acc = _dot_nt(xb[:, 0:ks], w_ref[0, cols, :])
        for s in range(1, N_DEV):
            acc = acc + _dot_nt(xb[:, s * ks:(s + 1) * ks], w_ref[s, cols, :])
        return acc

    def body(*refs):
        x_ref, w_ref = refs[0], refs[nx]
        extra_refs = refs[nx + 1:nx + 1 + n_extra]
        at = nx + 1 + n_extra
        if norm is not None:
            gain_ref, at = refs[at], at + 1
        if norm_bwd is not None:
            (y_ref, ygain_ref, dres_ref), at = refs[at:at + 3], at + 3
        if then_nt is not None:
            w2_ref, at = refs[at], at + 1
        outs = refs[len(ins):]
        if norm is not None:
            _, xh = _rms_stats(x_ref[...])
            xb = _bf(xh * gain_ref[...])
            outs[n_out][...] = xb
        else:
            xb = _bf(x_ref[...]) if nx == 1 else [_bf(r[...]) for r in refs[:nx]]
        for c in range(N // tn):
            cols = slice(c * tn, (c + 1) * tn)
            acc = product(xb, w_ref, cols, c)
            res = epi(acc, *[r[:, cols] for r in extra_refs]) if epi else (acc,)
            for o, r in zip(outs[:n_out], res):
                o[:, cols] = r.astype(o.dtype)
        if norm_bwd is not None:
            dx, dg = _rms_bwd_tile(y_ref[...], ygain_ref[...], outs[0][...])
            outs[0][...] = dres_ref[...] + dx
            dg_ref = outs[n_out]

            @pl.when(pl.program_id(0) == 0)
            def _():
                dg_ref[...] = jnp.zeros_like(dg_ref)

            dg_ref[...] += dg
        if then_nt is not None:
            yb = _bf(outs[0][...])
            for c in range(then_nt.shape[0] // tn):
                cols = slice(c * tn, (c + 1) * tn)
                outs[-1][:, cols] = _dot_nt(yb, w2_ref[cols, :])

    return pl.pallas_call(
        body, grid=(S // tm,), name=name, in_specs=in_specs, out_specs=out_specs, out_shape=out_shape,
        compiler_params=_params(("arbitrary",) if norm_bwd is not None else ("parallel",)),
    )(*ins)


def _mlp_fwd(h1, w_up, w_down, gain):
    S, K = h1.shape
    n_sh, _, fs = w_up.shape
    tm = min(512, S)

    def body(x_ref, wup_ref, wdown_ref, g_ref, hid_ref, relu_ref, m_ref, h2_ref):
        x = x_ref[...]
        _, xh = _rms_stats(x)
        mb = _bf(xh * g_ref[...])
        m_ref[...] = mb
        h2_ref[...] = x
        for c in range(n_sh):
            cols = slice(c * fs, (c + 1) * fs)
            r = jnp.maximum(_dot(mb, wup_ref[c]), 0.0)
            hd = _bf(r * r)
            hid_ref[:, cols] = hd
            relu_ref[:, cols] = _bf(r)
            h2_ref[...] += _dot(hd, wdown_ref[cols, :])

    tile = lambda width: pl.BlockSpec((tm, width), lambda i: (i, 0))
    once = lambda a: pl.BlockSpec(a.shape, lambda i, nd=a.ndim: (0,) * nd, pipeline_mode=pl.Buffered(1))
    F = n_sh * fs
    return pl.pallas_call(
        body, grid=(S // tm,), name="mlp_fwd",
        in_specs=[tile(K), once(w_up), once(w_down), pl.BlockSpec(gain.shape, lambda i: (0, 0))],
        out_specs=[tile(F), tile(F), tile(K), tile(K)],
        out_shape=[jax.ShapeDtypeStruct((S, F), BF16), jax.ShapeDtypeStruct((S, F), BF16),
                   jax.ShapeDtypeStruct((S, K), BF16), jax.ShapeDtypeStruct((S, K), F32)],
        compiler_params=_params(("parallel",)),
    )(h1, w_up, w_down, gain)


def _mm_tn(x, dy, *, name, tm, tn, out_dtype=F32, column_shards=False, after=None):
    xs = list(x) if isinstance(x, (list, tuple)) else [x]
    S, N = dy.shape
    K = x.shape[1] if len(xs) == 1 else tm * len(xs)
    waits = [] if after is None else [after]

    def body(*refs):
        dy_ref, out_ref = refs[len(xs)], refs[-1]
        if len(xs) == 1:
            out_ref[...] = _dot_tn(_bf(refs[0][...]), _bf(dy_ref[...])).astype(out_dtype)
        for k in range(len(xs) if len(xs) > 1 else 0):
            @pl.when(pl.program_id(0) == k)
            def _(k=k):
                out_ref[...] = _dot_tn(_bf(refs[k][...]), _bf(dy_ref[...])).astype(out_dtype)

    if column_shards:
        out_spec = pl.BlockSpec((None, tm, tn), lambda i, j: (j, i, 0))
        out_shape = jax.ShapeDtypeStruct((N // tn, K, tn), out_dtype)
    else:
        out_spec = pl.BlockSpec((tm, tn), lambda i, j: (i, j))
        out_shape = jax.ShapeDtypeStruct((K, N), out_dtype)
    return pl.pallas_call(
        body, grid=(K // tm, N // tn), name=name,
        in_specs=([pl.BlockSpec((S, tm), lambda i, j: (0, i))] if len(xs) == 1 else [pl.BlockSpec((S, tm), lambda i, j: (0, 0))] * len(xs))
        + [pl.BlockSpec((S, tn), lambda i, j: (0, j))] + [pl.BlockSpec(memory_space=pl.ANY)] * len(waits),
        out_specs=out_spec, out_shape=out_shape,
        compiler_params=_params(("parallel", "parallel")),
    )(*xs, dy, *waits)


def _rowwise(body, *, tiled, full, out_tiled, out_acc, name, tm=512, smem=()):
    S = tiled[0].shape[0]
    tm = min(tm, S)
    n_in = len(smem) + len(tiled) + len(full)

    def kern(*refs):
        @pl.when(pl.program_id(0) == 0)
        def _():
            for r in refs[n_in + len(out_tiled):]:
                r[...] = jnp.zeros_like(r)
        body(*refs)

    in_specs = [pl.BlockSpec(memory_space=pltpu.SMEM) for _ in smem]
    in_specs += [pl.BlockSpec((tm, a.shape[1]), lambda i: (i, 0)) for a in tiled]
    in_specs += [pl.BlockSpec(a.shape, lambda i, nd=a.ndim: (0,) * nd) for a in full]
    out_specs = [pl.BlockSpec((tm, w), lambda i: (i, 0)) for w, _ in out_tiled]
    out_specs += [pl.BlockSpec(shp, lambda i, nd=len(shp): (0,) * nd) for shp, _ in out_acc]
    out_shape = [jax.ShapeDtypeStruct((S, w), dt) for w, dt in out_tiled]
    out_shape += [jax.ShapeDtypeStruct(shp, dt) for shp, dt in out_acc]
    return pl.pallas_call(
        kern, grid=(S // tm,), name=name, in_specs=in_specs, out_specs=out_specs, out_shape=out_shape,
        compiler_params=_params(("arbitrary",)),
    )(*smem, *tiled, *full)


def _rms_stats(x):
    r = lax.rsqrt(jnp.mean(x * x, axis=-1, keepdims=True) + EPS)
    return r, x * r


def _rmsnorm_fwd(x, g, name, after):
    def body(x_ref, g_ref, _, o_ref):
        _, xh = _rms_stats(x_ref[...])
        o_ref[...] = _bf(xh * g_ref[...])

    return _rowwise(body, tiled=[x], full=[g, after], out_tiled=[(x.shape[1], BF16)], out_acc=[], name=name)[0]


def _rms_bwd_tile(x, g, dxn):
    r, xh = _rms_stats(x)
    dg = jnp.sum(dxn * xh, axis=0, keepdims=True)
    dn = dxn * g
    dx = r * (dn - xh * jnp.mean(dn * xh, axis=-1, keepdims=True))
    return dx, dg


def _ple_and_loss(h2, p, target, w_pg, w_pp, g_ple, g_final):
    S, n = h2.shape
    tm = min(512, S)
    tn = 512

    def body(h2_ref, p_ref, t_ref, wpg_ref, wpp_ref, gple_ref, gfin_ref,
             n3_ref, dh_ref, dgl_ref, dpp_ref, loss_ref, dg_ref, dgple_ref, pp, gate, h3):
        @pl.when(pl.program_id(0) == 0)
        def _():
            loss_ref[...] = jnp.zeros_like(loss_ref)
            dg_ref[...] = jnp.zeros_like(dg_ref)
            dgple_ref[...] = jnp.zeros_like(dgple_ref)

        x = h2_ref[...]
        _, xh = _rms_stats(x)
        n3 = _bf(xh * gple_ref[...])
        n3_ref[...] = n3
        pb = _bf(p_ref[...])
        for c in range(n // tn):
            cols = slice(c * tn, (c + 1) * tn)
            pp[:, cols] = _dot(pb, wpp_ref[:, cols])
            gt = _sigmoid(_dot(n3, wpg_ref[:, cols]))
            gate[:, cols] = gt
            h3[:, cols] = x[:, cols] + gt * pp[:, cols]
        y = h3[...]
        _, yh = _rms_stats(y)
        e = yh * gfin_ref[...] - t_ref[...]
        per_tok = jnp.mean(e * e, axis=-1, keepdims=True)
        loss_ref[...] += 0.5 * jnp.sum(per_tok, axis=0, keepdims=True)
        dh, dg = _rms_bwd_tile(y, gfin_ref[...], e * (1.0 / n))
        dg_ref[...] += dg
        gt = gate[...]
        dgl = _bf(dh * pp[...] * gt * (1.0 - gt))
        dgl_ref[...] = dgl
        dpp_ref[...] = _bf(dh * gt)
        for c in range(n // tn):
            cols = slice(c * tn, (c + 1) * tn)
            h3[:, cols] = _dot_nt(dgl, wpg_ref[cols, :])
        dx, dgp = _rms_bwd_tile(x, gple_ref[...], h3[...])
        dh_ref[...] = dh + dx
        dgple_ref[...] += dgp

    tile = lambda width: pl.BlockSpec((tm, width), lambda i: (i, 0))
    whole = lambda a: pl.BlockSpec(a.shape, lambda i, nd=a.ndim: (0,) * nd)
    return pl.pallas_call(
        body, grid=(S // tm,), name="ple_and_loss",
        in_specs=[tile(n), tile(p.shape[1]), tile(n), whole(w_pg), whole(w_pp), whole(g_ple), whole(g_final)],
        out_specs=[tile(n), tile(n), tile(n), tile(n), pl.BlockSpec((1, 128), lambda i: (0, 0)),
                   pl.BlockSpec((1, n), lambda i: (0, 0)), pl.BlockSpec((1, n), lambda i: (0, 0))],
        out_shape=[jax.ShapeDtypeStruct((S, n), BF16), jax.ShapeDtypeStruct((S, n), F32), jax.ShapeDtypeStruct((S, n), BF16),
                   jax.ShapeDtypeStruct((S, n), BF16), jax.ShapeDtypeStruct((1, 128), F32), jax.ShapeDtypeStruct((1, n), F32),
                   jax.ShapeDtypeStruct((1, n), F32)],
        scratch_shapes=[pltpu.VMEM((tm, n), F32)] * 3,
        compiler_params=_params(("arbitrary",)),
    )(h2, p, target, w_pg, w_pp, g_ple, g_final)


def _rope_tables(S):
    half = 32
    inv = (1.0 / (np.float32(ROPE_THETA) ** (np.arange(half, dtype=np.float32) * np.float32(2.0 / 64)))).astype(np.float32)
    ang = np.arange(S).astype(np.float32)[:, None] * inv[None, :]
    cos, sin = np.cos(ang), np.sin(ang)
    return jnp.asarray(np.tile(cos, (1, 4))), jnp.asarray(np.concatenate([-sin, sin, -sin, sin], axis=1))


def _attn_common(i, kc, kp, vc, vp, cc, sc, cp, sp):
    lane = lax.broadcasted_iota(jnp.int32, (1, HEAD_PAIR), 1)
    lane_lo = jnp.bitwise_and(lane, 63) < 32
    slot = [lane < 64, lane >= 64]

    def swap_halves(t):
        return jnp.where(lane_lo, pltpu.roll(t, 96, 1), pltpu.roll(t, 32, 1))

    def rope(t, cos, sin):
        return t * cos + swap_halves(t) * sin

    def unrope(d, cos, sin):
        return d * cos + swap_halves(d * sin)

    k2 = jnp.concatenate([rope(kp, cp, sp), rope(kc, cc, sc)], axis=0)
    v2 = jnp.concatenate([vp, vc], axis=0)
    r = lax.broadcasted_iota(jnp.int32, (ATTN_BLOCK, 2 * ATTN_BLOCK), 0)
    c = lax.broadcasted_iota(jnp.int32, (ATTN_BLOCK, 2 * ATTN_BLOCK), 1)
    valid = (c > r) & (c <= r + ATTN_BLOCK) & jnp.logical_or(c >= ATTN_BLOCK, i > 0)
    ks, vs = {}, {}
    for j in range(2):
        kn = jnp.where(slot[j], k2, 0.0)
        vn = jnp.where(slot[j], v2, 0.0)
        for s in range(2):
            ks[j, s] = _bf(kn if s == j else pltpu.roll(kn, 64, 1))
            vs[j, s] = _bf(vn if s == j else pltpu.roll(vn, 64, 1))
    return slot, rope, unrope, valid, ks, vs


def _attn_probs(scores, valid, sink):
    s = jnp.where(valid, scores * 0.125, NEG)
    m = jnp.maximum(jnp.max(s, axis=1, keepdims=True), sink)
    e = jnp.exp(s - m)
    z = jnp.sum(e, axis=1, keepdims=True) + jnp.exp(sink - m)
    return e * (1.0 / z), m + jnp.log(z)


def _attn_specs(S):
    nb = S // ATTN_BLOCK
    prev = lambda i: jnp.maximum(i - 1, 0)
    blk = lambda w, col, row=(lambda i: i): pl.BlockSpec((ATTN_BLOCK, w), lambda i: (row(i), col))
    in_specs = [pl.BlockSpec(memory_space=pltpu.SMEM),
                blk(512, BLK_Q), blk(128, BLK_K), blk(128, BLK_K, prev), blk(128, BLK_V), blk(128, BLK_V, prev),
                blk(128, 0), blk(128, 0), blk(128, 0, prev), blk(128, 0, prev)]
    return nb, in_specs


def _attn_fwd(pa, cos, sin, sinks):
    S = pa.shape[0]
    nb, in_specs = _attn_specs(S)

    def body(sinks_ref, q_ref, kc_ref, kp_ref, vc_ref, vp_ref, cc_ref, sc_ref, cp_ref, sp_ref, o_ref, lse_ref):
        i = pl.program_id(0)
        lane = lax.broadcasted_iota(jnp.int32, (1, HEAD_PAIR), 1)
        cc, sc = cc_ref[...], sc_ref[...]
        _, rope, _, valid, ks, vs = _attn_common(i, kc_ref[...], kp_ref[...], vc_ref[...], vp_ref[...],
                                                 cc, sc, cp_ref[...], sp_ref[...])
        pair_cols = [slice(HEAD_PAIR * pair, HEAD_PAIR * (pair + 1)) for pair in range(4)]
        qps = [_bf(rope(q_ref[:, cols], cc, sc)) for cols in pair_cols]
        outs, lses = {}, {}

        def head_program(h):
            pair, s = divmod(h, 2)
            j = h // 4
            scores = _dot_nt(qps[pair], ks[j, s])
            yield
            p, lse = _attn_probs(scores, valid, sinks_ref[h])
            outs[h] = _dot(_bf(p), vs[j, s])
            lses[h] = jnp.where(lane == h, lse, 0.0)

        _interleave(head_program(h) for h in range(8))
        for pair, cols in enumerate(pair_cols):
            o_ref[:, cols] = outs[2 * pair] + outs[2 * pair + 1]
        lse_ref[...] = sum((lses[h] for h in range(1, 8)), lses[0])

    return pl.pallas_call(
        body, grid=(nb,), name="attn_fwd", in_specs=in_specs,
        out_specs=[pl.BlockSpec((ATTN_BLOCK, 512), lambda i: (i, 0)), pl.BlockSpec((ATTN_BLOCK, 128), lambda i: (i, 0))],
        out_shape=[jax.ShapeDtypeStruct((S, 512), F32), jax.ShapeDtypeStruct((S, 128), F32)],
        compiler_params=_params(("parallel",)),
    )(sinks, pa, pa, pa, pa, pa, cos, sin, cos, sin)


def _attn_bwd(pa, cos, sin, sinks, dcat, attn, lse, after):
    S = pa.shape[0]
    nb, in_specs = _attn_specs(S)
    in_specs = in_specs + [pl.BlockSpec((ATTN_BLOCK, 512), lambda i: (i, 0))] * 2 + [pl.BlockSpec((ATTN_BLOCK, 128), lambda i: (i, 0))]
    in_specs = in_specs + [pl.BlockSpec(memory_space=pl.ANY)]

    def body(sinks_ref, q_ref, kc_ref, kp_ref, vc_ref, vp_ref, cc_ref, sc_ref, cp_ref, sp_ref, do_ref, o_ref, lse_ref, _,
             dq_ref, dk_ref, dv_ref, dsink_ref):
        i = pl.program_id(0)

        @pl.when(i == 0)
        def _():
            dk_ref[...] = jnp.zeros_like(dk_ref)
            dv_ref[...] = jnp.zeros_like(dv_ref)
            dsink_ref[...] = jnp.zeros_like(dsink_ref)

        cc, sc, cp, sp = cc_ref[...], sc_ref[...], cp_ref[...], sp_ref[...]
        slot, rope, unrope, valid, ks, vs = _attn_common(i, kc_ref[...], kp_ref[...], vc_ref[...], vp_ref[...], cc, sc, cp, sp)
        pair_cols = [slice(HEAD_PAIR * pair, HEAD_PAIR * (pair + 1)) for pair in range(4)]
        qps = [_bf(rope(q_ref[:, cols], cc, sc)) for cols in pair_cols]
        dobs = [_bf(do_ref[:, cols]) for cols in pair_cols]
        do_o = [do_ref[:, cols] * o_ref[:, cols] for cols in pair_cols]
        dqs, dks, dvs = {}, {}, {}

        def head_program(h):
            pair, s = divmod(h, 2)
            j = h // 4
            qp, dob = qps[pair], dobs[pair]
            scores = _dot_nt(qp, ks[j, s])
            dp = _dot_nt(dob, vs[j, s])
            yield
            lse_h = lse_ref[:, h:h + 1]
            p = jnp.exp(jnp.where(valid, scores * 0.125, NEG) - lse_h)
            yield
            dr = jnp.sum(jnp.where(slot[s], do_o[pair], 0.0), axis=1, keepdims=True)
            ds = _bf(p * (dp - dr) * 0.125)
            yield
            dsink_ref[h:h + 1, :] += -jnp.sum(jnp.exp(sinks_ref[h] - lse_h) * dr, axis=0, keepdims=True)
            dqs[h] = _dot(ds, ks[j, s])
            dk_h = _dot_tn(ds, qp)
            dv_h = _dot_tn(_bf(p), dob)
            yield
            dk_h, dv_h = jnp.where(slot[s], dk_h, 0.0), jnp.where(slot[s], dv_h, 0.0)
            if s != j:
                dk_h, dv_h = pltpu.roll(dk_h, 64, 1), pltpu.roll(dv_h, 64, 1)
            dks[h], dvs[h] = dk_h, dv_h

        _interleave(head_program(h) for h in range(8))
        dk2 = sum((dks[h] for h in range(1, 8)), dks[0])
        dv2 = sum((dvs[h] for h in range(1, 8)), dvs[0])
        for pair, cols in enumerate(pair_cols):
            dq_ref[:, cols] = _bf(unrope(dqs[2 * pair] + dqs[2 * pair + 1], cc, sc))
        cur = pl.ds(pl.multiple_of(i * ATTN_BLOCK, ATTN_BLOCK), ATTN_BLOCK)
        dk_ref[cur, :] += unrope(dk2[ATTN_BLOCK:], cc, sc)
        dv_ref[cur, :] += dv2[ATTN_BLOCK:]

        @pl.when(i > 0)
        def _():
            prv = pl.ds(pl.multiple_of((i - 1) * ATTN_BLOCK, ATTN_BLOCK), ATTN_BLOCK)
            dk_ref[prv, :] += unrope(dk2[:ATTN_BLOCK], cp, sp)
            dv_ref[prv, :] += dv2[:ATTN_BLOCK]

    whole = lambda w: pl.BlockSpec((S, w), lambda i: (0, 0))
    return pl.pallas_call(
        body, grid=(nb,), name="attn_bwd", in_specs=in_specs,
        out_specs=[pl.BlockSpec((ATTN_BLOCK, 512), lambda i: (i, BLK_Q)), whole(128), whole(128),
                   pl.BlockSpec((8, 128), lambda i: (0, 0))],
        out_shape=[jax.ShapeDtypeStruct((S, D_IN_PAD), BF16), jax.ShapeDtypeStruct((S, 128), F32),
                   jax.ShapeDtypeStruct((S, 128), F32), jax.ShapeDtypeStruct((8, 128), F32)],
        compiler_params=_params(("arbitrary",)),
    )(sinks, pa, pa, pa, pa, pa, cos, sin, cos, sin, dcat, attn, lse, after)


CONV_ROWS = 512
CONV_PAD = 8


def _conv_silu(scr, w, r0):
    y = w[3:4, :] * scr[pl.ds(CONV_PAD + r0, CONV_ROWS), :]
    for j in range(DN_CONV - 1):
        y = y + w[j:j + 1, :] * scr[pl.ds(CONV_PAD + r0 - 3 + j, CONV_ROWS), :]
    return y


def _dn_prep_fwd(pd, conv_w):
    S = pd.shape[0]
    assert S % CONV_ROWS == 0

    def body(x_ref, w_ref, o_ref, scr):
        b = pl.program_id(0)
        scr[0:CONV_PAD, :] = jnp.zeros((CONV_PAD, DN_DIM), F32)
        scr[pl.ds(CONV_PAD, S), :] = x_ref[...]
        w = w_ref[...]
        q_scale = jnp.where(b < DN_HEADS, DN_DIM ** -0.5, 1.0)
        for r0 in range(0, S, CONV_ROWS):
            y = _conv_silu(scr, w, r0)
            a = y * _sigmoid(y)
            rs = lax.rsqrt(jnp.sum(a * a, axis=1, keepdims=True) + EPS)
            o_ref[pl.ds(r0, CONV_ROWS), :] = a * jnp.where(b < 2 * DN_HEADS, rs * q_scale, 1.0)

    col = pl.BlockSpec((S, DN_DIM), lambda b: (0, b))
    return pl.pallas_call(
        body, grid=(3 * DN_HEADS,), name="dn_prep_fwd",
        in_specs=[pl.BlockSpec((S, DN_DIM), lambda b: (0, BLK_DN + b)), pl.BlockSpec((DN_CONV, DN_DIM), lambda b: (0, b))],
        out_specs=col,
        out_shape=jax.ShapeDtypeStruct((S, 3 * DN_HEADS * DN_DIM), F32),
        scratch_shapes=[pltpu.VMEM((S + CONV_PAD, DN_DIM), F32)],
        compiler_params=_params(("parallel",)),
    )(pd, conv_w)


def _dn_prep_bwd(pd, conv_w, dqkv, dproj, dk, dv):
    S = pd.shape[0]
    NB = 3 * DN_HEADS

    def body(x_ref, w_ref, d_ref, _, dk_ref, dv_ref, dx_ref, dw_ref, scr, dscr):
        b = pl.program_id(0)

        @pl.when(b == NB)
        def _():
            dx_ref[...] = _bf(dk_ref[...])

        @pl.when(b == NB + 1)
        def _():
            dx_ref[...] = _bf(dv_ref[...])

        @pl.when(b < NB)
        def _():
            scr[0:CONV_PAD, :] = jnp.zeros((CONV_PAD, DN_DIM), F32)
            scr[pl.ds(CONV_PAD, S), :] = x_ref[...]
            dscr[pl.ds(S, CONV_PAD), :] = jnp.zeros((CONV_PAD, DN_DIM), F32)
            w = w_ref[...]
            q_scale = jnp.where(b < DN_HEADS, DN_DIM ** -0.5, 1.0)
            is_qk = b < 2 * DN_HEADS
            dw = [jnp.zeros((1, DN_DIM), F32) for _ in range(DN_CONV)]
            for r0 in range(0, S, CONV_ROWS):
                y = _conv_silu(scr, w, r0)
                sg = _sigmoid(y)
                a = y * sg
                dout = d_ref[pl.ds(r0, CONV_ROWS), :]
                rs = lax.rsqrt(jnp.sum(a * a, axis=1, keepdims=True) + EPS)
                da_qk = q_scale * rs * (dout - a * (rs * rs) * jnp.sum(dout * a, axis=1, keepdims=True))
                dy = jnp.where(is_qk, da_qk, dout) * (sg * (1.0 + y * (1.0 - sg)))
                dscr[pl.ds(r0, CONV_ROWS), :] = dy
                for j in range(DN_CONV):
                    dw[j] = dw[j] + jnp.sum(dy * scr[pl.ds(CONV_PAD + r0 - 3 + j, CONV_ROWS), :], axis=0, keepdims=True)
            for j in range(DN_CONV):
                dw_ref[j:j + 1, :] = dw[j]
            for r0 in range(0, S, CONV_ROWS):
                dx = w[3:4, :] * dscr[pl.ds(r0, CONV_ROWS), :]
                for j in range(DN_CONV - 1):
                    dx = dx + w[j:j + 1, :] * dscr[pl.ds(r0 + 3 - j, CONV_ROWS), :]
                dx_ref[pl.ds(r0, CONV_ROWS), :] = _bf(dx)

    own = lambda b: jnp.minimum(b, NB - 1)
    col = pl.BlockSpec((S, DN_DIM), lambda b: (0, own(b)))
    proj_col = pl.BlockSpec((S, DN_DIM), lambda b: (0, BLK_DN + own(b)))
    wcol = pl.BlockSpec((DN_CONV, DN_DIM), lambda b: (0, own(b)))
    whole = pl.BlockSpec((S, DN_DIM), lambda b: (0, 0))
    assert BLK_K == BLK_DN + NB and BLK_V == BLK_K + 1
    return pl.pallas_call(
        body, grid=(NB + 2,), name="dn_prep_bwd",
        in_specs=[proj_col, wcol, col, pl.BlockSpec(memory_space=pl.ANY), whole, whole],
        out_specs=[pl.BlockSpec((S, DN_DIM), lambda b: (0, BLK_DN + b)), wcol],
        out_shape=[jax.ShapeDtypeStruct(dproj.shape, dproj.dtype), jax.ShapeDtypeStruct((DN_CONV, 3 * DN_HEADS * DN_DIM), F32)],
        scratch_shapes=[pltpu.VMEM((S + CONV_PAD, DN_DIM), F32), pltpu.VMEM((S + CONV_PAD, DN_DIM), F32)],
        input_output_aliases={3: 0},
        compiler_params=_params(("arbitrary",)),
    )(pd, conv_w, dqkv, dproj, dk, dv)


CPAD = 128
CHUNKS_LOCAL = 8
CHUNKS_SCAN = 8


def _chunk_masks():
    ii = lax.broadcasted_iota(jnp.int32, (DN_CHUNK, CPAD), 0)
    jj = lax.broadcasted_iota(jnp.int32, (DN_CHUNK, CPAD), 1)
    return ii, jj


def _rows_pad(a):
    return jnp.concatenate([a, jnp.zeros_like(a)], axis=0)


def _hi_lo(a):
    hi = _bf(a)
    return hi, _bf(a - hi.astype(F32))


def _double_step(t, p):
    C = DN_CHUNK
    th, tl = _hi_lo(t)
    ph, pl_ = _hi_lo(p)
    r1 = _dot(jnp.concatenate([th, tl, ph, pl_], axis=0), _rows_pad(ph))
    r2 = _dot(jnp.concatenate([th, ph], axis=0), _rows_pad(pl_))
    return t + (r1[:C] + r1[C:2 * C] + r2[:C]), r1[2 * C:3 * C] + r1[3 * C:] + r2[C:]


def _dot3_nt(a, b):
    C = DN_CHUNK
    ah, al = _hi_lo(a)
    bh, bl = _hi_lo(b)
    r1 = _dot_nt(jnp.concatenate([ah, al], axis=0), _rows_pad(bh))
    return r1[:C] + r1[C:] + _dot_nt(ah, _rows_pad(bl))


def _dot3_tn(a, b):
    C = DN_CHUNK
    ah, al = _hi_lo(a)
    bh, bl = _hi_lo(b)
    return _dot_tn(jnp.concatenate([ah, al, ah], axis=0), jnp.concatenate([bh, bh, bl], axis=0))[:C]


def _interleave(programs):
    programs = list(programs)
    while programs:
        alive = []
        for prog in programs:
            try:
                next(prog)
                alive.append(prog)
            except StopIteration:
                pass
        programs = alive


def _col_to_row(col, ii, jj):
    return jnp.sum(jnp.where(ii == jj, col, 0.0), axis=0, keepdims=True)


def _row_to_col(row, ii, jj):
    return jnp.sum(jnp.where(ii == jj, row, 0.0), axis=1, keepdims=True)


def _decay(gc_col, ii, jj):
    diff = gc_col - _col_to_row(gc_col, ii, jj)
    return jnp.where(jj <= ii, jnp.exp(jnp.where(jj <= ii, diff, 0.0)), 0.0)


def _softplus(x):
    return jnp.maximum(x, 0.0) + jnp.log(1.0 + jnp.exp(-jnp.abs(x)))


def _head(h):
    return slice(DN_DIM * h, DN_DIM * (h + 1))


def _dn_chunk_fwd(qkv, pg, a_log, dt_bias):
    S = qkv.shape[0]
    C = DN_CHUNK
    G = CHUNKS_LOCAL
    R = G * C
    steps = S // R

    def body(alog_ref, dtb_ref, qkv_ref, pg_ref, w_ref, u_ref, qg_ref, kd_ref, a_ref, t_ref, gcs_ref):
        ii, jj = _chunk_masks()
        lane = lax.broadcasted_iota(jnp.int32, (1, 128), 1)
        eye = (ii == jj).astype(F32)
        gcs_parts = [[] for _ in range(G)]

        def head_program(chunk, h):
            rows = slice(chunk * C, (chunk + 1) * C)
            q, k, v = qkv_ref[rows, _head(h)], qkv_ref[rows, _head(DN_HEADS + h)], qkv_ref[rows, _head(2 * DN_HEADS + h)]
            beta = _sigmoid(pg_ref[rows, h:h + 1])
            g_col = -jnp.exp(alog_ref[h]) * _softplus(pg_ref[rows, DN_HEADS + h:DN_HEADS + h + 1] + dtb_ref[h])
            g_row = _col_to_row(g_col, ii, jj)
            gc_col = jnp.sum(jnp.where(jj <= ii, g_row, 0.0), axis=1, keepdims=True)
            dec = _decay(gc_col, ii, jj)
            eg = jnp.exp(gc_col)
            kb, vb = k * beta, v * beta
            k_rows = _rows_pad(_bf(k))
            kk = _dot_nt(_bf(kb), k_rows)
            qk = _dot_nt(_bf(q), k_rows)
            yield
            t, pw = eye, -jnp.where(jj < ii, kk * dec, 0.0)
            for _ in range(6):
                t, pw = _double_step(t, pw)
                yield
            tb = _bf(t)
            u_ref[rows, _head(h)] = _dot(tb, _rows_pad(_bf(vb)))
            w_ref[rows, _head(h)] = _bf(_dot(tb, _rows_pad(_bf(kb * eg))))
            a_ref[h, rows] = _bf(qk * dec)
            t_ref[h, rows] = t
            qg_ref[rows, _head(h)] = _bf(q * eg)
            kd_ref[rows, _head(h)] = _bf(k * jnp.exp(gc_col[C - 1:C, :] - gc_col))
            gcs_parts[chunk].append(jnp.where(lane == h, gc_col, 0.0) + jnp.where(lane == DN_HEADS + h, beta, 0.0)
                                    + jnp.where(lane == 2 * DN_HEADS + h, g_col, 0.0))

        _interleave(head_program(chunk, h) for chunk in range(G) for h in range(DN_HEADS))
        for chunk in range(G):
            gcs_ref[chunk * C:(chunk + 1) * C, :] = sum(gcs_parts[chunk][1:], gcs_parts[chunk][0])

    smem = pl.BlockSpec(memory_space=pltpu.SMEM)
    wide = pl.BlockSpec((R, 512), lambda n: (n, 0))
    sq = pl.BlockSpec((DN_HEADS, R, CPAD), lambda n: (0, n, 0))
    narrow = pl.BlockSpec((R, 128), lambda n: (n, 0))
    f = lambda *shp: jax.ShapeDtypeStruct(shp, F32)
    b = lambda *shp: jax.ShapeDtypeStruct(shp, BF16)
    return pl.pallas_call(
        body, grid=(steps,), name="dn_chunk_fwd",
        in_specs=[smem, smem, pl.BlockSpec((R, 1536), lambda n: (n, 0)), pl.BlockSpec((R, 128), lambda n: (n, BLK_G))],
        out_specs=[wide, wide, wide, wide, sq, sq, narrow],
        out_shape=[b(S, 512), f(S, 512), b(S, 512), b(S, 512), b(DN_HEADS, S, CPAD), f(DN_HEADS, S, CPAD), f(S, 128)],
        compiler_params=_params(("parallel",)),
    )(a_log, dt_bias, qkv, pg)


def _gated_norm(o, z, gn):
    r, oh = _rms_stats(o)
    return oh * gn * (z * _sigmoid(z))


def _dn_scan_fwd(w, u, qg, kd, a, gcs, pz, gn):
    S = w.shape[0]
    C = DN_CHUNK
    nc = S // C
    G = CHUNKS_SCAN
    R = G * C

    def body(w_ref, u_ref, qg_ref, kd_ref, a_ref, gcs_ref, z_ref, gn_ref, o_ref, vn_ref, sst_ref, out_ref, state):
        @pl.when(pl.program_id(0) == 0)
        def _():
            state[...] = jnp.zeros_like(state)

        def head_program(chunk, h):
            hs = _head(h)
            rows = slice(chunk * C, (chunk + 1) * C)
            s_in = state[h]
            sb = _bf(s_in)
            sst_ref[chunk, h] = sb
            w_s = _dot(w_ref[rows, hs], sb)
            q_s = _dot(qg_ref[rows, hs], sb)
            yield
            vn = u_ref[rows, hs] - w_s
            vnb = _bf(vn)
            o = q_s + _dot(a_ref[h, rows], _rows_pad(vnb))
            k_v = _dot_tn(kd_ref[rows, hs], vnb)
            yield
            state[h] = s_in * jnp.exp(gcs_ref[(chunk + 1) * C - 1:(chunk + 1) * C, h:h + 1]) + k_v
            o_ref[rows, hs] = o
            vn_ref[rows, hs] = vnb
            out_ref[rows, hs] = _bf(_gated_norm(o, z_ref[rows, hs], gn_ref[...]))

        for chunk in range(G):
            _interleave(head_program(chunk, h) for h in range(DN_HEADS))

    wide = pl.BlockSpec((R, 512), lambda n: (n, 0))
    f = lambda *shp: jax.ShapeDtypeStruct(shp, F32)
    b = lambda *shp: jax.ShapeDtypeStruct(shp, BF16)
    return pl.pallas_call(
        body, grid=(nc // G,), name="dn_scan_fwd",
        in_specs=[wide, wide, wide, wide, pl.BlockSpec((DN_HEADS, R, CPAD), lambda n: (0, n, 0)),
                  pl.BlockSpec((R, 128), lambda n: (n, 0)), pl.BlockSpec((R, 512), lambda n: (n, BLK_Z)),
                  pl.BlockSpec((1, DN_DIM), lambda n: (0, 0))],
        out_specs=[wide, wide, pl.BlockSpec((G, DN_HEADS, DN_DIM, DN_DIM), lambda n: (n, 0, 0, 0)), wide],
        out_shape=[f(S, 512), b(S, 512), b(nc, DN_HEADS, DN_DIM, DN_DIM), b(S, 512)],
        scratch_shapes=[pltpu.VMEM((DN_HEADS, DN_DIM, DN_DIM), F32)],
        compiler_params=_params(("arbitrary",)),
    )(w, u, qg, kd, a, gcs, pz, gn)


def _dn_scan_bwd(dcat, o, pz, gn, sst, vnew, w, qg, kd, a, gcs, dproj):
    S = o.shape[0]
    C = DN_CHUNK
    G = CHUNKS_SCAN
    R = G * C
    steps = S // R

    def body(dy_ref, o_ref, z_ref, gn_ref, sst_ref, vn_ref, w_ref, qg_ref, kd_ref, a_ref, gcs_ref, _,
             du_ref, dw_ref, dqg_ref, dkd_ref, da_ref, dz_ref, dsc_ref, dgn_ref, dstate):
        @pl.when(pl.program_id(0) == 0)
        def _():
            dstate[...] = jnp.zeros_like(dstate)
            dgn_ref[...] = jnp.zeros_like(dgn_ref)

        gn_ = gn_ref[...]
        lane = lax.broadcasted_iota(jnp.int32, (C, 128), 1)
        row = lax.broadcasted_iota(jnp.int32, (C, 128), 0)
        dgn_parts = []

        def head_program(chunk, h, dsc_parts):
            hs = _head(h)
            rows = slice(chunk * C, (chunk + 1) * C)
            ov, z, dout = o_ref[rows, hs], z_ref[rows, hs], dy_ref[rows, hs]
            r, oh = _rms_stats(ov)
            sg = _sigmoid(z)
            don = dout * (z * sg)
            dz_ref[rows, hs] = _bf(dout * (oh * gn_) * (sg * (1.0 + z * (1.0 - sg))))
            dgn_parts.append(jnp.sum(don * oh, axis=0, keepdims=True))
            dn = don * gn_
            do = _bf(r * (dn - oh * jnp.mean(dn * oh, axis=-1, keepdims=True)))
            sb = sst_ref[chunk, h]
            s_in = sb.astype(F32)
            ds_out = dstate[h]
            dsb = _bf(ds_out)
            vnb = vn_ref[rows, hs]
            wb, qgb, kdb, ab = w_ref[rows, hs], qg_ref[rows, hs], kd_ref[rows, hs], a_ref[h, rows]
            dvn = _dot_tn(ab, do)[:C] + _dot(kdb, dsb)
            yield
            da_ref[h, rows] = _dot_nt(do, _rows_pad(vnb))
            dqg_ref[rows, hs] = _dot_nt(do, sb)
            dkd_ref[rows, hs] = _dot_nt(vnb, dsb)
            q_do = _dot_tn(qgb, do)
            yield
            dvnb = _bf(dvn)
            dw_ref[rows, hs] = _bf(-_dot_nt(dvnb, sb))
            w_dvn = _dot_tn(wb, dvnb)
            du_ref[rows, hs] = dvnb
            yield
            d_last = jnp.exp(gcs_ref[(chunk + 1) * C - 1:(chunk + 1) * C, h:h + 1])
            dd = jnp.sum(jnp.sum(ds_out * s_in, axis=1, keepdims=True), axis=0, keepdims=True)
            dsc_parts.append(jnp.where((lane == h) & (row == C - 1), dd * d_last, 0.0))
            dstate[h] = ds_out * d_last + q_do - w_dvn

        for chunk in reversed(range(G)):
            dsc_parts = []
            _interleave(head_program(chunk, h, dsc_parts) for h in range(DN_HEADS))
            dsc_ref[chunk * C:(chunk + 1) * C, :] = sum(dsc_parts[1:], dsc_parts[0])
        dgn_ref[...] += sum(dgn_parts[1:], dgn_parts[0])

    rev = lambda n: steps - 1 - n
    wide = pl.BlockSpec((R, 512), lambda n: (rev(n), 0))
    z_spec = pl.BlockSpec((R, 512), lambda n: (rev(n), BLK_Z))
    sq = pl.BlockSpec((DN_HEADS, R, CPAD), lambda n: (0, rev(n), 0))
    narrow = pl.BlockSpec((R, 128), lambda n: (rev(n), 0))
    gn_spec = pl.BlockSpec((1, DN_DIM), lambda n: (0, 0))
    f = lambda *shp: jax.ShapeDtypeStruct(shp, F32)
    b = lambda *shp: jax.ShapeDtypeStruct(shp, BF16)
    return pl.pallas_call(
        body, grid=(steps,), name="dn_scan_bwd",
        in_specs=[pl.BlockSpec((R, 512), lambda n: (rev(n), 1)), wide, z_spec, gn_spec,
                  pl.BlockSpec((G, DN_HEADS, DN_DIM, DN_DIM), lambda n: (rev(n), 0, 0, 0)),
                  wide, wide, wide, wide, sq, narrow, pl.BlockSpec(memory_space=pl.ANY)],
        out_specs=[wide, wide, wide, wide, sq, z_spec, narrow, gn_spec],
        out_shape=[b(S, 512), b(S, 512), f(S, 512), f(S, 512), f(DN_HEADS, S, CPAD),
                   jax.ShapeDtypeStruct(dproj.shape, dproj.dtype), f(S, 128), f(1, DN_DIM)],
        scratch_shapes=[pltpu.VMEM((DN_HEADS, DN_DIM, DN_DIM), F32)],
        input_output_aliases={11: 5},
        compiler_params=_params(("arbitrary",)),
    )(dcat, o, pz, gn, sst, vnew, w, qg, kd, a, gcs, dproj)


def _dn_chunk_bwd(qkv, pg, t_inv, gcs, du, dw, dqg, dkd, da, dsc, a_log, dt_bias, dproj):
    S = qkv.shape[0]
    C = DN_CHUNK
    G = CHUNKS_LOCAL
    R = G * C

    def body(alog_ref, dtb_ref, qkv_ref, pg_ref, t_ref, gcs_ref, du_ref, dw_ref, dqg_ref, dkd_ref, da_ref, dsc_ref, _,
             dqkv_ref, dpg_ref, acc_ref):
        @pl.when(pl.program_id(0) == 0)
        def _():
            acc_ref[...] = jnp.zeros_like(acc_ref)

        ii, jj = _chunk_masks()
        lane = lax.broadcasted_iota(jnp.int32, (1, 128), 1)
        row8 = lax.broadcasted_iota(jnp.int32, (8, 128), 0)
        lane8 = lax.broadcasted_iota(jnp.int32, (8, 128), 1)
        rowc = lax.broadcasted_iota(jnp.int32, (C, 1), 0)
        tril, strict = jj <= ii, jj < ii
        dpg_parts, acc_parts = [[] for _ in range(G)], []

        def head_program(chunk, h):
            rows = slice(chunk * C, (chunk + 1) * C)
            q, k, v = qkv_ref[rows, _head(h)], qkv_ref[rows, _head(DN_HEADS + h)], qkv_ref[rows, _head(2 * DN_HEADS + h)]
            gc_col, beta, g_col = gcs_ref[rows, h:h + 1], gcs_ref[rows, DN_HEADS + h:DN_HEADS + h + 1], \
                gcs_ref[rows, 2 * DN_HEADS + h:2 * DN_HEADS + h + 1]
            dec = _decay(gc_col, ii, jj)
            eg = jnp.exp(gc_col)
            g_last = gc_col[C - 1:C, :]
            ek = jnp.exp(g_last - gc_col)
            kb, vb = k * beta, v * beta
            kbg = kb * eg
            qb, kbb = _bf(q), _bf(kb)
            k_rows = _rows_pad(_bf(k))
            t = t_ref[h, rows]
            tb = _bf(t)
            dub, dwb = du_ref[rows, _head(h)], dw_ref[rows, _head(h)]
            dqg_, dkd_ = dqg_ref[rows, _head(h)], dkd_ref[rows, _head(h)]
            dt = _dot_nt(dub, _rows_pad(_bf(vb))) + _dot_nt(dwb, _rows_pad(_bf(kbg)))
            t_du_dw = _dot_tn(tb, jnp.concatenate([dub, dwb], axis=1))
            dvb, dkbg = t_du_dw[:C, :DN_DIM], t_du_dw[:C, DN_DIM:]
            kk = _dot_nt(kbb, k_rows)
            qk = _dot_nt(qb, k_rows)
            yield
            dt_t = _dot3_nt(dt, t)
            yield
            dl = -_dot3_tn(t, dt_t)
            yield
            dm = jnp.where(strict, dl * dec, 0.0)
            dqk = jnp.where(tril, da_ref[h, rows] * dec, 0.0)
            gmat = dm * kk + dqk * qk
            dgc = jnp.sum(gmat, axis=1, keepdims=True) - _row_to_col(jnp.sum(gmat, axis=0, keepdims=True), ii, jj)
            dmb, dqkb = _bf(dm), _bf(dqk)
            yield
            dkb = _dot(dmb, k_rows) + dkbg * eg
            dk = _dot_tn(jnp.concatenate([dmb, dqkb], axis=0), jnp.concatenate([kbb, qb], axis=0))[:C] + dkd_ * ek
            dq = _dot(dqkb, k_rows) + dqg_ * eg
            yield
            tk = jnp.sum(dkd_ * k * ek, axis=1, keepdims=True)
            dgc = dgc + jnp.sum(dqg_ * q * eg, axis=1, keepdims=True) - tk + jnp.sum(dkbg * kbg, axis=1, keepdims=True)
            dgl = jnp.sum(tk, axis=0, keepdims=True) + dsc_ref[(chunk + 1) * C - 1:(chunk + 1) * C, h:h + 1]
            dgc = dgc + jnp.where(rowc == C - 1, dgl, 0.0)
            yield
            dk = dk + dkb * beta
            dbeta = jnp.sum(dkb * k, axis=1, keepdims=True) + jnp.sum(dvb * v, axis=1, keepdims=True)
            dqkv_ref[rows, _head(h)] = dq
            dqkv_ref[rows, _head(DN_HEADS + h)] = dk
            dqkv_ref[rows, _head(2 * DN_HEADS + h)] = dvb * beta
            dg_col = jnp.sum(jnp.where(jj >= ii, _col_to_row(dgc, ii, jj), 0.0), axis=1, keepdims=True)
            yield
            db = dbeta * beta * (1.0 - beta)
            da_in = dg_col * (-jnp.exp(alog_ref[h])) * _sigmoid(pg_ref[rows, DN_HEADS + h:DN_HEADS + h + 1] + dtb_ref[h])
            dpg_parts[chunk].append(jnp.where(lane == h, db, 0.0) + jnp.where(lane == DN_HEADS + h, da_in, 0.0))
            acc_parts.append(jnp.where((row8 == 0) & (lane8 == h), jnp.sum(dg_col * g_col, axis=0, keepdims=True), 0.0)
                             + jnp.where((row8 == 1) & (lane8 == h), jnp.sum(da_in, axis=0, keepdims=True), 0.0))

        _interleave(head_program(chunk, h) for chunk in range(G) for h in range(DN_HEADS))
        for chunk in range(G):
            dpg = sum(dpg_parts[chunk][1:], dpg_parts[chunk][0])
            dpg_ref[chunk * C:(chunk + 1) * C, :] = _bf(jnp.concatenate([dpg, jnp.zeros_like(dpg)], axis=1))
        acc_ref[...] += sum(acc_parts[1:], acc_parts[0])

    smem = pl.BlockSpec(memory_space=pltpu.SMEM)
    wide = pl.BlockSpec((R, 512), lambda n: (n, 0))
    sq = pl.BlockSpec((DN_HEADS, R, CPAD), lambda n: (0, n, 0))
    narrow = pl.BlockSpec((R, 128), lambda n: (n, 0))
    qkv_spec = pl.BlockSpec((R, 1536), lambda n: (n, 0))
    f = lambda *shp: jax.ShapeDtypeStruct(shp, F32)
    return pl.pallas_call(
        body, grid=(S // R,), name="dn_chunk_bwd",
        in_specs=[smem, smem, qkv_spec, pl.BlockSpec((R, 128), lambda n: (n, BLK_G)), sq, narrow, wide, wide, wide, wide, sq,
                  narrow, pl.BlockSpec(memory_space=pl.ANY)],
        out_specs=[qkv_spec, pl.BlockSpec((R, 256), lambda n: (n, BLK_G_PAD)), pl.BlockSpec((8, 128), lambda n: (0, 0))],
        out_shape=[f(S, 1536), jax.ShapeDtypeStruct(dproj.shape, dproj.dtype), f(8, 128)],
        input_output_aliases={12: 1},
        compiler_params=_params(("arbitrary",)),
    )(a_log, dt_bias, qkv, pg, t_inv, gcs, du, dw, dqg, dkd, da, dsc, dproj)


_W_IN_SECTIONS = ((0, 0, 512), (2304, 512, 512), (768, 1024, 1536), (512, 2560, 256), (2816, 2816, 8))
_HALF = D_MODEL // 2
_SECTION_ROWS = 256


def _pack_pairs(x):
    bits = lax.bitcast_convert_type(x, jnp.uint32)
    return lax.bitcast_convert_type(bits[:, _HALF:] | (bits[:, :_HALF] >> 16), F32)


def _unpack_pairs(words):
    bits = lax.bitcast_convert_type(words, jnp.uint32)
    return (lax.bitcast_convert_type(bits << 16, F32),
            lax.bitcast_convert_type(bits & jnp.uint32(0xFFFF0000), F32))


def _w_in_to_internal(packed):
    starts = [dst for _, dst, _ in _W_IN_SECTIONS] + [D_IN_PAD]

    def body(x_hbm, o_ref, words, sems):
        copies = [pltpu.make_async_copy(x_hbm.at[pl.ds(src, rows), 0, :], words.at[pl.ds(dst, rows)], sems.at[i])
                  for i, (src, dst, rows) in enumerate(_W_IN_SECTIONS)]
        for cp in copies:
            cp.start()
        words[pl.ds(D_IN, D_IN_PAD - D_IN), :] = jnp.zeros((D_IN_PAD - D_IN, _HALF), F32)
        for i, cp in enumerate(copies):
            cp.wait()
            for r in range(starts[i], starts[i + 1], _SECTION_ROWS):
                for c, h in enumerate(_unpack_pairs(words[pl.ds(r, _SECTION_ROWS), :])):
                    o_ref[pl.ds(r, _SECTION_ROWS), c * _HALF:(c + 1) * _HALF] = _bf(h)

    assert all((b - a) % _SECTION_ROWS == 0 for a, b in zip(starts, starts[1:])) and starts[-2] + _W_IN_SECTIONS[-1][2] == D_IN
    return pl.pallas_call(body, name="w_in_to_internal", out_shape=jax.ShapeDtypeStruct((D_IN_PAD, D_MODEL), BF16),
                          in_specs=[pl.BlockSpec(memory_space=pl.ANY)],
                          scratch_shapes=[pltpu.VMEM((D_IN_PAD, _HALF), F32), pltpu.SemaphoreType.DMA((len(_W_IN_SECTIONS),))],
                          compiler_params=pltpu.CompilerParams(vmem_limit_bytes=VMEM_LIMIT))(packed)


def _w_in_from_internal(gt):
    def body(g_ref, o_hbm, words, sems):
        copies = []
        for i, (dst, src, rows) in enumerate(_W_IN_SECTIONS):
            for r in range(src, src + rows, _SECTION_ROWS):
                n = max(min(_SECTION_ROWS, src + rows - r), 16)
                words[pl.ds(r, n), :] = _pack_pairs(g_ref[pl.ds(r, n), :].astype(F32))
            copies.append(pltpu.make_async_copy(words.at[pl.ds(src, rows)], o_hbm.at[pl.ds(dst, rows), 0, :], sems.at[i]))
            copies[-1].start()
        for cp in copies:
            cp.wait()

    return pl.pallas_call(body, name="w_in_from_internal", out_shape=jax.ShapeDtypeStruct((D_IN, 1, _HALF), F32),
                          out_specs=pl.BlockSpec(memory_space=pl.ANY),
                          scratch_shapes=[pltpu.VMEM((D_IN_PAD, _HALF), F32), pltpu.SemaphoreType.DMA((len(_W_IN_SECTIONS),))],
                          compiler_params=pltpu.CompilerParams(vmem_limit_bytes=VMEM_LIMIT))(gt)


def _local_step(x, p, target, wts, first_weights, other_weights, ship_early, after):
    S = x.shape[0]
    cos, sin = _rope_tables(S)
    sinks, a_log, dt_bias = wts["sinks"].reshape(8), wts["a_log"].reshape(4), wts["dt_bias"].reshape(4)
    gn = wts["dn_norm"].reshape(1, DN_DIM)
    add = lambda acc, res: (acc + res,)

    u = _rmsnorm_fwd(x, wts["norm_mix"], "norm_mix_fwd", after)
    w_in_t, conv_w = first_weights(u)
    proj, = _mm(u, w_in_t, form="nt", name="in_proj", out_dtypes=[F32], tn=512)
    attn, lse = _attn_fwd(proj, cos, sin, sinks)
    qkv = _dn_prep_fwd(proj, conv_w)
    cw, cu, cqg, ckd, ca, ct, gcs = _dn_chunk_fwd(qkv, proj, a_log, dt_bias)
    o, vnew, sst, dn_out = _dn_scan_fwd(cw, cu, cqg, ckd, ca, gcs, proj, gn)
    w_o, = other_weights(("w_o",), dn_out)
    h1, = _mm([attn, dn_out], w_o, form="nn", name="out_proj", out_dtypes=[F32], tn=512, epi=add, extra=[x])

    w_up, w_down = other_weights(("w_up", "w_down"), h1)
    hid, relu, m, h2 = _mlp_fwd(h1, w_up, w_down, wts["norm_mlp"])
    w_pg, w_pp = other_weights(("w_ple_gate", "w_ple_proj"), h2)
    n3, dh2, dgl, dpp, loss, d_norm_final, d_norm_ple = _ple_and_loss(h2, p, target, w_pg, w_pp, wts["norm_ple"],
                                                                     wts["norm_final"].reshape(1, D_MODEL))
    g = {"norm_final": d_norm_final, "norm_ple": d_norm_ple}
    early = {"w_ple_gate": _mm_tn(n3, dgl, name="d_w_ple_gate", tm=512, tn=1024, out_dtype=BF16).reshape(N_DEV, 128, 1024),
             "w_ple_proj": _mm_tn(p, dpp, name="d_w_ple_proj", tm=256, tn=128, out_dtype=BF16, column_shards=True)}
    d_act, = _mm(dh2, w_down, form="nt", name="d_hidden", out_dtypes=[BF16], tn=512,
                 epi=lambda acc, r: (acc * (2.0 * r.astype(F32)),), extra=[relu])
    early["w_down"] = _mm_tn(hid, dh2, name="d_w_down", tm=512, tn=1024, out_dtype=BF16).reshape(N_DEV, 512, 1024)
    early["w_up"] = _mm_tn(m, d_act, name="d_w_up", tm=1024, tn=512, out_dtype=BF16, column_shards=True)
    token = ship_early(early)
    dh1, g["norm_mlp"], dcat = _mm(d_act, w_up, form="nt", name="d_m", out_dtypes=[F32], tn=512, after=token,
                                   norm_bwd=(h1, wts["norm_mlp"], dh2), then_nt=w_o)
    d_w_o = _mm_tn([attn, dn_out], dh1, name="d_w_o", tm=512, tn=512, out_dtype=BF16)
    token = ship_early({"w_o": d_w_o.reshape(N_DEV, 128, 1024)})
    dproj, dk, dv, dsinks = _attn_bwd(proj, cos, sin, sinks, dcat, attn, lse, token)
    g["sinks"] = dsinks[:, 0].reshape(1, 8)
    du_, dw_, dqg, dkd, da, dproj, dsc, g["dn_norm"] = _dn_scan_bwd(dcat, o, proj, gn, sst, vnew, cw, cqg, ckd, ca, gcs, dproj)
    dqkv, dproj, gate_acc = _dn_chunk_bwd(qkv, proj, ct, gcs, du_, dw_, dqg, dkd, da, dsc, a_log, dt_bias, dproj)
    g["a_log"], g["dt_bias"] = gate_acc[0:1, 0:4], gate_acc[1:2, 0:4]
    dproj, g["conv_w"] = _dn_prep_bwd(proj, conv_w, dqkv, dproj, dk, dv)
    token = ship_early({"w_in": _mm_tn(dproj, u, name="d_w_in", tm=512, tn=1024, out_dtype=BF16)})
    grad_x, g["norm_mix"] = _mm(dproj, w_in_t, form="nn", name="d_u", out_dtypes=[F32], tn=512, after=token,
                                norm_bwd=(x, wts["norm_mix"], dh1))
    return loss, grad_x, g


def _peer(k):
    x, y, c = lax.axis_index("x"), lax.axis_index("y"), lax.axis_index("c")
    px = 1 - x if k & 4 else x
    py = 1 - y if k & 2 else y
    pc = 1 - c if k & 1 else c
    return (px, py, pc), 4 * px + 2 * py + pc


def _exchange(srcs, name, gather):
    n = len(srcs)
    gathers = list(gather) if isinstance(gather, (list, tuple)) else [gather] * n
    shapes = [(N_DEV,) + s.shape if gt else s.shape for s, gt in zip(srcs, gathers)]

    def body(*refs):
        src_refs, out_refs = refs[:n], refs[n:2 * n]
        send_sems, recv_sems, local_sems = refs[2 * n:]
        _, me = _peer(0)
        piece = lambda a, d: src_refs[a] if gathers[a] else src_refs[a].at[d]
        local = [pltpu.make_async_copy(piece(a, me), out_refs[a].at[me], local_sems.at[a]) for a in range(n)]
        for cp in local:
            cp.start()
        copies = []
        for a in range(n):
            for k in range(1, N_DEV):
                dev, idx = _peer(k)
                cp = pltpu.make_async_remote_copy(src_ref=piece(a, idx), dst_ref=out_refs[a].at[me],
                                                  send_sem=send_sems.at[a, k - 1], recv_sem=recv_sems.at[a, k - 1],
                                                  device_id=dev, device_id_type=MESH)
                cp.start()
                copies.append(cp)
        for cp in copies:
            cp.wait_recv()
        for cp in copies:
            cp.wait_send()
        for cp in local:
            cp.wait()

    anywhere = pl.BlockSpec(memory_space=pl.ANY)
    return pl.pallas_call(
        body, name=name, in_specs=[anywhere] * n, out_specs=[anywhere] * n,
        out_shape=[jax.ShapeDtypeStruct(shp, s.dtype) for shp, s in zip(shapes, srcs)],
        scratch_shapes=[pltpu.SemaphoreType.DMA((n, N_DEV - 1)), pltpu.SemaphoreType.DMA((n, N_DEV - 1)),
                        pltpu.SemaphoreType.DMA((n,))],
    )(*srcs)


_HBM = pl.BlockSpec(memory_space=pltpu.HBM)
_SEM = pl.BlockSpec(memory_space=pltpu.SEMAPHORE)
_EFFECT = pltpu.SideEffectType.DATAFLOW_SIDE_EFFECTING


def _split_copies(src_refs, land_refs, send_sems, recv_sems, modes, which=None):
    _, me = _peer(0)
    local, remote = [], []
    which = range(len(src_refs)) if which is None else which
    for a, src, land in zip(which, src_refs, land_refs):
        if modes[a] == "columns":
            n_cols = src.shape[1]
            dst = land.at[:, pl.ds(pl.multiple_of(me * n_cols, n_cols), n_cols)]
        else:
            dst = land.at[me]
        part = lambda d: src.at[d] if modes[a] == "pieces" else src
        local.append(pltpu.make_async_copy(part(me), dst, recv_sems.at[a * N_DEV]))
        for k in ((2, 4, 6) if modes[a] == "chips" else range(1, N_DEV)):
            dev, idx = _peer(k)
            sem = a * N_DEV + k
            remote.append(pltpu.make_async_remote_copy(
                src_ref=part(idx), dst_ref=dst, send_sem=send_sems.at[sem], recv_sem=recv_sems.at[sem],
                device_id=dev, device_id_type=MESH))
    return local, remote


def _forward_copies(land_refs, send_sems, recv_sems):
    c = lax.axis_index("c")
    sibling, _ = _peer(1)
    copies = []
    for a, land in enumerate(land_refs):
        for chip in range(N_DEV // 2):
            slot = 2 * chip + c
            sem = a * (N_DEV // 2) + chip
            copies.append(pltpu.make_async_remote_copy(
                src_ref=land.at[slot], dst_ref=land.at[slot], send_sem=send_sems.at[sem], recv_sem=recv_sems.at[sem],
                device_id=sibling, device_id_type=MESH))
    return copies


def _forward_start(lands, name):
    n = len(lands)

    def body(*refs):
        for cp in _forward_copies(refs[:n], refs[n], refs[n + 1]):
            cp.start()
        refs[-1][...] = jnp.zeros_like(refs[-1])

    sems = pltpu.SemaphoreType.DMA((n * (N_DEV // 2),))
    out = pl.pallas_call(
        body, name=name,
        out_shape=(sems, sems, *[pltpu.HBM(t.shape, t.dtype) for t in lands], jax.ShapeDtypeStruct((8, 128), F32)),
        in_specs=[_HBM] * n, out_specs=(_SEM, _SEM, *[_HBM] * n, pl.BlockSpec(memory_space=pltpu.VMEM)),
        input_output_aliases={i: 2 + i for i in range(n)},
        compiler_params=pltpu.CompilerParams(has_side_effects=_EFFECT),
    )(*[pltpu.with_memory_space_constraint(t, pltpu.HBM) for t in lands])
    return out[:-1], out[-1]


def _forward_wait(handle, after, name):
    send_sems, recv_sems, *lands = handle
    n = len(lands)

    def body(*refs):
        for cp in _forward_copies(refs[:n], refs[n], refs[n + 1]):
            cp.wait_send()
            cp.wait_recv()

    return list(pl.pallas_call(
        body, name=name, out_shape=tuple(pltpu.HBM(t.shape, t.dtype) for t in lands),
        in_specs=[_HBM] * n + [_SEM, _SEM, pl.BlockSpec(memory_space=pl.ANY)], out_specs=tuple([_HBM] * n),
        input_output_aliases={i: i for i in range(n)},
        compiler_params=pltpu.CompilerParams(has_side_effects=_EFFECT),
    )(*lands, send_sems, recv_sems, after))


def _exchange_start(srcs, name, modes):
    n = len(srcs)
    modes = [modes] * n if isinstance(modes, str) else list(modes)
    lands = []
    for s, mode in zip(srcs, modes):
        shape = {"columns": (s.shape[0], N_DEV * s.shape[1]), "pieces": s.shape}.get(mode, (N_DEV,) + s.shape)
        lands.append(lax.empty(shape, s.dtype))

    def body(*refs):
        src_refs, land_refs = refs[:n], refs[n:2 * n]
        send_sems, recv_sems = refs[2 * n], refs[2 * n + 1]
        local, remote = _split_copies(src_refs, land_refs, send_sems, recv_sems, modes)
        for cp in local + remote:
            cp.start()
        refs[-1][...] = jnp.zeros_like(refs[-1])

    both = list(srcs) + lands
    sems = pltpu.SemaphoreType.DMA((n * N_DEV,))
    out = pl.pallas_call(
        body, name=name,
        out_shape=(sems, sems, *[pltpu.HBM(t.shape, t.dtype) for t in both], jax.ShapeDtypeStruct((8, 128), F32)),
        in_specs=[_HBM] * (2 * n), out_specs=(_SEM, _SEM, *[_HBM] * (2 * n), pl.BlockSpec(memory_space=pltpu.VMEM)),
        input_output_aliases={i: 2 + i for i in range(2 * n)},
        compiler_params=pltpu.CompilerParams(has_side_effects=_EFFECT),
    )(*[pltpu.with_memory_space_constraint(t, pltpu.HBM) for t in both])
    return (n, modes, out[:-1]), out[-1]


def _exchange_wait(handle, after, name, which=None):
    n_all, modes, (send_sems, recv_sems, *both_all) = handle
    which = list(range(n_all)) if which is None else list(which)
    n = len(which)
    both = [both_all[a] for a in which] + [both_all[n_all + a] for a in which]

    def body(*refs):
        src_refs, land_refs = refs[:n], refs[n:2 * n]
        local, remote = _split_copies(src_refs, land_refs, refs[2 * n], refs[2 * n + 1], modes, which)
        for cp in local:
            cp.wait()
        for cp in remote:
            cp.wait_send()
            cp.wait_recv()

    out = pl.pallas_call(
        body, name=name, out_shape=tuple(pltpu.HBM(t.shape, t.dtype) for t in both),
        in_specs=[_HBM] * (2 * n) + [_SEM, _SEM, pl.BlockSpec(memory_space=pl.ANY)], out_specs=tuple([_HBM] * (2 * n)),
        input_output_aliases={i: i for i in range(2 * n)},
        compiler_params=pltpu.CompilerParams(has_side_effects=_EFFECT),
    )(*both, send_sems, recv_sems, after)
    return list(out[n:])


def _cast_all(arrays, name, after):
    waits = [] if after is None else [after]

    def body(*refs):
        for src, dst in zip(refs[:len(arrays)], refs[len(arrays) + len(waits):]):
            if len(src.shape) == 2:
                dst[...] = _bf(src[...])
            else:
                dst[:, 0, :] = _pack_pairs(_bf(src[:, 0, :]).astype(F32))

    shapes = [jax.ShapeDtypeStruct(a.shape, BF16) if a.ndim == 2 else jax.ShapeDtypeStruct((a.shape[0], 1, a.shape[2] // 2), F32)
              for a in arrays]
    return pl.pallas_call(body, name=name, out_shape=shapes,
                          compiler_params=pltpu.CompilerParams(vmem_limit_bytes=VMEM_LIMIT))(*arrays, *waits)


def _adam_update(g, w, m, v):
    nm = ADAM_B1 * m + (1.0 - ADAM_B1) * g
    nv = ADAM_B2 * v + (1.0 - ADAM_B2) * (g * g)
    m_hat = nm / (1.0 - ADAM_B1 ** ADAM_STEP)
    v_hat = nv / (1.0 - ADAM_B2 ** ADAM_STEP)
    return -ADAM_LR * (m_hat / (jnp.sqrt(v_hat) + ADAM_EPS) + ADAM_WD * w), nm, nv


def _adamw(parts, w, m, v, name):
    n, R, W = parts.shape
    tm = 128 if R % 128 == 0 else R

    def body(p_ref, w_ref, m_ref, v_ref, g_ref, d_ref, nm_ref, nv_ref):
        g = p_ref[0].astype(F32)
        for s in range(1, n):
            g = g + p_ref[s].astype(F32)
        g_ref[...] = g
        d_ref[...], nm_ref[...], nv_ref[...] = _adam_update(g, w_ref[...], m_ref[...], v_ref[...])

    tile = pl.BlockSpec((tm, W), lambda i: (i, 0))
    return pl.pallas_call(
        body, grid=(R // tm,), name=name,
        in_specs=[pl.BlockSpec((n, tm, W), lambda i: (0, i, 0)), tile, tile, tile],
        out_specs=[tile] * 4, out_shape=[jax.ShapeDtypeStruct((R, W), F32)] * 4,
        compiler_params=_params(("parallel",)),
    )(parts, w, m, v)


def _adamw_rows_apart(parts, w, m, v, name):
    n, R, _, half = parts.shape

    def body(p_hbm, w_hbm, m_hbm, v_hbm, *rest):
        out_hbm, (words, given, results, sems) = rest[:4], rest[4:]
        loads = [pltpu.make_async_copy(p_hbm.at[s, :, 0, :], words.at[s], sems.at[s]) for s in range(n)]
        loads += [pltpu.make_async_copy(h.at[:, 0, :], given.at[i], sems.at[n + i]) for i, h in enumerate((w_hbm, m_hbm, v_hbm))]
        for cp in loads:
            cp.start()
        for cp in loads:
            cp.wait()
        part = lambda s: jnp.concatenate(_unpack_pairs(words[s]), axis=1)
        g = part(0)
        for s in range(1, n):
            g = g + part(s)
        stores = []
        for i, val in enumerate((g,) + _adam_update(g, given[0], given[1], given[2])):
            results[i] = val
            stores.append(pltpu.make_async_copy(results.at[i], out_hbm[i].at[:, 0, :], sems.at[n + 3 + i]))
            stores[-1].start()
        for cp in stores:
            cp.wait()

    anywhere = pl.BlockSpec(memory_space=pl.ANY)
    return pl.pallas_call(
        body, name=name, in_specs=[anywhere] * 4, out_specs=[anywhere] * 4,
        out_shape=[jax.ShapeDtypeStruct(w.shape, F32)] * 4,
        scratch_shapes=[pltpu.VMEM((n, R, half), F32), pltpu.VMEM((3, R, 2 * half), F32), pltpu.VMEM((4, R, 2 * half), F32),
                        pltpu.SemaphoreType.DMA((n + 7,))],
        compiler_params=pltpu.CompilerParams(vmem_limit_bytes=VMEM_LIMIT),
    )(parts, w, m, v)


_MATRICES = ("w_in", "w_o", "w_up", "w_down", "w_ple_gate", "w_ple_proj")


_OTHERS = ("w_o", "w_up", "w_down", "w_ple_gate", "w_ple_proj")
_OTHER_MODES = {"w_o": "slots", "w_up": "slots", "w_down": "slots", "w_ple_gate": "slots", "w_ple_proj": "columns"}


_VECTORS = ("norm_mix", "norm_mlp", "norm_ple", "norm_final", "a_log", "dt_bias", "sinks", "dn_norm")
_SMALL_ROWS, _LOSS_ROW, _CONV_ROW = 16, 8, 9


def _pack_small(vectors, loss, conv):
    def body(*refs):
        out = refs[-1]
        out[...] = jnp.zeros_like(out)
        for r, ref in enumerate(refs[:len(_VECTORS)]):
            out[r:r + 1, 0:ref.shape[1]] = ref[...]
        out[_LOSS_ROW:_LOSS_ROW + 1, 0:128] = refs[len(_VECTORS)][...]
        out[_CONV_ROW:_CONV_ROW + 6, :] = refs[len(_VECTORS) + 1][...]

    return pl.pallas_call(body, name="pack_small", out_shape=jax.ShapeDtypeStruct((_SMALL_ROWS, 1024), F32))(*vectors, loss, conv)


def _sum_slots(parts):
    def body(p_ref, o_ref):
        acc = p_ref[0]
        for s in range(1, parts.shape[0]):
            acc = acc + p_ref[s]
        o_ref[...] = acc

    return pl.pallas_call(body, name="sum_small", out_shape=jax.ShapeDtypeStruct(parts.shape[1:], parts.dtype))(parts)


def _adamw_vectors(summed, conv_g, wmv):
    names = _VECTORS + ("conv_w",)
    flat = [a for triple in wmv for a in triple]

    def body(*refs):
        sum_ref, conv_ref = refs[0], refs[1]
        ins, outs = refs[2:2 + len(flat)], refs[2 + len(flat):]
        for i in range(len(names)):
            w_ref, m_ref, v_ref = ins[3 * i:3 * i + 3]
            g = conv_ref[...] if i == len(_VECTORS) else sum_ref[i:i + 1, 0:w_ref.shape[1]]
            outs[4 * i][...] = g
            outs[4 * i + 1][...], outs[4 * i + 2][...], outs[4 * i + 3][...] = _adam_update(g, w_ref[...], m_ref[...], v_ref[...])

    out_shape = [jax.ShapeDtypeStruct(t[0].shape, F32) for t in wmv for _ in range(4)]
    res = pl.pallas_call(body, name="adamw_vectors", out_shape=out_shape)(summed, conv_g, *flat)
    return {n: res[4 * i:4 * i + 4] for i, n in enumerate(names)}


_ORDER = ("norm_mix", "w_in", "conv_w", "a_log", "dt_bias", "dn_norm", "sinks", "w_o", "norm_mlp", "w_up", "w_down",
          "norm_ple", "w_ple_gate", "w_ple_proj", "norm_final")


def kernel(x, p, norm_mix, w_in, conv_w, a_log, dt_bias, dn_norm, sinks, w_o, norm_mlp, w_up, w_down, norm_ple, w_ple_gate, w_ple_proj, norm_final, loss_target, m_norm_mix, m_w_in, m_conv_w, m_a_log, m_dt_bias, m_dn_norm, m_sinks, m_w_o, m_norm_mlp, m_w_up, m_w_down, m_norm_ple, m_w_ple_gate, m_w_ple_proj, m_norm_final, v_norm_mix, v_w_in, v_conv_w, v_a_log, v_dt_bias, v_dn_norm, v_sinks, v_w_o, v_norm_mlp, v_w_up, v_w_down, v_norm_ple, v_w_ple_gate, v_w_ple_proj, v_norm_final):
    w = dict(norm_mix=norm_mix, w_in=w_in, conv_w=conv_w[0], a_log=a_log, dt_bias=dt_bias, dn_norm=dn_norm, sinks=sinks,
             w_o=w_o[0], norm_mlp=norm_mlp, w_up=w_up[0], w_down=w_down[0], norm_ple=norm_ple, w_ple_gate=w_ple_gate[0],
             w_ple_proj=w_ple_proj[0], norm_final=norm_final)
    m = dict(norm_mix=m_norm_mix, w_in=m_w_in, conv_w=m_conv_w[0], a_log=m_a_log, dt_bias=m_dt_bias, dn_norm=m_dn_norm,
             sinks=m_sinks, w_o=m_w_o[0], norm_mlp=m_norm_mlp, w_up=m_w_up[0], w_down=m_w_down[0], norm_ple=m_norm_ple,
             w_ple_gate=m_w_ple_gate[0], w_ple_proj=m_w_ple_proj[0], norm_final=m_norm_final)
    v = dict(norm_mix=v_norm_mix, w_in=v_w_in, conv_w=v_conv_w[0], a_log=v_a_log, dt_bias=v_dt_bias, dn_norm=v_dn_norm,
             sinks=v_sinks, w_o=v_w_o[0], norm_mlp=v_norm_mlp, w_up=v_w_up[0], w_down=v_w_down[0], norm_ple=v_norm_ple,
             w_ple_gate=v_w_ple_gate[0], w_ple_proj=v_w_ple_proj[0], norm_final=v_norm_final)
    me = 4 * lax.axis_index("x") + 2 * lax.axis_index("y") + lax.axis_index("c")
    conv_shard = conv_w.shape[2]

    for d in (w, m, v):
        d["w_in"] = jnp.transpose(d["w_in"], (2, 0, 1))
    conv_pad = jnp.pad(w["conv_w"], ((0, 8 - DN_CONV), (0, 256 - conv_shard)))
    w_in_shard, = _cast_all([w["w_in"]], "cast_w_in", None)
    gathers_first, token_first = _exchange_start([w_in_shard, conv_pad], "gather_first_start", "chips")
    shards = _cast_all([w[n] for n in _OTHERS], "cast_others", token_first)
    gathers, token_gather = _exchange_start(list(shards), "gather_start", [_OTHER_MODES[n] for n in _OTHERS])

    def first_weights(after):
        over_ici = _exchange_wait(gathers_first, after, "gather_first_wait")
        handle, token = _forward_start(over_ici, "gather_first_forward")
        w_in_all, conv_all = _forward_wait(handle, token, "gather_first_forward_wait")
        conv_all = jnp.transpose(conv_all[:, :DN_CONV, :conv_shard], (1, 0, 2)).reshape(DN_CONV, N_DEV * conv_shard)
        return _w_in_to_internal(w_in_all.reshape(D_IN, 1, _HALF)), conv_all

    as_taken = {"w_o": lambda t: t.reshape(1024, 1024), "w_up": lambda t: t, "w_down": lambda t: t.reshape(4096, 1024),
                "w_ple_gate": lambda t: t.reshape(1024, 1024), "w_ple_proj": lambda t: t}

    def other_weights(names, after):
        which = [_OTHERS.index(n) for n in names]
        got = _exchange_wait(gathers, after, "gather_wait_" + names[0], which)
        return [as_taken[n](t) for n, t in zip(names, got)]

    shipped = []

    def ship_early(pieces):
        names = tuple(pieces)
        if names == ("w_in",):
            pieces = {"w_in": _w_in_from_internal(pieces["w_in"]).reshape(N_DEV, D_IN // N_DEV, 1, _HALF)}
        handle, token = _exchange_start([pieces[n] for n in names], "scatter_start_" + names[0], "pieces")
        shipped.append((names, handle))
        return token

    loss, grad_x, g = _local_step(x[0], p[0, 0], loss_target[0], w, first_weights, other_weights, ship_early, token_gather)

    row = lambda t: t.reshape(1, t.size)
    small = _pack_small([row(g[n]) for n in _VECTORS], loss, g["conv_w"].reshape(6, 1024))
    small_handle, token_small = _exchange_start([small], "gather_small_start", "slots")
    big, after = {}, token_small
    for names, handle in shipped[:-1]:
        for n, r in zip(names, _exchange_wait(handle, after, "scatter_wait_" + names[0])):
            big[n] = _adamw(r, w[n], m[n], v[n], "adamw_" + n)
            after = big[n][1]
    small_all, = _exchange_wait(small_handle, after, "gather_small_wait")
    summed = _sum_slots(small_all)
    conv_g = lax.dynamic_slice(summed[_CONV_ROW:_CONV_ROW + 6].reshape(DN_CONV, N_DEV * conv_shard), (0, me * conv_shard),
                               (DN_CONV, conv_shard))
    small_out = _adamw_vectors(summed, conv_g, [(row(w[n]), row(m[n]), row(v[n])) for n in _VECTORS]
                               + [(w["conv_w"], m["conv_w"], v["conv_w"])])
    names, handle = shipped[-1]
    for n, r in zip(names, _exchange_wait(handle, small_out["conv_w"][0], "scatter_wait_" + names[0])):
        big[n] = _adamw_rows_apart(r, w[n], m[n], v[n], "adamw_" + n)

    result = [summed[_LOSS_ROW, 0], grad_x[None]]
    for i in range(4):
        for n in _ORDER:
            if n == "w_in":
                result.append(jnp.transpose(big[n][i], (1, 2, 0)))
            elif n in _MATRICES:
                result.append(big[n][i][None])
            elif n == "conv_w":
                result.append(small_out[n][i][None])
            else:
                result.append(small_out[n][i].reshape(w[n].shape))
    return tuple(result)
```

```python
import jax
import jax.numpy as jnp
import numpy as np
from jax import lax
from jax.experimental import pallas as pl
from jax.experimental.pallas import tpu as pltpu

F32, BF16 = jnp.float32, jnp.bfloat16
EPS = 1e-6
D_MODEL = 1024
N_DEV = 8
ATTN_BLOCK = 128
HEAD_PAIR = 128
DN_HEADS = 4
DN_DIM = 128
DN_CHUNK = 64
DN_CONV = 4
ROPE_THETA = 10000.0
D_IN = 2824
D_IN_PAD = 3072
BLK_Q, BLK_Z = 0, 1
BLK_DN, BLK_K, BLK_V, BLK_G = 8, 20, 21, 22
BLK_G_PAD = 11
VMEM_LIMIT = 56 * 1024 * 1024
NEG = -1e30
ADAM_LR, ADAM_B1, ADAM_B2, ADAM_EPS, ADAM_WD, ADAM_STEP = 0.001, 0.9, 0.999, 1e-08, 0.01, 10
MESH = pl.DeviceIdType.MESH


def _bf(x):
    return x.astype(BF16)


def _dot(a, b):
    return jnp.dot(a, b, preferred_element_type=F32)


def _dot_nt(a, b):
    return lax.dot_general(a, b, (((1,), (1,)), ((), ())), preferred_element_type=F32)


def _dot_tn(a, b):
    return lax.dot_general(a, b, (((0,), (0,)), ((), ())), preferred_element_type=F32)


def _sigmoid(x):
    return 1.0 / (1.0 + jnp.exp(-x))


def _params(sem):
    return pltpu.CompilerParams(dimension_semantics=sem, vmem_limit_bytes=VMEM_LIMIT)


def _mm(x, w, *, form, name, out_dtypes, tn, epi=None, extra=(), tm=512, w_row_block=0, after=None, norm=None,
        norm_bwd=None, then_nt=None):
    assert norm is None or norm_bwd is None
    xs = list(x) if isinstance(x, (list, tuple)) else [x]
    nx = len(xs)
    S, K = xs[0].shape
    shards = w.ndim == 3
    N = (w.shape[2] * N_DEV if shards else w.shape[1]) if form == "nn" else w.shape[-2]
    assert not (shards and form == "nn" and tn != w.shape[2]) and (nx == 1 or (form == "nn" and not shards and norm is None))
    r0 = w_row_block * K
    tm = min(tm, S)
    n_extra, n_out = len(extra), len(out_dtypes)
    tile = lambda width: pl.BlockSpec((tm, width), lambda i: (i, 0))
    whole = lambda a: pl.BlockSpec(a.shape, lambda i, nd=a.ndim: (0,) * nd)
    ins, in_specs = [*xs, w, *extra], [tile(K)] * nx + [whole(w)] + [tile(N)] * n_extra
    if norm is not None:
        ins, in_specs = ins + [norm], in_specs + [whole(norm)]
    if norm_bwd is not None:
        ins, in_specs = ins + list(norm_bwd), in_specs + [tile(N), whole(norm_bwd[1]), tile(N)]
    if then_nt is not None:
        ins, in_specs = ins + [then_nt], in_specs + [whole(then_nt)]
    if after is not None:
        ins, in_specs = ins + [after], in_specs + [whole(after)]
    out_shape = [jax.ShapeDtypeStruct((S, N), dt) for dt in out_dtypes]
    out_specs = [tile(N)] * n_out
    if norm is not None:
        out_shape, out_specs = out_shape + [jax.ShapeDtypeStruct((S, K), BF16)], out_specs + [tile(K)]
    if norm_bwd is not None:
        out_shape, out_specs = out_shape + [jax.ShapeDtypeStruct((1, N), F32)], out_specs + [pl.BlockSpec((1, N), lambda i: (0, 0))]
    if then_nt is not None:
        out_shape, out_specs = out_shape + [jax.ShapeDtypeStruct((S, then_nt.shape[0]), F32)], out_specs + [tile(then_nt.shape[0])]

    def product(xb, w_ref, cols, c):
        if form == "nn" and nx > 1:
            return sum(_dot(part, w_ref[r0 + p * K:r0 + (p + 1) * K, cols]) for p, part in enumerate(xb))
        if form == "nn":
            return _dot(xb, w_ref[c] if shards else w_ref[r0:r0 + K, cols])
        if not shards:
            return _dot_nt(xb, w_ref[cols, :])
        ks = w.shape[2]
        acc = _dot_nt(xb[:, 0:ks], w_ref[0, cols, :])
        for s in range(1, N_DEV):
            acc = acc + _dot_nt(xb[:, s * ks:(s + 1) * ks], w_ref[s, cols, :])
        return acc

    def body(*refs):
        x_ref, w_ref = refs[0], refs[nx]
        extra_refs = refs[nx + 1:nx + 1 + n_extra]
        at = nx + 1 + n_extra
        if norm is not None:
            gain_ref, at = refs[at], at + 1
        if norm_bwd is not None:
            (y_ref, ygain_ref, dres_ref), at = refs[at:at + 3], at + 3
        if then_nt is not None:
            w2_ref, at = refs[at], at + 1
        outs = refs[len(ins):]
        if norm is not None:
            _, xh = _rms_stats(x_ref[...])
            xb = _bf(xh * gain_ref[...])
            outs[n_out][...] = xb
        else:
            xb = _bf(x_ref[...]) if nx == 1 else [_bf(r[...]) for r in refs[:nx]]
        for c in range(N // tn):
            cols = slice(c * tn, (c + 1) * tn)
            acc = product(xb, w_ref, cols, c)
            res = epi(acc, *[r[:, cols] for r in extra_refs]) if epi else (acc,)
            for o, r in zip(outs[:n_out], res):
                o[:, cols] = r.astype(o.dtype)
        if norm_bwd is not None:
            dx, dg = _rms_bwd_tile(y_ref[...], ygain_ref[...], outs[0][...])
            outs[0][...] = dres_ref[...] + dx
            dg_ref = outs[n_out]

            @pl.when(pl.program_id(0) == 0)
            def _():
                dg_ref[...] = jnp.zeros_like(dg_ref)

            dg_ref[...] += dg
        if then_nt is not None:
            yb = _bf(outs[0][...])
            for c in range(then_nt.shape[0] // tn):
                cols = slice(c * tn, (c + 1) * tn)
                outs[-1][:, cols] = _dot_nt(yb, w2_ref[cols, :])

    return pl.pallas_call(
        body, grid=(S // tm,), name=name, in_specs=in_specs, out_specs=out_specs, out_shape=out_shape,
        compiler_params=_params(("arbitrary",) if norm_bwd is not None else ("parallel",)),
    )(*ins)


def _mlp_fwd(h1, w_up, w_down, gain):
    S, K = h1.shape
    n_sh, _, fs = w_up.shape
    tm = min(512, S)

    def body(x_ref, wup_ref, wdown_ref, g_ref, hid_ref, relu_ref, m_ref, h2_ref):
        x = x_ref[...]
        _, xh = _rms_stats(x)
        mb = _bf(xh * g_ref[...])
        m_ref[...] = mb
        h2_ref[...] = x
        for c in range(n_sh):
            cols = slice(c * fs, (c + 1) * fs)
            r = jnp.maximum(_dot(mb, wup_ref[c]), 0.0)
            hd = _bf(r * r)
            hid_ref[:, cols] = hd
            relu_ref[:, cols] = _bf(r)
            h2_ref[...] += _dot(hd, wdown_ref[cols, :])

    tile = lambda width: pl.BlockSpec((tm, width), lambda i: (i, 0))
    once = lambda a: pl.BlockSpec(a.shape, lambda i, nd=a.ndim: (0,) * nd, pipeline_mode=pl.Buffered(1))
    F = n_sh * fs
    return pl.pallas_call(
        body, grid=(S // tm,), name="mlp_fwd",
        in_specs=[tile(K), once(w_up), once(w_down), pl.BlockSpec(gain.shape, lambda i: (0, 0))],
        out_specs=[tile(F), tile(F), tile(K), tile(K)],
        out_shape=[jax.ShapeDtypeStruct((S, F), BF16), jax.ShapeDtypeStruct((S, F), BF16),
                   jax.ShapeDtypeStruct((S, K), BF16), jax.ShapeDtypeStruct((S, K), F32)],
        compiler_params=_params(("parallel",)),
    )(h1, w_up, w_down, gain)


def _mm_tn(x, dy, *, name, tm, tn, out_dtype=F32, column_shards=False, after=None):
    xs = list(x) if isinstance(x, (list, tuple)) else [x]
    many = len(xs) > 1
    S, N = dy.shape
    K = tm * len(xs) if many else x.shape[1]
    waits = [] if after is None else [after]
    grid = (N // tn, K // tm) if many else (K // tm, N // tn)
    at = (lambda f: lambda j, i: f(i, j)) if many else (lambda f: f)

    def body(*refs):
        dy_ref, out_ref = refs[len(xs)], refs[-1]
        if not many:
            out_ref[...] = _dot_tn(_bf(refs[0][...]), _bf(dy_ref[...])).astype(out_dtype)
        for k in range(len(xs) if many else 0):
            @pl.when(pl.program_id(1) == k)
            def _(k=k):
                out_ref[...] = _dot_tn(_bf(refs[k][...]), _bf(dy_ref[...])).astype(out_dtype)

    if column_shards:
        out_spec = pl.BlockSpec((None, tm, tn), at(lambda i, j: (j, i, 0)))
        out_shape = jax.ShapeDtypeStruct((N // tn, K, tn), out_dtype)
    else:
        out_spec = pl.BlockSpec((tm, tn), at(lambda i, j: (i, j)))
        out_shape = jax.ShapeDtypeStruct((K, N), out_dtype)
    x_specs = [pl.BlockSpec((S, tm), at(lambda i, j: (0, 0)))] * len(xs) if many else [pl.BlockSpec((S, tm), lambda i, j: (0, i))]
    return pl.pallas_call(
        body, grid=grid, name=name,
        in_specs=x_specs + [pl.BlockSpec((S, tn), at(lambda i, j: (0, j)))] + [pl.BlockSpec(memory_space=pl.ANY)] * len(waits),
        out_specs=out_spec, out_shape=out_shape,
        compiler_params=_params(("parallel", "parallel")),
    )(*xs, dy, *waits)


def _rowwise(body, *, tiled, full, out_tiled, out_acc, name, tm=512, smem=()):
    S = tiled[0].shape[0]
    tm = min(tm, S)
    n_in = len(smem) + len(tiled) + len(full)

    def kern(*refs):
        @pl.when(pl.program_id(0) == 0)
        def _():
            for r in refs[n_in + len(out_tiled):]:
                r[...] = jnp.zeros_like(r)
        body(*refs)

    in_specs = [pl.BlockSpec(memory_space=pltpu.SMEM) for _ in smem]
    in_specs += [pl.BlockSpec((tm, a.shape[1]), lambda i: (i, 0)) for a in tiled]
    in_specs += [pl.BlockSpec(a.shape, lambda i, nd=a.ndim: (0,) * nd) for a in full]
    out_specs = [pl.BlockSpec((tm, w), lambda i: (i, 0)) for w, _ in out_tiled]
    out_specs += [pl.BlockSpec(shp, lambda i, nd=len(shp): (0,) * nd) for shp, _ in out_acc]
    out_shape = [jax.ShapeDtypeStruct((S, w), dt) for w, dt in out_tiled]
    out_shape += [jax.ShapeDtypeStruct(shp, dt) for shp, dt in out_acc]
    return pl.pallas_call(
        kern, grid=(S // tm,), name=name, in_specs=in_specs, out_specs=out_specs, out_shape=out_shape,
        compiler_params=_params(("arbitrary",)),
    )(*smem, *tiled, *full)


def _rms_stats(x):
    r = lax.rsqrt(jnp.mean(x * x, axis=-1, keepdims=True) + EPS)
    return r, x * r


def _rmsnorm_fwd(x, g, name, after):
    def body(x_ref, g_ref, _, o_ref):
        _, xh = _rms_stats(x_ref[...])
        o_ref[...] = _bf(xh * g_ref[...])

    return _rowwise(body, tiled=[x], full=[g, after], out_tiled=[(x.shape[1], BF16)], out_acc=[], name=name)[0]


def _rms_bwd_tile(x, g, dxn):
    r, xh = _rms_stats(x)
    dg = jnp.sum(dxn * xh, axis=0, keepdims=True)
    dn = dxn * g
    dx = r * (dn - xh * jnp.mean(dn * xh, axis=-1, keepdims=True))
    return dx, dg


def _ple_and_loss(h2, p, target, w_pg, w_pp, g_ple, g_final):
    S, n = h2.shape
    tm = min(512, S)
    tn = 512

    def body(h2_ref, p_ref, t_ref, wpg_ref, wpp_ref, gple_ref, gfin_ref,
             n3_ref, dh_ref, dgl_ref, dpp_ref, loss_ref, dg_ref, dgple_ref, pp, gate, h3):
        @pl.when(pl.program_id(0) == 0)
        def _():
            loss_ref[...] = jnp.zeros_like(loss_ref)
            dg_ref[...] = jnp.zeros_like(dg_ref)
            dgple_ref[...] = jnp.zeros_like(dgple_ref)

        x = h2_ref[...]
        _, xh = _rms_stats(x)
        n3 = _bf(xh * gple_ref[...])
        n3_ref[...] = n3
        pb = _bf(p_ref[...])
        for c in range(n // tn):
            cols = slice(c * tn, (c + 1) * tn)
            pp[:, cols] = _dot(pb, wpp_ref[:, cols])
            gt = _sigmoid(_dot(n3, wpg_ref[:, cols]))
            gate[:, cols] = gt
            h3[:, cols] = x[:, cols] + gt * pp[:, cols]
        y = h3[...]
        _, yh = _rms_stats(y)
        e = yh * gfin_ref[...] - t_ref[...]
        per_tok = jnp.mean(e * e, axis=-1, keepdims=True)
        loss_ref[...] += 0.5 * jnp.sum(per_tok, axis=0, keepdims=True)
        dh, dg = _rms_bwd_tile(y, gfin_ref[...], e * (1.0 / n))
        dg_ref[...] += dg
        gt = gate[...]
        dgl = _bf(dh * pp[...] * gt * (1.0 - gt))
        dgl_ref[...] = dgl
        dpp_ref[...] = _bf(dh * gt)
        for c in range(n // tn):
            cols = slice(c * tn, (c + 1) * tn)
            h3[:, cols] = _dot_nt(dgl, wpg_ref[cols, :])
        dx, dgp = _rms_bwd_tile(x, gple_ref[...], h3[...])
        dh_ref[...] = dh + dx
        dgple_ref[...] += dgp

    tile = lambda width: pl.BlockSpec((tm, width), lambda i: (i, 0))
    whole = lambda a: pl.BlockSpec(a.shape, lambda i, nd=a.ndim: (0,) * nd)
    return pl.pallas_call(
        body, grid=(S // tm,), name="ple_and_loss",
        in_specs=[tile(n), tile(p.shape[1]), tile(n), whole(w_pg), whole(w_pp), whole(g_ple), whole(g_final)],
        out_specs=[tile(n), tile(n), tile(n), tile(n), pl.BlockSpec((1, 128), lambda i: (0, 0)),
                   pl.BlockSpec((1, n), lambda i: (0, 0)), pl.BlockSpec((1, n), lambda i: (0, 0))],
        out_shape=[jax.ShapeDtypeStruct((S, n), BF16), jax.ShapeDtypeStruct((S, n), F32), jax.ShapeDtypeStruct((S, n), BF16),
                   jax.ShapeDtypeStruct((S, n), BF16), jax.ShapeDtypeStruct((1, 128), F32), jax.ShapeDtypeStruct((1, n), F32),
                   jax.ShapeDtypeStruct((1, n), F32)],
        scratch_shapes=[pltpu.VMEM((tm, n), F32)] * 3,
        compiler_params=_params(("arbitrary",)),
    )(h2, p, target, w_pg, w_pp, g_ple, g_final)


def _rope_tables(S):
    half = 32
    inv = (1.0 / (np.float32(ROPE_THETA) ** (np.arange(half, dtype=np.float32) * np.float32(2.0 / 64)))).astype(np.float32)
    ang = np.arange(S).astype(np.float32)[:, None] * inv[None, :]
    cos, sin = np.cos(ang), np.sin(ang)
    return jnp.asarray(np.tile(cos, (1, 4))), jnp.asarray(np.concatenate([-sin, sin, -sin, sin], axis=1))


def _attn_common(i, kc, kp, vc, vp, cc, sc, cp, sp):
    lane = lax.broadcasted_iota(jnp.int32, (1, HEAD_PAIR), 1)
    lane_lo = jnp.bitwise_and(lane, 63) < 32
    slot = [lane < 64, lane >= 64]

    def swap_halves(t):
        return jnp.where(lane_lo, pltpu.roll(t, 96, 1), pltpu.roll(t, 32, 1))

    def rope(t, cos, sin):
        return t * cos + swap_halves(t) * sin

    def unrope(d, cos, sin):
        return d * cos + swap_halves(d * sin)

    k2 = jnp.concatenate([rope(kp, cp, sp), rope(kc, cc, sc)], axis=0)
    v2 = jnp.concatenate([vp, vc], axis=0)
    r = lax.broadcasted_iota(jnp.int32, (ATTN_BLOCK, 2 * ATTN_BLOCK), 0)
    c = lax.broadcasted_iota(jnp.int32, (ATTN_BLOCK, 2 * ATTN_BLOCK), 1)
    valid = (c > r) & (c <= r + ATTN_BLOCK) & jnp.logical_or(c >= ATTN_BLOCK, i > 0)
    ks, vs = {}, {}
    for j in range(2):
        kn = jnp.where(slot[j], k2, 0.0)
        vn = jnp.where(slot[j], v2, 0.0)
        for s in range(2):
            ks[j, s] = _bf(kn if s == j else pltpu.roll(kn, 64, 1))
            vs[j, s] = _bf(vn if s == j else pltpu.roll(vn, 64, 1))
    return slot, rope, unrope, valid, ks, vs


def _attn_probs(scores, valid, sink):
    s = jnp.where(valid, scores * 0.125, NEG)
    m = jnp.maximum(jnp.max(s, axis=1, keepdims=True), sink)
    e = jnp.exp(s - m)
    z = jnp.sum(e, axis=1, keepdims=True) + jnp.exp(sink - m)
    return e * (1.0 / z), m + jnp.log(z)


def _attn_specs(S):
    nb = S // ATTN_BLOCK
    prev = lambda i: jnp.maximum(i - 1, 0)
    blk = lambda w, col, row=(lambda i: i): pl.BlockSpec((ATTN_BLOCK, w), lambda i: (row(i), col))
    in_specs = [pl.BlockSpec(memory_space=pltpu.SMEM),
                blk(512, BLK_Q), blk(128, BLK_K), blk(128, BLK_K, prev), blk(128, BLK_V), blk(128, BLK_V, prev),
                blk(128, 0), blk(128, 0), blk(128, 0, prev), blk(128, 0, prev)]
    return nb, in_specs


def _attn_fwd(pa, cos, sin, sinks):
    S = pa.shape[0]
    nb, in_specs = _attn_specs(S)

    def body(sinks_ref, q_ref, kc_ref, kp_ref, vc_ref, vp_ref, cc_ref, sc_ref, cp_ref, sp_ref, o_ref, lse_ref):
        i = pl.program_id(0)
        lane = lax.broadcasted_iota(jnp.int32, (1, HEAD_PAIR), 1)
        cc, sc = cc_ref[...], sc_ref[...]
        _, rope, _, valid, ks, vs = _attn_common(i, kc_ref[...], kp_ref[...], vc_ref[...], vp_ref[...],
                                                 cc, sc, cp_ref[...], sp_ref[...])
        pair_cols = [slice(HEAD_PAIR * pair, HEAD_PAIR * (pair + 1)) for pair in range(4)]
        qps = [_bf(rope(q_ref[:, cols], cc, sc)) for cols in pair_cols]
        outs, lses = {}, {}

        def head_program(h):
            pair, s = divmod(h, 2)
            j = h // 4
            scores = _dot_nt(qps[pair], ks[j, s])
            yield
            p, lse = _attn_probs(scores, valid, sinks_ref[h])
            outs[h] = _dot(_bf(p), vs[j, s])
            lses[h] = jnp.where(lane == h, lse, 0.0)

        _interleave(head_program(h) for h in range(8))
        for pair, cols in enumerate(pair_cols):
            o_ref[:, cols] = outs[2 * pair] + outs[2 * pair + 1]
        lse_ref[...] = sum((lses[h] for h in range(1, 8)), lses[0])

    return pl.pallas_call(
        body, grid=(nb,), name="attn_fwd", in_specs=in_specs,
        out_specs=[pl.BlockSpec((ATTN_BLOCK, 512), lambda i: (i, 0)), pl.BlockSpec((ATTN_BLOCK, 128), lambda i: (i, 0))],
        out_shape=[jax.ShapeDtypeStruct((S, 512), F32), jax.ShapeDtypeStruct((S, 128), F32)],
        compiler_params=_params(("parallel",)),
    )(sinks, pa, pa, pa, pa, pa, cos, sin, cos, sin)


def _attn_bwd(pa, cos, sin, sinks, dcat, attn, lse, after):
    S = pa.shape[0]
    nb, in_specs = _attn_specs(S)
    in_specs = in_specs + [pl.BlockSpec((ATTN_BLOCK, 512), lambda i: (i, 0))] * 2 + [pl.BlockSpec((ATTN_BLOCK, 128), lambda i: (i, 0))]
    in_specs = in_specs + [pl.BlockSpec(memory_space=pl.ANY)]

    def body(sinks_ref, q_ref, kc_ref, kp_ref, vc_ref, vp_ref, cc_ref, sc_ref, cp_ref, sp_ref, do_ref, o_ref, lse_ref, _,
             dq_ref, dk_ref, dv_ref, dsink_ref):
        i = pl.program_id(0)

        @pl.when(i == 0)
        def _():
            dk_ref[...] = jnp.zeros_like(dk_ref)
            dv_ref[...] = jnp.zeros_like(dv_ref)
            dsink_ref[...] = jnp.zeros_like(dsink_ref)

        cc, sc, cp, sp = cc_ref[...], sc_ref[...], cp_ref[...], sp_ref[...]
        slot, rope, unrope, valid, ks, vs = _attn_common(i, kc_ref[...], kp_ref[...], vc_ref[...], vp_ref[...], cc, sc, cp, sp)
        pair_cols = [slice(HEAD_PAIR * pair, HEAD_PAIR * (pair + 1)) for pair in range(4)]
        qps = [_bf(rope(q_ref[:, cols], cc, sc)) for cols in pair_cols]
        dobs = [_bf(do_ref[:, cols]) for cols in pair_cols]
        do_o = [do_ref[:, cols] * o_ref[:, cols] for cols in pair_cols]
        dqs, dks, dvs = {}, {}, {}

        def head_program(h):
            pair, s = divmod(h, 2)
            j = h // 4
            qp, dob = qps[pair], dobs[pair]
            scores = _dot_nt(qp, ks[j, s])
            dp = _dot_nt(dob, vs[j, s])
            yield
            lse_h = lse_ref[:, h:h + 1]
            p = jnp.exp(jnp.where(valid, scores * 0.125, NEG) - lse_h)
            yield
            dr = jnp.sum(jnp.where(slot[s], do_o[pair], 0.0), axis=1, keepdims=True)
            ds = _bf(p * (dp - dr) * 0.125)
            yield
            dsink_ref[h:h + 1, :] += -jnp.sum(jnp.exp(sinks_ref[h] - lse_h) * dr, axis=0, keepdims=True)
            dqs[h] = _dot(ds, ks[j, s])
            dk_h = _dot_tn(ds, qp)
            dv_h = _dot_tn(_bf(p), dob)
            yield
            dk_h, dv_h = jnp.where(slot[s], dk_h, 0.0), jnp.where(slot[s], dv_h, 0.0)
            if s != j:
                dk_h, dv_h = pltpu.roll(dk_h, 64, 1), pltpu.roll(dv_h, 64, 1)
            dks[h], dvs[h] = dk_h, dv_h

        _interleave(head_program(h) for h in range(8))
        dk2 = sum((dks[h] for h in range(1, 8)), dks[0])
        dv2 = sum((dvs[h] for h in range(1, 8)), dvs[0])
        for pair, cols in enumerate(pair_cols):
            dq_ref[:, cols] = _bf(unrope(dqs[2 * pair] + dqs[2 * pair + 1], cc, sc))
        cur = pl.ds(pl.multiple_of(i * ATTN_BLOCK, ATTN_BLOCK), ATTN_BLOCK)
        dk_ref[cur, :] += unrope(dk2[ATTN_BLOCK:], cc, sc)
        dv_ref[cur, :] += dv2[ATTN_BLOCK:]

        @pl.when(i > 0)
        def _():
            prv = pl.ds(pl.multiple_of((i - 1) * ATTN_BLOCK, ATTN_BLOCK), ATTN_BLOCK)
            dk_ref[prv, :] += unrope(dk2[:ATTN_BLOCK], cp, sp)
            dv_ref[prv, :] += dv2[:ATTN_BLOCK]

    whole = lambda w: pl.BlockSpec((S, w), lambda i: (0, 0))
    return pl.pallas_call(
        body, grid=(nb,), name="attn_bwd", in_specs=in_specs,
        out_specs=[pl.BlockSpec((ATTN_BLOCK, 512), lambda i: (i, BLK_Q)), whole(128), whole(128),
                   pl.BlockSpec((8, 128), lambda i: (0, 0))],
        out_shape=[jax.ShapeDtypeStruct((S, D_IN_PAD), BF16), jax.ShapeDtypeStruct((S, 128), F32),
                   jax.ShapeDtypeStruct((S, 128), F32), jax.ShapeDtypeStruct((8, 128), F32)],
        compiler_params=_params(("arbitrary",)),
    )(sinks, pa, pa, pa, pa, pa, cos, sin, cos, sin, dcat, attn, lse, after)


CONV_ROWS = 512
CONV_PAD = 8


def _conv_silu(scr, w, r0):
    y = w[3:4, :] * scr[pl.ds(CONV_PAD + r0, CONV_ROWS), :]
    for j in range(DN_CONV - 1):
        y = y + w[j:j + 1, :] * scr[pl.ds(CONV_PAD + r0 - 3 + j, CONV_ROWS), :]
    return y


def _dn_prep_fwd(pd, conv_w):
    S = pd.shape[0]
    assert S % CONV_ROWS == 0

    def body(x_ref, w_ref, o_ref, scr):
        b = pl.program_id(0)
        scr[0:CONV_PAD, :] = jnp.zeros((CONV_PAD, DN_DIM), F32)
        scr[pl.ds(CONV_PAD, S), :] = x_ref[...]
        w = w_ref[...]
        q_scale = jnp.where(b < DN_HEADS, DN_DIM ** -0.5, 1.0)
        for r0 in range(0, S, CONV_ROWS):
            y = _conv_silu(scr, w, r0)
            a = y * _sigmoid(y)
            rs = lax.rsqrt(jnp.sum(a * a, axis=1, keepdims=True) + EPS)
            o_ref[pl.ds(r0, CONV_ROWS), :] = a * jnp.where(b < 2 * DN_HEADS, rs * q_scale, 1.0)

    col = pl.BlockSpec((S, DN_DIM), lambda b: (0, b))
    return pl.pallas_call(
        body, grid=(3 * DN_HEADS,), name="dn_prep_fwd",
        in_specs=[pl.BlockSpec((S, DN_DIM), lambda b: (0, BLK_DN + b)), pl.BlockSpec((DN_CONV, DN_DIM), lambda b: (0, b))],
        out_specs=col,
        out_shape=jax.ShapeDtypeStruct((S, 3 * DN_HEADS * DN_DIM), F32),
        scratch_shapes=[pltpu.VMEM((S + CONV_PAD, DN_DIM), F32)],
        compiler_params=_params(("parallel",)),
    )(pd, conv_w)


def _dn_prep_bwd(pd, conv_w, dqkv, dproj, dk, dv):
    S = pd.shape[0]
    NB = 3 * DN_HEADS

    def body(x_ref, w_ref, d_ref, _, dk_ref, dv_ref, dx_ref, dw_ref, scr, dscr):
        b = pl.program_id(0)

        @pl.when(b == NB)
        def _():
            dx_ref[...] = _bf(dk_ref[...])

        @pl.when(b == NB + 1)
        def _():
            dx_ref[...] = _bf(dv_ref[...])

        @pl.when(b < NB)
        def _():
            scr[0:CONV_PAD, :] = jnp.zeros((CONV_PAD, DN_DIM), F32)
            scr[pl.ds(CONV_PAD, S), :] = x_ref[...]
            dscr[pl.ds(S, CONV_PAD), :] = jnp.zeros((CONV_PAD, DN_DIM), F32)
            w = w_ref[...]
            q_scale = jnp.where(b < DN_HEADS, DN_DIM ** -0.5, 1.0)
            is_qk = b < 2 * DN_HEADS
            dw = [jnp.zeros((1, DN_DIM), F32) for _ in range(DN_CONV)]
            for r0 in range(0, S, CONV_ROWS):
                y = _conv_silu(scr, w, r0)
                sg = _sigmoid(y)
                a = y * sg
                dout = d_ref[pl.ds(r0, CONV_ROWS), :]
                rs = lax.rsqrt(jnp.sum(a * a, axis=1, keepdims=True) + EPS)
                da_qk = q_scale * rs * (dout - a * (rs * rs) * jnp.sum(dout * a, axis=1, keepdims=True))
                dy = jnp.where(is_qk, da_qk, dout) * (sg * (1.0 + y * (1.0 - sg)))
                dscr[pl.ds(r0, CONV_ROWS), :] = dy
                for j in range(DN_CONV):
                    dw[j] = dw[j] + jnp.sum(dy * scr[pl.ds(CONV_PAD + r0 - 3 + j, CONV_ROWS), :], axis=0, keepdims=True)
            for j in range(DN_CONV):
                dw_ref[j:j + 1, :] = dw[j]
            for r0 in range(0, S, CONV_ROWS):
                dx = w[3:4, :] * dscr[pl.ds(r0, CONV_ROWS), :]
                for j in range(DN_CONV - 1):
                    dx = dx + w[j:j + 1, :] * dscr[pl.ds(r0 + 3 - j, CONV_ROWS), :]
                dx_ref[pl.ds(r0, CONV_ROWS), :] = _bf(dx)

    own = lambda b: jnp.minimum(b, NB - 1)
    col = pl.BlockSpec((S, DN_DIM), lambda b: (0, own(b)))
    proj_col = pl.BlockSpec((S, DN_DIM), lambda b: (0, BLK_DN + own(b)))
    wcol = pl.BlockSpec((DN_CONV, DN_DIM), lambda b: (0, own(b)))
    whole = pl.BlockSpec((S, DN_DIM), lambda b: (0, 0))
    assert BLK_K == BLK_DN + NB and BLK_V == BLK_K + 1
    return pl.pallas_call(
        body, grid=(NB + 2,), name="dn_prep_bwd",
        in_specs=[proj_col, wcol, col, pl.BlockSpec(memory_space=pl.ANY), whole, whole],
        out_specs=[pl.BlockSpec((S, DN_DIM), lambda b: (0, BLK_DN + b)), wcol],
        out_shape=[jax.ShapeDtypeStruct(dproj.shape, dproj.dtype), jax.ShapeDtypeStruct((DN_CONV, 3 * DN_HEADS * DN_DIM), F32)],
        scratch_shapes=[pltpu.VMEM((S + CONV_PAD, DN_DIM), F32), pltpu.VMEM((S + CONV_PAD, DN_DIM), F32)],
        input_output_aliases={3: 0},
        compiler_params=_params(("arbitrary",)),
    )(pd, conv_w, dqkv, dproj, dk, dv)


CPAD = 128
CHUNKS_LOCAL = 8
CHUNKS_SCAN = 8


def _chunk_masks():
    ii = lax.broadcasted_iota(jnp.int32, (DN_CHUNK, CPAD), 0)
    jj = lax.broadcasted_iota(jnp.int32, (DN_CHUNK, CPAD), 1)
    return ii, jj


def _rows_pad(a):
    return jnp.concatenate([a, jnp.zeros_like(a)], axis=0)


def _hi_lo(a):
    hi = _bf(a)
    return hi, _bf(a - hi.astype(F32))


def _double_step(t, p):
    C = DN_CHUNK
    th, tl = _hi_lo(t)
    ph, pl_ = _hi_lo(p)
    r1 = _dot(jnp.concatenate([th, tl, ph, pl_], axis=0), _rows_pad(ph))
    r2 = _dot(jnp.concatenate([th, ph], axis=0), _rows_pad(pl_))
    return t + (r1[:C] + r1[C:2 * C] + r2[:C]), r1[2 * C:3 * C] + r1[3 * C:] + r2[C:]


def _dot3_nt(a, b):
    C = DN_CHUNK
    ah, al = _hi_lo(a)
    bh, bl = _hi_lo(b)
    r1 = _dot_nt(jnp.concatenate([ah, al], axis=0), _rows_pad(bh))
    return r1[:C] + r1[C:] + _dot_nt(ah, _rows_pad(bl))


def _dot3_tn(a, b):
    C = DN_CHUNK
    ah, al = _hi_lo(a)
    bh, bl = _hi_lo(b)
    return _dot_tn(jnp.concatenate([ah, al, ah], axis=0), jnp.concatenate([bh, bh, bl], axis=0))[:C]


def _interleave(programs):
    programs = list(programs)
    while programs:
        alive = []
        for prog in programs:
            try:
                next(prog)
                alive.append(prog)
            except StopIteration:
                pass
        programs = alive


def _col_to_row(col, ii, jj):
    return jnp.sum(jnp.where(ii == jj, col, 0.0), axis=0, keepdims=True)


def _row_to_col(row, ii, jj):
    return jnp.sum(jnp.where(ii == jj, row, 0.0), axis=1, keepdims=True)


def _decay(gc_col, ii, jj):
    diff = gc_col - _col_to_row(gc_col, ii, jj)
    return jnp.where(jj <= ii, jnp.exp(jnp.where(jj <= ii, diff, 0.0)), 0.0)


def _softplus(x):
    return jnp.maximum(x, 0.0) + jnp.log(1.0 + jnp.exp(-jnp.abs(x)))


def _head(h):
    return slice(DN_DIM * h, DN_DIM * (h + 1))


def _dn_chunk_fwd(qkv, pg, a_log, dt_bias):
    S = qkv.shape[0]
    C = DN_CHUNK
    G = CHUNKS_LOCAL
    R = G * C
    steps = S // R

    def body(alog_ref, dtb_ref, qkv_ref, pg_ref, w_ref, u_ref, qg_ref, kd_ref, a_ref, t_ref, gcs_ref):
        ii, jj = _chunk_masks()
        lane = lax.broadcasted_iota(jnp.int32, (1, 128), 1)
        eye = (ii == jj).astype(F32)
        gcs_parts = [[] for _ in range(G)]

        def head_program(chunk, h):
            rows = slice(chunk * C, (chunk + 1) * C)
            q, k, v = qkv_ref[rows, _head(h)], qkv_ref[rows, _head(DN_HEADS + h)], qkv_ref[rows, _head(2 * DN_HEADS + h)]
            beta = _sigmoid(pg_ref[rows, h:h + 1])
            g_col = -jnp.exp(alog_ref[h]) * _softplus(pg_ref[rows, DN_HEADS + h:DN_HEADS + h + 1] + dtb_ref[h])
            g_row = _col_to_row(g_col, ii, jj)
            gc_col = jnp.sum(jnp.where(jj <= ii, g_row, 0.0), axis=1, keepdims=True)
            dec = _decay(gc_col, ii, jj)
            eg = jnp.exp(gc_col)
            kb, vb = k * beta, v * beta
            k_rows = _rows_pad(_bf(k))
            kk = _dot_nt(_bf(kb), k_rows)
            qk = _dot_nt(_bf(q), k_rows)
            yield
            t, pw = eye, -jnp.where(jj < ii, kk * dec, 0.0)
            for _ in range(6):
                t, pw = _double_step(t, pw)
                yield
            tb = _bf(t)
            u_ref[rows, _head(h)] = _dot(tb, _rows_pad(_bf(vb)))
            w_ref[rows, _head(h)] = _bf(_dot(tb, _rows_pad(_bf(kb * eg))))
            a_ref[h, rows] = _bf(qk * dec)
            t_ref[h, rows] = t
            qg_ref[rows, _head(h)] = _bf(q * eg)
            kd_ref[rows, _head(h)] = _bf(k * jnp.exp(gc_col[C - 1:C, :] - gc_col))
            gcs_parts[chunk].append(jnp.where(lane == h, gc_col, 0.0) + jnp.where(lane == DN_HEADS + h, beta, 0.0)
                                    + jnp.where(lane == 2 * DN_HEADS + h, g_col, 0.0))

        _interleave(head_program(chunk, h) for chunk in range(G) for h in range(DN_HEADS))
        for chunk in range(G):
            gcs_ref[chunk * C:(chunk + 1) * C, :] = sum(gcs_parts[chunk][1:], gcs_parts[chunk][0])

    smem = pl.BlockSpec(memory_space=pltpu.SMEM)
    wide = pl.BlockSpec((R, 512), lambda n: (n, 0))
    sq = pl.BlockSpec((DN_HEADS, R, CPAD), lambda n: (0, n, 0))
    narrow = pl.BlockSpec((R, 128), lambda n: (n, 0))
    f = lambda *shp: jax.ShapeDtypeStruct(shp, F32)
    b = lambda *shp: jax.ShapeDtypeStruct(shp, BF16)
    return pl.pallas_call(
        body, grid=(steps,), name="dn_chunk_fwd",
        in_specs=[smem, smem, pl.BlockSpec((R, 1536), lambda n: (n, 0)), pl.BlockSpec((R, 128), lambda n: (n, BLK_G))],
        out_specs=[wide, wide, wide, wide, sq, sq, narrow],
        out_shape=[b(S, 512), f(S, 512), b(S, 512), b(S, 512), b(DN_HEADS, S, CPAD), f(DN_HEADS, S, CPAD), f(S, 128)],
        compiler_params=_params(("parallel",)),
    )(a_log, dt_bias, qkv, pg)


def _gated_norm(o, z, gn):
    r, oh = _rms_stats(o)
    return oh * gn * (z * _sigmoid(z))


def _dn_scan_fwd(w, u, qg, kd, a, gcs, pz, gn):
    S = w.shape[0]
    C = DN_CHUNK
    nc = S // C
    G = CHUNKS_SCAN
    R = G * C

    def body(w_ref, u_ref, qg_ref, kd_ref, a_ref, gcs_ref, z_ref, gn_ref, o_ref, vn_ref, sst_ref, out_ref, state):
        @pl.when(pl.program_id(0) == 0)
        def _():
            state[...] = jnp.zeros_like(state)

        def head_program(chunk, h):
            hs = _head(h)
            rows = slice(chunk * C, (chunk + 1) * C)
            s_in = state[h]
            sb = _bf(s_in)
            sst_ref[chunk, h] = sb
            w_s = _dot(w_ref[rows, hs], sb)
            q_s = _dot(qg_ref[rows, hs], sb)
            yield
            vn = u_ref[rows, hs] - w_s
            vnb = _bf(vn)
            o = q_s + _dot(a_ref[h, rows], _rows_pad(vnb))
            k_v = _dot_tn(kd_ref[rows, hs], vnb)
            yield
            state[h] = s_in * jnp.exp(gcs_ref[(chunk + 1) * C - 1:(chunk + 1) * C, h:h + 1]) + k_v
            o_ref[rows, hs] = o
            vn_ref[rows, hs] = vnb
            out_ref[rows, hs] = _bf(_gated_norm(o, z_ref[rows, hs], gn_ref[...]))

        for chunk in range(G):
            _interleave(head_program(chunk, h) for h in range(DN_HEADS))

    wide = pl.BlockSpec((R, 512), lambda n: (n, 0))
    f = lambda *shp: jax.ShapeDtypeStruct(shp, F32)
    b = lambda *shp: jax.ShapeDtypeStruct(shp, BF16)
    return pl.pallas_call(
        body, grid=(nc // G,), name="dn_scan_fwd",
        in_specs=[wide, wide, wide, wide, pl.BlockSpec((DN_HEADS, R, CPAD), lambda n: (0, n, 0)),
                  pl.BlockSpec((R, 128), lambda n: (n, 0)), pl.BlockSpec((R, 512), lambda n: (n, BLK_Z)),
                  pl.BlockSpec((1, DN_DIM), lambda n: (0, 0))],
        out_specs=[wide, wide, pl.BlockSpec((G, DN_HEADS, DN_DIM, DN_DIM), lambda n: (n, 0, 0, 0)), wide],
        out_shape=[f(S, 512), b(S, 512), b(nc, DN_HEADS, DN_DIM, DN_DIM), b(S, 512)],
        scratch_shapes=[pltpu.VMEM((DN_HEADS, DN_DIM, DN_DIM), F32)],
        compiler_params=_params(("arbitrary",)),
    )(w, u, qg, kd, a, gcs, pz, gn)


def _dn_scan_bwd(dcat, o, pz, gn, sst, vnew, w, qg, kd, a, gcs, dproj):
    S = o.shape[0]
    C = DN_CHUNK
    G = CHUNKS_SCAN
    R = G * C
    steps = S // R

    def body(dy_ref, o_ref, z_ref, gn_ref, sst_ref, vn_ref, w_ref, qg_ref, kd_ref, a_ref, gcs_ref, _,
             du_ref, dw_ref, dqg_ref, dkd_ref, da_ref, dz_ref, dsc_ref, dgn_ref, dstate):
        @pl.when(pl.program_id(0) == 0)
        def _():
            dstate[...] = jnp.zeros_like(dstate)
            dgn_ref[...] = jnp.zeros_like(dgn_ref)

        gn_ = gn_ref[...]
        lane = lax.broadcasted_iota(jnp.int32, (C, 128), 1)
        row = lax.broadcasted_iota(jnp.int32, (C, 128), 0)
        dgn_parts = []

        def head_program(chunk, h, dsc_parts):
            hs = _head(h)
            rows = slice(chunk * C, (chunk + 1) * C)
            ov, z, dout = o_ref[rows, hs], z_ref[rows, hs], dy_ref[rows, hs]
            r, oh = _rms_stats(ov)
            sg = _sigmoid(z)
            don = dout * (z * sg)
            dz_ref[rows, hs] = _bf(dout * (oh * gn_) * (sg * (1.0 + z * (1.0 - sg))))
            dgn_parts.append(jnp.sum(don * oh, axis=0, keepdims=True))
            dn = don * gn_
            do = _bf(r * (dn - oh * jnp.mean(dn * oh, axis=-1, keepdims=True)))
            sb = sst_ref[chunk, h]
            s_in = sb.astype(F32)
            ds_out = dstate[h]
            dsb = _bf(ds_out)
            vnb = vn_ref[rows, hs]
            wb, qgb, kdb, ab = w_ref[rows, hs], qg_ref[rows, hs], kd_ref[rows, hs], a_ref[h, rows]
            dvn = _dot_tn(ab, do)[:C] + _dot(kdb, dsb)
            yield
            da_ref[h, rows] = _dot_nt(do, _rows_pad(vnb))
            dqg_ref[rows, hs] = _dot_nt(do, sb)
            dkd_ref[rows, hs] = _dot_nt(vnb, dsb)
            q_do = _dot_tn(qgb, do)
            yield
            dvnb = _bf(dvn)
            dw_ref[rows, hs] = _bf(-_dot_nt(dvnb, sb))
            w_dvn = _dot_tn(wb, dvnb)
            du_ref[rows, hs] = dvnb
            yield
            d_last = jnp.exp(gcs_ref[(chunk + 1) * C - 1:(chunk + 1) * C, h:h + 1])
            dd = jnp.sum(jnp.sum(ds_out * s_in, axis=1, keepdims=True), axis=0, keepdims=True)
            dsc_parts.append(jnp.where((lane == h) & (row == C - 1), dd * d_last, 0.0))
            dstate[h] = ds_out * d_last + q_do - w_dvn

        for chunk in reversed(range(G)):
            dsc_parts = []
            _interleave(head_program(chunk, h, dsc_parts) for h in range(DN_HEADS))
            dsc_ref[chunk * C:(chunk + 1) * C, :] = sum(dsc_parts[1:], dsc_parts[0])
        dgn_ref[...] += sum(dgn_parts[1:], dgn_parts[0])

    rev = lambda n: steps - 1 - n
    wide = pl.BlockSpec((R, 512), lambda n: (rev(n), 0))
    z_spec = pl.BlockSpec((R, 512), lambda n: (rev(n), BLK_Z))
    sq = pl.BlockSpec((DN_HEADS, R, CPAD), lambda n: (0, rev(n), 0))
    narrow = pl.BlockSpec((R, 128), lambda n: (rev(n), 0))
    gn_spec = pl.BlockSpec((1, DN_DIM), lambda n: (0, 0))
    f = lambda *shp: jax.ShapeDtypeStruct(shp, F32)
    b = lambda *shp: jax.ShapeDtypeStruct(shp, BF16)
    return pl.pallas_call(
        body, grid=(steps,), name="dn_scan_bwd",
        in_specs=[pl.BlockSpec((R, 512), lambda n: (rev(n), 1)), wide, z_spec, gn_spec,
                  pl.BlockSpec((G, DN_HEADS, DN_DIM, DN_DIM), lambda n: (rev(n), 0, 0, 0)),
                  wide, wide, wide, wide, sq, narrow, pl.BlockSpec(memory_space=pl.ANY)],
        out_specs=[wide, wide, wide, wide, sq, z_spec, narrow, gn_spec],
        out_shape=[b(S, 512), b(S, 512), f(S, 512), f(S, 512), f(DN_HEADS, S, CPAD),
                   jax.ShapeDtypeStruct(dproj.shape, dproj.dtype), f(S, 128), f(1, DN_DIM)],
        scratch_shapes=[pltpu.VMEM((DN_HEADS, DN_DIM, DN_DIM), F32)],
        input_output_aliases={11: 5},
        compiler_params=_params(("arbitrary",)),
    )(dcat, o, pz, gn, sst, vnew, w, qg, kd, a, gcs, dproj)


def _dn_chunk_bwd(qkv, pg, t_inv, gcs, du, dw, dqg, dkd, da, dsc, a_log, dt_bias, dproj):
    S = qkv.shape[0]
    C = DN_CHUNK
    G = CHUNKS_LOCAL
    R = G * C

    def body(alog_ref, dtb_ref, qkv_ref, pg_ref, t_ref, gcs_ref, du_ref, dw_ref, dqg_ref, dkd_ref, da_ref, dsc_ref, _,
             dqkv_ref, dpg_ref, acc_ref):
        @pl.when(pl.program_id(0) == 0)
        def _():
            acc_ref[...] = jnp.zeros_like(acc_ref)

        ii, jj = _chunk_masks()
        lane = lax.broadcasted_iota(jnp.int32, (1, 128), 1)
        row8 = lax.broadcasted_iota(jnp.int32, (8, 128), 0)
        lane8 = lax.broadcasted_iota(jnp.int32, (8, 128), 1)
        rowc = lax.broadcasted_iota(jnp.int32, (C, 1), 0)
        tril, strict = jj <= ii, jj < ii
        dpg_parts, acc_parts = [[] for _ in range(G)], []

        def head_program(chunk, h):
            rows = slice(chunk * C, (chunk + 1) * C)
            q, k, v = qkv_ref[rows, _head(h)], qkv_ref[rows, _head(DN_HEADS + h)], qkv_ref[rows, _head(2 * DN_HEADS + h)]
            gc_col, beta, g_col = gcs_ref[rows, h:h + 1], gcs_ref[rows, DN_HEADS + h:DN_HEADS + h + 1], \
                gcs_ref[rows, 2 * DN_HEADS + h:2 * DN_HEADS + h + 1]
            dec = _decay(gc_col, ii, jj)
            eg = jnp.exp(gc_col)
            g_last = gc_col[C - 1:C, :]
            ek = jnp.exp(g_last - gc_col)
            kb, vb = k * beta, v * beta
            kbg = kb * eg
            qb, kbb = _bf(q), _bf(kb)
            k_rows = _rows_pad(_bf(k))
            t = t_ref[h, rows]
            tb = _bf(t)
            dub, dwb = du_ref[rows, _head(h)], dw_ref[rows, _head(h)]
            dqg_, dkd_ = dqg_ref[rows, _head(h)], dkd_ref[rows, _head(h)]
            dt = _dot_nt(dub, _rows_pad(_bf(vb))) + _dot_nt(dwb, _rows_pad(_bf(kbg)))
            t_du_dw = _dot_tn(tb, jnp.concatenate([dub, dwb], axis=1))
            dvb, dkbg = t_du_dw[:C, :DN_DIM], t_du_dw[:C, DN_DIM:]
            kk = _dot_nt(kbb, k_rows)
            qk = _dot_nt(qb, k_rows)
            yield
            dt_t = _dot3_nt(dt, t)
            yield
            dl = -_dot3_tn(t, dt_t)
            yield
            dm = jnp.where(strict, dl * dec, 0.0)
            dqk = jnp.where(tril, da_ref[h, rows] * dec, 0.0)
            gmat = dm * kk + dqk * qk
            dgc = jnp.sum(gmat, axis=1, keepdims=True) - _row_to_col(jnp.sum(gmat, axis=0, keepdims=True), ii, jj)
            dmb, dqkb = _bf(dm), _bf(dqk)
            yield
            dkb = _dot(dmb, k_rows) + dkbg * eg
            dk = _dot_tn(jnp.concatenate([dmb, dqkb], axis=0), jnp.concatenate([kbb, qb], axis=0))[:C] + dkd_ * ek
            dq = _dot(dqkb, k_rows) + dqg_ * eg
            yield
            tk = jnp.sum(dkd_ * k * ek, axis=1, keepdims=True)
            dgc = dgc + jnp.sum(dqg_ * q * eg, axis=1, keepdims=True) - tk + jnp.sum(dkbg * kbg, axis=1, keepdims=True)
            dgl = jnp.sum(tk, axis=0, keepdims=True) + dsc_ref[(chunk + 1) * C - 1:(chunk + 1) * C, h:h + 1]
            dgc = dgc + jnp.where(rowc == C - 1, dgl, 0.0)
            yield
            dk = dk + dkb * beta
            dbeta = jnp.sum(dkb * k, axis=1, keepdims=True) + jnp.sum(dvb * v, axis=1, keepdims=True)
            dqkv_ref[rows, _head(h)] = dq
            dqkv_ref[rows, _head(DN_HEADS + h)] = dk
            dqkv_ref[rows, _head(2 * DN_HEADS + h)] = dvb * beta
            dg_col = jnp.sum(jnp.where(jj >= ii, _col_to_row(dgc, ii, jj), 0.0), axis=1, keepdims=True)
            yield
            db = dbeta * beta * (1.0 - beta)
            da_in = dg_col * (-jnp.exp(alog_ref[h])) * _sigmoid(pg_ref[rows, DN_HEADS + h:DN_HEADS + h + 1] + dtb_ref[h])
            dpg_parts[chunk].append(jnp.where(lane == h, db, 0.0) + jnp.where(lane == DN_HEADS + h, da_in, 0.0))
            acc_parts.append(jnp.where((row8 == 0) & (lane8 == h), jnp.sum(dg_col * g_col, axis=0, keepdims=True), 0.0)
                             + jnp.where((row8 == 1) & (lane8 == h), jnp.sum(da_in, axis=0, keepdims=True), 0.0))

        _interleave(head_program(chunk, h) for chunk in range(G) for h in range(DN_HEADS))
        for chunk in range(G):
            dpg = sum(dpg_parts[chunk][1:], dpg_parts[chunk][0])
            dpg_ref[chunk * C:(chunk + 1) * C, :] = _bf(jnp.concatenate([dpg, jnp.zeros_like(dpg)], axis=1))
        acc_ref[...] += sum(acc_parts[1:], acc_parts[0])

    smem = pl.BlockSpec(memory_space=pltpu.SMEM)
    wide = pl.BlockSpec((R, 512), lambda n: (n, 0))
    sq = pl.BlockSpec((DN_HEADS, R, CPAD), lambda n: (0, n, 0))
    narrow = pl.BlockSpec((R, 128), lambda n: (n, 0))
    qkv_spec = pl.BlockSpec((R, 1536), lambda n: (n, 0))
    f = lambda *shp: jax.ShapeDtypeStruct(shp, F32)
    return pl.pallas_call(
        body, grid=(S // R,), name="dn_chunk_bwd",
        in_specs=[smem, smem, qkv_spec, pl.BlockSpec((R, 128), lambda n: (n, BLK_G)), sq, narrow, wide, wide, wide, wide, sq,
                  narrow, pl.BlockSpec(memory_space=pl.ANY)],
        out_specs=[qkv_spec, pl.BlockSpec((R, 256), lambda n: (n, BLK_G_PAD)), pl.BlockSpec((8, 128), lambda n: (0, 0))],
        out_shape=[f(S, 1536), jax.ShapeDtypeStruct(dproj.shape, dproj.dtype), f(8, 128)],
        input_output_aliases={12: 1},
        compiler_params=_params(("arbitrary",)),
    )(a_log, dt_bias, qkv, pg, t_inv, gcs, du, dw, dqg, dkd, da, dsc, dproj)


_W_IN_SECTIONS = ((0, 0, 512), (2304, 512, 512), (768, 1024, 1536), (512, 2560, 256), (2816, 2816, 8))
_HALF = D_MODEL // 2
_SECTION_ROWS = 256


def _pack_pairs(x):
    bits = lax.bitcast_convert_type(x, jnp.uint32)
    return lax.bitcast_convert_type(bits[:, _HALF:] | (bits[:, :_HALF] >> 16), F32)


def _unpack_pairs(words):
    bits = lax.bitcast_convert_type(words, jnp.uint32)
    return (lax.bitcast_convert_type(bits << 16, F32),
            lax.bitcast_convert_type(bits & jnp.uint32(0xFFFF0000), F32))


def _w_in_to_internal(packed):
    starts = [dst for _, dst, _ in _W_IN_SECTIONS] + [D_IN_PAD]

    def body(x_hbm, o_ref, words, sems):
        copies = [pltpu.make_async_copy(x_hbm.at[pl.ds(src, rows), 0, :], words.at[pl.ds(dst, rows)], sems.at[i])
                  for i, (src, dst, rows) in enumerate(_W_IN_SECTIONS)]
        for cp in copies:
            cp.start()
        words[pl.ds(D_IN, D_IN_PAD - D_IN), :] = jnp.zeros((D_IN_PAD - D_IN, _HALF), F32)
        for i, cp in enumerate(copies):
            cp.wait()
            for r in range(starts[i], starts[i + 1], _SECTION_ROWS):
                for c, h in enumerate(_unpack_pairs(words[pl.ds(r, _SECTION_ROWS), :])):
                    o_ref[pl.ds(r, _SECTION_ROWS), c * _HALF:(c + 1) * _HALF] = _bf(h)

    assert all((b - a) % _SECTION_ROWS == 0 for a, b in zip(starts, starts[1:])) and starts[-2] + _W_IN_SECTIONS[-1][2] == D_IN
    return pl.pallas_call(body, name="w_in_to_internal", out_shape=jax.ShapeDtypeStruct((D_IN_PAD, D_MODEL), BF16),
                          in_specs=[pl.BlockSpec(memory_space=pl.ANY)],
                          scratch_shapes=[pltpu.VMEM((D_IN_PAD, _HALF), F32), pltpu.SemaphoreType.DMA((len(_W_IN_SECTIONS),))],
                          compiler_params=pltpu.CompilerParams(vmem_limit_bytes=VMEM_LIMIT))(packed)


def _w_in_from_internal(gt):
    def body(g_ref, o_hbm, words, sems):
        copies = []
        for i, (dst, src, rows) in enumerate(_W_IN_SECTIONS):
            for r in range(src, src + rows, _SECTION_ROWS):
                n = max(min(_SECTION_ROWS, src + rows - r), 16)
                words[pl.ds(r, n), :] = _pack_pairs(g_ref[pl.ds(r, n), :].astype(F32))
            copies.append(pltpu.make_async_copy(words.at[pl.ds(src, rows)], o_hbm.at[pl.ds(dst, rows), 0, :], sems.at[i]))
            copies[-1].start()
        for cp in copies:
            cp.wait()

    return pl.pallas_call(body, name="w_in_from_internal", out_shape=jax.ShapeDtypeStruct((D_IN, 1, _HALF), F32),
                          out_specs=pl.BlockSpec(memory_space=pl.ANY),
                          scratch_shapes=[pltpu.VMEM((D_IN_PAD, _HALF), F32), pltpu.SemaphoreType.DMA((len(_W_IN_SECTIONS),))],
                          compiler_params=pltpu.CompilerParams(vmem_limit_bytes=VMEM_LIMIT))(gt)


def _local_step(x, p, target, wts, first_weights, other_weights, ship_early, after):
    S = x.shape[0]
    cos, sin = _rope_tables(S)
    sinks, a_log, dt_bias = wts["sinks"].reshape(8), wts["a_log"].reshape(4), wts["dt_bias"].reshape(4)
    gn = wts["dn_norm"].reshape(1, DN_DIM)
    add = lambda acc, res: (acc + res,)

    u = _rmsnorm_fwd(x, wts["norm_mix"], "norm_mix_fwd", after)
    w_in_t, conv_w = first_weights(u)
    proj, = _mm(u, w_in_t, form="nt", name="in_proj", out_dtypes=[F32], tn=512)
    attn, lse = _attn_fwd(proj, cos, sin, sinks)
    qkv = _dn_prep_fwd(proj, conv_w)
    cw, cu, cqg, ckd, ca, ct, gcs = _dn_chunk_fwd(qkv, proj, a_log, dt_bias)
    o, vnew, sst, dn_out = _dn_scan_fwd(cw, cu, cqg, ckd, ca, gcs, proj, gn)
    w_o, = other_weights(("w_o",), dn_out)
    h1, = _mm([attn, dn_out], w_o, form="nn", name="out_proj", out_dtypes=[F32], tn=512, epi=add, extra=[x])

    w_up, w_down = other_weights(("w_up", "w_down"), h1)
    hid, relu, m, h2 = _mlp_fwd(h1, w_up, w_down, wts["norm_mlp"])
    w_pg, w_pp = other_weights(("w_ple_gate", "w_ple_proj"), h2)
    n3, dh2, dgl, dpp, loss, d_norm_final, d_norm_ple = _ple_and_loss(h2, p, target, w_pg, w_pp, wts["norm_ple"],
                                                                     wts["norm_final"].reshape(1, D_MODEL))
    g = {"norm_final": d_norm_final, "norm_ple": d_norm_ple}
    early = {"w_ple_gate": _mm_tn(n3, dgl, name="d_w_ple_gate", tm=512, tn=1024, out_dtype=BF16).reshape(N_DEV, 128, 1024),
             "w_ple_proj": _mm_tn(p, dpp, name="d_w_ple_proj", tm=256, tn=128, out_dtype=BF16, column_shards=True)}
    d_act, = _mm(dh2, w_down, form="nt", name="d_hidden", out_dtypes=[BF16], tn=512,
                 epi=lambda acc, r: (acc * (2.0 * r.astype(F32)),), extra=[relu])
    early["w_down"] = _mm_tn(hid, dh2, name="d_w_down", tm=512, tn=1024, out_dtype=BF16).reshape(N_DEV, 512, 1024)
    early["w_up"] = _mm_tn(m, d_act, name="d_w_up", tm=1024, tn=512, out_dtype=BF16, column_shards=True)
    token = ship_early(early)
    dh1, g["norm_mlp"], dcat = _mm(d_act, w_up, form="nt", name="d_m", out_dtypes=[F32], tn=512, after=token,
                                   norm_bwd=(h1, wts["norm_mlp"], dh2), then_nt=w_o)
    d_w_o = _mm_tn([attn, dn_out], dh1, name="d_w_o", tm=512, tn=512, out_dtype=BF16)
    token = ship_early({"w_o": d_w_o.reshape(N_DEV, 128, 1024)})
    dproj, dk, dv, dsinks = _attn_bwd(proj, cos, sin, sinks, dcat, attn, lse, token)
    g["sinks"] = dsinks[:, 0].reshape(1, 8)
    du_, dw_, dqg, dkd, da, dproj, dsc, g["dn_norm"] = _dn_scan_bwd(dcat, o, proj, gn, sst, vnew, cw, cqg, ckd, ca, gcs, dproj)
    dqkv, dproj, gate_acc = _dn_chunk_bwd(qkv, proj, ct, gcs, du_, dw_, dqg, dkd, da, dsc, a_log, dt_bias, dproj)
    g["a_log"], g["dt_bias"] = gate_acc[0:1, 0:4], gate_acc[1:2, 0:4]
    dproj, g["conv_w"] = _dn_prep_bwd(proj, conv_w, dqkv, dproj, dk, dv)
    token = ship_early({"w_in": _mm_tn(dproj, u, name="d_w_in", tm=512, tn=1024, out_dtype=BF16)})
    grad_x, g["norm_mix"] = _mm(dproj, w_in_t, form="nn", name="d_u", out_dtypes=[F32], tn=512, after=token,
                                norm_bwd=(x, wts["norm_mix"], dh1))
    return loss, grad_x, g


def _peer(k):
    x, y, c = lax.axis_index("x"), lax.axis_index("y"), lax.axis_index("c")
    px = 1 - x if k & 4 else x
    py = 1 - y if k & 2 else y
    pc = 1 - c if k & 1 else c
    return (px, py, pc), 4 * px + 2 * py + pc


def _exchange(srcs, name, gather):
    n = len(srcs)
    gathers = list(gather) if isinstance(gather, (list, tuple)) else [gather] * n
    shapes = [(N_DEV,) + s.shape if gt else s.shape for s, gt in zip(srcs, gathers)]

    def body(*refs):
        src_refs, out_refs = refs[:n], refs[n:2 * n]
        send_sems, recv_sems, local_sems = refs[2 * n:]
        _, me = _peer(0)
        piece = lambda a, d: src_refs[a] if gathers[a] else src_refs[a].at[d]
        local = [pltpu.make_async_copy(piece(a, me), out_refs[a].at[me], local_sems.at[a]) for a in range(n)]
        for cp in local:
            cp.start()
        copies = []
        for a in range(n):
            for k in range(1, N_DEV):
                dev, idx = _peer(k)
                cp = pltpu.make_async_remote_copy(src_ref=piece(a, idx), dst_ref=out_refs[a].at[me],
                                                  send_sem=send_sems.at[a, k - 1], recv_sem=recv_sems.at[a, k - 1],
                                                  device_id=dev, device_id_type=MESH)
                cp.start()
                copies.append(cp)
        for cp in copies:
            cp.wait_recv()
        for cp in copies:
            cp.wait_send()
        for cp in local:
            cp.wait()

    anywhere = pl.BlockSpec(memory_space=pl.ANY)
    return pl.pallas_call(
        body, name=name, in_specs=[anywhere] * n, out_specs=[anywhere] * n,
        out_shape=[jax.ShapeDtypeStruct(shp, s.dtype) for shp, s in zip(shapes, srcs)],
        scratch_shapes=[pltpu.SemaphoreType.DMA((n, N_DEV - 1)), pltpu.SemaphoreType.DMA((n, N_DEV - 1)),
                        pltpu.SemaphoreType.DMA((n,))],
    )(*srcs)


_HBM = pl.BlockSpec(memory_space=pltpu.HBM)
_SEM = pl.BlockSpec(memory_space=pltpu.SEMAPHORE)
_EFFECT = pltpu.SideEffectType.DATAFLOW_SIDE_EFFECTING


def _split_copies(src_refs, land_refs, send_sems, recv_sems, modes, which=None):
    _, me = _peer(0)
    local, remote = [], []
    which = range(len(src_refs)) if which is None else which
    for a, src, land in zip(which, src_refs, land_refs):
        if modes[a] == "columns":
            n_cols = src.shape[1]
            dst = land.at[:, pl.ds(pl.multiple_of(me * n_cols, n_cols), n_cols)]
        else:
            dst = land.at[me]
        part = lambda d: src.at[d] if modes[a] == "pieces" else src
        local.append(pltpu.make_async_copy(part(me), dst, recv_sems.at[a * N_DEV]))
        for k in ((2, 4, 6) if modes[a] == "chips" else range(1, N_DEV)):
            dev, idx = _peer(k)
            sem = a * N_DEV + k
            remote.append(pltpu.make_async_remote_copy(
                src_ref=part(idx), dst_ref=dst, send_sem=send_sems.at[sem], recv_sem=recv_sems.at[sem],
                device_id=dev, device_id_type=MESH))
    return local, remote


def _forward_copies(land_refs, send_sems, recv_sems):
    c = lax.axis_index("c")
    sibling, _ = _peer(1)
    copies = []
    for a, land in enumerate(land_refs):
        for chip in range(N_DEV // 2):
            slot = 2 * chip + c
            sem = a * (N_DEV // 2) + chip
            copies.append(pltpu.make_async_remote_copy(
                src_ref=land.at[slot], dst_ref=land.at[slot], send_sem=send_sems.at[sem], recv_sem=recv_sems.at[sem],
                device_id=sibling, device_id_type=MESH))
    return copies


def _forward_start(lands, name):
    n = len(lands)

    def body(*refs):
        for cp in _forward_copies(refs[:n], refs[n], refs[n + 1]):
            cp.start()
        refs[-1][...] = jnp.zeros_like(refs[-1])

    sems = pltpu.SemaphoreType.DMA((n * (N_DEV // 2),))
    out = pl.pallas_call(
        body, name=name,
        out_shape=(sems, sems, *[pltpu.HBM(t.shape, t.dtype) for t in lands], jax.ShapeDtypeStruct((8, 128), F32)),
        in_specs=[_HBM] * n, out_specs=(_SEM, _SEM, *[_HBM] * n, pl.BlockSpec(memory_space=pltpu.VMEM)),
        input_output_aliases={i: 2 + i for i in range(n)},
        compiler_params=pltpu.CompilerParams(has_side_effects=_EFFECT),
    )(*[pltpu.with_memory_space_constraint(t, pltpu.HBM) for t in lands])
    return out[:-1], out[-1]


def _forward_wait(handle, after, name):
    send_sems, recv_sems, *lands = handle
    n = len(lands)

    def body(*refs):
        for cp in _forward_copies(refs[:n], refs[n], refs[n + 1]):
            cp.wait_send()
            cp.wait_recv()

    return list(pl.pallas_call(
        body, name=name, out_shape=tuple(pltpu.HBM(t.shape, t.dtype) for t in lands),
        in_specs=[_HBM] * n + [_SEM, _SEM, pl.BlockSpec(memory_space=pl.ANY)], out_specs=tuple([_HBM] * n),
        input_output_aliases={i: i for i in range(n)},
        compiler_params=pltpu.CompilerParams(has_side_effects=_EFFECT),
    )(*lands, send_sems, recv_sems, after))


def _exchange_start(srcs, name, modes):
    n = len(srcs)
    modes = [modes] * n if isinstance(modes, str) else list(modes)
    lands = []
    for s, mode in zip(srcs, modes):
        shape = {"columns": (s.shape[0], N_DEV * s.shape[1]), "pieces": s.shape}.get(mode, (N_DEV,) + s.shape)
        lands.append(lax.empty(shape, s.dtype))

    def body(*refs):
        src_refs, land_refs = refs[:n], refs[n:2 * n]
        send_sems, recv_sems = refs[2 * n], refs[2 * n + 1]
        local, remote = _split_copies(src_refs, land_refs, send_sems, recv_sems, modes)
        for cp in local + remote:
            cp.start()
        refs[-1][...] = jnp.zeros_like(refs[-1])

    both = list(srcs) + lands
    sems = pltpu.SemaphoreType.DMA((n * N_DEV,))
    out = pl.pallas_call(
        body, name=name,
        out_shape=(sems, sems, *[pltpu.HBM(t.shape, t.dtype) for t in both], jax.ShapeDtypeStruct((8, 128), F32)),
        in_specs=[_HBM] * (2 * n), out_specs=(_SEM, _SEM, *[_HBM] * (2 * n), pl.BlockSpec(memory_space=pltpu.VMEM)),
        input_output_aliases={i: 2 + i for i in range(2 * n)},
        compiler_params=pltpu.CompilerParams(has_side_effects=_EFFECT),
    )(*[pltpu.with_memory_space_constraint(t, pltpu.HBM) for t in both])
    return (n, modes, out[:-1]), out[-1]


def _exchange_wait(handle, after, name, which=None):
    n_all, modes, (send_sems, recv_sems, *both_all) = handle
    which = list(range(n_all)) if which is None else list(which)
    n = len(which)
    both = [both_all[a] for a in which] + [both_all[n_all + a] for a in which]

    def body(*refs):
        src_refs, land_refs = refs[:n], refs[n:2 * n]
        local, remote = _split_copies(src_refs, land_refs, refs[2 * n], refs[2 * n + 1], modes, which)
        for cp in local:
            cp.wait()
        for cp in remote:
            cp.wait_send()
            cp.wait_recv()

    out = pl.pallas_call(
        body, name=name, out_shape=tuple(pltpu.HBM(t.shape, t.dtype) for t in both),
        in_specs=[_HBM] * (2 * n) + [_SEM, _SEM, pl.BlockSpec(memory_space=pl.ANY)], out_specs=tuple([_HBM] * (2 * n)),
        input_output_aliases={i: i for i in range(2 * n)},
        compiler_params=pltpu.CompilerParams(has_side_effects=_EFFECT),
    )(*both, send_sems, recv_sems, after)
    return list(out[n:])


def _cast_all(arrays, name, after):
    waits = [] if after is None else [after]

    def body(*refs):
        for src, dst in zip(refs[:len(arrays)], refs[len(arrays) + len(waits):]):
            if len(src.shape) == 2:
                dst[...] = _bf(src[...])
            else:
                dst[:, 0, :] = _pack_pairs(_bf(src[:, 0, :]).astype(F32))

    shapes = [jax.ShapeDtypeStruct(a.shape, BF16) if a.ndim == 2 else jax.ShapeDtypeStruct((a.shape[0], 1, a.shape[2] // 2), F32)
              for a in arrays]
    return pl.pallas_call(body, name=name, out_shape=shapes,
                          compiler_params=pltpu.CompilerParams(vmem_limit_bytes=VMEM_LIMIT))(*arrays, *waits)


def _adam_update(g, w, m, v):
    nm = ADAM_B1 * m + (1.0 - ADAM_B1) * g
    nv = ADAM_B2 * v + (1.0 - ADAM_B2) * (g * g)
    m_hat = nm / (1.0 - ADAM_B1 ** ADAM_STEP)
    v_hat = nv / (1.0 - ADAM_B2 ** ADAM_STEP)
    return -ADAM_LR * (m_hat / (jnp.sqrt(v_hat) + ADAM_EPS) + ADAM_WD * w), nm, nv


def _adamw(parts, w, m, v, name):
    n, R, W = parts.shape
    tm = 128 if R % 128 == 0 else R

    def body(p_ref, w_ref, m_ref, v_ref, g_ref, d_ref, nm_ref, nv_ref):
        g = p_ref[0].astype(F32)
        for s in range(1, n):
            g = g + p_ref[s].astype(F32)
        g_ref[...] = g
        d_ref[...], nm_ref[...], nv_ref[...] = _adam_update(g, w_ref[...], m_ref[...], v_ref[...])

    tile = pl.BlockSpec((tm, W), lambda i: (i, 0))
    return pl.pallas_call(
        body, grid=(R // tm,), name=name,
        in_specs=[pl.BlockSpec((n, tm, W), lambda i: (0, i, 0)), tile, tile, tile],
        out_specs=[tile] * 4, out_shape=[jax.ShapeDtypeStruct((R, W), F32)] * 4,
        compiler_params=_params(("parallel",)),
    )(parts, w, m, v)


def _adamw_rows_apart(parts, w, m, v, name):
    n, R, _, half = parts.shape

    def body(p_hbm, w_hbm, m_hbm, v_hbm, *rest):
        out_hbm, (words, given, results, sems) = rest[:4], rest[4:]
        loads = [pltpu.make_async_copy(p_hbm.at[s, :, 0, :], words.at[s], sems.at[s]) for s in range(n)]
        loads += [pltpu.make_async_copy(h.at[:, 0, :], given.at[i], sems.at[n + i]) for i, h in enumerate((w_hbm, m_hbm, v_hbm))]
        for cp in loads:
            cp.start()
        for cp in loads:
            cp.wait()
        part = lambda s: jnp.concatenate(_unpack_pairs(words[s]), axis=1)
        g = part(0)
        for s in range(1, n):
            g = g + part(s)
        stores = []
        for i, val in enumerate((g,) + _adam_update(g, given[0], given[1], given[2])):
            results[i] = val
            stores.append(pltpu.make_async_copy(results.at[i], out_hbm[i].at[:, 0, :], sems.at[n + 3 + i]))
            stores[-1].start()
        for cp in stores:
            cp.wait()

    anywhere = pl.BlockSpec(memory_space=pl.ANY)
    return pl.pallas_call(
        body, name=name, in_specs=[anywhere] * 4, out_specs=[anywhere] * 4,
        out_shape=[jax.ShapeDtypeStruct(w.shape, F32)] * 4,
        scratch_shapes=[pltpu.VMEM((n, R, half), F32), pltpu.VMEM((3, R, 2 * half), F32), pltpu.VMEM((4, R, 2 * half), F32),
                        pltpu.SemaphoreType.DMA((n + 7,))],
        compiler_params=pltpu.CompilerParams(vmem_limit_bytes=VMEM_LIMIT),
    )(parts, w, m, v)


_MATRICES = ("w_in", "w_o", "w_up", "w_down", "w_ple_gate", "w_ple_proj")


_OTHERS = ("w_o", "w_up", "w_down", "w_ple_gate", "w_ple_proj")
_OTHER_MODES = {"w_o": "slots", "w_up": "slots", "w_down": "slots", "w_ple_gate": "slots", "w_ple_proj": "columns"}


_VECTORS = ("norm_mix", "norm_mlp", "norm_ple", "norm_final", "a_log", "dt_bias", "sinks", "dn_norm")
_SMALL_ROWS, _LOSS_ROW, _CONV_ROW = 16, 8, 9


def _pack_small(vectors, loss, conv):
    def body(*refs):
        out = refs[-1]
        out[...] = jnp.zeros_like(out)
        for r, ref in enumerate(refs[:len(_VECTORS)]):
            out[r:r + 1, 0:ref.shape[1]] = ref[...]
        out[_LOSS_ROW:_LOSS_ROW + 1, 0:128] = refs[len(_VECTORS)][...]
        out[_CONV_ROW:_CONV_ROW + 6, :] = refs[len(_VECTORS) + 1][...]

    return pl.pallas_call(body, name="pack_small", out_shape=jax.ShapeDtypeStruct((_SMALL_ROWS, 1024), F32))(*vectors, loss, conv)


def _sum_slots(parts):
    def body(p_ref, o_ref):
        acc = p_ref[0]
        for s in range(1, parts.shape[0]):
            acc = acc + p_ref[s]
        o_ref[...] = acc

    return pl.pallas_call(body, name="sum_small", out_shape=jax.ShapeDtypeStruct(parts.shape[1:], parts.dtype))(parts)


def _adamw_vectors(summed, conv_g, wmv):
    names = _VECTORS + ("conv_w",)
    flat = [a for triple in wmv for a in triple]

    def body(*refs):
        sum_ref, conv_ref = refs[0], refs[1]
        ins, outs = refs[2:2 + len(flat)], refs[2 + len(flat):]
        for i in range(len(names)):
            w_ref, m_ref, v_ref = ins[3 * i:3 * i + 3]
            g = conv_ref[...] if i == len(_VECTORS) else sum_ref[i:i + 1, 0:w_ref.shape[1]]
            outs[4 * i][...] = g
            outs[4 * i + 1][...], outs[4 * i + 2][...], outs[4 * i + 3][...] = _adam_update(g, w_ref[...], m_ref[...], v_ref[...])

    out_shape = [jax.ShapeDtypeStruct(t[0].shape, F32) for t in wmv for _ in range(4)]
    res = pl.pallas_call(body, name="adamw_vectors", out_shape=out_shape)(summed, conv_g, *flat)
    return {n: res[4 * i:4 * i + 4] for i, n in enumerate(names)}


_ORDER = ("norm_mix", "w_in", "conv_w", "a_log", "dt_bias", "dn_norm", "sinks", "w_o", "norm_mlp", "w_up", "w_down",
          "norm_ple", "w_ple_gate", "w_ple_proj", "norm_final")


def kernel(x, p, norm_mix, w_in, conv_w, a_log, dt_bias, dn_norm, sinks, w_o, norm_mlp, w_up, w_down, norm_ple, w_ple_gate, w_ple_proj, norm_final, loss_target, m_norm_mix, m_w_in, m_conv_w, m_a_log, m_dt_bias, m_dn_norm, m_sinks, m_w_o, m_norm_mlp, m_w_up, m_w_down, m_norm_ple, m_w_ple_gate, m_w_ple_proj, m_norm_final, v_norm_mix, v_w_in, v_conv_w, v_a_log, v_dt_bias, v_dn_norm, v_sinks, v_w_o, v_norm_mlp, v_w_up, v_w_down, v_norm_ple, v_w_ple_gate, v_w_ple_proj, v_norm_final):
    w = dict(norm_mix=norm_mix, w_in=w_in, conv_w=conv_w[0], a_log=a_log, dt_bias=dt_bias, dn_norm=dn_norm, sinks=sinks,
             w_o=w_o[0], norm_mlp=norm_mlp, w_up=w_up[0], w_down=w_down[0], norm_ple=norm_ple, w_ple_gate=w_ple_gate[0],
             w_ple_proj=w_ple_proj[0], norm_final=norm_final)
    m = dict(norm_mix=m_norm_mix, w_in=m_w_in, conv_w=m_conv_w[0], a_log=m_a_log, dt_bias=m_dt_bias, dn_norm=m_dn_norm,
             sinks=m_sinks, w_o=m_w_o[0], norm_mlp=m_norm_mlp, w_up=m_w_up[0], w_down=m_w_down[0], norm_ple=m_norm_ple,
             w_ple_gate=m_w_ple_gate[0], w_ple_proj=m_w_ple_proj[0], norm_final=m_norm_final)
    v = dict(norm_mix=v_norm_mix, w_in=v_w_in, conv_w=v_conv_w[0], a_log=v_a_log, dt_bias=v_dt_bias, dn_norm=v_dn_norm,
             sinks=v_sinks, w_o=v_w_o[0], norm_mlp=v_norm_mlp, w_up=v_w_up[0], w_down=v_w_down[0], norm_ple=v_norm_ple,
             w_ple_gate=v_w_ple_gate[0], w_ple_proj=v_w_ple_proj[0], norm_final=v_norm_final)
    me = 4 * lax.axis_index("x") + 2 * lax.axis_index("y") + lax.axis_index("c")
    conv_shard = conv_w.shape[2]

    for d in (w, m, v):
        d["w_in"] = jnp.transpose(d["w_in"], (2, 0, 1))
    conv_pad = jnp.pad(w["conv_w"], ((0, 8 - DN_CONV), (0, 256 - conv_shard)))
    w_in_shard, = _cast_all([w["w_in"]], "cast_w_in", None)
    gathers_first, token_first = _exchange_start([w_in_shard, conv_pad], "gather_first_start", "chips")
    shards = _cast_all([w[n] for n in _OTHERS], "cast_others", token_first)
    gathers, token_gather = _exchange_start(list(shards), "gather_start", [_OTHER_MODES[n] for n in _OTHERS])

    def first_weights(after):
        over_ici = _exchange_wait(gathers_first, after, "gather_first_wait")
        handle, token = _forward_start(over_ici, "gather_first_forward")
        w_in_all, conv_all = _forward_wait(handle, token, "gather_first_forward_wait")
        conv_all = jnp.transpose(conv_all[:, :DN_CONV, :conv_shard], (1, 0, 2)).reshape(DN_CONV, N_DEV * conv_shard)
        return _w_in_to_internal(w_in_all.reshape(D_IN, 1, _HALF)), conv_all

    as_taken = {"w_o": lambda t: t.reshape(1024, 1024), "w_up": lambda t: t, "w_down": lambda t: t.reshape(4096, 1024),
                "w_ple_gate": lambda t: t.reshape(1024, 1024), "w_ple_proj": lambda t: t}

    def other_weights(names, after):
        which = [_OTHERS.index(n) for n in names]
        got = _exchange_wait(gathers, after, "gather_wait_" + names[0], which)
        return [as_taken[n](t) for n, t in zip(names, got)]

    shipped = []

    def ship_early(pieces):
        names = tuple(pieces)
        if names == ("w_in",):
            pieces = {"w_in": _w_in_from_internal(pieces["w_in"]).reshape(N_DEV, D_IN // N_DEV, 1, _HALF)}
        handle, token = _exchange_start([pieces[n] for n in names], "scatter_start_" + names[0], "pieces")
        shipped.append((names, handle))
        return token

    loss, grad_x, g = _local_step(x[0], p[0, 0], loss_target[0], w, first_weights, other_weights, ship_early, token_gather)

    row = lambda t: t.reshape(1, t.size)
    small = _pack_small([row(g[n]) for n in _VECTORS], loss, g["conv_w"].reshape(6, 1024))
    small_handle, token_small = _exchange_start([small], "gather_small_start", "slots")
    big, after = {}, token_small
    for names, handle in shipped[:-1]:
        for n, r in zip(names, _exchange_wait(handle, after, "scatter_wait_" + names[0])):
            big[n] = _adamw(r, w[n], m[n], v[n], "adamw_" + n)
            after = big[n][1]
    small_all, = _exchange_wait(small_handle, after, "gather_small_wait")
    summed = _sum_slots(small_all)
    conv_g = lax.dynamic_slice(summed[_CONV_ROW:_CONV_ROW + 6].reshape(DN_CONV, N_DEV * conv_shard), (0, me * conv_shard),
                               (DN_CONV, conv_shard))
    small_out = _adamw_vectors(summed, conv_g, [(row(w[n]), row(m[n]), row(v[n])) for n in _VECTORS]
                               + [(w["conv_w"], m["conv_w"], v["conv_w"])])
    names, handle = shipped[-1]
    for n, r in zip(names, _exchange_wait(handle, small_out["conv_w"][0], "scatter_wait_" + names[0])):
        big[n] = _adamw_rows_apart(r, w[n], m[n], v[n], "adamw_" + n)

    result = [summed[_LOSS_ROW, 0], grad_x[None]]
    for i in range(4):
        for n in _ORDER:
            if n == "w_in":
                result.append(jnp.transpose(big[n][i], (1, 2, 0)))
            elif n in _MATRICES:
                result.append(big[n][i][None])
            elif n == "conv_w":
                result.append(small_out[n][i][None])
            else:
                result.append(small_out[n][i].reshape(w[n].shape))
    return tuple(result)
```

```python
import jax
import jax.numpy as jnp
import numpy as np
from jax import lax
from jax.experimental import pallas as pl
from jax.experimental.pallas import tpu as pltpu

F32, BF16 = jnp.float32, jnp.bfloat16
EPS = 1e-6
D_MODEL = 1024
N_DEV = 8
ATTN_BLOCK = 128
HEAD_PAIR = 128
DN_HEADS = 4
DN_DIM = 128
DN_CHUNK = 64
DN_CONV = 4
ROPE_THETA = 10000.0
D_IN = 2824
D_IN_PAD = 3072
BLK_Q, BLK_Z = 0, 1
BLK_DN, BLK_K, BLK_V, BLK_G = 8, 20, 21, 22
BLK_G_PAD = 11
VMEM_LIMIT = 56 * 1024 * 1024
NEG = -1e30
ADAM_LR, ADAM_B1, ADAM_B2, ADAM_EPS, ADAM_WD, ADAM_STEP = 0.001, 0.9, 0.999, 1e-08, 0.01, 10
MESH = pl.DeviceIdType.MESH


def _bf(x):
    return x.astype(BF16)


def _dot(a, b):
    return jnp.dot(a, b, preferred_element_type=F32)


def _dot_nt(a, b):
    return lax.dot_general(a, b, (((1,), (1,)), ((), ())), preferred_element_type=F32)


def _dot_tn(a, b):
    return lax.dot_general(a, b, (((0,), (0,)), ((), ())), preferred_element_type=F32)


def _sigmoid(x):
    return 1.0 / (1.0 + jnp.exp(-x))


def _params(sem):
    return pltpu.CompilerParams(dimension_semantics=sem, vmem_limit_bytes=VMEM_LIMIT)


def _mm(x, w, *, form, name, out_dtypes, tn, epi=None, extra=(), tm=512, w_row_block=0, after=None, norm=None,
        norm_bwd=None, then_nt=None):
    assert norm is None or norm_bwd is None
    xs = list(x) if isinstance(x, (list, tuple)) else [x]
    nx = len(xs)
    S, K = xs[0].shape
    shards = w.ndim == 3
    N = (w.shape[2] * N_DEV if shards else w.shape[1]) if form == "nn" else w.shape[-2]
    assert not (shards and form == "nn" and tn != w.shape[2]) and (nx == 1 or (form == "nn" and not shards and norm is None))
    r0 = w_row_block * K
    tm = min(tm, S)
    n_extra, n_out = len(extra), len(out_dtypes)
    tile = lambda width: pl.BlockSpec((tm, width), lambda i: (i, 0))
    whole = lambda a: pl.BlockSpec(a.shape, lambda i, nd=a.ndim: (0,) * nd)
    ins, in_specs = [*xs, w, *extra], [tile(K)] * nx + [whole(w)] + [tile(N)] * n_extra
    if norm is not None:
        ins, in_specs = ins + [norm], in_specs + [whole(norm)]
    if norm_bwd is not None:
        ins, in_specs = ins + list(norm_bwd), in_specs + [tile(N), whole(norm_bwd[1]), tile(N)]
    if then_nt is not None:
        ins, in_specs = ins + [then_nt], in_specs + [whole(then_nt)]
    if after is not None:
        ins, in_specs = ins + [after], in_specs + [whole(after)]
    out_shape = [jax.ShapeDtypeStruct((S, N), dt) for dt in out_dtypes]
    out_specs = [tile(N)] * n_out
    if norm is not None:
        out_shape, out_specs = out_shape + [jax.ShapeDtypeStruct((S, K), BF16)], out_specs + [tile(K)]
    if norm_bwd is not None:
        out_shape, out_specs = out_shape + [jax.ShapeDtypeStruct((1, N), F32)], out_specs + [pl.BlockSpec((1, N), lambda i: (0, 0))]
    if then_nt is not None:
        out_shape, out_specs = out_shape + [jax.ShapeDtypeStruct((S, then_nt.shape[0]), F32)], out_specs + [tile(then_nt.shape[0])]

    def product(xb, w_ref, cols, c):
        if form == "nn" and nx > 1:
            return sum(_dot(part, w_ref[r0 + p * K:r0 + (p + 1) * K, cols]) for p, part in enumerate(xb))
        if form == "nn":
            return _dot(xb, w_ref[c] if shards else w_ref[r0:r0 + K, cols])
        if not shards:
            return _dot_nt(xb, w_ref[cols, :])
        ks = w.shape[2]
        acc = _dot_nt(xb[:, 0:ks], w_ref[0, cols, :])
        for s in range(1, N_DEV):
            acc = acc + _dot_nt(xb[:, s * ks:(s + 1) * ks], w_ref[s, cols, :])
        return acc

    def body(*refs):
        x_ref, w_ref = refs[0], refs[nx]
        extra_refs = refs[nx + 1:nx + 1 + n_extra]
        at = nx + 1 + n_extra
        if norm is not None:
            gain_ref, at = refs[at], at + 1
        if norm_bwd is not None:
            (y_ref, ygain_ref, dres_ref), at = refs[at:at + 3], at + 3
        if then_nt is not None:
            w2_ref, at = refs[at], at + 1
        outs = refs[len(ins):]
        if norm is not None:
            _, xh = _rms_stats(x_ref[...])
            xb = _bf(xh * gain_ref[...])
            outs[n_out][...] = xb
        else:
            xb = _bf(x_ref[...]) if nx == 1 else [_bf(r[...]) for r in refs[:nx]]
        for c in range(N // tn):
            cols = slice(c * tn, (c + 1) * tn)
            acc = product(xb, w_ref, cols, c)
            res = epi(acc, *[r[:, cols] for r in extra_refs]) if epi else (acc,)
            for o, r in zip(outs[:n_out], res):
                o[:, cols] = r.astype(o.dtype)
        if norm_bwd is not None:
            dx, dg = _rms_bwd_tile(y_ref[...], ygain_ref[...], outs[0][...])
            outs[0][...] = dres_ref[...] + dx
            dg_ref = outs[n_out]

            @pl.when(pl.program_id(0) == 0)
            def _():
                dg_ref[...] = jnp.zeros_like(dg_ref)

            dg_ref[...] += dg
        if then_nt is not None:
            yb = _bf(outs[0][...])
            for c in range(then_nt.shape[0] // tn):
                cols = slice(c * tn, (c + 1) * tn)
                outs[-1][:, cols] = _dot_nt(yb, w2_ref[cols, :])

    return pl.pallas_call(
        body, grid=(S // tm,), name=name, in_specs=in_specs, out_specs=out_specs, out_shape=out_shape,
        compiler_params=_params(("arbitrary",) if norm_bwd is not None else ("parallel",)),
    )(*ins)


def _mlp_fwd(h1, w_up, w_down, gain):
    S, K = h1.shape
    n_sh, _, fs = w_up.shape
    tm = min(512, S)

    def body(x_ref, wup_ref, wdown_ref, g_ref, hid_ref, relu_ref, m_ref, h2_ref):
        x = x_ref[...]
        _, xh = _rms_stats(x)
        mb = _bf(xh * g_ref[...])
        m_ref[...] = mb
        h2_ref[...] = x
        for c in range(n_sh):
            cols = slice(c * fs, (c + 1) * fs)
            r = jnp.maximum(_dot(mb, wup_ref[c]), 0.0)
            hd = _bf(r * r)
            hid_ref[:, cols] = hd
            relu_ref[:, cols] = _bf(r)
            h2_ref[...] += _dot(hd, wdown_ref[cols, :])

    tile = lambda width: pl.BlockSpec((tm, width), lambda i: (i, 0))
    once = lambda a: pl.BlockSpec(a.shape, lambda i, nd=a.ndim: (0,) * nd, pipeline_mode=pl.Buffered(1))
    F = n_sh * fs
    return pl.pallas_call(
        body, grid=(S // tm,), name="mlp_fwd",
        in_specs=[tile(K), once(w_up), once(w_down), pl.BlockSpec(gain.shape, lambda i: (0, 0))],
        out_specs=[tile(F), tile(F), tile(K), tile(K)],
        out_shape=[jax.ShapeDtypeStruct((S, F), BF16), jax.ShapeDtypeStruct((S, F), BF16),
                   jax.ShapeDtypeStruct((S, K), BF16), jax.ShapeDtypeStruct((S, K), F32)],
        compiler_params=_params(("parallel",)),
    )(h1, w_up, w_down, gain)


def _mm_tn(x, dy, *, name, tm, tn, out_dtype=F32, column_shards=False, after=None):
    xs = list(x) if isinstance(x, (list, tuple)) else [x]
    S, N = dy.shape
    K = x.shape[1] if len(xs) == 1 else tm * len(xs)
    waits = [] if after is None else [after]

    def body(*refs):
        dy_ref, out_ref = refs[len(xs)], refs[-1]
        if len(xs) == 1:
            out_ref[...] = _dot_tn(_bf(refs[0][...]), _bf(dy_ref[...])).astype(out_dtype)
        for k in range(len(xs) if len(xs) > 1 else 0):
            @pl.when(pl.program_id(0) == k)
            def _(k=k):
                out_ref[...] = _dot_tn(_bf(refs[k][...]), _bf(dy_ref[...])).astype(out_dtype)

    if column_shards:
        out_spec = pl.BlockSpec((None, tm, tn), lambda i, j: (j, i, 0))
        out_shape = jax.ShapeDtypeStruct((N // tn, K, tn), out_dtype)
    else:
        out_spec = pl.BlockSpec((tm, tn), lambda i, j: (i, j))
        out_shape = jax.ShapeDtypeStruct((K, N), out_dtype)
    return pl.pallas_call(
        body, grid=(K // tm, N // tn), name=name,
        in_specs=([pl.BlockSpec((S, tm), lambda i, j: (0, i))] if len(xs) == 1 else [pl.BlockSpec((S, tm), lambda i, j: (0, 0))] * len(xs))
        + [pl.BlockSpec((S, tn), lambda i, j: (0, j))] + [pl.BlockSpec(memory_space=pl.ANY)] * len(waits),
        out_specs=out_spec, out_shape=out_shape,
        compiler_params=_params(("parallel", "parallel")),
    )(*xs, dy, *waits)


def _rowwise(body, *, tiled, full, out_tiled, out_acc, name, tm=512, smem=()):
    S = tiled[0].shape[0]
    tm = min(tm, S)
    n_in = len(smem) + len(tiled) + len(full)

    def kern(*refs):
        @pl.when(pl.program_id(0) == 0)
        def _():
            for r in refs[n_in + len(out_tiled):]:
                r[...] = jnp.zeros_like(r)
        body(*refs)

    in_specs = [pl.BlockSpec(memory_space=pltpu.SMEM) for _ in smem]
    in_specs += [pl.BlockSpec((tm, a.shape[1]), lambda i: (i, 0)) for a in tiled]
    in_specs += [pl.BlockSpec(a.shape, lambda i, nd=a.ndim: (0,) * nd) for a in full]
    out_specs = [pl.BlockSpec((tm, w), lambda i: (i, 0)) for w, _ in out_tiled]
    out_specs += [pl.BlockSpec(shp, lambda i, nd=len(shp): (0,) * nd) for shp, _ in out_acc]
    out_shape = [jax.ShapeDtypeStruct((S, w), dt) for w, dt in out_tiled]
    out_shape += [jax.ShapeDtypeStruct(shp, dt) for shp, dt in out_acc]
    return pl.pallas_call(
        kern, grid=(S // tm,), name=name, in_specs=in_specs, out_specs=out_specs, out_shape=out_shape,
        compiler_params=_params(("arbitrary",)),
    )(*smem, *tiled, *full)


def _rms_stats(x):
    r = lax.rsqrt(jnp.mean(x * x, axis=-1, keepdims=True) + EPS)
    return r, x * r


def _rmsnorm_fwd(x, g, name, after):
    def body(x_ref, g_ref, _, o_ref):
        _, xh = _rms_stats(x_ref[...])
        o_ref[...] = _bf(xh * g_ref[...])

    return _rowwise(body, tiled=[x], full=[g, after], out_tiled=[(x.shape[1], BF16)], out_acc=[], name=name)[0]


def _rms_bwd_tile(x, g, dxn):
    r, xh = _rms_stats(x)
    dg = jnp.sum(dxn * xh, axis=0, keepdims=True)
    dn = dxn * g
    dx = r * (dn - xh * jnp.mean(dn * xh, axis=-1, keepdims=True))
    return dx, dg


def _ple_and_loss(h2, p, target, w_pg, w_pp, g_ple, g_final):
    S, n = h2.shape
    tm = min(512, S)
    tn = 512

    def body(h2_ref, p_ref, t_ref, wpg_ref, wpp_ref, gple_ref, gfin_ref,
             n3_ref, dh_ref, dgl_ref, dpp_ref, loss_ref, dg_ref, dgple_ref, pp, gate, h3):
        @pl.when(pl.program_id(0) == 0)
        def _():
            loss_ref[...] = jnp.zeros_like(loss_ref)
            dg_ref[...] = jnp.zeros_like(dg_ref)
            dgple_ref[...] = jnp.zeros_like(dgple_ref)

        x = h2_ref[...]
        _, xh = _rms_stats(x)
        n3 = _bf(xh * gple_ref[...])
        n3_ref[...] = n3
        pb = _bf(p_ref[...])
        for c in range(n // tn):
            cols = slice(c * tn, (c + 1) * tn)
            pp[:, cols] = _dot(pb, wpp_ref[:, cols])
            gt = _sigmoid(_dot(n3, wpg_ref[:, cols]))
            gate[:, cols] = gt
            h3[:, cols] = x[:, cols] + gt * pp[:, cols]
        y = h3[...]
        _, yh = _rms_stats(y)
        e = yh * gfin_ref[...] - t_ref[...]
        per_tok = jnp.mean(e * e, axis=-1, keepdims=True)
        loss_ref[...] += 0.5 * jnp.sum(per_tok, axis=0, keepdims=True)
        dh, dg = _rms_bwd_tile(y, gfin_ref[...], e * (1.0 / n))
        dg_ref[...] += dg
        gt = gate[...]
        dgl = _bf(dh * pp[...] * gt * (1.0 - gt))
        dgl_ref[...] = dgl
        dpp_ref[...] = _bf(dh * gt)
        for c in range(n // tn):
            cols = slice(c * tn, (c + 1) * tn)
            h3[:, cols] = _dot_nt(dgl, wpg_ref[cols, :])
        dx, dgp = _rms_bwd_tile(x, gple_ref[...], h3[...])
        dh_ref[...] = dh + dx
        dgple_ref[...] += dgp

    tile = lambda width: pl.BlockSpec((tm, width), lambda i: (i, 0))
    whole = lambda a: pl.BlockSpec(a.shape, lambda i, nd=a.ndim: (0,) * nd)
    return pl.pallas_call(
        body, grid=(S // tm,), name="ple_and_loss",
        in_specs=[tile(n), tile(p.shape[1]), tile(n), whole(w_pg), whole(w_pp), whole(g_ple), whole(g_final)],
        out_specs=[tile(n), tile(n), tile(n), tile(n), pl.BlockSpec((1, 128), lambda i: (0, 0)),
                   pl.BlockSpec((1, n), lambda i: (0, 0)), pl.BlockSpec((1, n), lambda i: (0, 0))],
        out_shape=[jax.ShapeDtypeStruct((S, n), BF16), jax.ShapeDtypeStruct((S, n), F32), jax.ShapeDtypeStruct((S, n), BF16),
                   jax.ShapeDtypeStruct((S, n), BF16), jax.ShapeDtypeStruct((1, 128), F32), jax.ShapeDtypeStruct((1, n), F32),
                   jax.ShapeDtypeStruct((1, n), F32)],
        scratch_shapes=[pltpu.VMEM((tm, n), F32)] * 3,
        compiler_params=_params(("arbitrary",)),
    )(h2, p, target, w_pg, w_pp, g_ple, g_final)


def _rope_tables(S):
    half = 32
    inv = (1.0 / (np.float32(ROPE_THETA) ** (np.arange(half, dtype=np.float32) * np.float32(2.0 / 64)))).astype(np.float32)
    ang = np.arange(S).astype(np.float32)[:, None] * inv[None, :]
    cos, sin = np.cos(ang), np.sin(ang)
    return jnp.asarray(np.tile(cos, (1, 4))), jnp.asarray(np.concatenate([-sin, sin, -sin, sin], axis=1))


def _attn_common(i, kc, kp, vc, vp, cc, sc, cp, sp):
    lane = lax.broadcasted_iota(jnp.int32, (1, HEAD_PAIR), 1)
    lane_lo = jnp.bitwise_and(lane, 63) < 32
    slot = [lane < 64, lane >= 64]

    def swap_halves(t):
        return jnp.where(lane_lo, pltpu.roll(t, 96, 1), pltpu.roll(t, 32, 1))

    def rope(t, cos, sin):
        return t * cos + swap_halves(t) * sin

    def unrope(d, cos, sin):
        return d * cos + swap_halves(d * sin)

    k2 = jnp.concatenate([rope(kp, cp, sp), rope(kc, cc, sc)], axis=0)
    v2 = jnp.concatenate([vp, vc], axis=0)
    r = lax.broadcasted_iota(jnp.int32, (ATTN_BLOCK, 2 * ATTN_BLOCK), 0)
    c = lax.broadcasted_iota(jnp.int32, (ATTN_BLOCK, 2 * ATTN_BLOCK), 1)
    valid = (c > r) & (c <= r + ATTN_BLOCK) & jnp.logical_or(c >= ATTN_BLOCK, i > 0)
    ks, vs = {}, {}
    for j in range(2):
        kn = jnp.where(slot[j], k2, 0.0)
        vn = jnp.where(slot[j], v2, 0.0)
        for s in range(2):
            ks[j, s] = _bf(kn if s == j else pltpu.roll(kn, 64, 1))
            vs[j, s] = _bf(vn if s == j else pltpu.roll(vn, 64, 1))
    return slot, rope, unrope, valid, ks, vs


def _attn_probs(scores, valid, sink):
    s = jnp.where(valid, scores * 0.125, NEG)
    m = jnp.maximum(jnp.max(s, axis=1, keepdims=True), sink)
    e = jnp.exp(s - m)
    z = jnp.sum(e, axis=1, keepdims=True) + jnp.exp(sink - m)
    return e * (1.0 / z), m + jnp.log(z)


def _attn_specs(S):
    nb = S // ATTN_BLOCK
    prev = lambda i: jnp.maximum(i - 1, 0)
    blk = lambda w, col, row=(lambda i: i): pl.BlockSpec((ATTN_BLOCK, w), lambda i: (row(i), col))
    in_specs = [pl.BlockSpec(memory_space=pltpu.SMEM),
                blk(512, BLK_Q), blk(128, BLK_K), blk(128, BLK_K, prev), blk(128, BLK_V), blk(128, BLK_V, prev),
                blk(128, 0), blk(128, 0), blk(128, 0, prev), blk(128, 0, prev)]
    return nb, in_specs


def _attn_fwd(pa, cos, sin, sinks):
    S = pa.shape[0]
    nb, in_specs = _attn_specs(S)

    def body(sinks_ref, q_ref, kc_ref, kp_ref, vc_ref, vp_ref, cc_ref, sc_ref, cp_ref, sp_ref, o_ref, lse_ref):
        i = pl.program_id(0)
        lane = lax.broadcasted_iota(jnp.int32, (1, HEAD_PAIR), 1)
        cc, sc = cc_ref[...], sc_ref[...]
        _, rope, _, valid, ks, vs = _attn_common(i, kc_ref[...], kp_ref[...], vc_ref[...], vp_ref[...],
                                                 cc, sc, cp_ref[...], sp_ref[...])
        pair_cols = [slice(HEAD_PAIR * pair, HEAD_PAIR * (pair + 1)) for pair in range(4)]
        qps = [_bf(rope(q_ref[:, cols], cc, sc)) for cols in pair_cols]
        outs, lses = {}, {}

        def head_program(h):
            pair, s = divmod(h, 2)
            j = h // 4
            scores = _dot_nt(qps[pair], ks[j, s])
            yield
            p, lse = _attn_probs(scores, valid, sinks_ref[h])
            outs[h] = _dot(_bf(p), vs[j, s])
            lses[h] = jnp.where(lane == h, lse, 0.0)

        _interleave(head_program(h) for h in range(8))
        for pair, cols in enumerate(pair_cols):
            o_ref[:, cols] = outs[2 * pair] + outs[2 * pair + 1]
        lse_ref[...] = sum((lses[h] for h in range(1, 8)), lses[0])

    return pl.pallas_call(
        body, grid=(nb,), name="attn_fwd", in_specs=in_specs,
        out_specs=[pl.BlockSpec((ATTN_BLOCK, 512), lambda i: (i, 0)), pl.BlockSpec((ATTN_BLOCK, 128), lambda i: (i, 0))],
        out_shape=[jax.ShapeDtypeStruct((S, 512), F32), jax.ShapeDtypeStruct((S, 128), F32)],
        compiler_params=_params(("parallel",)),
    )(sinks, pa, pa, pa, pa, pa, cos, sin, cos, sin)


def _attn_bwd(pa, cos, sin, sinks, dcat, attn, lse, after):
    S = pa.shape[0]
    nb, in_specs = _attn_specs(S)
    in_specs = in_specs + [pl.BlockSpec((ATTN_BLOCK, 512), lambda i: (i, 0))] * 2 + [pl.BlockSpec((ATTN_BLOCK, 128), lambda i: (i, 0))]
    in_specs = in_specs + [pl.BlockSpec(memory_space=pl.ANY)]

    def body(sinks_ref, q_ref, kc_ref, kp_ref, vc_ref, vp_ref, cc_ref, sc_ref, cp_ref, sp_ref, do_ref, o_ref, lse_ref, _,
             dq_ref, dk_ref, dv_ref, dsink_ref):
        i = pl.program_id(0)

        @pl.when(i == 0)
        def _():
            dk_ref[...] = jnp.zeros_like(dk_ref)
            dv_ref[...] = jnp.zeros_like(dv_ref)
            dsink_ref[...] = jnp.zeros_like(dsink_ref)

        cc, sc, cp, sp = cc_ref[...], sc_ref[...], cp_ref[...], sp_ref[...]
        slot, rope, unrope, valid, ks, vs = _attn_common(i, kc_ref[...], kp_ref[...], vc_ref[...], vp_ref[...], cc, sc, cp, sp)
        pair_cols = [slice(HEAD_PAIR * pair, HEAD_PAIR * (pair + 1)) for pair in range(4)]
        qps = [_bf(rope(q_ref[:, cols], cc, sc)) for cols in pair_cols]
        dobs = [_bf(do_ref[:, cols]) for cols in pair_cols]
        do_o = [do_ref[:, cols] * o_ref[:, cols] for cols in pair_cols]
        dqs, dks, dvs = {}, {}, {}

        def head_program(h):
            pair, s = divmod(h, 2)
            j = h // 4
            qp, dob = qps[pair], dobs[pair]
            scores = _dot_nt(qp, ks[j, s])
            dp = _dot_nt(dob, vs[j, s])
            yield
            lse_h = lse_ref[:, h:h + 1]
            p = jnp.exp(jnp.where(valid, scores * 0.125, NEG) - lse_h)
            yield
            dr = jnp.sum(jnp.where(slot[s], do_o[pair], 0.0), axis=1, keepdims=True)
            ds = _bf(p * (dp - dr) * 0.125)
            yield
            dsink_ref[h:h + 1, :] += -jnp.sum(jnp.exp(sinks_ref[h] - lse_h) * dr, axis=0, keepdims=True)
            dqs[h] = _dot(ds, ks[j, s])
            dk_h = _dot_tn(ds, qp)
            dv_h = _dot_tn(_bf(p), dob)
            yield
            dk_h, dv_h = jnp.where(slot[s], dk_h, 0.0), jnp.where(slot[s], dv_h, 0.0)
            if s != j:
                dk_h, dv_h = pltpu.roll(dk_h, 64, 1), pltpu.roll(dv_h, 64, 1)
            dks[h], dvs[h] = dk_h, dv_h

        _interleave(head_program(h) for h in range(8))
        dk2 = sum((dks[h] for h in range(1, 8)), dks[0])
        dv2 = sum((dvs[h] for h in range(1, 8)), dvs[0])
        for pair, cols in enumerate(pair_cols):
            dq_ref[:, cols] = _bf(unrope(dqs[2 * pair] + dqs[2 * pair + 1], cc, sc))
        cur = pl.ds(pl.multiple_of(i * ATTN_BLOCK, ATTN_BLOCK), ATTN_BLOCK)
        dk_ref[cur, :] += unrope(dk2[ATTN_BLOCK:], cc, sc)
        dv_ref[cur, :] += dv2[ATTN_BLOCK:]

        @pl.when(i > 0)
        def _():
            prv = pl.ds(pl.multiple_of((i - 1) * ATTN_BLOCK, ATTN_BLOCK), ATTN_BLOCK)
            dk_ref[prv, :] += unrope(dk2[:ATTN_BLOCK], cp, sp)
            dv_ref[prv, :] += dv2[:ATTN_BLOCK]

    whole = lambda w: pl.BlockSpec((S, w), lambda i: (0, 0))
    return pl.pallas_call(
        body, grid=(nb,), name="attn_bwd", in_specs=in_specs,
        out_specs=[pl.BlockSpec((ATTN_BLOCK, 512), lambda i: (i, BLK_Q)), whole(128), whole(128),
                   pl.BlockSpec((8, 128), lambda i: (0, 0))],
        out_shape=[jax.ShapeDtypeStruct((S, D_IN_PAD), BF16), jax.ShapeDtypeStruct((S, 128), F32),
                   jax.ShapeDtypeStruct((S, 128), F32), jax.ShapeDtypeStruct((8, 128), F32)],
        compiler_params=_params(("arbitrary",)),
    )(sinks, pa, pa, pa, pa, pa, cos, sin, cos, sin, dcat, attn, lse, after)


CONV_ROWS = 512
CONV_PAD = 8


def _conv_silu(scr, w, r0):
    y = w[3:4, :] * scr[pl.ds(CONV_PAD + r0, CONV_ROWS), :]
    for j in range(DN_CONV - 1):
        y = y + w[j:j + 1, :] * scr[pl.ds(CONV_PAD + r0 - 3 + j, CONV_ROWS), :]
    return y


def _dn_prep_fwd(pd, conv_w):
    S = pd.shape[0]
    assert S % CONV_ROWS == 0

    def body(x_ref, w_ref, o_ref, scr):
        b = pl.program_id(0)
        scr[0:CONV_PAD, :] = jnp.zeros((CONV_PAD, DN_DIM), F32)
        scr[pl.ds(CONV_PAD, S), :] = x_ref[...]
        w = w_ref[...]
        q_scale = jnp.where(b < DN_HEADS, DN_DIM ** -0.5, 1.0)
        for r0 in range(0, S, CONV_ROWS):
            y = _conv_silu(scr, w, r0)
            a = y * _sigmoid(y)
            rs = lax.rsqrt(jnp.sum(a * a, axis=1, keepdims=True) + EPS)
            o_ref[pl.ds(r0, CONV_ROWS), :] = a * jnp.where(b < 2 * DN_HEADS, rs * q_scale, 1.0)

    col = pl.BlockSpec((S, DN_DIM), lambda b: (0, b))
    return pl.pallas_call(
        body, grid=(3 * DN_HEADS,), name="dn_prep_fwd",
        in_specs=[pl.BlockSpec((S, DN_DIM), lambda b: (0, BLK_DN + b)), pl.BlockSpec((DN_CONV, DN_DIM), lambda b: (0, b))],
        out_specs=col,
        out_shape=jax.ShapeDtypeStruct((S, 3 * DN_HEADS * DN_DIM), F32),
        scratch_shapes=[pltpu.VMEM((S + CONV_PAD, DN_DIM), F32)],
        compiler_params=_params(("parallel",)),
    )(pd, conv_w)


def _dn_prep_bwd(pd, conv_w, dqkv, dproj, dk, dv):
    S = pd.shape[0]
    NB = 3 * DN_HEADS

    def body(x_ref, w_ref, d_ref, _, dk_ref, dv_ref, dx_ref, dw_ref, scr, dscr):
        b = pl.program_id(0)

        @pl.when(b == NB)
        def _():
            dx_ref[...] = _bf(dk_ref[...])

        @pl.when(b == NB + 1)
        def _():
            dx_ref[...] = _bf(dv_ref[...])

        @pl.when(b < NB)
        def _():
            scr[0:CONV_PAD, :] = jnp.zeros((CONV_PAD, DN_DIM), F32)
            scr[pl.ds(CONV_PAD, S), :] = x_ref[...]
            dscr[pl.ds(S, CONV_PAD), :] = jnp.zeros((CONV_PAD, DN_DIM), F32)
            w = w_ref[...]
            q_scale = jnp.where(b < DN_HEADS, DN_DIM ** -0.5, 1.0)
            is_qk = b < 2 * DN_HEADS
            dw = [jnp.zeros((1, DN_DIM), F32) for _ in range(DN_CONV)]
            for r0 in range(0, S, CONV_ROWS):
                y = _conv_silu(scr, w, r0)
                sg = _sigmoid(y)
                a = y * sg
                dout = d_ref[pl.ds(r0, CONV_ROWS), :]
                rs = lax.rsqrt(jnp.sum(a * a, axis=1, keepdims=True) + EPS)
                da_qk = q_scale * rs * (dout - a * (rs * rs) * jnp.sum(dout * a, axis=1, keepdims=True))
                dy = jnp.where(is_qk, da_qk, dout) * (sg * (1.0 + y * (1.0 - sg)))
                dscr[pl.ds(r0, CONV_ROWS), :] = dy
                for j in range(DN_CONV):
                    dw[j] = dw[j] + jnp.sum(dy * scr[pl.ds(CONV_PAD + r0 - 3 + j, CONV_ROWS), :], axis=0, keepdims=True)
            for j in range(DN_CONV):
                dw_ref[j:j + 1, :] = dw[j]
            for r0 in range(0, S, CONV_ROWS):
                dx = w[3:4, :] * dscr[pl.ds(r0, CONV_ROWS), :]
                for j in range(DN_CONV - 1):
                    dx = dx + w[j:j + 1, :] * dscr[pl.ds(r0 + 3 - j, CONV_ROWS), :]
                dx_ref[pl.ds(r0, CONV_ROWS), :] = _bf(dx)

    own = lambda b: jnp.minimum(b, NB - 1)
    col = pl.BlockSpec((S, DN_DIM), lambda b: (0, own(b)))
    proj_col = pl.BlockSpec((S, DN_DIM), lambda b: (0, BLK_DN + own(b)))
    wcol = pl.BlockSpec((DN_CONV, DN_DIM), lambda b: (0, own(b)))
    whole = pl.BlockSpec((S, DN_DIM), lambda b: (0, 0))
    assert BLK_K == BLK_DN + NB and BLK_V == BLK_K + 1
    return pl.pallas_call(
        body, grid=(NB + 2,), name="dn_prep_bwd",
        in_specs=[proj_col, wcol, col, pl.BlockSpec(memory_space=pl.ANY), whole, whole],
        out_specs=[pl.BlockSpec((S, DN_DIM), lambda b: (0, BLK_DN + b)), wcol],
        out_shape=[jax.ShapeDtypeStruct(dproj.shape, dproj.dtype), jax.ShapeDtypeStruct((DN_CONV, 3 * DN_HEADS * DN_DIM), F32)],
        scratch_shapes=[pltpu.VMEM((S + CONV_PAD, DN_DIM), F32), pltpu.VMEM((S + CONV_PAD, DN_DIM), F32)],
        input_output_aliases={3: 0},
        compiler_params=_params(("arbitrary",)),
    )(pd, conv_w, dqkv, dproj, dk, dv)


CPAD = 128
CHUNKS_LOCAL = 8
CHUNKS_SCAN = 8


def _chunk_masks():
    ii = lax.broadcasted_iota(jnp.int32, (DN_CHUNK, CPAD), 0)
    jj = lax.broadcasted_iota(jnp.int32, (DN_CHUNK, CPAD), 1)
    return ii, jj


def _rows_pad(a):
    return jnp.concatenate([a, jnp.zeros_like(a)], axis=0)


def _hi_lo(a):
    hi = _bf(a)
    return hi, _bf(a - hi.astype(F32))


def _double_step(t, p):
    C = DN_CHUNK
    th, tl = _hi_lo(t)
    ph, pl_ = _hi_lo(p)
    r1 = _dot(jnp.concatenate([th, tl, ph, pl_], axis=0), _rows_pad(ph))
    r2 = _dot(jnp.concatenate([th, ph], axis=0), _rows_pad(pl_))
    return t + (r1[:C] + r1[C:2 * C] + r2[:C]), r1[2 * C:3 * C] + r1[3 * C:] + r2[C:]


def _dot3_nt(a, b):
    C = DN_CHUNK
    ah, al = _hi_lo(a)
    bh, bl = _hi_lo(b)
    r1 = _dot_nt(jnp.concatenate([ah, al], axis=0), _rows_pad(bh))
    return r1[:C] + r1[C:] + _dot_nt(ah, _rows_pad(bl))


def _dot3_tn(a, b):
    C = DN_CHUNK
    ah, al = _hi_lo(a)
    bh, bl = _hi_lo(b)
    return _dot_tn(jnp.concatenate([ah, al, ah], axis=0), jnp.concatenate([bh, bh, bl], axis=0))[:C]


def _interleave(programs):
    programs = list(programs)
    while programs:
        alive = []
        for prog in programs:
            try:
                next(prog)
                alive.append(prog)
            except StopIteration:
                pass
        programs = alive


def _col_to_row(col, ii, jj):
    return jnp.sum(jnp.where(ii == jj, col, 0.0), axis=0, keepdims=True)


def _row_to_col(row, ii, jj):
    return jnp.sum(jnp.where(ii == jj, row, 0.0), axis=1, keepdims=True)


def _decay(gc_col, ii, jj):
    diff = gc_col - _col_to_row(gc_col, ii, jj)
    return jnp.where(jj <= ii, jnp.exp(jnp.where(jj <= ii, diff, 0.0)), 0.0)


def _softplus(x):
    return jnp.maximum(x, 0.0) + jnp.log(1.0 + jnp.exp(-jnp.abs(x)))


def _head(h):
    return slice(DN_DIM * h, DN_DIM * (h + 1))


def _dn_chunk_fwd(qkv, pg, a_log, dt_bias):
    S = qkv.shape[0]
    C = DN_CHUNK
    G = CHUNKS_LOCAL
    R = G * C
    steps = S // R

    def body(alog_ref, dtb_ref, qkv_ref, pg_ref, w_ref, u_ref, qg_ref, kd_ref, a_ref, t_ref, gcs_ref):
        ii, jj = _chunk_masks()
        lane = lax.broadcasted_iota(jnp.int32, (1, 128), 1)
        eye = (ii == jj).astype(F32)
        gcs_parts = [[] for _ in range(G)]

        def head_program(chunk, h):
            rows = slice(chunk * C, (chunk + 1) * C)
            q, k, v = qkv_ref[rows, _head(h)], qkv_ref[rows, _head(DN_HEADS + h)], qkv_ref[rows, _head(2 * DN_HEADS + h)]
            beta = _sigmoid(pg_ref[rows, h:h + 1])
            g_col = -jnp.exp(alog_ref[h]) * _softplus(pg_ref[rows, DN_HEADS + h:DN_HEADS + h + 1] + dtb_ref[h])
            g_row = _col_to_row(g_col, ii, jj)
            gc_col = jnp.sum(jnp.where(jj <= ii, g_row, 0.0), axis=1, keepdims=True)
            dec = _decay(gc_col, ii, jj)
            eg = jnp.exp(gc_col)
            kb, vb = k * beta, v * beta
            k_rows = _rows_pad(_bf(k))
            kk = _dot_nt(_bf(kb), k_rows)
            qk = _dot_nt(_bf(q), k_rows)
            yield
            t, pw = eye, -jnp.where(jj < ii, kk * dec, 0.0)
            for _ in range(6):
                t, pw = _double_step(t, pw)
                yield
            tb = _bf(t)
            u_ref[rows, _head(h)] = _dot(tb, _rows_pad(_bf(vb)))
            w_ref[rows, _head(h)] = _bf(_dot(tb, _rows_pad(_bf(kb * eg))))
            a_ref[h, rows] = _bf(qk * dec)
            t_ref[h, rows] = t
            qg_ref[rows, _head(h)] = _bf(q * eg)
            kd_ref[rows, _head(h)] = _bf(k * jnp.exp(gc_col[C - 1:C, :] - gc_col))
            gcs_parts[chunk].append(jnp.where(lane == h, gc_col, 0.0) + jnp.where(lane == DN_HEADS + h, beta, 0.0)
                                    + jnp.where(lane == 2 * DN_HEADS + h, g_col, 0.0))

        _interleave(head_program(chunk, h) for chunk in range(G) for h in range(DN_HEADS))
        for chunk in range(G):
            gcs_ref[chunk * C:(chunk + 1) * C, :] = sum(gcs_parts[chunk][1:], gcs_parts[chunk][0])

    smem = pl.BlockSpec(memory_space=pltpu.SMEM)
    wide = pl.BlockSpec((R, 512), lambda n: (n, 0))
    sq = pl.BlockSpec((DN_HEADS, R, CPAD), lambda n: (0, n, 0))
    narrow = pl.BlockSpec((R, 128), lambda n: (n, 0))
    f = lambda *shp: jax.ShapeDtypeStruct(shp, F32)
    b = lambda *shp: jax.ShapeDtypeStruct(shp, BF16)
    return pl.pallas_call(
        body, grid=(steps,), name="dn_chunk_fwd",
        in_specs=[smem, smem, pl.BlockSpec((R, 1536), lambda n: (n, 0)), pl.BlockSpec((R, 128), lambda n: (n, BLK_G))],
        out_specs=[wide, wide, wide, wide, sq, sq, narrow],
        out_shape=[b(S, 512), f(S, 512), b(S, 512), b(S, 512), b(DN_HEADS, S, CPAD), f(DN_HEADS, S, CPAD), f(S, 128)],
        compiler_params=_params(("parallel",)),
    )(a_log, dt_bias, qkv, pg)


def _gated_norm(o, z, gn):
    r, oh = _rms_stats(o)
    return oh * gn * (z * _sigmoid(z))


def _dn_scan_fwd(w, u, qg, kd, a, gcs, pz, gn):
    S = w.shape[0]
    C = DN_CHUNK
    nc = S // C
    G = CHUNKS_SCAN
    R = G * C

    def body(w_ref, u_ref, qg_ref, kd_ref, a_ref, gcs_ref, z_ref, gn_ref, o_ref, vn_ref, sst_ref, out_ref, state):
        @pl.when(pl.program_id(0) == 0)
        def _():
            state[...] = jnp.zeros_like(state)

        def head_program(chunk, h):
            hs = _head(h)
            rows = slice(chunk * C, (chunk + 1) * C)
            s_in = state[h]
            sb = _bf(s_in)
            sst_ref[chunk, h] = sb
            w_s = _dot(w_ref[rows, hs], sb)
            q_s = _dot(qg_ref[rows, hs], sb)
            yield
            vn = u_ref[rows, hs] - w_s
            vnb = _bf(vn)
            o = q_s + _dot(a_ref[h, rows], _rows_pad(vnb))
            k_v = _dot_tn(kd_ref[rows, hs], vnb)
            yield
            state[h] = s_in * jnp.exp(gcs_ref[(chunk + 1) * C - 1:(chunk + 1) * C, h:h + 1]) + k_v
            o_ref[rows, hs] = o
            vn_ref[rows, hs] = vnb
            out_ref[rows, hs] = _bf(_gated_norm(o, z_ref[rows, hs], gn_ref[...]))

        for chunk in range(G):
            _interleave(head_program(chunk, h) for h in range(DN_HEADS))

    wide = pl.BlockSpec((R, 512), lambda n: (n, 0))
    f = lambda *shp: jax.ShapeDtypeStruct(shp, F32)
    b = lambda *shp: jax.ShapeDtypeStruct(shp, BF16)
    return pl.pallas_call(
        body, grid=(nc // G,), name="dn_scan_fwd",
        in_specs=[wide, wide, wide, wide, pl.BlockSpec((DN_HEADS, R, CPAD), lambda n: (0, n, 0)),
                  pl.BlockSpec((R, 128), lambda n: (n, 0)), pl.BlockSpec((R, 512), lambda n: (n, BLK_Z)),
                  pl.BlockSpec((1, DN_DIM), lambda n: (0, 0))],
        out_specs=[wide, wide, pl.BlockSpec((G, DN_HEADS, DN_DIM, DN_DIM), lambda n: (n, 0, 0, 0)), wide],
        out_shape=[f(S, 512), b(S, 512), b(nc, DN_HEADS, DN_DIM, DN_DIM), b(S, 512)],
        scratch_shapes=[pltpu.VMEM((DN_HEADS, DN_DIM, DN_DIM), F32)],
        compiler_params=_params(("arbitrary",)),
    )(w, u, qg, kd, a, gcs, pz, gn)


def _dn_scan_bwd(dcat, o, pz, gn, sst, vnew, w, qg, kd, a, gcs, dproj):
    S = o.shape[0]
    C = DN_CHUNK
    G = CHUNKS_SCAN
    R = G * C
    steps = S // R

    def body(dy_ref, o_ref, z_ref, gn_ref, sst_ref, vn_ref, w_ref, qg_ref, kd_ref, a_ref, gcs_ref, _,
             du_ref, dw_ref, dqg_ref, dkd_ref, da_ref, dz_ref, dsc_ref, dgn_ref, dstate):
        @pl.when(pl.program_id(0) == 0)
        def _():
            dstate[...] = jnp.zeros_like(dstate)
            dgn_ref[...] = jnp.zeros_like(dgn_ref)

        gn_ = gn_ref[...]
        lane = lax.broadcasted_iota(jnp.int32, (C, 128), 1)
        row = lax.broadcasted_iota(jnp.int32, (C, 128), 0)
        dgn_parts = []

        def head_program(chunk, h, dsc_parts):
            hs = _head(h)
            rows = slice(chunk * C, (chunk + 1) * C)
            ov, z, dout = o_ref[rows, hs], z_ref[rows, hs], dy_ref[rows, hs]
            r, oh = _rms_stats(ov)
            sg = _sigmoid(z)
            don = dout * (z * sg)
            dz_ref[rows, hs] = _bf(dout * (oh * gn_) * (sg * (1.0 + z * (1.0 - sg))))
            dgn_parts.append(jnp.sum(don * oh, axis=0, keepdims=True))
            dn = don * gn_
            do = _bf(r * (dn - oh * jnp.mean(dn * oh, axis=-1, keepdims=True)))
            sb = sst_ref[chunk, h]
            s_in = sb.astype(F32)
            ds_out = dstate[h]
            dsb = _bf(ds_out)
            vnb = vn_ref[rows, hs]
            wb, qgb, kdb, ab = w_ref[rows, hs], qg_ref[rows, hs], kd_ref[rows, hs], a_ref[h, rows]
            dvn = _dot_tn(ab, do)[:C] + _dot(kdb, dsb)
            yield
            da_ref[h, rows] = _dot_nt(do, _rows_pad(vnb))
            dqg_ref[rows, hs] = _dot_nt(do, sb)
            dkd_ref[rows, hs] = _dot_nt(vnb, dsb)
            q_do = _dot_tn(qgb, do)
            yield
            dvnb = _bf(dvn)
            dw_ref[rows, hs] = _bf(-_dot_nt(dvnb, sb))
            w_dvn = _dot_tn(wb, dvnb)
            du_ref[rows, hs] = dvnb
            yield
            d_last = jnp.exp(gcs_ref[(chunk + 1) * C - 1:(chunk + 1) * C, h:h + 1])
            dd = jnp.sum(jnp.sum(ds_out * s_in, axis=1, keepdims=True), axis=0, keepdims=True)
            dsc_parts.append(jnp.where((lane == h) & (row == C - 1), dd * d_last, 0.0))
            dstate[h] = ds_out * d_last + q_do - w_dvn

        for chunk in reversed(range(G)):
            dsc_parts = []
            _interleave(head_program(chunk, h, dsc_parts) for h in range(DN_HEADS))
            dsc_ref[chunk * C:(chunk + 1) * C, :] = sum(dsc_parts[1:], dsc_parts[0])
        dgn_ref[...] += sum(dgn_parts[1:], dgn_parts[0])

    rev = lambda n: steps - 1 - n
    wide = pl.BlockSpec((R, 512), lambda n: (rev(n), 0))
    z_spec = pl.BlockSpec((R, 512), lambda n: (rev(n), BLK_Z))
    sq = pl.BlockSpec((DN_HEADS, R, CPAD), lambda n: (0, rev(n), 0))
    narrow = pl.BlockSpec((R, 128), lambda n: (rev(n), 0))
    gn_spec = pl.BlockSpec((1, DN_DIM), lambda n: (0, 0))
    f = lambda *shp: jax.ShapeDtypeStruct(shp, F32)
    b = lambda *shp: jax.ShapeDtypeStruct(shp, BF16)
    return pl.pallas_call(
        body, grid=(steps,), name="dn_scan_bwd",
        in_specs=[pl.BlockSpec((R, 512), lambda n: (rev(n), 1)), wide, z_spec, gn_spec,
                  pl.BlockSpec((G, DN_HEADS, DN_DIM, DN_DIM), lambda n: (rev(n), 0, 0, 0)),
                  wide, wide, wide, wide, sq, narrow, pl.BlockSpec(memory_space=pl.ANY)],
        out_specs=[wide, wide, wide, wide, sq, z_spec, narrow, gn_spec],
        out_shape=[b(S, 512), b(S, 512), f(S, 512), f(S, 512), f(DN_HEADS, S, CPAD),
                   jax.ShapeDtypeStruct(dproj.shape, dproj.dtype), f(S, 128), f(1, DN_DIM)],
        scratch_shapes=[pltpu.VMEM((DN_HEADS, DN_DIM, DN_DIM), F32)],
        input_output_aliases={11: 5},
        compiler_params=_params(("arbitrary",)),
    )(dcat, o, pz, gn, sst, vnew, w, qg, kd, a, gcs, dproj)


def _dn_chunk_bwd(qkv, pg, t_inv, gcs, du, dw, dqg, dkd, da, dsc, a_log, dt_bias, dproj):
    S = qkv.shape[0]
    C = DN_CHUNK
    G = CHUNKS_LOCAL
    R = G * C

    def body(alog_ref, dtb_ref, qkv_ref, pg_ref, t_ref, gcs_ref, du_ref, dw_ref, dqg_ref, dkd_ref, da_ref, dsc_ref, _,
             dqkv_ref, dpg_ref, acc_ref):
        @pl.when(pl.program_id(0) == 0)
        def _():
            acc_ref[...] = jnp.zeros_like(acc_ref)

        ii, jj = _chunk_masks()
        lane = lax.broadcasted_iota(jnp.int32, (1, 128), 1)
        row8 = lax.broadcasted_iota(jnp.int32, (8, 128), 0)
        lane8 = lax.broadcasted_iota(jnp.int32, (8, 128), 1)
        rowc = lax.broadcasted_iota(jnp.int32, (C, 1), 0)
        tril, strict = jj <= ii, jj < ii
        dpg_parts, acc_parts = [[] for _ in range(G)], []

        def head_program(chunk, h):
            rows = slice(chunk * C, (chunk + 1) * C)
            q, k, v = qkv_ref[rows, _head(h)], qkv_ref[rows, _head(DN_HEADS + h)], qkv_ref[rows, _head(2 * DN_HEADS + h)]
            gc_col, beta, g_col = gcs_ref[rows, h:h + 1], gcs_ref[rows, DN_HEADS + h:DN_HEADS + h + 1], \
                gcs_ref[rows, 2 * DN_HEADS + h:2 * DN_HEADS + h + 1]
            dec = _decay(gc_col, ii, jj)
            eg = jnp.exp(gc_col)
            g_last = gc_col[C - 1:C, :]
            ek = jnp.exp(g_last - gc_col)
            kb, vb = k * beta, v * beta
            kbg = kb * eg
            qb, kbb = _bf(q), _bf(kb)
            k_rows = _rows_pad(_bf(k))
            t = t_ref[h, rows]
            tb = _bf(t)
            dub, dwb = du_ref[rows, _head(h)], dw_ref[rows, _head(h)]
            dqg_, dkd_ = dqg_ref[rows, _head(h)], dkd_ref[rows, _head(h)]
            dt = _dot_nt(dub, _rows_pad(_bf(vb))) + _dot_nt(dwb, _rows_pad(_bf(kbg)))
            t_du_dw = _dot_tn(tb, jnp.concatenate([dub, dwb], axis=1))
            dvb, dkbg = t_du_dw[:C, :DN_DIM], t_du_dw[:C, DN_DIM:]
            kk = _dot_nt(kbb, k_rows)
            qk = _dot_nt(qb, k_rows)
            yield
            dt_t = _dot3_nt(dt, t)
            yield
            dl = -_dot3_tn(t, dt_t)
            yield
            dm = jnp.where(strict, dl * dec, 0.0)
            dqk = jnp.where(tril, da_ref[h, rows] * dec, 0.0)
            gmat = dm * kk + dqk * qk
            dgc = jnp.sum(gmat, axis=1, keepdims=True) - _row_to_col(jnp.sum(gmat, axis=0, keepdims=True), ii, jj)
            dmb, dqkb = _bf(dm), _bf(dqk)
            yield
            dkb = _dot(dmb, k_rows) + dkbg * eg
            dk = _dot_tn(jnp.concatenate([dmb, dqkb], axis=0), jnp.concatenate([kbb, qb], axis=0))[:C] + dkd_ * ek
            dq = _dot(dqkb, k_rows) + dqg_ * eg
            yield
            tk = jnp.sum(dkd_ * k * ek, axis=1, keepdims=True)
            dgc = dgc + jnp.sum(dqg_ * q * eg, axis=1, keepdims=True) - tk + jnp.sum(dkbg * kbg, axis=1, keepdims=True)
            dgl = jnp.sum(tk, axis=0, keepdims=True) + dsc_ref[(chunk + 1) * C - 1:(chunk + 1) * C, h:h + 1]
            dgc = dgc + jnp.where(rowc == C - 1, dgl, 0.0)
            yield
            dk = dk + dkb * beta
            dbeta = jnp.sum(dkb * k, axis=1, keepdims=True) + jnp.sum(dvb * v, axis=1, keepdims=True)
            dqkv_ref[rows, _head(h)] = dq
            dqkv_ref[rows, _head(DN_HEADS + h)] = dk
            dqkv_ref[rows, _head(2 * DN_HEADS + h)] = dvb * beta
            dg_col = jnp.sum(jnp.where(jj >= ii, _col_to_row(dgc, ii, jj), 0.0), axis=1, keepdims=True)
            yield
            db = dbeta * beta * (1.0 - beta)
            da_in = dg_col * (-jnp.exp(alog_ref[h])) * _sigmoid(pg_ref[rows, DN_HEADS + h:DN_HEADS + h + 1] + dtb_ref[h])
            dpg_parts[chunk].append(jnp.where(lane == h, db, 0.0) + jnp.where(lane == DN_HEADS + h, da_in, 0.0))
            acc_parts.append(jnp.where((row8 == 0) & (lane8 == h), jnp.sum(dg_col * g_col, axis=0, keepdims=True), 0.0)
                             + jnp.where((row8 == 1) & (lane8 == h), jnp.sum(da_in, axis=0, keepdims=True), 0.0))

        _interleave(head_program(chunk, h) for chunk in range(G) for h in range(DN_HEADS))
        for chunk in range(G):
            dpg = sum(dpg_parts[chunk][1:], dpg_parts[chunk][0])
            dpg_ref[chunk * C:(chunk + 1) * C, :] = _bf(jnp.concatenate([dpg, jnp.zeros_like(dpg)], axis=1))
        acc_ref[...] += sum(acc_parts[1:], acc_parts[0])

    smem = pl.BlockSpec(memory_space=pltpu.SMEM)
    wide = pl.BlockSpec((R, 512), lambda n: (n, 0))
    sq = pl.BlockSpec((DN_HEADS, R, CPAD), lambda n: (0, n, 0))
    narrow = pl.BlockSpec((R, 128), lambda n: (n, 0))
    qkv_spec = pl.BlockSpec((R, 1536), lambda n: (n, 0))
    f = lambda *shp: jax.ShapeDtypeStruct(shp, F32)
    return pl.pallas_call(
        body, grid=(S // R,), name="dn_chunk_bwd",
        in_specs=[smem, smem, qkv_spec, pl.BlockSpec((R, 128), lambda n: (n, BLK_G)), sq, narrow, wide, wide, wide, wide, sq,
                  narrow, pl.BlockSpec(memory_space=pl.ANY)],
        out_specs=[qkv_spec, pl.BlockSpec((R, 256), lambda n: (n, BLK_G_PAD)), pl.BlockSpec((8, 128), lambda n: (0, 0))],
        out_shape=[f(S, 1536), jax.ShapeDtypeStruct(dproj.shape, dproj.dtype), f(8, 128)],
        input_output_aliases={12: 1},
        compiler_params=_params(("arbitrary",)),
    )(a_log, dt_bias, qkv, pg, t_inv, gcs, du, dw, dqg, dkd, da, dsc, dproj)


_W_IN_SECTIONS = ((0, 0, 512), (2304, 512, 512), (768, 1024, 1536), (512, 2560, 256), (2816, 2816, 8))
_HALF = D_MODEL // 2
_SECTION_ROWS = 256


def _pack_pairs(x):
    bits = lax.bitcast_convert_type(x, jnp.uint32)
    return lax.bitcast_convert_type(bits[:, _HALF:] | (bits[:, :_HALF] >> 16), F32)


def _unpack_pairs(words):
    bits = lax.bitcast_convert_type(words, jnp.uint32)
    return (lax.bitcast_convert_type(bits << 16, F32),
            lax.bitcast_convert_type(bits & jnp.uint32(0xFFFF0000), F32))


def _w_in_to_internal(packed):
    starts = [dst for _, dst, _ in _W_IN_SECTIONS] + [D_IN_PAD]

    def body(x_hbm, o_ref, words, sems):
        copies = [pltpu.make_async_copy(x_hbm.at[pl.ds(src, rows), 0, :], words.at[pl.ds(dst, rows)], sems.at[i])
                  for i, (src, dst, rows) in enumerate(_W_IN_SECTIONS)]
        for cp in copies:
            cp.start()
        words[pl.ds(D_IN, D_IN_PAD - D_IN), :] = jnp.zeros((D_IN_PAD - D_IN, _HALF), F32)
        for i, cp in enumerate(copies):
            cp.wait()
            for r in range(starts[i], starts[i + 1], _SECTION_ROWS):
                for c, h in enumerate(_unpack_pairs(words[pl.ds(r, _SECTION_ROWS), :])):
                    o_ref[pl.ds(r, _SECTION_ROWS), c * _HALF:(c + 1) * _HALF] = _bf(h)

    assert all((b - a) % _SECTION_ROWS == 0 for a, b in zip(starts, starts[1:])) and starts[-2] + _W_IN_SECTIONS[-1][2] == D_IN
    return pl.pallas_call(body, name="w_in_to_internal", out_shape=jax.ShapeDtypeStruct((D_IN_PAD, D_MODEL), BF16),
                          in_specs=[pl.BlockSpec(memory_space=pl.ANY)],
                          scratch_shapes=[pltpu.VMEM((D_IN_PAD, _HALF), F32), pltpu.SemaphoreType.DMA((len(_W_IN_SECTIONS),))],
                          compiler_params=pltpu.CompilerParams(vmem_limit_bytes=VMEM_LIMIT))(packed)


def _w_in_from_internal(gt):
    def body(g_ref, o_hbm, words, sems):
        copies = []
        for i, (dst, src, rows) in enumerate(_W_IN_SECTIONS):
            for r in range(src, src + rows, _SECTION_ROWS):
                n = max(min(_SECTION_ROWS, src + rows - r), 16)
                words[pl.ds(r, n), :] = _pack_pairs(g_ref[pl.ds(r, n), :].astype(F32))
            copies.append(pltpu.make_async_copy(words.at[pl.ds(src, rows)], o_hbm.at[pl.ds(dst, rows), 0, :], sems.at[i]))
            copies[-1].start()
        for cp in copies:
            cp.wait()

    return pl.pallas_call(body, name="w_in_from_internal", out_shape=jax.ShapeDtypeStruct((D_IN, 1, _HALF), F32),
                          out_specs=pl.BlockSpec(memory_space=pl.ANY),
                          scratch_shapes=[pltpu.VMEM((D_IN_PAD, _HALF), F32), pltpu.SemaphoreType.DMA((len(_W_IN_SECTIONS),))],
                          compiler_params=pltpu.CompilerParams(vmem_limit_bytes=VMEM_LIMIT))(gt)


def _local_step(x, p, target, wts, first_weights, other_weights, ship_early, after):
    S = x.shape[0]
    cos, sin = _rope_tables(S)
    sinks, a_log, dt_bias = wts["sinks"].reshape(8), wts["a_log"].reshape(4), wts["dt_bias"].reshape(4)
    gn = wts["dn_norm"].reshape(1, DN_DIM)
    add = lambda acc, res: (acc + res,)

    u = _rmsnorm_fwd(x, wts["norm_mix"], "norm_mix_fwd", after)
    w_in_t, conv_w = first_weights(u)
    proj, = _mm(u, w_in_t, form="nt", name="in_proj", out_dtypes=[F32], tn=512)
    attn, lse = _attn_fwd(proj, cos, sin, sinks)
    qkv = _dn_prep_fwd(proj, conv_w)
    cw, cu, cqg, ckd, ca, ct, gcs = _dn_chunk_fwd(qkv, proj, a_log, dt_bias)
    o, vnew, sst, dn_out = _dn_scan_fwd(cw, cu, cqg, ckd, ca, gcs, proj, gn)
    w_o, = other_weights(("w_o",), dn_out)
    h1, = _mm([attn, dn_out], w_o, form="nn", name="out_proj", out_dtypes=[F32], tn=512, epi=add, extra=[x])

    w_up, w_down = other_weights(("w_up", "w_down"), h1)
    hid, relu, m, h2 = _mlp_fwd(h1, w_up, w_down, wts["norm_mlp"])
    w_pg, w_pp = other_weights(("w_ple_gate", "w_ple_proj"), h2)
    n3, dh2, dgl, dpp, loss, d_norm_final, d_norm_ple = _ple_and_loss(h2, p, target, w_pg, w_pp, wts["norm_ple"],
                                                                     wts["norm_final"].reshape(1, D_MODEL))
    g = {"norm_final": d_norm_final, "norm_ple": d_norm_ple}
    early = {"w_ple_gate": _mm_tn(n3, dgl, name="d_w_ple_gate", tm=512, tn=1024, out_dtype=BF16).reshape(N_DEV, 128, 1024),
             "w_ple_proj": _mm_tn(p, dpp, name="d_w_ple_proj", tm=256, tn=128, out_dtype=BF16, column_shards=True)}
    d_act, = _mm(dh2, w_down, form="nt", name="d_hidden", out_dtypes=[BF16], tn=512,
                 epi=lambda acc, r: (acc * (2.0 * r.astype(F32)),), extra=[relu])
    early["w_down"] = _mm_tn(hid, dh2, name="d_w_down", tm=512, tn=1024, out_dtype=BF16).reshape(N_DEV, 512, 1024)
    early["w_up"] = _mm_tn(m, d_act, name="d_w_up", tm=1024, tn=512, out_dtype=BF16, column_shards=True)
    token = ship_early(early)
    dh1, g["norm_mlp"], dcat = _mm(d_act, w_up, form="nt", name="d_m", out_dtypes=[F32], tn=512, after=token,
                                   norm_bwd=(h1, wts["norm_mlp"], dh2), then_nt=w_o)
    d_w_o = _mm_tn([attn, dn_out], dh1, name="d_w_o", tm=512, tn=512, out_dtype=BF16)
    token = ship_early({"w_o": d_w_o.reshape(N_DEV, 128, 1024)})
    dproj, dk, dv, dsinks = _attn_bwd(proj, cos, sin, sinks, dcat, attn, lse, token)
    g["sinks"] = dsinks[:, 0].reshape(1, 8)
    du_, dw_, dqg, dkd, da, dproj, dsc, g["dn_norm"] = _dn_scan_bwd(dcat, o, proj, gn, sst, vnew, cw, cqg, ckd, ca, gcs, dproj)
    dqkv, dproj, gate_acc = _dn_chunk_bwd(qkv, proj, ct, gcs, du_, dw_, dqg, dkd, da, dsc, a_log, dt_bias, dproj)
    g["a_log"], g["dt_bias"] = gate_acc[0:1, 0:4], gate_acc[1:2, 0:4]
    dproj, g["conv_w"] = _dn_prep_bwd(proj, conv_w, dqkv, dproj, dk, dv)
    token = ship_early({"w_in": _mm_tn(dproj, u, name="d_w_in", tm=512, tn=1024, out_dtype=BF16)})
    grad_x, g["norm_mix"] = _mm(dproj, w_in_t, form="nn", name="d_u", out_dtypes=[F32], tn=512, after=token,
                                norm_bwd=(x, wts["norm_mix"], dh1))
    return loss, grad_x, g


def _peer(k):
    x, y, c = lax.axis_index("x"), lax.axis_index("y"), lax.axis_index("c")
    px = 1 - x if k & 4 else x
    py = 1 - y if k & 2 else y
    pc = 1 - c if k & 1 else c
    return (px, py, pc), 4 * px + 2 * py + pc


def _exchange(srcs, name, gather):
    n = len(srcs)
    gathers = list(gather) if isinstance(gather, (list, tuple)) else [gather] * n
    shapes = [(N_DEV,) + s.shape if gt else s.shape for s, gt in zip(srcs, gathers)]

    def body(*refs):
        src_refs, out_refs = refs[:n], refs[n:2 * n]
        send_sems, recv_sems, local_sems = refs[2 * n:]
        _, me = _peer(0)
        piece = lambda a, d: src_refs[a] if gathers[a] else src_refs[a].at[d]
        local = [pltpu.make_async_copy(piece(a, me), out_refs[a].at[me], local_sems.at[a]) for a in range(n)]
        for cp in local:
            cp.start()
        copies = []
        for a in range(n):
            for k in range(1, N_DEV):
                dev, idx = _peer(k)
                cp = pltpu.make_async_remote_copy(src_ref=piece(a, idx), dst_ref=out_refs[a].at[me],
                                                  send_sem=send_sems.at[a, k - 1], recv_sem=recv_sems.at[a, k - 1],
                                                  device_id=dev, device_id_type=MESH)
                cp.start()
                copies.append(cp)
        for cp in copies:
            cp.wait_recv()
        for cp in copies:
            cp.wait_send()
        for cp in local:
            cp.wait()

    anywhere = pl.BlockSpec(memory_space=pl.ANY)
    return pl.pallas_call(
        body, name=name, in_specs=[anywhere] * n, out_specs=[anywhere] * n,
        out_shape=[jax.ShapeDtypeStruct(shp, s.dtype) for shp, s in zip(shapes, srcs)],
        scratch_shapes=[pltpu.SemaphoreType.DMA((n, N_DEV - 1)), pltpu.SemaphoreType.DMA((n, N_DEV - 1)),
                        pltpu.SemaphoreType.DMA((n,))],
    )(*srcs)


_HBM = pl.BlockSpec(memory_space=pltpu.HBM)
_SEM = pl.BlockSpec(memory_space=pltpu.SEMAPHORE)
_EFFECT = pltpu.SideEffectType.DATAFLOW_SIDE_EFFECTING


def _split_copies(src_refs, land_refs, send_sems, recv_sems, modes, which=None):
    _, me = _peer(0)
    local, remote = [], []
    which = range(len(src_refs)) if which is None else which
    for a, src, land in zip(which, src_refs, land_refs):
        if modes[a] == "columns":
            n_cols = src.shape[1]
            dst = land.at[:, pl.ds(pl.multiple_of(me * n_cols, n_cols), n_cols)]
        else:
            dst = land.at[me]
        part = lambda d: src.at[d] if modes[a] == "pieces" else src
        local.append(pltpu.make_async_copy(part(me), dst, recv_sems.at[a * N_DEV]))
        for k in ((2, 4, 6) if modes[a] == "chips" else range(1, N_DEV)):
            dev, idx = _peer(k)
            sem = a * N_DEV + k
            remote.append(pltpu.make_async_remote_copy(
                src_ref=part(idx), dst_ref=dst, send_sem=send_sems.at[sem], recv_sem=recv_sems.at[sem],
                device_id=dev, device_id_type=MESH))
    return local, remote


def _forward_copies(land_refs, send_sems, recv_sems):
    c = lax.axis_index("c")
    sibling, _ = _peer(1)
    copies = []
    for a, land in enumerate(land_refs):
        for chip in range(N_DEV // 2):
            slot = 2 * chip + c
            sem = a * (N_DEV // 2) + chip
            copies.append(pltpu.make_async_remote_copy(
                src_ref=land.at[slot], dst_ref=land.at[slot], send_sem=send_sems.at[sem], recv_sem=recv_sems.at[sem],
                device_id=sibling, device_id_type=MESH))
    return copies


def _forward_start(lands, name):
    n = len(lands)

    def body(*refs):
        for cp in _forward_copies(refs[:n], refs[n], refs[n + 1]):
            cp.start()
        refs[-1][...] = jnp.zeros_like(refs[-1])

    sems = pltpu.SemaphoreType.DMA((n * (N_DEV // 2),))
    out = pl.pallas_call(
        body, name=name,
        out_shape=(sems, sems, *[pltpu.HBM(t.shape, t.dtype) for t in lands], jax.ShapeDtypeStruct((8, 128), F32)),
        in_specs=[_HBM] * n, out_specs=(_SEM, _SEM, *[_HBM] * n, pl.BlockSpec(memory_space=pltpu.VMEM)),
        input_output_aliases={i: 2 + i for i in range(n)},
        compiler_params=pltpu.CompilerParams(has_side_effects=_EFFECT),
    )(*[pltpu.with_memory_space_constraint(t, pltpu.HBM) for t in lands])
    return out[:-1], out[-1]


def _forward_wait(handle, after, name):
    send_sems, recv_sems, *lands = handle
    n = len(lands)

    def body(*refs):
        for cp in _forward_copies(refs[:n], refs[n], refs[n + 1]):
            cp.wait_send()
            cp.wait_recv()

    return list(pl.pallas_call(
        body, name=name, out_shape=tuple(pltpu.HBM(t.shape, t.dtype) for t in lands),
        in_specs=[_HBM] * n + [_SEM, _SEM, pl.BlockSpec(memory_space=pl.ANY)], out_specs=tuple([_HBM] * n),
        input_output_aliases={i: i for i in range(n)},
        compiler_params=pltpu.CompilerParams(has_side_effects=_EFFECT),
    )(*lands, send_sems, recv_sems, after))


def _exchange_start(srcs, name, modes):
    n = len(srcs)
    modes = [modes] * n if isinstance(modes, str) else list(modes)
    lands = []
    for s, mode in zip(srcs, modes):
        shape = {"columns": (s.shape[0], N_DEV * s.shape[1]), "pieces": s.shape}.get(mode, (N_DEV,) + s.shape)
        lands.append(lax.empty(shape, s.dtype))

    def body(*refs):
        src_refs, land_refs = refs[:n], refs[n:2 * n]
        send_sems, recv_sems = refs[2 * n], refs[2 * n + 1]
        local, remote = _split_copies(src_refs, land_refs, send_sems, recv_sems, modes)
        for cp in local + remote:
            cp.start()
        refs[-1][...] = jnp.zeros_like(refs[-1])

    both = list(srcs) + lands
    sems = pltpu.SemaphoreType.DMA((n * N_DEV,))
    out = pl.pallas_call(
        body, name=name,
        out_shape=(sems, sems, *[pltpu.HBM(t.shape, t.dtype) for t in both], jax.ShapeDtypeStruct((8, 128), F32)),
        in_specs=[_HBM] * (2 * n), out_specs=(_SEM, _SEM, *[_HBM] * (2 * n), pl.BlockSpec(memory_space=pltpu.VMEM)),
        input_output_aliases={i: 2 + i for i in range(2 * n)},
        compiler_params=pltpu.CompilerParams(has_side_effects=_EFFECT),
    )(*[pltpu.with_memory_space_constraint(t, pltpu.HBM) for t in both])
    return (n, modes, out[:-1]), out[-1]


def _exchange_wait(handle, after, name, which=None):
    n_all, modes, (send_sems, recv_sems, *both_all) = handle
    which = list(range(n_all)) if which is None else list(which)
    n = len(which)
    both = [both_all[a] for a in which] + [both_all[n_all + a] for a in which]

    def body(*refs):
        src_refs, land_refs = refs[:n], refs[n:2 * n]
        local, remote = _split_copies(src_refs, land_refs, refs[2 * n], refs[2 * n + 1], modes, which)
        for cp in local:
            cp.wait()
        for cp in remote:
            cp.wait_send()
            cp.wait_recv()

    out = pl.pallas_call(
        body, name=name, out_shape=tuple(pltpu.HBM(t.shape, t.dtype) for t in both),
        in_specs=[_HBM] * (2 * n) + [_SEM, _SEM, pl.BlockSpec(memory_space=pl.ANY)], out_specs=tuple([_HBM] * (2 * n)),
        input_output_aliases={i: i for i in range(2 * n)},
        compiler_params=pltpu.CompilerParams(has_side_effects=_EFFECT),
    )(*both, send_sems, recv_sems, after)
    return list(out[n:])


def _cast_all(arrays, name, after):
    waits = [] if after is None else [after]

    def body(*refs):
        for src, dst in zip(refs[:len(arrays)], refs[len(arrays) + len(waits):]):
            if len(src.shape) == 2:
                dst[...] = _bf(src[...])
            else:
                dst[:, 0, :] = _pack_pairs(_bf(src[:, 0, :]).astype(F32))

    shapes = [jax.ShapeDtypeStruct(a.shape, BF16) if a.ndim == 2 else jax.ShapeDtypeStruct((a.shape[0], 1, a.shape[2] // 2), F32)
              for a in arrays]
    return pl.pallas_call(body, name=name, out_shape=shapes,
                          compiler_params=pltpu.CompilerParams(vmem_limit_bytes=VMEM_LIMIT))(*arrays, *waits)


def _adam_update(g, w, m, v):
    nm = ADAM_B1 * m + (1.0 - ADAM_B1) * g
    nv = ADAM_B2 * v + (1.0 - ADAM_B2) * (g * g)
    m_hat = nm / (1.0 - ADAM_B1 ** ADAM_STEP)
    v_hat = nv / (1.0 - ADAM_B2 ** ADAM_STEP)
    return -ADAM_LR * (m_hat / (jnp.sqrt(v_hat) + ADAM_EPS) + ADAM_WD * w), nm, nv


def _adamw(parts, w, m, v, name):
    n, R, W = parts.shape
    tm = 128 if R % 128 == 0 else R

    def body(p_ref, w_ref, m_ref, v_ref, g_ref, d_ref, nm_ref, nv_ref):
        g = p_ref[0].astype(F32)
        for s in range(1, n):
            g = g + p_ref[s].astype(F32)
        g_ref[...] = g
        d_ref[...], nm_ref[...], nv_ref[...] = _adam_update(g, w_ref[...], m_ref[...], v_ref[...])

    stream = lambda shape, index: pl.BlockSpec(shape, index, pipeline_mode=pl.Buffered(3))
    tile = pl.BlockSpec((tm, W), lambda i: (i, 0))

    def pipeline(*hbm_refs):
        pltpu.emit_pipeline(
            body, grid=(R // tm,),
            in_specs=[stream((n, tm, W), lambda i: (0, i, 0))] + [stream((tm, W), lambda i: (i, 0))] * 3,
            out_specs=[tile] * 4)(*hbm_refs)

    anywhere = pl.BlockSpec(memory_space=pl.ANY)
    return pl.pallas_call(
        pipeline, name=name, in_specs=[anywhere] * 4, out_specs=[anywhere] * 4,
        out_shape=[jax.ShapeDtypeStruct((R, W), F32)] * 4,
        compiler_params=pltpu.CompilerParams(vmem_limit_bytes=VMEM_LIMIT),
    )(parts, w, m, v)


def _adamw_rows_apart(parts, w, m, v, name):
    n, R, _, half = parts.shape

    def body(p_hbm, w_hbm, m_hbm, v_hbm, *rest):
        out_hbm, (words, given, results, sems) = rest[:4], rest[4:]
        loads = [pltpu.make_async_copy(p_hbm.at[s, :, 0, :], words.at[s], sems.at[s]) for s in range(n)]
        loads += [pltpu.make_async_copy(h.at[:, 0, :], given.at[i], sems.at[n + i]) for i, h in enumerate((w_hbm, m_hbm, v_hbm))]
        for cp in loads:
            cp.start()
        for cp in loads:
            cp.wait()
        part = lambda s: jnp.concatenate(_unpack_pairs(words[s]), axis=1)
        g = part(0)
        for s in range(1, n):
            g = g + part(s)
        stores = []
        for i, val in enumerate((g,) + _adam_update(g, given[0], given[1], given[2])):
            results[i] = val
            stores.append(pltpu.make_async_copy(results.at[i], out_hbm[i].at[:, 0, :], sems.at[n + 3 + i]))
            stores[-1].start()
        for cp in stores:
            cp.wait()

    anywhere = pl.BlockSpec(memory_space=pl.ANY)
    return pl.pallas_call(
        body, name=name, in_specs=[anywhere] * 4, out_specs=[anywhere] * 4,
        out_shape=[jax.ShapeDtypeStruct(w.shape, F32)] * 4,
        scratch_shapes=[pltpu.VMEM((n, R, half), F32), pltpu.VMEM((3, R, 2 * half), F32), pltpu.VMEM((4, R, 2 * half), F32),
                        pltpu.SemaphoreType.DMA((n + 7,))],
        compiler_params=pltpu.CompilerParams(vmem_limit_bytes=VMEM_LIMIT),
    )(parts, w, m, v)


_MATRICES = ("w_in", "w_o", "w_up", "w_down", "w_ple_gate", "w_ple_proj")


_OTHERS = ("w_o", "w_up", "w_down", "w_ple_gate", "w_ple_proj")
_OTHER_MODES = {"w_o": "slots", "w_up": "slots", "w_down": "slots", "w_ple_gate": "slots", "w_ple_proj": "columns"}


_VECTORS = ("norm_mix", "norm_mlp", "norm_ple", "norm_final", "a_log", "dt_bias", "sinks", "dn_norm")
_SMALL_ROWS, _LOSS_ROW, _CONV_ROW = 16, 8, 9


def _pack_small(vectors, loss, conv):
    def body(*refs):
        out = refs[-1]
        out[...] = jnp.zeros_like(out)
        for r, ref in enumerate(refs[:len(_VECTORS)]):
            out[r:r + 1, 0:ref.shape[1]] = ref[...]
        out[_LOSS_ROW:_LOSS_ROW + 1, 0:128] = refs[len(_VECTORS)][...]
        out[_CONV_ROW:_CONV_ROW + 6, :] = refs[len(_VECTORS) + 1][...]

    return pl.pallas_call(body, name="pack_small", out_shape=jax.ShapeDtypeStruct((_SMALL_ROWS, 1024), F32))(*vectors, loss, conv)


def _sum_slots(parts):
    def body(p_ref, o_ref):
        acc = p_ref[0]
        for s in range(1, parts.shape[0]):
            acc = acc + p_ref[s]
        o_ref[...] = acc

    return pl.pallas_call(body, name="sum_small", out_shape=jax.ShapeDtypeStruct(parts.shape[1:], parts.dtype))(parts)


def _adamw_vectors(summed, conv_g, wmv):
    names = _VECTORS + ("conv_w",)
    flat = [a for triple in wmv for a in triple]

    def body(*refs):
        sum_ref, conv_ref = refs[0], refs[1]
        ins, outs = refs[2:2 + len(flat)], refs[2 + len(flat):]
        for i in range(len(names)):
            w_ref, m_ref, v_ref = ins[3 * i:3 * i + 3]
            g = conv_ref[...] if i == len(_VECTORS) else sum_ref[i:i + 1, 0:w_ref.shape[1]]
            outs[4 * i][...] = g
            outs[4 * i + 1][...], outs[4 * i + 2][...], outs[4 * i + 3][...] = _adam_update(g, w_ref[...], m_ref[...], v_ref[...])

    out_shape = [jax.ShapeDtypeStruct(t[0].shape, F32) for t in wmv for _ in range(4)]
    res = pl.pallas_call(body, name="adamw_vectors", out_shape=out_shape)(summed, conv_g, *flat)
    return {n: res[4 * i:4 * i + 4] for i, n in enumerate(names)}


_ORDER = ("norm_mix", "w_in", "conv_w", "a_log", "dt_bias", "dn_norm", "sinks", "w_o", "norm_mlp", "w_up", "w_down",
          "norm_ple", "w_ple_gate", "w_ple_proj", "norm_final")


def kernel(x, p, norm_mix, w_in, conv_w, a_log, dt_bias, dn_norm, sinks, w_o, norm_mlp, w_up, w_down, norm_ple, w_ple_gate, w_ple_proj, norm_final, loss_target, m_norm_mix, m_w_in, m_conv_w, m_a_log, m_dt_bias, m_dn_norm, m_sinks, m_w_o, m_norm_mlp, m_w_up, m_w_down, m_norm_ple, m_w_ple_gate, m_w_ple_proj, m_norm_final, v_norm_mix, v_w_in, v_conv_w, v_a_log, v_dt_bias, v_dn_norm, v_sinks, v_w_o, v_norm_mlp, v_w_up, v_w_down, v_norm_ple, v_w_ple_gate, v_w_ple_proj, v_norm_final):
    w = dict(norm_mix=norm_mix, w_in=w_in, conv_w=conv_w[0], a_log=a_log, dt_bias=dt_bias, dn_norm=dn_norm, sinks=sinks,
             w_o=w_o[0], norm_mlp=norm_mlp, w_up=w_up[0], w_down=w_down[0], norm_ple=norm_ple, w_ple_gate=w_ple_gate[0],
             w_ple_proj=w_ple_proj[0], norm_final=norm_final)
    m = dict(norm_mix=m_norm_mix, w_in=m_w_in, conv_w=m_conv_w[0], a_log=m_a_log, dt_bias=m_dt_bias, dn_norm=m_dn_norm,
             sinks=m_sinks, w_o=m_w_o[0], norm_mlp=m_norm_mlp, w_up=m_w_up[0], w_down=m_w_down[0], norm_ple=m_norm_ple,
             w_ple_gate=m_w_ple_gate[0], w_ple_proj=m_w_ple_proj[0], norm_final=m_norm_final)
    v = dict(norm_mix=v_norm_mix, w_in=v_w_in, conv_w=v_conv_w[0], a_log=v_a_log, dt_bias=v_dt_bias, dn_norm=v_dn_norm,
             sinks=v_sinks, w_o=v_w_o[0], norm_mlp=v_norm_mlp, w_up=v_w_up[0], w_down=v_w_down[0], norm_ple=v_norm_ple,
             w_ple_gate=v_w_ple_gate[0], w_ple_proj=v_w_ple_proj[0], norm_final=v_norm_final)
    me = 4 * lax.axis_index("x") + 2 * lax.axis_index("y") + lax.axis_index("c")
    conv_shard = conv_w.shape[2]

    for d in (w, m, v):
        d["w_in"] = jnp.transpose(d["w_in"], (2, 0, 1))
    conv_pad = jnp.pad(w["conv_w"], ((0, 8 - DN_CONV), (0, 256 - conv_shard)))
    w_in_shard, = _cast_all([w["w_in"]], "cast_w_in", None)
    gathers_first, token_first = _exchange_start([w_in_shard, conv_pad], "gather_first_start", "chips")
    shards = _cast_all([w[n] for n in _OTHERS], "cast_others", token_first)
    gathers, token_gather = _exchange_start(list(shards), "gather_start", [_OTHER_MODES[n] for n in _OTHERS])

    def first_weights(after):
        over_ici = _exchange_wait(gathers_first, after, "gather_first_wait")
        handle, token = _forward_start(over_ici, "gather_first_forward")
        w_in_all, conv_all = _forward_wait(handle, token, "gather_first_forward_wait")
        conv_all = jnp.transpose(conv_all[:, :DN_CONV, :conv_shard], (1, 0, 2)).reshape(DN_CONV, N_DEV * conv_shard)
        return _w_in_to_internal(w_in_all.reshape(D_IN, 1, _HALF)), conv_all

    as_taken = {"w_o": lambda t: t.reshape(1024, 1024), "w_up": lambda t: t, "w_down": lambda t: t.reshape(4096, 1024),
                "w_ple_gate": lambda t: t.reshape(1024, 1024), "w_ple_proj": lambda t: t}

    def other_weights(names, after):
        which = [_OTHERS.index(n) for n in names]
        got = _exchange_wait(gathers, after, "gather_wait_" + names[0], which)
        return [as_taken[n](t) for n, t in zip(names, got)]

    shipped = []

    def ship_early(pieces):
        names = tuple(pieces)
        if names == ("w_in",):
            pieces = {"w_in": _w_in_from_internal(pieces["w_in"]).reshape(N_DEV, D_IN // N_DEV, 1, _HALF)}
        handle, token = _exchange_start([pieces[n] for n in names], "scatter_start_" + names[0], "pieces")
        shipped.append((names, handle))
        return token

    loss, grad_x, g = _local_step(x[0], p[0, 0], loss_target[0], w, first_weights, other_weights, ship_early, token_gather)

    row = lambda t: t.reshape(1, t.size)
    small = _pack_small([row(g[n]) for n in _VECTORS], loss, g["conv_w"].reshape(6, 1024))
    small_handle, token_small = _exchange_start([small], "gather_small_start", "slots")
    big, after = {}, token_small
    for names, handle in shipped[:-1]:
        for n, r in zip(names, _exchange_wait(handle, after, "scatter_wait_" + names[0])):
            big[n] = _adamw(r, w[n], m[n], v[n], "adamw_" + n)
            after = big[n][1]
    small_all, = _exchange_wait(small_handle, after, "gather_small_wait")
    summed = _sum_slots(small_all)
    conv_g = lax.dynamic_slice(summed[_CONV_ROW:_CONV_ROW + 6].reshape(DN_CONV, N_DEV * conv_shard), (0, me * conv_shard),
                               (DN_CONV, conv_shard))
    small_out = _adamw_vectors(summed, conv_g, [(row(w[n]), row(m[n]), row(v[n])) for n in _VECTORS]
                               + [(w["conv_w"], m["conv_w"], v["conv_w"])])
    names, handle = shipped[-1]
    for n, r in zip(names, _exchange_wait(handle, small_out["conv_w"][0], "scatter_wait_" + names[0])):
        big[n] = _adamw_rows_apart(r, w[n], m[n], v[n], "adamw_" + n)

    result = [summed[_LOSS_ROW, 0], grad_x[None]]
    for i in range(4):
        for n in _ORDER:
            if n == "w_in":
                result.append(jnp.transpose(big[n][i], (1, 2, 0)))
            elif n in _MATRICES:
                result.append(big[n][i][None])
            elif n == "conv_w":
                result.append(small_out[n][i][None])
            else:
                result.append(small_out[n][i].reshape(w[n].shape))
    return tuple(result)
```

```python
import jax
import jax.numpy as jnp
import numpy as np
from jax import lax
from jax.experimental import pallas as pl
from jax.experimental.pallas import tpu as pltpu

F32, BF16 = jnp.float32, jnp.bfloat16
EPS = 1e-6
D_MODEL = 1024
N_DEV = 8
ATTN_BLOCK = 128
HEAD_PAIR = 128
DN_HEADS = 4
DN_DIM = 128
DN_CHUNK = 64
DN_CONV = 4
ROPE_THETA = 10000.0
D_IN = 2824
D_IN_PAD = 3072
BLK_Q, BLK_Z = 0, 1
BLK_DN, BLK_K, BLK_V, BLK_G = 8, 20, 21, 22
BLK_G_PAD = 11
VMEM_LIMIT = 56 * 1024 * 1024
NEG = -1e30
ADAM_LR, ADAM_B1, ADAM_B2, ADAM_EPS, ADAM_WD, ADAM_STEP = 0.001, 0.9, 0.999, 1e-08, 0.01, 10
MESH = pl.DeviceIdType.MESH


def _bf(x):
    return x.astype(BF16)


def _dot(a, b):
    return jnp.dot(a, b, preferred_element_type=F32)


def _dot_nt(a, b):
    return lax.dot_general(a, b, (((1,), (1,)), ((), ())), preferred_element_type=F32)


def _dot_tn(a, b):
    return lax.dot_general(a, b, (((0,), (0,)), ((), ())), preferred_element_type=F32)


def _sigmoid(x):
    return 1.0 / (1.0 + jnp.exp(-x))


def _params(sem):
    return pltpu.CompilerParams(dimension_semantics=sem, vmem_limit_bytes=VMEM_LIMIT)


def _mm(x, w, *, form, name, out_dtypes, tn, epi=None, extra=(), tm=512, w_row_block=0, after=None, norm=None,
        norm_bwd=None, then_nt=None):
    assert norm is None or norm_bwd is None
    xs = list(x) if isinstance(x, (list, tuple)) else [x]
    nx = len(xs)
    S, K = xs[0].shape
    shards = w.ndim == 3
    N = (w.shape[2] * N_DEV if shards else w.shape[1]) if form == "nn" else w.shape[-2]
    assert not (shards and form == "nn" and tn != w.shape[2]) and (nx == 1 or (form == "nn" and not shards and norm is None))
    r0 = w_row_block * K
    tm = min(tm, S)
    n_extra, n_out = len(extra), len(out_dtypes)
    tile = lambda width: pl.BlockSpec((tm, width), lambda i: (i, 0))
    whole = lambda a: pl.BlockSpec(a.shape, lambda i, nd=a.ndim: (0,) * nd)
    ins, in_specs = [*xs, w, *extra], [tile(K)] * nx + [whole(w)] + [tile(N)] * n_extra
    if norm is not None:
        ins, in_specs = ins + [norm], in_specs + [whole(norm)]
    if norm_bwd is not None:
        ins, in_specs = ins + list(norm_bwd), in_specs + [tile(N), whole(norm_bwd[1]), tile(N)]
    if then_nt is not None:
        ins, in_specs = ins + [then_nt], in_specs + [whole(then_nt)]
    if after is not None:
        ins, in_specs = ins + [after], in_specs + [whole(after)]
    out_shape = [jax.ShapeDtypeStruct((S, N), dt) for dt in out_dtypes]
    out_specs = [tile(N)] * n_out
    if norm is not None:
        out_shape, out_specs = out_shape + [jax.ShapeDtypeStruct((S, K), BF16)], out_specs + [tile(K)]
    if norm_bwd is not None:
        out_shape, out_specs = out_shape + [jax.ShapeDtypeStruct((1, N), F32)], out_specs + [pl.BlockSpec((1, N), lambda i: (0, 0))]
    if then_nt is not None:
        out_shape, out_specs = out_shape + [jax.ShapeDtypeStruct((S, then_nt.shape[0]), F32)], out_specs + [tile(then_nt.shape[0])]

    def product(xb, w_ref, cols, c):
        if form == "nn" and nx > 1:
            return sum(_dot(part, w_ref[r0 + p * K:r0 + (p + 1) * K, cols]) for p, part in enumerate(xb))
        if form == "nn":
            return _dot(xb, w_ref[c] if shards else w_ref[r0:r0 + K, cols])
        if not shards:
            return _dot_nt(xb, w_ref[cols, :])
        ks = w.shape[2]
        acc = _dot_nt(xb[:, 0:ks], w_ref[0, cols, :])
        for s in range(1, N_DEV):
            acc = acc + _dot_nt(xb[:, s * ks:(s + 1) * ks], w_ref[s, cols, :])
        return acc

    def body(*refs):
        x_ref, w_ref = refs[0], refs[nx]
        extra_refs = refs[nx + 1:nx + 1 + n_extra]
        at = nx + 1 + n_extra
        if norm is not None:
            gain_ref, at = refs[at], at + 1
        if norm_bwd is not None:
            (y_ref, ygain_ref, dres_ref), at = refs[at:at + 3], at + 3
        if then_nt is not None:
            w2_ref, at = refs[at], at + 1
        outs = refs[len(ins):]
        if norm is not None:
            _, xh = _rms_stats(x_ref[...])
            xb = _bf(xh * gain_ref[...])
            outs[n_out][...] = xb
        else:
            xb = _bf(x_ref[...]) if nx == 1 else [_bf(r[...]) for r in refs[:nx]]
        for c in range(N // tn):
            cols = slice(c * tn, (c + 1) * tn)
            acc = product(xb, w_ref, cols, c)
            res = epi(acc, *[r[:, cols] for r in extra_refs]) if epi else (acc,)
            for o, r in zip(outs[:n_out], res):
                o[:, cols] = r.astype(o.dtype)
        if norm_bwd is not None:
            dx, dg = _rms_bwd_tile(y_ref[...], ygain_ref[...], outs[0][...])
            outs[0][...] = dres_ref[...] + dx
            dg_ref = outs[n_out]

            @pl.when(pl.program_id(0) == 0)
            def _():
                dg_ref[...] = jnp.zeros_like(dg_ref)

            dg_ref[...] += dg
        if then_nt is not None:
            yb = _bf(outs[0][...])
            for c in range(then_nt.shape[0] // tn):
                cols = slice(c * tn, (c + 1) * tn)
                outs[-1][:, cols] = _dot_nt(yb, w2_ref[cols, :])

    return pl.pallas_call(
        body, grid=(S // tm,), name=name, in_specs=in_specs, out_specs=out_specs, out_shape=out_shape,
        compiler_params=_params(("arbitrary",) if norm_bwd is not None else ("parallel",)),
    )(*ins)


def _mlp_fwd(h1, w_up, w_down, gain):
    S, K = h1.shape
    n_sh, _, fs = w_up.shape
    tm = min(512, S)

    def body(x_ref, wup_ref, wdown_ref, g_ref, hid_ref, relu_ref, m_ref, h2_ref):
        x = x_ref[...]
        _, xh = _rms_stats(x)
        mb = _bf(xh * g_ref[...])
        m_ref[...] = mb
        h2_ref[...] = x
        for c in range(n_sh):
            cols = slice(c * fs, (c + 1) * fs)
            r = jnp.maximum(_dot(mb, wup_ref[c]), 0.0)
            hd = _bf(r * r)
            hid_ref[:, cols] = hd
            relu_ref[:, cols] = _bf(r)
            h2_ref[...] += _dot(hd, wdown_ref[cols, :])

    tile = lambda width: pl.BlockSpec((tm, width), lambda i: (i, 0))
    once = lambda a: pl.BlockSpec(a.shape, lambda i, nd=a.ndim: (0,) * nd, pipeline_mode=pl.Buffered(1))
    F = n_sh * fs
    return pl.pallas_call(
        body, grid=(S // tm,), name="mlp_fwd",
        in_specs=[tile(K), once(w_up), once(w_down), pl.BlockSpec(gain.shape, lambda i: (0, 0))],
        out_specs=[tile(F), tile(F), tile(K), tile(K)],
        out_shape=[jax.ShapeDtypeStruct((S, F), BF16), jax.ShapeDtypeStruct((S, F), BF16),
                   jax.ShapeDtypeStruct((S, K), BF16), jax.ShapeDtypeStruct((S, K), F32)],
        compiler_params=_params(("parallel",)),
    )(h1, w_up, w_down, gain)


def _mm_tn(x, dy, *, name, tm, tn, out_dtype=F32, column_shards=False, after=None):
    xs = list(x) if isinstance(x, (list, tuple)) else [x]
    S, N = dy.shape
    K = x.shape[1] if len(xs) == 1 else tm * len(xs)
    waits = [] if after is None else [after]

    def body(*refs):
        dy_ref, out_ref = refs[len(xs)], refs[-1]
        if len(xs) == 1:
            out_ref[...] = _dot_tn(_bf(refs[0][...]), _bf(dy_ref[...])).astype(out_dtype)
        for k in range(len(xs) if len(xs) > 1 else 0):
            @pl.when(pl.program_id(0) == k)
            def _(k=k):
                out_ref[...] = _dot_tn(_bf(refs[k][...]), _bf(dy_ref[...])).astype(out_dtype)

    if column_shards:
        out_spec = pl.BlockSpec((None, tm, tn), lambda i, j: (j, i, 0))
        out_shape = jax.ShapeDtypeStruct((N // tn, K, tn), out_dtype)
    else:
        out_spec = pl.BlockSpec((tm, tn), lambda i, j: (i, j))
        out_shape = jax.ShapeDtypeStruct((K, N), out_dtype)
    return pl.pallas_call(
        body, grid=(K // tm, N // tn), name=name,
        in_specs=([pl.BlockSpec((S, tm), lambda i, j: (0, i))] if len(xs) == 1 else [pl.BlockSpec((S, tm), lambda i, j: (0, 0))] * len(xs))
        + [pl.BlockSpec((S, tn), lambda i, j: (0, j))] + [pl.BlockSpec(memory_space=pl.ANY)] * len(waits),
        out_specs=out_spec, out_shape=out_shape,
        compiler_params=_params(("parallel", "parallel")),
    )(*xs, dy, *waits)


def _rowwise(body, *, tiled, full, out_tiled, out_acc, name, tm=512, smem=()):
    S = tiled[0].shape[0]
    tm = min(tm, S)
    n_in = len(smem) + len(tiled) + len(full)

    def kern(*refs):
        @pl.when(pl.program_id(0) == 0)
        def _():
            for r in refs[n_in + len(out_tiled):]:
                r[...] = jnp.zeros_like(r)
        body(*refs)

    in_specs = [pl.BlockSpec(memory_space=pltpu.SMEM) for _ in smem]
    in_specs += [pl.BlockSpec((tm, a.shape[1]), lambda i: (i, 0)) for a in tiled]
    in_specs += [pl.BlockSpec(a.shape, lambda i, nd=a.ndim: (0,) * nd) for a in full]
    out_specs = [pl.BlockSpec((tm, w), lambda i: (i, 0)) for w, _ in out_tiled]
    out_specs += [pl.BlockSpec(shp, lambda i, nd=len(shp): (0,) * nd) for shp, _ in out_acc]
    out_shape = [jax.ShapeDtypeStruct((S, w), dt) for w, dt in out_tiled]
    out_shape += [jax.ShapeDtypeStruct(shp, dt) for shp, dt in out_acc]
    return pl.pallas_call(
        kern, grid=(S // tm,), name=name, in_specs=in_specs, out_specs=out_specs, out_shape=out_shape,
        compiler_params=_params(("arbitrary",)),
    )(*smem, *tiled, *full)


def _rms_stats(x):
    r = lax.rsqrt(jnp.mean(x * x, axis=-1, keepdims=True) + EPS)
    return r, x * r


def _rmsnorm_fwd(x, g, name, after):
    def body(x_ref, g_ref, _, o_ref):
        _, xh = _rms_stats(x_ref[...])
        o_ref[...] = _bf(xh * g_ref[...])

    return _rowwise(body, tiled=[x], full=[g, after], out_tiled=[(x.shape[1], BF16)], out_acc=[], name=name)[0]


def _rms_bwd_tile(x, g, dxn):
    r, xh = _rms_stats(x)
    dg = jnp.sum(dxn * xh, axis=0, keepdims=True)
    dn = dxn * g
    dx = r * (dn - xh * jnp.mean(dn * xh, axis=-1, keepdims=True))
    return dx, dg


def _ple_and_loss(h2, p, target, w_pg, w_pp, g_ple, g_final):
    S, n = h2.shape
    tm = min(512, S)
    tn = 512

    def body(h2_ref, p_ref, t_ref, wpg_ref, wpp_ref, gple_ref, gfin_ref,
             n3_ref, dh_ref, dgl_ref, dpp_ref, loss_ref, dg_ref, dgple_ref, pp, gate, h3):
        @pl.when(pl.program_id(0) == 0)
        def _():
            loss_ref[...] = jnp.zeros_like(loss_ref)
            dg_ref[...] = jnp.zeros_like(dg_ref)
            dgple_ref[...] = jnp.zeros_like(dgple_ref)

        x = h2_ref[...]
        _, xh = _rms_stats(x)
        n3 = _bf(xh * gple_ref[...])
        n3_ref[...] = n3
        pb = _bf(p_ref[...])
        for c in range(n // tn):
            cols = slice(c * tn, (c + 1) * tn)
            pp[:, cols] = _dot(pb, wpp_ref[:, cols])
            gt = _sigmoid(_dot(n3, wpg_ref[:, cols]))
            gate[:, cols] = gt
            h3[:, cols] = x[:, cols] + gt * pp[:, cols]
        y = h3[...]
        _, yh = _rms_stats(y)
        e = yh * gfin_ref[...] - t_ref[...]
        per_tok = jnp.mean(e * e, axis=-1, keepdims=True)
        loss_ref[...] += 0.5 * jnp.sum(per_tok, axis=0, keepdims=True)
        dh, dg = _rms_bwd_tile(y, gfin_ref[...], e * (1.0 / n))
        dg_ref[...] += dg
        gt = gate[...]
        dgl = _bf(dh * pp[...] * gt * (1.0 - gt))
        dgl_ref[...] = dgl
        dpp_ref[...] = _bf(dh * gt)
        for c in range(n // tn):
            cols = slice(c * tn, (c + 1) * tn)
            h3[:, cols] = _dot_nt(dgl, wpg_ref[cols, :])
        dx, dgp = _rms_bwd_tile(x, gple_ref[...], h3[...])
        dh_ref[...] = dh + dx
        dgple_ref[...] += dgp

    tile = lambda width: pl.BlockSpec((tm, width), lambda i: (i, 0))
    whole = lambda a: pl.BlockSpec(a.shape, lambda i, nd=a.ndim: (0,) * nd)
    return pl.pallas_call(
        body, grid=(S // tm,), name="ple_and_loss",
        in_specs=[tile(n), tile(p.shape[1]), tile(n), whole(w_pg), whole(w_pp), whole(g_ple), whole(g_final)],
        out_specs=[tile(n), tile(n), tile(n), tile(n), pl.BlockSpec((1, 128), lambda i: (0, 0)),
                   pl.BlockSpec((1, n), lambda i: (0, 0)), pl.BlockSpec((1, n), lambda i: (0, 0))],
        out_shape=[jax.ShapeDtypeStruct((S, n), BF16), jax.ShapeDtypeStruct((S, n), F32), jax.ShapeDtypeStruct((S, n), BF16),
                   jax.ShapeDtypeStruct((S, n), BF16), jax.ShapeDtypeStruct((1, 128), F32), jax.ShapeDtypeStruct((1, n), F32),
                   jax.ShapeDtypeStruct((1, n), F32)],
        scratch_shapes=[pltpu.VMEM((tm, n), F32)] * 3,
        compiler_params=_params(("arbitrary",)),
    )(h2, p, target, w_pg, w_pp, g_ple, g_final)


def _rope_tables(S):
    half = 32
    inv = (1.0 / (np.float32(ROPE_THETA) ** (np.arange(half, dtype=np.float32) * np.float32(2.0 / 64)))).astype(np.float32)
    ang = np.arange(S).astype(np.float32)[:, None] * inv[None, :]
    cos, sin = np.cos(ang), np.sin(ang)
    return jnp.asarray(np.tile(cos, (1, 4))), jnp.asarray(np.concatenate([-sin, sin, -sin, sin], axis=1))


def _attn_common(i, kc, kp, vc, vp, cc, sc, cp, sp):
    lane = lax.broadcasted_iota(jnp.int32, (1, HEAD_PAIR), 1)
    lane_lo = jnp.bitwise_and(lane, 63) < 32
    slot = [lane < 64, lane >= 64]

    def swap_halves(t):
        return jnp.where(lane_lo, pltpu.roll(t, 96, 1), pltpu.roll(t, 32, 1))

    def rope(t, cos, sin):
        return t * cos + swap_halves(t) * sin

    def unrope(d, cos, sin):
        return d * cos + swap_halves(d * sin)

    k2 = jnp.concatenate([rope(kp, cp, sp), rope(kc, cc, sc)], axis=0)
    v2 = jnp.concatenate([vp, vc], axis=0)
    r = lax.broadcasted_iota(jnp.int32, (ATTN_BLOCK, 2 * ATTN_BLOCK), 0)
    c = lax.broadcasted_iota(jnp.int32, (ATTN_BLOCK, 2 * ATTN_BLOCK), 1)
    valid = (c > r) & (c <= r + ATTN_BLOCK) & jnp.logical_or(c >= ATTN_BLOCK, i > 0)
    ks, vs = {}, {}
    for j in range(2):
        kn = jnp.where(slot[j], k2, 0.0)
        vn = jnp.where(slot[j], v2, 0.0)
        for s in range(2):
            ks[j, s] = _bf(kn if s == j else pltpu.roll(kn, 64, 1))
            vs[j, s] = _bf(vn if s == j else pltpu.roll(vn, 64, 1))
    return slot, rope, unrope, valid, ks, vs


def _attn_probs(scores, valid, sink):
    s = jnp.where(valid, scores * 0.125, NEG)
    m = jnp.maximum(jnp.max(s, axis=1, keepdims=True), sink)
    e = jnp.exp(s - m)
    z = jnp.sum(e, axis=1, keepdims=True) + jnp.exp(sink - m)
    return e * (1.0 / z), m + jnp.log(z)


def _attn_specs(S):
    nb = S // ATTN_BLOCK
    prev = lambda i: jnp.maximum(i - 1, 0)
    blk = lambda w, col, row=(lambda i: i): pl.BlockSpec((ATTN_BLOCK, w), lambda i: (row(i), col))
    in_specs = [pl.BlockSpec(memory_space=pltpu.SMEM),
                blk(512, BLK_Q), blk(128, BLK_K), blk(128, BLK_K, prev), blk(128, BLK_V), blk(128, BLK_V, prev),
                blk(128, 0), blk(128, 0), blk(128, 0, prev), blk(128, 0, prev)]
    return nb, in_specs


def _attn_fwd(pa, cos, sin, sinks):
    S = pa.shape[0]
    nb, in_specs = _attn_specs(S)

    def body(sinks_ref, q_ref, kc_ref, kp_ref, vc_ref, vp_ref, cc_ref, sc_ref, cp_ref, sp_ref, o_ref, lse_ref):
        i = pl.program_id(0)
        lane = lax.broadcasted_iota(jnp.int32, (1, HEAD_PAIR), 1)
        cc, sc = cc_ref[...], sc_ref[...]
        _, rope, _, valid, ks, vs = _attn_common(i, kc_ref[...], kp_ref[...], vc_ref[...], vp_ref[...],
                                                 cc, sc, cp_ref[...], sp_ref[...])
        pair_cols = [slice(HEAD_PAIR * pair, HEAD_PAIR * (pair + 1)) for pair in range(4)]
        qps = [_bf(rope(q_ref[:, cols], cc, sc)) for cols in pair_cols]
        outs, lses = {}, {}

        def head_program(h):
            pair, s = divmod(h, 2)
            j = h // 4
            scores = _dot_nt(qps[pair], ks[j, s])
            yield
            p, lse = _attn_probs(scores, valid, sinks_ref[h])
            outs[h] = _dot(_bf(p), vs[j, s])
            lses[h] = jnp.where(lane == h, lse, 0.0)

        _interleave(head_program(h) for h in range(8))
        for pair, cols in enumerate(pair_cols):
            o_ref[:, cols] = outs[2 * pair] + outs[2 * pair + 1]
        lse_ref[...] = sum((lses[h] for h in range(1, 8)), lses[0])

    return pl.pallas_call(
        body, grid=(nb,), name="attn_fwd", in_specs=in_specs,
        out_specs=[pl.BlockSpec((ATTN_BLOCK, 512), lambda i: (i, 0)), pl.BlockSpec((ATTN_BLOCK, 128), lambda i: (i, 0))],
        out_shape=[jax.ShapeDtypeStruct((S, 512), F32), jax.ShapeDtypeStruct((S, 128), F32)],
        compiler_params=_params(("parallel",)),
    )(sinks, pa, pa, pa, pa, pa, cos, sin, cos, sin)


def _attn_bwd(pa, cos, sin, sinks, dcat, attn, lse, after):
    S = pa.shape[0]
    nb, in_specs = _attn_specs(S)
    in_specs = in_specs + [pl.BlockSpec((ATTN_BLOCK, 512), lambda i: (i, 0))] * 2 + [pl.BlockSpec((ATTN_BLOCK, 128), lambda i: (i, 0))]
    in_specs = in_specs + [pl.BlockSpec(memory_space=pl.ANY)]

    def body(sinks_ref, q_ref, kc_ref, kp_ref, vc_ref, vp_ref, cc_ref, sc_ref, cp_ref, sp_ref, do_ref, o_ref, lse_ref, _,
             dq_ref, dk_ref, dv_ref, dsink_ref):
        i = pl.program_id(0)

        @pl.when(i == 0)
        def _():
            dk_ref[...] = jnp.zeros_like(dk_ref)
            dv_ref[...] = jnp.zeros_like(dv_ref)
            dsink_ref[...] = jnp.zeros_like(dsink_ref)

        cc, sc, cp, sp = cc_ref[...], sc_ref[...], cp_ref[...], sp_ref[...]
        slot, rope, unrope, valid, ks, vs = _attn_common(i, kc_ref[...], kp_ref[...], vc_ref[...], vp_ref[...], cc, sc, cp, sp)
        pair_cols = [slice(HEAD_PAIR * pair, HEAD_PAIR * (pair + 1)) for pair in range(4)]
        qps = [_bf(rope(q_ref[:, cols], cc, sc)) for cols in pair_cols]
        dobs = [_bf(do_ref[:, cols]) for cols in pair_cols]
        do_o = [do_ref[:, cols] * o_ref[:, cols] for cols in pair_cols]
        dqs, dks, dvs = {}, {}, {}

        def head_program(h):
            pair, s = divmod(h, 2)
            j = h // 4
            qp, dob = qps[pair], dobs[pair]
            scores = _dot_nt(qp, ks[j, s])
            dp = _dot_nt(dob, vs[j, s])
            yield
            lse_h = lse_ref[:, h:h + 1]
            p = jnp.exp(jnp.where(valid, scores * 0.125, NEG) - lse_h)
            yield
            dr = jnp.sum(jnp.where(slot[s], do_o[pair], 0.0), axis=1, keepdims=True)
            ds = _bf(p * (dp - dr) * 0.125)
            yield
            dsink_ref[h:h + 1, :] += -jnp.sum(jnp.exp(sinks_ref[h] - lse_h) * dr, axis=0, keepdims=True)
            dqs[h] = _dot(ds, ks[j, s])
            dk_h = _dot_tn(ds, qp)
            dv_h = _dot_tn(_bf(p), dob)
            yield
            dk_h, dv_h = jnp.where(slot[s], dk_h, 0.0), jnp.where(slot[s], dv_h, 0.0)
            if s != j:
                dk_h, dv_h = pltpu.roll(dk_h, 64, 1), pltpu.roll(dv_h, 64, 1)
            dks[h], dvs[h] = dk_h, dv_h

        _interleave(head_program(h) for h in range(8))
        dk2 = sum((dks[h] for h in range(1, 8)), dks[0])
        dv2 = sum((dvs[h] for h in range(1, 8)), dvs[0])
        for pair, cols in enumerate(pair_cols):
            dq_ref[:, cols] = _bf(unrope(dqs[2 * pair] + dqs[2 * pair + 1], cc, sc))
        cur = pl.ds(pl.multiple_of(i * ATTN_BLOCK, ATTN_BLOCK), ATTN_BLOCK)
        dk_ref[cur, :] += unrope(dk2[ATTN_BLOCK:], cc, sc)
        dv_ref[cur, :] += dv2[ATTN_BLOCK:]

        @pl.when(i > 0)
        def _():
            prv = pl.ds(pl.multiple_of((i - 1) * ATTN_BLOCK, ATTN_BLOCK), ATTN_BLOCK)
            dk_ref[prv, :] += unrope(dk2[:ATTN_BLOCK], cp, sp)
            dv_ref[prv, :] += dv2[:ATTN_BLOCK]

    whole = lambda w: pl.BlockSpec((S, w), lambda i: (0, 0))
    return pl.pallas_call(
        body, grid=(nb,), name="attn_bwd", in_specs=in_specs,
        out_specs=[pl.BlockSpec((ATTN_BLOCK, 512), lambda i: (i, BLK_Q)), whole(128), whole(128),
                   pl.BlockSpec((8, 128), lambda i: (0, 0))],
        out_shape=[jax.ShapeDtypeStruct((S, D_IN_PAD), BF16), jax.ShapeDtypeStruct((S, 128), F32),
                   jax.ShapeDtypeStruct((S, 128), F32), jax.ShapeDtypeStruct((8, 128), F32)],
        compiler_params=_params(("arbitrary",)),
    )(sinks, pa, pa, pa, pa, pa, cos, sin, cos, sin, dcat, attn, lse, after)


CONV_ROWS = 512
CONV_PAD = 8


def _conv_silu(scr, w, r0):
    y = w[3:4, :] * scr[pl.ds(CONV_PAD + r0, CONV_ROWS), :]
    for j in range(DN_CONV - 1):
        y = y + w[j:j + 1, :] * scr[pl.ds(CONV_PAD + r0 - 3 + j, CONV_ROWS), :]
    return y


def _dn_prep_fwd(pd, conv_w):
    S = pd.shape[0]
    assert S % CONV_ROWS == 0

    def body(x_ref, w_ref, o_ref, scr):
        b = pl.program_id(0)
        scr[0:CONV_PAD, :] = jnp.zeros((CONV_PAD, DN_DIM), F32)
        scr[pl.ds(CONV_PAD, S), :] = x_ref[...]
        w = w_ref[...]
        q_scale = jnp.where(b < DN_HEADS, DN_DIM ** -0.5, 1.0)
        for r0 in range(0, S, CONV_ROWS):
            y = _conv_silu(scr, w, r0)
            a = y * _sigmoid(y)
            rs = lax.rsqrt(jnp.sum(a * a, axis=1, keepdims=True) + EPS)
            o_ref[pl.ds(r0, CONV_ROWS), :] = a * jnp.where(b < 2 * DN_HEADS, rs * q_scale, 1.0)

    col = pl.BlockSpec((S, DN_DIM), lambda b: (0, b))
    return pl.pallas_call(
        body, grid=(3 * DN_HEADS,), name="dn_prep_fwd",
        in_specs=[pl.BlockSpec((S, DN_DIM), lambda b: (0, BLK_DN + b)), pl.BlockSpec((DN_CONV, DN_DIM), lambda b: (0, b))],
        out_specs=col,
        out_shape=jax.ShapeDtypeStruct((S, 3 * DN_HEADS * DN_DIM), F32),
        scratch_shapes=[pltpu.VMEM((S + CONV_PAD, DN_DIM), F32)],
        compiler_params=_params(("parallel",)),
    )(pd, conv_w)


def _dn_prep_bwd(pd, conv_w, dqkv, dproj, dk, dv):
    S = pd.shape[0]
    NB = 3 * DN_HEADS

    def body(x_ref, w_ref, d_ref, _, dk_ref, dv_ref, dx_ref, dw_ref, scr, dscr):
        b = pl.program_id(0)

        @pl.when(b == NB)
        def _():
            dx_ref[...] = _bf(dk_ref[...])

        @pl.when(b == NB + 1)
        def _():
            dx_ref[...] = _bf(dv_ref[...])

        @pl.when(b < NB)
        def _():
            scr[0:CONV_PAD, :] = jnp.zeros((CONV_PAD, DN_DIM), F32)
            scr[pl.ds(CONV_PAD, S), :] = x_ref[...]
            dscr[pl.ds(S, CONV_PAD), :] = jnp.zeros((CONV_PAD, DN_DIM), F32)
            w = w_ref[...]
            q_scale = jnp.where(b < DN_HEADS, DN_DIM ** -0.5, 1.0)
            is_qk = b < 2 * DN_HEADS
            dw = [jnp.zeros((1, DN_DIM), F32) for _ in range(DN_CONV)]
            for r0 in range(0, S, CONV_ROWS):
                y = _conv_silu(scr, w, r0)
                sg = _sigmoid(y)
                a = y * sg
                dout = d_ref[pl.ds(r0, CONV_ROWS), :]
                rs = lax.rsqrt(jnp.sum(a * a, axis=1, keepdims=True) + EPS)
                da_qk = q_scale * rs * (dout - a * (rs * rs) * jnp.sum(dout * a, axis=1, keepdims=True))
                dy = jnp.where(is_qk, da_qk, dout) * (sg * (1.0 + y * (1.0 - sg)))
                dscr[pl.ds(r0, CONV_ROWS), :] = dy
                for j in range(DN_CONV):
                    dw[j] = dw[j] + jnp.sum(dy * scr[pl.ds(CONV_PAD + r0 - 3 + j, CONV_ROWS), :], axis=0, keepdims=True)
            for j in range(DN_CONV):
                dw_ref[j:j + 1, :] = dw[j]
            for r0 in range(0, S, CONV_ROWS):
                dx = w[3:4, :] * dscr[pl.ds(r0, CONV_ROWS), :]
                for j in range(DN_CONV - 1):
                    dx = dx + w[j:j + 1, :] * dscr[pl.ds(r0 + 3 - j, CONV_ROWS), :]
                dx_ref[pl.ds(r0, CONV_ROWS), :] = _bf(dx)

    own = lambda b: jnp.minimum(b, NB - 1)
    col = pl.BlockSpec((S, DN_DIM), lambda b: (0, own(b)))
    proj_col = pl.BlockSpec((S, DN_DIM), lambda b: (0, BLK_DN + own(b)))
    wcol = pl.BlockSpec((DN_CONV, DN_DIM), lambda b: (0, own(b)))
    whole = pl.BlockSpec((S, DN_DIM), lambda b: (0, 0))
    assert BLK_K == BLK_DN + NB and BLK_V == BLK_K + 1
    return pl.pallas_call(
        body, grid=(NB + 2,), name="dn_prep_bwd",
        in_specs=[proj_col, wcol, col, pl.BlockSpec(memory_space=pl.ANY), whole, whole],
        out_specs=[pl.BlockSpec((S, DN_DIM), lambda b: (0, BLK_DN + b)), wcol],
        out_shape=[jax.ShapeDtypeStruct(dproj.shape, dproj.dtype), jax.ShapeDtypeStruct((DN_CONV, 3 * DN_HEADS * DN_DIM), F32)],
        scratch_shapes=[pltpu.VMEM((S + CONV_PAD, DN_DIM), F32), pltpu.VMEM((S + CONV_PAD, DN_DIM), F32)],
        input_output_aliases={3: 0},
        compiler_params=_params(("arbitrary",)),
    )(pd, conv_w, dqkv, dproj, dk, dv)


CPAD = 128
CHUNKS_LOCAL = 8
CHUNKS_SCAN = 8


def _chunk_masks():
    ii = lax.broadcasted_iota(jnp.int32, (DN_CHUNK, CPAD), 0)
    jj = lax.broadcasted_iota(jnp.int32, (DN_CHUNK, CPAD), 1)
    return ii, jj


def _rows_pad(a):
    return jnp.concatenate([a, jnp.zeros_like(a)], axis=0)


def _hi_lo(a):
    hi = _bf(a)
    return hi, _bf(a - hi.astype(F32))


def _double_step(t, p):
    C = DN_CHUNK
    th, tl = _hi_lo(t)
    ph, pl_ = _hi_lo(p)
    r1 = _dot(jnp.concatenate([th, tl, ph, pl_], axis=0), _rows_pad(ph))
    r2 = _dot(jnp.concatenate([th, ph], axis=0), _rows_pad(pl_))
    return t + (r1[:C] + r1[C:2 * C] + r2[:C]), r1[2 * C:3 * C] + r1[3 * C:] + r2[C:]


def _dot3_nt(a, b):
    C = DN_CHUNK
    ah, al = _hi_lo(a)
    bh, bl = _hi_lo(b)
    r1 = _dot_nt(jnp.concatenate([ah, al], axis=0), _rows_pad(bh))
    return r1[:C] + r1[C:] + _dot_nt(ah, _rows_pad(bl))


def _dot3_tn(a, b):
    C = DN_CHUNK
    ah, al = _hi_lo(a)
    bh, bl = _hi_lo(b)
    return _dot_tn(jnp.concatenate([ah, al, ah], axis=0), jnp.concatenate([bh, bh, bl], axis=0))[:C]


def _interleave(programs):
    programs = list(programs)
    while programs:
        alive = []
        for prog in programs:
            try:
                next(prog)
                alive.append(prog)
            except StopIteration:
                pass
        programs = alive


def _col_to_row(col, ii, jj):
    return jnp.sum(jnp.where(ii == jj, col, 0.0), axis=0, keepdims=True)


def _row_to_col(row, ii, jj):
    return jnp.sum(jnp.where(ii == jj, row, 0.0), axis=1, keepdims=True)


def _decay(gc_col, ii, jj):
    diff = gc_col - _col_to_row(gc_col, ii, jj)
    return jnp.where(jj <= ii, jnp.exp(jnp.where(jj <= ii, diff, 0.0)), 0.0)


def _softplus(x):
    return jnp.maximum(x, 0.0) + jnp.log(1.0 + jnp.exp(-jnp.abs(x)))


def _head(h):
    return slice(DN_DIM * h, DN_DIM * (h + 1))


def _dn_chunk_fwd(qkv, pg, a_log, dt_bias):
    S = qkv.shape[0]
    C = DN_CHUNK
    G = CHUNKS_LOCAL
    R = G * C
    steps = S // R

    def body(alog_ref, dtb_ref, qkv_ref, pg_ref, w_ref, u_ref, qg_ref, kd_ref, a_ref, t_ref, gcs_ref):
        ii, jj = _chunk_masks()
        lane = lax.broadcasted_iota(jnp.int32, (1, 128), 1)
        eye = (ii == jj).astype(F32)
        gcs_parts = [[] for _ in range(G)]

        def head_program(chunk, h):
            rows = slice(chunk * C, (chunk + 1) * C)
            q, k, v = qkv_ref[rows, _head(h)], qkv_ref[rows, _head(DN_HEADS + h)], qkv_ref[rows, _head(2 * DN_HEADS + h)]
            beta = _sigmoid(pg_ref[rows, h:h + 1])
            g_col = -jnp.exp(alog_ref[h]) * _softplus(pg_ref[rows, DN_HEADS + h:DN_HEADS + h + 1] + dtb_ref[h])
            g_row = _col_to_row(g_col, ii, jj)
            gc_col = jnp.sum(jnp.where(jj <= ii, g_row, 0.0), axis=1, keepdims=True)
            dec = _decay(gc_col, ii, jj)
            eg = jnp.exp(gc_col)
            kb, vb = k * beta, v * beta
            k_rows = _rows_pad(_bf(k))
            kk = _dot_nt(_bf(kb), k_rows)
            qk = _dot_nt(_bf(q), k_rows)
            yield
            t, pw = eye, -jnp.where(jj < ii, kk * dec, 0.0)
            for _ in range(6):
                t, pw = _double_step(t, pw)
                yield
            tb = _bf(t)
            u_ref[rows, _head(h)] = _dot(tb, _rows_pad(_bf(vb)))
            w_ref[rows, _head(h)] = _bf(_dot(tb, _rows_pad(_bf(kb * eg))))
            a_ref[h, rows] = _bf(qk * dec)
            t_ref[h, rows] = t
            qg_ref[rows, _head(h)] = _bf(q * eg)
            kd_ref[rows, _head(h)] = _bf(k * jnp.exp(gc_col[C - 1:C, :] - gc_col))
            gcs_parts[chunk].append(jnp.where(lane == h, gc_col, 0.0) + jnp.where(lane == DN_HEADS + h, beta, 0.0)
                                    + jnp.where(lane == 2 * DN_HEADS + h, g_col, 0.0))

        _interleave(head_program(chunk, h) for chunk in range(G) for h in range(DN_HEADS))
        for chunk in range(G):
            gcs_ref[chunk * C:(chunk + 1) * C, :] = sum(gcs_parts[chunk][1:], gcs_parts[chunk][0])

    smem = pl.BlockSpec(memory_space=pltpu.SMEM)
    wide = pl.BlockSpec((R, 512), lambda n: (n, 0))
    sq = pl.BlockSpec((DN_HEADS, R, CPAD), lambda n: (0, n, 0))
    narrow = pl.BlockSpec((R, 128), lambda n: (n, 0))
    f = lambda *shp: jax.ShapeDtypeStruct(shp, F32)
    b = lambda *shp: jax.ShapeDtypeStruct(shp, BF16)
    return pl.pallas_call(
        body, grid=(steps,), name="dn_chunk_fwd",
        in_specs=[smem, smem, pl.BlockSpec((R, 1536), lambda n: (n, 0)), pl.BlockSpec((R, 128), lambda n: (n, BLK_G))],
        out_specs=[wide, wide, wide, wide, sq, sq, narrow],
        out_shape=[b(S, 512), f(S, 512), b(S, 512), b(S, 512), b(DN_HEADS, S, CPAD), f(DN_HEADS, S, CPAD), f(S, 128)],
        compiler_params=_params(("parallel",)),
    )(a_log, dt_bias, qkv, pg)


def _gated_norm(o, z, gn):
    r, oh = _rms_stats(o)
    return oh * gn * (z * _sigmoid(z))


def _dn_scan_fwd(w, u, qg, kd, a, gcs, pz, gn):
    S = w.shape[0]
    C = DN_CHUNK
    nc = S // C
    G = CHUNKS_SCAN
    R = G * C

    def body(w_ref, u_ref, qg_ref, kd_ref, a_ref, gcs_ref, z_ref, gn_ref, o_ref, vn_ref, sst_ref, out_ref, state):
        @pl.when(pl.program_id(0) == 0)
        def _():
            state[...] = jnp.zeros_like(state)

        def head_program(chunk, h):
            hs = _head(h)
            rows = slice(chunk * C, (chunk + 1) * C)
            s_in = state[h]
            sb = _bf(s_in)
            sst_ref[chunk, h] = sb
            w_s = _dot(w_ref[rows, hs], sb)
            q_s = _dot(qg_ref[rows, hs], sb)
            yield
            vn = u_ref[rows, hs] - w_s
            vnb = _bf(vn)
            o = q_s + _dot(a_ref[h, rows], _rows_pad(vnb))
            k_v = _dot_tn(kd_ref[rows, hs], vnb)
            yield
            state[h] = s_in * jnp.exp(gcs_ref[(chunk + 1) * C - 1:(chunk + 1) * C, h:h + 1]) + k_v
            o_ref[rows, hs] = o
            vn_ref[rows, hs] = vnb
            out_ref[rows, hs] = _bf(_gated_norm(o, z_ref[rows, hs], gn_ref[...]))

        for chunk in range(G):
            _interleave(head_program(chunk, h) for h in range(DN_HEADS))

    wide = pl.BlockSpec((R, 512), lambda n: (n, 0))
    f = lambda *shp: jax.ShapeDtypeStruct(shp, F32)
    b = lambda *shp: jax.ShapeDtypeStruct(shp, BF16)
    return pl.pallas_call(
        body, grid=(nc // G,), name="dn_scan_fwd",
        in_specs=[wide, wide, wide, wide, pl.BlockSpec((DN_HEADS, R, CPAD), lambda n: (0, n, 0)),
                  pl.BlockSpec((R, 128), lambda n: (n, 0)), pl.BlockSpec((R, 512), lambda n: (n, BLK_Z)),
                  pl.BlockSpec((1, DN_DIM), lambda n: (0, 0))],
        out_specs=[wide, wide, pl.BlockSpec((G, DN_HEADS, DN_DIM, DN_DIM), lambda n: (n, 0, 0, 0)), wide],
        out_shape=[f(S, 512), b(S, 512), b(nc, DN_HEADS, DN_DIM, DN_DIM), b(S, 512)],
        scratch_shapes=[pltpu.VMEM((DN_HEADS, DN_DIM, DN_DIM), F32)],
        compiler_params=_params(("arbitrary",)),
    )(w, u, qg, kd, a, gcs, pz, gn)


def _dn_scan_bwd(dcat, o, pz, gn, sst, vnew, w, qg, kd, a, gcs, dproj):
    S = o.shape[0]
    C = DN_CHUNK
    G = CHUNKS_SCAN
    R = G * C
    steps = S // R

    def body(dy_ref, o_ref, z_ref, gn_ref, sst_ref, vn_ref, w_ref, qg_ref, kd_ref, a_ref, gcs_ref, _,
             du_ref, dw_ref, dqg_ref, dkd_ref, da_ref, dz_ref, dsc_ref, dgn_ref, dstate):
        @pl.when(pl.program_id(0) == 0)
        def _():
            dstate[...] = jnp.zeros_like(dstate)
            dgn_ref[...] = jnp.zeros_like(dgn_ref)

        gn_ = gn_ref[...]
        lane = lax.broadcasted_iota(jnp.int32, (C, 128), 1)
        row = lax.broadcasted_iota(jnp.int32, (C, 128), 0)
        dgn_parts = []

        def head_program(chunk, h, dsc_parts):
            hs = _head(h)
            rows = slice(chunk * C, (chunk + 1) * C)
            ov, z, dout = o_ref[rows, hs], z_ref[rows, hs], dy_ref[rows, hs]
            r, oh = _rms_stats(ov)
            sg = _sigmoid(z)
            don = dout * (z * sg)
            dz_ref[rows, hs] = _bf(dout * (oh * gn_) * (sg * (1.0 + z * (1.0 - sg))))
            dgn_parts.append(jnp.sum(don * oh, axis=0, keepdims=True))
            dn = don * gn_
            do = _bf(r * (dn - oh * jnp.mean(dn * oh, axis=-1, keepdims=True)))
            sb = sst_ref[chunk, h]
            s_in = sb.astype(F32)
            ds_out = dstate[h]
            dsb = _bf(ds_out)
            vnb = vn_ref[rows, hs]
            wb, qgb, kdb, ab = w_ref[rows, hs], qg_ref[rows, hs], kd_ref[rows, hs], a_ref[h, rows]
            dvn = _dot_tn(ab, do)[:C] + _dot(kdb, dsb)
            yield
            da_ref[h, rows] = _dot_nt(do, _rows_pad(vnb))
            dqg_ref[rows, hs] = _dot_nt(do, sb)
            dkd_ref[rows, hs] = _dot_nt(vnb, dsb)
            q_do = _dot_tn(qgb, do)
            yield
            dvnb = _bf(dvn)
            dw_ref[rows, hs] = _bf(-_dot_nt(dvnb, sb))
            w_dvn = _dot_tn(wb, dvnb)
            du_ref[rows, hs] = dvnb
            yield
            d_last = jnp.exp(gcs_ref[(chunk + 1) * C - 1:(chunk + 1) * C, h:h + 1])
            dd = jnp.sum(jnp.sum(ds_out * s_in, axis=1, keepdims=True), axis=0, keepdims=True)
            dsc_parts.append(jnp.where((lane == h) & (row == C - 1), dd * d_last, 0.0))
            dstate[h] = ds_out * d_last + q_do - w_dvn

        for chunk in reversed(range(G)):
            dsc_parts = []
            _interleave(head_program(chunk, h, dsc_parts) for h in range(DN_HEADS))
            dsc_ref[chunk * C:(chunk + 1) * C, :] = sum(dsc_parts[1:], dsc_parts[0])
        dgn_ref[...] += sum(dgn_parts[1:], dgn_parts[0])

    rev = lambda n: steps - 1 - n
    wide = pl.BlockSpec((R, 512), lambda n: (rev(n), 0))
    z_spec = pl.BlockSpec((R, 512), lambda n: (rev(n), BLK_Z))
    sq = pl.BlockSpec((DN_HEADS, R, CPAD), lambda n: (0, rev(n), 0))
    narrow = pl.BlockSpec((R, 128), lambda n: (rev(n), 0))
    gn_spec = pl.BlockSpec((1, DN_DIM), lambda n: (0, 0))
    f = lambda *shp: jax.ShapeDtypeStruct(shp, F32)
    b = lambda *shp: jax.ShapeDtypeStruct(shp, BF16)
    return pl.pallas_call(
        body, grid=(steps,), name="dn_scan_bwd",
        in_specs=[pl.BlockSpec((R, 512), lambda n: (rev(n), 1)), wide, z_spec, gn_spec,
                  pl.BlockSpec((G, DN_HEADS, DN_DIM, DN_DIM), lambda n: (rev(n), 0, 0, 0)),
                  wide, wide, wide, wide, sq, narrow, pl.BlockSpec(memory_space=pl.ANY)],
        out_specs=[wide, wide, wide, wide, sq, z_spec, narrow, gn_spec],
        out_shape=[b(S, 512), b(S, 512), f(S, 512), f(S, 512), f(DN_HEADS, S, CPAD),
                   jax.ShapeDtypeStruct(dproj.shape, dproj.dtype), f(S, 128), f(1, DN_DIM)],
        scratch_shapes=[pltpu.VMEM((DN_HEADS, DN_DIM, DN_DIM), F32)],
        input_output_aliases={11: 5},
        compiler_params=_params(("arbitrary",)),
    )(dcat, o, pz, gn, sst, vnew, w, qg, kd, a, gcs, dproj)


def _dn_chunk_bwd(qkv, pg, t_inv, gcs, du, dw, dqg, dkd, da, dsc, a_log, dt_bias, dproj):
    S = qkv.shape[0]
    C = DN_CHUNK
    G = CHUNKS_LOCAL
    R = G * C

    def body(alog_ref, dtb_ref, qkv_ref, pg_ref, t_ref, gcs_ref, du_ref, dw_ref, dqg_ref, dkd_ref, da_ref, dsc_ref, _,
             dqkv_ref, dpg_ref, acc_ref):
        @pl.when(pl.program_id(0) == 0)
        def _():
            acc_ref[...] = jnp.zeros_like(acc_ref)

        ii, jj = _chunk_masks()
        lane = lax.broadcasted_iota(jnp.int32, (1, 128), 1)
        row8 = lax.broadcasted_iota(jnp.int32, (8, 128), 0)
        lane8 = lax.broadcasted_iota(jnp.int32, (8, 128), 1)
        rowc = lax.broadcasted_iota(jnp.int32, (C, 1), 0)
        tril, strict = jj <= ii, jj < ii
        dpg_parts, acc_parts = [[] for _ in range(G)], []

        def head_program(chunk, h):
            rows = slice(chunk * C, (chunk + 1) * C)
            q, k, v = qkv_ref[rows, _head(h)], qkv_ref[rows, _head(DN_HEADS + h)], qkv_ref[rows, _head(2 * DN_HEADS + h)]
            gc_col, beta, g_col = gcs_ref[rows, h:h + 1], gcs_ref[rows, DN_HEADS + h:DN_HEADS + h + 1], \
                gcs_ref[rows, 2 * DN_HEADS + h:2 * DN_HEADS + h + 1]
            dec = _decay(gc_col, ii, jj)
            eg = jnp.exp(gc_col)
            g_last = gc_col[C - 1:C, :]
            ek = jnp.exp(g_last - gc_col)
            kb, vb = k * beta, v * beta
            kbg = kb * eg
            qb, kbb = _bf(q), _bf(kb)
            k_rows = _rows_pad(_bf(k))
            t = t_ref[h, rows]
            tb = _bf(t)
            dub, dwb = du_ref[rows, _head(h)], dw_ref[rows, _head(h)]
            dqg_, dkd_ = dqg_ref[rows, _head(h)], dkd_ref[rows, _head(h)]
            dt = _dot_nt(dub, _rows_pad(_bf(vb))) + _dot_nt(dwb, _rows_pad(_bf(kbg)))
            t_du_dw = _dot_tn(tb, jnp.concatenate([dub, dwb], axis=1))
            dvb, dkbg = t_du_dw[:C, :DN_DIM], t_du_dw[:C, DN_DIM:]
            kk = _dot_nt(kbb, k_rows)
            qk = _dot_nt(qb, k_rows)
            yield
            dt_t = _dot3_nt(dt, t)
            yield
            dl = -_dot3_tn(t, dt_t)
            yield
            dm = jnp.where(strict, dl * dec, 0.0)
            dqk = jnp.where(tril, da_ref[h, rows] * dec, 0.0)
            gmat = dm * kk + dqk * qk
            dgc = jnp.sum(gmat, axis=1, keepdims=True) - _row_to_col(jnp.sum(gmat, axis=0, keepdims=True), ii, jj)
            dmb, dqkb = _bf(dm), _bf(dqk)
            yield
            dkb = _dot(dmb, k_rows) + dkbg * eg
            dk = _dot_tn(jnp.concatenate([dmb, dqkb], axis=0), jnp.concatenate([kbb, qb], axis=0))[:C] + dkd_ * ek
            dq = _dot(dqkb, k_rows) + dqg_ * eg
            yield
            tk = jnp.sum(dkd_ * k * ek, axis=1, keepdims=True)
            dgc = dgc + jnp.sum(dqg_ * q * eg, axis=1, keepdims=True) - tk + jnp.sum(dkbg * kbg, axis=1, keepdims=True)
            dgl = jnp.sum(tk, axis=0, keepdims=True) + dsc_ref[(chunk + 1) * C - 1:(chunk + 1) * C, h:h + 1]
            dgc = dgc + jnp.where(rowc == C - 1, dgl, 0.0)
            yield
            dk = dk + dkb * beta
            dbeta = jnp.sum(dkb * k, axis=1, keepdims=True) + jnp.sum(dvb * v, axis=1, keepdims=True)
            dqkv_ref[rows, _head(h)] = dq
            dqkv_ref[rows, _head(DN_HEADS + h)] = dk
            dqkv_ref[rows, _head(2 * DN_HEADS + h)] = dvb * beta
            dg_col = jnp.sum(jnp.where(jj >= ii, _col_to_row(dgc, ii, jj), 0.0), axis=1, keepdims=True)
            yield
            db = dbeta * beta * (1.0 - beta)
            da_in = dg_col * (-jnp.exp(alog_ref[h])) * _sigmoid(pg_ref[rows, DN_HEADS + h:DN_HEADS + h + 1] + dtb_ref[h])
            dpg_parts[chunk].append(jnp.where(lane == h, db, 0.0) + jnp.where(lane == DN_HEADS + h, da_in, 0.0))
            acc_parts.append(jnp.where((row8 == 0) & (lane8 == h), jnp.sum(dg_col * g_col, axis=0, keepdims=True), 0.0)
                             + jnp.where((row8 == 1) & (lane8 == h), jnp.sum(da_in, axis=0, keepdims=True), 0.0))

        _interleave(head_program(chunk, h) for chunk in range(G) for h in range(DN_HEADS))
        for chunk in range(G):
            dpg = sum(dpg_parts[chunk][1:], dpg_parts[chunk][0])
            dpg_ref[chunk * C:(chunk + 1) * C, :] = _bf(jnp.concatenate([dpg, jnp.zeros_like(dpg)], axis=1))
        acc_ref[...] += sum(acc_parts[1:], acc_parts[0])

    smem = pl.BlockSpec(memory_space=pltpu.SMEM)
    wide = pl.BlockSpec((R, 512), lambda n: (n, 0))
    sq = pl.BlockSpec((DN_HEADS, R, CPAD), lambda n: (0, n, 0))
    narrow = pl.BlockSpec((R, 128), lambda n: (n, 0))
    qkv_spec = pl.BlockSpec((R, 1536), lambda n: (n, 0))
    f = lambda *shp: jax.ShapeDtypeStruct(shp, F32)
    return pl.pallas_call(
        body, grid=(S // R,), name="dn_chunk_bwd",
        in_specs=[smem, smem, qkv_spec, pl.BlockSpec((R, 128), lambda n: (n, BLK_G)), sq, narrow, wide, wide, wide, wide, sq,
                  narrow, pl.BlockSpec(memory_space=pl.ANY)],
        out_specs=[qkv_spec, pl.BlockSpec((R, 256), lambda n: (n, BLK_G_PAD)), pl.BlockSpec((8, 128), lambda n: (0, 0))],
        out_shape=[f(S, 1536), jax.ShapeDtypeStruct(dproj.shape, dproj.dtype), f(8, 128)],
        input_output_aliases={12: 1},
        compiler_params=_params(("arbitrary",)),
    )(a_log, dt_bias, qkv, pg, t_inv, gcs, du, dw, dqg, dkd, da, dsc, dproj)


_W_IN_SECTIONS = ((0, 0, 512), (2304, 512, 512), (768, 1024, 1536), (512, 2560, 256), (2816, 2816, 8))
_HALF = D_MODEL // 2
_SECTION_ROWS = 256


def _pack_pairs(x):
    bits = lax.bitcast_convert_type(x, jnp.uint32)
    return lax.bitcast_convert_type(bits[:, _HALF:] | (bits[:, :_HALF] >> 16), F32)


def _unpack_pairs(words):
    bits = lax.bitcast_convert_type(words, jnp.uint32)
    return (lax.bitcast_convert_type(bits << 16, F32),
            lax.bitcast_convert_type(bits & jnp.uint32(0xFFFF0000), F32))


def _w_in_to_internal(packed):
    starts = [dst for _, dst, _ in _W_IN_SECTIONS] + [D_IN_PAD]

    def body(x_hbm, o_ref, words, sems):
        copies = [pltpu.make_async_copy(x_hbm.at[pl.ds(src, rows), 0, :], words.at[pl.ds(dst, rows)], sems.at[i])
                  for i, (src, dst, rows) in enumerate(_W_IN_SECTIONS)]
        for cp in copies:
            cp.start()
        words[pl.ds(D_IN, D_IN_PAD - D_IN), :] = jnp.zeros((D_IN_PAD - D_IN, _HALF), F32)
        for i, cp in enumerate(copies):
            cp.wait()
            for r in range(starts[i], starts[i + 1], _SECTION_ROWS):
                for c, h in enumerate(_unpack_pairs(words[pl.ds(r, _SECTION_ROWS), :])):
                    o_ref[pl.ds(r, _SECTION_ROWS), c * _HALF:(c + 1) * _HALF] = _bf(h)

    assert all((b - a) % _SECTION_ROWS == 0 for a, b in zip(starts, starts[1:])) and starts[-2] + _W_IN_SECTIONS[-1][2] == D_IN
    return pl.pallas_call(body, name="w_in_to_internal", out_shape=jax.ShapeDtypeStruct((D_IN_PAD, D_MODEL), BF16),
                          in_specs=[pl.BlockSpec(memory_space=pl.ANY)],
                          scratch_shapes=[pltpu.VMEM((D_IN_PAD, _HALF), F32), pltpu.SemaphoreType.DMA((len(_W_IN_SECTIONS),))],
                          compiler_params=pltpu.CompilerParams(vmem_limit_bytes=VMEM_LIMIT))(packed)


def _w_in_from_internal(gt):
    def body(g_ref, o_hbm, words, sems):
        copies = []
        for i, (dst, src, rows) in enumerate(_W_IN_SECTIONS):
            for r in range(src, src + rows, _SECTION_ROWS):
                n = max(min(_SECTION_ROWS, src + rows - r), 16)
                words[pl.ds(r, n), :] = _pack_pairs(g_ref[pl.ds(r, n), :].astype(F32))
            copies.append(pltpu.make_async_copy(words.at[pl.ds(src, rows)], o_hbm.at[pl.ds(dst, rows), 0, :], sems.at[i]))
            copies[-1].start()
        for cp in copies:
            cp.wait()

    return pl.pallas_call(body, name="w_in_from_internal", out_shape=jax.ShapeDtypeStruct((D_IN, 1, _HALF), F32),
                          out_specs=pl.BlockSpec(memory_space=pl.ANY),
                          scratch_shapes=[pltpu.VMEM((D_IN_PAD, _HALF), F32), pltpu.SemaphoreType.DMA((len(_W_IN_SECTIONS),))],
                          compiler_params=pltpu.CompilerParams(vmem_limit_bytes=VMEM_LIMIT))(gt)


def _local_step(x, p, target, wts, first_weights, other_weights, ship_early, after):
    S = x.shape[0]
    cos, sin = _rope_tables(S)
    sinks, a_log, dt_bias = wts["sinks"].reshape(8), wts["a_log"].reshape(4), wts["dt_bias"].reshape(4)
    gn = wts["dn_norm"].reshape(1, DN_DIM)
    add = lambda acc, res: (acc + res,)

    u = _rmsnorm_fwd(x, wts["norm_mix"], "norm_mix_fwd", after)
    w_in_t, conv_w = first_weights(u)
    proj, = _mm(u, w_in_t, form="nt", name="in_proj", out_dtypes=[F32], tn=512)
    attn, lse = _attn_fwd(proj, cos, sin, sinks)
    qkv = _dn_prep_fwd(proj, conv_w)
    cw, cu, cqg, ckd, ca, ct, gcs = _dn_chunk_fwd(qkv, proj, a_log, dt_bias)
    o, vnew, sst, dn_out = _dn_scan_fwd(cw, cu, cqg, ckd, ca, gcs, proj, gn)
    w_o, = other_weights(("w_o",), dn_out)
    h1, = _mm([attn, dn_out], w_o, form="nn", name="out_proj", out_dtypes=[F32], tn=512, epi=add, extra=[x])

    w_up, w_down = other_weights(("w_up", "w_down"), h1)
    hid, relu, m, h2 = _mlp_fwd(h1, w_up, w_down, wts["norm_mlp"])
    w_pg, w_pp = other_weights(("w_ple_gate", "w_ple_proj"), h2)
    n3, dh2, dgl, dpp, loss, d_norm_final, d_norm_ple = _ple_and_loss(h2, p, target, w_pg, w_pp, wts["norm_ple"],
                                                                     wts["norm_final"].reshape(1, D_MODEL))
    g = {"norm_final": d_norm_final, "norm_ple": d_norm_ple}
    early = {"w_ple_gate": _mm_tn(n3, dgl, name="d_w_ple_gate", tm=512, tn=1024, out_dtype=BF16).reshape(N_DEV, 128, 1024),
             "w_ple_proj": _mm_tn(p, dpp, name="d_w_ple_proj", tm=256, tn=128, out_dtype=BF16, column_shards=True)}
    d_act, = _mm(dh2, w_down, form="nt", name="d_hidden", out_dtypes=[BF16], tn=512,
                 epi=lambda acc, r: (acc * (2.0 * r.astype(F32)),), extra=[relu])
    early["w_down"] = _mm_tn(hid, dh2, name="d_w_down", tm=512, tn=1024, out_dtype=BF16).reshape(N_DEV, 512, 1024)
    early["w_up"] = _mm_tn(m, d_act, name="d_w_up", tm=1024, tn=512, out_dtype=BF16, column_shards=True)
    token = ship_early(early)
    dh1, g["norm_mlp"], dcat = _mm(d_act, w_up, form="nt", name="d_m", out_dtypes=[F32], tn=512, after=token,
                                   norm_bwd=(h1, wts["norm_mlp"], dh2), then_nt=w_o)
    d_w_o = _mm_tn([attn, dn_out], dh1, name="d_w_o", tm=512, tn=512, out_dtype=BF16)
    token = ship_early({"w_o": d_w_o.reshape(N_DEV, 128, 1024)})
    dproj, dk, dv, dsinks = _attn_bwd(proj, cos, sin, sinks, dcat, attn, lse, token)
    g["sinks"] = dsinks[:, 0].reshape(1, 8)
    du_, dw_, dqg, dkd, da, dproj, dsc, g["dn_norm"] = _dn_scan_bwd(dcat, o, proj, gn, sst, vnew, cw, cqg, ckd, ca, gcs, dproj)
    dqkv, dproj, gate_acc = _dn_chunk_bwd(qkv, proj, ct, gcs, du_, dw_, dqg, dkd, da, dsc, a_log, dt_bias, dproj)
    g["a_log"], g["dt_bias"] = gate_acc[0:1, 0:4], gate_acc[1:2, 0:4]
    dproj, g["conv_w"] = _dn_prep_bwd(proj, conv_w, dqkv, dproj, dk, dv)
    token = ship_early({"w_in": _mm_tn(dproj, u, name="d_w_in", tm=512, tn=1024, out_dtype=BF16)})
    grad_x, g["norm_mix"] = _mm(dproj, w_in_t, form="nn", name="d_u", out_dtypes=[F32], tn=512, after=token,
                                norm_bwd=(x, wts["norm_mix"], dh1))
    return loss, grad_x, g


def _peer(k):
    x, y, c = lax.axis_index("x"), lax.axis_index("y"), lax.axis_index("c")
    px = 1 - x if k & 4 else x
    py = 1 - y if k & 2 else y
    pc = 1 - c if k & 1 else c
    return (px, py, pc), 4 * px + 2 * py + pc


def _exchange(srcs, name, gather):
    n = len(srcs)
    gathers = list(gather) if isinstance(gather, (list, tuple)) else [gather] * n
    shapes = [(N_DEV,) + s.shape if gt else s.shape for s, gt in zip(srcs, gathers)]

    def body(*refs):
        src_refs, out_refs = refs[:n], refs[n:2 * n]
        send_sems, recv_sems, local_sems = refs[2 * n:]
        _, me = _peer(0)
        piece = lambda a, d: src_refs[a] if gathers[a] else src_refs[a].at[d]
        local = [pltpu.make_async_copy(piece(a, me), out_refs[a].at[me], local_sems.at[a]) for a in range(n)]
        for cp in local:
            cp.start()
        copies = []
        for a in range(n):
            for k in range(1, N_DEV):
                dev, idx = _peer(k)
                cp = pltpu.make_async_remote_copy(src_ref=piece(a, idx), dst_ref=out_refs[a].at[me],
                                                  send_sem=send_sems.at[a, k - 1], recv_sem=recv_sems.at[a, k - 1],
                                                  device_id=dev, device_id_type=MESH)
                cp.start()
                copies.append(cp)
        for cp in copies:
            cp.wait_recv()
        for cp in copies:
            cp.wait_send()
        for cp in local:
            cp.wait()

    anywhere = pl.BlockSpec(memory_space=pl.ANY)
    return pl.pallas_call(
        body, name=name, in_specs=[anywhere] * n, out_specs=[anywhere] * n,
        out_shape=[jax.ShapeDtypeStruct(shp, s.dtype) for shp, s in zip(shapes, srcs)],
        scratch_shapes=[pltpu.SemaphoreType.DMA((n, N_DEV - 1)), pltpu.SemaphoreType.DMA((n, N_DEV - 1)),
                        pltpu.SemaphoreType.DMA((n,))],
    )(*srcs)


_HBM = pl.BlockSpec(memory_space=pltpu.HBM)
_SEM = pl.BlockSpec(memory_space=pltpu.SEMAPHORE)
_EFFECT = pltpu.SideEffectType.DATAFLOW_SIDE_EFFECTING


def _split_copies(src_refs, land_refs, send_sems, recv_sems, modes, which=None):
    _, me = _peer(0)
    local, remote = [], []
    which = range(len(src_refs)) if which is None else which
    for a, src, land in zip(which, src_refs, land_refs):
        if modes[a] == "columns":
            n_cols = src.shape[1]
            dst = land.at[:, pl.ds(pl.multiple_of(me * n_cols, n_cols), n_cols)]
        else:
            dst = land.at[me]
        part = lambda d: src.at[d] if modes[a] == "pieces" else src
        local.append(pltpu.make_async_copy(part(me), dst, recv_sems.at[a * N_DEV]))
        for k in ((2, 4, 6) if modes[a] == "chips" else range(1, N_DEV)):
            dev, idx = _peer(k)
            sem = a * N_DEV + k
            remote.append(pltpu.make_async_remote_copy(
                src_ref=part(idx), dst_ref=dst, send_sem=send_sems.at[sem], recv_sem=recv_sems.at[sem],
                device_id=dev, device_id_type=MESH))
    return local, remote


def _forward_copies(land_refs, send_sems, recv_sems):
    c = lax.axis_index("c")
    sibling, _ = _peer(1)
    copies = []
    for a, land in enumerate(land_refs):
        for chip in range(N_DEV // 2):
            slot = 2 * chip + c
            sem = a * (N_DEV // 2) + chip
            copies.append(pltpu.make_async_remote_copy(
                src_ref=land.at[slot], dst_ref=land.at[slot], send_sem=send_sems.at[sem], recv_sem=recv_sems.at[sem],
                device_id=sibling, device_id_type=MESH))
    return copies


def _forward_start(lands, name):
    n = len(lands)

    def body(*refs):
        for cp in _forward_copies(refs[:n], refs[n], refs[n + 1]):
            cp.start()
        refs[-1][...] = jnp.zeros_like(refs[-1])

    sems = pltpu.SemaphoreType.DMA((n * (N_DEV // 2),))
    out = pl.pallas_call(
        body, name=name,
        out_shape=(sems, sems, *[pltpu.HBM(t.shape, t.dtype) for t in lands], jax.ShapeDtypeStruct((8, 128), F32)),
        in_specs=[_HBM] * n, out_specs=(_SEM, _SEM, *[_HBM] * n, pl.BlockSpec(memory_space=pltpu.VMEM)),
        input_output_aliases={i: 2 + i for i in range(n)},
        compiler_params=pltpu.CompilerParams(has_side_effects=_EFFECT),
    )(*[pltpu.with_memory_space_constraint(t, pltpu.HBM) for t in lands])
    return out[:-1], out[-1]


def _forward_wait(handle, after, name):
    send_sems, recv_sems, *lands = handle
    n = len(lands)

    def body(*refs):
        for cp in _forward_copies(refs[:n], refs[n], refs[n + 1]):
            cp.wait_send()
            cp.wait_recv()

    return list(pl.pallas_call(
        body, name=name, out_shape=tuple(pltpu.HBM(t.shape, t.dtype) for t in lands),
        in_specs=[_HBM] * n + [_SEM, _SEM, pl.BlockSpec(memory_space=pl.ANY)], out_specs=tuple([_HBM] * n),
        input_output_aliases={i: i for i in range(n)},
        compiler_params=pltpu.CompilerParams(has_side_effects=_EFFECT),
    )(*lands, send_sems, recv_sems, after))


def _exchange_start(srcs, name, modes):
    n = len(srcs)
    modes = [modes] * n if isinstance(modes, str) else list(modes)
    lands = []
    for s, mode in zip(srcs, modes):
        shape = {"columns": (s.shape[0], N_DEV * s.shape[1]), "pieces": s.shape}.get(mode, (N_DEV,) + s.shape)
        lands.append(lax.empty(shape, s.dtype))

    def body(*refs):
        src_refs, land_refs = refs[:n], refs[n:2 * n]
        send_sems, recv_sems = refs[2 * n], refs[2 * n + 1]
        local, remote = _split_copies(src_refs, land_refs, send_sems, recv_sems, modes)
        for cp in local + remote:
            cp.start()
        refs[-1][...] = jnp.zeros_like(refs[-1])

    both = list(srcs) + lands
    sems = pltpu.SemaphoreType.DMA((n * N_DEV,))
    out = pl.pallas_call(
        body, name=name,
        out_shape=(sems, sems, *[pltpu.HBM(t.shape, t.dtype) for t in both], jax.ShapeDtypeStruct((8, 128), F32)),
        in_specs=[_HBM] * (2 * n), out_specs=(_SEM, _SEM, *[_HBM] * (2 * n), pl.BlockSpec(memory_space=pltpu.VMEM)),
        input_output_aliases={i: 2 + i for i in range(2 * n)},
        compiler_params=pltpu.CompilerParams(has_side_effects=_EFFECT),
    )(*[pltpu.with_memory_space_constraint(t, pltpu.HBM) for t in both])
    return (n, modes, out[:-1]), out[-1]


def _exchange_wait(handle, after, name, which=None):
    n_all, modes, (send_sems, recv_sems, *both_all) = handle
    which = list(range(n_all)) if which is None else list(which)
    n = len(which)
    both = [both_all[a] for a in which] + [both_all[n_all + a] for a in which]

    def body(*refs):
        src_refs, land_refs = refs[:n], refs[n:2 * n]
        local, remote = _split_copies(src_refs, land_refs, refs[2 * n], refs[2 * n + 1], modes, which)
        for cp in local:
            cp.wait()
        for cp in remote:
            cp.wait_send()
            cp.wait_recv()

    out = pl.pallas_call(
        body, name=name, out_shape=tuple(pltpu.HBM(t.shape, t.dtype) for t in both),
        in_specs=[_HBM] * (2 * n) + [_SEM, _SEM, pl.BlockSpec(memory_space=pl.ANY)], out_specs=tuple([_HBM] * (2 * n)),
        input_output_aliases={i: i for i in range(2 * n)},
        compiler_params=pltpu.CompilerParams(has_side_effects=_EFFECT),
    )(*both, send_sems, recv_sems, after)
    return list(out[n:])


def _cast_all(arrays, name, after):
    waits = [] if after is None else [after]

    def body(*refs):
        for src, dst in zip(refs[:len(arrays)], refs[len(arrays) + len(waits):]):
            if len(src.shape) == 2:
                dst[...] = _bf(src[...])
            else:
                dst[:, 0, :] = _pack_pairs(_bf(src[:, 0, :]).astype(F32))

    shapes = [jax.ShapeDtypeStruct(a.shape, BF16) if a.ndim == 2 else jax.ShapeDtypeStruct((a.shape[0], 1, a.shape[2] // 2), F32)
              for a in arrays]
    return pl.pallas_call(body, name=name, out_shape=shapes,
                          compiler_params=pltpu.CompilerParams(vmem_limit_bytes=VMEM_LIMIT))(*arrays, *waits)


def _adam_update(g, w, m, v):
    nm = ADAM_B1 * m + (1.0 - ADAM_B1) * g
    nv = ADAM_B2 * v + (1.0 - ADAM_B2) * (g * g)
    m_hat = nm / (1.0 - ADAM_B1 ** ADAM_STEP)
    v_hat = nv / (1.0 - ADAM_B2 ** ADAM_STEP)
    return -ADAM_LR * (m_hat / (jnp.sqrt(v_hat) + ADAM_EPS) + ADAM_WD * w), nm, nv


def _adamw(parts, w, m, v, name):
    n, R, W = parts.shape
    tm = 128 if R % 128 == 0 else R

    def body(p_ref, w_ref, m_ref, v_ref, g_ref, d_ref, nm_ref, nv_ref):
        g = p_ref[0].astype(F32)
        for s in range(1, n):
            g = g + p_ref[s].astype(F32)
        g_ref[...] = g
        d_ref[...], nm_ref[...], nv_ref[...] = _adam_update(g, w_ref[...], m_ref[...], v_ref[...])

    tile = pl.BlockSpec((tm, W), lambda i: (i, 0))
    if R // tm < 4:
        return pl.pallas_call(
            body, grid=(R // tm,), name=name,
            in_specs=[pl.BlockSpec((n, tm, W), lambda i: (0, i, 0)), tile, tile, tile],
            out_specs=[tile] * 4, out_shape=[jax.ShapeDtypeStruct((R, W), F32)] * 4,
            compiler_params=_params(("parallel",)),
        )(parts, w, m, v)

    stream = lambda shape, index: pl.BlockSpec(shape, index, pipeline_mode=pl.Buffered(3))

    def pipeline(*hbm_refs):
        pltpu.emit_pipeline(
            body, grid=(R // tm,),
            in_specs=[stream((n, tm, W), lambda i: (0, i, 0))] + [stream((tm, W), lambda i: (i, 0))] * 3,
            out_specs=[tile] * 4)(*hbm_refs)

    anywhere = pl.BlockSpec(memory_space=pl.ANY)
    return pl.pallas_call(
        pipeline, name=name, in_specs=[anywhere] * 4, out_specs=[anywhere] * 4,
        out_shape=[jax.ShapeDtypeStruct((R, W), F32)] * 4,
        compiler_params=pltpu.CompilerParams(vmem_limit_bytes=VMEM_LIMIT),
    )(parts, w, m, v)


def _adamw_rows_apart(parts, w, m, v, name):
    n, R, _, half = parts.shape

    def body(p_hbm, w_hbm, m_hbm, v_hbm, *rest):
        out_hbm, (words, given, results, sems) = rest[:4], rest[4:]
        loads = [pltpu.make_async_copy(p_hbm.at[s, :, 0, :], words.at[s], sems.at[s]) for s in range(n)]
        loads += [pltpu.make_async_copy(h.at[:, 0, :], given.at[i], sems.at[n + i]) for i, h in enumerate((w_hbm, m_hbm, v_hbm))]
        for cp in loads:
            cp.start()
        for cp in loads:
            cp.wait()
        part = lambda s: jnp.concatenate(_unpack_pairs(words[s]), axis=1)
        g = part(0)
        for s in range(1, n):
            g = g + part(s)
        stores = []
        for i, val in enumerate((g,) + _adam_update(g, given[0], given[1], given[2])):
            results[i] = val
            stores.append(pltpu.make_async_copy(results.at[i], out_hbm[i].at[:, 0, :], sems.at[n + 3 + i]))
            stores[-1].start()
        for cp in stores:
            cp.wait()

    anywhere = pl.BlockSpec(memory_space=pl.ANY)
    return pl.pallas_call(
        body, name=name, in_specs=[anywhere] * 4, out_specs=[anywhere] * 4,
        out_shape=[jax.ShapeDtypeStruct(w.shape, F32)] * 4,
        scratch_shapes=[pltpu.VMEM((n, R, half), F32), pltpu.VMEM((3, R, 2 * half), F32), pltpu.VMEM((4, R, 2 * half), F32),
                        pltpu.SemaphoreType.DMA((n + 7,))],
        compiler_params=pltpu.CompilerParams(vmem_limit_bytes=VMEM_LIMIT),
    )(parts, w, m, v)


_MATRICES = ("w_in", "w_o", "w_up", "w_down", "w_ple_gate", "w_ple_proj")


_OTHERS = ("w_o", "w_up", "w_down", "w_ple_gate", "w_ple_proj")
_OTHER_MODES = {"w_o": "slots", "w_up": "slots", "w_down": "slots", "w_ple_gate": "slots", "w_ple_proj": "columns"}


_VECTORS = ("norm_mix", "norm_mlp", "norm_ple", "norm_final", "a_log", "dt_bias", "sinks", "dn_norm")
_SMALL_ROWS, _LOSS_ROW, _CONV_ROW = 16, 8, 9


def _pack_small(vectors, loss, conv):
    def body(*refs):
        out = refs[-1]
        out[...] = jnp.zeros_like(out)
        for r, ref in enumerate(refs[:len(_VECTORS)]):
            out[r:r + 1, 0:ref.shape[1]] = ref[...]
        out[_LOSS_ROW:_LOSS_ROW + 1, 0:128] = refs[len(_VECTORS)][...]
        out[_CONV_ROW:_CONV_ROW + 6, :] = refs[len(_VECTORS) + 1][...]

    return pl.pallas_call(body, name="pack_small", out_shape=jax.ShapeDtypeStruct((_SMALL_ROWS, 1024), F32))(*vectors, loss, conv)


def _sum_slots(parts):
    def body(p_ref, o_ref):
        acc = p_ref[0]
        for s in range(1, parts.shape[0]):
            acc = acc + p_ref[s]
        o_ref[...] = acc

    return pl.pallas_call(body, name="sum_small", out_shape=jax.ShapeDtypeStruct(parts.shape[1:], parts.dtype))(parts)


def _adamw_vectors(summed, conv_g, wmv):
    names = _VECTORS + ("conv_w",)
    flat = [a for triple in wmv for a in triple]

    def body(*refs):
        sum_ref, conv_ref = refs[0], refs[1]
        ins, outs = refs[2:2 + len(flat)], refs[2 + len(flat):]
        for i in range(len(names)):
            w_ref, m_ref, v_ref = ins[3 * i:3 * i + 3]
            g = conv_ref[...] if i == len(_VECTORS) else sum_ref[i:i + 1, 0:w_ref.shape[1]]
            outs[4 * i][...] = g
            outs[4 * i + 1][...], outs[4 * i + 2][...], outs[4 * i + 3][...] = _adam_update(g, w_ref[...], m_ref[...], v_ref[...])

    out_shape = [jax.ShapeDtypeStruct(t[0].shape, F32) for t in wmv for _ in range(4)]
    res = pl.pallas_call(body, name="adamw_vectors", out_shape=out_shape)(summed, conv_g, *flat)
    return {n: res[4 * i:4 * i + 4] for i, n in enumerate(names)}


_ORDER = ("norm_mix", "w_in", "conv_w", "a_log", "dt_bias", "dn_norm", "sinks", "w_o", "norm_mlp", "w_up", "w_down",
          "norm_ple", "w_ple_gate", "w_ple_proj", "norm_final")


def kernel(x, p, norm_mix, w_in, conv_w, a_log, dt_bias, dn_norm, sinks, w_o, norm_mlp, w_up, w_down, norm_ple, w_ple_gate, w_ple_proj, norm_final, loss_target, m_norm_mix, m_w_in, m_conv_w, m_a_log, m_dt_bias, m_dn_norm, m_sinks, m_w_o, m_norm_mlp, m_w_up, m_w_down, m_norm_ple, m_w_ple_gate, m_w_ple_proj, m_norm_final, v_norm_mix, v_w_in, v_conv_w, v_a_log, v_dt_bias, v_dn_norm, v_sinks, v_w_o, v_norm_mlp, v_w_up, v_w_down, v_norm_ple, v_w_ple_gate, v_w_ple_proj, v_norm_final):
    w = dict(norm_mix=norm_mix, w_in=w_in, conv_w=conv_w[0], a_log=a_log, dt_bias=dt_bias, dn_norm=dn_norm, sinks=sinks,
             w_o=w_o[0], norm_mlp=norm_mlp, w_up=w_up[0], w_down=w_down[0], norm_ple=norm_ple, w_ple_gate=w_ple_gate[0],
             w_ple_proj=w_ple_proj[0], norm_final=norm_final)
    m = dict(norm_mix=m_norm_mix, w_in=m_w_in, conv_w=m_conv_w[0], a_log=m_a_log, dt_bias=m_dt_bias, dn_norm=m_dn_norm,
             sinks=m_sinks, w_o=m_w_o[0], norm_mlp=m_norm_mlp, w_up=m_w_up[0], w_down=m_w_down[0], norm_ple=m_norm_ple,
             w_ple_gate=m_w_ple_gate[0], w_ple_proj=m_w_ple_proj[0], norm_final=m_norm_final)
    v = dict(norm_mix=v_norm_mix, w_in=v_w_in, conv_w=v_conv_w[0], a_log=v_a_log, dt_bias=v_dt_bias, dn_norm=v_dn_norm,
             sinks=v_sinks, w_o=v_w_o[0], norm_mlp=v_norm_mlp, w_up=v_w_up[0], w_down=v_w_down[0], norm_ple=v_norm_ple,
             w_ple_gate=v_w_ple_gate[0], w_ple_proj=v_w_ple_proj[0], norm_final=v_norm_final)
    me = 4 * lax.axis_index("x") + 2 * lax.axis_index("y") + lax.axis_index("c")
    conv_shard = conv_w.shape[2]

    for d in (w, m, v):
        d["w_in"] = jnp.transpose(d["w_in"], (2, 0, 1))
    conv_pad = jnp.pad(w["conv_w"], ((0, 8 - DN_CONV), (0, 256 - conv_shard)))
    w_in_shard, = _cast_all([w["w_in"]], "cast_w_in", None)
    gathers_first, token_first = _exchange_start([w_in_shard, conv_pad], "gather_first_start", "chips")
    shards = _cast_all([w[n] for n in _OTHERS], "cast_others", token_first)
    gathers, token_gather = _exchange_start(list(shards), "gather_start", [_OTHER_MODES[n] for n in _OTHERS])

    def first_weights(after):
        over_ici = _exchange_wait(gathers_first, after, "gather_first_wait")
        handle, token = _forward_start(over_ici, "gather_first_forward")
        w_in_all, conv_all = _forward_wait(handle, token, "gather_first_forward_wait")
        conv_all = jnp.transpose(conv_all[:, :DN_CONV, :conv_shard], (1, 0, 2)).reshape(DN_CONV, N_DEV * conv_shard)
        return _w_in_to_internal(w_in_all.reshape(D_IN, 1, _HALF)), conv_all

    as_taken = {"w_o": lambda t: t.reshape(1024, 1024), "w_up": lambda t: t, "w_down": lambda t: t.reshape(4096, 1024),
                "w_ple_gate": lambda t: t.reshape(1024, 1024), "w_ple_proj": lambda t: t}

    def other_weights(names, after):
        which = [_OTHERS.index(n) for n in names]
        got = _exchange_wait(gathers, after, "gather_wait_" + names[0], which)
        return [as_taken[n](t) for n, t in zip(names, got)]

    shipped = []

    def ship_early(pieces):
        names = tuple(pieces)
        if names == ("w_in",):
            pieces = {"w_in": _w_in_from_internal(pieces["w_in"]).reshape(N_DEV, D_IN // N_DEV, 1, _HALF)}
        handle, token = _exchange_start([pieces[n] for n in names], "scatter_start_" + names[0], "pieces")
        shipped.append((names, handle))
        return token

    loss, grad_x, g = _local_step(x[0], p[0, 0], loss_target[0], w, first_weights, other_weights, ship_early, token_gather)

    row = lambda t: t.reshape(1, t.size)
    small = _pack_small([row(g[n]) for n in _VECTORS], loss, g["conv_w"].reshape(6, 1024))
    small_handle, token_small = _exchange_start([small], "gather_small_start", "slots")
    big, after = {}, token_small
    for names, handle in shipped[:-1]:
        for n, r in zip(names, _exchange_wait(handle, after, "scatter_wait_" + names[0])):
            big[n] = _adamw(r, w[n], m[n], v[n], "adamw_" + n)
            after = big[n][1]
    small_all, = _exchange_wait(small_handle, after, "gather_small_wait")
    summed = _sum_slots(small_all)
    conv_g = lax.dynamic_slice(summed[_CONV_ROW:_CONV_ROW + 6].reshape(DN_CONV, N_DEV * conv_shard), (0, me * conv_shard),
                               (DN_CONV, conv_shard))
    small_out = _adamw_vectors(summed, conv_g, [(row(w[n]), row(m[n]), row(v[n])) for n in _VECTORS]
                               + [(w["conv_w"], m["conv_w"], v["conv_w"])])
    names, handle = shipped[-1]
    for n, r in zip(names, _exchange_wait(handle, small_out["conv_w"][0], "scatter_wait_" + names[0])):
        big[n] = _adamw_rows_apart(r, w[n], m[n], v[n], "adamw_" + n)

    result = [summed[_LOSS_ROW, 0], grad_x[None]]
    for i in range(4):
        for n in _ORDER:
            if n == "w_in":
                result.append(jnp.transpose(big[n][i], (1, 2, 0)))
            elif n in _MATRICES:
                result.append(big[n][i][None])
            elif n == "conv_w":
                result.append(small_out[n][i][None])
            else:
                result.append(small_out[n][i].reshape(w[n].shape))
    return tuple(result)
```

```python
import jax
import jax.numpy as jnp
import numpy as np
from jax import lax
from jax.experimental import pallas as pl
from jax.experimental.pallas import tpu as pltpu

F32, BF16 = jnp.float32, jnp.bfloat16
EPS = 1e-6
D_MODEL = 1024
N_DEV = 8
ATTN_BLOCK = 128
HEAD_PAIR = 128
DN_HEADS = 4
DN_DIM = 128
DN_CHUNK = 64
DN_CONV = 4
ROPE_THETA = 10000.0
D_IN = 2824
D_IN_PAD = 3072
BLK_Q, BLK_Z = 0, 1
BLK_DN, BLK_K, BLK_V, BLK_G = 8, 20, 21, 22
BLK_G_PAD = 11
VMEM_LIMIT = 56 * 1024 * 1024
NEG = -1e30
ADAM_LR, ADAM_B1, ADAM_B2, ADAM_EPS, ADAM_WD, ADAM_STEP = 0.001, 0.9, 0.999, 1e-08, 0.01, 10
MESH = pl.DeviceIdType.MESH


def _bf(x):
    return x.astype(BF16)


def _dot(a, b):
    return jnp.dot(a, b, preferred_element_type=F32)


def _dot_nt(a, b):
    return lax.dot_general(a, b, (((1,), (1,)), ((), ())), preferred_element_type=F32)


def _dot_tn(a, b):
    return lax.dot_general(a, b, (((0,), (0,)), ((), ())), preferred_element_type=F32)


def _sigmoid(x):
    return 1.0 / (1.0 + jnp.exp(-x))


def _params(sem):
    return pltpu.CompilerParams(dimension_semantics=sem, vmem_limit_bytes=VMEM_LIMIT)


def _mm(x, w, *, form, name, out_dtypes, tn, epi=None, extra=(), tm=512, w_row_block=0, after=None, norm=None,
        norm_bwd=None, then_nt=None):
    assert norm is None or norm_bwd is None
    xs = list(x) if isinstance(x, (list, tuple)) else [x]
    nx = len(xs)
    S, K = xs[0].shape
    shards = w.ndim == 3
    N = (w.shape[2] * N_DEV if shards else w.shape[1]) if form == "nn" else w.shape[-2]
    assert not (shards and form == "nn" and tn != w.shape[2]) and (nx == 1 or (form == "nn" and not shards and norm is None))
    r0 = w_row_block * K
    tm = min(tm, S)
    n_extra, n_out = len(extra), len(out_dtypes)
    tile = lambda width: pl.BlockSpec((tm, width), lambda i: (i, 0))
    whole = lambda a: pl.BlockSpec(a.shape, lambda i, nd=a.ndim: (0,) * nd)
    ins, in_specs = [*xs, w, *extra], [tile(K)] * nx + [whole(w)] + [tile(N)] * n_extra
    if norm is not None:
        ins, in_specs = ins + [norm], in_specs + [whole(norm)]
    if norm_bwd is not None:
        ins, in_specs = ins + list(norm_bwd), in_specs + [tile(N), whole(norm_bwd[1]), tile(N)]
    if then_nt is not None:
        ins, in_specs = ins + [then_nt], in_specs + [whole(then_nt)]
    if after is not None:
        ins, in_specs = ins + [after], in_specs + [whole(after)]
    out_shape = [jax.ShapeDtypeStruct((S, N), dt) for dt in out_dtypes]
    out_specs = [tile(N)] * n_out
    if norm is not None:
        out_shape, out_specs = out_shape + [jax.ShapeDtypeStruct((S, K), BF16)], out_specs + [tile(K)]
    if norm_bwd is not None:
        out_shape, out_specs = out_shape + [jax.ShapeDtypeStruct((1, N), F32)], out_specs + [pl.BlockSpec((1, N), lambda i: (0, 0))]
    if then_nt is not None:
        out_shape, out_specs = out_shape + [jax.ShapeDtypeStruct((S, then_nt.shape[0]), F32)], out_specs + [tile(then_nt.shape[0])]

    def product(xb, w_ref, cols, c):
        if form == "nn" and nx > 1:
            return sum(_dot(part, w_ref[r0 + p * K:r0 + (p + 1) * K, cols]) for p, part in enumerate(xb))
        if form == "nn":
            return _dot(xb, w_ref[c] if shards else w_ref[r0:r0 + K, cols])
        if not shards:
            return _dot_nt(xb, w_ref[cols, :])
        ks = w.shape[2]
        acc = _dot_nt(xb[:, 0:ks], w_ref[0, cols, :])
        for s in range(1, N_DEV):
            acc = acc + _dot_nt(xb[:, s * ks:(s + 1) * ks], w_ref[s, cols, :])
        return acc

    def body(*refs):
        x_ref, w_ref = refs[0], refs[nx]
        extra_refs = refs[nx + 1:nx + 1 + n_extra]
        at = nx + 1 + n_extra
        if norm is not None:
            gain_ref, at = refs[at], at + 1
        if norm_bwd is not None:
            (y_ref, ygain_ref, dres_ref), at = refs[at:at + 3], at + 3
        if then_nt is not None:
            w2_ref, at = refs[at], at + 1
        outs = refs[len(ins):]
        if norm is not None:
            _, xh = _rms_stats(x_ref[...])
            xb = _bf(xh * gain_ref[...])
            outs[n_out][...] = xb
        else:
            xb = _bf(x_ref[...]) if nx == 1 else [_bf(r[...]) for r in refs[:nx]]
        for c in range(N // tn):
            cols = slice(c * tn, (c + 1) * tn)
            acc = product(xb, w_ref, cols, c)
            res = epi(acc, *[r[:, cols] for r in extra_refs]) if epi else (acc,)
            for o, r in zip(outs[:n_out], res):
                o[:, cols] = r.astype(o.dtype)
        if norm_bwd is not None:
            dx, dg = _rms_bwd_tile(y_ref[...], ygain_ref[...], outs[0][...])
            outs[0][...] = dres_ref[...] + dx
            dg_ref = outs[n_out]

            @pl.when(pl.program_id(0) == 0)
            def _():
                dg_ref[...] = jnp.zeros_like(dg_ref)

            dg_ref[...] += dg
        if then_nt is not None:
            yb = _bf(outs[0][...])
            for c in range(then_nt.shape[0] // tn):
                cols = slice(c * tn, (c + 1) * tn)
                outs[-1][:, cols] = _dot_nt(yb, w2_ref[cols, :])

    return pl.pallas_call(
        body, grid=(S // tm,), name=name, in_specs=in_specs, out_specs=out_specs, out_shape=out_shape,
        compiler_params=_params(("arbitrary",) if norm_bwd is not None else ("parallel",)),
    )(*ins)


def _mlp_fwd(h1, w_up, w_down, gain):
    S, K = h1.shape
    n_sh, _, fs = w_up.shape
    tm = min(512, S)

    def body(x_ref, wup_ref, wdown_ref, g_ref, hid_ref, relu_ref, m_ref, h2_ref):
        x = x_ref[...]
        _, xh = _rms_stats(x)
        mb = _bf(xh * g_ref[...])
        m_ref[...] = mb
        h2_ref[...] = x
        for c in range(n_sh):
            cols = slice(c * fs, (c + 1) * fs)
            r = jnp.maximum(_dot(mb, wup_ref[c]), 0.0)
            hd = _bf(r * r)
            hid_ref[:, cols] = hd
            relu_ref[:, cols] = _bf(r)
            h2_ref[...] += _dot(hd, wdown_ref[cols, :])

    tile = lambda width: pl.BlockSpec((tm, width), lambda i: (i, 0))
    once = lambda a: pl.BlockSpec(a.shape, lambda i, nd=a.ndim: (0,) * nd, pipeline_mode=pl.Buffered(1))
    F = n_sh * fs
    return pl.pallas_call(
        body, grid=(S // tm,), name="mlp_fwd",
        in_specs=[tile(K), once(w_up), once(w_down), pl.BlockSpec(gain.shape, lambda i: (0, 0))],
        out_specs=[tile(F), tile(F), tile(K), tile(K)],
        out_shape=[jax.ShapeDtypeStruct((S, F), BF16), jax.ShapeDtypeStruct((S, F), BF16),
                   jax.ShapeDtypeStruct((S, K), BF16), jax.ShapeDtypeStruct((S, K), F32)],
        compiler_params=_params(("parallel",)),
    )(h1, w_up, w_down, gain)


def _mm_tn(x, dy, *, name, tm, tn, out_dtype=F32, column_shards=False, after=None):
    xs = list(x) if isinstance(x, (list, tuple)) else [x]
    S, N = dy.shape
    K = x.shape[1] if len(xs) == 1 else tm * len(xs)
    waits = [] if after is None else [after]

    def body(*refs):
        dy_ref, out_ref = refs[len(xs)], refs[-1]
        if len(xs) == 1:
            out_ref[...] = _dot_tn(_bf(refs[0][...]), _bf(dy_ref[...])).astype(out_dtype)
        for k in range(len(xs) if len(xs) > 1 else 0):
            @pl.when(pl.program_id(0) == k)
            def _(k=k):
                out_ref[...] = _dot_tn(_bf(refs[k][...]), _bf(dy_ref[...])).astype(out_dtype)

    if column_shards:
        out_spec = pl.BlockSpec((None, tm, tn), lambda i, j: (j, i, 0))
        out_shape = jax.ShapeDtypeStruct((N // tn, K, tn), out_dtype)
    else:
        out_spec = pl.BlockSpec((tm, tn), lambda i, j: (i, j))
        out_shape = jax.ShapeDtypeStruct((K, N), out_dtype)
    return pl.pallas_call(
        body, grid=(K // tm, N // tn), name=name,
        in_specs=([pl.BlockSpec((S, tm), lambda i, j: (0, i))] if len(xs) == 1 else [pl.BlockSpec((S, tm), lambda i, j: (0, 0))] * len(xs))
        + [pl.BlockSpec((S, tn), lambda i, j: (0, j))] + [pl.BlockSpec(memory_space=pl.ANY)] * len(waits),
        out_specs=out_spec, out_shape=out_shape,
        compiler_params=_params(("parallel", "parallel")),
    )(*xs, dy, *waits)


def _rowwise(body, *, tiled, full, out_tiled, out_acc, name, tm=512, smem=()):
    S = tiled[0].shape[0]
    tm = min(tm, S)
    n_in = len(smem) + len(tiled) + len(full)

    def kern(*refs):
        @pl.when(pl.program_id(0) == 0)
        def _():
            for r in refs[n_in + len(out_tiled):]:
                r[...] = jnp.zeros_like(r)
        body(*refs)

    in_specs = [pl.BlockSpec(memory_space=pltpu.SMEM) for _ in smem]
    in_specs += [pl.BlockSpec((tm, a.shape[1]), lambda i: (i, 0)) for a in tiled]
    in_specs += [pl.BlockSpec(a.shape, lambda i, nd=a.ndim: (0,) * nd) for a in full]
    out_specs = [pl.BlockSpec((tm, w), lambda i: (i, 0)) for w, _ in out_tiled]
    out_specs += [pl.BlockSpec(shp, lambda i, nd=len(shp): (0,) * nd) for shp, _ in out_acc]
    out_shape = [jax.ShapeDtypeStruct((S, w), dt) for w, dt in out_tiled]
    out_shape += [jax.ShapeDtypeStruct(shp, dt) for shp, dt in out_acc]
    return pl.pallas_call(
        kern, grid=(S // tm,), name=name, in_specs=in_specs, out_specs=out_specs, out_shape=out_shape,
        compiler_params=_params(("arbitrary",)),
    )(*smem, *tiled, *full)


def _rms_stats(x):
    r = lax.rsqrt(jnp.mean(x * x, axis=-1, keepdims=True) + EPS)
    return r, x * r


def _rmsnorm_fwd(x, g, name, after):
    def body(x_ref, g_ref, _, o_ref):
        _, xh = _rms_stats(x_ref[...])
        o_ref[...] = _bf(xh * g_ref[...])

    return _rowwise(body, tiled=[x], full=[g, after], out_tiled=[(x.shape[1], BF16)], out_acc=[], name=name)[0]


def _rms_bwd_tile(x, g, dxn):
    r, xh = _rms_stats(x)
    dg = jnp.sum(dxn * xh, axis=0, keepdims=True)
    dn = dxn * g
    dx = r * (dn - xh * jnp.mean(dn * xh, axis=-1, keepdims=True))
    return dx, dg


def _ple_and_loss(h2, p, target, w_pg, w_pp, g_ple, g_final):
    S, n = h2.shape
    tm = min(512, S)
    tn = 512

    def body(h2_ref, p_ref, t_ref, wpg_ref, wpp_ref, gple_ref, gfin_ref,
             n3_ref, dh_ref, dgl_ref, dpp_ref, loss_ref, dg_ref, dgple_ref, pp, gate, h3):
        @pl.when(pl.program_id(0) == 0)
        def _():
            loss_ref[...] = jnp.zeros_like(loss_ref)
            dg_ref[...] = jnp.zeros_like(dg_ref)
            dgple_ref[...] = jnp.zeros_like(dgple_ref)

        x = h2_ref[...]
        _, xh = _rms_stats(x)
        n3 = _bf(xh * gple_ref[...])
        n3_ref[...] = n3
        pb = _bf(p_ref[...])
        for c in range(n // tn):
            cols = slice(c * tn, (c + 1) * tn)
            pp[:, cols] = _dot(pb, wpp_ref[:, cols])
            gt = _sigmoid(_dot(n3, wpg_ref[:, cols]))
            gate[:, cols] = gt
            h3[:, cols] = x[:, cols] + gt * pp[:, cols]
        y = h3[...]
        _, yh = _rms_stats(y)
        e = yh * gfin_ref[...] - t_ref[...]
        per_tok = jnp.mean(e * e, axis=-1, keepdims=True)
        loss_ref[...] += 0.5 * jnp.sum(per_tok, axis=0, keepdims=True)
        dh, dg = _rms_bwd_tile(y, gfin_ref[...], e * (1.0 / n))
        dg_ref[...] += dg
        gt = gate[...]
        dgl = _bf(dh * pp[...] * gt * (1.0 - gt))
        dgl_ref[...] = dgl
        dpp_ref[...] = _bf(dh * gt)
        for c in range(n // tn):
            cols = slice(c * tn, (c + 1) * tn)
            h3[:, cols] = _dot_nt(dgl, wpg_ref[cols, :])
        dx, dgp = _rms_bwd_tile(x, gple_ref[...], h3[...])
        dh_ref[...] = dh + dx
        dgple_ref[...] += dgp

    tile = lambda width: pl.BlockSpec((tm, width), lambda i: (i, 0))
    whole = lambda a: pl.BlockSpec(a.shape, lambda i, nd=a.ndim: (0,) * nd)
    return pl.pallas_call(
        body, grid=(S // tm,), name="ple_and_loss",
        in_specs=[tile(n), tile(p.shape[1]), tile(n), whole(w_pg), whole(w_pp), whole(g_ple), whole(g_final)],
        out_specs=[tile(n), tile(n), tile(n), tile(n), pl.BlockSpec((1, 128), lambda i: (0, 0)),
                   pl.BlockSpec((1, n), lambda i: (0, 0)), pl.BlockSpec((1, n), lambda i: (0, 0))],
        out_shape=[jax.ShapeDtypeStruct((S, n), BF16), jax.ShapeDtypeStruct((S, n), F32), jax.ShapeDtypeStruct((S, n), BF16),
                   jax.ShapeDtypeStruct((S, n), BF16), jax.ShapeDtypeStruct((1, 128), F32), jax.ShapeDtypeStruct((1, n), F32),
                   jax.ShapeDtypeStruct((1, n), F32)],
        scratch_shapes=[pltpu.VMEM((tm, n), F32)] * 3,
        compiler_params=_params(("arbitrary",)),
    )(h2, p, target, w_pg, w_pp, g_ple, g_final)


def _rope_tables(S):
    half = 32
    inv = (1.0 / (np.float32(ROPE_THETA) ** (np.arange(half, dtype=np.float32) * np.float32(2.0 / 64)))).astype(np.float32)
    ang = np.arange(S).astype(np.float32)[:, None] * inv[None, :]
    cos, sin = np.cos(ang), np.sin(ang)
    return jnp.asarray(np.tile(cos, (1, 4))), jnp.asarray(np.concatenate([-sin, sin, -sin, sin], axis=1))


def _attn_common(i, kc, kp, vc, vp, cc, sc, cp, sp):
    lane = lax.broadcasted_iota(jnp.int32, (1, HEAD_PAIR), 1)
    lane_lo = jnp.bitwise_and(lane, 63) < 32
    slot = [lane < 64, lane >= 64]

    def swap_halves(t):
        return jnp.where(lane_lo, pltpu.roll(t, 96, 1), pltpu.roll(t, 32, 1))

    def rope(t, cos, sin):
        return t * cos + swap_halves(t) * sin

    def unrope(d, cos, sin):
        return d * cos + swap_halves(d * sin)

    k2 = jnp.concatenate([rope(kp, cp, sp), rope(kc, cc, sc)], axis=0)
    v2 = jnp.concatenate([vp, vc], axis=0)
    r = lax.broadcasted_iota(jnp.int32, (ATTN_BLOCK, 2 * ATTN_BLOCK), 0)
    c = lax.broadcasted_iota(jnp.int32, (ATTN_BLOCK, 2 * ATTN_BLOCK), 1)
    valid = (c > r) & (c <= r + ATTN_BLOCK) & jnp.logical_or(c >= ATTN_BLOCK, i > 0)
    ks, vs = {}, {}
    for j in range(2):
        kn = jnp.where(slot[j], k2, 0.0)
        vn = jnp.where(slot[j], v2, 0.0)
        for s in range(2):
            ks[j, s] = _bf(kn if s == j else pltpu.roll(kn, 64, 1))
            vs[j, s] = _bf(vn if s == j else pltpu.roll(vn, 64, 1))
    return slot, rope, unrope, valid, ks, vs


def _attn_probs(scores, valid, sink):
    s = jnp.where(valid, scores * 0.125, NEG)
    m = jnp.maximum(jnp.max(s, axis=1, keepdims=True), sink)
    e = jnp.exp(s - m)
    z = jnp.sum(e, axis=1, keepdims=True) + jnp.exp(sink - m)
    return e * (1.0 / z), m + jnp.log(z)


def _attn_specs(S):
    nb = S // ATTN_BLOCK
    prev = lambda i: jnp.maximum(i - 1, 0)
    blk = lambda w, col, row=(lambda i: i): pl.BlockSpec((ATTN_BLOCK, w), lambda i: (row(i), col))
    in_specs = [pl.BlockSpec(memory_space=pltpu.SMEM),
                blk(512, BLK_Q), blk(128, BLK_K), blk(128, BLK_K, prev), blk(128, BLK_V), blk(128, BLK_V, prev),
                blk(128, 0), blk(128, 0), blk(128, 0, prev), blk(128, 0, prev)]
    return nb, in_specs


def _attn_fwd(pa, cos, sin, sinks):
    S = pa.shape[0]
    nb, in_specs = _attn_specs(S)

    def body(sinks_ref, q_ref, kc_ref, kp_ref, vc_ref, vp_ref, cc_ref, sc_ref, cp_ref, sp_ref, o_ref, lse_ref):
        i = pl.program_id(0)
        lane = lax.broadcasted_iota(jnp.int32, (1, HEAD_PAIR), 1)
        cc, sc = cc_ref[...], sc_ref[...]
        _, rope, _, valid, ks, vs = _attn_common(i, kc_ref[...], kp_ref[...], vc_ref[...], vp_ref[...],
                                                 cc, sc, cp_ref[...], sp_ref[...])
        pair_cols = [slice(HEAD_PAIR * pair, HEAD_PAIR * (pair + 1)) for pair in range(4)]
        qps = [_bf(rope(q_ref[:, cols], cc, sc)) for cols in pair_cols]
        outs, lses = {}, {}

        def head_program(h):
            pair, s = divmod(h, 2)
            j = h // 4
            scores = _dot_nt(qps[pair], ks[j, s])
            yield
            p, lse = _attn_probs(scores, valid, sinks_ref[h])
            outs[h] = _dot(_bf(p), vs[j, s])
            lses[h] = jnp.where(lane == h, lse, 0.0)

        _interleave(head_program(h) for h in range(8))
        for pair, cols in enumerate(pair_cols):
            o_ref[:, cols] = outs[2 * pair] + outs[2 * pair + 1]
        lse_ref[...] = sum((lses[h] for h in range(1, 8)), lses[0])

    return pl.pallas_call(
        body, grid=(nb,), name="attn_fwd", in_specs=in_specs,
        out_specs=[pl.BlockSpec((ATTN_BLOCK, 512), lambda i: (i, 0)), pl.BlockSpec((ATTN_BLOCK, 128), lambda i: (i, 0))],
        out_shape=[jax.ShapeDtypeStruct((S, 512), F32), jax.ShapeDtypeStruct((S, 128), F32)],
        compiler_params=_params(("parallel",)),
    )(sinks, pa, pa, pa, pa, pa, cos, sin, cos, sin)


def _attn_bwd(pa, cos, sin, sinks, dcat, attn, lse, after):
    S = pa.shape[0]
    nb, in_specs = _attn_specs(S)
    in_specs = in_specs + [pl.BlockSpec((ATTN_BLOCK, 512), lambda i: (i, 0))] * 2 + [pl.BlockSpec((ATTN_BLOCK, 128), lambda i: (i, 0))]
    in_specs = in_specs + [pl.BlockSpec(memory_space=pl.ANY)]

    def body(sinks_ref, q_ref, kc_ref, kp_ref, vc_ref, vp_ref, cc_ref, sc_ref, cp_ref, sp_ref, do_ref, o_ref, lse_ref, _,
             dq_ref, dk_ref, dv_ref, dsink_ref):
        i = pl.program_id(0)

        @pl.when(i == 0)
        def _():
            dk_ref[...] = jnp.zeros_like(dk_ref)
            dv_ref[...] = jnp.zeros_like(dv_ref)
            dsink_ref[...] = jnp.zeros_like(dsink_ref)

        cc, sc, cp, sp = cc_ref[...], sc_ref[...], cp_ref[...], sp_ref[...]
        slot, rope, unrope, valid, ks, vs = _attn_common(i, kc_ref[...], kp_ref[...], vc_ref[...], vp_ref[...], cc, sc, cp, sp)
        pair_cols = [slice(HEAD_PAIR * pair, HEAD_PAIR * (pair + 1)) for pair in range(4)]
        qps = [_bf(rope(q_ref[:, cols], cc, sc)) for cols in pair_cols]
        dobs = [_bf(do_ref[:, cols]) for cols in pair_cols]
        do_o = [do_ref[:, cols] * o_ref[:, cols] for cols in pair_cols]
        dqs, dks, dvs = {}, {}, {}

        def head_program(h):
            pair, s = divmod(h, 2)
            j = h // 4
            qp, dob = qps[pair], dobs[pair]
            scores = _dot_nt(qp, ks[j, s])
            dp = _dot_nt(dob, vs[j, s])
            yield
            lse_h = lse_ref[:, h:h + 1]
            p = jnp.exp(jnp.where(valid, scores * 0.125, NEG) - lse_h)
            yield
            dr = jnp.sum(jnp.where(slot[s], do_o[pair], 0.0), axis=1, keepdims=True)
            ds = _bf(p * (dp - dr) * 0.125)
            yield
            dsink_ref[h:h + 1, :] += -jnp.sum(jnp.exp(sinks_ref[h] - lse_h) * dr, axis=0, keepdims=True)
            dqs[h] = _dot(ds, ks[j, s])
            dk_h = _dot_tn(ds, qp)
            dv_h = _dot_tn(_bf(p), dob)
            yield
            dk_h, dv_h = jnp.where(slot[s], dk_h, 0.0), jnp.where(slot[s], dv_h, 0.0)
            if s != j:
                dk_h, dv_h = pltpu.roll(dk_h, 64, 1), pltpu.roll(dv_h, 64, 1)
            dks[h], dvs[h] = dk_h, dv_h

        _interleave(head_program(h) for h in range(8))
        dk2 = sum((dks[h] for h in range(1, 8)), dks[0])
        dv2 = sum((dvs[h] for h in range(1, 8)), dvs[0])
        for pair, cols in enumerate(pair_cols):
            dq_ref[:, cols] = _bf(unrope(dqs[2 * pair] + dqs[2 * pair + 1], cc, sc))
        cur = pl.ds(pl.multiple_of(i * ATTN_BLOCK, ATTN_BLOCK), ATTN_BLOCK)
        dk_ref[cur, :] += unrope(dk2[ATTN_BLOCK:], cc, sc)
        dv_ref[cur, :] += dv2[ATTN_BLOCK:]

        @pl.when(i > 0)
        def _():
            prv = pl.ds(pl.multiple_of((i - 1) * ATTN_BLOCK, ATTN_BLOCK), ATTN_BLOCK)
            dk_ref[prv, :] += unrope(dk2[:ATTN_BLOCK], cp, sp)
            dv_ref[prv, :] += dv2[:ATTN_BLOCK]

    whole = lambda w: pl.BlockSpec((S, w), lambda i: (0, 0))
    return pl.pallas_call(
        body, grid=(nb,), name="attn_bwd", in_specs=in_specs,
        out_specs=[pl.BlockSpec((ATTN_BLOCK, 512), lambda i: (i, BLK_Q)), whole(128), whole(128),
                   pl.BlockSpec((8, 128), lambda i: (0, 0))],
        out_shape=[jax.ShapeDtypeStruct((S, D_IN_PAD), BF16), jax.ShapeDtypeStruct((S, 128), F32),
                   jax.ShapeDtypeStruct((S, 128), F32), jax.ShapeDtypeStruct((8, 128), F32)],
        compiler_params=_params(("arbitrary",)),
    )(sinks, pa, pa, pa, pa, pa, cos, sin, cos, sin, dcat, attn, lse, after)


CONV_ROWS = 512
CONV_PAD = 8


def _conv_silu(scr, w, r0):
    y = w[3:4, :] * scr[pl.ds(CONV_PAD + r0, CONV_ROWS), :]
    for j in range(DN_CONV - 1):
        y = y + w[j:j + 1, :] * scr[pl.ds(CONV_PAD + r0 - 3 + j, CONV_ROWS), :]
    return y


def _dn_prep_fwd(pd, conv_w):
    S = pd.shape[0]
    assert S % CONV_ROWS == 0

    def body(x_ref, w_ref, o_ref, scr):
        b = pl.program_id(0)
        scr[0:CONV_PAD, :] = jnp.zeros((CONV_PAD, DN_DIM), F32)
        scr[pl.ds(CONV_PAD, S), :] = x_ref[...]
        w = w_ref[...]
        q_scale = jnp.where(b < DN_HEADS, DN_DIM ** -0.5, 1.0)
        for r0 in range(0, S, CONV_ROWS):
            y = _conv_silu(scr, w, r0)
            a = y * _sigmoid(y)
            rs = lax.rsqrt(jnp.sum(a * a, axis=1, keepdims=True) + EPS)
            o_ref[pl.ds(r0, CONV_ROWS), :] = a * jnp.where(b < 2 * DN_HEADS, rs * q_scale, 1.0)

    col = pl.BlockSpec((S, DN_DIM), lambda b: (0, b))
    return pl.pallas_call(
        body, grid=(3 * DN_HEADS,), name="dn_prep_fwd",
        in_specs=[pl.BlockSpec((S, DN_DIM), lambda b: (0, BLK_DN + b)), pl.BlockSpec((DN_CONV, DN_DIM), lambda b: (0, b))],
        out_specs=col,
        out_shape=jax.ShapeDtypeStruct((S, 3 * DN_HEADS * DN_DIM), F32),
        scratch_shapes=[pltpu.VMEM((S + CONV_PAD, DN_DIM), F32)],
        compiler_params=_params(("parallel",)),
    )(pd, conv_w)


def _dn_prep_bwd(pd, conv_w, dqkv, dproj, dk, dv):
    S = pd.shape[0]
    NB = 3 * DN_HEADS

    def body(x_ref, w_ref, d_ref, _, dk_ref, dv_ref, dx_ref, dw_ref, scr, dscr):
        b = pl.program_id(0)

        @pl.when(b == NB)
        def _():
            dx_ref[...] = _bf(dk_ref[...])

        @pl.when(b == NB + 1)
        def _():
            dx_ref[...] = _bf(dv_ref[...])

        @pl.when(b < NB)
        def _():
            scr[0:CONV_PAD, :] = jnp.zeros((CONV_PAD, DN_DIM), F32)
            scr[pl.ds(CONV_PAD, S), :] = x_ref[...]
            dscr[pl.ds(S, CONV_PAD), :] = jnp.zeros((CONV_PAD, DN_DIM), F32)
            w = w_ref[...]
            q_scale = jnp.where(b < DN_HEADS, DN_DIM ** -0.5, 1.0)
            is_qk = b < 2 * DN_HEADS
            dw = [jnp.zeros((1, DN_DIM), F32) for _ in range(DN_CONV)]
            for r0 in range(0, S, CONV_ROWS):
                y = _conv_silu(scr, w, r0)
                sg = _sigmoid(y)
                a = y * sg
                dout = d_ref[pl.ds(r0, CONV_ROWS), :]
                rs = lax.rsqrt(jnp.sum(a * a, axis=1, keepdims=True) + EPS)
                da_qk = q_scale * rs * (dout - a * (rs * rs) * jnp.sum(dout * a, axis=1, keepdims=True))
                dy = jnp.where(is_qk, da_qk, dout) * (sg * (1.0 + y * (1.0 - sg)))
                dscr[pl.ds(r0, CONV_ROWS), :] = dy
                for j in range(DN_CONV):
                    dw[j] = dw[j] + jnp.sum(dy * scr[pl.ds(CONV_PAD + r0 - 3 + j, CONV_ROWS), :], axis=0, keepdims=True)
            for j in range(DN_CONV):
                dw_ref[j:j + 1, :] = dw[j]
            for r0 in range(0, S, CONV_ROWS):
                dx = w[3:4, :] * dscr[pl.ds(r0, CONV_ROWS), :]
                for j in range(DN_CONV - 1):
                    dx = dx + w[j:j + 1, :] * dscr[pl.ds(r0 + 3 - j, CONV_ROWS), :]
                dx_ref[pl.ds(r0, CONV_ROWS), :] = _bf(dx)

    own = lambda b: jnp.minimum(b, NB - 1)
    col = pl.BlockSpec((S, DN_DIM), lambda b: (0, own(b)))
    proj_col = pl.BlockSpec((S, DN_DIM), lambda b: (0, BLK_DN + own(b)))
    wcol = pl.BlockSpec((DN_CONV, DN_DIM), lambda b: (0, own(b)))
    whole = pl.BlockSpec((S, DN_DIM), lambda b: (0, 0))
    assert BLK_K == BLK_DN + NB and BLK_V == BLK_K + 1
    return pl.pallas_call(
        body, grid=(NB + 2,), name="dn_prep_bwd",
        in_specs=[proj_col, wcol, col, pl.BlockSpec(memory_space=pl.ANY), whole, whole],
        out_specs=[pl.BlockSpec((S, DN_DIM), lambda b: (0, BLK_DN + b)), wcol],
        out_shape=[jax.ShapeDtypeStruct(dproj.shape, dproj.dtype), jax.ShapeDtypeStruct((DN_CONV, 3 * DN_HEADS * DN_DIM), F32)],
        scratch_shapes=[pltpu.VMEM((S + CONV_PAD, DN_DIM), F32), pltpu.VMEM((S + CONV_PAD, DN_DIM), F32)],
        input_output_aliases={3: 0},
        compiler_params=_params(("arbitrary",)),
    )(pd, conv_w, dqkv, dproj, dk, dv)


CPAD = 128
CHUNKS_LOCAL = 8
CHUNKS_SCAN = 8


def _chunk_masks():
    ii = lax.broadcasted_iota(jnp.int32, (DN_CHUNK, CPAD), 0)
    jj = lax.broadcasted_iota(jnp.int32, (DN_CHUNK, CPAD), 1)
    return ii, jj


def _rows_pad(a):
    return jnp.concatenate([a, jnp.zeros_like(a)], axis=0)


def _hi_lo(a):
    hi = _bf(a)
    return hi, _bf(a - hi.astype(F32))


def _double_step(t, p):
    C = DN_CHUNK
    th, tl = _hi_lo(t)
    ph, pl_ = _hi_lo(p)
    r1 = _dot(jnp.concatenate([th, tl, ph, pl_], axis=0), _rows_pad(ph))
    r2 = _dot(jnp.concatenate([th, ph], axis=0), _rows_pad(pl_))
    return t + (r1[:C] + r1[C:2 * C] + r2[:C]), r1[2 * C:3 * C] + r1[3 * C:] + r2[C:]


def _dot3_nt(a, b):
    C = DN_CHUNK
    ah, al = _hi_lo(a)
    bh, bl = _hi_lo(b)
    r1 = _dot_nt(jnp.concatenate([ah, al], axis=0), _rows_pad(bh))
    return r1[:C] + r1[C:] + _dot_nt(ah, _rows_pad(bl))


def _dot3_tn(a, b):
    C = DN_CHUNK
    ah, al = _hi_lo(a)
    bh, bl = _hi_lo(b)
    return _dot_tn(jnp.concatenate([ah, al, ah], axis=0), jnp.concatenate([bh, bh, bl], axis=0))[:C]


def _interleave(programs):
    programs = list(programs)
    while programs:
        alive = []
        for prog in programs:
            try:
                next(prog)
                alive.append(prog)
            except StopIteration:
                pass
        programs = alive


def _col_to_row(col, ii, jj):
    return jnp.sum(jnp.where(ii == jj, col, 0.0), axis=0, keepdims=True)


def _row_to_col(row, ii, jj):
    return jnp.sum(jnp.where(ii == jj, row, 0.0), axis=1, keepdims=True)


def _decay(gc_col, ii, jj):
    diff = gc_col - _col_to_row(gc_col, ii, jj)
    return jnp.where(jj <= ii, jnp.exp(jnp.where(jj <= ii, diff, 0.0)), 0.0)


def _softplus(x):
    return jnp.maximum(x, 0.0) + jnp.log(1.0 + jnp.exp(-jnp.abs(x)))


def _head(h):
    return slice(DN_DIM * h, DN_DIM * (h + 1))


def _dn_chunk_fwd(qkv, pg, a_log, dt_bias):
    S = qkv.shape[0]
    C = DN_CHUNK
    G = CHUNKS_LOCAL
    R = G * C
    steps = S // R

    def body(alog_ref, dtb_ref, qkv_ref, pg_ref, w_ref, u_ref, qg_ref, kd_ref, a_ref, t_ref, gcs_ref):
        ii, jj = _chunk_masks()
        lane = lax.broadcasted_iota(jnp.int32, (1, 128), 1)
        eye = (ii == jj).astype(F32)
        gcs_parts = [[] for _ in range(G)]

        def head_program(chunk, h):
            rows = slice(chunk * C, (chunk + 1) * C)
            q, k, v = qkv_ref[rows, _head(h)], qkv_ref[rows, _head(DN_HEADS + h)], qkv_ref[rows, _head(2 * DN_HEADS + h)]
            beta = _sigmoid(pg_ref[rows, h:h + 1])
            g_col = -jnp.exp(alog_ref[h]) * _softplus(pg_ref[rows, DN_HEADS + h:DN_HEADS + h + 1] + dtb_ref[h])
            g_row = _col_to_row(g_col, ii, jj)
            gc_col = jnp.sum(jnp.where(jj <= ii, g_row, 0.0), axis=1, keepdims=True)
            dec = _decay(gc_col, ii, jj)
            eg = jnp.exp(gc_col)
            kb, vb = k * beta, v * beta
            k_rows = _rows_pad(_bf(k))
            kk = _dot_nt(_bf(kb), k_rows)
            qk = _dot_nt(_bf(q), k_rows)
            yield
            t, pw = eye, -jnp.where(jj < ii, kk * dec, 0.0)
            for _ in range(6):
                t, pw = _double_step(t, pw)
                yield
            tb = _bf(t)
            u_ref[rows, _head(h)] = _dot(tb, _rows_pad(_bf(vb)))
            w_ref[rows, _head(h)] = _bf(_dot(tb, _rows_pad(_bf(kb * eg))))
            a_ref[h, rows] = _bf(qk * dec)
            t_ref[h, rows] = t
            qg_ref[rows, _head(h)] = _bf(q * eg)
            kd_ref[rows, _head(h)] = _bf(k * jnp.exp(gc_col[C - 1:C, :] - gc_col))
            gcs_parts[chunk].append(jnp.where(lane == h, gc_col, 0.0) + jnp.where(lane == DN_HEADS + h, beta, 0.0)
                                    + jnp.where(lane == 2 * DN_HEADS + h, g_col, 0.0))

        _interleave(head_program(chunk, h) for chunk in range(G) for h in range(DN_HEADS))
        for chunk in range(G):
            gcs_ref[chunk * C:(chunk + 1) * C, :] = sum(gcs_parts[chunk][1:], gcs_parts[chunk][0])

    smem = pl.BlockSpec(memory_space=pltpu.SMEM)
    wide = pl.BlockSpec((R, 512), lambda n: (n, 0))
    sq = pl.BlockSpec((DN_HEADS, R, CPAD), lambda n: (0, n, 0))
    narrow = pl.BlockSpec((R, 128), lambda n: (n, 0))
    f = lambda *shp: jax.ShapeDtypeStruct(shp, F32)
    b = lambda *shp: jax.ShapeDtypeStruct(shp, BF16)
    return pl.pallas_call(
        body, grid=(steps,), name="dn_chunk_fwd",
        in_specs=[smem, smem, pl.BlockSpec((R, 1536), lambda n: (n, 0)), pl.BlockSpec((R, 128), lambda n: (n, BLK_G))],
        out_specs=[wide, wide, wide, wide, sq, sq, narrow],
        out_shape=[b(S, 512), f(S, 512), b(S, 512), b(S, 512), b(DN_HEADS, S, CPAD), f(DN_HEADS, S, CPAD), f(S, 128)],
        compiler_params=_params(("parallel",)),
    )(a_log, dt_bias, qkv, pg)


def _gated_norm(o, z, gn):
    r, oh = _rms_stats(o)
    return oh * gn * (z * _sigmoid(z))


def _dn_scan_fwd(w, u, qg, kd, a, gcs, pz, gn):
    S = w.shape[0]
    C = DN_CHUNK
    nc = S // C
    G = CHUNKS_SCAN
    R = G * C

    def body(w_ref, u_ref, qg_ref, kd_ref, a_ref, gcs_ref, z_ref, gn_ref, o_ref, vn_ref, sst_ref, out_ref, state):
        @pl.when(pl.program_id(0) == 0)
        def _():
            state[...] = jnp.zeros_like(state)

        def head_program(chunk, h):
            hs = _head(h)
            rows = slice(chunk * C, (chunk + 1) * C)
            s_in = state[h]
            sb = _bf(s_in)
            sst_ref[chunk, h] = sb
            w_s = _dot(w_ref[rows, hs], sb)
            q_s = _dot(qg_ref[rows, hs], sb)
            yield
            vn = u_ref[rows, hs] - w_s
            vnb = _bf(vn)
            o = q_s + _dot(a_ref[h, rows], _rows_pad(vnb))
            k_v = _dot_tn(kd_ref[rows, hs], vnb)
            yield
            state[h] = s_in * jnp.exp(gcs_ref[(chunk + 1) * C - 1:(chunk + 1) * C, h:h + 1]) + k_v
            o_ref[rows, hs] = o
            vn_ref[rows, hs] = vnb
            out_ref[rows, hs] = _bf(_gated_norm(o, z_ref[rows, hs], gn_ref[...]))

        for chunk in range(G):
            _interleave(head_program(chunk, h) for h in range(DN_HEADS))

    wide = pl.BlockSpec((R, 512), lambda n: (n, 0))
    f = lambda *shp: jax.ShapeDtypeStruct(shp, F32)
    b = lambda *shp: jax.ShapeDtypeStruct(shp, BF16)
    return pl.pallas_call(
        body, grid=(nc // G,), name="dn_scan_fwd",
        in_specs=[wide, wide, wide, wide, pl.BlockSpec((DN_HEADS, R, CPAD), lambda n: (0, n, 0)),
                  pl.BlockSpec((R, 128), lambda n: (n, 0)), pl.BlockSpec((R, 512), lambda n: (n, BLK_Z)),
                  pl.BlockSpec((1, DN_DIM), lambda n: (0, 0))],
        out_specs=[wide, wide, pl.BlockSpec((G, DN_HEADS, DN_DIM, DN_DIM), lambda n: (n, 0, 0, 0)), wide],
        out_shape=[f(S, 512), b(S, 512), b(nc, DN_HEADS, DN_DIM, DN_DIM), b(S, 512)],
        scratch_shapes=[pltpu.VMEM((DN_HEADS, DN_DIM, DN_DIM), F32)],
        compiler_params=_params(("arbitrary",)),
    )(w, u, qg, kd, a, gcs, pz, gn)


def _dn_scan_bwd(dcat, o, pz, gn, sst, vnew, w, qg, kd, a, gcs, dproj):
    S = o.shape[0]
    C = DN_CHUNK
    G = CHUNKS_SCAN
    R = G * C
    steps = S // R

    def body(dy_ref, o_ref, z_ref, gn_ref, sst_ref, vn_ref, w_ref, qg_ref, kd_ref, a_ref, gcs_ref, _,
             du_ref, dw_ref, dqg_ref, dkd_ref, da_ref, dz_ref, dsc_ref, dgn_ref, dstate):
        @pl.when(pl.program_id(0) == 0)
        def _():
            dstate[...] = jnp.zeros_like(dstate)
            dgn_ref[...] = jnp.zeros_like(dgn_ref)

        gn_ = gn_ref[...]
        lane = lax.broadcasted_iota(jnp.int32, (C, 128), 1)
        row = lax.broadcasted_iota(jnp.int32, (C, 128), 0)
        dgn_parts = []

        def head_program(chunk, h, dsc_parts):
            hs = _head(h)
            rows = slice(chunk * C, (chunk + 1) * C)
            ov, z, dout = o_ref[rows, hs], z_ref[rows, hs], dy_ref[rows, hs]
            r, oh = _rms_stats(ov)
            sg = _sigmoid(z)
            don = dout * (z * sg)
            dz_ref[rows, hs] = _bf(dout * (oh * gn_) * (sg * (1.0 + z * (1.0 - sg))))
            dgn_parts.append(jnp.sum(don * oh, axis=0, keepdims=True))
            dn = don * gn_
            do = _bf(r * (dn - oh * jnp.mean(dn * oh, axis=-1, keepdims=True)))
            sb = sst_ref[chunk, h]
            s_in = sb.astype(F32)
            ds_out = dstate[h]
            dsb = _bf(ds_out)
            vnb = vn_ref[rows, hs]
            wb, qgb, kdb, ab = w_ref[rows, hs], qg_ref[rows, hs], kd_ref[rows, hs], a_ref[h, rows]
            dvn = _dot_tn(ab, do)[:C] + _dot(kdb, dsb)
            yield
            da_ref[h, rows] = _dot_nt(do, _rows_pad(vnb))
            dqg_ref[rows, hs] = _dot_nt(do, sb)
            dkd_ref[rows, hs] = _dot_nt(vnb, dsb)
            q_do = _dot_tn(qgb, do)
            yield
            dvnb = _bf(dvn)
            dw_ref[rows, hs] = _bf(-_dot_nt(dvnb, sb))
            w_dvn = _dot_tn(wb, dvnb)
            du_ref[rows, hs] = dvnb
            yield
            d_last = jnp.exp(gcs_ref[(chunk + 1) * C - 1:(chunk + 1) * C, h:h + 1])
            dd = jnp.sum(jnp.sum(ds_out * s_in, axis=1, keepdims=True), axis=0, keepdims=True)
            dsc_parts.append(jnp.where((lane == h) & (row == C - 1), dd * d_last, 0.0))
            dstate[h] = ds_out * d_last + q_do - w_dvn

        for chunk in reversed(range(G)):
            dsc_parts = []
            _interleave(head_program(chunk, h, dsc_parts) for h in range(DN_HEADS))
            dsc_ref[chunk * C:(chunk + 1) * C, :] = sum(dsc_parts[1:], dsc_parts[0])
        dgn_ref[...] += sum(dgn_parts[1:], dgn_parts[0])

    rev = lambda n: steps - 1 - n
    wide = pl.BlockSpec((R, 512), lambda n: (rev(n), 0))
    z_spec = pl.BlockSpec((R, 512), lambda n: (rev(n), BLK_Z))
    sq = pl.BlockSpec((DN_HEADS, R, CPAD), lambda n: (0, rev(n), 0))
    narrow = pl.BlockSpec((R, 128), lambda n: (rev(n), 0))
    gn_spec = pl.BlockSpec((1, DN_DIM), lambda n: (0, 0))
    f = lambda *shp: jax.ShapeDtypeStruct(shp, F32)
    b = lambda *shp: jax.ShapeDtypeStruct(shp, BF16)
    return pl.pallas_call(
        body, grid=(steps,), name="dn_scan_bwd",
        in_specs=[pl.BlockSpec((R, 512), lambda n: (rev(n), 1)), wide, z_spec, gn_spec,
                  pl.BlockSpec((G, DN_HEADS, DN_DIM, DN_DIM), lambda n: (rev(n), 0, 0, 0)),
                  wide, wide, wide, wide, sq, narrow, pl.BlockSpec(memory_space=pl.ANY)],
        out_specs=[wide, wide, wide, wide, sq, z_spec, narrow, gn_spec],
        out_shape=[b(S, 512), b(S, 512), f(S, 512), f(S, 512), f(DN_HEADS, S, CPAD),
                   jax.ShapeDtypeStruct(dproj.shape, dproj.dtype), f(S, 128), f(1, DN_DIM)],
        scratch_shapes=[pltpu.VMEM((DN_HEADS, DN_DIM, DN_DIM), F32)],
        input_output_aliases={11: 5},
        compiler_params=_params(("arbitrary",)),
    )(dcat, o, pz, gn, sst, vnew, w, qg, kd, a, gcs, dproj)


def _dn_chunk_bwd(qkv, pg, t_inv, gcs, du, dw, dqg, dkd, da, dsc, a_log, dt_bias, dproj):
    S = qkv.shape[0]
    C = DN_CHUNK
    G = CHUNKS_LOCAL
    R = G * C

    def body(alog_ref, dtb_ref, qkv_ref, pg_ref, t_ref, gcs_ref, du_ref, dw_ref, dqg_ref, dkd_ref, da_ref, dsc_ref, _,
             dqkv_ref, dpg_ref, acc_ref):
        @pl.when(pl.program_id(0) == 0)
        def _():
            acc_ref[...] = jnp.zeros_like(acc_ref)

        ii, jj = _chunk_masks()
        lane = lax.broadcasted_iota(jnp.int32, (1, 128), 1)
        row8 = lax.broadcasted_iota(jnp.int32, (8, 128), 0)
        lane8 = lax.broadcasted_iota(jnp.int32, (8, 128), 1)
        rowc = lax.broadcasted_iota(jnp.int32, (C, 1), 0)
        tril, strict = jj <= ii, jj < ii
        dpg_parts, acc_parts = [[] for _ in range(G)], []

        def head_program(chunk, h):
            rows = slice(chunk * C, (chunk + 1) * C)
            q, k, v = qkv_ref[rows, _head(h)], qkv_ref[rows, _head(DN_HEADS + h)], qkv_ref[rows, _head(2 * DN_HEADS + h)]
            gc_col, beta, g_col = gcs_ref[rows, h:h + 1], gcs_ref[rows, DN_HEADS + h:DN_HEADS + h + 1], \
                gcs_ref[rows, 2 * DN_HEADS + h:2 * DN_HEADS + h + 1]
            dec = _decay(gc_col, ii, jj)
            eg = jnp.exp(gc_col)
            g_last = gc_col[C - 1:C, :]
            ek = jnp.exp(g_last - gc_col)
            kb, vb = k * beta, v * beta
            kbg = kb * eg
            qb, kbb = _bf(q), _bf(kb)
            k_rows = _rows_pad(_bf(k))
            t = t_ref[h, rows]
            tb = _bf(t)
            dub, dwb = du_ref[rows, _head(h)], dw_ref[rows, _head(h)]
            dqg_, dkd_ = dqg_ref[rows, _head(h)], dkd_ref[rows, _head(h)]
            dt = _dot_nt(dub, _rows_pad(_bf(vb))) + _dot_nt(dwb, _rows_pad(_bf(kbg)))
            t_du_dw = _dot_tn(tb, jnp.concatenate([dub, dwb], axis=1))
            dvb, dkbg = t_du_dw[:C, :DN_DIM], t_du_dw[:C, DN_DIM:]
            kk = _dot_nt(kbb, k_rows)
            qk = _dot_nt(qb, k_rows)
            yield
            dt_t = _dot3_nt(dt, t)
            yield
            dl = -_dot3_tn(t, dt_t)
            yield
            dm = jnp.where(strict, dl * dec, 0.0)
            dqk = jnp.where(tril, da_ref[h, rows] * dec, 0.0)
            gmat = dm * kk + dqk * qk
            dgc = jnp.sum(gmat, axis=1, keepdims=True) - _row_to_col(jnp.sum(gmat, axis=0, keepdims=True), ii, jj)
            dmb, dqkb = _bf(dm), _bf(dqk)
            yield
            dkb = _dot(dmb, k_rows) + dkbg * eg
            dk = _dot_tn(jnp.concatenate([dmb, dqkb], axis=0), jnp.concatenate([kbb, qb], axis=0))[:C] + dkd_ * ek
            dq = _dot(dqkb, k_rows) + dqg_ * eg
            yield
            tk = jnp.sum(dkd_ * k * ek, axis=1, keepdims=True)
            dgc = dgc + jnp.sum(dqg_ * q * eg, axis=1, keepdims=True) - tk + jnp.sum(dkbg * kbg, axis=1, keepdims=True)
            dgl = jnp.sum(tk, axis=0, keepdims=True) + dsc_ref[(chunk + 1) * C - 1:(chunk + 1) * C, h:h + 1]
            dgc = dgc + jnp.where(rowc == C - 1, dgl, 0.0)
            yield
            dk = dk + dkb * beta
            dbeta = jnp.sum(dkb * k, axis=1, keepdims=True) + jnp.sum(dvb * v, axis=1, keepdims=True)
            dqkv_ref[rows, _head(h)] = dq
            dqkv_ref[rows, _head(DN_HEADS + h)] = dk
            dqkv_ref[rows, _head(2 * DN_HEADS + h)] = dvb * beta
            dg_col = jnp.sum(jnp.where(jj >= ii, _col_to_row(dgc, ii, jj), 0.0), axis=1, keepdims=True)
            yield
            db = dbeta * beta * (1.0 - beta)
            da_in = dg_col * (-jnp.exp(alog_ref[h])) * _sigmoid(pg_ref[rows, DN_HEADS + h:DN_HEADS + h + 1] + dtb_ref[h])
            dpg_parts[chunk].append(jnp.where(lane == h, db, 0.0) + jnp.where(lane == DN_HEADS + h, da_in, 0.0))
            acc_parts.append(jnp.where((row8 == 0) & (lane8 == h), jnp.sum(dg_col * g_col, axis=0, keepdims=True), 0.0)
                             + jnp.where((row8 == 1) & (lane8 == h), jnp.sum(da_in, axis=0, keepdims=True), 0.0))

        _interleave(head_program(chunk, h) for chunk in range(G) for h in range(DN_HEADS))
        for chunk in range(G):
            dpg = sum(dpg_parts[chunk][1:], dpg_parts[chunk][0])
            dpg_ref[chunk * C:(chunk + 1) * C, :] = _bf(jnp.concatenate([dpg, jnp.zeros_like(dpg)], axis=1))
        acc_ref[...] += sum(acc_parts[1:], acc_parts[0])

    smem = pl.BlockSpec(memory_space=pltpu.SMEM)
    wide = pl.BlockSpec((R, 512), lambda n: (n, 0))
    sq = pl.BlockSpec((DN_HEADS, R, CPAD), lambda n: (0, n, 0))
    narrow = pl.BlockSpec((R, 128), lambda n: (n, 0))
    qkv_spec = pl.BlockSpec((R, 1536), lambda n: (n, 0))
    f = lambda *shp: jax.ShapeDtypeStruct(shp, F32)
    return pl.pallas_call(
        body, grid=(S // R,), name="dn_chunk_bwd",
        in_specs=[smem, smem, qkv_spec, pl.BlockSpec((R, 128), lambda n: (n, BLK_G)), sq, narrow, wide, wide, wide, wide, sq,
                  narrow, pl.BlockSpec(memory_space=pl.ANY)],
        out_specs=[qkv_spec, pl.BlockSpec((R, 256), lambda n: (n, BLK_G_PAD)), pl.BlockSpec((8, 128), lambda n: (0, 0))],
        out_shape=[f(S, 1536), jax.ShapeDtypeStruct(dproj.shape, dproj.dtype), f(8, 128)],
        input_output_aliases={12: 1},
        compiler_params=_params(("arbitrary",)),
    )(a_log, dt_bias, qkv, pg, t_inv, gcs, du, dw, dqg, dkd, da, dsc, dproj)


_W_IN_SECTIONS = ((0, 0, 512), (2304, 512, 512), (768, 1024, 1536), (512, 2560, 256), (2816, 2816, 8))
_HALF = D_MODEL // 2
_SECTION_ROWS = 256


def _pack_pairs(x):
    bits = lax.bitcast_convert_type(x, jnp.uint32)
    return lax.bitcast_convert_type(bits[:, _HALF:] | (bits[:, :_HALF] >> 16), F32)


def _unpack_pairs(words):
    bits = lax.bitcast_convert_type(words, jnp.uint32)
    return (lax.bitcast_convert_type(bits << 16, F32),
            lax.bitcast_convert_type(bits & jnp.uint32(0xFFFF0000), F32))


def _w_in_to_internal(packed):
    starts = [dst for _, dst, _ in _W_IN_SECTIONS] + [D_IN_PAD]

    def body(x_hbm, o_ref, words, sems):
        copies = [pltpu.make_async_copy(x_hbm.at[pl.ds(src, rows), 0, :], words.at[pl.ds(dst, rows)], sems.at[i])
                  for i, (src, dst, rows) in enumerate(_W_IN_SECTIONS)]
        for cp in copies:
            cp.start()
        words[pl.ds(D_IN, D_IN_PAD - D_IN), :] = jnp.zeros((D_IN_PAD - D_IN, _HALF), F32)
        for i, cp in enumerate(copies):
            cp.wait()
            for r in range(starts[i], starts[i + 1], _SECTION_ROWS):
                for c, h in enumerate(_unpack_pairs(words[pl.ds(r, _SECTION_ROWS), :])):
                    o_ref[pl.ds(r, _SECTION_ROWS), c * _HALF:(c + 1) * _HALF] = _bf(h)

    assert all((b - a) % _SECTION_ROWS == 0 for a, b in zip(starts, starts[1:])) and starts[-2] + _W_IN_SECTIONS[-1][2] == D_IN
    return pl.pallas_call(body, name="w_in_to_internal", out_shape=jax.ShapeDtypeStruct((D_IN_PAD, D_MODEL), BF16),
                          in_specs=[pl.BlockSpec(memory_space=pl.ANY)],
                          scratch_shapes=[pltpu.VMEM((D_IN_PAD, _HALF), F32), pltpu.SemaphoreType.DMA((len(_W_IN_SECTIONS),))],
                          compiler_params=pltpu.CompilerParams(vmem_limit_bytes=VMEM_LIMIT))(packed)


def _w_in_from_internal(gt):
    def body(g_ref, o_hbm, words, sems):
        copies = []
        for i, (dst, src, rows) in enumerate(_W_IN_SECTIONS):
            for r in range(src, src + rows, _SECTION_ROWS):
                n = max(min(_SECTION_ROWS, src + rows - r), 16)
                words[pl.ds(r, n), :] = _pack_pairs(g_ref[pl.ds(r, n), :].astype(F32))
            copies.append(pltpu.make_async_copy(words.at[pl.ds(src, rows)], o_hbm.at[pl.ds(dst, rows), 0, :], sems.at[i]))
            copies[-1].start()
        for cp in copies:
            cp.wait()

    return pl.pallas_call(body, name="w_in_from_internal", out_shape=jax.ShapeDtypeStruct((D_IN, 1, _HALF), F32),
                          out_specs=pl.BlockSpec(memory_space=pl.ANY),
                          scratch_shapes=[pltpu.VMEM((D_IN_PAD, _HALF), F32), pltpu.SemaphoreType.DMA((len(_W_IN_SECTIONS),))],
                          compiler_params=pltpu.CompilerParams(vmem_limit_bytes=VMEM_LIMIT))(gt)


def _local_step(x, p, target, wts, first_weights, other_weights, ship_early, after):
    S = x.shape[0]
    cos, sin = _rope_tables(S)
    sinks, a_log, dt_bias = wts["sinks"].reshape(8), wts["a_log"].reshape(4), wts["dt_bias"].reshape(4)
    gn = wts["dn_norm"].reshape(1, DN_DIM)
    add = lambda acc, res: (acc + res,)

    u = _rmsnorm_fwd(x, wts["norm_mix"], "norm_mix_fwd", after)
    w_in_t, conv_w = first_weights(u)
    proj, = _mm(u, w_in_t, form="nt", name="in_proj", out_dtypes=[F32], tn=512)
    attn, lse = _attn_fwd(proj, cos, sin, sinks)
    qkv = _dn_prep_fwd(proj, conv_w)
    cw, cu, cqg, ckd, ca, ct, gcs = _dn_chunk_fwd(qkv, proj, a_log, dt_bias)
    o, vnew, sst, dn_out = _dn_scan_fwd(cw, cu, cqg, ckd, ca, gcs, proj, gn)
    w_o, = other_weights(("w_o",), dn_out)
    h1, = _mm([attn, dn_out], w_o, form="nn", name="out_proj", out_dtypes=[F32], tn=512, epi=add, extra=[x])

    w_up, w_down = other_weights(("w_up", "w_down"), h1)
    hid, relu, m, h2 = _mlp_fwd(h1, w_up, w_down, wts["norm_mlp"])
    w_pg, w_pp = other_weights(("w_ple_gate", "w_ple_proj"), h2)
    n3, dh2, dgl, dpp, loss, d_norm_final, d_norm_ple = _ple_and_loss(h2, p, target, w_pg, w_pp, wts["norm_ple"],
                                                                     wts["norm_final"].reshape(1, D_MODEL))
    g = {"norm_final": d_norm_final, "norm_ple": d_norm_ple}
    early = {"w_ple_gate": _mm_tn(n3, dgl, name="d_w_ple_gate", tm=512, tn=1024, out_dtype=BF16).reshape(N_DEV, 128, 1024),
             "w_ple_proj": _mm_tn(p, dpp, name="d_w_ple_proj", tm=256, tn=128, out_dtype=BF16, column_shards=True)}
    d_act, = _mm(dh2, w_down, form="nt", name="d_hidden", out_dtypes=[BF16], tn=512,
                 epi=lambda acc, r: (acc * (2.0 * r.astype(F32)),), extra=[relu])
    early["w_down"] = _mm_tn(hid, dh2, name="d_w_down", tm=512, tn=1024, out_dtype=BF16).reshape(N_DEV, 512, 1024)
    early["w_up"] = _mm_tn(m, d_act, name="d_w_up", tm=1024, tn=512, out_dtype=BF16, column_shards=True)
    token = ship_early(early)
    dh1, g["norm_mlp"], dcat = _mm(d_act, w_up, form="nt", name="d_m", out_dtypes=[F32], tn=512, after=token,
                                   norm_bwd=(h1, wts["norm_mlp"], dh2), then_nt=w_o)
    d_w_o = _mm_tn([attn, dn_out], dh1, name="d_w_o", tm=512, tn=512, out_dtype=BF16)
    token = ship_early({"w_o": d_w_o.reshape(N_DEV, 128, 1024)})
    dproj, dk, dv, dsinks = _attn_bwd(proj, cos, sin, sinks, dcat, attn, lse, token)
    g["sinks"] = dsinks[:, 0].reshape(1, 8)
    du_, dw_, dqg, dkd, da, dproj, dsc, g["dn_norm"] = _dn_scan_bwd(dcat, o, proj, gn, sst, vnew, cw, cqg, ckd, ca, gcs, dproj)
    dqkv, dproj, gate_acc = _dn_chunk_bwd(qkv, proj, ct, gcs, du_, dw_, dqg, dkd, da, dsc, a_log, dt_bias, dproj)
    g["a_log"], g["dt_bias"] = gate_acc[0:1, 0:4], gate_acc[1:2, 0:4]
    dproj, g["conv_w"] = _dn_prep_bwd(proj, conv_w, dqkv, dproj, dk, dv)
    token = ship_early({"w_in": _mm_tn(dproj, u, name="d_w_in", tm=512, tn=1024, out_dtype=BF16)})
    grad_x, g["norm_mix"] = _mm(dproj, w_in_t, form="nn", name="d_u", out_dtypes=[F32], tn=512, after=token,
                                norm_bwd=(x, wts["norm_mix"], dh1))
    return loss, grad_x, g


def _peer(k):
    x, y, c = lax.axis_index("x"), lax.axis_index("y"), lax.axis_index("c")
    px = 1 - x if k & 4 else x
    py = 1 - y if k & 2 else y
    pc = 1 - c if k & 1 else c
    return (px, py, pc), 4 * px + 2 * py + pc


def _exchange(srcs, name, gather):
    n = len(srcs)
    gathers = list(gather) if isinstance(gather, (list, tuple)) else [gather] * n
    shapes = [(N_DEV,) + s.shape if gt else s.shape for s, gt in zip(srcs, gathers)]

    def body(*refs):
        src_refs, out_refs = refs[:n], refs[n:2 * n]
        send_sems, recv_sems, local_sems = refs[2 * n:]
        _, me = _peer(0)
        piece = lambda a, d: src_refs[a] if gathers[a] else src_refs[a].at[d]
        local = [pltpu.make_async_copy(piece(a, me), out_refs[a].at[me], local_sems.at[a]) for a in range(n)]
        for cp in local:
            cp.start()
        copies = []
        for a in range(n):
            for k in range(1, N_DEV):
                dev, idx = _peer(k)
                cp = pltpu.make_async_remote_copy(src_ref=piece(a, idx), dst_ref=out_refs[a].at[me],
                                                  send_sem=send_sems.at[a, k - 1], recv_sem=recv_sems.at[a, k - 1],
                                                  device_id=dev, device_id_type=MESH)
                cp.start()
                copies.append(cp)
        for cp in copies:
            cp.wait_recv()
        for cp in copies:
            cp.wait_send()
        for cp in local:
            cp.wait()

    anywhere = pl.BlockSpec(memory_space=pl.ANY)
    return pl.pallas_call(
        body, name=name, in_specs=[anywhere] * n, out_specs=[anywhere] * n,
        out_shape=[jax.ShapeDtypeStruct(shp, s.dtype) for shp, s in zip(shapes, srcs)],
        scratch_shapes=[pltpu.SemaphoreType.DMA((n, N_DEV - 1)), pltpu.SemaphoreType.DMA((n, N_DEV - 1)),
                        pltpu.SemaphoreType.DMA((n,))],
    )(*srcs)


_HBM = pl.BlockSpec(memory_space=pltpu.HBM)
_SEM = pl.BlockSpec(memory_space=pltpu.SEMAPHORE)
_EFFECT = pltpu.SideEffectType.DATAFLOW_SIDE_EFFECTING


def _split_copies(src_refs, land_refs, send_sems, recv_sems, modes, which=None):
    _, me = _peer(0)
    local, remote = [], []
    which = range(len(src_refs)) if which is None else which
    for a, src, land in zip(which, src_refs, land_refs):
        if modes[a] == "columns":
            n_cols = src.shape[1]
            dst = land.at[:, pl.ds(pl.multiple_of(me * n_cols, n_cols), n_cols)]
        else:
            dst = land.at[me]
        part = lambda d: src.at[d] if modes[a] == "pieces" else src
        local.append(pltpu.make_async_copy(part(me), dst, recv_sems.at[a * N_DEV]))
        for k in ((2, 4, 6) if modes[a] == "chips" else range(1, N_DEV)):
            dev, idx = _peer(k)
            sem = a * N_DEV + k
            remote.append(pltpu.make_async_remote_copy(
                src_ref=part(idx), dst_ref=dst, send_sem=send_sems.at[sem], recv_sem=recv_sems.at[sem],
                device_id=dev, device_id_type=MESH))
    return local, remote


def _forward_copies(land_refs, send_sems, recv_sems):
    c = lax.axis_index("c")
    sibling, _ = _peer(1)
    copies = []
    for a, land in enumerate(land_refs):
        for chip in range(N_DEV // 2):
            slot = 2 * chip + c
            sem = a * (N_DEV // 2) + chip
            copies.append(pltpu.make_async_remote_copy(
                src_ref=land.at[slot], dst_ref=land.at[slot], send_sem=send_sems.at[sem], recv_sem=recv_sems.at[sem],
                device_id=sibling, device_id_type=MESH))
    return copies


def _forward_start(lands, name):
    n = len(lands)

    def body(*refs):
        for cp in _forward_copies(refs[:n], refs[n], refs[n + 1]):
            cp.start()
        refs[-1][...] = jnp.zeros_like(refs[-1])

    sems = pltpu.SemaphoreType.DMA((n * (N_DEV // 2),))
    out = pl.pallas_call(
        body, name=name,
        out_shape=(sems, sems, *[pltpu.HBM(t.shape, t.dtype) for t in lands], jax.ShapeDtypeStruct((8, 128), F32)),
        in_specs=[_HBM] * n, out_specs=(_SEM, _SEM, *[_HBM] * n, pl.BlockSpec(memory_space=pltpu.VMEM)),
        input_output_aliases={i: 2 + i for i in range(n)},
        compiler_params=pltpu.CompilerParams(has_side_effects=_EFFECT),
    )(*[pltpu.with_memory_space_constraint(t, pltpu.HBM) for t in lands])
    return out[:-1], out[-1]


def _forward_wait(handle, after, name):
    send_sems, recv_sems, *lands = handle
    n = len(lands)

    def body(*refs):
        for cp in _forward_copies(refs[:n], refs[n], refs[n + 1]):
            cp.wait_send()
            cp.wait_recv()

    return list(pl.pallas_call(
        body, name=name, out_shape=tuple(pltpu.HBM(t.shape, t.dtype) for t in lands),
        in_specs=[_HBM] * n + [_SEM, _SEM, pl.BlockSpec(memory_space=pl.ANY)], out_specs=tuple([_HBM] * n),
        input_output_aliases={i: i for i in range(n)},
        compiler_params=pltpu.CompilerParams(has_side_effects=_EFFECT),
    )(*lands, send_sems, recv_sems, after))


def _exchange_start(srcs, name, modes):
    n = len(srcs)
    modes = [modes] * n if isinstance(modes, str) else list(modes)
    lands = []
    for s, mode in zip(srcs, modes):
        shape = {"columns": (s.shape[0], N_DEV * s.shape[1]), "pieces": s.shape}.get(mode, (N_DEV,) + s.shape)
        lands.append(lax.empty(shape, s.dtype))

    def body(*refs):
        src_refs, land_refs = refs[:n], refs[n:2 * n]
        send_sems, recv_sems = refs[2 * n], refs[2 * n + 1]
        local, remote = _split_copies(src_refs, land_refs, send_sems, recv_sems, modes)
        for cp in local + remote:
            cp.start()
        refs[-1][...] = jnp.zeros_like(refs[-1])

    both = list(srcs) + lands
    sems = pltpu.SemaphoreType.DMA((n * N_DEV,))
    out = pl.pallas_call(
        body, name=name,
        out_shape=(sems, sems, *[pltpu.HBM(t.shape, t.dtype) for t in both], jax.ShapeDtypeStruct((8, 128), F32)),
        in_specs=[_HBM] * (2 * n), out_specs=(_SEM, _SEM, *[_HBM] * (2 * n), pl.BlockSpec(memory_space=pltpu.VMEM)),
        input_output_aliases={i: 2 + i for i in range(2 * n)},
        compiler_params=pltpu.CompilerParams(has_side_effects=_EFFECT),
    )(*[pltpu.with_memory_space_constraint(t, pltpu.HBM) for t in both])
    return (n, modes, out[:-1]), out[-1]


def _exchange_wait(handle, after, name, which=None):
    n_all, modes, (send_sems, recv_sems, *both_all) = handle
    which = list(range(n_all)) if which is None else list(which)
    n = len(which)
    both = [both_all[a] for a in which] + [both_all[n_all + a] for a in which]

    def body(*refs):
        src_refs, land_refs = refs[:n], refs[n:2 * n]
        local, remote = _split_copies(src_refs, land_refs, refs[2 * n], refs[2 * n + 1], modes, which)
        for cp in local:
            cp.wait()
        for cp in remote:
            cp.wait_send()
            cp.wait_recv()

    out = pl.pallas_call(
        body, name=name, out_shape=tuple(pltpu.HBM(t.shape, t.dtype) for t in both),
        in_specs=[_HBM] * (2 * n) + [_SEM, _SEM, pl.BlockSpec(memory_space=pl.ANY)], out_specs=tuple([_HBM] * (2 * n)),
        input_output_aliases={i: i for i in range(2 * n)},
        compiler_params=pltpu.CompilerParams(has_side_effects=_EFFECT),
    )(*both, send_sems, recv_sems, after)
    return list(out[n:])


def _cast_all(arrays, name, after):
    waits = [] if after is None else [after]

    def body(*refs):
        for src, dst in zip(refs[:len(arrays)], refs[len(arrays) + len(waits):]):
            if len(src.shape) == 2:
                dst[...] = _bf(src[...])
            else:
                dst[:, 0, :] = _pack_pairs(_bf(src[:, 0, :]).astype(F32))

    shapes = [jax.ShapeDtypeStruct(a.shape, BF16) if a.ndim == 2 else jax.ShapeDtypeStruct((a.shape[0], 1, a.shape[2] // 2), F32)
              for a in arrays]
    return pl.pallas_call(body, name=name, out_shape=shapes,
                          compiler_params=pltpu.CompilerParams(vmem_limit_bytes=VMEM_LIMIT))(*arrays, *waits)


def _adam_update(g, w, m, v):
    nm = ADAM_B1 * m + (1.0 - ADAM_B1) * g
    nv = ADAM_B2 * v + (1.0 - ADAM_B2) * (g * g)
    m_hat = nm / (1.0 - ADAM_B1 ** ADAM_STEP)
    v_hat = nv / (1.0 - ADAM_B2 ** ADAM_STEP)
    return -ADAM_LR * (m_hat / (jnp.sqrt(v_hat) + ADAM_EPS) + ADAM_WD * w), nm, nv


def _adamw(parts, w, m, v, name):
    n, R, W = parts.shape
    tm = 128 if R % 128 == 0 else R

    def body(p_ref, w_ref, m_ref, v_ref, g_ref, d_ref, nm_ref, nv_ref):
        g = p_ref[0].astype(F32)
        for s in range(1, n):
            g = g + p_ref[s].astype(F32)
        g_ref[...] = g
        d_ref[...], nm_ref[...], nv_ref[...] = _adam_update(g, w_ref[...], m_ref[...], v_ref[...])

    tile = pl.BlockSpec((tm, W), lambda i: (i, 0))
    return pl.pallas_call(
        body, grid=(R // tm,), name=name,
        in_specs=[pl.BlockSpec((n, tm, W), lambda i: (0, i, 0)), tile, tile, tile],
        out_specs=[tile] * 4, out_shape=[jax.ShapeDtypeStruct((R, W), F32)] * 4,
        compiler_params=_params(("parallel",)),
    )(parts, w, m, v)


def _adamw_rows_apart(parts, w, m, v, name):
    n, R, _, half = parts.shape

    def body(p_hbm, w_hbm, m_hbm, v_hbm, *rest):
        out_hbm, (words, given, results, sems) = rest[:4], rest[4:]
        loads = [pltpu.make_async_copy(p_hbm.at[s, :, 0, :], words.at[s], sems.at[s]) for s in range(n)]
        loads += [pltpu.make_async_copy(h.at[:, 0, :], given.at[i], sems.at[n + i]) for i, h in enumerate((w_hbm, m_hbm, v_hbm))]
        for cp in loads:
            cp.start()
        for cp in loads:
            cp.wait()
        part = lambda s: jnp.concatenate(_unpack_pairs(words[s]), axis=1)
        g = part(0)
        for s in range(1, n):
            g = g + part(s)
        stores = []
        for i, val in enumerate((g,) + _adam_update(g, given[0], given[1], given[2])):
            results[i] = val
            stores.append(pltpu.make_async_copy(results.at[i], out_hbm[i].at[:, 0, :], sems.at[n + 3 + i]))
            stores[-1].start()
        for cp in stores:
            cp.wait()

    anywhere = pl.BlockSpec(memory_space=pl.ANY)
    return pl.pallas_call(
        body, name=name, in_specs=[anywhere] * 4, out_specs=[anywhere] * 4,
        out_shape=[jax.ShapeDtypeStruct(w.shape, F32)] * 4,
        scratch_shapes=[pltpu.VMEM((n, R, half), F32), pltpu.VMEM((3, R, 2 * half), F32), pltpu.VMEM((4, R, 2 * half), F32),
                        pltpu.SemaphoreType.DMA((n + 7,))],
        compiler_params=pltpu.CompilerParams(vmem_limit_bytes=VMEM_LIMIT),
    )(parts, w, m, v)


_MATRICES = ("w_in", "w_o", "w_up", "w_down", "w_ple_gate", "w_ple_proj")


_OTHERS = ("w_o", "w_up", "w_down", "w_ple_gate", "w_ple_proj")
_OTHER_MODES = {"w_o": "slots", "w_up": "slots", "w_down": "slots", "w_ple_gate": "slots", "w_ple_proj": "columns"}


_VECTORS = ("norm_mix", "norm_mlp", "norm_ple", "norm_final", "a_log", "dt_bias", "sinks", "dn_norm")
_SMALL_ROWS, _LOSS_ROW, _CONV_ROW = 16, 8, 9


def _pack_small(vectors, loss, conv):
    def body(*refs):
        out = refs[-1]
        out[...] = jnp.zeros_like(out)
        for r, ref in enumerate(refs[:len(_VECTORS)]):
            out[r:r + 1, 0:ref.shape[1]] = ref[...]
        out[_LOSS_ROW:_LOSS_ROW + 1, 0:128] = refs[len(_VECTORS)][...]
        out[_CONV_ROW:_CONV_ROW + 6, :] = refs[len(_VECTORS) + 1][...]

    return pl.pallas_call(body, name="pack_small", out_shape=jax.ShapeDtypeStruct((_SMALL_ROWS, 1024), F32))(*vectors, loss, conv)


def _sum_slots(parts):
    def body(p_ref, o_ref):
        acc = p_ref[0]
        for s in range(1, parts.shape[0]):
            acc = acc + p_ref[s]
        o_ref[...] = acc

    return pl.pallas_call(body, name="sum_small", out_shape=jax.ShapeDtypeStruct(parts.shape[1:], parts.dtype))(parts)


def _adamw_vectors(summed, conv_g, wmv):
    names = _VECTORS + ("conv_w",)
    flat = [a for triple in wmv for a in triple]

    def body(*refs):
        sum_ref, conv_ref = refs[0], refs[1]
        ins, outs = refs[2:2 + len(flat)], refs[2 + len(flat):]
        for i in range(len(names)):
            w_ref, m_ref, v_ref = ins[3 * i:3 * i + 3]
            g = conv_ref[...] if i == len(_VECTORS) else sum_ref[i:i + 1, 0:w_ref.shape[1]]
            outs[4 * i][...] = g
            outs[4 * i + 1][...], outs[4 * i + 2][...], outs[4 * i + 3][...] = _adam_update(g, w_ref[...], m_ref[...], v_ref[...])

    out_shape = [jax.ShapeDtypeStruct(t[0].shape, F32) for t in wmv for _ in range(4)]
    res = pl.pallas_call(body, name="adamw_vectors", out_shape=out_shape)(summed, conv_g, *flat)
    return {n: res[4 * i:4 * i + 4] for i, n in enumerate(names)}


_ORDER = ("norm_mix", "w_in", "conv_w", "a_log", "dt_bias", "dn_norm", "sinks", "w_o", "norm_mlp", "w_up", "w_down",
          "norm_ple", "w_ple_gate", "w_ple_proj", "norm_final")


def kernel(x, p, norm_mix, w_in, conv_w, a_log, dt_bias, dn_norm, sinks, w_o, norm_mlp, w_up, w_down, norm_ple, w_ple_gate, w_ple_proj, norm_final, loss_target, m_norm_mix, m_w_in, m_conv_w, m_a_log, m_dt_bias, m_dn_norm, m_sinks, m_w_o, m_norm_mlp, m_w_up, m_w_down, m_norm_ple, m_w_ple_gate, m_w_ple_proj, m_norm_final, v_norm_mix, v_w_in, v_conv_w, v_a_log, v_dt_bias, v_dn_norm, v_sinks, v_w_o, v_norm_mlp, v_w_up, v_w_down, v_norm_ple, v_w_ple_gate, v_w_ple_proj, v_norm_final):
    w = dict(norm_mix=norm_mix, w_in=w_in, conv_w=conv_w[0], a_log=a_log, dt_bias=dt_bias, dn_norm=dn_norm, sinks=sinks,
             w_o=w_o[0], norm_mlp=norm_mlp, w_up=w_up[0], w_down=w_down[0], norm_ple=norm_ple, w_ple_gate=w_ple_gate[0],
             w_ple_proj=w_ple_proj[0], norm_final=norm_final)
    m = dict(norm_mix=m_norm_mix, w_in=m_w_in, conv_w=m_conv_w[0], a_log=m_a_log, dt_bias=m_dt_bias, dn_norm=m_dn_norm,
             sinks=m_sinks, w_o=m_w_o[0], norm_mlp=m_norm_mlp, w_up=m_w_up[0], w_down=m_w_down[0], norm_ple=m_norm_ple,
             w_ple_gate=m_w_ple_gate[0], w_ple_proj=m_w_ple_proj[0], norm_final=m_norm_final)
    v = dict(norm_mix=v_norm_mix, w_in=v_w_in, conv_w=v_conv_w[0], a_log=v_a_log, dt_bias=v_dt_bias, dn_norm=v_dn_norm,
             sinks=v_sinks, w_o=v_w_o[0], norm_mlp=v_norm_mlp, w_up=v_w_up[0], w_down=v_w_down[0], norm_ple=v_norm_ple,
             w_ple_gate=v_w_ple_gate[0], w_ple_proj=v_w_ple_proj[0], norm_final=v_norm_final)
    me = 4 * lax.axis_index("x") + 2 * lax.axis_index("y") + lax.axis_index("c")
    conv_shard = conv_w.shape[2]

    for d in (w, m, v):
        d["w_in"] = jnp.transpose(d["w_in"], (2, 0, 1))
    conv_pad = jnp.pad(w["conv_w"], ((0, 8 - DN_CONV), (0, 256 - conv_shard)))
    w_in_shard, = _cast_all([w["w_in"]], "cast_w_in", None)
    gathers_first, token_first = _exchange_start([w_in_shard, conv_pad], "gather_first_start", "chips")
    shards = _cast_all([w[n] for n in _OTHERS], "cast_others", token_first)
    gathers, token_gather = _exchange_start(list(shards), "gather_start", [_OTHER_MODES[n] for n in _OTHERS])

    def first_weights(after):
        over_ici = _exchange_wait(gathers_first, after, "gather_first_wait")
        handle, token = _forward_start(over_ici, "gather_first_forward")
        w_in_all, conv_all = _forward_wait(handle, token, "gather_first_forward_wait")
        conv_all = jnp.transpose(conv_all[:, :DN_CONV, :conv_shard], (1, 0, 2)).reshape(DN_CONV, N_DEV * conv_shard)
        return _w_in_to_internal(w_in_all.reshape(D_IN, 1, _HALF)), conv_all

    as_taken = {"w_o": lambda t: t.reshape(1024, 1024), "w_up": lambda t: t, "w_down": lambda t: t.reshape(4096, 1024),
                "w_ple_gate": lambda t: t.reshape(1024, 1024), "w_ple_proj": lambda t: t}

    def other_weights(names, after):
        which = [_OTHERS.index(n) for n in names]
        got = _exchange_wait(gathers, after, "gather_wait_" + names[0], which)
        return [as_taken[n](t) for n, t in zip(names, got)]

    shipped = []

    def ship_early(pieces):
        names = tuple(pieces)
        if names == ("w_in",):
            pieces = {"w_in": _w_in_from_internal(pieces["w_in"]).reshape(N_DEV, D_IN // N_DEV, 1, _HALF)}
        handle, token = _exchange_start([pieces[n] for n in names], "scatter_start_" + names[0], "pieces")
        shipped.append((names, handle))
        return token

    loss, grad_x, g = _local_step(x[0], p[0, 0], loss_target[0], w, first_weights, other_weights, ship_early, token_gather)

    row = lambda t: t.reshape(1, t.size)
    small = _pack_small([row(g[n]) for n in _VECTORS], loss, g["conv_w"].reshape(6, 1024))
    small_handle, token_small = _exchange_start([small], "gather_small_start", "slots")
    big, after = {}, token_small
    for names, handle in shipped[:-1]:
        for n, r in zip(names, _exchange_wait(handle, after, "scatter_wait_" + names[0])):
            big[n] = _adamw(r, w[n], m[n], v[n], "adamw_" + n)
            after = big[n][1]
    names, handle = shipped[-1]
    for n, r in zip(names, _exchange_wait(handle, after, "scatter_wait_" + names[0])):
        big[n] = _adamw_rows_apart(r, w[n], m[n], v[n], "adamw_" + n)
        after = big[n][1]
    small_all, = _exchange_wait(small_handle, after, "gather_small_wait")
    summed = _sum_slots(small_all)
    conv_g = lax.dynamic_slice(summed[_CONV_ROW:_CONV_ROW + 6].reshape(DN_CONV, N_DEV * conv_shard), (0, me * conv_shard),
                               (DN_CONV, conv_shard))
    small_out = _adamw_vectors(summed, conv_g, [(row(w[n]), row(m[n]), row(v[n])) for n in _VECTORS]
                               + [(w["conv_w"], m["conv_w"], v["conv_w"])])

    result = [summed[_LOSS_ROW, 0], grad_x[None]]
    for i in range(4):
        for n in _ORDER:
            if n == "w_in":
                result.append(jnp.transpose(big[n][i], (1, 2, 0)))
            elif n in _MATRICES:
                result.append(big[n][i][None])
            elif n == "conv_w":
                result.append(small_out[n][i][None])
            else:
                result.append(small_out[n][i].reshape(w[n].shape))
    return tuple(result)
```
